```python
import math
import jax, jax.numpy as jnp
from jax import lax
import numpy as np

D_MODEL = 1024
BATCH = 8
SEQ = 4096
DEPTH = 2

HEAD_DIM = 64
N_GROUPS = D_MODEL // HEAD_DIM
N_MEM_HEADS = 4
N_MIX_HEADS = N_GROUPS - N_MEM_HEADS
D_MIX = N_MIX_HEADS * HEAD_DIM
D_MEM = N_MEM_HEADS * HEAD_DIM
CHUNK = 128
Q_BLOCK = 128
N_MEM_TOKENS = 256
D_FF = 4 * D_MODEL
N_A_LAYERS = DEPTH - DEPTH // 2
N_B_LAYERS = DEPTH // 2
DN_ALPHA = (2 * DEPTH) ** 0.25
DN_BETA = (8 * DEPTH) ** -0.25
LN_EPS = 1e-5

kernel_name = "yoco_gmlp_fox_memory_hybrid"


def layer_norm(x, g, b):
    xf = x.astype(jnp.float32)
    mu = jnp.mean(xf, axis=-1, keepdims=True)
    var = jnp.mean(jnp.square(xf - mu), axis=-1, keepdims=True)
    y = (xf - mu) * lax.rsqrt(var + LN_EPS)
    return (y * g.astype(jnp.float32) + b.astype(jnp.float32)).astype(x.dtype)


def chunked_sgu(u, v, w_s, b_s):
    bsz, s, g, dh = v.shape
    n_chunks = s // CHUNK
    causal = jnp.tril(jnp.ones((CHUNK, CHUNK), dtype=bool))
    w = jnp.where(causal[None], w_s, jnp.zeros_like(w_s))
    vc = v.reshape(bsz, n_chunks, CHUNK, g, dh)
    mixed = jnp.einsum('gts,bnsgd->bntgd', w, vc) + b_s.T[None, None, :, :, None]
    return u * mixed.reshape(bsz, s, g, dh)


def fox_attention(q, k, v, logcum):
    bsz, s, h, dh = q.shape
    n_blk = s // Q_BLOCK
    scale = 1.0 / math.sqrt(dh)
    qb = q.reshape(bsz, n_blk, Q_BLOCK, h, dh).transpose(1, 0, 2, 3, 4)
    cb = logcum.reshape(bsz, h, n_blk, Q_BLOCK).transpose(2, 0, 1, 3)
    kpos = jnp.arange(s)

    def one_block(args):
        qi, ci, i = args
        sc = jnp.einsum('bqhd,bkhd->bhqk', qi, k).astype(jnp.float32) * scale
        sc = sc + ci[..., None] - logcum[:, :, None, :]
        qpos = i * Q_BLOCK + jnp.arange(Q_BLOCK)
        mask = kpos[None, :] <= qpos[:, None]
        sc = jnp.where(mask[None, None], sc, -jnp.inf)
        p = jax.nn.softmax(sc, axis=-1)
        return jnp.einsum('bhqk,bkhd->bqhd', p.astype(v.dtype), v)

    out = lax.map(one_block, (qb, cb, jnp.arange(n_blk)))
    return out.transpose(1, 0, 2, 3, 4).reshape(bsz, s, h, dh)


def memory_attention(qm, mk, mv):
    scale = 1.0 / math.sqrt(qm.shape[-1])
    sc = jnp.einsum('bshd,bmhd->bhsm', qm, mk).astype(jnp.float32) * scale
    p = jax.nn.softmax(sc, axis=-1)
    out = jnp.einsum('bhsm,bmhd->bshd', p.astype(mv.dtype), mv)
    return out.reshape(qm.shape[0], qm.shape[1], D_MEM)


def sq_relu_mlp(x, w_up, w_down):
    h = jnp.square(jax.nn.relu(x @ w_up))
    return h @ w_down


def _fwd_setup_inputs(seed: int = 0) -> dict:
    key = jax.random.key(seed)
    ks = jax.random.split(key, 20)
    f32 = jnp.float32
    d = D_MODEL
    nrm = lambda k, shp: jax.random.normal(k, shp, f32)
    x = nrm(ks[0], (BATCH, SEQ, d))
    mem = nrm(ks[1], (BATCH, N_MEM_TOKENS, d))
    a_w_in = nrm(ks[2], (N_A_LAYERS, d, 2 * D_MIX + D_MEM)) * d ** -0.5
    a_sgu_ln_g = 1.0 + 0.02 * nrm(ks[3], (N_A_LAYERS, D_MIX))
    a_sgu_ln_b = 0.02 * nrm(ks[4], (N_A_LAYERS, D_MIX))
    a_w_s = nrm(ks[5], (N_A_LAYERS, N_MIX_HEADS, CHUNK, CHUNK)) * (0.5 * CHUNK ** -0.5)
    a_b_s = 1.0 + 0.02 * nrm(ks[6], (N_A_LAYERS, N_MIX_HEADS, CHUNK))
    kv_wk = nrm(ks[7], (d, D_MIX)) * d ** -0.5
    kv_wv = nrm(ks[8], (d, D_MIX)) * d ** -0.5 * DN_BETA
    kv_wf = nrm(ks[9], (d, N_MIX_HEADS)) * d ** -0.5
    kv_w = jnp.concatenate([kv_wk, kv_wv, kv_wf], axis=-1)
    kv_b_f = jnp.linspace(1.0, 6.0, N_MIX_HEADS, dtype=f32) + 0.01 * nrm(ks[10], (N_MIX_HEADS,))
    b_w_q = nrm(ks[11], (N_B_LAYERS, d, D_MIX + D_MEM)) * d ** -0.5
    mk = nrm(ks[12], (DEPTH, d, D_MEM)) * d ** -0.5
    mv = nrm(ks[13], (DEPTH, d, D_MEM)) * d ** -0.5 * DN_BETA
    mem_w_kv = jnp.concatenate([mk, mv], axis=-1)
    w_o = nrm(ks[14], (DEPTH, D_MIX + D_MEM, d)) * (D_MIX + D_MEM) ** -0.5 * DN_BETA
    ln_g = 1.0 + 0.02 * nrm(ks[15], (DEPTH, 2, d))
    ln_b = 0.02 * nrm(ks[16], (DEPTH, 2, d))
    w_up = nrm(ks[17], (DEPTH, d, D_FF)) * d ** -0.5
    w_down = nrm(ks[18], (DEPTH, D_FF, d)) * D_FF ** -0.5 * DN_BETA
    return {"x": x, "mem": mem, "a_w_in": a_w_in, "a_sgu_ln_g": a_sgu_ln_g,
            "a_sgu_ln_b": a_sgu_ln_b, "a_w_s": a_w_s, "a_b_s": a_b_s,
            "kv_w": kv_w, "kv_b_f": kv_b_f, "b_w_q": b_w_q, "mem_w_kv": mem_w_kv,
            "w_o": w_o, "ln_g": ln_g, "ln_b": ln_b, "w_up": w_up, "w_down": w_down}


def _fwd_reference(x, mem, a_w_in, a_sgu_ln_g, a_sgu_ln_b, a_w_s, a_b_s, kv_w, kv_b_f,
              b_w_q, mem_w_kv, w_o, ln_g, ln_b, w_up, w_down):
    bsz, s, _ = x.shape
    m = mem.shape[1]
    k_sh = v_sh = logcum = None
    for layer in range(DEPTH):
        if layer < N_A_LAYERS:
            z = x @ a_w_in[layer]
            zu = jax.nn.gelu(z[..., :D_MIX])
            zv = layer_norm(jax.nn.gelu(z[..., D_MIX:2 * D_MIX]), a_sgu_ln_g[layer], a_sgu_ln_b[layer])
            qm = z[..., 2 * D_MIX:]
            mix = chunked_sgu(zu.reshape(bsz, s, N_MIX_HEADS, HEAD_DIM),
                              zv.reshape(bsz, s, N_MIX_HEADS, HEAD_DIM),
                              a_w_s[layer], a_b_s[layer]).reshape(bsz, s, D_MIX)
        else:
            if layer == N_A_LAYERS:
                kvf = x @ kv_w
                k_sh = kvf[..., :D_MIX].reshape(bsz, s, N_MIX_HEADS, HEAD_DIM)
                v_sh = kvf[..., D_MIX:2 * D_MIX].reshape(bsz, s, N_MIX_HEADS, HEAD_DIM)
                f_logit = kvf[..., 2 * D_MIX:].astype(jnp.float32) + kv_b_f.astype(jnp.float32)
                logcum = jnp.cumsum(jax.nn.log_sigmoid(f_logit), axis=1).transpose(0, 2, 1)
            j = layer - N_A_LAYERS
            z = x @ b_w_q[j]
            q = z[..., :D_MIX].reshape(bsz, s, N_MIX_HEADS, HEAD_DIM)
            qm = z[..., D_MIX:]
            mix = fox_attention(q, k_sh, v_sh, logcum).reshape(bsz, s, D_MIX)
        mkv = mem @ mem_w_kv[layer]
        mk = mkv[..., :D_MEM].reshape(bsz, m, N_MEM_HEADS, HEAD_DIM)
        mv = mkv[..., D_MEM:].reshape(bsz, m, N_MEM_HEADS, HEAD_DIM)
        mo = memory_attention(qm.reshape(bsz, s, N_MEM_HEADS, HEAD_DIM), mk, mv)
        y = jnp.concatenate([mix, mo], axis=-1) @ w_o[layer]
        x = layer_norm(DN_ALPHA * x + y, ln_g[layer, 0], ln_b[layer, 0])
        x = layer_norm(DN_ALPHA * x + sq_relu_mlp(x, w_up[layer], w_down[layer]),
                       ln_g[layer, 1], ln_b[layer, 1])
    return x


import jax as _jax
import jax.numpy as _jnp

TWIN_FORMAT = 'train_step'
FWD_PARAMS = ['x', 'mem', 'a_w_in', 'a_sgu_ln_g', 'a_sgu_ln_b', 'a_w_s', 'a_b_s', 'kv_w', 'kv_b_f', 'b_w_q', 'mem_w_kv', 'w_o', 'ln_g', 'ln_b', 'w_up', 'w_down']
TWIN_WEIGHTS = ['a_w_in', 'a_sgu_ln_g', 'a_sgu_ln_b', 'a_w_s', 'a_b_s', 'kv_w', 'kv_b_f', 'b_w_q', 'mem_w_kv', 'w_o', 'ln_g', 'ln_b', 'w_up', 'w_down']
TWIN_DIFF_INPUT = 'x'
TWIN_INPUTS = ['x', 'mem', 'a_w_in', 'a_sgu_ln_g', 'a_sgu_ln_b', 'a_w_s', 'a_b_s', 'kv_w', 'kv_b_f', 'b_w_q', 'mem_w_kv', 'w_o', 'ln_g', 'ln_b', 'w_up', 'w_down', 'loss_target', 'm_a_w_in', 'm_a_sgu_ln_g', 'm_a_sgu_ln_b', 'm_a_w_s', 'm_a_b_s', 'm_kv_w', 'm_kv_b_f', 'm_b_w_q', 'm_mem_w_kv', 'm_w_o', 'm_ln_g', 'm_ln_b', 'm_w_up', 'm_w_down', 'v_a_w_in', 'v_a_sgu_ln_g', 'v_a_sgu_ln_b', 'v_a_w_s', 'v_a_b_s', 'v_kv_w', 'v_kv_b_f', 'v_b_w_q', 'v_mem_w_kv', 'v_w_o', 'v_ln_g', 'v_ln_b', 'v_w_up', 'v_w_down']
TWIN_OUTPUTS = ['loss', 'grad_x', 'grad_a_w_in', 'grad_a_sgu_ln_g', 'grad_a_sgu_ln_b', 'grad_a_w_s', 'grad_a_b_s', 'grad_kv_w', 'grad_kv_b_f', 'grad_b_w_q', 'grad_mem_w_kv', 'grad_w_o', 'grad_ln_g', 'grad_ln_b', 'grad_w_up', 'grad_w_down', 'delta_a_w_in', 'delta_a_sgu_ln_g', 'delta_a_sgu_ln_b', 'delta_a_w_s', 'delta_a_b_s', 'delta_kv_w', 'delta_kv_b_f', 'delta_b_w_q', 'delta_mem_w_kv', 'delta_w_o', 'delta_ln_g', 'delta_ln_b', 'delta_w_up', 'delta_w_down', 'new_m_a_w_in', 'new_m_a_sgu_ln_g', 'new_m_a_sgu_ln_b', 'new_m_a_w_s', 'new_m_a_b_s', 'new_m_kv_w', 'new_m_kv_b_f', 'new_m_b_w_q', 'new_m_mem_w_kv', 'new_m_w_o', 'new_m_ln_g', 'new_m_ln_b', 'new_m_w_up', 'new_m_w_down', 'new_v_a_w_in', 'new_v_a_sgu_ln_g', 'new_v_a_sgu_ln_b', 'new_v_a_w_s', 'new_v_a_b_s', 'new_v_kv_w', 'new_v_kv_b_f', 'new_v_b_w_q', 'new_v_mem_w_kv', 'new_v_w_o', 'new_v_ln_g', 'new_v_ln_b', 'new_v_w_up', 'new_v_w_down']
TWIN_LEAF_KINDS = {'loss': 'loss', 'grad_x': 'grad_x', 'grad_a_w_in': 'grad_w', 'grad_a_sgu_ln_g': 'grad_w', 'grad_a_sgu_ln_b': 'grad_w', 'grad_a_w_s': 'grad_w', 'grad_a_b_s': 'grad_w', 'grad_kv_w': 'grad_w', 'grad_kv_b_f': 'grad_w', 'grad_b_w_q': 'grad_w', 'grad_mem_w_kv': 'grad_w', 'grad_w_o': 'grad_w', 'grad_ln_g': 'grad_w', 'grad_ln_b': 'grad_w', 'grad_w_up': 'grad_w', 'grad_w_down': 'grad_w', 'delta_a_w_in': 'delta_w', 'delta_a_sgu_ln_g': 'delta_w', 'delta_a_sgu_ln_b': 'delta_w', 'delta_a_w_s': 'delta_w', 'delta_a_b_s': 'delta_w', 'delta_kv_w': 'delta_w', 'delta_kv_b_f': 'delta_w', 'delta_b_w_q': 'delta_w', 'delta_mem_w_kv': 'delta_w', 'delta_w_o': 'delta_w', 'delta_ln_g': 'delta_w', 'delta_ln_b': 'delta_w', 'delta_w_up': 'delta_w', 'delta_w_down': 'delta_w', 'new_m_a_w_in': 'new_m', 'new_m_a_sgu_ln_g': 'new_m', 'new_m_a_sgu_ln_b': 'new_m', 'new_m_a_w_s': 'new_m', 'new_m_a_b_s': 'new_m', 'new_m_kv_w': 'new_m', 'new_m_kv_b_f': 'new_m', 'new_m_b_w_q': 'new_m', 'new_m_mem_w_kv': 'new_m', 'new_m_w_o': 'new_m', 'new_m_ln_g': 'new_m', 'new_m_ln_b': 'new_m', 'new_m_w_up': 'new_m', 'new_m_w_down': 'new_m', 'new_v_a_w_in': 'new_v', 'new_v_a_sgu_ln_g': 'new_v', 'new_v_a_sgu_ln_b': 'new_v', 'new_v_a_w_s': 'new_v', 'new_v_a_b_s': 'new_v', 'new_v_kv_w': 'new_v', 'new_v_kv_b_f': 'new_v', 'new_v_b_w_q': 'new_v', 'new_v_mem_w_kv': 'new_v', 'new_v_w_o': 'new_v', 'new_v_ln_g': 'new_v', 'new_v_ln_b': 'new_v', 'new_v_w_up': 'new_v', 'new_v_w_down': 'new_v'}


def _forward(args):
    return _fwd_reference(*[args[k] for k in FWD_PARAMS])


def _output_shape():
    def fwd():
        inp = _fwd_setup_inputs(0)
        return _fwd_reference(*[inp[k] for k in FWD_PARAMS])
    out = _jax.eval_shape(fwd)
    return out.shape, out.dtype

N_MICROBATCH = 1
ADAM_LR = 0.001
ADAM_B1 = 0.9
ADAM_B2 = 0.999
ADAM_EPS = 1e-08
ADAM_WD = 0.01
ADAM_STEP = 10
PER_EXAMPLE_BATCH_AXIS = {'x': 0, 'mem': 0, 'loss_target': 0}
SHARED_INPUTS = []
_WEIGHT_DTYPES = {'a_w_in': _jnp.float32, 'a_sgu_ln_g': _jnp.float32, 'a_sgu_ln_b': _jnp.float32, 'a_w_s': _jnp.float32, 'a_b_s': _jnp.float32, 'kv_w': _jnp.float32, 'kv_b_f': _jnp.float32, 'b_w_q': _jnp.float32, 'mem_w_kv': _jnp.float32, 'w_o': _jnp.float32, 'ln_g': _jnp.float32, 'ln_b': _jnp.float32, 'w_up': _jnp.float32, 'w_down': _jnp.float32}
MOMENT_SCALE = {'a_w_in': 3.238890e-02, 'a_sgu_ln_g': 1.491630e-02, 'a_sgu_ln_b': 1.386722e-02, 'a_w_s': 2.108073e-02, 'a_b_s': 3.016073e-02, 'kv_w': 2.668156e-02, 'kv_b_f': 4.659881e-02, 'b_w_q': 7.048208e-03, 'mem_w_kv': 6.051091e-03, 'w_o': 8.442500e-02, 'ln_g': 1.610466e+01, 'ln_b': 3.955221e+00, 'w_up': 4.248717e-02, 'w_down': 2.273704e-01}


def _to_microbatches(a, axis):
    t = _jnp.moveaxis(a, axis, 0)
    t = t.reshape((N_MICROBATCH, t.shape[0] // N_MICROBATCH) + t.shape[1:])
    return _jnp.moveaxis(t, 1, axis + 1)


def setup_inputs(seed: int = 0) -> dict:
    inp = _fwd_setup_inputs(seed)
    key = _jax.random.fold_in(_jax.random.key(seed), 7919)
    shape, _ = _output_shape()
    out = dict(inp)
    out["loss_target"] = _jax.random.normal(_jax.random.fold_in(key, 0), shape, _jnp.float32)
    for i, name in enumerate(TWIN_WEIGHTS):
        w = inp[name].astype(_jnp.float32)
        if MOMENT_SCALE is None:
            s = _jnp.sqrt(_jnp.mean(_jnp.square(w)) + 1e-30)
        else:
            s = MOMENT_SCALE[name]
        km, kv = _jax.random.split(_jax.random.fold_in(key, i + 1))
        out[name] = w
        out["m_" + name] = s * _jax.random.normal(km, w.shape, _jnp.float32)
        out["v_" + name] = (s * s) * _jax.random.uniform(kv, w.shape, _jnp.float32, 0.5, 1.5)
    if N_MICROBATCH > 1:
        for name, axis in PER_EXAMPLE_BATCH_AXIS.items():
            out[name] = _to_microbatches(out[name], axis)
    return {'x': out['x'], 'mem': out['mem'], 'a_w_in': out['a_w_in'], 'a_sgu_ln_g': out['a_sgu_ln_g'], 'a_sgu_ln_b': out['a_sgu_ln_b'], 'a_w_s': out['a_w_s'], 'a_b_s': out['a_b_s'], 'kv_w': out['kv_w'], 'kv_b_f': out['kv_b_f'], 'b_w_q': out['b_w_q'], 'mem_w_kv': out['mem_w_kv'], 'w_o': out['w_o'], 'ln_g': out['ln_g'], 'ln_b': out['ln_b'], 'w_up': out['w_up'], 'w_down': out['w_down'], 'loss_target': out['loss_target'], 'm_a_w_in': out['m_a_w_in'], 'm_a_sgu_ln_g': out['m_a_sgu_ln_g'], 'm_a_sgu_ln_b': out['m_a_sgu_ln_b'], 'm_a_w_s': out['m_a_w_s'], 'm_a_b_s': out['m_a_b_s'], 'm_kv_w': out['m_kv_w'], 'm_kv_b_f': out['m_kv_b_f'], 'm_b_w_q': out['m_b_w_q'], 'm_mem_w_kv': out['m_mem_w_kv'], 'm_w_o': out['m_w_o'], 'm_ln_g': out['m_ln_g'], 'm_ln_b': out['m_ln_b'], 'm_w_up': out['m_w_up'], 'm_w_down': out['m_w_down'], 'v_a_w_in': out['v_a_w_in'], 'v_a_sgu_ln_g': out['v_a_sgu_ln_g'], 'v_a_sgu_ln_b': out['v_a_sgu_ln_b'], 'v_a_w_s': out['v_a_w_s'], 'v_a_b_s': out['v_a_b_s'], 'v_kv_w': out['v_kv_w'], 'v_kv_b_f': out['v_kv_b_f'], 'v_b_w_q': out['v_b_w_q'], 'v_mem_w_kv': out['v_mem_w_kv'], 'v_w_o': out['v_w_o'], 'v_ln_g': out['v_ln_g'], 'v_ln_b': out['v_ln_b'], 'v_w_up': out['v_w_up'], 'v_w_down': out['v_w_down']}


def _loss(weights, diff, rest, loss_target):
    with _jax.named_scope("forward"):
        args = {**rest, TWIN_DIFF_INPUT: diff, **{k: w.astype(_WEIGHT_DTYPES[k]) for k, w in weights.items()}}
        y = _forward(args)
    with _jax.named_scope("loss_head"):
        err = _jnp.square(y.astype(_jnp.float32) - loss_target)
        return 0.5 * _jnp.sum(_jnp.mean(err, axis=-1)) if err.ndim else 0.5 * err


def _adamw(w, g, m, v):
    m = ADAM_B1 * m + (1.0 - ADAM_B1) * g
    v = ADAM_B2 * v + (1.0 - ADAM_B2) * _jnp.square(g)
    m_hat = m / (1.0 - ADAM_B1 ** ADAM_STEP)
    v_hat = v / (1.0 - ADAM_B2 ** ADAM_STEP)
    delta = -ADAM_LR * (m_hat / (_jnp.sqrt(v_hat) + ADAM_EPS) + ADAM_WD * w)
    return delta, m, v


def reference(x, mem, a_w_in, a_sgu_ln_g, a_sgu_ln_b, a_w_s, a_b_s, kv_w, kv_b_f, b_w_q, mem_w_kv, w_o, ln_g, ln_b, w_up, w_down, loss_target, m_a_w_in, m_a_sgu_ln_g, m_a_sgu_ln_b, m_a_w_s, m_a_b_s, m_kv_w, m_kv_b_f, m_b_w_q, m_mem_w_kv, m_w_o, m_ln_g, m_ln_b, m_w_up, m_w_down, v_a_w_in, v_a_sgu_ln_g, v_a_sgu_ln_b, v_a_w_s, v_a_b_s, v_kv_w, v_kv_b_f, v_b_w_q, v_mem_w_kv, v_w_o, v_ln_g, v_ln_b, v_w_up, v_w_down):
    given = dict(x=x, mem=mem, a_w_in=a_w_in, a_sgu_ln_g=a_sgu_ln_g, a_sgu_ln_b=a_sgu_ln_b, a_w_s=a_w_s, a_b_s=a_b_s, kv_w=kv_w, kv_b_f=kv_b_f, b_w_q=b_w_q, mem_w_kv=mem_w_kv, w_o=w_o, ln_g=ln_g, ln_b=ln_b, w_up=w_up, w_down=w_down, loss_target=loss_target, m_a_w_in=m_a_w_in, m_a_sgu_ln_g=m_a_sgu_ln_g, m_a_sgu_ln_b=m_a_sgu_ln_b, m_a_w_s=m_a_w_s, m_a_b_s=m_a_b_s, m_kv_w=m_kv_w, m_kv_b_f=m_kv_b_f, m_b_w_q=m_b_w_q, m_mem_w_kv=m_mem_w_kv, m_w_o=m_w_o, m_ln_g=m_ln_g, m_ln_b=m_ln_b, m_w_up=m_w_up, m_w_down=m_w_down, v_a_w_in=v_a_w_in, v_a_sgu_ln_g=v_a_sgu_ln_g, v_a_sgu_ln_b=v_a_sgu_ln_b, v_a_w_s=v_a_w_s, v_a_b_s=v_a_b_s, v_kv_w=v_kv_w, v_kv_b_f=v_kv_b_f, v_b_w_q=v_b_w_q, v_mem_w_kv=v_mem_w_kv, v_w_o=v_w_o, v_ln_g=v_ln_g, v_ln_b=v_ln_b, v_w_up=v_w_up, v_w_down=v_w_down)
    weights = {n: given[n] for n in TWIN_WEIGHTS}
    shared = {n: given[n] for n in SHARED_INPUTS}
    per_example = {n: given[n] for n in ['x', 'mem']}
    grad_fn = _jax.value_and_grad(_loss, argnums=(0, 1))

    def one_microbatch(ex, loss_target):
        ex = dict(ex)
        diff = ex.pop(TWIN_DIFF_INPUT)
        return grad_fn(weights, diff, {**shared, **ex}, loss_target)

    if N_MICROBATCH == 1:
        loss, (grad_w, grad_x) = one_microbatch(per_example, given["loss_target"])
    else:
        def body(carry, xs):
            loss_sum, grad_sum = carry
            l_k, (gw_k, gx_k) = one_microbatch(xs[0], xs[1])
            with _jax.named_scope("update"):
                return (loss_sum + l_k, _jax.tree.map(_jnp.add, grad_sum, gw_k)), gx_k

        init = (_jnp.zeros((), _jnp.float32), _jax.tree.map(_jnp.zeros_like, weights))
        (loss, grad_w), grad_x = _jax.lax.scan(body, init, (per_example, given["loss_target"]))
    with _jax.named_scope("update"):
        delta_w, new_m, new_v = {}, {}, {}
        for n in TWIN_WEIGHTS:
            delta_w[n], new_m[n], new_v[n] = _adamw(weights[n], grad_w[n], given["m_" + n], given["v_" + n])
    return (loss, grad_x, *[grad_w[n] for n in TWIN_WEIGHTS], *[delta_w[n] for n in TWIN_WEIGHTS],
            *[new_m[n] for n in TWIN_WEIGHTS], *[new_v[n] for n in TWIN_WEIGHTS])
```

```python
import functools
import math

import jax
import jax.numpy as jnp
from jax import lax
from jax.experimental import pallas as pl
from jax.experimental.pallas import tpu as pltpu

F32 = jnp.float32
BF16 = jnp.bfloat16

HEAD_DIM = 64
PAIR = 2 * HEAD_DIM
CHUNK = 128
LN_EPS = 1e-5
ALPHA = 4 ** 0.25
QK_SCALE = 1.0 / math.sqrt(HEAD_DIM)
NEG = -1e30
N_CHIPS = 4
FOX_BLOCK = 256
LANES = 128
VMEM_LIMIT = 48 * 2 ** 20

ADAM_LR, ADAM_B1, ADAM_B2, ADAM_EPS, ADAM_WD, ADAM_STEP = 0.001, 0.9, 0.999, 1e-08, 0.01, 10

MESH = pl.DeviceIdType.MESH
ANY = pl.BlockSpec(memory_space=pl.ANY)


def _dot(a, b):
    return jnp.dot(a, b, preferred_element_type=F32)


def _dot_nt(a, b):
    return lax.dot_general(a, b, (((1,), (1,)), ((), ())), preferred_element_type=F32)


def _dot_tn(a, b):
    return lax.dot_general(a, b, (((0,), (0,)), ((), ())), preferred_element_type=F32)


def _ln_stats(r):
    mu = jnp.mean(r, axis=-1, keepdims=True)
    d = r - mu
    var = jnp.mean(d * d, axis=-1, keepdims=True)
    rstd = lax.rsqrt(var + LN_EPS)
    return d * rstd, rstd


def _ln_bwd(dy, xhat, rstd, g):
    dxh = dy * g
    m1 = jnp.mean(dxh, axis=-1, keepdims=True)
    m2 = jnp.mean(dxh * xhat, axis=-1, keepdims=True)
    return rstd * (dxh - m1 - xhat * m2)


_GELU_K = math.sqrt(2.0 / math.pi)


def _gelu(x):
    return 0.5 * x * (1.0 + jnp.tanh(_GELU_K * (x + 0.044715 * x * x * x)))


def _gelu_grad(x):
    t = jnp.tanh(_GELU_K * (x + 0.044715 * x * x * x))
    return 0.5 * (1.0 + t) + 0.5 * x * (1.0 - t * t) * _GELU_K * (1.0 + 3 * 0.044715 * x * x)


def _lane_mask(e):
    lane = lax.broadcasted_iota(jnp.int32, (1, PAIR), 1)
    return (lane >= HEAD_DIM * e) & (lane < HEAD_DIM * (e + 1))


def _first_head():
    return lax.broadcasted_iota(jnp.int32, (1, PAIR), 1) < HEAD_DIM


def _mem_probs(qh, mkp):
    sc = _dot_nt(qh, mkp) * QK_SCALE
    ex = jnp.exp(sc - jnp.max(sc, axis=-1, keepdims=True))
    return ex / jnp.sum(ex, axis=-1, keepdims=True)


def _mem_attn_fwd(qm_b, mkv_ref, dq_dim):
    outs = []
    for mp in range(dq_dim // PAIR):
        qp = qm_b[:, mp * PAIR:(mp + 1) * PAIR]
        mkp = mkv_ref[:, mp * PAIR:(mp + 1) * PAIR]
        mvp = mkv_ref[:, dq_dim + mp * PAIR:dq_dim + (mp + 1) * PAIR]
        heads = []
        for e in (0, 1):
            qh = jnp.where(_lane_mask(e), qp, jnp.zeros_like(qp))
            p = _mem_probs(qh, mkp)
            heads.append(_dot(p.astype(BF16), mvp))
        outs.append(jnp.where(_first_head(), heads[0], heads[1]))
    return outs


def _mem_attn_bwd(qm_b, mkv_ref, dmo_pairs, dmkv_ref, dq_dim):
    dqs = []
    for mp in range(dq_dim // PAIR):
        ks = slice(mp * PAIR, (mp + 1) * PAIR)
        vs = slice(dq_dim + mp * PAIR, dq_dim + (mp + 1) * PAIR)
        qp = qm_b[:, ks]
        mkp = mkv_ref[:, ks]
        mvp = mkv_ref[:, vs]
        dmo_b = dmo_pairs[mp].astype(BF16)
        dq = None
        dmk = None
        dmv = None
        for e in (0, 1):
            hm = _lane_mask(e)
            qh = jnp.where(hm, qp, jnp.zeros_like(qp))
            p = _mem_probs(qh, mkp)
            doh = jnp.where(hm, dmo_b, jnp.zeros_like(dmo_b))
            dp = _dot_nt(doh, mvp)
            ds = p * (dp - jnp.sum(dp * p, axis=-1, keepdims=True))
            dsb = (ds * QK_SCALE).astype(BF16)
            kh = jnp.where(hm, mkp, jnp.zeros_like(mkp))
            dq_e = _dot(dsb, kh)
            dmk_e = _dot_tn(dsb, qh)
            dmv_e = _dot_tn(p.astype(BF16), doh)
            dq = dq_e if dq is None else dq + dq_e
            dmk = dmk_e if dmk is None else dmk + dmk_e
            dmv = dmv_e if dmv is None else dmv + dmv_e
        dmkv_ref[:, ks] += dmk
        dmkv_ref[:, vs] += dmv
        dqs.append(dq)
    return dqs


def _params(n_axes=1):
    return pltpu.CompilerParams(dimension_semantics=("arbitrary",) * n_axes, vmem_limit_bytes=VMEM_LIMIT)


def _full(shape):
    return pl.BlockSpec(shape, lambda *_: (0,) * len(shape))


def _rows(tm, cols):
    return pl.BlockSpec((tm, cols), lambda i: (i, 0))


def _pick(n, pref):
    if n <= pref:
        return n
    best = LANES
    for t in range(LANES, pref + 1, LANES):
        if n % t == 0:
            best = t
    return best if 2 * best >= pref or n > 2 * pref else n


def _mm_tn(a, b, name, out_split=1):
    R, M = a.shape
    _, N = b.shape
    bm, br = _pick(M, 512), _pick(R, 512)
    ncol = N // out_split
    bn = _pick(ncol, 1024)
    per = ncol // bn

    def body(a_ref, b_ref, o_ref):
        @pl.when(pl.program_id(2) == 0)
        def _():
            o_ref[...] = jnp.zeros_like(o_ref)

        o_ref[...] += _dot_tn(a_ref[...], b_ref[...])

    if out_split == 1:
        out_shape = jax.ShapeDtypeStruct((M, N), F32)
        out_spec = pl.BlockSpec((bm, bn), lambda i, j, r: (i, j))
    else:
        out_shape = jax.ShapeDtypeStruct((out_split, M, ncol), F32)
        out_spec = pl.BlockSpec((None, bm, bn), lambda i, j, r: (j // per, i, j % per))
    return pl.pallas_call(
        body, name=name, out_shape=out_shape, grid=(M // bm, N // bn, R // br),
        in_specs=[pl.BlockSpec((br, bm), lambda i, j, r: (r, i)), pl.BlockSpec((br, bn), lambda i, j, r: (r, j))],
        out_specs=out_spec,
        compiler_params=pltpu.CompilerParams(dimension_semantics=("parallel", "parallel", "arbitrary"),
                                             vmem_limit_bytes=VMEM_LIMIT),
    )(a, b)


def _mm_small(a, b, name):
    def body(a_ref, b_ref, o_ref):
        o_ref[...] = _dot(a_ref[...], b_ref[...]).astype(BF16)

    return pl.pallas_call(body, name=name, out_shape=jax.ShapeDtypeStruct((a.shape[0], b.shape[1]), BF16),
                          compiler_params=pltpu.CompilerParams(vmem_limit_bytes=VMEM_LIMIT))(a, b)


def _mlp_fwd(x, wup4, wdown4, g, b, name):
    S, D = x.shape
    NC, _, FC = wup4.shape
    tm = min(256, S)

    def body(x_ref, wup_ref, wdown_ref, g_ref, b_ref, xo_ref, xb_ref, r_ref, h_ref):
        xv = x_ref[...]
        xb = xv.astype(BF16)
        y = jnp.zeros((tm, D), F32)
        for c in range(NC):
            a = jnp.maximum(_dot(xb, wup_ref[c]), 0.0)
            hb = (a * a).astype(BF16)
            h_ref[:, c * FC:(c + 1) * FC] = hb
            y = y + _dot(hb, wdown_ref[c])
        r = ALPHA * xv + y
        xhat, _ = _ln_stats(r)
        xo = xhat * g_ref[...] + b_ref[...]
        xo_ref[...] = xo
        xb_ref[...] = xo.astype(BF16)
        r_ref[...] = r

    return pl.pallas_call(
        body, name=name, grid=(S // tm,),
        out_shape=(jax.ShapeDtypeStruct((S, D), F32), jax.ShapeDtypeStruct((S, D), BF16),
                   jax.ShapeDtypeStruct((S, D), F32), jax.ShapeDtypeStruct((S, NC * FC), BF16)),
        in_specs=[_rows(tm, D), _full(wup4.shape), _full(wdown4.shape), _full((1, D)), _full((1, D))],
        out_specs=(_rows(tm, D), _rows(tm, D), _rows(tm, D), _rows(tm, NC * FC)),
        compiler_params=_params(),
    )(x, wup4, wdown4, g, b)


def _mlp_bwd(dxo, r, h, wup4, wdown4, g, name):
    S, D = r.shape
    NC, _, FC = wup4.shape
    tm = min(256, S)

    def body(dxo_ref, r_ref, h_ref, wup_ref, wdown_ref, g_ref, dx_ref, drb_ref, dpre_ref, dgb_ref):
        @pl.when(pl.program_id(0) == 0)
        def _():
            dgb_ref[...] = jnp.zeros_like(dgb_ref)

        xhat, rstd = _ln_stats(r_ref[...])
        dy = dxo_ref[...]
        dgb_ref[0:1, :] += jnp.sum(dy * xhat, axis=0, keepdims=True)
        dgb_ref[1:2, :] += jnp.sum(dy, axis=0, keepdims=True)
        dr = _ln_bwd(dy, xhat, rstd, g_ref[...])
        drb = dr.astype(BF16)
        drb_ref[...] = drb
        dx = ALPHA * dr
        for c in range(NC):
            dh = _dot_nt(drb, wdown_ref[c])
            a = jnp.sqrt(h_ref[:, c * FC:(c + 1) * FC].astype(F32))
            dpre = (2.0 * a * dh).astype(BF16)
            dpre_ref[:, c * FC:(c + 1) * FC] = dpre
            dx = dx + _dot_nt(dpre, wup_ref[c])
        dx_ref[...] = dx

    return pl.pallas_call(
        body, name=name, grid=(S // tm,),
        out_shape=(jax.ShapeDtypeStruct((S, D), F32), jax.ShapeDtypeStruct((S, D), BF16),
                   jax.ShapeDtypeStruct((S, NC * FC), BF16), jax.ShapeDtypeStruct((2, D), F32)),
        in_specs=[_rows(tm, D), _rows(tm, D), _rows(tm, NC * FC), _full(wup4.shape), _full(wdown4.shape),
                  _full((1, D))],
        out_specs=(_rows(tm, D), _rows(tm, D), _rows(tm, NC * FC), _full((2, D))),
        compiler_params=_params(),
    )(dxo, r, h, wup4, wdown4, g)


def _loss_grad(xo, target, name):
    S, D = xo.shape
    tm = min(512, S)

    def body(x_ref, t_ref, d_ref, l_ref):
        @pl.when(pl.program_id(0) == 0)
        def _():
            l_ref[...] = jnp.zeros_like(l_ref)

        err = x_ref[...] - t_ref[...]
        d_ref[...] = err * (1.0 / D)
        l_ref[...] += (0.5 / D) * jnp.sum(err * err)

    return pl.pallas_call(
        body, name=name, grid=(S // tm,),
        out_shape=(jax.ShapeDtypeStruct((S, D), F32), jax.ShapeDtypeStruct((8, LANES), F32)),
        in_specs=[_rows(tm, D), _rows(tm, D)], out_specs=(_rows(tm, D), _full((8, LANES))),
        compiler_params=_params(),
    )(xo, target)


def _tail_fwd_store(xv, cat_ref, wo_ref, g_ref, b_ref, xo_ref, xb_ref, r_ref):
    y = _dot(cat_ref[...], wo_ref[...])
    r = ALPHA * xv + y
    xhat, _ = _ln_stats(r)
    xo = xhat * g_ref[...] + b_ref[...]
    xo_ref[...] = xo
    xb_ref[...] = xo.astype(BF16)
    r_ref[...] = r


def _tail_bwd_head(dxo_ref, r_ref, g_ref, wo_ref, dgb_ref, drb_ref):
    xhat, rstd = _ln_stats(r_ref[...])
    dy = dxo_ref[...]
    dgb_ref[0:1, :] += jnp.sum(dy * xhat, axis=0, keepdims=True)
    dgb_ref[1:2, :] += jnp.sum(dy, axis=0, keepdims=True)
    dr = _ln_bwd(dy, xhat, rstd, g_ref[...])
    drb = dr.astype(BF16)
    drb_ref[...] = drb
    return dr, _dot_nt(drb, wo_ref[...])


def _causal(shape):
    return lax.broadcasted_iota(jnp.int32, shape, 1) <= lax.broadcasted_iota(jnp.int32, shape, 0)


def _sgu_mixed(ws_ref, bst_ref, gp, vpair):
    tril = _causal((CHUNK, CHUNK))
    w0 = jnp.where(tril, ws_ref[2 * gp], 0.0).astype(BF16)
    w1 = jnp.where(tril, ws_ref[2 * gp + 1], 0.0).astype(BF16)
    m0 = _dot(w0, vpair) + bst_ref[:, 2 * gp:2 * gp + 1]
    m1 = _dot(w1, vpair) + bst_ref[:, 2 * gp + 1:2 * gp + 2]
    return jnp.where(_first_head(), m0, m1), w0, w1


def _mixer_a_fwd(x, wa, sg, sb, ws, bst, mkv, wo, g, b, name):
    S, D = x.shape
    DQ = mkv.shape[1] // 2
    DM = D - DQ
    NP = DM // PAIR
    tm = min(512, S)

    def body(x_ref, wa_ref, sg_ref, sb_ref, ws_ref, bst_ref, mkv_ref, wo_ref, g_ref, b_ref,
             xo_ref, xb_ref, r_ref, cat_ref, z_s, zv_s):
        xv = x_ref[...]
        z_s[...] = _dot(xv.astype(BF16), wa_ref[...])
        vhat, _ = _ln_stats(_gelu(z_s[:, DM:2 * DM]))
        zv_s[...] = (vhat * sg_ref[...] + sb_ref[...]).astype(BF16)
        for c in range(tm // CHUNK):
            rs = slice(c * CHUNK, (c + 1) * CHUNK)
            for gp in range(NP):
                ls = slice(gp * PAIR, (gp + 1) * PAIR)
                mixed, _, _ = _sgu_mixed(ws_ref, bst_ref, gp, zv_s[rs, ls])
                cat_ref[rs, ls] = (_gelu(z_s[rs, ls]) * mixed).astype(BF16)
        mo = _mem_attn_fwd(z_s[:, 2 * DM:].astype(BF16), mkv_ref, DQ)
        for mp in range(DQ // PAIR):
            cat_ref[:, DM + mp * PAIR:DM + (mp + 1) * PAIR] = mo[mp].astype(BF16)
        _tail_fwd_store(xv, cat_ref, wo_ref, g_ref, b_ref, xo_ref, xb_ref, r_ref)

    return pl.pallas_call(
        body, name=name, grid=(S // tm,),
        out_shape=(jax.ShapeDtypeStruct((S, D), F32), jax.ShapeDtypeStruct((S, D), BF16),
                   jax.ShapeDtypeStruct((S, D), F32), jax.ShapeDtypeStruct((S, D), BF16)),
        in_specs=[_rows(tm, D), _full(wa.shape), _full(sg.shape), _full(sb.shape), _full(ws.shape), _full(bst.shape),
                  _full(mkv.shape), _full(wo.shape), _full((1, D)), _full((1, D))],
        out_specs=(_rows(tm, D), _rows(tm, D), _rows(tm, D), _rows(tm, D)),
        scratch_shapes=[pltpu.VMEM((tm, 2 * DM + DQ), F32), pltpu.VMEM((tm, DM), BF16)],
        compiler_params=_params(),
    )(x, wa, sg, sb, ws, bst, mkv, wo, g, b)


def _mixer_a_bwd(x, dxo, r, wa, sg, sb, ws, bst, mkv, wo, g, name):
    S, D = x.shape
    DQ = mkv.shape[1] // 2
    DM = D - DQ
    NP = DM // PAIR
    NG = DM // HEAD_DIM
    tm = min(256, S)

    def body(x_ref, dxo_ref, r_ref, wa_ref, sg_ref, sb_ref, ws_ref, bst_ref, mkv_ref, wo_ref, g_ref,
             dx_ref, dz_ref, drb_ref, dgb_ref, dsgb_ref, dws_ref, dbst_ref, dmkv_ref, z_s, zv_s, dzv_s):
        @pl.when(pl.program_id(0) == 0)
        def _():
            dgb_ref[...] = jnp.zeros_like(dgb_ref)
            dsgb_ref[...] = jnp.zeros_like(dsgb_ref)
            dws_ref[...] = jnp.zeros_like(dws_ref)
            dbst_ref[...] = jnp.zeros_like(dbst_ref)
            dmkv_ref[...] = jnp.zeros_like(dmkv_ref)

        dr, dcat = _tail_bwd_head(dxo_ref, r_ref, g_ref, wo_ref, dgb_ref, drb_ref)
        z_s[...] = _dot(x_ref[...].astype(BF16), wa_ref[...])
        vhat, vrstd = _ln_stats(_gelu(z_s[:, DM:2 * DM]))
        zv_s[...] = (vhat * sg_ref[...] + sb_ref[...]).astype(BF16)
        tril = _causal((CHUNK, CHUNK))
        lane = lax.broadcasted_iota(jnp.int32, (1, LANES), 1)
        for c in range(tm // CHUNK):
            rs = slice(c * CHUNK, (c + 1) * CHUNK)
            for gp in range(NP):
                ls = slice(gp * PAIR, (gp + 1) * PAIR)
                vpair = zv_s[rs, ls]
                mixed, w0, w1 = _sgu_mixed(ws_ref, bst_ref, gp, vpair)
                u_pre = z_s[rs, ls]
                dmix = dcat[rs, ls]
                dz_ref[rs, ls] = (dmix * mixed * _gelu_grad(u_pre)).astype(BF16)
                dmixed = dmix * _gelu(u_pre)
                first = _first_head()
                d0 = jnp.where(first, dmixed, 0.0)
                d1 = jnp.where(first, 0.0, dmixed)
                d0b = d0.astype(BF16)
                d1b = d1.astype(BF16)
                dzv_s[rs, ls] = _dot_tn(w0, d0b) + _dot_tn(w1, d1b)
                dws_ref[2 * gp] += jnp.where(tril, _dot_nt(d0b, vpair), 0.0)
                dws_ref[2 * gp + 1] += jnp.where(tril, _dot_nt(d1b, vpair), 0.0)
                dbst_ref[...] += (jnp.where(lane == 2 * gp, jnp.sum(d0, axis=-1, keepdims=True), 0.0)
                                  + jnp.where(lane == 2 * gp + 1, jnp.sum(d1, axis=-1, keepdims=True), 0.0))
        dzv = dzv_s[...]
        dsgb_ref[0:1, :] += jnp.sum(dzv * vhat, axis=0, keepdims=True)
        dsgb_ref[1:2, :] += jnp.sum(dzv, axis=0, keepdims=True)
        dgv = _ln_bwd(dzv, vhat, vrstd, sg_ref[...])
        dz_ref[:, DM:2 * DM] = (dgv * _gelu_grad(z_s[:, DM:2 * DM])).astype(BF16)
        dmo = [dcat[:, DM + mp * PAIR:DM + (mp + 1) * PAIR] for mp in range(DQ // PAIR)]
        dqm = _mem_attn_bwd(z_s[:, 2 * DM:].astype(BF16), mkv_ref, dmo, dmkv_ref, DQ)
        for mp in range(DQ // PAIR):
            dz_ref[:, 2 * DM + mp * PAIR:2 * DM + (mp + 1) * PAIR] = dqm[mp].astype(BF16)
        dx_ref[...] = ALPHA * dr + _dot_nt(dz_ref[...], wa_ref[...])

    ZW = 2 * DM + DQ
    M = mkv.shape[0]
    return pl.pallas_call(
        body, name=name, grid=(S // tm,),
        out_shape=(jax.ShapeDtypeStruct((S, D), F32), jax.ShapeDtypeStruct((S, ZW), BF16),
                   jax.ShapeDtypeStruct((S, D), BF16), jax.ShapeDtypeStruct((2, D), F32),
                   jax.ShapeDtypeStruct((2, DM), F32), jax.ShapeDtypeStruct((NG, CHUNK, CHUNK), F32),
                   jax.ShapeDtypeStruct((CHUNK, LANES), F32), jax.ShapeDtypeStruct((M, 2 * DQ), F32)),
        in_specs=[_rows(tm, D), _rows(tm, D), _rows(tm, D), _full(wa.shape), _full(sg.shape), _full(sb.shape),
                  _full(ws.shape), _full(bst.shape), _full(mkv.shape), _full(wo.shape), _full((1, D))],
        out_specs=(_rows(tm, D), _rows(tm, ZW), _rows(tm, D), _full((2, D)), _full((2, DM)),
                   _full((NG, CHUNK, CHUNK)), _full((CHUNK, LANES)), _full((M, 2 * DQ))),
        scratch_shapes=[pltpu.VMEM((tm, ZW), F32), pltpu.VMEM((tm, DM), BF16), pltpu.VMEM((tm, DM), F32)],
        compiler_params=_params(),
    )(x, dxo, r, wa, sg, sb, ws, bst, mkv, wo, g)


def _proj_b_fwd(xb, wq, wkv, dm, name):
    S, D = xb.shape
    DQ = D - dm
    tm = min(512, S)

    def body(x_ref, wq_ref, wkv_ref, q_ref, qm_ref, k_ref, v_ref, fl_ref):
        xv = x_ref[...]
        z = _dot(xv, wq_ref[...])
        q_ref[...] = z[:, :dm].astype(BF16)
        qm_ref[...] = z[:, dm:].astype(BF16)
        kvf = _dot(xv, wkv_ref[...])
        k_ref[...] = kvf[:, :dm].astype(BF16)
        v_ref[...] = kvf[:, dm:2 * dm].astype(BF16)
        fl_ref[...] = kvf[:, 2 * dm:]

    return pl.pallas_call(
        body, name=name, grid=(S // tm,),
        out_shape=(jax.ShapeDtypeStruct((S, dm), BF16), jax.ShapeDtypeStruct((S, DQ), BF16),
                   jax.ShapeDtypeStruct((S, dm), BF16), jax.ShapeDtypeStruct((S, dm), BF16),
                   jax.ShapeDtypeStruct((S, LANES), F32)),
        in_specs=[_rows(tm, D), _full(wq.shape), _full(wkv.shape)],
        out_specs=(_rows(tm, dm), _rows(tm, DQ), _rows(tm, dm), _rows(tm, dm), _rows(tm, LANES)),
        compiler_params=_params(),
    )(xb, wq, wkv)


def _proj_b_bwd(dr, dqz, dkvf, wq, wkv, name):
    S, D = dr.shape
    tm = min(512, S)

    def body(dr_ref, dqz_ref, dkvf_ref, wq_ref, wkv_ref, dx_ref):
        dx_ref[...] = (ALPHA * dr_ref[...] + _dot_nt(dqz_ref[...], wq_ref[...])
                       + _dot_nt(dkvf_ref[...], wkv_ref[...]))

    return pl.pallas_call(
        body, name=name, grid=(S // tm,), out_shape=jax.ShapeDtypeStruct((S, D), F32),
        in_specs=[_rows(tm, D), _rows(tm, dqz.shape[1]), _rows(tm, dkvf.shape[1]), _full(wq.shape), _full(wkv.shape)],
        out_specs=_rows(tm, D), compiler_params=_params(),
    )(dr, dqz, dkvf, wq, wkv)


def _split3(v):
    hi = v.astype(BF16)
    r1 = v - hi.astype(F32)
    mid = r1.astype(BF16)
    lo = (r1 - mid.astype(F32)).astype(BF16)
    return hi, mid, lo


def _tri_sum(v, tri_b):
    hi, mid, lo = _split3(v)
    return _dot(hi, tri_b) + _dot(mid, tri_b) + _dot(lo, tri_b)


def _log_sigmoid(x):
    return jnp.minimum(x, 0.0) - jnp.log(1.0 + jnp.exp(-jnp.abs(x)))


def _forget_cumsum(flt, bf, name):
    H, S = flt.shape
    nchunk = S // LANES

    def body(fl_ref, bf_ref, c_ref):
        upper = (lax.broadcasted_iota(jnp.int32, (LANES, LANES), 0)
                 <= lax.broadcasted_iota(jnp.int32, (LANES, LANES), 1)).astype(BF16)

        def step(n, carry):
            s0 = pl.multiple_of(n * LANES, LANES)
            lf = _log_sigmoid(fl_ref[:, pl.ds(s0, LANES)] + bf_ref[...])
            cs = _tri_sum(lf, upper) + carry
            c_ref[:, pl.ds(s0, LANES)] = cs
            return cs[:, LANES - 1:LANES]

        lax.fori_loop(0, nchunk, step, jnp.zeros((H, 1), F32))

    return pl.pallas_call(body, name=name, out_shape=jax.ShapeDtypeStruct((H, S), F32),
                          compiler_params=pltpu.CompilerParams(vmem_limit_bytes=VMEM_LIMIT))(flt, bf)


def _forget_cumsum_bwd(dct, flt, bf, name):
    H, S = flt.shape
    nchunk = S // LANES

    def body(dc_ref, fl_ref, bf_ref, dfl_ref, dbf_ref):
        lower = (lax.broadcasted_iota(jnp.int32, (LANES, LANES), 0)
                 >= lax.broadcasted_iota(jnp.int32, (LANES, LANES), 1)).astype(BF16)

        def step(n, carry):
            tail, tot = carry
            s0 = pl.multiple_of((nchunk - 1 - n) * LANES, LANES)
            suffix = _tri_sum(dc_ref[:, pl.ds(s0, LANES)], lower) + tail
            xv = fl_ref[:, pl.ds(s0, LANES)] + bf_ref[...]
            dfl = suffix * (1.0 / (1.0 + jnp.exp(xv)))
            dfl_ref[:, pl.ds(s0, LANES)] = dfl
            return suffix[:, 0:1], tot + jnp.sum(dfl, axis=-1, keepdims=True)

        _, tot = lax.fori_loop(0, nchunk, step, (jnp.zeros((H, 1), F32), jnp.zeros((H, 1), F32)))
        dbf_ref[...] = jnp.broadcast_to(tot, (H, LANES))

    return pl.pallas_call(body, name=name,
                          out_shape=(jax.ShapeDtypeStruct((H, S), F32), jax.ShapeDtypeStruct((H, LANES), F32)),
                          compiler_params=pltpu.CompilerParams(vmem_limit_bytes=VMEM_LIMIT))(dct, flt, bf)


def _fox_fwd(q, k, v, ct3, name):
    S, DM = q.shape
    NP = DM // PAIR
    T = min(FOX_BLOCK, S)
    nq = S // T

    def body(q_ref, k_ref, v_ref, c_ref, o_ref, lse_ref):
        tril = _causal((T, T))

        def q_block(qi, _):
            t0 = pl.multiple_of(qi * T, T)
            qb = q_ref[pl.ds(t0, T), :]
            outs, lses = [], []
            for e in (0, 1):
                qh = jnp.where(_lane_mask(e), qb, jnp.zeros_like(qb))
                cref = c_ref[e:e + 1, pl.ds(t0, LANES)][:, 0:1]

                def kv_step(kj, carry, masked, qh=qh, cref=cref, e=e):
                    m, l, acc = carry
                    s0 = pl.multiple_of(kj * T, T)
                    s = _dot_nt(qh, k_ref[pl.ds(s0, T), :]) * QK_SCALE + (cref - c_ref[e:e + 1, pl.ds(s0, T)])
                    if masked:
                        s = jnp.where(tril, s, NEG)
                    m_new = jnp.maximum(m, jnp.max(s, axis=-1, keepdims=True))
                    a = jnp.exp(m - m_new)
                    p = jnp.exp(s - m_new)
                    l = a * l + jnp.sum(p, axis=-1, keepdims=True)
                    acc = a * acc + _dot(p.astype(BF16), v_ref[pl.ds(s0, T), :])
                    return m_new, l, acc

                init = (jnp.full((T, 1), NEG, F32), jnp.zeros((T, 1), F32), jnp.zeros((T, PAIR), F32))
                carry = lax.fori_loop(0, qi, functools.partial(kv_step, masked=False), init)
                m, l, acc = kv_step(qi, carry, True)
                outs.append(acc / l)
                lses.append(m + jnp.log(l))
            first = _first_head()
            o_ref[pl.ds(t0, T), :] = jnp.where(first, outs[0], outs[1]).astype(BF16)
            lse_ref[pl.ds(t0, T), :] = jnp.where(first, lses[0], lses[1])
            return 0

        lax.fori_loop(0, nq, q_block, 0)

    col = pl.BlockSpec((S, PAIR), lambda hp: (0, hp))
    return pl.pallas_call(
        body, name=name, grid=(NP,),
        out_shape=(jax.ShapeDtypeStruct((S, DM), BF16), jax.ShapeDtypeStruct((NP, S, LANES), F32)),
        in_specs=[col, col, col, pl.BlockSpec((None, 2, S), lambda hp: (hp, 0, 0))],
        out_specs=(col, pl.BlockSpec((None, S, LANES), lambda hp: (hp, 0, 0))),
        compiler_params=_params(),
    )(q, k, v, ct3)


def _fox_bwd(q, k, v, o, do, ct3, lse, name):
    S, DM = q.shape
    NP = DM // PAIR
    T = min(FOX_BLOCK, S)
    nq = S // T

    def body(q_ref, k_ref, v_ref, o_ref, do_ref, c_ref, lse_ref, dq_ref, dk_ref, dv_ref, dc_ref, dq_acc, drow_acc):
        tril = _causal((T, T))
        lane = lax.broadcasted_iota(jnp.int32, (1, LANES), 1)
        dq_acc[...] = jnp.zeros_like(dq_acc)
        drow_acc[...] = jnp.zeros_like(drow_acc)

        def kv_block(kj, _):
            s0 = pl.multiple_of(kj * T, T)
            kb = k_ref[pl.ds(s0, T), :]
            vb = v_ref[pl.ds(s0, T), :]
            dk_tot = jnp.zeros((T, PAIR), F32)
            dv_tot = jnp.zeros((T, PAIR), F32)
            for e in (0, 1):
                hm = _lane_mask(e)
                kh = jnp.where(hm, kb, jnp.zeros_like(kb))
                crow = c_ref[e:e + 1, pl.ds(s0, T)]

                def q_step(qi, carry, masked, hm=hm, kh=kh, crow=crow, e=e):
                    dk_a, dv_a, dc_a = carry
                    t0 = pl.multiple_of(qi * T, T)
                    qb = q_ref[pl.ds(t0, T), :]
                    dob = do_ref[pl.ds(t0, T), :]
                    doh = jnp.where(hm, dob, jnp.zeros_like(dob))
                    cref = c_ref[e:e + 1, pl.ds(t0, LANES)][:, 0:1]
                    lse_c = lse_ref[pl.ds(t0, T), :][:, HEAD_DIM * e:HEAD_DIM * e + 1]
                    s = _dot_nt(qb, kh) * QK_SCALE + (cref - crow)
                    if masked:
                        s = jnp.where(tril, s, NEG)
                    p = jnp.exp(s - lse_c)
                    dv_a = dv_a + _dot_tn(p.astype(BF16), doh)
                    dp = _dot_nt(doh, vb)
                    dd = jnp.sum(doh.astype(F32) * o_ref[pl.ds(t0, T), :].astype(F32), axis=-1, keepdims=True)
                    ds = p * (dp - dd)
                    dsb = (ds * QK_SCALE).astype(BF16)
                    dq_acc[pl.ds(t0, T), :] += _dot(dsb, kh)
                    dk_a = dk_a + _dot_tn(dsb, jnp.where(hm, qb, jnp.zeros_like(qb)))
                    dc_a = dc_a - jnp.sum(ds, axis=0, keepdims=True)
                    drow_acc[pl.ds(t0, T), :] += jnp.where(lane == e, jnp.sum(ds, axis=-1, keepdims=True), 0.0)
                    return dk_a, dv_a, dc_a

                init = (jnp.zeros((T, PAIR), F32), jnp.zeros((T, PAIR), F32), jnp.zeros((1, T), F32))
                carry = q_step(kj, init, True)
                dk_a, dv_a, dc_a = lax.fori_loop(kj + 1, nq, functools.partial(q_step, masked=False), carry)
                dk_tot = dk_tot + dk_a
                dv_tot = dv_tot + dv_a
                dc_ref[e:e + 1, pl.ds(s0, T)] = dc_a
            dk_ref[pl.ds(s0, T), :] = dk_tot.astype(BF16)
            dv_ref[pl.ds(s0, T), :] = dv_tot.astype(BF16)
            return 0

        lax.fori_loop(0, nq, kv_block, 0)
        dq_ref[...] = dq_acc[...].astype(BF16)

        def add_rows(qi, _):
            t0 = pl.multiple_of(qi * T, T)
            dc_ref[:, pl.ds(t0, T)] += drow_acc[pl.ds(t0, T), :].T[0:2, :]
            return 0

        lax.fori_loop(0, nq, add_rows, 0)

    col = pl.BlockSpec((S, PAIR), lambda hp: (0, hp))
    ct_spec = pl.BlockSpec((None, 2, S), lambda hp: (hp, 0, 0))
    return pl.pallas_call(
        body, name=name, grid=(NP,),
        out_shape=(jax.ShapeDtypeStruct((S, DM), BF16), jax.ShapeDtypeStruct((S, DM), BF16),
                   jax.ShapeDtypeStruct((S, DM), BF16), jax.ShapeDtypeStruct((NP, 2, S), F32)),
        in_specs=[col, col, col, col, col, ct_spec, pl.BlockSpec((None, S, LANES), lambda hp: (hp, 0, 0))],
        out_specs=(col, col, col, ct_spec),
        scratch_shapes=[pltpu.VMEM((S, PAIR), F32), pltpu.VMEM((S, LANES), F32)],
        compiler_params=_params(),
    )(q, k, v, o, do, ct3, lse)


def _mixer_b_fwd(x, o, qm, mkv, wo, g, b, name):
    S, D = x.shape
    DQ = qm.shape[1]
    DM = D - DQ
    tm = min(512, S)

    def body(x_ref, o_ref, qm_ref, mkv_ref, wo_ref, g_ref, b_ref, xo_ref, xb_ref, r_ref, cat_ref):
        cat_ref[:, :DM] = o_ref[...]
        mo = _mem_attn_fwd(qm_ref[...], mkv_ref, DQ)
        for mp in range(DQ // PAIR):
            cat_ref[:, DM + mp * PAIR:DM + (mp + 1) * PAIR] = mo[mp].astype(BF16)
        _tail_fwd_store(x_ref[...], cat_ref, wo_ref, g_ref, b_ref, xo_ref, xb_ref, r_ref)

    return pl.pallas_call(
        body, name=name, grid=(S // tm,),
        out_shape=(jax.ShapeDtypeStruct((S, D), F32), jax.ShapeDtypeStruct((S, D), BF16),
                   jax.ShapeDtypeStruct((S, D), F32), jax.ShapeDtypeStruct((S, D), BF16)),
        in_specs=[_rows(tm, D), _rows(tm, DM), _rows(tm, DQ), _full(mkv.shape), _full(wo.shape), _full((1, D)),
                  _full((1, D))],
        out_specs=(_rows(tm, D), _rows(tm, D), _rows(tm, D), _rows(tm, D)),
        compiler_params=_params(),
    )(x, o, qm, mkv, wo, g, b)


def _mixer_b_bwd(dxo, r, qm, mkv, wo, g, name):
    S, D = r.shape
    DQ = qm.shape[1]
    DM = D - DQ
    M = mkv.shape[0]
    tm = min(512, S)

    def body(dxo_ref, r_ref, qm_ref, mkv_ref, wo_ref, g_ref, dr_ref, drb_ref, dcat_ref, dgb_ref, dmkv_ref):
        @pl.when(pl.program_id(0) == 0)
        def _():
            dgb_ref[...] = jnp.zeros_like(dgb_ref)
            dmkv_ref[...] = jnp.zeros_like(dmkv_ref)

        dr, dcat = _tail_bwd_head(dxo_ref, r_ref, g_ref, wo_ref, dgb_ref, drb_ref)
        dr_ref[...] = dr
        dcat_ref[:, :DM] = dcat[:, :DM].astype(BF16)
        dmo = [dcat[:, DM + mp * PAIR:DM + (mp + 1) * PAIR] for mp in range(DQ // PAIR)]
        dqm = _mem_attn_bwd(qm_ref[...], mkv_ref, dmo, dmkv_ref, DQ)
        for mp in range(DQ // PAIR):
            dcat_ref[:, DM + mp * PAIR:DM + (mp + 1) * PAIR] = dqm[mp].astype(BF16)

    return pl.pallas_call(
        body, name=name, grid=(S // tm,),
        out_shape=(jax.ShapeDtypeStruct((S, D), F32), jax.ShapeDtypeStruct((S, D), BF16),
                   jax.ShapeDtypeStruct((S, D), BF16), jax.ShapeDtypeStruct((2, D), F32),
                   jax.ShapeDtypeStruct((M, 2 * DQ), F32)),
        in_specs=[_rows(tm, D), _rows(tm, D), _rows(tm, DQ), _full(mkv.shape), _full(wo.shape), _full((1, D))],
        out_specs=(_rows(tm, D), _rows(tm, D), _rows(tm, D), _full((2, D)), _full((M, 2 * DQ))),
        compiler_params=_params(),
    )(dxo, r, qm, mkv, wo, g)


def _local_step(x, mem, target, w):
    S, D = x.shape
    DQ = w["mkv"][0].shape[1] // 2
    DM = D - DQ
    NG = DM // HEAD_DIM
    NP = DM // PAIR
    ln_g, ln_b = w["ln_g"], w["ln_b"]
    row = lambda a, i: a[i:i + 1]
    memb = mem.astype(BF16)
    x0b = x.astype(BF16)
    bst = jnp.pad(w["b_s"].T, ((0, 0), (0, LANES - NG)))

    mkv0 = _mm_small(memb, w["mkv"][0], "mem_kv_0")
    x1, x1b, r1, cat0 = _mixer_a_fwd(x, w["wa"], w["sg"], w["sb"], w["ws"], bst, mkv0, w["wo"][0],
                                      row(ln_g, 0), row(ln_b, 0), "mixer_a_fwd")
    x2, x2b, r2, h0 = _mlp_fwd(x1, w["wup"][0], w["wdown"][0], row(ln_g, 1), row(ln_b, 1), "mlp_fwd_0")
    q, qm, k, v, fl = _proj_b_fwd(x2b, w["wq"], w["wkv"], DM, "proj_b_fwd")
    flt = fl[:, :16].T
    bfc = jnp.pad(w["bf"], (0, 16 - NG)).reshape(16, 1)
    ct = _forget_cumsum(flt, bfc, "forget_cumsum")
    ct3 = ct[:NG].reshape(NP, 2, S)
    o, lse = _fox_fwd(q, k, v, ct3, "fox_fwd")
    mkv1 = _mm_small(memb, w["mkv"][1], "mem_kv_1")
    x3, x3b, r3, cat1 = _mixer_b_fwd(x2, o, qm, mkv1, w["wo"][1], row(ln_g, 2), row(ln_b, 2), "mixer_b_fwd")
    x4, _, r4, h1 = _mlp_fwd(x3, w["wup"][1], w["wdown"][1], row(ln_g, 3), row(ln_b, 3), "mlp_fwd_1")

    dx4, loss_part = _loss_grad(x4, target, "loss_grad")
    dx3, dr4b, dpre1, dgb11 = _mlp_bwd(dx4, r4, h1, w["wup"][1], w["wdown"][1], row(ln_g, 3), "mlp_bwd_1")
    dr3, dr3b, dcat1, dgb10, dmkv1 = _mixer_b_bwd(dx3, r3, qm, mkv1, w["wo"][1], row(ln_g, 2), "mixer_b_bwd")
    do = dcat1[:, :DM]
    dq, dk, dv, dct3 = _fox_bwd(q, k, v, o, do, ct3, lse, "fox_bwd")
    dct = jnp.pad(dct3.reshape(NG, S), ((0, 16 - NG), (0, 0)))
    dflt, dbf = _forget_cumsum_bwd(dct, flt, bfc, "forget_cumsum_bwd")
    dfl = jnp.pad(dflt.T, ((0, 0), (0, LANES - 16))).astype(BF16)
    dqz = jnp.concatenate([dq, dcat1[:, DM:]], axis=1)
    dkvf = jnp.concatenate([dk, dv, dfl], axis=1)
    dx2 = _proj_b_bwd(dr3, dqz, dkvf, w["wq"], w["wkv"], "proj_b_bwd")
    dx1, dr2b, dpre0, dgb01 = _mlp_bwd(dx2, r2, h0, w["wup"][0], w["wdown"][0], row(ln_g, 1), "mlp_bwd_0")
    dx0, dz, dr1b, dgb00, dsgb, dws, dbst, dmkv0 = _mixer_a_bwd(
        x, dx1, r1, w["wa"], w["sg"], w["sb"], w["ws"], bst, mkv0, w["wo"][0], row(ln_g, 0), "mixer_a_bwd")

    nc = w["wup"][0].shape[0]
    grads = {
        "wdown": [_mm_tn(h0, dr2b, "dw_down_0"), _mm_tn(h1, dr4b, "dw_down_1")],
        "wup": [_mm_tn(x1b, dpre0, "dw_up_0", out_split=nc), _mm_tn(x3b, dpre1, "dw_up_1", out_split=nc)],
        "wo": [_mm_tn(cat0, dr1b, "dw_o_0"), _mm_tn(cat1, dr3b, "dw_o_1")],
        "wq": _mm_tn(x2b, dqz, "dw_q"),
        "wkv": _mm_tn(x2b, dkvf, "dw_kv"),
        "wa": _mm_tn(x0b, dz, "dw_a"),
        "mkv": [_mm_tn(memb, dmkv0.astype(BF16), "dw_mkv_0"), _mm_tn(memb, dmkv1.astype(BF16), "dw_mkv_1")],
        "ws": dws, "b_s": dbst[:, :NG].T, "sg": dsgb[0:1], "sb": dsgb[1:2], "bf": dbf[:NG, 0],
        "ln_g": jnp.concatenate([dgb00[0:1], dgb01[0:1], dgb10[0:1], dgb11[0:1]], axis=0),
        "ln_b": jnp.concatenate([dgb00[1:2], dgb01[1:2], dgb10[1:2], dgb11[1:2]], axis=0),
    }
    return loss_part, dx0, grads


def _place():
    xi, yi, ci = lax.axis_index("x"), lax.axis_index("y"), lax.axis_index("c")
    peers = [(xi, 1 - yi), (1 - xi, yi), (1 - xi, 1 - yi)]
    return xi, yi, ci, peers


def _comm_call(body, name, ins, out_shapes, n_remote, n_local):
    return pl.pallas_call(
        body, name=name, out_shape=out_shapes, in_specs=[ANY] * len(ins), out_specs=[ANY] * len(out_shapes),
        scratch_shapes=[pltpu.SemaphoreType.DMA((n_remote,)), pltpu.SemaphoreType.DMA((n_remote,)),
                        pltpu.SemaphoreType.DMA((max(n_local, 1),))],
        compiler_params=pltpu.CompilerParams(has_side_effects=True),
    )(*ins)


def _chip_allgather(xs, halved, name):
    n = len(xs)
    out_shapes = [jax.ShapeDtypeStruct((N_CHIPS,) + (a.shape[1:] if hv else a.shape), a.dtype)
                  for a, hv in zip(xs, halved)]

    def body(*refs):
        x_refs, o_refs = refs[:n], refs[n:2 * n]
        send_sems, recv_sems, loc_sems = refs[2 * n:]
        xi, yi, ci, peers = _place()
        j = 2 * xi + yi
        sends, locs = [], []
        for i in range(n):
            src = x_refs[i].at[ci] if halved[i] else x_refs[i]
            loc = pltpu.make_async_copy(src, o_refs[i].at[j], loc_sems.at[i])
            loc.start()
            locs.append(loc)
            for d, (px, py) in enumerate(peers):
                cp = pltpu.make_async_remote_copy(src_ref=src, dst_ref=o_refs[i].at[j], send_sem=send_sems.at[3 * i + d],
                                                  recv_sem=recv_sems.at[3 * i + d], device_id=(px, py, ci),
                                                  device_id_type=MESH)
                cp.start()
                sends.append(cp)
        for i in range(n):
            src = x_refs[i].at[ci] if halved[i] else x_refs[i]
            for d, (px, py) in enumerate(peers):
                pltpu.make_async_remote_copy(src_ref=src, dst_ref=o_refs[i].at[2 * px + py],
                                             send_sem=send_sems.at[3 * i + d], recv_sem=recv_sems.at[3 * i + d],
                                             device_id=(px, py, ci), device_id_type=MESH).wait_recv()
        for cp in sends:
            cp.wait_send()
        for loc in locs:
            loc.wait()

    return _comm_call(body, name, xs, out_shapes, 3 * n, n)


def _pair_swap(xs, name):
    n = len(xs)
    out_shapes = [jax.ShapeDtypeStruct((a.shape[0],) + a.shape[2:], a.dtype) for a in xs]

    def body(*refs):
        x_refs, o_refs = refs[:n], refs[n:2 * n]
        send_sems, recv_sems, _ = refs[2 * n:]
        xi, yi, ci, _ = _place()
        cps = []
        for i in range(n):
            cp = pltpu.make_async_remote_copy(src_ref=x_refs[i].at[:, 1 - ci], dst_ref=o_refs[i],
                                              send_sem=send_sems.at[i], recv_sem=recv_sems.at[i],
                                              device_id=(xi, yi, 1 - ci), device_id_type=MESH)
            cp.start()
            cps.append(cp)
        for cp in cps:
            cp.wait_recv()
        for cp in cps:
            cp.wait_send()

    return _comm_call(body, name, xs, out_shapes, n, 0)


def _chip_exchange(xs, name):
    n = len(xs)
    out_shapes = [jax.ShapeDtypeStruct((3,) + a.shape[1:], a.dtype) for a in xs]

    def body(*refs):
        x_refs, o_refs = refs[:n], refs[n:2 * n]
        send_sems, recv_sems, _ = refs[2 * n:]
        xi, yi, ci, peers = _place()
        cps = []
        for i in range(n):
            for d, (px, py) in enumerate(peers):
                cp = pltpu.make_async_remote_copy(src_ref=x_refs[i].at[2 * px + py], dst_ref=o_refs[i].at[d],
                                                  send_sem=send_sems.at[3 * i + d], recv_sem=recv_sems.at[3 * i + d],
                                                  device_id=(px, py, ci), device_id_type=MESH)
                cp.start()
                cps.append(cp)
        for cp in cps:
            cp.wait_recv()
        for cp in cps:
            cp.wait_send()

    return _comm_call(body, name, xs, out_shapes, 3 * n, 0)


def _pair_allgather(groups, name):
    flat = [a for grp in groups for a in grp]
    n = len(flat)
    out_shapes = [jax.ShapeDtypeStruct((len(grp), grp[0].shape[0], 2) + grp[0].shape[1:], grp[0].dtype)
                  for grp in groups]
    where = [(gi, li) for gi, grp in enumerate(groups) for li in range(len(grp))]

    def body(*refs):
        x_refs, o_refs = refs[:n], refs[n:n + len(groups)]
        send_sems, recv_sems, loc_sems = refs[n + len(groups):]
        xi, yi, ci, _ = _place()
        sends, locs = [], []
        for i, (gi, li) in enumerate(where):
            loc = pltpu.make_async_copy(x_refs[i], o_refs[gi].at[li, :, ci], loc_sems.at[i])
            loc.start()
            locs.append(loc)
            cp = pltpu.make_async_remote_copy(src_ref=x_refs[i], dst_ref=o_refs[gi].at[li, :, ci],
                                              send_sem=send_sems.at[i], recv_sem=recv_sems.at[i],
                                              device_id=(xi, yi, 1 - ci), device_id_type=MESH)
            cp.start()
            sends.append(cp)
        for i, (gi, li) in enumerate(where):
            pltpu.make_async_remote_copy(src_ref=x_refs[i], dst_ref=o_refs[gi].at[li, :, 1 - ci],
                                         send_sem=send_sems.at[i], recv_sem=recv_sems.at[i],
                                         device_id=(xi, yi, 1 - ci), device_id_type=MESH).wait_recv()
        for cp in sends:
            cp.wait_send()
        for loc in locs:
            loc.wait()

    return _comm_call(body, name, flat, out_shapes, n, n)


def _row_block(rows, cols):
    want = max(8, (2 ** 18 // max(cols, 1)) // 8 * 8)
    if rows <= want:
        return rows
    best = 8
    for t in range(8, want + 1, 8):
        if rows % t == 0:
            best = t
    return best


def _pair_add(g4, recv, sel, out_dtype, name):
    A, _, H, C = g4.shape
    bh = _row_block(H, C)

    def body(sel_ref, g_ref, r_ref, o_ref):
        o_ref[...] = (g_ref[...] + r_ref[...]).astype(out_dtype)

    return pl.pallas_call(
        body, name=name, out_shape=jax.ShapeDtypeStruct((A, H, C), out_dtype),
        grid_spec=pltpu.PrefetchScalarGridSpec(
            num_scalar_prefetch=1, grid=(A, H // bh),
            in_specs=[pl.BlockSpec((None, None, bh, C), lambda a, i, s: (a, s[1], i, 0)),
                      pl.BlockSpec((None, bh, C), lambda a, i, s: (a, i, 0))],
            out_specs=pl.BlockSpec((None, bh, C), lambda a, i, s: (a, i, 0))),
        compiler_params=pltpu.CompilerParams(dimension_semantics=("arbitrary", "arbitrary"),
                                             vmem_limit_bytes=VMEM_LIMIT),
    )(sel, g4, recv)


def _chip_reduce(g4, recv1, recv2, sel, name):
    _, _, H, C = g4.shape
    bh = _row_block(H, C)

    def body(sel_ref, g_ref, r1_ref, r2_ref, o_ref):
        acc = g_ref[...] + r1_ref[...]
        for d in range(3):
            acc = acc + r2_ref[d].astype(F32)
        o_ref[...] = acc

    return pl.pallas_call(
        body, name=name, out_shape=jax.ShapeDtypeStruct((1, H, C), F32),
        grid_spec=pltpu.PrefetchScalarGridSpec(
            num_scalar_prefetch=1, grid=(H // bh,),
            in_specs=[pl.BlockSpec((None, None, bh, C), lambda i, s: (s[0], s[1], i, 0)),
                      pl.BlockSpec((None, bh, C), lambda i, s: (s[0], i, 0)),
                      pl.BlockSpec((3, bh, C), lambda i, s: (0, i, 0))],
            out_specs=pl.BlockSpec((None, bh, C), lambda i, s: (0, i, 0))),
        compiler_params=pltpu.CompilerParams(dimension_semantics=("arbitrary",), vmem_limit_bytes=VMEM_LIMIT),
    )(sel, g4, recv1, recv2)


def _adamw(g, w, m, v, name):
    R, C = g.shape
    br = _row_block(R, C)

    def body(g_ref, w_ref, m_ref, v_ref, d_ref, mo_ref, vo_ref):
        gv = g_ref[...]
        mn = ADAM_B1 * m_ref[...] + (1.0 - ADAM_B1) * gv
        vn = ADAM_B2 * v_ref[...] + (1.0 - ADAM_B2) * (gv * gv)
        m_hat = mn / (1.0 - ADAM_B1 ** ADAM_STEP)
        v_hat = vn / (1.0 - ADAM_B2 ** ADAM_STEP)
        d_ref[...] = -ADAM_LR * (m_hat / (jnp.sqrt(v_hat) + ADAM_EPS) + ADAM_WD * w_ref[...])
        mo_ref[...] = mn
        vo_ref[...] = vn

    spec = pl.BlockSpec((br, C), lambda i: (i, 0))
    return pl.pallas_call(
        body, name=name, grid=(R // br,), out_shape=(jax.ShapeDtypeStruct((R, C), F32),) * 3,
        in_specs=[spec] * 4, out_specs=(spec,) * 3, compiler_params=_params(),
    )(g, w, m, v)


def _pack_rows(arrs, total_rows):
    parts = []
    for a in arrs:
        flat = a.reshape(-1)
        pad = (-flat.shape[0]) % LANES
        parts.append(jnp.pad(flat, (0, pad)).reshape(-1, LANES))
    packed = jnp.concatenate(parts, axis=0)
    return jnp.pad(packed, ((0, total_rows - packed.shape[0]), (0, 0)))


def _unpack_rows(packed, shapes):
    out, r = [], 0
    for shp in shapes:
        size = math.prod(shp)
        rows = -(-size // LANES)
        out.append(packed[r:r + rows].reshape(-1)[:size].reshape(shp))
        r += rows
    return out


def _rows_of(shapes):
    return sum(-(-math.prod(s) // LANES) for s in shapes)


def kernel(x, mem, a_w_in, a_sgu_ln_g, a_sgu_ln_b, a_w_s, a_b_s, kv_w, kv_b_f, b_w_q, mem_w_kv, w_o, ln_g, ln_b, w_up, w_down, loss_target, m_a_w_in, m_a_sgu_ln_g, m_a_sgu_ln_b, m_a_w_s, m_a_b_s, m_kv_w, m_kv_b_f, m_b_w_q, m_mem_w_kv, m_w_o, m_ln_g, m_ln_b, m_w_up, m_w_down, v_a_w_in, v_a_sgu_ln_g, v_a_sgu_ln_b, v_a_w_s, v_a_b_s, v_kv_w, v_kv_b_f, v_b_w_q, v_mem_w_kv, v_w_o, v_ln_g, v_ln_b, v_w_up, v_w_down):
    xi, yi, ci = lax.axis_index("x"), lax.axis_index("y"), lax.axis_index("c")
    chip = 2 * xi + yi
    sel = jnp.stack([chip, ci]).astype(jnp.int32)
    S, D = x.shape[1], x.shape[2]
    DQ = mem_w_kv.shape[2] // 2
    DM = D - DQ
    NG = DM // HEAD_DIM
    KVW = kv_w.shape[1] * N_CHIPS

    shards = [a_w_in[0], kv_w, b_w_q[0], mem_w_kv[0], mem_w_kv[1], w_o[0], w_o[1], w_up[0], w_up[1], w_down[0],
              w_down[1]]
    halves = [s.astype(BF16).reshape(2, s.shape[0] // 2, s.shape[1]) for s in shards]
    ln_pack = jnp.concatenate([ln_g.reshape(4, -1), ln_b.reshape(4, -1)], axis=0)
    got = _chip_allgather(halves + [ln_pack], [True] * len(halves) + [False], "gather_weights_ici")
    full = _pair_allgather([[a] for a in got[:-1]], "gather_weights_d2d")
    full = [a.reshape(N_CHIPS, 2 * a.shape[3], a.shape[4]) for a in full]
    wa_g, wkv_g, wq_g, mkv0_g, mkv1_g, wo0_g, wo1_g, wup0, wup1, wdown0, wdown1 = full
    cols = lambda a: a.transpose(1, 0, 2).reshape(a.shape[1], -1)
    rows = lambda a: a.reshape(-1, a.shape[2])
    wkv_full = cols(wkv_g)
    wkv_pad = jnp.concatenate([wkv_full[:, :2 * DM], jnp.pad(wkv_full[:, 2 * DM:], ((0, 0), (0, LANES - NG)))],
                              axis=1)
    ln_full = got[-1].transpose(1, 0, 2).reshape(8, D)
    weights = {
        "wa": cols(wa_g), "wkv": wkv_pad, "wq": rows(wq_g), "mkv": [rows(mkv0_g), rows(mkv1_g)],
        "wo": [rows(wo0_g), rows(wo1_g)], "wup": [wup0, wup1], "wdown": [wdown0, wdown1],
        "sg": a_sgu_ln_g, "sb": a_sgu_ln_b, "ws": a_w_s[0], "b_s": a_b_s[0], "bf": kv_b_f,
        "ln_g": ln_full[:4], "ln_b": ln_full[4:],
    }

    loss_part, grad_x, gr = _local_step(x[0], mem[0], loss_target[0], weights)
    loss = lax.psum(loss_part[0, 0], ("x", "y", "c"))

    by_cols = lambda a, n: a.reshape(a.shape[0], N_CHIPS, n).transpose(1, 0, 2)
    by_rows = lambda a: a.reshape(N_CHIPS, a.shape[0] // N_CHIPS, a.shape[1])
    small_shapes = [a_w_s.shape, a_b_s.shape, a_sgu_ln_g.shape, a_sgu_ln_b.shape, kv_b_f.shape]
    ln_shape = (4, D)
    n_small = _rows_of(small_shapes + [ln_shape, ln_shape])
    n_small = -(-n_small // 64) * 64
    small = _pack_rows([gr["ws"], gr["b_s"], gr["sg"], gr["sb"], gr["bf"], gr["ln_g"], gr["ln_b"]], n_small)
    big = [by_cols(gr["wa"], a_w_in.shape[2]), by_cols(gr["wkv"][:, :KVW], kv_w.shape[1]), by_rows(gr["wq"]),
           by_rows(gr["mkv"][0]), by_rows(gr["mkv"][1]), by_rows(gr["wo"][0]), by_rows(gr["wo"][1]),
           gr["wup"][0], gr["wup"][1], by_rows(gr["wdown"][0]), by_rows(gr["wdown"][1])]
    parts = big + [small.reshape(N_CHIPS, n_small // N_CHIPS, LANES)]
    g4 = [a.reshape(N_CHIPS, 2, a.shape[1] // 2, a.shape[2]) for a in parts]
    recv1 = _pair_swap(g4, "reduce_pair_swap")
    wire = [BF16] * len(big) + [F32]
    summed = [_pair_add(g, r1, sel, dt, f"reduce_pair_add_{i}") for i, (g, r1, dt) in enumerate(zip(g4, recv1, wire))]
    recv2 = _chip_exchange(summed, "reduce_chip_exchange")
    mine = [_chip_reduce(g, r1, r2, sel, f"reduce_chip_sum_{i}") for i, (g, r1, r2) in enumerate(zip(g4, recv1, recv2))]
    groups = [[mine[0]], [mine[1]], [mine[2]], [mine[3], mine[4]], [mine[5], mine[6]], [mine[7], mine[8]],
              [mine[9], mine[10]], [mine[11]]]
    red = _pair_allgather(groups, "reduce_pair_gather")
    red = [a.reshape(a.shape[0], 2 * a.shape[3], a.shape[4]) for a in red]
    g_a_w_in, g_kv_w, g_b_w_q, g_mem_w_kv, g_w_o, g_w_up, g_w_down, small_mine = red
    g_kv_w = g_kv_w[0]
    small_all = _chip_allgather([small_mine[0]], [False], "gather_small_grads")[0].reshape(n_small, LANES)
    g_ws, g_bs, g_sg, g_sb, g_bf, g_lng, g_lnb = _unpack_rows(small_all, small_shapes + [ln_shape, ln_shape])
    dsh = D // N_CHIPS
    g_ln_g = lax.dynamic_slice_in_dim(g_lng, chip * dsh, dsh, axis=1).reshape(ln_g.shape)
    g_ln_b = lax.dynamic_slice_in_dim(g_lnb, chip * dsh, dsh, axis=1).reshape(ln_b.shape)

    def update(name, g, w, m, v):
        flat = lambda a: a.reshape(-1, a.shape[-1])
        d, mn, vn = _adamw(flat(g), flat(w), flat(m), flat(v), "adamw_" + name)
        return d.reshape(w.shape), mn.reshape(w.shape), vn.reshape(w.shape)

    upd = {
        "a_w_in": update("a_w_in", g_a_w_in, a_w_in, m_a_w_in, v_a_w_in),
        "kv_w": update("kv_w", g_kv_w, kv_w, m_kv_w, v_kv_w),
        "b_w_q": update("b_w_q", g_b_w_q, b_w_q, m_b_w_q, v_b_w_q),
        "mem_w_kv": update("mem_w_kv", g_mem_w_kv, mem_w_kv, m_mem_w_kv, v_mem_w_kv),
        "w_o": update("w_o", g_w_o, w_o, m_w_o, v_w_o),
        "w_up": update("w_up", g_w_up, w_up, m_w_up, v_w_up),
        "w_down": update("w_down", g_w_down, w_down, m_w_down, v_w_down),
    }
    tiny_g = [g_sg, g_sb, g_ws, g_bs, g_bf, g_ln_g, g_ln_b]
    tiny_w = [a_sgu_ln_g, a_sgu_ln_b, a_w_s, a_b_s, kv_b_f, ln_g, ln_b]
    tiny_m = [m_a_sgu_ln_g, m_a_sgu_ln_b, m_a_w_s, m_a_b_s, m_kv_b_f, m_ln_g, m_ln_b]
    tiny_v = [v_a_sgu_ln_g, v_a_sgu_ln_b, v_a_w_s, v_a_b_s, v_kv_b_f, v_ln_g, v_ln_b]
    tiny_shapes = [a.shape for a in tiny_w]
    n_tiny = -(-_rows_of(tiny_shapes) // 8) * 8
    td, tm_, tv = _adamw(_pack_rows(tiny_g, n_tiny), _pack_rows(tiny_w, n_tiny), _pack_rows(tiny_m, n_tiny),
                         _pack_rows(tiny_v, n_tiny), "adamw_small")
    for name, d, mn, vn in zip(["a_sgu_ln_g", "a_sgu_ln_b", "a_w_s", "a_b_s", "kv_b_f", "ln_g", "ln_b"],
                               _unpack_rows(td, tiny_shapes), _unpack_rows(tm_, tiny_shapes),
                               _unpack_rows(tv, tiny_shapes)):
        upd[name] = (d, mn, vn)

    order = ["a_w_in", "a_sgu_ln_g", "a_sgu_ln_b", "a_w_s", "a_b_s", "kv_w", "kv_b_f", "b_w_q", "mem_w_kv", "w_o",
             "ln_g", "ln_b", "w_up", "w_down"]
    grads = {"a_w_in": g_a_w_in.reshape(a_w_in.shape), "a_sgu_ln_g": g_sg.reshape(a_sgu_ln_g.shape),
             "a_sgu_ln_b": g_sb.reshape(a_sgu_ln_b.shape), "a_w_s": g_ws.reshape(a_w_s.shape),
             "a_b_s": g_bs.reshape(a_b_s.shape), "kv_w": g_kv_w, "kv_b_f": g_bf, "b_w_q": g_b_w_q.reshape(b_w_q.shape),
             "mem_w_kv": g_mem_w_kv, "w_o": g_w_o, "ln_g": g_ln_g, "ln_b": g_ln_b, "w_up": g_w_up, "w_down": g_w_down}
    return (loss, grad_x[None], *[grads[n] for n in order], *[upd[n][0] for n in order],
            *[upd[n][1] for n in order], *[upd[n][2] for n in order])
```

```python
import functools
import math

import jax
import jax.numpy as jnp
from jax import lax
from jax.experimental import pallas as pl
from jax.experimental.pallas import tpu as pltpu

F32 = jnp.float32
BF16 = jnp.bfloat16

HEAD_DIM = 64
PAIR = 2 * HEAD_DIM
CHUNK = 128
LN_EPS = 1e-5
ALPHA = 4 ** 0.25
QK_SCALE = 1.0 / math.sqrt(HEAD_DIM)
NEG = -1e30
N_CHIPS = 4
FOX_BLOCK = 256
LANES = 128
VMEM_LIMIT = 48 * 2 ** 20

ADAM_LR, ADAM_B1, ADAM_B2, ADAM_EPS, ADAM_WD, ADAM_STEP = 0.001, 0.9, 0.999, 1e-08, 0.01, 10

MESH = pl.DeviceIdType.MESH
ANY = pl.BlockSpec(memory_space=pl.ANY)


def _dot(a, b):
    return jnp.dot(a, b, preferred_element_type=F32)


def _dot_nt(a, b):
    return lax.dot_general(a, b, (((1,), (1,)), ((), ())), preferred_element_type=F32)


def _dot_tn(a, b):
    return lax.dot_general(a, b, (((0,), (0,)), ((), ())), preferred_element_type=F32)


def _ln_stats(r):
    mu = jnp.mean(r, axis=-1, keepdims=True)
    d = r - mu
    var = jnp.mean(d * d, axis=-1, keepdims=True)
    rstd = lax.rsqrt(var + LN_EPS)
    return d * rstd, rstd


def _ln_bwd(dy, xhat, rstd, g):
    dxh = dy * g
    m1 = jnp.mean(dxh, axis=-1, keepdims=True)
    m2 = jnp.mean(dxh * xhat, axis=-1, keepdims=True)
    return rstd * (dxh - m1 - xhat * m2)


_GELU_K = math.sqrt(2.0 / math.pi)


def _gelu(x):
    return 0.5 * x * (1.0 + jnp.tanh(_GELU_K * (x + 0.044715 * x * x * x)))


def _gelu_grad(x):
    t = jnp.tanh(_GELU_K * (x + 0.044715 * x * x * x))
    return 0.5 * (1.0 + t) + 0.5 * x * (1.0 - t * t) * _GELU_K * (1.0 + 3 * 0.044715 * x * x)


def _lane_mask(e):
    lane = lax.broadcasted_iota(jnp.int32, (1, PAIR), 1)
    return (lane >= HEAD_DIM * e) & (lane < HEAD_DIM * (e + 1))


def _first_head():
    return lax.broadcasted_iota(jnp.int32, (1, PAIR), 1) < HEAD_DIM


def _mem_probs(qh, mkp):
    sc = _dot_nt(qh, mkp) * QK_SCALE
    ex = jnp.exp(sc - jnp.max(sc, axis=-1, keepdims=True))
    return ex / jnp.sum(ex, axis=-1, keepdims=True)


def _mem_attn_fwd(qm_b, mkv_ref, dq_dim):
    outs = []
    for mp in range(dq_dim // PAIR):
        qp = qm_b[:, mp * PAIR:(mp + 1) * PAIR]
        mkp = mkv_ref[:, mp * PAIR:(mp + 1) * PAIR]
        mvp = mkv_ref[:, dq_dim + mp * PAIR:dq_dim + (mp + 1) * PAIR]
        heads = []
        for e in (0, 1):
            qh = jnp.where(_lane_mask(e), qp, jnp.zeros_like(qp))
            p = _mem_probs(qh, mkp)
            heads.append(_dot(p.astype(BF16), mvp))
        outs.append(jnp.where(_first_head(), heads[0], heads[1]))
    return outs


def _mem_attn_bwd(qm_b, mkv_ref, dmo_pairs, dmkv_ref, dq_dim):
    dqs = []
    for mp in range(dq_dim // PAIR):
        ks = slice(mp * PAIR, (mp + 1) * PAIR)
        vs = slice(dq_dim + mp * PAIR, dq_dim + (mp + 1) * PAIR)
        qp = qm_b[:, ks]
        mkp = mkv_ref[:, ks]
        mvp = mkv_ref[:, vs]
        dmo_b = dmo_pairs[mp].astype(BF16)
        dq = None
        dmk = None
        dmv = None
        for e in (0, 1):
            hm = _lane_mask(e)
            qh = jnp.where(hm, qp, jnp.zeros_like(qp))
            p = _mem_probs(qh, mkp)
            doh = jnp.where(hm, dmo_b, jnp.zeros_like(dmo_b))
            dp = _dot_nt(doh, mvp)
            ds = p * (dp - jnp.sum(dp * p, axis=-1, keepdims=True))
            dsb = (ds * QK_SCALE).astype(BF16)
            kh = jnp.where(hm, mkp, jnp.zeros_like(mkp))
            dq_e = _dot(dsb, kh)
            dmk_e = _dot_tn(dsb, qh)
            dmv_e = _dot_tn(p.astype(BF16), doh)
            dq = dq_e if dq is None else dq + dq_e
            dmk = dmk_e if dmk is None else dmk + dmk_e
            dmv = dmv_e if dmv is None else dmv + dmv_e
        dmkv_ref[:, ks] += dmk
        dmkv_ref[:, vs] += dmv
        dqs.append(dq)
    return dqs


def _params(n_axes=1):
    return pltpu.CompilerParams(dimension_semantics=("arbitrary",) * n_axes, vmem_limit_bytes=VMEM_LIMIT)


def _full(shape):
    return pl.BlockSpec(shape, lambda *_: (0,) * len(shape))


def _rows(tm, cols):
    return pl.BlockSpec((tm, cols), lambda i: (i, 0))


def _pick(n, pref):
    if n <= pref:
        return n
    best = LANES
    for t in range(LANES, pref + 1, LANES):
        if n % t == 0:
            best = t
    return best if 2 * best >= pref or n > 2 * pref else n


def _mm_tn(a, b, name, out_split=1):
    R, M = a.shape
    _, N = b.shape
    bm, br = _pick(M, 512), _pick(R, 512)
    ncol = N // out_split
    bn = _pick(ncol, 1024)
    per = ncol // bn

    def body(a_ref, b_ref, o_ref):
        @pl.when(pl.program_id(2) == 0)
        def _():
            o_ref[...] = jnp.zeros_like(o_ref)

        o_ref[...] += _dot_tn(a_ref[...], b_ref[...])

    if out_split == 1:
        out_shape = jax.ShapeDtypeStruct((M, N), F32)
        out_spec = pl.BlockSpec((bm, bn), lambda i, j, r: (i, j))
    else:
        out_shape = jax.ShapeDtypeStruct((out_split, M, ncol), F32)
        out_spec = pl.BlockSpec((None, bm, bn), lambda i, j, r: (j // per, i, j % per))
    return pl.pallas_call(
        body, name=name, out_shape=out_shape, grid=(M // bm, N // bn, R // br),
        in_specs=[pl.BlockSpec((br, bm), lambda i, j, r: (r, i)), pl.BlockSpec((br, bn), lambda i, j, r: (r, j))],
        out_specs=out_spec,
        compiler_params=pltpu.CompilerParams(dimension_semantics=("parallel", "parallel", "arbitrary"),
                                             vmem_limit_bytes=VMEM_LIMIT),
    )(a, b)


def _mm_small(a, b, name):
    def body(a_ref, b_ref, o_ref):
        o_ref[...] = _dot(a_ref[...], b_ref[...]).astype(BF16)

    return pl.pallas_call(body, name=name, out_shape=jax.ShapeDtypeStruct((a.shape[0], b.shape[1]), BF16),
                          compiler_params=pltpu.CompilerParams(vmem_limit_bytes=VMEM_LIMIT))(a, b)


def _mlp_fwd(x, wup4, wdown4, g, b, name):
    S, D = x.shape
    NC, _, FC = wup4.shape
    tm = min(256, S)

    def body(x_ref, wup_ref, wdown_ref, g_ref, b_ref, xo_ref, xb_ref, r_ref, h_ref):
        xv = x_ref[...]
        xb = xv.astype(BF16)
        y = jnp.zeros((tm, D), F32)
        for c in range(NC):
            a = jnp.maximum(_dot(xb, wup_ref[c]), 0.0)
            hb = (a * a).astype(BF16)
            h_ref[:, c * FC:(c + 1) * FC] = hb
            y = y + _dot(hb, wdown_ref[c])
        r = ALPHA * xv + y
        xhat, _ = _ln_stats(r)
        xo = xhat * g_ref[...] + b_ref[...]
        xo_ref[...] = xo
        xb_ref[...] = xo.astype(BF16)
        r_ref[...] = r

    return pl.pallas_call(
        body, name=name, grid=(S // tm,),
        out_shape=(jax.ShapeDtypeStruct((S, D), F32), jax.ShapeDtypeStruct((S, D), BF16),
                   jax.ShapeDtypeStruct((S, D), F32), jax.ShapeDtypeStruct((S, NC * FC), BF16)),
        in_specs=[_rows(tm, D), _full(wup4.shape), _full(wdown4.shape), _full((1, D)), _full((1, D))],
        out_specs=(_rows(tm, D), _rows(tm, D), _rows(tm, D), _rows(tm, NC * FC)),
        compiler_params=_params(),
    )(x, wup4, wdown4, g, b)


def _mlp_bwd(dxo, r, h, wup4, wdown4, g, name):
    S, D = r.shape
    NC, _, FC = wup4.shape
    tm = min(256, S)

    def body(dxo_ref, r_ref, h_ref, wup_ref, wdown_ref, g_ref, dx_ref, drb_ref, dpre_ref, dgb_ref):
        @pl.when(pl.program_id(0) == 0)
        def _():
            dgb_ref[...] = jnp.zeros_like(dgb_ref)

        xhat, rstd = _ln_stats(r_ref[...])
        dy = dxo_ref[...]
        dgb_ref[0:1, :] += jnp.sum(dy * xhat, axis=0, keepdims=True)
        dgb_ref[1:2, :] += jnp.sum(dy, axis=0, keepdims=True)
        dr = _ln_bwd(dy, xhat, rstd, g_ref[...])
        drb = dr.astype(BF16)
        drb_ref[...] = drb
        dx = ALPHA * dr
        for c in range(NC):
            dh = _dot_nt(drb, wdown_ref[c])
            a = jnp.sqrt(h_ref[:, c * FC:(c + 1) * FC].astype(F32))
            dpre = (2.0 * a * dh).astype(BF16)
            dpre_ref[:, c * FC:(c + 1) * FC] = dpre
            dx = dx + _dot_nt(dpre, wup_ref[c])
        dx_ref[...] = dx

    return pl.pallas_call(
        body, name=name, grid=(S // tm,),
        out_shape=(jax.ShapeDtypeStruct((S, D), F32), jax.ShapeDtypeStruct((S, D), BF16),
                   jax.ShapeDtypeStruct((S, NC * FC), BF16), jax.ShapeDtypeStruct((2, D), F32)),
        in_specs=[_rows(tm, D), _rows(tm, D), _rows(tm, NC * FC), _full(wup4.shape), _full(wdown4.shape),
                  _full((1, D))],
        out_specs=(_rows(tm, D), _rows(tm, D), _rows(tm, NC * FC), _full((2, D))),
        compiler_params=_params(),
    )(dxo, r, h, wup4, wdown4, g)


def _loss_grad(xo, target, name):
    S, D = xo.shape
    tm = min(512, S)

    def body(x_ref, t_ref, d_ref, l_ref):
        @pl.when(pl.program_id(0) == 0)
        def _():
            l_ref[...] = jnp.zeros_like(l_ref)

        err = x_ref[...] - t_ref[...]
        d_ref[...] = err * (1.0 / D)
        l_ref[...] += (0.5 / D) * jnp.sum(err * err)

    return pl.pallas_call(
        body, name=name, grid=(S // tm,),
        out_shape=(jax.ShapeDtypeStruct((S, D), F32), jax.ShapeDtypeStruct((8, LANES), F32)),
        in_specs=[_rows(tm, D), _rows(tm, D)], out_specs=(_rows(tm, D), _full((8, LANES))),
        compiler_params=_params(),
    )(xo, target)


def _tail_fwd_store(xv, cat_ref, wo_ref, g_ref, b_ref, xo_ref, xb_ref, r_ref):
    y = _dot(cat_ref[...], wo_ref[...])
    r = ALPHA * xv + y
    xhat, _ = _ln_stats(r)
    xo = xhat * g_ref[...] + b_ref[...]
    xo_ref[...] = xo
    xb_ref[...] = xo.astype(BF16)
    r_ref[...] = r


def _tail_bwd_head(dxo_ref, r_ref, g_ref, wo_ref, dgb_ref, drb_ref):
    xhat, rstd = _ln_stats(r_ref[...])
    dy = dxo_ref[...]
    dgb_ref[0:1, :] += jnp.sum(dy * xhat, axis=0, keepdims=True)
    dgb_ref[1:2, :] += jnp.sum(dy, axis=0, keepdims=True)
    dr = _ln_bwd(dy, xhat, rstd, g_ref[...])
    drb = dr.astype(BF16)
    drb_ref[...] = drb
    return dr, _dot_nt(drb, wo_ref[...])


def _causal(shape):
    return lax.broadcasted_iota(jnp.int32, shape, 1) <= lax.broadcasted_iota(jnp.int32, shape, 0)


def _sgu_mixed(ws_ref, bst_ref, gp, vpair):
    tril = _causal((CHUNK, CHUNK))
    w0 = jnp.where(tril, ws_ref[2 * gp], 0.0).astype(BF16)
    w1 = jnp.where(tril, ws_ref[2 * gp + 1], 0.0).astype(BF16)
    m0 = _dot(w0, vpair) + bst_ref[:, 2 * gp:2 * gp + 1]
    m1 = _dot(w1, vpair) + bst_ref[:, 2 * gp + 1:2 * gp + 2]
    return jnp.where(_first_head(), m0, m1), w0, w1


def _mixer_a_fwd(x, wa, sg, sb, ws, bst, mkv, wo, g, b, name):
    S, D = x.shape
    DQ = mkv.shape[1] // 2
    DM = D - DQ
    NP = DM // PAIR
    tm = min(512, S)

    def body(x_ref, wa_ref, sg_ref, sb_ref, ws_ref, bst_ref, mkv_ref, wo_ref, g_ref, b_ref,
             xo_ref, xb_ref, r_ref, cat_ref, z_s, zv_s):
        xv = x_ref[...]
        z_s[...] = _dot(xv.astype(BF16), wa_ref[...])
        vhat, _ = _ln_stats(_gelu(z_s[:, DM:2 * DM]))
        zv_s[...] = (vhat * sg_ref[...] + sb_ref[...]).astype(BF16)
        for c in range(tm // CHUNK):
            rs = slice(c * CHUNK, (c + 1) * CHUNK)
            for gp in range(NP):
                ls = slice(gp * PAIR, (gp + 1) * PAIR)
                mixed, _, _ = _sgu_mixed(ws_ref, bst_ref, gp, zv_s[rs, ls])
                cat_ref[rs, ls] = (_gelu(z_s[rs, ls]) * mixed).astype(BF16)
        mo = _mem_attn_fwd(z_s[:, 2 * DM:].astype(BF16), mkv_ref, DQ)
        for mp in range(DQ // PAIR):
            cat_ref[:, DM + mp * PAIR:DM + (mp + 1) * PAIR] = mo[mp].astype(BF16)
        _tail_fwd_store(xv, cat_ref, wo_ref, g_ref, b_ref, xo_ref, xb_ref, r_ref)

    return pl.pallas_call(
        body, name=name, grid=(S // tm,),
        out_shape=(jax.ShapeDtypeStruct((S, D), F32), jax.ShapeDtypeStruct((S, D), BF16),
                   jax.ShapeDtypeStruct((S, D), F32), jax.ShapeDtypeStruct((S, D), BF16)),
        in_specs=[_rows(tm, D), _full(wa.shape), _full(sg.shape), _full(sb.shape), _full(ws.shape), _full(bst.shape),
                  _full(mkv.shape), _full(wo.shape), _full((1, D)), _full((1, D))],
        out_specs=(_rows(tm, D), _rows(tm, D), _rows(tm, D), _rows(tm, D)),
        scratch_shapes=[pltpu.VMEM((tm, 2 * DM + DQ), F32), pltpu.VMEM((tm, DM), BF16)],
        compiler_params=_params(),
    )(x, wa, sg, sb, ws, bst, mkv, wo, g, b)


def _mixer_a_bwd(x, dxo, r, wa, sg, sb, ws, bst, mkv, wo, g, name):
    S, D = x.shape
    DQ = mkv.shape[1] // 2
    DM = D - DQ
    NP = DM // PAIR
    NG = DM // HEAD_DIM
    tm = min(256, S)

    def body(x_ref, dxo_ref, r_ref, wa_ref, sg_ref, sb_ref, ws_ref, bst_ref, mkv_ref, wo_ref, g_ref,
             dx_ref, dz_ref, drb_ref, dgb_ref, dsgb_ref, dws_ref, dbst_ref, dmkv_ref, z_s, zv_s, dzv_s):
        @pl.when(pl.program_id(0) == 0)
        def _():
            dgb_ref[...] = jnp.zeros_like(dgb_ref)
            dsgb_ref[...] = jnp.zeros_like(dsgb_ref)
            dws_ref[...] = jnp.zeros_like(dws_ref)
            dbst_ref[...] = jnp.zeros_like(dbst_ref)
            dmkv_ref[...] = jnp.zeros_like(dmkv_ref)

        dr, dcat = _tail_bwd_head(dxo_ref, r_ref, g_ref, wo_ref, dgb_ref, drb_ref)
        z_s[...] = _dot(x_ref[...].astype(BF16), wa_ref[...])
        vhat, vrstd = _ln_stats(_gelu(z_s[:, DM:2 * DM]))
        zv_s[...] = (vhat * sg_ref[...] + sb_ref[...]).astype(BF16)
        tril = _causal((CHUNK, CHUNK))
        lane = lax.broadcasted_iota(jnp.int32, (1, LANES), 1)
        for c in range(tm // CHUNK):
            rs = slice(c * CHUNK, (c + 1) * CHUNK)
            for gp in range(NP):
                ls = slice(gp * PAIR, (gp + 1) * PAIR)
                vpair = zv_s[rs, ls]
                mixed, w0, w1 = _sgu_mixed(ws_ref, bst_ref, gp, vpair)
                u_pre = z_s[rs, ls]
                dmix = dcat[rs, ls]
                dz_ref[rs, ls] = (dmix * mixed * _gelu_grad(u_pre)).astype(BF16)
                dmixed = dmix * _gelu(u_pre)
                first = _first_head()
                d0 = jnp.where(first, dmixed, 0.0)
                d1 = jnp.where(first, 0.0, dmixed)
                d0b = d0.astype(BF16)
                d1b = d1.astype(BF16)
                dzv_s[rs, ls] = _dot_tn(w0, d0b) + _dot_tn(w1, d1b)
                dws_ref[2 * gp] += jnp.where(tril, _dot_nt(d0b, vpair), 0.0)
                dws_ref[2 * gp + 1] += jnp.where(tril, _dot_nt(d1b, vpair), 0.0)
                dbst_ref[...] += (jnp.where(lane == 2 * gp, jnp.sum(d0, axis=-1, keepdims=True), 0.0)
                                  + jnp.where(lane == 2 * gp + 1, jnp.sum(d1, axis=-1, keepdims=True), 0.0))
        dzv = dzv_s[...]
        dsgb_ref[0:1, :] += jnp.sum(dzv * vhat, axis=0, keepdims=True)
        dsgb_ref[1:2, :] += jnp.sum(dzv, axis=0, keepdims=True)
        dgv = _ln_bwd(dzv, vhat, vrstd, sg_ref[...])
        dz_ref[:, DM:2 * DM] = (dgv * _gelu_grad(z_s[:, DM:2 * DM])).astype(BF16)
        dmo = [dcat[:, DM + mp * PAIR:DM + (mp + 1) * PAIR] for mp in range(DQ // PAIR)]
        dqm = _mem_attn_bwd(z_s[:, 2 * DM:].astype(BF16), mkv_ref, dmo, dmkv_ref, DQ)
        for mp in range(DQ // PAIR):
            dz_ref[:, 2 * DM + mp * PAIR:2 * DM + (mp + 1) * PAIR] = dqm[mp].astype(BF16)
        dx_ref[...] = ALPHA * dr + _dot_nt(dz_ref[...], wa_ref[...])

    ZW = 2 * DM + DQ
    M = mkv.shape[0]
    return pl.pallas_call(
        body, name=name, grid=(S // tm,),
        out_shape=(jax.ShapeDtypeStruct((S, D), F32), jax.ShapeDtypeStruct((S, ZW), BF16),
                   jax.ShapeDtypeStruct((S, D), BF16), jax.ShapeDtypeStruct((2, D), F32),
                   jax.ShapeDtypeStruct((2, DM), F32), jax.ShapeDtypeStruct((NG, CHUNK, CHUNK), F32),
                   jax.ShapeDtypeStruct((CHUNK, LANES), F32), jax.ShapeDtypeStruct((M, 2 * DQ), F32)),
        in_specs=[_rows(tm, D), _rows(tm, D), _rows(tm, D), _full(wa.shape), _full(sg.shape), _full(sb.shape),
                  _full(ws.shape), _full(bst.shape), _full(mkv.shape), _full(wo.shape), _full((1, D))],
        out_specs=(_rows(tm, D), _rows(tm, ZW), _rows(tm, D), _full((2, D)), _full((2, DM)),
                   _full((NG, CHUNK, CHUNK)), _full((CHUNK, LANES)), _full((M, 2 * DQ))),
        scratch_shapes=[pltpu.VMEM((tm, ZW), F32), pltpu.VMEM((tm, DM), BF16), pltpu.VMEM((tm, DM), F32)],
        compiler_params=_params(),
    )(x, dxo, r, wa, sg, sb, ws, bst, mkv, wo, g)


def _proj_b_fwd(xb, wq, wkv, dm, name):
    S, D = xb.shape
    DQ = D - dm
    tm = min(512, S)

    def body(x_ref, wq_ref, wkv_ref, q_ref, qm_ref, k_ref, v_ref, fl_ref):
        xv = x_ref[...]
        z = _dot(xv, wq_ref[...])
        q_ref[...] = z[:, :dm].astype(BF16)
        qm_ref[...] = z[:, dm:].astype(BF16)
        kvf = _dot(xv, wkv_ref[...])
        k_ref[...] = kvf[:, :dm].astype(BF16)
        v_ref[...] = kvf[:, dm:2 * dm].astype(BF16)
        fl_ref[...] = kvf[:, 2 * dm:]

    return pl.pallas_call(
        body, name=name, grid=(S // tm,),
        out_shape=(jax.ShapeDtypeStruct((S, dm), BF16), jax.ShapeDtypeStruct((S, DQ), BF16),
                   jax.ShapeDtypeStruct((S, dm), BF16), jax.ShapeDtypeStruct((S, dm), BF16),
                   jax.ShapeDtypeStruct((S, LANES), F32)),
        in_specs=[_rows(tm, D), _full(wq.shape), _full(wkv.shape)],
        out_specs=(_rows(tm, dm), _rows(tm, DQ), _rows(tm, dm), _rows(tm, dm), _rows(tm, LANES)),
        compiler_params=_params(),
    )(xb, wq, wkv)


def _proj_b_bwd(dr, dqz, dkvf, wq, wkv, name):
    S, D = dr.shape
    tm = min(512, S)

    def body(dr_ref, dqz_ref, dkvf_ref, wq_ref, wkv_ref, dx_ref):
        dx_ref[...] = (ALPHA * dr_ref[...] + _dot_nt(dqz_ref[...], wq_ref[...])
                       + _dot_nt(dkvf_ref[...], wkv_ref[...]))

    return pl.pallas_call(
        body, name=name, grid=(S // tm,), out_shape=jax.ShapeDtypeStruct((S, D), F32),
        in_specs=[_rows(tm, D), _rows(tm, dqz.shape[1]), _rows(tm, dkvf.shape[1]), _full(wq.shape), _full(wkv.shape)],
        out_specs=_rows(tm, D), compiler_params=_params(),
    )(dr, dqz, dkvf, wq, wkv)


def _split3(v):
    hi = v.astype(BF16)
    r1 = v - hi.astype(F32)
    mid = r1.astype(BF16)
    lo = (r1 - mid.astype(F32)).astype(BF16)
    return hi, mid, lo


def _tri_sum(v, tri_b):
    hi, mid, lo = _split3(v)
    return _dot(hi, tri_b) + _dot(mid, tri_b) + _dot(lo, tri_b)


def _log_sigmoid(x):
    return jnp.minimum(x, 0.0) - jnp.log(1.0 + jnp.exp(-jnp.abs(x)))


def _forget_cumsum(flt, bf, name):
    H, S = flt.shape
    nchunk = S // LANES

    def body(fl_ref, bf_ref, c_ref):
        upper = (lax.broadcasted_iota(jnp.int32, (LANES, LANES), 0)
                 <= lax.broadcasted_iota(jnp.int32, (LANES, LANES), 1)).astype(BF16)

        def step(n, carry):
            s0 = pl.multiple_of(n * LANES, LANES)
            lf = _log_sigmoid(fl_ref[:, pl.ds(s0, LANES)] + bf_ref[...])
            cs = _tri_sum(lf, upper) + carry
            c_ref[:, pl.ds(s0, LANES)] = cs
            return cs[:, LANES - 1:LANES]

        lax.fori_loop(0, nchunk, step, jnp.zeros((H, 1), F32))

    return pl.pallas_call(body, name=name, out_shape=jax.ShapeDtypeStruct((H, S), F32),
                          compiler_params=pltpu.CompilerParams(vmem_limit_bytes=VMEM_LIMIT))(flt, bf)


def _forget_cumsum_bwd(dct, flt, bf, name):
    H, S = flt.shape
    nchunk = S // LANES

    def body(dc_ref, fl_ref, bf_ref, dfl_ref, dbf_ref):
        lower = (lax.broadcasted_iota(jnp.int32, (LANES, LANES), 0)
                 >= lax.broadcasted_iota(jnp.int32, (LANES, LANES), 1)).astype(BF16)

        def step(n, carry):
            tail, tot = carry
            s0 = pl.multiple_of((nchunk - 1 - n) * LANES, LANES)
            suffix = _tri_sum(dc_ref[:, pl.ds(s0, LANES)], lower) + tail
            xv = fl_ref[:, pl.ds(s0, LANES)] + bf_ref[...]
            dfl = suffix * (1.0 / (1.0 + jnp.exp(xv)))
            dfl_ref[:, pl.ds(s0, LANES)] = dfl
            return suffix[:, 0:1], tot + jnp.sum(dfl, axis=-1, keepdims=True)

        _, tot = lax.fori_loop(0, nchunk, step, (jnp.zeros((H, 1), F32), jnp.zeros((H, 1), F32)))
        dbf_ref[...] = jnp.broadcast_to(tot, (H, LANES))

    return pl.pallas_call(body, name=name,
                          out_shape=(jax.ShapeDtypeStruct((H, S), F32), jax.ShapeDtypeStruct((H, LANES), F32)),
                          compiler_params=pltpu.CompilerParams(vmem_limit_bytes=VMEM_LIMIT))(dct, flt, bf)


def _fox_fwd(q, k, v, ct3, name):
    S, DM = q.shape
    NP = DM // PAIR
    T = min(FOX_BLOCK, S)
    nq = S // T

    def body(q_ref, k_ref, v_ref, c_ref, o_ref, lse_ref):
        tril = _causal((T, T))

        def q_block(qi, _):
            t0 = pl.multiple_of(qi * T, T)
            qb = q_ref[pl.ds(t0, T), :]
            outs, lses = [], []
            for e in (0, 1):
                qh = jnp.where(_lane_mask(e), qb, jnp.zeros_like(qb))
                cref = c_ref[e:e + 1, pl.ds(t0, LANES)][:, 0:1]

                def kv_step(kj, carry, masked, qh=qh, cref=cref, e=e):
                    m, l, acc = carry
                    s0 = pl.multiple_of(kj * T, T)
                    s = _dot_nt(qh, k_ref[pl.ds(s0, T), :]) * QK_SCALE + (cref - c_ref[e:e + 1, pl.ds(s0, T)])
                    if masked:
                        s = jnp.where(tril, s, NEG)
                    m_new = jnp.maximum(m, jnp.max(s, axis=-1, keepdims=True))
                    a = jnp.exp(m - m_new)
                    p = jnp.exp(s - m_new)
                    l = a * l + jnp.sum(p, axis=-1, keepdims=True)
                    acc = a * acc + _dot(p.astype(BF16), v_ref[pl.ds(s0, T), :])
                    return m_new, l, acc

                init = (jnp.full((T, 1), NEG, F32), jnp.zeros((T, 1), F32), jnp.zeros((T, PAIR), F32))
                carry = lax.fori_loop(0, qi, functools.partial(kv_step, masked=False), init)
                m, l, acc = kv_step(qi, carry, True)
                outs.append(acc / l)
                lses.append(m + jnp.log(l))
            first = _first_head()
            o_ref[pl.ds(t0, T), :] = jnp.where(first, outs[0], outs[1]).astype(BF16)
            lse_ref[pl.ds(t0, T), :] = jnp.where(first, lses[0], lses[1])
            return 0

        lax.fori_loop(0, nq, q_block, 0)

    col = pl.BlockSpec((S, PAIR), lambda hp: (0, hp))
    return pl.pallas_call(
        body, name=name, grid=(NP,),
        out_shape=(jax.ShapeDtypeStruct((S, DM), BF16), jax.ShapeDtypeStruct((NP, S, LANES), F32)),
        in_specs=[col, col, col, pl.BlockSpec((None, 2, S), lambda hp: (hp, 0, 0))],
        out_specs=(col, pl.BlockSpec((None, S, LANES), lambda hp: (hp, 0, 0))),
        compiler_params=_params(),
    )(q, k, v, ct3)


def _fox_bwd(q, k, v, o, do, ct3, lse, name):
    S, DM = q.shape
    NP = DM // PAIR
    T = min(FOX_BLOCK, S)
    nq = S // T

    def body(q_ref, k_ref, v_ref, o_ref, do_ref, c_ref, lse_ref, dq_ref, dk_ref, dv_ref, dc_ref, dq_acc, drow_acc):
        tril = _causal((T, T))
        lane = lax.broadcasted_iota(jnp.int32, (1, LANES), 1)
        dq_acc[...] = jnp.zeros_like(dq_acc)
        drow_acc[...] = jnp.zeros_like(drow_acc)

        def kv_block(kj, _):
            s0 = pl.multiple_of(kj * T, T)
            kb = k_ref[pl.ds(s0, T), :]
            vb = v_ref[pl.ds(s0, T), :]
            dk_tot = jnp.zeros((T, PAIR), F32)
            dv_tot = jnp.zeros((T, PAIR), F32)
            for e in (0, 1):
                hm = _lane_mask(e)
                kh = jnp.where(hm, kb, jnp.zeros_like(kb))
                crow = c_ref[e:e + 1, pl.ds(s0, T)]

                def q_step(qi, carry, masked, hm=hm, kh=kh, crow=crow, e=e):
                    dk_a, dv_a, dc_a = carry
                    t0 = pl.multiple_of(qi * T, T)
                    qb = q_ref[pl.ds(t0, T), :]
                    dob = do_ref[pl.ds(t0, T), :]
                    doh = jnp.where(hm, dob, jnp.zeros_like(dob))
                    cref = c_ref[e:e + 1, pl.ds(t0, LANES)][:, 0:1]
                    lse_c = lse_ref[pl.ds(t0, T), :][:, HEAD_DIM * e:HEAD_DIM * e + 1]
                    s = _dot_nt(qb, kh) * QK_SCALE + (cref - crow)
                    if masked:
                        s = jnp.where(tril, s, NEG)
                    p = jnp.exp(s - lse_c)
                    dv_a = dv_a + _dot_tn(p.astype(BF16), doh)
                    dp = _dot_nt(doh, vb)
                    dd = jnp.sum(doh.astype(F32) * o_ref[pl.ds(t0, T), :].astype(F32), axis=-1, keepdims=True)
                    ds = p * (dp - dd)
                    dsb = (ds * QK_SCALE).astype(BF16)
                    dq_acc[pl.ds(t0, T), :] += _dot(dsb, kh)
                    dk_a = dk_a + _dot_tn(dsb, jnp.where(hm, qb, jnp.zeros_like(qb)))
                    dc_a = dc_a - jnp.sum(ds, axis=0, keepdims=True)
                    drow_acc[pl.ds(t0, T), :] += jnp.where(lane == e, jnp.sum(ds, axis=-1, keepdims=True), 0.0)
                    return dk_a, dv_a, dc_a

                init = (jnp.zeros((T, PAIR), F32), jnp.zeros((T, PAIR), F32), jnp.zeros((1, T), F32))
                carry = q_step(kj, init, True)
                dk_a, dv_a, dc_a = lax.fori_loop(kj + 1, nq, functools.partial(q_step, masked=False), carry)
                dk_tot = dk_tot + dk_a
                dv_tot = dv_tot + dv_a
                dc_ref[e:e + 1, pl.ds(s0, T)] = dc_a
            dk_ref[pl.ds(s0, T), :] = dk_tot.astype(BF16)
            dv_ref[pl.ds(s0, T), :] = dv_tot.astype(BF16)
            return 0

        lax.fori_loop(0, nq, kv_block, 0)
        dq_ref[...] = dq_acc[...].astype(BF16)

        def add_rows(qi, _):
            t0 = pl.multiple_of(qi * T, T)
            dc_ref[:, pl.ds(t0, T)] += drow_acc[pl.ds(t0, T), :].T[0:2, :]
            return 0

        lax.fori_loop(0, nq, add_rows, 0)

    col = pl.BlockSpec((S, PAIR), lambda hp: (0, hp))
    ct_spec = pl.BlockSpec((None, 2, S), lambda hp: (hp, 0, 0))
    return pl.pallas_call(
        body, name=name, grid=(NP,),
        out_shape=(jax.ShapeDtypeStruct((S, DM), BF16), jax.ShapeDtypeStruct((S, DM), BF16),
                   jax.ShapeDtypeStruct((S, DM), BF16), jax.ShapeDtypeStruct((NP, 2, S), F32)),
        in_specs=[col, col, col, col, col, ct_spec, pl.BlockSpec((None, S, LANES), lambda hp: (hp, 0, 0))],
        out_specs=(col, col, col, ct_spec),
        scratch_shapes=[pltpu.VMEM((S, PAIR), F32), pltpu.VMEM((S, LANES), F32)],
        compiler_params=_params(),
    )(q, k, v, o, do, ct3, lse)


def _mixer_b_fwd(x, o, qm, mkv, wo, g, b, name):
    S, D = x.shape
    DQ = qm.shape[1]
    DM = D - DQ
    tm = min(512, S)

    def body(x_ref, o_ref, qm_ref, mkv_ref, wo_ref, g_ref, b_ref, xo_ref, xb_ref, r_ref, cat_ref):
        cat_ref[:, :DM] = o_ref[...]
        mo = _mem_attn_fwd(qm_ref[...], mkv_ref, DQ)
        for mp in range(DQ // PAIR):
            cat_ref[:, DM + mp * PAIR:DM + (mp + 1) * PAIR] = mo[mp].astype(BF16)
        _tail_fwd_store(x_ref[...], cat_ref, wo_ref, g_ref, b_ref, xo_ref, xb_ref, r_ref)

    return pl.pallas_call(
        body, name=name, grid=(S // tm,),
        out_shape=(jax.ShapeDtypeStruct((S, D), F32), jax.ShapeDtypeStruct((S, D), BF16),
                   jax.ShapeDtypeStruct((S, D), F32), jax.ShapeDtypeStruct((S, D), BF16)),
        in_specs=[_rows(tm, D), _rows(tm, DM), _rows(tm, DQ), _full(mkv.shape), _full(wo.shape), _full((1, D)),
                  _full((1, D))],
        out_specs=(_rows(tm, D), _rows(tm, D), _rows(tm, D), _rows(tm, D)),
        compiler_params=_params(),
    )(x, o, qm, mkv, wo, g, b)


def _mixer_b_bwd(dxo, r, qm, mkv, wo, g, name):
    S, D = r.shape
    DQ = qm.shape[1]
    DM = D - DQ
    M = mkv.shape[0]
    tm = min(512, S)

    def body(dxo_ref, r_ref, qm_ref, mkv_ref, wo_ref, g_ref, dr_ref, drb_ref, dcat_ref, dgb_ref, dmkv_ref):
        @pl.when(pl.program_id(0) == 0)
        def _():
            dgb_ref[...] = jnp.zeros_like(dgb_ref)
            dmkv_ref[...] = jnp.zeros_like(dmkv_ref)

        dr, dcat = _tail_bwd_head(dxo_ref, r_ref, g_ref, wo_ref, dgb_ref, drb_ref)
        dr_ref[...] = dr
        dcat_ref[:, :DM] = dcat[:, :DM].astype(BF16)
        dmo = [dcat[:, DM + mp * PAIR:DM + (mp + 1) * PAIR] for mp in range(DQ // PAIR)]
        dqm = _mem_attn_bwd(qm_ref[...], mkv_ref, dmo, dmkv_ref, DQ)
        for mp in range(DQ // PAIR):
            dcat_ref[:, DM + mp * PAIR:DM + (mp + 1) * PAIR] = dqm[mp].astype(BF16)

    return pl.pallas_call(
        body, name=name, grid=(S // tm,),
        out_shape=(jax.ShapeDtypeStruct((S, D), F32), jax.ShapeDtypeStruct((S, D), BF16),
                   jax.ShapeDtypeStruct((S, D), BF16), jax.ShapeDtypeStruct((2, D), F32),
                   jax.ShapeDtypeStruct((M, 2 * DQ), F32)),
        in_specs=[_rows(tm, D), _rows(tm, D), _rows(tm, DQ), _full(mkv.shape), _full(wo.shape), _full((1, D))],
        out_specs=(_rows(tm, D), _rows(tm, D), _rows(tm, D), _full((2, D)), _full((M, 2 * DQ))),
        compiler_params=_params(),
    )(dxo, r, qm, mkv, wo, g)


def _local_step(x, mem, target, w):
    S, D = x.shape
    DQ = w["mkv"][0].shape[1] // 2
    DM = D - DQ
    NG = DM // HEAD_DIM
    NP = DM // PAIR
    ln_g, ln_b = w["ln_g"], w["ln_b"]
    row = lambda a, i: a[i:i + 1]
    memb = mem.astype(BF16)
    x0b = x.astype(BF16)
    bst = jnp.pad(w["b_s"].T, ((0, 0), (0, LANES - NG)))

    mkv0 = _mm_small(memb, w["mkv"][0], "mem_kv_0")
    x1, x1b, r1, cat0 = _mixer_a_fwd(x, w["wa"], w["sg"], w["sb"], w["ws"], bst, mkv0, w["wo"][0],
                                      row(ln_g, 0), row(ln_b, 0), "mixer_a_fwd")
    x2, x2b, r2, h0 = _mlp_fwd(x1, w["wup"][0], w["wdown"][0], row(ln_g, 1), row(ln_b, 1), "mlp_fwd_0")
    q, qm, k, v, fl = _proj_b_fwd(x2b, w["wq"], w["wkv"], DM, "proj_b_fwd")
    flt = fl[:, :16].T
    bfc = jnp.pad(w["bf"], (0, 16 - NG)).reshape(16, 1)
    ct = _forget_cumsum(flt, bfc, "forget_cumsum")
    ct3 = ct[:NG].reshape(NP, 2, S)
    o, lse = _fox_fwd(q, k, v, ct3, "fox_fwd")
    mkv1 = _mm_small(memb, w["mkv"][1], "mem_kv_1")
    x3, x3b, r3, cat1 = _mixer_b_fwd(x2, o, qm, mkv1, w["wo"][1], row(ln_g, 2), row(ln_b, 2), "mixer_b_fwd")
    x4, _, r4, h1 = _mlp_fwd(x3, w["wup"][1], w["wdown"][1], row(ln_g, 3), row(ln_b, 3), "mlp_fwd_1")

    dx4, loss_part = _loss_grad(x4, target, "loss_grad")
    dx3, dr4b, dpre1, dgb11 = _mlp_bwd(dx4, r4, h1, w["wup"][1], w["wdown"][1], row(ln_g, 3), "mlp_bwd_1")
    dr3, dr3b, dcat1, dgb10, dmkv1 = _mixer_b_bwd(dx3, r3, qm, mkv1, w["wo"][1], row(ln_g, 2), "mixer_b_bwd")
    do = dcat1[:, :DM]
    dq, dk, dv, dct3 = _fox_bwd(q, k, v, o, do, ct3, lse, "fox_bwd")
    dct = jnp.pad(dct3.reshape(NG, S), ((0, 16 - NG), (0, 0)))
    dflt, dbf = _forget_cumsum_bwd(dct, flt, bfc, "forget_cumsum_bwd")
    dfl = jnp.pad(dflt.T, ((0, 0), (0, LANES - 16))).astype(BF16)
    dqz = jnp.concatenate([dq, dcat1[:, DM:]], axis=1)
    dkvf = jnp.concatenate([dk, dv, dfl], axis=1)
    dx2 = _proj_b_bwd(dr3, dqz, dkvf, w["wq"], w["wkv"], "proj_b_bwd")
    dx1, dr2b, dpre0, dgb01 = _mlp_bwd(dx2, r2, h0, w["wup"][0], w["wdown"][0], row(ln_g, 1), "mlp_bwd_0")
    dx0, dz, dr1b, dgb00, dsgb, dws, dbst, dmkv0 = _mixer_a_bwd(
        x, dx1, r1, w["wa"], w["sg"], w["sb"], w["ws"], bst, mkv0, w["wo"][0], row(ln_g, 0), "mixer_a_bwd")

    nc = w["wup"][0].shape[0]
    grads = {
        "wdown": [_mm_tn(h0, dr2b, "dw_down_0"), _mm_tn(h1, dr4b, "dw_down_1")],
        "wup": [_mm_tn(x1b, dpre0, "dw_up_0", out_split=nc), _mm_tn(x3b, dpre1, "dw_up_1", out_split=nc)],
        "wo": [_mm_tn(cat0, dr1b, "dw_o_0"), _mm_tn(cat1, dr3b, "dw_o_1")],
        "wq": _mm_tn(x2b, dqz, "dw_q"),
        "wkv": _mm_tn(x2b, dkvf, "dw_kv"),
        "wa": _mm_tn(x0b, dz, "dw_a"),
        "mkv": [_mm_tn(memb, dmkv0.astype(BF16), "dw_mkv_0"), _mm_tn(memb, dmkv1.astype(BF16), "dw_mkv_1")],
        "ws": dws, "b_s": dbst[:, :NG].T, "sg": dsgb[0:1], "sb": dsgb[1:2], "bf": dbf[:NG, 0],
        "ln_g": jnp.concatenate([dgb00[0:1], dgb01[0:1], dgb10[0:1], dgb11[0:1]], axis=0),
        "ln_b": jnp.concatenate([dgb00[1:2], dgb01[1:2], dgb10[1:2], dgb11[1:2]], axis=0),
    }
    return loss_part, dx0, grads


def _place():
    xi, yi, ci = lax.axis_index("x"), lax.axis_index("y"), lax.axis_index("c")
    peers = [(xi, 1 - yi), (1 - xi, yi), (1 - xi, 1 - yi)]
    return xi, yi, ci, peers


def _comm_call(body, name, ins, out_shapes, n_remote, n_local):
    return pl.pallas_call(
        body, name=name, out_shape=out_shapes, in_specs=[ANY] * len(ins), out_specs=[ANY] * len(out_shapes),
        scratch_shapes=[pltpu.SemaphoreType.DMA((n_remote,)), pltpu.SemaphoreType.DMA((n_remote,)),
                        pltpu.SemaphoreType.DMA((max(n_local, 1),))],
        compiler_params=pltpu.CompilerParams(has_side_effects=True),
    )(*ins)


def _chip_allgather(xs, halved, name):
    n = len(xs)
    out_shapes = [jax.ShapeDtypeStruct((N_CHIPS,) + (a.shape[1:] if hv else a.shape), a.dtype)
                  for a, hv in zip(xs, halved)]

    def body(*refs):
        x_refs, o_refs = refs[:n], refs[n:2 * n]
        send_sems, recv_sems, loc_sems = refs[2 * n:]
        xi, yi, ci, peers = _place()
        j = 2 * xi + yi
        sends, locs = [], []
        for i in range(n):
            src = x_refs[i].at[ci] if halved[i] else x_refs[i]
            loc = pltpu.make_async_copy(src, o_refs[i].at[j], loc_sems.at[i])
            loc.start()
            locs.append(loc)
            for d, (px, py) in enumerate(peers):
                cp = pltpu.make_async_remote_copy(src_ref=src, dst_ref=o_refs[i].at[j], send_sem=send_sems.at[3 * i + d],
                                                  recv_sem=recv_sems.at[3 * i + d], device_id=(px, py, ci),
                                                  device_id_type=MESH)
                cp.start()
                sends.append(cp)
        for i in range(n):
            src = x_refs[i].at[ci] if halved[i] else x_refs[i]
            for d, (px, py) in enumerate(peers):
                pltpu.make_async_remote_copy(src_ref=src, dst_ref=o_refs[i].at[2 * px + py],
                                             send_sem=send_sems.at[3 * i + d], recv_sem=recv_sems.at[3 * i + d],
                                             device_id=(px, py, ci), device_id_type=MESH).wait_recv()
        for cp in sends:
            cp.wait_send()
        for loc in locs:
            loc.wait()

    return _comm_call(body, name, xs, out_shapes, 3 * n, n)


def _pair_swap(xs, name):
    n = len(xs)
    out_shapes = [jax.ShapeDtypeStruct((a.shape[0],) + a.shape[2:], a.dtype) for a in xs]

    def body(*refs):
        x_refs, o_refs = refs[:n], refs[n:2 * n]
        send_sems, recv_sems, _ = refs[2 * n:]
        xi, yi, ci, _ = _place()
        cps = []
        for i in range(n):
            cp = pltpu.make_async_remote_copy(src_ref=x_refs[i].at[:, 1 - ci], dst_ref=o_refs[i],
                                              send_sem=send_sems.at[i], recv_sem=recv_sems.at[i],
                                              device_id=(xi, yi, 1 - ci), device_id_type=MESH)
            cp.start()
            cps.append(cp)
        for cp in cps:
            cp.wait_recv()
        for cp in cps:
            cp.wait_send()

    return _comm_call(body, name, xs, out_shapes, n, 0)


def _chip_exchange(xs, name):
    n = len(xs)
    out_shapes = [jax.ShapeDtypeStruct((3,) + a.shape[1:], a.dtype) for a in xs]

    def body(*refs):
        x_refs, o_refs = refs[:n], refs[n:2 * n]
        send_sems, recv_sems, _ = refs[2 * n:]
        xi, yi, ci, peers = _place()
        cps = []
        for i in range(n):
            for d, (px, py) in enumerate(peers):
                cp = pltpu.make_async_remote_copy(src_ref=x_refs[i].at[2 * px + py], dst_ref=o_refs[i].at[d],
                                                  send_sem=send_sems.at[3 * i + d], recv_sem=recv_sems.at[3 * i + d],
                                                  device_id=(px, py, ci), device_id_type=MESH)
                cp.start()
                cps.append(cp)
        for cp in cps:
            cp.wait_recv()
        for cp in cps:
            cp.wait_send()

    return _comm_call(body, name, xs, out_shapes, 3 * n, 0)


def _inplace_call(body, name, bufs, sem_counts):
    n = len(bufs)
    return pl.pallas_call(
        body, name=name, out_shape=[jax.ShapeDtypeStruct(a.shape, a.dtype) for a in bufs],
        in_specs=[ANY] * n, out_specs=[ANY] * n, input_output_aliases={i: i for i in range(n)},
        scratch_shapes=[pltpu.SemaphoreType.DMA((k,)) for k in sem_counts],
        compiler_params=pltpu.CompilerParams(has_side_effects=True),
    )(*bufs)


def _gather_weights(bufs, ln_buf, name):
    n = len(bufs)

    def body(*refs):
        w = refs[n + 1:2 * n + 1]
        ln = refs[2 * n + 1]
        ici_send, ici_recv, d2d_send, d2d_recv = refs[2 * n + 2:]
        xi, yi, ci, peers = _place()
        j = 2 * xi + yi
        sends = []
        for i in range(n + 1):
            mine = w[i].at[j, ci] if i < n else ln.at[j]
            for d, (px, py) in enumerate(peers):
                cp = pltpu.make_async_remote_copy(src_ref=mine, dst_ref=mine, send_sem=ici_send.at[3 * i + d],
                                                  recv_sem=ici_recv.at[3 * i + d], device_id=(px, py, ci),
                                                  device_id_type=MESH)
                cp.start()
                sends.append(cp)
        for d, (px, py) in enumerate(peers):
            for i in range(n):
                got = w[i].at[2 * px + py, ci]
                pltpu.make_async_remote_copy(src_ref=got, dst_ref=got, send_sem=ici_send.at[3 * i + d],
                                             recv_sem=ici_recv.at[3 * i + d], device_id=(px, py, ci),
                                             device_id_type=MESH).wait_recv()
                fwd = pltpu.make_async_remote_copy(src_ref=got, dst_ref=got, send_sem=d2d_send.at[3 * i + d],
                                                   recv_sem=d2d_recv.at[3 * i + d], device_id=(xi, yi, 1 - ci),
                                                   device_id_type=MESH)
                fwd.start()
                sends.append(fwd)
        for d, (px, py) in enumerate(peers):
            for i in range(n):
                other = w[i].at[2 * px + py, 1 - ci]
                pltpu.make_async_remote_copy(src_ref=other, dst_ref=other, send_sem=d2d_send.at[3 * i + d],
                                             recv_sem=d2d_recv.at[3 * i + d], device_id=(xi, yi, 1 - ci),
                                             device_id_type=MESH).wait_recv()
            got = ln.at[2 * px + py]
            pltpu.make_async_remote_copy(src_ref=got, dst_ref=got, send_sem=ici_send.at[3 * n + d],
                                         recv_sem=ici_recv.at[3 * n + d], device_id=(px, py, ci),
                                         device_id_type=MESH).wait_recv()
        for cp in sends:
            cp.wait_send()

    out = _inplace_call(body, name, list(bufs) + [ln_buf], [3 * n + 3, 3 * n + 3, 3 * n, 3 * n])
    return out[:n], out[n]


def _pair_gather(bufs, name):
    n = len(bufs)
    where = [(i, l) for i, a in enumerate(bufs) for l in range(a.shape[0])]

    def body(*refs):
        g = refs[n:2 * n]
        send_sems, recv_sems = refs[2 * n:]
        xi, yi, ci, _ = _place()
        sends = []
        for k, (i, l) in enumerate(where):
            mine = g[i].at[l, :, ci]
            cp = pltpu.make_async_remote_copy(src_ref=mine, dst_ref=mine, send_sem=send_sems.at[k],
                                              recv_sem=recv_sems.at[k], device_id=(xi, yi, 1 - ci), device_id_type=MESH)
            cp.start()
            sends.append(cp)
        for k, (i, l) in enumerate(where):
            other = g[i].at[l, :, 1 - ci]
            pltpu.make_async_remote_copy(src_ref=other, dst_ref=other, send_sem=send_sems.at[k],
                                         recv_sem=recv_sems.at[k], device_id=(xi, yi, 1 - ci),
                                         device_id_type=MESH).wait_recv()
        for cp in sends:
            cp.wait_send()

    return _inplace_call(body, name, list(bufs), [len(where), len(where)])


def _row_block(rows, cols):
    want = max(8, (2 ** 18 // max(cols, 1)) // 8 * 8)
    if rows <= want:
        return rows
    best = 8
    for t in range(8, want + 1, 8):
        if rows % t == 0:
            best = t
    return best


def _pair_add(g4, recv, sel, out_dtype, name):
    A, _, H, C = g4.shape
    bh = _row_block(H, C)

    def body(sel_ref, g_ref, r_ref, o_ref):
        o_ref[...] = (g_ref[...] + r_ref[...]).astype(out_dtype)

    return pl.pallas_call(
        body, name=name, out_shape=jax.ShapeDtypeStruct((A, H, C), out_dtype),
        grid_spec=pltpu.PrefetchScalarGridSpec(
            num_scalar_prefetch=1, grid=(A, H // bh),
            in_specs=[pl.BlockSpec((None, None, bh, C), lambda a, i, s: (a, s[1], i, 0)),
                      pl.BlockSpec((None, bh, C), lambda a, i, s: (a, i, 0))],
            out_specs=pl.BlockSpec((None, bh, C), lambda a, i, s: (a, i, 0))),
        compiler_params=pltpu.CompilerParams(dimension_semantics=("arbitrary", "arbitrary"),
                                             vmem_limit_bytes=VMEM_LIMIT),
    )(sel, g4, recv)


def _chip_reduce(g4, recv1, recv2, sel, buf, layer, n_layers, name):
    _, _, H, C = g4.shape
    bh = _row_block(H, C)

    def body(sel_ref, g_ref, r1_ref, r2_ref, *rest):
        acc = g_ref[...] + r1_ref[...]
        for d in range(3):
            acc = acc + r2_ref[d].astype(F32)
        rest[-1][...] = acc

    in_specs = [pl.BlockSpec((None, None, bh, C), lambda i, s: (s[0], s[1], i, 0)),
                pl.BlockSpec((None, bh, C), lambda i, s: (s[0], i, 0)),
                pl.BlockSpec((3, bh, C), lambda i, s: (0, i, 0))]
    args = [sel, g4, recv1, recv2]
    aliases = {}
    if buf is not None:
        in_specs.append(ANY)
        args.append(buf)
        aliases = {4: 0}
    return pl.pallas_call(
        body, name=name, out_shape=jax.ShapeDtypeStruct((n_layers, 1, 2, H, C), F32),
        grid_spec=pltpu.PrefetchScalarGridSpec(
            num_scalar_prefetch=1, grid=(H // bh,), in_specs=in_specs,
            out_specs=pl.BlockSpec((None, None, None, bh, C), lambda i, s: (layer, 0, s[1], i, 0))),
        input_output_aliases=aliases,
        compiler_params=pltpu.CompilerParams(dimension_semantics=("arbitrary",), vmem_limit_bytes=VMEM_LIMIT),
    )(*args)


def _cast_place(shard2, sel, name):
    _, H, C = shard2.shape
    bh = _row_block(H, C)

    def body(sel_ref, x_ref, o_ref):
        o_ref[...] = x_ref[...].astype(BF16)

    return pl.pallas_call(
        body, name=name, out_shape=jax.ShapeDtypeStruct((N_CHIPS, 2, H, C), BF16),
        grid_spec=pltpu.PrefetchScalarGridSpec(
            num_scalar_prefetch=1, grid=(2, H // bh),
            in_specs=[pl.BlockSpec((None, bh, C), lambda h, i, s: (h, i, 0))],
            out_specs=pl.BlockSpec((None, None, bh, C), lambda h, i, s: (s[0], h, i, 0))),
        compiler_params=pltpu.CompilerParams(dimension_semantics=("arbitrary", "arbitrary"),
                                             vmem_limit_bytes=VMEM_LIMIT),
    )(sel, shard2)


def _adamw(g, w, m, v, name):
    R, C = g.shape
    br = _row_block(R, C)

    def body(g_ref, w_ref, m_ref, v_ref, d_ref, mo_ref, vo_ref):
        gv = g_ref[...]
        mn = ADAM_B1 * m_ref[...] + (1.0 - ADAM_B1) * gv
        vn = ADAM_B2 * v_ref[...] + (1.0 - ADAM_B2) * (gv * gv)
        m_hat = mn / (1.0 - ADAM_B1 ** ADAM_STEP)
        v_hat = vn / (1.0 - ADAM_B2 ** ADAM_STEP)
        d_ref[...] = -ADAM_LR * (m_hat / (jnp.sqrt(v_hat) + ADAM_EPS) + ADAM_WD * w_ref[...])
        mo_ref[...] = mn
        vo_ref[...] = vn

    spec = pl.BlockSpec((br, C), lambda i: (i, 0))
    return pl.pallas_call(
        body, name=name, grid=(R // br,), out_shape=(jax.ShapeDtypeStruct((R, C), F32),) * 3,
        in_specs=[spec] * 4, out_specs=(spec,) * 3, compiler_params=_params(),
    )(g, w, m, v)


def _pack_rows(arrs, total_rows):
    parts = []
    for a in arrs:
        flat = a.reshape(-1)
        pad = (-flat.shape[0]) % LANES
        parts.append(jnp.pad(flat, (0, pad)).reshape(-1, LANES))
    packed = jnp.concatenate(parts, axis=0)
    return jnp.pad(packed, ((0, total_rows - packed.shape[0]), (0, 0)))


def _unpack_rows(packed, shapes):
    out, r = [], 0
    for shp in shapes:
        size = math.prod(shp)
        rows = -(-size // LANES)
        out.append(packed[r:r + rows].reshape(-1)[:size].reshape(shp))
        r += rows
    return out


def _rows_of(shapes):
    return sum(-(-math.prod(s) // LANES) for s in shapes)


def kernel(x, mem, a_w_in, a_sgu_ln_g, a_sgu_ln_b, a_w_s, a_b_s, kv_w, kv_b_f, b_w_q, mem_w_kv, w_o, ln_g, ln_b, w_up, w_down, loss_target, m_a_w_in, m_a_sgu_ln_g, m_a_sgu_ln_b, m_a_w_s, m_a_b_s, m_kv_w, m_kv_b_f, m_b_w_q, m_mem_w_kv, m_w_o, m_ln_g, m_ln_b, m_w_up, m_w_down, v_a_w_in, v_a_sgu_ln_g, v_a_sgu_ln_b, v_a_w_s, v_a_b_s, v_kv_w, v_kv_b_f, v_b_w_q, v_mem_w_kv, v_w_o, v_ln_g, v_ln_b, v_w_up, v_w_down):
    xi, yi, ci = lax.axis_index("x"), lax.axis_index("y"), lax.axis_index("c")
    chip = 2 * xi + yi
    sel = jnp.stack([chip, ci]).astype(jnp.int32)
    S, D = x.shape[1], x.shape[2]
    DQ = mem_w_kv.shape[2] // 2
    DM = D - DQ
    NG = DM // HEAD_DIM
    KVW = kv_w.shape[1] * N_CHIPS

    shards = [a_w_in[0], kv_w, b_w_q[0], mem_w_kv[0], mem_w_kv[1], w_o[0], w_o[1], w_up[0], w_up[1], w_down[0],
              w_down[1]]
    placed = [_cast_place(s.reshape(2, s.shape[0] // 2, s.shape[1]), sel, f"cast_place_{i}")
              for i, s in enumerate(shards)]
    ln_pack = jnp.concatenate([ln_g.reshape(4, -1), ln_b.reshape(4, -1)], axis=0)
    ln_buf = lax.dynamic_update_slice(jnp.zeros((N_CHIPS,) + ln_pack.shape, F32), ln_pack[None], (chip, 0, 0))
    full, ln_all = _gather_weights(placed, ln_buf, "gather_weights")
    full = [a.reshape(N_CHIPS, 2 * a.shape[2], a.shape[3]) for a in full]
    wa_g, wkv_g, wq_g, mkv0_g, mkv1_g, wo0_g, wo1_g, wup0, wup1, wdown0, wdown1 = full
    cols = lambda a: a.transpose(1, 0, 2).reshape(a.shape[1], -1)
    rows = lambda a: a.reshape(-1, a.shape[2])
    wkv_full = cols(wkv_g)
    wkv_pad = jnp.concatenate([wkv_full[:, :2 * DM], jnp.pad(wkv_full[:, 2 * DM:], ((0, 0), (0, LANES - NG)))],
                              axis=1)
    ln_full = ln_all.transpose(1, 0, 2).reshape(8, D)
    weights = {
        "wa": cols(wa_g), "wkv": wkv_pad, "wq": rows(wq_g), "mkv": [rows(mkv0_g), rows(mkv1_g)],
        "wo": [rows(wo0_g), rows(wo1_g)], "wup": [wup0, wup1], "wdown": [wdown0, wdown1],
        "sg": a_sgu_ln_g, "sb": a_sgu_ln_b, "ws": a_w_s[0], "b_s": a_b_s[0], "bf": kv_b_f,
        "ln_g": ln_full[:4], "ln_b": ln_full[4:],
    }

    loss_part, grad_x, gr = _local_step(x[0], mem[0], loss_target[0], weights)
    loss = lax.psum(loss_part[0, 0], ("x", "y", "c"))

    by_cols = lambda a, n: a.reshape(a.shape[0], N_CHIPS, n).transpose(1, 0, 2)
    by_rows = lambda a: a.reshape(N_CHIPS, a.shape[0] // N_CHIPS, a.shape[1])
    small_shapes = [a_w_s.shape, a_b_s.shape, a_sgu_ln_g.shape, a_sgu_ln_b.shape, kv_b_f.shape]
    ln_shape = (4, D)
    n_small = _rows_of(small_shapes + [ln_shape, ln_shape])
    n_small = -(-n_small // 64) * 64
    small = _pack_rows([gr["ws"], gr["b_s"], gr["sg"], gr["sb"], gr["bf"], gr["ln_g"], gr["ln_b"]], n_small)
    big = [by_cols(gr["wa"], a_w_in.shape[2]), by_cols(gr["wkv"][:, :KVW], kv_w.shape[1]), by_rows(gr["wq"]),
           by_rows(gr["mkv"][0]), by_rows(gr["mkv"][1]), by_rows(gr["wo"][0]), by_rows(gr["wo"][1]),
           gr["wup"][0], gr["wup"][1], by_rows(gr["wdown"][0]), by_rows(gr["wdown"][1])]
    parts = big + [small.reshape(N_CHIPS, n_small // N_CHIPS, LANES)]
    g4 = [a.reshape(N_CHIPS, 2, a.shape[1] // 2, a.shape[2]) for a in parts]
    recv1 = _pair_swap(g4, "reduce_pair_swap")
    wire = [BF16] * len(big) + [F32]
    summed = [_pair_add(g, r1, sel, dt, f"reduce_pair_add_{i}") for i, (g, r1, dt) in enumerate(zip(g4, recv1, wire))]
    recv2 = _chip_exchange(summed, "reduce_chip_exchange")
    groups = [[0], [1], [2], [3, 4], [5, 6], [7, 8], [9, 10], [11]]
    mine = []
    for grp in groups:
        buf = None
        for layer, i in enumerate(grp):
            buf = _chip_reduce(g4[i], recv1[i], recv2[i], sel, buf, layer, len(grp), f"reduce_chip_sum_{i}")
        mine.append(buf)
    red = _pair_gather(mine, "reduce_pair_gather")
    red = [a.reshape(a.shape[0], 2 * a.shape[3], a.shape[4]) for a in red]
    g_a_w_in, g_kv_w, g_b_w_q, g_mem_w_kv, g_w_o, g_w_up, g_w_down, small_mine = red
    g_kv_w = g_kv_w[0]
    small_all = _chip_allgather([small_mine[0]], [False], "gather_small_grads")[0].reshape(n_small, LANES)
    g_ws, g_bs, g_sg, g_sb, g_bf, g_lng, g_lnb = _unpack_rows(small_all, small_shapes + [ln_shape, ln_shape])
    dsh = D // N_CHIPS
    g_ln_g = lax.dynamic_slice_in_dim(g_lng, chip * dsh, dsh, axis=1).reshape(ln_g.shape)
    g_ln_b = lax.dynamic_slice_in_dim(g_lnb, chip * dsh, dsh, axis=1).reshape(ln_b.shape)

    def update(name, g, w, m, v):
        flat = lambda a: a.reshape(-1, a.shape[-1])
        d, mn, vn = _adamw(flat(g), flat(w), flat(m), flat(v), "adamw_" + name)
        return d.reshape(w.shape), mn.reshape(w.shape), vn.reshape(w.shape)

    upd = {
        "a_w_in": update("a_w_in", g_a_w_in, a_w_in, m_a_w_in, v_a_w_in),
        "kv_w": update("kv_w", g_kv_w, kv_w, m_kv_w, v_kv_w),
        "b_w_q": update("b_w_q", g_b_w_q, b_w_q, m_b_w_q, v_b_w_q),
        "mem_w_kv": update("mem_w_kv", g_mem_w_kv, mem_w_kv, m_mem_w_kv, v_mem_w_kv),
        "w_o": update("w_o", g_w_o, w_o, m_w_o, v_w_o),
        "w_up": update("w_up", g_w_up, w_up, m_w_up, v_w_up),
        "w_down": update("w_down", g_w_down, w_down, m_w_down, v_w_down),
    }
    tiny_g = [g_sg, g_sb, g_ws, g_bs, g_bf, g_ln_g, g_ln_b]
    tiny_w = [a_sgu_ln_g, a_sgu_ln_b, a_w_s, a_b_s, kv_b_f, ln_g, ln_b]
    tiny_m = [m_a_sgu_ln_g, m_a_sgu_ln_b, m_a_w_s, m_a_b_s, m_kv_b_f, m_ln_g, m_ln_b]
    tiny_v = [v_a_sgu_ln_g, v_a_sgu_ln_b, v_a_w_s, v_a_b_s, v_kv_b_f, v_ln_g, v_ln_b]
    tiny_shapes = [a.shape for a in tiny_w]
    n_tiny = -(-_rows_of(tiny_shapes) // 8) * 8
    td, tm_, tv = _adamw(_pack_rows(tiny_g, n_tiny), _pack_rows(tiny_w, n_tiny), _pack_rows(tiny_m, n_tiny),
                         _pack_rows(tiny_v, n_tiny), "adamw_small")
    for name, d, mn, vn in zip(["a_sgu_ln_g", "a_sgu_ln_b", "a_w_s", "a_b_s", "kv_b_f", "ln_g", "ln_b"],
                               _unpack_rows(td, tiny_shapes), _unpack_rows(tm_, tiny_shapes),
                               _unpack_rows(tv, tiny_shapes)):
        upd[name] = (d, mn, vn)

    order = ["a_w_in", "a_sgu_ln_g", "a_sgu_ln_b", "a_w_s", "a_b_s", "kv_w", "kv_b_f", "b_w_q", "mem_w_kv", "w_o",
             "ln_g", "ln_b", "w_up", "w_down"]
    grads = {"a_w_in": g_a_w_in.reshape(a_w_in.shape), "a_sgu_ln_g": g_sg.reshape(a_sgu_ln_g.shape),
             "a_sgu_ln_b": g_sb.reshape(a_sgu_ln_b.shape), "a_w_s": g_ws.reshape(a_w_s.shape),
             "a_b_s": g_bs.reshape(a_b_s.shape), "kv_w": g_kv_w, "kv_b_f": g_bf, "b_w_q": g_b_w_q.reshape(b_w_q.shape),
             "mem_w_kv": g_mem_w_kv, "w_o": g_w_o, "ln_g": g_ln_g, "ln_b": g_ln_b, "w_up": g_w_up, "w_down": g_w_down}
    return (loss, grad_x[None], *[grads[n] for n in order], *[upd[n][0] for n in order],
            *[upd[n][1] for n in order], *[upd[n][2] for n in order])
```

```python
import functools
import math

import jax
import jax.numpy as jnp
from jax import lax
from jax.experimental import pallas as pl
from jax.experimental.pallas import tpu as pltpu

F32 = jnp.float32
BF16 = jnp.bfloat16

HEAD_DIM = 64
PAIR = 2 * HEAD_DIM
CHUNK = 128
LN_EPS = 1e-5
ALPHA = 4 ** 0.25
QK_SCALE = 1.0 / math.sqrt(HEAD_DIM)
NEG = -1e30
N_CHIPS = 4
FOX_BLOCK = 256
LANES = 128
VMEM_LIMIT = 48 * 2 ** 20

ADAM_LR, ADAM_B1, ADAM_B2, ADAM_EPS, ADAM_WD, ADAM_STEP = 0.001, 0.9, 0.999, 1e-08, 0.01, 10

MESH = pl.DeviceIdType.MESH
ANY = pl.BlockSpec(memory_space=pl.ANY)


def _dot(a, b):
    return jnp.dot(a, b, preferred_element_type=F32)


def _dot_nt(a, b):
    return lax.dot_general(a, b, (((1,), (1,)), ((), ())), preferred_element_type=F32)


def _dot_tn(a, b):
    return lax.dot_general(a, b, (((0,), (0,)), ((), ())), preferred_element_type=F32)


def _ln_stats(r):
    mu = jnp.mean(r, axis=-1, keepdims=True)
    d = r - mu
    var = jnp.mean(d * d, axis=-1, keepdims=True)
    rstd = lax.rsqrt(var + LN_EPS)
    return d * rstd, rstd


def _ln_bwd(dy, xhat, rstd, g):
    dxh = dy * g
    m1 = jnp.mean(dxh, axis=-1, keepdims=True)
    m2 = jnp.mean(dxh * xhat, axis=-1, keepdims=True)
    return rstd * (dxh - m1 - xhat * m2)


_GELU_K = math.sqrt(2.0 / math.pi)


def _gelu(x):
    return 0.5 * x * (1.0 + jnp.tanh(_GELU_K * (x + 0.044715 * x * x * x)))


def _gelu_grad(x):
    t = jnp.tanh(_GELU_K * (x + 0.044715 * x * x * x))
    return 0.5 * (1.0 + t) + 0.5 * x * (1.0 - t * t) * _GELU_K * (1.0 + 3 * 0.044715 * x * x)


def _lane_mask(e):
    lane = lax.broadcasted_iota(jnp.int32, (1, PAIR), 1)
    return (lane >= HEAD_DIM * e) & (lane < HEAD_DIM * (e + 1))


def _first_head():
    return lax.broadcasted_iota(jnp.int32, (1, PAIR), 1) < HEAD_DIM


def _mem_probs(qh, mkp):
    sc = _dot_nt(qh, mkp) * QK_SCALE
    ex = jnp.exp(sc - jnp.max(sc, axis=-1, keepdims=True))
    return ex / jnp.sum(ex, axis=-1, keepdims=True)


def _mem_attn_fwd(qm_b, mkv_ref, dq_dim):
    outs = []
    for mp in range(dq_dim // PAIR):
        qp = qm_b[:, mp * PAIR:(mp + 1) * PAIR]
        mkp = mkv_ref[:, mp * PAIR:(mp + 1) * PAIR]
        mvp = mkv_ref[:, dq_dim + mp * PAIR:dq_dim + (mp + 1) * PAIR]
        heads = []
        for e in (0, 1):
            qh = jnp.where(_lane_mask(e), qp, jnp.zeros_like(qp))
            p = _mem_probs(qh, mkp)
            heads.append(_dot(p.astype(BF16), mvp))
        outs.append(jnp.where(_first_head(), heads[0], heads[1]))
    return outs


def _mem_attn_bwd(qm_b, mkv_ref, dmo_pairs, dmkv_ref, dq_dim):
    dqs = []
    for mp in range(dq_dim // PAIR):
        ks = slice(mp * PAIR, (mp + 1) * PAIR)
        vs = slice(dq_dim + mp * PAIR, dq_dim + (mp + 1) * PAIR)
        qp = qm_b[:, ks]
        mkp = mkv_ref[:, ks]
        mvp = mkv_ref[:, vs]
        dmo_b = dmo_pairs[mp].astype(BF16)
        dq = None
        dmk = None
        dmv = None
        for e in (0, 1):
            hm = _lane_mask(e)
            qh = jnp.where(hm, qp, jnp.zeros_like(qp))
            p = _mem_probs(qh, mkp)
            doh = jnp.where(hm, dmo_b, jnp.zeros_like(dmo_b))
            dp = _dot_nt(doh, mvp)
            ds = p * (dp - jnp.sum(dp * p, axis=-1, keepdims=True))
            dsb = (ds * QK_SCALE).astype(BF16)
            kh = jnp.where(hm, mkp, jnp.zeros_like(mkp))
            dq_e = _dot(dsb, kh)
            dmk_e = _dot_tn(dsb, qh)
            dmv_e = _dot_tn(p.astype(BF16), doh)
            dq = dq_e if dq is None else dq + dq_e
            dmk = dmk_e if dmk is None else dmk + dmk_e
            dmv = dmv_e if dmv is None else dmv + dmv_e
        dmkv_ref[:, ks] += dmk
        dmkv_ref[:, vs] += dmv
        dqs.append(dq)
    return dqs


def _params(n_axes=1):
    return pltpu.CompilerParams(dimension_semantics=("arbitrary",) * n_axes, vmem_limit_bytes=VMEM_LIMIT)


def _full(shape):
    return pl.BlockSpec(shape, lambda *_: (0,) * len(shape))


def _rows(tm, cols):
    return pl.BlockSpec((tm, cols), lambda i: (i, 0))


def _pick(n, pref):
    if n <= pref:
        return n
    best = LANES
    for t in range(LANES, pref + 1, LANES):
        if n % t == 0:
            best = t
    return best if 2 * best >= pref or n > 2 * pref else n


def _mm_tn(a, b, name, out_split=1):
    R, M = a.shape
    _, N = b.shape
    bm, br = _pick(M, 512), _pick(R, 512)
    ncol = N // out_split
    bn = _pick(ncol, 1024)
    per = ncol // bn

    def body(a_ref, b_ref, o_ref):
        @pl.when(pl.program_id(2) == 0)
        def _():
            o_ref[...] = jnp.zeros_like(o_ref)

        o_ref[...] += _dot_tn(a_ref[...], b_ref[...])

    if out_split == 1:
        out_shape = jax.ShapeDtypeStruct((M, N), F32)
        out_spec = pl.BlockSpec((bm, bn), lambda i, j, r: (i, j))
    else:
        out_shape = jax.ShapeDtypeStruct((out_split, M, ncol), F32)
        out_spec = pl.BlockSpec((None, bm, bn), lambda i, j, r: (j // per, i, j % per))
    return pl.pallas_call(
        body, name=name, out_shape=out_shape, grid=(M // bm, N // bn, R // br),
        in_specs=[pl.BlockSpec((br, bm), lambda i, j, r: (r, i)), pl.BlockSpec((br, bn), lambda i, j, r: (r, j))],
        out_specs=out_spec,
        compiler_params=pltpu.CompilerParams(dimension_semantics=("parallel", "parallel", "arbitrary"),
                                             vmem_limit_bytes=VMEM_LIMIT),
    )(a, b)


def _mm_small(a, b, name):
    def body(a_ref, b_ref, o_ref):
        o_ref[...] = _dot(a_ref[...], b_ref[...]).astype(BF16)

    return pl.pallas_call(body, name=name, out_shape=jax.ShapeDtypeStruct((a.shape[0], b.shape[1]), BF16),
                          compiler_params=pltpu.CompilerParams(vmem_limit_bytes=VMEM_LIMIT))(a, b)


def _mlp_fwd(x, wup4, wdown4, g, b, name):
    S, D = x.shape
    NC, _, FC = wup4.shape
    tm = min(256, S)

    def body(x_ref, wup_ref, wdown_ref, g_ref, b_ref, xo_ref, xb_ref, r_ref, h_ref):
        xv = x_ref[...]
        xb = xv.astype(BF16)
        y = jnp.zeros((tm, D), F32)
        for c in range(NC):
            a = jnp.maximum(_dot(xb, wup_ref[c]), 0.0)
            hb = (a * a).astype(BF16)
            h_ref[:, c * FC:(c + 1) * FC] = hb
            y = y + _dot(hb, wdown_ref[c])
        r = ALPHA * xv + y
        xhat, _ = _ln_stats(r)
        xo = xhat * g_ref[...] + b_ref[...]
        xo_ref[...] = xo
        xb_ref[...] = xo.astype(BF16)
        r_ref[...] = r

    return pl.pallas_call(
        body, name=name, grid=(S // tm,),
        out_shape=(jax.ShapeDtypeStruct((S, D), F32), jax.ShapeDtypeStruct((S, D), BF16),
                   jax.ShapeDtypeStruct((S, D), F32), jax.ShapeDtypeStruct((S, NC * FC), BF16)),
        in_specs=[_rows(tm, D), _full(wup4.shape), _full(wdown4.shape), _full((1, D)), _full((1, D))],
        out_specs=(_rows(tm, D), _rows(tm, D), _rows(tm, D), _rows(tm, NC * FC)),
        compiler_params=_params(),
    )(x, wup4, wdown4, g, b)


def _mlp_bwd(dxo, r, h, wup4, wdown4, g, name):
    S, D = r.shape
    NC, _, FC = wup4.shape
    tm = min(256, S)

    def body(dxo_ref, r_ref, h_ref, wup_ref, wdown_ref, g_ref, dx_ref, drb_ref, dpre_ref, dgb_ref):
        @pl.when(pl.program_id(0) == 0)
        def _():
            dgb_ref[...] = jnp.zeros_like(dgb_ref)

        xhat, rstd = _ln_stats(r_ref[...])
        dy = dxo_ref[...]
        dgb_ref[0:1, :] += jnp.sum(dy * xhat, axis=0, keepdims=True)
        dgb_ref[1:2, :] += jnp.sum(dy, axis=0, keepdims=True)
        dr = _ln_bwd(dy, xhat, rstd, g_ref[...])
        drb = dr.astype(BF16)
        drb_ref[...] = drb
        dx = ALPHA * dr
        for c in range(NC):
            dh = _dot_nt(drb, wdown_ref[c])
            a = jnp.sqrt(h_ref[:, c * FC:(c + 1) * FC].astype(F32))
            dpre = (2.0 * a * dh).astype(BF16)
            dpre_ref[:, c * FC:(c + 1) * FC] = dpre
            dx = dx + _dot_nt(dpre, wup_ref[c])
        dx_ref[...] = dx

    return pl.pallas_call(
        body, name=name, grid=(S // tm,),
        out_shape=(jax.ShapeDtypeStruct((S, D), F32), jax.ShapeDtypeStruct((S, D), BF16),
                   jax.ShapeDtypeStruct((S, NC * FC), BF16), jax.ShapeDtypeStruct((2, D), F32)),
        in_specs=[_rows(tm, D), _rows(tm, D), _rows(tm, NC * FC), _full(wup4.shape), _full(wdown4.shape),
                  _full((1, D))],
        out_specs=(_rows(tm, D), _rows(tm, D), _rows(tm, NC * FC), _full((2, D))),
        compiler_params=_params(),
    )(dxo, r, h, wup4, wdown4, g)


def _loss_grad(xo, target, name):
    S, D = xo.shape
    tm = min(512, S)

    def body(x_ref, t_ref, d_ref, l_ref):
        @pl.when(pl.program_id(0) == 0)
        def _():
            l_ref[...] = jnp.zeros_like(l_ref)

        err = x_ref[...] - t_ref[...]
        d_ref[...] = err * (1.0 / D)
        l_ref[...] += (0.5 / D) * jnp.sum(err * err)

    return pl.pallas_call(
        body, name=name, grid=(S // tm,),
        out_shape=(jax.ShapeDtypeStruct((S, D), F32), jax.ShapeDtypeStruct((8, LANES), F32)),
        in_specs=[_rows(tm, D), _rows(tm, D)], out_specs=(_rows(tm, D), _full((8, LANES))),
        compiler_params=_params(),
    )(xo, target)


def _tail_fwd_store(xv, cat_ref, wo_ref, g_ref, b_ref, xo_ref, xb_ref, r_ref):
    y = _dot(cat_ref[...], wo_ref[...])
    r = ALPHA * xv + y
    xhat, _ = _ln_stats(r)
    xo = xhat * g_ref[...] + b_ref[...]
    xo_ref[...] = xo
    xb_ref[...] = xo.astype(BF16)
    r_ref[...] = r


def _tail_bwd_head(dxo_ref, r_ref, g_ref, wo_ref, dgb_ref, drb_ref):
    xhat, rstd = _ln_stats(r_ref[...])
    dy = dxo_ref[...]
    dgb_ref[0:1, :] += jnp.sum(dy * xhat, axis=0, keepdims=True)
    dgb_ref[1:2, :] += jnp.sum(dy, axis=0, keepdims=True)
    dr = _ln_bwd(dy, xhat, rstd, g_ref[...])
    drb = dr.astype(BF16)
    drb_ref[...] = drb
    return dr, _dot_nt(drb, wo_ref[...])


def _causal(shape):
    return lax.broadcasted_iota(jnp.int32, shape, 1) <= lax.broadcasted_iota(jnp.int32, shape, 0)


def _sgu_mixed(ws_ref, bst_ref, gp, vpair):
    tril = _causal((CHUNK, CHUNK))
    w0 = jnp.where(tril, ws_ref[2 * gp], 0.0).astype(BF16)
    w1 = jnp.where(tril, ws_ref[2 * gp + 1], 0.0).astype(BF16)
    m0 = _dot(w0, vpair) + bst_ref[:, 2 * gp:2 * gp + 1]
    m1 = _dot(w1, vpair) + bst_ref[:, 2 * gp + 1:2 * gp + 2]
    return jnp.where(_first_head(), m0, m1), w0, w1


def _mixer_a_fwd(x, wa, sg, sb, ws, bst, mkv, wo, g, b, name):
    S, D = x.shape
    DQ = mkv.shape[1] // 2
    DM = D - DQ
    NP = DM // PAIR
    tm = min(512, S)

    def body(x_ref, wa_ref, sg_ref, sb_ref, ws_ref, bst_ref, mkv_ref, wo_ref, g_ref, b_ref,
             xo_ref, xb_ref, r_ref, cat_ref, z_s, zv_s):
        xv = x_ref[...]
        z_s[...] = _dot(xv.astype(BF16), wa_ref[...])
        vhat, _ = _ln_stats(_gelu(z_s[:, DM:2 * DM]))
        zv_s[...] = (vhat * sg_ref[...] + sb_ref[...]).astype(BF16)
        for c in range(tm // CHUNK):
            rs = slice(c * CHUNK, (c + 1) * CHUNK)
            for gp in range(NP):
                ls = slice(gp * PAIR, (gp + 1) * PAIR)
                mixed, _, _ = _sgu_mixed(ws_ref, bst_ref, gp, zv_s[rs, ls])
                cat_ref[rs, ls] = (_gelu(z_s[rs, ls]) * mixed).astype(BF16)
        mo = _mem_attn_fwd(z_s[:, 2 * DM:].astype(BF16), mkv_ref, DQ)
        for mp in range(DQ // PAIR):
            cat_ref[:, DM + mp * PAIR:DM + (mp + 1) * PAIR] = mo[mp].astype(BF16)
        _tail_fwd_store(xv, cat_ref, wo_ref, g_ref, b_ref, xo_ref, xb_ref, r_ref)

    return pl.pallas_call(
        body, name=name, grid=(S // tm,),
        out_shape=(jax.ShapeDtypeStruct((S, D), F32), jax.ShapeDtypeStruct((S, D), BF16),
                   jax.ShapeDtypeStruct((S, D), F32), jax.ShapeDtypeStruct((S, D), BF16)),
        in_specs=[_rows(tm, D), _full(wa.shape), _full(sg.shape), _full(sb.shape), _full(ws.shape), _full(bst.shape),
                  _full(mkv.shape), _full(wo.shape), _full((1, D)), _full((1, D))],
        out_specs=(_rows(tm, D), _rows(tm, D), _rows(tm, D), _rows(tm, D)),
        scratch_shapes=[pltpu.VMEM((tm, 2 * DM + DQ), F32), pltpu.VMEM((tm, DM), BF16)],
        compiler_params=_params(),
    )(x, wa, sg, sb, ws, bst, mkv, wo, g, b)


def _mixer_a_bwd(x, dxo, r, wa, sg, sb, ws, bst, mkv, wo, g, name):
    S, D = x.shape
    DQ = mkv.shape[1] // 2
    DM = D - DQ
    NP = DM // PAIR
    NG = DM // HEAD_DIM
    tm = min(256, S)

    def body(x_ref, dxo_ref, r_ref, wa_ref, sg_ref, sb_ref, ws_ref, bst_ref, mkv_ref, wo_ref, g_ref,
             dx_ref, dz_ref, drb_ref, dgb_ref, dsgb_ref, dws_ref, dbst_ref, dmkv_ref, z_s, zv_s, dzv_s):
        @pl.when(pl.program_id(0) == 0)
        def _():
            dgb_ref[...] = jnp.zeros_like(dgb_ref)
            dsgb_ref[...] = jnp.zeros_like(dsgb_ref)
            dws_ref[...] = jnp.zeros_like(dws_ref)
            dbst_ref[...] = jnp.zeros_like(dbst_ref)
            dmkv_ref[...] = jnp.zeros_like(dmkv_ref)

        dr, dcat = _tail_bwd_head(dxo_ref, r_ref, g_ref, wo_ref, dgb_ref, drb_ref)
        z_s[...] = _dot(x_ref[...].astype(BF16), wa_ref[...])
        vhat, vrstd = _ln_stats(_gelu(z_s[:, DM:2 * DM]))
        zv_s[...] = (vhat * sg_ref[...] + sb_ref[...]).astype(BF16)
        tril = _causal((CHUNK, CHUNK))
        lane = lax.broadcasted_iota(jnp.int32, (1, LANES), 1)
        for c in range(tm // CHUNK):
            rs = slice(c * CHUNK, (c + 1) * CHUNK)
            for gp in range(NP):
                ls = slice(gp * PAIR, (gp + 1) * PAIR)
                vpair = zv_s[rs, ls]
                mixed, w0, w1 = _sgu_mixed(ws_ref, bst_ref, gp, vpair)
                u_pre = z_s[rs, ls]
                dmix = dcat[rs, ls]
                dz_ref[rs, ls] = (dmix * mixed * _gelu_grad(u_pre)).astype(BF16)
                dmixed = dmix * _gelu(u_pre)
                first = _first_head()
                d0 = jnp.where(first, dmixed, 0.0)
                d1 = jnp.where(first, 0.0, dmixed)
                d0b = d0.astype(BF16)
                d1b = d1.astype(BF16)
                dzv_s[rs, ls] = _dot_tn(w0, d0b) + _dot_tn(w1, d1b)
                dws_ref[2 * gp] += jnp.where(tril, _dot_nt(d0b, vpair), 0.0)
                dws_ref[2 * gp + 1] += jnp.where(tril, _dot_nt(d1b, vpair), 0.0)
                dbst_ref[...] += (jnp.where(lane == 2 * gp, jnp.sum(d0, axis=-1, keepdims=True), 0.0)
                                  + jnp.where(lane == 2 * gp + 1, jnp.sum(d1, axis=-1, keepdims=True), 0.0))
        dzv = dzv_s[...]
        dsgb_ref[0:1, :] += jnp.sum(dzv * vhat, axis=0, keepdims=True)
        dsgb_ref[1:2, :] += jnp.sum(dzv, axis=0, keepdims=True)
        dgv = _ln_bwd(dzv, vhat, vrstd, sg_ref[...])
        dz_ref[:, DM:2 * DM] = (dgv * _gelu_grad(z_s[:, DM:2 * DM])).astype(BF16)
        dmo = [dcat[:, DM + mp * PAIR:DM + (mp + 1) * PAIR] for mp in range(DQ // PAIR)]
        dqm = _mem_attn_bwd(z_s[:, 2 * DM:].astype(BF16), mkv_ref, dmo, dmkv_ref, DQ)
        for mp in range(DQ // PAIR):
            dz_ref[:, 2 * DM + mp * PAIR:2 * DM + (mp + 1) * PAIR] = dqm[mp].astype(BF16)
        dx_ref[...] = ALPHA * dr + _dot_nt(dz_ref[...], wa_ref[...])

    ZW = 2 * DM + DQ
    M = mkv.shape[0]
    return pl.pallas_call(
        body, name=name, grid=(S // tm,),
        out_shape=(jax.ShapeDtypeStruct((S, D), F32), jax.ShapeDtypeStruct((S, ZW), BF16),
                   jax.ShapeDtypeStruct((S, D), BF16), jax.ShapeDtypeStruct((2, D), F32),
                   jax.ShapeDtypeStruct((2, DM), F32), jax.ShapeDtypeStruct((NG, CHUNK, CHUNK), F32),
                   jax.ShapeDtypeStruct((CHUNK, LANES), F32), jax.ShapeDtypeStruct((M, 2 * DQ), F32)),
        in_specs=[_rows(tm, D), _rows(tm, D), _rows(tm, D), _full(wa.shape), _full(sg.shape), _full(sb.shape),
                  _full(ws.shape), _full(bst.shape), _full(mkv.shape), _full(wo.shape), _full((1, D))],
        out_specs=(_rows(tm, D), _rows(tm, ZW), _rows(tm, D), _full((2, D)), _full((2, DM)),
                   _full((NG, CHUNK, CHUNK)), _full((CHUNK, LANES)), _full((M, 2 * DQ))),
        scratch_shapes=[pltpu.VMEM((tm, ZW), F32), pltpu.VMEM((tm, DM), BF16), pltpu.VMEM((tm, DM), F32)],
        compiler_params=_params(),
    )(x, dxo, r, wa, sg, sb, ws, bst, mkv, wo, g)


def _proj_b_fwd(xb, wq, wkv, dm, name):
    S, D = xb.shape
    DQ = D - dm
    tm = min(512, S)

    def body(x_ref, wq_ref, wkv_ref, q_ref, qm_ref, k_ref, v_ref, fl_ref):
        xv = x_ref[...]
        z = _dot(xv, wq_ref[...])
        q_ref[...] = z[:, :dm].astype(BF16)
        qm_ref[...] = z[:, dm:].astype(BF16)
        kvf = _dot(xv, wkv_ref[...])
        k_ref[...] = kvf[:, :dm].astype(BF16)
        v_ref[...] = kvf[:, dm:2 * dm].astype(BF16)
        fl_ref[...] = kvf[:, 2 * dm:]

    return pl.pallas_call(
        body, name=name, grid=(S // tm,),
        out_shape=(jax.ShapeDtypeStruct((S, dm), BF16), jax.ShapeDtypeStruct((S, DQ), BF16),
                   jax.ShapeDtypeStruct((S, dm), BF16), jax.ShapeDtypeStruct((S, dm), BF16),
                   jax.ShapeDtypeStruct((S, LANES), F32)),
        in_specs=[_rows(tm, D), _full(wq.shape), _full(wkv.shape)],
        out_specs=(_rows(tm, dm), _rows(tm, DQ), _rows(tm, dm), _rows(tm, dm), _rows(tm, LANES)),
        compiler_params=_params(),
    )(xb, wq, wkv)


def _proj_b_bwd(dr, dqz, dkvf, wq, wkv, name):
    S, D = dr.shape
    tm = min(512, S)

    def body(dr_ref, dqz_ref, dkvf_ref, wq_ref, wkv_ref, dx_ref):
        dx_ref[...] = (ALPHA * dr_ref[...] + _dot_nt(dqz_ref[...], wq_ref[...])
                       + _dot_nt(dkvf_ref[...], wkv_ref[...]))

    return pl.pallas_call(
        body, name=name, grid=(S // tm,), out_shape=jax.ShapeDtypeStruct((S, D), F32),
        in_specs=[_rows(tm, D), _rows(tm, dqz.shape[1]), _rows(tm, dkvf.shape[1]), _full(wq.shape), _full(wkv.shape)],
        out_specs=_rows(tm, D), compiler_params=_params(),
    )(dr, dqz, dkvf, wq, wkv)


def _split3(v):
    hi = v.astype(BF16)
    r1 = v - hi.astype(F32)
    mid = r1.astype(BF16)
    lo = (r1 - mid.astype(F32)).astype(BF16)
    return hi, mid, lo


def _tri_sum(v, tri_b):
    hi, mid, lo = _split3(v)
    return _dot(hi, tri_b) + _dot(mid, tri_b) + _dot(lo, tri_b)


def _log_sigmoid(x):
    return jnp.minimum(x, 0.0) - jnp.log(1.0 + jnp.exp(-jnp.abs(x)))


def _forget_cumsum(flt, bf, name):
    H, S = flt.shape
    nchunk = S // LANES

    def body(fl_ref, bf_ref, c_ref):
        upper = (lax.broadcasted_iota(jnp.int32, (LANES, LANES), 0)
                 <= lax.broadcasted_iota(jnp.int32, (LANES, LANES), 1)).astype(BF16)

        def step(n, carry):
            s0 = pl.multiple_of(n * LANES, LANES)
            lf = _log_sigmoid(fl_ref[:, pl.ds(s0, LANES)] + bf_ref[...])
            cs = _tri_sum(lf, upper) + carry
            c_ref[:, pl.ds(s0, LANES)] = cs
            return cs[:, LANES - 1:LANES]

        lax.fori_loop(0, nchunk, step, jnp.zeros((H, 1), F32))

    return pl.pallas_call(body, name=name, out_shape=jax.ShapeDtypeStruct((H, S), F32),
                          compiler_params=pltpu.CompilerParams(vmem_limit_bytes=VMEM_LIMIT))(flt, bf)


def _forget_cumsum_bwd(dct, flt, bf, name):
    H, S = flt.shape
    nchunk = S // LANES

    def body(dc_ref, fl_ref, bf_ref, dfl_ref, dbf_ref):
        lower = (lax.broadcasted_iota(jnp.int32, (LANES, LANES), 0)
                 >= lax.broadcasted_iota(jnp.int32, (LANES, LANES), 1)).astype(BF16)

        def step(n, carry):
            tail, tot = carry
            s0 = pl.multiple_of((nchunk - 1 - n) * LANES, LANES)
            suffix = _tri_sum(dc_ref[:, pl.ds(s0, LANES)], lower) + tail
            xv = fl_ref[:, pl.ds(s0, LANES)] + bf_ref[...]
            dfl = suffix * (1.0 / (1.0 + jnp.exp(xv)))
            dfl_ref[:, pl.ds(s0, LANES)] = dfl
            return suffix[:, 0:1], tot + jnp.sum(dfl, axis=-1, keepdims=True)

        _, tot = lax.fori_loop(0, nchunk, step, (jnp.zeros((H, 1), F32), jnp.zeros((H, 1), F32)))
        dbf_ref[...] = jnp.broadcast_to(tot, (H, LANES))

    return pl.pallas_call(body, name=name,
                          out_shape=(jax.ShapeDtypeStruct((H, S), F32), jax.ShapeDtypeStruct((H, LANES), F32)),
                          compiler_params=pltpu.CompilerParams(vmem_limit_bytes=VMEM_LIMIT))(dct, flt, bf)


def _fox_fwd(q, k, v, ct3, name):
    S, DM = q.shape
    NP = DM // PAIR
    T = min(FOX_BLOCK, S)
    nq = S // T

    def body(q_ref, k_ref, v_ref, c_ref, o_ref, lse_ref):
        tril = _causal((T, T))
        heads = (_lane_mask(0), _lane_mask(1))

        def step(carry, qh, cref, kb, vb, crow, masked):
            m, l, acc = carry
            s = _dot_nt(qh, kb) + (cref - crow)
            if masked:
                s = jnp.where(tril, s, NEG)
            m_new = jnp.maximum(m, jnp.max(s, axis=-1, keepdims=True))
            a = jnp.exp(m - m_new)
            p = jnp.exp(s - m_new)
            l = a * l + jnp.sum(p, axis=-1, keepdims=True)
            acc = a * acc + _dot(p.astype(BF16), vb)
            return m_new, l, acc

        def kv_block(kj):
            s0 = pl.multiple_of(kj * T, T)
            return k_ref[pl.ds(s0, T), :], v_ref[pl.ds(s0, T), :], [c_ref[e:e + 1, pl.ds(s0, T)] for e in (0, 1)]

        def q_pair(a, _):
            t0s = [pl.multiple_of((2 * a + r) * T, T) for r in (0, 1)]
            qh, cref = {}, {}
            for r in (0, 1):
                qb = q_ref[pl.ds(t0s[r], T), :] * QK_SCALE
                for e in (0, 1):
                    qh[r, e] = jnp.where(heads[e], qb, jnp.zeros_like(qb))
                    cref[r, e] = c_ref[e:e + 1, pl.ds(t0s[r], LANES)][:, 0:1]
            chains = [(r, e) for r in (0, 1) for e in (0, 1)]

            def full(kj, carries):
                kb, vb, crow = kv_block(kj)
                return tuple(step(carries[i], qh[r, e], cref[r, e], kb, vb, crow[e], False)
                             for i, (r, e) in enumerate(chains))

            init = (jnp.full((T, 1), NEG, F32), jnp.zeros((T, 1), F32), jnp.zeros((T, PAIR), F32))
            carries = list(lax.fori_loop(0, 2 * a, full, (init,) * 4))
            kb, vb, crow = kv_block(2 * a)
            for i, (r, e) in enumerate(chains):
                carries[i] = step(carries[i], qh[r, e], cref[r, e], kb, vb, crow[e], r == 0)
            kb, vb, crow = kv_block(2 * a + 1)
            for i, (r, e) in enumerate(chains):
                if r == 1:
                    carries[i] = step(carries[i], qh[r, e], cref[r, e], kb, vb, crow[e], True)
            first = _first_head()
            for r in (0, 1):
                (m0, l0, acc0), (m1, l1, acc1) = carries[2 * r], carries[2 * r + 1]
                o_ref[pl.ds(t0s[r], T), :] = jnp.where(first, acc0 / l0, acc1 / l1).astype(BF16)
                lse_ref[pl.ds(t0s[r], T), :] = jnp.where(first, m0 + jnp.log(l0), m1 + jnp.log(l1))
            return 0

        lax.fori_loop(0, nq // 2, q_pair, 0)

    col = pl.BlockSpec((S, PAIR), lambda hp: (0, hp))
    return pl.pallas_call(
        body, name=name, grid=(NP,),
        out_shape=(jax.ShapeDtypeStruct((S, DM), BF16), jax.ShapeDtypeStruct((NP, S, LANES), F32)),
        in_specs=[col, col, col, pl.BlockSpec((None, 2, S), lambda hp: (hp, 0, 0))],
        out_specs=(col, pl.BlockSpec((None, S, LANES), lambda hp: (hp, 0, 0))),
        compiler_params=_params(),
    )(q, k, v, ct3)


def _fox_bwd(q, k, v, o, do, ct3, lse, name):
    S, DM = q.shape
    NP = DM // PAIR
    T = min(FOX_BLOCK, S)
    nq = S // T

    def body(q_ref, k_ref, v_ref, o_ref, do_ref, c_ref, lse_ref, dq_ref, dk_ref, dv_ref, dc_ref,
             dq_acc, drow_acc, dk_acc, dv_acc):
        tril = _causal((T, T))
        lane = lax.broadcasted_iota(jnp.int32, (1, LANES), 1)
        heads = (_lane_mask(0), _lane_mask(1))
        dq_acc[...] = jnp.zeros_like(dq_acc)
        drow_acc[...] = jnp.zeros_like(drow_acc)
        dc_ref[...] = jnp.zeros_like(dc_ref)

        def kv_pair(b, _):
            dk_acc[...] = jnp.zeros_like(dk_acc)
            dv_acc[...] = jnp.zeros_like(dv_acc)
            s0s = [pl.multiple_of((2 * b + jj) * T, T) for jj in (0, 1)]
            kh, ks, vbs, crow = {}, {}, [], {}
            for jj in (0, 1):
                kb = k_ref[pl.ds(s0s[jj], T), :]
                vbs.append(v_ref[pl.ds(s0s[jj], T), :])
                for e in (0, 1):
                    kh[jj, e] = jnp.where(heads[e], kb, jnp.zeros_like(kb))
                    ks[jj, e] = kh[jj, e] * QK_SCALE
                    crow[jj, e] = c_ref[e:e + 1, pl.ds(s0s[jj], T)]

            def q_step(qi, blocks):
                t0 = pl.multiple_of(qi * T, T)
                qs = q_ref[pl.ds(t0, T), :] * QK_SCALE
                dob = do_ref[pl.ds(t0, T), :]
                prod = dob.astype(F32) * o_ref[pl.ds(t0, T), :].astype(F32)
                lse_t = lse_ref[pl.ds(t0, T), :]
                dq = jnp.zeros((T, PAIR), F32)
                drow = jnp.zeros((T, LANES), F32)
                for e in (0, 1):
                    doh = jnp.where(heads[e], dob, jnp.zeros_like(dob))
                    qh = jnp.where(heads[e], qs, jnp.zeros_like(qs))
                    dd = jnp.sum(jnp.where(heads[e], prod, 0.0), axis=-1, keepdims=True)
                    cref = c_ref[e:e + 1, pl.ds(t0, LANES)][:, 0:1]
                    lse_c = lse_t[:, HEAD_DIM * e:HEAD_DIM * e + 1]
                    for jj, masked in blocks:
                        s = _dot_nt(qs, kh[jj, e]) + (cref - crow[jj, e])
                        if masked:
                            s = jnp.where(tril, s, NEG)
                        p = jnp.exp(s - lse_c)
                        dv_acc[jj] += _dot_tn(p.astype(BF16), doh)
                        ds = p * (_dot_nt(doh, vbs[jj]) - dd)
                        dsb = ds.astype(BF16)
                        dq = dq + _dot(dsb, ks[jj, e])
                        dk_acc[jj] += _dot_tn(dsb, qh)
                        dc_ref[e:e + 1, pl.ds(s0s[jj], T)] -= jnp.sum(ds, axis=0, keepdims=True)
                        drow = drow + jnp.where(lane == e, jnp.sum(ds, axis=-1, keepdims=True), 0.0)
                dq_acc[pl.ds(t0, T), :] += dq
                drow_acc[pl.ds(t0, T), :] += drow

            q_step(2 * b, [(0, True)])
            q_step(2 * b + 1, [(0, False), (1, True)])

            def rest(qi, _):
                q_step(qi, [(0, False), (1, False)])
                return 0

            lax.fori_loop(2 * b + 2, nq, rest, 0)
            for jj in (0, 1):
                dk_ref[pl.ds(s0s[jj], T), :] = dk_acc[jj].astype(BF16)
                dv_ref[pl.ds(s0s[jj], T), :] = dv_acc[jj].astype(BF16)
            return 0

        lax.fori_loop(0, nq // 2, kv_pair, 0)
        dq_ref[...] = dq_acc[...].astype(BF16)

        def add_rows(qi, _):
            t0 = pl.multiple_of(qi * T, T)
            dc_ref[:, pl.ds(t0, T)] += drow_acc[pl.ds(t0, T), :].T[0:2, :]
            return 0

        lax.fori_loop(0, nq, add_rows, 0)

    col = pl.BlockSpec((S, PAIR), lambda hp: (0, hp))
    ct_spec = pl.BlockSpec((None, 2, S), lambda hp: (hp, 0, 0))
    return pl.pallas_call(
        body, name=name, grid=(NP,),
        out_shape=(jax.ShapeDtypeStruct((S, DM), BF16), jax.ShapeDtypeStruct((S, DM), BF16),
                   jax.ShapeDtypeStruct((S, DM), BF16), jax.ShapeDtypeStruct((NP, 2, S), F32)),
        in_specs=[col, col, col, col, col, ct_spec, pl.BlockSpec((None, S, LANES), lambda hp: (hp, 0, 0))],
        out_specs=(col, col, col, ct_spec),
        scratch_shapes=[pltpu.VMEM((S, PAIR), F32), pltpu.VMEM((S, LANES), F32), pltpu.VMEM((2, T, PAIR), F32),
                        pltpu.VMEM((2, T, PAIR), F32)],
        compiler_params=_params(),
    )(q, k, v, o, do, ct3, lse)


def _mixer_b_fwd(x, o, qm, mkv, wo, g, b, name):
    S, D = x.shape
    DQ = qm.shape[1]
    DM = D - DQ
    tm = min(512, S)

    def body(x_ref, o_ref, qm_ref, mkv_ref, wo_ref, g_ref, b_ref, xo_ref, xb_ref, r_ref, cat_ref):
        cat_ref[:, :DM] = o_ref[...]
        mo = _mem_attn_fwd(qm_ref[...], mkv_ref, DQ)
        for mp in range(DQ // PAIR):
            cat_ref[:, DM + mp * PAIR:DM + (mp + 1) * PAIR] = mo[mp].astype(BF16)
        _tail_fwd_store(x_ref[...], cat_ref, wo_ref, g_ref, b_ref, xo_ref, xb_ref, r_ref)

    return pl.pallas_call(
        body, name=name, grid=(S // tm,),
        out_shape=(jax.ShapeDtypeStruct((S, D), F32), jax.ShapeDtypeStruct((S, D), BF16),
                   jax.ShapeDtypeStruct((S, D), F32), jax.ShapeDtypeStruct((S, D), BF16)),
        in_specs=[_rows(tm, D), _rows(tm, DM), _rows(tm, DQ), _full(mkv.shape), _full(wo.shape), _full((1, D)),
                  _full((1, D))],
        out_specs=(_rows(tm, D), _rows(tm, D), _rows(tm, D), _rows(tm, D)),
        compiler_params=_params(),
    )(x, o, qm, mkv, wo, g, b)


def _mixer_b_bwd(dxo, r, qm, mkv, wo, g, name):
    S, D = r.shape
    DQ = qm.shape[1]
    DM = D - DQ
    M = mkv.shape[0]
    tm = min(512, S)

    def body(dxo_ref, r_ref, qm_ref, mkv_ref, wo_ref, g_ref, dr_ref, drb_ref, dcat_ref, dgb_ref, dmkv_ref):
        @pl.when(pl.program_id(0) == 0)
        def _():
            dgb_ref[...] = jnp.zeros_like(dgb_ref)
            dmkv_ref[...] = jnp.zeros_like(dmkv_ref)

        dr, dcat = _tail_bwd_head(dxo_ref, r_ref, g_ref, wo_ref, dgb_ref, drb_ref)
        dr_ref[...] = dr
        dcat_ref[:, :DM] = dcat[:, :DM].astype(BF16)
        dmo = [dcat[:, DM + mp * PAIR:DM + (mp + 1) * PAIR] for mp in range(DQ // PAIR)]
        dqm = _mem_attn_bwd(qm_ref[...], mkv_ref, dmo, dmkv_ref, DQ)
        for mp in range(DQ // PAIR):
            dcat_ref[:, DM + mp * PAIR:DM + (mp + 1) * PAIR] = dqm[mp].astype(BF16)

    return pl.pallas_call(
        body, name=name, grid=(S // tm,),
        out_shape=(jax.ShapeDtypeStruct((S, D), F32), jax.ShapeDtypeStruct((S, D), BF16),
                   jax.ShapeDtypeStruct((S, D), BF16), jax.ShapeDtypeStruct((2, D), F32),
                   jax.ShapeDtypeStruct((M, 2 * DQ), F32)),
        in_specs=[_rows(tm, D), _rows(tm, D), _rows(tm, DQ), _full(mkv.shape), _full(wo.shape), _full((1, D))],
        out_specs=(_rows(tm, D), _rows(tm, D), _rows(tm, D), _full((2, D)), _full((M, 2 * DQ))),
        compiler_params=_params(),
    )(dxo, r, qm, mkv, wo, g)


def _local_step(x, mem, target, w):
    S, D = x.shape
    DQ = w["mkv"][0].shape[1] // 2
    DM = D - DQ
    NG = DM // HEAD_DIM
    NP = DM // PAIR
    ln_g, ln_b = w["ln_g"], w["ln_b"]
    row = lambda a, i: a[i:i + 1]
    memb = mem.astype(BF16)
    x0b = x.astype(BF16)
    bst = jnp.pad(w["b_s"].T, ((0, 0), (0, LANES - NG)))

    mkv0 = _mm_small(memb, w["mkv"][0], "mem_kv_0")
    x1, x1b, r1, cat0 = _mixer_a_fwd(x, w["wa"], w["sg"], w["sb"], w["ws"], bst, mkv0, w["wo"][0],
                                      row(ln_g, 0), row(ln_b, 0), "mixer_a_fwd")
    x2, x2b, r2, h0 = _mlp_fwd(x1, w["wup"][0], w["wdown"][0], row(ln_g, 1), row(ln_b, 1), "mlp_fwd_0")
    q, qm, k, v, fl = _proj_b_fwd(x2b, w["wq"], w["wkv"], DM, "proj_b_fwd")
    flt = fl[:, :16].T
    bfc = jnp.pad(w["bf"], (0, 16 - NG)).reshape(16, 1)
    ct = _forget_cumsum(flt, bfc, "forget_cumsum")
    ct3 = ct[:NG].reshape(NP, 2, S)
    o, lse = _fox_fwd(q, k, v, ct3, "fox_fwd")
    mkv1 = _mm_small(memb, w["mkv"][1], "mem_kv_1")
    x3, x3b, r3, cat1 = _mixer_b_fwd(x2, o, qm, mkv1, w["wo"][1], row(ln_g, 2), row(ln_b, 2), "mixer_b_fwd")
    x4, _, r4, h1 = _mlp_fwd(x3, w["wup"][1], w["wdown"][1], row(ln_g, 3), row(ln_b, 3), "mlp_fwd_1")

    dx4, loss_part = _loss_grad(x4, target, "loss_grad")
    dx3, dr4b, dpre1, dgb11 = _mlp_bwd(dx4, r4, h1, w["wup"][1], w["wdown"][1], row(ln_g, 3), "mlp_bwd_1")
    dr3, dr3b, dcat1, dgb10, dmkv1 = _mixer_b_bwd(dx3, r3, qm, mkv1, w["wo"][1], row(ln_g, 2), "mixer_b_bwd")
    do = dcat1[:, :DM]
    dq, dk, dv, dct3 = _fox_bwd(q, k, v, o, do, ct3, lse, "fox_bwd")
    dct = jnp.pad(dct3.reshape(NG, S), ((0, 16 - NG), (0, 0)))
    dflt, dbf = _forget_cumsum_bwd(dct, flt, bfc, "forget_cumsum_bwd")
    dfl = jnp.pad(dflt.T, ((0, 0), (0, LANES - 16))).astype(BF16)
    dqz = jnp.concatenate([dq, dcat1[:, DM:]], axis=1)
    dkvf = jnp.concatenate([dk, dv, dfl], axis=1)
    dx2 = _proj_b_bwd(dr3, dqz, dkvf, w["wq"], w["wkv"], "proj_b_bwd")
    dx1, dr2b, dpre0, dgb01 = _mlp_bwd(dx2, r2, h0, w["wup"][0], w["wdown"][0], row(ln_g, 1), "mlp_bwd_0")
    dx0, dz, dr1b, dgb00, dsgb, dws, dbst, dmkv0 = _mixer_a_bwd(
        x, dx1, r1, w["wa"], w["sg"], w["sb"], w["ws"], bst, mkv0, w["wo"][0], row(ln_g, 0), "mixer_a_bwd")

    nc = w["wup"][0].shape[0]
    grads = {
        "wdown": [_mm_tn(h0, dr2b, "dw_down_0"), _mm_tn(h1, dr4b, "dw_down_1")],
        "wup": [_mm_tn(x1b, dpre0, "dw_up_0", out_split=nc), _mm_tn(x3b, dpre1, "dw_up_1", out_split=nc)],
        "wo": [_mm_tn(cat0, dr1b, "dw_o_0"), _mm_tn(cat1, dr3b, "dw_o_1")],
        "wq": _mm_tn(x2b, dqz, "dw_q"),
        "wkv": _mm_tn(x2b, dkvf, "dw_kv"),
        "wa": _mm_tn(x0b, dz, "dw_a"),
        "mkv": [_mm_tn(memb, dmkv0.astype(BF16), "dw_mkv_0"), _mm_tn(memb, dmkv1.astype(BF16), "dw_mkv_1")],
        "ws": dws, "b_s": dbst[:, :NG].T, "sg": dsgb[0:1], "sb": dsgb[1:2], "bf": dbf[:NG, 0],
        "ln_g": jnp.concatenate([dgb00[0:1], dgb01[0:1], dgb10[0:1], dgb11[0:1]], axis=0),
        "ln_b": jnp.concatenate([dgb00[1:2], dgb01[1:2], dgb10[1:2], dgb11[1:2]], axis=0),
    }
    return loss_part, dx0, grads


def _place():
    xi, yi, ci = lax.axis_index("x"), lax.axis_index("y"), lax.axis_index("c")
    peers = [(xi, 1 - yi), (1 - xi, yi), (1 - xi, 1 - yi)]
    return xi, yi, ci, peers


def _comm_call(body, name, ins, out_shapes, n_remote, n_local):
    return pl.pallas_call(
        body, name=name, out_shape=out_shapes, in_specs=[ANY] * len(ins), out_specs=[ANY] * len(out_shapes),
        scratch_shapes=[pltpu.SemaphoreType.DMA((n_remote,)), pltpu.SemaphoreType.DMA((n_remote,)),
                        pltpu.SemaphoreType.DMA((max(n_local, 1),))],
        compiler_params=pltpu.CompilerParams(has_side_effects=True),
    )(*ins)


def _chip_allgather(xs, halved, name):
    n = len(xs)
    out_shapes = [jax.ShapeDtypeStruct((N_CHIPS,) + (a.shape[1:] if hv else a.shape), a.dtype)
                  for a, hv in zip(xs, halved)]

    def body(*refs):
        x_refs, o_refs = refs[:n], refs[n:2 * n]
        send_sems, recv_sems, loc_sems = refs[2 * n:]
        xi, yi, ci, peers = _place()
        j = 2 * xi + yi
        sends, locs = [], []
        for i in range(n):
            src = x_refs[i].at[ci] if halved[i] else x_refs[i]
            loc = pltpu.make_async_copy(src, o_refs[i].at[j], loc_sems.at[i])
            loc.start()
            locs.append(loc)
            for d, (px, py) in enumerate(peers):
                cp = pltpu.make_async_remote_copy(src_ref=src, dst_ref=o_refs[i].at[j], send_sem=send_sems.at[3 * i + d],
                                                  recv_sem=recv_sems.at[3 * i + d], device_id=(px, py, ci),
                                                  device_id_type=MESH)
                cp.start()
                sends.append(cp)
        for i in range(n):
            src = x_refs[i].at[ci] if halved[i] else x_refs[i]
            for d, (px, py) in enumerate(peers):
                pltpu.make_async_remote_copy(src_ref=src, dst_ref=o_refs[i].at[2 * px + py],
                                             send_sem=send_sems.at[3 * i + d], recv_sem=recv_sems.at[3 * i + d],
                                             device_id=(px, py, ci), device_id_type=MESH).wait_recv()
        for cp in sends:
            cp.wait_send()
        for loc in locs:
            loc.wait()

    return _comm_call(body, name, xs, out_shapes, 3 * n, n)


def _pair_swap(xs, name):
    n = len(xs)
    out_shapes = [jax.ShapeDtypeStruct((a.shape[0],) + a.shape[2:], a.dtype) for a in xs]

    def body(*refs):
        x_refs, o_refs = refs[:n], refs[n:2 * n]
        send_sems, recv_sems, _ = refs[2 * n:]
        xi, yi, ci, _ = _place()
        cps = []
        for i in range(n):
            cp = pltpu.make_async_remote_copy(src_ref=x_refs[i].at[:, 1 - ci], dst_ref=o_refs[i],
                                              send_sem=send_sems.at[i], recv_sem=recv_sems.at[i],
                                              device_id=(xi, yi, 1 - ci), device_id_type=MESH)
            cp.start()
            cps.append(cp)
        for cp in cps:
            cp.wait_recv()
        for cp in cps:
            cp.wait_send()

    return _comm_call(body, name, xs, out_shapes, n, 0)


def _chip_exchange(xs, name):
    n = len(xs)
    out_shapes = [jax.ShapeDtypeStruct((3,) + a.shape[1:], a.dtype) for a in xs]

    def body(*refs):
        x_refs, o_refs = refs[:n], refs[n:2 * n]
        send_sems, recv_sems, _ = refs[2 * n:]
        xi, yi, ci, peers = _place()
        cps = []
        for i in range(n):
            for d, (px, py) in enumerate(peers):
                cp = pltpu.make_async_remote_copy(src_ref=x_refs[i].at[2 * px + py], dst_ref=o_refs[i].at[d],
                                                  send_sem=send_sems.at[3 * i + d], recv_sem=recv_sems.at[3 * i + d],
                                                  device_id=(px, py, ci), device_id_type=MESH)
                cp.start()
                cps.append(cp)
        for cp in cps:
            cp.wait_recv()
        for cp in cps:
            cp.wait_send()

    return _comm_call(body, name, xs, out_shapes, 3 * n, 0)


def _inplace_call(body, name, bufs, sem_counts):
    n = len(bufs)
    return pl.pallas_call(
        body, name=name, out_shape=[jax.ShapeDtypeStruct(a.shape, a.dtype) for a in bufs],
        in_specs=[ANY] * n, out_specs=[ANY] * n, input_output_aliases={i: i for i in range(n)},
        scratch_shapes=[pltpu.SemaphoreType.DMA((k,)) for k in sem_counts],
        compiler_params=pltpu.CompilerParams(has_side_effects=True),
    )(*bufs)


def _gather_weights(bufs, ln_buf, name):
    n = len(bufs)

    def body(*refs):
        w = refs[n + 1:2 * n + 1]
        ln = refs[2 * n + 1]
        ici_send, ici_recv, d2d_send, d2d_recv = refs[2 * n + 2:]
        xi, yi, ci, peers = _place()
        j = 2 * xi + yi
        sends = []
        for i in range(n + 1):
            mine = w[i].at[j, ci] if i < n else ln.at[j]
            for d, (px, py) in enumerate(peers):
                cp = pltpu.make_async_remote_copy(src_ref=mine, dst_ref=mine, send_sem=ici_send.at[3 * i + d],
                                                  recv_sem=ici_recv.at[3 * i + d], device_id=(px, py, ci),
                                                  device_id_type=MESH)
                cp.start()
                sends.append(cp)
        for d, (px, py) in enumerate(peers):
            for i in range(n):
                got = w[i].at[2 * px + py, ci]
                pltpu.make_async_remote_copy(src_ref=got, dst_ref=got, send_sem=ici_send.at[3 * i + d],
                                             recv_sem=ici_recv.at[3 * i + d], device_id=(px, py, ci),
                                             device_id_type=MESH).wait_recv()
                fwd = pltpu.make_async_remote_copy(src_ref=got, dst_ref=got, send_sem=d2d_send.at[3 * i + d],
                                                   recv_sem=d2d_recv.at[3 * i + d], device_id=(xi, yi, 1 - ci),
                                                   device_id_type=MESH)
                fwd.start()
                sends.append(fwd)
        for d, (px, py) in enumerate(peers):
            for i in range(n):
                other = w[i].at[2 * px + py, 1 - ci]
                pltpu.make_async_remote_copy(src_ref=other, dst_ref=other, send_sem=d2d_send.at[3 * i + d],
                                             recv_sem=d2d_recv.at[3 * i + d], device_id=(xi, yi, 1 - ci),
                                             device_id_type=MESH).wait_recv()
            got = ln.at[2 * px + py]
            pltpu.make_async_remote_copy(src_ref=got, dst_ref=got, send_sem=ici_send.at[3 * n + d],
                                         recv_sem=ici_recv.at[3 * n + d], device_id=(px, py, ci),
                                         device_id_type=MESH).wait_recv()
        for cp in sends:
            cp.wait_send()

    out = _inplace_call(body, name, list(bufs) + [ln_buf], [3 * n + 3, 3 * n + 3, 3 * n, 3 * n])
    return out[:n], out[n]


def _pair_gather(bufs, name):
    n = len(bufs)
    where = [(i, l) for i, a in enumerate(bufs) for l in range(a.shape[0])]

    def body(*refs):
        g = refs[n:2 * n]
        send_sems, recv_sems = refs[2 * n:]
        xi, yi, ci, _ = _place()
        sends = []
        for k, (i, l) in enumerate(where):
            mine = g[i].at[l, :, ci]
            cp = pltpu.make_async_remote_copy(src_ref=mine, dst_ref=mine, send_sem=send_sems.at[k],
                                              recv_sem=recv_sems.at[k], device_id=(xi, yi, 1 - ci), device_id_type=MESH)
            cp.start()
            sends.append(cp)
        for k, (i, l) in enumerate(where):
            other = g[i].at[l, :, 1 - ci]
            pltpu.make_async_remote_copy(src_ref=other, dst_ref=other, send_sem=send_sems.at[k],
                                         recv_sem=recv_sems.at[k], device_id=(xi, yi, 1 - ci),
                                         device_id_type=MESH).wait_recv()
        for cp in sends:
            cp.wait_send()

    return _inplace_call(body, name, list(bufs), [len(where), len(where)])


def _row_block(rows, cols):
    want = max(8, (2 ** 18 // max(cols, 1)) // 8 * 8)
    if rows <= want:
        return rows
    best = 8
    for t in range(8, want + 1, 8):
        if rows % t == 0:
            best = t
    return best


def _pair_add(g4, recv, sel, out_dtype, name):
    A, _, H, C = g4.shape
    bh = _row_block(H, C)

    def body(sel_ref, g_ref, r_ref, o_ref):
        o_ref[...] = (g_ref[...] + r_ref[...]).astype(out_dtype)

    return pl.pallas_call(
        body, name=name, out_shape=jax.ShapeDtypeStruct((A, H, C), out_dtype),
        grid_spec=pltpu.PrefetchScalarGridSpec(
            num_scalar_prefetch=1, grid=(A, H // bh),
            in_specs=[pl.BlockSpec((None, None, bh, C), lambda a, i, s: (a, s[1], i, 0)),
                      pl.BlockSpec((None, bh, C), lambda a, i, s: (a, i, 0))],
            out_specs=pl.BlockSpec((None, bh, C), lambda a, i, s: (a, i, 0))),
        compiler_params=pltpu.CompilerParams(dimension_semantics=("arbitrary", "arbitrary"),
                                             vmem_limit_bytes=VMEM_LIMIT),
    )(sel, g4, recv)


def _chip_reduce(g4, recv1, recv2, sel, buf, layer, n_layers, name):
    _, _, H, C = g4.shape
    bh = _row_block(H, C)

    def body(sel_ref, g_ref, r1_ref, r2_ref, *rest):
        acc = g_ref[...] + r1_ref[...]
        for d in range(3):
            acc = acc + r2_ref[d].astype(F32)
        rest[-1][...] = acc

    in_specs = [pl.BlockSpec((None, None, bh, C), lambda i, s: (s[0], s[1], i, 0)),
                pl.BlockSpec((None, bh, C), lambda i, s: (s[0], i, 0)),
                pl.BlockSpec((3, bh, C), lambda i, s: (0, i, 0))]
    args = [sel, g4, recv1, recv2]
    aliases = {}
    if buf is not None:
        in_specs.append(ANY)
        args.append(buf)
        aliases = {4: 0}
    return pl.pallas_call(
        body, name=name, out_shape=jax.ShapeDtypeStruct((n_layers, 1, 2, H, C), F32),
        grid_spec=pltpu.PrefetchScalarGridSpec(
            num_scalar_prefetch=1, grid=(H // bh,), in_specs=in_specs,
            out_specs=pl.BlockSpec((None, None, None, bh, C), lambda i, s: (layer, 0, s[1], i, 0))),
        input_output_aliases=aliases,
        compiler_params=pltpu.CompilerParams(dimension_semantics=("arbitrary",), vmem_limit_bytes=VMEM_LIMIT),
    )(*args)


def _cast_place(shard2, sel, name):
    _, H, C = shard2.shape
    bh = _row_block(H, C)

    def body(sel_ref, x_ref, o_ref):
        o_ref[...] = x_ref[...].astype(BF16)

    return pl.pallas_call(
        body, name=name, out_shape=jax.ShapeDtypeStruct((N_CHIPS, 2, H, C), BF16),
        grid_spec=pltpu.PrefetchScalarGridSpec(
            num_scalar_prefetch=1, grid=(2, H // bh),
            in_specs=[pl.BlockSpec((None, bh, C), lambda h, i, s: (h, i, 0))],
            out_specs=pl.BlockSpec((None, None, bh, C), lambda h, i, s: (s[0], h, i, 0))),
        compiler_params=pltpu.CompilerParams(dimension_semantics=("arbitrary", "arbitrary"),
                                             vmem_limit_bytes=VMEM_LIMIT),
    )(sel, shard2)


def _adamw(g, w, m, v, name):
    R, C = g.shape
    br = _row_block(R, C)

    def body(g_ref, w_ref, m_ref, v_ref, d_ref, mo_ref, vo_ref):
        gv = g_ref[...]
        mn = ADAM_B1 * m_ref[...] + (1.0 - ADAM_B1) * gv
        vn = ADAM_B2 * v_ref[...] + (1.0 - ADAM_B2) * (gv * gv)
        m_hat = mn / (1.0 - ADAM_B1 ** ADAM_STEP)
        v_hat = vn / (1.0 - ADAM_B2 ** ADAM_STEP)
        d_ref[...] = -ADAM_LR * (m_hat / (jnp.sqrt(v_hat) + ADAM_EPS) + ADAM_WD * w_ref[...])
        mo_ref[...] = mn
        vo_ref[...] = vn

    spec = pl.BlockSpec((br, C), lambda i: (i, 0))
    return pl.pallas_call(
        body, name=name, grid=(R // br,), out_shape=(jax.ShapeDtypeStruct((R, C), F32),) * 3,
        in_specs=[spec] * 4, out_specs=(spec,) * 3, compiler_params=_params(),
    )(g, w, m, v)


def _pack_rows(arrs, total_rows):
    parts = []
    for a in arrs:
        flat = a.reshape(-1)
        pad = (-flat.shape[0]) % LANES
        parts.append(jnp.pad(flat, (0, pad)).reshape(-1, LANES))
    packed = jnp.concatenate(parts, axis=0)
    return jnp.pad(packed, ((0, total_rows - packed.shape[0]), (0, 0)))


def _unpack_rows(packed, shapes):
    out, r = [], 0
    for shp in shapes:
        size = math.prod(shp)
        rows = -(-size // LANES)
        out.append(packed[r:r + rows].reshape(-1)[:size].reshape(shp))
        r += rows
    return out


def _rows_of(shapes):
    return sum(-(-math.prod(s) // LANES) for s in shapes)


def kernel(x, mem, a_w_in, a_sgu_ln_g, a_sgu_ln_b, a_w_s, a_b_s, kv_w, kv_b_f, b_w_q, mem_w_kv, w_o, ln_g, ln_b, w_up, w_down, loss_target, m_a_w_in, m_a_sgu_ln_g, m_a_sgu_ln_b, m_a_w_s, m_a_b_s, m_kv_w, m_kv_b_f, m_b_w_q, m_mem_w_kv, m_w_o, m_ln_g, m_ln_b, m_w_up, m_w_down, v_a_w_in, v_a_sgu_ln_g, v_a_sgu_ln_b, v_a_w_s, v_a_b_s, v_kv_w, v_kv_b_f, v_b_w_q, v_mem_w_kv, v_w_o, v_ln_g, v_ln_b, v_w_up, v_w_down):
    xi, yi, ci = lax.axis_index("x"), lax.axis_index("y"), lax.axis_index("c")
    chip = 2 * xi + yi
    sel = jnp.stack([chip, ci]).astype(jnp.int32)
    S, D = x.shape[1], x.shape[2]
    DQ = mem_w_kv.shape[2] // 2
    DM = D - DQ
    NG = DM // HEAD_DIM
    KVW = kv_w.shape[1] * N_CHIPS

    shards = [a_w_in[0], kv_w, b_w_q[0], mem_w_kv[0], mem_w_kv[1], w_o[0], w_o[1], w_up[0], w_up[1], w_down[0],
              w_down[1]]
    placed = [_cast_place(s.reshape(2, s.shape[0] // 2, s.shape[1]), sel, f"cast_place_{i}")
              for i, s in enumerate(shards)]
    ln_pack = jnp.concatenate([ln_g.reshape(4, -1), ln_b.reshape(4, -1)], axis=0)
    ln_buf = lax.dynamic_update_slice(jnp.zeros((N_CHIPS,) + ln_pack.shape, F32), ln_pack[None], (chip, 0, 0))
    full, ln_all = _gather_weights(placed, ln_buf, "gather_weights")
    full = [a.reshape(N_CHIPS, 2 * a.shape[2], a.shape[3]) for a in full]
    wa_g, wkv_g, wq_g, mkv0_g, mkv1_g, wo0_g, wo1_g, wup0, wup1, wdown0, wdown1 = full
    cols = lambda a: a.transpose(1, 0, 2).reshape(a.shape[1], -1)
    rows = lambda a: a.reshape(-1, a.shape[2])
    wkv_full = cols(wkv_g)
    wkv_pad = jnp.concatenate([wkv_full[:, :2 * DM], jnp.pad(wkv_full[:, 2 * DM:], ((0, 0), (0, LANES - NG)))],
                              axis=1)
    ln_full = ln_all.transpose(1, 0, 2).reshape(8, D)
    weights = {
        "wa": cols(wa_g), "wkv": wkv_pad, "wq": rows(wq_g), "mkv": [rows(mkv0_g), rows(mkv1_g)],
        "wo": [rows(wo0_g), rows(wo1_g)], "wup": [wup0, wup1], "wdown": [wdown0, wdown1],
        "sg": a_sgu_ln_g, "sb": a_sgu_ln_b, "ws": a_w_s[0], "b_s": a_b_s[0], "bf": kv_b_f,
        "ln_g": ln_full[:4], "ln_b": ln_full[4:],
    }

    loss_part, grad_x, gr = _local_step(x[0], mem[0], loss_target[0], weights)
    loss = lax.psum(loss_part[0, 0], ("x", "y", "c"))

    by_cols = lambda a, n: a.reshape(a.shape[0], N_CHIPS, n).transpose(1, 0, 2)
    by_rows = lambda a: a.reshape(N_CHIPS, a.shape[0] // N_CHIPS, a.shape[1])
    small_shapes = [a_w_s.shape, a_b_s.shape, a_sgu_ln_g.shape, a_sgu_ln_b.shape, kv_b_f.shape]
    ln_shape = (4, D)
    n_small = _rows_of(small_shapes + [ln_shape, ln_shape])
    n_small = -(-n_small // 64) * 64
    small = _pack_rows([gr["ws"], gr["b_s"], gr["sg"], gr["sb"], gr["bf"], gr["ln_g"], gr["ln_b"]], n_small)
    big = [by_cols(gr["wa"], a_w_in.shape[2]), by_cols(gr["wkv"][:, :KVW], kv_w.shape[1]), by_rows(gr["wq"]),
           by_rows(gr["mkv"][0]), by_rows(gr["mkv"][1]), by_rows(gr["wo"][0]), by_rows(gr["wo"][1]),
           gr["wup"][0], gr["wup"][1], by_rows(gr["wdown"][0]), by_rows(gr["wdown"][1])]
    parts = big + [small.reshape(N_CHIPS, n_small // N_CHIPS, LANES)]
    g4 = [a.reshape(N_CHIPS, 2, a.shape[1] // 2, a.shape[2]) for a in parts]
    recv1 = _pair_swap(g4, "reduce_pair_swap")
    wire = [BF16] * len(big) + [F32]
    summed = [_pair_add(g, r1, sel, dt, f"reduce_pair_add_{i}") for i, (g, r1, dt) in enumerate(zip(g4, recv1, wire))]
    recv2 = _chip_exchange(summed, "reduce_chip_exchange")
    groups = [[0], [1], [2], [3, 4], [5, 6], [7, 8], [9, 10], [11]]
    mine = []
    for grp in groups:
        buf = None
        for layer, i in enumerate(grp):
            buf = _chip_reduce(g4[i], recv1[i], recv2[i], sel, buf, layer, len(grp), f"reduce_chip_sum_{i}")
        mine.append(buf)
    red = _pair_gather(mine, "reduce_pair_gather")
    red = [a.reshape(a.shape[0], 2 * a.shape[3], a.shape[4]) for a in red]
    g_a_w_in, g_kv_w, g_b_w_q, g_mem_w_kv, g_w_o, g_w_up, g_w_down, small_mine = red
    g_kv_w = g_kv_w[0]
    small_all = _chip_allgather([small_mine[0]], [False], "gather_small_grads")[0].reshape(n_small, LANES)
    g_ws, g_bs, g_sg, g_sb, g_bf, g_lng, g_lnb = _unpack_rows(small_all, small_shapes + [ln_shape, ln_shape])
    dsh = D // N_CHIPS
    g_ln_g = lax.dynamic_slice_in_dim(g_lng, chip * dsh, dsh, axis=1).reshape(ln_g.shape)
    g_ln_b = lax.dynamic_slice_in_dim(g_lnb, chip * dsh, dsh, axis=1).reshape(ln_b.shape)

    def update(name, g, w, m, v):
        flat = lambda a: a.reshape(-1, a.shape[-1])
        d, mn, vn = _adamw(flat(g), flat(w), flat(m), flat(v), "adamw_" + name)
        return d.reshape(w.shape), mn.reshape(w.shape), vn.reshape(w.shape)

    upd = {
        "a_w_in": update("a_w_in", g_a_w_in, a_w_in, m_a_w_in, v_a_w_in),
        "kv_w": update("kv_w", g_kv_w, kv_w, m_kv_w, v_kv_w),
        "b_w_q": update("b_w_q", g_b_w_q, b_w_q, m_b_w_q, v_b_w_q),
        "mem_w_kv": update("mem_w_kv", g_mem_w_kv, mem_w_kv, m_mem_w_kv, v_mem_w_kv),
        "w_o": update("w_o", g_w_o, w_o, m_w_o, v_w_o),
        "w_up": update("w_up", g_w_up, w_up, m_w_up, v_w_up),
        "w_down": update("w_down", g_w_down, w_down, m_w_down, v_w_down),
    }
    tiny_g = [g_sg, g_sb, g_ws, g_bs, g_bf, g_ln_g, g_ln_b]
    tiny_w = [a_sgu_ln_g, a_sgu_ln_b, a_w_s, a_b_s, kv_b_f, ln_g, ln_b]
    tiny_m = [m_a_sgu_ln_g, m_a_sgu_ln_b, m_a_w_s, m_a_b_s, m_kv_b_f, m_ln_g, m_ln_b]
    tiny_v = [v_a_sgu_ln_g, v_a_sgu_ln_b, v_a_w_s, v_a_b_s, v_kv_b_f, v_ln_g, v_ln_b]
    tiny_shapes = [a.shape for a in tiny_w]
    n_tiny = -(-_rows_of(tiny_shapes) // 8) * 8
    td, tm_, tv = _adamw(_pack_rows(tiny_g, n_tiny), _pack_rows(tiny_w, n_tiny), _pack_rows(tiny_m, n_tiny),
                         _pack_rows(tiny_v, n_tiny), "adamw_small")
    for name, d, mn, vn in zip(["a_sgu_ln_g", "a_sgu_ln_b", "a_w_s", "a_b_s", "kv_b_f", "ln_g", "ln_b"],
                               _unpack_rows(td, tiny_shapes), _unpack_rows(tm_, tiny_shapes),
                               _unpack_rows(tv, tiny_shapes)):
        upd[name] = (d, mn, vn)

    order = ["a_w_in", "a_sgu_ln_g", "a_sgu_ln_b", "a_w_s", "a_b_s", "kv_w", "kv_b_f", "b_w_q", "mem_w_kv", "w_o",
             "ln_g", "ln_b", "w_up", "w_down"]
    grads = {"a_w_in": g_a_w_in.reshape(a_w_in.shape), "a_sgu_ln_g": g_sg.reshape(a_sgu_ln_g.shape),
             "a_sgu_ln_b": g_sb.reshape(a_sgu_ln_b.shape), "a_w_s": g_ws.reshape(a_w_s.shape),
             "a_b_s": g_bs.reshape(a_b_s.shape), "kv_w": g_kv_w, "kv_b_f": g_bf, "b_w_q": g_b_w_q.reshape(b_w_q.shape),
             "mem_w_kv": g_mem_w_kv, "w_o": g_w_o, "ln_g": g_ln_g, "ln_b": g_ln_b, "w_up": g_w_up, "w_down": g_w_down}
    return (loss, grad_x[None], *[grads[n] for n in order], *[upd[n][0] for n in order],
            *[upd[n][1] for n in order], *[upd[n][2] for n in order])
```

```python
import functools
import math

import jax
import jax.numpy as jnp
from jax import lax
from jax.experimental import pallas as pl
from jax.experimental.pallas import tpu as pltpu

F32 = jnp.float32
BF16 = jnp.bfloat16

HEAD_DIM = 64
PAIR = 2 * HEAD_DIM
CHUNK = 128
LN_EPS = 1e-5
ALPHA = 4 ** 0.25
QK_SCALE = 1.0 / math.sqrt(HEAD_DIM)
NEG = -1e30
N_CHIPS = 4
FOX_BLOCK = 256
LANES = 128
VMEM_LIMIT = 48 * 2 ** 20

ADAM_LR, ADAM_B1, ADAM_B2, ADAM_EPS, ADAM_WD, ADAM_STEP = 0.001, 0.9, 0.999, 1e-08, 0.01, 10

MESH = pl.DeviceIdType.MESH
ANY = pl.BlockSpec(memory_space=pl.ANY)


def _dot(a, b):
    return jnp.dot(a, b, preferred_element_type=F32)


def _dot_nt(a, b):
    return lax.dot_general(a, b, (((1,), (1,)), ((), ())), preferred_element_type=F32)


def _dot_tn(a, b):
    return lax.dot_general(a, b, (((0,), (0,)), ((), ())), preferred_element_type=F32)


def _ln_stats(r):
    mu = jnp.mean(r, axis=-1, keepdims=True)
    d = r - mu
    var = jnp.mean(d * d, axis=-1, keepdims=True)
    rstd = lax.rsqrt(var + LN_EPS)
    return d * rstd, rstd


def _ln_bwd(dy, xhat, rstd, g):
    dxh = dy * g
    m1 = jnp.mean(dxh, axis=-1, keepdims=True)
    m2 = jnp.mean(dxh * xhat, axis=-1, keepdims=True)
    return rstd * (dxh - m1 - xhat * m2)


_GELU_K = math.sqrt(2.0 / math.pi)


def _gelu(x):
    return 0.5 * x * (1.0 + jnp.tanh(_GELU_K * (x + 0.044715 * x * x * x)))


def _gelu_grad(x):
    t = jnp.tanh(_GELU_K * (x + 0.044715 * x * x * x))
    return 0.5 * (1.0 + t) + 0.5 * x * (1.0 - t * t) * _GELU_K * (1.0 + 3 * 0.044715 * x * x)


def _lane_mask(e):
    lane = lax.broadcasted_iota(jnp.int32, (1, PAIR), 1)
    return (lane >= HEAD_DIM * e) & (lane < HEAD_DIM * (e + 1))


def _first_head():
    return lax.broadcasted_iota(jnp.int32, (1, PAIR), 1) < HEAD_DIM


def _mem_probs(qh, mkp):
    sc = _dot_nt(qh, mkp) * QK_SCALE
    ex = jnp.exp(sc - jnp.max(sc, axis=-1, keepdims=True))
    return ex / jnp.sum(ex, axis=-1, keepdims=True)


def _mem_attn_fwd(qm_b, mkv_ref, dq_dim):
    outs = []
    for mp in range(dq_dim // PAIR):
        qp = qm_b[:, mp * PAIR:(mp + 1) * PAIR]
        mkp = mkv_ref[:, mp * PAIR:(mp + 1) * PAIR]
        mvp = mkv_ref[:, dq_dim + mp * PAIR:dq_dim + (mp + 1) * PAIR]
        heads = []
        for e in (0, 1):
            qh = jnp.where(_lane_mask(e), qp, jnp.zeros_like(qp))
            p = _mem_probs(qh, mkp)
            heads.append(_dot(p.astype(BF16), mvp))
        outs.append(jnp.where(_first_head(), heads[0], heads[1]))
    return outs


def _mem_attn_bwd(qm_b, mkv_ref, dmo_pairs, dmkv_ref, dq_dim):
    dqs = []
    for mp in range(dq_dim // PAIR):
        ks = slice(mp * PAIR, (mp + 1) * PAIR)
        vs = slice(dq_dim + mp * PAIR, dq_dim + (mp + 1) * PAIR)
        qp = qm_b[:, ks]
        mkp = mkv_ref[:, ks]
        mvp = mkv_ref[:, vs]
        dmo_b = dmo_pairs[mp].astype(BF16)
        dq = None
        dmk = None
        dmv = None
        for e in (0, 1):
            hm = _lane_mask(e)
            qh = jnp.where(hm, qp, jnp.zeros_like(qp))
            p = _mem_probs(qh, mkp)
            doh = jnp.where(hm, dmo_b, jnp.zeros_like(dmo_b))
            dp = _dot_nt(doh, mvp)
            ds = p * (dp - jnp.sum(dp * p, axis=-1, keepdims=True))
            dsb = (ds * QK_SCALE).astype(BF16)
            kh = jnp.where(hm, mkp, jnp.zeros_like(mkp))
            dq_e = _dot(dsb, kh)
            dmk_e = _dot_tn(dsb, qh)
            dmv_e = _dot_tn(p.astype(BF16), doh)
            dq = dq_e if dq is None else dq + dq_e
            dmk = dmk_e if dmk is None else dmk + dmk_e
            dmv = dmv_e if dmv is None else dmv + dmv_e
        dmkv_ref[:, ks] += dmk
        dmkv_ref[:, vs] += dmv
        dqs.append(dq)
    return dqs


def _params(n_axes=1):
    return pltpu.CompilerParams(dimension_semantics=("arbitrary",) * n_axes, vmem_limit_bytes=VMEM_LIMIT)


def _full(shape):
    return pl.BlockSpec(shape, lambda *_: (0,) * len(shape))


def _rows(tm, cols):
    return pl.BlockSpec((tm, cols), lambda i: (i, 0))


def _pick(n, pref):
    if n <= pref:
        return n
    best = LANES
    for t in range(LANES, pref + 1, LANES):
        if n % t == 0:
            best = t
    return best if 2 * best >= pref or n > 2 * pref else n


class _Comm:
    def __init__(self, ins, out_shapes, aliases, sem_counts, start, finish):
        self.ins, self.out_shapes, self.aliases, self.sem_counts = list(ins), list(out_shapes), dict(aliases), sem_counts
        self.start, self.finish = start, finish


def _pcall(body, *, name, grid, in_specs, out_specs, out_shape, args, scratch_shapes=(), comm=None):
    n_in, n_out, n_scr = len(in_specs), len(out_shape), len(scratch_shapes)
    if comm is None:
        res = pl.pallas_call(body, name=name, grid=grid, out_shape=tuple(out_shape), in_specs=list(in_specs),
                             out_specs=tuple(out_specs), scratch_shapes=list(scratch_shapes),
                             compiler_params=_params())(*args)
        return tuple(res), ()
    nci, nco = len(comm.ins), len(comm.out_shapes)
    last = grid[0] - 1

    def wrapped(*refs):
        ins, refs = refs[:n_in], refs[n_in:]
        cins, refs = refs[:nci], refs[nci:]
        outs, refs = refs[:n_out], refs[n_out:]
        couts, refs = refs[:nco], refs[nco:]
        scr, sems = refs[:n_scr], refs[n_scr:]

        @pl.when(pl.program_id(0) == 0)
        def _():
            comm.start(cins, couts, sems)

        body(*ins, *outs, *scr)

        @pl.when(pl.program_id(0) == last)
        def _():
            comm.finish(cins, couts, sems)

    res = pl.pallas_call(
        wrapped, name=name, grid=grid, out_shape=tuple(out_shape) + tuple(comm.out_shapes),
        in_specs=list(in_specs) + [ANY] * nci, out_specs=tuple(out_specs) + (ANY,) * nco,
        input_output_aliases={n_in + a: n_out + b for a, b in comm.aliases.items()},
        scratch_shapes=list(scratch_shapes) + [pltpu.SemaphoreType.DMA((k,)) for k in comm.sem_counts],
        compiler_params=pltpu.CompilerParams(dimension_semantics=("arbitrary",), vmem_limit_bytes=VMEM_LIMIT,
                                             has_side_effects=True),
    )(*args, *comm.ins)
    return tuple(res[:n_out]), tuple(res[n_out:])


def _run_comm(comm, name):
    nci, nco = len(comm.ins), len(comm.out_shapes)

    def body(*refs):
        cins, couts, sems = refs[:nci], refs[nci:nci + nco], refs[nci + nco:]
        comm.start(cins, couts, sems)
        comm.finish(cins, couts, sems)

    return pl.pallas_call(
        body, name=name, out_shape=tuple(comm.out_shapes), in_specs=[ANY] * nci, out_specs=(ANY,) * nco,
        input_output_aliases=comm.aliases, scratch_shapes=[pltpu.SemaphoreType.DMA((k,)) for k in comm.sem_counts],
        compiler_params=pltpu.CompilerParams(has_side_effects=True),
    )(*comm.ins)


def _mm_tn(a, b, name, out_split=1):
    R, M = a.shape
    _, N = b.shape
    bm, br = _pick(M, 512), _pick(R, 512)
    ncol = N // out_split
    bn = _pick(ncol, 1024)
    per = ncol // bn

    def body(a_ref, b_ref, o_ref):
        @pl.when(pl.program_id(2) == 0)
        def _():
            o_ref[...] = jnp.zeros_like(o_ref)

        o_ref[...] += _dot_tn(a_ref[...], b_ref[...])

    if out_split == 1:
        out_shape = jax.ShapeDtypeStruct((M, N), F32)
        out_spec = pl.BlockSpec((bm, bn), lambda i, j, r: (i, j))
    else:
        out_shape = jax.ShapeDtypeStruct((out_split, M, ncol), F32)
        out_spec = pl.BlockSpec((None, bm, bn), lambda i, j, r: (j // per, i, j % per))
    return pl.pallas_call(
        body, name=name, out_shape=out_shape, grid=(M // bm, N // bn, R // br),
        in_specs=[pl.BlockSpec((br, bm), lambda i, j, r: (r, i)), pl.BlockSpec((br, bn), lambda i, j, r: (r, j))],
        out_specs=out_spec,
        compiler_params=pltpu.CompilerParams(dimension_semantics=("parallel", "parallel", "arbitrary"),
                                             vmem_limit_bytes=VMEM_LIMIT),
    )(a, b)


def _mm_small(a, b, name):
    def body(a_ref, b_ref, o_ref):
        o_ref[...] = _dot(a_ref[...], b_ref[...]).astype(BF16)

    return pl.pallas_call(body, name=name, out_shape=jax.ShapeDtypeStruct((a.shape[0], b.shape[1]), BF16),
                          compiler_params=pltpu.CompilerParams(vmem_limit_bytes=VMEM_LIMIT))(a, b)


def _mlp_fwd(x, wup4, wdown4, g, b, name, comm=None):
    S, D = x.shape
    NC, _, FC = wup4.shape
    tm = min(256, S)

    def body(x_ref, wup_ref, wdown_ref, g_ref, b_ref, xo_ref, xb_ref, r_ref, h_ref):
        xv = x_ref[...]
        xb = xv.astype(BF16)
        y = jnp.zeros((tm, D), F32)
        for c in range(NC):
            a = jnp.maximum(_dot(xb, wup_ref[c]), 0.0)
            hb = (a * a).astype(BF16)
            h_ref[:, c * FC:(c + 1) * FC] = hb
            y = y + _dot(hb, wdown_ref[c])
        r = ALPHA * xv + y
        xhat, _ = _ln_stats(r)
        xo = xhat * g_ref[...] + b_ref[...]
        xo_ref[...] = xo
        xb_ref[...] = xo.astype(BF16)
        r_ref[...] = r

    return _pcall(
        body, name=name, grid=(S // tm,), comm=comm,
        out_shape=(jax.ShapeDtypeStruct((S, D), F32), jax.ShapeDtypeStruct((S, D), BF16),
                   jax.ShapeDtypeStruct((S, D), F32), jax.ShapeDtypeStruct((S, NC * FC), BF16)),
        in_specs=[_rows(tm, D), _full(wup4.shape), _full(wdown4.shape), _full((1, D)), _full((1, D))],
        out_specs=(_rows(tm, D), _rows(tm, D), _rows(tm, D), _rows(tm, NC * FC)),
        args=(x, wup4, wdown4, g, b))


def _mlp_bwd(dxo, r, h, wup4, wdown4, g, name, comm=None):
    S, D = r.shape
    NC, _, FC = wup4.shape
    tm = min(256, S)

    def body(dxo_ref, r_ref, h_ref, wup_ref, wdown_ref, g_ref, dx_ref, drb_ref, dpre_ref, dgb_ref):
        @pl.when(pl.program_id(0) == 0)
        def _():
            dgb_ref[...] = jnp.zeros_like(dgb_ref)

        xhat, rstd = _ln_stats(r_ref[...])
        dy = dxo_ref[...]
        dgb_ref[0:1, :] += jnp.sum(dy * xhat, axis=0, keepdims=True)
        dgb_ref[1:2, :] += jnp.sum(dy, axis=0, keepdims=True)
        dr = _ln_bwd(dy, xhat, rstd, g_ref[...])
        drb = dr.astype(BF16)
        drb_ref[...] = drb
        dx = ALPHA * dr
        for c in range(NC):
            dh = _dot_nt(drb, wdown_ref[c])
            a = jnp.sqrt(h_ref[:, c * FC:(c + 1) * FC].astype(F32))
            dpre = (2.0 * a * dh).astype(BF16)
            dpre_ref[:, c * FC:(c + 1) * FC] = dpre
            dx = dx + _dot_nt(dpre, wup_ref[c])
        dx_ref[...] = dx

    return _pcall(
        body, name=name, grid=(S // tm,), comm=comm,
        out_shape=(jax.ShapeDtypeStruct((S, D), F32), jax.ShapeDtypeStruct((S, D), BF16),
                   jax.ShapeDtypeStruct((S, NC * FC), BF16), jax.ShapeDtypeStruct((2, D), F32)),
        in_specs=[_rows(tm, D), _rows(tm, D), _rows(tm, NC * FC), _full(wup4.shape), _full(wdown4.shape),
                  _full((1, D))],
        out_specs=(_rows(tm, D), _rows(tm, D), _rows(tm, NC * FC), _full((2, D))),
        args=(dxo, r, h, wup4, wdown4, g))


def _loss_grad(xo, target, name):
    S, D = xo.shape
    tm = min(512, S)

    def body(x_ref, t_ref, d_ref, l_ref):
        @pl.when(pl.program_id(0) == 0)
        def _():
            l_ref[...] = jnp.zeros_like(l_ref)

        err = x_ref[...] - t_ref[...]
        d_ref[...] = err * (1.0 / D)
        l_ref[...] += (0.5 / D) * jnp.sum(err * err)

    return pl.pallas_call(
        body, name=name, grid=(S // tm,),
        out_shape=(jax.ShapeDtypeStruct((S, D), F32), jax.ShapeDtypeStruct((8, LANES), F32)),
        in_specs=[_rows(tm, D), _rows(tm, D)], out_specs=(_rows(tm, D), _full((8, LANES))),
        compiler_params=_params(),
    )(xo, target)


def _tail_fwd_store(xv, cat_ref, wo_ref, g_ref, b_ref, xo_ref, xb_ref, r_ref):
    y = _dot(cat_ref[...], wo_ref[...])
    r = ALPHA * xv + y
    xhat, _ = _ln_stats(r)
    xo = xhat * g_ref[...] + b_ref[...]
    xo_ref[...] = xo
    xb_ref[...] = xo.astype(BF16)
    r_ref[...] = r


def _tail_bwd_head(dxo_ref, r_ref, g_ref, wo_ref, dgb_ref, drb_ref):
    xhat, rstd = _ln_stats(r_ref[...])
    dy = dxo_ref[...]
    dgb_ref[0:1, :] += jnp.sum(dy * xhat, axis=0, keepdims=True)
    dgb_ref[1:2, :] += jnp.sum(dy, axis=0, keepdims=True)
    dr = _ln_bwd(dy, xhat, rstd, g_ref[...])
    drb = dr.astype(BF16)
    drb_ref[...] = drb
    return dr, _dot_nt(drb, wo_ref[...])


def _causal(shape):
    return lax.broadcasted_iota(jnp.int32, shape, 1) <= lax.broadcasted_iota(jnp.int32, shape, 0)


def _sgu_mixed(ws_ref, bst_ref, gp, vpair):
    tril = _causal((CHUNK, CHUNK))
    w0 = jnp.where(tril, ws_ref[2 * gp], 0.0).astype(BF16)
    w1 = jnp.where(tril, ws_ref[2 * gp + 1], 0.0).astype(BF16)
    m0 = _dot(w0, vpair) + bst_ref[:, 2 * gp:2 * gp + 1]
    m1 = _dot(w1, vpair) + bst_ref[:, 2 * gp + 1:2 * gp + 2]
    return jnp.where(_first_head(), m0, m1), w0, w1


def _mixer_a_fwd(x, wa, sg, sb, ws, bst, mkv, wo, g, b, name, comm=None):
    S, D = x.shape
    DQ = mkv.shape[1] // 2
    DM = D - DQ
    NP = DM // PAIR
    tm = min(512, S)

    def body(x_ref, wa_ref, sg_ref, sb_ref, ws_ref, bst_ref, mkv_ref, wo_ref, g_ref, b_ref,
             xo_ref, xb_ref, r_ref, cat_ref, z_s, zv_s):
        xv = x_ref[...]
        z_s[...] = _dot(xv.astype(BF16), wa_ref[...])
        vhat, _ = _ln_stats(_gelu(z_s[:, DM:2 * DM]))
        zv_s[...] = (vhat * sg_ref[...] + sb_ref[...]).astype(BF16)
        for c in range(tm // CHUNK):
            rs = slice(c * CHUNK, (c + 1) * CHUNK)
            for gp in range(NP):
                ls = slice(gp * PAIR, (gp + 1) * PAIR)
                mixed, _, _ = _sgu_mixed(ws_ref, bst_ref, gp, zv_s[rs, ls])
                cat_ref[rs, ls] = (_gelu(z_s[rs, ls]) * mixed).astype(BF16)
        mo = _mem_attn_fwd(z_s[:, 2 * DM:].astype(BF16), mkv_ref, DQ)
        for mp in range(DQ // PAIR):
            cat_ref[:, DM + mp * PAIR:DM + (mp + 1) * PAIR] = mo[mp].astype(BF16)
        _tail_fwd_store(xv, cat_ref, wo_ref, g_ref, b_ref, xo_ref, xb_ref, r_ref)

    return _pcall(
        body, name=name, grid=(S // tm,), comm=comm,
        out_shape=(jax.ShapeDtypeStruct((S, D), F32), jax.ShapeDtypeStruct((S, D), BF16),
                   jax.ShapeDtypeStruct((S, D), F32), jax.ShapeDtypeStruct((S, D), BF16)),
        in_specs=[_rows(tm, D), _full(wa.shape), _full(sg.shape), _full(sb.shape), _full(ws.shape), _full(bst.shape),
                  _full(mkv.shape), _full(wo.shape), _full((1, D)), _full((1, D))],
        out_specs=(_rows(tm, D), _rows(tm, D), _rows(tm, D), _rows(tm, D)),
        scratch_shapes=[pltpu.VMEM((tm, 2 * DM + DQ), F32), pltpu.VMEM((tm, DM), BF16)],
        args=(x, wa, sg, sb, ws, bst, mkv, wo, g, b))


def _mixer_a_bwd(x, dxo, r, wa, sg, sb, ws, bst, mkv, wo, g, name, comm=None):
    S, D = x.shape
    DQ = mkv.shape[1] // 2
    DM = D - DQ
    NP = DM // PAIR
    NG = DM // HEAD_DIM
    tm = min(256, S)

    def body(x_ref, dxo_ref, r_ref, wa_ref, sg_ref, sb_ref, ws_ref, bst_ref, mkv_ref, wo_ref, g_ref,
             dx_ref, dz_ref, drb_ref, dgb_ref, dsgb_ref, dws_ref, dbst_ref, dmkv_ref, z_s, zv_s, dzv_s):
        @pl.when(pl.program_id(0) == 0)
        def _():
            dgb_ref[...] = jnp.zeros_like(dgb_ref)
            dsgb_ref[...] = jnp.zeros_like(dsgb_ref)
            dws_ref[...] = jnp.zeros_like(dws_ref)
            dbst_ref[...] = jnp.zeros_like(dbst_ref)
            dmkv_ref[...] = jnp.zeros_like(dmkv_ref)

        dr, dcat = _tail_bwd_head(dxo_ref, r_ref, g_ref, wo_ref, dgb_ref, drb_ref)
        z_s[...] = _dot(x_ref[...].astype(BF16), wa_ref[...])
        vhat, vrstd = _ln_stats(_gelu(z_s[:, DM:2 * DM]))
        zv_s[...] = (vhat * sg_ref[...] + sb_ref[...]).astype(BF16)
        tril = _causal((CHUNK, CHUNK))
        lane = lax.broadcasted_iota(jnp.int32, (1, LANES), 1)
        for c in range(tm // CHUNK):
            rs = slice(c * CHUNK, (c + 1) * CHUNK)
            for gp in range(NP):
                ls = slice(gp * PAIR, (gp + 1) * PAIR)
                vpair = zv_s[rs, ls]
                mixed, w0, w1 = _sgu_mixed(ws_ref, bst_ref, gp, vpair)
                u_pre = z_s[rs, ls]
                dmix = dcat[rs, ls]
                dz_ref[rs, ls] = (dmix * mixed * _gelu_grad(u_pre)).astype(BF16)
                dmixed = dmix * _gelu(u_pre)
                first = _first_head()
                d0 = jnp.where(first, dmixed, 0.0)
                d1 = jnp.where(first, 0.0, dmixed)
                d0b = d0.astype(BF16)
                d1b = d1.astype(BF16)
                dzv_s[rs, ls] = _dot_tn(w0, d0b) + _dot_tn(w1, d1b)
                dws_ref[2 * gp] += jnp.where(tril, _dot_nt(d0b, vpair), 0.0)
                dws_ref[2 * gp + 1] += jnp.where(tril, _dot_nt(d1b, vpair), 0.0)
                dbst_ref[...] += (jnp.where(lane == 2 * gp, jnp.sum(d0, axis=-1, keepdims=True), 0.0)
                                  + jnp.where(lane == 2 * gp + 1, jnp.sum(d1, axis=-1, keepdims=True), 0.0))
        dzv = dzv_s[...]
        dsgb_ref[0:1, :] += jnp.sum(dzv * vhat, axis=0, keepdims=True)
        dsgb_ref[1:2, :] += jnp.sum(dzv, axis=0, keepdims=True)
        dgv = _ln_bwd(dzv, vhat, vrstd, sg_ref[...])
        dz_ref[:, DM:2 * DM] = (dgv * _gelu_grad(z_s[:, DM:2 * DM])).astype(BF16)
        dmo = [dcat[:, DM + mp * PAIR:DM + (mp + 1) * PAIR] for mp in range(DQ // PAIR)]
        dqm = _mem_attn_bwd(z_s[:, 2 * DM:].astype(BF16), mkv_ref, dmo, dmkv_ref, DQ)
        for mp in range(DQ // PAIR):
            dz_ref[:, 2 * DM + mp * PAIR:2 * DM + (mp + 1) * PAIR] = dqm[mp].astype(BF16)
        dx_ref[...] = ALPHA * dr + _dot_nt(dz_ref[...], wa_ref[...])

    ZW = 2 * DM + DQ
    M = mkv.shape[0]
    return _pcall(
        body, name=name, grid=(S // tm,), comm=comm,
        out_shape=(jax.ShapeDtypeStruct((S, D), F32), jax.ShapeDtypeStruct((S, ZW), BF16),
                   jax.ShapeDtypeStruct((S, D), BF16), jax.ShapeDtypeStruct((2, D), F32),
                   jax.ShapeDtypeStruct((2, DM), F32), jax.ShapeDtypeStruct((NG, CHUNK, CHUNK), F32),
                   jax.ShapeDtypeStruct((CHUNK, LANES), F32), jax.ShapeDtypeStruct((M, 2 * DQ), F32)),
        in_specs=[_rows(tm, D), _rows(tm, D), _rows(tm, D), _full(wa.shape), _full(sg.shape), _full(sb.shape),
                  _full(ws.shape), _full(bst.shape), _full(mkv.shape), _full(wo.shape), _full((1, D))],
        out_specs=(_rows(tm, D), _rows(tm, ZW), _rows(tm, D), _full((2, D)), _full((2, DM)),
                   _full((NG, CHUNK, CHUNK)), _full((CHUNK, LANES)), _full((M, 2 * DQ))),
        scratch_shapes=[pltpu.VMEM((tm, ZW), F32), pltpu.VMEM((tm, DM), BF16), pltpu.VMEM((tm, DM), F32)],
        args=(x, dxo, r, wa, sg, sb, ws, bst, mkv, wo, g))


def _proj_b_fwd(xb, wq, wkv, dm, name):
    S, D = xb.shape
    DQ = D - dm
    tm = min(512, S)

    def body(x_ref, wq_ref, wkv_ref, q_ref, qm_ref, k_ref, v_ref, fl_ref):
        xv = x_ref[...]
        z = _dot(xv, wq_ref[...])
        q_ref[...] = z[:, :dm].astype(BF16)
        qm_ref[...] = z[:, dm:].astype(BF16)
        kvf = _dot(xv, wkv_ref[...])
        k_ref[...] = kvf[:, :dm].astype(BF16)
        v_ref[...] = kvf[:, dm:2 * dm].astype(BF16)
        fl_ref[...] = kvf[:, 2 * dm:]

    return pl.pallas_call(
        body, name=name, grid=(S // tm,),
        out_shape=(jax.ShapeDtypeStruct((S, dm), BF16), jax.ShapeDtypeStruct((S, DQ), BF16),
                   jax.ShapeDtypeStruct((S, dm), BF16), jax.ShapeDtypeStruct((S, dm), BF16),
                   jax.ShapeDtypeStruct((S, LANES), F32)),
        in_specs=[_rows(tm, D), _full(wq.shape), _full(wkv.shape)],
        out_specs=(_rows(tm, dm), _rows(tm, DQ), _rows(tm, dm), _rows(tm, dm), _rows(tm, LANES)),
        compiler_params=_params(),
    )(xb, wq, wkv)


def _proj_b_bwd(dr, dqz, dkvf, wq, wkv, name):
    S, D = dr.shape
    tm = min(512, S)

    def body(dr_ref, dqz_ref, dkvf_ref, wq_ref, wkv_ref, dx_ref):
        dx_ref[...] = (ALPHA * dr_ref[...] + _dot_nt(dqz_ref[...], wq_ref[...])
                       + _dot_nt(dkvf_ref[...], wkv_ref[...]))

    return pl.pallas_call(
        body, name=name, grid=(S // tm,), out_shape=jax.ShapeDtypeStruct((S, D), F32),
        in_specs=[_rows(tm, D), _rows(tm, dqz.shape[1]), _rows(tm, dkvf.shape[1]), _full(wq.shape), _full(wkv.shape)],
        out_specs=_rows(tm, D), compiler_params=_params(),
    )(dr, dqz, dkvf, wq, wkv)


def _split3(v):
    hi = v.astype(BF16)
    r1 = v - hi.astype(F32)
    mid = r1.astype(BF16)
    lo = (r1 - mid.astype(F32)).astype(BF16)
    return hi, mid, lo


def _tri_sum(v, tri_b):
    hi, mid, lo = _split3(v)
    return _dot(hi, tri_b) + _dot(mid, tri_b) + _dot(lo, tri_b)


def _log_sigmoid(x):
    return jnp.minimum(x, 0.0) - jnp.log(1.0 + jnp.exp(-jnp.abs(x)))


def _forget_cumsum(flt, bf, name):
    H, S = flt.shape
    nchunk = S // LANES

    def body(fl_ref, bf_ref, c_ref):
        upper = (lax.broadcasted_iota(jnp.int32, (LANES, LANES), 0)
                 <= lax.broadcasted_iota(jnp.int32, (LANES, LANES), 1)).astype(BF16)

        def step(n, carry):
            s0 = pl.multiple_of(n * LANES, LANES)
            lf = _log_sigmoid(fl_ref[:, pl.ds(s0, LANES)] + bf_ref[...])
            cs = _tri_sum(lf, upper) + carry
            c_ref[:, pl.ds(s0, LANES)] = cs
            return cs[:, LANES - 1:LANES]

        lax.fori_loop(0, nchunk, step, jnp.zeros((H, 1), F32))

    return pl.pallas_call(body, name=name, out_shape=jax.ShapeDtypeStruct((H, S), F32),
                          compiler_params=pltpu.CompilerParams(vmem_limit_bytes=VMEM_LIMIT))(flt, bf)


def _forget_cumsum_bwd(dct, flt, bf, name):
    H, S = flt.shape
    nchunk = S // LANES

    def body(dc_ref, fl_ref, bf_ref, dfl_ref, dbf_ref):
        lower = (lax.broadcasted_iota(jnp.int32, (LANES, LANES), 0)
                 >= lax.broadcasted_iota(jnp.int32, (LANES, LANES), 1)).astype(BF16)

        def step(n, carry):
            tail, tot = carry
            s0 = pl.multiple_of((nchunk - 1 - n) * LANES, LANES)
            suffix = _tri_sum(dc_ref[:, pl.ds(s0, LANES)], lower) + tail
            xv = fl_ref[:, pl.ds(s0, LANES)] + bf_ref[...]
            dfl = suffix * (1.0 / (1.0 + jnp.exp(xv)))
            dfl_ref[:, pl.ds(s0, LANES)] = dfl
            return suffix[:, 0:1], tot + jnp.sum(dfl, axis=-1, keepdims=True)

        _, tot = lax.fori_loop(0, nchunk, step, (jnp.zeros((H, 1), F32), jnp.zeros((H, 1), F32)))
        dbf_ref[...] = jnp.broadcast_to(tot, (H, LANES))

    return pl.pallas_call(body, name=name,
                          out_shape=(jax.ShapeDtypeStruct((H, S), F32), jax.ShapeDtypeStruct((H, LANES), F32)),
                          compiler_params=pltpu.CompilerParams(vmem_limit_bytes=VMEM_LIMIT))(dct, flt, bf)


def _fox_fwd(q, k, v, ct3, name, comm=None):
    S, DM = q.shape
    NP = DM // PAIR
    T = min(FOX_BLOCK, S)
    nq = S // T

    def body(q_ref, k_ref, v_ref, c_ref, o_ref, lse_ref):
        tril = _causal((T, T))
        heads = (_lane_mask(0), _lane_mask(1))

        def step(carry, qh, cref, kb, vb, crow, masked):
            m, l, acc = carry
            s = _dot_nt(qh, kb) + (cref - crow)
            if masked:
                s = jnp.where(tril, s, NEG)
            m_new = jnp.maximum(m, jnp.max(s, axis=-1, keepdims=True))
            a = jnp.exp(m - m_new)
            p = jnp.exp(s - m_new)
            l = a * l + jnp.sum(p, axis=-1, keepdims=True)
            acc = a * acc + _dot(p.astype(BF16), vb)
            return m_new, l, acc

        def kv_block(kj):
            s0 = pl.multiple_of(kj * T, T)
            return k_ref[pl.ds(s0, T), :], v_ref[pl.ds(s0, T), :], [c_ref[e:e + 1, pl.ds(s0, T)] for e in (0, 1)]

        def q_pair(a, _):
            t0s = [pl.multiple_of((2 * a + r) * T, T) for r in (0, 1)]
            qh, cref = {}, {}
            for r in (0, 1):
                qb = q_ref[pl.ds(t0s[r], T), :] * QK_SCALE
                for e in (0, 1):
                    qh[r, e] = jnp.where(heads[e], qb, jnp.zeros_like(qb))
                    cref[r, e] = c_ref[e:e + 1, pl.ds(t0s[r], LANES)][:, 0:1]
            chains = [(r, e) for r in (0, 1) for e in (0, 1)]

            def full(kj, carries):
                kb, vb, crow = kv_block(kj)
                return tuple(step(carries[i], qh[r, e], cref[r, e], kb, vb, crow[e], False)
                             for i, (r, e) in enumerate(chains))

            init = (jnp.full((T, 1), NEG, F32), jnp.zeros((T, 1), F32), jnp.zeros((T, PAIR), F32))
            carries = list(lax.fori_loop(0, 2 * a, full, (init,) * 4))
            kb, vb, crow = kv_block(2 * a)
            for i, (r, e) in enumerate(chains):
                carries[i] = step(carries[i], qh[r, e], cref[r, e], kb, vb, crow[e], r == 0)
            kb, vb, crow = kv_block(2 * a + 1)
            for i, (r, e) in enumerate(chains):
                if r == 1:
                    carries[i] = step(carries[i], qh[r, e], cref[r, e], kb, vb, crow[e], True)
            first = _first_head()
            for r in (0, 1):
                (m0, l0, acc0), (m1, l1, acc1) = carries[2 * r], carries[2 * r + 1]
                o_ref[pl.ds(t0s[r], T), :] = jnp.where(first, acc0 / l0, acc1 / l1).astype(BF16)
                lse_ref[pl.ds(t0s[r], T), :] = jnp.where(first, m0 + jnp.log(l0), m1 + jnp.log(l1))
            return 0

        lax.fori_loop(0, nq // 2, q_pair, 0)

    col = pl.BlockSpec((S, PAIR), lambda hp: (0, hp))
    return _pcall(
        body, name=name, grid=(NP,), comm=comm,
        out_shape=(jax.ShapeDtypeStruct((S, DM), BF16), jax.ShapeDtypeStruct((NP, S, LANES), F32)),
        in_specs=[col, col, col, pl.BlockSpec((None, 2, S), lambda hp: (hp, 0, 0))],
        out_specs=(col, pl.BlockSpec((None, S, LANES), lambda hp: (hp, 0, 0))),
        args=(q, k, v, ct3))


def _fox_bwd(q, k, v, o, do, ct3, lse, name, comm=None):
    S, DM = q.shape
    NP = DM // PAIR
    T = min(FOX_BLOCK, S)
    nq = S // T

    def body(q_ref, k_ref, v_ref, o_ref, do_ref, c_ref, lse_ref, dq_ref, dk_ref, dv_ref, dc_ref,
             dq_acc, drow_acc, dk_acc, dv_acc):
        tril = _causal((T, T))
        lane = lax.broadcasted_iota(jnp.int32, (1, LANES), 1)
        heads = (_lane_mask(0), _lane_mask(1))
        dq_acc[...] = jnp.zeros_like(dq_acc)
        drow_acc[...] = jnp.zeros_like(drow_acc)
        dc_ref[...] = jnp.zeros_like(dc_ref)

        def kv_pair(b, _):
            dk_acc[...] = jnp.zeros_like(dk_acc)
            dv_acc[...] = jnp.zeros_like(dv_acc)
            s0s = [pl.multiple_of((2 * b + jj) * T, T) for jj in (0, 1)]
            kh, ks, vbs, crow = {}, {}, [], {}
            for jj in (0, 1):
                kb = k_ref[pl.ds(s0s[jj], T), :]
                vbs.append(v_ref[pl.ds(s0s[jj], T), :])
                for e in (0, 1):
                    kh[jj, e] = jnp.where(heads[e], kb, jnp.zeros_like(kb))
                    ks[jj, e] = kh[jj, e] * QK_SCALE
                    crow[jj, e] = c_ref[e:e + 1, pl.ds(s0s[jj], T)]

            def q_step(qi, blocks):
                t0 = pl.multiple_of(qi * T, T)
                qs = q_ref[pl.ds(t0, T), :] * QK_SCALE
                dob = do_ref[pl.ds(t0, T), :]
                prod = dob.astype(F32) * o_ref[pl.ds(t0, T), :].astype(F32)
                lse_t = lse_ref[pl.ds(t0, T), :]
                dq = jnp.zeros((T, PAIR), F32)
                drow = jnp.zeros((T, LANES), F32)
                for e in (0, 1):
                    doh = jnp.where(heads[e], dob, jnp.zeros_like(dob))
                    qh = jnp.where(heads[e], qs, jnp.zeros_like(qs))
                    dd = jnp.sum(jnp.where(heads[e], prod, 0.0), axis=-1, keepdims=True)
                    cref = c_ref[e:e + 1, pl.ds(t0, LANES)][:, 0:1]
                    lse_c = lse_t[:, HEAD_DIM * e:HEAD_DIM * e + 1]
                    for jj, masked in blocks:
                        s = _dot_nt(qs, kh[jj, e]) + (cref - crow[jj, e])
                        if masked:
                            s = jnp.where(tril, s, NEG)
                        p = jnp.exp(s - lse_c)
                        dv_acc[jj] += _dot_tn(p.astype(BF16), doh)
                        ds = p * (_dot_nt(doh, vbs[jj]) - dd)
                        dsb = ds.astype(BF16)
                        dq = dq + _dot(dsb, ks[jj, e])
                        dk_acc[jj] += _dot_tn(dsb, qh)
                        dc_ref[e:e + 1, pl.ds(s0s[jj], T)] -= jnp.sum(ds, axis=0, keepdims=True)
                        drow = drow + jnp.where(lane == e, jnp.sum(ds, axis=-1, keepdims=True), 0.0)
                dq_acc[pl.ds(t0, T), :] += dq
                drow_acc[pl.ds(t0, T), :] += drow

            q_step(2 * b, [(0, True)])
            q_step(2 * b + 1, [(0, False), (1, True)])

            def rest(qi, _):
                q_step(qi, [(0, False), (1, False)])
                return 0

            lax.fori_loop(2 * b + 2, nq, rest, 0)
            for jj in (0, 1):
                dk_ref[pl.ds(s0s[jj], T), :] = dk_acc[jj].astype(BF16)
                dv_ref[pl.ds(s0s[jj], T), :] = dv_acc[jj].astype(BF16)
            return 0

        lax.fori_loop(0, nq // 2, kv_pair, 0)
        dq_ref[...] = dq_acc[...].astype(BF16)

        def add_rows(qi, _):
            t0 = pl.multiple_of(qi * T, T)
            dc_ref[:, pl.ds(t0, T)] += drow_acc[pl.ds(t0, T), :].T[0:2, :]
            return 0

        lax.fori_loop(0, nq, add_rows, 0)

    col = pl.BlockSpec((S, PAIR), lambda hp: (0, hp))
    ct_spec = pl.BlockSpec((None, 2, S), lambda hp: (hp, 0, 0))
    return _pcall(
        body, name=name, grid=(NP,), comm=comm,
        out_shape=(jax.ShapeDtypeStruct((S, DM), BF16), jax.ShapeDtypeStruct((S, DM), BF16),
                   jax.ShapeDtypeStruct((S, DM), BF16), jax.ShapeDtypeStruct((NP, 2, S), F32)),
        in_specs=[col, col, col, col, col, ct_spec, pl.BlockSpec((None, S, LANES), lambda hp: (hp, 0, 0))],
        out_specs=(col, col, col, ct_spec),
        scratch_shapes=[pltpu.VMEM((S, PAIR), F32), pltpu.VMEM((S, LANES), F32), pltpu.VMEM((2, T, PAIR), F32),
                        pltpu.VMEM((2, T, PAIR), F32)],
        args=(q, k, v, o, do, ct3, lse))


def _mixer_b_fwd(x, o, qm, mkv, wo, g, b, name):
    S, D = x.shape
    DQ = qm.shape[1]
    DM = D - DQ
    tm = min(512, S)

    def body(x_ref, o_ref, qm_ref, mkv_ref, wo_ref, g_ref, b_ref, xo_ref, xb_ref, r_ref, cat_ref):
        cat_ref[:, :DM] = o_ref[...]
        mo = _mem_attn_fwd(qm_ref[...], mkv_ref, DQ)
        for mp in range(DQ // PAIR):
            cat_ref[:, DM + mp * PAIR:DM + (mp + 1) * PAIR] = mo[mp].astype(BF16)
        _tail_fwd_store(x_ref[...], cat_ref, wo_ref, g_ref, b_ref, xo_ref, xb_ref, r_ref)

    return pl.pallas_call(
        body, name=name, grid=(S // tm,),
        out_shape=(jax.ShapeDtypeStruct((S, D), F32), jax.ShapeDtypeStruct((S, D), BF16),
                   jax.ShapeDtypeStruct((S, D), F32), jax.ShapeDtypeStruct((S, D), BF16)),
        in_specs=[_rows(tm, D), _rows(tm, DM), _rows(tm, DQ), _full(mkv.shape), _full(wo.shape), _full((1, D)),
                  _full((1, D))],
        out_specs=(_rows(tm, D), _rows(tm, D), _rows(tm, D), _rows(tm, D)),
        compiler_params=_params(),
    )(x, o, qm, mkv, wo, g, b)


def _mixer_b_bwd(dxo, r, qm, mkv, wo, g, name):
    S, D = r.shape
    DQ = qm.shape[1]
    DM = D - DQ
    M = mkv.shape[0]
    tm = min(512, S)

    def body(dxo_ref, r_ref, qm_ref, mkv_ref, wo_ref, g_ref, dr_ref, drb_ref, dcat_ref, dgb_ref, dmkv_ref):
        @pl.when(pl.program_id(0) == 0)
        def _():
            dgb_ref[...] = jnp.zeros_like(dgb_ref)
            dmkv_ref[...] = jnp.zeros_like(dmkv_ref)

        dr, dcat = _tail_bwd_head(dxo_ref, r_ref, g_ref, wo_ref, dgb_ref, drb_ref)
        dr_ref[...] = dr
        dcat_ref[:, :DM] = dcat[:, :DM].astype(BF16)
        dmo = [dcat[:, DM + mp * PAIR:DM + (mp + 1) * PAIR] for mp in range(DQ // PAIR)]
        dqm = _mem_attn_bwd(qm_ref[...], mkv_ref, dmo, dmkv_ref, DQ)
        for mp in range(DQ // PAIR):
            dcat_ref[:, DM + mp * PAIR:DM + (mp + 1) * PAIR] = dqm[mp].astype(BF16)

    return pl.pallas_call(
        body, name=name, grid=(S // tm,),
        out_shape=(jax.ShapeDtypeStruct((S, D), F32), jax.ShapeDtypeStruct((S, D), BF16),
                   jax.ShapeDtypeStruct((S, D), BF16), jax.ShapeDtypeStruct((2, D), F32),
                   jax.ShapeDtypeStruct((M, 2 * DQ), F32)),
        in_specs=[_rows(tm, D), _rows(tm, D), _rows(tm, DQ), _full(mkv.shape), _full(wo.shape), _full((1, D))],
        out_specs=(_rows(tm, D), _rows(tm, D), _rows(tm, D), _full((2, D)), _full((M, 2 * DQ))),
        compiler_params=_params(),
    )(dxo, r, qm, mkv, wo, g)


class _NoExchange:
    def plan(self, stage, w, grads):
        return None

    def done(self, stage, results, w):
        pass


def _local_step(x, mem, target, w, hooks):
    S, D = x.shape
    DQ = w["mkv0"].shape[1] // 2
    DM = D - DQ
    NG = DM // HEAD_DIM
    NP = DM // PAIR
    ln_g, ln_b = w["ln_g"], w["ln_b"]
    row = lambda a, i: a[i:i + 1]
    memb = mem.astype(BF16)
    x0b = x.astype(BF16)
    bst = jnp.pad(w["b_s"].T, ((0, 0), (0, LANES - NG)))
    grads = {}

    def staged(stage, fn, *args):
        out, got = fn(*args, stage, comm=hooks.plan(stage, w, grads))
        hooks.done(stage, got, w)
        return out

    mkv0 = _mm_small(memb, w["mkv0"], "mem_kv_0")
    x1, x1b, r1, cat0 = staged("mixer_a_fwd", _mixer_a_fwd, x, w["wa"], w["sg"], w["sb"], w["ws"], bst, mkv0,
                               w["wo0"], row(ln_g, 0), row(ln_b, 0))
    x2, x2b, r2, h0 = staged("mlp_fwd_0", _mlp_fwd, x1, w["wup0"], w["wdown0"], row(ln_g, 1), row(ln_b, 1))
    q, qm, k, v, fl = _proj_b_fwd(x2b, w["wq"], w["wkv"], DM, "proj_b_fwd")
    flt = fl[:, :16].T
    bfc = jnp.pad(w["bf"], (0, 16 - NG)).reshape(16, 1)
    ct = _forget_cumsum(flt, bfc, "forget_cumsum")
    ct3 = ct[:NG].reshape(NP, 2, S)
    o, lse = staged("fox_fwd", _fox_fwd, q, k, v, ct3)
    mkv1 = _mm_small(memb, w["mkv1"], "mem_kv_1")
    x3, x3b, r3, cat1 = _mixer_b_fwd(x2, o, qm, mkv1, w["wo1"], row(ln_g, 2), row(ln_b, 2), "mixer_b_fwd")
    (x4, _, r4, h1), _ = _mlp_fwd(x3, w["wup1"], w["wdown1"], row(ln_g, 3), row(ln_b, 3), "mlp_fwd_1")

    nc = w["wup0"].shape[0]
    dx4, loss_part = _loss_grad(x4, target, "loss_grad")
    (dx3, dr4b, dpre1, dgb11), _ = _mlp_bwd(dx4, r4, h1, w["wup1"], w["wdown1"], row(ln_g, 3), "mlp_bwd_1")
    grads["wdown1"] = _mm_tn(h1, dr4b, "dw_down_1")
    grads["wup1"] = _mm_tn(x3b, dpre1, "dw_up_1", out_split=nc)
    dr3, dr3b, dcat1, dgb10, dmkv1 = _mixer_b_bwd(dx3, r3, qm, mkv1, w["wo1"], row(ln_g, 2), "mixer_b_bwd")
    do = dcat1[:, :DM]
    dq, dk, dv, dct3 = staged("fox_bwd", _fox_bwd, q, k, v, o, do, ct3, lse)
    dct = jnp.pad(dct3.reshape(NG, S), ((0, 16 - NG), (0, 0)))
    dflt, dbf = _forget_cumsum_bwd(dct, flt, bfc, "forget_cumsum_bwd")
    dfl = jnp.pad(dflt.T, ((0, 0), (0, LANES - 16))).astype(BF16)
    dqz = jnp.concatenate([dq, dcat1[:, DM:]], axis=1)
    dkvf = jnp.concatenate([dk, dv, dfl], axis=1)
    dx2 = _proj_b_bwd(dr3, dqz, dkvf, w["wq"], w["wkv"], "proj_b_bwd")
    grads["wo1"] = _mm_tn(cat1, dr3b, "dw_o_1")
    grads["wq"] = _mm_tn(x2b, dqz, "dw_q")
    grads["wkv"] = _mm_tn(x2b, dkvf, "dw_kv")
    grads["mkv1"] = _mm_tn(memb, dmkv1.astype(BF16), "dw_mkv_1")
    dx1, dr2b, dpre0, dgb01 = staged("mlp_bwd_0", _mlp_bwd, dx2, r2, h0, w["wup0"], w["wdown0"], row(ln_g, 1))
    grads["wdown0"] = _mm_tn(h0, dr2b, "dw_down_0")
    grads["wup0"] = _mm_tn(x1b, dpre0, "dw_up_0", out_split=nc)
    dx0, dz, dr1b, dgb00, dsgb, dws, dbst, dmkv0 = staged(
        "mixer_a_bwd", _mixer_a_bwd, x, dx1, r1, w["wa"], w["sg"], w["sb"], w["ws"], bst, mkv0, w["wo0"], row(ln_g, 0))
    grads["wo0"] = _mm_tn(cat0, dr1b, "dw_o_0")
    grads["wa"] = _mm_tn(x0b, dz, "dw_a")
    grads["mkv0"] = _mm_tn(memb, dmkv0.astype(BF16), "dw_mkv_0")
    grads.update({
        "ws": dws, "b_s": dbst[:, :NG].T, "sg": dsgb[0:1], "sb": dsgb[1:2], "bf": dbf[:NG, 0],
        "ln_g": jnp.concatenate([dgb00[0:1], dgb01[0:1], dgb10[0:1], dgb11[0:1]], axis=0),
        "ln_b": jnp.concatenate([dgb00[1:2], dgb01[1:2], dgb10[1:2], dgb11[1:2]], axis=0),
    })
    return loss_part, dx0, grads


def _place():
    xi, yi, ci = lax.axis_index("x"), lax.axis_index("y"), lax.axis_index("c")
    peers = [(xi, 1 - yi), (1 - xi, yi), (1 - xi, 1 - yi)]
    return xi, yi, ci, peers


def _comm_call(body, name, ins, out_shapes, n_remote, n_local):
    return pl.pallas_call(
        body, name=name, out_shape=out_shapes, in_specs=[ANY] * len(ins), out_specs=[ANY] * len(out_shapes),
        scratch_shapes=[pltpu.SemaphoreType.DMA((n_remote,)), pltpu.SemaphoreType.DMA((n_remote,)),
                        pltpu.SemaphoreType.DMA((max(n_local, 1),))],
        compiler_params=pltpu.CompilerParams(has_side_effects=True),
    )(*ins)


def _chip_allgather(xs, halved, name):
    n = len(xs)
    out_shapes = [jax.ShapeDtypeStruct((N_CHIPS,) + (a.shape[1:] if hv else a.shape), a.dtype)
                  for a, hv in zip(xs, halved)]

    def body(*refs):
        x_refs, o_refs = refs[:n], refs[n:2 * n]
        send_sems, recv_sems, loc_sems = refs[2 * n:]
        xi, yi, ci, peers = _place()
        j = 2 * xi + yi
        sends, locs = [], []
        for i in range(n):
            src = x_refs[i].at[ci] if halved[i] else x_refs[i]
            loc = pltpu.make_async_copy(src, o_refs[i].at[j], loc_sems.at[i])
            loc.start()
            locs.append(loc)
            for d, (px, py) in enumerate(peers):
                cp = pltpu.make_async_remote_copy(src_ref=src, dst_ref=o_refs[i].at[j], send_sem=send_sems.at[3 * i + d],
                                                  recv_sem=recv_sems.at[3 * i + d], device_id=(px, py, ci),
                                                  device_id_type=MESH)
                cp.start()
                sends.append(cp)
        for i in range(n):
            src = x_refs[i].at[ci] if halved[i] else x_refs[i]
            for d, (px, py) in enumerate(peers):
                pltpu.make_async_remote_copy(src_ref=src, dst_ref=o_refs[i].at[2 * px + py],
                                             send_sem=send_sems.at[3 * i + d], recv_sem=recv_sems.at[3 * i + d],
                                             device_id=(px, py, ci), device_id_type=MESH).wait_recv()
        for cp in sends:
            cp.wait_send()
        for loc in locs:
            loc.wait()

    return _comm_call(body, name, xs, out_shapes, 3 * n, n)


def _pair_swap(xs, name):
    n = len(xs)
    out_shapes = [jax.ShapeDtypeStruct((a.shape[0],) + a.shape[2:], a.dtype) for a in xs]

    def body(*refs):
        x_refs, o_refs = refs[:n], refs[n:2 * n]
        send_sems, recv_sems, _ = refs[2 * n:]
        xi, yi, ci, _ = _place()
        cps = []
        for i in range(n):
            cp = pltpu.make_async_remote_copy(src_ref=x_refs[i].at[:, 1 - ci], dst_ref=o_refs[i],
                                              send_sem=send_sems.at[i], recv_sem=recv_sems.at[i],
                                              device_id=(xi, yi, 1 - ci), device_id_type=MESH)
            cp.start()
            cps.append(cp)
        for cp in cps:
            cp.wait_recv()
        for cp in cps:
            cp.wait_send()

    return _comm_call(body, name, xs, out_shapes, n, 0)


def _chip_exchange(xs):
    n = len(xs)
    out_shapes = [jax.ShapeDtypeStruct((3,) + a.shape[1:], a.dtype) for a in xs]

    def copies(x_refs, o_refs, sems):
        send_sems, recv_sems = sems
        xi, yi, ci, peers = _place()
        return [pltpu.make_async_remote_copy(src_ref=x_refs[i].at[2 * px + py], dst_ref=o_refs[i].at[d],
                                             send_sem=send_sems.at[3 * i + d], recv_sem=recv_sems.at[3 * i + d],
                                             device_id=(px, py, ci), device_id_type=MESH)
                for i in range(n) for d, (px, py) in enumerate(peers)]

    def start(x_refs, o_refs, sems):
        for cp in copies(x_refs, o_refs, sems):
            cp.start()

    def finish(x_refs, o_refs, sems):
        cps = copies(x_refs, o_refs, sems)
        for cp in cps:
            cp.wait_recv()
        for cp in cps:
            cp.wait_send()

    return _Comm(xs, out_shapes, {}, [3 * n, 3 * n], start, finish)


def _inplace_call(body, name, bufs, sem_counts):
    n = len(bufs)
    return pl.pallas_call(
        body, name=name, out_shape=[jax.ShapeDtypeStruct(a.shape, a.dtype) for a in bufs],
        in_specs=[ANY] * n, out_specs=[ANY] * n, input_output_aliases={i: i for i in range(n)},
        scratch_shapes=[pltpu.SemaphoreType.DMA((k,)) for k in sem_counts],
        compiler_params=pltpu.CompilerParams(has_side_effects=True),
    )(*bufs)


def _gather_weights(bufs, whole=()):
    n, nw = len(bufs), len(whole)

    def ici(w, sems, i, d, px, py, ci, slot):
        ref = w[i].at[slot, ci] if i < n else w[i].at[slot]
        return pltpu.make_async_remote_copy(src_ref=ref, dst_ref=ref, send_sem=sems[0].at[3 * i + d],
                                            recv_sem=sems[1].at[3 * i + d], device_id=(px, py, ci), device_id_type=MESH)

    def d2d(w, sems, i, d, xi, yi, ci, slot, half):
        ref = w[i].at[slot, half]
        return pltpu.make_async_remote_copy(src_ref=ref, dst_ref=ref, send_sem=sems[2].at[3 * i + d],
                                            recv_sem=sems[3].at[3 * i + d], device_id=(xi, yi, 1 - ci),
                                            device_id_type=MESH)

    def start(_, w, sems):
        xi, yi, ci, peers = _place()
        for i in range(n + nw):
            for d, (px, py) in enumerate(peers):
                ici(w, sems, i, d, px, py, ci, 2 * xi + yi).start()

    def finish(_, w, sems):
        xi, yi, ci, peers = _place()
        for d, (px, py) in enumerate(peers):
            for i in range(n + nw):
                ici(w, sems, i, d, px, py, ci, 2 * px + py).wait_recv()
                if i < n:
                    d2d(w, sems, i, d, xi, yi, ci, 2 * px + py, ci).start()
        for d, (px, py) in enumerate(peers):
            for i in range(n):
                d2d(w, sems, i, d, xi, yi, ci, 2 * px + py, 1 - ci).wait_recv()
        for d, (px, py) in enumerate(peers):
            for i in range(n + nw):
                ici(w, sems, i, d, px, py, ci, 2 * xi + yi).wait_send()
                if i < n:
                    d2d(w, sems, i, d, xi, yi, ci, 2 * px + py, ci).wait_send()

    arrays = list(bufs) + list(whole)
    k = 3 * (n + nw)
    return _Comm(arrays, [jax.ShapeDtypeStruct(a.shape, a.dtype) for a in arrays], {i: i for i in range(n + nw)},
                 [k, k, max(3 * n, 1), max(3 * n, 1)], start, finish)


def _pair_gather(bufs, name):
    n = len(bufs)
    where = [(i, l) for i, a in enumerate(bufs) for l in range(a.shape[0])]

    def body(*refs):
        g = refs[n:2 * n]
        send_sems, recv_sems = refs[2 * n:]
        xi, yi, ci, _ = _place()
        sends = []
        for k, (i, l) in enumerate(where):
            mine = g[i].at[l, :, ci]
            cp = pltpu.make_async_remote_copy(src_ref=mine, dst_ref=mine, send_sem=send_sems.at[k],
                                              recv_sem=recv_sems.at[k], device_id=(xi, yi, 1 - ci), device_id_type=MESH)
            cp.start()
            sends.append(cp)
        for k, (i, l) in enumerate(where):
            other = g[i].at[l, :, 1 - ci]
            pltpu.make_async_remote_copy(src_ref=other, dst_ref=other, send_sem=send_sems.at[k],
                                         recv_sem=recv_sems.at[k], device_id=(xi, yi, 1 - ci),
                                         device_id_type=MESH).wait_recv()
        for cp in sends:
            cp.wait_send()

    return _inplace_call(body, name, list(bufs), [len(where), len(where)])


def _row_block(rows, cols):
    want = max(8, (2 ** 18 // max(cols, 1)) // 8 * 8)
    if rows <= want:
        return rows
    best = 8
    for t in range(8, want + 1, 8):
        if rows % t == 0:
            best = t
    return best


def _pair_add(g4, recv, sel, out_dtype, name):
    A, _, H, C = g4.shape
    bh = _row_block(H, C)

    def body(sel_ref, g_ref, r_ref, o_ref):
        o_ref[...] = (g_ref[...] + r_ref[...]).astype(out_dtype)

    return pl.pallas_call(
        body, name=name, out_shape=jax.ShapeDtypeStruct((A, H, C), out_dtype),
        grid_spec=pltpu.PrefetchScalarGridSpec(
            num_scalar_prefetch=1, grid=(A, H // bh),
            in_specs=[pl.BlockSpec((None, None, bh, C), lambda a, i, s: (a, s[1], i, 0)),
                      pl.BlockSpec((None, bh, C), lambda a, i, s: (a, i, 0))],
            out_specs=pl.BlockSpec((None, bh, C), lambda a, i, s: (a, i, 0))),
        compiler_params=pltpu.CompilerParams(dimension_semantics=("arbitrary", "arbitrary"),
                                             vmem_limit_bytes=VMEM_LIMIT),
    )(sel, g4, recv)


def _chip_reduce(g4, recv1, recv2, sel, buf, layer, n_layers, name):
    _, _, H, C = g4.shape
    bh = _row_block(H, C)

    def body(sel_ref, g_ref, r1_ref, r2_ref, *rest):
        acc = g_ref[...] + r1_ref[...]
        for d in range(3):
            acc = acc + r2_ref[d].astype(F32)
        rest[-1][...] = acc

    in_specs = [pl.BlockSpec((None, None, bh, C), lambda i, s: (s[0], s[1], i, 0)),
                pl.BlockSpec((None, bh, C), lambda i, s: (s[0], i, 0)),
                pl.BlockSpec((3, bh, C), lambda i, s: (0, i, 0))]
    args = [sel, g4, recv1, recv2]
    aliases = {}
    if buf is not None:
        in_specs.append(ANY)
        args.append(buf)
        aliases = {4: 0}
    return pl.pallas_call(
        body, name=name, out_shape=jax.ShapeDtypeStruct((n_layers, 1, 2, H, C), F32),
        grid_spec=pltpu.PrefetchScalarGridSpec(
            num_scalar_prefetch=1, grid=(H // bh,), in_specs=in_specs,
            out_specs=pl.BlockSpec((None, None, None, bh, C), lambda i, s: (layer, 0, s[1], i, 0))),
        input_output_aliases=aliases,
        compiler_params=pltpu.CompilerParams(dimension_semantics=("arbitrary",), vmem_limit_bytes=VMEM_LIMIT),
    )(*args)


def _cast_place(shard2, sel, name):
    _, H, C = shard2.shape
    bh = _row_block(H, C)

    def body(sel_ref, x_ref, o_ref):
        o_ref[...] = x_ref[...].astype(BF16)

    return pl.pallas_call(
        body, name=name, out_shape=jax.ShapeDtypeStruct((N_CHIPS, 2, H, C), BF16),
        grid_spec=pltpu.PrefetchScalarGridSpec(
            num_scalar_prefetch=1, grid=(2, H // bh),
            in_specs=[pl.BlockSpec((None, bh, C), lambda h, i, s: (h, i, 0))],
            out_specs=pl.BlockSpec((None, None, bh, C), lambda h, i, s: (s[0], h, i, 0))),
        compiler_params=pltpu.CompilerParams(dimension_semantics=("arbitrary", "arbitrary"),
                                             vmem_limit_bytes=VMEM_LIMIT),
    )(sel, shard2)


def _adamw(g, w, m, v, name):
    R, C = g.shape
    br = _row_block(R, C)

    def body(g_ref, w_ref, m_ref, v_ref, d_ref, mo_ref, vo_ref):
        gv = g_ref[...]
        mn = ADAM_B1 * m_ref[...] + (1.0 - ADAM_B1) * gv
        vn = ADAM_B2 * v_ref[...] + (1.0 - ADAM_B2) * (gv * gv)
        m_hat = mn / (1.0 - ADAM_B1 ** ADAM_STEP)
        v_hat = vn / (1.0 - ADAM_B2 ** ADAM_STEP)
        d_ref[...] = -ADAM_LR * (m_hat / (jnp.sqrt(v_hat) + ADAM_EPS) + ADAM_WD * w_ref[...])
        mo_ref[...] = mn
        vo_ref[...] = vn

    spec = pl.BlockSpec((br, C), lambda i: (i, 0))
    return pl.pallas_call(
        body, name=name, grid=(R // br,), out_shape=(jax.ShapeDtypeStruct((R, C), F32),) * 3,
        in_specs=[spec] * 4, out_specs=(spec,) * 3, compiler_params=_params(),
    )(g, w, m, v)


def _pack_rows(arrs, total_rows):
    parts = []
    for a in arrs:
        flat = a.reshape(-1)
        pad = (-flat.shape[0]) % LANES
        parts.append(jnp.pad(flat, (0, pad)).reshape(-1, LANES))
    packed = jnp.concatenate(parts, axis=0)
    return jnp.pad(packed, ((0, total_rows - packed.shape[0]), (0, 0)))


def _unpack_rows(packed, shapes):
    out, r = [], 0
    for shp in shapes:
        size = math.prod(shp)
        rows = -(-size // LANES)
        out.append(packed[r:r + rows].reshape(-1)[:size].reshape(shp))
        r += rows
    return out


def _rows_of(shapes):
    return sum(-(-math.prod(s) // LANES) for s in shapes)


_GATHER_STAGES = {"start": ["wa", "mkv0", "wo0"], "mixer_a_fwd": ["wup0", "wdown0"],
                  "mlp_fwd_0": ["wq", "wkv", "mkv1", "wo1"], "fox_fwd": ["wup1", "wdown1"]}
_REDUCE_STAGES = {"fox_bwd": ["wup1", "wdown1"], "mlp_bwd_0": ["wo1", "wq", "wkv", "mkv1"],
                  "mixer_a_bwd": ["wup0", "wdown0"], "end": ["wa", "wo0", "mkv0", "small"]}
_SMALL = ["ws", "b_s", "sg", "sb", "bf", "ln_g", "ln_b"]


class _Staged:
    def __init__(self, placed, sel, dm, ng, shard_cols, n_small):
        self.placed, self.sel, self.dm, self.ng, self.shard_cols, self.n_small = placed, sel, dm, ng, shard_cols, n_small
        self.g4, self.recv1, self.recv2 = {}, {}, {}

    def _weight(self, name, a):
        a = a.reshape(N_CHIPS, 2 * a.shape[2], a.shape[3])
        if name in ("wa", "wkv"):
            a = a.transpose(1, 0, 2).reshape(a.shape[1], -1)
            if name == "wkv":
                a = jnp.concatenate([a[:, :2 * self.dm],
                                     jnp.pad(a[:, 2 * self.dm:], ((0, 0), (0, LANES - self.ng)))], axis=1)
            return a
        if name.startswith("wup") or name.startswith("wdown"):
            return a
        return a.reshape(-1, a.shape[2])

    def _partial(self, name, grads):
        if name == "small":
            a = _pack_rows([grads[n] for n in _SMALL], self.n_small).reshape(N_CHIPS, -1, LANES)
        elif name in self.shard_cols:
            n = self.shard_cols[name]
            g = grads[name][:, :N_CHIPS * n]
            a = g.reshape(g.shape[0], N_CHIPS, n).transpose(1, 0, 2)
        elif name.startswith("wup"):
            a = grads[name]
        else:
            g = grads[name]
            a = g.reshape(N_CHIPS, g.shape[0] // N_CHIPS, g.shape[1])
        return a.reshape(N_CHIPS, 2, a.shape[1] // 2, a.shape[2])

    def plan(self, stage, w, grads):
        if stage in _GATHER_STAGES:
            return _gather_weights([self.placed[n] for n in _GATHER_STAGES[stage]])
        if stage in _REDUCE_STAGES:
            names = _REDUCE_STAGES[stage]
            g4 = [self._partial(n, grads) for n in names]
            recv1 = _pair_swap(g4, "reduce_pair_swap_" + stage)
            summed = []
            for n, g, r1 in zip(names, g4, recv1):
                self.g4[n], self.recv1[n] = g, r1
                summed.append(_pair_add(g, r1, self.sel, F32 if n == "small" else BF16, "reduce_pair_add_" + n))
            return _chip_exchange(summed)
        return None

    def done(self, stage, results, w):
        if stage in _GATHER_STAGES:
            for n, a in zip(_GATHER_STAGES[stage], results):
                w[n] = self._weight(n, a)
        elif stage in _REDUCE_STAGES:
            for n, r in zip(_REDUCE_STAGES[stage], results):
                self.recv2[n] = r


def kernel(x, mem, a_w_in, a_sgu_ln_g, a_sgu_ln_b, a_w_s, a_b_s, kv_w, kv_b_f, b_w_q, mem_w_kv, w_o, ln_g, ln_b, w_up, w_down, loss_target, m_a_w_in, m_a_sgu_ln_g, m_a_sgu_ln_b, m_a_w_s, m_a_b_s, m_kv_w, m_kv_b_f, m_b_w_q, m_mem_w_kv, m_w_o, m_ln_g, m_ln_b, m_w_up, m_w_down, v_a_w_in, v_a_sgu_ln_g, v_a_sgu_ln_b, v_a_w_s, v_a_b_s, v_kv_w, v_kv_b_f, v_b_w_q, v_mem_w_kv, v_w_o, v_ln_g, v_ln_b, v_w_up, v_w_down):
    xi, yi, ci = lax.axis_index("x"), lax.axis_index("y"), lax.axis_index("c")
    chip = 2 * xi + yi
    sel = jnp.stack([chip, ci]).astype(jnp.int32)
    S, D = x.shape[1], x.shape[2]
    DQ = mem_w_kv.shape[2] // 2
    DM = D - DQ
    NG = DM // HEAD_DIM
    KVW = kv_w.shape[1] * N_CHIPS

    shards = {"wa": a_w_in[0], "wkv": kv_w, "wq": b_w_q[0], "mkv0": mem_w_kv[0], "mkv1": mem_w_kv[1],
              "wo0": w_o[0], "wo1": w_o[1], "wup0": w_up[0], "wup1": w_up[1], "wdown0": w_down[0], "wdown1": w_down[1]}
    placed = {n: _cast_place(s.reshape(2, s.shape[0] // 2, s.shape[1]), sel, "cast_place_" + n)
              for n, s in shards.items()}
    ln_pack = jnp.concatenate([ln_g.reshape(4, -1), ln_b.reshape(4, -1)], axis=0)
    ln_buf = lax.dynamic_update_slice(jnp.zeros((N_CHIPS,) + ln_pack.shape, F32), ln_pack[None], (chip, 0, 0))
    small_shapes = [a_w_s.shape, a_b_s.shape, a_sgu_ln_g.shape, a_sgu_ln_b.shape, kv_b_f.shape]
    ln_shape = (4, D)
    n_small = -(-_rows_of(small_shapes + [ln_shape, ln_shape]) // 64) * 64
    hooks = _Staged(placed, sel, DM, NG, {"wa": a_w_in.shape[2], "wkv": kv_w.shape[1]}, n_small)
    weights = {"sg": a_sgu_ln_g, "sb": a_sgu_ln_b, "ws": a_w_s[0], "b_s": a_b_s[0], "bf": kv_b_f}
    got = _run_comm(_gather_weights([placed[n] for n in _GATHER_STAGES["start"]], whole=[ln_buf]),
                    "gather_weights_first")
    hooks.done("start", got[:-1], weights)
    ln_full = got[-1].transpose(1, 0, 2).reshape(8, D)
    weights["ln_g"], weights["ln_b"] = ln_full[:4], ln_full[4:]

    loss_part, grad_x, gr = _local_step(x[0], mem[0], loss_target[0], weights, hooks)
    loss = lax.psum(loss_part[0, 0], ("x", "y", "c"))

    hooks.done("end", _run_comm(hooks.plan("end", weights, gr), "reduce_chip_exchange_end"), weights)
    groups = [["wa"], ["wkv"], ["wq"], ["mkv0", "mkv1"], ["wo0", "wo1"], ["wup0", "wup1"], ["wdown0", "wdown1"],
              ["small"]]
    mine = []
    for grp in groups:
        buf = None
        for layer, n in enumerate(grp):
            buf = _chip_reduce(hooks.g4[n], hooks.recv1[n], hooks.recv2[n], sel, buf, layer, len(grp),
                               "reduce_chip_sum_" + n)
        mine.append(buf)
    red = _pair_gather(mine, "reduce_pair_gather")
    red = [a.reshape(a.shape[0], 2 * a.shape[3], a.shape[4]) for a in red]
    g_a_w_in, g_kv_w, g_b_w_q, g_mem_w_kv, g_w_o, g_w_up, g_w_down, small_mine = red
    g_kv_w = g_kv_w[0]
    small_all = _chip_allgather([small_mine[0]], [False], "gather_small_grads")[0].reshape(n_small, LANES)
    g_ws, g_bs, g_sg, g_sb, g_bf, g_lng, g_lnb = _unpack_rows(small_all, small_shapes + [ln_shape, ln_shape])
    dsh = D // N_CHIPS
    g_ln_g = lax.dynamic_slice_in_dim(g_lng, chip * dsh, dsh, axis=1).reshape(ln_g.shape)
    g_ln_b = lax.dynamic_slice_in_dim(g_lnb, chip * dsh, dsh, axis=1).reshape(ln_b.shape)

    def update(name, g, w, m, v):
        flat = lambda a: a.reshape(-1, a.shape[-1])
        d, mn, vn = _adamw(flat(g), flat(w), flat(m), flat(v), "adamw_" + name)
        return d.reshape(w.shape), mn.reshape(w.shape), vn.reshape(w.shape)

    upd = {
        "a_w_in": update("a_w_in", g_a_w_in, a_w_in, m_a_w_in, v_a_w_in),
        "kv_w": update("kv_w", g_kv_w, kv_w, m_kv_w, v_kv_w),
        "b_w_q": update("b_w_q", g_b_w_q, b_w_q, m_b_w_q, v_b_w_q),
        "mem_w_kv": update("mem_w_kv", g_mem_w_kv, mem_w_kv, m_mem_w_kv, v_mem_w_kv),
        "w_o": update("w_o", g_w_o, w_o, m_w_o, v_w_o),
        "w_up": update("w_up", g_w_up, w_up, m_w_up, v_w_up),
        "w_down": update("w_down", g_w_down, w_down, m_w_down, v_w_down),
    }
    tiny_g = [g_sg, g_sb, g_ws, g_bs, g_bf, g_ln_g, g_ln_b]
    tiny_w = [a_sgu_ln_g, a_sgu_ln_b, a_w_s, a_b_s, kv_b_f, ln_g, ln_b]
    tiny_m = [m_a_sgu_ln_g, m_a_sgu_ln_b, m_a_w_s, m_a_b_s, m_kv_b_f, m_ln_g, m_ln_b]
    tiny_v = [v_a_sgu_ln_g, v_a_sgu_ln_b, v_a_w_s, v_a_b_s, v_kv_b_f, v_ln_g, v_ln_b]
    tiny_shapes = [a.shape for a in tiny_w]
    n_tiny = -(-_rows_of(tiny_shapes) // 8) * 8
    td, tm_, tv = _adamw(_pack_rows(tiny_g, n_tiny), _pack_rows(tiny_w, n_tiny), _pack_rows(tiny_m, n_tiny),
                         _pack_rows(tiny_v, n_tiny), "adamw_small")
    for name, d, mn, vn in zip(["a_sgu_ln_g", "a_sgu_ln_b", "a_w_s", "a_b_s", "kv_b_f", "ln_g", "ln_b"],
                               _unpack_rows(td, tiny_shapes), _unpack_rows(tm_, tiny_shapes),
                               _unpack_rows(tv, tiny_shapes)):
        upd[name] = (d, mn, vn)

    order = ["a_w_in", "a_sgu_ln_g", "a_sgu_ln_b", "a_w_s", "a_b_s", "kv_w", "kv_b_f", "b_w_q", "mem_w_kv", "w_o",
             "ln_g", "ln_b", "w_up", "w_down"]
    grads = {"a_w_in": g_a_w_in.reshape(a_w_in.shape), "a_sgu_ln_g": g_sg.reshape(a_sgu_ln_g.shape),
             "a_sgu_ln_b": g_sb.reshape(a_sgu_ln_b.shape), "a_w_s": g_ws.reshape(a_w_s.shape),
             "a_b_s": g_bs.reshape(a_b_s.shape), "kv_w": g_kv_w, "kv_b_f": g_bf, "b_w_q": g_b_w_q.reshape(b_w_q.shape),
             "mem_w_kv": g_mem_w_kv, "w_o": g_w_o, "ln_g": g_ln_g, "ln_b": g_ln_b, "w_up": g_w_up, "w_down": g_w_down}
    return (loss, grad_x[None], *[grads[n] for n in order], *[upd[n][0] for n in order],
            *[upd[n][1] for n in order], *[upd[n][2] for n in order])
```

```python
import functools
import math

import jax
import jax.numpy as jnp
from jax import lax
from jax.experimental import pallas as pl
from jax.experimental.pallas import tpu as pltpu

F32 = jnp.float32
BF16 = jnp.bfloat16

HEAD_DIM = 64
PAIR = 2 * HEAD_DIM
CHUNK = 128
LN_EPS = 1e-5
ALPHA = 4 ** 0.25
QK_SCALE = 1.0 / math.sqrt(HEAD_DIM)
NEG = -1e30
N_CHIPS = 4
FOX_BLOCK = 256
LANES = 128
VMEM_LIMIT = 48 * 2 ** 20

ADAM_LR, ADAM_B1, ADAM_B2, ADAM_EPS, ADAM_WD, ADAM_STEP = 0.001, 0.9, 0.999, 1e-08, 0.01, 10

MESH = pl.DeviceIdType.MESH
ANY = pl.BlockSpec(memory_space=pl.ANY)


def _dot(a, b):
    return jnp.dot(a, b, preferred_element_type=F32)


def _dot_nt(a, b):
    return lax.dot_general(a, b, (((1,), (1,)), ((), ())), preferred_element_type=F32)


def _dot_tn(a, b):
    return lax.dot_general(a, b, (((0,), (0,)), ((), ())), preferred_element_type=F32)


def _ln_stats(r):
    mu = jnp.mean(r, axis=-1, keepdims=True)
    d = r - mu
    var = jnp.mean(d * d, axis=-1, keepdims=True)
    rstd = lax.rsqrt(var + LN_EPS)
    return d * rstd, rstd


def _ln_bwd(dy, xhat, rstd, g):
    dxh = dy * g
    m1 = jnp.mean(dxh, axis=-1, keepdims=True)
    m2 = jnp.mean(dxh * xhat, axis=-1, keepdims=True)
    return rstd * (dxh - m1 - xhat * m2)


_GELU_K = math.sqrt(2.0 / math.pi)


def _gelu(x):
    return 0.5 * x * (1.0 + jnp.tanh(_GELU_K * (x + 0.044715 * x * x * x)))


def _gelu_grad(x):
    t = jnp.tanh(_GELU_K * (x + 0.044715 * x * x * x))
    return 0.5 * (1.0 + t) + 0.5 * x * (1.0 - t * t) * _GELU_K * (1.0 + 3 * 0.044715 * x * x)


def _lane_mask(e):
    lane = lax.broadcasted_iota(jnp.int32, (1, PAIR), 1)
    return (lane >= HEAD_DIM * e) & (lane < HEAD_DIM * (e + 1))


def _first_head():
    return lax.broadcasted_iota(jnp.int32, (1, PAIR), 1) < HEAD_DIM


def _mem_probs(qh, mkp):
    sc = _dot_nt(qh, mkp) * QK_SCALE
    ex = jnp.exp(sc - jnp.max(sc, axis=-1, keepdims=True))
    return ex / jnp.sum(ex, axis=-1, keepdims=True)


def _mem_attn_fwd(qm_b, mkv_ref, dq_dim):
    outs = []
    for mp in range(dq_dim // PAIR):
        qp = qm_b[:, mp * PAIR:(mp + 1) * PAIR]
        mkp = mkv_ref[:, mp * PAIR:(mp + 1) * PAIR]
        mvp = mkv_ref[:, dq_dim + mp * PAIR:dq_dim + (mp + 1) * PAIR]
        heads = []
        for e in (0, 1):
            qh = jnp.where(_lane_mask(e), qp, jnp.zeros_like(qp))
            p = _mem_probs(qh, mkp)
            heads.append(_dot(p.astype(BF16), mvp))
        outs.append(jnp.where(_first_head(), heads[0], heads[1]))
    return outs


def _mem_attn_bwd(qm_b, mkv_ref, dmo_pairs, dmkv_ref, dq_dim):
    dqs = []
    for mp in range(dq_dim // PAIR):
        ks = slice(mp * PAIR, (mp + 1) * PAIR)
        vs = slice(dq_dim + mp * PAIR, dq_dim + (mp + 1) * PAIR)
        qp = qm_b[:, ks]
        mkp = mkv_ref[:, ks]
        mvp = mkv_ref[:, vs]
        dmo_b = dmo_pairs[mp].astype(BF16)
        dq = None
        dmk = None
        dmv = None
        for e in (0, 1):
            hm = _lane_mask(e)
            qh = jnp.where(hm, qp, jnp.zeros_like(qp))
            p = _mem_probs(qh, mkp)
            doh = jnp.where(hm, dmo_b, jnp.zeros_like(dmo_b))
            dp = _dot_nt(doh, mvp)
            ds = p * (dp - jnp.sum(dp * p, axis=-1, keepdims=True))
            dsb = (ds * QK_SCALE).astype(BF16)
            kh = jnp.where(hm, mkp, jnp.zeros_like(mkp))
            dq_e = _dot(dsb, kh)
            dmk_e = _dot_tn(dsb, qh)
            dmv_e = _dot_tn(p.astype(BF16), doh)
            dq = dq_e if dq is None else dq + dq_e
            dmk = dmk_e if dmk is None else dmk + dmk_e
            dmv = dmv_e if dmv is None else dmv + dmv_e
        dmkv_ref[:, ks] += dmk
        dmkv_ref[:, vs] += dmv
        dqs.append(dq)
    return dqs


def _params(n_axes=1):
    return pltpu.CompilerParams(dimension_semantics=("arbitrary",) * n_axes, vmem_limit_bytes=VMEM_LIMIT)


def _full(shape):
    return pl.BlockSpec(shape, lambda *_: (0,) * len(shape))


def _rows(tm, cols):
    return pl.BlockSpec((tm, cols), lambda i: (i, 0))


def _cols(rows, tm):
    return pl.BlockSpec((rows, tm), lambda i: (0, i))


def _pick(n, pref):
    if n <= pref:
        return n
    best = LANES
    for t in range(LANES, pref + 1, LANES):
        if n % t == 0:
            best = t
    return best if 2 * best >= pref or n > 2 * pref else n


class _Comm:
    def __init__(self, ins, out_shapes, aliases, sem_counts, start, finish):
        self.ins, self.out_shapes, self.aliases, self.sem_counts = list(ins), list(out_shapes), dict(aliases), sem_counts
        self.start, self.finish = start, finish


def _pcall(body, *, name, grid, in_specs, out_specs, out_shape, args, scratch_shapes=(), comm=None):
    n_in, n_out, n_scr = len(in_specs), len(out_shape), len(scratch_shapes)
    if comm is None:
        res = pl.pallas_call(body, name=name, grid=grid, out_shape=tuple(out_shape), in_specs=list(in_specs),
                             out_specs=tuple(out_specs), scratch_shapes=list(scratch_shapes),
                             compiler_params=_params())(*args)
        return tuple(res), ()
    nci, nco = len(comm.ins), len(comm.out_shapes)
    last = grid[0] - 1

    def wrapped(*refs):
        ins, refs = refs[:n_in], refs[n_in:]
        cins, refs = refs[:nci], refs[nci:]
        outs, refs = refs[:n_out], refs[n_out:]
        couts, refs = refs[:nco], refs[nco:]
        scr, sems = refs[:n_scr], refs[n_scr:]

        @pl.when(pl.program_id(0) == 0)
        def _():
            comm.start(cins, couts, sems)

        body(*ins, *outs, *scr)

        @pl.when(pl.program_id(0) == last)
        def _():
            comm.finish(cins, couts, sems)

    res = pl.pallas_call(
        wrapped, name=name, grid=grid, out_shape=tuple(out_shape) + tuple(comm.out_shapes),
        in_specs=list(in_specs) + [ANY] * nci, out_specs=tuple(out_specs) + (ANY,) * nco,
        input_output_aliases={n_in + a: n_out + b for a, b in comm.aliases.items()},
        scratch_shapes=list(scratch_shapes) + [pltpu.SemaphoreType.DMA((k,)) for k in comm.sem_counts],
        compiler_params=pltpu.CompilerParams(dimension_semantics=("arbitrary",), vmem_limit_bytes=VMEM_LIMIT,
                                             has_side_effects=True),
    )(*args, *comm.ins)
    return tuple(res[:n_out]), tuple(res[n_out:])


def _run_comm(comm, name):
    nci, nco = len(comm.ins), len(comm.out_shapes)

    def body(*refs):
        cins, couts, sems = refs[:nci], refs[nci:nci + nco], refs[nci + nco:]
        comm.start(cins, couts, sems)
        comm.finish(cins, couts, sems)

    return pl.pallas_call(
        body, name=name, out_shape=tuple(comm.out_shapes), in_specs=[ANY] * nci, out_specs=(ANY,) * nco,
        input_output_aliases=comm.aliases, scratch_shapes=[pltpu.SemaphoreType.DMA((k,)) for k in comm.sem_counts],
        compiler_params=pltpu.CompilerParams(has_side_effects=True),
    )(*comm.ins)


def _mm_tn(a, b, name, out_split=1):
    R, M = a.shape
    _, N = b.shape
    bm, br = _pick(M, 512), _pick(R, 512)
    ncol = N // out_split
    bn = _pick(ncol, 1024)
    per = ncol // bn

    def body(a_ref, b_ref, o_ref):
        @pl.when(pl.program_id(2) == 0)
        def _():
            o_ref[...] = jnp.zeros_like(o_ref)

        o_ref[...] += _dot_tn(a_ref[...], b_ref[...])

    if out_split == 1:
        out_shape = jax.ShapeDtypeStruct((M, N), F32)
        out_spec = pl.BlockSpec((bm, bn), lambda i, j, r: (i, j))
    else:
        out_shape = jax.ShapeDtypeStruct((out_split, M, ncol), F32)
        out_spec = pl.BlockSpec((None, bm, bn), lambda i, j, r: (j // per, i, j % per))
    return pl.pallas_call(
        body, name=name, out_shape=out_shape, grid=(M // bm, N // bn, R // br),
        in_specs=[pl.BlockSpec((br, bm), lambda i, j, r: (r, i)), pl.BlockSpec((br, bn), lambda i, j, r: (r, j))],
        out_specs=out_spec,
        compiler_params=pltpu.CompilerParams(dimension_semantics=("parallel", "parallel", "arbitrary"),
                                             vmem_limit_bytes=VMEM_LIMIT),
    )(a, b)


def _mm_nn(at, b, name, out_split=1):
    M, R = at.shape
    _, N = b.shape
    bm, br = _pick(M, 1024), _pick(R, 512)
    ncol = N // out_split
    bn = _pick(ncol, 1024)
    per = ncol // bn

    def body(a_ref, b_ref, o_ref):
        @pl.when(pl.program_id(2) == 0)
        def _():
            o_ref[...] = jnp.zeros_like(o_ref)

        o_ref[...] += _dot(a_ref[...], b_ref[...])

    if out_split == 1:
        out_shape = jax.ShapeDtypeStruct((M, N), F32)
        out_spec = pl.BlockSpec((bm, bn), lambda i, j, r: (i, j))
    else:
        out_shape = jax.ShapeDtypeStruct((out_split, M, ncol), F32)
        out_spec = pl.BlockSpec((None, bm, bn), lambda i, j, r: (j // per, i, j % per))
    return pl.pallas_call(
        body, name=name, out_shape=out_shape, grid=(M // bm, N // bn, R // br),
        in_specs=[pl.BlockSpec((bm, br), lambda i, j, r: (i, r)), pl.BlockSpec((br, bn), lambda i, j, r: (r, j))],
        out_specs=out_spec,
        compiler_params=pltpu.CompilerParams(dimension_semantics=("parallel", "parallel", "arbitrary"),
                                             vmem_limit_bytes=VMEM_LIMIT),
    )(at, b)


def _mm_small(a, b, name):
    def body(a_ref, b_ref, o_ref):
        o_ref[...] = _dot(a_ref[...], b_ref[...]).astype(BF16)

    return pl.pallas_call(body, name=name, out_shape=jax.ShapeDtypeStruct((a.shape[0], b.shape[1]), BF16),
                          compiler_params=pltpu.CompilerParams(vmem_limit_bytes=VMEM_LIMIT))(a, b)


def _mlp_fwd(x, wup4, wdown4, g, b, name, comm=None):
    S, D = x.shape
    NC, _, FC = wup4.shape
    tm = min(256, S)

    def body(x_ref, wup_ref, wdown_ref, g_ref, b_ref, xo_ref, xb_ref, xbt_ref, r_ref, h_ref):
        xv = x_ref[...]
        xb = xv.astype(BF16)
        y = jnp.zeros((tm, D), F32)
        for c in range(NC):
            a = jnp.maximum(_dot(xb, wup_ref[c]), 0.0)
            hb = (a * a).astype(BF16)
            h_ref[:, c * FC:(c + 1) * FC] = hb
            y = y + _dot(hb, wdown_ref[c])
        r = ALPHA * xv + y
        xhat, _ = _ln_stats(r)
        xo = xhat * g_ref[...] + b_ref[...]
        xo_ref[...] = xo
        xb_ref[...] = xo.astype(BF16)
        xbt_ref[...] = xo.T.astype(BF16)
        r_ref[...] = r

    return _pcall(
        body, name=name, grid=(S // tm,), comm=comm,
        out_shape=(jax.ShapeDtypeStruct((S, D), F32), jax.ShapeDtypeStruct((S, D), BF16),
                   jax.ShapeDtypeStruct((D, S), BF16), jax.ShapeDtypeStruct((S, D), F32),
                   jax.ShapeDtypeStruct((S, NC * FC), BF16)),
        in_specs=[_rows(tm, D), _full(wup4.shape), _full(wdown4.shape), _full((1, D)), _full((1, D))],
        out_specs=(_rows(tm, D), _rows(tm, D), _cols(D, tm), _rows(tm, D), _rows(tm, NC * FC)),
        args=(x, wup4, wdown4, g, b))


def _mlp_bwd(dxo, r, h, wup4, wdown4, g, name, comm=None):
    S, D = r.shape
    NC, _, FC = wup4.shape
    tm = min(256, S)

    def body(dxo_ref, r_ref, h_ref, wup_ref, wdown_ref, g_ref, dx_ref, drbt_ref, dpre_ref, dgb_ref):
        @pl.when(pl.program_id(0) == 0)
        def _():
            dgb_ref[...] = jnp.zeros_like(dgb_ref)

        xhat, rstd = _ln_stats(r_ref[...])
        dy = dxo_ref[...]
        dgb_ref[0:1, :] += jnp.sum(dy * xhat, axis=0, keepdims=True)
        dgb_ref[1:2, :] += jnp.sum(dy, axis=0, keepdims=True)
        dr = _ln_bwd(dy, xhat, rstd, g_ref[...])
        drb = dr.astype(BF16)
        drbt_ref[...] = dr.T.astype(BF16)
        dx = ALPHA * dr
        for c in range(NC):
            dh = _dot_nt(drb, wdown_ref[c])
            a = jnp.sqrt(h_ref[:, c * FC:(c + 1) * FC].astype(F32))
            dpre = (2.0 * a * dh).astype(BF16)
            dpre_ref[:, c * FC:(c + 1) * FC] = dpre
            dx = dx + _dot_nt(dpre, wup_ref[c])
        dx_ref[...] = dx

    return _pcall(
        body, name=name, grid=(S // tm,), comm=comm,
        out_shape=(jax.ShapeDtypeStruct((S, D), F32), jax.ShapeDtypeStruct((D, S), BF16),
                   jax.ShapeDtypeStruct((S, NC * FC), BF16), jax.ShapeDtypeStruct((2, D), F32)),
        in_specs=[_rows(tm, D), _rows(tm, D), _rows(tm, NC * FC), _full(wup4.shape), _full(wdown4.shape),
                  _full((1, D))],
        out_specs=(_rows(tm, D), _cols(D, tm), _rows(tm, NC * FC), _full((2, D))),
        args=(dxo, r, h, wup4, wdown4, g))


def _loss_grad(xo, target, name):
    S, D = xo.shape
    tm = min(512, S)

    def body(x_ref, t_ref, d_ref, l_ref):
        @pl.when(pl.program_id(0) == 0)
        def _():
            l_ref[...] = jnp.zeros_like(l_ref)

        err = x_ref[...] - t_ref[...]
        d_ref[...] = err * (1.0 / D)
        l_ref[...] += (0.5 / D) * jnp.sum(err * err)

    return pl.pallas_call(
        body, name=name, grid=(S // tm,),
        out_shape=(jax.ShapeDtypeStruct((S, D), F32), jax.ShapeDtypeStruct((8, LANES), F32)),
        in_specs=[_rows(tm, D), _rows(tm, D)], out_specs=(_rows(tm, D), _full((8, LANES))),
        compiler_params=_params(),
    )(xo, target)


def _tail_fwd_store(xv, cat_ref, wo_ref, g_ref, b_ref, xo_ref, xbt_ref, r_ref, catt_ref):
    cat = cat_ref[...]
    y = _dot(cat, wo_ref[...])
    r = ALPHA * xv + y
    xhat, _ = _ln_stats(r)
    xo = xhat * g_ref[...] + b_ref[...]
    xo_ref[...] = xo
    xbt_ref[...] = xo.T.astype(BF16)
    catt_ref[...] = cat.astype(F32).T.astype(BF16)
    r_ref[...] = r


def _tail_bwd_head(dxo_ref, r_ref, g_ref, wo_ref, dgb_ref, drb_ref):
    xhat, rstd = _ln_stats(r_ref[...])
    dy = dxo_ref[...]
    dgb_ref[0:1, :] += jnp.sum(dy * xhat, axis=0, keepdims=True)
    dgb_ref[1:2, :] += jnp.sum(dy, axis=0, keepdims=True)
    dr = _ln_bwd(dy, xhat, rstd, g_ref[...])
    drb = dr.astype(BF16)
    drb_ref[...] = drb
    return dr, _dot_nt(drb, wo_ref[...])


def _causal(shape):
    return lax.broadcasted_iota(jnp.int32, shape, 1) <= lax.broadcasted_iota(jnp.int32, shape, 0)


def _sgu_mixed(ws_ref, bst_ref, gp, vpair):
    tril = _causal((CHUNK, CHUNK))
    w0 = jnp.where(tril, ws_ref[2 * gp], 0.0).astype(BF16)
    w1 = jnp.where(tril, ws_ref[2 * gp + 1], 0.0).astype(BF16)
    m0 = _dot(w0, vpair) + bst_ref[:, 2 * gp:2 * gp + 1]
    m1 = _dot(w1, vpair) + bst_ref[:, 2 * gp + 1:2 * gp + 2]
    return jnp.where(_first_head(), m0, m1), w0, w1


def _mixer_a_fwd(x, wa, sg, sb, ws, bst, mkv, wo, g, b, name, comm=None):
    S, D = x.shape
    DQ = mkv.shape[1] // 2
    DM = D - DQ
    NP = DM // PAIR
    tm = min(512, S)

    def body(x_ref, wa_ref, sg_ref, sb_ref, ws_ref, bst_ref, mkv_ref, wo_ref, g_ref, b_ref,
             xo_ref, xbt_ref, r_ref, catt_ref, z_s, zv_s, cat_ref):
        xv = x_ref[...]
        z_s[...] = _dot(xv.astype(BF16), wa_ref[...])
        vhat, _ = _ln_stats(_gelu(z_s[:, DM:2 * DM]))
        zv_s[...] = (vhat * sg_ref[...] + sb_ref[...]).astype(BF16)
        for c in range(tm // CHUNK):
            rs = slice(c * CHUNK, (c + 1) * CHUNK)
            for gp in range(NP):
                ls = slice(gp * PAIR, (gp + 1) * PAIR)
                mixed, _, _ = _sgu_mixed(ws_ref, bst_ref, gp, zv_s[rs, ls])
                cat_ref[rs, ls] = (_gelu(z_s[rs, ls]) * mixed).astype(BF16)
        mo = _mem_attn_fwd(z_s[:, 2 * DM:].astype(BF16), mkv_ref, DQ)
        for mp in range(DQ // PAIR):
            cat_ref[:, DM + mp * PAIR:DM + (mp + 1) * PAIR] = mo[mp].astype(BF16)
        _tail_fwd_store(xv, cat_ref, wo_ref, g_ref, b_ref, xo_ref, xbt_ref, r_ref, catt_ref)

    return _pcall(
        body, name=name, grid=(S // tm,), comm=comm,
        out_shape=(jax.ShapeDtypeStruct((S, D), F32), jax.ShapeDtypeStruct((D, S), BF16),
                   jax.ShapeDtypeStruct((S, D), F32), jax.ShapeDtypeStruct((D, S), BF16)),
        in_specs=[_rows(tm, D), _full(wa.shape), _full(sg.shape), _full(sb.shape), _full(ws.shape), _full(bst.shape),
                  _full(mkv.shape), _full(wo.shape), _full((1, D)), _full((1, D))],
        out_specs=(_rows(tm, D), _cols(D, tm), _rows(tm, D), _cols(D, tm)),
        scratch_shapes=[pltpu.VMEM((tm, 2 * DM + DQ), F32), pltpu.VMEM((tm, DM), BF16), pltpu.VMEM((tm, D), BF16)],
        args=(x, wa, sg, sb, ws, bst, mkv, wo, g, b))


def _mixer_a_bwd(x, dxo, r, wa, sg, sb, ws, bst, mkv, wo, g, name, comm=None):
    S, D = x.shape
    DQ = mkv.shape[1] // 2
    DM = D - DQ
    NP = DM // PAIR
    NG = DM // HEAD_DIM
    tm = min(256, S)

    def body(x_ref, dxo_ref, r_ref, wa_ref, sg_ref, sb_ref, ws_ref, bst_ref, mkv_ref, wo_ref, g_ref,
             dx_ref, dz_ref, drb_ref, dgb_ref, dsgb_ref, dws_ref, dbst_ref, dmkv_ref, z_s, zv_s, dzv_s):
        @pl.when(pl.program_id(0) == 0)
        def _():
            dgb_ref[...] = jnp.zeros_like(dgb_ref)
            dsgb_ref[...] = jnp.zeros_like(dsgb_ref)
            dws_ref[...] = jnp.zeros_like(dws_ref)
            dbst_ref[...] = jnp.zeros_like(dbst_ref)
            dmkv_ref[...] = jnp.zeros_like(dmkv_ref)

        dr, dcat = _tail_bwd_head(dxo_ref, r_ref, g_ref, wo_ref, dgb_ref, drb_ref)
        z_s[...] = _dot(x_ref[...].astype(BF16), wa_ref[...])
        vhat, vrstd = _ln_stats(_gelu(z_s[:, DM:2 * DM]))
        zv_s[...] = (vhat * sg_ref[...] + sb_ref[...]).astype(BF16)
        tril = _causal((CHUNK, CHUNK))
        lane = lax.broadcasted_iota(jnp.int32, (1, LANES), 1)
        for c in range(tm // CHUNK):
            rs = slice(c * CHUNK, (c + 1) * CHUNK)
            for gp in range(NP):
                ls = slice(gp * PAIR, (gp + 1) * PAIR)
                vpair = zv_s[rs, ls]
                mixed, w0, w1 = _sgu_mixed(ws_ref, bst_ref, gp, vpair)
                u_pre = z_s[rs, ls]
                dmix = dcat[rs, ls]
                dz_ref[rs, ls] = (dmix * mixed * _gelu_grad(u_pre)).astype(BF16)
                dmixed = dmix * _gelu(u_pre)
                first = _first_head()
                d0 = jnp.where(first, dmixed, 0.0)
                d1 = jnp.where(first, 0.0, dmixed)
                d0b = d0.astype(BF16)
                d1b = d1.astype(BF16)
                dzv_s[rs, ls] = _dot_tn(w0, d0b) + _dot_tn(w1, d1b)
                dws_ref[2 * gp] += jnp.where(tril, _dot_nt(d0b, vpair), 0.0)
                dws_ref[2 * gp + 1] += jnp.where(tril, _dot_nt(d1b, vpair), 0.0)
                dbst_ref[...] += (jnp.where(lane == 2 * gp, jnp.sum(d0, axis=-1, keepdims=True), 0.0)
                                  + jnp.where(lane == 2 * gp + 1, jnp.sum(d1, axis=-1, keepdims=True), 0.0))
        dzv = dzv_s[...]
        dsgb_ref[0:1, :] += jnp.sum(dzv * vhat, axis=0, keepdims=True)
        dsgb_ref[1:2, :] += jnp.sum(dzv, axis=0, keepdims=True)
        dgv = _ln_bwd(dzv, vhat, vrstd, sg_ref[...])
        dz_ref[:, DM:2 * DM] = (dgv * _gelu_grad(z_s[:, DM:2 * DM])).astype(BF16)
        dmo = [dcat[:, DM + mp * PAIR:DM + (mp + 1) * PAIR] for mp in range(DQ // PAIR)]
        dqm = _mem_attn_bwd(z_s[:, 2 * DM:].astype(BF16), mkv_ref, dmo, dmkv_ref, DQ)
        for mp in range(DQ // PAIR):
            dz_ref[:, 2 * DM + mp * PAIR:2 * DM + (mp + 1) * PAIR] = dqm[mp].astype(BF16)
        dx_ref[...] = ALPHA * dr + _dot_nt(dz_ref[...], wa_ref[...])

    ZW = 2 * DM + DQ
    M = mkv.shape[0]
    return _pcall(
        body, name=name, grid=(S // tm,), comm=comm,
        out_shape=(jax.ShapeDtypeStruct((S, D), F32), jax.ShapeDtypeStruct((S, ZW), BF16),
                   jax.ShapeDtypeStruct((S, D), BF16), jax.ShapeDtypeStruct((2, D), F32),
                   jax.ShapeDtypeStruct((2, DM), F32), jax.ShapeDtypeStruct((NG, CHUNK, CHUNK), F32),
                   jax.ShapeDtypeStruct((CHUNK, LANES), F32), jax.ShapeDtypeStruct((M, 2 * DQ), F32)),
        in_specs=[_rows(tm, D), _rows(tm, D), _rows(tm, D), _full(wa.shape), _full(sg.shape), _full(sb.shape),
                  _full(ws.shape), _full(bst.shape), _full(mkv.shape), _full(wo.shape), _full((1, D))],
        out_specs=(_rows(tm, D), _rows(tm, ZW), _rows(tm, D), _full((2, D)), _full((2, DM)),
                   _full((NG, CHUNK, CHUNK)), _full((CHUNK, LANES)), _full((M, 2 * DQ))),
        scratch_shapes=[pltpu.VMEM((tm, ZW), F32), pltpu.VMEM((tm, DM), BF16), pltpu.VMEM((tm, DM), F32)],
        args=(x, dxo, r, wa, sg, sb, ws, bst, mkv, wo, g))


def _proj_b_fwd(xb, wq, wkv, dm, name):
    S, D = xb.shape
    DQ = D - dm
    tm = min(512, S)

    def body(x_ref, wq_ref, wkv_ref, q_ref, qm_ref, k_ref, v_ref, fl_ref):
        xv = x_ref[...]
        z = _dot(xv, wq_ref[...])
        q_ref[...] = z[:, :dm].astype(BF16)
        qm_ref[...] = z[:, dm:].astype(BF16)
        kvf = _dot(xv, wkv_ref[...])
        k_ref[...] = kvf[:, :dm].astype(BF16)
        v_ref[...] = kvf[:, dm:2 * dm].astype(BF16)
        fl_ref[...] = kvf[:, 2 * dm:]

    return pl.pallas_call(
        body, name=name, grid=(S // tm,),
        out_shape=(jax.ShapeDtypeStruct((S, dm), BF16), jax.ShapeDtypeStruct((S, DQ), BF16),
                   jax.ShapeDtypeStruct((S, dm), BF16), jax.ShapeDtypeStruct((S, dm), BF16),
                   jax.ShapeDtypeStruct((S, LANES), F32)),
        in_specs=[_rows(tm, D), _full(wq.shape), _full(wkv.shape)],
        out_specs=(_rows(tm, dm), _rows(tm, DQ), _rows(tm, dm), _rows(tm, dm), _rows(tm, LANES)),
        compiler_params=_params(),
    )(xb, wq, wkv)


def _proj_b_bwd(dr, dqz, dkvf, wq, wkv, name):
    S, D = dr.shape
    tm = min(512, S)

    def body(dr_ref, dqz_ref, dkvf_ref, wq_ref, wkv_ref, dx_ref):
        dx_ref[...] = (ALPHA * dr_ref[...] + _dot_nt(dqz_ref[...], wq_ref[...])
                       + _dot_nt(dkvf_ref[...], wkv_ref[...]))

    return pl.pallas_call(
        body, name=name, grid=(S // tm,), out_shape=jax.ShapeDtypeStruct((S, D), F32),
        in_specs=[_rows(tm, D), _rows(tm, dqz.shape[1]), _rows(tm, dkvf.shape[1]), _full(wq.shape), _full(wkv.shape)],
        out_specs=_rows(tm, D), compiler_params=_params(),
    )(dr, dqz, dkvf, wq, wkv)


def _split3(v):
    hi = v.astype(BF16)
    r1 = v - hi.astype(F32)
    mid = r1.astype(BF16)
    lo = (r1 - mid.astype(F32)).astype(BF16)
    return hi, mid, lo


def _tri_sum(v, tri_b):
    hi, mid, lo = _split3(v)
    return _dot(hi, tri_b) + _dot(mid, tri_b) + _dot(lo, tri_b)


def _log_sigmoid(x):
    return jnp.minimum(x, 0.0) - jnp.log(1.0 + jnp.exp(-jnp.abs(x)))


def _forget_cumsum(flt, bf, name):
    H, S = flt.shape
    nchunk = S // LANES

    def body(fl_ref, bf_ref, p_ref):
        upper = (lax.broadcasted_iota(jnp.int32, (LANES, LANES), 0)
                 <= lax.broadcasted_iota(jnp.int32, (LANES, LANES), 1)).astype(BF16)

        def step(n, carry):
            s0 = pl.multiple_of(n * LANES, LANES)
            lf = _log_sigmoid(fl_ref[:, pl.ds(s0, LANES)] + bf_ref[...])
            cs = _tri_sum(lf, upper) + carry
            for i, piece in enumerate(_split3(-cs)):
                p_ref[i, :, pl.ds(s0, LANES)] = piece.astype(F32)
            return cs[:, LANES - 1:LANES]

        lax.fori_loop(0, nchunk, step, jnp.zeros((H, 1), F32))

    return pl.pallas_call(body, name=name, out_shape=jax.ShapeDtypeStruct((3, H, S), F32),
                          compiler_params=pltpu.CompilerParams(vmem_limit_bytes=VMEM_LIMIT))(flt, bf)


def _forget_cumsum_bwd(dct, flt, bf, name):
    H, S = flt.shape
    nchunk = S // LANES

    def body(dc_ref, fl_ref, bf_ref, dfl_ref, dbf_ref):
        lower = (lax.broadcasted_iota(jnp.int32, (LANES, LANES), 0)
                 >= lax.broadcasted_iota(jnp.int32, (LANES, LANES), 1)).astype(BF16)

        def step(n, carry):
            tail, tot = carry
            s0 = pl.multiple_of((nchunk - 1 - n) * LANES, LANES)
            suffix = _tri_sum(dc_ref[:, pl.ds(s0, LANES)], lower) + tail
            xv = fl_ref[:, pl.ds(s0, LANES)] + bf_ref[...]
            dfl = suffix * (1.0 / (1.0 + jnp.exp(xv)))
            dfl_ref[:, pl.ds(s0, LANES)] = dfl
            return suffix[:, 0:1], tot + jnp.sum(dfl, axis=-1, keepdims=True)

        _, tot = lax.fori_loop(0, nchunk, step, (jnp.zeros((H, 1), F32), jnp.zeros((H, 1), F32)))
        dbf_ref[...] = jnp.broadcast_to(tot, (H, LANES))

    return pl.pallas_call(body, name=name,
                          out_shape=(jax.ShapeDtypeStruct((H, S), F32), jax.ShapeDtypeStruct((H, LANES), F32)),
                          compiler_params=pltpu.CompilerParams(vmem_limit_bytes=VMEM_LIMIT))(dct, flt, bf)


N_BIAS = 3


def _bias_ones():
    lane = lax.broadcasted_iota(jnp.int32, (1, PAIR), 1)
    return ((lane & (HEAD_DIM - 1)) < N_BIAS).astype(BF16)


def _fox_fwd(q, k, v, ca, name, comm=None):
    S, DM = q.shape
    NP = DM // PAIR
    T = min(FOX_BLOCK, S)
    nq = S // T

    def body(q_ref, k_ref, v_ref, ca_ref, o_ref, lse_ref):
        tril = _causal((T, T))
        heads = (_lane_mask(0), _lane_mask(1))
        ones3 = _bias_ones()

        def step(carry, qaug, kaug, vaug, masked):
            m, acc = carry
            s = _dot_nt(qaug, kaug)
            if masked:
                s = jnp.where(tril, s, NEG)
            m_new = jnp.maximum(m, jnp.max(s, axis=-1, keepdims=True))
            p = jnp.exp(s - m_new).astype(BF16)
            return m_new, jnp.exp(m - m_new) * acc + _dot(p, vaug)

        def kv_block(kj):
            s0 = pl.multiple_of(kj * T, T)
            kb, vb, cab = k_ref[pl.ds(s0, T), :], v_ref[pl.ds(s0, T), :], ca_ref[pl.ds(s0, T), :]
            return ([jnp.where(heads[e], kb, cab) for e in (0, 1)],
                    [jnp.where(heads[e], vb, jnp.ones_like(vb)) for e in (0, 1)])

        def q_pair(a, _):
            t0s = [pl.multiple_of((2 * a + r) * T, T) for r in (0, 1)]
            qaug = {}
            for r in (0, 1):
                qb = q_ref[pl.ds(t0s[r], T), :] * QK_SCALE
                for e in (0, 1):
                    qaug[r, e] = jnp.where(heads[e], qb, ones3)
            chains = [(r, e) for r in (0, 1) for e in (0, 1)]

            def full(kj, carries):
                kaug, vaug = kv_block(kj)
                return tuple(step(carries[i], qaug[r, e], kaug[e], vaug[e], False) for i, (r, e) in enumerate(chains))

            init = (jnp.full((T, 1), NEG, F32), jnp.zeros((T, PAIR), F32))
            carries = list(lax.fori_loop(0, 2 * a, full, (init,) * 4))
            kaug, vaug = kv_block(2 * a)
            for i, (r, e) in enumerate(chains):
                carries[i] = step(carries[i], qaug[r, e], kaug[e], vaug[e], r == 0)
            kaug, vaug = kv_block(2 * a + 1)
            for i, (r, e) in enumerate(chains):
                if r == 1:
                    carries[i] = step(carries[i], qaug[r, e], kaug[e], vaug[e], True)
            first = _first_head()
            for r in (0, 1):
                (m0, acc0), (m1, acc1) = carries[2 * r], carries[2 * r + 1]
                l0, l1 = acc0[:, HEAD_DIM:HEAD_DIM + 1], acc1[:, 0:1]
                o_ref[pl.ds(t0s[r], T), :] = jnp.where(first, acc0 / l0, acc1 / l1).astype(BF16)
                lse_t = jnp.where(first, m0 + jnp.log(l0), m1 + jnp.log(l1)).T
                lse_ref[0:1, pl.ds(t0s[r], T)] = lse_t[0:1, :]
                lse_ref[1:2, pl.ds(t0s[r], T)] = lse_t[HEAD_DIM:HEAD_DIM + 1, :]
            return 0

        lax.fori_loop(0, nq // 2, q_pair, 0)

    col = pl.BlockSpec((S, PAIR), lambda hp: (0, hp))
    return _pcall(
        body, name=name, grid=(NP,), comm=comm,
        out_shape=(jax.ShapeDtypeStruct((S, DM), BF16), jax.ShapeDtypeStruct((NP, 2, S), F32)),
        in_specs=[col, col, col, col], out_specs=(col, pl.BlockSpec((None, 2, S), lambda hp: (hp, 0, 0))),
        args=(q, k, v, ca))


def _fox_bwd(q, k, v, o, do, ca, lse, name, comm=None):
    S, DM = q.shape
    NP = DM // PAIR
    T = min(FOX_BLOCK, S)
    nq = S // T

    def body(q_ref, k_ref, v_ref, o_ref, do_ref, ca_ref, lse_ref, dq_ref, dk_ref, dv_ref, dc_ref,
             dq_acc, dcol_acc, dk_acc, dv_acc, dd_s):
        keep = lax.broadcasted_iota(jnp.int32, (T, T), 0) <= lax.broadcasted_iota(jnp.int32, (T, T), 1)
        lane = lax.broadcasted_iota(jnp.int32, (1, LANES), 1)
        heads = (_lane_mask(0), _lane_mask(1))
        ones3 = _bias_ones()
        dq_acc[...] = jnp.zeros_like(dq_acc)
        dcol_acc[...] = jnp.zeros_like(dcol_acc)
        dc_ref[...] = jnp.zeros_like(dc_ref)

        def row_dots(qi, _):
            t0 = pl.multiple_of(qi * T, T)
            prod = do_ref[pl.ds(t0, T), :].astype(F32) * o_ref[pl.ds(t0, T), :].astype(F32)
            cols = [jnp.sum(jnp.where(heads[e], prod, 0.0), axis=-1, keepdims=True) for e in (0, 1)]
            tile = jnp.where(lane == 0, cols[0], jnp.where(lane == 1, cols[1], 0.0))
            dd_s[:, pl.ds(t0, T)] = tile.T[0:8, :]
            return 0

        lax.fori_loop(0, nq, row_dots, 0)

        def kv_pair(b, _):
            dk_acc[...] = jnp.zeros_like(dk_acc)
            dv_acc[...] = jnp.zeros_like(dv_acc)
            s0s = [pl.multiple_of((2 * b + jj) * T, T) for jj in (0, 1)]
            kaug, ks, vbs = {}, {}, []
            for jj in (0, 1):
                kb = k_ref[pl.ds(s0s[jj], T), :]
                cab = ca_ref[pl.ds(s0s[jj], T), :]
                vbs.append(v_ref[pl.ds(s0s[jj], T), :])
                for e in (0, 1):
                    kaug[jj, e] = jnp.where(heads[e], kb, cab)
                    ks[jj, e] = jnp.where(heads[e], kb, jnp.zeros_like(kb)) * QK_SCALE

            def q_step(qi, blocks):
                t0 = pl.multiple_of(qi * T, T)
                qs = q_ref[pl.ds(t0, T), :] * QK_SCALE
                dob = do_ref[pl.ds(t0, T), :]
                dq = jnp.zeros((T, PAIR), F32)
                for e in (0, 1):
                    doh = jnp.where(heads[e], dob, jnp.zeros_like(dob))
                    qh = jnp.where(heads[e], qs, jnp.zeros_like(qs))
                    qaug = jnp.where(heads[e], qs, ones3)
                    lse_r = lse_ref[e:e + 1, pl.ds(t0, T)]
                    dd_r = dd_s[e:e + 1, pl.ds(t0, T)]
                    drow = jnp.zeros((1, T), F32)
                    for jj, masked in blocks:
                        st = _dot_nt(kaug[jj, e], qaug)
                        if masked:
                            st = jnp.where(keep, st, NEG)
                        pt = jnp.exp(st - lse_r)
                        dv_acc[jj] += _dot(pt.astype(BF16), doh)
                        dst = pt * (_dot_nt(vbs[jj], doh) - dd_r)
                        dsb = dst.astype(BF16)
                        dk_acc[jj] += _dot(dsb, qh)
                        dq = dq + _dot_tn(dsb, ks[jj, e])
                        dcol_acc[pl.ds(s0s[jj], T), :] -= jnp.where(lane == e, jnp.sum(dst, axis=-1, keepdims=True), 0.0)
                        drow = drow + jnp.sum(dst, axis=0, keepdims=True)
                    dc_ref[e:e + 1, pl.ds(t0, T)] += drow
                dq_acc[pl.ds(t0, T), :] += dq

            q_step(2 * b, [(0, True)])
            q_step(2 * b + 1, [(0, False), (1, True)])

            def rest(qi, _):
                q_step(qi, [(0, False), (1, False)])
                return 0

            lax.fori_loop(2 * b + 2, nq, rest, 0)
            for jj in (0, 1):
                dk_ref[pl.ds(s0s[jj], T), :] = dk_acc[jj].astype(BF16)
                dv_ref[pl.ds(s0s[jj], T), :] = dv_acc[jj].astype(BF16)
            return 0

        lax.fori_loop(0, nq // 2, kv_pair, 0)
        dq_ref[...] = dq_acc[...].astype(BF16)

        def add_cols(kj, _):
            s0 = pl.multiple_of(kj * T, T)
            dc_ref[:, pl.ds(s0, T)] += dcol_acc[pl.ds(s0, T), :].T[0:2, :]
            return 0

        lax.fori_loop(0, nq, add_cols, 0)

    col = pl.BlockSpec((S, PAIR), lambda hp: (0, hp))
    row2 = pl.BlockSpec((None, 2, S), lambda hp: (hp, 0, 0))
    return _pcall(
        body, name=name, grid=(NP,), comm=comm,
        out_shape=(jax.ShapeDtypeStruct((S, DM), BF16), jax.ShapeDtypeStruct((S, DM), BF16),
                   jax.ShapeDtypeStruct((S, DM), BF16), jax.ShapeDtypeStruct((NP, 2, S), F32)),
        in_specs=[col, col, col, col, col, col, row2], out_specs=(col, col, col, row2),
        scratch_shapes=[pltpu.VMEM((S, PAIR), F32), pltpu.VMEM((S, LANES), F32), pltpu.VMEM((2, T, PAIR), F32),
                        pltpu.VMEM((2, T, PAIR), F32), pltpu.VMEM((8, S), F32)],
        args=(q, k, v, o, do, ca, lse))


def _mixer_b_fwd(x, o, qm, mkv, wo, g, b, name):
    S, D = x.shape
    DQ = qm.shape[1]
    DM = D - DQ
    tm = min(512, S)

    def body(x_ref, o_ref, qm_ref, mkv_ref, wo_ref, g_ref, b_ref, xo_ref, xbt_ref, r_ref, catt_ref, cat_ref):
        cat_ref[:, :DM] = o_ref[...]
        mo = _mem_attn_fwd(qm_ref[...], mkv_ref, DQ)
        for mp in range(DQ // PAIR):
            cat_ref[:, DM + mp * PAIR:DM + (mp + 1) * PAIR] = mo[mp].astype(BF16)
        _tail_fwd_store(x_ref[...], cat_ref, wo_ref, g_ref, b_ref, xo_ref, xbt_ref, r_ref, catt_ref)

    return pl.pallas_call(
        body, name=name, grid=(S // tm,),
        out_shape=(jax.ShapeDtypeStruct((S, D), F32), jax.ShapeDtypeStruct((D, S), BF16),
                   jax.ShapeDtypeStruct((S, D), F32), jax.ShapeDtypeStruct((D, S), BF16)),
        in_specs=[_rows(tm, D), _rows(tm, DM), _rows(tm, DQ), _full(mkv.shape), _full(wo.shape), _full((1, D)),
                  _full((1, D))],
        out_specs=(_rows(tm, D), _cols(D, tm), _rows(tm, D), _cols(D, tm)),
        scratch_shapes=[pltpu.VMEM((tm, D), BF16)],
        compiler_params=_params(),
    )(x, o, qm, mkv, wo, g, b)


def _mixer_b_bwd(dxo, r, qm, mkv, wo, g, name):
    S, D = r.shape
    DQ = qm.shape[1]
    DM = D - DQ
    M = mkv.shape[0]
    tm = min(512, S)

    def body(dxo_ref, r_ref, qm_ref, mkv_ref, wo_ref, g_ref, dr_ref, drb_ref, dcat_ref, dgb_ref, dmkv_ref):
        @pl.when(pl.program_id(0) == 0)
        def _():
            dgb_ref[...] = jnp.zeros_like(dgb_ref)
            dmkv_ref[...] = jnp.zeros_like(dmkv_ref)

        dr, dcat = _tail_bwd_head(dxo_ref, r_ref, g_ref, wo_ref, dgb_ref, drb_ref)
        dr_ref[...] = dr
        dcat_ref[:, :DM] = dcat[:, :DM].astype(BF16)
        dmo = [dcat[:, DM + mp * PAIR:DM + (mp + 1) * PAIR] for mp in range(DQ // PAIR)]
        dqm = _mem_attn_bwd(qm_ref[...], mkv_ref, dmo, dmkv_ref, DQ)
        for mp in range(DQ // PAIR):
            dcat_ref[:, DM + mp * PAIR:DM + (mp + 1) * PAIR] = dqm[mp].astype(BF16)

    return pl.pallas_call(
        body, name=name, grid=(S // tm,),
        out_shape=(jax.ShapeDtypeStruct((S, D), F32), jax.ShapeDtypeStruct((S, D), BF16),
                   jax.ShapeDtypeStruct((S, D), BF16), jax.ShapeDtypeStruct((2, D), F32),
                   jax.ShapeDtypeStruct((M, 2 * DQ), F32)),
        in_specs=[_rows(tm, D), _rows(tm, D), _rows(tm, DQ), _full(mkv.shape), _full(wo.shape), _full((1, D))],
        out_specs=(_rows(tm, D), _rows(tm, D), _rows(tm, D), _full((2, D)), _full((M, 2 * DQ))),
        compiler_params=_params(),
    )(dxo, r, qm, mkv, wo, g)


class _NoExchange:
    def plan(self, stage, w, grads):
        return None

    def done(self, stage, results, w):
        pass


def _local_step(x, mem, target, w, hooks):
    S, D = x.shape
    DQ = w["mkv0"].shape[1] // 2
    DM = D - DQ
    NG = DM // HEAD_DIM
    NP = DM // PAIR
    ln_g, ln_b = w["ln_g"], w["ln_b"]
    row = lambda a, i: a[i:i + 1]
    memb = mem.astype(BF16)
    bst =jnp.pad(w["b_s"].T, ((0, 0), (0, LANES - NG)))
    grads = {}

    def staged(stage, fn, *args):
        out, got = fn(*args, stage, comm=hooks.plan(stage, w, grads))
        hooks.done(stage, got, w)
        return out

    mkv0 = _mm_small(memb, w["mkv0"], "mem_kv_0")
    x1, x1t, r1, cat0t = staged("mixer_a_fwd", _mixer_a_fwd, x, w["wa"], w["sg"], w["sb"], w["ws"], bst, mkv0,
                                w["wo0"], row(ln_g, 0), row(ln_b, 0))
    x2, x2b, x2t, r2, h0 = staged("mlp_fwd_0", _mlp_fwd, x1, w["wup0"], w["wdown0"], row(ln_g, 1), row(ln_b, 1))
    q, qm, k, v, fl = _proj_b_fwd(x2b, w["wq"], w["wkv"], DM, "proj_b_fwd")
    flt = fl[:, :16].T
    bfc = jnp.pad(w["bf"], (0, 16 - NG)).reshape(16, 1)
    pieces = _forget_cumsum(flt, bfc, "forget_cumsum")
    pieces = jnp.pad(pieces[:, :NG].transpose(2, 1, 0), ((0, 0), (0, 0), (0, HEAD_DIM - N_BIAS)))
    ca = pieces.reshape(S, NP, 2, HEAD_DIM)[:, :, ::-1].reshape(S, DM).astype(BF16)
    o, lse = staged("fox_fwd", _fox_fwd, q, k, v, ca)
    mkv1 = _mm_small(memb, w["mkv1"], "mem_kv_1")
    x3, x3t, r3, cat1t = _mixer_b_fwd(x2, o, qm, mkv1, w["wo1"], row(ln_g, 2), row(ln_b, 2), "mixer_b_fwd")
    (x4, _, _, r4, h1), _ = _mlp_fwd(x3, w["wup1"], w["wdown1"], row(ln_g, 3), row(ln_b, 3), "mlp_fwd_1")

    nc = w["wup0"].shape[0]
    dx4, loss_part = _loss_grad(x4, target, "loss_grad")
    (dx3, dr4t, dpre1, dgb11), _ = _mlp_bwd(dx4, r4, h1, w["wup1"], w["wdown1"], row(ln_g, 3), "mlp_bwd_1")
    grads["wdown1"] = _mm_nn(dr4t, h1, "dw_down_1", out_split=nc)
    grads["wup1"] = _mm_nn(x3t, dpre1, "dw_up_1", out_split=nc)
    dr3, dr3b, dcat1, dgb10, dmkv1 = _mixer_b_bwd(dx3, r3, qm, mkv1, w["wo1"], row(ln_g, 2), "mixer_b_bwd")
    do = dcat1[:, :DM]
    dq, dk, dv, dct3 = staged("fox_bwd", _fox_bwd, q, k, v, o, do, ca, lse)
    dct = jnp.pad(dct3.reshape(NG, S), ((0, 16 - NG), (0, 0)))
    dflt, dbf = _forget_cumsum_bwd(dct, flt, bfc, "forget_cumsum_bwd")
    dfl = jnp.pad(dflt.T, ((0, 0), (0, LANES - 16))).astype(BF16)
    dqz = jnp.concatenate([dq, dcat1[:, DM:]], axis=1)
    dkvf = jnp.concatenate([dk, dv, dfl], axis=1)
    dx2 = _proj_b_bwd(dr3, dqz, dkvf, w["wq"], w["wkv"], "proj_b_bwd")
    grads["wo1"] = _mm_nn(cat1t, dr3b, "dw_o_1")
    grads["wq"] = _mm_nn(x2t, dqz, "dw_q")
    grads["wkv"] = _mm_nn(x2t, dkvf, "dw_kv")
    grads["mkv1"] = _mm_tn(memb, dmkv1.astype(BF16), "dw_mkv_1")
    dx1, dr2t, dpre0, dgb01 = staged("mlp_bwd_0", _mlp_bwd, dx2, r2, h0, w["wup0"], w["wdown0"], row(ln_g, 1))
    grads["wdown0"] = _mm_nn(dr2t, h0, "dw_down_0", out_split=nc)
    grads["wup0"] = _mm_nn(x1t, dpre0, "dw_up_0", out_split=nc)
    dx0, dz, dr1b, dgb00, dsgb, dws, dbst, dmkv0 = staged(
        "mixer_a_bwd", _mixer_a_bwd, x, dx1, r1, w["wa"], w["sg"], w["sb"], w["ws"], bst, mkv0, w["wo0"], row(ln_g, 0))
    grads["wo0"] = _mm_nn(cat0t, dr1b, "dw_o_0")
    grads["wa"] = _mm_nn(x.T.astype(BF16), dz, "dw_a")
    grads["mkv0"] = _mm_tn(memb, dmkv0.astype(BF16), "dw_mkv_0")
    grads.update({
        "ws": dws, "b_s": dbst[:, :NG].T, "sg": dsgb[0:1], "sb": dsgb[1:2], "bf": dbf[:NG, 0],
        "ln_g": jnp.concatenate([dgb00[0:1], dgb01[0:1], dgb10[0:1], dgb11[0:1]], axis=0),
        "ln_b": jnp.concatenate([dgb00[1:2], dgb01[1:2], dgb10[1:2], dgb11[1:2]], axis=0),
    })
    return loss_part, dx0, grads


def _place():
    xi, yi, ci = lax.axis_index("x"), lax.axis_index("y"), lax.axis_index("c")
    peers = [(xi, 1 - yi), (1 - xi, yi), (1 - xi, 1 - yi)]
    return xi, yi, ci, peers


def _comm_call(body, name, ins, out_shapes, n_remote, n_local):
    return pl.pallas_call(
        body, name=name, out_shape=out_shapes, in_specs=[ANY] * len(ins), out_specs=[ANY] * len(out_shapes),
        scratch_shapes=[pltpu.SemaphoreType.DMA((n_remote,)), pltpu.SemaphoreType.DMA((n_remote,)),
                        pltpu.SemaphoreType.DMA((max(n_local, 1),))],
        compiler_params=pltpu.CompilerParams(has_side_effects=True),
    )(*ins)


def _chip_allgather(xs, halved, name):
    n = len(xs)
    out_shapes = [jax.ShapeDtypeStruct((N_CHIPS,) + (a.shape[1:] if hv else a.shape), a.dtype)
                  for a, hv in zip(xs, halved)]

    def body(*refs):
        x_refs, o_refs = refs[:n], refs[n:2 * n]
        send_sems, recv_sems, loc_sems = refs[2 * n:]
        xi, yi, ci, peers = _place()
        j = 2 * xi + yi
        sends, locs = [], []
        for i in range(n):
            src = x_refs[i].at[ci] if halved[i] else x_refs[i]
            loc = pltpu.make_async_copy(src, o_refs[i].at[j], loc_sems.at[i])
            loc.start()
            locs.append(loc)
            for d, (px, py) in enumerate(peers):
                cp = pltpu.make_async_remote_copy(src_ref=src, dst_ref=o_refs[i].at[j], send_sem=send_sems.at[3 * i + d],
                                                  recv_sem=recv_sems.at[3 * i + d], device_id=(px, py, ci),
                                                  device_id_type=MESH)
                cp.start()
                sends.append(cp)
        for i in range(n):
            src = x_refs[i].at[ci] if halved[i] else x_refs[i]
            for d, (px, py) in enumerate(peers):
                pltpu.make_async_remote_copy(src_ref=src, dst_ref=o_refs[i].at[2 * px + py],
                                             send_sem=send_sems.at[3 * i + d], recv_sem=recv_sems.at[3 * i + d],
                                             device_id=(px, py, ci), device_id_type=MESH).wait_recv()
        for cp in sends:
            cp.wait_send()
        for loc in locs:
            loc.wait()

    return _comm_call(body, name, xs, out_shapes, 3 * n, n)


def _pair_swap(xs, name):
    n = len(xs)
    out_shapes = [jax.ShapeDtypeStruct((a.shape[0],) + a.shape[2:], a.dtype) for a in xs]

    def body(*refs):
        x_refs, o_refs = refs[:n], refs[n:2 * n]
        send_sems, recv_sems, _ = refs[2 * n:]
        xi, yi, ci, _ = _place()
        cps = []
        for i in range(n):
            cp = pltpu.make_async_remote_copy(src_ref=x_refs[i].at[:, 1 - ci], dst_ref=o_refs[i],
                                              send_sem=send_sems.at[i], recv_sem=recv_sems.at[i],
                                              device_id=(xi, yi, 1 - ci), device_id_type=MESH)
            cp.start()
            cps.append(cp)
        for cp in cps:
            cp.wait_recv()
        for cp in cps:
            cp.wait_send()

    return _comm_call(body, name, xs, out_shapes, n, 0)


def _chip_exchange(xs):
    n = len(xs)
    out_shapes = [jax.ShapeDtypeStruct((3,) + a.shape[1:], a.dtype) for a in xs]

    def copies(x_refs, o_refs, sems):
        send_sems, recv_sems = sems
        xi, yi, ci, peers = _place()
        return [pltpu.make_async_remote_copy(src_ref=x_refs[i].at[2 * px + py], dst_ref=o_refs[i].at[d],
                                             send_sem=send_sems.at[3 * i + d], recv_sem=recv_sems.at[3 * i + d],
                                             device_id=(px, py, ci), device_id_type=MESH)
                for i in range(n) for d, (px, py) in enumerate(peers)]

    def start(x_refs, o_refs, sems):
        for cp in copies(x_refs, o_refs, sems):
            cp.start()

    def finish(x_refs, o_refs, sems):
        cps = copies(x_refs, o_refs, sems)
        for cp in cps:
            cp.wait_recv()
        for cp in cps:
            cp.wait_send()

    return _Comm(xs, out_shapes, {}, [3 * n, 3 * n], start, finish)


def _inplace_call(body, name, bufs, sem_counts):
    n = len(bufs)
    return pl.pallas_call(
        body, name=name, out_shape=[jax.ShapeDtypeStruct(a.shape, a.dtype) for a in bufs],
        in_specs=[ANY] * n, out_specs=[ANY] * n, input_output_aliases={i: i for i in range(n)},
        scratch_shapes=[pltpu.SemaphoreType.DMA((k,)) for k in sem_counts],
        compiler_params=pltpu.CompilerParams(has_side_effects=True),
    )(*bufs)


def _gather_weights(bufs, whole=()):
    n, nw = len(bufs), len(whole)

    def ici(w, sems, i, d, px, py, ci, slot):
        ref = w[i].at[slot, ci] if i < n else w[i].at[slot]
        return pltpu.make_async_remote_copy(src_ref=ref, dst_ref=ref, send_sem=sems[0].at[3 * i + d],
                                            recv_sem=sems[1].at[3 * i + d], device_id=(px, py, ci), device_id_type=MESH)

    def d2d(w, sems, i, d, xi, yi, ci, slot, half):
        ref = w[i].at[slot, half]
        return pltpu.make_async_remote_copy(src_ref=ref, dst_ref=ref, send_sem=sems[2].at[3 * i + d],
                                            recv_sem=sems[3].at[3 * i + d], device_id=(xi, yi, 1 - ci),
                                            device_id_type=MESH)

    def start(_, w, sems):
        xi, yi, ci, peers = _place()
        for i in range(n + nw):
            for d, (px, py) in enumerate(peers):
                ici(w, sems, i, d, px, py, ci, 2 * xi + yi).start()

    def finish(_, w, sems):
        xi, yi, ci, peers = _place()
        for d, (px, py) in enumerate(peers):
            for i in range(n + nw):
                ici(w, sems, i, d, px, py, ci, 2 * px + py).wait_recv()
                if i < n:
                    d2d(w, sems, i, d, xi, yi, ci, 2 * px + py, ci).start()
        for d, (px, py) in enumerate(peers):
            for i in range(n):
                d2d(w, sems, i, d, xi, yi, ci, 2 * px + py, 1 - ci).wait_recv()
        for d, (px, py) in enumerate(peers):
            for i in range(n + nw):
                ici(w, sems, i, d, px, py, ci, 2 * xi + yi).wait_send()
                if i < n:
                    d2d(w, sems, i, d, xi, yi, ci, 2 * px + py, ci).wait_send()

    arrays = list(bufs) + list(whole)
    k = 3 * (n + nw)
    return _Comm(arrays, [jax.ShapeDtypeStruct(a.shape, a.dtype) for a in arrays], {i: i for i in range(n + nw)},
                 [k, k, max(3 * n, 1), max(3 * n, 1)], start, finish)


def _pair_gather(bufs, name):
    n = len(bufs)
    where = [(i, l) for i, a in enumerate(bufs) for l in range(a.shape[0])]

    def body(*refs):
        g = refs[n:2 * n]
        send_sems, recv_sems = refs[2 * n:]
        xi, yi, ci, _ = _place()
        sends = []
        for k, (i, l) in enumerate(where):
            mine = g[i].at[l, :, ci]
            cp = pltpu.make_async_remote_copy(src_ref=mine, dst_ref=mine, send_sem=send_sems.at[k],
                                              recv_sem=recv_sems.at[k], device_id=(xi, yi, 1 - ci), device_id_type=MESH)
            cp.start()
            sends.append(cp)
        for k, (i, l) in enumerate(where):
            other = g[i].at[l, :, 1 - ci]
            pltpu.make_async_remote_copy(src_ref=other, dst_ref=other, send_sem=send_sems.at[k],
                                         recv_sem=recv_sems.at[k], device_id=(xi, yi, 1 - ci),
                                         device_id_type=MESH).wait_recv()
        for cp in sends:
            cp.wait_send()

    return _inplace_call(body, name, list(bufs), [len(where), len(where)])


def _row_block(rows, cols):
    want = max(8, (2 ** 18 // max(cols, 1)) // 8 * 8)
    if rows <= want:
        return rows
    best = 8
    for t in range(8, want + 1, 8):
        if rows % t == 0:
            best = t
    return best


def _pair_add(g4, recv, sel, out_dtype, name):
    A, _, H, C = g4.shape
    bh = _row_block(H, C)

    def body(sel_ref, g_ref, r_ref, o_ref):
        o_ref[...] = (g_ref[...] + r_ref[...]).astype(out_dtype)

    return pl.pallas_call(
        body, name=name, out_shape=jax.ShapeDtypeStruct((A, H, C), out_dtype),
        grid_spec=pltpu.PrefetchScalarGridSpec(
            num_scalar_prefetch=1, grid=(A, H // bh),
            in_specs=[pl.BlockSpec((None, None, bh, C), lambda a, i, s: (a, s[1], i, 0)),
                      pl.BlockSpec((None, bh, C), lambda a, i, s: (a, i, 0))],
            out_specs=pl.BlockSpec((None, bh, C), lambda a, i, s: (a, i, 0))),
        compiler_params=pltpu.CompilerParams(dimension_semantics=("arbitrary", "arbitrary"),
                                             vmem_limit_bytes=VMEM_LIMIT),
    )(sel, g4, recv)


def _chip_reduce(g4, recv1, recv2, sel, buf, layer, n_layers, name):
    _, _, H, C = g4.shape
    bh = _row_block(H, C)

    def body(sel_ref, g_ref, r1_ref, r2_ref, *rest):
        acc = g_ref[...] + r1_ref[...]
        for d in range(3):
            acc = acc + r2_ref[d].astype(F32)
        rest[-1][...] = acc

    in_specs = [pl.BlockSpec((None, None, bh, C), lambda i, s: (s[0], s[1], i, 0)),
                pl.BlockSpec((None, bh, C), lambda i, s: (s[0], i, 0)),
                pl.BlockSpec((3, bh, C), lambda i, s: (0, i, 0))]
    args = [sel, g4, recv1, recv2]
    aliases = {}
    if buf is not None:
        in_specs.append(ANY)
        args.append(buf)
        aliases = {4: 0}
    return pl.pallas_call(
        body, name=name, out_shape=jax.ShapeDtypeStruct((n_layers, 1, 2, H, C), F32),
        grid_spec=pltpu.PrefetchScalarGridSpec(
            num_scalar_prefetch=1, grid=(H // bh,), in_specs=in_specs,
            out_specs=pl.BlockSpec((None, None, None, bh, C), lambda i, s: (layer, 0, s[1], i, 0))),
        input_output_aliases=aliases,
        compiler_params=pltpu.CompilerParams(dimension_semantics=("arbitrary",), vmem_limit_bytes=VMEM_LIMIT),
    )(*args)


def _cast_place(shard2, sel, name):
    _, H, C = shard2.shape
    bh = _row_block(H, C)

    def body(sel_ref, x_ref, o_ref):
        o_ref[...] = x_ref[...].astype(BF16)

    return pl.pallas_call(
        body, name=name, out_shape=jax.ShapeDtypeStruct((N_CHIPS, 2, H, C), BF16),
        grid_spec=pltpu.PrefetchScalarGridSpec(
            num_scalar_prefetch=1, grid=(2, H // bh),
            in_specs=[pl.BlockSpec((None, bh, C), lambda h, i, s: (h, i, 0))],
            out_specs=pl.BlockSpec((None, None, bh, C), lambda h, i, s: (s[0], h, i, 0))),
        compiler_params=pltpu.CompilerParams(dimension_semantics=("arbitrary", "arbitrary"),
                                             vmem_limit_bytes=VMEM_LIMIT),
    )(sel, shard2)


def _adamw(g, w, m, v, name):
    R, C = g.shape
    br = _row_block(R, C)

    def body(g_ref, w_ref, m_ref, v_ref, d_ref, mo_ref, vo_ref):
        gv = g_ref[...]
        mn = ADAM_B1 * m_ref[...] + (1.0 - ADAM_B1) * gv
        vn = ADAM_B2 * v_ref[...] + (1.0 - ADAM_B2) * (gv * gv)
        m_hat = mn / (1.0 - ADAM_B1 ** ADAM_STEP)
        v_hat = vn / (1.0 - ADAM_B2 ** ADAM_STEP)
        d_ref[...] = -ADAM_LR * (m_hat / (jnp.sqrt(v_hat) + ADAM_EPS) + ADAM_WD * w_ref[...])
        mo_ref[...] = mn
        vo_ref[...] = vn

    spec = pl.BlockSpec((br, C), lambda i: (i, 0))
    return pl.pallas_call(
        body, name=name, grid=(R // br,), out_shape=(jax.ShapeDtypeStruct((R, C), F32),) * 3,
        in_specs=[spec] * 4, out_specs=(spec,) * 3, compiler_params=_params(),
    )(g, w, m, v)


def _part_rows(shape):
    return -(-math.prod(shape) // (8 * LANES)) * 8


def _pack_rows(arrs, total_rows):
    parts = []
    for a in arrs:
        flat = a.reshape(-1)
        rows = _part_rows(a.shape)
        parts.append(jnp.pad(flat, (0, rows * LANES - flat.shape[0])).reshape(rows, LANES))
    packed = jnp.concatenate(parts, axis=0)
    return jnp.pad(packed, ((0, total_rows - packed.shape[0]), (0, 0)))


def _unpack_rows(packed, shapes):
    out, r = [], 0
    for shp in shapes:
        size, rows = math.prod(shp), _part_rows(shp)
        out.append(packed[r:r + rows].reshape(-1)[:size].reshape(shp))
        r += rows
    return out


def _rows_of(shapes):
    return sum(_part_rows(s) for s in shapes)


_GATHER_STAGES = {"start": ["wa", "mkv0", "wo0"], "mixer_a_fwd": ["wup0", "wdown0"],
                  "mlp_fwd_0": ["wq", "wkv", "mkv1", "wo1"], "fox_fwd": ["wup1", "wdown1"]}
_REDUCE_STAGES = {"fox_bwd": ["wup1", "wdown1"], "mlp_bwd_0": ["wo1", "wq", "wkv", "mkv1"],
                  "mixer_a_bwd": ["wup0", "wdown0"], "end": ["wa", "wo0", "mkv0", "small"]}
_SMALL = ["ws", "b_s", "sg", "sb", "bf", "ln_g", "ln_b"]


class _Staged:
    def __init__(self, placed, sel, dm, ng, shard_cols, n_small):
        self.placed, self.sel, self.dm, self.ng, self.shard_cols, self.n_small = placed, sel, dm, ng, shard_cols, n_small
        self.g4, self.recv1, self.recv2 = {}, {}, {}

    def _weight(self, name, a):
        a = a.reshape(N_CHIPS, 2 * a.shape[2], a.shape[3])
        if name in ("wa", "wkv"):
            a = a.transpose(1, 0, 2).reshape(a.shape[1], -1)
            if name == "wkv":
                a = jnp.concatenate([a[:, :2 * self.dm],
                                     jnp.pad(a[:, 2 * self.dm:], ((0, 0), (0, LANES - self.ng)))], axis=1)
            return a
        if name.startswith("wup") or name.startswith("wdown"):
            return a
        return a.reshape(-1, a.shape[2])

    def _partial(self, name, grads):
        if name == "small":
            a = _pack_rows([grads[n] for n in _SMALL], self.n_small).reshape(N_CHIPS, -1, LANES)
        elif name in self.shard_cols:
            n = self.shard_cols[name]
            g = grads[name][:, :N_CHIPS * n]
            a = g.reshape(g.shape[0], N_CHIPS, n).transpose(1, 0, 2)
        elif name.startswith("wup") or name.startswith("wdown"):
            a = grads[name]
        else:
            g = grads[name]
            a = g.reshape(N_CHIPS, g.shape[0] // N_CHIPS, g.shape[1])
        return a.reshape(N_CHIPS, 2, a.shape[1] // 2, a.shape[2])

    def plan(self, stage, w, grads):
        if stage in _GATHER_STAGES:
            return _gather_weights([self.placed[n] for n in _GATHER_STAGES[stage]])
        if stage in _REDUCE_STAGES:
            names = _REDUCE_STAGES[stage]
            g4 = [self._partial(n, grads) for n in names]
            recv1 = _pair_swap(g4, "reduce_pair_swap_" + stage)
            summed = []
            for n, g, r1 in zip(names, g4, recv1):
                self.g4[n], self.recv1[n] = g, r1
                summed.append(_pair_add(g, r1, self.sel, F32 if n == "small" else BF16, "reduce_pair_add_" + n))
            return _chip_exchange(summed)
        return None

    def done(self, stage, results, w):
        if stage in _GATHER_STAGES:
            for n, a in zip(_GATHER_STAGES[stage], results):
                w[n] = self._weight(n, a)
        elif stage in _REDUCE_STAGES:
            for n, r in zip(_REDUCE_STAGES[stage], results):
                self.recv2[n] = r


def kernel(x, mem, a_w_in, a_sgu_ln_g, a_sgu_ln_b, a_w_s, a_b_s, kv_w, kv_b_f, b_w_q, mem_w_kv, w_o, ln_g, ln_b, w_up, w_down, loss_target, m_a_w_in, m_a_sgu_ln_g, m_a_sgu_ln_b, m_a_w_s, m_a_b_s, m_kv_w, m_kv_b_f, m_b_w_q, m_mem_w_kv, m_w_o, m_ln_g, m_ln_b, m_w_up, m_w_down, v_a_w_in, v_a_sgu_ln_g, v_a_sgu_ln_b, v_a_w_s, v_a_b_s, v_kv_w, v_kv_b_f, v_b_w_q, v_mem_w_kv, v_w_o, v_ln_g, v_ln_b, v_w_up, v_w_down):
    xi, yi, ci = lax.axis_index("x"), lax.axis_index("y"), lax.axis_index("c")
    chip = 2 * xi + yi
    sel = jnp.stack([chip, ci]).astype(jnp.int32)
    S, D = x.shape[1], x.shape[2]
    DQ = mem_w_kv.shape[2] // 2
    DM = D - DQ
    NG = DM // HEAD_DIM

    shards = {"wa": a_w_in[0], "wkv": kv_w, "wq": b_w_q[0], "mkv0": mem_w_kv[0], "mkv1": mem_w_kv[1],
              "wo0": w_o[0], "wo1": w_o[1], "wup0": w_up[0], "wup1": w_up[1], "wdown0": w_down[0], "wdown1": w_down[1]}
    placed = {n: _cast_place(s.reshape(2, s.shape[0] // 2, s.shape[1]), sel, "cast_place_" + n)
              for n, s in shards.items()}
    ln_pack = jnp.concatenate([ln_g.reshape(4, -1), ln_b.reshape(4, -1)], axis=0)
    ln_buf = lax.dynamic_update_slice(jnp.zeros((N_CHIPS,) + ln_pack.shape, F32), ln_pack[None], (chip, 0, 0))
    small_shapes = [a_w_s.shape, a_b_s.shape, a_sgu_ln_g.shape, a_sgu_ln_b.shape, kv_b_f.shape]
    ln_shape = (4, D)
    n_small = -(-_rows_of(small_shapes + [ln_shape, ln_shape]) // 64) * 64
    hooks = _Staged(placed, sel, DM, NG, {"wa": a_w_in.shape[2], "wkv": kv_w.shape[1]}, n_small)
    weights = {"sg": a_sgu_ln_g, "sb": a_sgu_ln_b, "ws": a_w_s[0], "b_s": a_b_s[0], "bf": kv_b_f}
    got = _run_comm(_gather_weights([placed[n] for n in _GATHER_STAGES["start"]], whole=[ln_buf]),
                    "gather_weights_first")
    hooks.done("start", got[:-1], weights)
    ln_full = got[-1].transpose(1, 0, 2).reshape(8, D)
    weights["ln_g"], weights["ln_b"] = ln_full[:4], ln_full[4:]

    loss_part, grad_x, gr = _local_step(x[0], mem[0], loss_target[0], weights, hooks)
    loss = lax.psum(loss_part[0, 0], ("x", "y", "c"))

    hooks.done("end", _run_comm(hooks.plan("end", weights, gr), "reduce_chip_exchange_end"), weights)
    groups = [["wa"], ["wkv"], ["wq"], ["mkv0", "mkv1"], ["wo0", "wo1"], ["wup0", "wup1"], ["wdown0", "wdown1"],
              ["small"]]
    mine = []
    for grp in groups:
        buf = None
        for layer, n in enumerate(grp):
            buf = _chip_reduce(hooks.g4[n], hooks.recv1[n], hooks.recv2[n], sel, buf, layer, len(grp),
                               "reduce_chip_sum_" + n)
        mine.append(buf)
    red = _pair_gather(mine, "reduce_pair_gather")
    red = [a.reshape(a.shape[0], 2 * a.shape[3], a.shape[4]) for a in red]
    g_a_w_in, g_kv_w, g_b_w_q, g_mem_w_kv, g_w_o, g_w_up, g_w_down, small_mine = red
    g_kv_w = g_kv_w[0]
    g_w_down = g_w_down.transpose(0, 2, 1)
    small_all = _chip_allgather([small_mine[0]], [False], "gather_small_grads")[0].reshape(n_small, LANES)
    g_ws, g_bs, g_sg, g_sb, g_bf, g_lng, g_lnb = _unpack_rows(small_all, small_shapes + [ln_shape, ln_shape])
    dsh = D // N_CHIPS
    g_ln_g = lax.dynamic_slice_in_dim(g_lng, chip * dsh, dsh, axis=1).reshape(ln_g.shape)
    g_ln_b = lax.dynamic_slice_in_dim(g_lnb, chip * dsh, dsh, axis=1).reshape(ln_b.shape)

    def update(name, g, w, m, v):
        flat = lambda a: a.reshape(-1, a.shape[-1])
        d, mn, vn = _adamw(flat(g), flat(w), flat(m), flat(v), "adamw_" + name)
        return d.reshape(w.shape), mn.reshape(w.shape), vn.reshape(w.shape)

    upd = {
        "a_w_in": update("a_w_in", g_a_w_in, a_w_in, m_a_w_in, v_a_w_in),
        "kv_w": update("kv_w", g_kv_w, kv_w, m_kv_w, v_kv_w),
        "b_w_q": update("b_w_q", g_b_w_q, b_w_q, m_b_w_q, v_b_w_q),
        "mem_w_kv": update("mem_w_kv", g_mem_w_kv, mem_w_kv, m_mem_w_kv, v_mem_w_kv),
        "w_o": update("w_o", g_w_o, w_o, m_w_o, v_w_o),
        "w_up": update("w_up", g_w_up, w_up, m_w_up, v_w_up),
        "w_down": update("w_down", g_w_down, w_down, m_w_down, v_w_down),
    }
    tiny_g = [g_sg, g_sb, g_ws, g_bs, g_bf, g_ln_g, g_ln_b]
    tiny_w = [a_sgu_ln_g, a_sgu_ln_b, a_w_s, a_b_s, kv_b_f, ln_g, ln_b]
    tiny_m = [m_a_sgu_ln_g, m_a_sgu_ln_b, m_a_w_s, m_a_b_s, m_kv_b_f, m_ln_g, m_ln_b]
    tiny_v = [v_a_sgu_ln_g, v_a_sgu_ln_b, v_a_w_s, v_a_b_s, v_kv_b_f, v_ln_g, v_ln_b]
    tiny_shapes = [a.shape for a in tiny_w]
    n_tiny = -(-_rows_of(tiny_shapes) // 8) * 8
    td, tm_, tv = _adamw(_pack_rows(tiny_g, n_tiny), _pack_rows(tiny_w, n_tiny), _pack_rows(tiny_m, n_tiny),
                         _pack_rows(tiny_v, n_tiny), "adamw_small")
    for name, d, mn, vn in zip(["a_sgu_ln_g", "a_sgu_ln_b", "a_w_s", "a_b_s", "kv_b_f", "ln_g", "ln_b"],
                               _unpack_rows(td, tiny_shapes), _unpack_rows(tm_, tiny_shapes),
                               _unpack_rows(tv, tiny_shapes)):
        upd[name] = (d, mn, vn)

    order = ["a_w_in", "a_sgu_ln_g", "a_sgu_ln_b", "a_w_s", "a_b_s", "kv_w", "kv_b_f", "b_w_q", "mem_w_kv", "w_o",
             "ln_g", "ln_b", "w_up", "w_down"]
    grads = {"a_w_in": g_a_w_in.reshape(a_w_in.shape), "a_sgu_ln_g": g_sg.reshape(a_sgu_ln_g.shape),
             "a_sgu_ln_b": g_sb.reshape(a_sgu_ln_b.shape), "a_w_s": g_ws.reshape(a_w_s.shape),
             "a_b_s": g_bs.reshape(a_b_s.shape), "kv_w": g_kv_w, "kv_b_f": g_bf, "b_w_q": g_b_w_q.reshape(b_w_q.shape),
             "mem_w_kv": g_mem_w_kv, "w_o": g_w_o, "ln_g": g_ln_g, "ln_b": g_ln_b, "w_up": g_w_up, "w_down": g_w_down}
    return (loss, grad_x[None], *[grads[n] for n in order], *[upd[n][0] for n in order],
            *[upd[n][1] for n in order], *[upd[n][2] for n in order])
```

```python
import functools
import math

import jax
import jax.numpy as jnp
from jax import lax
from jax.experimental import pallas as pl
from jax.experimental.pallas import tpu as pltpu

F32 = jnp.float32
BF16 = jnp.bfloat16

HEAD_DIM = 64
PAIR = 2 * HEAD_DIM
CHUNK = 128
LN_EPS = 1e-5
ALPHA = 4 ** 0.25
QK_SCALE = 1.0 / math.sqrt(HEAD_DIM)
NEG = -1e30
N_CHIPS = 4
FOX_BLOCK = 256
LANES = 128
VMEM_LIMIT = 48 * 2 ** 20

ADAM_LR, ADAM_B1, ADAM_B2, ADAM_EPS, ADAM_WD, ADAM_STEP = 0.001, 0.9, 0.999, 1e-08, 0.01, 10

MESH = pl.DeviceIdType.MESH
ANY = pl.BlockSpec(memory_space=pl.ANY)


def _dot(a, b):
    return jnp.dot(a, b, preferred_element_type=F32)


def _dot_nt(a, b):
    return lax.dot_general(a, b, (((1,), (1,)), ((), ())), preferred_element_type=F32)


def _dot_tn(a, b):
    return lax.dot_general(a, b, (((0,), (0,)), ((), ())), preferred_element_type=F32)


def _ln_stats(r):
    mu = jnp.mean(r, axis=-1, keepdims=True)
    d = r - mu
    var = jnp.mean(d * d, axis=-1, keepdims=True)
    rstd = lax.rsqrt(var + LN_EPS)
    return d * rstd, rstd


def _ln_bwd(dy, xhat, rstd, g):
    dxh = dy * g
    m1 = jnp.mean(dxh, axis=-1, keepdims=True)
    m2 = jnp.mean(dxh * xhat, axis=-1, keepdims=True)
    return rstd * (dxh - m1 - xhat * m2)


_GELU_K = math.sqrt(2.0 / math.pi)


def _gelu(x):
    return 0.5 * x * (1.0 + jnp.tanh(_GELU_K * (x + 0.044715 * x * x * x)))


def _gelu_grad(x):
    t = jnp.tanh(_GELU_K * (x + 0.044715 * x * x * x))
    return 0.5 * (1.0 + t) + 0.5 * x * (1.0 - t * t) * _GELU_K * (1.0 + 3 * 0.044715 * x * x)


def _lane_mask(e):
    lane = lax.broadcasted_iota(jnp.int32, (1, PAIR), 1)
    return (lane >= HEAD_DIM * e) & (lane < HEAD_DIM * (e + 1))


def _first_head():
    return lax.broadcasted_iota(jnp.int32, (1, PAIR), 1) < HEAD_DIM


def _mem_probs(qh, mkp):
    sc = _dot_nt(qh, mkp) * QK_SCALE
    ex = jnp.exp(sc - jnp.max(sc, axis=-1, keepdims=True))
    return ex / jnp.sum(ex, axis=-1, keepdims=True)


def _mem_attn_fwd(qm_b, mkv_ref, dq_dim):
    outs = []
    for mp in range(dq_dim // PAIR):
        qp = qm_b[:, mp * PAIR:(mp + 1) * PAIR]
        mkp = mkv_ref[:, mp * PAIR:(mp + 1) * PAIR]
        mvp = mkv_ref[:, dq_dim + mp * PAIR:dq_dim + (mp + 1) * PAIR]
        heads = []
        for e in (0, 1):
            qh = jnp.where(_lane_mask(e), qp, jnp.zeros_like(qp))
            p = _mem_probs(qh, mkp)
            heads.append(_dot(p.astype(BF16), mvp))
        outs.append(jnp.where(_first_head(), heads[0], heads[1]))
    return outs


def _mem_attn_bwd(qm_b, mkv_ref, dmo_pairs, dmkv_ref, dq_dim):
    dqs = []
    for mp in range(dq_dim // PAIR):
        ks = slice(mp * PAIR, (mp + 1) * PAIR)
        vs = slice(dq_dim + mp * PAIR, dq_dim + (mp + 1) * PAIR)
        qp = qm_b[:, ks]
        mkp = mkv_ref[:, ks]
        mvp = mkv_ref[:, vs]
        dmo_b = dmo_pairs[mp].astype(BF16)
        dq = None
        dmk = None
        dmv = None
        for e in (0, 1):
            hm = _lane_mask(e)
            qh = jnp.where(hm, qp, jnp.zeros_like(qp))
            p = _mem_probs(qh, mkp)
            doh = jnp.where(hm, dmo_b, jnp.zeros_like(dmo_b))
            dp = _dot_nt(doh, mvp)
            ds = p * (dp - jnp.sum(dp * p, axis=-1, keepdims=True))
            dsb = (ds * QK_SCALE).astype(BF16)
            kh = jnp.where(hm, mkp, jnp.zeros_like(mkp))
            dq_e = _dot(dsb, kh)
            dmk_e = _dot_tn(dsb, qh)
            dmv_e = _dot_tn(p.astype(BF16), doh)
            dq = dq_e if dq is None else dq + dq_e
            dmk = dmk_e if dmk is None else dmk + dmk_e
            dmv = dmv_e if dmv is None else dmv + dmv_e
        dmkv_ref[:, ks] += dmk
        dmkv_ref[:, vs] += dmv
        dqs.append(dq)
    return dqs


def _params(n_axes=1):
    return pltpu.CompilerParams(dimension_semantics=("arbitrary",) * n_axes, vmem_limit_bytes=VMEM_LIMIT)


def _full(shape):
    return pl.BlockSpec(shape, lambda *_: (0,) * len(shape))


def _rows(tm, cols):
    return pl.BlockSpec((tm, cols), lambda i: (i, 0))


def _cols(rows, tm):
    return pl.BlockSpec((rows, tm), lambda i: (0, i))


def _pick(n, pref):
    if n <= pref:
        return n
    best = LANES
    for t in range(LANES, pref + 1, LANES):
        if n % t == 0:
            best = t
    return best if 2 * best >= pref or n > 2 * pref else n


class _Comm:
    def __init__(self, ins, out_shapes, aliases, sem_counts, start, finish):
        self.ins, self.out_shapes, self.aliases, self.sem_counts = list(ins), list(out_shapes), dict(aliases), sem_counts
        self.start, self.finish = start, finish


def _pcall(body, *, name, grid, in_specs, out_specs, out_shape, args, scratch_shapes=(), comm=None):
    n_in, n_out, n_scr = len(in_specs), len(out_shape), len(scratch_shapes)
    if comm is None:
        res = pl.pallas_call(body, name=name, grid=grid, out_shape=tuple(out_shape), in_specs=list(in_specs),
                             out_specs=tuple(out_specs), scratch_shapes=list(scratch_shapes),
                             compiler_params=_params())(*args)
        return tuple(res), ()
    nci, nco = len(comm.ins), len(comm.out_shapes)
    last = grid[0] - 1

    def wrapped(*refs):
        ins, refs = refs[:n_in], refs[n_in:]
        cins, refs = refs[:nci], refs[nci:]
        outs, refs = refs[:n_out], refs[n_out:]
        couts, refs = refs[:nco], refs[nco:]
        scr, sems = refs[:n_scr], refs[n_scr:]

        @pl.when(pl.program_id(0) == 0)
        def _():
            comm.start(cins, couts, sems)

        body(*ins, *outs, *scr)

        @pl.when(pl.program_id(0) == last)
        def _():
            comm.finish(cins, couts, sems)

    res = pl.pallas_call(
        wrapped, name=name, grid=grid, out_shape=tuple(out_shape) + tuple(comm.out_shapes),
        in_specs=list(in_specs) + [ANY] * nci, out_specs=tuple(out_specs) + (ANY,) * nco,
        input_output_aliases={n_in + a: n_out + b for a, b in comm.aliases.items()},
        scratch_shapes=list(scratch_shapes) + [pltpu.SemaphoreType.DMA((k,)) for k in comm.sem_counts],
        compiler_params=pltpu.CompilerParams(dimension_semantics=("arbitrary",), vmem_limit_bytes=VMEM_LIMIT,
                                             has_side_effects=True),
    )(*args, *comm.ins)
    return tuple(res[:n_out]), tuple(res[n_out:])


def _run_comm(comm, name):
    nci, nco = len(comm.ins), len(comm.out_shapes)

    def body(*refs):
        cins, couts, sems = refs[:nci], refs[nci:nci + nco], refs[nci + nco:]
        comm.start(cins, couts, sems)
        comm.finish(cins, couts, sems)

    return pl.pallas_call(
        body, name=name, out_shape=tuple(comm.out_shapes), in_specs=[ANY] * nci, out_specs=(ANY,) * nco,
        input_output_aliases=comm.aliases, scratch_shapes=[pltpu.SemaphoreType.DMA((k,)) for k in comm.sem_counts],
        compiler_params=pltpu.CompilerParams(has_side_effects=True),
    )(*comm.ins)


def _mm_tn(a, b, name, out_split=1):
    R, M = a.shape
    _, N = b.shape
    bm, br = _pick(M, 512), _pick(R, 512)
    ncol = N // out_split
    bn = _pick(ncol, 1024)
    per = ncol // bn

    def body(a_ref, b_ref, o_ref):
        @pl.when(pl.program_id(2) == 0)
        def _():
            o_ref[...] = jnp.zeros_like(o_ref)

        o_ref[...] += _dot_tn(a_ref[...], b_ref[...])

    if out_split == 1:
        out_shape = jax.ShapeDtypeStruct((M, N), F32)
        out_spec = pl.BlockSpec((bm, bn), lambda i, j, r: (i, j))
    else:
        out_shape = jax.ShapeDtypeStruct((out_split, M, ncol), F32)
        out_spec = pl.BlockSpec((None, bm, bn), lambda i, j, r: (j // per, i, j % per))
    return pl.pallas_call(
        body, name=name, out_shape=out_shape, grid=(M // bm, N // bn, R // br),
        in_specs=[pl.BlockSpec((br, bm), lambda i, j, r: (r, i)), pl.BlockSpec((br, bn), lambda i, j, r: (r, j))],
        out_specs=out_spec,
        compiler_params=pltpu.CompilerParams(dimension_semantics=("parallel", "parallel", "arbitrary"),
                                             vmem_limit_bytes=VMEM_LIMIT),
    )(a, b)


def _mm_nn(at, b, name, out_split=1):
    M, R = at.shape
    _, N = b.shape
    bm, br = _pick(M, 1024), _pick(R, 512)
    ncol = N // out_split
    bn = _pick(ncol, 1024)
    per = ncol // bn

    def body(a_ref, b_ref, o_ref):
        @pl.when(pl.program_id(2) == 0)
        def _():
            o_ref[...] = jnp.zeros_like(o_ref)

        o_ref[...] += _dot(a_ref[...], b_ref[...])

    if out_split == 1:
        out_shape = jax.ShapeDtypeStruct((M, N), F32)
        out_spec = pl.BlockSpec((bm, bn), lambda i, j, r: (i, j))
    else:
        out_shape = jax.ShapeDtypeStruct((out_split, M, ncol), F32)
        out_spec = pl.BlockSpec((None, bm, bn), lambda i, j, r: (j // per, i, j % per))
    return pl.pallas_call(
        body, name=name, out_shape=out_shape, grid=(M // bm, N // bn, R // br),
        in_specs=[pl.BlockSpec((bm, br), lambda i, j, r: (i, r)), pl.BlockSpec((br, bn), lambda i, j, r: (r, j))],
        out_specs=out_spec,
        compiler_params=pltpu.CompilerParams(dimension_semantics=("parallel", "parallel", "arbitrary"),
                                             vmem_limit_bytes=VMEM_LIMIT),
    )(at, b)


def _mm_small(a, b, name):
    def body(a_ref, b_ref, o_ref):
        o_ref[...] = _dot(a_ref[...], b_ref[...]).astype(BF16)

    return pl.pallas_call(body, name=name, out_shape=jax.ShapeDtypeStruct((a.shape[0], b.shape[1]), BF16),
                          compiler_params=pltpu.CompilerParams(vmem_limit_bytes=VMEM_LIMIT))(a, b)


def _mlp_fwd(x, wup4, wdown4, g, b, name, comm=None):
    S, D = x.shape
    NC, _, FC = wup4.shape
    tm = min(256, S)

    def body(x_ref, wup_ref, wdown_ref, g_ref, b_ref, xo_ref, xb_ref, xbt_ref, r_ref, h_ref):
        xv = x_ref[...]
        xb = xv.astype(BF16)
        y = jnp.zeros((tm, D), F32)
        for c in range(NC):
            a = jnp.maximum(_dot(xb, wup_ref[c]), 0.0)
            hb = (a * a).astype(BF16)
            h_ref[:, c * FC:(c + 1) * FC] = hb
            y = y + _dot(hb, wdown_ref[c])
        r = ALPHA * xv + y
        xhat, _ = _ln_stats(r)
        xo = xhat * g_ref[...] + b_ref[...]
        xo_ref[...] = xo
        xb_ref[...] = xo.astype(BF16)
        xbt_ref[...] = xo.T.astype(BF16)
        r_ref[...] = r

    return _pcall(
        body, name=name, grid=(S // tm,), comm=comm,
        out_shape=(jax.ShapeDtypeStruct((S, D), F32), jax.ShapeDtypeStruct((S, D), BF16),
                   jax.ShapeDtypeStruct((D, S), BF16), jax.ShapeDtypeStruct((S, D), F32),
                   jax.ShapeDtypeStruct((S, NC * FC), BF16)),
        in_specs=[_rows(tm, D), _full(wup4.shape), _full(wdown4.shape), _full((1, D)), _full((1, D))],
        out_specs=(_rows(tm, D), _rows(tm, D), _cols(D, tm), _rows(tm, D), _rows(tm, NC * FC)),
        args=(x, wup4, wdown4, g, b))


def _mlp_bwd(dxo, r, h, wup4, wdown4, g, name, comm=None):
    S, D = r.shape
    NC, _, FC = wup4.shape
    tm = min(256, S)

    def body(dxo_ref, r_ref, h_ref, wup_ref, wdown_ref, g_ref, dx_ref, drbt_ref, dpre_ref, dgb_ref):
        @pl.when(pl.program_id(0) == 0)
        def _():
            dgb_ref[...] = jnp.zeros_like(dgb_ref)

        xhat, rstd = _ln_stats(r_ref[...])
        dy = dxo_ref[...]
        dgb_ref[0:1, :] += jnp.sum(dy * xhat, axis=0, keepdims=True)
        dgb_ref[1:2, :] += jnp.sum(dy, axis=0, keepdims=True)
        dr = _ln_bwd(dy, xhat, rstd, g_ref[...])
        drb = dr.astype(BF16)
        drbt_ref[...] = dr.T.astype(BF16)
        dx = ALPHA * dr
        for c in range(NC):
            dh = _dot_nt(drb, wdown_ref[c])
            a = jnp.sqrt(h_ref[:, c * FC:(c + 1) * FC].astype(F32))
            dpre = (2.0 * a * dh).astype(BF16)
            dpre_ref[:, c * FC:(c + 1) * FC] = dpre
            dx = dx + _dot_nt(dpre, wup_ref[c])
        dx_ref[...] = dx

    return _pcall(
        body, name=name, grid=(S // tm,), comm=comm,
        out_shape=(jax.ShapeDtypeStruct((S, D), F32), jax.ShapeDtypeStruct((D, S), BF16),
                   jax.ShapeDtypeStruct((S, NC * FC), BF16), jax.ShapeDtypeStruct((2, D), F32)),
        in_specs=[_rows(tm, D), _rows(tm, D), _rows(tm, NC * FC), _full(wup4.shape), _full(wdown4.shape),
                  _full((1, D))],
        out_specs=(_rows(tm, D), _cols(D, tm), _rows(tm, NC * FC), _full((2, D))),
        args=(dxo, r, h, wup4, wdown4, g))


def _loss_grad(xo, target, name):
    S, D = xo.shape
    tm = min(512, S)

    def body(x_ref, t_ref, d_ref, l_ref):
        @pl.when(pl.program_id(0) == 0)
        def _():
            l_ref[...] = jnp.zeros_like(l_ref)

        err = x_ref[...] - t_ref[...]
        d_ref[...] = err * (1.0 / D)
        l_ref[...] += (0.5 / D) * jnp.sum(err * err)

    return pl.pallas_call(
        body, name=name, grid=(S // tm,),
        out_shape=(jax.ShapeDtypeStruct((S, D), F32), jax.ShapeDtypeStruct((8, LANES), F32)),
        in_specs=[_rows(tm, D), _rows(tm, D)], out_specs=(_rows(tm, D), _full((8, LANES))),
        compiler_params=_params(),
    )(xo, target)


def _tail_fwd_store(xv, cat_ref, wo_ref, g_ref, b_ref, xo_ref, xbt_ref, r_ref, catt_ref):
    cat = cat_ref[...]
    y = _dot(cat, wo_ref[...])
    r = ALPHA * xv + y
    xhat, _ = _ln_stats(r)
    xo = xhat * g_ref[...] + b_ref[...]
    xo_ref[...] = xo
    xbt_ref[...] = xo.T.astype(BF16)
    catt_ref[...] = cat.astype(F32).T.astype(BF16)
    r_ref[...] = r


def _tail_bwd_head(dxo_ref, r_ref, g_ref, wo_ref, dgb_ref, drb_ref):
    xhat, rstd = _ln_stats(r_ref[...])
    dy = dxo_ref[...]
    dgb_ref[0:1, :] += jnp.sum(dy * xhat, axis=0, keepdims=True)
    dgb_ref[1:2, :] += jnp.sum(dy, axis=0, keepdims=True)
    dr = _ln_bwd(dy, xhat, rstd, g_ref[...])
    drb = dr.astype(BF16)
    drb_ref[...] = drb
    return dr, _dot_nt(drb, wo_ref[...])


def _causal(shape):
    return lax.broadcasted_iota(jnp.int32, shape, 1) <= lax.broadcasted_iota(jnp.int32, shape, 0)


def _sgu_mixed(ws_ref, bst_ref, gp, vpair):
    tril = _causal((CHUNK, CHUNK))
    w0 = jnp.where(tril, ws_ref[2 * gp], 0.0).astype(BF16)
    w1 = jnp.where(tril, ws_ref[2 * gp + 1], 0.0).astype(BF16)
    m0 = _dot(w0, vpair) + bst_ref[:, 2 * gp:2 * gp + 1]
    m1 = _dot(w1, vpair) + bst_ref[:, 2 * gp + 1:2 * gp + 2]
    return jnp.where(_first_head(), m0, m1), w0, w1


def _mixer_a_fwd(x, wa, sg, sb, ws, bst, mkv, wo, g, b, name, comm=None):
    S, D = x.shape
    DQ = mkv.shape[1] // 2
    DM = D - DQ
    NP = DM // PAIR
    tm = min(512, S)

    def body(x_ref, wa_ref, sg_ref, sb_ref, ws_ref, bst_ref, mkv_ref, wo_ref, g_ref, b_ref,
             xo_ref, xbt_ref, r_ref, catt_ref, z_s, zv_s, cat_ref):
        xv = x_ref[...]
        z_s[...] = _dot(xv.astype(BF16), wa_ref[...])
        vhat, _ = _ln_stats(_gelu(z_s[:, DM:2 * DM]))
        zv_s[...] = (vhat * sg_ref[...] + sb_ref[...]).astype(BF16)
        for c in range(tm // CHUNK):
            rs = slice(c * CHUNK, (c + 1) * CHUNK)
            for gp in range(NP):
                ls = slice(gp * PAIR, (gp + 1) * PAIR)
                mixed, _, _ = _sgu_mixed(ws_ref, bst_ref, gp, zv_s[rs, ls])
                cat_ref[rs, ls] = (_gelu(z_s[rs, ls]) * mixed).astype(BF16)
        mo = _mem_attn_fwd(z_s[:, 2 * DM:].astype(BF16), mkv_ref, DQ)
        for mp in range(DQ // PAIR):
            cat_ref[:, DM + mp * PAIR:DM + (mp + 1) * PAIR] = mo[mp].astype(BF16)
        _tail_fwd_store(xv, cat_ref, wo_ref, g_ref, b_ref, xo_ref, xbt_ref, r_ref, catt_ref)

    return _pcall(
        body, name=name, grid=(S // tm,), comm=comm,
        out_shape=(jax.ShapeDtypeStruct((S, D), F32), jax.ShapeDtypeStruct((D, S), BF16),
                   jax.ShapeDtypeStruct((S, D), F32), jax.ShapeDtypeStruct((D, S), BF16)),
        in_specs=[_rows(tm, D), _full(wa.shape), _full(sg.shape), _full(sb.shape), _full(ws.shape), _full(bst.shape),
                  _full(mkv.shape), _full(wo.shape), _full((1, D)), _full((1, D))],
        out_specs=(_rows(tm, D), _cols(D, tm), _rows(tm, D), _cols(D, tm)),
        scratch_shapes=[pltpu.VMEM((tm, 2 * DM + DQ), F32), pltpu.VMEM((tm, DM), BF16), pltpu.VMEM((tm, D), BF16)],
        args=(x, wa, sg, sb, ws, bst, mkv, wo, g, b))


def _mixer_a_bwd(x, dxo, r, wa, sg, sb, ws, bst, mkv, wo, g, name, comm=None):
    S, D = x.shape
    DQ = mkv.shape[1] // 2
    DM = D - DQ
    NP = DM // PAIR
    NG = DM // HEAD_DIM
    tm = min(256, S)

    def body(x_ref, dxo_ref, r_ref, wa_ref, sg_ref, sb_ref, ws_ref, bst_ref, mkv_ref, wo_ref, g_ref,
             dx_ref, dz_ref, drb_ref, dgb_ref, dsgb_ref, dws_ref, dbst_ref, dmkv_ref, z_s, zv_s, dzv_s):
        @pl.when(pl.program_id(0) == 0)
        def _():
            dgb_ref[...] = jnp.zeros_like(dgb_ref)
            dsgb_ref[...] = jnp.zeros_like(dsgb_ref)
            dws_ref[...] = jnp.zeros_like(dws_ref)
            dbst_ref[...] = jnp.zeros_like(dbst_ref)
            dmkv_ref[...] = jnp.zeros_like(dmkv_ref)

        dr, dcat = _tail_bwd_head(dxo_ref, r_ref, g_ref, wo_ref, dgb_ref, drb_ref)
        z_s[...] = _dot(x_ref[...].astype(BF16), wa_ref[...])
        vhat, vrstd = _ln_stats(_gelu(z_s[:, DM:2 * DM]))
        zv_s[...] = (vhat * sg_ref[...] + sb_ref[...]).astype(BF16)
        tril = _causal((CHUNK, CHUNK))
        lane = lax.broadcasted_iota(jnp.int32, (1, LANES), 1)
        for c in range(tm // CHUNK):
            rs = slice(c * CHUNK, (c + 1) * CHUNK)
            for gp in range(NP):
                ls = slice(gp * PAIR, (gp + 1) * PAIR)
                vpair = zv_s[rs, ls]
                mixed, w0, w1 = _sgu_mixed(ws_ref, bst_ref, gp, vpair)
                u_pre = z_s[rs, ls]
                dmix = dcat[rs, ls]
                dz_ref[rs, ls] = (dmix * mixed * _gelu_grad(u_pre)).astype(BF16)
                dmixed = dmix * _gelu(u_pre)
                first = _first_head()
                d0 = jnp.where(first, dmixed, 0.0)
                d1 = jnp.where(first, 0.0, dmixed)
                d0b = d0.astype(BF16)
                d1b = d1.astype(BF16)
                dzv_s[rs, ls] = _dot_tn(w0, d0b) + _dot_tn(w1, d1b)
                dws_ref[2 * gp] += jnp.where(tril, _dot_nt(d0b, vpair), 0.0)
                dws_ref[2 * gp + 1] += jnp.where(tril, _dot_nt(d1b, vpair), 0.0)
                dbst_ref[...] += (jnp.where(lane == 2 * gp, jnp.sum(d0, axis=-1, keepdims=True), 0.0)
                                  + jnp.where(lane == 2 * gp + 1, jnp.sum(d1, axis=-1, keepdims=True), 0.0))
        dzv = dzv_s[...]
        dsgb_ref[0:1, :] += jnp.sum(dzv * vhat, axis=0, keepdims=True)
        dsgb_ref[1:2, :] += jnp.sum(dzv, axis=0, keepdims=True)
        dgv = _ln_bwd(dzv, vhat, vrstd, sg_ref[...])
        dz_ref[:, DM:2 * DM] = (dgv * _gelu_grad(z_s[:, DM:2 * DM])).astype(BF16)
        dmo = [dcat[:, DM + mp * PAIR:DM + (mp + 1) * PAIR] for mp in range(DQ // PAIR)]
        dqm = _mem_attn_bwd(z_s[:, 2 * DM:].astype(BF16), mkv_ref, dmo, dmkv_ref, DQ)
        for mp in range(DQ // PAIR):
            dz_ref[:, 2 * DM + mp * PAIR:2 * DM + (mp + 1) * PAIR] = dqm[mp].astype(BF16)
        dx_ref[...] = ALPHA * dr + _dot_nt(dz_ref[...], wa_ref[...])

    ZW = 2 * DM + DQ
    M = mkv.shape[0]
    return _pcall(
        body, name=name, grid=(S // tm,), comm=comm,
        out_shape=(jax.ShapeDtypeStruct((S, D), F32), jax.ShapeDtypeStruct((S, ZW), BF16),
                   jax.ShapeDtypeStruct((S, D), BF16), jax.ShapeDtypeStruct((2, D), F32),
                   jax.ShapeDtypeStruct((2, DM), F32), jax.ShapeDtypeStruct((NG, CHUNK, CHUNK), F32),
                   jax.ShapeDtypeStruct((CHUNK, LANES), F32), jax.ShapeDtypeStruct((M, 2 * DQ), F32)),
        in_specs=[_rows(tm, D), _rows(tm, D), _rows(tm, D), _full(wa.shape), _full(sg.shape), _full(sb.shape),
                  _full(ws.shape), _full(bst.shape), _full(mkv.shape), _full(wo.shape), _full((1, D))],
        out_specs=(_rows(tm, D), _rows(tm, ZW), _rows(tm, D), _full((2, D)), _full((2, DM)),
                   _full((NG, CHUNK, CHUNK)), _full((CHUNK, LANES)), _full((M, 2 * DQ))),
        scratch_shapes=[pltpu.VMEM((tm, ZW), F32), pltpu.VMEM((tm, DM), BF16), pltpu.VMEM((tm, DM), F32)],
        args=(x, dxo, r, wa, sg, sb, ws, bst, mkv, wo, g))


def _proj_b_fwd(xb, wq, wkv, dm, name):
    S, D = xb.shape
    DQ = D - dm
    tm = min(512, S)

    def body(x_ref, wq_ref, wkv_ref, q_ref, qm_ref, k_ref, v_ref, fl_ref):
        xv = x_ref[...]
        z = _dot(xv, wq_ref[...])
        q_ref[...] = z[:, :dm].astype(BF16)
        qm_ref[...] = z[:, dm:].astype(BF16)
        kvf = _dot(xv, wkv_ref[...])
        k_ref[...] = kvf[:, :dm].astype(BF16)
        v_ref[...] = kvf[:, dm:2 * dm].astype(BF16)
        fl_ref[...] = kvf[:, 2 * dm:]

    return pl.pallas_call(
        body, name=name, grid=(S // tm,),
        out_shape=(jax.ShapeDtypeStruct((S, dm), BF16), jax.ShapeDtypeStruct((S, DQ), BF16),
                   jax.ShapeDtypeStruct((S, dm), BF16), jax.ShapeDtypeStruct((S, dm), BF16),
                   jax.ShapeDtypeStruct((S, LANES), F32)),
        in_specs=[_rows(tm, D), _full(wq.shape), _full(wkv.shape)],
        out_specs=(_rows(tm, dm), _rows(tm, DQ), _rows(tm, dm), _rows(tm, dm), _rows(tm, LANES)),
        compiler_params=_params(),
    )(xb, wq, wkv)


def _proj_b_bwd(dr, dqz, dkvf, wq, wkv, name):
    S, D = dr.shape
    tm = min(512, S)

    def body(dr_ref, dqz_ref, dkvf_ref, wq_ref, wkv_ref, dx_ref):
        dx_ref[...] = (ALPHA * dr_ref[...] + _dot_nt(dqz_ref[...], wq_ref[...])
                       + _dot_nt(dkvf_ref[...], wkv_ref[...]))

    return pl.pallas_call(
        body, name=name, grid=(S // tm,), out_shape=jax.ShapeDtypeStruct((S, D), F32),
        in_specs=[_rows(tm, D), _rows(tm, dqz.shape[1]), _rows(tm, dkvf.shape[1]), _full(wq.shape), _full(wkv.shape)],
        out_specs=_rows(tm, D), compiler_params=_params(),
    )(dr, dqz, dkvf, wq, wkv)


def _split3(v):
    hi = v.astype(BF16)
    r1 = v - hi.astype(F32)
    mid = r1.astype(BF16)
    lo = (r1 - mid.astype(F32)).astype(BF16)
    return hi, mid, lo


def _tri_sum(v, tri_b):
    hi, mid, lo = _split3(v)
    return _dot(hi, tri_b) + _dot(mid, tri_b) + _dot(lo, tri_b)


def _log_sigmoid(x):
    return jnp.minimum(x, 0.0) - jnp.log(1.0 + jnp.exp(-jnp.abs(x)))


def _forget_cumsum(flt, bf, name):
    H, S = flt.shape
    nchunk = S // LANES

    def body(fl_ref, bf_ref, p_ref):
        upper = (lax.broadcasted_iota(jnp.int32, (LANES, LANES), 0)
                 <= lax.broadcasted_iota(jnp.int32, (LANES, LANES), 1)).astype(BF16)

        def step(n, carry):
            s0 = pl.multiple_of(n * LANES, LANES)
            lf = _log_sigmoid(fl_ref[:, pl.ds(s0, LANES)] + bf_ref[...])
            cs = _tri_sum(lf, upper) + carry
            for i, piece in enumerate(_split3(-cs)):
                p_ref[i, :, pl.ds(s0, LANES)] = piece.astype(F32)
            return cs[:, LANES - 1:LANES]

        lax.fori_loop(0, nchunk, step, jnp.zeros((H, 1), F32))

    return pl.pallas_call(body, name=name, out_shape=jax.ShapeDtypeStruct((3, H, S), F32),
                          compiler_params=pltpu.CompilerParams(vmem_limit_bytes=VMEM_LIMIT))(flt, bf)


def _forget_cumsum_bwd(dct, flt, bf, name):
    H, S = flt.shape
    nchunk = S // LANES

    def body(dc_ref, fl_ref, bf_ref, dfl_ref, dbf_ref):
        lower = (lax.broadcasted_iota(jnp.int32, (LANES, LANES), 0)
                 >= lax.broadcasted_iota(jnp.int32, (LANES, LANES), 1)).astype(BF16)

        def step(n, carry):
            tail, tot = carry
            s0 = pl.multiple_of((nchunk - 1 - n) * LANES, LANES)
            suffix = _tri_sum(dc_ref[:, pl.ds(s0, LANES)], lower) + tail
            xv = fl_ref[:, pl.ds(s0, LANES)] + bf_ref[...]
            dfl = suffix * (1.0 / (1.0 + jnp.exp(xv)))
            dfl_ref[:, pl.ds(s0, LANES)] = dfl
            return suffix[:, 0:1], tot + jnp.sum(dfl, axis=-1, keepdims=True)

        _, tot = lax.fori_loop(0, nchunk, step, (jnp.zeros((H, 1), F32), jnp.zeros((H, 1), F32)))
        dbf_ref[...] = jnp.broadcast_to(tot, (H, LANES))

    return pl.pallas_call(body, name=name,
                          out_shape=(jax.ShapeDtypeStruct((H, S), F32), jax.ShapeDtypeStruct((H, LANES), F32)),
                          compiler_params=pltpu.CompilerParams(vmem_limit_bytes=VMEM_LIMIT))(dct, flt, bf)


N_BIAS = 3


def _bias_ones():
    lane = lax.broadcasted_iota(jnp.int32, (1, PAIR), 1)
    return ((lane & (HEAD_DIM - 1)) < N_BIAS).astype(BF16)


def _fox_fwd(q, k, v, ca, name, comm=None):
    S, DM = q.shape
    NP = DM // PAIR
    T = min(FOX_BLOCK, S)
    nq = S // T

    def body(q_ref, k_ref, v_ref, ca_ref, o_ref, lse_ref):
        tril = _causal((T, T))
        heads = (_lane_mask(0), _lane_mask(1))
        ones3 = _bias_ones()

        def step(carry, qaug, kaug, vaug, masked):
            m, acc = carry
            s = _dot_nt(qaug, kaug)
            if masked:
                s = jnp.where(tril, s, NEG)
            m_new = jnp.maximum(m, jnp.max(s, axis=-1, keepdims=True))
            p = jnp.exp(s - m_new).astype(BF16)
            return m_new, jnp.exp(m - m_new) * acc + _dot(p, vaug)

        def kv_block(kj):
            s0 = pl.multiple_of(kj * T, T)
            kb, vb, cab = k_ref[pl.ds(s0, T), :], v_ref[pl.ds(s0, T), :], ca_ref[pl.ds(s0, T), :]
            return ([jnp.where(heads[e], kb, cab) for e in (0, 1)],
                    [jnp.where(heads[e], vb, jnp.ones_like(vb)) for e in (0, 1)])

        def q_pair(a, _):
            t0s = [pl.multiple_of((2 * a + r) * T, T) for r in (0, 1)]
            qaug = {}
            for r in (0, 1):
                qb = q_ref[pl.ds(t0s[r], T), :] * QK_SCALE
                for e in (0, 1):
                    qaug[r, e] = jnp.where(heads[e], qb, ones3)
            chains = [(r, e) for r in (0, 1) for e in (0, 1)]

            def full(i, carries):
                carries = list(carries)
                for kk in (0, 1):
                    kaug, vaug = kv_block(2 * i + kk)
                    scores = [_dot_nt(qaug[r, e], kaug[e]) for r, e in chains]
                    for n, (r, e) in enumerate(chains):
                        m, acc = carries[n]
                        m_new = jnp.maximum(m, jnp.max(scores[n], axis=-1, keepdims=True))
                        p = jnp.exp(scores[n] - m_new).astype(BF16)
                        carries[n] = (m_new, jnp.exp(m - m_new) * acc + _dot(p, vaug[e]))
                return tuple(carries)

            init = (jnp.full((T, 1), NEG, F32), jnp.zeros((T, PAIR), F32))
            carries = list(lax.fori_loop(0, a, full, (init,) * 4))
            kaug, vaug = kv_block(2 * a)
            for i, (r, e) in enumerate(chains):
                carries[i] = step(carries[i], qaug[r, e], kaug[e], vaug[e], r == 0)
            kaug, vaug = kv_block(2 * a + 1)
            for i, (r, e) in enumerate(chains):
                if r == 1:
                    carries[i] = step(carries[i], qaug[r, e], kaug[e], vaug[e], True)
            first = _first_head()
            for r in (0, 1):
                (m0, acc0), (m1, acc1) = carries[2 * r], carries[2 * r + 1]
                l0, l1 = acc0[:, HEAD_DIM:HEAD_DIM + 1], acc1[:, 0:1]
                o_ref[pl.ds(t0s[r], T), :] = jnp.where(first, acc0 / l0, acc1 / l1).astype(BF16)
                lse_t = jnp.where(first, m0 + jnp.log(l0), m1 + jnp.log(l1)).T
                lse_ref[0:1, pl.ds(t0s[r], T)] = lse_t[0:1, :]
                lse_ref[1:2, pl.ds(t0s[r], T)] = lse_t[HEAD_DIM:HEAD_DIM + 1, :]
            return 0

        lax.fori_loop(0, nq // 2, q_pair, 0)

    col = pl.BlockSpec((S, PAIR), lambda hp: (0, hp))
    return _pcall(
        body, name=name, grid=(NP,), comm=comm,
        out_shape=(jax.ShapeDtypeStruct((S, DM), BF16), jax.ShapeDtypeStruct((NP, 2, S), F32)),
        in_specs=[col, col, col, col], out_specs=(col, pl.BlockSpec((None, 2, S), lambda hp: (hp, 0, 0))),
        args=(q, k, v, ca))


def _fox_bwd(q, k, v, o, do, ca, lse, name, comm=None):
    S, DM = q.shape
    NP = DM // PAIR
    T = min(FOX_BLOCK, S)
    nq = S // T

    def body(q_ref, k_ref, v_ref, o_ref, do_ref, ca_ref, lse_ref, dq_ref, dk_ref, dv_ref, dc_ref,
             dq_acc, dcol_acc, dk_acc, dv_acc, dd_s):
        keep = lax.broadcasted_iota(jnp.int32, (T, T), 0) <= lax.broadcasted_iota(jnp.int32, (T, T), 1)
        lane = lax.broadcasted_iota(jnp.int32, (1, LANES), 1)
        heads = (_lane_mask(0), _lane_mask(1))
        ones3 = _bias_ones()
        dq_acc[...] = jnp.zeros_like(dq_acc)
        dcol_acc[...] = jnp.zeros_like(dcol_acc)
        dc_ref[...] = jnp.zeros_like(dc_ref)

        def row_dots(qi, _):
            t0 = pl.multiple_of(qi * T, T)
            prod = do_ref[pl.ds(t0, T), :].astype(F32) * o_ref[pl.ds(t0, T), :].astype(F32)
            cols = [jnp.sum(jnp.where(heads[e], prod, 0.0), axis=-1, keepdims=True) for e in (0, 1)]
            tile = jnp.where(lane == 0, cols[0], jnp.where(lane == 1, cols[1], 0.0))
            dd_s[:, pl.ds(t0, T)] = tile.T[0:8, :]
            return 0

        lax.fori_loop(0, nq, row_dots, 0)

        def kv_pair(b, _):
            dk_acc[...] = jnp.zeros_like(dk_acc)
            dv_acc[...] = jnp.zeros_like(dv_acc)
            s0s = [pl.multiple_of((2 * b + jj) * T, T) for jj in (0, 1)]
            kaug, ks, vbs = {}, {}, []
            for jj in (0, 1):
                kb = k_ref[pl.ds(s0s[jj], T), :]
                cab = ca_ref[pl.ds(s0s[jj], T), :]
                vbs.append(v_ref[pl.ds(s0s[jj], T), :])
                for e in (0, 1):
                    kaug[jj, e] = jnp.where(heads[e], kb, cab)
                    ks[jj, e] = jnp.where(heads[e], kb, jnp.zeros_like(kb)) * QK_SCALE

            def q_step(qi, blocks):
                t0 = pl.multiple_of(qi * T, T)
                qs = q_ref[pl.ds(t0, T), :] * QK_SCALE
                dob = do_ref[pl.ds(t0, T), :]
                doh = [jnp.where(heads[e], dob, jnp.zeros_like(dob)) for e in (0, 1)]
                qh = [jnp.where(heads[e], qs, jnp.zeros_like(qs)) for e in (0, 1)]
                chains = [(e, jj, masked) for e in (0, 1) for jj, masked in blocks]
                scores = [_dot_nt(kaug[jj, e], jnp.where(heads[e], qs, ones3)) for e, jj, _ in chains]
                dps = [_dot_nt(vbs[jj], doh[e]) for e, jj, _ in chains]
                dq = jnp.zeros((T, PAIR), F32)
                drow = [jnp.zeros((1, T), F32), jnp.zeros((1, T), F32)]
                for n, (e, jj, masked) in enumerate(chains):
                    st = jnp.where(keep, scores[n], NEG) if masked else scores[n]
                    pt = jnp.exp(st - lse_ref[e:e + 1, pl.ds(t0, T)])
                    dv_acc[jj] += _dot(pt.astype(BF16), doh[e])
                    dst = pt * (dps[n] - dd_s[e:e + 1, pl.ds(t0, T)])
                    dsb = dst.astype(BF16)
                    dk_acc[jj] += _dot(dsb, qh[e])
                    dq = dq + _dot_tn(dsb, ks[jj, e])
                    dcol_acc[pl.ds(s0s[jj], T), :] -= jnp.where(lane == e, jnp.sum(dst, axis=-1, keepdims=True), 0.0)
                    drow[e] = drow[e] + jnp.sum(dst, axis=0, keepdims=True)
                for e in (0, 1):
                    dc_ref[e:e + 1, pl.ds(t0, T)] += drow[e]
                dq_acc[pl.ds(t0, T), :] += dq

            q_step(2 * b, [(0, True)])
            q_step(2 * b + 1, [(0, False), (1, True)])

            def rest(i, _):
                q_step(2 * b + 2 + 2 * i, [(0, False), (1, False)])
                q_step(2 * b + 3 + 2 * i, [(0, False), (1, False)])
                return 0

            lax.fori_loop(0, (nq - 2) // 2 - b, rest, 0)
            for jj in (0, 1):
                dk_ref[pl.ds(s0s[jj], T), :] = dk_acc[jj].astype(BF16)
                dv_ref[pl.ds(s0s[jj], T), :] = dv_acc[jj].astype(BF16)
            return 0

        lax.fori_loop(0, nq // 2, kv_pair, 0)
        dq_ref[...] = dq_acc[...].astype(BF16)

        def add_cols(kj, _):
            s0 = pl.multiple_of(kj * T, T)
            dc_ref[:, pl.ds(s0, T)] += dcol_acc[pl.ds(s0, T), :].T[0:2, :]
            return 0

        lax.fori_loop(0, nq, add_cols, 0)

    col = pl.BlockSpec((S, PAIR), lambda hp: (0, hp))
    row2 = pl.BlockSpec((None, 2, S), lambda hp: (hp, 0, 0))
    return _pcall(
        body, name=name, grid=(NP,), comm=comm,
        out_shape=(jax.ShapeDtypeStruct((S, DM), BF16), jax.ShapeDtypeStruct((S, DM), BF16),
                   jax.ShapeDtypeStruct((S, DM), BF16), jax.ShapeDtypeStruct((NP, 2, S), F32)),
        in_specs=[col, col, col, col, col, col, row2], out_specs=(col, col, col, row2),
        scratch_shapes=[pltpu.VMEM((S, PAIR), F32), pltpu.VMEM((S, LANES), F32), pltpu.VMEM((2, T, PAIR), F32),
                        pltpu.VMEM((2, T, PAIR), F32), pltpu.VMEM((8, S), F32)],
        args=(q, k, v, o, do, ca, lse))


def _mixer_b_fwd(x, o, qm, mkv, wo, g, b, name):
    S, D = x.shape
    DQ = qm.shape[1]
    DM = D - DQ
    tm = min(512, S)

    def body(x_ref, o_ref, qm_ref, mkv_ref, wo_ref, g_ref, b_ref, xo_ref, xbt_ref, r_ref, catt_ref, cat_ref):
        cat_ref[:, :DM] = o_ref[...]
        mo = _mem_attn_fwd(qm_ref[...], mkv_ref, DQ)
        for mp in range(DQ // PAIR):
            cat_ref[:, DM + mp * PAIR:DM + (mp + 1) * PAIR] = mo[mp].astype(BF16)
        _tail_fwd_store(x_ref[...], cat_ref, wo_ref, g_ref, b_ref, xo_ref, xbt_ref, r_ref, catt_ref)

    return pl.pallas_call(
        body, name=name, grid=(S // tm,),
        out_shape=(jax.ShapeDtypeStruct((S, D), F32), jax.ShapeDtypeStruct((D, S), BF16),
                   jax.ShapeDtypeStruct((S, D), F32), jax.ShapeDtypeStruct((D, S), BF16)),
        in_specs=[_rows(tm, D), _rows(tm, DM), _rows(tm, DQ), _full(mkv.shape), _full(wo.shape), _full((1, D)),
                  _full((1, D))],
        out_specs=(_rows(tm, D), _cols(D, tm), _rows(tm, D), _cols(D, tm)),
        scratch_shapes=[pltpu.VMEM((tm, D), BF16)],
        compiler_params=_params(),
    )(x, o, qm, mkv, wo, g, b)


def _mixer_b_bwd(dxo, r, qm, mkv, wo, g, name):
    S, D = r.shape
    DQ = qm.shape[1]
    DM = D - DQ
    M = mkv.shape[0]
    tm = min(512, S)

    def body(dxo_ref, r_ref, qm_ref, mkv_ref, wo_ref, g_ref, dr_ref, drb_ref, dcat_ref, dgb_ref, dmkv_ref):
        @pl.when(pl.program_id(0) == 0)
        def _():
            dgb_ref[...] = jnp.zeros_like(dgb_ref)
            dmkv_ref[...] = jnp.zeros_like(dmkv_ref)

        dr, dcat = _tail_bwd_head(dxo_ref, r_ref, g_ref, wo_ref, dgb_ref, drb_ref)
        dr_ref[...] = dr
        dcat_ref[:, :DM] = dcat[:, :DM].astype(BF16)
        dmo = [dcat[:, DM + mp * PAIR:DM + (mp + 1) * PAIR] for mp in range(DQ // PAIR)]
        dqm = _mem_attn_bwd(qm_ref[...], mkv_ref, dmo, dmkv_ref, DQ)
        for mp in range(DQ // PAIR):
            dcat_ref[:, DM + mp * PAIR:DM + (mp + 1) * PAIR] = dqm[mp].astype(BF16)

    return pl.pallas_call(
        body, name=name, grid=(S // tm,),
        out_shape=(jax.ShapeDtypeStruct((S, D), F32), jax.ShapeDtypeStruct((S, D), BF16),
                   jax.ShapeDtypeStruct((S, D), BF16), jax.ShapeDtypeStruct((2, D), F32),
                   jax.ShapeDtypeStruct((M, 2 * DQ), F32)),
        in_specs=[_rows(tm, D), _rows(tm, D), _rows(tm, DQ), _full(mkv.shape), _full(wo.shape), _full((1, D))],
        out_specs=(_rows(tm, D), _rows(tm, D), _rows(tm, D), _full((2, D)), _full((M, 2 * DQ))),
        compiler_params=_params(),
    )(dxo, r, qm, mkv, wo, g)


class _NoExchange:
    def plan(self, stage, w, grads):
        return None

    def done(self, stage, results, w):
        pass


def _local_step(x, mem, target, w, hooks):
    S, D = x.shape
    DQ = w["mkv0"].shape[1] // 2
    DM = D - DQ
    NG = DM // HEAD_DIM
    NP = DM // PAIR
    ln_g, ln_b = w["ln_g"], w["ln_b"]
    row = lambda a, i: a[i:i + 1]
    memb = mem.astype(BF16)
    bst =jnp.pad(w["b_s"].T, ((0, 0), (0, LANES - NG)))
    grads = {}

    def staged(stage, fn, *args):
        out, got = fn(*args, stage, comm=hooks.plan(stage, w, grads))
        hooks.done(stage, got, w)
        return out

    mkv0 = _mm_small(memb, w["mkv0"], "mem_kv_0")
    x1, x1t, r1, cat0t = staged("mixer_a_fwd", _mixer_a_fwd, x, w["wa"], w["sg"], w["sb"], w["ws"], bst, mkv0,
                                w["wo0"], row(ln_g, 0), row(ln_b, 0))
    x2, x2b, x2t, r2, h0 = staged("mlp_fwd_0", _mlp_fwd, x1, w["wup0"], w["wdown0"], row(ln_g, 1), row(ln_b, 1))
    q, qm, k, v, fl = _proj_b_fwd(x2b, w["wq"], w["wkv"], DM, "proj_b_fwd")
    flt = fl[:, :16].T
    bfc = jnp.pad(w["bf"], (0, 16 - NG)).reshape(16, 1)
    pieces = _forget_cumsum(flt, bfc, "forget_cumsum")
    pieces = jnp.pad(pieces[:, :NG].transpose(2, 1, 0), ((0, 0), (0, 0), (0, HEAD_DIM - N_BIAS)))
    ca = pieces.reshape(S, NP, 2, HEAD_DIM)[:, :, ::-1].reshape(S, DM).astype(BF16)
    o, lse = staged("fox_fwd", _fox_fwd, q, k, v, ca)
    mkv1 = _mm_small(memb, w["mkv1"], "mem_kv_1")
    x3, x3t, r3, cat1t = _mixer_b_fwd(x2, o, qm, mkv1, w["wo1"], row(ln_g, 2), row(ln_b, 2), "mixer_b_fwd")
    (x4, _, _, r4, h1), _ = _mlp_fwd(x3, w["wup1"], w["wdown1"], row(ln_g, 3), row(ln_b, 3), "mlp_fwd_1")

    nc = w["wup0"].shape[0]
    dx4, loss_part = _loss_grad(x4, target, "loss_grad")
    (dx3, dr4t, dpre1, dgb11), _ = _mlp_bwd(dx4, r4, h1, w["wup1"], w["wdown1"], row(ln_g, 3), "mlp_bwd_1")
    grads["wdown1"] = _mm_nn(dr4t, h1, "dw_down_1", out_split=nc)
    grads["wup1"] = _mm_nn(x3t, dpre1, "dw_up_1", out_split=nc)
    dr3, dr3b, dcat1, dgb10, dmkv1 = _mixer_b_bwd(dx3, r3, qm, mkv1, w["wo1"], row(ln_g, 2), "mixer_b_bwd")
    do = dcat1[:, :DM]
    dq, dk, dv, dct3 = staged("fox_bwd", _fox_bwd, q, k, v, o, do, ca, lse)
    dct = jnp.pad(dct3.reshape(NG, S), ((0, 16 - NG), (0, 0)))
    dflt, dbf = _forget_cumsum_bwd(dct, flt, bfc, "forget_cumsum_bwd")
    dfl = jnp.pad(dflt.T, ((0, 0), (0, LANES - 16))).astype(BF16)
    dqz = jnp.concatenate([dq, dcat1[:, DM:]], axis=1)
    dkvf = jnp.concatenate([dk, dv, dfl], axis=1)
    dx2 = _proj_b_bwd(dr3, dqz, dkvf, w["wq"], w["wkv"], "proj_b_bwd")
    grads["wo1"] = _mm_nn(cat1t, dr3b, "dw_o_1")
    grads["wq"] = _mm_nn(x2t, dqz, "dw_q")
    grads["wkv"] = _mm_nn(x2t, dkvf, "dw_kv")
    grads["mkv1"] = _mm_tn(memb, dmkv1.astype(BF16), "dw_mkv_1")
    dx1, dr2t, dpre0, dgb01 = staged("mlp_bwd_0", _mlp_bwd, dx2, r2, h0, w["wup0"], w["wdown0"], row(ln_g, 1))
    grads["wdown0"] = _mm_nn(dr2t, h0, "dw_down_0", out_split=nc)
    grads["wup0"] = _mm_nn(x1t, dpre0, "dw_up_0", out_split=nc)
    dx0, dz, dr1b, dgb00, dsgb, dws, dbst, dmkv0 = staged(
        "mixer_a_bwd", _mixer_a_bwd, x, dx1, r1, w["wa"], w["sg"], w["sb"], w["ws"], bst, mkv0, w["wo0"], row(ln_g, 0))
    grads["wo0"] = _mm_nn(cat0t, dr1b, "dw_o_0")
    grads["wa"] = _mm_nn(x.T.astype(BF16), dz, "dw_a")
    grads["mkv0"] = _mm_tn(memb, dmkv0.astype(BF16), "dw_mkv_0")
    grads.update({
        "ws": dws, "b_s": dbst[:, :NG].T, "sg": dsgb[0:1], "sb": dsgb[1:2], "bf": dbf[:NG, 0],
        "ln_g": jnp.concatenate([dgb00[0:1], dgb01[0:1], dgb10[0:1], dgb11[0:1]], axis=0),
        "ln_b": jnp.concatenate([dgb00[1:2], dgb01[1:2], dgb10[1:2], dgb11[1:2]], axis=0),
    })
    return loss_part, dx0, grads


def _place():
    xi, yi, ci = lax.axis_index("x"), lax.axis_index("y"), lax.axis_index("c")
    peers = [(xi, 1 - yi), (1 - xi, yi), (1 - xi, 1 - yi)]
    return xi, yi, ci, peers


def _comm_call(body, name, ins, out_shapes, n_remote, n_local):
    return pl.pallas_call(
        body, name=name, out_shape=out_shapes, in_specs=[ANY] * len(ins), out_specs=[ANY] * len(out_shapes),
        scratch_shapes=[pltpu.SemaphoreType.DMA((n_remote,)), pltpu.SemaphoreType.DMA((n_remote,)),
                        pltpu.SemaphoreType.DMA((max(n_local, 1),))],
        compiler_params=pltpu.CompilerParams(has_side_effects=True),
    )(*ins)


def _chip_allgather(xs, halved, name):
    n = len(xs)
    out_shapes = [jax.ShapeDtypeStruct((N_CHIPS,) + (a.shape[1:] if hv else a.shape), a.dtype)
                  for a, hv in zip(xs, halved)]

    def body(*refs):
        x_refs, o_refs = refs[:n], refs[n:2 * n]
        send_sems, recv_sems, loc_sems = refs[2 * n:]
        xi, yi, ci, peers = _place()
        j = 2 * xi + yi
        sends, locs = [], []
        for i in range(n):
            src = x_refs[i].at[ci] if halved[i] else x_refs[i]
            loc = pltpu.make_async_copy(src, o_refs[i].at[j], loc_sems.at[i])
            loc.start()
            locs.append(loc)
            for d, (px, py) in enumerate(peers):
                cp = pltpu.make_async_remote_copy(src_ref=src, dst_ref=o_refs[i].at[j], send_sem=send_sems.at[3 * i + d],
                                                  recv_sem=recv_sems.at[3 * i + d], device_id=(px, py, ci),
                                                  device_id_type=MESH)
                cp.start()
                sends.append(cp)
        for i in range(n):
            src = x_refs[i].at[ci] if halved[i] else x_refs[i]
            for d, (px, py) in enumerate(peers):
                pltpu.make_async_remote_copy(src_ref=src, dst_ref=o_refs[i].at[2 * px + py],
                                             send_sem=send_sems.at[3 * i + d], recv_sem=recv_sems.at[3 * i + d],
                                             device_id=(px, py, ci), device_id_type=MESH).wait_recv()
        for cp in sends:
            cp.wait_send()
        for loc in locs:
            loc.wait()

    return _comm_call(body, name, xs, out_shapes, 3 * n, n)


def _pair_swap(xs, name):
    n = len(xs)
    out_shapes = [jax.ShapeDtypeStruct((a.shape[0],) + a.shape[2:], a.dtype) for a in xs]

    def body(*refs):
        x_refs, o_refs = refs[:n], refs[n:2 * n]
        send_sems, recv_sems, _ = refs[2 * n:]
        xi, yi, ci, _ = _place()
        cps = []
        for i in range(n):
            cp = pltpu.make_async_remote_copy(src_ref=x_refs[i].at[:, 1 - ci], dst_ref=o_refs[i],
                                              send_sem=send_sems.at[i], recv_sem=recv_sems.at[i],
                                              device_id=(xi, yi, 1 - ci), device_id_type=MESH)
            cp.start()
            cps.append(cp)
        for cp in cps:
            cp.wait_recv()
        for cp in cps:
            cp.wait_send()

    return _comm_call(body, name, xs, out_shapes, n, 0)


def _chip_exchange(xs):
    n = len(xs)
    out_shapes = [jax.ShapeDtypeStruct((3,) + a.shape[1:], a.dtype) for a in xs]

    def copies(x_refs, o_refs, sems):
        send_sems, recv_sems = sems
        xi, yi, ci, peers = _place()
        return [pltpu.make_async_remote_copy(src_ref=x_refs[i].at[2 * px + py], dst_ref=o_refs[i].at[d],
                                             send_sem=send_sems.at[3 * i + d], recv_sem=recv_sems.at[3 * i + d],
                                             device_id=(px, py, ci), device_id_type=MESH)
                for i in range(n) for d, (px, py) in enumerate(peers)]

    def start(x_refs, o_refs, sems):
        for cp in copies(x_refs, o_refs, sems):
            cp.start()

    def finish(x_refs, o_refs, sems):
        cps = copies(x_refs, o_refs, sems)
        for cp in cps:
            cp.wait_recv()
        for cp in cps:
            cp.wait_send()

    return _Comm(xs, out_shapes, {}, [3 * n, 3 * n], start, finish)


def _inplace_call(body, name, bufs, sem_counts):
    n = len(bufs)
    return pl.pallas_call(
        body, name=name, out_shape=[jax.ShapeDtypeStruct(a.shape, a.dtype) for a in bufs],
        in_specs=[ANY] * n, out_specs=[ANY] * n, input_output_aliases={i: i for i in range(n)},
        scratch_shapes=[pltpu.SemaphoreType.DMA((k,)) for k in sem_counts],
        compiler_params=pltpu.CompilerParams(has_side_effects=True),
    )(*bufs)


def _gather_weights(bufs, whole=()):
    n, nw = len(bufs), len(whole)

    def ici(w, sems, i, d, px, py, ci, slot):
        ref = w[i].at[slot, ci] if i < n else w[i].at[slot]
        return pltpu.make_async_remote_copy(src_ref=ref, dst_ref=ref, send_sem=sems[0].at[3 * i + d],
                                            recv_sem=sems[1].at[3 * i + d], device_id=(px, py, ci), device_id_type=MESH)

    def d2d(w, sems, i, d, xi, yi, ci, slot, half):
        ref = w[i].at[slot, half]
        return pltpu.make_async_remote_copy(src_ref=ref, dst_ref=ref, send_sem=sems[2].at[3 * i + d],
                                            recv_sem=sems[3].at[3 * i + d], device_id=(xi, yi, 1 - ci),
                                            device_id_type=MESH)

    def start(_, w, sems):
        xi, yi, ci, peers = _place()
        for i in range(n + nw):
            for d, (px, py) in enumerate(peers):
                ici(w, sems, i, d, px, py, ci, 2 * xi + yi).start()

    def finish(_, w, sems):
        xi, yi, ci, peers = _place()
        for d, (px, py) in enumerate(peers):
            for i in range(n + nw):
                ici(w, sems, i, d, px, py, ci, 2 * px + py).wait_recv()
                if i < n:
                    d2d(w, sems, i, d, xi, yi, ci, 2 * px + py, ci).start()
        for d, (px, py) in enumerate(peers):
            for i in range(n):
                d2d(w, sems, i, d, xi, yi, ci, 2 * px + py, 1 - ci).wait_recv()
        for d, (px, py) in enumerate(peers):
            for i in range(n + nw):
                ici(w, sems, i, d, px, py, ci, 2 * xi + yi).wait_send()
                if i < n:
                    d2d(w, sems, i, d, xi, yi, ci, 2 * px + py, ci).wait_send()

    arrays = list(bufs) + list(whole)
    k = 3 * (n + nw)
    return _Comm(arrays, [jax.ShapeDtypeStruct(a.shape, a.dtype) for a in arrays], {i: i for i in range(n + nw)},
                 [k, k, max(3 * n, 1), max(3 * n, 1)], start, finish)


def _pair_gather(bufs, name):
    n = len(bufs)
    where = [(i, l) for i, a in enumerate(bufs) for l in range(a.shape[0])]

    def body(*refs):
        g = refs[n:2 * n]
        send_sems, recv_sems = refs[2 * n:]
        xi, yi, ci, _ = _place()
        sends = []
        for k, (i, l) in enumerate(where):
            mine = g[i].at[l, :, ci]
            cp = pltpu.make_async_remote_copy(src_ref=mine, dst_ref=mine, send_sem=send_sems.at[k],
                                              recv_sem=recv_sems.at[k], device_id=(xi, yi, 1 - ci), device_id_type=MESH)
            cp.start()
            sends.append(cp)
        for k, (i, l) in enumerate(where):
            other = g[i].at[l, :, 1 - ci]
            pltpu.make_async_remote_copy(src_ref=other, dst_ref=other, send_sem=send_sems.at[k],
                                         recv_sem=recv_sems.at[k], device_id=(xi, yi, 1 - ci),
                                         device_id_type=MESH).wait_recv()
        for cp in sends:
            cp.wait_send()

    return _inplace_call(body, name, list(bufs), [len(where), len(where)])


def _row_block(rows, cols):
    want = max(8, (2 ** 18 // max(cols, 1)) // 8 * 8)
    if rows <= want:
        return rows
    best = 8
    for t in range(8, want + 1, 8):
        if rows % t == 0:
            best = t
    return best


def _pair_add(g4, recv, sel, out_dtype, name):
    A, _, H, C = g4.shape
    bh = _row_block(H, C)

    def body(sel_ref, g_ref, r_ref, o_ref):
        o_ref[...] = (g_ref[...] + r_ref[...]).astype(out_dtype)

    return pl.pallas_call(
        body, name=name, out_shape=jax.ShapeDtypeStruct((A, H, C), out_dtype),
        grid_spec=pltpu.PrefetchScalarGridSpec(
            num_scalar_prefetch=1, grid=(A, H // bh),
            in_specs=[pl.BlockSpec((None, None, bh, C), lambda a, i, s: (a, s[1], i, 0)),
                      pl.BlockSpec((None, bh, C), lambda a, i, s: (a, i, 0))],
            out_specs=pl.BlockSpec((None, bh, C), lambda a, i, s: (a, i, 0))),
        compiler_params=pltpu.CompilerParams(dimension_semantics=("arbitrary", "arbitrary"),
                                             vmem_limit_bytes=VMEM_LIMIT),
    )(sel, g4, recv)


def _chip_reduce(g4, recv1, recv2, sel, buf, layer, n_layers, name):
    _, _, H, C = g4.shape
    bh = _row_block(H, C)

    def body(sel_ref, g_ref, r1_ref, r2_ref, *rest):
        acc = g_ref[...] + r1_ref[...]
        for d in range(3):
            acc = acc + r2_ref[d].astype(F32)
        rest[-1][...] = acc

    in_specs = [pl.BlockSpec((None, None, bh, C), lambda i, s: (s[0], s[1], i, 0)),
                pl.BlockSpec((None, bh, C), lambda i, s: (s[0], i, 0)),
                pl.BlockSpec((3, bh, C), lambda i, s: (0, i, 0))]
    args = [sel, g4, recv1, recv2]
    aliases = {}
    if buf is not None:
        in_specs.append(ANY)
        args.append(buf)
        aliases = {4: 0}
    return pl.pallas_call(
        body, name=name, out_shape=jax.ShapeDtypeStruct((n_layers, 1, 2, H, C), F32),
        grid_spec=pltpu.PrefetchScalarGridSpec(
            num_scalar_prefetch=1, grid=(H // bh,), in_specs=in_specs,
            out_specs=pl.BlockSpec((None, None, None, bh, C), lambda i, s: (layer, 0, s[1], i, 0))),
        input_output_aliases=aliases,
        compiler_params=pltpu.CompilerParams(dimension_semantics=("arbitrary",), vmem_limit_bytes=VMEM_LIMIT),
    )(*args)


def _cast_place(shard2, sel, name):
    _, H, C = shard2.shape
    bh = _row_block(H, C)

    def body(sel_ref, x_ref, o_ref):
        o_ref[...] = x_ref[...].astype(BF16)

    return pl.pallas_call(
        body, name=name, out_shape=jax.ShapeDtypeStruct((N_CHIPS, 2, H, C), BF16),
        grid_spec=pltpu.PrefetchScalarGridSpec(
            num_scalar_prefetch=1, grid=(2, H // bh),
            in_specs=[pl.BlockSpec((None, bh, C), lambda h, i, s: (h, i, 0))],
            out_specs=pl.BlockSpec((None, None, bh, C), lambda h, i, s: (s[0], h, i, 0))),
        compiler_params=pltpu.CompilerParams(dimension_semantics=("arbitrary", "arbitrary"),
                                             vmem_limit_bytes=VMEM_LIMIT),
    )(sel, shard2)


def _adamw(g, w, m, v, name):
    R, C = g.shape
    br = _row_block(R, C)

    def body(g_ref, w_ref, m_ref, v_ref, d_ref, mo_ref, vo_ref):
        gv = g_ref[...]
        mn = ADAM_B1 * m_ref[...] + (1.0 - ADAM_B1) * gv
        vn = ADAM_B2 * v_ref[...] + (1.0 - ADAM_B2) * (gv * gv)
        m_hat = mn / (1.0 - ADAM_B1 ** ADAM_STEP)
        v_hat = vn / (1.0 - ADAM_B2 ** ADAM_STEP)
        d_ref[...] = -ADAM_LR * (m_hat / (jnp.sqrt(v_hat) + ADAM_EPS) + ADAM_WD * w_ref[...])
        mo_ref[...] = mn
        vo_ref[...] = vn

    spec = pl.BlockSpec((br, C), lambda i: (i, 0))
    return pl.pallas_call(
        body, name=name, grid=(R // br,), out_shape=(jax.ShapeDtypeStruct((R, C), F32),) * 3,
        in_specs=[spec] * 4, out_specs=(spec,) * 3, compiler_params=_params(),
    )(g, w, m, v)


def _part_rows(shape):
    return -(-math.prod(shape) // (8 * LANES)) * 8


def _pack_rows(arrs, total_rows):
    parts = []
    for a in arrs:
        flat = a.reshape(-1)
        rows = _part_rows(a.shape)
        parts.append(jnp.pad(flat, (0, rows * LANES - flat.shape[0])).reshape(rows, LANES))
    packed = jnp.concatenate(parts, axis=0)
    return jnp.pad(packed, ((0, total_rows - packed.shape[0]), (0, 0)))


def _unpack_rows(packed, shapes):
    out, r = [], 0
    for shp in shapes:
        size, rows = math.prod(shp), _part_rows(shp)
        out.append(packed[r:r + rows].reshape(-1)[:size].reshape(shp))
        r += rows
    return out


def _rows_of(shapes):
    return sum(_part_rows(s) for s in shapes)


_GATHER_STAGES = {"start": ["wa", "mkv0", "wo0"], "mixer_a_fwd": ["wup0", "wdown0"],
                  "mlp_fwd_0": ["wq", "wkv", "mkv1", "wo1"], "fox_fwd": ["wup1", "wdown1"]}
_REDUCE_STAGES = {"fox_bwd": ["wup1", "wdown1"], "mlp_bwd_0": ["wo1", "wq", "wkv", "mkv1"],
                  "mixer_a_bwd": ["wup0", "wdown0"], "end": ["wa", "wo0", "mkv0", "small"]}
_SMALL = ["ws", "b_s", "sg", "sb", "bf", "ln_g", "ln_b"]


class _Staged:
    def __init__(self, placed, sel, dm, ng, shard_cols, n_small):
        self.placed, self.sel, self.dm, self.ng, self.shard_cols, self.n_small = placed, sel, dm, ng, shard_cols, n_small
        self.g4, self.recv1, self.recv2 = {}, {}, {}

    def _weight(self, name, a):
        a = a.reshape(N_CHIPS, 2 * a.shape[2], a.shape[3])
        if name in ("wa", "wkv"):
            a = a.transpose(1, 0, 2).reshape(a.shape[1], -1)
            if name == "wkv":
                a = jnp.concatenate([a[:, :2 * self.dm],
                                     jnp.pad(a[:, 2 * self.dm:], ((0, 0), (0, LANES - self.ng)))], axis=1)
            return a
        if name.startswith("wup") or name.startswith("wdown"):
            return a
        return a.reshape(-1, a.shape[2])

    def _partial(self, name, grads):
        if name == "small":
            a = _pack_rows([grads[n] for n in _SMALL], self.n_small).reshape(N_CHIPS, -1, LANES)
        elif name in self.shard_cols:
            n = self.shard_cols[name]
            g = grads[name][:, :N_CHIPS * n]
            a = g.reshape(g.shape[0], N_CHIPS, n).transpose(1, 0, 2)
        elif name.startswith("wup") or name.startswith("wdown"):
            a = grads[name]
        else:
            g = grads[name]
            a = g.reshape(N_CHIPS, g.shape[0] // N_CHIPS, g.shape[1])
        return a.reshape(N_CHIPS, 2, a.shape[1] // 2, a.shape[2])

    def plan(self, stage, w, grads):
        if stage in _GATHER_STAGES:
            return _gather_weights([self.placed[n] for n in _GATHER_STAGES[stage]])
        if stage in _REDUCE_STAGES:
            names = _REDUCE_STAGES[stage]
            g4 = [self._partial(n, grads) for n in names]
            recv1 = _pair_swap(g4, "reduce_pair_swap_" + stage)
            summed = []
            for n, g, r1 in zip(names, g4, recv1):
                self.g4[n], self.recv1[n] = g, r1
                summed.append(_pair_add(g, r1, self.sel, F32 if n == "small" else BF16, "reduce_pair_add_" + n))
            return _chip_exchange(summed)
        return None

    def done(self, stage, results, w):
        if stage in _GATHER_STAGES:
            for n, a in zip(_GATHER_STAGES[stage], results):
                w[n] = self._weight(n, a)
        elif stage in _REDUCE_STAGES:
            for n, r in zip(_REDUCE_STAGES[stage], results):
                self.recv2[n] = r


def kernel(x, mem, a_w_in, a_sgu_ln_g, a_sgu_ln_b, a_w_s, a_b_s, kv_w, kv_b_f, b_w_q, mem_w_kv, w_o, ln_g, ln_b, w_up, w_down, loss_target, m_a_w_in, m_a_sgu_ln_g, m_a_sgu_ln_b, m_a_w_s, m_a_b_s, m_kv_w, m_kv_b_f, m_b_w_q, m_mem_w_kv, m_w_o, m_ln_g, m_ln_b, m_w_up, m_w_down, v_a_w_in, v_a_sgu_ln_g, v_a_sgu_ln_b, v_a_w_s, v_a_b_s, v_kv_w, v_kv_b_f, v_b_w_q, v_mem_w_kv, v_w_o, v_ln_g, v_ln_b, v_w_up, v_w_down):
    xi, yi, ci = lax.axis_index("x"), lax.axis_index("y"), lax.axis_index("c")
    chip = 2 * xi + yi
    sel = jnp.stack([chip, ci]).astype(jnp.int32)
    S, D = x.shape[1], x.shape[2]
    DQ = mem_w_kv.shape[2] // 2
    DM = D - DQ
    NG = DM // HEAD_DIM

    shards = {"wa": a_w_in[0], "wkv": kv_w, "wq": b_w_q[0], "mkv0": mem_w_kv[0], "mkv1": mem_w_kv[1],
              "wo0": w_o[0], "wo1": w_o[1], "wup0": w_up[0], "wup1": w_up[1], "wdown0": w_down[0], "wdown1": w_down[1]}
    placed = {n: _cast_place(s.reshape(2, s.shape[0] // 2, s.shape[1]), sel, "cast_place_" + n)
              for n, s in shards.items()}
    ln_pack = jnp.concatenate([ln_g.reshape(4, -1), ln_b.reshape(4, -1)], axis=0)
    ln_buf = lax.dynamic_update_slice(jnp.zeros((N_CHIPS,) + ln_pack.shape, F32), ln_pack[None], (chip, 0, 0))
    small_shapes = [a_w_s.shape, a_b_s.shape, a_sgu_ln_g.shape, a_sgu_ln_b.shape, kv_b_f.shape]
    ln_shape = (4, D)
    n_small = -(-_rows_of(small_shapes + [ln_shape, ln_shape]) // 64) * 64
    hooks = _Staged(placed, sel, DM, NG, {"wa": a_w_in.shape[2], "wkv": kv_w.shape[1]}, n_small)
    weights = {"sg": a_sgu_ln_g, "sb": a_sgu_ln_b, "ws": a_w_s[0], "b_s": a_b_s[0], "bf": kv_b_f}
    got = _run_comm(_gather_weights([placed[n] for n in _GATHER_STAGES["start"]], whole=[ln_buf]),
                    "gather_weights_first")
    hooks.done("start", got[:-1], weights)
    ln_full = got[-1].transpose(1, 0, 2).reshape(8, D)
    weights["ln_g"], weights["ln_b"] = ln_full[:4], ln_full[4:]

    loss_part, grad_x, gr = _local_step(x[0], mem[0], loss_target[0], weights, hooks)
    loss = lax.psum(loss_part[0, 0], ("x", "y", "c"))

    hooks.done("end", _run_comm(hooks.plan("end", weights, gr), "reduce_chip_exchange_end"), weights)
    groups = [["wa"], ["wkv"], ["wq"], ["mkv0", "mkv1"], ["wo0", "wo1"], ["wup0", "wup1"], ["wdown0", "wdown1"],
              ["small"]]
    mine = []
    for grp in groups:
        buf = None
        for layer, n in enumerate(grp):
            buf = _chip_reduce(hooks.g4[n], hooks.recv1[n], hooks.recv2[n], sel, buf, layer, len(grp),
                               "reduce_chip_sum_" + n)
        mine.append(buf)
    red = _pair_gather(mine, "reduce_pair_gather")
    red = [a.reshape(a.shape[0], 2 * a.shape[3], a.shape[4]) for a in red]
    g_a_w_in, g_kv_w, g_b_w_q, g_mem_w_kv, g_w_o, g_w_up, g_w_down, small_mine = red
    g_kv_w = g_kv_w[0]
    g_w_down = g_w_down.transpose(0, 2, 1)
    small_all = _chip_allgather([small_mine[0]], [False], "gather_small_grads")[0].reshape(n_small, LANES)
    g_ws, g_bs, g_sg, g_sb, g_bf, g_lng, g_lnb = _unpack_rows(small_all, small_shapes + [ln_shape, ln_shape])
    dsh = D // N_CHIPS
    g_ln_g = lax.dynamic_slice_in_dim(g_lng, chip * dsh, dsh, axis=1).reshape(ln_g.shape)
    g_ln_b = lax.dynamic_slice_in_dim(g_lnb, chip * dsh, dsh, axis=1).reshape(ln_b.shape)

    def update(name, g, w, m, v):
        flat = lambda a: a.reshape(-1, a.shape[-1])
        d, mn, vn = _adamw(flat(g), flat(w), flat(m), flat(v), "adamw_" + name)
        return d.reshape(w.shape), mn.reshape(w.shape), vn.reshape(w.shape)

    upd = {
        "a_w_in": update("a_w_in", g_a_w_in, a_w_in, m_a_w_in, v_a_w_in),
        "kv_w": update("kv_w", g_kv_w, kv_w, m_kv_w, v_kv_w),
        "b_w_q": update("b_w_q", g_b_w_q, b_w_q, m_b_w_q, v_b_w_q),
        "mem_w_kv": update("mem_w_kv", g_mem_w_kv, mem_w_kv, m_mem_w_kv, v_mem_w_kv),
        "w_o": update("w_o", g_w_o, w_o, m_w_o, v_w_o),
        "w_up": update("w_up", g_w_up, w_up, m_w_up, v_w_up),
        "w_down": update("w_down", g_w_down, w_down, m_w_down, v_w_down),
    }
    tiny_g = [g_sg, g_sb, g_ws, g_bs, g_bf, g_ln_g, g_ln_b]
    tiny_w = [a_sgu_ln_g, a_sgu_ln_b, a_w_s, a_b_s, kv_b_f, ln_g, ln_b]
    tiny_m = [m_a_sgu_ln_g, m_a_sgu_ln_b, m_a_w_s, m_a_b_s, m_kv_b_f, m_ln_g, m_ln_b]
    tiny_v = [v_a_sgu_ln_g, v_a_sgu_ln_b, v_a_w_s, v_a_b_s, v_kv_b_f, v_ln_g, v_ln_b]
    tiny_shapes = [a.shape for a in tiny_w]
    n_tiny = -(-_rows_of(tiny_shapes) // 8) * 8
    td, tm_, tv = _adamw(_pack_rows(tiny_g, n_tiny), _pack_rows(tiny_w, n_tiny), _pack_rows(tiny_m, n_tiny),
                         _pack_rows(tiny_v, n_tiny), "adamw_small")
    for name, d, mn, vn in zip(["a_sgu_ln_g", "a_sgu_ln_b", "a_w_s", "a_b_s", "kv_b_f", "ln_g", "ln_b"],
                               _unpack_rows(td, tiny_shapes), _unpack_rows(tm_, tiny_shapes),
                               _unpack_rows(tv, tiny_shapes)):
        upd[name] = (d, mn, vn)

    order = ["a_w_in", "a_sgu_ln_g", "a_sgu_ln_b", "a_w_s", "a_b_s", "kv_w", "kv_b_f", "b_w_q", "mem_w_kv", "w_o",
             "ln_g", "ln_b", "w_up", "w_down"]
    grads = {"a_w_in": g_a_w_in.reshape(a_w_in.shape), "a_sgu_ln_g": g_sg.reshape(a_sgu_ln_g.shape),
             "a_sgu_ln_b": g_sb.reshape(a_sgu_ln_b.shape), "a_w_s": g_ws.reshape(a_w_s.shape),
             "a_b_s": g_bs.reshape(a_b_s.shape), "kv_w": g_kv_w, "kv_b_f": g_bf, "b_w_q": g_b_w_q.reshape(b_w_q.shape),
             "mem_w_kv": g_mem_w_kv, "w_o": g_w_o, "ln_g": g_ln_g, "ln_b": g_ln_b, "w_up": g_w_up, "w_down": g_w_down}
    return (loss, grad_x[None], *[grads[n] for n in order], *[upd[n][0] for n in order],
            *[upd[n][1] for n in order], *[upd[n][2] for n in order])
```

```python
import functools
import math

import jax
import jax.numpy as jnp
from jax import lax
from jax.experimental import pallas as pl
from jax.experimental.pallas import tpu as pltpu

F32 = jnp.float32
BF16 = jnp.bfloat16

HEAD_DIM = 64
PAIR = 2 * HEAD_DIM
CHUNK = 128
LN_EPS = 1e-5
ALPHA = 4 ** 0.25
QK_SCALE = 1.0 / math.sqrt(HEAD_DIM)
NEG = -1e30
N_CHIPS = 4
FOX_BLOCK = 256
LANES = 128
VMEM_LIMIT = 48 * 2 ** 20

ADAM_LR, ADAM_B1, ADAM_B2, ADAM_EPS, ADAM_WD, ADAM_STEP = 0.001, 0.9, 0.999, 1e-08, 0.01, 10

MESH = pl.DeviceIdType.MESH
ANY = pl.BlockSpec(memory_space=pl.ANY)


def _dot(a, b):
    return jnp.dot(a, b, preferred_element_type=F32)


def _dot_nt(a, b):
    return lax.dot_general(a, b, (((1,), (1,)), ((), ())), preferred_element_type=F32)


def _dot_tn(a, b):
    return lax.dot_general(a, b, (((0,), (0,)), ((), ())), preferred_element_type=F32)


def _ln_stats(r):
    mu = jnp.mean(r, axis=-1, keepdims=True)
    d = r - mu
    var = jnp.mean(d * d, axis=-1, keepdims=True)
    rstd = lax.rsqrt(var + LN_EPS)
    return d * rstd, rstd


def _ln_bwd(dy, xhat, rstd, g):
    dxh = dy * g
    m1 = jnp.mean(dxh, axis=-1, keepdims=True)
    m2 = jnp.mean(dxh * xhat, axis=-1, keepdims=True)
    return rstd * (dxh - m1 - xhat * m2)


_GELU_K = math.sqrt(2.0 / math.pi)


def _gelu(x):
    return 0.5 * x * (1.0 + jnp.tanh(_GELU_K * (x + 0.044715 * x * x * x)))


def _gelu_grad(x):
    t = jnp.tanh(_GELU_K * (x + 0.044715 * x * x * x))
    return 0.5 * (1.0 + t) + 0.5 * x * (1.0 - t * t) * _GELU_K * (1.0 + 3 * 0.044715 * x * x)


def _lane_mask(e):
    lane = lax.broadcasted_iota(jnp.int32, (1, PAIR), 1)
    return (lane >= HEAD_DIM * e) & (lane < HEAD_DIM * (e + 1))


def _first_head():
    return lax.broadcasted_iota(jnp.int32, (1, PAIR), 1) < HEAD_DIM


def _mem_probs(qh, mkp):
    sc = _dot_nt(qh, mkp) * QK_SCALE
    ex = jnp.exp(sc - jnp.max(sc, axis=-1, keepdims=True))
    return ex / jnp.sum(ex, axis=-1, keepdims=True)


def _mem_attn_fwd(qm_b, mkv_ref, dq_dim):
    outs = []
    for mp in range(dq_dim // PAIR):
        qp = qm_b[:, mp * PAIR:(mp + 1) * PAIR]
        mkp = mkv_ref[:, mp * PAIR:(mp + 1) * PAIR]
        mvp = mkv_ref[:, dq_dim + mp * PAIR:dq_dim + (mp + 1) * PAIR]
        heads = []
        for e in (0, 1):
            qh = jnp.where(_lane_mask(e), qp, jnp.zeros_like(qp))
            p = _mem_probs(qh, mkp)
            heads.append(_dot(p.astype(BF16), mvp))
        outs.append(jnp.where(_first_head(), heads[0], heads[1]))
    return outs


def _mem_attn_bwd(qm_b, mkv_ref, dmo_pairs, dmkv_ref, dq_dim):
    dqs = []
    for mp in range(dq_dim // PAIR):
        ks = slice(mp * PAIR, (mp + 1) * PAIR)
        vs = slice(dq_dim + mp * PAIR, dq_dim + (mp + 1) * PAIR)
        qp = qm_b[:, ks]
        mkp = mkv_ref[:, ks]
        mvp = mkv_ref[:, vs]
        dmo_b = dmo_pairs[mp].astype(BF16)
        dq = None
        dmk = None
        dmv = None
        for e in (0, 1):
            hm = _lane_mask(e)
            qh = jnp.where(hm, qp, jnp.zeros_like(qp))
            p = _mem_probs(qh, mkp)
            doh = jnp.where(hm, dmo_b, jnp.zeros_like(dmo_b))
            dp = _dot_nt(doh, mvp)
            ds = p * (dp - jnp.sum(dp * p, axis=-1, keepdims=True))
            dsb = (ds * QK_SCALE).astype(BF16)
            kh = jnp.where(hm, mkp, jnp.zeros_like(mkp))
            dq_e = _dot(dsb, kh)
            dmk_e = _dot_tn(dsb, qh)
            dmv_e = _dot_tn(p.astype(BF16), doh)
            dq = dq_e if dq is None else dq + dq_e
            dmk = dmk_e if dmk is None else dmk + dmk_e
            dmv = dmv_e if dmv is None else dmv + dmv_e
        dmkv_ref[:, ks] += dmk
        dmkv_ref[:, vs] += dmv
        dqs.append(dq)
    return dqs


def _params(n_axes=1):
    return pltpu.CompilerParams(dimension_semantics=("arbitrary",) * n_axes, vmem_limit_bytes=VMEM_LIMIT)


def _full(shape):
    return pl.BlockSpec(shape, lambda *_: (0,) * len(shape))


def _rows(tm, cols):
    return pl.BlockSpec((tm, cols), lambda i: (i, 0))


def _cols(rows, tm):
    return pl.BlockSpec((rows, tm), lambda i: (0, i))


def _pick(n, pref):
    if n <= pref:
        return n
    best = LANES
    for t in range(LANES, pref + 1, LANES):
        if n % t == 0:
            best = t
    return best if 2 * best >= pref or n > 2 * pref else n


class _Comm:
    def __init__(self, ins, out_shapes, aliases, sem_counts, start, finish):
        self.ins, self.out_shapes, self.aliases, self.sem_counts = list(ins), list(out_shapes), dict(aliases), sem_counts
        self.start, self.finish = start, finish


def _pcall(body, *, name, grid, in_specs, out_specs, out_shape, args, scratch_shapes=(), comm=None):
    n_in, n_out, n_scr = len(in_specs), len(out_shape), len(scratch_shapes)
    if comm is None:
        res = pl.pallas_call(body, name=name, grid=grid, out_shape=tuple(out_shape), in_specs=list(in_specs),
                             out_specs=tuple(out_specs), scratch_shapes=list(scratch_shapes),
                             compiler_params=_params())(*args)
        return tuple(res), ()
    nci, nco = len(comm.ins), len(comm.out_shapes)
    last = grid[0] - 1

    def wrapped(*refs):
        ins, refs = refs[:n_in], refs[n_in:]
        cins, refs = refs[:nci], refs[nci:]
        outs, refs = refs[:n_out], refs[n_out:]
        couts, refs = refs[:nco], refs[nco:]
        scr, sems = refs[:n_scr], refs[n_scr:]

        @pl.when(pl.program_id(0) == 0)
        def _():
            comm.start(cins, couts, sems)

        body(*ins, *outs, *scr)

        @pl.when(pl.program_id(0) == last)
        def _():
            comm.finish(cins, couts, sems)

    res = pl.pallas_call(
        wrapped, name=name, grid=grid, out_shape=tuple(out_shape) + tuple(comm.out_shapes),
        in_specs=list(in_specs) + [ANY] * nci, out_specs=tuple(out_specs) + (ANY,) * nco,
        input_output_aliases={n_in + a: n_out + b for a, b in comm.aliases.items()},
        scratch_shapes=list(scratch_shapes) + [pltpu.SemaphoreType.DMA((k,)) for k in comm.sem_counts],
        compiler_params=pltpu.CompilerParams(dimension_semantics=("arbitrary",), vmem_limit_bytes=VMEM_LIMIT,
                                             has_side_effects=True),
    )(*args, *comm.ins)
    return tuple(res[:n_out]), tuple(res[n_out:])


def _run_comm(comm, name):
    nci, nco = len(comm.ins), len(comm.out_shapes)

    def body(*refs):
        cins, couts, sems = refs[:nci], refs[nci:nci + nco], refs[nci + nco:]
        comm.start(cins, couts, sems)
        comm.finish(cins, couts, sems)

    return pl.pallas_call(
        body, name=name, out_shape=tuple(comm.out_shapes), in_specs=[ANY] * nci, out_specs=(ANY,) * nco,
        input_output_aliases=comm.aliases, scratch_shapes=[pltpu.SemaphoreType.DMA((k,)) for k in comm.sem_counts],
        compiler_params=pltpu.CompilerParams(has_side_effects=True),
    )(*comm.ins)


def _mm(a, b, name, a_is_transposed, out_split=1):
    (M, R) = a.shape if a_is_transposed else a.shape[::-1]
    _, N = b.shape
    bm, br = _pick(M, 1024 if a_is_transposed else 512), _pick(R, 512)
    ncol = N // out_split
    bn = _pick(ncol, 1024)
    per = ncol // bn
    last = R // br - 1

    def body(a_ref, b_ref, o_ref, acc_ref):
        @pl.when(pl.program_id(2) == 0)
        def _():
            acc_ref[...] = jnp.zeros_like(acc_ref)

        acc_ref[...] += (_dot if a_is_transposed else _dot_tn)(a_ref[...], b_ref[...])

        @pl.when(pl.program_id(2) == last)
        def _():
            o_ref[...] = acc_ref[...].astype(BF16)

    if out_split == 1:
        out_shape = jax.ShapeDtypeStruct((M, N), BF16)
        out_spec = pl.BlockSpec((bm, bn), lambda i, j, r: (i, j))
    else:
        out_shape = jax.ShapeDtypeStruct((out_split, M, ncol), BF16)
        out_spec = pl.BlockSpec((None, bm, bn), lambda i, j, r: (j // per, i, j % per))
    a_spec = (pl.BlockSpec((bm, br), lambda i, j, r: (i, r)) if a_is_transposed
              else pl.BlockSpec((br, bm), lambda i, j, r: (r, i)))
    return pl.pallas_call(
        body, name=name, out_shape=out_shape, grid=(M // bm, N // bn, R // br),
        in_specs=[a_spec, pl.BlockSpec((br, bn), lambda i, j, r: (r, j))], out_specs=out_spec,
        scratch_shapes=[pltpu.VMEM((bm, bn), F32)],
        compiler_params=pltpu.CompilerParams(dimension_semantics=("parallel", "parallel", "arbitrary"),
                                             vmem_limit_bytes=VMEM_LIMIT),
    )(a, b)


def _mm_tn(a, b, name, out_split=1):
    return _mm(a, b, name, False, out_split)


def _mm_nn(at, b, name, out_split=1):
    return _mm(at, b, name, True, out_split)


def _mm_small(a, b, name):
    def body(a_ref, b_ref, o_ref):
        o_ref[...] = _dot(a_ref[...], b_ref[...]).astype(BF16)

    return pl.pallas_call(body, name=name, out_shape=jax.ShapeDtypeStruct((a.shape[0], b.shape[1]), BF16),
                          compiler_params=pltpu.CompilerParams(vmem_limit_bytes=VMEM_LIMIT))(a, b)


def _mlp_fwd(x, wup4, wdown4, g, b, name, comm=None):
    S, D = x.shape
    NC, _, FC = wup4.shape
    tm = min(256, S)

    def body(x_ref, wup_ref, wdown_ref, g_ref, b_ref, xo_ref, xb_ref, xbt_ref, r_ref, h_ref):
        xv = x_ref[...]
        xb = xv.astype(BF16)
        y = jnp.zeros((tm, D), F32)
        for c in range(NC):
            a = jnp.maximum(_dot(xb, wup_ref[c]), 0.0)
            hb = (a * a).astype(BF16)
            h_ref[:, c * FC:(c + 1) * FC] = hb
            y = y + _dot(hb, wdown_ref[c])
        r = ALPHA * xv + y
        xhat, _ = _ln_stats(r)
        xo = xhat * g_ref[...] + b_ref[...]
        xo_ref[...] = xo
        xb_ref[...] = xo.astype(BF16)
        xbt_ref[...] = xo.T.astype(BF16)
        r_ref[...] = r

    return _pcall(
        body, name=name, grid=(S // tm,), comm=comm,
        out_shape=(jax.ShapeDtypeStruct((S, D), F32), jax.ShapeDtypeStruct((S, D), BF16),
                   jax.ShapeDtypeStruct((D, S), BF16), jax.ShapeDtypeStruct((S, D), F32),
                   jax.ShapeDtypeStruct((S, NC * FC), BF16)),
        in_specs=[_rows(tm, D), _full(wup4.shape), _full(wdown4.shape), _full((1, D)), _full((1, D))],
        out_specs=(_rows(tm, D), _rows(tm, D), _cols(D, tm), _rows(tm, D), _rows(tm, NC * FC)),
        args=(x, wup4, wdown4, g, b))


def _mlp_bwd(dxo, r, h, wup4, wdown4, g, name, comm=None):
    S, D = r.shape
    NC, _, FC = wup4.shape
    tm = min(256, S)

    def body(dxo_ref, r_ref, h_ref, wup_ref, wdown_ref, g_ref, dx_ref, drbt_ref, dpre_ref, dgb_ref):
        @pl.when(pl.program_id(0) == 0)
        def _():
            dgb_ref[...] = jnp.zeros_like(dgb_ref)

        xhat, rstd = _ln_stats(r_ref[...])
        dy = dxo_ref[...]
        dgb_ref[0:1, :] += jnp.sum(dy * xhat, axis=0, keepdims=True)
        dgb_ref[1:2, :] += jnp.sum(dy, axis=0, keepdims=True)
        dr = _ln_bwd(dy, xhat, rstd, g_ref[...])
        drb = dr.astype(BF16)
        drbt_ref[...] = dr.T.astype(BF16)
        dx = ALPHA * dr
        for c in range(NC):
            dh = _dot_nt(drb, wdown_ref[c])
            a = jnp.sqrt(h_ref[:, c * FC:(c + 1) * FC].astype(F32))
            dpre = (2.0 * a * dh).astype(BF16)
            dpre_ref[:, c * FC:(c + 1) * FC] = dpre
            dx = dx + _dot_nt(dpre, wup_ref[c])
        dx_ref[...] = dx

    return _pcall(
        body, name=name, grid=(S // tm,), comm=comm,
        out_shape=(jax.ShapeDtypeStruct((S, D), F32), jax.ShapeDtypeStruct((D, S), BF16),
                   jax.ShapeDtypeStruct((S, NC * FC), BF16), jax.ShapeDtypeStruct((2, D), F32)),
        in_specs=[_rows(tm, D), _rows(tm, D), _rows(tm, NC * FC), _full(wup4.shape), _full(wdown4.shape),
                  _full((1, D))],
        out_specs=(_rows(tm, D), _cols(D, tm), _rows(tm, NC * FC), _full((2, D))),
        args=(dxo, r, h, wup4, wdown4, g))


def _loss_grad(xo, target, name):
    S, D = xo.shape
    tm = min(512, S)

    def body(x_ref, t_ref, d_ref, l_ref):
        @pl.when(pl.program_id(0) == 0)
        def _():
            l_ref[...] = jnp.zeros_like(l_ref)

        err = x_ref[...] - t_ref[...]
        d_ref[...] = err * (1.0 / D)
        l_ref[...] += (0.5 / D) * jnp.sum(err * err)

    return pl.pallas_call(
        body, name=name, grid=(S // tm,),
        out_shape=(jax.ShapeDtypeStruct((S, D), F32), jax.ShapeDtypeStruct((8, LANES), F32)),
        in_specs=[_rows(tm, D), _rows(tm, D)], out_specs=(_rows(tm, D), _full((8, LANES))),
        compiler_params=_params(),
    )(xo, target)


def _tail_fwd_store(xv, cat_ref, wo_ref, g_ref, b_ref, xo_ref, xbt_ref, r_ref, catt_ref):
    cat = cat_ref[...]
    y = _dot(cat, wo_ref[...])
    r = ALPHA * xv + y
    xhat, _ = _ln_stats(r)
    xo = xhat * g_ref[...] + b_ref[...]
    xo_ref[...] = xo
    xbt_ref[...] = xo.T.astype(BF16)
    catt_ref[...] = cat.astype(F32).T.astype(BF16)
    r_ref[...] = r


def _tail_bwd_head(dxo_ref, r_ref, g_ref, wo_ref, dgb_ref, drb_ref):
    xhat, rstd = _ln_stats(r_ref[...])
    dy = dxo_ref[...]
    dgb_ref[0:1, :] += jnp.sum(dy * xhat, axis=0, keepdims=True)
    dgb_ref[1:2, :] += jnp.sum(dy, axis=0, keepdims=True)
    dr = _ln_bwd(dy, xhat, rstd, g_ref[...])
    drb = dr.astype(BF16)
    drb_ref[...] = drb
    return dr, _dot_nt(drb, wo_ref[...])


def _causal(shape):
    return lax.broadcasted_iota(jnp.int32, shape, 1) <= lax.broadcasted_iota(jnp.int32, shape, 0)


def _sgu_mixed(ws_ref, bst_ref, gp, vpair):
    tril = _causal((CHUNK, CHUNK))
    w0 = jnp.where(tril, ws_ref[2 * gp], 0.0).astype(BF16)
    w1 = jnp.where(tril, ws_ref[2 * gp + 1], 0.0).astype(BF16)
    m0 = _dot(w0, vpair) + bst_ref[:, 2 * gp:2 * gp + 1]
    m1 = _dot(w1, vpair) + bst_ref[:, 2 * gp + 1:2 * gp + 2]
    return jnp.where(_first_head(), m0, m1), w0, w1


def _mixer_a_fwd(x, wa, sg, sb, ws, bst, mkv, wo, g, b, name, comm=None):
    S, D = x.shape
    DQ = mkv.shape[1] // 2
    DM = D - DQ
    NP = DM // PAIR
    tm = min(512, S)

    def body(x_ref, wa_ref, sg_ref, sb_ref, ws_ref, bst_ref, mkv_ref, wo_ref, g_ref, b_ref,
             xo_ref, xbt_ref, r_ref, catt_ref, xint_ref, z_s, zv_s, cat_ref):
        xv = x_ref[...]
        xint_ref[...] = xv.T.astype(BF16)
        z_s[...] = _dot(xv.astype(BF16), wa_ref[...])
        vhat, _ = _ln_stats(_gelu(z_s[:, DM:2 * DM]))
        zv_s[...] = (vhat * sg_ref[...] + sb_ref[...]).astype(BF16)
        for c in range(tm // CHUNK):
            rs = slice(c * CHUNK, (c + 1) * CHUNK)
            for gp in range(NP):
                ls = slice(gp * PAIR, (gp + 1) * PAIR)
                mixed, _, _ = _sgu_mixed(ws_ref, bst_ref, gp, zv_s[rs, ls])
                cat_ref[rs, ls] = (_gelu(z_s[rs, ls]) * mixed).astype(BF16)
        mo = _mem_attn_fwd(z_s[:, 2 * DM:].astype(BF16), mkv_ref, DQ)
        for mp in range(DQ // PAIR):
            cat_ref[:, DM + mp * PAIR:DM + (mp + 1) * PAIR] = mo[mp].astype(BF16)
        _tail_fwd_store(xv, cat_ref, wo_ref, g_ref, b_ref, xo_ref, xbt_ref, r_ref, catt_ref)

    return _pcall(
        body, name=name, grid=(S // tm,), comm=comm,
        out_shape=(jax.ShapeDtypeStruct((S, D), F32), jax.ShapeDtypeStruct((D, S), BF16),
                   jax.ShapeDtypeStruct((S, D), F32), jax.ShapeDtypeStruct((D, S), BF16),
                   jax.ShapeDtypeStruct((D, S), BF16)),
        in_specs=[_rows(tm, D), _full(wa.shape), _full(sg.shape), _full(sb.shape), _full(ws.shape), _full(bst.shape),
                  _full(mkv.shape), _full(wo.shape), _full((1, D)), _full((1, D))],
        out_specs=(_rows(tm, D), _cols(D, tm), _rows(tm, D), _cols(D, tm), _cols(D, tm)),
        scratch_shapes=[pltpu.VMEM((tm, 2 * DM + DQ), F32), pltpu.VMEM((tm, DM), BF16), pltpu.VMEM((tm, D), BF16)],
        args=(x, wa, sg, sb, ws, bst, mkv, wo, g, b))


def _mixer_a_bwd(x, dxo, r, wa, sg, sb, ws, bst, mkv, wo, g, name, comm=None):
    S, D = x.shape
    DQ = mkv.shape[1] // 2
    DM = D - DQ
    NP = DM // PAIR
    NG = DM // HEAD_DIM
    tm = min(256, S)

    def body(x_ref, dxo_ref, r_ref, wa_ref, sg_ref, sb_ref, ws_ref, bst_ref, mkv_ref, wo_ref, g_ref,
             dx_ref, dz_ref, drb_ref, dgb_ref, dsgb_ref, dws_ref, dbst_ref, dmkv_ref, z_s, zv_s, dzv_s):
        @pl.when(pl.program_id(0) == 0)
        def _():
            dgb_ref[...] = jnp.zeros_like(dgb_ref)
            dsgb_ref[...] = jnp.zeros_like(dsgb_ref)
            dws_ref[...] = jnp.zeros_like(dws_ref)
            dbst_ref[...] = jnp.zeros_like(dbst_ref)
            dmkv_ref[...] = jnp.zeros_like(dmkv_ref)

        dr, dcat = _tail_bwd_head(dxo_ref, r_ref, g_ref, wo_ref, dgb_ref, drb_ref)
        z_s[...] = _dot(x_ref[...].astype(BF16), wa_ref[...])
        vhat, vrstd = _ln_stats(_gelu(z_s[:, DM:2 * DM]))
        zv_s[...] = (vhat * sg_ref[...] + sb_ref[...]).astype(BF16)
        tril = _causal((CHUNK, CHUNK))
        lane = lax.broadcasted_iota(jnp.int32, (1, LANES), 1)
        for c in range(tm // CHUNK):
            rs = slice(c * CHUNK, (c + 1) * CHUNK)
            for gp in range(NP):
                ls = slice(gp * PAIR, (gp + 1) * PAIR)
                vpair = zv_s[rs, ls]
                mixed, w0, w1 = _sgu_mixed(ws_ref, bst_ref, gp, vpair)
                u_pre = z_s[rs, ls]
                dmix = dcat[rs, ls]
                dz_ref[rs, ls] = (dmix * mixed * _gelu_grad(u_pre)).astype(BF16)
                dmixed = dmix * _gelu(u_pre)
                first = _first_head()
                d0 = jnp.where(first, dmixed, 0.0)
                d1 = jnp.where(first, 0.0, dmixed)
                d0b = d0.astype(BF16)
                d1b = d1.astype(BF16)
                dzv_s[rs, ls] = _dot_tn(w0, d0b) + _dot_tn(w1, d1b)
                dws_ref[2 * gp] += jnp.where(tril, _dot_nt(d0b, vpair), 0.0)
                dws_ref[2 * gp + 1] += jnp.where(tril, _dot_nt(d1b, vpair), 0.0)
                dbst_ref[...] += (jnp.where(lane == 2 * gp, jnp.sum(d0, axis=-1, keepdims=True), 0.0)
                                  + jnp.where(lane == 2 * gp + 1, jnp.sum(d1, axis=-1, keepdims=True), 0.0))
        dzv = dzv_s[...]
        dsgb_ref[0:1, :] += jnp.sum(dzv * vhat, axis=0, keepdims=True)
        dsgb_ref[1:2, :] += jnp.sum(dzv, axis=0, keepdims=True)
        dgv = _ln_bwd(dzv, vhat, vrstd, sg_ref[...])
        dz_ref[:, DM:2 * DM] = (dgv * _gelu_grad(z_s[:, DM:2 * DM])).astype(BF16)
        dmo = [dcat[:, DM + mp * PAIR:DM + (mp + 1) * PAIR] for mp in range(DQ // PAIR)]
        dqm = _mem_attn_bwd(z_s[:, 2 * DM:].astype(BF16), mkv_ref, dmo, dmkv_ref, DQ)
        for mp in range(DQ // PAIR):
            dz_ref[:, 2 * DM + mp * PAIR:2 * DM + (mp + 1) * PAIR] = dqm[mp].astype(BF16)
        dx_ref[...] = ALPHA * dr + _dot_nt(dz_ref[...], wa_ref[...])

    ZW = 2 * DM + DQ
    M = mkv.shape[0]
    return _pcall(
        body, name=name, grid=(S // tm,), comm=comm,
        out_shape=(jax.ShapeDtypeStruct((S, D), F32), jax.ShapeDtypeStruct((S, ZW), BF16),
                   jax.ShapeDtypeStruct((S, D), BF16), jax.ShapeDtypeStruct((2, D), F32),
                   jax.ShapeDtypeStruct((2, DM), F32), jax.ShapeDtypeStruct((NG, CHUNK, CHUNK), F32),
                   jax.ShapeDtypeStruct((CHUNK, LANES), F32), jax.ShapeDtypeStruct((M, 2 * DQ), F32)),
        in_specs=[_rows(tm, D), _rows(tm, D), _rows(tm, D), _full(wa.shape), _full(sg.shape), _full(sb.shape),
                  _full(ws.shape), _full(bst.shape), _full(mkv.shape), _full(wo.shape), _full((1, D))],
        out_specs=(_rows(tm, D), _rows(tm, ZW), _rows(tm, D), _full((2, D)), _full((2, DM)),
                   _full((NG, CHUNK, CHUNK)), _full((CHUNK, LANES)), _full((M, 2 * DQ))),
        scratch_shapes=[pltpu.VMEM((tm, ZW), F32), pltpu.VMEM((tm, DM), BF16), pltpu.VMEM((tm, DM), F32)],
        args=(x, dxo, r, wa, sg, sb, ws, bst, mkv, wo, g))


def _proj_b_fwd(xb, wq, wkv, dm, name):
    S, D = xb.shape
    DQ = D - dm
    tm = min(512, S)

    def body(x_ref, wq_ref, wkv_ref, q_ref, qm_ref, k_ref, v_ref, fl_ref):
        xv = x_ref[...]
        z = _dot(xv, wq_ref[...])
        q_ref[...] = z[:, :dm].astype(BF16)
        qm_ref[...] = z[:, dm:].astype(BF16)
        kvf = _dot(xv, wkv_ref[...])
        k_ref[...] = kvf[:, :dm].astype(BF16)
        v_ref[...] = kvf[:, dm:2 * dm].astype(BF16)
        fl_ref[...] = kvf[:, 2 * dm:]

    return pl.pallas_call(
        body, name=name, grid=(S // tm,),
        out_shape=(jax.ShapeDtypeStruct((S, dm), BF16), jax.ShapeDtypeStruct((S, DQ), BF16),
                   jax.ShapeDtypeStruct((S, dm), BF16), jax.ShapeDtypeStruct((S, dm), BF16),
                   jax.ShapeDtypeStruct((S, LANES), F32)),
        in_specs=[_rows(tm, D), _full(wq.shape), _full(wkv.shape)],
        out_specs=(_rows(tm, dm), _rows(tm, DQ), _rows(tm, dm), _rows(tm, dm), _rows(tm, LANES)),
        compiler_params=_params(),
    )(xb, wq, wkv)


def _proj_b_bwd(dr, dqz, dkvf, wq, wkv, name):
    S, D = dr.shape
    tm = min(512, S)

    def body(dr_ref, dqz_ref, dkvf_ref, wq_ref, wkv_ref, dx_ref):
        dx_ref[...] = (ALPHA * dr_ref[...] + _dot_nt(dqz_ref[...], wq_ref[...])
                       + _dot_nt(dkvf_ref[...], wkv_ref[...]))

    return pl.pallas_call(
        body, name=name, grid=(S // tm,), out_shape=jax.ShapeDtypeStruct((S, D), F32),
        in_specs=[_rows(tm, D), _rows(tm, dqz.shape[1]), _rows(tm, dkvf.shape[1]), _full(wq.shape), _full(wkv.shape)],
        out_specs=_rows(tm, D), compiler_params=_params(),
    )(dr, dqz, dkvf, wq, wkv)


def _split3(v):
    hi = v.astype(BF16)
    r1 = v - hi.astype(F32)
    mid = r1.astype(BF16)
    lo = (r1 - mid.astype(F32)).astype(BF16)
    return hi, mid, lo


def _tri_sum(v, tri_b):
    hi, mid, lo = _split3(v)
    return _dot(hi, tri_b) + _dot(mid, tri_b) + _dot(lo, tri_b)


def _log_sigmoid(x):
    return jnp.minimum(x, 0.0) - jnp.log(1.0 + jnp.exp(-jnp.abs(x)))


def _forget_cumsum(flt, bf, name):
    H, S = flt.shape
    nchunk = S // LANES

    def body(fl_ref, bf_ref, p_ref):
        upper = (lax.broadcasted_iota(jnp.int32, (LANES, LANES), 0)
                 <= lax.broadcasted_iota(jnp.int32, (LANES, LANES), 1)).astype(BF16)

        def step(n, carry):
            s0 = pl.multiple_of(n * LANES, LANES)
            lf = _log_sigmoid(fl_ref[:, pl.ds(s0, LANES)] + bf_ref[...])
            cs = _tri_sum(lf, upper) + carry
            for i, piece in enumerate(_split3(-cs)):
                p_ref[i, :, pl.ds(s0, LANES)] = piece.astype(F32)
            return cs[:, LANES - 1:LANES]

        lax.fori_loop(0, nchunk, step, jnp.zeros((H, 1), F32))

    return pl.pallas_call(body, name=name, out_shape=jax.ShapeDtypeStruct((3, H, S), F32),
                          compiler_params=pltpu.CompilerParams(vmem_limit_bytes=VMEM_LIMIT))(flt, bf)


def _forget_cumsum_bwd(dct, flt, bf, name):
    H, S = flt.shape
    nchunk = S // LANES

    def body(dc_ref, fl_ref, bf_ref, dfl_ref, dbf_ref):
        lower = (lax.broadcasted_iota(jnp.int32, (LANES, LANES), 0)
                 >= lax.broadcasted_iota(jnp.int32, (LANES, LANES), 1)).astype(BF16)

        def step(n, carry):
            tail, tot = carry
            s0 = pl.multiple_of((nchunk - 1 - n) * LANES, LANES)
            suffix = _tri_sum(dc_ref[:, pl.ds(s0, LANES)], lower) + tail
            xv = fl_ref[:, pl.ds(s0, LANES)] + bf_ref[...]
            dfl = suffix * (1.0 / (1.0 + jnp.exp(xv)))
            dfl_ref[:, pl.ds(s0, LANES)] = dfl
            return suffix[:, 0:1], tot + jnp.sum(dfl, axis=-1, keepdims=True)

        _, tot = lax.fori_loop(0, nchunk, step, (jnp.zeros((H, 1), F32), jnp.zeros((H, 1), F32)))
        dbf_ref[...] = jnp.broadcast_to(tot, (H, LANES))

    return pl.pallas_call(body, name=name,
                          out_shape=(jax.ShapeDtypeStruct((H, S), F32), jax.ShapeDtypeStruct((H, LANES), F32)),
                          compiler_params=pltpu.CompilerParams(vmem_limit_bytes=VMEM_LIMIT))(dct, flt, bf)


N_BIAS = 3


def _bias_ones():
    lane = lax.broadcasted_iota(jnp.int32, (1, PAIR), 1)
    return ((lane & (HEAD_DIM - 1)) < N_BIAS).astype(BF16)


def _fox_fwd(q, k, v, ca, name, comm=None):
    S, DM = q.shape
    NP = DM // PAIR
    T = min(FOX_BLOCK, S)
    nq = S // T

    def body(q_ref, k_ref, v_ref, ca_ref, o_ref, lse_ref):
        tril = _causal((T, T))
        heads = (_lane_mask(0), _lane_mask(1))
        ones3 = _bias_ones()

        def step(carry, qaug, kaug, vaug, masked):
            m, acc = carry
            s = _dot_nt(qaug, kaug)
            if masked:
                s = jnp.where(tril, s, NEG)
            m_new = jnp.maximum(m, jnp.max(s, axis=-1, keepdims=True))
            p = jnp.exp(s - m_new).astype(BF16)
            return m_new, jnp.exp(m - m_new) * acc + _dot(p, vaug)

        def kv_block(kj):
            s0 = pl.multiple_of(kj * T, T)
            kb, vb, cab = k_ref[pl.ds(s0, T), :], v_ref[pl.ds(s0, T), :], ca_ref[pl.ds(s0, T), :]
            return ([jnp.where(heads[e], kb, cab) for e in (0, 1)],
                    [jnp.where(heads[e], vb, jnp.ones_like(vb)) for e in (0, 1)])

        def q_pair(a, _):
            t0s = [pl.multiple_of((2 * a + r) * T, T) for r in (0, 1)]
            qaug = {}
            for r in (0, 1):
                qb = q_ref[pl.ds(t0s[r], T), :] * QK_SCALE
                for e in (0, 1):
                    qaug[r, e] = jnp.where(heads[e], qb, ones3)
            chains = [(r, e) for r in (0, 1) for e in (0, 1)]

            def full(first_block, n_blocks, carries):
                carries = list(carries)
                for kk in range(n_blocks):
                    kaug, vaug = kv_block(first_block + kk)
                    scores = [_dot_nt(qaug[r, e], kaug[e]) for r, e in chains]
                    for n, (r, e) in enumerate(chains):
                        m, acc = carries[n]
                        m_new = jnp.maximum(m, jnp.max(scores[n], axis=-1, keepdims=True))
                        p = jnp.exp(scores[n] - m_new).astype(BF16)
                        carries[n] = (m_new, jnp.exp(m - m_new) * acc + _dot(p, vaug[e]))
                return tuple(carries)

            init = (jnp.full((T, 1), NEG, F32), jnp.zeros((T, PAIR), F32))
            carries = lax.fori_loop(0, a // 2, lambda i, c: full(4 * i, 4, c), (init,) * 4)
            carries = list(lax.fori_loop(0, a % 2, lambda i, c: full(2 * a - 2, 2, c), carries))
            kaug, vaug = kv_block(2 * a)
            for i, (r, e) in enumerate(chains):
                carries[i] = step(carries[i], qaug[r, e], kaug[e], vaug[e], r == 0)
            kaug, vaug = kv_block(2 * a + 1)
            for i, (r, e) in enumerate(chains):
                if r == 1:
                    carries[i] = step(carries[i], qaug[r, e], kaug[e], vaug[e], True)
            first = _first_head()
            for r in (0, 1):
                (m0, acc0), (m1, acc1) = carries[2 * r], carries[2 * r + 1]
                l0, l1 = acc0[:, HEAD_DIM:HEAD_DIM + 1], acc1[:, 0:1]
                o_ref[pl.ds(t0s[r], T), :] = jnp.where(first, acc0 / l0, acc1 / l1).astype(BF16)
                lse_t = jnp.where(first, m0 + jnp.log(l0), m1 + jnp.log(l1)).T
                lse_ref[0:1, pl.ds(t0s[r], T)] = lse_t[0:1, :]
                lse_ref[1:2, pl.ds(t0s[r], T)] = lse_t[HEAD_DIM:HEAD_DIM + 1, :]
            return 0

        lax.fori_loop(0, nq // 2, q_pair, 0)

    col = pl.BlockSpec((S, PAIR), lambda hp: (0, hp))
    return _pcall(
        body, name=name, grid=(NP,), comm=comm,
        out_shape=(jax.ShapeDtypeStruct((S, DM), BF16), jax.ShapeDtypeStruct((NP, 2, S), F32)),
        in_specs=[col, col, col, col], out_specs=(col, pl.BlockSpec((None, 2, S), lambda hp: (hp, 0, 0))),
        args=(q, k, v, ca))


def _fox_bwd(q, k, v, o, do, ca, lse, name, comm=None):
    S, DM = q.shape
    NP = DM // PAIR
    T = min(FOX_BLOCK, S)
    nq = S // T

    def body(q_ref, k_ref, v_ref, o_ref, do_ref, ca_ref, lse_ref, dq_ref, dk_ref, dv_ref, dc_ref,
             dq_acc, dcol_acc, dk_acc, dv_acc, dd_s):
        keep = lax.broadcasted_iota(jnp.int32, (T, T), 0) <= lax.broadcasted_iota(jnp.int32, (T, T), 1)
        lane = lax.broadcasted_iota(jnp.int32, (1, LANES), 1)
        heads = (_lane_mask(0), _lane_mask(1))
        ones3 = _bias_ones()
        dq_acc[...] = jnp.zeros_like(dq_acc)
        dcol_acc[...] = jnp.zeros_like(dcol_acc)
        dc_ref[...] = jnp.zeros_like(dc_ref)

        def row_dots(qi, _):
            t0 = pl.multiple_of(qi * T, T)
            prod = do_ref[pl.ds(t0, T), :].astype(F32) * o_ref[pl.ds(t0, T), :].astype(F32)
            cols = [jnp.sum(jnp.where(heads[e], prod, 0.0), axis=-1, keepdims=True) for e in (0, 1)]
            tile = jnp.where(lane == 0, cols[0], jnp.where(lane == 1, cols[1], 0.0))
            dd_s[:, pl.ds(t0, T)] = tile.T[0:8, :]
            return 0

        lax.fori_loop(0, nq, row_dots, 0)

        def kv_pair(b, _):
            dk_acc[...] = jnp.zeros_like(dk_acc)
            dv_acc[...] = jnp.zeros_like(dv_acc)
            s0s = [pl.multiple_of((2 * b + jj) * T, T) for jj in (0, 1)]
            kaug, ks, vbs = {}, {}, []
            for jj in (0, 1):
                kb = k_ref[pl.ds(s0s[jj], T), :]
                cab = ca_ref[pl.ds(s0s[jj], T), :]
                vbs.append(v_ref[pl.ds(s0s[jj], T), :])
                for e in (0, 1):
                    kaug[jj, e] = jnp.where(heads[e], kb, cab)
                    ks[jj, e] = jnp.where(heads[e], kb, jnp.zeros_like(kb)) * QK_SCALE

            def q_step(qi, blocks):
                t0 = pl.multiple_of(qi * T, T)
                qs = q_ref[pl.ds(t0, T), :] * QK_SCALE
                dob = do_ref[pl.ds(t0, T), :]
                doh = [jnp.where(heads[e], dob, jnp.zeros_like(dob)) for e in (0, 1)]
                qh = [jnp.where(heads[e], qs, jnp.zeros_like(qs)) for e in (0, 1)]
                chains = [(e, jj, masked) for e in (0, 1) for jj, masked in blocks]
                scores = [_dot_nt(kaug[jj, e], jnp.where(heads[e], qs, ones3)) for e, jj, _ in chains]
                dps = [_dot_nt(vbs[jj], doh[e]) for e, jj, _ in chains]
                dq = jnp.zeros((T, PAIR), F32)
                drow = [jnp.zeros((1, T), F32), jnp.zeros((1, T), F32)]
                for n, (e, jj, masked) in enumerate(chains):
                    st = jnp.where(keep, scores[n], NEG) if masked else scores[n]
                    pt = jnp.exp(st - lse_ref[e:e + 1, pl.ds(t0, T)])
                    dv_acc[jj] += _dot(pt.astype(BF16), doh[e])
                    dst = pt * (dps[n] - dd_s[e:e + 1, pl.ds(t0, T)])
                    dsb = dst.astype(BF16)
                    dk_acc[jj] += _dot(dsb, qh[e])
                    dq = dq + _dot_tn(dsb, ks[jj, e])
                    dcol_acc[pl.ds(s0s[jj], T), :] -= jnp.where(lane == e, jnp.sum(dst, axis=-1, keepdims=True), 0.0)
                    drow[e] = drow[e] + jnp.sum(dst, axis=0, keepdims=True)
                for e in (0, 1):
                    dc_ref[e:e + 1, pl.ds(t0, T)] += drow[e]
                dq_acc[pl.ds(t0, T), :] += dq

            q_step(2 * b, [(0, True)])
            q_step(2 * b + 1, [(0, False), (1, True)])

            def rest(i, _):
                q_step(2 * b + 2 + 2 * i, [(0, False), (1, False)])
                q_step(2 * b + 3 + 2 * i, [(0, False), (1, False)])
                return 0

            lax.fori_loop(0, (nq - 2) // 2 - b, rest, 0)
            for jj in (0, 1):
                dk_ref[pl.ds(s0s[jj], T), :] = dk_acc[jj].astype(BF16)
                dv_ref[pl.ds(s0s[jj], T), :] = dv_acc[jj].astype(BF16)
            return 0

        lax.fori_loop(0, nq // 2, kv_pair, 0)
        dq_ref[...] = dq_acc[...].astype(BF16)

        def add_cols(kj, _):
            s0 = pl.multiple_of(kj * T, T)
            dc_ref[:, pl.ds(s0, T)] += dcol_acc[pl.ds(s0, T), :].T[0:2, :]
            return 0

        lax.fori_loop(0, nq, add_cols, 0)

    col = pl.BlockSpec((S, PAIR), lambda hp: (0, hp))
    row2 = pl.BlockSpec((None, 2, S), lambda hp: (hp, 0, 0))
    return _pcall(
        body, name=name, grid=(NP,), comm=comm,
        out_shape=(jax.ShapeDtypeStruct((S, DM), BF16), jax.ShapeDtypeStruct((S, DM), BF16),
                   jax.ShapeDtypeStruct((S, DM), BF16), jax.ShapeDtypeStruct((NP, 2, S), F32)),
        in_specs=[col, col, col, col, col, col, row2], out_specs=(col, col, col, row2),
        scratch_shapes=[pltpu.VMEM((S, PAIR), F32), pltpu.VMEM((S, LANES), F32), pltpu.VMEM((2, T, PAIR), F32),
                        pltpu.VMEM((2, T, PAIR), F32), pltpu.VMEM((8, S), F32)],
        args=(q, k, v, o, do, ca, lse))


def _mixer_b_fwd(x, o, qm, mkv, wo, g, b, name):
    S, D = x.shape
    DQ = qm.shape[1]
    DM = D - DQ
    tm = min(512, S)

    def body(x_ref, o_ref, qm_ref, mkv_ref, wo_ref, g_ref, b_ref, xo_ref, xbt_ref, r_ref, catt_ref, cat_ref):
        cat_ref[:, :DM] = o_ref[...]
        mo = _mem_attn_fwd(qm_ref[...], mkv_ref, DQ)
        for mp in range(DQ // PAIR):
            cat_ref[:, DM + mp * PAIR:DM + (mp + 1) * PAIR] = mo[mp].astype(BF16)
        _tail_fwd_store(x_ref[...], cat_ref, wo_ref, g_ref, b_ref, xo_ref, xbt_ref, r_ref, catt_ref)

    return pl.pallas_call(
        body, name=name, grid=(S // tm,),
        out_shape=(jax.ShapeDtypeStruct((S, D), F32), jax.ShapeDtypeStruct((D, S), BF16),
                   jax.ShapeDtypeStruct((S, D), F32), jax.ShapeDtypeStruct((D, S), BF16)),
        in_specs=[_rows(tm, D), _rows(tm, DM), _rows(tm, DQ), _full(mkv.shape), _full(wo.shape), _full((1, D)),
                  _full((1, D))],
        out_specs=(_rows(tm, D), _cols(D, tm), _rows(tm, D), _cols(D, tm)),
        scratch_shapes=[pltpu.VMEM((tm, D), BF16)],
        compiler_params=_params(),
    )(x, o, qm, mkv, wo, g, b)


def _mixer_b_bwd(dxo, r, qm, mkv, wo, g, name):
    S, D = r.shape
    DQ = qm.shape[1]
    DM = D - DQ
    M = mkv.shape[0]
    tm = min(512, S)

    def body(dxo_ref, r_ref, qm_ref, mkv_ref, wo_ref, g_ref, dr_ref, drb_ref, dcat_ref, dgb_ref, dmkv_ref):
        @pl.when(pl.program_id(0) == 0)
        def _():
            dgb_ref[...] = jnp.zeros_like(dgb_ref)
            dmkv_ref[...] = jnp.zeros_like(dmkv_ref)

        dr, dcat = _tail_bwd_head(dxo_ref, r_ref, g_ref, wo_ref, dgb_ref, drb_ref)
        dr_ref[...] = dr
        dcat_ref[:, :DM] = dcat[:, :DM].astype(BF16)
        dmo = [dcat[:, DM + mp * PAIR:DM + (mp + 1) * PAIR] for mp in range(DQ // PAIR)]
        dqm = _mem_attn_bwd(qm_ref[...], mkv_ref, dmo, dmkv_ref, DQ)
        for mp in range(DQ // PAIR):
            dcat_ref[:, DM + mp * PAIR:DM + (mp + 1) * PAIR] = dqm[mp].astype(BF16)

    return pl.pallas_call(
        body, name=name, grid=(S // tm,),
        out_shape=(jax.ShapeDtypeStruct((S, D), F32), jax.ShapeDtypeStruct((S, D), BF16),
                   jax.ShapeDtypeStruct((S, D), BF16), jax.ShapeDtypeStruct((2, D), F32),
                   jax.ShapeDtypeStruct((M, 2 * DQ), F32)),
        in_specs=[_rows(tm, D), _rows(tm, D), _rows(tm, DQ), _full(mkv.shape), _full(wo.shape), _full((1, D))],
        out_specs=(_rows(tm, D), _rows(tm, D), _rows(tm, D), _full((2, D)), _full((M, 2 * DQ))),
        compiler_params=_params(),
    )(dxo, r, qm, mkv, wo, g)


class _NoExchange:
    def plan(self, stage, w, grads):
        return None

    def done(self, stage, results, w):
        pass


def _local_step(x, mem, target, w, hooks):
    S, D = x.shape
    DQ = w["mkv0"].shape[1] // 2
    DM = D - DQ
    NG = DM // HEAD_DIM
    NP = DM // PAIR
    ln_g, ln_b = w["ln_g"], w["ln_b"]
    row = lambda a, i: a[i:i + 1]
    memb = mem.astype(BF16)
    bst =jnp.pad(w["b_s"].T, ((0, 0), (0, LANES - NG)))
    grads = {}

    def staged(stage, fn, *args):
        out, got = fn(*args, stage, comm=hooks.plan(stage, w, grads))
        hooks.done(stage, got, w)
        return out

    mkv0 = _mm_small(memb, w["mkv0"], "mem_kv_0")
    x1, x1t, r1, cat0t, x0t = staged("mixer_a_fwd", _mixer_a_fwd, x, w["wa"], w["sg"], w["sb"], w["ws"], bst, mkv0,
                                w["wo0"], row(ln_g, 0), row(ln_b, 0))
    x2, x2b, x2t, r2, h0 = staged("mlp_fwd_0", _mlp_fwd, x1, w["wup0"], w["wdown0"], row(ln_g, 1), row(ln_b, 1))
    q, qm, k, v, fl = _proj_b_fwd(x2b, w["wq"], w["wkv"], DM, "proj_b_fwd")
    flt = fl[:, :16].T
    bfc = jnp.pad(w["bf"], (0, 16 - NG)).reshape(16, 1)
    pieces = _forget_cumsum(flt, bfc, "forget_cumsum")
    pieces = jnp.pad(pieces[:, :NG].transpose(2, 1, 0), ((0, 0), (0, 0), (0, HEAD_DIM - N_BIAS)))
    ca = pieces.reshape(S, NP, 2, HEAD_DIM)[:, :, ::-1].reshape(S, DM).astype(BF16)
    o, lse = staged("fox_fwd", _fox_fwd, q, k, v, ca)
    mkv1 = _mm_small(memb, w["mkv1"], "mem_kv_1")
    x3, x3t, r3, cat1t = _mixer_b_fwd(x2, o, qm, mkv1, w["wo1"], row(ln_g, 2), row(ln_b, 2), "mixer_b_fwd")
    (x4, _, _, r4, h1), _ = _mlp_fwd(x3, w["wup1"], w["wdown1"], row(ln_g, 3), row(ln_b, 3), "mlp_fwd_1")

    nc = w["wup0"].shape[0]
    dx4, loss_part = _loss_grad(x4, target, "loss_grad")
    (dx3, dr4t, dpre1, dgb11), _ = _mlp_bwd(dx4, r4, h1, w["wup1"], w["wdown1"], row(ln_g, 3), "mlp_bwd_1")
    grads["wdown1"] = _mm_nn(dr4t, h1, "dw_down_1", out_split=nc)
    grads["wup1"] = _mm_nn(x3t, dpre1, "dw_up_1", out_split=nc)
    dr3, dr3b, dcat1, dgb10, dmkv1 = _mixer_b_bwd(dx3, r3, qm, mkv1, w["wo1"], row(ln_g, 2), "mixer_b_bwd")
    do = dcat1[:, :DM]
    dq, dk, dv, dct3 = staged("fox_bwd", _fox_bwd, q, k, v, o, do, ca, lse)
    dct = jnp.pad(dct3.reshape(NG, S), ((0, 16 - NG), (0, 0)))
    dflt, dbf = _forget_cumsum_bwd(dct, flt, bfc, "forget_cumsum_bwd")
    dfl = jnp.pad(dflt.T, ((0, 0), (0, LANES - 16))).astype(BF16)
    dqz = jnp.concatenate([dq, dcat1[:, DM:]], axis=1)
    dkvf = jnp.concatenate([dk, dv, dfl], axis=1)
    dx2 = _proj_b_bwd(dr3, dqz, dkvf, w["wq"], w["wkv"], "proj_b_bwd")
    grads["wo1"] = _mm_nn(cat1t, dr3b, "dw_o_1")
    grads["wq"] = _mm_nn(x2t, dqz, "dw_q")
    grads["wkv"] = _mm_nn(x2t, dkvf, "dw_kv")
    grads["mkv1"] = _mm_tn(memb, dmkv1.astype(BF16), "dw_mkv_1")
    dx1, dr2t, dpre0, dgb01 = staged("mlp_bwd_0", _mlp_bwd, dx2, r2, h0, w["wup0"], w["wdown0"], row(ln_g, 1))
    grads["wdown0"] = _mm_nn(dr2t, h0, "dw_down_0", out_split=nc)
    grads["wup0"] = _mm_nn(x1t, dpre0, "dw_up_0", out_split=nc)
    dx0, dz, dr1b, dgb00, dsgb, dws, dbst, dmkv0 = staged(
        "mixer_a_bwd", _mixer_a_bwd, x, dx1, r1, w["wa"], w["sg"], w["sb"], w["ws"], bst, mkv0, w["wo0"], row(ln_g, 0))
    grads["wo0"] = _mm_nn(cat0t, dr1b, "dw_o_0")
    grads["wa"] = _mm_nn(x0t, dz, "dw_a")
    grads["mkv0"] = _mm_tn(memb, dmkv0.astype(BF16), "dw_mkv_0")
    grads.update({
        "ws": dws, "b_s": dbst[:, :NG].T, "sg": dsgb[0:1], "sb": dsgb[1:2], "bf": dbf[:NG, 0],
        "ln_g": jnp.concatenate([dgb00[0:1], dgb01[0:1], dgb10[0:1], dgb11[0:1]], axis=0),
        "ln_b": jnp.concatenate([dgb00[1:2], dgb01[1:2], dgb10[1:2], dgb11[1:2]], axis=0),
    })
    return loss_part, dx0, grads


def _place():
    xi, yi, ci = lax.axis_index("x"), lax.axis_index("y"), lax.axis_index("c")
    peers = [(xi, 1 - yi), (1 - xi, yi), (1 - xi, 1 - yi)]
    return xi, yi, ci, peers


def _comm_call(body, name, ins, out_shapes, n_remote, n_local):
    return pl.pallas_call(
        body, name=name, out_shape=out_shapes, in_specs=[ANY] * len(ins), out_specs=[ANY] * len(out_shapes),
        scratch_shapes=[pltpu.SemaphoreType.DMA((n_remote,)), pltpu.SemaphoreType.DMA((n_remote,)),
                        pltpu.SemaphoreType.DMA((max(n_local, 1),))],
        compiler_params=pltpu.CompilerParams(has_side_effects=True),
    )(*ins)


def _chip_allgather(xs, halved, name):
    n = len(xs)
    out_shapes = [jax.ShapeDtypeStruct((N_CHIPS,) + (a.shape[1:] if hv else a.shape), a.dtype)
                  for a, hv in zip(xs, halved)]

    def body(*refs):
        x_refs, o_refs = refs[:n], refs[n:2 * n]
        send_sems, recv_sems, loc_sems = refs[2 * n:]
        xi, yi, ci, peers = _place()
        j = 2 * xi + yi
        sends, locs = [], []
        for i in range(n):
            src = x_refs[i].at[ci] if halved[i] else x_refs[i]
            loc = pltpu.make_async_copy(src, o_refs[i].at[j], loc_sems.at[i])
            loc.start()
            locs.append(loc)
            for d, (px, py) in enumerate(peers):
                cp = pltpu.make_async_remote_copy(src_ref=src, dst_ref=o_refs[i].at[j], send_sem=send_sems.at[3 * i + d],
                                                  recv_sem=recv_sems.at[3 * i + d], device_id=(px, py, ci),
                                                  device_id_type=MESH)
                cp.start()
                sends.append(cp)
        for i in range(n):
            src = x_refs[i].at[ci] if halved[i] else x_refs[i]
            for d, (px, py) in enumerate(peers):
                pltpu.make_async_remote_copy(src_ref=src, dst_ref=o_refs[i].at[2 * px + py],
                                             send_sem=send_sems.at[3 * i + d], recv_sem=recv_sems.at[3 * i + d],
                                             device_id=(px, py, ci), device_id_type=MESH).wait_recv()
        for cp in sends:
            cp.wait_send()
        for loc in locs:
            loc.wait()

    return _comm_call(body, name, xs, out_shapes, 3 * n, n)


def _pair_swap(xs, name):
    n = len(xs)
    out_shapes = [jax.ShapeDtypeStruct((a.shape[0],) + a.shape[2:], a.dtype) for a in xs]

    def body(*refs):
        x_refs, o_refs = refs[:n], refs[n:2 * n]
        send_sems, recv_sems, _ = refs[2 * n:]
        xi, yi, ci, _ = _place()
        cps = []
        for i in range(n):
            cp = pltpu.make_async_remote_copy(src_ref=x_refs[i].at[:, 1 - ci], dst_ref=o_refs[i],
                                              send_sem=send_sems.at[i], recv_sem=recv_sems.at[i],
                                              device_id=(xi, yi, 1 - ci), device_id_type=MESH)
            cp.start()
            cps.append(cp)
        for cp in cps:
            cp.wait_recv()
        for cp in cps:
            cp.wait_send()

    return _comm_call(body, name, xs, out_shapes, n, 0)


def _chip_exchange(xs):
    n = len(xs)
    out_shapes = [jax.ShapeDtypeStruct((3,) + a.shape[1:], a.dtype) for a in xs]

    def copies(x_refs, o_refs, sems):
        send_sems, recv_sems = sems
        xi, yi, ci, peers = _place()
        return [pltpu.make_async_remote_copy(src_ref=x_refs[i].at[2 * px + py], dst_ref=o_refs[i].at[d],
                                             send_sem=send_sems.at[3 * i + d], recv_sem=recv_sems.at[3 * i + d],
                                             device_id=(px, py, ci), device_id_type=MESH)
                for i in range(n) for d, (px, py) in enumerate(peers)]

    def start(x_refs, o_refs, sems):
        for cp in copies(x_refs, o_refs, sems):
            cp.start()

    def finish(x_refs, o_refs, sems):
        cps = copies(x_refs, o_refs, sems)
        for cp in cps:
            cp.wait_recv()
        for cp in cps:
            cp.wait_send()

    return _Comm(xs, out_shapes, {}, [3 * n, 3 * n], start, finish)


def _inplace_call(body, name, bufs, sem_counts):
    n = len(bufs)
    return pl.pallas_call(
        body, name=name, out_shape=[jax.ShapeDtypeStruct(a.shape, a.dtype) for a in bufs],
        in_specs=[ANY] * n, out_specs=[ANY] * n, input_output_aliases={i: i for i in range(n)},
        scratch_shapes=[pltpu.SemaphoreType.DMA((k,)) for k in sem_counts],
        compiler_params=pltpu.CompilerParams(has_side_effects=True),
    )(*bufs)


def _gather_weights(bufs, whole=()):
    n, nw = len(bufs), len(whole)

    def ici(w, sems, i, d, px, py, ci, slot):
        ref = w[i].at[slot, ci] if i < n else w[i].at[slot]
        return pltpu.make_async_remote_copy(src_ref=ref, dst_ref=ref, send_sem=sems[0].at[3 * i + d],
                                            recv_sem=sems[1].at[3 * i + d], device_id=(px, py, ci), device_id_type=MESH)

    def d2d(w, sems, i, d, xi, yi, ci, slot, half):
        ref = w[i].at[slot, half]
        return pltpu.make_async_remote_copy(src_ref=ref, dst_ref=ref, send_sem=sems[2].at[3 * i + d],
                                            recv_sem=sems[3].at[3 * i + d], device_id=(xi, yi, 1 - ci),
                                            device_id_type=MESH)

    def start(_, w, sems):
        xi, yi, ci, peers = _place()
        for i in range(n + nw):
            for d, (px, py) in enumerate(peers):
                ici(w, sems, i, d, px, py, ci, 2 * xi + yi).start()

    def finish(_, w, sems):
        xi, yi, ci, peers = _place()
        for d, (px, py) in enumerate(peers):
            for i in range(n + nw):
                ici(w, sems, i, d, px, py, ci, 2 * px + py).wait_recv()
                if i < n:
                    d2d(w, sems, i, d, xi, yi, ci, 2 * px + py, ci).start()
        for d, (px, py) in enumerate(peers):
            for i in range(n):
                d2d(w, sems, i, d, xi, yi, ci, 2 * px + py, 1 - ci).wait_recv()
        for d, (px, py) in enumerate(peers):
            for i in range(n + nw):
                ici(w, sems, i, d, px, py, ci, 2 * xi + yi).wait_send()
                if i < n:
                    d2d(w, sems, i, d, xi, yi, ci, 2 * px + py, ci).wait_send()

    arrays = list(bufs) + list(whole)
    k = 3 * (n + nw)
    return _Comm(arrays, [jax.ShapeDtypeStruct(a.shape, a.dtype) for a in arrays], {i: i for i in range(n + nw)},
                 [k, k, max(3 * n, 1), max(3 * n, 1)], start, finish)


def _pair_gather(bufs, name):
    n = len(bufs)
    where = [(i, l) for i, a in enumerate(bufs) for l in range(a.shape[0])]

    def body(*refs):
        g = refs[n:2 * n]
        send_sems, recv_sems = refs[2 * n:]
        xi, yi, ci, _ = _place()
        sends = []
        for k, (i, l) in enumerate(where):
            mine = g[i].at[l, :, ci]
            cp = pltpu.make_async_remote_copy(src_ref=mine, dst_ref=mine, send_sem=send_sems.at[k],
                                              recv_sem=recv_sems.at[k], device_id=(xi, yi, 1 - ci), device_id_type=MESH)
            cp.start()
            sends.append(cp)
        for k, (i, l) in enumerate(where):
            other = g[i].at[l, :, 1 - ci]
            pltpu.make_async_remote_copy(src_ref=other, dst_ref=other, send_sem=send_sems.at[k],
                                         recv_sem=recv_sems.at[k], device_id=(xi, yi, 1 - ci),
                                         device_id_type=MESH).wait_recv()
        for cp in sends:
            cp.wait_send()

    return _inplace_call(body, name, list(bufs), [len(where), len(where)])


def _row_block(rows, cols):
    want = max(8, (2 ** 18 // max(cols, 1)) // 8 * 8)
    if rows <= want:
        return rows
    best = 8
    for t in range(8, want + 1, 8):
        if rows % t == 0:
            best = t
    return best


def _pair_add(g4, recv, sel, out_dtype, name):
    A, _, H, C = g4.shape
    bh = _row_block(H, C)

    def body(sel_ref, g_ref, r_ref, o_ref):
        o_ref[...] = (g_ref[...].astype(F32) + r_ref[...].astype(F32)).astype(out_dtype)

    return pl.pallas_call(
        body, name=name, out_shape=jax.ShapeDtypeStruct((A, H, C), out_dtype),
        grid_spec=pltpu.PrefetchScalarGridSpec(
            num_scalar_prefetch=1, grid=(A, H // bh),
            in_specs=[pl.BlockSpec((None, None, bh, C), lambda a, i, s: (a, s[1], i, 0)),
                      pl.BlockSpec((None, bh, C), lambda a, i, s: (a, i, 0))],
            out_specs=pl.BlockSpec((None, bh, C), lambda a, i, s: (a, i, 0))),
        compiler_params=pltpu.CompilerParams(dimension_semantics=("arbitrary", "arbitrary"),
                                             vmem_limit_bytes=VMEM_LIMIT),
    )(sel, g4, recv)


def _chip_reduce(g4, recv1, recv2, sel, buf, layer, n_layers, name):
    _, _, H, C = g4.shape
    bh = _row_block(H, C)

    def body(sel_ref, g_ref, r1_ref, r2_ref, *rest):
        acc = g_ref[...].astype(F32) + r1_ref[...].astype(F32)
        for d in range(3):
            acc = acc + r2_ref[d].astype(F32)
        rest[-1][...] = acc

    in_specs = [pl.BlockSpec((None, None, bh, C), lambda i, s: (s[0], s[1], i, 0)),
                pl.BlockSpec((None, bh, C), lambda i, s: (s[0], i, 0)),
                pl.BlockSpec((3, bh, C), lambda i, s: (0, i, 0))]
    args = [sel, g4, recv1, recv2]
    aliases = {}
    if buf is not None:
        in_specs.append(ANY)
        args.append(buf)
        aliases = {4: 0}
    return pl.pallas_call(
        body, name=name, out_shape=jax.ShapeDtypeStruct((n_layers, 1, 2, H, C), F32),
        grid_spec=pltpu.PrefetchScalarGridSpec(
            num_scalar_prefetch=1, grid=(H // bh,), in_specs=in_specs,
            out_specs=pl.BlockSpec((None, None, None, bh, C), lambda i, s: (layer, 0, s[1], i, 0))),
        input_output_aliases=aliases,
        compiler_params=pltpu.CompilerParams(dimension_semantics=("arbitrary",), vmem_limit_bytes=VMEM_LIMIT),
    )(*args)


def _cast_place(shard2, sel, name):
    _, H, C = shard2.shape
    bh = _row_block(H, C)

    def body(sel_ref, x_ref, o_ref):
        o_ref[...] = x_ref[...].astype(BF16)

    return pl.pallas_call(
        body, name=name, out_shape=jax.ShapeDtypeStruct((N_CHIPS, 2, H, C), BF16),
        grid_spec=pltpu.PrefetchScalarGridSpec(
            num_scalar_prefetch=1, grid=(2, H // bh),
            in_specs=[pl.BlockSpec((None, bh, C), lambda h, i, s: (h, i, 0))],
            out_specs=pl.BlockSpec((None, None, bh, C), lambda h, i, s: (s[0], h, i, 0))),
        compiler_params=pltpu.CompilerParams(dimension_semantics=("arbitrary", "arbitrary"),
                                             vmem_limit_bytes=VMEM_LIMIT),
    )(sel, shard2)


def _adamw(g, w, m, v, name):
    R, C = g.shape
    br = _row_block(R, C)

    def body(g_ref, w_ref, m_ref, v_ref, d_ref, mo_ref, vo_ref):
        gv = g_ref[...]
        mn = ADAM_B1 * m_ref[...] + (1.0 - ADAM_B1) * gv
        vn = ADAM_B2 * v_ref[...] + (1.0 - ADAM_B2) * (gv * gv)
        m_hat = mn / (1.0 - ADAM_B1 ** ADAM_STEP)
        v_hat = vn / (1.0 - ADAM_B2 ** ADAM_STEP)
        d_ref[...] = -ADAM_LR * (m_hat / (jnp.sqrt(v_hat) + ADAM_EPS) + ADAM_WD * w_ref[...])
        mo_ref[...] = mn
        vo_ref[...] = vn

    spec = pl.BlockSpec((br, C), lambda i: (i, 0))
    return pl.pallas_call(
        body, name=name, grid=(R // br,), out_shape=(jax.ShapeDtypeStruct((R, C), F32),) * 3,
        in_specs=[spec] * 4, out_specs=(spec,) * 3, compiler_params=_params(),
    )(g, w, m, v)


def _part_rows(shape):
    return -(-math.prod(shape) // (8 * LANES)) * 8


def _pack_rows(arrs, total_rows):
    parts = []
    for a in arrs:
        flat = a.reshape(-1)
        rows = _part_rows(a.shape)
        parts.append(jnp.pad(flat, (0, rows * LANES - flat.shape[0])).reshape(rows, LANES))
    packed = jnp.concatenate(parts, axis=0)
    return jnp.pad(packed, ((0, total_rows - packed.shape[0]), (0, 0)))


def _unpack_rows(packed, shapes):
    out, r = [], 0
    for shp in shapes:
        size, rows = math.prod(shp), _part_rows(shp)
        out.append(packed[r:r + rows].reshape(-1)[:size].reshape(shp))
        r += rows
    return out


def _rows_of(shapes):
    return sum(_part_rows(s) for s in shapes)


_GATHER_STAGES = {"start": ["wa", "mkv0", "wo0"], "mixer_a_fwd": ["wup0", "wdown0"],
                  "mlp_fwd_0": ["wq", "wkv", "mkv1", "wo1"], "fox_fwd": ["wup1", "wdown1"]}
_REDUCE_STAGES = {"fox_bwd": ["wup1", "wdown1"], "mlp_bwd_0": ["wo1", "wq", "wkv", "mkv1"],
                  "mixer_a_bwd": ["wup0", "wdown0"], "end": ["wa", "wo0", "mkv0", "small"]}
_SMALL = ["ws", "b_s", "sg", "sb", "bf", "ln_g", "ln_b"]


class _Staged:
    def __init__(self, placed, sel, dm, ng, shard_cols, n_small):
        self.placed, self.sel, self.dm, self.ng, self.shard_cols, self.n_small = placed, sel, dm, ng, shard_cols, n_small
        self.g4, self.recv1, self.recv2 = {}, {}, {}

    def _weight(self, name, a):
        a = a.reshape(N_CHIPS, 2 * a.shape[2], a.shape[3])
        if name in ("wa", "wkv"):
            a = a.transpose(1, 0, 2).reshape(a.shape[1], -1)
            if name == "wkv":
                a = jnp.concatenate([a[:, :2 * self.dm],
                                     jnp.pad(a[:, 2 * self.dm:], ((0, 0), (0, LANES - self.ng)))], axis=1)
            return a
        if name.startswith("wup") or name.startswith("wdown"):
            return a
        return a.reshape(-1, a.shape[2])

    def _partial(self, name, grads):
        if name == "small":
            a = _pack_rows([grads[n] for n in _SMALL], self.n_small).reshape(N_CHIPS, -1, LANES)
        elif name in self.shard_cols:
            n = self.shard_cols[name]
            g = grads[name][:, :N_CHIPS * n]
            a = g.reshape(g.shape[0], N_CHIPS, n).transpose(1, 0, 2)
        elif name.startswith("wup") or name.startswith("wdown"):
            a = grads[name]
        else:
            g = grads[name]
            a = g.reshape(N_CHIPS, g.shape[0] // N_CHIPS, g.shape[1])
        return a.reshape(N_CHIPS, 2, a.shape[1] // 2, a.shape[2])

    def plan(self, stage, w, grads):
        if stage in _GATHER_STAGES:
            return _gather_weights([self.placed[n] for n in _GATHER_STAGES[stage]])
        if stage in _REDUCE_STAGES:
            names = _REDUCE_STAGES[stage]
            g4 = [self._partial(n, grads) for n in names]
            recv1 = _pair_swap(g4, "reduce_pair_swap_" + stage)
            summed = []
            for n, g, r1 in zip(names, g4, recv1):
                self.g4[n], self.recv1[n] = g, r1
                summed.append(_pair_add(g, r1, self.sel, F32 if n == "small" else BF16, "reduce_pair_add_" + n))
            return _chip_exchange(summed)
        return None

    def done(self, stage, results, w):
        if stage in _GATHER_STAGES:
            for n, a in zip(_GATHER_STAGES[stage], results):
                w[n] = self._weight(n, a)
        elif stage in _REDUCE_STAGES:
            for n, r in zip(_REDUCE_STAGES[stage], results):
                self.recv2[n] = r


def kernel(x, mem, a_w_in, a_sgu_ln_g, a_sgu_ln_b, a_w_s, a_b_s, kv_w, kv_b_f, b_w_q, mem_w_kv, w_o, ln_g, ln_b, w_up, w_down, loss_target, m_a_w_in, m_a_sgu_ln_g, m_a_sgu_ln_b, m_a_w_s, m_a_b_s, m_kv_w, m_kv_b_f, m_b_w_q, m_mem_w_kv, m_w_o, m_ln_g, m_ln_b, m_w_up, m_w_down, v_a_w_in, v_a_sgu_ln_g, v_a_sgu_ln_b, v_a_w_s, v_a_b_s, v_kv_w, v_kv_b_f, v_b_w_q, v_mem_w_kv, v_w_o, v_ln_g, v_ln_b, v_w_up, v_w_down):
    xi, yi, ci = lax.axis_index("x"), lax.axis_index("y"), lax.axis_index("c")
    chip = 2 * xi + yi
    sel = jnp.stack([chip, ci]).astype(jnp.int32)
    S, D = x.shape[1], x.shape[2]
    DQ = mem_w_kv.shape[2] // 2
    DM = D - DQ
    NG = DM // HEAD_DIM

    shards = {"wa": a_w_in[0], "wkv": kv_w, "wq": b_w_q[0], "mkv0": mem_w_kv[0], "mkv1": mem_w_kv[1],
              "wo0": w_o[0], "wo1": w_o[1], "wup0": w_up[0], "wup1": w_up[1], "wdown0": w_down[0], "wdown1": w_down[1]}
    placed = {n: _cast_place(s.reshape(2, s.shape[0] // 2, s.shape[1]), sel, "cast_place_" + n)
              for n, s in shards.items()}
    ln_pack = jnp.concatenate([ln_g.reshape(4, -1), ln_b.reshape(4, -1)], axis=0)
    ln_buf = lax.dynamic_update_slice(jnp.zeros((N_CHIPS,) + ln_pack.shape, F32), ln_pack[None], (chip, 0, 0))
    small_shapes = [a_w_s.shape, a_b_s.shape, a_sgu_ln_g.shape, a_sgu_ln_b.shape, kv_b_f.shape]
    ln_shape = (4, D)
    n_small = -(-_rows_of(small_shapes + [ln_shape, ln_shape]) // 64) * 64
    hooks = _Staged(placed, sel, DM, NG, {"wa": a_w_in.shape[2], "wkv": kv_w.shape[1]}, n_small)
    weights = {"sg": a_sgu_ln_g, "sb": a_sgu_ln_b, "ws": a_w_s[0], "b_s": a_b_s[0], "bf": kv_b_f}
    got = _run_comm(_gather_weights([placed[n] for n in _GATHER_STAGES["start"]], whole=[ln_buf]),
                    "gather_weights_first")
    hooks.done("start", got[:-1], weights)
    ln_full = got[-1].transpose(1, 0, 2).reshape(8, D)
    weights["ln_g"], weights["ln_b"] = ln_full[:4], ln_full[4:]

    loss_part, grad_x, gr = _local_step(x[0], mem[0], loss_target[0], weights, hooks)
    loss = lax.psum(loss_part[0, 0], ("x", "y", "c"))

    hooks.done("end", _run_comm(hooks.plan("end", weights, gr), "reduce_chip_exchange_end"), weights)
    groups = [["wa"], ["wkv"], ["wq"], ["mkv0", "mkv1"], ["wo0", "wo1"], ["wup0", "wup1"], ["wdown0", "wdown1"],
              ["small"]]
    mine = []
    for grp in groups:
        buf = None
        for layer, n in enumerate(grp):
            buf = _chip_reduce(hooks.g4[n], hooks.recv1[n], hooks.recv2[n], sel, buf, layer, len(grp),
                               "reduce_chip_sum_" + n)
        mine.append(buf)
    red = _pair_gather(mine, "reduce_pair_gather")
    red = [a.reshape(a.shape[0], 2 * a.shape[3], a.shape[4]) for a in red]
    g_a_w_in, g_kv_w, g_b_w_q, g_mem_w_kv, g_w_o, g_w_up, g_w_down, small_mine = red
    g_kv_w = g_kv_w[0]
    g_w_down = g_w_down.transpose(0, 2, 1)
    small_all = _chip_allgather([small_mine[0]], [False], "gather_small_grads")[0].reshape(n_small, LANES)
    g_ws, g_bs, g_sg, g_sb, g_bf, g_lng, g_lnb = _unpack_rows(small_all, small_shapes + [ln_shape, ln_shape])
    dsh = D // N_CHIPS
    g_ln_g = lax.dynamic_slice_in_dim(g_lng, chip * dsh, dsh, axis=1).reshape(ln_g.shape)
    g_ln_b = lax.dynamic_slice_in_dim(g_lnb, chip * dsh, dsh, axis=1).reshape(ln_b.shape)

    def update(name, g, w, m, v):
        flat = lambda a: a.reshape(-1, a.shape[-1])
        d, mn, vn = _adamw(flat(g), flat(w), flat(m), flat(v), "adamw_" + name)
        return d.reshape(w.shape), mn.reshape(w.shape), vn.reshape(w.shape)

    upd = {
        "a_w_in": update("a_w_in", g_a_w_in, a_w_in, m_a_w_in, v_a_w_in),
        "kv_w": update("kv_w", g_kv_w, kv_w, m_kv_w, v_kv_w),
        "b_w_q": update("b_w_q", g_b_w_q, b_w_q, m_b_w_q, v_b_w_q),
        "mem_w_kv": update("mem_w_kv", g_mem_w_kv, mem_w_kv, m_mem_w_kv, v_mem_w_kv),
        "w_o": update("w_o", g_w_o, w_o, m_w_o, v_w_o),
        "w_up": update("w_up", g_w_up, w_up, m_w_up, v_w_up),
        "w_down": update("w_down", g_w_down, w_down, m_w_down, v_w_down),
    }
    tiny_g = [g_sg, g_sb, g_ws, g_bs, g_bf, g_ln_g, g_ln_b]
    tiny_w = [a_sgu_ln_g, a_sgu_ln_b, a_w_s, a_b_s, kv_b_f, ln_g, ln_b]
    tiny_m = [m_a_sgu_ln_g, m_a_sgu_ln_b, m_a_w_s, m_a_b_s, m_kv_b_f, m_ln_g, m_ln_b]
    tiny_v = [v_a_sgu_ln_g, v_a_sgu_ln_b, v_a_w_s, v_a_b_s, v_kv_b_f, v_ln_g, v_ln_b]
    tiny_shapes = [a.shape for a in tiny_w]
    n_tiny = -(-_rows_of(tiny_shapes) // 8) * 8
    td, tm_, tv = _adamw(_pack_rows(tiny_g, n_tiny), _pack_rows(tiny_w, n_tiny), _pack_rows(tiny_m, n_tiny),
                         _pack_rows(tiny_v, n_tiny), "adamw_small")
    for name, d, mn, vn in zip(["a_sgu_ln_g", "a_sgu_ln_b", "a_w_s", "a_b_s", "kv_b_f", "ln_g", "ln_b"],
                               _unpack_rows(td, tiny_shapes), _unpack_rows(tm_, tiny_shapes),
                               _unpack_rows(tv, tiny_shapes)):
        upd[name] = (d, mn, vn)

    order = ["a_w_in", "a_sgu_ln_g", "a_sgu_ln_b", "a_w_s", "a_b_s", "kv_w", "kv_b_f", "b_w_q", "mem_w_kv", "w_o",
             "ln_g", "ln_b", "w_up", "w_down"]
    grads = {"a_w_in": g_a_w_in.reshape(a_w_in.shape), "a_sgu_ln_g": g_sg.reshape(a_sgu_ln_g.shape),
             "a_sgu_ln_b": g_sb.reshape(a_sgu_ln_b.shape), "a_w_s": g_ws.reshape(a_w_s.shape),
             "a_b_s": g_bs.reshape(a_b_s.shape), "kv_w": g_kv_w, "kv_b_f": g_bf, "b_w_q": g_b_w_q.reshape(b_w_q.shape),
             "mem_w_kv": g_mem_w_kv, "w_o": g_w_o, "ln_g": g_ln_g, "ln_b": g_ln_b, "w_up": g_w_up, "w_down": g_w_down}
    return (loss, grad_x[None], *[grads[n] for n in order], *[upd[n][0] for n in order],
            *[upd[n][1] for n in order], *[upd[n][2] for n in order])
```

```python
import functools
import math

import jax
import jax.numpy as jnp
from jax import lax
from jax.experimental import pallas as pl
from jax.experimental.pallas import tpu as pltpu

F32 = jnp.float32
BF16 = jnp.bfloat16

HEAD_DIM = 64
PAIR = 2 * HEAD_DIM
CHUNK = 128
LN_EPS = 1e-5
ALPHA = 4 ** 0.25
QK_SCALE = 1.0 / math.sqrt(HEAD_DIM)
NEG = -1e30
N_CHIPS = 4
FOX_BLOCK = 256
LANES = 128
VMEM_LIMIT = 48 * 2 ** 20

ADAM_LR, ADAM_B1, ADAM_B2, ADAM_EPS, ADAM_WD, ADAM_STEP = 0.001, 0.9, 0.999, 1e-08, 0.01, 10

MESH = pl.DeviceIdType.MESH
ANY = pl.BlockSpec(memory_space=pl.ANY)


def _dot(a, b):
    return jnp.dot(a, b, preferred_element_type=F32)


def _dot_nt(a, b):
    return lax.dot_general(a, b, (((1,), (1,)), ((), ())), preferred_element_type=F32)


def _dot_tn(a, b):
    return lax.dot_general(a, b, (((0,), (0,)), ((), ())), preferred_element_type=F32)


def _ln_stats(r):
    mu = jnp.mean(r, axis=-1, keepdims=True)
    d = r - mu
    var = jnp.mean(d * d, axis=-1, keepdims=True)
    rstd = lax.rsqrt(var + LN_EPS)
    return d * rstd, rstd


def _ln_bwd(dy, xhat, rstd, g):
    dxh = dy * g
    m1 = jnp.mean(dxh, axis=-1, keepdims=True)
    m2 = jnp.mean(dxh * xhat, axis=-1, keepdims=True)
    return rstd * (dxh - m1 - xhat * m2)


_GELU_K = math.sqrt(2.0 / math.pi)


def _gelu(x):
    return 0.5 * x * (1.0 + jnp.tanh(_GELU_K * (x + 0.044715 * x * x * x)))


def _gelu_grad(x):
    t = jnp.tanh(_GELU_K * (x + 0.044715 * x * x * x))
    return 0.5 * (1.0 + t) + 0.5 * x * (1.0 - t * t) * _GELU_K * (1.0 + 3 * 0.044715 * x * x)


def _lane_mask(e):
    lane = lax.broadcasted_iota(jnp.int32, (1, PAIR), 1)
    return (lane >= HEAD_DIM * e) & (lane < HEAD_DIM * (e + 1))


def _first_head():
    return lax.broadcasted_iota(jnp.int32, (1, PAIR), 1) < HEAD_DIM


def _mem_probs(qh, mkp):
    sc = _dot_nt(qh, mkp) * QK_SCALE
    ex = jnp.exp(sc - jnp.max(sc, axis=-1, keepdims=True))
    return ex / jnp.sum(ex, axis=-1, keepdims=True)


def _mem_attn_fwd(qm_b, mkv_ref, dq_dim):
    outs = []
    for mp in range(dq_dim // PAIR):
        qp = qm_b[:, mp * PAIR:(mp + 1) * PAIR]
        mkp = mkv_ref[:, mp * PAIR:(mp + 1) * PAIR]
        mvp = mkv_ref[:, dq_dim + mp * PAIR:dq_dim + (mp + 1) * PAIR]
        heads = []
        for e in (0, 1):
            qh = jnp.where(_lane_mask(e), qp, jnp.zeros_like(qp))
            p = _mem_probs(qh, mkp)
            heads.append(_dot(p.astype(BF16), mvp))
        outs.append(jnp.where(_first_head(), heads[0], heads[1]))
    return outs


def _mem_attn_bwd(qm_b, mkv_ref, dmo_pairs, dmkv_ref, dq_dim):
    dqs = []
    for mp in range(dq_dim // PAIR):
        ks = slice(mp * PAIR, (mp + 1) * PAIR)
        vs = slice(dq_dim + mp * PAIR, dq_dim + (mp + 1) * PAIR)
        qp = qm_b[:, ks]
        mkp = mkv_ref[:, ks]
        mvp = mkv_ref[:, vs]
        dmo_b = dmo_pairs[mp].astype(BF16)
        dq = None
        dmk = None
        dmv = None
        for e in (0, 1):
            hm = _lane_mask(e)
            qh = jnp.where(hm, qp, jnp.zeros_like(qp))
            p = _mem_probs(qh, mkp)
            doh = jnp.where(hm, dmo_b, jnp.zeros_like(dmo_b))
            dp = _dot_nt(doh, mvp)
            ds = p * (dp - jnp.sum(dp * p, axis=-1, keepdims=True))
            dsb = (ds * QK_SCALE).astype(BF16)
            kh = jnp.where(hm, mkp, jnp.zeros_like(mkp))
            dq_e = _dot(dsb, kh)
            dmk_e = _dot_tn(dsb, qh)
            dmv_e = _dot_tn(p.astype(BF16), doh)
            dq = dq_e if dq is None else dq + dq_e
            dmk = dmk_e if dmk is None else dmk + dmk_e
            dmv = dmv_e if dmv is None else dmv + dmv_e
        dmkv_ref[:, ks] += dmk
        dmkv_ref[:, vs] += dmv
        dqs.append(dq)
    return dqs


def _params(n_axes=1):
    return pltpu.CompilerParams(dimension_semantics=("arbitrary",) * n_axes, vmem_limit_bytes=VMEM_LIMIT)


def _full(shape):
    return pl.BlockSpec(shape, lambda *_: (0,) * len(shape))


def _rows(tm, cols):
    return pl.BlockSpec((tm, cols), lambda i: (i, 0))


def _cols(rows, tm):
    return pl.BlockSpec((rows, tm), lambda i: (0, i))


def _pick(n, pref):
    if n <= pref:
        return n
    best = LANES
    for t in range(LANES, pref + 1, LANES):
        if n % t == 0:
            best = t
    return best if 2 * best >= pref or n > 2 * pref else n


class _Comm:
    def __init__(self, ins, out_shapes, aliases, sem_counts, start, finish):
        self.ins, self.out_shapes, self.aliases, self.sem_counts = list(ins), list(out_shapes), dict(aliases), sem_counts
        self.start, self.finish = start, finish


def _pcall(body, *, name, grid, in_specs, out_specs, out_shape, args, scratch_shapes=(), comm=None):
    n_in, n_out, n_scr = len(in_specs), len(out_shape), len(scratch_shapes)
    if comm is None:
        res = pl.pallas_call(body, name=name, grid=grid, out_shape=tuple(out_shape), in_specs=list(in_specs),
                             out_specs=tuple(out_specs), scratch_shapes=list(scratch_shapes),
                             compiler_params=_params())(*args)
        return tuple(res), ()
    nci, nco = len(comm.ins), len(comm.out_shapes)
    last = grid[0] - 1

    def wrapped(*refs):
        ins, refs = refs[:n_in], refs[n_in:]
        cins, refs = refs[:nci], refs[nci:]
        outs, refs = refs[:n_out], refs[n_out:]
        couts, refs = refs[:nco], refs[nco:]
        scr, sems = refs[:n_scr], refs[n_scr:]

        @pl.when(pl.program_id(0) == 0)
        def _():
            comm.start(cins, couts, sems)

        body(*ins, *outs, *scr)

        @pl.when(pl.program_id(0) == last)
        def _():
            comm.finish(cins, couts, sems)

    res = pl.pallas_call(
        wrapped, name=name, grid=grid, out_shape=tuple(out_shape) + tuple(comm.out_shapes),
        in_specs=list(in_specs) + [ANY] * nci, out_specs=tuple(out_specs) + (ANY,) * nco,
        input_output_aliases={n_in + a: n_out + b for a, b in comm.aliases.items()},
        scratch_shapes=list(scratch_shapes) + [pltpu.SemaphoreType.DMA((k,)) for k in comm.sem_counts],
        compiler_params=pltpu.CompilerParams(dimension_semantics=("arbitrary",), vmem_limit_bytes=VMEM_LIMIT,
                                             has_side_effects=True),
    )(*args, *comm.ins)
    return tuple(res[:n_out]), tuple(res[n_out:])


def _run_comm(comm, name):
    nci, nco = len(comm.ins), len(comm.out_shapes)

    def body(*refs):
        cins, couts, sems = refs[:nci], refs[nci:nci + nco], refs[nci + nco:]
        comm.start(cins, couts, sems)
        comm.finish(cins, couts, sems)

    return pl.pallas_call(
        body, name=name, out_shape=tuple(comm.out_shapes), in_specs=[ANY] * nci, out_specs=(ANY,) * nco,
        input_output_aliases=comm.aliases, scratch_shapes=[pltpu.SemaphoreType.DMA((k,)) for k in comm.sem_counts],
        compiler_params=pltpu.CompilerParams(has_side_effects=True),
    )(*comm.ins)


def _mm(a, b, name, a_is_transposed, out_split=1):
    (M, R) = a.shape if a_is_transposed else a.shape[::-1]
    _, N = b.shape
    bm, br = _pick(M, 1024 if a_is_transposed else 512), _pick(R, 1024 if a_is_transposed else 512)
    ncol = N // out_split
    bn = _pick(ncol, 1024)
    per = ncol // bn
    last = R // br - 1

    def body(a_ref, b_ref, o_ref, acc_ref):
        @pl.when(pl.program_id(2) == 0)
        def _():
            acc_ref[...] = jnp.zeros_like(acc_ref)

        acc_ref[...] += (_dot if a_is_transposed else _dot_tn)(a_ref[...], b_ref[...])

        @pl.when(pl.program_id(2) == last)
        def _():
            o_ref[...] = acc_ref[...].astype(BF16)

    if out_split == 1:
        out_shape = jax.ShapeDtypeStruct((M, N), BF16)
        out_spec = pl.BlockSpec((bm, bn), lambda i, j, r: (i, j))
    else:
        out_shape = jax.ShapeDtypeStruct((out_split, M, ncol), BF16)
        out_spec = pl.BlockSpec((None, bm, bn), lambda i, j, r: (j // per, i, j % per))
    a_spec = (pl.BlockSpec((bm, br), lambda i, j, r: (i, r)) if a_is_transposed
              else pl.BlockSpec((br, bm), lambda i, j, r: (r, i)))
    return pl.pallas_call(
        body, name=name, out_shape=out_shape, grid=(M // bm, N // bn, R // br),
        in_specs=[a_spec, pl.BlockSpec((br, bn), lambda i, j, r: (r, j))], out_specs=out_spec,
        scratch_shapes=[pltpu.VMEM((bm, bn), F32)],
        compiler_params=pltpu.CompilerParams(dimension_semantics=("parallel", "parallel", "arbitrary"),
                                             vmem_limit_bytes=VMEM_LIMIT),
    )(a, b)


def _mm_tn(a, b, name, out_split=1):
    return _mm(a, b, name, False, out_split)


def _mm_nn(at, b, name, out_split=1):
    return _mm(at, b, name, True, out_split)


def _mm_small(a, b, name):
    def body(a_ref, b_ref, o_ref):
        o_ref[...] = _dot(a_ref[...], b_ref[...]).astype(BF16)

    return pl.pallas_call(body, name=name, out_shape=jax.ShapeDtypeStruct((a.shape[0], b.shape[1]), BF16),
                          compiler_params=pltpu.CompilerParams(vmem_limit_bytes=VMEM_LIMIT))(a, b)


def _mlp_out(x_ref, wup_ref, wdown_ref, g_ref, b_ref, r_ref, h_ref):
    NC, _, FC = wup_ref.shape
    xv = x_ref[...]
    xb = xv.astype(BF16)
    y = jnp.zeros(xv.shape, F32)
    for c in range(NC):
        a = jnp.maximum(_dot(xb, wup_ref[c]), 0.0)
        hb = (a * a).astype(BF16)
        h_ref[:, c * FC:(c + 1) * FC] = hb
        y = y + _dot(hb, wdown_ref[c])
    r = ALPHA * xv + y
    r_ref[...] = r
    xhat, _ = _ln_stats(r)
    return xhat * g_ref[...] + b_ref[...]


def _mlp_fwd_loss(x, wup4, wdown4, g, b, target, name):
    S, D = x.shape
    NC, _, FC = wup4.shape
    tm = min(256, S)

    def body(x_ref, wup_ref, wdown_ref, g_ref, b_ref, t_ref, d_ref, r_ref, h_ref, l_ref):
        @pl.when(pl.program_id(0) == 0)
        def _():
            l_ref[...] = jnp.zeros_like(l_ref)

        err = _mlp_out(x_ref, wup_ref, wdown_ref, g_ref, b_ref, r_ref, h_ref) - t_ref[...]
        d_ref[...] = err * (1.0 / D)
        l_ref[...] += (0.5 / D) * jnp.sum(err * err)

    return pl.pallas_call(
        body, name=name, grid=(S // tm,),
        out_shape=(jax.ShapeDtypeStruct((S, D), F32), jax.ShapeDtypeStruct((S, D), F32),
                   jax.ShapeDtypeStruct((S, NC * FC), BF16), jax.ShapeDtypeStruct((8, LANES), F32)),
        in_specs=[_rows(tm, D), _full(wup4.shape), _full(wdown4.shape), _full((1, D)), _full((1, D)), _rows(tm, D)],
        out_specs=(_rows(tm, D), _rows(tm, D), _rows(tm, NC * FC), _full((8, LANES))),
        compiler_params=_params(),
    )(x, wup4, wdown4, g, b, target)


def _mlp_fwd(x, wup4, wdown4, g, b, name, comm=None):
    S, D = x.shape
    NC, _, FC = wup4.shape
    tm = min(256, S)

    def body(x_ref, wup_ref, wdown_ref, g_ref, b_ref, xo_ref, xb_ref, xbt_ref, r_ref, h_ref):
        xo = _mlp_out(x_ref, wup_ref, wdown_ref, g_ref, b_ref, r_ref, h_ref)
        xo_ref[...] = xo
        xb_ref[...] = xo.astype(BF16)
        xbt_ref[...] = xo.T.astype(BF16)

    return _pcall(
        body, name=name, grid=(S // tm,), comm=comm,
        out_shape=(jax.ShapeDtypeStruct((S, D), F32), jax.ShapeDtypeStruct((S, D), BF16),
                   jax.ShapeDtypeStruct((D, S), BF16), jax.ShapeDtypeStruct((S, D), F32),
                   jax.ShapeDtypeStruct((S, NC * FC), BF16)),
        in_specs=[_rows(tm, D), _full(wup4.shape), _full(wdown4.shape), _full((1, D)), _full((1, D))],
        out_specs=(_rows(tm, D), _rows(tm, D), _cols(D, tm), _rows(tm, D), _rows(tm, NC * FC)),
        args=(x, wup4, wdown4, g, b))


def _mlp_bwd(dxo, r, h, wup4, wdown4, g, name, comm=None):
    S, D = r.shape
    NC, _, FC = wup4.shape
    tm = min(256, S)

    def body(dxo_ref, r_ref, h_ref, wup_ref, wdown_ref, g_ref, dx_ref, drbt_ref, dpre_ref, dgb_ref):
        @pl.when(pl.program_id(0) == 0)
        def _():
            dgb_ref[...] = jnp.zeros_like(dgb_ref)

        xhat, rstd = _ln_stats(r_ref[...])
        dy = dxo_ref[...]
        dgb_ref[0:1, :] += jnp.sum(dy * xhat, axis=0, keepdims=True)
        dgb_ref[1:2, :] += jnp.sum(dy, axis=0, keepdims=True)
        dr = _ln_bwd(dy, xhat, rstd, g_ref[...])
        drb = dr.astype(BF16)
        drbt_ref[...] = dr.T.astype(BF16)
        dx = ALPHA * dr
        for c in range(NC):
            dh = _dot_nt(drb, wdown_ref[c])
            a = jnp.sqrt(h_ref[:, c * FC:(c + 1) * FC].astype(F32))
            dpre = (2.0 * a * dh).astype(BF16)
            dpre_ref[:, c * FC:(c + 1) * FC] = dpre
            dx = dx + _dot_nt(dpre, wup_ref[c])
        dx_ref[...] = dx

    return _pcall(
        body, name=name, grid=(S // tm,), comm=comm,
        out_shape=(jax.ShapeDtypeStruct((S, D), F32), jax.ShapeDtypeStruct((D, S), BF16),
                   jax.ShapeDtypeStruct((S, NC * FC), BF16), jax.ShapeDtypeStruct((2, D), F32)),
        in_specs=[_rows(tm, D), _rows(tm, D), _rows(tm, NC * FC), _full(wup4.shape), _full(wdown4.shape),
                  _full((1, D))],
        out_specs=(_rows(tm, D), _cols(D, tm), _rows(tm, NC * FC), _full((2, D))),
        args=(dxo, r, h, wup4, wdown4, g))


def _tail_fwd_store(xv, cat_ref, wo_ref, g_ref, b_ref, xo_ref, xbt_ref, r_ref, catt_ref):
    cat = cat_ref[...]
    y = _dot(cat, wo_ref[...])
    r = ALPHA * xv + y
    xhat, _ = _ln_stats(r)
    xo = xhat * g_ref[...] + b_ref[...]
    xo_ref[...] = xo
    xbt_ref[...] = xo.T.astype(BF16)
    catt_ref[...] = cat.astype(F32).T.astype(BF16)
    r_ref[...] = r


def _tail_bwd_head(dxo_ref, r_ref, g_ref, wo_ref, dgb_ref, drb_ref):
    xhat, rstd = _ln_stats(r_ref[...])
    dy = dxo_ref[...]
    dgb_ref[0:1, :] += jnp.sum(dy * xhat, axis=0, keepdims=True)
    dgb_ref[1:2, :] += jnp.sum(dy, axis=0, keepdims=True)
    dr = _ln_bwd(dy, xhat, rstd, g_ref[...])
    drb = dr.astype(BF16)
    drb_ref[...] = drb
    return dr, _dot_nt(drb, wo_ref[...])


def _causal(shape):
    return lax.broadcasted_iota(jnp.int32, shape, 1) <= lax.broadcasted_iota(jnp.int32, shape, 0)


def _sgu_mixed(ws_ref, bst_ref, gp, vpair):
    tril = _causal((CHUNK, CHUNK))
    w0 = jnp.where(tril, ws_ref[2 * gp], 0.0).astype(BF16)
    w1 = jnp.where(tril, ws_ref[2 * gp + 1], 0.0).astype(BF16)
    m0 = _dot(w0, vpair) + bst_ref[:, 2 * gp:2 * gp + 1]
    m1 = _dot(w1, vpair) + bst_ref[:, 2 * gp + 1:2 * gp + 2]
    return jnp.where(_first_head(), m0, m1), w0, w1


def _mixer_a_fwd(x, wa, sg, sb, ws, bst, mkv, wo, g, b, name, comm=None):
    S, D = x.shape
    DQ = mkv.shape[1] // 2
    DM = D - DQ
    NP = DM // PAIR
    tm = min(512, S)

    def body(x_ref, wa_ref, sg_ref, sb_ref, ws_ref, bst_ref, mkv_ref, wo_ref, g_ref, b_ref,
             xo_ref, xbt_ref, r_ref, catt_ref, xint_ref, z_s, zv_s, cat_ref):
        xv = x_ref[...]
        xint_ref[...] = xv.T.astype(BF16)
        z_s[...] = _dot(xv.astype(BF16), wa_ref[...])
        vhat, _ = _ln_stats(_gelu(z_s[:, DM:2 * DM]))
        zv_s[...] = (vhat * sg_ref[...] + sb_ref[...]).astype(BF16)
        for c in range(tm // CHUNK):
            rs = slice(c * CHUNK, (c + 1) * CHUNK)
            for gp in range(NP):
                ls = slice(gp * PAIR, (gp + 1) * PAIR)
                mixed, _, _ = _sgu_mixed(ws_ref, bst_ref, gp, zv_s[rs, ls])
                cat_ref[rs, ls] = (_gelu(z_s[rs, ls]) * mixed).astype(BF16)
        mo = _mem_attn_fwd(z_s[:, 2 * DM:].astype(BF16), mkv_ref, DQ)
        for mp in range(DQ // PAIR):
            cat_ref[:, DM + mp * PAIR:DM + (mp + 1) * PAIR] = mo[mp].astype(BF16)
        _tail_fwd_store(xv, cat_ref, wo_ref, g_ref, b_ref, xo_ref, xbt_ref, r_ref, catt_ref)

    return _pcall(
        body, name=name, grid=(S // tm,), comm=comm,
        out_shape=(jax.ShapeDtypeStruct((S, D), F32), jax.ShapeDtypeStruct((D, S), BF16),
                   jax.ShapeDtypeStruct((S, D), F32), jax.ShapeDtypeStruct((D, S), BF16),
                   jax.ShapeDtypeStruct((D, S), BF16)),
        in_specs=[_rows(tm, D), _full(wa.shape), _full(sg.shape), _full(sb.shape), _full(ws.shape), _full(bst.shape),
                  _full(mkv.shape), _full(wo.shape), _full((1, D)), _full((1, D))],
        out_specs=(_rows(tm, D), _cols(D, tm), _rows(tm, D), _cols(D, tm), _cols(D, tm)),
        scratch_shapes=[pltpu.VMEM((tm, 2 * DM + DQ), F32), pltpu.VMEM((tm, DM), BF16), pltpu.VMEM((tm, D), BF16)],
        args=(x, wa, sg, sb, ws, bst, mkv, wo, g, b))


def _mixer_a_bwd(x, dxo, r, wa, sg, sb, ws, bst, mkv, wo, g, name, comm=None):
    S, D = x.shape
    DQ = mkv.shape[1] // 2
    DM = D - DQ
    NP = DM // PAIR
    NG = DM // HEAD_DIM
    tm = min(256, S)

    def body(x_ref, dxo_ref, r_ref, wa_ref, sg_ref, sb_ref, ws_ref, bst_ref, mkv_ref, wo_ref, g_ref,
             dx_ref, dz_ref, drb_ref, dgb_ref, dsgb_ref, dws_ref, dbst_ref, dmkv_ref, z_s, zv_s, dzv_s):
        @pl.when(pl.program_id(0) == 0)
        def _():
            dgb_ref[...] = jnp.zeros_like(dgb_ref)
            dsgb_ref[...] = jnp.zeros_like(dsgb_ref)
            dws_ref[...] = jnp.zeros_like(dws_ref)
            dbst_ref[...] = jnp.zeros_like(dbst_ref)
            dmkv_ref[...] = jnp.zeros_like(dmkv_ref)

        dr, dcat = _tail_bwd_head(dxo_ref, r_ref, g_ref, wo_ref, dgb_ref, drb_ref)
        z_s[...] = _dot(x_ref[...].astype(BF16), wa_ref[...])
        vhat, vrstd = _ln_stats(_gelu(z_s[:, DM:2 * DM]))
        zv_s[...] = (vhat * sg_ref[...] + sb_ref[...]).astype(BF16)
        tril = _causal((CHUNK, CHUNK))
        lane = lax.broadcasted_iota(jnp.int32, (1, LANES), 1)
        for c in range(tm // CHUNK):
            rs = slice(c * CHUNK, (c + 1) * CHUNK)
            for gp in range(NP):
                ls = slice(gp * PAIR, (gp + 1) * PAIR)
                vpair = zv_s[rs, ls]
                mixed, w0, w1 = _sgu_mixed(ws_ref, bst_ref, gp, vpair)
                u_pre = z_s[rs, ls]
                dmix = dcat[rs, ls]
                dz_ref[rs, ls] = (dmix * mixed * _gelu_grad(u_pre)).astype(BF16)
                dmixed = dmix * _gelu(u_pre)
                first = _first_head()
                d0 = jnp.where(first, dmixed, 0.0)
                d1 = jnp.where(first, 0.0, dmixed)
                d0b = d0.astype(BF16)
                d1b = d1.astype(BF16)
                dzv_s[rs, ls] = _dot_tn(w0, d0b) + _dot_tn(w1, d1b)
                dws_ref[2 * gp] += jnp.where(tril, _dot_nt(d0b, vpair), 0.0)
                dws_ref[2 * gp + 1] += jnp.where(tril, _dot_nt(d1b, vpair), 0.0)
                dbst_ref[...] += (jnp.where(lane == 2 * gp, jnp.sum(d0, axis=-1, keepdims=True), 0.0)
                                  + jnp.where(lane == 2 * gp + 1, jnp.sum(d1, axis=-1, keepdims=True), 0.0))
        dzv = dzv_s[...]
        dsgb_ref[0:1, :] += jnp.sum(dzv * vhat, axis=0, keepdims=True)
        dsgb_ref[1:2, :] += jnp.sum(dzv, axis=0, keepdims=True)
        dgv = _ln_bwd(dzv, vhat, vrstd, sg_ref[...])
        dz_ref[:, DM:2 * DM] = (dgv * _gelu_grad(z_s[:, DM:2 * DM])).astype(BF16)
        dmo = [dcat[:, DM + mp * PAIR:DM + (mp + 1) * PAIR] for mp in range(DQ // PAIR)]
        dqm = _mem_attn_bwd(z_s[:, 2 * DM:].astype(BF16), mkv_ref, dmo, dmkv_ref, DQ)
        for mp in range(DQ // PAIR):
            dz_ref[:, 2 * DM + mp * PAIR:2 * DM + (mp + 1) * PAIR] = dqm[mp].astype(BF16)
        dx_ref[...] = ALPHA * dr + _dot_nt(dz_ref[...], wa_ref[...])

    ZW = 2 * DM + DQ
    M = mkv.shape[0]
    return _pcall(
        body, name=name, grid=(S // tm,), comm=comm,
        out_shape=(jax.ShapeDtypeStruct((S, D), F32), jax.ShapeDtypeStruct((S, ZW), BF16),
                   jax.ShapeDtypeStruct((S, D), BF16), jax.ShapeDtypeStruct((2, D), F32),
                   jax.ShapeDtypeStruct((2, DM), F32), jax.ShapeDtypeStruct((NG, CHUNK, CHUNK), F32),
                   jax.ShapeDtypeStruct((CHUNK, LANES), F32), jax.ShapeDtypeStruct((M, 2 * DQ), F32)),
        in_specs=[_rows(tm, D), _rows(tm, D), _rows(tm, D), _full(wa.shape), _full(sg.shape), _full(sb.shape),
                  _full(ws.shape), _full(bst.shape), _full(mkv.shape), _full(wo.shape), _full((1, D))],
        out_specs=(_rows(tm, D), _rows(tm, ZW), _rows(tm, D), _full((2, D)), _full((2, DM)),
                   _full((NG, CHUNK, CHUNK)), _full((CHUNK, LANES)), _full((M, 2 * DQ))),
        scratch_shapes=[pltpu.VMEM((tm, ZW), F32), pltpu.VMEM((tm, DM), BF16), pltpu.VMEM((tm, DM), F32)],
        args=(x, dxo, r, wa, sg, sb, ws, bst, mkv, wo, g))


def _proj_b_fwd(xb, wq, wkv, dm, name):
    S, D = xb.shape
    DQ = D - dm
    tm = min(512, S)

    def body(x_ref, wq_ref, wkv_ref, q_ref, qm_ref, k_ref, v_ref, fl_ref):
        xv = x_ref[...]
        z = _dot(xv, wq_ref[...])
        q_ref[...] = z[:, :dm].astype(BF16)
        qm_ref[...] = z[:, dm:].astype(BF16)
        kvf = _dot(xv, wkv_ref[...])
        k_ref[...] = kvf[:, :dm].astype(BF16)
        v_ref[...] = kvf[:, dm:2 * dm].astype(BF16)
        fl_ref[...] = kvf[:, 2 * dm:]

    return pl.pallas_call(
        body, name=name, grid=(S // tm,),
        out_shape=(jax.ShapeDtypeStruct((S, dm), BF16), jax.ShapeDtypeStruct((S, DQ), BF16),
                   jax.ShapeDtypeStruct((S, dm), BF16), jax.ShapeDtypeStruct((S, dm), BF16),
                   jax.ShapeDtypeStruct((S, LANES), F32)),
        in_specs=[_rows(tm, D), _full(wq.shape), _full(wkv.shape)],
        out_specs=(_rows(tm, dm), _rows(tm, DQ), _rows(tm, dm), _rows(tm, dm), _rows(tm, LANES)),
        compiler_params=_params(),
    )(xb, wq, wkv)


def _proj_b_bwd(dr, dqz, dkvf, wq, wkv, name):
    S, D = dr.shape
    tm = min(512, S)

    def body(dr_ref, dqz_ref, dkvf_ref, wq_ref, wkv_ref, dx_ref):
        dx_ref[...] = (ALPHA * dr_ref[...] + _dot_nt(dqz_ref[...], wq_ref[...])
                       + _dot_nt(dkvf_ref[...], wkv_ref[...]))

    return pl.pallas_call(
        body, name=name, grid=(S // tm,), out_shape=jax.ShapeDtypeStruct((S, D), F32),
        in_specs=[_rows(tm, D), _rows(tm, dqz.shape[1]), _rows(tm, dkvf.shape[1]), _full(wq.shape), _full(wkv.shape)],
        out_specs=_rows(tm, D), compiler_params=_params(),
    )(dr, dqz, dkvf, wq, wkv)


def _split3(v):
    hi = v.astype(BF16)
    r1 = v - hi.astype(F32)
    mid = r1.astype(BF16)
    lo = (r1 - mid.astype(F32)).astype(BF16)
    return hi, mid, lo


def _tri_sum(v, tri_b):
    hi, mid, lo = _split3(v)
    return _dot(hi, tri_b) + _dot(mid, tri_b) + _dot(lo, tri_b)


def _log_sigmoid(x):
    return jnp.minimum(x, 0.0) - jnp.log(1.0 + jnp.exp(-jnp.abs(x)))


def _forget_cumsum(flt, bf, name):
    H, S = flt.shape
    nchunk = S // LANES

    def body(fl_ref, bf_ref, p_ref):
        upper = (lax.broadcasted_iota(jnp.int32, (LANES, LANES), 0)
                 <= lax.broadcasted_iota(jnp.int32, (LANES, LANES), 1)).astype(BF16)

        def step(n, carry):
            s0 = pl.multiple_of(n * LANES, LANES)
            lf = _log_sigmoid(fl_ref[:, pl.ds(s0, LANES)] + bf_ref[...])
            cs = _tri_sum(lf, upper) + carry
            for i, piece in enumerate(_split3(-cs)):
                p_ref[i, :, pl.ds(s0, LANES)] = piece.astype(F32)
            return cs[:, LANES - 1:LANES]

        lax.fori_loop(0, nchunk, step, jnp.zeros((H, 1), F32))

    return pl.pallas_call(body, name=name, out_shape=jax.ShapeDtypeStruct((3, H, S), F32),
                          compiler_params=pltpu.CompilerParams(vmem_limit_bytes=VMEM_LIMIT))(flt, bf)


def _forget_cumsum_bwd(dct, flt, bf, name):
    H, S = flt.shape
    nchunk = S // LANES

    def body(dc_ref, fl_ref, bf_ref, dfl_ref, dbf_ref):
        lower = (lax.broadcasted_iota(jnp.int32, (LANES, LANES), 0)
                 >= lax.broadcasted_iota(jnp.int32, (LANES, LANES), 1)).astype(BF16)

        def step(n, carry):
            tail, tot = carry
            s0 = pl.multiple_of((nchunk - 1 - n) * LANES, LANES)
            suffix = _tri_sum(dc_ref[:, pl.ds(s0, LANES)], lower) + tail
            xv = fl_ref[:, pl.ds(s0, LANES)] + bf_ref[...]
            dfl = suffix * (1.0 / (1.0 + jnp.exp(xv)))
            dfl_ref[:, pl.ds(s0, LANES)] = dfl
            return suffix[:, 0:1], tot + jnp.sum(dfl, axis=-1, keepdims=True)

        _, tot = lax.fori_loop(0, nchunk, step, (jnp.zeros((H, 1), F32), jnp.zeros((H, 1), F32)))
        dbf_ref[...] = jnp.broadcast_to(tot, (H, LANES))

    return pl.pallas_call(body, name=name,
                          out_shape=(jax.ShapeDtypeStruct((H, S), F32), jax.ShapeDtypeStruct((H, LANES), F32)),
                          compiler_params=pltpu.CompilerParams(vmem_limit_bytes=VMEM_LIMIT))(dct, flt, bf)


N_BIAS = 3


def _bias_ones():
    lane = lax.broadcasted_iota(jnp.int32, (1, PAIR), 1)
    return ((lane & (HEAD_DIM - 1)) < N_BIAS).astype(BF16)


def _fox_fwd(q, k, v, ca, name, comm=None):
    S, DM = q.shape
    NP = DM // PAIR
    T = min(FOX_BLOCK, S)
    nq = S // T

    def body(q_ref, k_ref, v_ref, ca_ref, o_ref, lse_ref):
        tril = _causal((T, T))
        heads = (_lane_mask(0), _lane_mask(1))
        ones3 = _bias_ones()

        def step(carry, qaug, kaug, vaug, masked):
            m, acc = carry
            s = _dot_nt(qaug, kaug)
            if masked:
                s = jnp.where(tril, s, NEG)
            m_new = jnp.maximum(m, jnp.max(s, axis=-1, keepdims=True))
            p = jnp.exp(s - m_new).astype(BF16)
            return m_new, jnp.exp(m - m_new) * acc + _dot(p, vaug)

        def kv_block(kj):
            s0 = pl.multiple_of(kj * T, T)
            kb, vb, cab = k_ref[pl.ds(s0, T), :], v_ref[pl.ds(s0, T), :], ca_ref[pl.ds(s0, T), :]
            return ([jnp.where(heads[e], kb, cab) for e in (0, 1)],
                    [jnp.where(heads[e], vb, jnp.ones_like(vb)) for e in (0, 1)])

        def q_pair(a, _):
            t0s = [pl.multiple_of((2 * a + r) * T, T) for r in (0, 1)]
            qaug = {}
            for r in (0, 1):
                qb = q_ref[pl.ds(t0s[r], T), :] * QK_SCALE
                for e in (0, 1):
                    qaug[r, e] = jnp.where(heads[e], qb, ones3)
            chains = [(r, e) for r in (0, 1) for e in (0, 1)]

            def full(first_block, n_blocks, carries):
                carries = list(carries)
                for kk in range(n_blocks):
                    kaug, vaug = kv_block(first_block + kk)
                    scores = [_dot_nt(qaug[r, e], kaug[e]) for r, e in chains]
                    for n, (r, e) in enumerate(chains):
                        m, acc = carries[n]
                        m_new = jnp.maximum(m, jnp.max(scores[n], axis=-1, keepdims=True))
                        p = jnp.exp(scores[n] - m_new).astype(BF16)
                        carries[n] = (m_new, jnp.exp(m - m_new) * acc + _dot(p, vaug[e]))
                return tuple(carries)

            init = (jnp.full((T, 1), NEG, F32), jnp.zeros((T, PAIR), F32))
            carries = lax.fori_loop(0, a // 2, lambda i, c: full(4 * i, 4, c), (init,) * 4)
            carries = list(lax.fori_loop(0, a % 2, lambda i, c: full(2 * a - 2, 2, c), carries))
            kaug, vaug = kv_block(2 * a)
            for i, (r, e) in enumerate(chains):
                carries[i] = step(carries[i], qaug[r, e], kaug[e], vaug[e], r == 0)
            kaug, vaug = kv_block(2 * a + 1)
            for i, (r, e) in enumerate(chains):
                if r == 1:
                    carries[i] = step(carries[i], qaug[r, e], kaug[e], vaug[e], True)
            first = _first_head()
            for r in (0, 1):
                (m0, acc0), (m1, acc1) = carries[2 * r], carries[2 * r + 1]
                l0, l1 = acc0[:, HEAD_DIM:HEAD_DIM + 1], acc1[:, 0:1]
                o_ref[pl.ds(t0s[r], T), :] = jnp.where(first, acc0 / l0, acc1 / l1).astype(BF16)
                lse_t = jnp.where(first, m0 + jnp.log(l0), m1 + jnp.log(l1)).T
                lse_ref[0:1, pl.ds(t0s[r], T)] = lse_t[0:1, :]
                lse_ref[1:2, pl.ds(t0s[r], T)] = lse_t[HEAD_DIM:HEAD_DIM + 1, :]
            return 0

        lax.fori_loop(0, nq // 2, q_pair, 0)

    col = pl.BlockSpec((S, PAIR), lambda hp: (0, hp))
    return _pcall(
        body, name=name, grid=(NP,), comm=comm,
        out_shape=(jax.ShapeDtypeStruct((S, DM), BF16), jax.ShapeDtypeStruct((NP, 2, S), F32)),
        in_specs=[col, col, col, col], out_specs=(col, pl.BlockSpec((None, 2, S), lambda hp: (hp, 0, 0))),
        args=(q, k, v, ca))


def _fox_bwd(q, k, v, o, do, ca, lse, name, comm=None):
    S, DM = q.shape
    NP = DM // PAIR
    T = min(FOX_BLOCK, S)
    nq = S // T

    def body(q_ref, k_ref, v_ref, o_ref, do_ref, ca_ref, lse_ref, dq_ref, dk_ref, dv_ref, dc_ref,
             dq_acc, dcol_acc, dk_acc, dv_acc, dd_s):
        keep = lax.broadcasted_iota(jnp.int32, (T, T), 0) <= lax.broadcasted_iota(jnp.int32, (T, T), 1)
        lane = lax.broadcasted_iota(jnp.int32, (1, LANES), 1)
        heads = (_lane_mask(0), _lane_mask(1))
        ones3 = _bias_ones()
        dq_acc[...] = jnp.zeros_like(dq_acc)
        dcol_acc[...] = jnp.zeros_like(dcol_acc)
        dc_ref[...] = jnp.zeros_like(dc_ref)

        def row_dots(qi, _):
            t0 = pl.multiple_of(qi * T, T)
            prod = do_ref[pl.ds(t0, T), :].astype(F32) * o_ref[pl.ds(t0, T), :].astype(F32)
            cols = [jnp.sum(jnp.where(heads[e], prod, 0.0), axis=-1, keepdims=True) for e in (0, 1)]
            tile = jnp.where(lane == 0, cols[0], jnp.where(lane == 1, cols[1], 0.0))
            dd_s[:, pl.ds(t0, T)] = tile.T[0:8, :]
            return 0

        lax.fori_loop(0, nq, row_dots, 0)

        def kv_pair(b, _):
            dk_acc[...] = jnp.zeros_like(dk_acc)
            dv_acc[...] = jnp.zeros_like(dv_acc)
            s0s = [pl.multiple_of((2 * b + jj) * T, T) for jj in (0, 1)]
            kaug, ks, vbs = {}, {}, []
            for jj in (0, 1):
                kb = k_ref[pl.ds(s0s[jj], T), :]
                cab = ca_ref[pl.ds(s0s[jj], T), :]
                vbs.append(v_ref[pl.ds(s0s[jj], T), :])
                for e in (0, 1):
                    kaug[jj, e] = jnp.where(heads[e], kb, cab)
                    ks[jj, e] = jnp.where(heads[e], kb, jnp.zeros_like(kb)) * QK_SCALE

            def q_step(qi, blocks):
                t0 = pl.multiple_of(qi * T, T)
                qs = q_ref[pl.ds(t0, T), :] * QK_SCALE
                dob = do_ref[pl.ds(t0, T), :]
                doh = [jnp.where(heads[e], dob, jnp.zeros_like(dob)) for e in (0, 1)]
                qh = [jnp.where(heads[e], qs, jnp.zeros_like(qs)) for e in (0, 1)]
                chains = [(e, jj, masked) for e in (0, 1) for jj, masked in blocks]
                scores = [_dot_nt(kaug[jj, e], jnp.where(heads[e], qs, ones3)) for e, jj, _ in chains]
                dps = [_dot_nt(vbs[jj], doh[e]) for e, jj, _ in chains]
                dq = jnp.zeros((T, PAIR), F32)
                drow = [jnp.zeros((1, T), F32), jnp.zeros((1, T), F32)]
                for n, (e, jj, masked) in enumerate(chains):
                    st = jnp.where(keep, scores[n], NEG) if masked else scores[n]
                    pt = jnp.exp(st - lse_ref[e:e + 1, pl.ds(t0, T)])
                    dv_acc[jj] += _dot(pt.astype(BF16), doh[e])
                    dst = pt * (dps[n] - dd_s[e:e + 1, pl.ds(t0, T)])
                    dsb = dst.astype(BF16)
                    dk_acc[jj] += _dot(dsb, qh[e])
                    dq = dq + _dot_tn(dsb, ks[jj, e])
                    dcol_acc[pl.ds(s0s[jj], T), :] -= jnp.where(lane == e, jnp.sum(dst, axis=-1, keepdims=True), 0.0)
                    drow[e] = drow[e] + jnp.sum(dst, axis=0, keepdims=True)
                for e in (0, 1):
                    dc_ref[e:e + 1, pl.ds(t0, T)] += drow[e]
                dq_acc[pl.ds(t0, T), :] += dq

            q_step(2 * b, [(0, True)])
            q_step(2 * b + 1, [(0, False), (1, True)])

            def rest(i, _):
                q_step(2 * b + 2 + 2 * i, [(0, False), (1, False)])
                q_step(2 * b + 3 + 2 * i, [(0, False), (1, False)])
                return 0

            lax.fori_loop(0, (nq - 2) // 2 - b, rest, 0)
            for jj in (0, 1):
                dk_ref[pl.ds(s0s[jj], T), :] = dk_acc[jj].astype(BF16)
                dv_ref[pl.ds(s0s[jj], T), :] = dv_acc[jj].astype(BF16)
            return 0

        lax.fori_loop(0, nq // 2, kv_pair, 0)
        dq_ref[...] = dq_acc[...].astype(BF16)

        def add_cols(kj, _):
            s0 = pl.multiple_of(kj * T, T)
            dc_ref[:, pl.ds(s0, T)] += dcol_acc[pl.ds(s0, T), :].T[0:2, :]
            return 0

        lax.fori_loop(0, nq, add_cols, 0)

    col = pl.BlockSpec((S, PAIR), lambda hp: (0, hp))
    row2 = pl.BlockSpec((None, 2, S), lambda hp: (hp, 0, 0))
    return _pcall(
        body, name=name, grid=(NP,), comm=comm,
        out_shape=(jax.ShapeDtypeStruct((S, DM), BF16), jax.ShapeDtypeStruct((S, DM), BF16),
                   jax.ShapeDtypeStruct((S, DM), BF16), jax.ShapeDtypeStruct((NP, 2, S), F32)),
        in_specs=[col, col, col, col, col, col, row2], out_specs=(col, col, col, row2),
        scratch_shapes=[pltpu.VMEM((S, PAIR), F32), pltpu.VMEM((S, LANES), F32), pltpu.VMEM((2, T, PAIR), F32),
                        pltpu.VMEM((2, T, PAIR), F32), pltpu.VMEM((8, S), F32)],
        args=(q, k, v, o, do, ca, lse))


def _mixer_b_fwd(x, o, qm, mkv, wo, g, b, name):
    S, D = x.shape
    DQ = qm.shape[1]
    DM = D - DQ
    tm = min(512, S)

    def body(x_ref, o_ref, qm_ref, mkv_ref, wo_ref, g_ref, b_ref, xo_ref, xbt_ref, r_ref, catt_ref, cat_ref):
        cat_ref[:, :DM] = o_ref[...]
        mo = _mem_attn_fwd(qm_ref[...], mkv_ref, DQ)
        for mp in range(DQ // PAIR):
            cat_ref[:, DM + mp * PAIR:DM + (mp + 1) * PAIR] = mo[mp].astype(BF16)
        _tail_fwd_store(x_ref[...], cat_ref, wo_ref, g_ref, b_ref, xo_ref, xbt_ref, r_ref, catt_ref)

    return pl.pallas_call(
        body, name=name, grid=(S // tm,),
        out_shape=(jax.ShapeDtypeStruct((S, D), F32), jax.ShapeDtypeStruct((D, S), BF16),
                   jax.ShapeDtypeStruct((S, D), F32), jax.ShapeDtypeStruct((D, S), BF16)),
        in_specs=[_rows(tm, D), _rows(tm, DM), _rows(tm, DQ), _full(mkv.shape), _full(wo.shape), _full((1, D)),
                  _full((1, D))],
        out_specs=(_rows(tm, D), _cols(D, tm), _rows(tm, D), _cols(D, tm)),
        scratch_shapes=[pltpu.VMEM((tm, D), BF16)],
        compiler_params=_params(),
    )(x, o, qm, mkv, wo, g, b)


def _mixer_b_bwd(dxo, r, qm, mkv, wo, g, name):
    S, D = r.shape
    DQ = qm.shape[1]
    DM = D - DQ
    M = mkv.shape[0]
    tm = min(512, S)

    def body(dxo_ref, r_ref, qm_ref, mkv_ref, wo_ref, g_ref, dr_ref, drb_ref, dcat_ref, dgb_ref, dmkv_ref):
        @pl.when(pl.program_id(0) == 0)
        def _():
            dgb_ref[...] = jnp.zeros_like(dgb_ref)
            dmkv_ref[...] = jnp.zeros_like(dmkv_ref)

        dr, dcat = _tail_bwd_head(dxo_ref, r_ref, g_ref, wo_ref, dgb_ref, drb_ref)
        dr_ref[...] = dr
        dcat_ref[:, :DM] = dcat[:, :DM].astype(BF16)
        dmo = [dcat[:, DM + mp * PAIR:DM + (mp + 1) * PAIR] for mp in range(DQ // PAIR)]
        dqm = _mem_attn_bwd(qm_ref[...], mkv_ref, dmo, dmkv_ref, DQ)
        for mp in range(DQ // PAIR):
            dcat_ref[:, DM + mp * PAIR:DM + (mp + 1) * PAIR] = dqm[mp].astype(BF16)

    return pl.pallas_call(
        body, name=name, grid=(S // tm,),
        out_shape=(jax.ShapeDtypeStruct((S, D), F32), jax.ShapeDtypeStruct((S, D), BF16),
                   jax.ShapeDtypeStruct((S, D), BF16), jax.ShapeDtypeStruct((2, D), F32),
                   jax.ShapeDtypeStruct((M, 2 * DQ), F32)),
        in_specs=[_rows(tm, D), _rows(tm, D), _rows(tm, DQ), _full(mkv.shape), _full(wo.shape), _full((1, D))],
        out_specs=(_rows(tm, D), _rows(tm, D), _rows(tm, D), _full((2, D)), _full((M, 2 * DQ))),
        compiler_params=_params(),
    )(dxo, r, qm, mkv, wo, g)


class _NoExchange:
    def plan(self, stage, w, grads):
        return None

    def done(self, stage, results, w):
        pass


def _local_step(x, mem, target, w, hooks):
    S, D = x.shape
    DQ = w["mkv0"].shape[1] // 2
    DM = D - DQ
    NG = DM // HEAD_DIM
    NP = DM // PAIR
    ln_g, ln_b = w["ln_g"], w["ln_b"]
    row = lambda a, i: a[i:i + 1]
    memb = mem.astype(BF16)
    bst =jnp.pad(w["b_s"].T, ((0, 0), (0, LANES - NG)))
    grads = {}

    def staged(stage, fn, *args):
        out, got = fn(*args, stage, comm=hooks.plan(stage, w, grads))
        hooks.done(stage, got, w)
        return out

    mkv0 = _mm_small(memb, w["mkv0"], "mem_kv_0")
    x1, x1t, r1, cat0t, x0t = staged("mixer_a_fwd", _mixer_a_fwd, x, w["wa"], w["sg"], w["sb"], w["ws"], bst, mkv0,
                                w["wo0"], row(ln_g, 0), row(ln_b, 0))
    x2, x2b, x2t, r2, h0 = staged("mlp_fwd_0", _mlp_fwd, x1, w["wup0"], w["wdown0"], row(ln_g, 1), row(ln_b, 1))
    q, qm, k, v, fl = _proj_b_fwd(x2b, w["wq"], w["wkv"], DM, "proj_b_fwd")
    flt = fl[:, :16].T
    bfc = jnp.pad(w["bf"], (0, 16 - NG)).reshape(16, 1)
    pieces = _forget_cumsum(flt, bfc, "forget_cumsum")
    pieces = jnp.pad(pieces[:, :NG].transpose(2, 1, 0), ((0, 0), (0, 0), (0, HEAD_DIM - N_BIAS)))
    ca = pieces.reshape(S, NP, 2, HEAD_DIM)[:, :, ::-1].reshape(S, DM).astype(BF16)
    o, lse = staged("fox_fwd", _fox_fwd, q, k, v, ca)
    mkv1 = _mm_small(memb, w["mkv1"], "mem_kv_1")
    x3, x3t, r3, cat1t = _mixer_b_fwd(x2, o, qm, mkv1, w["wo1"], row(ln_g, 2), row(ln_b, 2), "mixer_b_fwd")
    dx4, r4, h1, loss_part = _mlp_fwd_loss(x3, w["wup1"], w["wdown1"], row(ln_g, 3), row(ln_b, 3), target,
                                           "mlp_fwd_1")

    nc = w["wup0"].shape[0]
    (dx3, dr4t, dpre1, dgb11), _ = _mlp_bwd(dx4, r4, h1, w["wup1"], w["wdown1"], row(ln_g, 3), "mlp_bwd_1")
    grads["wdown1"] = _mm_nn(dr4t, h1, "dw_down_1", out_split=nc)
    grads["wup1"] = _mm_nn(x3t, dpre1, "dw_up_1", out_split=nc)
    dr3, dr3b, dcat1, dgb10, dmkv1 = _mixer_b_bwd(dx3, r3, qm, mkv1, w["wo1"], row(ln_g, 2), "mixer_b_bwd")
    do = dcat1[:, :DM]
    dq, dk, dv, dct3 = staged("fox_bwd", _fox_bwd, q, k, v, o, do, ca, lse)
    dct = jnp.pad(dct3.reshape(NG, S), ((0, 16 - NG), (0, 0)))
    dflt, dbf = _forget_cumsum_bwd(dct, flt, bfc, "forget_cumsum_bwd")
    dfl = jnp.pad(dflt.T, ((0, 0), (0, LANES - 16))).astype(BF16)
    dqz = jnp.concatenate([dq, dcat1[:, DM:]], axis=1)
    dkvf = jnp.concatenate([dk, dv, dfl], axis=1)
    dx2 = _proj_b_bwd(dr3, dqz, dkvf, w["wq"], w["wkv"], "proj_b_bwd")
    grads["wo1"] = _mm_nn(cat1t, dr3b, "dw_o_1")
    grads["wq"] = _mm_nn(x2t, dqz, "dw_q")
    grads["wkv"] = _mm_nn(x2t, dkvf, "dw_kv")
    grads["mkv1"] = _mm_tn(memb, dmkv1.astype(BF16), "dw_mkv_1")
    dx1, dr2t, dpre0, dgb01 = staged("mlp_bwd_0", _mlp_bwd, dx2, r2, h0, w["wup0"], w["wdown0"], row(ln_g, 1))
    grads["wdown0"] = _mm_nn(dr2t, h0, "dw_down_0", out_split=nc)
    grads["wup0"] = _mm_nn(x1t, dpre0, "dw_up_0", out_split=nc)
    dx0, dz, dr1b, dgb00, dsgb, dws, dbst, dmkv0 = staged(
        "mixer_a_bwd", _mixer_a_bwd, x, dx1, r1, w["wa"], w["sg"], w["sb"], w["ws"], bst, mkv0, w["wo0"], row(ln_g, 0))
    grads["wo0"] = _mm_nn(cat0t, dr1b, "dw_o_0")
    grads["wa"] = _mm_nn(x0t, dz, "dw_a")
    grads["mkv0"] = _mm_tn(memb, dmkv0.astype(BF16), "dw_mkv_0")
    grads.update({
        "ws": dws, "b_s": dbst[:, :NG].T, "sg": dsgb[0:1], "sb": dsgb[1:2], "bf": dbf[:NG, 0],
        "ln_g": jnp.concatenate([dgb00[0:1], dgb01[0:1], dgb10[0:1], dgb11[0:1]], axis=0),
        "ln_b": jnp.concatenate([dgb00[1:2], dgb01[1:2], dgb10[1:2], dgb11[1:2]], axis=0),
    })
    return loss_part, dx0, grads


def _place():
    xi, yi, ci = lax.axis_index("x"), lax.axis_index("y"), lax.axis_index("c")
    peers = [(xi, 1 - yi), (1 - xi, yi), (1 - xi, 1 - yi)]
    return xi, yi, ci, peers


def _comm_call(body, name, ins, out_shapes, n_remote, n_local):
    return pl.pallas_call(
        body, name=name, out_shape=out_shapes, in_specs=[ANY] * len(ins), out_specs=[ANY] * len(out_shapes),
        scratch_shapes=[pltpu.SemaphoreType.DMA((n_remote,)), pltpu.SemaphoreType.DMA((n_remote,)),
                        pltpu.SemaphoreType.DMA((max(n_local, 1),))],
        compiler_params=pltpu.CompilerParams(has_side_effects=True),
    )(*ins)


def _pair_swap(xs, name):
    n = len(xs)
    out_shapes = [jax.ShapeDtypeStruct((a.shape[0],) + a.shape[2:], a.dtype) for a in xs]

    def body(*refs):
        x_refs, o_refs = refs[:n], refs[n:2 * n]
        send_sems, recv_sems, _ = refs[2 * n:]
        xi, yi, ci, _ = _place()
        cps = []
        for i in range(n):
            cp = pltpu.make_async_remote_copy(src_ref=x_refs[i].at[:, 1 - ci], dst_ref=o_refs[i],
                                              send_sem=send_sems.at[i], recv_sem=recv_sems.at[i],
                                              device_id=(xi, yi, 1 - ci), device_id_type=MESH)
            cp.start()
            cps.append(cp)
        for cp in cps:
            cp.wait_recv()
        for cp in cps:
            cp.wait_send()

    return _comm_call(body, name, xs, out_shapes, n, 0)


def _chip_exchange(xs):
    n = len(xs)
    out_shapes = [jax.ShapeDtypeStruct((3,) + a.shape[1:], a.dtype) for a in xs]

    def copies(x_refs, o_refs, sems):
        send_sems, recv_sems = sems
        xi, yi, ci, peers = _place()
        return [pltpu.make_async_remote_copy(src_ref=x_refs[i].at[2 * px + py], dst_ref=o_refs[i].at[d],
                                             send_sem=send_sems.at[3 * i + d], recv_sem=recv_sems.at[3 * i + d],
                                             device_id=(px, py, ci), device_id_type=MESH)
                for i in range(n) for d, (px, py) in enumerate(peers)]

    def start(x_refs, o_refs, sems):
        for cp in copies(x_refs, o_refs, sems):
            cp.start()

    def finish(x_refs, o_refs, sems):
        cps = copies(x_refs, o_refs, sems)
        for cp in cps:
            cp.wait_recv()
        for cp in cps:
            cp.wait_send()

    return _Comm(xs, out_shapes, {}, [3 * n, 3 * n], start, finish)


def _inplace_call(body, name, bufs, sem_counts):
    n = len(bufs)
    return pl.pallas_call(
        body, name=name, out_shape=[jax.ShapeDtypeStruct(a.shape, a.dtype) for a in bufs],
        in_specs=[ANY] * n, out_specs=[ANY] * n, input_output_aliases={i: i for i in range(n)},
        scratch_shapes=[pltpu.SemaphoreType.DMA((k,)) for k in sem_counts],
        compiler_params=pltpu.CompilerParams(has_side_effects=True),
    )(*bufs)


def _gather_weights(bufs, whole=()):
    n, nw = len(bufs), len(whole)

    def ici(w, sems, i, d, px, py, ci, slot):
        ref = w[i].at[slot, ci] if i < n else w[i].at[slot]
        return pltpu.make_async_remote_copy(src_ref=ref, dst_ref=ref, send_sem=sems[0].at[3 * i + d],
                                            recv_sem=sems[1].at[3 * i + d], device_id=(px, py, ci), device_id_type=MESH)

    def d2d(w, sems, i, d, xi, yi, ci, slot, half):
        ref = w[i].at[slot, half]
        return pltpu.make_async_remote_copy(src_ref=ref, dst_ref=ref, send_sem=sems[2].at[3 * i + d],
                                            recv_sem=sems[3].at[3 * i + d], device_id=(xi, yi, 1 - ci),
                                            device_id_type=MESH)

    def start(_, w, sems):
        xi, yi, ci, peers = _place()
        for i in range(n + nw):
            for d, (px, py) in enumerate(peers):
                ici(w, sems, i, d, px, py, ci, 2 * xi + yi).start()

    def finish(_, w, sems):
        xi, yi, ci, peers = _place()
        for d, (px, py) in enumerate(peers):
            for i in range(n + nw):
                ici(w, sems, i, d, px, py, ci, 2 * px + py).wait_recv()
                if i < n:
                    d2d(w, sems, i, d, xi, yi, ci, 2 * px + py, ci).start()
        for d, (px, py) in enumerate(peers):
            for i in range(n):
                d2d(w, sems, i, d, xi, yi, ci, 2 * px + py, 1 - ci).wait_recv()
        for d, (px, py) in enumerate(peers):
            for i in range(n + nw):
                ici(w, sems, i, d, px, py, ci, 2 * xi + yi).wait_send()
                if i < n:
                    d2d(w, sems, i, d, xi, yi, ci, 2 * px + py, ci).wait_send()

    arrays = list(bufs) + list(whole)
    k = 3 * (n + nw)
    return _Comm(arrays, [jax.ShapeDtypeStruct(a.shape, a.dtype) for a in arrays], {i: i for i in range(n + nw)},
                 [k, k, max(3 * n, 1), max(3 * n, 1)], start, finish)


def _pair_gather(bufs, spread, name):
    n = len(bufs)
    where = [(i, l) for i, a in enumerate(bufs) for l in range(a.shape[0])]
    nw = len(where)

    def body(*refs):
        g = refs[n + 1:2 * n + 1]
        sp = refs[2 * n + 1]
        send_sems, recv_sems = refs[2 * n + 2:]
        xi, yi, ci, peers = _place()
        chips = [(xi, yi)] + peers
        others = [(d, f) for d in range(4) for f in (0, 1) if (d, f) != (0, 0)]
        sends = []
        for k, (i, l) in enumerate(where):
            mine = g[i].at[l, :, ci]
            cp = pltpu.make_async_remote_copy(src_ref=mine, dst_ref=mine, send_sem=send_sems.at[k],
                                              recv_sem=recv_sems.at[k], device_id=(xi, yi, 1 - ci), device_id_type=MESH)
            cp.start()
            sends.append(cp)
        mine = sp.at[2 * xi + yi, ci]
        for k, (d, f) in enumerate(others):
            cp = pltpu.make_async_remote_copy(src_ref=mine, dst_ref=mine, send_sem=send_sems.at[nw + k],
                                              recv_sem=recv_sems.at[nw + k],
                                              device_id=(chips[d][0], chips[d][1], ci if f == 0 else 1 - ci),
                                              device_id_type=MESH)
            cp.start()
            sends.append(cp)
        for k, (i, l) in enumerate(where):
            other = g[i].at[l, :, 1 - ci]
            pltpu.make_async_remote_copy(src_ref=other, dst_ref=other, send_sem=send_sems.at[k],
                                         recv_sem=recv_sems.at[k], device_id=(xi, yi, 1 - ci),
                                         device_id_type=MESH).wait_recv()
        for k, (d, f) in enumerate(others):
            px, py, pc = chips[d][0], chips[d][1], (ci if f == 0 else 1 - ci)
            theirs = sp.at[2 * px + py, pc]
            pltpu.make_async_remote_copy(src_ref=theirs, dst_ref=theirs, send_sem=send_sems.at[nw + k],
                                         recv_sem=recv_sems.at[nw + k], device_id=(px, py, pc),
                                         device_id_type=MESH).wait_recv()
        for cp in sends:
            cp.wait_send()

    out = _inplace_call(body, name, list(bufs) + [spread], [nw + 7, nw + 7])
    return out[:n], out[n]


def _row_block(rows, cols):
    want = max(8, (2 ** 18 // max(cols, 1)) // 8 * 8)
    if rows <= want:
        return rows
    best = 8
    for t in range(8, want + 1, 8):
        if rows % t == 0:
            best = t
    return best


def _pair_add(g4, recv, sel, out_dtype, name):
    A, _, H, C = g4.shape
    bh = _row_block(H, C)

    def body(sel_ref, g_ref, r_ref, o_ref):
        o_ref[...] = (g_ref[...].astype(F32) + r_ref[...].astype(F32)).astype(out_dtype)

    return pl.pallas_call(
        body, name=name, out_shape=jax.ShapeDtypeStruct((A, H, C), out_dtype),
        grid_spec=pltpu.PrefetchScalarGridSpec(
            num_scalar_prefetch=1, grid=(A, H // bh),
            in_specs=[pl.BlockSpec((None, None, bh, C), lambda a, i, s: (a, s[1], i, 0)),
                      pl.BlockSpec((None, bh, C), lambda a, i, s: (a, i, 0))],
            out_specs=pl.BlockSpec((None, bh, C), lambda a, i, s: (a, i, 0))),
        compiler_params=pltpu.CompilerParams(dimension_semantics=("arbitrary", "arbitrary"),
                                             vmem_limit_bytes=VMEM_LIMIT),
    )(sel, g4, recv)


def _chip_reduce(g4, recv1, recv2, sel, buf, layer, n_layers, name, by_chip=False):
    _, _, H, C = g4.shape
    bh = _row_block(H, C)
    if by_chip:
        out_shape = jax.ShapeDtypeStruct((N_CHIPS, 2, H, C), F32)
        out_spec = pl.BlockSpec((None, None, bh, C), lambda i, s: (s[0], s[1], i, 0))
    else:
        out_shape = jax.ShapeDtypeStruct((n_layers, 1, 2, H, C), F32)
        out_spec = pl.BlockSpec((None, None, None, bh, C), lambda i, s: (layer, 0, s[1], i, 0))

    def body(sel_ref, g_ref, r1_ref, r2_ref, *rest):
        acc = g_ref[...].astype(F32) + r1_ref[...].astype(F32)
        for d in range(3):
            acc = acc + r2_ref[d].astype(F32)
        rest[-1][...] = acc

    in_specs = [pl.BlockSpec((None, None, bh, C), lambda i, s: (s[0], s[1], i, 0)),
                pl.BlockSpec((None, bh, C), lambda i, s: (s[0], i, 0)),
                pl.BlockSpec((3, bh, C), lambda i, s: (0, i, 0))]
    args = [sel, g4, recv1, recv2]
    aliases = {}
    if buf is not None:
        in_specs.append(ANY)
        args.append(buf)
        aliases = {4: 0}
    return pl.pallas_call(
        body, name=name, out_shape=out_shape,
        grid_spec=pltpu.PrefetchScalarGridSpec(num_scalar_prefetch=1, grid=(H // bh,), in_specs=in_specs,
                                               out_specs=out_spec),
        input_output_aliases=aliases,
        compiler_params=pltpu.CompilerParams(dimension_semantics=("arbitrary",), vmem_limit_bytes=VMEM_LIMIT),
    )(*args)


def _cast_place(shard2, sel, name):
    _, H, C = shard2.shape
    bh = _row_block(H, C)

    def body(sel_ref, x_ref, o_ref):
        o_ref[...] = x_ref[...].astype(BF16)

    return pl.pallas_call(
        body, name=name, out_shape=jax.ShapeDtypeStruct((N_CHIPS, 2, H, C), BF16),
        grid_spec=pltpu.PrefetchScalarGridSpec(
            num_scalar_prefetch=1, grid=(2, H // bh),
            in_specs=[pl.BlockSpec((None, bh, C), lambda h, i, s: (h, i, 0))],
            out_specs=pl.BlockSpec((None, None, bh, C), lambda h, i, s: (s[0], h, i, 0))),
        compiler_params=pltpu.CompilerParams(dimension_semantics=("arbitrary", "arbitrary"),
                                             vmem_limit_bytes=VMEM_LIMIT),
    )(sel, shard2)


def _adamw(g, w, m, v, name):
    R, C = g.shape
    br = _row_block(R, C)

    def body(g_ref, w_ref, m_ref, v_ref, d_ref, mo_ref, vo_ref):
        gv = g_ref[...]
        mn = ADAM_B1 * m_ref[...] + (1.0 - ADAM_B1) * gv
        vn = ADAM_B2 * v_ref[...] + (1.0 - ADAM_B2) * (gv * gv)
        m_hat = mn / (1.0 - ADAM_B1 ** ADAM_STEP)
        v_hat = vn / (1.0 - ADAM_B2 ** ADAM_STEP)
        d_ref[...] = -ADAM_LR * (m_hat / (jnp.sqrt(v_hat) + ADAM_EPS) + ADAM_WD * w_ref[...])
        mo_ref[...] = mn
        vo_ref[...] = vn

    spec = pl.BlockSpec((br, C), lambda i: (i, 0))
    return pl.pallas_call(
        body, name=name, grid=(R // br,), out_shape=(jax.ShapeDtypeStruct((R, C), F32),) * 3,
        in_specs=[spec] * 4, out_specs=(spec,) * 3, compiler_params=_params(),
    )(g, w, m, v)


def _part_rows(shape):
    return -(-math.prod(shape) // (8 * LANES)) * 8


def _pack_rows(arrs, total_rows):
    parts = []
    for a in arrs:
        flat = a.reshape(-1)
        rows = _part_rows(a.shape)
        parts.append(jnp.pad(flat, (0, rows * LANES - flat.shape[0])).reshape(rows, LANES))
    packed = jnp.concatenate(parts, axis=0)
    return jnp.pad(packed, ((0, total_rows - packed.shape[0]), (0, 0)))


def _unpack_rows(packed, shapes):
    out, r = [], 0
    for shp in shapes:
        size, rows = math.prod(shp), _part_rows(shp)
        out.append(packed[r:r + rows].reshape(-1)[:size].reshape(shp))
        r += rows
    return out


def _rows_of(shapes):
    return sum(_part_rows(s) for s in shapes)


_GATHER_STAGES = {"start": ["wa", "mkv0", "wo0"], "mixer_a_fwd": ["wup0", "wdown0"],
                  "mlp_fwd_0": ["wq", "wkv", "mkv1", "wo1"], "fox_fwd": ["wup1", "wdown1"]}
_REDUCE_STAGES = {"fox_bwd": ["wup1", "wdown1"], "mlp_bwd_0": ["wo1", "wq", "wkv", "mkv1"],
                  "mixer_a_bwd": ["wup0", "wdown0"], "end": ["wa", "wo0", "mkv0", "small"]}
_SMALL = ["ws", "b_s", "sg", "sb", "bf", "ln_g", "ln_b"]


class _Staged:
    def __init__(self, placed, sel, dm, ng, shard_cols, n_small):
        self.placed, self.sel, self.dm, self.ng, self.shard_cols, self.n_small = placed, sel, dm, ng, shard_cols, n_small
        self.g4, self.recv1, self.recv2 = {}, {}, {}

    def _weight(self, name, a):
        a = a.reshape(N_CHIPS, 2 * a.shape[2], a.shape[3])
        if name in ("wa", "wkv"):
            a = a.transpose(1, 0, 2).reshape(a.shape[1], -1)
            if name == "wkv":
                a = jnp.concatenate([a[:, :2 * self.dm],
                                     jnp.pad(a[:, 2 * self.dm:], ((0, 0), (0, LANES - self.ng)))], axis=1)
            return a
        if name.startswith("wup") or name.startswith("wdown"):
            return a
        return a.reshape(-1, a.shape[2])

    def _partial(self, name, grads):
        if name == "small":
            a = _pack_rows([grads[n] for n in _SMALL], self.n_small).reshape(N_CHIPS, -1, LANES)
        elif name in self.shard_cols:
            n = self.shard_cols[name]
            g = grads[name][:, :N_CHIPS * n]
            a = g.reshape(g.shape[0], N_CHIPS, n).transpose(1, 0, 2)
        elif name.startswith("wup") or name.startswith("wdown"):
            a = grads[name]
        else:
            g = grads[name]
            a = g.reshape(N_CHIPS, g.shape[0] // N_CHIPS, g.shape[1])
        return a.reshape(N_CHIPS, 2, a.shape[1] // 2, a.shape[2])

    def plan(self, stage, w, grads):
        if stage in _GATHER_STAGES:
            return _gather_weights([self.placed[n] for n in _GATHER_STAGES[stage]])
        if stage in _REDUCE_STAGES:
            names = _REDUCE_STAGES[stage]
            g4 = [self._partial(n, grads) for n in names]
            recv1 = _pair_swap(g4, "reduce_pair_swap_" + stage)
            summed = []
            for n, g, r1 in zip(names, g4, recv1):
                self.g4[n], self.recv1[n] = g, r1
                summed.append(_pair_add(g, r1, self.sel, F32 if n == "small" else BF16, "reduce_pair_add_" + n))
            return _chip_exchange(summed)
        return None

    def done(self, stage, results, w):
        if stage in _GATHER_STAGES:
            for n, a in zip(_GATHER_STAGES[stage], results):
                w[n] = self._weight(n, a)
        elif stage in _REDUCE_STAGES:
            for n, r in zip(_REDUCE_STAGES[stage], results):
                self.recv2[n] = r


def kernel(x, mem, a_w_in, a_sgu_ln_g, a_sgu_ln_b, a_w_s, a_b_s, kv_w, kv_b_f, b_w_q, mem_w_kv, w_o, ln_g, ln_b, w_up, w_down, loss_target, m_a_w_in, m_a_sgu_ln_g, m_a_sgu_ln_b, m_a_w_s, m_a_b_s, m_kv_w, m_kv_b_f, m_b_w_q, m_mem_w_kv, m_w_o, m_ln_g, m_ln_b, m_w_up, m_w_down, v_a_w_in, v_a_sgu_ln_g, v_a_sgu_ln_b, v_a_w_s, v_a_b_s, v_kv_w, v_kv_b_f, v_b_w_q, v_mem_w_kv, v_w_o, v_ln_g, v_ln_b, v_w_up, v_w_down):
    xi, yi, ci = lax.axis_index("x"), lax.axis_index("y"), lax.axis_index("c")
    chip = 2 * xi + yi
    sel = jnp.stack([chip, ci]).astype(jnp.int32)
    S, D = x.shape[1], x.shape[2]
    DQ = mem_w_kv.shape[2] // 2
    DM = D - DQ
    NG = DM // HEAD_DIM

    shards = {"wa": a_w_in[0], "wkv": kv_w, "wq": b_w_q[0], "mkv0": mem_w_kv[0], "mkv1": mem_w_kv[1],
              "wo0": w_o[0], "wo1": w_o[1], "wup0": w_up[0], "wup1": w_up[1], "wdown0": w_down[0], "wdown1": w_down[1]}
    placed = {n: _cast_place(s.reshape(2, s.shape[0] // 2, s.shape[1]), sel, "cast_place_" + n)
              for n, s in shards.items()}
    ln_pack = jnp.concatenate([ln_g.reshape(4, -1), ln_b.reshape(4, -1)], axis=0)
    ln_buf = lax.dynamic_update_slice(jnp.zeros((N_CHIPS,) + ln_pack.shape, F32), ln_pack[None], (chip, 0, 0))
    small_shapes = [a_w_s.shape, a_b_s.shape, a_sgu_ln_g.shape, a_sgu_ln_b.shape, kv_b_f.shape]
    ln_shape = (4, D)
    n_small = -(-_rows_of(small_shapes + [ln_shape, ln_shape]) // 64) * 64
    hooks = _Staged(placed, sel, DM, NG, {"wa": a_w_in.shape[2], "wkv": kv_w.shape[1]}, n_small)
    weights = {"sg": a_sgu_ln_g, "sb": a_sgu_ln_b, "ws": a_w_s[0], "b_s": a_b_s[0], "bf": kv_b_f}
    got = _run_comm(_gather_weights([placed[n] for n in _GATHER_STAGES["start"]], whole=[ln_buf]),
                    "gather_weights_first")
    hooks.done("start", got[:-1], weights)
    ln_full = got[-1].transpose(1, 0, 2).reshape(8, D)
    weights["ln_g"], weights["ln_b"] = ln_full[:4], ln_full[4:]

    loss_part, grad_x, gr = _local_step(x[0], mem[0], loss_target[0], weights, hooks)
    loss = lax.psum(loss_part[0, 0], ("x", "y", "c"))

    hooks.done("end", _run_comm(hooks.plan("end", weights, gr), "reduce_chip_exchange_end"), weights)
    groups = [["wa"], ["wkv"], ["wq"], ["mkv0", "mkv1"], ["wo0", "wo1"], ["wup0", "wup1"], ["wdown0", "wdown1"]]
    mine = []
    for grp in groups:
        buf = None
        for layer, n in enumerate(grp):
            buf = _chip_reduce(hooks.g4[n], hooks.recv1[n], hooks.recv2[n], sel, buf, layer, len(grp),
                               "reduce_chip_sum_" + n)
        mine.append(buf)
    small_mine = _chip_reduce(hooks.g4["small"], hooks.recv1["small"], hooks.recv2["small"], sel, None, 0, 1,
                              "reduce_chip_sum_small", by_chip=True)
    red, small_all = _pair_gather(mine, small_mine, "reduce_pair_gather")
    red = [a.reshape(a.shape[0], 2 * a.shape[3], a.shape[4]) for a in red]
    g_a_w_in, g_kv_w, g_b_w_q, g_mem_w_kv, g_w_o, g_w_up, g_w_down = red
    g_kv_w = g_kv_w[0]
    g_w_down = g_w_down.transpose(0, 2, 1)
    small_all = small_all.reshape(n_small, LANES)
    g_ws, g_bs, g_sg, g_sb, g_bf, g_lng, g_lnb = _unpack_rows(small_all, small_shapes + [ln_shape, ln_shape])
    dsh = D // N_CHIPS
    g_ln_g = lax.dynamic_slice_in_dim(g_lng, chip * dsh, dsh, axis=1).reshape(ln_g.shape)
    g_ln_b = lax.dynamic_slice_in_dim(g_lnb, chip * dsh, dsh, axis=1).reshape(ln_b.shape)

    def update(name, g, w, m, v):
        flat = lambda a: a.reshape(-1, a.shape[-1])
        d, mn, vn = _adamw(flat(g), flat(w), flat(m), flat(v), "adamw_" + name)
        return d.reshape(w.shape), mn.reshape(w.shape), vn.reshape(w.shape)

    upd = {
        "a_w_in": update("a_w_in", g_a_w_in, a_w_in, m_a_w_in, v_a_w_in),
        "kv_w": update("kv_w", g_kv_w, kv_w, m_kv_w, v_kv_w),
        "b_w_q": update("b_w_q", g_b_w_q, b_w_q, m_b_w_q, v_b_w_q),
        "mem_w_kv": update("mem_w_kv", g_mem_w_kv, mem_w_kv, m_mem_w_kv, v_mem_w_kv),
        "w_o": update("w_o", g_w_o, w_o, m_w_o, v_w_o),
        "w_up": update("w_up", g_w_up, w_up, m_w_up, v_w_up),
        "w_down": update("w_down", g_w_down, w_down, m_w_down, v_w_down),
    }
    tiny_g = [g_sg, g_sb, g_ws, g_bs, g_bf, g_ln_g, g_ln_b]
    tiny_w = [a_sgu_ln_g, a_sgu_ln_b, a_w_s, a_b_s, kv_b_f, ln_g, ln_b]
    tiny_m = [m_a_sgu_ln_g, m_a_sgu_ln_b, m_a_w_s, m_a_b_s, m_kv_b_f, m_ln_g, m_ln_b]
    tiny_v = [v_a_sgu_ln_g, v_a_sgu_ln_b, v_a_w_s, v_a_b_s, v_kv_b_f, v_ln_g, v_ln_b]
    tiny_shapes = [a.shape for a in tiny_w]
    n_tiny = -(-_rows_of(tiny_shapes) // 8) * 8
    td, tm_, tv = _adamw(_pack_rows(tiny_g, n_tiny), _pack_rows(tiny_w, n_tiny), _pack_rows(tiny_m, n_tiny),
                         _pack_rows(tiny_v, n_tiny), "adamw_small")
    for name, d, mn, vn in zip(["a_sgu_ln_g", "a_sgu_ln_b", "a_w_s", "a_b_s", "kv_b_f", "ln_g", "ln_b"],
                               _unpack_rows(td, tiny_shapes), _unpack_rows(tm_, tiny_shapes),
                               _unpack_rows(tv, tiny_shapes)):
        upd[name] = (d, mn, vn)

    order = ["a_w_in", "a_sgu_ln_g", "a_sgu_ln_b", "a_w_s", "a_b_s", "kv_w", "kv_b_f", "b_w_q", "mem_w_kv", "w_o",
             "ln_g", "ln_b", "w_up", "w_down"]
    grads = {"a_w_in": g_a_w_in.reshape(a_w_in.shape), "a_sgu_ln_g": g_sg.reshape(a_sgu_ln_g.shape),
             "a_sgu_ln_b": g_sb.reshape(a_sgu_ln_b.shape), "a_w_s": g_ws.reshape(a_w_s.shape),
             "a_b_s": g_bs.reshape(a_b_s.shape), "kv_w": g_kv_w, "kv_b_f": g_bf, "b_w_q": g_b_w_q.reshape(b_w_q.shape),
             "mem_w_kv": g_mem_w_kv, "w_o": g_w_o, "ln_g": g_ln_g, "ln_b": g_ln_b, "w_up": g_w_up, "w_down": g_w_down}
    return (loss, grad_x[None], *[grads[n] for n in order], *[upd[n][0] for n in order],
            *[upd[n][1] for n in order], *[upd[n][2] for n in order])
```

```python
import functools
import math

import jax
import jax.numpy as jnp
from jax import lax
from jax.experimental import pallas as pl
from jax.experimental.pallas import tpu as pltpu

F32 = jnp.float32
BF16 = jnp.bfloat16

HEAD_DIM = 64
PAIR = 2 * HEAD_DIM
CHUNK = 128
LN_EPS = 1e-5
ALPHA = 4 ** 0.25
QK_SCALE = 1.0 / math.sqrt(HEAD_DIM)
NEG = -1e30
N_CHIPS = 4
FOX_BLOCK = 256
LANES = 128
VMEM_LIMIT = 48 * 2 ** 20

ADAM_LR, ADAM_B1, ADAM_B2, ADAM_EPS, ADAM_WD, ADAM_STEP = 0.001, 0.9, 0.999, 1e-08, 0.01, 10

MESH = pl.DeviceIdType.MESH
ANY = pl.BlockSpec(memory_space=pl.ANY)


def _dot(a, b):
    return jnp.dot(a, b, preferred_element_type=F32)


def _dot_nt(a, b):
    return lax.dot_general(a, b, (((1,), (1,)), ((), ())), preferred_element_type=F32)


def _dot_tn(a, b):
    return lax.dot_general(a, b, (((0,), (0,)), ((), ())), preferred_element_type=F32)


def _ln_stats(r):
    mu = jnp.mean(r, axis=-1, keepdims=True)
    d = r - mu
    var = jnp.mean(d * d, axis=-1, keepdims=True)
    rstd = lax.rsqrt(var + LN_EPS)
    return d * rstd, rstd


def _ln_bwd(dy, xhat, rstd, g):
    dxh = dy * g
    m1 = jnp.mean(dxh, axis=-1, keepdims=True)
    m2 = jnp.mean(dxh * xhat, axis=-1, keepdims=True)
    return rstd * (dxh - m1 - xhat * m2)


_GELU_K = math.sqrt(2.0 / math.pi)


def _gelu(x):
    return 0.5 * x * (1.0 + jnp.tanh(_GELU_K * (x + 0.044715 * x * x * x)))


def _gelu_grad(x):
    t = jnp.tanh(_GELU_K * (x + 0.044715 * x * x * x))
    return 0.5 * (1.0 + t) + 0.5 * x * (1.0 - t * t) * _GELU_K * (1.0 + 3 * 0.044715 * x * x)


def _lane_mask(e):
    lane = lax.broadcasted_iota(jnp.int32, (1, PAIR), 1)
    return (lane >= HEAD_DIM * e) & (lane < HEAD_DIM * (e + 1))


def _first_head():
    return lax.broadcasted_iota(jnp.int32, (1, PAIR), 1) < HEAD_DIM


def _mem_probs(qh, mkp):
    sc = _dot_nt(qh, mkp) * QK_SCALE
    ex = jnp.exp(sc - jnp.max(sc, axis=-1, keepdims=True))
    return ex / jnp.sum(ex, axis=-1, keepdims=True)


def _mem_attn_fwd(qm_b, mkv_ref, dq_dim):
    outs = []
    for mp in range(dq_dim // PAIR):
        qp = qm_b[:, mp * PAIR:(mp + 1) * PAIR]
        mkp = mkv_ref[:, mp * PAIR:(mp + 1) * PAIR]
        mvp = mkv_ref[:, dq_dim + mp * PAIR:dq_dim + (mp + 1) * PAIR]
        heads = []
        for e in (0, 1):
            qh = jnp.where(_lane_mask(e), qp, jnp.zeros_like(qp))
            p = _mem_probs(qh, mkp)
            heads.append(_dot(p.astype(BF16), mvp))
        outs.append(jnp.where(_first_head(), heads[0], heads[1]))
    return outs


def _mem_attn_bwd(qm_b, mkv_ref, dmo_pairs, dmkv_ref, dq_dim):
    dqs = []
    for mp in range(dq_dim // PAIR):
        ks = slice(mp * PAIR, (mp + 1) * PAIR)
        vs = slice(dq_dim + mp * PAIR, dq_dim + (mp + 1) * PAIR)
        qp = qm_b[:, ks]
        mkp = mkv_ref[:, ks]
        mvp = mkv_ref[:, vs]
        dmo_b = dmo_pairs[mp].astype(BF16)
        dq = None
        dmk = None
        dmv = None
        for e in (0, 1):
            hm = _lane_mask(e)
            qh = jnp.where(hm, qp, jnp.zeros_like(qp))
            p = _mem_probs(qh, mkp)
            doh = jnp.where(hm, dmo_b, jnp.zeros_like(dmo_b))
            dp = _dot_nt(doh, mvp)
            ds = p * (dp - jnp.sum(dp * p, axis=-1, keepdims=True))
            dsb = (ds * QK_SCALE).astype(BF16)
            kh = jnp.where(hm, mkp, jnp.zeros_like(mkp))
            dq_e = _dot(dsb, kh)
            dmk_e = _dot_tn(dsb, qh)
            dmv_e = _dot_tn(p.astype(BF16), doh)
            dq = dq_e if dq is None else dq + dq_e
            dmk = dmk_e if dmk is None else dmk + dmk_e
            dmv = dmv_e if dmv is None else dmv + dmv_e
        dmkv_ref[:, ks] += dmk
        dmkv_ref[:, vs] += dmv
        dqs.append(dq)
    return dqs


def _params(n_axes=1):
    return pltpu.CompilerParams(dimension_semantics=("arbitrary",) * n_axes, vmem_limit_bytes=VMEM_LIMIT)


def _full(shape):
    return pl.BlockSpec(shape, lambda *_: (0,) * len(shape))


def _rows(tm, cols):
    return pl.BlockSpec((tm, cols), lambda i: (i, 0))


def _cols(rows, tm):
    return pl.BlockSpec((rows, tm), lambda i: (0, i))


def _pick(n, pref):
    if n <= pref:
        return n
    best = LANES
    for t in range(LANES, pref + 1, LANES):
        if n % t == 0:
            best = t
    return best if 2 * best >= pref or n > 2 * pref else n


class _Comm:
    def __init__(self, ins, out_shapes, aliases, sem_counts, start, finish):
        self.ins, self.out_shapes, self.aliases, self.sem_counts = list(ins), list(out_shapes), dict(aliases), sem_counts
        self.start, self.finish = start, finish


def _pcall(body, *, name, grid, in_specs, out_specs, out_shape, args, scratch_shapes=(), comm=None):
    n_in, n_out, n_scr = len(in_specs), len(out_shape), len(scratch_shapes)
    if comm is None:
        res = pl.pallas_call(body, name=name, grid=grid, out_shape=tuple(out_shape), in_specs=list(in_specs),
                             out_specs=tuple(out_specs), scratch_shapes=list(scratch_shapes),
                             compiler_params=_params())(*args)
        return tuple(res), ()
    nci, nco = len(comm.ins), len(comm.out_shapes)
    last = grid[0] - 1

    def wrapped(*refs):
        ins, refs = refs[:n_in], refs[n_in:]
        cins, refs = refs[:nci], refs[nci:]
        outs, refs = refs[:n_out], refs[n_out:]
        couts, refs = refs[:nco], refs[nco:]
        scr, sems = refs[:n_scr], refs[n_scr:]

        @pl.when(pl.program_id(0) == 0)
        def _():
            comm.start(cins, couts, sems)

        body(*ins, *outs, *scr)

        @pl.when(pl.program_id(0) == last)
        def _():
            comm.finish(cins, couts, sems)

    res = pl.pallas_call(
        wrapped, name=name, grid=grid, out_shape=tuple(out_shape) + tuple(comm.out_shapes),
        in_specs=list(in_specs) + [ANY] * nci, out_specs=tuple(out_specs) + (ANY,) * nco,
        input_output_aliases={n_in + a: n_out + b for a, b in comm.aliases.items()},
        scratch_shapes=list(scratch_shapes) + [pltpu.SemaphoreType.DMA((k,)) for k in comm.sem_counts],
        compiler_params=pltpu.CompilerParams(dimension_semantics=("arbitrary",), vmem_limit_bytes=VMEM_LIMIT,
                                             has_side_effects=True),
    )(*args, *comm.ins)
    return tuple(res[:n_out]), tuple(res[n_out:])


def _run_comm(comm, name):
    nci, nco = len(comm.ins), len(comm.out_shapes)

    def body(*refs):
        cins, couts, sems = refs[:nci], refs[nci:nci + nco], refs[nci + nco:]
        comm.start(cins, couts, sems)
        comm.finish(cins, couts, sems)

    return pl.pallas_call(
        body, name=name, out_shape=tuple(comm.out_shapes), in_specs=[ANY] * nci, out_specs=(ANY,) * nco,
        input_output_aliases=comm.aliases, scratch_shapes=[pltpu.SemaphoreType.DMA((k,)) for k in comm.sem_counts],
        compiler_params=pltpu.CompilerParams(has_side_effects=True),
    )(*comm.ins)


def _mm(a, b, name, a_is_transposed, out_split=1):
    (M, R) = a.shape if a_is_transposed else a.shape[::-1]
    _, N = b.shape
    bm, br = _pick(M, 1024 if a_is_transposed else 512), _pick(R, 1024 if a_is_transposed else 512)
    ncol = N // out_split
    bn = _pick(ncol, 1024)
    per = ncol // bn
    last = R // br - 1

    def body(a_ref, b_ref, o_ref, acc_ref):
        @pl.when(pl.program_id(2) == 0)
        def _():
            acc_ref[...] = jnp.zeros_like(acc_ref)

        acc_ref[...] += (_dot if a_is_transposed else _dot_tn)(a_ref[...], b_ref[...])

        @pl.when(pl.program_id(2) == last)
        def _():
            o_ref[...] = acc_ref[...].astype(BF16)

    if out_split == 1:
        out_shape = jax.ShapeDtypeStruct((M, N), BF16)
        out_spec = pl.BlockSpec((bm, bn), lambda i, j, r: (i, j))
    else:
        out_shape = jax.ShapeDtypeStruct((out_split, M, ncol), BF16)
        out_spec = pl.BlockSpec((None, bm, bn), lambda i, j, r: (j // per, i, j % per))
    a_spec = (pl.BlockSpec((bm, br), lambda i, j, r: (i, r)) if a_is_transposed
              else pl.BlockSpec((br, bm), lambda i, j, r: (r, i)))
    return pl.pallas_call(
        body, name=name, out_shape=out_shape, grid=(M // bm, N // bn, R // br),
        in_specs=[a_spec, pl.BlockSpec((br, bn), lambda i, j, r: (r, j))], out_specs=out_spec,
        scratch_shapes=[pltpu.VMEM((bm, bn), F32)],
        compiler_params=pltpu.CompilerParams(dimension_semantics=("parallel", "parallel", "arbitrary"),
                                             vmem_limit_bytes=VMEM_LIMIT),
    )(a, b)


def _mm_tn(a, b, name, out_split=1):
    return _mm(a, b, name, False, out_split)


def _mm_nn(at, b, name, out_split=1):
    return _mm(at, b, name, True, out_split)


def _mm_small(a, b, name):
    def body(a_ref, b_ref, o_ref):
        o_ref[...] = _dot(a_ref[...], b_ref[...]).astype(BF16)

    return pl.pallas_call(body, name=name, out_shape=jax.ShapeDtypeStruct((a.shape[0], b.shape[1]), BF16),
                          compiler_params=pltpu.CompilerParams(vmem_limit_bytes=VMEM_LIMIT))(a, b)


def _mlp_out(x_ref, wup_ref, wdown_ref, g_ref, b_ref, r_ref, h_ref):
    NC, _, FC = wup_ref.shape
    xv = x_ref[...]
    xb = xv.astype(BF16)
    y = jnp.zeros(xv.shape, F32)
    for c in range(NC):
        a = jnp.maximum(_dot(xb, wup_ref[c]), 0.0)
        hb = (a * a).astype(BF16)
        h_ref[:, c * FC:(c + 1) * FC] = hb
        y = y + _dot(hb, wdown_ref[c])
    r = ALPHA * xv + y
    r_ref[...] = r
    xhat, _ = _ln_stats(r)
    return xhat * g_ref[...] + b_ref[...]


def _mlp_fwd_loss(x, wup4, wdown4, g, b, target, name):
    S, D = x.shape
    NC, _, FC = wup4.shape
    tm = min(256, S)

    def body(x_ref, wup_ref, wdown_ref, g_ref, b_ref, t_ref, d_ref, r_ref, h_ref, l_ref):
        @pl.when(pl.program_id(0) == 0)
        def _():
            l_ref[...] = jnp.zeros_like(l_ref)

        err = _mlp_out(x_ref, wup_ref, wdown_ref, g_ref, b_ref, r_ref, h_ref) - t_ref[...]
        d_ref[...] = err * (1.0 / D)
        l_ref[...] += (0.5 / D) * jnp.sum(err * err)

    return pl.pallas_call(
        body, name=name, grid=(S // tm,),
        out_shape=(jax.ShapeDtypeStruct((S, D), F32), jax.ShapeDtypeStruct((S, D), F32),
                   jax.ShapeDtypeStruct((S, NC * FC), BF16), jax.ShapeDtypeStruct((8, LANES), F32)),
        in_specs=[_rows(tm, D), _full(wup4.shape), _full(wdown4.shape), _full((1, D)), _full((1, D)), _rows(tm, D)],
        out_specs=(_rows(tm, D), _rows(tm, D), _rows(tm, NC * FC), _full((8, LANES))),
        compiler_params=_params(),
    )(x, wup4, wdown4, g, b, target)


def _mlp_fwd(x, wup4, wdown4, g, b, name, comm=None):
    S, D = x.shape
    NC, _, FC = wup4.shape
    tm = min(256, S)

    def body(x_ref, wup_ref, wdown_ref, g_ref, b_ref, xo_ref, xb_ref, xbt_ref, r_ref, h_ref):
        xo = _mlp_out(x_ref, wup_ref, wdown_ref, g_ref, b_ref, r_ref, h_ref)
        xo_ref[...] = xo
        xb_ref[...] = xo.astype(BF16)
        xbt_ref[...] = xo.T.astype(BF16)

    return _pcall(
        body, name=name, grid=(S // tm,), comm=comm,
        out_shape=(jax.ShapeDtypeStruct((S, D), F32), jax.ShapeDtypeStruct((S, D), BF16),
                   jax.ShapeDtypeStruct((D, S), BF16), jax.ShapeDtypeStruct((S, D), F32),
                   jax.ShapeDtypeStruct((S, NC * FC), BF16)),
        in_specs=[_rows(tm, D), _full(wup4.shape), _full(wdown4.shape), _full((1, D)), _full((1, D))],
        out_specs=(_rows(tm, D), _rows(tm, D), _cols(D, tm), _rows(tm, D), _rows(tm, NC * FC)),
        args=(x, wup4, wdown4, g, b))


def _mlp_bwd(dxo, r, h, wup4, wdown4, g, name, comm=None):
    S, D = r.shape
    NC, _, FC = wup4.shape
    tm = min(256, S)

    def body(dxo_ref, r_ref, h_ref, wup_ref, wdown_ref, g_ref, dx_ref, drbt_ref, dpre_ref, dgb_ref):
        @pl.when(pl.program_id(0) == 0)
        def _():
            dgb_ref[...] = jnp.zeros_like(dgb_ref)

        xhat, rstd = _ln_stats(r_ref[...])
        dy = dxo_ref[...]
        dgb_ref[0:1, :] += jnp.sum(dy * xhat, axis=0, keepdims=True)
        dgb_ref[1:2, :] += jnp.sum(dy, axis=0, keepdims=True)
        dr = _ln_bwd(dy, xhat, rstd, g_ref[...])
        drb = dr.astype(BF16)
        drbt_ref[...] = dr.T.astype(BF16)
        dx = ALPHA * dr
        for c in range(NC):
            dh = _dot_nt(drb, wdown_ref[c])
            a = jnp.sqrt(h_ref[:, c * FC:(c + 1) * FC].astype(F32))
            dpre = (2.0 * a * dh).astype(BF16)
            dpre_ref[:, c * FC:(c + 1) * FC] = dpre
            dx = dx + _dot_nt(dpre, wup_ref[c])
        dx_ref[...] = dx

    return _pcall(
        body, name=name, grid=(S // tm,), comm=comm,
        out_shape=(jax.ShapeDtypeStruct((S, D), F32), jax.ShapeDtypeStruct((D, S), BF16),
                   jax.ShapeDtypeStruct((S, NC * FC), BF16), jax.ShapeDtypeStruct((2, D), F32)),
        in_specs=[_rows(tm, D), _rows(tm, D), _rows(tm, NC * FC), _full(wup4.shape), _full(wdown4.shape),
                  _full((1, D))],
        out_specs=(_rows(tm, D), _cols(D, tm), _rows(tm, NC * FC), _full((2, D))),
        args=(dxo, r, h, wup4, wdown4, g))


def _tail_fwd_store(xv, cat_ref, wo_ref, g_ref, b_ref, xo_ref, xbt_ref, r_ref, catt_ref):
    cat = cat_ref[...]
    y = _dot(cat, wo_ref[...])
    r = ALPHA * xv + y
    xhat, _ = _ln_stats(r)
    xo = xhat * g_ref[...] + b_ref[...]
    xo_ref[...] = xo
    xbt_ref[...] = xo.T.astype(BF16)
    catt_ref[...] = cat.astype(F32).T.astype(BF16)
    r_ref[...] = r


def _tail_bwd_head(dxo_ref, r_ref, g_ref, wo_ref, dgb_ref, drb_ref):
    xhat, rstd = _ln_stats(r_ref[...])
    dy = dxo_ref[...]
    dgb_ref[0:1, :] += jnp.sum(dy * xhat, axis=0, keepdims=True)
    dgb_ref[1:2, :] += jnp.sum(dy, axis=0, keepdims=True)
    dr = _ln_bwd(dy, xhat, rstd, g_ref[...])
    drb = dr.astype(BF16)
    drb_ref[...] = drb
    return dr, _dot_nt(drb, wo_ref[...])


def _causal(shape):
    return lax.broadcasted_iota(jnp.int32, shape, 1) <= lax.broadcasted_iota(jnp.int32, shape, 0)


def _sgu_mixed(ws_ref, bst_ref, gp, vpair):
    tril = _causal((CHUNK, CHUNK))
    w0 = jnp.where(tril, ws_ref[2 * gp], 0.0).astype(BF16)
    w1 = jnp.where(tril, ws_ref[2 * gp + 1], 0.0).astype(BF16)
    m0 = _dot(w0, vpair) + bst_ref[:, 2 * gp:2 * gp + 1]
    m1 = _dot(w1, vpair) + bst_ref[:, 2 * gp + 1:2 * gp + 2]
    return jnp.where(_first_head(), m0, m1), w0, w1


def _mixer_a_fwd(x, wa, sg, sb, ws, bst, mkv, wo, g, b, name, comm=None):
    S, D = x.shape
    DQ = mkv.shape[1] // 2
    DM = D - DQ
    NP = DM // PAIR
    tm = min(512, S)

    def body(x_ref, wa_ref, sg_ref, sb_ref, ws_ref, bst_ref, mkv_ref, wo_ref, g_ref, b_ref,
             xo_ref, xbt_ref, r_ref, catt_ref, xint_ref, z_s, zv_s, cat_ref):
        xv = x_ref[...]
        xint_ref[...] = xv.T.astype(BF16)
        z_s[...] = _dot(xv.astype(BF16), wa_ref[...])
        vhat, _ = _ln_stats(_gelu(z_s[:, DM:2 * DM]))
        zv_s[...] = (vhat * sg_ref[...] + sb_ref[...]).astype(BF16)
        for c in range(tm // CHUNK):
            rs = slice(c * CHUNK, (c + 1) * CHUNK)
            for gp in range(NP):
                ls = slice(gp * PAIR, (gp + 1) * PAIR)
                mixed, _, _ = _sgu_mixed(ws_ref, bst_ref, gp, zv_s[rs, ls])
                cat_ref[rs, ls] = (_gelu(z_s[rs, ls]) * mixed).astype(BF16)
        mo = _mem_attn_fwd(z_s[:, 2 * DM:].astype(BF16), mkv_ref, DQ)
        for mp in range(DQ // PAIR):
            cat_ref[:, DM + mp * PAIR:DM + (mp + 1) * PAIR] = mo[mp].astype(BF16)
        _tail_fwd_store(xv, cat_ref, wo_ref, g_ref, b_ref, xo_ref, xbt_ref, r_ref, catt_ref)

    return _pcall(
        body, name=name, grid=(S // tm,), comm=comm,
        out_shape=(jax.ShapeDtypeStruct((S, D), F32), jax.ShapeDtypeStruct((D, S), BF16),
                   jax.ShapeDtypeStruct((S, D), F32), jax.ShapeDtypeStruct((D, S), BF16),
                   jax.ShapeDtypeStruct((D, S), BF16)),
        in_specs=[_rows(tm, D), _full(wa.shape), _full(sg.shape), _full(sb.shape), _full(ws.shape), _full(bst.shape),
                  _full(mkv.shape), _full(wo.shape), _full((1, D)), _full((1, D))],
        out_specs=(_rows(tm, D), _cols(D, tm), _rows(tm, D), _cols(D, tm), _cols(D, tm)),
        scratch_shapes=[pltpu.VMEM((tm, 2 * DM + DQ), F32), pltpu.VMEM((tm, DM), BF16), pltpu.VMEM((tm, D), BF16)],
        args=(x, wa, sg, sb, ws, bst, mkv, wo, g, b))


def _mixer_a_bwd(x, dxo, r, wa, sg, sb, ws, bst, mkv, wo, g, name, comm=None):
    S, D = x.shape
    DQ = mkv.shape[1] // 2
    DM = D - DQ
    NP = DM // PAIR
    NG = DM // HEAD_DIM
    tm = min(256, S)

    def body(x_ref, dxo_ref, r_ref, wa_ref, sg_ref, sb_ref, ws_ref, bst_ref, mkv_ref, wo_ref, g_ref,
             dx_ref, dz_ref, drb_ref, dgb_ref, dsgb_ref, dws_ref, dbst_ref, dmkv_ref, z_s, zv_s, dzv_s):
        @pl.when(pl.program_id(0) == 0)
        def _():
            dgb_ref[...] = jnp.zeros_like(dgb_ref)
            dsgb_ref[...] = jnp.zeros_like(dsgb_ref)
            dws_ref[...] = jnp.zeros_like(dws_ref)
            dbst_ref[...] = jnp.zeros_like(dbst_ref)
            dmkv_ref[...] = jnp.zeros_like(dmkv_ref)

        dr, dcat = _tail_bwd_head(dxo_ref, r_ref, g_ref, wo_ref, dgb_ref, drb_ref)
        z_s[...] = _dot(x_ref[...].astype(BF16), wa_ref[...])
        vhat, vrstd = _ln_stats(_gelu(z_s[:, DM:2 * DM]))
        zv_s[...] = (vhat * sg_ref[...] + sb_ref[...]).astype(BF16)
        tril = _causal((CHUNK, CHUNK))
        lane = lax.broadcasted_iota(jnp.int32, (1, LANES), 1)
        for c in range(tm // CHUNK):
            rs = slice(c * CHUNK, (c + 1) * CHUNK)
            for gp in range(NP):
                ls = slice(gp * PAIR, (gp + 1) * PAIR)
                vpair = zv_s[rs, ls]
                mixed, w0, w1 = _sgu_mixed(ws_ref, bst_ref, gp, vpair)
                u_pre = z_s[rs, ls]
                dmix = dcat[rs, ls]
                dz_ref[rs, ls] = (dmix * mixed * _gelu_grad(u_pre)).astype(BF16)
                dmixed = dmix * _gelu(u_pre)
                first = _first_head()
                d0 = jnp.where(first, dmixed, 0.0)
                d1 = jnp.where(first, 0.0, dmixed)
                d0b = d0.astype(BF16)
                d1b = d1.astype(BF16)
                dzv_s[rs, ls] = _dot_tn(w0, d0b) + _dot_tn(w1, d1b)
                dws_ref[2 * gp] += jnp.where(tril, _dot_nt(d0b, vpair), 0.0)
                dws_ref[2 * gp + 1] += jnp.where(tril, _dot_nt(d1b, vpair), 0.0)
                dbst_ref[...] += (jnp.where(lane == 2 * gp, jnp.sum(d0, axis=-1, keepdims=True), 0.0)
                                  + jnp.where(lane == 2 * gp + 1, jnp.sum(d1, axis=-1, keepdims=True), 0.0))
        dzv = dzv_s[...]
        dsgb_ref[0:1, :] += jnp.sum(dzv * vhat, axis=0, keepdims=True)
        dsgb_ref[1:2, :] += jnp.sum(dzv, axis=0, keepdims=True)
        dgv = _ln_bwd(dzv, vhat, vrstd, sg_ref[...])
        dz_ref[:, DM:2 * DM] = (dgv * _gelu_grad(z_s[:, DM:2 * DM])).astype(BF16)
        dmo = [dcat[:, DM + mp * PAIR:DM + (mp + 1) * PAIR] for mp in range(DQ // PAIR)]
        dqm = _mem_attn_bwd(z_s[:, 2 * DM:].astype(BF16), mkv_ref, dmo, dmkv_ref, DQ)
        for mp in range(DQ // PAIR):
            dz_ref[:, 2 * DM + mp * PAIR:2 * DM + (mp + 1) * PAIR] = dqm[mp].astype(BF16)
        dx_ref[...] = ALPHA * dr + _dot_nt(dz_ref[...], wa_ref[...])

    ZW = 2 * DM + DQ
    M = mkv.shape[0]
    return _pcall(
        body, name=name, grid=(S // tm,), comm=comm,
        out_shape=(jax.ShapeDtypeStruct((S, D), F32), jax.ShapeDtypeStruct((S, ZW), BF16),
                   jax.ShapeDtypeStruct((S, D), BF16), jax.ShapeDtypeStruct((2, D), F32),
                   jax.ShapeDtypeStruct((2, DM), F32), jax.ShapeDtypeStruct((NG, CHUNK, CHUNK), F32),
                   jax.ShapeDtypeStruct((CHUNK, LANES), F32), jax.ShapeDtypeStruct((M, 2 * DQ), F32)),
        in_specs=[_rows(tm, D), _rows(tm, D), _rows(tm, D), _full(wa.shape), _full(sg.shape), _full(sb.shape),
                  _full(ws.shape), _full(bst.shape), _full(mkv.shape), _full(wo.shape), _full((1, D))],
        out_specs=(_rows(tm, D), _rows(tm, ZW), _rows(tm, D), _full((2, D)), _full((2, DM)),
                   _full((NG, CHUNK, CHUNK)), _full((CHUNK, LANES)), _full((M, 2 * DQ))),
        scratch_shapes=[pltpu.VMEM((tm, ZW), F32), pltpu.VMEM((tm, DM), BF16), pltpu.VMEM((tm, DM), F32)],
        args=(x, dxo, r, wa, sg, sb, ws, bst, mkv, wo, g))


def _proj_b_fwd(xb, wq, wkv, dm, name):
    S, D = xb.shape
    DQ = D - dm
    tm = min(512, S)

    def body(x_ref, wq_ref, wkv_ref, q_ref, qm_ref, k_ref, v_ref, fl_ref):
        xv = x_ref[...]
        z = _dot(xv, wq_ref[...])
        q_ref[...] = z[:, :dm].astype(BF16)
        qm_ref[...] = z[:, dm:].astype(BF16)
        kvf = _dot(xv, wkv_ref[...])
        k_ref[...] = kvf[:, :dm].astype(BF16)
        v_ref[...] = kvf[:, dm:2 * dm].astype(BF16)
        fl_ref[...] = kvf[:, 2 * dm:]

    return pl.pallas_call(
        body, name=name, grid=(S // tm,),
        out_shape=(jax.ShapeDtypeStruct((S, dm), BF16), jax.ShapeDtypeStruct((S, DQ), BF16),
                   jax.ShapeDtypeStruct((S, dm), BF16), jax.ShapeDtypeStruct((S, dm), BF16),
                   jax.ShapeDtypeStruct((S, LANES), F32)),
        in_specs=[_rows(tm, D), _full(wq.shape), _full(wkv.shape)],
        out_specs=(_rows(tm, dm), _rows(tm, DQ), _rows(tm, dm), _rows(tm, dm), _rows(tm, LANES)),
        compiler_params=_params(),
    )(xb, wq, wkv)


def _proj_b_bwd(dr, dqz, dkvf, wq, wkv, name):
    S, D = dr.shape
    tm = min(512, S)

    def body(dr_ref, dqz_ref, dkvf_ref, wq_ref, wkv_ref, dx_ref):
        dx_ref[...] = (ALPHA * dr_ref[...] + _dot_nt(dqz_ref[...], wq_ref[...])
                       + _dot_nt(dkvf_ref[...], wkv_ref[...]))

    return pl.pallas_call(
        body, name=name, grid=(S // tm,), out_shape=jax.ShapeDtypeStruct((S, D), F32),
        in_specs=[_rows(tm, D), _rows(tm, dqz.shape[1]), _rows(tm, dkvf.shape[1]), _full(wq.shape), _full(wkv.shape)],
        out_specs=_rows(tm, D), compiler_params=_params(),
    )(dr, dqz, dkvf, wq, wkv)


def _split3(v):
    hi = v.astype(BF16)
    r1 = v - hi.astype(F32)
    mid = r1.astype(BF16)
    lo = (r1 - mid.astype(F32)).astype(BF16)
    return hi, mid, lo


def _tri_sum(v, tri_b):
    hi, mid, lo = _split3(v)
    return _dot(hi, tri_b) + _dot(mid, tri_b) + _dot(lo, tri_b)


def _log_sigmoid(x):
    return jnp.minimum(x, 0.0) - jnp.log(1.0 + jnp.exp(-jnp.abs(x)))


def _forget_cumsum(flt, bf, name):
    H, S = flt.shape
    nchunk = S // LANES

    def body(fl_ref, bf_ref, p_ref):
        upper = (lax.broadcasted_iota(jnp.int32, (LANES, LANES), 0)
                 <= lax.broadcasted_iota(jnp.int32, (LANES, LANES), 1)).astype(BF16)

        def step(n, carry):
            s0 = pl.multiple_of(n * LANES, LANES)
            lf = _log_sigmoid(fl_ref[:, pl.ds(s0, LANES)] + bf_ref[...])
            cs = _tri_sum(lf, upper) + carry
            for i, piece in enumerate(_split3(-cs)):
                p_ref[i, :, pl.ds(s0, LANES)] = piece.astype(F32)
            return cs[:, LANES - 1:LANES]

        lax.fori_loop(0, nchunk, step, jnp.zeros((H, 1), F32))

    return pl.pallas_call(body, name=name, out_shape=jax.ShapeDtypeStruct((3, H, S), F32),
                          compiler_params=pltpu.CompilerParams(vmem_limit_bytes=VMEM_LIMIT))(flt, bf)


def _forget_cumsum_bwd(dct, flt, bf, name):
    H, S = flt.shape
    nchunk = S // LANES

    def body(dc_ref, fl_ref, bf_ref, dfl_ref, dbf_ref):
        lower = (lax.broadcasted_iota(jnp.int32, (LANES, LANES), 0)
                 >= lax.broadcasted_iota(jnp.int32, (LANES, LANES), 1)).astype(BF16)

        def step(n, carry):
            tail, tot = carry
            s0 = pl.multiple_of((nchunk - 1 - n) * LANES, LANES)
            suffix = _tri_sum(dc_ref[:, pl.ds(s0, LANES)], lower) + tail
            xv = fl_ref[:, pl.ds(s0, LANES)] + bf_ref[...]
            dfl = suffix * (1.0 / (1.0 + jnp.exp(xv)))
            dfl_ref[:, pl.ds(s0, LANES)] = dfl
            return suffix[:, 0:1], tot + jnp.sum(dfl, axis=-1, keepdims=True)

        _, tot = lax.fori_loop(0, nchunk, step, (jnp.zeros((H, 1), F32), jnp.zeros((H, 1), F32)))
        dbf_ref[...] = jnp.broadcast_to(tot, (H, LANES))

    return pl.pallas_call(body, name=name,
                          out_shape=(jax.ShapeDtypeStruct((H, S), F32), jax.ShapeDtypeStruct((H, LANES), F32)),
                          compiler_params=pltpu.CompilerParams(vmem_limit_bytes=VMEM_LIMIT))(dct, flt, bf)


N_BIAS = 3


def _bias_ones():
    lane = lax.broadcasted_iota(jnp.int32, (1, PAIR), 1)
    return ((lane & (HEAD_DIM - 1)) < N_BIAS).astype(BF16)


def _fox_fwd(q, k, v, ca, name, comm=None):
    S, DM = q.shape
    NP = DM // PAIR
    T = min(FOX_BLOCK, S)
    nq = S // T

    def body(q_ref, k_ref, v_ref, ca_ref, o_ref, lse_ref):
        tril = _causal((T, T))
        heads = (_lane_mask(0), _lane_mask(1))
        ones3 = _bias_ones()

        def step(carry, qaug, kaug, vaug, masked):
            m, acc = carry
            s = _dot_nt(qaug, kaug)
            if masked:
                s = jnp.where(tril, s, NEG)
            m_new = jnp.maximum(m, jnp.max(s, axis=-1, keepdims=True))
            p = jnp.exp(s - m_new).astype(BF16)
            return m_new, jnp.exp(m - m_new) * acc + _dot(p, vaug)

        def kv_block(kj):
            s0 = pl.multiple_of(kj * T, T)
            kb, vb, cab = k_ref[pl.ds(s0, T), :], v_ref[pl.ds(s0, T), :], ca_ref[pl.ds(s0, T), :]
            return ([jnp.where(heads[e], kb, cab) for e in (0, 1)],
                    [jnp.where(heads[e], vb, jnp.ones_like(vb)) for e in (0, 1)])

        def q_pair(a, _):
            t0s = [pl.multiple_of((2 * a + r) * T, T) for r in (0, 1)]
            qaug = {}
            for r in (0, 1):
                qb = q_ref[pl.ds(t0s[r], T), :] * QK_SCALE
                for e in (0, 1):
                    qaug[r, e] = jnp.where(heads[e], qb, ones3)
            chains = [(r, e) for r in (0, 1) for e in (0, 1)]

            def full(first_block, n_blocks, carries):
                carries = list(carries)
                for kk in range(n_blocks):
                    kaug, vaug = kv_block(first_block + kk)
                    scores = [_dot_nt(qaug[r, e], kaug[e]) for r, e in chains]
                    for n, (r, e) in enumerate(chains):
                        m, acc = carries[n]
                        m_new = jnp.maximum(m, jnp.max(scores[n], axis=-1, keepdims=True))
                        p = jnp.exp(scores[n] - m_new).astype(BF16)
                        carries[n] = (m_new, jnp.exp(m - m_new) * acc + _dot(p, vaug[e]))
                return tuple(carries)

            init = (jnp.full((T, 1), NEG, F32), jnp.zeros((T, PAIR), F32))
            carries = lax.fori_loop(0, a // 2, lambda i, c: full(4 * i, 4, c), (init,) * 4)
            carries = list(lax.fori_loop(0, a % 2, lambda i, c: full(2 * a - 2, 2, c), carries))
            kaug, vaug = kv_block(2 * a)
            for i, (r, e) in enumerate(chains):
                carries[i] = step(carries[i], qaug[r, e], kaug[e], vaug[e], r == 0)
            kaug, vaug = kv_block(2 * a + 1)
            for i, (r, e) in enumerate(chains):
                if r == 1:
                    carries[i] = step(carries[i], qaug[r, e], kaug[e], vaug[e], True)
            first = _first_head()
            for r in (0, 1):
                (m0, acc0), (m1, acc1) = carries[2 * r], carries[2 * r + 1]
                l0, l1 = acc0[:, HEAD_DIM:HEAD_DIM + 1], acc1[:, 0:1]
                o_ref[pl.ds(t0s[r], T), :] = jnp.where(first, acc0 / l0, acc1 / l1).astype(BF16)
                lse_t = jnp.where(first, m0 + jnp.log(l0), m1 + jnp.log(l1)).T
                lse_ref[0:1, pl.ds(t0s[r], T)] = lse_t[0:1, :]
                lse_ref[1:2, pl.ds(t0s[r], T)] = lse_t[HEAD_DIM:HEAD_DIM + 1, :]
            return 0

        lax.fori_loop(0, nq // 2, q_pair, 0)

    col = pl.BlockSpec((S, PAIR), lambda hp: (0, hp))
    return _pcall(
        body, name=name, grid=(NP,), comm=comm,
        out_shape=(jax.ShapeDtypeStruct((S, DM), BF16), jax.ShapeDtypeStruct((NP, 2, S), F32)),
        in_specs=[col, col, col, col], out_specs=(col, pl.BlockSpec((None, 2, S), lambda hp: (hp, 0, 0))),
        args=(q, k, v, ca))


def _fox_bwd(q, k, v, o, do, ca, lse, name, comm=None):
    S, DM = q.shape
    NP = DM // PAIR
    T = min(FOX_BLOCK, S)
    nq = S // T

    def body(q_ref, k_ref, v_ref, o_ref, do_ref, ca_ref, lse_ref, dq_ref, dk_ref, dv_ref, dc_ref,
             dq_acc, dcol_acc, dk_acc, dv_acc, dd_s):
        keep = lax.broadcasted_iota(jnp.int32, (T, T), 0) <= lax.broadcasted_iota(jnp.int32, (T, T), 1)
        lane = lax.broadcasted_iota(jnp.int32, (1, LANES), 1)
        heads = (_lane_mask(0), _lane_mask(1))
        ones3 = _bias_ones()
        dq_acc[...] = jnp.zeros_like(dq_acc)
        dcol_acc[...] = jnp.zeros_like(dcol_acc)
        dc_ref[...] = jnp.zeros_like(dc_ref)

        def row_dots(qi, _):
            t0 = pl.multiple_of(qi * T, T)
            prod = do_ref[pl.ds(t0, T), :].astype(F32) * o_ref[pl.ds(t0, T), :].astype(F32)
            cols = [jnp.sum(jnp.where(heads[e], prod, 0.0), axis=-1, keepdims=True) for e in (0, 1)]
            tile = jnp.where(lane == 0, cols[0], jnp.where(lane == 1, cols[1], 0.0))
            dd_s[:, pl.ds(t0, T)] = tile.T[0:8, :]
            return 0

        lax.fori_loop(0, nq, row_dots, 0)

        def kv_pair(b, _):
            dk_acc[...] = jnp.zeros_like(dk_acc)
            dv_acc[...] = jnp.zeros_like(dv_acc)
            s0s = [pl.multiple_of((2 * b + jj) * T, T) for jj in (0, 1)]
            kaug, ks, vbs = {}, {}, []
            for jj in (0, 1):
                kb = k_ref[pl.ds(s0s[jj], T), :]
                cab = ca_ref[pl.ds(s0s[jj], T), :]
                vbs.append(v_ref[pl.ds(s0s[jj], T), :])
                for e in (0, 1):
                    kaug[jj, e] = jnp.where(heads[e], kb, cab)
                    ks[jj, e] = jnp.where(heads[e], kb, jnp.zeros_like(kb)) * QK_SCALE

            def q_step(qi, blocks):
                t0 = pl.multiple_of(qi * T, T)
                qs = q_ref[pl.ds(t0, T), :] * QK_SCALE
                dob = do_ref[pl.ds(t0, T), :]
                doh = [jnp.where(heads[e], dob, jnp.zeros_like(dob)) for e in (0, 1)]
                qh = [jnp.where(heads[e], qs, jnp.zeros_like(qs)) for e in (0, 1)]
                chains = [(e, jj, masked) for e in (0, 1) for jj, masked in blocks]
                scores = [_dot_nt(kaug[jj, e], jnp.where(heads[e], qs, ones3)) for e, jj, _ in chains]
                dps = [_dot_nt(vbs[jj], doh[e]) for e, jj, _ in chains]
                dq = jnp.zeros((T, PAIR), F32)
                drow = [jnp.zeros((1, T), F32), jnp.zeros((1, T), F32)]
                for n, (e, jj, masked) in enumerate(chains):
                    st = jnp.where(keep, scores[n], NEG) if masked else scores[n]
                    pt = jnp.exp(st - lse_ref[e:e + 1, pl.ds(t0, T)])
                    dv_acc[jj] += _dot(pt.astype(BF16), doh[e])
                    dst = pt * (dps[n] - dd_s[e:e + 1, pl.ds(t0, T)])
                    dsb = dst.astype(BF16)
                    dk_acc[jj] += _dot(dsb, qh[e])
                    dq = dq + _dot_tn(dsb, ks[jj, e])
                    dcol_acc[pl.ds(s0s[jj], T), :] -= jnp.where(lane == e, jnp.sum(dst, axis=-1, keepdims=True), 0.0)
                    drow[e] = drow[e] + jnp.sum(dst, axis=0, keepdims=True)
                for e in (0, 1):
                    dc_ref[e:e + 1, pl.ds(t0, T)] += drow[e]
                dq_acc[pl.ds(t0, T), :] += dq

            q_step(2 * b, [(0, True)])
            q_step(2 * b + 1, [(0, False), (1, True)])

            def rest(i, _):
                q_step(2 * b + 2 + 2 * i, [(0, False), (1, False)])
                q_step(2 * b + 3 + 2 * i, [(0, False), (1, False)])
                return 0

            lax.fori_loop(0, (nq - 2) // 2 - b, rest, 0)
            for jj in (0, 1):
                dk_ref[pl.ds(s0s[jj], T), :] = dk_acc[jj].astype(BF16)
                dv_ref[pl.ds(s0s[jj], T), :] = dv_acc[jj].astype(BF16)
            return 0

        lax.fori_loop(0, nq // 2, kv_pair, 0)
        dq_ref[...] = dq_acc[...].astype(BF16)

        def add_cols(kj, _):
            s0 = pl.multiple_of(kj * T, T)
            dc_ref[:, pl.ds(s0, T)] += dcol_acc[pl.ds(s0, T), :].T[0:2, :]
            return 0

        lax.fori_loop(0, nq, add_cols, 0)

    col = pl.BlockSpec((S, PAIR), lambda hp: (0, hp))
    row2 = pl.BlockSpec((None, 2, S), lambda hp: (hp, 0, 0))
    return _pcall(
        body, name=name, grid=(NP,), comm=comm,
        out_shape=(jax.ShapeDtypeStruct((S, DM), BF16), jax.ShapeDtypeStruct((S, DM), BF16),
                   jax.ShapeDtypeStruct((S, DM), BF16), jax.ShapeDtypeStruct((NP, 2, S), F32)),
        in_specs=[col, col, col, col, col, col, row2], out_specs=(col, col, col, row2),
        scratch_shapes=[pltpu.VMEM((S, PAIR), F32), pltpu.VMEM((S, LANES), F32), pltpu.VMEM((2, T, PAIR), F32),
                        pltpu.VMEM((2, T, PAIR), F32), pltpu.VMEM((8, S), F32)],
        args=(q, k, v, o, do, ca, lse))


def _mixer_b_fwd(x, o, qm, mkv, wo, g, b, name):
    S, D = x.shape
    DQ = qm.shape[1]
    DM = D - DQ
    tm = min(512, S)

    def body(x_ref, o_ref, qm_ref, mkv_ref, wo_ref, g_ref, b_ref, xo_ref, xbt_ref, r_ref, catt_ref, cat_ref):
        cat_ref[:, :DM] = o_ref[...]
        mo = _mem_attn_fwd(qm_ref[...], mkv_ref, DQ)
        for mp in range(DQ // PAIR):
            cat_ref[:, DM + mp * PAIR:DM + (mp + 1) * PAIR] = mo[mp].astype(BF16)
        _tail_fwd_store(x_ref[...], cat_ref, wo_ref, g_ref, b_ref, xo_ref, xbt_ref, r_ref, catt_ref)

    return pl.pallas_call(
        body, name=name, grid=(S // tm,),
        out_shape=(jax.ShapeDtypeStruct((S, D), F32), jax.ShapeDtypeStruct((D, S), BF16),
                   jax.ShapeDtypeStruct((S, D), F32), jax.ShapeDtypeStruct((D, S), BF16)),
        in_specs=[_rows(tm, D), _rows(tm, DM), _rows(tm, DQ), _full(mkv.shape), _full(wo.shape), _full((1, D)),
                  _full((1, D))],
        out_specs=(_rows(tm, D), _cols(D, tm), _rows(tm, D), _cols(D, tm)),
        scratch_shapes=[pltpu.VMEM((tm, D), BF16)],
        compiler_params=_params(),
    )(x, o, qm, mkv, wo, g, b)


def _mixer_b_bwd(dxo, r, qm, mkv, wo, g, name, comm=None):
    S, D = r.shape
    DQ = qm.shape[1]
    DM = D - DQ
    M = mkv.shape[0]
    tm = min(512, S)

    def body(dxo_ref, r_ref, qm_ref, mkv_ref, wo_ref, g_ref, dr_ref, drb_ref, dcat_ref, dgb_ref, dmkv_ref):
        @pl.when(pl.program_id(0) == 0)
        def _():
            dgb_ref[...] = jnp.zeros_like(dgb_ref)
            dmkv_ref[...] = jnp.zeros_like(dmkv_ref)

        dr, dcat = _tail_bwd_head(dxo_ref, r_ref, g_ref, wo_ref, dgb_ref, drb_ref)
        dr_ref[...] = dr
        dcat_ref[:, :DM] = dcat[:, :DM].astype(BF16)
        dmo = [dcat[:, DM + mp * PAIR:DM + (mp + 1) * PAIR] for mp in range(DQ // PAIR)]
        dqm = _mem_attn_bwd(qm_ref[...], mkv_ref, dmo, dmkv_ref, DQ)
        for mp in range(DQ // PAIR):
            dcat_ref[:, DM + mp * PAIR:DM + (mp + 1) * PAIR] = dqm[mp].astype(BF16)

    return _pcall(
        body, name=name, grid=(S // tm,), comm=comm,
        out_shape=(jax.ShapeDtypeStruct((S, D), F32), jax.ShapeDtypeStruct((S, D), BF16),
                   jax.ShapeDtypeStruct((S, D), BF16), jax.ShapeDtypeStruct((2, D), F32),
                   jax.ShapeDtypeStruct((M, 2 * DQ), F32)),
        in_specs=[_rows(tm, D), _rows(tm, D), _rows(tm, DQ), _full(mkv.shape), _full(wo.shape), _full((1, D))],
        out_specs=(_rows(tm, D), _rows(tm, D), _rows(tm, D), _full((2, D)), _full((M, 2 * DQ))),
        args=(dxo, r, qm, mkv, wo, g))


class _NoExchange:
    def plan(self, stage, w, grads):
        return None

    def done(self, stage, results, w):
        pass


def _local_step(x, mem, target, w, hooks):
    S, D = x.shape
    DQ = w["mkv0"].shape[1] // 2
    DM = D - DQ
    NG = DM // HEAD_DIM
    NP = DM // PAIR
    ln_g, ln_b = w["ln_g"], w["ln_b"]
    row = lambda a, i: a[i:i + 1]
    memb = mem.astype(BF16)
    bst =jnp.pad(w["b_s"].T, ((0, 0), (0, LANES - NG)))
    grads = {}

    def staged(stage, fn, *args):
        out, got = fn(*args, stage, comm=hooks.plan(stage, w, grads))
        hooks.done(stage, got, w)
        return out

    mkv0 = _mm_small(memb, w["mkv0"], "mem_kv_0")
    x1, x1t, r1, cat0t, x0t = staged("mixer_a_fwd", _mixer_a_fwd, x, w["wa"], w["sg"], w["sb"], w["ws"], bst, mkv0,
                                w["wo0"], row(ln_g, 0), row(ln_b, 0))
    x2, x2b, x2t, r2, h0 = staged("mlp_fwd_0", _mlp_fwd, x1, w["wup0"], w["wdown0"], row(ln_g, 1), row(ln_b, 1))
    q, qm, k, v, fl = _proj_b_fwd(x2b, w["wq"], w["wkv"], DM, "proj_b_fwd")
    flt = fl[:, :16].T
    bfc = jnp.pad(w["bf"], (0, 16 - NG)).reshape(16, 1)
    pieces = _forget_cumsum(flt, bfc, "forget_cumsum")
    pieces = jnp.pad(pieces[:, :NG].transpose(2, 1, 0), ((0, 0), (0, 0), (0, HEAD_DIM - N_BIAS)))
    ca = pieces.reshape(S, NP, 2, HEAD_DIM)[:, :, ::-1].reshape(S, DM).astype(BF16)
    o, lse = staged("fox_fwd", _fox_fwd, q, k, v, ca)
    mkv1 = _mm_small(memb, w["mkv1"], "mem_kv_1")
    x3, x3t, r3, cat1t = _mixer_b_fwd(x2, o, qm, mkv1, w["wo1"], row(ln_g, 2), row(ln_b, 2), "mixer_b_fwd")
    dx4, r4, h1, loss_part = _mlp_fwd_loss(x3, w["wup1"], w["wdown1"], row(ln_g, 3), row(ln_b, 3), target,
                                           "mlp_fwd_1")

    nc = w["wup0"].shape[0]
    (dx3, dr4t, dpre1, dgb11), _ = _mlp_bwd(dx4, r4, h1, w["wup1"], w["wdown1"], row(ln_g, 3), "mlp_bwd_1")
    grads["wdown1"] = _mm_nn(dr4t, h1, "dw_down_1", out_split=nc)
    grads["wup1"] = _mm_nn(x3t, dpre1, "dw_up_1", out_split=nc)
    dr3, dr3b, dcat1, dgb10, dmkv1 = staged("mixer_b_bwd", _mixer_b_bwd, dx3, r3, qm, mkv1, w["wo1"], row(ln_g, 2))
    do = dcat1[:, :DM]
    dq, dk, dv, dct3 = staged("fox_bwd", _fox_bwd, q, k, v, o, do, ca, lse)
    dct = jnp.pad(dct3.reshape(NG, S), ((0, 16 - NG), (0, 0)))
    dflt, dbf = _forget_cumsum_bwd(dct, flt, bfc, "forget_cumsum_bwd")
    dfl = jnp.pad(dflt.T, ((0, 0), (0, LANES - 16))).astype(BF16)
    dqz = jnp.concatenate([dq, dcat1[:, DM:]], axis=1)
    dkvf = jnp.concatenate([dk, dv, dfl], axis=1)
    dx2 = _proj_b_bwd(dr3, dqz, dkvf, w["wq"], w["wkv"], "proj_b_bwd")
    grads["wo1"] = _mm_nn(cat1t, dr3b, "dw_o_1")
    grads["wq"] = _mm_nn(x2t, dqz, "dw_q")
    grads["wkv"] = _mm_nn(x2t, dkvf, "dw_kv")
    grads["mkv1"] = _mm_tn(memb, dmkv1.astype(BF16), "dw_mkv_1")
    dx1, dr2t, dpre0, dgb01 = staged("mlp_bwd_0", _mlp_bwd, dx2, r2, h0, w["wup0"], w["wdown0"], row(ln_g, 1))
    grads["wdown0"] = _mm_nn(dr2t, h0, "dw_down_0", out_split=nc)
    grads["wup0"] = _mm_nn(x1t, dpre0, "dw_up_0", out_split=nc)
    dx0, dz, dr1b, dgb00, dsgb, dws, dbst, dmkv0 = staged(
        "mixer_a_bwd", _mixer_a_bwd, x, dx1, r1, w["wa"], w["sg"], w["sb"], w["ws"], bst, mkv0, w["wo0"], row(ln_g, 0))
    grads["wo0"] = _mm_nn(cat0t, dr1b, "dw_o_0")
    grads["wa"] = _mm_nn(x0t, dz, "dw_a")
    grads["mkv0"] = _mm_tn(memb, dmkv0.astype(BF16), "dw_mkv_0")
    grads.update({
        "ws": dws, "b_s": dbst[:, :NG].T, "sg": dsgb[0:1], "sb": dsgb[1:2], "bf": dbf[:NG, 0], "loss": loss_part[0:1],
        "ln_g": jnp.concatenate([dgb00[0:1], dgb01[0:1], dgb10[0:1], dgb11[0:1]], axis=0),
        "ln_b": jnp.concatenate([dgb00[1:2], dgb01[1:2], dgb10[1:2], dgb11[1:2]], axis=0),
    })
    return loss_part, dx0, grads


def _place():
    xi, yi, ci = lax.axis_index("x"), lax.axis_index("y"), lax.axis_index("c")
    peers = [(xi, 1 - yi), (1 - xi, yi), (1 - xi, 1 - yi)]
    return xi, yi, ci, peers


def _comm_call(body, name, ins, out_shapes, n_remote, n_local):
    return pl.pallas_call(
        body, name=name, out_shape=out_shapes, in_specs=[ANY] * len(ins), out_specs=[ANY] * len(out_shapes),
        scratch_shapes=[pltpu.SemaphoreType.DMA((n_remote,)), pltpu.SemaphoreType.DMA((n_remote,)),
                        pltpu.SemaphoreType.DMA((max(n_local, 1),))],
        compiler_params=pltpu.CompilerParams(has_side_effects=True),
    )(*ins)


def _pair_swap(xs):
    n = len(xs)
    out_shapes = [jax.ShapeDtypeStruct((a.shape[0],) + a.shape[2:], a.dtype) for a in xs]

    def copies(x_refs, o_refs, sems):
        send_sems, recv_sems = sems
        xi, yi, ci, _ = _place()
        return [pltpu.make_async_remote_copy(src_ref=x_refs[i].at[:, 1 - ci], dst_ref=o_refs[i],
                                             send_sem=send_sems.at[i], recv_sem=recv_sems.at[i],
                                             device_id=(xi, yi, 1 - ci), device_id_type=MESH) for i in range(n)]

    def start(x_refs, o_refs, sems):
        for cp in copies(x_refs, o_refs, sems):
            cp.start()

    def finish(x_refs, o_refs, sems):
        cps = copies(x_refs, o_refs, sems)
        for cp in cps:
            cp.wait_recv()
        for cp in cps:
            cp.wait_send()

    return _Comm(xs, out_shapes, {}, [n, n], start, finish)


def _chip_exchange(xs):
    n = len(xs)
    out_shapes = [jax.ShapeDtypeStruct((3,) + a.shape[1:], a.dtype) for a in xs]

    def copies(x_refs, o_refs, sems):
        send_sems, recv_sems = sems
        xi, yi, ci, peers = _place()
        return [pltpu.make_async_remote_copy(src_ref=x_refs[i].at[2 * px + py], dst_ref=o_refs[i].at[d],
                                             send_sem=send_sems.at[3 * i + d], recv_sem=recv_sems.at[3 * i + d],
                                             device_id=(px, py, ci), device_id_type=MESH)
                for i in range(n) for d, (px, py) in enumerate(peers)]

    def start(x_refs, o_refs, sems):
        for cp in copies(x_refs, o_refs, sems):
            cp.start()

    def finish(x_refs, o_refs, sems):
        cps = copies(x_refs, o_refs, sems)
        for cp in cps:
            cp.wait_recv()
        for cp in cps:
            cp.wait_send()

    return _Comm(xs, out_shapes, {}, [3 * n, 3 * n], start, finish)


def _inplace_call(body, name, bufs, sem_counts):
    n = len(bufs)
    return pl.pallas_call(
        body, name=name, out_shape=[jax.ShapeDtypeStruct(a.shape, a.dtype) for a in bufs],
        in_specs=[ANY] * n, out_specs=[ANY] * n, input_output_aliases={i: i for i in range(n)},
        scratch_shapes=[pltpu.SemaphoreType.DMA((k,)) for k in sem_counts],
        compiler_params=pltpu.CompilerParams(has_side_effects=True),
    )(*bufs)


def _gather_weights(bufs, whole=()):
    n, nw = len(bufs), len(whole)

    def ici(w, sems, i, d, px, py, ci, slot):
        ref = w[i].at[slot, ci] if i < n else w[i].at[slot]
        return pltpu.make_async_remote_copy(src_ref=ref, dst_ref=ref, send_sem=sems[0].at[3 * i + d],
                                            recv_sem=sems[1].at[3 * i + d], device_id=(px, py, ci), device_id_type=MESH)

    def d2d(w, sems, i, d, xi, yi, ci, slot, half):
        ref = w[i].at[slot, half]
        return pltpu.make_async_remote_copy(src_ref=ref, dst_ref=ref, send_sem=sems[2].at[3 * i + d],
                                            recv_sem=sems[3].at[3 * i + d], device_id=(xi, yi, 1 - ci),
                                            device_id_type=MESH)

    def start(_, w, sems):
        xi, yi, ci, peers = _place()
        for i in range(n + nw):
            for d, (px, py) in enumerate(peers):
                ici(w, sems, i, d, px, py, ci, 2 * xi + yi).start()

    def finish(_, w, sems):
        xi, yi, ci, peers = _place()
        for d, (px, py) in enumerate(peers):
            for i in range(n + nw):
                ici(w, sems, i, d, px, py, ci, 2 * px + py).wait_recv()
                if i < n:
                    d2d(w, sems, i, d, xi, yi, ci, 2 * px + py, ci).start()
        for d, (px, py) in enumerate(peers):
            for i in range(n):
                d2d(w, sems, i, d, xi, yi, ci, 2 * px + py, 1 - ci).wait_recv()
        for d, (px, py) in enumerate(peers):
            for i in range(n + nw):
                ici(w, sems, i, d, px, py, ci, 2 * xi + yi).wait_send()
                if i < n:
                    d2d(w, sems, i, d, xi, yi, ci, 2 * px + py, ci).wait_send()

    arrays = list(bufs) + list(whole)
    k = 3 * (n + nw)
    return _Comm(arrays, [jax.ShapeDtypeStruct(a.shape, a.dtype) for a in arrays], {i: i for i in range(n + nw)},
                 [k, k, max(3 * n, 1), max(3 * n, 1)], start, finish)


def _pair_gather(bufs, spread, name):
    n = len(bufs)
    where = [(i, l) for i, a in enumerate(bufs) for l in range(a.shape[0])]
    nw = len(where)

    def body(*refs):
        g = refs[n + 1:2 * n + 1]
        sp = refs[2 * n + 1]
        send_sems, recv_sems = refs[2 * n + 2:]
        xi, yi, ci, peers = _place()
        chips = [(xi, yi)] + peers
        others = [(d, f) for d in range(4) for f in (0, 1) if (d, f) != (0, 0)]
        sends = []
        for k, (i, l) in enumerate(where):
            mine = g[i].at[l, :, ci]
            cp = pltpu.make_async_remote_copy(src_ref=mine, dst_ref=mine, send_sem=send_sems.at[k],
                                              recv_sem=recv_sems.at[k], device_id=(xi, yi, 1 - ci), device_id_type=MESH)
            cp.start()
            sends.append(cp)
        mine = sp.at[2 * xi + yi, ci]
        for k, (d, f) in enumerate(others):
            cp = pltpu.make_async_remote_copy(src_ref=mine, dst_ref=mine, send_sem=send_sems.at[nw + k],
                                              recv_sem=recv_sems.at[nw + k],
                                              device_id=(chips[d][0], chips[d][1], ci if f == 0 else 1 - ci),
                                              device_id_type=MESH)
            cp.start()
            sends.append(cp)
        for k, (i, l) in enumerate(where):
            other = g[i].at[l, :, 1 - ci]
            pltpu.make_async_remote_copy(src_ref=other, dst_ref=other, send_sem=send_sems.at[k],
                                         recv_sem=recv_sems.at[k], device_id=(xi, yi, 1 - ci),
                                         device_id_type=MESH).wait_recv()
        for k, (d, f) in enumerate(others):
            px, py, pc = chips[d][0], chips[d][1], (ci if f == 0 else 1 - ci)
            theirs = sp.at[2 * px + py, pc]
            pltpu.make_async_remote_copy(src_ref=theirs, dst_ref=theirs, send_sem=send_sems.at[nw + k],
                                         recv_sem=recv_sems.at[nw + k], device_id=(px, py, pc),
                                         device_id_type=MESH).wait_recv()
        for cp in sends:
            cp.wait_send()

    out = _inplace_call(body, name, list(bufs) + [spread], [nw + 7, nw + 7])
    return out[:n], out[n]


def _row_block(rows, cols):
    want = max(8, (2 ** 18 // max(cols, 1)) // 8 * 8)
    if rows <= want:
        return rows
    best = 8
    for t in range(8, want + 1, 8):
        if rows % t == 0:
            best = t
    return best


def _pair_add(g4, recv, sel, out_dtype, name):
    A, _, H, C = g4.shape
    bh = _row_block(H, C)

    def body(sel_ref, g_ref, r_ref, o_ref):
        o_ref[...] = (g_ref[...].astype(F32) + r_ref[...].astype(F32)).astype(out_dtype)

    return pl.pallas_call(
        body, name=name, out_shape=jax.ShapeDtypeStruct((A, H, C), out_dtype),
        grid_spec=pltpu.PrefetchScalarGridSpec(
            num_scalar_prefetch=1, grid=(A, H // bh),
            in_specs=[pl.BlockSpec((None, None, bh, C), lambda a, i, s: (a, s[1], i, 0)),
                      pl.BlockSpec((None, bh, C), lambda a, i, s: (a, i, 0))],
            out_specs=pl.BlockSpec((None, bh, C), lambda a, i, s: (a, i, 0))),
        compiler_params=pltpu.CompilerParams(dimension_semantics=("arbitrary", "arbitrary"),
                                             vmem_limit_bytes=VMEM_LIMIT),
    )(sel, g4, recv)


def _chip_reduce(g4, recv1, recv2, sel, buf, layer, n_layers, name, by_chip=False):
    _, _, H, C = g4.shape
    bh = _row_block(H, C)
    if by_chip:
        out_shape = jax.ShapeDtypeStruct((N_CHIPS, 2, H, C), F32)
        out_spec = pl.BlockSpec((None, None, bh, C), lambda i, s: (s[0], s[1], i, 0))
    else:
        out_shape = jax.ShapeDtypeStruct((n_layers, 1, 2, H, C), F32)
        out_spec = pl.BlockSpec((None, None, None, bh, C), lambda i, s: (layer, 0, s[1], i, 0))

    def body(sel_ref, g_ref, r1_ref, r2_ref, *rest):
        acc = g_ref[...].astype(F32) + r1_ref[...].astype(F32)
        for d in range(3):
            acc = acc + r2_ref[d].astype(F32)
        rest[-1][...] = acc

    in_specs = [pl.BlockSpec((None, None, bh, C), lambda i, s: (s[0], s[1], i, 0)),
                pl.BlockSpec((None, bh, C), lambda i, s: (s[0], i, 0)),
                pl.BlockSpec((3, bh, C), lambda i, s: (0, i, 0))]
    args = [sel, g4, recv1, recv2]
    aliases = {}
    if buf is not None:
        in_specs.append(ANY)
        args.append(buf)
        aliases = {4: 0}
    return pl.pallas_call(
        body, name=name, out_shape=out_shape,
        grid_spec=pltpu.PrefetchScalarGridSpec(num_scalar_prefetch=1, grid=(H // bh,), in_specs=in_specs,
                                               out_specs=out_spec),
        input_output_aliases=aliases,
        compiler_params=pltpu.CompilerParams(dimension_semantics=("arbitrary",), vmem_limit_bytes=VMEM_LIMIT),
    )(*args)


def _cast_place(shards, layer, sel, name):
    _, _, H, C = shards.shape
    bh = _row_block(H, C)

    def body(sel_ref, x_ref, o_ref):
        o_ref[...] = x_ref[...].astype(BF16)

    return pl.pallas_call(
        body, name=name, out_shape=jax.ShapeDtypeStruct((N_CHIPS, 2, H, C), BF16),
        grid_spec=pltpu.PrefetchScalarGridSpec(
            num_scalar_prefetch=1, grid=(2, H // bh),
            in_specs=[pl.BlockSpec((None, None, bh, C), lambda h, i, s: (layer, h, i, 0))],
            out_specs=pl.BlockSpec((None, None, bh, C), lambda h, i, s: (s[0], h, i, 0))),
        compiler_params=pltpu.CompilerParams(dimension_semantics=("arbitrary", "arbitrary"),
                                             vmem_limit_bytes=VMEM_LIMIT),
    )(sel, shards)


def _adamw(g, w, m, v, name):
    R, C = g.shape
    br = _row_block(R, C)

    def body(g_ref, w_ref, m_ref, v_ref, d_ref, mo_ref, vo_ref):
        gv = g_ref[...]
        mn = ADAM_B1 * m_ref[...] + (1.0 - ADAM_B1) * gv
        vn = ADAM_B2 * v_ref[...] + (1.0 - ADAM_B2) * (gv * gv)
        m_hat = mn / (1.0 - ADAM_B1 ** ADAM_STEP)
        v_hat = vn / (1.0 - ADAM_B2 ** ADAM_STEP)
        d_ref[...] = -ADAM_LR * (m_hat / (jnp.sqrt(v_hat) + ADAM_EPS) + ADAM_WD * w_ref[...])
        mo_ref[...] = mn
        vo_ref[...] = vn

    spec = pl.BlockSpec((br, C), lambda i: (i, 0))
    return pl.pallas_call(
        body, name=name, grid=(R // br,), out_shape=(jax.ShapeDtypeStruct((R, C), F32),) * 3,
        in_specs=[spec] * 4, out_specs=(spec,) * 3, compiler_params=_params(),
    )(g, w, m, v)


def _part_rows(shape):
    return -(-math.prod(shape) // (8 * LANES)) * 8


def _pack_rows(arrs, total_rows):
    parts = []
    for a in arrs:
        flat = a.reshape(-1)
        rows = _part_rows(a.shape)
        parts.append(jnp.pad(flat, (0, rows * LANES - flat.shape[0])).reshape(rows, LANES))
    packed = jnp.concatenate(parts, axis=0)
    return jnp.pad(packed, ((0, total_rows - packed.shape[0]), (0, 0)))


def _unpack_rows(packed, shapes):
    out, r = [], 0
    for shp in shapes:
        size, rows = math.prod(shp), _part_rows(shp)
        out.append(packed[r:r + rows].reshape(-1)[:size].reshape(shp))
        r += rows
    return out


def _rows_of(shapes):
    return sum(_part_rows(s) for s in shapes)


_GATHER_STAGES = {"start": ["wa", "mkv0", "wo0"], "mixer_a_fwd": ["wup0", "wdown0"],
                  "mlp_fwd_0": ["wq", "wkv", "mkv1", "wo1"], "fox_fwd": ["wup1", "wdown1"]}
_SWAP_STAGES = {"mixer_b_bwd": ["wup1", "wdown1"], "mlp_bwd_0": ["wo1", "wq", "wkv", "mkv1"]}
_REDUCE_STAGES = {"fox_bwd": ["wup1", "wdown1"], "mixer_a_bwd": ["wo1", "wq", "wkv", "mkv1", "wup0", "wdown0"],
                  "end": ["wa", "wo0", "mkv0", "small"]}
_SMALL = ["ws", "b_s", "sg", "sb", "bf", "ln_g", "ln_b", "loss"]


class _Staged:
    def __init__(self, placed, sel, dm, ng, shard_cols, n_small):
        self.placed, self.sel, self.dm, self.ng, self.shard_cols, self.n_small = placed, sel, dm, ng, shard_cols, n_small
        self.g4, self.recv1, self.recv2 = {}, {}, {}

    def _weight(self, name, a):
        a = a.reshape(N_CHIPS, 2 * a.shape[2], a.shape[3])
        if name in ("wa", "wkv"):
            a = a.transpose(1, 0, 2).reshape(a.shape[1], -1)
            if name == "wkv":
                a = jnp.concatenate([a[:, :2 * self.dm],
                                     jnp.pad(a[:, 2 * self.dm:], ((0, 0), (0, LANES - self.ng)))], axis=1)
            return a
        if name.startswith("wup") or name.startswith("wdown"):
            return a
        return a.reshape(-1, a.shape[2])

    def _partial(self, name, grads):
        if name == "small":
            a = _pack_rows([grads[n] for n in _SMALL], self.n_small).reshape(N_CHIPS, -1, LANES)
        elif name in self.shard_cols:
            n = self.shard_cols[name]
            g = grads[name][:, :N_CHIPS * n]
            a = g.reshape(g.shape[0], N_CHIPS, n).transpose(1, 0, 2)
        elif name.startswith("wup") or name.startswith("wdown"):
            a = grads[name]
        else:
            g = grads[name]
            a = g.reshape(N_CHIPS, g.shape[0] // N_CHIPS, g.shape[1])
        return a.reshape(N_CHIPS, 2, a.shape[1] // 2, a.shape[2])

    def plan(self, stage, w, grads):
        if stage in _GATHER_STAGES:
            return _gather_weights([self.placed[n] for n in _GATHER_STAGES[stage]])
        if stage in _SWAP_STAGES:
            for n in _SWAP_STAGES[stage]:
                self.g4[n] = self._partial(n, grads)
            return _pair_swap([self.g4[n] for n in _SWAP_STAGES[stage]])
        if stage in _REDUCE_STAGES:
            names = _REDUCE_STAGES[stage]
            late = [n for n in names if n not in self.recv1]
            for n in late:
                self.g4[n] = self._partial(n, grads)
            if late:
                got = _run_comm(_pair_swap([self.g4[n] for n in late]), "reduce_pair_swap_" + stage)
                self.recv1.update(zip(late, got))
            return _chip_exchange([_pair_add(self.g4[n], self.recv1[n], self.sel, F32 if n == "small" else BF16,
                                             "reduce_pair_add_" + n) for n in names])
        return None

    def done(self, stage, results, w):
        if stage in _GATHER_STAGES:
            for n, a in zip(_GATHER_STAGES[stage], results):
                w[n] = self._weight(n, a)
        elif stage in _SWAP_STAGES:
            self.recv1.update(zip(_SWAP_STAGES[stage], results))
        elif stage in _REDUCE_STAGES:
            self.recv2.update(zip(_REDUCE_STAGES[stage], results))


def kernel(x, mem, a_w_in, a_sgu_ln_g, a_sgu_ln_b, a_w_s, a_b_s, kv_w, kv_b_f, b_w_q, mem_w_kv, w_o, ln_g, ln_b, w_up, w_down, loss_target, m_a_w_in, m_a_sgu_ln_g, m_a_sgu_ln_b, m_a_w_s, m_a_b_s, m_kv_w, m_kv_b_f, m_b_w_q, m_mem_w_kv, m_w_o, m_ln_g, m_ln_b, m_w_up, m_w_down, v_a_w_in, v_a_sgu_ln_g, v_a_sgu_ln_b, v_a_w_s, v_a_b_s, v_kv_w, v_kv_b_f, v_b_w_q, v_mem_w_kv, v_w_o, v_ln_g, v_ln_b, v_w_up, v_w_down):
    xi, yi, ci = lax.axis_index("x"), lax.axis_index("y"), lax.axis_index("c")
    chip = 2 * xi + yi
    sel = jnp.stack([chip, ci]).astype(jnp.int32)
    S, D = x.shape[1], x.shape[2]
    DQ = mem_w_kv.shape[2] // 2
    DM = D - DQ
    NG = DM // HEAD_DIM

    shards = {"wa": (a_w_in, 0), "wkv": (kv_w[None], 0), "wq": (b_w_q, 0), "mkv0": (mem_w_kv, 0), "mkv1": (mem_w_kv, 1),
              "wo0": (w_o, 0), "wo1": (w_o, 1), "wup0": (w_up, 0), "wup1": (w_up, 1), "wdown0": (w_down, 0),
              "wdown1": (w_down, 1)}
    placed = {n: _cast_place(s.reshape(s.shape[0], 2, s.shape[1] // 2, s.shape[2]), layer, sel, "cast_place_" + n)
              for n, (s, layer) in shards.items()}
    ln_pack = jnp.concatenate([ln_g.reshape(4, -1), ln_b.reshape(4, -1)], axis=0)
    ln_buf = lax.dynamic_update_slice(jnp.zeros((N_CHIPS,) + ln_pack.shape, F32), ln_pack[None], (chip, 0, 0))
    ln_shape = (4, D)
    small_shapes = [a_w_s.shape, a_b_s.shape, a_sgu_ln_g.shape, a_sgu_ln_b.shape, kv_b_f.shape, ln_shape, ln_shape,
                    (1, LANES)]
    n_small = -(-_rows_of(small_shapes) // 64) * 64
    hooks = _Staged(placed, sel, DM, NG, {"wa": a_w_in.shape[2], "wkv": kv_w.shape[1]}, n_small)
    weights = {"sg": a_sgu_ln_g, "sb": a_sgu_ln_b, "ws": a_w_s[0], "b_s": a_b_s[0], "bf": kv_b_f}
    got = _run_comm(_gather_weights([placed[n] for n in _GATHER_STAGES["start"]], whole=[ln_buf]),
                    "gather_weights_first")
    hooks.done("start", got[:-1], weights)
    ln_full = got[-1].transpose(1, 0, 2).reshape(8, D)
    weights["ln_g"], weights["ln_b"] = ln_full[:4], ln_full[4:]

    _, grad_x, gr = _local_step(x[0], mem[0], loss_target[0], weights, hooks)

    hooks.done("end", _run_comm(hooks.plan("end", weights, gr), "reduce_chip_exchange_end"), weights)
    groups = [["wa"], ["wkv"], ["wq"], ["mkv0", "mkv1"], ["wo0", "wo1"], ["wup0", "wup1"], ["wdown0", "wdown1"]]
    mine = []
    for grp in groups:
        buf = None
        for layer, n in enumerate(grp):
            buf = _chip_reduce(hooks.g4[n], hooks.recv1[n], hooks.recv2[n], sel, buf, layer, len(grp),
                               "reduce_chip_sum_" + n)
        mine.append(buf)
    small_mine = _chip_reduce(hooks.g4["small"], hooks.recv1["small"], hooks.recv2["small"], sel, None, 0, 1,
                              "reduce_chip_sum_small", by_chip=True)
    red, small_all = _pair_gather(mine, small_mine, "reduce_pair_gather")
    red = [a.reshape(a.shape[0], 2 * a.shape[3], a.shape[4]) for a in red]
    g_a_w_in, g_kv_w, g_b_w_q, g_mem_w_kv, g_w_o, g_w_up, g_w_down = red
    g_kv_w = g_kv_w[0]
    g_w_down = g_w_down.transpose(0, 2, 1)
    small_all = small_all.reshape(n_small, LANES)
    g_ws, g_bs, g_sg, g_sb, g_bf, g_lng, g_lnb, loss_row = _unpack_rows(small_all, small_shapes)
    loss = loss_row[0, 0]
    dsh = D // N_CHIPS
    g_ln_g = lax.dynamic_slice_in_dim(g_lng, chip * dsh, dsh, axis=1).reshape(ln_g.shape)
    g_ln_b = lax.dynamic_slice_in_dim(g_lnb, chip * dsh, dsh, axis=1).reshape(ln_b.shape)

    def update(name, g, w, m, v):
        flat = lambda a: a.reshape(-1, a.shape[-1])
        d, mn, vn = _adamw(flat(g), flat(w), flat(m), flat(v), "adamw_" + name)
        return d.reshape(w.shape), mn.reshape(w.shape), vn.reshape(w.shape)

    upd = {
        "a_w_in": update("a_w_in", g_a_w_in, a_w_in, m_a_w_in, v_a_w_in),
        "kv_w": update("kv_w", g_kv_w, kv_w, m_kv_w, v_kv_w),
        "b_w_q": update("b_w_q", g_b_w_q, b_w_q, m_b_w_q, v_b_w_q),
        "mem_w_kv": update("mem_w_kv", g_mem_w_kv, mem_w_kv, m_mem_w_kv, v_mem_w_kv),
        "w_o": update("w_o", g_w_o, w_o, m_w_o, v_w_o),
        "w_up": update("w_up", g_w_up, w_up, m_w_up, v_w_up),
        "w_down": update("w_down", g_w_down, w_down, m_w_down, v_w_down),
    }
    tiny_g = [g_sg, g_sb, g_ws, g_bs, g_bf, g_ln_g, g_ln_b]
    tiny_w = [a_sgu_ln_g, a_sgu_ln_b, a_w_s, a_b_s, kv_b_f, ln_g, ln_b]
    tiny_m = [m_a_sgu_ln_g, m_a_sgu_ln_b, m_a_w_s, m_a_b_s, m_kv_b_f, m_ln_g, m_ln_b]
    tiny_v = [v_a_sgu_ln_g, v_a_sgu_ln_b, v_a_w_s, v_a_b_s, v_kv_b_f, v_ln_g, v_ln_b]
    tiny_shapes = [a.shape for a in tiny_w]
    n_tiny = -(-_rows_of(tiny_shapes) // 8) * 8
    td, tm_, tv = _adamw(_pack_rows(tiny_g, n_tiny), _pack_rows(tiny_w, n_tiny), _pack_rows(tiny_m, n_tiny),
                         _pack_rows(tiny_v, n_tiny), "adamw_small")
    for name, d, mn, vn in zip(["a_sgu_ln_g", "a_sgu_ln_b", "a_w_s", "a_b_s", "kv_b_f", "ln_g", "ln_b"],
                               _unpack_rows(td, tiny_shapes), _unpack_rows(tm_, tiny_shapes),
                               _unpack_rows(tv, tiny_shapes)):
        upd[name] = (d, mn, vn)

    order = ["a_w_in", "a_sgu_ln_g", "a_sgu_ln_b", "a_w_s", "a_b_s", "kv_w", "kv_b_f", "b_w_q", "mem_w_kv", "w_o",
             "ln_g", "ln_b", "w_up", "w_down"]
    grads = {"a_w_in": g_a_w_in.reshape(a_w_in.shape), "a_sgu_ln_g": g_sg.reshape(a_sgu_ln_g.shape),
             "a_sgu_ln_b": g_sb.reshape(a_sgu_ln_b.shape), "a_w_s": g_ws.reshape(a_w_s.shape),
             "a_b_s": g_bs.reshape(a_b_s.shape), "kv_w": g_kv_w, "kv_b_f": g_bf, "b_w_q": g_b_w_q.reshape(b_w_q.shape),
             "mem_w_kv": g_mem_w_kv, "w_o": g_w_o, "ln_g": g_ln_g, "ln_b": g_ln_b, "w_up": g_w_up, "w_down": g_w_down}
    return (loss, grad_x[None], *[grads[n] for n in order], *[upd[n][0] for n in order],
            *[upd[n][1] for n in order], *[upd[n][2] for n in order])
```

```python
import functools
import math

import jax
import jax.numpy as jnp
from jax import lax
from jax.experimental import pallas as pl
from jax.experimental.pallas import tpu as pltpu

F32 = jnp.float32
BF16 = jnp.bfloat16

HEAD_DIM = 64
PAIR = 2 * HEAD_DIM
CHUNK = 128
LN_EPS = 1e-5
ALPHA = 4 ** 0.25
QK_SCALE = 1.0 / math.sqrt(HEAD_DIM)
NEG = -1e30
N_CHIPS = 4
FOX_BLOCK = 256
LANES = 128
VMEM_LIMIT = 48 * 2 ** 20

ADAM_LR, ADAM_B1, ADAM_B2, ADAM_EPS, ADAM_WD, ADAM_STEP = 0.001, 0.9, 0.999, 1e-08, 0.01, 10

MESH = pl.DeviceIdType.MESH
ANY = pl.BlockSpec(memory_space=pl.ANY)


def _dot(a, b):
    return jnp.dot(a, b, preferred_element_type=F32)


def _dot_nt(a, b):
    return lax.dot_general(a, b, (((1,), (1,)), ((), ())), preferred_element_type=F32)


def _dot_tn(a, b):
    return lax.dot_general(a, b, (((0,), (0,)), ((), ())), preferred_element_type=F32)


def _ln_stats(r):
    mu = jnp.mean(r, axis=-1, keepdims=True)
    d = r - mu
    var = jnp.mean(d * d, axis=-1, keepdims=True)
    rstd = lax.rsqrt(var + LN_EPS)
    return d * rstd, rstd


def _ln_bwd(dy, xhat, rstd, g):
    dxh = dy * g
    m1 = jnp.mean(dxh, axis=-1, keepdims=True)
    m2 = jnp.mean(dxh * xhat, axis=-1, keepdims=True)
    return rstd * (dxh - m1 - xhat * m2)


_GELU_K = math.sqrt(2.0 / math.pi)


def _gelu(x):
    return 0.5 * x * (1.0 + jnp.tanh(_GELU_K * (x + 0.044715 * x * x * x)))


def _gelu_grad(x):
    t = jnp.tanh(_GELU_K * (x + 0.044715 * x * x * x))
    return 0.5 * (1.0 + t) + 0.5 * x * (1.0 - t * t) * _GELU_K * (1.0 + 3 * 0.044715 * x * x)


def _lane_mask(e):
    lane = lax.broadcasted_iota(jnp.int32, (1, PAIR), 1)
    return (lane >= HEAD_DIM * e) & (lane < HEAD_DIM * (e + 1))


def _first_head():
    return lax.broadcasted_iota(jnp.int32, (1, PAIR), 1) < HEAD_DIM


def _mem_probs(qh, mkp):
    sc = _dot_nt(qh, mkp) * QK_SCALE
    ex = jnp.exp(sc - jnp.max(sc, axis=-1, keepdims=True))
    return ex / jnp.sum(ex, axis=-1, keepdims=True)


def _mem_attn_fwd(qm_b, mkv_ref, dq_dim):
    outs = []
    for mp in range(dq_dim // PAIR):
        qp = qm_b[:, mp * PAIR:(mp + 1) * PAIR]
        mkp = mkv_ref[:, mp * PAIR:(mp + 1) * PAIR]
        mvp = mkv_ref[:, dq_dim + mp * PAIR:dq_dim + (mp + 1) * PAIR]
        heads = []
        for e in (0, 1):
            qh = jnp.where(_lane_mask(e), qp, jnp.zeros_like(qp))
            p = _mem_probs(qh, mkp)
            heads.append(_dot(p.astype(BF16), mvp))
        outs.append(jnp.where(_first_head(), heads[0], heads[1]))
    return outs


def _mem_attn_bwd(qm_b, mkv_ref, dmo_pairs, dmkv_ref, dq_dim):
    dqs = []
    for mp in range(dq_dim // PAIR):
        ks = slice(mp * PAIR, (mp + 1) * PAIR)
        vs = slice(dq_dim + mp * PAIR, dq_dim + (mp + 1) * PAIR)
        qp = qm_b[:, ks]
        mkp = mkv_ref[:, ks]
        mvp = mkv_ref[:, vs]
        dmo_b = dmo_pairs[mp].astype(BF16)
        dq = None
        dmk = None
        dmv = None
        for e in (0, 1):
            hm = _lane_mask(e)
            qh = jnp.where(hm, qp, jnp.zeros_like(qp))
            p = _mem_probs(qh, mkp)
            doh = jnp.where(hm, dmo_b, jnp.zeros_like(dmo_b))
            dp = _dot_nt(doh, mvp)
            ds = p * (dp - jnp.sum(dp * p, axis=-1, keepdims=True))
            dsb = (ds * QK_SCALE).astype(BF16)
            kh = jnp.where(hm, mkp, jnp.zeros_like(mkp))
            dq_e = _dot(dsb, kh)
            dmk_e = _dot_tn(dsb, qh)
            dmv_e = _dot_tn(p.astype(BF16), doh)
            dq = dq_e if dq is None else dq + dq_e
            dmk = dmk_e if dmk is None else dmk + dmk_e
            dmv = dmv_e if dmv is None else dmv + dmv_e
        dmkv_ref[:, ks] += dmk
        dmkv_ref[:, vs] += dmv
        dqs.append(dq)
    return dqs


def _params(n_axes=1):
    return pltpu.CompilerParams(dimension_semantics=("arbitrary",) * n_axes, vmem_limit_bytes=VMEM_LIMIT)


def _full(shape):
    return pl.BlockSpec(shape, lambda *_: (0,) * len(shape))


def _rows(tm, cols):
    return pl.BlockSpec((tm, cols), lambda i: (i, 0))


def _cols(rows, tm):
    return pl.BlockSpec((rows, tm), lambda i: (0, i))


def _pick(n, pref):
    if n <= pref:
        return n
    best = LANES
    for t in range(LANES, pref + 1, LANES):
        if n % t == 0:
            best = t
    return best if 2 * best >= pref or n > 2 * pref else n


class _Comm:
    def __init__(self, ins, out_shapes, aliases, sem_counts, start, finish):
        self.ins, self.out_shapes, self.aliases, self.sem_counts = list(ins), list(out_shapes), dict(aliases), sem_counts
        self.start, self.finish = start, finish


def _pcall(body, *, name, grid, in_specs, out_specs, out_shape, args, scratch_shapes=(), comm=None, aliases=None):
    n_in, n_out, n_scr = len(in_specs), len(out_shape), len(scratch_shapes)
    aliases = dict(aliases or {})
    if comm is None:
        res = pl.pallas_call(body, name=name, grid=grid, out_shape=tuple(out_shape), in_specs=list(in_specs),
                             out_specs=tuple(out_specs), scratch_shapes=list(scratch_shapes),
                             input_output_aliases=aliases, compiler_params=_params())(*args)
        return tuple(res), ()
    nci, nco = len(comm.ins), len(comm.out_shapes)
    last = grid[0] - 1

    def wrapped(*refs):
        ins, refs = refs[:n_in], refs[n_in:]
        cins, refs = refs[:nci], refs[nci:]
        outs, refs = refs[:n_out], refs[n_out:]
        couts, refs = refs[:nco], refs[nco:]
        scr, sems = refs[:n_scr], refs[n_scr:]

        @pl.when(pl.program_id(0) == 0)
        def _():
            comm.start(cins, couts, sems)

        body(*ins, *outs, *scr)

        @pl.when(pl.program_id(0) == last)
        def _():
            comm.finish(cins, couts, sems)

    res = pl.pallas_call(
        wrapped, name=name, grid=grid, out_shape=tuple(out_shape) + tuple(comm.out_shapes),
        in_specs=list(in_specs) + [ANY] * nci, out_specs=tuple(out_specs) + (ANY,) * nco,
        input_output_aliases={**aliases, **{n_in + a: n_out + b for a, b in comm.aliases.items()}},
        scratch_shapes=list(scratch_shapes) + [pltpu.SemaphoreType.DMA((k,)) for k in comm.sem_counts],
        compiler_params=pltpu.CompilerParams(dimension_semantics=("arbitrary",), vmem_limit_bytes=VMEM_LIMIT,
                                             has_side_effects=True),
    )(*args, *comm.ins)
    return tuple(res[:n_out]), tuple(res[n_out:])


def _run_comm(comm, name):
    nci, nco = len(comm.ins), len(comm.out_shapes)

    def body(*refs):
        cins, couts, sems = refs[:nci], refs[nci:nci + nco], refs[nci + nco:]
        comm.start(cins, couts, sems)
        comm.finish(cins, couts, sems)

    return pl.pallas_call(
        body, name=name, out_shape=tuple(comm.out_shapes), in_specs=[ANY] * nci, out_specs=(ANY,) * nco,
        input_output_aliases=comm.aliases, scratch_shapes=[pltpu.SemaphoreType.DMA((k,)) for k in comm.sem_counts],
        compiler_params=pltpu.CompilerParams(has_side_effects=True),
    )(*comm.ins)


def _mm(a, b, name, a_is_transposed, out_split=1):
    (M, R) = a.shape if a_is_transposed else a.shape[::-1]
    _, N = b.shape
    bm, br = _pick(M, 1024 if a_is_transposed else 512), _pick(R, 1024 if a_is_transposed else 512)
    ncol = N // out_split
    bn = _pick(ncol, 1024)
    per = ncol // bn
    last = R // br - 1

    def body(a_ref, b_ref, o_ref, acc_ref):
        @pl.when(pl.program_id(2) == 0)
        def _():
            acc_ref[...] = jnp.zeros_like(acc_ref)

        acc_ref[...] += (_dot if a_is_transposed else _dot_tn)(a_ref[...], b_ref[...])

        @pl.when(pl.program_id(2) == last)
        def _():
            o_ref[...] = acc_ref[...].astype(BF16)

    if out_split == 1:
        out_shape = jax.ShapeDtypeStruct((M, N), BF16)
        out_spec = pl.BlockSpec((bm, bn), lambda i, j, r: (i, j))
    else:
        out_shape = jax.ShapeDtypeStruct((out_split, M, ncol), BF16)
        out_spec = pl.BlockSpec((None, bm, bn), lambda i, j, r: (j // per, i, j % per))
    a_spec = (pl.BlockSpec((bm, br), lambda i, j, r: (i, r)) if a_is_transposed
              else pl.BlockSpec((br, bm), lambda i, j, r: (r, i)))
    return pl.pallas_call(
        body, name=name, out_shape=out_shape, grid=(M // bm, N // bn, R // br),
        in_specs=[a_spec, pl.BlockSpec((br, bn), lambda i, j, r: (r, j))], out_specs=out_spec,
        scratch_shapes=[pltpu.VMEM((bm, bn), F32)],
        compiler_params=pltpu.CompilerParams(dimension_semantics=("parallel", "parallel", "arbitrary"),
                                             vmem_limit_bytes=VMEM_LIMIT),
    )(a, b)


def _mm_tn(a, b, name, out_split=1):
    return _mm(a, b, name, False, out_split)


def _mm_nn(at, b, name, out_split=1):
    return _mm(at, b, name, True, out_split)


def _mm_small(a, b, name):
    def body(a_ref, b_ref, o_ref):
        o_ref[...] = _dot(a_ref[...], b_ref[...]).astype(BF16)

    return pl.pallas_call(body, name=name, out_shape=jax.ShapeDtypeStruct((a.shape[0], b.shape[1]), BF16),
                          compiler_params=pltpu.CompilerParams(vmem_limit_bytes=VMEM_LIMIT))(a, b)


def _mlp_out(x_ref, wup_ref, wdown_ref, g_ref, b_ref, r_ref, h_ref):
    NC, _, FC = wup_ref.shape
    xv = x_ref[...]
    xb = xv.astype(BF16)
    y = jnp.zeros(xv.shape, F32)
    for c in range(NC):
        a = jnp.maximum(_dot(xb, wup_ref[c]), 0.0)
        hb = (a * a).astype(BF16)
        h_ref[:, c * FC:(c + 1) * FC] = hb
        y = y + _dot(hb, wdown_ref[c])
    r = ALPHA * xv + y
    r_ref[...] = r
    xhat, _ = _ln_stats(r)
    return xhat * g_ref[...] + b_ref[...]


def _mlp_fwd_loss(x, wup4, wdown4, g, b, target, name):
    S, D = x.shape
    NC, _, FC = wup4.shape
    tm = min(256, S)

    def body(x_ref, wup_ref, wdown_ref, g_ref, b_ref, t_ref, d_ref, r_ref, h_ref, l_ref):
        @pl.when(pl.program_id(0) == 0)
        def _():
            l_ref[...] = jnp.zeros_like(l_ref)

        err = _mlp_out(x_ref, wup_ref, wdown_ref, g_ref, b_ref, r_ref, h_ref) - t_ref[...]
        d_ref[...] = err * (1.0 / D)
        l_ref[...] += (0.5 / D) * jnp.sum(err * err)

    return pl.pallas_call(
        body, name=name, grid=(S // tm,),
        out_shape=(jax.ShapeDtypeStruct((S, D), F32), jax.ShapeDtypeStruct((S, D), F32),
                   jax.ShapeDtypeStruct((S, NC * FC), BF16), jax.ShapeDtypeStruct((8, LANES), F32)),
        in_specs=[_rows(tm, D), _full(wup4.shape), _full(wdown4.shape), _full((1, D)), _full((1, D)), _rows(tm, D)],
        out_specs=(_rows(tm, D), _rows(tm, D), _rows(tm, NC * FC), _full((8, LANES))),
        compiler_params=_params(),
    )(x, wup4, wdown4, g, b, target)


def _mlp_fwd(x, wup4, wdown4, g, b, name, comm=None):
    S, D = x.shape
    NC, _, FC = wup4.shape
    tm = min(256, S)

    def body(x_ref, wup_ref, wdown_ref, g_ref, b_ref, xo_ref, xb_ref, xbt_ref, r_ref, h_ref):
        xo = _mlp_out(x_ref, wup_ref, wdown_ref, g_ref, b_ref, r_ref, h_ref)
        xo_ref[...] = xo
        xb_ref[...] = xo.astype(BF16)
        xbt_ref[...] = xo.T.astype(BF16)

    return _pcall(
        body, name=name, grid=(S // tm,), comm=comm,
        out_shape=(jax.ShapeDtypeStruct((S, D), F32), jax.ShapeDtypeStruct((S, D), BF16),
                   jax.ShapeDtypeStruct((D, S), BF16), jax.ShapeDtypeStruct((S, D), F32),
                   jax.ShapeDtypeStruct((S, NC * FC), BF16)),
        in_specs=[_rows(tm, D), _full(wup4.shape), _full(wdown4.shape), _full((1, D)), _full((1, D))],
        out_specs=(_rows(tm, D), _rows(tm, D), _cols(D, tm), _rows(tm, D), _rows(tm, NC * FC)),
        args=(x, wup4, wdown4, g, b))


def _mlp_bwd(dxo, r, h, wup4, wdown4, g, name, comm=None):
    S, D = r.shape
    NC, _, FC = wup4.shape
    tm = min(256, S)

    def body(dxo_ref, r_ref, h_ref, wup_ref, wdown_ref, g_ref, dx_ref, drbt_ref, dpre_ref, dgb_ref):
        @pl.when(pl.program_id(0) == 0)
        def _():
            dgb_ref[...] = jnp.zeros_like(dgb_ref)

        xhat, rstd = _ln_stats(r_ref[...])
        dy = dxo_ref[...]
        dgb_ref[0:1, :] += jnp.sum(dy * xhat, axis=0, keepdims=True)
        dgb_ref[1:2, :] += jnp.sum(dy, axis=0, keepdims=True)
        dr = _ln_bwd(dy, xhat, rstd, g_ref[...])
        drb = dr.astype(BF16)
        drbt_ref[...] = dr.T.astype(BF16)
        dx = ALPHA * dr
        for c in range(NC):
            dh = _dot_nt(drb, wdown_ref[c])
            a = jnp.sqrt(h_ref[:, c * FC:(c + 1) * FC].astype(F32))
            dpre = (2.0 * a * dh).astype(BF16)
            dpre_ref[:, c * FC:(c + 1) * FC] = dpre
            dx = dx + _dot_nt(dpre, wup_ref[c])
        dx_ref[...] = dx

    return _pcall(
        body, name=name, grid=(S // tm,), comm=comm,
        out_shape=(jax.ShapeDtypeStruct((S, D), F32), jax.ShapeDtypeStruct((D, S), BF16),
                   jax.ShapeDtypeStruct((S, NC * FC), BF16), jax.ShapeDtypeStruct((2, D), F32)),
        in_specs=[_rows(tm, D), _rows(tm, D), _rows(tm, NC * FC), _full(wup4.shape), _full(wdown4.shape),
                  _full((1, D))],
        out_specs=(_rows(tm, D), _cols(D, tm), _rows(tm, NC * FC), _full((2, D))),
        args=(dxo, r, h, wup4, wdown4, g))


def _tail_fwd_store(xv, cat_ref, wo_ref, g_ref, b_ref, xo_ref, xbt_ref, r_ref, catt_ref):
    cat = cat_ref[...]
    y = _dot(cat, wo_ref[...])
    r = ALPHA * xv + y
    xhat, _ = _ln_stats(r)
    xo = xhat * g_ref[...] + b_ref[...]
    xo_ref[...] = xo
    xbt_ref[...] = xo.T.astype(BF16)
    catt_ref[...] = cat.astype(F32).T.astype(BF16)
    r_ref[...] = r


def _tail_bwd_head(dxo_ref, r_ref, g_ref, wo_ref, dgb_ref, drb_ref):
    xhat, rstd = _ln_stats(r_ref[...])
    dy = dxo_ref[...]
    dgb_ref[0:1, :] += jnp.sum(dy * xhat, axis=0, keepdims=True)
    dgb_ref[1:2, :] += jnp.sum(dy, axis=0, keepdims=True)
    dr = _ln_bwd(dy, xhat, rstd, g_ref[...])
    drb = dr.astype(BF16)
    drb_ref[...] = drb
    return dr, _dot_nt(drb, wo_ref[...])


def _causal(shape):
    return lax.broadcasted_iota(jnp.int32, shape, 1) <= lax.broadcasted_iota(jnp.int32, shape, 0)


def _sgu_mixed(ws_ref, bst_ref, gp, vpair):
    tril = _causal((CHUNK, CHUNK))
    w0 = jnp.where(tril, ws_ref[2 * gp], 0.0).astype(BF16)
    w1 = jnp.where(tril, ws_ref[2 * gp + 1], 0.0).astype(BF16)
    m0 = _dot(w0, vpair) + bst_ref[:, 2 * gp:2 * gp + 1]
    m1 = _dot(w1, vpair) + bst_ref[:, 2 * gp + 1:2 * gp + 2]
    return jnp.where(_first_head(), m0, m1), w0, w1


def _mixer_a_fwd(x, wa, sg, sb, ws, bst, mkv, wo, g, b, name, comm=None):
    S, D = x.shape
    DQ = mkv.shape[1] // 2
    DM = D - DQ
    NP = DM // PAIR
    tm = min(512, S)

    def body(x_ref, wa_ref, sg_ref, sb_ref, ws_ref, bst_ref, mkv_ref, wo_ref, g_ref, b_ref,
             xo_ref, xbt_ref, r_ref, catt_ref, xint_ref, z_s, zv_s, cat_ref):
        xv = x_ref[...]
        xint_ref[...] = xv.T.astype(BF16)
        z_s[...] = _dot(xv.astype(BF16), wa_ref[...])
        vhat, _ = _ln_stats(_gelu(z_s[:, DM:2 * DM]))
        zv_s[...] = (vhat * sg_ref[...] + sb_ref[...]).astype(BF16)
        for c in range(tm // CHUNK):
            rs = slice(c * CHUNK, (c + 1) * CHUNK)
            for gp in range(NP):
                ls = slice(gp * PAIR, (gp + 1) * PAIR)
                mixed, _, _ = _sgu_mixed(ws_ref, bst_ref, gp, zv_s[rs, ls])
                cat_ref[rs, ls] = (_gelu(z_s[rs, ls]) * mixed).astype(BF16)
        mo = _mem_attn_fwd(z_s[:, 2 * DM:].astype(BF16), mkv_ref, DQ)
        for mp in range(DQ // PAIR):
            cat_ref[:, DM + mp * PAIR:DM + (mp + 1) * PAIR] = mo[mp].astype(BF16)
        _tail_fwd_store(xv, cat_ref, wo_ref, g_ref, b_ref, xo_ref, xbt_ref, r_ref, catt_ref)

    return _pcall(
        body, name=name, grid=(S // tm,), comm=comm,
        out_shape=(jax.ShapeDtypeStruct((S, D), F32), jax.ShapeDtypeStruct((D, S), BF16),
                   jax.ShapeDtypeStruct((S, D), F32), jax.ShapeDtypeStruct((D, S), BF16),
                   jax.ShapeDtypeStruct((D, S), BF16)),
        in_specs=[_rows(tm, D), _full(wa.shape), _full(sg.shape), _full(sb.shape), _full(ws.shape), _full(bst.shape),
                  _full(mkv.shape), _full(wo.shape), _full((1, D)), _full((1, D))],
        out_specs=(_rows(tm, D), _cols(D, tm), _rows(tm, D), _cols(D, tm), _cols(D, tm)),
        scratch_shapes=[pltpu.VMEM((tm, 2 * DM + DQ), F32), pltpu.VMEM((tm, DM), BF16), pltpu.VMEM((tm, D), BF16)],
        args=(x, wa, sg, sb, ws, bst, mkv, wo, g, b))


def _mixer_a_bwd(x, dxo, r, wa, sg, sb, ws, bst, mkv, wo, g, name, comm=None):
    S, D = x.shape
    DQ = mkv.shape[1] // 2
    DM = D - DQ
    NP = DM // PAIR
    NG = DM // HEAD_DIM
    tm = min(256, S)

    def body(x_ref, dxo_ref, r_ref, wa_ref, sg_ref, sb_ref, ws_ref, bst_ref, mkv_ref, wo_ref, g_ref,
             dx_ref, dz_ref, drb_ref, dgb_ref, dsgb_ref, dws_ref, dbst_ref, dmkv_ref, z_s, zv_s, dzv_s):
        @pl.when(pl.program_id(0) == 0)
        def _():
            dgb_ref[...] = jnp.zeros_like(dgb_ref)
            dsgb_ref[...] = jnp.zeros_like(dsgb_ref)
            dws_ref[...] = jnp.zeros_like(dws_ref)
            dbst_ref[...] = jnp.zeros_like(dbst_ref)
            dmkv_ref[...] = jnp.zeros_like(dmkv_ref)

        dr, dcat = _tail_bwd_head(dxo_ref, r_ref, g_ref, wo_ref, dgb_ref, drb_ref)
        z_s[...] = _dot(x_ref[...].astype(BF16), wa_ref[...])
        vhat, vrstd = _ln_stats(_gelu(z_s[:, DM:2 * DM]))
        zv_s[...] = (vhat * sg_ref[...] + sb_ref[...]).astype(BF16)
        tril = _causal((CHUNK, CHUNK))
        lane = lax.broadcasted_iota(jnp.int32, (1, LANES), 1)
        for c in range(tm // CHUNK):
            rs = slice(c * CHUNK, (c + 1) * CHUNK)
            for gp in range(NP):
                ls = slice(gp * PAIR, (gp + 1) * PAIR)
                vpair = zv_s[rs, ls]
                mixed, w0, w1 = _sgu_mixed(ws_ref, bst_ref, gp, vpair)
                u_pre = z_s[rs, ls]
                dmix = dcat[rs, ls]
                dz_ref[rs, ls] = (dmix * mixed * _gelu_grad(u_pre)).astype(BF16)
                dmixed = dmix * _gelu(u_pre)
                first = _first_head()
                d0 = jnp.where(first, dmixed, 0.0)
                d1 = jnp.where(first, 0.0, dmixed)
                d0b = d0.astype(BF16)
                d1b = d1.astype(BF16)
                dzv_s[rs, ls] = _dot_tn(w0, d0b) + _dot_tn(w1, d1b)
                dws_ref[2 * gp] += jnp.where(tril, _dot_nt(d0b, vpair), 0.0)
                dws_ref[2 * gp + 1] += jnp.where(tril, _dot_nt(d1b, vpair), 0.0)
                dbst_ref[...] += (jnp.where(lane == 2 * gp, jnp.sum(d0, axis=-1, keepdims=True), 0.0)
                                  + jnp.where(lane == 2 * gp + 1, jnp.sum(d1, axis=-1, keepdims=True), 0.0))
        dzv = dzv_s[...]
        dsgb_ref[0:1, :] += jnp.sum(dzv * vhat, axis=0, keepdims=True)
        dsgb_ref[1:2, :] += jnp.sum(dzv, axis=0, keepdims=True)
        dgv = _ln_bwd(dzv, vhat, vrstd, sg_ref[...])
        dz_ref[:, DM:2 * DM] = (dgv * _gelu_grad(z_s[:, DM:2 * DM])).astype(BF16)
        dmo = [dcat[:, DM + mp * PAIR:DM + (mp + 1) * PAIR] for mp in range(DQ // PAIR)]
        dqm = _mem_attn_bwd(z_s[:, 2 * DM:].astype(BF16), mkv_ref, dmo, dmkv_ref, DQ)
        for mp in range(DQ // PAIR):
            dz_ref[:, 2 * DM + mp * PAIR:2 * DM + (mp + 1) * PAIR] = dqm[mp].astype(BF16)
        dx_ref[...] = ALPHA * dr + _dot_nt(dz_ref[...], wa_ref[...])

    ZW = 2 * DM + DQ
    M = mkv.shape[0]
    return _pcall(
        body, name=name, grid=(S // tm,), comm=comm,
        out_shape=(jax.ShapeDtypeStruct((S, D), F32), jax.ShapeDtypeStruct((S, ZW), BF16),
                   jax.ShapeDtypeStruct((S, D), BF16), jax.ShapeDtypeStruct((2, D), F32),
                   jax.ShapeDtypeStruct((2, DM), F32), jax.ShapeDtypeStruct((NG, CHUNK, CHUNK), F32),
                   jax.ShapeDtypeStruct((CHUNK, LANES), F32), jax.ShapeDtypeStruct((M, 2 * DQ), F32)),
        in_specs=[_rows(tm, D), _rows(tm, D), _rows(tm, D), _full(wa.shape), _full(sg.shape), _full(sb.shape),
                  _full(ws.shape), _full(bst.shape), _full(mkv.shape), _full(wo.shape), _full((1, D))],
        out_specs=(_rows(tm, D), _rows(tm, ZW), _rows(tm, D), _full((2, D)), _full((2, DM)),
                   _full((NG, CHUNK, CHUNK)), _full((CHUNK, LANES)), _full((M, 2 * DQ))),
        scratch_shapes=[pltpu.VMEM((tm, ZW), F32), pltpu.VMEM((tm, DM), BF16), pltpu.VMEM((tm, DM), F32)],
        args=(x, dxo, r, wa, sg, sb, ws, bst, mkv, wo, g))


def _proj_b_fwd(xb, wq, wkv, dm, name):
    S, D = xb.shape
    DQ = D - dm
    tm = min(512, S)

    def body(x_ref, wq_ref, wkv_ref, q_ref, qm_ref, k_ref, v_ref, fl_ref):
        xv = x_ref[...]
        z = _dot(xv, wq_ref[...])
        q_ref[...] = z[:, :dm].astype(BF16)
        qm_ref[...] = z[:, dm:].astype(BF16)
        kvf = _dot(xv, wkv_ref[...])
        k_ref[...] = kvf[:, :dm].astype(BF16)
        v_ref[...] = kvf[:, dm:2 * dm].astype(BF16)
        fl_ref[...] = kvf[:, 2 * dm:]

    return pl.pallas_call(
        body, name=name, grid=(S // tm,),
        out_shape=(jax.ShapeDtypeStruct((S, dm), BF16), jax.ShapeDtypeStruct((S, DQ), BF16),
                   jax.ShapeDtypeStruct((S, dm), BF16), jax.ShapeDtypeStruct((S, dm), BF16),
                   jax.ShapeDtypeStruct((S, LANES), F32)),
        in_specs=[_rows(tm, D), _full(wq.shape), _full(wkv.shape)],
        out_specs=(_rows(tm, dm), _rows(tm, DQ), _rows(tm, dm), _rows(tm, dm), _rows(tm, LANES)),
        compiler_params=_params(),
    )(xb, wq, wkv)


def _proj_b_bwd(dr, dqz, dkvf, wq, wkv, name):
    S, D = dr.shape
    tm = min(512, S)

    def body(dr_ref, dqz_ref, dkvf_ref, wq_ref, wkv_ref, dx_ref):
        dx_ref[...] = (ALPHA * dr_ref[...] + _dot_nt(dqz_ref[...], wq_ref[...])
                       + _dot_nt(dkvf_ref[...], wkv_ref[...]))

    return pl.pallas_call(
        body, name=name, grid=(S // tm,), out_shape=jax.ShapeDtypeStruct((S, D), F32),
        in_specs=[_rows(tm, D), _rows(tm, dqz.shape[1]), _rows(tm, dkvf.shape[1]), _full(wq.shape), _full(wkv.shape)],
        out_specs=_rows(tm, D), compiler_params=_params(),
    )(dr, dqz, dkvf, wq, wkv)


def _split3(v):
    hi = v.astype(BF16)
    r1 = v - hi.astype(F32)
    mid = r1.astype(BF16)
    lo = (r1 - mid.astype(F32)).astype(BF16)
    return hi, mid, lo


def _tri_sum(v, tri_b):
    hi, mid, lo = _split3(v)
    return _dot(hi, tri_b) + _dot(mid, tri_b) + _dot(lo, tri_b)


def _log_sigmoid(x):
    return jnp.minimum(x, 0.0) - jnp.log(1.0 + jnp.exp(-jnp.abs(x)))


def _forget_cumsum(flt, bf, name):
    H, S = flt.shape
    nchunk = S // LANES

    def body(fl_ref, bf_ref, p_ref):
        upper = (lax.broadcasted_iota(jnp.int32, (LANES, LANES), 0)
                 <= lax.broadcasted_iota(jnp.int32, (LANES, LANES), 1)).astype(BF16)

        def step(n, carry):
            s0 = pl.multiple_of(n * LANES, LANES)
            lf = _log_sigmoid(fl_ref[:, pl.ds(s0, LANES)] + bf_ref[...])
            cs = _tri_sum(lf, upper) + carry
            for i, piece in enumerate(_split3(-cs)):
                p_ref[i, :, pl.ds(s0, LANES)] = piece.astype(F32)
            return cs[:, LANES - 1:LANES]

        lax.fori_loop(0, nchunk, step, jnp.zeros((H, 1), F32))

    return pl.pallas_call(body, name=name, out_shape=jax.ShapeDtypeStruct((3, H, S), F32),
                          compiler_params=pltpu.CompilerParams(vmem_limit_bytes=VMEM_LIMIT))(flt, bf)


def _forget_cumsum_bwd(dct, flt, bf, name):
    H, S = flt.shape
    nchunk = S // LANES

    def body(dc_ref, fl_ref, bf_ref, dfl_ref, dbf_ref):
        lower = (lax.broadcasted_iota(jnp.int32, (LANES, LANES), 0)
                 >= lax.broadcasted_iota(jnp.int32, (LANES, LANES), 1)).astype(BF16)

        def step(n, carry):
            tail, tot = carry
            s0 = pl.multiple_of((nchunk - 1 - n) * LANES, LANES)
            suffix = _tri_sum(dc_ref[:, pl.ds(s0, LANES)], lower) + tail
            xv = fl_ref[:, pl.ds(s0, LANES)] + bf_ref[...]
            dfl = suffix * (1.0 / (1.0 + jnp.exp(xv)))
            dfl_ref[:, pl.ds(s0, LANES)] = dfl
            return suffix[:, 0:1], tot + jnp.sum(dfl, axis=-1, keepdims=True)

        _, tot = lax.fori_loop(0, nchunk, step, (jnp.zeros((H, 1), F32), jnp.zeros((H, 1), F32)))
        dbf_ref[...] = jnp.broadcast_to(tot, (H, LANES))

    return pl.pallas_call(body, name=name,
                          out_shape=(jax.ShapeDtypeStruct((H, S), F32), jax.ShapeDtypeStruct((H, LANES), F32)),
                          compiler_params=pltpu.CompilerParams(vmem_limit_bytes=VMEM_LIMIT))(dct, flt, bf)


N_BIAS = 3


def _bias_ones():
    lane = lax.broadcasted_iota(jnp.int32, (1, PAIR), 1)
    return ((lane & (HEAD_DIM - 1)) < N_BIAS).astype(BF16)


def _fox_fwd(q, k, v, ca, name, comm=None):
    S, DM = q.shape
    NP = DM // PAIR
    T = min(FOX_BLOCK, S)
    nq = S // T

    def body(q_ref, k_ref, v_ref, ca_ref, o_ref, lse_ref):
        tril = _causal((T, T))
        heads = (_lane_mask(0), _lane_mask(1))
        ones3 = _bias_ones()

        def step(carry, qaug, kaug, vaug, masked):
            m, acc = carry
            s = _dot_nt(qaug, kaug)
            if masked:
                s = jnp.where(tril, s, NEG)
            m_new = jnp.maximum(m, jnp.max(s, axis=-1, keepdims=True))
            p = jnp.exp(s - m_new).astype(BF16)
            return m_new, jnp.exp(m - m_new) * acc + _dot(p, vaug)

        def kv_block(kj):
            s0 = pl.multiple_of(kj * T, T)
            kb, vb, cab = k_ref[pl.ds(s0, T), :], v_ref[pl.ds(s0, T), :], ca_ref[pl.ds(s0, T), :]
            return ([jnp.where(heads[e], kb, cab) for e in (0, 1)],
                    [jnp.where(heads[e], vb, jnp.ones_like(vb)) for e in (0, 1)])

        def q_pair(a, _):
            t0s = [pl.multiple_of((2 * a + r) * T, T) for r in (0, 1)]
            qaug = {}
            for r in (0, 1):
                qb = q_ref[pl.ds(t0s[r], T), :] * QK_SCALE
                for e in (0, 1):
                    qaug[r, e] = jnp.where(heads[e], qb, ones3)
            chains = [(r, e) for r in (0, 1) for e in (0, 1)]

            def full(first_block, n_blocks, carries):
                carries = list(carries)
                for kk in range(n_blocks):
                    kaug, vaug = kv_block(first_block + kk)
                    scores = [_dot_nt(qaug[r, e], kaug[e]) for r, e in chains]
                    for n, (r, e) in enumerate(chains):
                        m, acc = carries[n]
                        m_new = jnp.maximum(m, jnp.max(scores[n], axis=-1, keepdims=True))
                        p = jnp.exp(scores[n] - m_new).astype(BF16)
                        carries[n] = (m_new, jnp.exp(m - m_new) * acc + _dot(p, vaug[e]))
                return tuple(carries)

            init = (jnp.full((T, 1), NEG, F32), jnp.zeros((T, PAIR), F32))
            carries = lax.fori_loop(0, a // 2, lambda i, c: full(4 * i, 4, c), (init,) * 4)
            carries = list(lax.fori_loop(0, a % 2, lambda i, c: full(2 * a - 2, 2, c), carries))
            kaug, vaug = kv_block(2 * a)
            for i, (r, e) in enumerate(chains):
                carries[i] = step(carries[i], qaug[r, e], kaug[e], vaug[e], r == 0)
            kaug, vaug = kv_block(2 * a + 1)
            for i, (r, e) in enumerate(chains):
                if r == 1:
                    carries[i] = step(carries[i], qaug[r, e], kaug[e], vaug[e], True)
            first = _first_head()
            for r in (0, 1):
                (m0, acc0), (m1, acc1) = carries[2 * r], carries[2 * r + 1]
                l0, l1 = acc0[:, HEAD_DIM:HEAD_DIM + 1], acc1[:, 0:1]
                o_ref[pl.ds(t0s[r], T), :] = jnp.where(first, acc0 / l0, acc1 / l1).astype(BF16)
                lse_t = jnp.where(first, m0 + jnp.log(l0), m1 + jnp.log(l1)).T
                lse_ref[0:1, pl.ds(t0s[r], T)] = lse_t[0:1, :]
                lse_ref[1:2, pl.ds(t0s[r], T)] = lse_t[HEAD_DIM:HEAD_DIM + 1, :]
            return 0

        lax.fori_loop(0, nq // 2, q_pair, 0)

    col = pl.BlockSpec((S, PAIR), lambda hp: (0, hp))
    return _pcall(
        body, name=name, grid=(NP,), comm=comm,
        out_shape=(jax.ShapeDtypeStruct((S, DM), BF16), jax.ShapeDtypeStruct((NP, 2, S), F32)),
        in_specs=[col, col, col, col], out_specs=(col, pl.BlockSpec((None, 2, S), lambda hp: (hp, 0, 0))),
        args=(q, k, v, ca))


def _fox_bwd(q, k, v, o, do, ca, lse, name, comm=None):
    S, DM = q.shape
    NP = DM // PAIR
    T = min(FOX_BLOCK, S)
    nq = S // T

    def body(q_ref, k_ref, v_ref, o_ref, do_ref, ca_ref, lse_ref, dq_ref, dk_ref, dv_ref, dc_ref,
             dq_acc, dcol_acc, dk_acc, dv_acc, dd_s):
        keep = lax.broadcasted_iota(jnp.int32, (T, T), 0) <= lax.broadcasted_iota(jnp.int32, (T, T), 1)
        lane = lax.broadcasted_iota(jnp.int32, (1, LANES), 1)
        heads = (_lane_mask(0), _lane_mask(1))
        ones3 = _bias_ones()
        dq_acc[...] = jnp.zeros_like(dq_acc)
        dcol_acc[...] = jnp.zeros_like(dcol_acc)
        dc_ref[...] = jnp.zeros_like(dc_ref)

        def row_dots(qi, _):
            t0 = pl.multiple_of(qi * T, T)
            prod = do_ref[pl.ds(t0, T), :].astype(F32) * o_ref[pl.ds(t0, T), :].astype(F32)
            cols = [jnp.sum(jnp.where(heads[e], prod, 0.0), axis=-1, keepdims=True) for e in (0, 1)]
            tile = jnp.where(lane == 0, cols[0], jnp.where(lane == 1, cols[1], 0.0))
            dd_s[:, pl.ds(t0, T)] = tile.T[0:8, :]
            return 0

        lax.fori_loop(0, nq, row_dots, 0)

        def kv_pair(b, _):
            dk_acc[...] = jnp.zeros_like(dk_acc)
            dv_acc[...] = jnp.zeros_like(dv_acc)
            s0s = [pl.multiple_of((2 * b + jj) * T, T) for jj in (0, 1)]
            kaug, ks, vbs = {}, {}, []
            for jj in (0, 1):
                kb = k_ref[pl.ds(s0s[jj], T), :]
                cab = ca_ref[pl.ds(s0s[jj], T), :]
                vbs.append(v_ref[pl.ds(s0s[jj], T), :])
                for e in (0, 1):
                    kaug[jj, e] = jnp.where(heads[e], kb, cab)
                    ks[jj, e] = jnp.where(heads[e], kb, jnp.zeros_like(kb)) * QK_SCALE

            def q_step(qi, blocks):
                t0 = pl.multiple_of(qi * T, T)
                qs = q_ref[pl.ds(t0, T), :] * QK_SCALE
                dob = do_ref[pl.ds(t0, T), :]
                doh = [jnp.where(heads[e], dob, jnp.zeros_like(dob)) for e in (0, 1)]
                qh = [jnp.where(heads[e], qs, jnp.zeros_like(qs)) for e in (0, 1)]
                chains = [(e, jj, masked) for e in (0, 1) for jj, masked in blocks]
                scores = [_dot_nt(kaug[jj, e], jnp.where(heads[e], qs, ones3)) for e, jj, _ in chains]
                dps = [_dot_nt(vbs[jj], doh[e]) for e, jj, _ in chains]
                dq = jnp.zeros((T, PAIR), F32)
                drow = [jnp.zeros((1, T), F32), jnp.zeros((1, T), F32)]
                for n, (e, jj, masked) in enumerate(chains):
                    st = jnp.where(keep, scores[n], NEG) if masked else scores[n]
                    pt = jnp.exp(st - lse_ref[e:e + 1, pl.ds(t0, T)])
                    dv_acc[jj] += _dot(pt.astype(BF16), doh[e])
                    dst = pt * (dps[n] - dd_s[e:e + 1, pl.ds(t0, T)])
                    dsb = dst.astype(BF16)
                    dk_acc[jj] += _dot(dsb, qh[e])
                    dq = dq + _dot_tn(dsb, ks[jj, e])
                    dcol_acc[pl.ds(s0s[jj], T), :] -= jnp.where(lane == e, jnp.sum(dst, axis=-1, keepdims=True), 0.0)
                    drow[e] = drow[e] + jnp.sum(dst, axis=0, keepdims=True)
                for e in (0, 1):
                    dc_ref[e:e + 1, pl.ds(t0, T)] += drow[e]
                dq_acc[pl.ds(t0, T), :] += dq

            q_step(2 * b, [(0, True)])
            q_step(2 * b + 1, [(0, False), (1, True)])

            def rest(i, _):
                q_step(2 * b + 2 + 2 * i, [(0, False), (1, False)])
                q_step(2 * b + 3 + 2 * i, [(0, False), (1, False)])
                return 0

            lax.fori_loop(0, (nq - 2) // 2 - b, rest, 0)
            for jj in (0, 1):
                dk_ref[pl.ds(s0s[jj], T), :] = dk_acc[jj].astype(BF16)
                dv_ref[pl.ds(s0s[jj], T), :] = dv_acc[jj].astype(BF16)
            return 0

        lax.fori_loop(0, nq // 2, kv_pair, 0)
        dq_ref[...] = dq_acc[...].astype(BF16)

        def add_cols(kj, _):
            s0 = pl.multiple_of(kj * T, T)
            dc_ref[:, pl.ds(s0, T)] += dcol_acc[pl.ds(s0, T), :].T[0:2, :]
            return 0

        lax.fori_loop(0, nq, add_cols, 0)

    col = pl.BlockSpec((S, PAIR), lambda hp: (0, hp))
    row2 = pl.BlockSpec((None, 2, S), lambda hp: (hp, 0, 0))
    return _pcall(
        body, name=name, grid=(NP,), comm=comm,
        out_shape=(jax.ShapeDtypeStruct(do.shape, BF16), jax.ShapeDtypeStruct((S, DM), BF16),
                   jax.ShapeDtypeStruct((S, DM), BF16), jax.ShapeDtypeStruct((NP, 2, S), F32)),
        in_specs=[col, col, col, col, col, col, row2], out_specs=(col, col, col, row2),
        scratch_shapes=[pltpu.VMEM((S, PAIR), F32), pltpu.VMEM((S, LANES), F32), pltpu.VMEM((2, T, PAIR), F32),
                        pltpu.VMEM((2, T, PAIR), F32), pltpu.VMEM((8, S), F32)],
        args=(q, k, v, o, do, ca, lse), aliases={4: 0})


def _mixer_b_fwd(x, o, qm, mkv, wo, g, b, name):
    S, D = x.shape
    DQ = qm.shape[1]
    DM = D - DQ
    tm = min(512, S)

    def body(x_ref, o_ref, qm_ref, mkv_ref, wo_ref, g_ref, b_ref, xo_ref, xbt_ref, r_ref, catt_ref, cat_ref):
        cat_ref[:, :DM] = o_ref[...]
        mo = _mem_attn_fwd(qm_ref[...], mkv_ref, DQ)
        for mp in range(DQ // PAIR):
            cat_ref[:, DM + mp * PAIR:DM + (mp + 1) * PAIR] = mo[mp].astype(BF16)
        _tail_fwd_store(x_ref[...], cat_ref, wo_ref, g_ref, b_ref, xo_ref, xbt_ref, r_ref, catt_ref)

    return pl.pallas_call(
        body, name=name, grid=(S // tm,),
        out_shape=(jax.ShapeDtypeStruct((S, D), F32), jax.ShapeDtypeStruct((D, S), BF16),
                   jax.ShapeDtypeStruct((S, D), F32), jax.ShapeDtypeStruct((D, S), BF16)),
        in_specs=[_rows(tm, D), _rows(tm, DM), _rows(tm, DQ), _full(mkv.shape), _full(wo.shape), _full((1, D)),
                  _full((1, D))],
        out_specs=(_rows(tm, D), _cols(D, tm), _rows(tm, D), _cols(D, tm)),
        scratch_shapes=[pltpu.VMEM((tm, D), BF16)],
        compiler_params=_params(),
    )(x, o, qm, mkv, wo, g, b)


def _mixer_b_bwd(dxo, r, qm, mkv, wo, g, name, comm=None):
    S, D = r.shape
    DQ = qm.shape[1]
    DM = D - DQ
    M = mkv.shape[0]
    tm = min(512, S)

    def body(dxo_ref, r_ref, qm_ref, mkv_ref, wo_ref, g_ref, dr_ref, drb_ref, dcat_ref, dgb_ref, dmkv_ref):
        @pl.when(pl.program_id(0) == 0)
        def _():
            dgb_ref[...] = jnp.zeros_like(dgb_ref)
            dmkv_ref[...] = jnp.zeros_like(dmkv_ref)

        dr, dcat = _tail_bwd_head(dxo_ref, r_ref, g_ref, wo_ref, dgb_ref, drb_ref)
        dr_ref[...] = dr
        dcat_ref[:, :DM] = dcat[:, :DM].astype(BF16)
        dmo = [dcat[:, DM + mp * PAIR:DM + (mp + 1) * PAIR] for mp in range(DQ // PAIR)]
        dqm = _mem_attn_bwd(qm_ref[...], mkv_ref, dmo, dmkv_ref, DQ)
        for mp in range(DQ // PAIR):
            dcat_ref[:, DM + mp * PAIR:DM + (mp + 1) * PAIR] = dqm[mp].astype(BF16)

    return _pcall(
        body, name=name, grid=(S // tm,), comm=comm,
        out_shape=(jax.ShapeDtypeStruct((S, D), F32), jax.ShapeDtypeStruct((S, D), BF16),
                   jax.ShapeDtypeStruct((S, D), BF16), jax.ShapeDtypeStruct((2, D), F32),
                   jax.ShapeDtypeStruct((M, 2 * DQ), F32)),
        in_specs=[_rows(tm, D), _rows(tm, D), _rows(tm, DQ), _full(mkv.shape), _full(wo.shape), _full((1, D))],
        out_specs=(_rows(tm, D), _rows(tm, D), _rows(tm, D), _full((2, D)), _full((M, 2 * DQ))),
        args=(dxo, r, qm, mkv, wo, g))


class _NoExchange:
    def plan(self, stage, w, grads):
        return None

    def done(self, stage, results, w):
        pass


def _local_step(x, mem, target, w, hooks):
    S, D = x.shape
    DQ = w["mkv0"].shape[1] // 2
    DM = D - DQ
    NG = DM // HEAD_DIM
    NP = DM // PAIR
    ln_g, ln_b = w["ln_g"], w["ln_b"]
    row = lambda a, i: a[i:i + 1]
    memb = mem.astype(BF16)
    bst =jnp.pad(w["b_s"].T, ((0, 0), (0, LANES - NG)))
    grads = {}

    def staged(stage, fn, *args):
        out, got = fn(*args, stage, comm=hooks.plan(stage, w, grads))
        hooks.done(stage, got, w)
        return out

    mkv0 = _mm_small(memb, w["mkv0"], "mem_kv_0")
    x1, x1t, r1, cat0t, x0t = staged("mixer_a_fwd", _mixer_a_fwd, x, w["wa"], w["sg"], w["sb"], w["ws"], bst, mkv0,
                                w["wo0"], row(ln_g, 0), row(ln_b, 0))
    x2, x2b, x2t, r2, h0 = staged("mlp_fwd_0", _mlp_fwd, x1, w["wup0"], w["wdown0"], row(ln_g, 1), row(ln_b, 1))
    q, qm, k, v, fl = _proj_b_fwd(x2b, w["wq"], w["wkv"], DM, "proj_b_fwd")
    flt = fl[:, :16].T
    bfc = jnp.pad(w["bf"], (0, 16 - NG)).reshape(16, 1)
    pieces = _forget_cumsum(flt, bfc, "forget_cumsum")
    pieces = jnp.pad(pieces[:, :NG].transpose(2, 1, 0), ((0, 0), (0, 0), (0, HEAD_DIM - N_BIAS)))
    ca = pieces.reshape(S, NP, 2, HEAD_DIM)[:, :, ::-1].reshape(S, DM).astype(BF16)
    o, lse = staged("fox_fwd", _fox_fwd, q, k, v, ca)
    mkv1 = _mm_small(memb, w["mkv1"], "mem_kv_1")
    x3, x3t, r3, cat1t = _mixer_b_fwd(x2, o, qm, mkv1, w["wo1"], row(ln_g, 2), row(ln_b, 2), "mixer_b_fwd")
    dx4, r4, h1, loss_part = _mlp_fwd_loss(x3, w["wup1"], w["wdown1"], row(ln_g, 3), row(ln_b, 3), target,
                                           "mlp_fwd_1")

    nc = w["wup0"].shape[0]
    (dx3, dr4t, dpre1, dgb11), _ = _mlp_bwd(dx4, r4, h1, w["wup1"], w["wdown1"], row(ln_g, 3), "mlp_bwd_1")
    grads["wdown1"] = _mm_nn(dr4t, h1, "dw_down_1", out_split=nc)
    grads["wup1"] = _mm_nn(x3t, dpre1, "dw_up_1", out_split=nc)
    dr3, dr3b, dcat1, dgb10, dmkv1 = staged("mixer_b_bwd", _mixer_b_bwd, dx3, r3, qm, mkv1, w["wo1"], row(ln_g, 2))
    dqz, dk, dv, dct3 = staged("fox_bwd", _fox_bwd, q, k, v, o, dcat1, ca, lse)
    dct = jnp.pad(dct3.reshape(NG, S), ((0, 16 - NG), (0, 0)))
    dflt, dbf = _forget_cumsum_bwd(dct, flt, bfc, "forget_cumsum_bwd")
    dfl = jnp.pad(dflt.T, ((0, 0), (0, LANES - 16))).astype(BF16)
    dkvf = jnp.concatenate([dk, dv, dfl], axis=1)
    dx2 = _proj_b_bwd(dr3, dqz, dkvf, w["wq"], w["wkv"], "proj_b_bwd")
    grads["wo1"] = _mm_nn(cat1t, dr3b, "dw_o_1")
    grads["wq"] = _mm_nn(x2t, dqz, "dw_q")
    grads["wkv"] = _mm_nn(x2t, dkvf, "dw_kv")
    grads["mkv1"] = _mm_tn(memb, dmkv1.astype(BF16), "dw_mkv_1")
    dx1, dr2t, dpre0, dgb01 = staged("mlp_bwd_0", _mlp_bwd, dx2, r2, h0, w["wup0"], w["wdown0"], row(ln_g, 1))
    grads["wdown0"] = _mm_nn(dr2t, h0, "dw_down_0", out_split=nc)
    grads["wup0"] = _mm_nn(x1t, dpre0, "dw_up_0", out_split=nc)
    dx0, dz, dr1b, dgb00, dsgb, dws, dbst, dmkv0 = staged(
        "mixer_a_bwd", _mixer_a_bwd, x, dx1, r1, w["wa"], w["sg"], w["sb"], w["ws"], bst, mkv0, w["wo0"], row(ln_g, 0))
    grads["wo0"] = _mm_nn(cat0t, dr1b, "dw_o_0")
    grads["wa"] = _mm_nn(x0t, dz, "dw_a")
    grads["mkv0"] = _mm_tn(memb, dmkv0.astype(BF16), "dw_mkv_0")
    grads.update({
        "ws": dws, "b_s": dbst[:, :NG].T, "sg": dsgb[0:1], "sb": dsgb[1:2], "bf": dbf[:NG, 0], "loss": loss_part[0:1],
        "ln_g": jnp.concatenate([dgb00[0:1], dgb01[0:1], dgb10[0:1], dgb11[0:1]], axis=0),
        "ln_b": jnp.concatenate([dgb00[1:2], dgb01[1:2], dgb10[1:2], dgb11[1:2]], axis=0),
    })
    return loss_part, dx0, grads


def _place():
    xi, yi, ci = lax.axis_index("x"), lax.axis_index("y"), lax.axis_index("c")
    peers = [(xi, 1 - yi), (1 - xi, yi), (1 - xi, 1 - yi)]
    return xi, yi, ci, peers


def _comm_call(body, name, ins, out_shapes, n_remote, n_local):
    return pl.pallas_call(
        body, name=name, out_shape=out_shapes, in_specs=[ANY] * len(ins), out_specs=[ANY] * len(out_shapes),
        scratch_shapes=[pltpu.SemaphoreType.DMA((n_remote,)), pltpu.SemaphoreType.DMA((n_remote,)),
                        pltpu.SemaphoreType.DMA((max(n_local, 1),))],
        compiler_params=pltpu.CompilerParams(has_side_effects=True),
    )(*ins)


def _pair_swap(xs):
    n = len(xs)
    out_shapes = [jax.ShapeDtypeStruct((a.shape[0],) + a.shape[2:], a.dtype) for a in xs]

    def copies(x_refs, o_refs, sems):
        send_sems, recv_sems = sems
        xi, yi, ci, _ = _place()
        return [pltpu.make_async_remote_copy(src_ref=x_refs[i].at[:, 1 - ci], dst_ref=o_refs[i],
                                             send_sem=send_sems.at[i], recv_sem=recv_sems.at[i],
                                             device_id=(xi, yi, 1 - ci), device_id_type=MESH) for i in range(n)]

    def start(x_refs, o_refs, sems):
        for cp in copies(x_refs, o_refs, sems):
            cp.start()

    def finish(x_refs, o_refs, sems):
        cps = copies(x_refs, o_refs, sems)
        for cp in cps:
            cp.wait_recv()
        for cp in cps:
            cp.wait_send()

    return _Comm(xs, out_shapes, {}, [n, n], start, finish)


def _chip_exchange(xs):
    n = len(xs)
    out_shapes = [jax.ShapeDtypeStruct((3,) + a.shape[1:], a.dtype) for a in xs]

    def copies(x_refs, o_refs, sems):
        send_sems, recv_sems = sems
        xi, yi, ci, peers = _place()
        return [pltpu.make_async_remote_copy(src_ref=x_refs[i].at[2 * px + py], dst_ref=o_refs[i].at[d],
                                             send_sem=send_sems.at[3 * i + d], recv_sem=recv_sems.at[3 * i + d],
                                             device_id=(px, py, ci), device_id_type=MESH)
                for i in range(n) for d, (px, py) in enumerate(peers)]

    def start(x_refs, o_refs, sems):
        for cp in copies(x_refs, o_refs, sems):
            cp.start()

    def finish(x_refs, o_refs, sems):
        cps = copies(x_refs, o_refs, sems)
        for cp in cps:
            cp.wait_recv()
        for cp in cps:
            cp.wait_send()

    return _Comm(xs, out_shapes, {}, [3 * n, 3 * n], start, finish)


def _inplace_call(body, name, bufs, sem_counts):
    n = len(bufs)
    return pl.pallas_call(
        body, name=name, out_shape=[jax.ShapeDtypeStruct(a.shape, a.dtype) for a in bufs],
        in_specs=[ANY] * n, out_specs=[ANY] * n, input_output_aliases={i: i for i in range(n)},
        scratch_shapes=[pltpu.SemaphoreType.DMA((k,)) for k in sem_counts],
        compiler_params=pltpu.CompilerParams(has_side_effects=True),
    )(*bufs)


def _gather_weights(bufs, whole=()):
    n, nw = len(bufs), len(whole)

    def ici(w, sems, i, d, px, py, ci, slot):
        ref = w[i].at[slot, ci] if i < n else w[i].at[slot]
        return pltpu.make_async_remote_copy(src_ref=ref, dst_ref=ref, send_sem=sems[0].at[3 * i + d],
                                            recv_sem=sems[1].at[3 * i + d], device_id=(px, py, ci), device_id_type=MESH)

    def d2d(w, sems, i, d, xi, yi, ci, slot, half):
        ref = w[i].at[slot, half]
        return pltpu.make_async_remote_copy(src_ref=ref, dst_ref=ref, send_sem=sems[2].at[3 * i + d],
                                            recv_sem=sems[3].at[3 * i + d], device_id=(xi, yi, 1 - ci),
                                            device_id_type=MESH)

    def start(_, w, sems):
        xi, yi, ci, peers = _place()
        for i in range(n + nw):
            for d, (px, py) in enumerate(peers):
                ici(w, sems, i, d, px, py, ci, 2 * xi + yi).start()

    def finish(_, w, sems):
        xi, yi, ci, peers = _place()
        for d, (px, py) in enumerate(peers):
            for i in range(n + nw):
                ici(w, sems, i, d, px, py, ci, 2 * px + py).wait_recv()
                if i < n:
                    d2d(w, sems, i, d, xi, yi, ci, 2 * px + py, ci).start()
        for d, (px, py) in enumerate(peers):
            for i in range(n):
                d2d(w, sems, i, d, xi, yi, ci, 2 * px + py, 1 - ci).wait_recv()
        for d, (px, py) in enumerate(peers):
            for i in range(n + nw):
                ici(w, sems, i, d, px, py, ci, 2 * xi + yi).wait_send()
                if i < n:
                    d2d(w, sems, i, d, xi, yi, ci, 2 * px + py, ci).wait_send()

    arrays = list(bufs) + list(whole)
    k = 3 * (n + nw)
    return _Comm(arrays, [jax.ShapeDtypeStruct(a.shape, a.dtype) for a in arrays], {i: i for i in range(n + nw)},
                 [k, k, max(3 * n, 1), max(3 * n, 1)], start, finish)


def _pair_gather(bufs, spread, name):
    n = len(bufs)
    where = [(i, l) for i, a in enumerate(bufs) for l in range(a.shape[0])]
    nw = len(where)

    def body(*refs):
        g = refs[n + 1:2 * n + 1]
        sp = refs[2 * n + 1]
        send_sems, recv_sems = refs[2 * n + 2:]
        xi, yi, ci, peers = _place()
        chips = [(xi, yi)] + peers
        others = [(d, f) for d in range(4) for f in (0, 1) if (d, f) != (0, 0)]
        sends = []
        for k, (i, l) in enumerate(where):
            mine = g[i].at[l, :, ci]
            cp = pltpu.make_async_remote_copy(src_ref=mine, dst_ref=mine, send_sem=send_sems.at[k],
                                              recv_sem=recv_sems.at[k], device_id=(xi, yi, 1 - ci), device_id_type=MESH)
            cp.start()
            sends.append(cp)
        mine = sp.at[2 * xi + yi, ci]
        for k, (d, f) in enumerate(others):
            cp = pltpu.make_async_remote_copy(src_ref=mine, dst_ref=mine, send_sem=send_sems.at[nw + k],
                                              recv_sem=recv_sems.at[nw + k],
                                              device_id=(chips[d][0], chips[d][1], ci if f == 0 else 1 - ci),
                                              device_id_type=MESH)
            cp.start()
            sends.append(cp)
        for k, (i, l) in enumerate(where):
            other = g[i].at[l, :, 1 - ci]
            pltpu.make_async_remote_copy(src_ref=other, dst_ref=other, send_sem=send_sems.at[k],
                                         recv_sem=recv_sems.at[k], device_id=(xi, yi, 1 - ci),
                                         device_id_type=MESH).wait_recv()
        for k, (d, f) in enumerate(others):
            px, py, pc = chips[d][0], chips[d][1], (ci if f == 0 else 1 - ci)
            theirs = sp.at[2 * px + py, pc]
            pltpu.make_async_remote_copy(src_ref=theirs, dst_ref=theirs, send_sem=send_sems.at[nw + k],
                                         recv_sem=recv_sems.at[nw + k], device_id=(px, py, pc),
                                         device_id_type=MESH).wait_recv()
        for cp in sends:
            cp.wait_send()

    out = _inplace_call(body, name, list(bufs) + [spread], [nw + 7, nw + 7])
    return out[:n], out[n]


def _row_block(rows, cols):
    want = max(8, (2 ** 18 // max(cols, 1)) // 8 * 8)
    if rows <= want:
        return rows
    best = 8
    for t in range(8, want + 1, 8):
        if rows % t == 0:
            best = t
    return best


def _pair_add(g4, recv, sel, out_dtype, name):
    A, _, H, C = g4.shape
    bh = _row_block(H, C)

    def body(sel_ref, g_ref, r_ref, o_ref):
        o_ref[...] = (g_ref[...].astype(F32) + r_ref[...].astype(F32)).astype(out_dtype)

    return pl.pallas_call(
        body, name=name, out_shape=jax.ShapeDtypeStruct((A, H, C), out_dtype),
        grid_spec=pltpu.PrefetchScalarGridSpec(
            num_scalar_prefetch=1, grid=(A, H // bh),
            in_specs=[pl.BlockSpec((None, None, bh, C), lambda a, i, s: (a, s[1], i, 0)),
                      pl.BlockSpec((None, bh, C), lambda a, i, s: (a, i, 0))],
            out_specs=pl.BlockSpec((None, bh, C), lambda a, i, s: (a, i, 0))),
        compiler_params=pltpu.CompilerParams(dimension_semantics=("arbitrary", "arbitrary"),
                                             vmem_limit_bytes=VMEM_LIMIT),
    )(sel, g4, recv)


def _chip_reduce(g4, recv1, recv2, sel, buf, layer, n_layers, name, by_chip=False):
    _, _, H, C = g4.shape
    bh = _row_block(H, C)
    if by_chip:
        out_shape = jax.ShapeDtypeStruct((N_CHIPS, 2, H, C), F32)
        out_spec = pl.BlockSpec((None, None, bh, C), lambda i, s: (s[0], s[1], i, 0))
    else:
        out_shape = jax.ShapeDtypeStruct((n_layers, 1, 2, H, C), F32)
        out_spec = pl.BlockSpec((None, None, None, bh, C), lambda i, s: (layer, 0, s[1], i, 0))

    def body(sel_ref, g_ref, r1_ref, r2_ref, *rest):
        acc = g_ref[...].astype(F32) + r1_ref[...].astype(F32)
        for d in range(3):
            acc = acc + r2_ref[d].astype(F32)
        rest[-1][...] = acc

    in_specs = [pl.BlockSpec((None, None, bh, C), lambda i, s: (s[0], s[1], i, 0)),
                pl.BlockSpec((None, bh, C), lambda i, s: (s[0], i, 0)),
                pl.BlockSpec((3, bh, C), lambda i, s: (0, i, 0))]
    args = [sel, g4, recv1, recv2]
    aliases = {}
    if buf is not None:
        in_specs.append(ANY)
        args.append(buf)
        aliases = {4: 0}
    return pl.pallas_call(
        body, name=name, out_shape=out_shape,
        grid_spec=pltpu.PrefetchScalarGridSpec(num_scalar_prefetch=1, grid=(H // bh,), in_specs=in_specs,
                                               out_specs=out_spec),
        input_output_aliases=aliases,
        compiler_params=pltpu.CompilerParams(dimension_semantics=("arbitrary",), vmem_limit_bytes=VMEM_LIMIT),
    )(*args)


def _cast_place(shards, layer, sel, name):
    _, _, H, C = shards.shape
    bh = _row_block(H, C)

    def body(sel_ref, x_ref, o_ref):
        o_ref[...] = x_ref[...].astype(BF16)

    return pl.pallas_call(
        body, name=name, out_shape=jax.ShapeDtypeStruct((N_CHIPS, 2, H, C), BF16),
        grid_spec=pltpu.PrefetchScalarGridSpec(
            num_scalar_prefetch=1, grid=(2, H // bh),
            in_specs=[pl.BlockSpec((None, None, bh, C), lambda h, i, s: (layer, h, i, 0))],
            out_specs=pl.BlockSpec((None, None, bh, C), lambda h, i, s: (s[0], h, i, 0))),
        compiler_params=pltpu.CompilerParams(dimension_semantics=("arbitrary", "arbitrary"),
                                             vmem_limit_bytes=VMEM_LIMIT),
    )(sel, shards)


def _adamw(g, w, m, v, name):
    R, C = g.shape
    br = _row_block(R, C)

    def body(g_ref, w_ref, m_ref, v_ref, d_ref, mo_ref, vo_ref):
        gv = g_ref[...]
        mn = ADAM_B1 * m_ref[...] + (1.0 - ADAM_B1) * gv
        vn = ADAM_B2 * v_ref[...] + (1.0 - ADAM_B2) * (gv * gv)
        m_hat = mn / (1.0 - ADAM_B1 ** ADAM_STEP)
        v_hat = vn / (1.0 - ADAM_B2 ** ADAM_STEP)
        d_ref[...] = -ADAM_LR * (m_hat / (jnp.sqrt(v_hat) + ADAM_EPS) + ADAM_WD * w_ref[...])
        mo_ref[...] = mn
        vo_ref[...] = vn

    spec = pl.BlockSpec((br, C), lambda i: (i, 0))
    return pl.pallas_call(
        body, name=name, grid=(R // br,), out_shape=(jax.ShapeDtypeStruct((R, C), F32),) * 3,
        in_specs=[spec] * 4, out_specs=(spec,) * 3, compiler_params=_params(),
    )(g, w, m, v)


def _part_rows(shape):
    return -(-math.prod(shape) // (8 * LANES)) * 8


def _pack_rows(arrs, total_rows):
    parts = []
    for a in arrs:
        flat = a.reshape(-1)
        rows = _part_rows(a.shape)
        parts.append(jnp.pad(flat, (0, rows * LANES - flat.shape[0])).reshape(rows, LANES))
    packed = jnp.concatenate(parts, axis=0)
    return jnp.pad(packed, ((0, total_rows - packed.shape[0]), (0, 0)))


def _unpack_rows(packed, shapes):
    out, r = [], 0
    for shp in shapes:
        size, rows = math.prod(shp), _part_rows(shp)
        out.append(packed[r:r + rows].reshape(-1)[:size].reshape(shp))
        r += rows
    return out


def _rows_of(shapes):
    return sum(_part_rows(s) for s in shapes)


_GATHER_STAGES = {"start": ["wa", "mkv0", "wo0"], "mixer_a_fwd": ["wup0", "wdown0"],
                  "mlp_fwd_0": ["wq", "wkv", "mkv1", "wo1"], "fox_fwd": ["wup1", "wdown1"]}
_SWAP_STAGES = {"mixer_b_bwd": ["wup1", "wdown1"]}
_REDUCE_STAGES = {"fox_bwd": ["wup1", "wdown1"], "mlp_bwd_0": ["wo1", "wq", "wkv", "mkv1"],
                  "mixer_a_bwd": ["wup0", "wdown0"], "end": ["wa", "wo0", "mkv0", "small"]}
_SMALL = ["ws", "b_s", "sg", "sb", "bf", "ln_g", "ln_b", "loss"]


class _Staged:
    def __init__(self, placed, sel, dm, ng, shard_cols, n_small):
        self.placed, self.sel, self.dm, self.ng, self.shard_cols, self.n_small = placed, sel, dm, ng, shard_cols, n_small
        self.g4, self.recv1, self.recv2 = {}, {}, {}

    def _weight(self, name, a):
        a = a.reshape(N_CHIPS, 2 * a.shape[2], a.shape[3])
        if name in ("wa", "wkv"):
            a = a.transpose(1, 0, 2).reshape(a.shape[1], -1)
            if name == "wkv":
                a = jnp.concatenate([a[:, :2 * self.dm],
                                     jnp.pad(a[:, 2 * self.dm:], ((0, 0), (0, LANES - self.ng)))], axis=1)
            return a
        if name.startswith("wup") or name.startswith("wdown"):
            return a
        return a.reshape(-1, a.shape[2])

    def _partial(self, name, grads):
        if name == "small":
            a = _pack_rows([grads[n] for n in _SMALL], self.n_small).reshape(N_CHIPS, -1, LANES)
        elif name in self.shard_cols:
            n = self.shard_cols[name]
            g = grads[name][:, :N_CHIPS * n]
            a = g.reshape(g.shape[0], N_CHIPS, n).transpose(1, 0, 2)
        elif name.startswith("wup") or name.startswith("wdown"):
            a = grads[name]
        else:
            g = grads[name]
            a = g.reshape(N_CHIPS, g.shape[0] // N_CHIPS, g.shape[1])
        return a.reshape(N_CHIPS, 2, a.shape[1] // 2, a.shape[2])

    def plan(self, stage, w, grads):
        if stage in _GATHER_STAGES:
            return _gather_weights([self.placed[n] for n in _GATHER_STAGES[stage]])
        if stage in _SWAP_STAGES:
            for n in _SWAP_STAGES[stage]:
                self.g4[n] = self._partial(n, grads)
            return _pair_swap([self.g4[n] for n in _SWAP_STAGES[stage]])
        if stage in _REDUCE_STAGES:
            names = _REDUCE_STAGES[stage]
            late = [n for n in names if n not in self.recv1]
            for n in late:
                self.g4[n] = self._partial(n, grads)
            if late:
                got = _run_comm(_pair_swap([self.g4[n] for n in late]), "reduce_pair_swap_" + stage)
                self.recv1.update(zip(late, got))
            return _chip_exchange([_pair_add(self.g4[n], self.recv1[n], self.sel, F32 if n == "small" else BF16,
                                             "reduce_pair_add_" + n) for n in names])
        return None

    def done(self, stage, results, w):
        if stage in _GATHER_STAGES:
            for n, a in zip(_GATHER_STAGES[stage], results):
                w[n] = self._weight(n, a)
        elif stage in _SWAP_STAGES:
            self.recv1.update(zip(_SWAP_STAGES[stage], results))
        elif stage in _REDUCE_STAGES:
            self.recv2.update(zip(_REDUCE_STAGES[stage], results))


def kernel(x, mem, a_w_in, a_sgu_ln_g, a_sgu_ln_b, a_w_s, a_b_s, kv_w, kv_b_f, b_w_q, mem_w_kv, w_o, ln_g, ln_b, w_up, w_down, loss_target, m_a_w_in, m_a_sgu_ln_g, m_a_sgu_ln_b, m_a_w_s, m_a_b_s, m_kv_w, m_kv_b_f, m_b_w_q, m_mem_w_kv, m_w_o, m_ln_g, m_ln_b, m_w_up, m_w_down, v_a_w_in, v_a_sgu_ln_g, v_a_sgu_ln_b, v_a_w_s, v_a_b_s, v_kv_w, v_kv_b_f, v_b_w_q, v_mem_w_kv, v_w_o, v_ln_g, v_ln_b, v_w_up, v_w_down):
    xi, yi, ci = lax.axis_index("x"), lax.axis_index("y"), lax.axis_index("c")
    chip = 2 * xi + yi
    sel = jnp.stack([chip, ci]).astype(jnp.int32)
    S, D = x.shape[1], x.shape[2]
    DQ = mem_w_kv.shape[2] // 2
    DM = D - DQ
    NG = DM // HEAD_DIM

    shards = {"wa": (a_w_in, 0), "wkv": (kv_w[None], 0), "wq": (b_w_q, 0), "mkv0": (mem_w_kv, 0), "mkv1": (mem_w_kv, 1),
              "wo0": (w_o, 0), "wo1": (w_o, 1), "wup0": (w_up, 0), "wup1": (w_up, 1), "wdown0": (w_down, 0),
              "wdown1": (w_down, 1)}
    placed = {n: _cast_place(s.reshape(s.shape[0], 2, s.shape[1] // 2, s.shape[2]), layer, sel, "cast_place_" + n)
              for n, (s, layer) in shards.items()}
    ln_pack = jnp.concatenate([ln_g.reshape(4, -1), ln_b.reshape(4, -1)], axis=0)
    ln_buf = lax.dynamic_update_slice(jnp.zeros((N_CHIPS,) + ln_pack.shape, F32), ln_pack[None], (chip, 0, 0))
    ln_shape = (4, D)
    small_shapes = [a_w_s.shape, a_b_s.shape, a_sgu_ln_g.shape, a_sgu_ln_b.shape, kv_b_f.shape, ln_shape, ln_shape,
                    (1, LANES)]
    n_small = -(-_rows_of(small_shapes) // 64) * 64
    hooks = _Staged(placed, sel, DM, NG, {"wa": a_w_in.shape[2], "wkv": kv_w.shape[1]}, n_small)
    weights = {"sg": a_sgu_ln_g, "sb": a_sgu_ln_b, "ws": a_w_s[0], "b_s": a_b_s[0], "bf": kv_b_f}
    got = _run_comm(_gather_weights([placed[n] for n in _GATHER_STAGES["start"]], whole=[ln_buf]),
                    "gather_weights_first")
    hooks.done("start", got[:-1], weights)
    ln_full = got[-1].transpose(1, 0, 2).reshape(8, D)
    weights["ln_g"], weights["ln_b"] = ln_full[:4], ln_full[4:]

    _, grad_x, gr = _local_step(x[0], mem[0], loss_target[0], weights, hooks)

    hooks.done("end", _run_comm(hooks.plan("end", weights, gr), "reduce_chip_exchange_end"), weights)
    groups = [["wa"], ["wkv"], ["wq"], ["mkv0", "mkv1"], ["wo0", "wo1"], ["wup0", "wup1"], ["wdown0", "wdown1"]]
    mine = []
    for grp in groups:
        buf = None
        for layer, n in enumerate(grp):
            buf = _chip_reduce(hooks.g4[n], hooks.recv1[n], hooks.recv2[n], sel, buf, layer, len(grp),
                               "reduce_chip_sum_" + n)
        mine.append(buf)
    small_mine = _chip_reduce(hooks.g4["small"], hooks.recv1["small"], hooks.recv2["small"], sel, None, 0, 1,
                              "reduce_chip_sum_small", by_chip=True)
    red, small_all = _pair_gather(mine, small_mine, "reduce_pair_gather")
    red = [a.reshape(a.shape[0], 2 * a.shape[3], a.shape[4]) for a in red]
    g_a_w_in, g_kv_w, g_b_w_q, g_mem_w_kv, g_w_o, g_w_up, g_w_down = red
    g_kv_w = g_kv_w[0]
    g_w_down = g_w_down.transpose(0, 2, 1)
    small_all = small_all.reshape(n_small, LANES)
    g_ws, g_bs, g_sg, g_sb, g_bf, g_lng, g_lnb, loss_row = _unpack_rows(small_all, small_shapes)
    loss = loss_row[0, 0]
    dsh = D // N_CHIPS
    g_ln_g = lax.dynamic_slice_in_dim(g_lng, chip * dsh, dsh, axis=1).reshape(ln_g.shape)
    g_ln_b = lax.dynamic_slice_in_dim(g_lnb, chip * dsh, dsh, axis=1).reshape(ln_b.shape)

    def update(name, g, w, m, v):
        flat = lambda a: a.reshape(-1, a.shape[-1])
        d, mn, vn = _adamw(flat(g), flat(w), flat(m), flat(v), "adamw_" + name)
        return d.reshape(w.shape), mn.reshape(w.shape), vn.reshape(w.shape)

    upd = {
        "a_w_in": update("a_w_in", g_a_w_in, a_w_in, m_a_w_in, v_a_w_in),
        "kv_w": update("kv_w", g_kv_w, kv_w, m_kv_w, v_kv_w),
        "b_w_q": update("b_w_q", g_b_w_q, b_w_q, m_b_w_q, v_b_w_q),
        "mem_w_kv": update("mem_w_kv", g_mem_w_kv, mem_w_kv, m_mem_w_kv, v_mem_w_kv),
        "w_o": update("w_o", g_w_o, w_o, m_w_o, v_w_o),
        "w_up": update("w_up", g_w_up, w_up, m_w_up, v_w_up),
        "w_down": update("w_down", g_w_down, w_down, m_w_down, v_w_down),
    }
    tiny_g = [g_sg, g_sb, g_ws, g_bs, g_bf, g_ln_g, g_ln_b]
    tiny_w = [a_sgu_ln_g, a_sgu_ln_b, a_w_s, a_b_s, kv_b_f, ln_g, ln_b]
    tiny_m = [m_a_sgu_ln_g, m_a_sgu_ln_b, m_a_w_s, m_a_b_s, m_kv_b_f, m_ln_g, m_ln_b]
    tiny_v = [v_a_sgu_ln_g, v_a_sgu_ln_b, v_a_w_s, v_a_b_s, v_kv_b_f, v_ln_g, v_ln_b]
    tiny_shapes = [a.shape for a in tiny_w]
    n_tiny = -(-_rows_of(tiny_shapes) // 8) * 8
    td, tm_, tv = _adamw(_pack_rows(tiny_g, n_tiny), _pack_rows(tiny_w, n_tiny), _pack_rows(tiny_m, n_tiny),
                         _pack_rows(tiny_v, n_tiny), "adamw_small")
    for name, d, mn, vn in zip(["a_sgu_ln_g", "a_sgu_ln_b", "a_w_s", "a_b_s", "kv_b_f", "ln_g", "ln_b"],
                               _unpack_rows(td, tiny_shapes), _unpack_rows(tm_, tiny_shapes),
                               _unpack_rows(tv, tiny_shapes)):
        upd[name] = (d, mn, vn)

    order = ["a_w_in", "a_sgu_ln_g", "a_sgu_ln_b", "a_w_s", "a_b_s", "kv_w", "kv_b_f", "b_w_q", "mem_w_kv", "w_o",
             "ln_g", "ln_b", "w_up", "w_down"]
    grads = {"a_w_in": g_a_w_in.reshape(a_w_in.shape), "a_sgu_ln_g": g_sg.reshape(a_sgu_ln_g.shape),
             "a_sgu_ln_b": g_sb.reshape(a_sgu_ln_b.shape), "a_w_s": g_ws.reshape(a_w_s.shape),
             "a_b_s": g_bs.reshape(a_b_s.shape), "kv_w": g_kv_w, "kv_b_f": g_bf, "b_w_q": g_b_w_q.reshape(b_w_q.shape),
             "mem_w_kv": g_mem_w_kv, "w_o": g_w_o, "ln_g": g_ln_g, "ln_b": g_ln_b, "w_up": g_w_up, "w_down": g_w_down}
    return (loss, grad_x[None], *[grads[n] for n in order], *[upd[n][0] for n in order],
            *[upd[n][1] for n in order], *[upd[n][2] for n in order])
```

```python
import functools
import math

import jax
import jax.numpy as jnp
from jax import lax
from jax.experimental import pallas as pl
from jax.experimental.pallas import tpu as pltpu

F32 = jnp.float32
BF16 = jnp.bfloat16

HEAD_DIM = 64
PAIR = 2 * HEAD_DIM
CHUNK = 128
LN_EPS = 1e-5
ALPHA = 4 ** 0.25
QK_SCALE = 1.0 / math.sqrt(HEAD_DIM)
NEG = -1e30
N_CHIPS = 4
FOX_BLOCK = 256
LANES = 128
VMEM_LIMIT = 48 * 2 ** 20

ADAM_LR, ADAM_B1, ADAM_B2, ADAM_EPS, ADAM_WD, ADAM_STEP = 0.001, 0.9, 0.999, 1e-08, 0.01, 10

MESH = pl.DeviceIdType.MESH
ANY = pl.BlockSpec(memory_space=pl.ANY)


def _dot(a, b):
    return jnp.dot(a, b, preferred_element_type=F32)


def _dot_nt(a, b):
    return lax.dot_general(a, b, (((1,), (1,)), ((), ())), preferred_element_type=F32)


def _dot_tn(a, b):
    return lax.dot_general(a, b, (((0,), (0,)), ((), ())), preferred_element_type=F32)


def _ln_stats(r):
    mu = jnp.mean(r, axis=-1, keepdims=True)
    d = r - mu
    var = jnp.mean(d * d, axis=-1, keepdims=True)
    rstd = lax.rsqrt(var + LN_EPS)
    return d * rstd, rstd


def _ln_bwd(dy, xhat, rstd, g):
    dxh = dy * g
    m1 = jnp.mean(dxh, axis=-1, keepdims=True)
    m2 = jnp.mean(dxh * xhat, axis=-1, keepdims=True)
    return rstd * (dxh - m1 - xhat * m2)


_GELU_K = math.sqrt(2.0 / math.pi)


def _gelu(x):
    return 0.5 * x * (1.0 + jnp.tanh(_GELU_K * (x + 0.044715 * x * x * x)))


def _gelu_grad(x):
    t = jnp.tanh(_GELU_K * (x + 0.044715 * x * x * x))
    return 0.5 * (1.0 + t) + 0.5 * x * (1.0 - t * t) * _GELU_K * (1.0 + 3 * 0.044715 * x * x)


def _lane_mask(e):
    lane = lax.broadcasted_iota(jnp.int32, (1, PAIR), 1)
    return (lane >= HEAD_DIM * e) & (lane < HEAD_DIM * (e + 1))


def _first_head():
    return lax.broadcasted_iota(jnp.int32, (1, PAIR), 1) < HEAD_DIM


def _mem_probs(qh, mkp):
    sc = _dot_nt(qh, mkp) * QK_SCALE
    ex = jnp.exp(sc - jnp.max(sc, axis=-1, keepdims=True))
    return ex / jnp.sum(ex, axis=-1, keepdims=True)


def _mem_attn_fwd(qm_b, mkv_ref, dq_dim):
    outs = []
    for mp in range(dq_dim // PAIR):
        qp = qm_b[:, mp * PAIR:(mp + 1) * PAIR]
        mkp = mkv_ref[:, mp * PAIR:(mp + 1) * PAIR]
        mvp = mkv_ref[:, dq_dim + mp * PAIR:dq_dim + (mp + 1) * PAIR]
        heads = []
        for e in (0, 1):
            qh = jnp.where(_lane_mask(e), qp, jnp.zeros_like(qp))
            p = _mem_probs(qh, mkp)
            heads.append(_dot(p.astype(BF16), mvp))
        outs.append(jnp.where(_first_head(), heads[0], heads[1]))
    return outs


def _mem_attn_bwd(qm_b, mkv_ref, dmo_pairs, dmkv_ref, dq_dim):
    dqs = []
    for mp in range(dq_dim // PAIR):
        ks = slice(mp * PAIR, (mp + 1) * PAIR)
        vs = slice(dq_dim + mp * PAIR, dq_dim + (mp + 1) * PAIR)
        qp = qm_b[:, ks]
        mkp = mkv_ref[:, ks]
        mvp = mkv_ref[:, vs]
        dmo_b = dmo_pairs[mp].astype(BF16)
        dq = None
        dmk = None
        dmv = None
        for e in (0, 1):
            hm = _lane_mask(e)
            qh = jnp.where(hm, qp, jnp.zeros_like(qp))
            p = _mem_probs(qh, mkp)
            doh = jnp.where(hm, dmo_b, jnp.zeros_like(dmo_b))
            dp = _dot_nt(doh, mvp)
            ds = p * (dp - jnp.sum(dp * p, axis=-1, keepdims=True))
            dsb = (ds * QK_SCALE).astype(BF16)
            kh = jnp.where(hm, mkp, jnp.zeros_like(mkp))
            dq_e = _dot(dsb, kh)
            dmk_e = _dot_tn(dsb, qh)
            dmv_e = _dot_tn(p.astype(BF16), doh)
            dq = dq_e if dq is None else dq + dq_e
            dmk = dmk_e if dmk is None else dmk + dmk_e
            dmv = dmv_e if dmv is None else dmv + dmv_e
        dmkv_ref[:, ks] += dmk
        dmkv_ref[:, vs] += dmv
        dqs.append(dq)
    return dqs


def _params(n_axes=1):
    return pltpu.CompilerParams(dimension_semantics=("arbitrary",) * n_axes, vmem_limit_bytes=VMEM_LIMIT)


def _full(shape):
    return pl.BlockSpec(shape, lambda *_: (0,) * len(shape))


def _rows(tm, cols):
    return pl.BlockSpec((tm, cols), lambda i: (i, 0))


def _cols(rows, tm):
    return pl.BlockSpec((rows, tm), lambda i: (0, i))


def _pick(n, pref):
    if n <= pref:
        return n
    best = LANES
    for t in range(LANES, pref + 1, LANES):
        if n % t == 0:
            best = t
    return best if 2 * best >= pref or n > 2 * pref else n


class _Comm:
    def __init__(self, ins, out_shapes, aliases, sem_counts, start, finish):
        self.ins, self.out_shapes, self.aliases, self.sem_counts = list(ins), list(out_shapes), dict(aliases), sem_counts
        self.start, self.finish = start, finish


def _pcall(body, *, name, grid, in_specs, out_specs, out_shape, args, scratch_shapes=(), comm=None, aliases=None):
    n_in, n_out, n_scr = len(in_specs), len(out_shape), len(scratch_shapes)
    aliases = dict(aliases or {})
    if comm is None:
        res = pl.pallas_call(body, name=name, grid=grid, out_shape=tuple(out_shape), in_specs=list(in_specs),
                             out_specs=tuple(out_specs), scratch_shapes=list(scratch_shapes),
                             input_output_aliases=aliases, compiler_params=_params())(*args)
        return tuple(res), ()
    nci, nco = len(comm.ins), len(comm.out_shapes)
    last = grid[0] - 1

    def wrapped(*refs):
        ins, refs = refs[:n_in], refs[n_in:]
        cins, refs = refs[:nci], refs[nci:]
        outs, refs = refs[:n_out], refs[n_out:]
        couts, refs = refs[:nco], refs[nco:]
        scr, sems = refs[:n_scr], refs[n_scr:]

        @pl.when(pl.program_id(0) == 0)
        def _():
            comm.start(cins, couts, sems)

        body(*ins, *outs, *scr)

        @pl.when(pl.program_id(0) == last)
        def _():
            comm.finish(cins, couts, sems)

    res = pl.pallas_call(
        wrapped, name=name, grid=grid, out_shape=tuple(out_shape) + tuple(comm.out_shapes),
        in_specs=list(in_specs) + [ANY] * nci, out_specs=tuple(out_specs) + (ANY,) * nco,
        input_output_aliases={**aliases, **{n_in + a: n_out + b for a, b in comm.aliases.items()}},
        scratch_shapes=list(scratch_shapes) + [pltpu.SemaphoreType.DMA((k,)) for k in comm.sem_counts],
        compiler_params=pltpu.CompilerParams(dimension_semantics=("arbitrary",), vmem_limit_bytes=VMEM_LIMIT,
                                             has_side_effects=True),
    )(*args, *comm.ins)
    return tuple(res[:n_out]), tuple(res[n_out:])


def _run_comm(comm, name):
    nci, nco = len(comm.ins), len(comm.out_shapes)

    def body(*refs):
        cins, couts, sems = refs[:nci], refs[nci:nci + nco], refs[nci + nco:]
        comm.start(cins, couts, sems)
        comm.finish(cins, couts, sems)

    return pl.pallas_call(
        body, name=name, out_shape=tuple(comm.out_shapes), in_specs=[ANY] * nci, out_specs=(ANY,) * nco,
        input_output_aliases=comm.aliases, scratch_shapes=[pltpu.SemaphoreType.DMA((k,)) for k in comm.sem_counts],
        compiler_params=pltpu.CompilerParams(has_side_effects=True),
    )(*comm.ins)


def _mm(a, b, name, a_is_transposed, out_split=1):
    (M, R) = a.shape if a_is_transposed else a.shape[::-1]
    _, N = b.shape
    bm, br = _pick(M, 1024 if a_is_transposed else 512), _pick(R, 1024 if a_is_transposed else 512)
    ncol = N // out_split
    bn = _pick(ncol, 1024)
    per = ncol // bn
    last = R // br - 1

    def body(a_ref, b_ref, o_ref, acc_ref):
        @pl.when(pl.program_id(2) == 0)
        def _():
            acc_ref[...] = jnp.zeros_like(acc_ref)

        acc_ref[...] += (_dot if a_is_transposed else _dot_tn)(a_ref[...], b_ref[...])

        @pl.when(pl.program_id(2) == last)
        def _():
            o_ref[...] = acc_ref[...].astype(BF16)

    if out_split == 1:
        out_shape = jax.ShapeDtypeStruct((M, N), BF16)
        out_spec = pl.BlockSpec((bm, bn), lambda i, j, r: (i, j))
    else:
        out_shape = jax.ShapeDtypeStruct((out_split, M, ncol), BF16)
        out_spec = pl.BlockSpec((None, bm, bn), lambda i, j, r: (j // per, i, j % per))
    a_spec = (pl.BlockSpec((bm, br), lambda i, j, r: (i, r)) if a_is_transposed
              else pl.BlockSpec((br, bm), lambda i, j, r: (r, i)))
    return pl.pallas_call(
        body, name=name, out_shape=out_shape, grid=(M // bm, N // bn, R // br),
        in_specs=[a_spec, pl.BlockSpec((br, bn), lambda i, j, r: (r, j))], out_specs=out_spec,
        scratch_shapes=[pltpu.VMEM((bm, bn), F32)],
        compiler_params=pltpu.CompilerParams(dimension_semantics=("parallel", "parallel", "arbitrary"),
                                             vmem_limit_bytes=VMEM_LIMIT),
    )(a, b)


def _mm_tn(a, b, name, out_split=1):
    return _mm(a, b, name, False, out_split)


def _mm_nn(at, b, name, out_split=1):
    return _mm(at, b, name, True, out_split)


def _mm_small(a, b, name):
    def body(a_ref, b_ref, o_ref):
        o_ref[...] = _dot(a_ref[...], b_ref[...]).astype(BF16)

    return pl.pallas_call(body, name=name, out_shape=jax.ShapeDtypeStruct((a.shape[0], b.shape[1]), BF16),
                          compiler_params=pltpu.CompilerParams(vmem_limit_bytes=VMEM_LIMIT))(a, b)


def _mlp_out(x_ref, wup_ref, wdown_ref, g_ref, b_ref, r_ref, h_ref):
    NC, _, FC = wup_ref.shape
    xv = x_ref[...]
    xb = xv.astype(BF16)
    y = jnp.zeros(xv.shape, F32)
    for c in range(NC):
        a = jnp.maximum(_dot(xb, wup_ref[c]), 0.0)
        hb = (a * a).astype(BF16)
        h_ref[:, c * FC:(c + 1) * FC] = hb
        y = y + _dot(hb, wdown_ref[c])
    r = ALPHA * xv + y
    r_ref[...] = r
    xhat, _ = _ln_stats(r)
    return xhat * g_ref[...] + b_ref[...]


def _mlp_fwd_loss(x, wup4, wdown4, g, b, target, name):
    S, D = x.shape
    NC, _, FC = wup4.shape
    tm = min(256, S)

    def body(x_ref, wup_ref, wdown_ref, g_ref, b_ref, t_ref, d_ref, r_ref, h_ref, l_ref):
        @pl.when(pl.program_id(0) == 0)
        def _():
            l_ref[...] = jnp.zeros_like(l_ref)

        err = _mlp_out(x_ref, wup_ref, wdown_ref, g_ref, b_ref, r_ref, h_ref) - t_ref[...]
        d_ref[...] = err * (1.0 / D)
        l_ref[...] += (0.5 / D) * jnp.sum(err * err)

    return pl.pallas_call(
        body, name=name, grid=(S // tm,),
        out_shape=(jax.ShapeDtypeStruct((S, D), F32), jax.ShapeDtypeStruct((S, D), F32),
                   jax.ShapeDtypeStruct((S, NC * FC), BF16), jax.ShapeDtypeStruct((8, LANES), F32)),
        in_specs=[_rows(tm, D), _full(wup4.shape), _full(wdown4.shape), _full((1, D)), _full((1, D)), _rows(tm, D)],
        out_specs=(_rows(tm, D), _rows(tm, D), _rows(tm, NC * FC), _full((8, LANES))),
        compiler_params=_params(),
    )(x, wup4, wdown4, g, b, target)


def _mlp_fwd(x, wup4, wdown4, g, b, name, comm=None):
    S, D = x.shape
    NC, _, FC = wup4.shape
    tm = min(256, S)

    def body(x_ref, wup_ref, wdown_ref, g_ref, b_ref, xo_ref, xb_ref, xbt_ref, r_ref, h_ref):
        xo = _mlp_out(x_ref, wup_ref, wdown_ref, g_ref, b_ref, r_ref, h_ref)
        xo_ref[...] = xo
        xb_ref[...] = xo.astype(BF16)
        xbt_ref[...] = xo.T.astype(BF16)

    return _pcall(
        body, name=name, grid=(S // tm,), comm=comm,
        out_shape=(jax.ShapeDtypeStruct((S, D), F32), jax.ShapeDtypeStruct((S, D), BF16),
                   jax.ShapeDtypeStruct((D, S), BF16), jax.ShapeDtypeStruct((S, D), F32),
                   jax.ShapeDtypeStruct((S, NC * FC), BF16)),
        in_specs=[_rows(tm, D), _full(wup4.shape), _full(wdown4.shape), _full((1, D)), _full((1, D))],
        out_specs=(_rows(tm, D), _rows(tm, D), _cols(D, tm), _rows(tm, D), _rows(tm, NC * FC)),
        args=(x, wup4, wdown4, g, b))


def _mlp_bwd(dxo, r, h, wup4, wdown4, g, name, comm=None):
    S, D = r.shape
    NC, _, FC = wup4.shape
    tm = min(256, S)

    def body(dxo_ref, r_ref, h_ref, wup_ref, wdown_ref, g_ref, dx_ref, drbt_ref, dpre_ref, dgb_ref):
        @pl.when(pl.program_id(0) == 0)
        def _():
            dgb_ref[...] = jnp.zeros_like(dgb_ref)

        xhat, rstd = _ln_stats(r_ref[...])
        dy = dxo_ref[...]
        dgb_ref[0:1, :] += jnp.sum(dy * xhat, axis=0, keepdims=True)
        dgb_ref[1:2, :] += jnp.sum(dy, axis=0, keepdims=True)
        dr = _ln_bwd(dy, xhat, rstd, g_ref[...])
        drb = dr.astype(BF16)
        drbt_ref[...] = dr.T.astype(BF16)
        dx = ALPHA * dr
        for c in range(NC):
            dh = _dot_nt(drb, wdown_ref[c])
            a = jnp.sqrt(h_ref[:, c * FC:(c + 1) * FC].astype(F32))
            dpre = (2.0 * a * dh).astype(BF16)
            dpre_ref[:, c * FC:(c + 1) * FC] = dpre
            dx = dx + _dot_nt(dpre, wup_ref[c])
        dx_ref[...] = dx

    return _pcall(
        body, name=name, grid=(S // tm,), comm=comm,
        out_shape=(jax.ShapeDtypeStruct((S, D), F32), jax.ShapeDtypeStruct((D, S), BF16),
                   jax.ShapeDtypeStruct((S, NC * FC), BF16), jax.ShapeDtypeStruct((2, D), F32)),
        in_specs=[_rows(tm, D), _rows(tm, D), _rows(tm, NC * FC), _full(wup4.shape), _full(wdown4.shape),
                  _full((1, D))],
        out_specs=(_rows(tm, D), _cols(D, tm), _rows(tm, NC * FC), _full((2, D))),
        args=(dxo, r, h, wup4, wdown4, g))


def _tail_fwd_store(xv, cat_ref, wo_ref, g_ref, b_ref, xo_ref, xbt_ref, r_ref, catt_ref):
    cat = cat_ref[...]
    y = _dot(cat, wo_ref[...])
    r = ALPHA * xv + y
    xhat, _ = _ln_stats(r)
    xo = xhat * g_ref[...] + b_ref[...]
    xo_ref[...] = xo
    xbt_ref[...] = xo.T.astype(BF16)
    catt_ref[...] = cat.astype(F32).T.astype(BF16)
    r_ref[...] = r


def _tail_bwd_head(dxo_ref, r_ref, g_ref, wo_ref, dgb_ref, drb_ref):
    xhat, rstd = _ln_stats(r_ref[...])
    dy = dxo_ref[...]
    dgb_ref[0:1, :] += jnp.sum(dy * xhat, axis=0, keepdims=True)
    dgb_ref[1:2, :] += jnp.sum(dy, axis=0, keepdims=True)
    dr = _ln_bwd(dy, xhat, rstd, g_ref[...])
    drb = dr.astype(BF16)
    drb_ref[...] = drb
    return dr, _dot_nt(drb, wo_ref[...])


def _causal(shape):
    return lax.broadcasted_iota(jnp.int32, shape, 1) <= lax.broadcasted_iota(jnp.int32, shape, 0)


def _sgu_mixed(ws_ref, bst_ref, gp, vpair):
    tril = _causal((CHUNK, CHUNK))
    w0 = jnp.where(tril, ws_ref[2 * gp], 0.0).astype(BF16)
    w1 = jnp.where(tril, ws_ref[2 * gp + 1], 0.0).astype(BF16)
    m0 = _dot(w0, vpair) + bst_ref[:, 2 * gp:2 * gp + 1]
    m1 = _dot(w1, vpair) + bst_ref[:, 2 * gp + 1:2 * gp + 2]
    return jnp.where(_first_head(), m0, m1), w0, w1


def _mixer_a_fwd(x, wa, sg, sb, ws, bst, mkv, wo, g, b, name, comm=None):
    S, D = x.shape
    DQ = mkv.shape[1] // 2
    DM = D - DQ
    NP = DM // PAIR
    tm = min(512, S)

    def body(x_ref, wa_ref, sg_ref, sb_ref, ws_ref, bst_ref, mkv_ref, wo_ref, g_ref, b_ref,
             xo_ref, xbt_ref, r_ref, catt_ref, xint_ref, z_s, zv_s, cat_ref):
        xv = x_ref[...]
        xint_ref[...] = xv.T.astype(BF16)
        z_s[...] = _dot(xv.astype(BF16), wa_ref[...])
        vhat, _ = _ln_stats(_gelu(z_s[:, DM:2 * DM]))
        zv_s[...] = (vhat * sg_ref[...] + sb_ref[...]).astype(BF16)
        for c in range(tm // CHUNK):
            rs = slice(c * CHUNK, (c + 1) * CHUNK)
            for gp in range(NP):
                ls = slice(gp * PAIR, (gp + 1) * PAIR)
                mixed, _, _ = _sgu_mixed(ws_ref, bst_ref, gp, zv_s[rs, ls])
                cat_ref[rs, ls] = (_gelu(z_s[rs, ls]) * mixed).astype(BF16)
        mo = _mem_attn_fwd(z_s[:, 2 * DM:].astype(BF16), mkv_ref, DQ)
        for mp in range(DQ // PAIR):
            cat_ref[:, DM + mp * PAIR:DM + (mp + 1) * PAIR] = mo[mp].astype(BF16)
        _tail_fwd_store(xv, cat_ref, wo_ref, g_ref, b_ref, xo_ref, xbt_ref, r_ref, catt_ref)

    return _pcall(
        body, name=name, grid=(S // tm,), comm=comm,
        out_shape=(jax.ShapeDtypeStruct((S, D), F32), jax.ShapeDtypeStruct((D, S), BF16),
                   jax.ShapeDtypeStruct((S, D), F32), jax.ShapeDtypeStruct((D, S), BF16),
                   jax.ShapeDtypeStruct((D, S), BF16)),
        in_specs=[_rows(tm, D), _full(wa.shape), _full(sg.shape), _full(sb.shape), _full(ws.shape), _full(bst.shape),
                  _full(mkv.shape), _full(wo.shape), _full((1, D)), _full((1, D))],
        out_specs=(_rows(tm, D), _cols(D, tm), _rows(tm, D), _cols(D, tm), _cols(D, tm)),
        scratch_shapes=[pltpu.VMEM((tm, 2 * DM + DQ), F32), pltpu.VMEM((tm, DM), BF16), pltpu.VMEM((tm, D), BF16)],
        args=(x, wa, sg, sb, ws, bst, mkv, wo, g, b))


def _mixer_a_bwd(x, dxo, r, wa, sg, sb, ws, bst, mkv, wo, g, name, comm=None):
    S, D = x.shape
    DQ = mkv.shape[1] // 2
    DM = D - DQ
    NP = DM // PAIR
    NG = DM // HEAD_DIM
    tm = min(256, S)

    def body(x_ref, dxo_ref, r_ref, wa_ref, sg_ref, sb_ref, ws_ref, bst_ref, mkv_ref, wo_ref, g_ref,
             dx_ref, dz_ref, drb_ref, dgb_ref, dsgb_ref, dws_ref, dbst_ref, dmkv_ref, z_s, zv_s, dzv_s):
        @pl.when(pl.program_id(0) == 0)
        def _():
            dgb_ref[...] = jnp.zeros_like(dgb_ref)
            dsgb_ref[...] = jnp.zeros_like(dsgb_ref)
            dws_ref[...] = jnp.zeros_like(dws_ref)
            dbst_ref[...] = jnp.zeros_like(dbst_ref)
            dmkv_ref[...] = jnp.zeros_like(dmkv_ref)

        dr, dcat = _tail_bwd_head(dxo_ref, r_ref, g_ref, wo_ref, dgb_ref, drb_ref)
        z_s[...] = _dot(x_ref[...].astype(BF16), wa_ref[...])
        vhat, vrstd = _ln_stats(_gelu(z_s[:, DM:2 * DM]))
        zv_s[...] = (vhat * sg_ref[...] + sb_ref[...]).astype(BF16)
        tril = _causal((CHUNK, CHUNK))
        lane = lax.broadcasted_iota(jnp.int32, (1, LANES), 1)
        for c in range(tm // CHUNK):
            rs = slice(c * CHUNK, (c + 1) * CHUNK)
            for gp in range(NP):
                ls = slice(gp * PAIR, (gp + 1) * PAIR)
                vpair = zv_s[rs, ls]
                mixed, w0, w1 = _sgu_mixed(ws_ref, bst_ref, gp, vpair)
                u_pre = z_s[rs, ls]
                dmix = dcat[rs, ls]
                dz_ref[rs, ls] = (dmix * mixed * _gelu_grad(u_pre)).astype(BF16)
                dmixed = dmix * _gelu(u_pre)
                first = _first_head()
                d0 = jnp.where(first, dmixed, 0.0)
                d1 = jnp.where(first, 0.0, dmixed)
                d0b = d0.astype(BF16)
                d1b = d1.astype(BF16)
                dzv_s[rs, ls] = _dot_tn(w0, d0b) + _dot_tn(w1, d1b)
                dws_ref[2 * gp] += jnp.where(tril, _dot_nt(d0b, vpair), 0.0)
                dws_ref[2 * gp + 1] += jnp.where(tril, _dot_nt(d1b, vpair), 0.0)
                dbst_ref[...] += (jnp.where(lane == 2 * gp, jnp.sum(d0, axis=-1, keepdims=True), 0.0)
                                  + jnp.where(lane == 2 * gp + 1, jnp.sum(d1, axis=-1, keepdims=True), 0.0))
        dzv = dzv_s[...]
        dsgb_ref[0:1, :] += jnp.sum(dzv * vhat, axis=0, keepdims=True)
        dsgb_ref[1:2, :] += jnp.sum(dzv, axis=0, keepdims=True)
        dgv = _ln_bwd(dzv, vhat, vrstd, sg_ref[...])
        dz_ref[:, DM:2 * DM] = (dgv * _gelu_grad(z_s[:, DM:2 * DM])).astype(BF16)
        dmo = [dcat[:, DM + mp * PAIR:DM + (mp + 1) * PAIR] for mp in range(DQ // PAIR)]
        dqm = _mem_attn_bwd(z_s[:, 2 * DM:].astype(BF16), mkv_ref, dmo, dmkv_ref, DQ)
        for mp in range(DQ // PAIR):
            dz_ref[:, 2 * DM + mp * PAIR:2 * DM + (mp + 1) * PAIR] = dqm[mp].astype(BF16)
        dx_ref[...] = ALPHA * dr + _dot_nt(dz_ref[...], wa_ref[...])

    ZW = 2 * DM + DQ
    M = mkv.shape[0]
    return _pcall(
        body, name=name, grid=(S // tm,), comm=comm,
        out_shape=(jax.ShapeDtypeStruct((S, D), F32), jax.ShapeDtypeStruct((S, ZW), BF16),
                   jax.ShapeDtypeStruct((S, D), BF16), jax.ShapeDtypeStruct((2, D), F32),
                   jax.ShapeDtypeStruct((2, DM), F32), jax.ShapeDtypeStruct((NG, CHUNK, CHUNK), F32),
                   jax.ShapeDtypeStruct((CHUNK, LANES), F32), jax.ShapeDtypeStruct((M, 2 * DQ), F32)),
        in_specs=[_rows(tm, D), _rows(tm, D), _rows(tm, D), _full(wa.shape), _full(sg.shape), _full(sb.shape),
                  _full(ws.shape), _full(bst.shape), _full(mkv.shape), _full(wo.shape), _full((1, D))],
        out_specs=(_rows(tm, D), _rows(tm, ZW), _rows(tm, D), _full((2, D)), _full((2, DM)),
                   _full((NG, CHUNK, CHUNK)), _full((CHUNK, LANES)), _full((M, 2 * DQ))),
        scratch_shapes=[pltpu.VMEM((tm, ZW), F32), pltpu.VMEM((tm, DM), BF16), pltpu.VMEM((tm, DM), F32)],
        args=(x, dxo, r, wa, sg, sb, ws, bst, mkv, wo, g))


def _proj_b_fwd(xb, wq, wkv, dm, name):
    S, D = xb.shape
    DQ = D - dm
    tm = min(512, S)

    def body(x_ref, wq_ref, wkv_ref, q_ref, qm_ref, k_ref, v_ref, fl_ref):
        xv = x_ref[...]
        z = _dot(xv, wq_ref[...])
        q_ref[...] = z[:, :dm].astype(BF16)
        qm_ref[...] = z[:, dm:].astype(BF16)
        kvf = _dot(xv, wkv_ref[...])
        k_ref[...] = kvf[:, :dm].astype(BF16)
        v_ref[...] = kvf[:, dm:2 * dm].astype(BF16)
        fl_ref[...] = kvf[:, 2 * dm:]

    return pl.pallas_call(
        body, name=name, grid=(S // tm,),
        out_shape=(jax.ShapeDtypeStruct((S, dm), BF16), jax.ShapeDtypeStruct((S, DQ), BF16),
                   jax.ShapeDtypeStruct((S, dm), BF16), jax.ShapeDtypeStruct((S, dm), BF16),
                   jax.ShapeDtypeStruct((S, LANES), F32)),
        in_specs=[_rows(tm, D), _full(wq.shape), _full(wkv.shape)],
        out_specs=(_rows(tm, dm), _rows(tm, DQ), _rows(tm, dm), _rows(tm, dm), _rows(tm, LANES)),
        compiler_params=_params(),
    )(xb, wq, wkv)


def _proj_b_bwd(dr, dqz, dkvf, wq, wkv, name):
    S, D = dr.shape
    tm = min(512, S)

    def body(dr_ref, dqz_ref, dkvf_ref, wq_ref, wkv_ref, dx_ref):
        dx_ref[...] = (ALPHA * dr_ref[...] + _dot_nt(dqz_ref[...], wq_ref[...])
                       + _dot_nt(dkvf_ref[...], wkv_ref[...]))

    return pl.pallas_call(
        body, name=name, grid=(S // tm,), out_shape=jax.ShapeDtypeStruct((S, D), F32),
        in_specs=[_rows(tm, D), _rows(tm, dqz.shape[1]), _rows(tm, dkvf.shape[1]), _full(wq.shape), _full(wkv.shape)],
        out_specs=_rows(tm, D), compiler_params=_params(),
    )(dr, dqz, dkvf, wq, wkv)


def _split3(v):
    hi = v.astype(BF16)
    r1 = v - hi.astype(F32)
    mid = r1.astype(BF16)
    lo = (r1 - mid.astype(F32)).astype(BF16)
    return hi, mid, lo


def _tri_sum(v, tri_b):
    hi, mid, lo = _split3(v)
    return _dot(hi, tri_b) + _dot(mid, tri_b) + _dot(lo, tri_b)


def _log_sigmoid(x):
    return jnp.minimum(x, 0.0) - jnp.log(1.0 + jnp.exp(-jnp.abs(x)))


def _forget_cumsum(flt, bf, name):
    H, S = flt.shape
    nchunk = S // LANES

    def body(fl_ref, bf_ref, p_ref):
        upper = (lax.broadcasted_iota(jnp.int32, (LANES, LANES), 0)
                 <= lax.broadcasted_iota(jnp.int32, (LANES, LANES), 1)).astype(BF16)

        def step(n, carry):
            s0 = pl.multiple_of(n * LANES, LANES)
            lf = _log_sigmoid(fl_ref[:, pl.ds(s0, LANES)] + bf_ref[...])
            cs = _tri_sum(lf, upper) + carry
            for i, piece in enumerate(_split3(-cs)):
                p_ref[i, :, pl.ds(s0, LANES)] = piece.astype(F32)
            return cs[:, LANES - 1:LANES]

        lax.fori_loop(0, nchunk, step, jnp.zeros((H, 1), F32))

    return pl.pallas_call(body, name=name, out_shape=jax.ShapeDtypeStruct((3, H, S), F32),
                          compiler_params=pltpu.CompilerParams(vmem_limit_bytes=VMEM_LIMIT))(flt, bf)


def _forget_cumsum_bwd(dct, flt, bf, name):
    H, S = flt.shape
    nchunk = S // LANES

    def body(dc_ref, fl_ref, bf_ref, dfl_ref, dbf_ref):
        lower = (lax.broadcasted_iota(jnp.int32, (LANES, LANES), 0)
                 >= lax.broadcasted_iota(jnp.int32, (LANES, LANES), 1)).astype(BF16)

        def step(n, carry):
            tail, tot = carry
            s0 = pl.multiple_of((nchunk - 1 - n) * LANES, LANES)
            suffix = _tri_sum(dc_ref[:, pl.ds(s0, LANES)], lower) + tail
            xv = fl_ref[:, pl.ds(s0, LANES)] + bf_ref[...]
            dfl = suffix * (1.0 / (1.0 + jnp.exp(xv)))
            dfl_ref[:, pl.ds(s0, LANES)] = dfl
            return suffix[:, 0:1], tot + jnp.sum(dfl, axis=-1, keepdims=True)

        _, tot = lax.fori_loop(0, nchunk, step, (jnp.zeros((H, 1), F32), jnp.zeros((H, 1), F32)))
        dbf_ref[...] = jnp.broadcast_to(tot, (H, LANES))

    return pl.pallas_call(body, name=name,
                          out_shape=(jax.ShapeDtypeStruct((H, S), F32), jax.ShapeDtypeStruct((H, LANES), F32)),
                          compiler_params=pltpu.CompilerParams(vmem_limit_bytes=VMEM_LIMIT))(dct, flt, bf)


N_BIAS = 3


def _bias_ones():
    lane = lax.broadcasted_iota(jnp.int32, (1, PAIR), 1)
    return ((lane & (HEAD_DIM - 1)) < N_BIAS).astype(BF16)


def _fox_fwd(q, k, v, ca, name, comm=None):
    S, DM = q.shape
    NP = DM // PAIR
    T = min(FOX_BLOCK, S)
    nq = S // T

    def body(q_ref, k_ref, v_ref, ca_ref, o_ref, lse_ref):
        tril = _causal((T, T))
        heads = (_lane_mask(0), _lane_mask(1))
        ones3 = _bias_ones()

        def step(carry, qaug, kaug, vaug, masked):
            m, acc = carry
            s = _dot_nt(qaug, kaug)
            if masked:
                s = jnp.where(tril, s, NEG)
            m_new = jnp.maximum(m, jnp.max(s, axis=-1, keepdims=True))
            p = jnp.exp(s - m_new).astype(BF16)
            return m_new, jnp.exp(m - m_new) * acc + _dot(p, vaug)

        def kv_block(kj):
            s0 = pl.multiple_of(kj * T, T)
            kb, vb, cab = k_ref[pl.ds(s0, T), :], v_ref[pl.ds(s0, T), :], ca_ref[pl.ds(s0, T), :]
            return ([jnp.where(heads[e], kb, cab) for e in (0, 1)],
                    [jnp.where(heads[e], vb, jnp.ones_like(vb)) for e in (0, 1)])

        def q_pair(a, _):
            t0s = [pl.multiple_of((2 * a + r) * T, T) for r in (0, 1)]
            qaug = {}
            for r in (0, 1):
                qb = q_ref[pl.ds(t0s[r], T), :] * QK_SCALE
                for e in (0, 1):
                    qaug[r, e] = jnp.where(heads[e], qb, ones3)
            chains = [(r, e) for r in (0, 1) for e in (0, 1)]

            def full(first_block, n_blocks, carries):
                blocks = [kv_block(first_block + kk) for kk in range(n_blocks)]
                scores = [[_dot_nt(qaug[r, e], kaug[e]) for r, e in chains] for kaug, _ in blocks]
                out = []
                for n, (r, e) in enumerate(chains):
                    m, acc = carries[n]
                    top = functools.reduce(jnp.maximum, [scores[kk][n] for kk in range(n_blocks)])
                    m_new = jnp.maximum(m, jnp.max(top, axis=-1, keepdims=True))
                    acc = jnp.exp(m - m_new) * acc
                    for kk in range(n_blocks):
                        acc = acc + _dot(jnp.exp(scores[kk][n] - m_new).astype(BF16), blocks[kk][1][e])
                    out.append((m_new, acc))
                return tuple(out)

            init = (jnp.full((T, 1), NEG, F32), jnp.zeros((T, PAIR), F32))
            carries = lax.fori_loop(0, a // 2, lambda i, c: full(4 * i, 4, c), (init,) * 4)
            carries = list(lax.fori_loop(0, a % 2, lambda i, c: full(2 * a - 2, 2, c), carries))
            kaug, vaug = kv_block(2 * a)
            for i, (r, e) in enumerate(chains):
                carries[i] = step(carries[i], qaug[r, e], kaug[e], vaug[e], r == 0)
            kaug, vaug = kv_block(2 * a + 1)
            for i, (r, e) in enumerate(chains):
                if r == 1:
                    carries[i] = step(carries[i], qaug[r, e], kaug[e], vaug[e], True)
            first = _first_head()
            for r in (0, 1):
                (m0, acc0), (m1, acc1) = carries[2 * r], carries[2 * r + 1]
                l0, l1 = acc0[:, HEAD_DIM:HEAD_DIM + 1], acc1[:, 0:1]
                o_ref[pl.ds(t0s[r], T), :] = jnp.where(first, acc0 / l0, acc1 / l1).astype(BF16)
                lse_t = jnp.where(first, m0 + jnp.log(l0), m1 + jnp.log(l1)).T
                lse_ref[0:1, pl.ds(t0s[r], T)] = lse_t[0:1, :]
                lse_ref[1:2, pl.ds(t0s[r], T)] = lse_t[HEAD_DIM:HEAD_DIM + 1, :]
            return 0

        lax.fori_loop(0, nq // 2, q_pair, 0)

    col = pl.BlockSpec((S, PAIR), lambda hp: (0, hp))
    return _pcall(
        body, name=name, grid=(NP,), comm=comm,
        out_shape=(jax.ShapeDtypeStruct((S, DM), BF16), jax.ShapeDtypeStruct((NP, 2, S), F32)),
        in_specs=[col, col, col, col], out_specs=(col, pl.BlockSpec((None, 2, S), lambda hp: (hp, 0, 0))),
        args=(q, k, v, ca))


def _fox_bwd(q, k, v, o, do, ca, lse, name, comm=None):
    S, DM = q.shape
    NP = DM // PAIR
    T = min(FOX_BLOCK, S)
    nq = S // T

    def body(q_ref, k_ref, v_ref, o_ref, do_ref, ca_ref, lse_ref, dq_ref, dk_ref, dv_ref, dc_ref,
             dq_acc, dcol_acc, dk_acc, dv_acc, dd_s):
        keep = lax.broadcasted_iota(jnp.int32, (T, T), 0) <= lax.broadcasted_iota(jnp.int32, (T, T), 1)
        lane = lax.broadcasted_iota(jnp.int32, (1, LANES), 1)
        heads = (_lane_mask(0), _lane_mask(1))
        ones3 = _bias_ones()
        dq_acc[...] = jnp.zeros_like(dq_acc)
        dcol_acc[...] = jnp.zeros_like(dcol_acc)
        dc_ref[...] = jnp.zeros_like(dc_ref)

        def row_dots(qi, _):
            t0 = pl.multiple_of(qi * T, T)
            prod = do_ref[pl.ds(t0, T), :].astype(F32) * o_ref[pl.ds(t0, T), :].astype(F32)
            cols = [jnp.sum(jnp.where(heads[e], prod, 0.0), axis=-1, keepdims=True) for e in (0, 1)]
            tile = jnp.where(lane == 0, cols[0], jnp.where(lane == 1, cols[1], 0.0))
            dd_s[:, pl.ds(t0, T)] = tile.T[0:8, :]
            return 0

        lax.fori_loop(0, nq, row_dots, 0)

        def kv_pair(b, _):
            dk_acc[...] = jnp.zeros_like(dk_acc)
            dv_acc[...] = jnp.zeros_like(dv_acc)
            s0s = [pl.multiple_of((2 * b + jj) * T, T) for jj in (0, 1)]
            kaug, ks, vbs = {}, {}, []
            for jj in (0, 1):
                kb = k_ref[pl.ds(s0s[jj], T), :]
                cab = ca_ref[pl.ds(s0s[jj], T), :]
                vbs.append(v_ref[pl.ds(s0s[jj], T), :])
                for e in (0, 1):
                    kaug[jj, e] = jnp.where(heads[e], kb, cab)
                    ks[jj, e] = jnp.where(heads[e], kb, jnp.zeros_like(kb)) * QK_SCALE

            def q_step(qi, blocks):
                t0 = pl.multiple_of(qi * T, T)
                qs = q_ref[pl.ds(t0, T), :] * QK_SCALE
                dob = do_ref[pl.ds(t0, T), :]
                doh = [jnp.where(heads[e], dob, jnp.zeros_like(dob)) for e in (0, 1)]
                qh = [jnp.where(heads[e], qs, jnp.zeros_like(qs)) for e in (0, 1)]
                chains = [(e, jj, masked) for e in (0, 1) for jj, masked in blocks]
                scores = [_dot_nt(kaug[jj, e], jnp.where(heads[e], qs, ones3)) for e, jj, _ in chains]
                dps = [_dot_nt(vbs[jj], doh[e]) for e, jj, _ in chains]
                dq = jnp.zeros((T, PAIR), F32)
                drow = [jnp.zeros((1, T), F32), jnp.zeros((1, T), F32)]
                for n, (e, jj, masked) in enumerate(chains):
                    st = jnp.where(keep, scores[n], NEG) if masked else scores[n]
                    pt = jnp.exp(st - lse_ref[e:e + 1, pl.ds(t0, T)])
                    dv_acc[jj] += _dot(pt.astype(BF16), doh[e])
                    dst = pt * (dps[n] - dd_s[e:e + 1, pl.ds(t0, T)])
                    dsb = dst.astype(BF16)
                    dk_acc[jj] += _dot(dsb, qh[e])
                    dq = dq + _dot_tn(dsb, ks[jj, e])
                    dcol_acc[pl.ds(s0s[jj], T), :] -= jnp.where(lane == e, jnp.sum(dst, axis=-1, keepdims=True), 0.0)
                    drow[e] = drow[e] + jnp.sum(dst, axis=0, keepdims=True)
                for e in (0, 1):
                    dc_ref[e:e + 1, pl.ds(t0, T)] += drow[e]
                dq_acc[pl.ds(t0, T), :] += dq

            q_step(2 * b, [(0, True)])
            q_step(2 * b + 1, [(0, False), (1, True)])

            def rest(i, _):
                q_step(2 * b + 2 + 2 * i, [(0, False), (1, False)])
                q_step(2 * b + 3 + 2 * i, [(0, False), (1, False)])
                return 0

            lax.fori_loop(0, (nq - 2) // 2 - b, rest, 0)
            for jj in (0, 1):
                dk_ref[pl.ds(s0s[jj], T), :] = dk_acc[jj].astype(BF16)
                dv_ref[pl.ds(s0s[jj], T), :] = dv_acc[jj].astype(BF16)
            return 0

        lax.fori_loop(0, nq // 2, kv_pair, 0)
        dq_ref[...] = dq_acc[...].astype(BF16)

        def add_cols(kj, _):
            s0 = pl.multiple_of(kj * T, T)
            dc_ref[:, pl.ds(s0, T)] += dcol_acc[pl.ds(s0, T), :].T[0:2, :]
            return 0

        lax.fori_loop(0, nq, add_cols, 0)

    col = pl.BlockSpec((S, PAIR), lambda hp: (0, hp))
    row2 = pl.BlockSpec((None, 2, S), lambda hp: (hp, 0, 0))
    return _pcall(
        body, name=name, grid=(NP,), comm=comm,
        out_shape=(jax.ShapeDtypeStruct(do.shape, BF16), jax.ShapeDtypeStruct((S, DM), BF16),
                   jax.ShapeDtypeStruct((S, DM), BF16), jax.ShapeDtypeStruct((NP, 2, S), F32)),
        in_specs=[col, col, col, col, col, col, row2], out_specs=(col, col, col, row2),
        scratch_shapes=[pltpu.VMEM((S, PAIR), F32), pltpu.VMEM((S, LANES), F32), pltpu.VMEM((2, T, PAIR), F32),
                        pltpu.VMEM((2, T, PAIR), F32), pltpu.VMEM((8, S), F32)],
        args=(q, k, v, o, do, ca, lse), aliases={4: 0})


def _mixer_b_fwd(x, o, qm, mkv, wo, g, b, name):
    S, D = x.shape
    DQ = qm.shape[1]
    DM = D - DQ
    tm = min(512, S)

    def body(x_ref, o_ref, qm_ref, mkv_ref, wo_ref, g_ref, b_ref, xo_ref, xbt_ref, r_ref, catt_ref, cat_ref):
        cat_ref[:, :DM] = o_ref[...]
        mo = _mem_attn_fwd(qm_ref[...], mkv_ref, DQ)
        for mp in range(DQ // PAIR):
            cat_ref[:, DM + mp * PAIR:DM + (mp + 1) * PAIR] = mo[mp].astype(BF16)
        _tail_fwd_store(x_ref[...], cat_ref, wo_ref, g_ref, b_ref, xo_ref, xbt_ref, r_ref, catt_ref)

    return pl.pallas_call(
        body, name=name, grid=(S // tm,),
        out_shape=(jax.ShapeDtypeStruct((S, D), F32), jax.ShapeDtypeStruct((D, S), BF16),
                   jax.ShapeDtypeStruct((S, D), F32), jax.ShapeDtypeStruct((D, S), BF16)),
        in_specs=[_rows(tm, D), _rows(tm, DM), _rows(tm, DQ), _full(mkv.shape), _full(wo.shape), _full((1, D)),
                  _full((1, D))],
        out_specs=(_rows(tm, D), _cols(D, tm), _rows(tm, D), _cols(D, tm)),
        scratch_shapes=[pltpu.VMEM((tm, D), BF16)],
        compiler_params=_params(),
    )(x, o, qm, mkv, wo, g, b)


def _mixer_b_bwd(dxo, r, qm, mkv, wo, g, name, comm=None):
    S, D = r.shape
    DQ = qm.shape[1]
    DM = D - DQ
    M = mkv.shape[0]
    tm = min(512, S)

    def body(dxo_ref, r_ref, qm_ref, mkv_ref, wo_ref, g_ref, dr_ref, drb_ref, dcat_ref, dgb_ref, dmkv_ref):
        @pl.when(pl.program_id(0) == 0)
        def _():
            dgb_ref[...] = jnp.zeros_like(dgb_ref)
            dmkv_ref[...] = jnp.zeros_like(dmkv_ref)

        dr, dcat = _tail_bwd_head(dxo_ref, r_ref, g_ref, wo_ref, dgb_ref, drb_ref)
        dr_ref[...] = dr
        dcat_ref[:, :DM] = dcat[:, :DM].astype(BF16)
        dmo = [dcat[:, DM + mp * PAIR:DM + (mp + 1) * PAIR] for mp in range(DQ // PAIR)]
        dqm = _mem_attn_bwd(qm_ref[...], mkv_ref, dmo, dmkv_ref, DQ)
        for mp in range(DQ // PAIR):
            dcat_ref[:, DM + mp * PAIR:DM + (mp + 1) * PAIR] = dqm[mp].astype(BF16)

    return _pcall(
        body, name=name, grid=(S // tm,), comm=comm,
        out_shape=(jax.ShapeDtypeStruct((S, D), F32), jax.ShapeDtypeStruct((S, D), BF16),
                   jax.ShapeDtypeStruct((S, D), BF16), jax.ShapeDtypeStruct((2, D), F32),
                   jax.ShapeDtypeStruct((M, 2 * DQ), F32)),
        in_specs=[_rows(tm, D), _rows(tm, D), _rows(tm, DQ), _full(mkv.shape), _full(wo.shape), _full((1, D))],
        out_specs=(_rows(tm, D), _rows(tm, D), _rows(tm, D), _full((2, D)), _full((M, 2 * DQ))),
        args=(dxo, r, qm, mkv, wo, g))


class _NoExchange:
    def plan(self, stage, w, grads):
        return None

    def done(self, stage, results, w):
        pass


def _local_step(x, mem, target, w, hooks):
    S, D = x.shape
    DQ = w["mkv0"].shape[1] // 2
    DM = D - DQ
    NG = DM // HEAD_DIM
    NP = DM // PAIR
    ln_g, ln_b = w["ln_g"], w["ln_b"]
    row = lambda a, i: a[i:i + 1]
    memb = mem.astype(BF16)
    bst =jnp.pad(w["b_s"].T, ((0, 0), (0, LANES - NG)))
    grads = {}

    def staged(stage, fn, *args):
        out, got = fn(*args, stage, comm=hooks.plan(stage, w, grads))
        hooks.done(stage, got, w)
        return out

    mkv0 = _mm_small(memb, w["mkv0"], "mem_kv_0")
    x1, x1t, r1, cat0t, x0t = staged("mixer_a_fwd", _mixer_a_fwd, x, w["wa"], w["sg"], w["sb"], w["ws"], bst, mkv0,
                                w["wo0"], row(ln_g, 0), row(ln_b, 0))
    x2, x2b, x2t, r2, h0 = staged("mlp_fwd_0", _mlp_fwd, x1, w["wup0"], w["wdown0"], row(ln_g, 1), row(ln_b, 1))
    q, qm, k, v, fl = _proj_b_fwd(x2b, w["wq"], w["wkv"], DM, "proj_b_fwd")
    flt = fl[:, :16].T
    bfc = jnp.pad(w["bf"], (0, 16 - NG)).reshape(16, 1)
    pieces = _forget_cumsum(flt, bfc, "forget_cumsum")
    pieces = jnp.pad(pieces[:, :NG].transpose(2, 1, 0), ((0, 0), (0, 0), (0, HEAD_DIM - N_BIAS)))
    ca = pieces.reshape(S, NP, 2, HEAD_DIM)[:, :, ::-1].reshape(S, DM).astype(BF16)
    o, lse = staged("fox_fwd", _fox_fwd, q, k, v, ca)
    mkv1 = _mm_small(memb, w["mkv1"], "mem_kv_1")
    x3, x3t, r3, cat1t = _mixer_b_fwd(x2, o, qm, mkv1, w["wo1"], row(ln_g, 2), row(ln_b, 2), "mixer_b_fwd")
    dx4, r4, h1, loss_part = _mlp_fwd_loss(x3, w["wup1"], w["wdown1"], row(ln_g, 3), row(ln_b, 3), target,
                                           "mlp_fwd_1")

    nc = w["wup0"].shape[0]
    (dx3, dr4t, dpre1, dgb11), _ = _mlp_bwd(dx4, r4, h1, w["wup1"], w["wdown1"], row(ln_g, 3), "mlp_bwd_1")
    grads["wdown1"] = _mm_nn(dr4t, h1, "dw_down_1", out_split=nc)
    grads["wup1"] = _mm_nn(x3t, dpre1, "dw_up_1", out_split=nc)
    dr3, dr3b, dcat1, dgb10, dmkv1 = staged("mixer_b_bwd", _mixer_b_bwd, dx3, r3, qm, mkv1, w["wo1"], row(ln_g, 2))
    dqz, dk, dv, dct3 = staged("fox_bwd", _fox_bwd, q, k, v, o, dcat1, ca, lse)
    dct = jnp.pad(dct3.reshape(NG, S), ((0, 16 - NG), (0, 0)))
    dflt, dbf = _forget_cumsum_bwd(dct, flt, bfc, "forget_cumsum_bwd")
    dfl = jnp.pad(dflt.T, ((0, 0), (0, LANES - 16))).astype(BF16)
    dkvf = jnp.concatenate([dk, dv, dfl], axis=1)
    dx2 = _proj_b_bwd(dr3, dqz, dkvf, w["wq"], w["wkv"], "proj_b_bwd")
    grads["wo1"] = _mm_nn(cat1t, dr3b, "dw_o_1")
    grads["wq"] = _mm_nn(x2t, dqz, "dw_q")
    grads["wkv"] = _mm_nn(x2t, dkvf, "dw_kv")
    grads["mkv1"] = _mm_tn(memb, dmkv1.astype(BF16), "dw_mkv_1")
    dx1, dr2t, dpre0, dgb01 = staged("mlp_bwd_0", _mlp_bwd, dx2, r2, h0, w["wup0"], w["wdown0"], row(ln_g, 1))
    grads["wdown0"] = _mm_nn(dr2t, h0, "dw_down_0", out_split=nc)
    grads["wup0"] = _mm_nn(x1t, dpre0, "dw_up_0", out_split=nc)
    dx0, dz, dr1b, dgb00, dsgb, dws, dbst, dmkv0 = staged(
        "mixer_a_bwd", _mixer_a_bwd, x, dx1, r1, w["wa"], w["sg"], w["sb"], w["ws"], bst, mkv0, w["wo0"], row(ln_g, 0))
    grads["wo0"] = _mm_nn(cat0t, dr1b, "dw_o_0")
    grads["wa"] = _mm_nn(x0t, dz, "dw_a")
    grads["mkv0"] = _mm_tn(memb, dmkv0.astype(BF16), "dw_mkv_0")
    grads.update({
        "ws": dws, "b_s": dbst[:, :NG].T, "sg": dsgb[0:1], "sb": dsgb[1:2], "bf": dbf[:NG, 0], "loss": loss_part[0:1],
        "ln_g": jnp.concatenate([dgb00[0:1], dgb01[0:1], dgb10[0:1], dgb11[0:1]], axis=0),
        "ln_b": jnp.concatenate([dgb00[1:2], dgb01[1:2], dgb10[1:2], dgb11[1:2]], axis=0),
    })
    return loss_part, dx0, grads


def _place():
    xi, yi, ci = lax.axis_index("x"), lax.axis_index("y"), lax.axis_index("c")
    peers = [(xi, 1 - yi), (1 - xi, yi), (1 - xi, 1 - yi)]
    return xi, yi, ci, peers


def _comm_call(body, name, ins, out_shapes, n_remote, n_local):
    return pl.pallas_call(
        body, name=name, out_shape=out_shapes, in_specs=[ANY] * len(ins), out_specs=[ANY] * len(out_shapes),
        scratch_shapes=[pltpu.SemaphoreType.DMA((n_remote,)), pltpu.SemaphoreType.DMA((n_remote,)),
                        pltpu.SemaphoreType.DMA((max(n_local, 1),))],
        compiler_params=pltpu.CompilerParams(has_side_effects=True),
    )(*ins)


def _pair_swap(xs):
    n = len(xs)
    out_shapes = [jax.ShapeDtypeStruct((a.shape[0],) + a.shape[2:], a.dtype) for a in xs]

    def copies(x_refs, o_refs, sems):
        send_sems, recv_sems = sems
        xi, yi, ci, _ = _place()
        return [pltpu.make_async_remote_copy(src_ref=x_refs[i].at[:, 1 - ci], dst_ref=o_refs[i],
                                             send_sem=send_sems.at[i], recv_sem=recv_sems.at[i],
                                             device_id=(xi, yi, 1 - ci), device_id_type=MESH) for i in range(n)]

    def start(x_refs, o_refs, sems):
        for cp in copies(x_refs, o_refs, sems):
            cp.start()

    def finish(x_refs, o_refs, sems):
        cps = copies(x_refs, o_refs, sems)
        for cp in cps:
            cp.wait_recv()
        for cp in cps:
            cp.wait_send()

    return _Comm(xs, out_shapes, {}, [n, n], start, finish)


def _chip_exchange(xs):
    n = len(xs)
    out_shapes = [jax.ShapeDtypeStruct((3,) + a.shape[1:], a.dtype) for a in xs]

    def copies(x_refs, o_refs, sems):
        send_sems, recv_sems = sems
        xi, yi, ci, peers = _place()
        return [pltpu.make_async_remote_copy(src_ref=x_refs[i].at[2 * px + py], dst_ref=o_refs[i].at[d],
                                             send_sem=send_sems.at[3 * i + d], recv_sem=recv_sems.at[3 * i + d],
                                             device_id=(px, py, ci), device_id_type=MESH)
                for i in range(n) for d, (px, py) in enumerate(peers)]

    def start(x_refs, o_refs, sems):
        for cp in copies(x_refs, o_refs, sems):
            cp.start()

    def finish(x_refs, o_refs, sems):
        cps = copies(x_refs, o_refs, sems)
        for cp in cps:
            cp.wait_recv()
        for cp in cps:
            cp.wait_send()

    return _Comm(xs, out_shapes, {}, [3 * n, 3 * n], start, finish)


def _inplace_call(body, name, bufs, sem_counts):
    n = len(bufs)
    return pl.pallas_call(
        body, name=name, out_shape=[jax.ShapeDtypeStruct(a.shape, a.dtype) for a in bufs],
        in_specs=[ANY] * n, out_specs=[ANY] * n, input_output_aliases={i: i for i in range(n)},
        scratch_shapes=[pltpu.SemaphoreType.DMA((k,)) for k in sem_counts],
        compiler_params=pltpu.CompilerParams(has_side_effects=True),
    )(*bufs)


def _gather_weights(bufs, whole=()):
    n, nw = len(bufs), len(whole)

    def ici(w, sems, i, d, px, py, ci, slot):
        ref = w[i].at[slot, ci] if i < n else w[i].at[slot]
        return pltpu.make_async_remote_copy(src_ref=ref, dst_ref=ref, send_sem=sems[0].at[3 * i + d],
                                            recv_sem=sems[1].at[3 * i + d], device_id=(px, py, ci), device_id_type=MESH)

    def d2d(w, sems, i, d, xi, yi, ci, slot, half):
        ref = w[i].at[slot, half]
        return pltpu.make_async_remote_copy(src_ref=ref, dst_ref=ref, send_sem=sems[2].at[3 * i + d],
                                            recv_sem=sems[3].at[3 * i + d], device_id=(xi, yi, 1 - ci),
                                            device_id_type=MESH)

    def start(_, w, sems):
        xi, yi, ci, peers = _place()
        for i in range(n + nw):
            for d, (px, py) in enumerate(peers):
                ici(w, sems, i, d, px, py, ci, 2 * xi + yi).start()

    def finish(_, w, sems):
        xi, yi, ci, peers = _place()
        for d, (px, py) in enumerate(peers):
            for i in range(n + nw):
                ici(w, sems, i, d, px, py, ci, 2 * px + py).wait_recv()
                if i < n:
                    d2d(w, sems, i, d, xi, yi, ci, 2 * px + py, ci).start()
        for d, (px, py) in enumerate(peers):
            for i in range(n):
                d2d(w, sems, i, d, xi, yi, ci, 2 * px + py, 1 - ci).wait_recv()
        for d, (px, py) in enumerate(peers):
            for i in range(n + nw):
                ici(w, sems, i, d, px, py, ci, 2 * xi + yi).wait_send()
                if i < n:
                    d2d(w, sems, i, d, xi, yi, ci, 2 * px + py, ci).wait_send()

    arrays = list(bufs) + list(whole)
    k = 3 * (n + nw)
    return _Comm(arrays, [jax.ShapeDtypeStruct(a.shape, a.dtype) for a in arrays], {i: i for i in range(n + nw)},
                 [k, k, max(3 * n, 1), max(3 * n, 1)], start, finish)


def _pair_gather(bufs, spread, name):
    n = len(bufs)
    where = [(i, l) for i, a in enumerate(bufs) for l in range(a.shape[0])]
    nw = len(where)

    def body(*refs):
        g = refs[n + 1:2 * n + 1]
        sp = refs[2 * n + 1]
        send_sems, recv_sems = refs[2 * n + 2:]
        xi, yi, ci, peers = _place()
        chips = [(xi, yi)] + peers
        others = [(d, f) for d in range(4) for f in (0, 1) if (d, f) != (0, 0)]
        sends = []
        for k, (i, l) in enumerate(where):
            mine = g[i].at[l, :, ci]
            cp = pltpu.make_async_remote_copy(src_ref=mine, dst_ref=mine, send_sem=send_sems.at[k],
                                              recv_sem=recv_sems.at[k], device_id=(xi, yi, 1 - ci), device_id_type=MESH)
            cp.start()
            sends.append(cp)
        mine = sp.at[2 * xi + yi, ci]
        for k, (d, f) in enumerate(others):
            cp = pltpu.make_async_remote_copy(src_ref=mine, dst_ref=mine, send_sem=send_sems.at[nw + k],
                                              recv_sem=recv_sems.at[nw + k],
                                              device_id=(chips[d][0], chips[d][1], ci if f == 0 else 1 - ci),
                                              device_id_type=MESH)
            cp.start()
            sends.append(cp)
        for k, (i, l) in enumerate(where):
            other = g[i].at[l, :, 1 - ci]
            pltpu.make_async_remote_copy(src_ref=other, dst_ref=other, send_sem=send_sems.at[k],
                                         recv_sem=recv_sems.at[k], device_id=(xi, yi, 1 - ci),
                                         device_id_type=MESH).wait_recv()
        for k, (d, f) in enumerate(others):
            px, py, pc = chips[d][0], chips[d][1], (ci if f == 0 else 1 - ci)
            theirs = sp.at[2 * px + py, pc]
            pltpu.make_async_remote_copy(src_ref=theirs, dst_ref=theirs, send_sem=send_sems.at[nw + k],
                                         recv_sem=recv_sems.at[nw + k], device_id=(px, py, pc),
                                         device_id_type=MESH).wait_recv()
        for cp in sends:
            cp.wait_send()

    out = _inplace_call(body, name, list(bufs) + [spread], [nw + 7, nw + 7])
    return out[:n], out[n]


def _row_block(rows, cols):
    want = max(8, (2 ** 18 // max(cols, 1)) // 8 * 8)
    if rows <= want:
        return rows
    best = 8
    for t in range(8, want + 1, 8):
        if rows % t == 0:
            best = t
    return best


def _pair_add(g4, recv, sel, out_dtype, name):
    A, _, H, C = g4.shape
    bh = _row_block(H, C)

    def body(sel_ref, g_ref, r_ref, o_ref):
        o_ref[...] = (g_ref[...].astype(F32) + r_ref[...].astype(F32)).astype(out_dtype)

    return pl.pallas_call(
        body, name=name, out_shape=jax.ShapeDtypeStruct((A, H, C), out_dtype),
        grid_spec=pltpu.PrefetchScalarGridSpec(
            num_scalar_prefetch=1, grid=(A, H // bh),
            in_specs=[pl.BlockSpec((None, None, bh, C), lambda a, i, s: (a, s[1], i, 0)),
                      pl.BlockSpec((None, bh, C), lambda a, i, s: (a, i, 0))],
            out_specs=pl.BlockSpec((None, bh, C), lambda a, i, s: (a, i, 0))),
        compiler_params=pltpu.CompilerParams(dimension_semantics=("arbitrary", "arbitrary"),
                                             vmem_limit_bytes=VMEM_LIMIT),
    )(sel, g4, recv)


def _chip_reduce(g4, recv1, recv2, sel, buf, layer, n_layers, name, by_chip=False):
    _, _, H, C = g4.shape
    bh = _row_block(H, C)
    if by_chip:
        out_shape = jax.ShapeDtypeStruct((N_CHIPS, 2, H, C), F32)
        out_spec = pl.BlockSpec((None, None, bh, C), lambda i, s: (s[0], s[1], i, 0))
    else:
        out_shape = jax.ShapeDtypeStruct((n_layers, 1, 2, H, C), F32)
        out_spec = pl.BlockSpec((None, None, None, bh, C), lambda i, s: (layer, 0, s[1], i, 0))

    def body(sel_ref, g_ref, r1_ref, r2_ref, *rest):
        acc = g_ref[...].astype(F32) + r1_ref[...].astype(F32)
        for d in range(3):
            acc = acc + r2_ref[d].astype(F32)
        rest[-1][...] = acc

    in_specs = [pl.BlockSpec((None, None, bh, C), lambda i, s: (s[0], s[1], i, 0)),
                pl.BlockSpec((None, bh, C), lambda i, s: (s[0], i, 0)),
                pl.BlockSpec((3, bh, C), lambda i, s: (0, i, 0))]
    args = [sel, g4, recv1, recv2]
    aliases = {}
    if buf is not None:
        in_specs.append(ANY)
        args.append(buf)
        aliases = {4: 0}
    return pl.pallas_call(
        body, name=name, out_shape=out_shape,
        grid_spec=pltpu.PrefetchScalarGridSpec(num_scalar_prefetch=1, grid=(H // bh,), in_specs=in_specs,
                                               out_specs=out_spec),
        input_output_aliases=aliases,
        compiler_params=pltpu.CompilerParams(dimension_semantics=("arbitrary",), vmem_limit_bytes=VMEM_LIMIT),
    )(*args)


def _cast_place(shards, layer, sel, name):
    _, _, H, C = shards.shape
    bh = _row_block(H, C)

    def body(sel_ref, x_ref, o_ref):
        o_ref[...] = x_ref[...].astype(BF16)

    return pl.pallas_call(
        body, name=name, out_shape=jax.ShapeDtypeStruct((N_CHIPS, 2, H, C), BF16),
        grid_spec=pltpu.PrefetchScalarGridSpec(
            num_scalar_prefetch=1, grid=(2, H // bh),
            in_specs=[pl.BlockSpec((None, None, bh, C), lambda h, i, s: (layer, h, i, 0))],
            out_specs=pl.BlockSpec((None, None, bh, C), lambda h, i, s: (s[0], h, i, 0))),
        compiler_params=pltpu.CompilerParams(dimension_semantics=("arbitrary", "arbitrary"),
                                             vmem_limit_bytes=VMEM_LIMIT),
    )(sel, shards)


def _adamw(g, w, m, v, name):
    R, C = g.shape
    br = _row_block(R, C)

    def body(g_ref, w_ref, m_ref, v_ref, d_ref, mo_ref, vo_ref):
        gv = g_ref[...]
        mn = ADAM_B1 * m_ref[...] + (1.0 - ADAM_B1) * gv
        vn = ADAM_B2 * v_ref[...] + (1.0 - ADAM_B2) * (gv * gv)
        m_hat = mn / (1.0 - ADAM_B1 ** ADAM_STEP)
        v_hat = vn / (1.0 - ADAM_B2 ** ADAM_STEP)
        d_ref[...] = -ADAM_LR * (m_hat / (jnp.sqrt(v_hat) + ADAM_EPS) + ADAM_WD * w_ref[...])
        mo_ref[...] = mn
        vo_ref[...] = vn

    spec = pl.BlockSpec((br, C), lambda i: (i, 0))
    return pl.pallas_call(
        body, name=name, grid=(R // br,), out_shape=(jax.ShapeDtypeStruct((R, C), F32),) * 3,
        in_specs=[spec] * 4, out_specs=(spec,) * 3, compiler_params=_params(),
    )(g, w, m, v)


def _part_rows(shape):
    return -(-math.prod(shape) // (8 * LANES)) * 8


def _pack_rows(arrs, total_rows):
    parts = []
    for a in arrs:
        flat = a.reshape(-1)
        rows = _part_rows(a.shape)
        parts.append(jnp.pad(flat, (0, rows * LANES - flat.shape[0])).reshape(rows, LANES))
    packed = jnp.concatenate(parts, axis=0)
    return jnp.pad(packed, ((0, total_rows - packed.shape[0]), (0, 0)))


def _unpack_rows(packed, shapes):
    out, r = [], 0
    for shp in shapes:
        size, rows = math.prod(shp), _part_rows(shp)
        out.append(packed[r:r + rows].reshape(-1)[:size].reshape(shp))
        r += rows
    return out


def _rows_of(shapes):
    return sum(_part_rows(s) for s in shapes)


_GATHER_STAGES = {"start": ["wa", "mkv0", "wo0"], "mixer_a_fwd": ["wup0", "wdown0"],
                  "mlp_fwd_0": ["wq", "wkv", "mkv1", "wo1"], "fox_fwd": ["wup1", "wdown1"]}
_SWAP_STAGES = {"mixer_b_bwd": ["wup1", "wdown1"]}
_REDUCE_STAGES = {"fox_bwd": ["wup1", "wdown1"], "mlp_bwd_0": ["wo1", "wq", "wkv", "mkv1"],
                  "mixer_a_bwd": ["wup0", "wdown0"], "end": ["wa", "wo0", "mkv0", "small"]}
_SMALL = ["ws", "b_s", "sg", "sb", "bf", "ln_g", "ln_b", "loss"]


class _Staged:
    def __init__(self, placed, sel, dm, ng, shard_cols, n_small):
        self.placed, self.sel, self.dm, self.ng, self.shard_cols, self.n_small = placed, sel, dm, ng, shard_cols, n_small
        self.g4, self.recv1, self.recv2 = {}, {}, {}

    def _weight(self, name, a):
        a = a.reshape(N_CHIPS, 2 * a.shape[2], a.shape[3])
        if name in ("wa", "wkv"):
            a = a.transpose(1, 0, 2).reshape(a.shape[1], -1)
            if name == "wkv":
                a = jnp.concatenate([a[:, :2 * self.dm],
                                     jnp.pad(a[:, 2 * self.dm:], ((0, 0), (0, LANES - self.ng)))], axis=1)
            return a
        if name.startswith("wup") or name.startswith("wdown"):
            return a
        return a.reshape(-1, a.shape[2])

    def _partial(self, name, grads):
        if name == "small":
            a = _pack_rows([grads[n] for n in _SMALL], self.n_small).reshape(N_CHIPS, -1, LANES)
        elif name in self.shard_cols:
            n = self.shard_cols[name]
            g = grads[name][:, :N_CHIPS * n]
            a = g.reshape(g.shape[0], N_CHIPS, n).transpose(1, 0, 2)
        elif name.startswith("wup") or name.startswith("wdown"):
            a = grads[name]
        else:
            g = grads[name]
            a = g.reshape(N_CHIPS, g.shape[0] // N_CHIPS, g.shape[1])
        return a.reshape(N_CHIPS, 2, a.shape[1] // 2, a.shape[2])

    def plan(self, stage, w, grads):
        if stage in _GATHER_STAGES:
            return _gather_weights([self.placed[n] for n in _GATHER_STAGES[stage]])
        if stage in _SWAP_STAGES:
            for n in _SWAP_STAGES[stage]:
                self.g4[n] = self._partial(n, grads)
            return _pair_swap([self.g4[n] for n in _SWAP_STAGES[stage]])
        if stage in _REDUCE_STAGES:
            names = _REDUCE_STAGES[stage]
            late = [n for n in names if n not in self.recv1]
            for n in late:
                self.g4[n] = self._partial(n, grads)
            if late:
                got = _run_comm(_pair_swap([self.g4[n] for n in late]), "reduce_pair_swap_" + stage)
                self.recv1.update(zip(late, got))
            return _chip_exchange([_pair_add(self.g4[n], self.recv1[n], self.sel, F32 if n == "small" else BF16,
                                             "reduce_pair_add_" + n) for n in names])
        return None

    def done(self, stage, results, w):
        if stage in _GATHER_STAGES:
            for n, a in zip(_GATHER_STAGES[stage], results):
                w[n] = self._weight(n, a)
        elif stage in _SWAP_STAGES:
            self.recv1.update(zip(_SWAP_STAGES[stage], results))
        elif stage in _REDUCE_STAGES:
            self.recv2.update(zip(_REDUCE_STAGES[stage], results))


def kernel(x, mem, a_w_in, a_sgu_ln_g, a_sgu_ln_b, a_w_s, a_b_s, kv_w, kv_b_f, b_w_q, mem_w_kv, w_o, ln_g, ln_b, w_up, w_down, loss_target, m_a_w_in, m_a_sgu_ln_g, m_a_sgu_ln_b, m_a_w_s, m_a_b_s, m_kv_w, m_kv_b_f, m_b_w_q, m_mem_w_kv, m_w_o, m_ln_g, m_ln_b, m_w_up, m_w_down, v_a_w_in, v_a_sgu_ln_g, v_a_sgu_ln_b, v_a_w_s, v_a_b_s, v_kv_w, v_kv_b_f, v_b_w_q, v_mem_w_kv, v_w_o, v_ln_g, v_ln_b, v_w_up, v_w_down):
    xi, yi, ci = lax.axis_index("x"), lax.axis_index("y"), lax.axis_index("c")
    chip = 2 * xi + yi
    sel = jnp.stack([chip, ci]).astype(jnp.int32)
    S, D = x.shape[1], x.shape[2]
    DQ = mem_w_kv.shape[2] // 2
    DM = D - DQ
    NG = DM // HEAD_DIM

    shards = {"wa": (a_w_in, 0), "wkv": (kv_w[None], 0), "wq": (b_w_q, 0), "mkv0": (mem_w_kv, 0), "mkv1": (mem_w_kv, 1),
              "wo0": (w_o, 0), "wo1": (w_o, 1), "wup0": (w_up, 0), "wup1": (w_up, 1), "wdown0": (w_down, 0),
              "wdown1": (w_down, 1)}
    placed = {n: _cast_place(s.reshape(s.shape[0], 2, s.shape[1] // 2, s.shape[2]), layer, sel, "cast_place_" + n)
              for n, (s, layer) in shards.items()}
    ln_pack = jnp.concatenate([ln_g.reshape(4, -1), ln_b.reshape(4, -1)], axis=0)
    ln_buf = lax.dynamic_update_slice(jnp.zeros((N_CHIPS,) + ln_pack.shape, F32), ln_pack[None], (chip, 0, 0))
    ln_shape = (4, D)
    small_shapes = [a_w_s.shape, a_b_s.shape, a_sgu_ln_g.shape, a_sgu_ln_b.shape, kv_b_f.shape, ln_shape, ln_shape,
                    (1, LANES)]
    n_small = -(-_rows_of(small_shapes) // 64) * 64
    hooks = _Staged(placed, sel, DM, NG, {"wa": a_w_in.shape[2], "wkv": kv_w.shape[1]}, n_small)
    weights = {"sg": a_sgu_ln_g, "sb": a_sgu_ln_b, "ws": a_w_s[0], "b_s": a_b_s[0], "bf": kv_b_f}
    got = _run_comm(_gather_weights([placed[n] for n in _GATHER_STAGES["start"]], whole=[ln_buf]),
                    "gather_weights_first")
    hooks.done("start", got[:-1], weights)
    ln_full = got[-1].transpose(1, 0, 2).reshape(8, D)
    weights["ln_g"], weights["ln_b"] = ln_full[:4], ln_full[4:]

    _, grad_x, gr = _local_step(x[0], mem[0], loss_target[0], weights, hooks)

    hooks.done("end", _run_comm(hooks.plan("end", weights, gr), "reduce_chip_exchange_end"), weights)
    groups = [["wa"], ["wkv"], ["wq"], ["mkv0", "mkv1"], ["wo0", "wo1"], ["wup0", "wup1"], ["wdown0", "wdown1"]]
    mine = []
    for grp in groups:
        buf = None
        for layer, n in enumerate(grp):
            buf = _chip_reduce(hooks.g4[n], hooks.recv1[n], hooks.recv2[n], sel, buf, layer, len(grp),
                               "reduce_chip_sum_" + n)
        mine.append(buf)
    small_mine = _chip_reduce(hooks.g4["small"], hooks.recv1["small"], hooks.recv2["small"], sel, None, 0, 1,
                              "reduce_chip_sum_small", by_chip=True)
    red, small_all = _pair_gather(mine, small_mine, "reduce_pair_gather")
    red = [a.reshape(a.shape[0], 2 * a.shape[3], a.shape[4]) for a in red]
    g_a_w_in, g_kv_w, g_b_w_q, g_mem_w_kv, g_w_o, g_w_up, g_w_down = red
    g_kv_w = g_kv_w[0]
    g_w_down = g_w_down.transpose(0, 2, 1)
    small_all = small_all.reshape(n_small, LANES)
    g_ws, g_bs, g_sg, g_sb, g_bf, g_lng, g_lnb, loss_row = _unpack_rows(small_all, small_shapes)
    loss = loss_row[0, 0]
    dsh = D // N_CHIPS
    g_ln_g = lax.dynamic_slice_in_dim(g_lng, chip * dsh, dsh, axis=1).reshape(ln_g.shape)
    g_ln_b = lax.dynamic_slice_in_dim(g_lnb, chip * dsh, dsh, axis=1).reshape(ln_b.shape)

    def update(name, g, w, m, v):
        flat = lambda a: a.reshape(-1, a.shape[-1])
        d, mn, vn = _adamw(flat(g), flat(w), flat(m), flat(v), "adamw_" + name)
        return d.reshape(w.shape), mn.reshape(w.shape), vn.reshape(w.shape)

    upd = {
        "a_w_in": update("a_w_in", g_a_w_in, a_w_in, m_a_w_in, v_a_w_in),
        "kv_w": update("kv_w", g_kv_w, kv_w, m_kv_w, v_kv_w),
        "b_w_q": update("b_w_q", g_b_w_q, b_w_q, m_b_w_q, v_b_w_q),
        "mem_w_kv": update("mem_w_kv", g_mem_w_kv, mem_w_kv, m_mem_w_kv, v_mem_w_kv),
        "w_o": update("w_o", g_w_o, w_o, m_w_o, v_w_o),
        "w_up": update("w_up", g_w_up, w_up, m_w_up, v_w_up),
        "w_down": update("w_down", g_w_down, w_down, m_w_down, v_w_down),
    }
    tiny_g = [g_sg, g_sb, g_ws, g_bs, g_bf, g_ln_g, g_ln_b]
    tiny_w = [a_sgu_ln_g, a_sgu_ln_b, a_w_s, a_b_s, kv_b_f, ln_g, ln_b]
    tiny_m = [m_a_sgu_ln_g, m_a_sgu_ln_b, m_a_w_s, m_a_b_s, m_kv_b_f, m_ln_g, m_ln_b]
    tiny_v = [v_a_sgu_ln_g, v_a_sgu_ln_b, v_a_w_s, v_a_b_s, v_kv_b_f, v_ln_g, v_ln_b]
    tiny_shapes = [a.shape for a in tiny_w]
    n_tiny = -(-_rows_of(tiny_shapes) // 8) * 8
    td, tm_, tv = _adamw(_pack_rows(tiny_g, n_tiny), _pack_rows(tiny_w, n_tiny), _pack_rows(tiny_m, n_tiny),
                         _pack_rows(tiny_v, n_tiny), "adamw_small")
    for name, d, mn, vn in zip(["a_sgu_ln_g", "a_sgu_ln_b", "a_w_s", "a_b_s", "kv_b_f", "ln_g", "ln_b"],
                               _unpack_rows(td, tiny_shapes), _unpack_rows(tm_, tiny_shapes),
                               _unpack_rows(tv, tiny_shapes)):
        upd[name] = (d, mn, vn)

    order = ["a_w_in", "a_sgu_ln_g", "a_sgu_ln_b", "a_w_s", "a_b_s", "kv_w", "kv_b_f", "b_w_q", "mem_w_kv", "w_o",
             "ln_g", "ln_b", "w_up", "w_down"]
    grads = {"a_w_in": g_a_w_in.reshape(a_w_in.shape), "a_sgu_ln_g": g_sg.reshape(a_sgu_ln_g.shape),
             "a_sgu_ln_b": g_sb.reshape(a_sgu_ln_b.shape), "a_w_s": g_ws.reshape(a_w_s.shape),
             "a_b_s": g_bs.reshape(a_b_s.shape), "kv_w": g_kv_w, "kv_b_f": g_bf, "b_w_q": g_b_w_q.reshape(b_w_q.shape),
             "mem_w_kv": g_mem_w_kv, "w_o": g_w_o, "ln_g": g_ln_g, "ln_b": g_ln_b, "w_up": g_w_up, "w_down": g_w_down}
    return (loss, grad_x[None], *[grads[n] for n in order], *[upd[n][0] for n in order],
            *[upd[n][1] for n in order], *[upd[n][2] for n in order])
```

```python
import functools
import math

import jax
import jax.numpy as jnp
from jax import lax
from jax.experimental import pallas as pl
from jax.experimental.pallas import tpu as pltpu

F32 = jnp.float32
BF16 = jnp.bfloat16

HEAD_DIM = 64
PAIR = 2 * HEAD_DIM
CHUNK = 128
LN_EPS = 1e-5
ALPHA = 4 ** 0.25
QK_SCALE = 1.0 / math.sqrt(HEAD_DIM)
NEG = -1e30
N_CHIPS = 4
FOX_BLOCK = 256
LANES = 128
VMEM_LIMIT = 48 * 2 ** 20

ADAM_LR, ADAM_B1, ADAM_B2, ADAM_EPS, ADAM_WD, ADAM_STEP = 0.001, 0.9, 0.999, 1e-08, 0.01, 10

MESH = pl.DeviceIdType.MESH
ANY = pl.BlockSpec(memory_space=pl.ANY)


def _dot(a, b):
    return jnp.dot(a, b, preferred_element_type=F32)


def _dot_nt(a, b):
    return lax.dot_general(a, b, (((1,), (1,)), ((), ())), preferred_element_type=F32)


def _dot_tn(a, b):
    return lax.dot_general(a, b, (((0,), (0,)), ((), ())), preferred_element_type=F32)


def _ln_stats(r):
    mu = jnp.mean(r, axis=-1, keepdims=True)
    d = r - mu
    var = jnp.mean(d * d, axis=-1, keepdims=True)
    rstd = lax.rsqrt(var + LN_EPS)
    return d * rstd, rstd


def _ln_bwd(dy, xhat, rstd, g):
    dxh = dy * g
    m1 = jnp.mean(dxh, axis=-1, keepdims=True)
    m2 = jnp.mean(dxh * xhat, axis=-1, keepdims=True)
    return rstd * (dxh - m1 - xhat * m2)


_GELU_K = math.sqrt(2.0 / math.pi)


def _gelu(x):
    return 0.5 * x * (1.0 + jnp.tanh(_GELU_K * (x + 0.044715 * x * x * x)))


def _gelu_grad(x):
    t = jnp.tanh(_GELU_K * (x + 0.044715 * x * x * x))
    return 0.5 * (1.0 + t) + 0.5 * x * (1.0 - t * t) * _GELU_K * (1.0 + 3 * 0.044715 * x * x)


def _lane_mask(e):
    lane = lax.broadcasted_iota(jnp.int32, (1, PAIR), 1)
    return (lane >= HEAD_DIM * e) & (lane < HEAD_DIM * (e + 1))


def _first_head():
    return lax.broadcasted_iota(jnp.int32, (1, PAIR), 1) < HEAD_DIM


def _mem_probs(qh, mkp):
    sc = _dot_nt(qh, mkp) * QK_SCALE
    ex = jnp.exp(sc - jnp.max(sc, axis=-1, keepdims=True))
    return ex / jnp.sum(ex, axis=-1, keepdims=True)


def _mem_attn_fwd(qm_b, mkv_ref, dq_dim):
    outs = []
    for mp in range(dq_dim // PAIR):
        qp = qm_b[:, mp * PAIR:(mp + 1) * PAIR]
        mkp = mkv_ref[:, mp * PAIR:(mp + 1) * PAIR]
        mvp = mkv_ref[:, dq_dim + mp * PAIR:dq_dim + (mp + 1) * PAIR]
        heads = []
        for e in (0, 1):
            qh = jnp.where(_lane_mask(e), qp, jnp.zeros_like(qp))
            p = _mem_probs(qh, mkp)
            heads.append(_dot(p.astype(BF16), mvp))
        outs.append(jnp.where(_first_head(), heads[0], heads[1]))
    return outs


def _mem_attn_bwd(qm_b, mkv_ref, dmo_pairs, dmkv_ref, dq_dim):
    dqs = []
    for mp in range(dq_dim // PAIR):
        ks = slice(mp * PAIR, (mp + 1) * PAIR)
        vs = slice(dq_dim + mp * PAIR, dq_dim + (mp + 1) * PAIR)
        qp = qm_b[:, ks]
        mkp = mkv_ref[:, ks]
        mvp = mkv_ref[:, vs]
        dmo_b = dmo_pairs[mp].astype(BF16)
        dq = None
        dmk = None
        dmv = None
        for e in (0, 1):
            hm = _lane_mask(e)
            qh = jnp.where(hm, qp, jnp.zeros_like(qp))
            p = _mem_probs(qh, mkp)
            doh = jnp.where(hm, dmo_b, jnp.zeros_like(dmo_b))
            dp = _dot_nt(doh, mvp)
            ds = p * (dp - jnp.sum(dp * p, axis=-1, keepdims=True))
            dsb = (ds * QK_SCALE).astype(BF16)
            kh = jnp.where(hm, mkp, jnp.zeros_like(mkp))
            dq_e = _dot(dsb, kh)
            dmk_e = _dot_tn(dsb, qh)
            dmv_e = _dot_tn(p.astype(BF16), doh)
            dq = dq_e if dq is None else dq + dq_e
            dmk = dmk_e if dmk is None else dmk + dmk_e
            dmv = dmv_e if dmv is None else dmv + dmv_e
        dmkv_ref[:, ks] += dmk
        dmkv_ref[:, vs] += dmv
        dqs.append(dq)
    return dqs


def _params(n_axes=1):
    return pltpu.CompilerParams(dimension_semantics=("arbitrary",) * n_axes, vmem_limit_bytes=VMEM_LIMIT)


def _full(shape):
    return pl.BlockSpec(shape, lambda *_: (0,) * len(shape))


def _rows(tm, cols):
    return pl.BlockSpec((tm, cols), lambda i: (i, 0))


def _cols(rows, tm):
    return pl.BlockSpec((rows, tm), lambda i: (0, i))


def _pick(n, pref):
    if n <= pref:
        return n
    best = LANES
    for t in range(LANES, pref + 1, LANES):
        if n % t == 0:
            best = t
    return best if 2 * best >= pref or n > 2 * pref else n


class _Comm:
    def __init__(self, ins, out_shapes, aliases, sem_counts, start, finish):
        self.ins, self.out_shapes, self.aliases, self.sem_counts = list(ins), list(out_shapes), dict(aliases), sem_counts
        self.start, self.finish = start, finish


def _pcall(body, *, name, grid, in_specs, out_specs, out_shape, args, scratch_shapes=(), comm=None, aliases=None):
    n_in, n_out, n_scr = len(in_specs), len(out_shape), len(scratch_shapes)
    aliases = dict(aliases or {})
    if comm is None:
        res = pl.pallas_call(body, name=name, grid=grid, out_shape=tuple(out_shape), in_specs=list(in_specs),
                             out_specs=tuple(out_specs), scratch_shapes=list(scratch_shapes),
                             input_output_aliases=aliases, compiler_params=_params())(*args)
        return tuple(res), ()
    nci, nco = len(comm.ins), len(comm.out_shapes)
    last = grid[0] - 1

    def wrapped(*refs):
        ins, refs = refs[:n_in], refs[n_in:]
        cins, refs = refs[:nci], refs[nci:]
        outs, refs = refs[:n_out], refs[n_out:]
        couts, refs = refs[:nco], refs[nco:]
        scr, sems = refs[:n_scr], refs[n_scr:]

        @pl.when(pl.program_id(0) == 0)
        def _():
            comm.start(cins, couts, sems)

        body(*ins, *outs, *scr)

        @pl.when(pl.program_id(0) == last)
        def _():
            comm.finish(cins, couts, sems)

    res = pl.pallas_call(
        wrapped, name=name, grid=grid, out_shape=tuple(out_shape) + tuple(comm.out_shapes),
        in_specs=list(in_specs) + [ANY] * nci, out_specs=tuple(out_specs) + (ANY,) * nco,
        input_output_aliases={**aliases, **{n_in + a: n_out + b for a, b in comm.aliases.items()}},
        scratch_shapes=list(scratch_shapes) + [pltpu.SemaphoreType.DMA((k,)) for k in comm.sem_counts],
        compiler_params=pltpu.CompilerParams(dimension_semantics=("arbitrary",), vmem_limit_bytes=VMEM_LIMIT,
                                             has_side_effects=True),
    )(*args, *comm.ins)
    return tuple(res[:n_out]), tuple(res[n_out:])


def _run_comm(comm, name):
    nci, nco = len(comm.ins), len(comm.out_shapes)

    def body(*refs):
        cins, couts, sems = refs[:nci], refs[nci:nci + nco], refs[nci + nco:]
        comm.start(cins, couts, sems)
        comm.finish(cins, couts, sems)

    return pl.pallas_call(
        body, name=name, out_shape=tuple(comm.out_shapes), in_specs=[ANY] * nci, out_specs=(ANY,) * nco,
        input_output_aliases=comm.aliases, scratch_shapes=[pltpu.SemaphoreType.DMA((k,)) for k in comm.sem_counts],
        compiler_params=pltpu.CompilerParams(has_side_effects=True),
    )(*comm.ins)


def _mm(a, b, name, a_is_transposed, out_split=1):
    (M, R) = a.shape if a_is_transposed else a.shape[::-1]
    _, N = b.shape
    bm, br = _pick(M, 1024 if a_is_transposed else 512), _pick(R, 1024 if a_is_transposed else 512)
    ncol = N // out_split
    bn = _pick(ncol, 1024)
    per = ncol // bn
    last = R // br - 1

    def body(a_ref, b_ref, o_ref, acc_ref):
        @pl.when(pl.program_id(2) == 0)
        def _():
            acc_ref[...] = jnp.zeros_like(acc_ref)

        acc_ref[...] += (_dot if a_is_transposed else _dot_tn)(a_ref[...], b_ref[...])

        @pl.when(pl.program_id(2) == last)
        def _():
            o_ref[...] = acc_ref[...].astype(BF16)

    if out_split == 1:
        out_shape = jax.ShapeDtypeStruct((M, N), BF16)
        out_spec = pl.BlockSpec((bm, bn), lambda i, j, r: (i, j))
    else:
        out_shape = jax.ShapeDtypeStruct((out_split, M, ncol), BF16)
        out_spec = pl.BlockSpec((None, bm, bn), lambda i, j, r: (j // per, i, j % per))
    a_spec = (pl.BlockSpec((bm, br), lambda i, j, r: (i, r)) if a_is_transposed
              else pl.BlockSpec((br, bm), lambda i, j, r: (r, i)))
    return pl.pallas_call(
        body, name=name, out_shape=out_shape, grid=(M // bm, N // bn, R // br),
        in_specs=[a_spec, pl.BlockSpec((br, bn), lambda i, j, r: (r, j))], out_specs=out_spec,
        scratch_shapes=[pltpu.VMEM((bm, bn), F32)],
        compiler_params=pltpu.CompilerParams(dimension_semantics=("parallel", "parallel", "arbitrary"),
                                             vmem_limit_bytes=VMEM_LIMIT),
    )(a, b)


def _mm_tn(a, b, name, out_split=1):
    return _mm(a, b, name, False, out_split)


def _mm_nn(at, b, name, out_split=1):
    return _mm(at, b, name, True, out_split)


def _mm_small(a, b, name):
    def body(a_ref, b_ref, o_ref):
        o_ref[...] = _dot(a_ref[...], b_ref[...]).astype(BF16)

    return pl.pallas_call(body, name=name, out_shape=jax.ShapeDtypeStruct((a.shape[0], b.shape[1]), BF16),
                          compiler_params=pltpu.CompilerParams(vmem_limit_bytes=VMEM_LIMIT))(a, b)


def _mlp_out(x_ref, wup_ref, wdown_ref, g_ref, b_ref, r_ref, h_ref):
    NC, _, FC = wup_ref.shape
    xv = x_ref[...]
    xb = xv.astype(BF16)
    y = jnp.zeros(xv.shape, F32)
    for c in range(NC):
        a = jnp.maximum(_dot(xb, wup_ref[c]), 0.0)
        hb = (a * a).astype(BF16)
        h_ref[:, c * FC:(c + 1) * FC] = hb
        y = y + _dot(hb, wdown_ref[c])
    r = ALPHA * xv + y
    r_ref[...] = r
    xhat, _ = _ln_stats(r)
    return xhat * g_ref[...] + b_ref[...]


def _mlp_fwd_loss(x, wup4, wdown4, g, b, target, name):
    S, D = x.shape
    NC, _, FC = wup4.shape
    tm = min(256, S)

    def body(x_ref, wup_ref, wdown_ref, g_ref, b_ref, t_ref, d_ref, r_ref, h_ref, l_ref):
        @pl.when(pl.program_id(0) == 0)
        def _():
            l_ref[...] = jnp.zeros_like(l_ref)

        err = _mlp_out(x_ref, wup_ref, wdown_ref, g_ref, b_ref, r_ref, h_ref) - t_ref[...]
        d_ref[...] = err * (1.0 / D)
        l_ref[...] += (0.5 / D) * jnp.sum(err * err)

    return pl.pallas_call(
        body, name=name, grid=(S // tm,),
        out_shape=(jax.ShapeDtypeStruct((S, D), F32), jax.ShapeDtypeStruct((S, D), F32),
                   jax.ShapeDtypeStruct((S, NC * FC), BF16), jax.ShapeDtypeStruct((8, LANES), F32)),
        in_specs=[_rows(tm, D), _full(wup4.shape), _full(wdown4.shape), _full((1, D)), _full((1, D)), _rows(tm, D)],
        out_specs=(_rows(tm, D), _rows(tm, D), _rows(tm, NC * FC), _full((8, LANES))),
        compiler_params=_params(),
    )(x, wup4, wdown4, g, b, target)


def _mlp_fwd(x, wup4, wdown4, g, b, name, comm=None):
    S, D = x.shape
    NC, _, FC = wup4.shape
    tm = min(256, S)

    def body(x_ref, wup_ref, wdown_ref, g_ref, b_ref, xo_ref, xb_ref, xbt_ref, r_ref, h_ref):
        xo = _mlp_out(x_ref, wup_ref, wdown_ref, g_ref, b_ref, r_ref, h_ref)
        xo_ref[...] = xo
        xb_ref[...] = xo.astype(BF16)
        xbt_ref[...] = xo.T.astype(BF16)

    return _pcall(
        body, name=name, grid=(S // tm,), comm=comm,
        out_shape=(jax.ShapeDtypeStruct((S, D), F32), jax.ShapeDtypeStruct((S, D), BF16),
                   jax.ShapeDtypeStruct((D, S), BF16), jax.ShapeDtypeStruct((S, D), F32),
                   jax.ShapeDtypeStruct((S, NC * FC), BF16)),
        in_specs=[_rows(tm, D), _full(wup4.shape), _full(wdown4.shape), _full((1, D)), _full((1, D))],
        out_specs=(_rows(tm, D), _rows(tm, D), _cols(D, tm), _rows(tm, D), _rows(tm, NC * FC)),
        args=(x, wup4, wdown4, g, b))


def _mlp_bwd(dxo, r, h, wup4, wdown4, g, name, comm=None):
    S, D = r.shape
    NC, _, FC = wup4.shape
    tm = min(256, S)

    def body(dxo_ref, r_ref, h_ref, wup_ref, wdown_ref, g_ref, dx_ref, drbt_ref, dpre_ref, dgb_ref):
        @pl.when(pl.program_id(0) == 0)
        def _():
            dgb_ref[...] = jnp.zeros_like(dgb_ref)

        xhat, rstd = _ln_stats(r_ref[...])
        dy = dxo_ref[...]
        dgb_ref[0:1, :] += jnp.sum(dy * xhat, axis=0, keepdims=True)
        dgb_ref[1:2, :] += jnp.sum(dy, axis=0, keepdims=True)
        dr = _ln_bwd(dy, xhat, rstd, g_ref[...])
        drb = dr.astype(BF16)
        drbt_ref[...] = dr.T.astype(BF16)
        dx = ALPHA * dr
        for c in range(NC):
            dh = _dot_nt(drb, wdown_ref[c])
            a = jnp.sqrt(h_ref[:, c * FC:(c + 1) * FC].astype(F32))
            dpre = (2.0 * a * dh).astype(BF16)
            dpre_ref[:, c * FC:(c + 1) * FC] = dpre
            dx = dx + _dot_nt(dpre, wup_ref[c])
        dx_ref[...] = dx

    return _pcall(
        body, name=name, grid=(S // tm,), comm=comm,
        out_shape=(jax.ShapeDtypeStruct((S, D), F32), jax.ShapeDtypeStruct((D, S), BF16),
                   jax.ShapeDtypeStruct((S, NC * FC), BF16), jax.ShapeDtypeStruct((2, D), F32)),
        in_specs=[_rows(tm, D), _rows(tm, D), _rows(tm, NC * FC), _full(wup4.shape), _full(wdown4.shape),
                  _full((1, D))],
        out_specs=(_rows(tm, D), _cols(D, tm), _rows(tm, NC * FC), _full((2, D))),
        args=(dxo, r, h, wup4, wdown4, g))


def _tail_fwd_store(xv, cat_ref, wo_ref, g_ref, b_ref, xo_ref, xbt_ref, r_ref, catt_ref):
    cat = cat_ref[...]
    y = _dot(cat, wo_ref[...])
    r = ALPHA * xv + y
    xhat, _ = _ln_stats(r)
    xo = xhat * g_ref[...] + b_ref[...]
    xo_ref[...] = xo
    xbt_ref[...] = xo.T.astype(BF16)
    catt_ref[...] = cat.astype(F32).T.astype(BF16)
    r_ref[...] = r


def _tail_bwd_head(dxo_ref, r_ref, g_ref, wo_ref, dgb_ref, drb_ref):
    xhat, rstd = _ln_stats(r_ref[...])
    dy = dxo_ref[...]
    dgb_ref[0:1, :] += jnp.sum(dy * xhat, axis=0, keepdims=True)
    dgb_ref[1:2, :] += jnp.sum(dy, axis=0, keepdims=True)
    dr = _ln_bwd(dy, xhat, rstd, g_ref[...])
    drb = dr.astype(BF16)
    drb_ref[...] = drb
    return dr, _dot_nt(drb, wo_ref[...])


def _causal(shape):
    return lax.broadcasted_iota(jnp.int32, shape, 1) <= lax.broadcasted_iota(jnp.int32, shape, 0)


def _sgu_mixed(ws_ref, bst_ref, gp, vpair):
    tril = _causal((CHUNK, CHUNK))
    w0 = jnp.where(tril, ws_ref[2 * gp], 0.0).astype(BF16)
    w1 = jnp.where(tril, ws_ref[2 * gp + 1], 0.0).astype(BF16)
    m0 = _dot(w0, vpair) + bst_ref[:, 2 * gp:2 * gp + 1]
    m1 = _dot(w1, vpair) + bst_ref[:, 2 * gp + 1:2 * gp + 2]
    return jnp.where(_first_head(), m0, m1), w0, w1


def _mixer_a_fwd(x, wa, sg, sb, ws, bst, mkv, wo, g, b, name, comm=None):
    S, D = x.shape
    DQ = mkv.shape[1] // 2
    DM = D - DQ
    NP = DM // PAIR
    tm = min(512, S)

    def body(x_ref, wa_ref, sg_ref, sb_ref, ws_ref, bst_ref, mkv_ref, wo_ref, g_ref, b_ref,
             xo_ref, xbt_ref, r_ref, catt_ref, xint_ref, z_s, zv_s, cat_ref):
        xv = x_ref[...]
        xint_ref[...] = xv.T.astype(BF16)
        z_s[...] = _dot(xv.astype(BF16), wa_ref[...])
        vhat, _ = _ln_stats(_gelu(z_s[:, DM:2 * DM]))
        zv_s[...] = (vhat * sg_ref[...] + sb_ref[...]).astype(BF16)
        for c in range(tm // CHUNK):
            rs = slice(c * CHUNK, (c + 1) * CHUNK)
            for gp in range(NP):
                ls = slice(gp * PAIR, (gp + 1) * PAIR)
                mixed, _, _ = _sgu_mixed(ws_ref, bst_ref, gp, zv_s[rs, ls])
                cat_ref[rs, ls] = (_gelu(z_s[rs, ls]) * mixed).astype(BF16)
        mo = _mem_attn_fwd(z_s[:, 2 * DM:].astype(BF16), mkv_ref, DQ)
        for mp in range(DQ // PAIR):
            cat_ref[:, DM + mp * PAIR:DM + (mp + 1) * PAIR] = mo[mp].astype(BF16)
        _tail_fwd_store(xv, cat_ref, wo_ref, g_ref, b_ref, xo_ref, xbt_ref, r_ref, catt_ref)

    return _pcall(
        body, name=name, grid=(S // tm,), comm=comm,
        out_shape=(jax.ShapeDtypeStruct((S, D), F32), jax.ShapeDtypeStruct((D, S), BF16),
                   jax.ShapeDtypeStruct((S, D), F32), jax.ShapeDtypeStruct((D, S), BF16),
                   jax.ShapeDtypeStruct((D, S), BF16)),
        in_specs=[_rows(tm, D), _full(wa.shape), _full(sg.shape), _full(sb.shape), _full(ws.shape), _full(bst.shape),
                  _full(mkv.shape), _full(wo.shape), _full((1, D)), _full((1, D))],
        out_specs=(_rows(tm, D), _cols(D, tm), _rows(tm, D), _cols(D, tm), _cols(D, tm)),
        scratch_shapes=[pltpu.VMEM((tm, 2 * DM + DQ), F32), pltpu.VMEM((tm, DM), BF16), pltpu.VMEM((tm, D), BF16)],
        args=(x, wa, sg, sb, ws, bst, mkv, wo, g, b))


def _mixer_a_bwd(x, dxo, r, wa, sg, sb, ws, bst, mkv, wo, g, name, comm=None):
    S, D = x.shape
    DQ = mkv.shape[1] // 2
    DM = D - DQ
    NP = DM // PAIR
    NG = DM // HEAD_DIM
    tm = min(512, S)

    def body(x_ref, dxo_ref, r_ref, wa_ref, sg_ref, sb_ref, ws_ref, bst_ref, mkv_ref, wo_ref, g_ref,
             dx_ref, dz_ref, drb_ref, dgb_ref, dsgb_ref, dws_ref, dbst_ref, dmkv_ref, z_s, zv_s, dzv_s):
        @pl.when(pl.program_id(0) == 0)
        def _():
            dgb_ref[...] = jnp.zeros_like(dgb_ref)
            dsgb_ref[...] = jnp.zeros_like(dsgb_ref)
            dws_ref[...] = jnp.zeros_like(dws_ref)
            dbst_ref[...] = jnp.zeros_like(dbst_ref)
            dmkv_ref[...] = jnp.zeros_like(dmkv_ref)

        dr, dcat = _tail_bwd_head(dxo_ref, r_ref, g_ref, wo_ref, dgb_ref, drb_ref)
        z_s[...] = _dot(x_ref[...].astype(BF16), wa_ref[...])
        vhat, vrstd = _ln_stats(_gelu(z_s[:, DM:2 * DM]))
        zv_s[...] = (vhat * sg_ref[...] + sb_ref[...]).astype(BF16)
        tril = _causal((CHUNK, CHUNK))
        lane = lax.broadcasted_iota(jnp.int32, (1, LANES), 1)
        for c in range(tm // CHUNK):
            rs = slice(c * CHUNK, (c + 1) * CHUNK)
            for gp in range(NP):
                ls = slice(gp * PAIR, (gp + 1) * PAIR)
                vpair = zv_s[rs, ls]
                mixed, w0, w1 = _sgu_mixed(ws_ref, bst_ref, gp, vpair)
                u_pre = z_s[rs, ls]
                dmix = dcat[rs, ls]
                dz_ref[rs, ls] = (dmix * mixed * _gelu_grad(u_pre)).astype(BF16)
                dmixed = dmix * _gelu(u_pre)
                first = _first_head()
                d0 = jnp.where(first, dmixed, 0.0)
                d1 = jnp.where(first, 0.0, dmixed)
                d0b = d0.astype(BF16)
                d1b = d1.astype(BF16)
                dzv_s[rs, ls] = _dot_tn(w0, d0b) + _dot_tn(w1, d1b)
                dws_ref[2 * gp] += jnp.where(tril, _dot_nt(d0b, vpair), 0.0)
                dws_ref[2 * gp + 1] += jnp.where(tril, _dot_nt(d1b, vpair), 0.0)
                dbst_ref[...] += (jnp.where(lane == 2 * gp, jnp.sum(d0, axis=-1, keepdims=True), 0.0)
                                  + jnp.where(lane == 2 * gp + 1, jnp.sum(d1, axis=-1, keepdims=True), 0.0))
        dzv = dzv_s[...]
        dsgb_ref[0:1, :] += jnp.sum(dzv * vhat, axis=0, keepdims=True)
        dsgb_ref[1:2, :] += jnp.sum(dzv, axis=0, keepdims=True)
        dgv = _ln_bwd(dzv, vhat, vrstd, sg_ref[...])
        dz_ref[:, DM:2 * DM] = (dgv * _gelu_grad(z_s[:, DM:2 * DM])).astype(BF16)
        dmo = [dcat[:, DM + mp * PAIR:DM + (mp + 1) * PAIR] for mp in range(DQ // PAIR)]
        dqm = _mem_attn_bwd(z_s[:, 2 * DM:].astype(BF16), mkv_ref, dmo, dmkv_ref, DQ)
        for mp in range(DQ // PAIR):
            dz_ref[:, 2 * DM + mp * PAIR:2 * DM + (mp + 1) * PAIR] = dqm[mp].astype(BF16)
        dx_ref[...] = ALPHA * dr + _dot_nt(dz_ref[...], wa_ref[...])

    ZW = 2 * DM + DQ
    M = mkv.shape[0]
    return _pcall(
        body, name=name, grid=(S // tm,), comm=comm,
        out_shape=(jax.ShapeDtypeStruct((S, D), F32), jax.ShapeDtypeStruct((S, ZW), BF16),
                   jax.ShapeDtypeStruct((S, D), BF16), jax.ShapeDtypeStruct((2, D), F32),
                   jax.ShapeDtypeStruct((2, DM), F32), jax.ShapeDtypeStruct((NG, CHUNK, CHUNK), F32),
                   jax.ShapeDtypeStruct((CHUNK, LANES), F32), jax.ShapeDtypeStruct((M, 2 * DQ), F32)),
        in_specs=[_rows(tm, D), _rows(tm, D), _rows(tm, D), _full(wa.shape), _full(sg.shape), _full(sb.shape),
                  _full(ws.shape), _full(bst.shape), _full(mkv.shape), _full(wo.shape), _full((1, D))],
        out_specs=(_rows(tm, D), _rows(tm, ZW), _rows(tm, D), _full((2, D)), _full((2, DM)),
                   _full((NG, CHUNK, CHUNK)), _full((CHUNK, LANES)), _full((M, 2 * DQ))),
        scratch_shapes=[pltpu.VMEM((tm, ZW), F32), pltpu.VMEM((tm, DM), BF16), pltpu.VMEM((tm, DM), F32)],
        args=(x, dxo, r, wa, sg, sb, ws, bst, mkv, wo, g))


def _proj_b_fwd(xb, wq, wkv, dm, name):
    S, D = xb.shape
    DQ = D - dm
    tm = min(512, S)

    def body(x_ref, wq_ref, wkv_ref, q_ref, qm_ref, k_ref, v_ref, fl_ref):
        xv = x_ref[...]
        z = _dot(xv, wq_ref[...])
        q_ref[...] = z[:, :dm].astype(BF16)
        qm_ref[...] = z[:, dm:].astype(BF16)
        kvf = _dot(xv, wkv_ref[...])
        k_ref[...] = kvf[:, :dm].astype(BF16)
        v_ref[...] = kvf[:, dm:2 * dm].astype(BF16)
        fl_ref[...] = kvf[:, 2 * dm:]

    return pl.pallas_call(
        body, name=name, grid=(S // tm,),
        out_shape=(jax.ShapeDtypeStruct((S, dm), BF16), jax.ShapeDtypeStruct((S, DQ), BF16),
                   jax.ShapeDtypeStruct((S, dm), BF16), jax.ShapeDtypeStruct((S, dm), BF16),
                   jax.ShapeDtypeStruct((S, LANES), F32)),
        in_specs=[_rows(tm, D), _full(wq.shape), _full(wkv.shape)],
        out_specs=(_rows(tm, dm), _rows(tm, DQ), _rows(tm, dm), _rows(tm, dm), _rows(tm, LANES)),
        compiler_params=_params(),
    )(xb, wq, wkv)


def _proj_b_bwd(dr, dqz, dkvf, wq, wkv, name):
    S, D = dr.shape
    tm = min(512, S)

    def body(dr_ref, dqz_ref, dkvf_ref, wq_ref, wkv_ref, dx_ref):
        dx_ref[...] = (ALPHA * dr_ref[...] + _dot_nt(dqz_ref[...], wq_ref[...])
                       + _dot_nt(dkvf_ref[...], wkv_ref[...]))

    return pl.pallas_call(
        body, name=name, grid=(S // tm,), out_shape=jax.ShapeDtypeStruct((S, D), F32),
        in_specs=[_rows(tm, D), _rows(tm, dqz.shape[1]), _rows(tm, dkvf.shape[1]), _full(wq.shape), _full(wkv.shape)],
        out_specs=_rows(tm, D), compiler_params=_params(),
    )(dr, dqz, dkvf, wq, wkv)


def _split3(v):
    hi = v.astype(BF16)
    r1 = v - hi.astype(F32)
    mid = r1.astype(BF16)
    lo = (r1 - mid.astype(F32)).astype(BF16)
    return hi, mid, lo


def _tri_sum(v, tri_b):
    hi, mid, lo = _split3(v)
    return _dot(hi, tri_b) + _dot(mid, tri_b) + _dot(lo, tri_b)


def _log_sigmoid(x):
    return jnp.minimum(x, 0.0) - jnp.log(1.0 + jnp.exp(-jnp.abs(x)))


def _forget_cumsum(flt, bf, name):
    H, S = flt.shape
    nchunk = S // LANES

    def body(fl_ref, bf_ref, p_ref):
        upper = (lax.broadcasted_iota(jnp.int32, (LANES, LANES), 0)
                 <= lax.broadcasted_iota(jnp.int32, (LANES, LANES), 1)).astype(BF16)

        def step(n, carry):
            s0 = pl.multiple_of(n * LANES, LANES)
            lf = _log_sigmoid(fl_ref[:, pl.ds(s0, LANES)] + bf_ref[...])
            cs = _tri_sum(lf, upper) + carry
            for i, piece in enumerate(_split3(-cs)):
                p_ref[i, :, pl.ds(s0, LANES)] = piece.astype(F32)
            return cs[:, LANES - 1:LANES]

        lax.fori_loop(0, nchunk, step, jnp.zeros((H, 1), F32))

    return pl.pallas_call(body, name=name, out_shape=jax.ShapeDtypeStruct((3, H, S), F32),
                          compiler_params=pltpu.CompilerParams(vmem_limit_bytes=VMEM_LIMIT))(flt, bf)


def _forget_cumsum_bwd(dct, flt, bf, name):
    H, S = flt.shape
    nchunk = S // LANES

    def body(dc_ref, fl_ref, bf_ref, dfl_ref, dbf_ref):
        lower = (lax.broadcasted_iota(jnp.int32, (LANES, LANES), 0)
                 >= lax.broadcasted_iota(jnp.int32, (LANES, LANES), 1)).astype(BF16)

        def step(n, carry):
            tail, tot = carry
            s0 = pl.multiple_of((nchunk - 1 - n) * LANES, LANES)
            suffix = _tri_sum(dc_ref[:, pl.ds(s0, LANES)], lower) + tail
            xv = fl_ref[:, pl.ds(s0, LANES)] + bf_ref[...]
            dfl = suffix * (1.0 / (1.0 + jnp.exp(xv)))
            dfl_ref[:, pl.ds(s0, LANES)] = dfl
            return suffix[:, 0:1], tot + jnp.sum(dfl, axis=-1, keepdims=True)

        _, tot = lax.fori_loop(0, nchunk, step, (jnp.zeros((H, 1), F32), jnp.zeros((H, 1), F32)))
        dbf_ref[...] = jnp.broadcast_to(tot, (H, LANES))

    return pl.pallas_call(body, name=name,
                          out_shape=(jax.ShapeDtypeStruct((H, S), F32), jax.ShapeDtypeStruct((H, LANES), F32)),
                          compiler_params=pltpu.CompilerParams(vmem_limit_bytes=VMEM_LIMIT))(dct, flt, bf)


N_BIAS = 3


def _bias_ones():
    lane = lax.broadcasted_iota(jnp.int32, (1, PAIR), 1)
    return ((lane & (HEAD_DIM - 1)) < N_BIAS).astype(BF16)


def _fox_fwd(q, k, v, ca, name, comm=None):
    S, DM = q.shape
    NP = DM // PAIR
    T = min(FOX_BLOCK, S)
    nq = S // T

    def body(q_ref, k_ref, v_ref, ca_ref, o_ref, lse_ref):
        tril = _causal((T, T))
        heads = (_lane_mask(0), _lane_mask(1))
        ones3 = _bias_ones()

        def step(carry, qaug, kaug, vaug, masked):
            m, acc = carry
            s = _dot_nt(qaug, kaug)
            if masked:
                s = jnp.where(tril, s, NEG)
            m_new = jnp.maximum(m, jnp.max(s, axis=-1, keepdims=True))
            p = jnp.exp(s - m_new).astype(BF16)
            return m_new, jnp.exp(m - m_new) * acc + _dot(p, vaug)

        def kv_block(kj):
            s0 = pl.multiple_of(kj * T, T)
            kb, vb, cab = k_ref[pl.ds(s0, T), :], v_ref[pl.ds(s0, T), :], ca_ref[pl.ds(s0, T), :]
            return ([jnp.where(heads[e], kb, cab) for e in (0, 1)],
                    [jnp.where(heads[e], vb, jnp.ones_like(vb)) for e in (0, 1)])

        def q_pair(a, _):
            t0s = [pl.multiple_of((2 * a + r) * T, T) for r in (0, 1)]
            qaug = {}
            for r in (0, 1):
                qb = q_ref[pl.ds(t0s[r], T), :] * QK_SCALE
                for e in (0, 1):
                    qaug[r, e] = jnp.where(heads[e], qb, ones3)
            chains = [(r, e) for r in (0, 1) for e in (0, 1)]

            def full(first_block, n_blocks, carries):
                blocks = [kv_block(first_block + kk) for kk in range(n_blocks)]
                scores = [[_dot_nt(qaug[r, e], kaug[e]) for r, e in chains] for kaug, _ in blocks]
                out = []
                for n, (r, e) in enumerate(chains):
                    m, acc = carries[n]
                    top = functools.reduce(jnp.maximum, [scores[kk][n] for kk in range(n_blocks)])
                    m_new = jnp.maximum(m, jnp.max(top, axis=-1, keepdims=True))
                    acc = jnp.exp(m - m_new) * acc
                    for kk in range(n_blocks):
                        acc = acc + _dot(jnp.exp(scores[kk][n] - m_new).astype(BF16), blocks[kk][1][e])
                    out.append((m_new, acc))
                return tuple(out)

            init = (jnp.full((T, 1), NEG, F32), jnp.zeros((T, PAIR), F32))
            carries = lax.fori_loop(0, a // 2, lambda i, c: full(4 * i, 4, c), (init,) * 4)
            carries = list(lax.fori_loop(0, a % 2, lambda i, c: full(2 * a - 2, 2, c), carries))
            kaug, vaug = kv_block(2 * a)
            for i, (r, e) in enumerate(chains):
                carries[i] = step(carries[i], qaug[r, e], kaug[e], vaug[e], r == 0)
            kaug, vaug = kv_block(2 * a + 1)
            for i, (r, e) in enumerate(chains):
                if r == 1:
                    carries[i] = step(carries[i], qaug[r, e], kaug[e], vaug[e], True)
            first = _first_head()
            for r in (0, 1):
                (m0, acc0), (m1, acc1) = carries[2 * r], carries[2 * r + 1]
                l0, l1 = acc0[:, HEAD_DIM:HEAD_DIM + 1], acc1[:, 0:1]
                o_ref[pl.ds(t0s[r], T), :] = jnp.where(first, acc0 / l0, acc1 / l1).astype(BF16)
                lse_t = jnp.where(first, m0 + jnp.log(l0), m1 + jnp.log(l1)).T
                lse_ref[0:1, pl.ds(t0s[r], T)] = lse_t[0:1, :]
                lse_ref[1:2, pl.ds(t0s[r], T)] = lse_t[HEAD_DIM:HEAD_DIM + 1, :]
            return 0

        lax.fori_loop(0, nq // 2, q_pair, 0)

    col = pl.BlockSpec((S, PAIR), lambda hp: (0, hp))
    return _pcall(
        body, name=name, grid=(NP,), comm=comm,
        out_shape=(jax.ShapeDtypeStruct((S, DM), BF16), jax.ShapeDtypeStruct((NP, 2, S), F32)),
        in_specs=[col, col, col, col], out_specs=(col, pl.BlockSpec((None, 2, S), lambda hp: (hp, 0, 0))),
        args=(q, k, v, ca))


def _fox_bwd(q, k, v, o, do, ca, lse, name, comm=None):
    S, DM = q.shape
    NP = DM // PAIR
    T = min(FOX_BLOCK, S)
    nq = S // T

    def body(q_ref, k_ref, v_ref, o_ref, do_ref, ca_ref, lse_ref, dq_ref, dk_ref, dv_ref, dc_ref,
             dq_acc, dcol_acc, dk_acc, dv_acc, dd_s):
        keep = lax.broadcasted_iota(jnp.int32, (T, T), 0) <= lax.broadcasted_iota(jnp.int32, (T, T), 1)
        lane = lax.broadcasted_iota(jnp.int32, (1, LANES), 1)
        heads = (_lane_mask(0), _lane_mask(1))
        ones3 = _bias_ones()
        dq_acc[...] = jnp.zeros_like(dq_acc)
        dcol_acc[...] = jnp.zeros_like(dcol_acc)
        dc_ref[...] = jnp.zeros_like(dc_ref)

        def row_dots(qi, _):
            t0 = pl.multiple_of(qi * T, T)
            prod = do_ref[pl.ds(t0, T), :].astype(F32) * o_ref[pl.ds(t0, T), :].astype(F32)
            cols = [jnp.sum(jnp.where(heads[e], prod, 0.0), axis=-1, keepdims=True) for e in (0, 1)]
            tile = jnp.where(lane == 0, cols[0], jnp.where(lane == 1, cols[1], 0.0))
            dd_s[:, pl.ds(t0, T)] = tile.T[0:8, :]
            return 0

        lax.fori_loop(0, nq, row_dots, 0)

        def kv_pair(b, _):
            dk_acc[...] = jnp.zeros_like(dk_acc)
            dv_acc[...] = jnp.zeros_like(dv_acc)
            s0s = [pl.multiple_of((2 * b + jj) * T, T) for jj in (0, 1)]
            kaug, ks, vbs = {}, {}, []
            for jj in (0, 1):
                kb = k_ref[pl.ds(s0s[jj], T), :]
                cab = ca_ref[pl.ds(s0s[jj], T), :]
                vbs.append(v_ref[pl.ds(s0s[jj], T), :])
                for e in (0, 1):
                    kaug[jj, e] = jnp.where(heads[e], kb, cab)
                    ks[jj, e] = jnp.where(heads[e], kb, jnp.zeros_like(kb)) * QK_SCALE

            def q_step(qi, blocks):
                t0 = pl.multiple_of(qi * T, T)
                qs = q_ref[pl.ds(t0, T), :] * QK_SCALE
                dob = do_ref[pl.ds(t0, T), :]
                doh = [jnp.where(heads[e], dob, jnp.zeros_like(dob)) for e in (0, 1)]
                qh = [jnp.where(heads[e], qs, jnp.zeros_like(qs)) for e in (0, 1)]
                chains = [(e, jj, masked) for e in (0, 1) for jj, masked in blocks]
                scores = [_dot_nt(kaug[jj, e], jnp.where(heads[e], qs, ones3)) for e, jj, _ in chains]
                dps = [_dot_nt(vbs[jj], doh[e]) for e, jj, _ in chains]
                dq = jnp.zeros((T, PAIR), F32)
                drow = [jnp.zeros((1, T), F32), jnp.zeros((1, T), F32)]
                for n, (e, jj, masked) in enumerate(chains):
                    st = jnp.where(keep, scores[n], NEG) if masked else scores[n]
                    pt = jnp.exp(st - lse_ref[e:e + 1, pl.ds(t0, T)])
                    dv_acc[jj] += _dot(pt.astype(BF16), doh[e])
                    dst = pt * (dps[n] - dd_s[e:e + 1, pl.ds(t0, T)])
                    dsb = dst.astype(BF16)
                    dk_acc[jj] += _dot(dsb, qh[e])
                    dq = dq + _dot_tn(dsb, ks[jj, e])
                    dcol_acc[pl.ds(s0s[jj], T), :] -= jnp.where(lane == e, jnp.sum(dst, axis=-1, keepdims=True), 0.0)
                    drow[e] = drow[e] + jnp.sum(dst, axis=0, keepdims=True)
                for e in (0, 1):
                    dc_ref[e:e + 1, pl.ds(t0, T)] += drow[e]
                dq_acc[pl.ds(t0, T), :] += dq

            q_step(2 * b, [(0, True)])
            q_step(2 * b + 1, [(0, False), (1, True)])

            def rest(i, _):
                q_step(2 * b + 2 + 2 * i, [(0, False), (1, False)])
                q_step(2 * b + 3 + 2 * i, [(0, False), (1, False)])
                return 0

            lax.fori_loop(0, (nq - 2) // 2 - b, rest, 0)
            for jj in (0, 1):
                dk_ref[pl.ds(s0s[jj], T), :] = dk_acc[jj].astype(BF16)
                dv_ref[pl.ds(s0s[jj], T), :] = dv_acc[jj].astype(BF16)
            return 0

        lax.fori_loop(0, nq // 2, kv_pair, 0)
        dq_ref[...] = dq_acc[...].astype(BF16)

        def add_cols(kj, _):
            s0 = pl.multiple_of(kj * T, T)
            dc_ref[:, pl.ds(s0, T)] += dcol_acc[pl.ds(s0, T), :].T[0:2, :]
            return 0

        lax.fori_loop(0, nq, add_cols, 0)

    col = pl.BlockSpec((S, PAIR), lambda hp: (0, hp))
    row2 = pl.BlockSpec((None, 2, S), lambda hp: (hp, 0, 0))
    return _pcall(
        body, name=name, grid=(NP,), comm=comm,
        out_shape=(jax.ShapeDtypeStruct(do.shape, BF16), jax.ShapeDtypeStruct((S, DM), BF16),
                   jax.ShapeDtypeStruct((S, DM), BF16), jax.ShapeDtypeStruct((NP, 2, S), F32)),
        in_specs=[col, col, col, col, col, col, row2], out_specs=(col, col, col, row2),
        scratch_shapes=[pltpu.VMEM((S, PAIR), F32), pltpu.VMEM((S, LANES), F32), pltpu.VMEM((2, T, PAIR), F32),
                        pltpu.VMEM((2, T, PAIR), F32), pltpu.VMEM((8, S), F32)],
        args=(q, k, v, o, do, ca, lse), aliases={4: 0})


def _mixer_b_fwd(x, o, qm, mkv, wo, g, b, name):
    S, D = x.shape
    DQ = qm.shape[1]
    DM = D - DQ
    tm = min(512, S)

    def body(x_ref, o_ref, qm_ref, mkv_ref, wo_ref, g_ref, b_ref, xo_ref, xbt_ref, r_ref, catt_ref, cat_ref):
        cat_ref[:, :DM] = o_ref[...]
        mo = _mem_attn_fwd(qm_ref[...], mkv_ref, DQ)
        for mp in range(DQ // PAIR):
            cat_ref[:, DM + mp * PAIR:DM + (mp + 1) * PAIR] = mo[mp].astype(BF16)
        _tail_fwd_store(x_ref[...], cat_ref, wo_ref, g_ref, b_ref, xo_ref, xbt_ref, r_ref, catt_ref)

    return pl.pallas_call(
        body, name=name, grid=(S // tm,),
        out_shape=(jax.ShapeDtypeStruct((S, D), F32), jax.ShapeDtypeStruct((D, S), BF16),
                   jax.ShapeDtypeStruct((S, D), F32), jax.ShapeDtypeStruct((D, S), BF16)),
        in_specs=[_rows(tm, D), _rows(tm, DM), _rows(tm, DQ), _full(mkv.shape), _full(wo.shape), _full((1, D)),
                  _full((1, D))],
        out_specs=(_rows(tm, D), _cols(D, tm), _rows(tm, D), _cols(D, tm)),
        scratch_shapes=[pltpu.VMEM((tm, D), BF16)],
        compiler_params=_params(),
    )(x, o, qm, mkv, wo, g, b)


def _mixer_b_bwd(dxo, r, qm, mkv, wo, g, name, comm=None):
    S, D = r.shape
    DQ = qm.shape[1]
    DM = D - DQ
    M = mkv.shape[0]
    tm = min(512, S)

    def body(dxo_ref, r_ref, qm_ref, mkv_ref, wo_ref, g_ref, dr_ref, drb_ref, dcat_ref, dgb_ref, dmkv_ref):
        @pl.when(pl.program_id(0) == 0)
        def _():
            dgb_ref[...] = jnp.zeros_like(dgb_ref)
            dmkv_ref[...] = jnp.zeros_like(dmkv_ref)

        dr, dcat = _tail_bwd_head(dxo_ref, r_ref, g_ref, wo_ref, dgb_ref, drb_ref)
        dr_ref[...] = dr
        dcat_ref[:, :DM] = dcat[:, :DM].astype(BF16)
        dmo = [dcat[:, DM + mp * PAIR:DM + (mp + 1) * PAIR] for mp in range(DQ // PAIR)]
        dqm = _mem_attn_bwd(qm_ref[...], mkv_ref, dmo, dmkv_ref, DQ)
        for mp in range(DQ // PAIR):
            dcat_ref[:, DM + mp * PAIR:DM + (mp + 1) * PAIR] = dqm[mp].astype(BF16)

    return _pcall(
        body, name=name, grid=(S // tm,), comm=comm,
        out_shape=(jax.ShapeDtypeStruct((S, D), F32), jax.ShapeDtypeStruct((S, D), BF16),
                   jax.ShapeDtypeStruct((S, D), BF16), jax.ShapeDtypeStruct((2, D), F32),
                   jax.ShapeDtypeStruct((M, 2 * DQ), F32)),
        in_specs=[_rows(tm, D), _rows(tm, D), _rows(tm, DQ), _full(mkv.shape), _full(wo.shape), _full((1, D))],
        out_specs=(_rows(tm, D), _rows(tm, D), _rows(tm, D), _full((2, D)), _full((M, 2 * DQ))),
        args=(dxo, r, qm, mkv, wo, g))


class _NoExchange:
    def plan(self, stage, w, grads):
        return None

    def done(self, stage, results, w):
        pass


def _local_step(x, mem, target, w, hooks):
    S, D = x.shape
    DQ = w["mkv0"].shape[1] // 2
    DM = D - DQ
    NG = DM // HEAD_DIM
    NP = DM // PAIR
    ln_g, ln_b = w["ln_g"], w["ln_b"]
    row = lambda a, i: a[i:i + 1]
    memb = mem.astype(BF16)
    bst =jnp.pad(w["b_s"].T, ((0, 0), (0, LANES - NG)))
    grads = {}

    def staged(stage, fn, *args):
        out, got = fn(*args, stage, comm=hooks.plan(stage, w, grads))
        hooks.done(stage, got, w)
        return out

    mkv0 = _mm_small(memb, w["mkv0"], "mem_kv_0")
    x1, x1t, r1, cat0t, x0t = staged("mixer_a_fwd", _mixer_a_fwd, x, w["wa"], w["sg"], w["sb"], w["ws"], bst, mkv0,
                                w["wo0"], row(ln_g, 0), row(ln_b, 0))
    x2, x2b, x2t, r2, h0 = staged("mlp_fwd_0", _mlp_fwd, x1, w["wup0"], w["wdown0"], row(ln_g, 1), row(ln_b, 1))
    q, qm, k, v, fl = _proj_b_fwd(x2b, w["wq"], w["wkv"], DM, "proj_b_fwd")
    flt = fl[:, :16].T
    bfc = jnp.pad(w["bf"], (0, 16 - NG)).reshape(16, 1)
    pieces = _forget_cumsum(flt, bfc, "forget_cumsum")
    pieces = jnp.pad(pieces[:, :NG].transpose(2, 1, 0), ((0, 0), (0, 0), (0, HEAD_DIM - N_BIAS)))
    ca = pieces.reshape(S, NP, 2, HEAD_DIM)[:, :, ::-1].reshape(S, DM).astype(BF16)
    o, lse = staged("fox_fwd", _fox_fwd, q, k, v, ca)
    mkv1 = _mm_small(memb, w["mkv1"], "mem_kv_1")
    x3, x3t, r3, cat1t = _mixer_b_fwd(x2, o, qm, mkv1, w["wo1"], row(ln_g, 2), row(ln_b, 2), "mixer_b_fwd")
    dx4, r4, h1, loss_part = _mlp_fwd_loss(x3, w["wup1"], w["wdown1"], row(ln_g, 3), row(ln_b, 3), target,
                                           "mlp_fwd_1")

    nc = w["wup0"].shape[0]
    (dx3, dr4t, dpre1, dgb11), _ = _mlp_bwd(dx4, r4, h1, w["wup1"], w["wdown1"], row(ln_g, 3), "mlp_bwd_1")
    grads["wdown1"] = _mm_nn(dr4t, h1, "dw_down_1", out_split=nc)
    grads["wup1"] = _mm_nn(x3t, dpre1, "dw_up_1", out_split=nc)
    dr3, dr3b, dcat1, dgb10, dmkv1 = staged("mixer_b_bwd", _mixer_b_bwd, dx3, r3, qm, mkv1, w["wo1"], row(ln_g, 2))
    dqz, dk, dv, dct3 = staged("fox_bwd", _fox_bwd, q, k, v, o, dcat1, ca, lse)
    dct = jnp.pad(dct3.reshape(NG, S), ((0, 16 - NG), (0, 0)))
    dflt, dbf = _forget_cumsum_bwd(dct, flt, bfc, "forget_cumsum_bwd")
    dfl = jnp.pad(dflt.T, ((0, 0), (0, LANES - 16))).astype(BF16)
    dkvf = jnp.concatenate([dk, dv, dfl], axis=1)
    dx2 = _proj_b_bwd(dr3, dqz, dkvf, w["wq"], w["wkv"], "proj_b_bwd")
    grads["wo1"] = _mm_nn(cat1t, dr3b, "dw_o_1")
    grads["wq"] = _mm_nn(x2t, dqz, "dw_q")
    grads["wkv"] = _mm_nn(x2t, dkvf, "dw_kv")
    grads["mkv1"] = _mm_tn(memb, dmkv1.astype(BF16), "dw_mkv_1")
    dx1, dr2t, dpre0, dgb01 = staged("mlp_bwd_0", _mlp_bwd, dx2, r2, h0, w["wup0"], w["wdown0"], row(ln_g, 1))
    grads["wdown0"] = _mm_nn(dr2t, h0, "dw_down_0", out_split=nc)
    grads["wup0"] = _mm_nn(x1t, dpre0, "dw_up_0", out_split=nc)
    dx0, dz, dr1b, dgb00, dsgb, dws, dbst, dmkv0 = staged(
        "mixer_a_bwd", _mixer_a_bwd, x, dx1, r1, w["wa"], w["sg"], w["sb"], w["ws"], bst, mkv0, w["wo0"], row(ln_g, 0))
    grads["wo0"] = _mm_nn(cat0t, dr1b, "dw_o_0")
    grads["wa"] = _mm_nn(x0t, dz, "dw_a")
    grads["mkv0"] = _mm_tn(memb, dmkv0.astype(BF16), "dw_mkv_0")
    grads.update({
        "ws": dws, "b_s": dbst[:, :NG].T, "sg": dsgb[0:1], "sb": dsgb[1:2], "bf": dbf[:NG, 0], "loss": loss_part[0:1],
        "ln_g": jnp.concatenate([dgb00[0:1], dgb01[0:1], dgb10[0:1], dgb11[0:1]], axis=0),
        "ln_b": jnp.concatenate([dgb00[1:2], dgb01[1:2], dgb10[1:2], dgb11[1:2]], axis=0),
    })
    return loss_part, dx0, grads


def _place():
    xi, yi, ci = lax.axis_index("x"), lax.axis_index("y"), lax.axis_index("c")
    peers = [(xi, 1 - yi), (1 - xi, yi), (1 - xi, 1 - yi)]
    return xi, yi, ci, peers


def _comm_call(body, name, ins, out_shapes, n_remote, n_local):
    return pl.pallas_call(
        body, name=name, out_shape=out_shapes, in_specs=[ANY] * len(ins), out_specs=[ANY] * len(out_shapes),
        scratch_shapes=[pltpu.SemaphoreType.DMA((n_remote,)), pltpu.SemaphoreType.DMA((n_remote,)),
                        pltpu.SemaphoreType.DMA((max(n_local, 1),))],
        compiler_params=pltpu.CompilerParams(has_side_effects=True),
    )(*ins)


def _pair_swap(xs):
    n = len(xs)
    out_shapes = [jax.ShapeDtypeStruct((a.shape[0],) + a.shape[2:], a.dtype) for a in xs]

    def copies(x_refs, o_refs, sems):
        send_sems, recv_sems = sems
        xi, yi, ci, _ = _place()
        return [pltpu.make_async_remote_copy(src_ref=x_refs[i].at[:, 1 - ci], dst_ref=o_refs[i],
                                             send_sem=send_sems.at[i], recv_sem=recv_sems.at[i],
                                             device_id=(xi, yi, 1 - ci), device_id_type=MESH) for i in range(n)]

    def start(x_refs, o_refs, sems):
        for cp in copies(x_refs, o_refs, sems):
            cp.start()

    def finish(x_refs, o_refs, sems):
        cps = copies(x_refs, o_refs, sems)
        for cp in cps:
            cp.wait_recv()
        for cp in cps:
            cp.wait_send()

    return _Comm(xs, out_shapes, {}, [n, n], start, finish)


def _chip_exchange(xs):
    n = len(xs)
    out_shapes = [jax.ShapeDtypeStruct((3,) + a.shape[1:], a.dtype) for a in xs]

    def copies(x_refs, o_refs, sems):
        send_sems, recv_sems = sems
        xi, yi, ci, peers = _place()
        return [pltpu.make_async_remote_copy(src_ref=x_refs[i].at[2 * px + py], dst_ref=o_refs[i].at[d],
                                             send_sem=send_sems.at[3 * i + d], recv_sem=recv_sems.at[3 * i + d],
                                             device_id=(px, py, ci), device_id_type=MESH)
                for i in range(n) for d, (px, py) in enumerate(peers)]

    def start(x_refs, o_refs, sems):
        for cp in copies(x_refs, o_refs, sems):
            cp.start()

    def finish(x_refs, o_refs, sems):
        cps = copies(x_refs, o_refs, sems)
        for cp in cps:
            cp.wait_recv()
        for cp in cps:
            cp.wait_send()

    return _Comm(xs, out_shapes, {}, [3 * n, 3 * n], start, finish)


def _inplace_call(body, name, bufs, sem_counts):
    n = len(bufs)
    return pl.pallas_call(
        body, name=name, out_shape=[jax.ShapeDtypeStruct(a.shape, a.dtype) for a in bufs],
        in_specs=[ANY] * n, out_specs=[ANY] * n, input_output_aliases={i: i for i in range(n)},
        scratch_shapes=[pltpu.SemaphoreType.DMA((k,)) for k in sem_counts],
        compiler_params=pltpu.CompilerParams(has_side_effects=True),
    )(*bufs)


def _gather_weights(bufs, whole=()):
    n, nw = len(bufs), len(whole)

    def ici(w, sems, i, d, px, py, ci, slot):
        ref = w[i].at[slot, ci] if i < n else w[i].at[slot]
        return pltpu.make_async_remote_copy(src_ref=ref, dst_ref=ref, send_sem=sems[0].at[3 * i + d],
                                            recv_sem=sems[1].at[3 * i + d], device_id=(px, py, ci), device_id_type=MESH)

    def d2d(w, sems, i, d, xi, yi, ci, slot, half):
        ref = w[i].at[slot, half]
        return pltpu.make_async_remote_copy(src_ref=ref, dst_ref=ref, send_sem=sems[2].at[3 * i + d],
                                            recv_sem=sems[3].at[3 * i + d], device_id=(xi, yi, 1 - ci),
                                            device_id_type=MESH)

    def start(_, w, sems):
        xi, yi, ci, peers = _place()
        for i in range(n + nw):
            for d, (px, py) in enumerate(peers):
                ici(w, sems, i, d, px, py, ci, 2 * xi + yi).start()

    def finish(_, w, sems):
        xi, yi, ci, peers = _place()
        for d, (px, py) in enumerate(peers):
            for i in range(n + nw):
                ici(w, sems, i, d, px, py, ci, 2 * px + py).wait_recv()
                if i < n:
                    d2d(w, sems, i, d, xi, yi, ci, 2 * px + py, ci).start()
        for d, (px, py) in enumerate(peers):
            for i in range(n):
                d2d(w, sems, i, d, xi, yi, ci, 2 * px + py, 1 - ci).wait_recv()
        for d, (px, py) in enumerate(peers):
            for i in range(n + nw):
                ici(w, sems, i, d, px, py, ci, 2 * xi + yi).wait_send()
                if i < n:
                    d2d(w, sems, i, d, xi, yi, ci, 2 * px + py, ci).wait_send()

    arrays = list(bufs) + list(whole)
    k = 3 * (n + nw)
    return _Comm(arrays, [jax.ShapeDtypeStruct(a.shape, a.dtype) for a in arrays], {i: i for i in range(n + nw)},
                 [k, k, max(3 * n, 1), max(3 * n, 1)], start, finish)


def _pair_gather(bufs, spread, name):
    n = len(bufs)
    where = [(i, l) for i, a in enumerate(bufs) for l in range(a.shape[0])]
    nw = len(where)
    arrays = list(bufs) + ([] if spread is None else [spread])
    na = len(arrays)

    def body(*refs):
        g = refs[na:na + n]
        sp = refs[na + n] if spread is not None else None
        send_sems, recv_sems = refs[2 * na:]
        xi, yi, ci, peers = _place()
        chips = [(xi, yi)] + peers
        others = [(d, f) for d in range(4) for f in (0, 1) if (d, f) != (0, 0)] if spread is not None else []
        sends = []
        for k, (i, l) in enumerate(where):
            mine = g[i].at[l, :, ci]
            cp = pltpu.make_async_remote_copy(src_ref=mine, dst_ref=mine, send_sem=send_sems.at[k],
                                              recv_sem=recv_sems.at[k], device_id=(xi, yi, 1 - ci), device_id_type=MESH)
            cp.start()
            sends.append(cp)
        for k, (d, f) in enumerate(others):
            mine = sp.at[2 * xi + yi, ci]
            cp = pltpu.make_async_remote_copy(src_ref=mine, dst_ref=mine, send_sem=send_sems.at[nw + k],
                                              recv_sem=recv_sems.at[nw + k],
                                              device_id=(chips[d][0], chips[d][1], ci if f == 0 else 1 - ci),
                                              device_id_type=MESH)
            cp.start()
            sends.append(cp)
        for k, (i, l) in enumerate(where):
            other = g[i].at[l, :, 1 - ci]
            pltpu.make_async_remote_copy(src_ref=other, dst_ref=other, send_sem=send_sems.at[k],
                                         recv_sem=recv_sems.at[k], device_id=(xi, yi, 1 - ci),
                                         device_id_type=MESH).wait_recv()
        for k, (d, f) in enumerate(others):
            px, py, pc = chips[d][0], chips[d][1], (ci if f == 0 else 1 - ci)
            theirs = sp.at[2 * px + py, pc]
            pltpu.make_async_remote_copy(src_ref=theirs, dst_ref=theirs, send_sem=send_sems.at[nw + k],
                                         recv_sem=recv_sems.at[nw + k], device_id=(px, py, pc),
                                         device_id_type=MESH).wait_recv()
        for cp in sends:
            cp.wait_send()

    n_sems = nw + (7 if spread is not None else 0)
    out = _inplace_call(body, name, arrays, [n_sems, n_sems])
    return out[:n], (out[n] if spread is not None else None)


def _row_block(rows, cols):
    want = max(8, (2 ** 18 // max(cols, 1)) // 8 * 8)
    if rows <= want:
        return rows
    best = 8
    for t in range(8, want + 1, 8):
        if rows % t == 0:
            best = t
    return best


def _pair_add(g4, recv, sel, out_dtype, name):
    A, _, H, C = g4.shape
    bh = _row_block(H, C)

    def body(sel_ref, g_ref, r_ref, o_ref):
        o_ref[...] = (g_ref[...].astype(F32) + r_ref[...].astype(F32)).astype(out_dtype)

    return pl.pallas_call(
        body, name=name, out_shape=jax.ShapeDtypeStruct((A, H, C), out_dtype),
        grid_spec=pltpu.PrefetchScalarGridSpec(
            num_scalar_prefetch=1, grid=(A, H // bh),
            in_specs=[pl.BlockSpec((None, None, bh, C), lambda a, i, s: (a, s[1], i, 0)),
                      pl.BlockSpec((None, bh, C), lambda a, i, s: (a, i, 0))],
            out_specs=pl.BlockSpec((None, bh, C), lambda a, i, s: (a, i, 0))),
        compiler_params=pltpu.CompilerParams(dimension_semantics=("arbitrary", "arbitrary"),
                                             vmem_limit_bytes=VMEM_LIMIT),
    )(sel, g4, recv)


def _chip_reduce(g4, recv1, recv2, sel, buf, layer, n_layers, name, by_chip=False):
    _, _, H, C = g4.shape
    bh = _row_block(H, C)
    if by_chip:
        out_shape = jax.ShapeDtypeStruct((N_CHIPS, 2, H, C), F32)
        out_spec = pl.BlockSpec((None, None, bh, C), lambda i, s: (s[0], s[1], i, 0))
    else:
        out_shape = jax.ShapeDtypeStruct((n_layers, 1, 2, H, C), F32)
        out_spec = pl.BlockSpec((None, None, None, bh, C), lambda i, s: (layer, 0, s[1], i, 0))

    def body(sel_ref, g_ref, r1_ref, r2_ref, *rest):
        acc = g_ref[...].astype(F32) + r1_ref[...].astype(F32)
        for d in range(3):
            acc = acc + r2_ref[d].astype(F32)
        rest[-1][...] = acc

    in_specs = [pl.BlockSpec((None, None, bh, C), lambda i, s: (s[0], s[1], i, 0)),
                pl.BlockSpec((None, bh, C), lambda i, s: (s[0], i, 0)),
                pl.BlockSpec((3, bh, C), lambda i, s: (0, i, 0))]
    args = [sel, g4, recv1, recv2]
    aliases = {}
    if buf is not None:
        in_specs.append(ANY)
        args.append(buf)
        aliases = {4: 0}
    return pl.pallas_call(
        body, name=name, out_shape=out_shape,
        grid_spec=pltpu.PrefetchScalarGridSpec(num_scalar_prefetch=1, grid=(H // bh,), in_specs=in_specs,
                                               out_specs=out_spec),
        input_output_aliases=aliases,
        compiler_params=pltpu.CompilerParams(dimension_semantics=("arbitrary",), vmem_limit_bytes=VMEM_LIMIT),
    )(*args)


def _cast_place(shards, layer, sel, name):
    _, _, H, C = shards.shape
    bh = _row_block(H, C)

    def body(sel_ref, x_ref, o_ref):
        o_ref[...] = x_ref[...].astype(BF16)

    return pl.pallas_call(
        body, name=name, out_shape=jax.ShapeDtypeStruct((N_CHIPS, 2, H, C), BF16),
        grid_spec=pltpu.PrefetchScalarGridSpec(
            num_scalar_prefetch=1, grid=(2, H // bh),
            in_specs=[pl.BlockSpec((None, None, bh, C), lambda h, i, s: (layer, h, i, 0))],
            out_specs=pl.BlockSpec((None, None, bh, C), lambda h, i, s: (s[0], h, i, 0))),
        compiler_params=pltpu.CompilerParams(dimension_semantics=("arbitrary", "arbitrary"),
                                             vmem_limit_bytes=VMEM_LIMIT),
    )(sel, shards)


def _adamw_many(quads, name, comm=None):
    R, C = quads[0][0].shape
    br = _row_block(R, C)
    n = len(quads)

    def body(*refs):
        for k in range(n):
            g_ref, w_ref, m_ref, v_ref = refs[4 * k:4 * k + 4]
            d_ref, mo_ref, vo_ref = refs[4 * n + 3 * k:4 * n + 3 * k + 3]
            gv = g_ref[...]
            mn = ADAM_B1 * m_ref[...] + (1.0 - ADAM_B1) * gv
            vn = ADAM_B2 * v_ref[...] + (1.0 - ADAM_B2) * (gv * gv)
            m_hat = mn / (1.0 - ADAM_B1 ** ADAM_STEP)
            v_hat = vn / (1.0 - ADAM_B2 ** ADAM_STEP)
            d_ref[...] = -ADAM_LR * (m_hat / (jnp.sqrt(v_hat) + ADAM_EPS) + ADAM_WD * w_ref[...])
            mo_ref[...] = mn
            vo_ref[...] = vn

    spec = pl.BlockSpec((br, C), lambda i: (i, 0))
    res, got = _pcall(body, name=name, grid=(R // br,), comm=comm, out_shape=(jax.ShapeDtypeStruct((R, C), F32),) * (3 * n),
                      in_specs=[spec] * (4 * n), out_specs=(spec,) * (3 * n), args=[a for q in quads for a in q])
    return [res[3 * k:3 * k + 3] for k in range(n)], got


def _adamw(g, w, m, v, name):
    return _adamw_many([(g, w, m, v)], name)[0][0]


def _part_rows(shape):
    return -(-math.prod(shape) // (8 * LANES)) * 8


def _pack_rows(arrs, total_rows):
    parts = []
    for a in arrs:
        flat = a.reshape(-1)
        rows = _part_rows(a.shape)
        parts.append(jnp.pad(flat, (0, rows * LANES - flat.shape[0])).reshape(rows, LANES))
    packed = jnp.concatenate(parts, axis=0)
    return jnp.pad(packed, ((0, total_rows - packed.shape[0]), (0, 0)))


def _unpack_rows(packed, shapes):
    out, r = [], 0
    for shp in shapes:
        size, rows = math.prod(shp), _part_rows(shp)
        out.append(packed[r:r + rows].reshape(-1)[:size].reshape(shp))
        r += rows
    return out


def _rows_of(shapes):
    return sum(_part_rows(s) for s in shapes)


_GATHER_STAGES = {"start": ["wa", "mkv0", "wo0"], "mixer_a_fwd": ["wup0", "wdown0"],
                  "mlp_fwd_0": ["wq", "wkv", "mkv1", "wo1"], "fox_fwd": ["wup1", "wdown1"]}
_SWAP_STAGES = {"mixer_b_bwd": ["wup1", "wdown1"]}
_REDUCE_STAGES = {"fox_bwd": ["wup1", "wdown1"], "mlp_bwd_0": ["wo1", "wq", "wkv", "mkv1"],
                  "mixer_a_bwd": ["wup0", "wdown0"], "end": ["wa", "wo0", "mkv0", "small"]}
_SMALL = ["ws", "b_s", "sg", "sb", "bf", "ln_g", "ln_b", "loss"]


class _Staged:
    def __init__(self, placed, sel, dm, ng, shard_cols, n_small):
        self.placed, self.sel, self.dm, self.ng, self.shard_cols, self.n_small = placed, sel, dm, ng, shard_cols, n_small
        self.g4, self.recv1, self.recv2 = {}, {}, {}

    def _weight(self, name, a):
        a = a.reshape(N_CHIPS, 2 * a.shape[2], a.shape[3])
        if name in ("wa", "wkv"):
            a = a.transpose(1, 0, 2).reshape(a.shape[1], -1)
            if name == "wkv":
                a = jnp.concatenate([a[:, :2 * self.dm],
                                     jnp.pad(a[:, 2 * self.dm:], ((0, 0), (0, LANES - self.ng)))], axis=1)
            return a
        if name.startswith("wup") or name.startswith("wdown"):
            return a
        return a.reshape(-1, a.shape[2])

    def _partial(self, name, grads):
        if name == "small":
            a = _pack_rows([grads[n] for n in _SMALL], self.n_small).reshape(N_CHIPS, -1, LANES)
        elif name in self.shard_cols:
            n = self.shard_cols[name]
            g = grads[name][:, :N_CHIPS * n]
            a = g.reshape(g.shape[0], N_CHIPS, n).transpose(1, 0, 2)
        elif name.startswith("wup") or name.startswith("wdown"):
            a = grads[name]
        else:
            g = grads[name]
            a = g.reshape(N_CHIPS, g.shape[0] // N_CHIPS, g.shape[1])
        return a.reshape(N_CHIPS, 2, a.shape[1] // 2, a.shape[2])

    def plan(self, stage, w, grads):
        if stage in _GATHER_STAGES:
            return _gather_weights([self.placed[n] for n in _GATHER_STAGES[stage]])
        if stage in _SWAP_STAGES:
            for n in _SWAP_STAGES[stage]:
                self.g4[n] = self._partial(n, grads)
            return _pair_swap([self.g4[n] for n in _SWAP_STAGES[stage]])
        if stage in _REDUCE_STAGES:
            names = _REDUCE_STAGES[stage]
            late = [n for n in names if n not in self.recv1]
            for n in late:
                self.g4[n] = self._partial(n, grads)
            if late:
                got = _run_comm(_pair_swap([self.g4[n] for n in late]), "reduce_pair_swap_" + stage)
                self.recv1.update(zip(late, got))
            return _chip_exchange([_pair_add(self.g4[n], self.recv1[n], self.sel, F32 if n == "small" else BF16,
                                             "reduce_pair_add_" + n) for n in names])
        return None

    def done(self, stage, results, w):
        if stage in _GATHER_STAGES:
            for n, a in zip(_GATHER_STAGES[stage], results):
                w[n] = self._weight(n, a)
        elif stage in _SWAP_STAGES:
            self.recv1.update(zip(_SWAP_STAGES[stage], results))
        elif stage in _REDUCE_STAGES:
            self.recv2.update(zip(_REDUCE_STAGES[stage], results))


def kernel(x, mem, a_w_in, a_sgu_ln_g, a_sgu_ln_b, a_w_s, a_b_s, kv_w, kv_b_f, b_w_q, mem_w_kv, w_o, ln_g, ln_b, w_up, w_down, loss_target, m_a_w_in, m_a_sgu_ln_g, m_a_sgu_ln_b, m_a_w_s, m_a_b_s, m_kv_w, m_kv_b_f, m_b_w_q, m_mem_w_kv, m_w_o, m_ln_g, m_ln_b, m_w_up, m_w_down, v_a_w_in, v_a_sgu_ln_g, v_a_sgu_ln_b, v_a_w_s, v_a_b_s, v_kv_w, v_kv_b_f, v_b_w_q, v_mem_w_kv, v_w_o, v_ln_g, v_ln_b, v_w_up, v_w_down):
    xi, yi, ci = lax.axis_index("x"), lax.axis_index("y"), lax.axis_index("c")
    chip = 2 * xi + yi
    sel = jnp.stack([chip, ci]).astype(jnp.int32)
    S, D = x.shape[1], x.shape[2]
    DQ = mem_w_kv.shape[2] // 2
    DM = D - DQ
    NG = DM // HEAD_DIM

    shards = {"wa": (a_w_in, 0), "wkv": (kv_w[None], 0), "wq": (b_w_q, 0), "mkv0": (mem_w_kv, 0), "mkv1": (mem_w_kv, 1),
              "wo0": (w_o, 0), "wo1": (w_o, 1), "wup0": (w_up, 0), "wup1": (w_up, 1), "wdown0": (w_down, 0),
              "wdown1": (w_down, 1)}
    placed = {n: _cast_place(s.reshape(s.shape[0], 2, s.shape[1] // 2, s.shape[2]), layer, sel, "cast_place_" + n)
              for n, (s, layer) in shards.items()}
    ln_pack = jnp.concatenate([ln_g.reshape(4, -1), ln_b.reshape(4, -1)], axis=0)
    ln_buf = lax.dynamic_update_slice(jnp.zeros((N_CHIPS,) + ln_pack.shape, F32), ln_pack[None], (chip, 0, 0))
    ln_shape = (4, D)
    small_shapes = [a_w_s.shape, a_b_s.shape, a_sgu_ln_g.shape, a_sgu_ln_b.shape, kv_b_f.shape, ln_shape, ln_shape,
                    (1, LANES)]
    n_small = -(-_rows_of(small_shapes) // 64) * 64
    hooks = _Staged(placed, sel, DM, NG, {"wa": a_w_in.shape[2], "wkv": kv_w.shape[1]}, n_small)
    weights = {"sg": a_sgu_ln_g, "sb": a_sgu_ln_b, "ws": a_w_s[0], "b_s": a_b_s[0], "bf": kv_b_f}
    got = _run_comm(_gather_weights([placed[n] for n in _GATHER_STAGES["start"]], whole=[ln_buf]),
                    "gather_weights_first")
    hooks.done("start", got[:-1], weights)
    ln_full = got[-1].transpose(1, 0, 2).reshape(8, D)
    weights["ln_g"], weights["ln_b"] = ln_full[:4], ln_full[4:]

    _, grad_x, gr = _local_step(x[0], mem[0], loss_target[0], weights, hooks)

    def chip_sums(groups):
        bufs = []
        for grp in groups:
            buf = None
            for layer, n in enumerate(grp):
                buf = _chip_reduce(hooks.g4[n], hooks.recv1[n], hooks.recv2[n], sel, buf, layer, len(grp),
                                   "reduce_chip_sum_" + n)
            bufs.append(buf)
        return bufs

    whole = lambda a: a.reshape(a.shape[0], 2 * a.shape[3], a.shape[4])
    flat = lambda a: a.reshape(-1, a.shape[-1])

    def update(name, g, w, m, v):
        d, mn, vn = _adamw(flat(g), flat(w), flat(m), flat(v), "adamw_" + name)
        return d.reshape(w.shape), mn.reshape(w.shape), vn.reshape(w.shape)

    hooks.done("end", _run_comm(hooks.plan("end", weights, gr), "reduce_chip_exchange_end"), weights)
    small_mine = _chip_reduce(hooks.g4["small"], hooks.recv1["small"], hooks.recv2["small"], sel, None, 0, 1,
                              "reduce_chip_sum_small", by_chip=True)
    red, small_all = _pair_gather(chip_sums([["wa"], ["wkv"], ["wq"], ["mkv0", "mkv1"], ["wo0", "wo1"],
                                             ["wup0", "wup1"], ["wdown0", "wdown1"]]), small_mine, "reduce_pair_gather")
    g_a_w_in, g_kv_w, g_b_w_q, g_mem_w_kv, g_w_o, g_w_up, g_w_down = [whole(a) for a in red]
    g_kv_w = g_kv_w[0]
    g_w_down = g_w_down.transpose(0, 2, 1)
    small_all = small_all.reshape(n_small, LANES)
    g_ws, g_bs, g_sg, g_sb, g_bf, g_lng, g_lnb, loss_row = _unpack_rows(small_all, small_shapes)
    loss = loss_row[0, 0]
    dsh = D // N_CHIPS
    g_ln_g = lax.dynamic_slice_in_dim(g_lng, chip * dsh, dsh, axis=1).reshape(ln_g.shape)
    g_ln_b = lax.dynamic_slice_in_dim(g_lnb, chip * dsh, dsh, axis=1).reshape(ln_b.shape)

    upd = {
        "a_w_in": update("a_w_in", g_a_w_in, a_w_in, m_a_w_in, v_a_w_in),
        "kv_w": update("kv_w", g_kv_w, kv_w, m_kv_w, v_kv_w),
        "b_w_q": update("b_w_q", g_b_w_q, b_w_q, m_b_w_q, v_b_w_q),
        "mem_w_kv": update("mem_w_kv", g_mem_w_kv, mem_w_kv, m_mem_w_kv, v_mem_w_kv),
        "w_o": update("w_o", g_w_o, w_o, m_w_o, v_w_o),
        "w_up": update("w_up", g_w_up, w_up, m_w_up, v_w_up),
        "w_down": update("w_down", g_w_down, w_down, m_w_down, v_w_down),
    }
    tiny_g = [g_sg, g_sb, g_ws, g_bs, g_bf, g_ln_g, g_ln_b]
    tiny_w = [a_sgu_ln_g, a_sgu_ln_b, a_w_s, a_b_s, kv_b_f, ln_g, ln_b]
    tiny_m = [m_a_sgu_ln_g, m_a_sgu_ln_b, m_a_w_s, m_a_b_s, m_kv_b_f, m_ln_g, m_ln_b]
    tiny_v = [v_a_sgu_ln_g, v_a_sgu_ln_b, v_a_w_s, v_a_b_s, v_kv_b_f, v_ln_g, v_ln_b]
    tiny_shapes = [a.shape for a in tiny_w]
    n_tiny = -(-_rows_of(tiny_shapes) // 8) * 8
    td, tm_, tv = _adamw(_pack_rows(tiny_g, n_tiny), _pack_rows(tiny_w, n_tiny), _pack_rows(tiny_m, n_tiny),
                         _pack_rows(tiny_v, n_tiny), "adamw_small")
    for name, d, mn, vn in zip(["a_sgu_ln_g", "a_sgu_ln_b", "a_w_s", "a_b_s", "kv_b_f", "ln_g", "ln_b"],
                               _unpack_rows(td, tiny_shapes), _unpack_rows(tm_, tiny_shapes),
                               _unpack_rows(tv, tiny_shapes)):
        upd[name] = (d, mn, vn)

    order = ["a_w_in", "a_sgu_ln_g", "a_sgu_ln_b", "a_w_s", "a_b_s", "kv_w", "kv_b_f", "b_w_q", "mem_w_kv", "w_o",
             "ln_g", "ln_b", "w_up", "w_down"]
    grads = {"a_w_in": g_a_w_in.reshape(a_w_in.shape), "a_sgu_ln_g": g_sg.reshape(a_sgu_ln_g.shape),
             "a_sgu_ln_b": g_sb.reshape(a_sgu_ln_b.shape), "a_w_s": g_ws.reshape(a_w_s.shape),
             "a_b_s": g_bs.reshape(a_b_s.shape), "kv_w": g_kv_w, "kv_b_f": g_bf, "b_w_q": g_b_w_q.reshape(b_w_q.shape),
             "mem_w_kv": g_mem_w_kv, "w_o": g_w_o, "ln_g": g_ln_g, "ln_b": g_ln_b, "w_up": g_w_up, "w_down": g_w_down}
    return (loss, grad_x[None], *[grads[n] for n in order], *[upd[n][0] for n in order],
            *[upd[n][1] for n in order], *[upd[n][2] for n in order])
```

```python
import functools
import math

import jax
import jax.numpy as jnp
from jax import lax
from jax.experimental import pallas as pl
from jax.experimental.pallas import tpu as pltpu

F32 = jnp.float32
BF16 = jnp.bfloat16

HEAD_DIM = 64
PAIR = 2 * HEAD_DIM
CHUNK = 128
LN_EPS = 1e-5
ALPHA = 4 ** 0.25
QK_SCALE = 1.0 / math.sqrt(HEAD_DIM)
NEG = -1e30
N_CHIPS = 4
FOX_BLOCK = 256
LANES = 128
VMEM_LIMIT = 48 * 2 ** 20

ADAM_LR, ADAM_B1, ADAM_B2, ADAM_EPS, ADAM_WD, ADAM_STEP = 0.001, 0.9, 0.999, 1e-08, 0.01, 10

MESH = pl.DeviceIdType.MESH
ANY = pl.BlockSpec(memory_space=pl.ANY)


def _dot(a, b):
    return jnp.dot(a, b, preferred_element_type=F32)


def _dot_nt(a, b):
    return lax.dot_general(a, b, (((1,), (1,)), ((), ())), preferred_element_type=F32)


def _dot_tn(a, b):
    return lax.dot_general(a, b, (((0,), (0,)), ((), ())), preferred_element_type=F32)


def _ln_stats(r):
    mu = jnp.mean(r, axis=-1, keepdims=True)
    d = r - mu
    var = jnp.mean(d * d, axis=-1, keepdims=True)
    rstd = lax.rsqrt(var + LN_EPS)
    return d * rstd, rstd


def _ln_bwd(dy, xhat, rstd, g):
    dxh = dy * g
    m1 = jnp.mean(dxh, axis=-1, keepdims=True)
    m2 = jnp.mean(dxh * xhat, axis=-1, keepdims=True)
    return rstd * (dxh - m1 - xhat * m2)


_GELU_K = math.sqrt(2.0 / math.pi)


def _gelu(x):
    return 0.5 * x * (1.0 + jnp.tanh(_GELU_K * (x + 0.044715 * x * x * x)))


def _gelu_grad(x):
    t = jnp.tanh(_GELU_K * (x + 0.044715 * x * x * x))
    return 0.5 * (1.0 + t) + 0.5 * x * (1.0 - t * t) * _GELU_K * (1.0 + 3 * 0.044715 * x * x)


def _lane_mask(e):
    lane = lax.broadcasted_iota(jnp.int32, (1, PAIR), 1)
    return (lane >= HEAD_DIM * e) & (lane < HEAD_DIM * (e + 1))


def _first_head():
    return lax.broadcasted_iota(jnp.int32, (1, PAIR), 1) < HEAD_DIM


def _mem_probs(qh, mkp):
    sc = _dot_nt(qh, mkp) * QK_SCALE
    ex = jnp.exp(sc - jnp.max(sc, axis=-1, keepdims=True))
    return ex / jnp.sum(ex, axis=-1, keepdims=True)


def _mem_attn_fwd(qm_b, mkv_ref, dq_dim):
    outs = []
    for mp in range(dq_dim // PAIR):
        qp = qm_b[:, mp * PAIR:(mp + 1) * PAIR]
        mkp = mkv_ref[:, mp * PAIR:(mp + 1) * PAIR]
        mvp = mkv_ref[:, dq_dim + mp * PAIR:dq_dim + (mp + 1) * PAIR]
        heads = []
        for e in (0, 1):
            qh = jnp.where(_lane_mask(e), qp, jnp.zeros_like(qp))
            p = _mem_probs(qh, mkp)
            heads.append(_dot(p.astype(BF16), mvp))
        outs.append(jnp.where(_first_head(), heads[0], heads[1]))
    return outs


def _mem_attn_bwd(qm_b, mkv_ref, dmo_pairs, dmkv_ref, dq_dim):
    dqs = []
    for mp in range(dq_dim // PAIR):
        ks = slice(mp * PAIR, (mp + 1) * PAIR)
        vs = slice(dq_dim + mp * PAIR, dq_dim + (mp + 1) * PAIR)
        qp = qm_b[:, ks]
        mkp = mkv_ref[:, ks]
        mvp = mkv_ref[:, vs]
        dmo_b = dmo_pairs[mp].astype(BF16)
        dq = None
        dmk = None
        dmv = None
        for e in (0, 1):
            hm = _lane_mask(e)
            qh = jnp.where(hm, qp, jnp.zeros_like(qp))
            p = _mem_probs(qh, mkp)
            doh = jnp.where(hm, dmo_b, jnp.zeros_like(dmo_b))
            dp = _dot_nt(doh, mvp)
            ds = p * (dp - jnp.sum(dp * p, axis=-1, keepdims=True))
            dsb = (ds * QK_SCALE).astype(BF16)
            kh = jnp.where(hm, mkp, jnp.zeros_like(mkp))
            dq_e = _dot(dsb, kh)
            dmk_e = _dot_tn(dsb, qh)
            dmv_e = _dot_tn(p.astype(BF16), doh)
            dq = dq_e if dq is None else dq + dq_e
            dmk = dmk_e if dmk is None else dmk + dmk_e
            dmv = dmv_e if dmv is None else dmv + dmv_e
        dmkv_ref[:, ks] += dmk
        dmkv_ref[:, vs] += dmv
        dqs.append(dq)
    return dqs


def _params(n_axes=1):
    return pltpu.CompilerParams(dimension_semantics=("arbitrary",) * n_axes, vmem_limit_bytes=VMEM_LIMIT)


def _full(shape):
    return pl.BlockSpec(shape, lambda *_: (0,) * len(shape))


def _rows(tm, cols):
    return pl.BlockSpec((tm, cols), lambda i: (i, 0))


def _cols(rows, tm):
    return pl.BlockSpec((rows, tm), lambda i: (0, i))


def _pick(n, pref):
    if n <= pref:
        return n
    best = LANES
    for t in range(LANES, pref + 1, LANES):
        if n % t == 0:
            best = t
    return best if 2 * best >= pref or n > 2 * pref else n


class _Comm:
    def __init__(self, ins, out_shapes, aliases, sem_counts, start, finish):
        self.ins, self.out_shapes, self.aliases, self.sem_counts = list(ins), list(out_shapes), dict(aliases), sem_counts
        self.start, self.finish = start, finish


def _pcall(body, *, name, grid, in_specs, out_specs, out_shape, args, scratch_shapes=(), comm=None, aliases=None):
    n_in, n_out, n_scr = len(in_specs), len(out_shape), len(scratch_shapes)
    aliases = dict(aliases or {})
    if comm is None:
        res = pl.pallas_call(body, name=name, grid=grid, out_shape=tuple(out_shape), in_specs=list(in_specs),
                             out_specs=tuple(out_specs), scratch_shapes=list(scratch_shapes),
                             input_output_aliases=aliases, compiler_params=_params())(*args)
        return tuple(res), ()
    nci, nco = len(comm.ins), len(comm.out_shapes)
    last = grid[0] - 1

    def wrapped(*refs):
        ins, refs = refs[:n_in], refs[n_in:]
        cins, refs = refs[:nci], refs[nci:]
        outs, refs = refs[:n_out], refs[n_out:]
        couts, refs = refs[:nco], refs[nco:]
        scr, sems = refs[:n_scr], refs[n_scr:]

        @pl.when(pl.program_id(0) == 0)
        def _():
            comm.start(cins, couts, sems)

        body(*ins, *outs, *scr)

        @pl.when(pl.program_id(0) == last)
        def _():
            comm.finish(cins, couts, sems)

    res = pl.pallas_call(
        wrapped, name=name, grid=grid, out_shape=tuple(out_shape) + tuple(comm.out_shapes),
        in_specs=list(in_specs) + [ANY] * nci, out_specs=tuple(out_specs) + (ANY,) * nco,
        input_output_aliases={**aliases, **{n_in + a: n_out + b for a, b in comm.aliases.items()}},
        scratch_shapes=list(scratch_shapes) + [pltpu.SemaphoreType.DMA((k,)) for k in comm.sem_counts],
        compiler_params=pltpu.CompilerParams(dimension_semantics=("arbitrary",), vmem_limit_bytes=VMEM_LIMIT,
                                             has_side_effects=True),
    )(*args, *comm.ins)
    return tuple(res[:n_out]), tuple(res[n_out:])


def _run_comm(comm, name):
    nci, nco = len(comm.ins), len(comm.out_shapes)

    def body(*refs):
        cins, couts, sems = refs[:nci], refs[nci:nci + nco], refs[nci + nco:]
        comm.start(cins, couts, sems)
        comm.finish(cins, couts, sems)

    return pl.pallas_call(
        body, name=name, out_shape=tuple(comm.out_shapes), in_specs=[ANY] * nci, out_specs=(ANY,) * nco,
        input_output_aliases=comm.aliases, scratch_shapes=[pltpu.SemaphoreType.DMA((k,)) for k in comm.sem_counts],
        compiler_params=pltpu.CompilerParams(has_side_effects=True),
    )(*comm.ins)


def _mm(a, b, name, a_is_transposed, out_split=1):
    (M, R) = a.shape if a_is_transposed else a.shape[::-1]
    _, N = b.shape
    bm, br = _pick(M, 1024 if a_is_transposed else 512), _pick(R, 1024 if a_is_transposed else 512)
    ncol = N // out_split
    bn = _pick(ncol, 1024)
    per = ncol // bn
    last = R // br - 1

    def body(a_ref, b_ref, o_ref, acc_ref):
        @pl.when(pl.program_id(2) == 0)
        def _():
            acc_ref[...] = jnp.zeros_like(acc_ref)

        acc_ref[...] += (_dot if a_is_transposed else _dot_tn)(a_ref[...], b_ref[...])

        @pl.when(pl.program_id(2) == last)
        def _():
            o_ref[...] = acc_ref[...].astype(BF16)

    if out_split == 1:
        out_shape = jax.ShapeDtypeStruct((M, N), BF16)
        out_spec = pl.BlockSpec((bm, bn), lambda i, j, r: (i, j))
    else:
        out_shape = jax.ShapeDtypeStruct((out_split, M, ncol), BF16)
        out_spec = pl.BlockSpec((None, bm, bn), lambda i, j, r: (j // per, i, j % per))
    a_spec = (pl.BlockSpec((bm, br), lambda i, j, r: (i, r)) if a_is_transposed
              else pl.BlockSpec((br, bm), lambda i, j, r: (r, i)))
    return pl.pallas_call(
        body, name=name, out_shape=out_shape, grid=(M // bm, N // bn, R // br),
        in_specs=[a_spec, pl.BlockSpec((br, bn), lambda i, j, r: (r, j))], out_specs=out_spec,
        scratch_shapes=[pltpu.VMEM((bm, bn), F32)],
        compiler_params=pltpu.CompilerParams(dimension_semantics=("parallel", "parallel", "arbitrary"),
                                             vmem_limit_bytes=VMEM_LIMIT),
    )(a, b)


def _mm_tn(a, b, name, out_split=1):
    return _mm(a, b, name, False, out_split)


def _mm_nn(at, b, name, out_split=1):
    return _mm(at, b, name, True, out_split)


def _mm_small(a, b, name):
    def body(a_ref, b_ref, o_ref):
        o_ref[...] = _dot(a_ref[...], b_ref[...]).astype(BF16)

    return pl.pallas_call(body, name=name, out_shape=jax.ShapeDtypeStruct((a.shape[0], b.shape[1]), BF16),
                          compiler_params=pltpu.CompilerParams(vmem_limit_bytes=VMEM_LIMIT))(a, b)


def _mlp_out(x_ref, wup_ref, wdown_ref, g_ref, b_ref, r_ref, h_ref):
    NC, _, FC = wup_ref.shape
    xv = x_ref[...]
    xb = xv.astype(BF16)
    y = jnp.zeros(xv.shape, F32)
    for c in range(NC):
        a = jnp.maximum(_dot(xb, wup_ref[c]), 0.0)
        hb = (a * a).astype(BF16)
        h_ref[:, c * FC:(c + 1) * FC] = hb
        y = y + _dot(hb, wdown_ref[c])
    r = ALPHA * xv + y
    r_ref[...] = r
    xhat, _ = _ln_stats(r)
    return xhat * g_ref[...] + b_ref[...]


def _mlp_fwd_loss(x, wup4, wdown4, g, b, target, name):
    S, D = x.shape
    NC, _, FC = wup4.shape
    tm = min(256, S)

    def body(x_ref, wup_ref, wdown_ref, g_ref, b_ref, t_ref, d_ref, r_ref, h_ref, l_ref):
        @pl.when(pl.program_id(0) == 0)
        def _():
            l_ref[...] = jnp.zeros_like(l_ref)

        err = _mlp_out(x_ref, wup_ref, wdown_ref, g_ref, b_ref, r_ref, h_ref) - t_ref[...]
        d_ref[...] = err * (1.0 / D)
        l_ref[...] += (0.5 / D) * jnp.sum(err * err)

    return pl.pallas_call(
        body, name=name, grid=(S // tm,),
        out_shape=(jax.ShapeDtypeStruct((S, D), F32), jax.ShapeDtypeStruct((S, D), F32),
                   jax.ShapeDtypeStruct((S, NC * FC), BF16), jax.ShapeDtypeStruct((8, LANES), F32)),
        in_specs=[_rows(tm, D), _full(wup4.shape), _full(wdown4.shape), _full((1, D)), _full((1, D)), _rows(tm, D)],
        out_specs=(_rows(tm, D), _rows(tm, D), _rows(tm, NC * FC), _full((8, LANES))),
        compiler_params=_params(),
    )(x, wup4, wdown4, g, b, target)


def _mlp_fwd(x, wup4, wdown4, g, b, name, comm=None):
    S, D = x.shape
    NC, _, FC = wup4.shape
    tm = min(256, S)

    def body(x_ref, wup_ref, wdown_ref, g_ref, b_ref, xo_ref, xb_ref, xbt_ref, r_ref, h_ref):
        xo = _mlp_out(x_ref, wup_ref, wdown_ref, g_ref, b_ref, r_ref, h_ref)
        xo_ref[...] = xo
        xb_ref[...] = xo.astype(BF16)
        xbt_ref[...] = xo.T.astype(BF16)

    return _pcall(
        body, name=name, grid=(S // tm,), comm=comm,
        out_shape=(jax.ShapeDtypeStruct((S, D), F32), jax.ShapeDtypeStruct((S, D), BF16),
                   jax.ShapeDtypeStruct((D, S), BF16), jax.ShapeDtypeStruct((S, D), F32),
                   jax.ShapeDtypeStruct((S, NC * FC), BF16)),
        in_specs=[_rows(tm, D), _full(wup4.shape), _full(wdown4.shape), _full((1, D)), _full((1, D))],
        out_specs=(_rows(tm, D), _rows(tm, D), _cols(D, tm), _rows(tm, D), _rows(tm, NC * FC)),
        args=(x, wup4, wdown4, g, b))


def _mlp_bwd(dxo, r, h, wup4, wdown4, g, name, comm=None):
    S, D = r.shape
    NC, _, FC = wup4.shape
    tm = min(256, S)

    def body(dxo_ref, r_ref, h_ref, wup_ref, wdown_ref, g_ref, dx_ref, drbt_ref, dpre_ref, dgb_ref):
        @pl.when(pl.program_id(0) == 0)
        def _():
            dgb_ref[...] = jnp.zeros_like(dgb_ref)

        xhat, rstd = _ln_stats(r_ref[...])
        dy = dxo_ref[...]
        dgb_ref[0:1, :] += jnp.sum(dy * xhat, axis=0, keepdims=True)
        dgb_ref[1:2, :] += jnp.sum(dy, axis=0, keepdims=True)
        dr = _ln_bwd(dy, xhat, rstd, g_ref[...])
        drb = dr.astype(BF16)
        drbt_ref[...] = dr.T.astype(BF16)
        dx = ALPHA * dr
        for c in range(NC):
            dh = _dot_nt(drb, wdown_ref[c])
            a = jnp.sqrt(h_ref[:, c * FC:(c + 1) * FC].astype(F32))
            dpre = (2.0 * a * dh).astype(BF16)
            dpre_ref[:, c * FC:(c + 1) * FC] = dpre
            dx = dx + _dot_nt(dpre, wup_ref[c])
        dx_ref[...] = dx

    return _pcall(
        body, name=name, grid=(S // tm,), comm=comm,
        out_shape=(jax.ShapeDtypeStruct((S, D), F32), jax.ShapeDtypeStruct((D, S), BF16),
                   jax.ShapeDtypeStruct((S, NC * FC), BF16), jax.ShapeDtypeStruct((2, D), F32)),
        in_specs=[_rows(tm, D), _rows(tm, D), _rows(tm, NC * FC), _full(wup4.shape), _full(wdown4.shape),
                  _full((1, D))],
        out_specs=(_rows(tm, D), _cols(D, tm), _rows(tm, NC * FC), _full((2, D))),
        args=(dxo, r, h, wup4, wdown4, g))


def _tail_fwd_store(xv, cat_ref, wo_ref, g_ref, b_ref, xo_ref, xbt_ref, r_ref, catt_ref):
    cat = cat_ref[...]
    y = _dot(cat, wo_ref[...])
    r = ALPHA * xv + y
    xhat, _ = _ln_stats(r)
    xo = xhat * g_ref[...] + b_ref[...]
    xo_ref[...] = xo
    xbt_ref[...] = xo.T.astype(BF16)
    catt_ref[...] = cat.astype(F32).T.astype(BF16)
    r_ref[...] = r


def _tail_bwd_head(dxo_ref, r_ref, g_ref, wo_ref, dgb_ref, drb_ref):
    xhat, rstd = _ln_stats(r_ref[...])
    dy = dxo_ref[...]
    dgb_ref[0:1, :] += jnp.sum(dy * xhat, axis=0, keepdims=True)
    dgb_ref[1:2, :] += jnp.sum(dy, axis=0, keepdims=True)
    dr = _ln_bwd(dy, xhat, rstd, g_ref[...])
    drb = dr.astype(BF16)
    drb_ref[...] = drb
    return dr, _dot_nt(drb, wo_ref[...])


def _causal(shape):
    return lax.broadcasted_iota(jnp.int32, shape, 1) <= lax.broadcasted_iota(jnp.int32, shape, 0)


def _sgu_mixed(ws_ref, bst_ref, gp, vpair):
    tril = _causal((CHUNK, CHUNK))
    w0 = jnp.where(tril, ws_ref[2 * gp], 0.0).astype(BF16)
    w1 = jnp.where(tril, ws_ref[2 * gp + 1], 0.0).astype(BF16)
    m0 = _dot(w0, vpair) + bst_ref[:, 2 * gp:2 * gp + 1]
    m1 = _dot(w1, vpair) + bst_ref[:, 2 * gp + 1:2 * gp + 2]
    return jnp.where(_first_head(), m0, m1), w0, w1


def _mixer_a_fwd(x, wa, sg, sb, ws, bst, mkv, wo, g, b, name, comm=None):
    S, D = x.shape
    DQ = mkv.shape[1] // 2
    DM = D - DQ
    NP = DM // PAIR
    tm = min(512, S)

    def body(x_ref, wa_ref, sg_ref, sb_ref, ws_ref, bst_ref, mkv_ref, wo_ref, g_ref, b_ref,
             xo_ref, xbt_ref, r_ref, catt_ref, xint_ref, z_s, zv_s, cat_ref):
        xv = x_ref[...]
        xint_ref[...] = xv.T.astype(BF16)
        z_s[...] = _dot(xv.astype(BF16), wa_ref[...])
        vhat, _ = _ln_stats(_gelu(z_s[:, DM:2 * DM]))
        zv_s[...] = (vhat * sg_ref[...] + sb_ref[...]).astype(BF16)
        for c in range(tm // CHUNK):
            rs = slice(c * CHUNK, (c + 1) * CHUNK)
            for gp in range(NP):
                ls = slice(gp * PAIR, (gp + 1) * PAIR)
                mixed, _, _ = _sgu_mixed(ws_ref, bst_ref, gp, zv_s[rs, ls])
                cat_ref[rs, ls] = (_gelu(z_s[rs, ls]) * mixed).astype(BF16)
        mo = _mem_attn_fwd(z_s[:, 2 * DM:].astype(BF16), mkv_ref, DQ)
        for mp in range(DQ // PAIR):
            cat_ref[:, DM + mp * PAIR:DM + (mp + 1) * PAIR] = mo[mp].astype(BF16)
        _tail_fwd_store(xv, cat_ref, wo_ref, g_ref, b_ref, xo_ref, xbt_ref, r_ref, catt_ref)

    return _pcall(
        body, name=name, grid=(S // tm,), comm=comm,
        out_shape=(jax.ShapeDtypeStruct((S, D), F32), jax.ShapeDtypeStruct((D, S), BF16),
                   jax.ShapeDtypeStruct((S, D), F32), jax.ShapeDtypeStruct((D, S), BF16),
                   jax.ShapeDtypeStruct((D, S), BF16)),
        in_specs=[_rows(tm, D), _full(wa.shape), _full(sg.shape), _full(sb.shape), _full(ws.shape), _full(bst.shape),
                  _full(mkv.shape), _full(wo.shape), _full((1, D)), _full((1, D))],
        out_specs=(_rows(tm, D), _cols(D, tm), _rows(tm, D), _cols(D, tm), _cols(D, tm)),
        scratch_shapes=[pltpu.VMEM((tm, 2 * DM + DQ), F32), pltpu.VMEM((tm, DM), BF16), pltpu.VMEM((tm, D), BF16)],
        args=(x, wa, sg, sb, ws, bst, mkv, wo, g, b))


def _mixer_a_bwd(x, dxo, r, wa, sg, sb, ws, bst, mkv, wo, g, name, comm=None):
    S, D = x.shape
    DQ = mkv.shape[1] // 2
    DM = D - DQ
    NP = DM // PAIR
    NG = DM // HEAD_DIM
    tm = min(512, S)

    def body(x_ref, dxo_ref, r_ref, wa_ref, sg_ref, sb_ref, ws_ref, bst_ref, mkv_ref, wo_ref, g_ref,
             dx_ref, dz_ref, drb_ref, dgb_ref, dsgb_ref, dws_ref, dbst_ref, dmkv_ref, z_s, zv_s, dzv_s):
        @pl.when(pl.program_id(0) == 0)
        def _():
            dgb_ref[...] = jnp.zeros_like(dgb_ref)
            dsgb_ref[...] = jnp.zeros_like(dsgb_ref)
            dws_ref[...] = jnp.zeros_like(dws_ref)
            dbst_ref[...] = jnp.zeros_like(dbst_ref)
            dmkv_ref[...] = jnp.zeros_like(dmkv_ref)

        dr, dcat = _tail_bwd_head(dxo_ref, r_ref, g_ref, wo_ref, dgb_ref, drb_ref)
        z_s[...] = _dot(x_ref[...].astype(BF16), wa_ref[...])
        vhat, vrstd = _ln_stats(_gelu(z_s[:, DM:2 * DM]))
        zv_s[...] = (vhat * sg_ref[...] + sb_ref[...]).astype(BF16)
        tril = _causal((CHUNK, CHUNK))
        lane = lax.broadcasted_iota(jnp.int32, (1, LANES), 1)
        for c in range(tm // CHUNK):
            rs = slice(c * CHUNK, (c + 1) * CHUNK)
            for gp in range(NP):
                ls = slice(gp * PAIR, (gp + 1) * PAIR)
                vpair = zv_s[rs, ls]
                mixed, w0, w1 = _sgu_mixed(ws_ref, bst_ref, gp, vpair)
                u_pre = z_s[rs, ls]
                dmix = dcat[rs, ls]
                dz_ref[rs, ls] = (dmix * mixed * _gelu_grad(u_pre)).astype(BF16)
                dmixed = dmix * _gelu(u_pre)
                first = _first_head()
                d0 = jnp.where(first, dmixed, 0.0)
                d1 = jnp.where(first, 0.0, dmixed)
                d0b = d0.astype(BF16)
                d1b = d1.astype(BF16)
                dzv_s[rs, ls] = _dot_tn(w0, d0b) + _dot_tn(w1, d1b)
                dws_ref[2 * gp] += jnp.where(tril, _dot_nt(d0b, vpair), 0.0)
                dws_ref[2 * gp + 1] += jnp.where(tril, _dot_nt(d1b, vpair), 0.0)
                dbst_ref[...] += (jnp.where(lane == 2 * gp, jnp.sum(d0, axis=-1, keepdims=True), 0.0)
                                  + jnp.where(lane == 2 * gp + 1, jnp.sum(d1, axis=-1, keepdims=True), 0.0))
        dzv = dzv_s[...]
        dsgb_ref[0:1, :] += jnp.sum(dzv * vhat, axis=0, keepdims=True)
        dsgb_ref[1:2, :] += jnp.sum(dzv, axis=0, keepdims=True)
        dgv = _ln_bwd(dzv, vhat, vrstd, sg_ref[...])
        dz_ref[:, DM:2 * DM] = (dgv * _gelu_grad(z_s[:, DM:2 * DM])).astype(BF16)
        dmo = [dcat[:, DM + mp * PAIR:DM + (mp + 1) * PAIR] for mp in range(DQ // PAIR)]
        dqm = _mem_attn_bwd(z_s[:, 2 * DM:].astype(BF16), mkv_ref, dmo, dmkv_ref, DQ)
        for mp in range(DQ // PAIR):
            dz_ref[:, 2 * DM + mp * PAIR:2 * DM + (mp + 1) * PAIR] = dqm[mp].astype(BF16)
        dx_ref[...] = ALPHA * dr + _dot_nt(dz_ref[...], wa_ref[...])

    ZW = 2 * DM + DQ
    M = mkv.shape[0]
    return _pcall(
        body, name=name, grid=(S // tm,), comm=comm,
        out_shape=(jax.ShapeDtypeStruct((S, D), F32), jax.ShapeDtypeStruct((S, ZW), BF16),
                   jax.ShapeDtypeStruct((S, D), BF16), jax.ShapeDtypeStruct((2, D), F32),
                   jax.ShapeDtypeStruct((2, DM), F32), jax.ShapeDtypeStruct((NG, CHUNK, CHUNK), F32),
                   jax.ShapeDtypeStruct((CHUNK, LANES), F32), jax.ShapeDtypeStruct((M, 2 * DQ), F32)),
        in_specs=[_rows(tm, D), _rows(tm, D), _rows(tm, D), _full(wa.shape), _full(sg.shape), _full(sb.shape),
                  _full(ws.shape), _full(bst.shape), _full(mkv.shape), _full(wo.shape), _full((1, D))],
        out_specs=(_rows(tm, D), _rows(tm, ZW), _rows(tm, D), _full((2, D)), _full((2, DM)),
                   _full((NG, CHUNK, CHUNK)), _full((CHUNK, LANES)), _full((M, 2 * DQ))),
        scratch_shapes=[pltpu.VMEM((tm, ZW), F32), pltpu.VMEM((tm, DM), BF16), pltpu.VMEM((tm, DM), F32)],
        args=(x, dxo, r, wa, sg, sb, ws, bst, mkv, wo, g))


def _proj_b_fwd(xb, wq, wkv, dm, name):
    S, D = xb.shape
    DQ = D - dm
    tm = min(512, S)

    def body(x_ref, wq_ref, wkv_ref, q_ref, qm_ref, k_ref, v_ref, fl_ref):
        xv = x_ref[...]
        z = _dot(xv, wq_ref[...])
        q_ref[...] = z[:, :dm].astype(BF16)
        qm_ref[...] = z[:, dm:].astype(BF16)
        kvf = _dot(xv, wkv_ref[...])
        k_ref[...] = kvf[:, :dm].astype(BF16)
        v_ref[...] = kvf[:, dm:2 * dm].astype(BF16)
        fl_ref[...] = kvf[:, 2 * dm:]

    return pl.pallas_call(
        body, name=name, grid=(S // tm,),
        out_shape=(jax.ShapeDtypeStruct((S, dm), BF16), jax.ShapeDtypeStruct((S, DQ), BF16),
                   jax.ShapeDtypeStruct((S, dm), BF16), jax.ShapeDtypeStruct((S, dm), BF16),
                   jax.ShapeDtypeStruct((S, LANES), F32)),
        in_specs=[_rows(tm, D), _full(wq.shape), _full(wkv.shape)],
        out_specs=(_rows(tm, dm), _rows(tm, DQ), _rows(tm, dm), _rows(tm, dm), _rows(tm, LANES)),
        compiler_params=_params(),
    )(xb, wq, wkv)


def _proj_b_bwd(dr, dqz, dkvf, wq, wkv, name):
    S, D = dr.shape
    tm = min(512, S)

    def body(dr_ref, dqz_ref, dkvf_ref, wq_ref, wkv_ref, dx_ref):
        dx_ref[...] = (ALPHA * dr_ref[...] + _dot_nt(dqz_ref[...], wq_ref[...])
                       + _dot_nt(dkvf_ref[...], wkv_ref[...]))

    return pl.pallas_call(
        body, name=name, grid=(S // tm,), out_shape=jax.ShapeDtypeStruct((S, D), F32),
        in_specs=[_rows(tm, D), _rows(tm, dqz.shape[1]), _rows(tm, dkvf.shape[1]), _full(wq.shape), _full(wkv.shape)],
        out_specs=_rows(tm, D), compiler_params=_params(),
    )(dr, dqz, dkvf, wq, wkv)


def _split3(v):
    hi = v.astype(BF16)
    r1 = v - hi.astype(F32)
    mid = r1.astype(BF16)
    lo = (r1 - mid.astype(F32)).astype(BF16)
    return hi, mid, lo


def _tri_sum(v, tri_b):
    hi, mid, lo = _split3(v)
    return _dot(hi, tri_b) + _dot(mid, tri_b) + _dot(lo, tri_b)


def _log_sigmoid(x):
    return jnp.minimum(x, 0.0) - jnp.log(1.0 + jnp.exp(-jnp.abs(x)))


def _forget_cumsum(flt, bf, name):
    H, S = flt.shape
    nchunk = S // LANES

    def body(fl_ref, bf_ref, p_ref):
        upper = (lax.broadcasted_iota(jnp.int32, (LANES, LANES), 0)
                 <= lax.broadcasted_iota(jnp.int32, (LANES, LANES), 1)).astype(BF16)

        def step(n, carry):
            s0 = pl.multiple_of(n * LANES, LANES)
            lf = _log_sigmoid(fl_ref[:, pl.ds(s0, LANES)] + bf_ref[...])
            cs = _tri_sum(lf, upper) + carry
            for i, piece in enumerate(_split3(-cs)):
                p_ref[i, :, pl.ds(s0, LANES)] = piece.astype(F32)
            return cs[:, LANES - 1:LANES]

        lax.fori_loop(0, nchunk, step, jnp.zeros((H, 1), F32))

    return pl.pallas_call(body, name=name, out_shape=jax.ShapeDtypeStruct((3, H, S), F32),
                          compiler_params=pltpu.CompilerParams(vmem_limit_bytes=VMEM_LIMIT))(flt, bf)


def _forget_cumsum_bwd(dct, flt, bf, name):
    H, S = flt.shape
    nchunk = S // LANES

    def body(dc_ref, fl_ref, bf_ref, dfl_ref, dbf_ref):
        lower = (lax.broadcasted_iota(jnp.int32, (LANES, LANES), 0)
                 >= lax.broadcasted_iota(jnp.int32, (LANES, LANES), 1)).astype(BF16)

        def step(n, carry):
            tail, tot = carry
            s0 = pl.multiple_of((nchunk - 1 - n) * LANES, LANES)
            suffix = _tri_sum(dc_ref[:, pl.ds(s0, LANES)], lower) + tail
            xv = fl_ref[:, pl.ds(s0, LANES)] + bf_ref[...]
            dfl = suffix * (1.0 / (1.0 + jnp.exp(xv)))
            dfl_ref[:, pl.ds(s0, LANES)] = dfl
            return suffix[:, 0:1], tot + jnp.sum(dfl, axis=-1, keepdims=True)

        _, tot = lax.fori_loop(0, nchunk, step, (jnp.zeros((H, 1), F32), jnp.zeros((H, 1), F32)))
        dbf_ref[...] = jnp.broadcast_to(tot, (H, LANES))

    return pl.pallas_call(body, name=name,
                          out_shape=(jax.ShapeDtypeStruct((H, S), F32), jax.ShapeDtypeStruct((H, LANES), F32)),
                          compiler_params=pltpu.CompilerParams(vmem_limit_bytes=VMEM_LIMIT))(dct, flt, bf)


N_BIAS = 3


def _bias_ones():
    lane = lax.broadcasted_iota(jnp.int32, (1, PAIR), 1)
    return ((lane & (HEAD_DIM - 1)) < N_BIAS).astype(BF16)


def _fox_fwd(q, k, v, ca, name, comm=None):
    S, DM = q.shape
    NP = DM // PAIR
    T = min(FOX_BLOCK, S)
    nq = S // T

    def body(q_ref, k_ref, v_ref, ca_ref, o_ref, lse_ref):
        tril = _causal((T, T))
        heads = (_lane_mask(0), _lane_mask(1))
        ones3 = _bias_ones()

        def step(carry, qaug, kaug, vaug, masked):
            m, acc = carry
            s = _dot_nt(qaug, kaug)
            if masked:
                s = jnp.where(tril, s, NEG)
            m_new = jnp.maximum(m, jnp.max(s, axis=-1, keepdims=True))
            p = jnp.exp(s - m_new).astype(BF16)
            return m_new, jnp.exp(m - m_new) * acc + _dot(p, vaug)

        def kv_block(kj):
            s0 = pl.multiple_of(kj * T, T)
            kb, vb, cab = k_ref[pl.ds(s0, T), :], v_ref[pl.ds(s0, T), :], ca_ref[pl.ds(s0, T), :]
            return ([jnp.where(heads[e], kb, cab) for e in (0, 1)],
                    [jnp.where(heads[e], vb, jnp.ones_like(vb)) for e in (0, 1)])

        def q_pair(a, _):
            t0s = [pl.multiple_of((2 * a + r) * T, T) for r in (0, 1)]
            qaug = {}
            for r in (0, 1):
                qb = q_ref[pl.ds(t0s[r], T), :] * QK_SCALE
                for e in (0, 1):
                    qaug[r, e] = jnp.where(heads[e], qb, ones3)
            chains = [(r, e) for r in (0, 1) for e in (0, 1)]

            def full(first_block, n_blocks, carries):
                blocks = [kv_block(first_block + kk) for kk in range(n_blocks)]
                scores = [[_dot_nt(qaug[r, e], kaug[e]) for r, e in chains] for kaug, _ in blocks]
                out = []
                for n, (r, e) in enumerate(chains):
                    m, acc = carries[n]
                    top = functools.reduce(jnp.maximum, [scores[kk][n] for kk in range(n_blocks)])
                    m_new = jnp.maximum(m, jnp.max(top, axis=-1, keepdims=True))
                    acc = jnp.exp(m - m_new) * acc
                    for kk in range(n_blocks):
                        acc = acc + _dot(jnp.exp(scores[kk][n] - m_new).astype(BF16), blocks[kk][1][e])
                    out.append((m_new, acc))
                return tuple(out)

            init = (jnp.full((T, 1), NEG, F32), jnp.zeros((T, PAIR), F32))
            carries = lax.fori_loop(0, a // 2, lambda i, c: full(4 * i, 4, c), (init,) * 4)
            carries = list(lax.fori_loop(0, a % 2, lambda i, c: full(2 * a - 2, 2, c), carries))
            kaug, vaug = kv_block(2 * a)
            for i, (r, e) in enumerate(chains):
                carries[i] = step(carries[i], qaug[r, e], kaug[e], vaug[e], r == 0)
            kaug, vaug = kv_block(2 * a + 1)
            for i, (r, e) in enumerate(chains):
                if r == 1:
                    carries[i] = step(carries[i], qaug[r, e], kaug[e], vaug[e], True)
            first = _first_head()
            for r in (0, 1):
                (m0, acc0), (m1, acc1) = carries[2 * r], carries[2 * r + 1]
                l0, l1 = acc0[:, HEAD_DIM:HEAD_DIM + 1], acc1[:, 0:1]
                o_ref[pl.ds(t0s[r], T), :] = jnp.where(first, acc0 / l0, acc1 / l1).astype(BF16)
                lse_t = jnp.where(first, m0 + jnp.log(l0), m1 + jnp.log(l1)).T
                lse_ref[0:1, pl.ds(t0s[r], T)] = lse_t[0:1, :]
                lse_ref[1:2, pl.ds(t0s[r], T)] = lse_t[HEAD_DIM:HEAD_DIM + 1, :]
            return 0

        lax.fori_loop(0, nq // 2, q_pair, 0)

    col = pl.BlockSpec((S, PAIR), lambda hp: (0, hp))
    return _pcall(
        body, name=name, grid=(NP,), comm=comm,
        out_shape=(jax.ShapeDtypeStruct((S, DM), BF16), jax.ShapeDtypeStruct((NP, 2, S), F32)),
        in_specs=[col, col, col, col], out_specs=(col, pl.BlockSpec((None, 2, S), lambda hp: (hp, 0, 0))),
        args=(q, k, v, ca))


def _fox_bwd(q, k, v, o, do, ca, lse, name, comm=None):
    S, DM = q.shape
    NP = DM // PAIR
    T = min(FOX_BLOCK, S)
    nq = S // T

    def body(q_ref, k_ref, v_ref, o_ref, do_ref, ca_ref, lse_ref, dq_ref, dk_ref, dv_ref, dc_ref,
             dq_acc, dcol_acc, dk_acc, dv_acc, dd_s):
        keep = lax.broadcasted_iota(jnp.int32, (T, T), 0) <= lax.broadcasted_iota(jnp.int32, (T, T), 1)
        lane = lax.broadcasted_iota(jnp.int32, (1, LANES), 1)
        heads = (_lane_mask(0), _lane_mask(1))
        row = lax.broadcasted_iota(jnp.int32, (PAIR, 1), 0)
        head_rows = (row < HEAD_DIM, row >= HEAD_DIM)
        ones3 = _bias_ones()
        dq_acc[...] = jnp.zeros_like(dq_acc)
        dcol_acc[...] = jnp.zeros_like(dcol_acc)
        dc_ref[...] = jnp.zeros_like(dc_ref)

        def row_dots(qi, _):
            t0 = pl.multiple_of(qi * T, T)
            prod = do_ref[pl.ds(t0, T), :].astype(F32) * o_ref[pl.ds(t0, T), :].astype(F32)
            cols = [jnp.sum(jnp.where(heads[e], prod, 0.0), axis=-1, keepdims=True) for e in (0, 1)]
            tile = jnp.where(lane == 0, cols[0], jnp.where(lane == 1, cols[1], 0.0))
            dd_s[:, pl.ds(t0, T)] = tile.T[0:8, :]
            return 0

        lax.fori_loop(0, nq, row_dots, 0)

        def kv_pair(b, _):
            dk_acc[...] = jnp.zeros_like(dk_acc)
            dv_acc[...] = jnp.zeros_like(dv_acc)
            s0s = [pl.multiple_of((2 * b + jj) * T, T) for jj in (0, 1)]
            kaug, kst, vbs = {}, {}, []
            for jj in (0, 1):
                kb = k_ref[pl.ds(s0s[jj], T), :]
                cab = ca_ref[pl.ds(s0s[jj], T), :]
                vbs.append(v_ref[pl.ds(s0s[jj], T), :])
                kt = kb.astype(F32).T * QK_SCALE
                for e in (0, 1):
                    kaug[jj, e] = jnp.where(heads[e], kb, cab)
                    kst[jj, e] = jnp.where(head_rows[e], kt, 0.0).astype(BF16)

            def q_step(qi, blocks):
                t0 = pl.multiple_of(qi * T, T)
                qs = q_ref[pl.ds(t0, T), :] * QK_SCALE
                dob = do_ref[pl.ds(t0, T), :]
                doh = [jnp.where(heads[e], dob, jnp.zeros_like(dob)) for e in (0, 1)]
                qst, dot_ = qs.astype(F32).T, dob.astype(F32).T
                qht = [jnp.where(head_rows[e], qst, 0.0).astype(BF16) for e in (0, 1)]
                doht = [jnp.where(head_rows[e], dot_, 0.0).astype(BF16) for e in (0, 1)]
                chains = [(e, jj, masked) for e in (0, 1) for jj, masked in blocks]
                scores = [_dot_nt(kaug[jj, e], jnp.where(heads[e], qs, ones3)) for e, jj, _ in chains]
                dps = [_dot_nt(vbs[jj], doh[e]) for e, jj, _ in chains]
                dqt = jnp.zeros((PAIR, T), F32)
                drow = [jnp.zeros((1, T), F32), jnp.zeros((1, T), F32)]
                for n, (e, jj, masked) in enumerate(chains):
                    st = jnp.where(keep, scores[n], NEG) if masked else scores[n]
                    pt = jnp.exp(st - lse_ref[e:e + 1, pl.ds(t0, T)])
                    dv_acc[jj] += _dot_nt(doht[e], pt.astype(BF16))
                    dst = pt * (dps[n] - dd_s[e:e + 1, pl.ds(t0, T)])
                    dsb = dst.astype(BF16)
                    dk_acc[jj] += _dot_nt(qht[e], dsb)
                    dqt = dqt + _dot(kst[jj, e], dsb)
                    dcol_acc[pl.ds(s0s[jj], T), :] -= jnp.where(lane == e, jnp.sum(dst, axis=-1, keepdims=True), 0.0)
                    drow[e] = drow[e] + jnp.sum(dst, axis=0, keepdims=True)
                for e in (0, 1):
                    dc_ref[e:e + 1, pl.ds(t0, T)] += drow[e]
                dq_acc[:, pl.ds(t0, T)] += dqt

            q_step(2 * b, [(0, True)])
            q_step(2 * b + 1, [(0, False), (1, True)])

            def rest(i, _):
                q_step(2 * b + 2 + 2 * i, [(0, False), (1, False)])
                q_step(2 * b + 3 + 2 * i, [(0, False), (1, False)])
                return 0

            lax.fori_loop(0, (nq - 2) // 2 - b, rest, 0)
            for jj in (0, 1):
                dk_ref[pl.ds(s0s[jj], T), :] = dk_acc[jj].T.astype(BF16)
                dv_ref[pl.ds(s0s[jj], T), :] = dv_acc[jj].T.astype(BF16)
            return 0

        lax.fori_loop(0, nq // 2, kv_pair, 0)

        def add_cols(kj, _):
            s0 = pl.multiple_of(kj * T, T)
            dc_ref[:, pl.ds(s0, T)] += dcol_acc[pl.ds(s0, T), :].T[0:2, :]
            dq_ref[pl.ds(s0, T), :] = dq_acc[:, pl.ds(s0, T)].T.astype(BF16)
            return 0

        lax.fori_loop(0, nq, add_cols, 0)

    col = pl.BlockSpec((S, PAIR), lambda hp: (0, hp))
    row2 = pl.BlockSpec((None, 2, S), lambda hp: (hp, 0, 0))
    return _pcall(
        body, name=name, grid=(NP,), comm=comm,
        out_shape=(jax.ShapeDtypeStruct(do.shape, BF16), jax.ShapeDtypeStruct((S, DM), BF16),
                   jax.ShapeDtypeStruct((S, DM), BF16), jax.ShapeDtypeStruct((NP, 2, S), F32)),
        in_specs=[col, col, col, col, col, col, row2], out_specs=(col, col, col, row2),
        scratch_shapes=[pltpu.VMEM((PAIR, S), F32), pltpu.VMEM((S, LANES), F32), pltpu.VMEM((2, PAIR, T), F32),
                        pltpu.VMEM((2, PAIR, T), F32), pltpu.VMEM((8, S), F32)],
        args=(q, k, v, o, do, ca, lse), aliases={4: 0})


def _mixer_b_fwd(x, o, qm, mkv, wo, g, b, name):
    S, D = x.shape
    DQ = qm.shape[1]
    DM = D - DQ
    tm = min(512, S)

    def body(x_ref, o_ref, qm_ref, mkv_ref, wo_ref, g_ref, b_ref, xo_ref, xbt_ref, r_ref, catt_ref, cat_ref):
        cat_ref[:, :DM] = o_ref[...]
        mo = _mem_attn_fwd(qm_ref[...], mkv_ref, DQ)
        for mp in range(DQ // PAIR):
            cat_ref[:, DM + mp * PAIR:DM + (mp + 1) * PAIR] = mo[mp].astype(BF16)
        _tail_fwd_store(x_ref[...], cat_ref, wo_ref, g_ref, b_ref, xo_ref, xbt_ref, r_ref, catt_ref)

    return pl.pallas_call(
        body, name=name, grid=(S // tm,),
        out_shape=(jax.ShapeDtypeStruct((S, D), F32), jax.ShapeDtypeStruct((D, S), BF16),
                   jax.ShapeDtypeStruct((S, D), F32), jax.ShapeDtypeStruct((D, S), BF16)),
        in_specs=[_rows(tm, D), _rows(tm, DM), _rows(tm, DQ), _full(mkv.shape), _full(wo.shape), _full((1, D)),
                  _full((1, D))],
        out_specs=(_rows(tm, D), _cols(D, tm), _rows(tm, D), _cols(D, tm)),
        scratch_shapes=[pltpu.VMEM((tm, D), BF16)],
        compiler_params=_params(),
    )(x, o, qm, mkv, wo, g, b)


def _mixer_b_bwd(dxo, r, qm, mkv, wo, g, name, comm=None):
    S, D = r.shape
    DQ = qm.shape[1]
    DM = D - DQ
    M = mkv.shape[0]
    tm = min(512, S)

    def body(dxo_ref, r_ref, qm_ref, mkv_ref, wo_ref, g_ref, dr_ref, drb_ref, dcat_ref, dgb_ref, dmkv_ref):
        @pl.when(pl.program_id(0) == 0)
        def _():
            dgb_ref[...] = jnp.zeros_like(dgb_ref)
            dmkv_ref[...] = jnp.zeros_like(dmkv_ref)

        dr, dcat = _tail_bwd_head(dxo_ref, r_ref, g_ref, wo_ref, dgb_ref, drb_ref)
        dr_ref[...] = dr
        dcat_ref[:, :DM] = dcat[:, :DM].astype(BF16)
        dmo = [dcat[:, DM + mp * PAIR:DM + (mp + 1) * PAIR] for mp in range(DQ // PAIR)]
        dqm = _mem_attn_bwd(qm_ref[...], mkv_ref, dmo, dmkv_ref, DQ)
        for mp in range(DQ // PAIR):
            dcat_ref[:, DM + mp * PAIR:DM + (mp + 1) * PAIR] = dqm[mp].astype(BF16)

    return _pcall(
        body, name=name, grid=(S // tm,), comm=comm,
        out_shape=(jax.ShapeDtypeStruct((S, D), F32), jax.ShapeDtypeStruct((S, D), BF16),
                   jax.ShapeDtypeStruct((S, D), BF16), jax.ShapeDtypeStruct((2, D), F32),
                   jax.ShapeDtypeStruct((M, 2 * DQ), F32)),
        in_specs=[_rows(tm, D), _rows(tm, D), _rows(tm, DQ), _full(mkv.shape), _full(wo.shape), _full((1, D))],
        out_specs=(_rows(tm, D), _rows(tm, D), _rows(tm, D), _full((2, D)), _full((M, 2 * DQ))),
        args=(dxo, r, qm, mkv, wo, g))


class _NoExchange:
    def plan(self, stage, w, grads):
        return None

    def done(self, stage, results, w):
        pass


def _local_step(x, mem, target, w, hooks):
    S, D = x.shape
    DQ = w["mkv0"].shape[1] // 2
    DM = D - DQ
    NG = DM // HEAD_DIM
    NP = DM // PAIR
    ln_g, ln_b = w["ln_g"], w["ln_b"]
    row = lambda a, i: a[i:i + 1]
    memb = mem.astype(BF16)
    bst =jnp.pad(w["b_s"].T, ((0, 0), (0, LANES - NG)))
    grads = {}

    def staged(stage, fn, *args):
        out, got = fn(*args, stage, comm=hooks.plan(stage, w, grads))
        hooks.done(stage, got, w)
        return out

    mkv0 = _mm_small(memb, w["mkv0"], "mem_kv_0")
    x1, x1t, r1, cat0t, x0t = staged("mixer_a_fwd", _mixer_a_fwd, x, w["wa"], w["sg"], w["sb"], w["ws"], bst, mkv0,
                                w["wo0"], row(ln_g, 0), row(ln_b, 0))
    x2, x2b, x2t, r2, h0 = staged("mlp_fwd_0", _mlp_fwd, x1, w["wup0"], w["wdown0"], row(ln_g, 1), row(ln_b, 1))
    q, qm, k, v, fl = _proj_b_fwd(x2b, w["wq"], w["wkv"], DM, "proj_b_fwd")
    flt = fl[:, :16].T
    bfc = jnp.pad(w["bf"], (0, 16 - NG)).reshape(16, 1)
    pieces = _forget_cumsum(flt, bfc, "forget_cumsum")
    pieces = jnp.pad(pieces[:, :NG].transpose(2, 1, 0), ((0, 0), (0, 0), (0, HEAD_DIM - N_BIAS)))
    ca = pieces.reshape(S, NP, 2, HEAD_DIM)[:, :, ::-1].reshape(S, DM).astype(BF16)
    o, lse = staged("fox_fwd", _fox_fwd, q, k, v, ca)
    mkv1 = _mm_small(memb, w["mkv1"], "mem_kv_1")
    x3, x3t, r3, cat1t = _mixer_b_fwd(x2, o, qm, mkv1, w["wo1"], row(ln_g, 2), row(ln_b, 2), "mixer_b_fwd")
    dx4, r4, h1, loss_part = _mlp_fwd_loss(x3, w["wup1"], w["wdown1"], row(ln_g, 3), row(ln_b, 3), target,
                                           "mlp_fwd_1")

    nc = w["wup0"].shape[0]
    (dx3, dr4t, dpre1, dgb11), _ = _mlp_bwd(dx4, r4, h1, w["wup1"], w["wdown1"], row(ln_g, 3), "mlp_bwd_1")
    grads["wdown1"] = _mm_nn(dr4t, h1, "dw_down_1", out_split=nc)
    grads["wup1"] = _mm_nn(x3t, dpre1, "dw_up_1", out_split=nc)
    dr3, dr3b, dcat1, dgb10, dmkv1 = staged("mixer_b_bwd", _mixer_b_bwd, dx3, r3, qm, mkv1, w["wo1"], row(ln_g, 2))
    dqz, dk, dv, dct3 = staged("fox_bwd", _fox_bwd, q, k, v, o, dcat1, ca, lse)
    dct = jnp.pad(dct3.reshape(NG, S), ((0, 16 - NG), (0, 0)))
    dflt, dbf = _forget_cumsum_bwd(dct, flt, bfc, "forget_cumsum_bwd")
    dfl = jnp.pad(dflt.T, ((0, 0), (0, LANES - 16))).astype(BF16)
    dkvf = jnp.concatenate([dk, dv, dfl], axis=1)
    dx2 = _proj_b_bwd(dr3, dqz, dkvf, w["wq"], w["wkv"], "proj_b_bwd")
    grads["wo1"] = _mm_nn(cat1t, dr3b, "dw_o_1")
    grads["wq"] = _mm_nn(x2t, dqz, "dw_q")
    grads["wkv"] = _mm_nn(x2t, dkvf, "dw_kv")
    grads["mkv1"] = _mm_tn(memb, dmkv1.astype(BF16), "dw_mkv_1")
    dx1, dr2t, dpre0, dgb01 = staged("mlp_bwd_0", _mlp_bwd, dx2, r2, h0, w["wup0"], w["wdown0"], row(ln_g, 1))
    grads["wdown0"] = _mm_nn(dr2t, h0, "dw_down_0", out_split=nc)
    grads["wup0"] = _mm_nn(x1t, dpre0, "dw_up_0", out_split=nc)
    dx0, dz, dr1b, dgb00, dsgb, dws, dbst, dmkv0 = staged(
        "mixer_a_bwd", _mixer_a_bwd, x, dx1, r1, w["wa"], w["sg"], w["sb"], w["ws"], bst, mkv0, w["wo0"], row(ln_g, 0))
    grads["wo0"] = _mm_nn(cat0t, dr1b, "dw_o_0")
    grads["wa"] = _mm_nn(x0t, dz, "dw_a")
    grads["mkv0"] = _mm_tn(memb, dmkv0.astype(BF16), "dw_mkv_0")
    grads.update({
        "ws": dws, "b_s": dbst[:, :NG].T, "sg": dsgb[0:1], "sb": dsgb[1:2], "bf": dbf[:NG, 0], "loss": loss_part[0:1],
        "ln_g": jnp.concatenate([dgb00[0:1], dgb01[0:1], dgb10[0:1], dgb11[0:1]], axis=0),
        "ln_b": jnp.concatenate([dgb00[1:2], dgb01[1:2], dgb10[1:2], dgb11[1:2]], axis=0),
    })
    return loss_part, dx0, grads


def _place():
    xi, yi, ci = lax.axis_index("x"), lax.axis_index("y"), lax.axis_index("c")
    peers = [(xi, 1 - yi), (1 - xi, yi), (1 - xi, 1 - yi)]
    return xi, yi, ci, peers


def _comm_call(body, name, ins, out_shapes, n_remote, n_local):
    return pl.pallas_call(
        body, name=name, out_shape=out_shapes, in_specs=[ANY] * len(ins), out_specs=[ANY] * len(out_shapes),
        scratch_shapes=[pltpu.SemaphoreType.DMA((n_remote,)), pltpu.SemaphoreType.DMA((n_remote,)),
                        pltpu.SemaphoreType.DMA((max(n_local, 1),))],
        compiler_params=pltpu.CompilerParams(has_side_effects=True),
    )(*ins)


def _pair_swap(xs):
    n = len(xs)
    out_shapes = [jax.ShapeDtypeStruct((a.shape[0],) + a.shape[2:], a.dtype) for a in xs]

    def copies(x_refs, o_refs, sems):
        send_sems, recv_sems = sems
        xi, yi, ci, _ = _place()
        return [pltpu.make_async_remote_copy(src_ref=x_refs[i].at[:, 1 - ci], dst_ref=o_refs[i],
                                             send_sem=send_sems.at[i], recv_sem=recv_sems.at[i],
                                             device_id=(xi, yi, 1 - ci), device_id_type=MESH) for i in range(n)]

    def start(x_refs, o_refs, sems):
        for cp in copies(x_refs, o_refs, sems):
            cp.start()

    def finish(x_refs, o_refs, sems):
        cps = copies(x_refs, o_refs, sems)
        for cp in cps:
            cp.wait_recv()
        for cp in cps:
            cp.wait_send()

    return _Comm(xs, out_shapes, {}, [n, n], start, finish)


def _chip_exchange(xs):
    n = len(xs)
    out_shapes = [jax.ShapeDtypeStruct((3,) + a.shape[1:], a.dtype) for a in xs]

    def copies(x_refs, o_refs, sems):
        send_sems, recv_sems = sems
        xi, yi, ci, peers = _place()
        return [pltpu.make_async_remote_copy(src_ref=x_refs[i].at[2 * px + py], dst_ref=o_refs[i].at[d],
                                             send_sem=send_sems.at[3 * i + d], recv_sem=recv_sems.at[3 * i + d],
                                             device_id=(px, py, ci), device_id_type=MESH)
                for i in range(n) for d, (px, py) in enumerate(peers)]

    def start(x_refs, o_refs, sems):
        for cp in copies(x_refs, o_refs, sems):
            cp.start()

    def finish(x_refs, o_refs, sems):
        cps = copies(x_refs, o_refs, sems)
        for cp in cps:
            cp.wait_recv()
        for cp in cps:
            cp.wait_send()

    return _Comm(xs, out_shapes, {}, [3 * n, 3 * n], start, finish)


def _inplace_call(body, name, bufs, sem_counts):
    n = len(bufs)
    return pl.pallas_call(
        body, name=name, out_shape=[jax.ShapeDtypeStruct(a.shape, a.dtype) for a in bufs],
        in_specs=[ANY] * n, out_specs=[ANY] * n, input_output_aliases={i: i for i in range(n)},
        scratch_shapes=[pltpu.SemaphoreType.DMA((k,)) for k in sem_counts],
        compiler_params=pltpu.CompilerParams(has_side_effects=True),
    )(*bufs)


def _gather_weights(bufs, whole=()):
    n, nw = len(bufs), len(whole)

    def ici(w, sems, i, d, px, py, ci, slot):
        ref = w[i].at[slot, ci] if i < n else w[i].at[slot]
        return pltpu.make_async_remote_copy(src_ref=ref, dst_ref=ref, send_sem=sems[0].at[3 * i + d],
                                            recv_sem=sems[1].at[3 * i + d], device_id=(px, py, ci), device_id_type=MESH)

    def d2d(w, sems, i, d, xi, yi, ci, slot, half):
        ref = w[i].at[slot, half]
        return pltpu.make_async_remote_copy(src_ref=ref, dst_ref=ref, send_sem=sems[2].at[3 * i + d],
                                            recv_sem=sems[3].at[3 * i + d], device_id=(xi, yi, 1 - ci),
                                            device_id_type=MESH)

    def start(_, w, sems):
        xi, yi, ci, peers = _place()
        for i in range(n + nw):
            for d, (px, py) in enumerate(peers):
                ici(w, sems, i, d, px, py, ci, 2 * xi + yi).start()

    def finish(_, w, sems):
        xi, yi, ci, peers = _place()
        for d, (px, py) in enumerate(peers):
            for i in range(n + nw):
                ici(w, sems, i, d, px, py, ci, 2 * px + py).wait_recv()
                if i < n:
                    d2d(w, sems, i, d, xi, yi, ci, 2 * px + py, ci).start()
        for d, (px, py) in enumerate(peers):
            for i in range(n):
                d2d(w, sems, i, d, xi, yi, ci, 2 * px + py, 1 - ci).wait_recv()
        for d, (px, py) in enumerate(peers):
            for i in range(n + nw):
                ici(w, sems, i, d, px, py, ci, 2 * xi + yi).wait_send()
                if i < n:
                    d2d(w, sems, i, d, xi, yi, ci, 2 * px + py, ci).wait_send()

    arrays = list(bufs) + list(whole)
    k = 3 * (n + nw)
    return _Comm(arrays, [jax.ShapeDtypeStruct(a.shape, a.dtype) for a in arrays], {i: i for i in range(n + nw)},
                 [k, k, max(3 * n, 1), max(3 * n, 1)], start, finish)


def _pair_gather(bufs, spread, name):
    n = len(bufs)
    where = [(i, l) for i, a in enumerate(bufs) for l in range(a.shape[0])]
    nw = len(where)
    arrays = list(bufs) + ([] if spread is None else [spread])
    na = len(arrays)

    def body(*refs):
        g = refs[na:na + n]
        sp = refs[na + n] if spread is not None else None
        send_sems, recv_sems = refs[2 * na:]
        xi, yi, ci, peers = _place()
        chips = [(xi, yi)] + peers
        others = [(d, f) for d in range(4) for f in (0, 1) if (d, f) != (0, 0)] if spread is not None else []
        sends = []
        for k, (i, l) in enumerate(where):
            mine = g[i].at[l, :, ci]
            cp = pltpu.make_async_remote_copy(src_ref=mine, dst_ref=mine, send_sem=send_sems.at[k],
                                              recv_sem=recv_sems.at[k], device_id=(xi, yi, 1 - ci), device_id_type=MESH)
            cp.start()
            sends.append(cp)
        for k, (d, f) in enumerate(others):
            mine = sp.at[2 * xi + yi, ci]
            cp = pltpu.make_async_remote_copy(src_ref=mine, dst_ref=mine, send_sem=send_sems.at[nw + k],
                                              recv_sem=recv_sems.at[nw + k],
                                              device_id=(chips[d][0], chips[d][1], ci if f == 0 else 1 - ci),
                                              device_id_type=MESH)
            cp.start()
            sends.append(cp)
        for k, (i, l) in enumerate(where):
            other = g[i].at[l, :, 1 - ci]
            pltpu.make_async_remote_copy(src_ref=other, dst_ref=other, send_sem=send_sems.at[k],
                                         recv_sem=recv_sems.at[k], device_id=(xi, yi, 1 - ci),
                                         device_id_type=MESH).wait_recv()
        for k, (d, f) in enumerate(others):
            px, py, pc = chips[d][0], chips[d][1], (ci if f == 0 else 1 - ci)
            theirs = sp.at[2 * px + py, pc]
            pltpu.make_async_remote_copy(src_ref=theirs, dst_ref=theirs, send_sem=send_sems.at[nw + k],
                                         recv_sem=recv_sems.at[nw + k], device_id=(px, py, pc),
                                         device_id_type=MESH).wait_recv()
        for cp in sends:
            cp.wait_send()

    n_sems = nw + (7 if spread is not None else 0)
    out = _inplace_call(body, name, arrays, [n_sems, n_sems])
    return out[:n], (out[n] if spread is not None else None)


def _row_block(rows, cols):
    want = max(8, (2 ** 18 // max(cols, 1)) // 8 * 8)
    if rows <= want:
        return rows
    best = 8
    for t in range(8, want + 1, 8):
        if rows % t == 0:
            best = t
    return best


def _pair_add(g4, recv, sel, out_dtype, name):
    A, _, H, C = g4.shape
    bh = _row_block(H, C)

    def body(sel_ref, g_ref, r_ref, o_ref):
        o_ref[...] = (g_ref[...].astype(F32) + r_ref[...].astype(F32)).astype(out_dtype)

    return pl.pallas_call(
        body, name=name, out_shape=jax.ShapeDtypeStruct((A, H, C), out_dtype),
        grid_spec=pltpu.PrefetchScalarGridSpec(
            num_scalar_prefetch=1, grid=(A, H // bh),
            in_specs=[pl.BlockSpec((None, None, bh, C), lambda a, i, s: (a, s[1], i, 0)),
                      pl.BlockSpec((None, bh, C), lambda a, i, s: (a, i, 0))],
            out_specs=pl.BlockSpec((None, bh, C), lambda a, i, s: (a, i, 0))),
        compiler_params=pltpu.CompilerParams(dimension_semantics=("arbitrary", "arbitrary"),
                                             vmem_limit_bytes=VMEM_LIMIT),
    )(sel, g4, recv)


def _chip_reduce(g4, recv1, recv2, sel, buf, layer, n_layers, name, by_chip=False):
    _, _, H, C = g4.shape
    bh = _row_block(H, C)
    if by_chip:
        out_shape = jax.ShapeDtypeStruct((N_CHIPS, 2, H, C), F32)
        out_spec = pl.BlockSpec((None, None, bh, C), lambda i, s: (s[0], s[1], i, 0))
    else:
        out_shape = jax.ShapeDtypeStruct((n_layers, 1, 2, H, C), F32)
        out_spec = pl.BlockSpec((None, None, None, bh, C), lambda i, s: (layer, 0, s[1], i, 0))

    def body(sel_ref, g_ref, r1_ref, r2_ref, *rest):
        acc = g_ref[...].astype(F32) + r1_ref[...].astype(F32)
        for d in range(3):
            acc = acc + r2_ref[d].astype(F32)
        rest[-1][...] = acc

    in_specs = [pl.BlockSpec((None, None, bh, C), lambda i, s: (s[0], s[1], i, 0)),
                pl.BlockSpec((None, bh, C), lambda i, s: (s[0], i, 0)),
                pl.BlockSpec((3, bh, C), lambda i, s: (0, i, 0))]
    args = [sel, g4, recv1, recv2]
    aliases = {}
    if buf is not None:
        in_specs.append(ANY)
        args.append(buf)
        aliases = {4: 0}
    return pl.pallas_call(
        body, name=name, out_shape=out_shape,
        grid_spec=pltpu.PrefetchScalarGridSpec(num_scalar_prefetch=1, grid=(H // bh,), in_specs=in_specs,
                                               out_specs=out_spec),
        input_output_aliases=aliases,
        compiler_params=pltpu.CompilerParams(dimension_semantics=("arbitrary",), vmem_limit_bytes=VMEM_LIMIT),
    )(*args)


def _cast_place(shards, layer, sel, name):
    _, _, H, C = shards.shape
    bh = _row_block(H, C)

    def body(sel_ref, x_ref, o_ref):
        o_ref[...] = x_ref[...].astype(BF16)

    return pl.pallas_call(
        body, name=name, out_shape=jax.ShapeDtypeStruct((N_CHIPS, 2, H, C), BF16),
        grid_spec=pltpu.PrefetchScalarGridSpec(
            num_scalar_prefetch=1, grid=(2, H // bh),
            in_specs=[pl.BlockSpec((None, None, bh, C), lambda h, i, s: (layer, h, i, 0))],
            out_specs=pl.BlockSpec((None, None, bh, C), lambda h, i, s: (s[0], h, i, 0))),
        compiler_params=pltpu.CompilerParams(dimension_semantics=("arbitrary", "arbitrary"),
                                             vmem_limit_bytes=VMEM_LIMIT),
    )(sel, shards)


def _adamw_many(quads, name, comm=None):
    R, C = quads[0][0].shape
    br = _row_block(R, C)
    n = len(quads)

    def body(*refs):
        for k in range(n):
            g_ref, w_ref, m_ref, v_ref = refs[4 * k:4 * k + 4]
            d_ref, mo_ref, vo_ref = refs[4 * n + 3 * k:4 * n + 3 * k + 3]
            gv = g_ref[...]
            mn = ADAM_B1 * m_ref[...] + (1.0 - ADAM_B1) * gv
            vn = ADAM_B2 * v_ref[...] + (1.0 - ADAM_B2) * (gv * gv)
            m_hat = mn / (1.0 - ADAM_B1 ** ADAM_STEP)
            v_hat = vn / (1.0 - ADAM_B2 ** ADAM_STEP)
            d_ref[...] = -ADAM_LR * (m_hat / (jnp.sqrt(v_hat) + ADAM_EPS) + ADAM_WD * w_ref[...])
            mo_ref[...] = mn
            vo_ref[...] = vn

    spec = pl.BlockSpec((br, C), lambda i: (i, 0))
    res, got = _pcall(body, name=name, grid=(R // br,), comm=comm, out_shape=(jax.ShapeDtypeStruct((R, C), F32),) * (3 * n),
                      in_specs=[spec] * (4 * n), out_specs=(spec,) * (3 * n), args=[a for q in quads for a in q])
    return [res[3 * k:3 * k + 3] for k in range(n)], got


def _adamw(g, w, m, v, name):
    return _adamw_many([(g, w, m, v)], name)[0][0]


def _part_rows(shape):
    return -(-math.prod(shape) // (8 * LANES)) * 8


def _pack_rows(arrs, total_rows):
    parts = []
    for a in arrs:
        flat = a.reshape(-1)
        rows = _part_rows(a.shape)
        parts.append(jnp.pad(flat, (0, rows * LANES - flat.shape[0])).reshape(rows, LANES))
    packed = jnp.concatenate(parts, axis=0)
    return jnp.pad(packed, ((0, total_rows - packed.shape[0]), (0, 0)))


def _unpack_rows(packed, shapes):
    out, r = [], 0
    for shp in shapes:
        size, rows = math.prod(shp), _part_rows(shp)
        out.append(packed[r:r + rows].reshape(-1)[:size].reshape(shp))
        r += rows
    return out


def _rows_of(shapes):
    return sum(_part_rows(s) for s in shapes)


_GATHER_STAGES = {"start": ["wa", "mkv0", "wo0"], "mixer_a_fwd": ["wup0", "wdown0"],
                  "mlp_fwd_0": ["wq", "wkv", "mkv1", "wo1"], "fox_fwd": ["wup1", "wdown1"]}
_SWAP_STAGES = {"mixer_b_bwd": ["wup1", "wdown1"]}
_REDUCE_STAGES = {"fox_bwd": ["wup1", "wdown1"], "mlp_bwd_0": ["wo1", "wq", "wkv", "mkv1"],
                  "mixer_a_bwd": ["wup0", "wdown0"], "end": ["wa", "wo0", "mkv0", "small"]}
_SMALL = ["ws", "b_s", "sg", "sb", "bf", "ln_g", "ln_b", "loss"]


class _Staged:
    def __init__(self, placed, sel, dm, ng, shard_cols, n_small):
        self.placed, self.sel, self.dm, self.ng, self.shard_cols, self.n_small = placed, sel, dm, ng, shard_cols, n_small
        self.g4, self.recv1, self.recv2 = {}, {}, {}

    def _weight(self, name, a):
        a = a.reshape(N_CHIPS, 2 * a.shape[2], a.shape[3])
        if name in ("wa", "wkv"):
            a = a.transpose(1, 0, 2).reshape(a.shape[1], -1)
            if name == "wkv":
                a = jnp.concatenate([a[:, :2 * self.dm],
                                     jnp.pad(a[:, 2 * self.dm:], ((0, 0), (0, LANES - self.ng)))], axis=1)
            return a
        if name.startswith("wup") or name.startswith("wdown"):
            return a
        return a.reshape(-1, a.shape[2])

    def _partial(self, name, grads):
        if name == "small":
            a = _pack_rows([grads[n] for n in _SMALL], self.n_small).reshape(N_CHIPS, -1, LANES)
        elif name in self.shard_cols:
            n = self.shard_cols[name]
            g = grads[name][:, :N_CHIPS * n]
            a = g.reshape(g.shape[0], N_CHIPS, n).transpose(1, 0, 2)
        elif name.startswith("wup") or name.startswith("wdown"):
            a = grads[name]
        else:
            g = grads[name]
            a = g.reshape(N_CHIPS, g.shape[0] // N_CHIPS, g.shape[1])
        return a.reshape(N_CHIPS, 2, a.shape[1] // 2, a.shape[2])

    def plan(self, stage, w, grads):
        if stage in _GATHER_STAGES:
            return _gather_weights([self.placed[n] for n in _GATHER_STAGES[stage]])
        if stage in _SWAP_STAGES:
            for n in _SWAP_STAGES[stage]:
                self.g4[n] = self._partial(n, grads)
            return _pair_swap([self.g4[n] for n in _SWAP_STAGES[stage]])
        if stage in _REDUCE_STAGES:
            names = _REDUCE_STAGES[stage]
            late = [n for n in names if n not in self.recv1]
            for n in late:
                self.g4[n] = self._partial(n, grads)
            if late:
                got = _run_comm(_pair_swap([self.g4[n] for n in late]), "reduce_pair_swap_" + stage)
                self.recv1.update(zip(late, got))
            return _chip_exchange([_pair_add(self.g4[n], self.recv1[n], self.sel, F32 if n == "small" else BF16,
                                             "reduce_pair_add_" + n) for n in names])
        return None

    def done(self, stage, results, w):
        if stage in _GATHER_STAGES:
            for n, a in zip(_GATHER_STAGES[stage], results):
                w[n] = self._weight(n, a)
        elif stage in _SWAP_STAGES:
            self.recv1.update(zip(_SWAP_STAGES[stage], results))
        elif stage in _REDUCE_STAGES:
            self.recv2.update(zip(_REDUCE_STAGES[stage], results))


def kernel(x, mem, a_w_in, a_sgu_ln_g, a_sgu_ln_b, a_w_s, a_b_s, kv_w, kv_b_f, b_w_q, mem_w_kv, w_o, ln_g, ln_b, w_up, w_down, loss_target, m_a_w_in, m_a_sgu_ln_g, m_a_sgu_ln_b, m_a_w_s, m_a_b_s, m_kv_w, m_kv_b_f, m_b_w_q, m_mem_w_kv, m_w_o, m_ln_g, m_ln_b, m_w_up, m_w_down, v_a_w_in, v_a_sgu_ln_g, v_a_sgu_ln_b, v_a_w_s, v_a_b_s, v_kv_w, v_kv_b_f, v_b_w_q, v_mem_w_kv, v_w_o, v_ln_g, v_ln_b, v_w_up, v_w_down):
    xi, yi, ci = lax.axis_index("x"), lax.axis_index("y"), lax.axis_index("c")
    chip = 2 * xi + yi
    sel = jnp.stack([chip, ci]).astype(jnp.int32)
    S, D = x.shape[1], x.shape[2]
    DQ = mem_w_kv.shape[2] // 2
    DM = D - DQ
    NG = DM // HEAD_DIM

    shards = {"wa": (a_w_in, 0), "wkv": (kv_w[None], 0), "wq": (b_w_q, 0), "mkv0": (mem_w_kv, 0), "mkv1": (mem_w_kv, 1),
              "wo0": (w_o, 0), "wo1": (w_o, 1), "wup0": (w_up, 0), "wup1": (w_up, 1), "wdown0": (w_down, 0),
              "wdown1": (w_down, 1)}
    placed = {n: _cast_place(s.reshape(s.shape[0], 2, s.shape[1] // 2, s.shape[2]), layer, sel, "cast_place_" + n)
              for n, (s, layer) in shards.items()}
    ln_pack = jnp.concatenate([ln_g.reshape(4, -1), ln_b.reshape(4, -1)], axis=0)
    ln_buf = lax.dynamic_update_slice(jnp.zeros((N_CHIPS,) + ln_pack.shape, F32), ln_pack[None], (chip, 0, 0))
    ln_shape = (4, D)
    small_shapes = [a_w_s.shape, a_b_s.shape, a_sgu_ln_g.shape, a_sgu_ln_b.shape, kv_b_f.shape, ln_shape, ln_shape,
                    (1, LANES)]
    n_small = -(-_rows_of(small_shapes) // 64) * 64
    hooks = _Staged(placed, sel, DM, NG, {"wa": a_w_in.shape[2], "wkv": kv_w.shape[1]}, n_small)
    weights = {"sg": a_sgu_ln_g, "sb": a_sgu_ln_b, "ws": a_w_s[0], "b_s": a_b_s[0], "bf": kv_b_f}
    got = _run_comm(_gather_weights([placed[n] for n in _GATHER_STAGES["start"]], whole=[ln_buf]),
                    "gather_weights_first")
    hooks.done("start", got[:-1], weights)
    ln_full = got[-1].transpose(1, 0, 2).reshape(8, D)
    weights["ln_g"], weights["ln_b"] = ln_full[:4], ln_full[4:]

    _, grad_x, gr = _local_step(x[0], mem[0], loss_target[0], weights, hooks)

    def chip_sums(groups):
        bufs = []
        for grp in groups:
            buf = None
            for layer, n in enumerate(grp):
                buf = _chip_reduce(hooks.g4[n], hooks.recv1[n], hooks.recv2[n], sel, buf, layer, len(grp),
                                   "reduce_chip_sum_" + n)
            bufs.append(buf)
        return bufs

    whole = lambda a: a.reshape(a.shape[0], 2 * a.shape[3], a.shape[4])
    flat = lambda a: a.reshape(-1, a.shape[-1])

    def update(name, g, w, m, v):
        d, mn, vn = _adamw(flat(g), flat(w), flat(m), flat(v), "adamw_" + name)
        return d.reshape(w.shape), mn.reshape(w.shape), vn.reshape(w.shape)

    hooks.done("end", _run_comm(hooks.plan("end", weights, gr), "reduce_chip_exchange_end"), weights)
    small_mine = _chip_reduce(hooks.g4["small"], hooks.recv1["small"], hooks.recv2["small"], sel, None, 0, 1,
                              "reduce_chip_sum_small", by_chip=True)
    red, small_all = _pair_gather(chip_sums([["wa"], ["wkv"], ["wq"], ["mkv0", "mkv1"], ["wo0", "wo1"],
                                             ["wup0", "wup1"], ["wdown0", "wdown1"]]), small_mine, "reduce_pair_gather")
    g_a_w_in, g_kv_w, g_b_w_q, g_mem_w_kv, g_w_o, g_w_up, g_w_down = [whole(a) for a in red]
    g_kv_w = g_kv_w[0]
    g_w_down = g_w_down.transpose(0, 2, 1)
    small_all = small_all.reshape(n_small, LANES)
    g_ws, g_bs, g_sg, g_sb, g_bf, g_lng, g_lnb, loss_row = _unpack_rows(small_all, small_shapes)
    loss = loss_row[0, 0]
    dsh = D // N_CHIPS
    g_ln_g = lax.dynamic_slice_in_dim(g_lng, chip * dsh, dsh, axis=1).reshape(ln_g.shape)
    g_ln_b = lax.dynamic_slice_in_dim(g_lnb, chip * dsh, dsh, axis=1).reshape(ln_b.shape)

    upd = {
        "a_w_in": update("a_w_in", g_a_w_in, a_w_in, m_a_w_in, v_a_w_in),
        "kv_w": update("kv_w", g_kv_w, kv_w, m_kv_w, v_kv_w),
        "b_w_q": update("b_w_q", g_b_w_q, b_w_q, m_b_w_q, v_b_w_q),
        "mem_w_kv": update("mem_w_kv", g_mem_w_kv, mem_w_kv, m_mem_w_kv, v_mem_w_kv),
        "w_o": update("w_o", g_w_o, w_o, m_w_o, v_w_o),
        "w_up": update("w_up", g_w_up, w_up, m_w_up, v_w_up),
        "w_down": update("w_down", g_w_down, w_down, m_w_down, v_w_down),
    }
    tiny_g = [g_sg, g_sb, g_ws, g_bs, g_bf, g_ln_g, g_ln_b]
    tiny_w = [a_sgu_ln_g, a_sgu_ln_b, a_w_s, a_b_s, kv_b_f, ln_g, ln_b]
    tiny_m = [m_a_sgu_ln_g, m_a_sgu_ln_b, m_a_w_s, m_a_b_s, m_kv_b_f, m_ln_g, m_ln_b]
    tiny_v = [v_a_sgu_ln_g, v_a_sgu_ln_b, v_a_w_s, v_a_b_s, v_kv_b_f, v_ln_g, v_ln_b]
    tiny_shapes = [a.shape for a in tiny_w]
    n_tiny = -(-_rows_of(tiny_shapes) // 8) * 8
    td, tm_, tv = _adamw(_pack_rows(tiny_g, n_tiny), _pack_rows(tiny_w, n_tiny), _pack_rows(tiny_m, n_tiny),
                         _pack_rows(tiny_v, n_tiny), "adamw_small")
    for name, d, mn, vn in zip(["a_sgu_ln_g", "a_sgu_ln_b", "a_w_s", "a_b_s", "kv_b_f", "ln_g", "ln_b"],
                               _unpack_rows(td, tiny_shapes), _unpack_rows(tm_, tiny_shapes),
                               _unpack_rows(tv, tiny_shapes)):
        upd[name] = (d, mn, vn)

    order = ["a_w_in", "a_sgu_ln_g", "a_sgu_ln_b", "a_w_s", "a_b_s", "kv_w", "kv_b_f", "b_w_q", "mem_w_kv", "w_o",
             "ln_g", "ln_b", "w_up", "w_down"]
    grads = {"a_w_in": g_a_w_in.reshape(a_w_in.shape), "a_sgu_ln_g": g_sg.reshape(a_sgu_ln_g.shape),
             "a_sgu_ln_b": g_sb.reshape(a_sgu_ln_b.shape), "a_w_s": g_ws.reshape(a_w_s.shape),
             "a_b_s": g_bs.reshape(a_b_s.shape), "kv_w": g_kv_w, "kv_b_f": g_bf, "b_w_q": g_b_w_q.reshape(b_w_q.shape),
             "mem_w_kv": g_mem_w_kv, "w_o": g_w_o, "ln_g": g_ln_g, "ln_b": g_ln_b, "w_up": g_w_up, "w_down": g_w_down}
    return (loss, grad_x[None], *[grads[n] for n in order], *[upd[n][0] for n in order],
            *[upd[n][1] for n in order], *[upd[n][2] for n in order])
```

```python
import functools
import math

import jax
import jax.numpy as jnp
from jax import lax
from jax.experimental import pallas as pl
from jax.experimental.pallas import tpu as pltpu

F32 = jnp.float32
BF16 = jnp.bfloat16

HEAD_DIM = 64
PAIR = 2 * HEAD_DIM
CHUNK = 128
LN_EPS = 1e-5
ALPHA = 4 ** 0.25
QK_SCALE = 1.0 / math.sqrt(HEAD_DIM)
NEG = -1e30
N_CHIPS = 4
FOX_BLOCK = 256
LANES = 128
VMEM_LIMIT = 48 * 2 ** 20

ADAM_LR, ADAM_B1, ADAM_B2, ADAM_EPS, ADAM_WD, ADAM_STEP = 0.001, 0.9, 0.999, 1e-08, 0.01, 10

MESH = pl.DeviceIdType.MESH
ANY = pl.BlockSpec(memory_space=pl.ANY)


def _dot(a, b):
    return jnp.dot(a, b, preferred_element_type=F32)


def _dot_nt(a, b):
    return lax.dot_general(a, b, (((1,), (1,)), ((), ())), preferred_element_type=F32)


def _dot_tn(a, b):
    return lax.dot_general(a, b, (((0,), (0,)), ((), ())), preferred_element_type=F32)


def _ln_stats(r):
    mu = jnp.mean(r, axis=-1, keepdims=True)
    d = r - mu
    var = jnp.mean(d * d, axis=-1, keepdims=True)
    rstd = lax.rsqrt(var + LN_EPS)
    return d * rstd, rstd


def _ln_bwd(dy, xhat, rstd, g):
    dxh = dy * g
    m1 = jnp.mean(dxh, axis=-1, keepdims=True)
    m2 = jnp.mean(dxh * xhat, axis=-1, keepdims=True)
    return rstd * (dxh - m1 - xhat * m2)


_GELU_K = math.sqrt(2.0 / math.pi)


def _gelu(x):
    return 0.5 * x * (1.0 + jnp.tanh(_GELU_K * (x + 0.044715 * x * x * x)))


def _gelu_grad(x):
    t = jnp.tanh(_GELU_K * (x + 0.044715 * x * x * x))
    return 0.5 * (1.0 + t) + 0.5 * x * (1.0 - t * t) * _GELU_K * (1.0 + 3 * 0.044715 * x * x)


def _lane_mask(e):
    lane = lax.broadcasted_iota(jnp.int32, (1, PAIR), 1)
    return (lane >= HEAD_DIM * e) & (lane < HEAD_DIM * (e + 1))


def _first_head():
    return lax.broadcasted_iota(jnp.int32, (1, PAIR), 1) < HEAD_DIM


def _mem_probs(qh, mkp):
    sc = _dot_nt(qh, mkp) * QK_SCALE
    ex = jnp.exp(sc - jnp.max(sc, axis=-1, keepdims=True))
    return ex / jnp.sum(ex, axis=-1, keepdims=True)


def _mem_attn_fwd(qm_b, mkv_ref, dq_dim):
    outs = []
    for mp in range(dq_dim // PAIR):
        qp = qm_b[:, mp * PAIR:(mp + 1) * PAIR]
        mkp = mkv_ref[:, mp * PAIR:(mp + 1) * PAIR]
        mvp = mkv_ref[:, dq_dim + mp * PAIR:dq_dim + (mp + 1) * PAIR]
        heads = []
        for e in (0, 1):
            qh = jnp.where(_lane_mask(e), qp, jnp.zeros_like(qp))
            p = _mem_probs(qh, mkp)
            heads.append(_dot(p.astype(BF16), mvp))
        outs.append(jnp.where(_first_head(), heads[0], heads[1]))
    return outs


def _mem_attn_bwd(qm_b, mkv_ref, dmo_pairs, dmkv_ref, dq_dim):
    dqs = []
    for mp in range(dq_dim // PAIR):
        ks = slice(mp * PAIR, (mp + 1) * PAIR)
        vs = slice(dq_dim + mp * PAIR, dq_dim + (mp + 1) * PAIR)
        qp = qm_b[:, ks]
        mkp = mkv_ref[:, ks]
        mvp = mkv_ref[:, vs]
        dmo_b = dmo_pairs[mp].astype(BF16)
        dq = None
        dmk = None
        dmv = None
        for e in (0, 1):
            hm = _lane_mask(e)
            qh = jnp.where(hm, qp, jnp.zeros_like(qp))
            p = _mem_probs(qh, mkp)
            doh = jnp.where(hm, dmo_b, jnp.zeros_like(dmo_b))
            dp = _dot_nt(doh, mvp)
            ds = p * (dp - jnp.sum(dp * p, axis=-1, keepdims=True))
            dsb = (ds * QK_SCALE).astype(BF16)
            kh = jnp.where(hm, mkp, jnp.zeros_like(mkp))
            dq_e = _dot(dsb, kh)
            dmk_e = _dot_tn(dsb, qh)
            dmv_e = _dot_tn(p.astype(BF16), doh)
            dq = dq_e if dq is None else dq + dq_e
            dmk = dmk_e if dmk is None else dmk + dmk_e
            dmv = dmv_e if dmv is None else dmv + dmv_e
        dmkv_ref[:, ks] += dmk
        dmkv_ref[:, vs] += dmv
        dqs.append(dq)
    return dqs


def _params(n_axes=1):
    return pltpu.CompilerParams(dimension_semantics=("arbitrary",) * n_axes, vmem_limit_bytes=VMEM_LIMIT)


def _full(shape):
    return pl.BlockSpec(shape, lambda *_: (0,) * len(shape))


def _rows(tm, cols):
    return pl.BlockSpec((tm, cols), lambda i: (i, 0))


def _cols(rows, tm):
    return pl.BlockSpec((rows, tm), lambda i: (0, i))


def _pick(n, pref):
    if n <= pref:
        return n
    best = LANES
    for t in range(LANES, pref + 1, LANES):
        if n % t == 0:
            best = t
    return best if 2 * best >= pref or n > 2 * pref else n


class _Comm:
    def __init__(self, ins, out_shapes, aliases, sem_counts, start, finish):
        self.ins, self.out_shapes, self.aliases, self.sem_counts = list(ins), list(out_shapes), dict(aliases), sem_counts
        self.start, self.finish = start, finish


def _pcall(body, *, name, grid, in_specs, out_specs, out_shape, args, scratch_shapes=(), comm=None, aliases=None):
    n_in, n_out, n_scr = len(in_specs), len(out_shape), len(scratch_shapes)
    aliases = dict(aliases or {})
    if comm is None:
        res = pl.pallas_call(body, name=name, grid=grid, out_shape=tuple(out_shape), in_specs=list(in_specs),
                             out_specs=tuple(out_specs), scratch_shapes=list(scratch_shapes),
                             input_output_aliases=aliases, compiler_params=_params())(*args)
        return tuple(res), ()
    nci, nco = len(comm.ins), len(comm.out_shapes)
    last = grid[0] - 1

    def wrapped(*refs):
        ins, refs = refs[:n_in], refs[n_in:]
        cins, refs = refs[:nci], refs[nci:]
        outs, refs = refs[:n_out], refs[n_out:]
        couts, refs = refs[:nco], refs[nco:]
        scr, sems = refs[:n_scr], refs[n_scr:]

        @pl.when(pl.program_id(0) == 0)
        def _():
            comm.start(cins, couts, sems)

        body(*ins, *outs, *scr)

        @pl.when(pl.program_id(0) == last)
        def _():
            comm.finish(cins, couts, sems)

    res = pl.pallas_call(
        wrapped, name=name, grid=grid, out_shape=tuple(out_shape) + tuple(comm.out_shapes),
        in_specs=list(in_specs) + [ANY] * nci, out_specs=tuple(out_specs) + (ANY,) * nco,
        input_output_aliases={**aliases, **{n_in + a: n_out + b for a, b in comm.aliases.items()}},
        scratch_shapes=list(scratch_shapes) + [pltpu.SemaphoreType.DMA((k,)) for k in comm.sem_counts],
        compiler_params=pltpu.CompilerParams(dimension_semantics=("arbitrary",), vmem_limit_bytes=VMEM_LIMIT,
                                             has_side_effects=True),
    )(*args, *comm.ins)
    return tuple(res[:n_out]), tuple(res[n_out:])


def _run_comm(comm, name):
    nci, nco = len(comm.ins), len(comm.out_shapes)

    def body(*refs):
        cins, couts, sems = refs[:nci], refs[nci:nci + nco], refs[nci + nco:]
        comm.start(cins, couts, sems)
        comm.finish(cins, couts, sems)

    return pl.pallas_call(
        body, name=name, out_shape=tuple(comm.out_shapes), in_specs=[ANY] * nci, out_specs=(ANY,) * nco,
        input_output_aliases=comm.aliases, scratch_shapes=[pltpu.SemaphoreType.DMA((k,)) for k in comm.sem_counts],
        compiler_params=pltpu.CompilerParams(has_side_effects=True),
    )(*comm.ins)


def _mm(a, b, name, a_is_transposed, out_split=1):
    (M, R) = a.shape if a_is_transposed else a.shape[::-1]
    _, N = b.shape
    bm, br = _pick(M, 1024 if a_is_transposed else 512), _pick(R, 1024 if a_is_transposed else 512)
    ncol = N // out_split
    bn = _pick(ncol, 1024)
    per = ncol // bn
    last = R // br - 1

    def body(a_ref, b_ref, o_ref, acc_ref):
        @pl.when(pl.program_id(2) == 0)
        def _():
            acc_ref[...] = jnp.zeros_like(acc_ref)

        acc_ref[...] += (_dot if a_is_transposed else _dot_tn)(a_ref[...], b_ref[...])

        @pl.when(pl.program_id(2) == last)
        def _():
            o_ref[...] = acc_ref[...].astype(BF16)

    if out_split == 1:
        out_shape = jax.ShapeDtypeStruct((M, N), BF16)
        out_spec = pl.BlockSpec((bm, bn), lambda i, j, r: (i, j))
    else:
        out_shape = jax.ShapeDtypeStruct((out_split, M, ncol), BF16)
        out_spec = pl.BlockSpec((None, bm, bn), lambda i, j, r: (j // per, i, j % per))
    a_spec = (pl.BlockSpec((bm, br), lambda i, j, r: (i, r)) if a_is_transposed
              else pl.BlockSpec((br, bm), lambda i, j, r: (r, i)))
    return pl.pallas_call(
        body, name=name, out_shape=out_shape, grid=(M // bm, N // bn, R // br),
        in_specs=[a_spec, pl.BlockSpec((br, bn), lambda i, j, r: (r, j))], out_specs=out_spec,
        scratch_shapes=[pltpu.VMEM((bm, bn), F32)],
        compiler_params=pltpu.CompilerParams(dimension_semantics=("parallel", "parallel", "arbitrary"),
                                             vmem_limit_bytes=VMEM_LIMIT),
    )(a, b)


def _mm_tn(a, b, name, out_split=1):
    return _mm(a, b, name, False, out_split)


def _mm_nn(at, b, name, out_split=1):
    return _mm(at, b, name, True, out_split)


def _mm_small(a, b, name):
    def body(a_ref, b_ref, o_ref):
        o_ref[...] = _dot(a_ref[...], b_ref[...]).astype(BF16)

    return pl.pallas_call(body, name=name, out_shape=jax.ShapeDtypeStruct((a.shape[0], b.shape[1]), BF16),
                          compiler_params=pltpu.CompilerParams(vmem_limit_bytes=VMEM_LIMIT))(a, b)


def _mlp_out(x_ref, wup_ref, wdown_ref, g_ref, b_ref, r_ref, h_ref):
    NC, _, FC = wup_ref.shape
    xv = x_ref[...]
    xb = xv.astype(BF16)
    y = jnp.zeros(xv.shape, F32)
    for c in range(NC):
        a = jnp.maximum(_dot(xb, wup_ref[c]), 0.0)
        hb = (a * a).astype(BF16)
        h_ref[:, c * FC:(c + 1) * FC] = hb
        y = y + _dot(hb, wdown_ref[c])
    r = ALPHA * xv + y
    r_ref[...] = r
    xhat, _ = _ln_stats(r)
    return xhat * g_ref[...] + b_ref[...]


def _mlp_fwd_loss(x, wup4, wdown4, g, b, target, name):
    S, D = x.shape
    NC, _, FC = wup4.shape
    tm = min(256, S)

    def body(x_ref, wup_ref, wdown_ref, g_ref, b_ref, t_ref, d_ref, r_ref, h_ref, l_ref):
        @pl.when(pl.program_id(0) == 0)
        def _():
            l_ref[...] = jnp.zeros_like(l_ref)

        err = _mlp_out(x_ref, wup_ref, wdown_ref, g_ref, b_ref, r_ref, h_ref) - t_ref[...]
        d_ref[...] = err * (1.0 / D)
        l_ref[...] += (0.5 / D) * jnp.sum(err * err)

    return pl.pallas_call(
        body, name=name, grid=(S // tm,),
        out_shape=(jax.ShapeDtypeStruct((S, D), F32), jax.ShapeDtypeStruct((S, D), F32),
                   jax.ShapeDtypeStruct((S, NC * FC), BF16), jax.ShapeDtypeStruct((8, LANES), F32)),
        in_specs=[_rows(tm, D), _full(wup4.shape), _full(wdown4.shape), _full((1, D)), _full((1, D)), _rows(tm, D)],
        out_specs=(_rows(tm, D), _rows(tm, D), _rows(tm, NC * FC), _full((8, LANES))),
        compiler_params=_params(),
    )(x, wup4, wdown4, g, b, target)


def _mlp_fwd(x, wup4, wdown4, g, b, name, comm=None):
    S, D = x.shape
    NC, _, FC = wup4.shape
    tm = min(256, S)

    def body(x_ref, wup_ref, wdown_ref, g_ref, b_ref, xo_ref, xb_ref, xbt_ref, r_ref, h_ref):
        xo = _mlp_out(x_ref, wup_ref, wdown_ref, g_ref, b_ref, r_ref, h_ref)
        xo_ref[...] = xo
        xb_ref[...] = xo.astype(BF16)
        xbt_ref[...] = xo.T.astype(BF16)

    return _pcall(
        body, name=name, grid=(S // tm,), comm=comm,
        out_shape=(jax.ShapeDtypeStruct((S, D), F32), jax.ShapeDtypeStruct((S, D), BF16),
                   jax.ShapeDtypeStruct((D, S), BF16), jax.ShapeDtypeStruct((S, D), F32),
                   jax.ShapeDtypeStruct((S, NC * FC), BF16)),
        in_specs=[_rows(tm, D), _full(wup4.shape), _full(wdown4.shape), _full((1, D)), _full((1, D))],
        out_specs=(_rows(tm, D), _rows(tm, D), _cols(D, tm), _rows(tm, D), _rows(tm, NC * FC)),
        args=(x, wup4, wdown4, g, b))


def _mlp_bwd(dxo, r, h, wup4, wdown4, g, name, comm=None):
    S, D = r.shape
    NC, _, FC = wup4.shape
    tm = min(256, S)

    def body(dxo_ref, r_ref, h_ref, wup_ref, wdown_ref, g_ref, dx_ref, drbt_ref, dpre_ref, dgb_ref):
        @pl.when(pl.program_id(0) == 0)
        def _():
            dgb_ref[...] = jnp.zeros_like(dgb_ref)

        xhat, rstd = _ln_stats(r_ref[...])
        dy = dxo_ref[...]
        dgb_ref[0:1, :] += jnp.sum(dy * xhat, axis=0, keepdims=True)
        dgb_ref[1:2, :] += jnp.sum(dy, axis=0, keepdims=True)
        dr = _ln_bwd(dy, xhat, rstd, g_ref[...])
        drb = dr.astype(BF16)
        drbt_ref[...] = dr.T.astype(BF16)
        dx = ALPHA * dr
        for c in range(NC):
            dh = _dot_nt(drb, wdown_ref[c])
            a = jnp.sqrt(h_ref[:, c * FC:(c + 1) * FC].astype(F32))
            dpre = (2.0 * a * dh).astype(BF16)
            dpre_ref[:, c * FC:(c + 1) * FC] = dpre
            dx = dx + _dot_nt(dpre, wup_ref[c])
        dx_ref[...] = dx

    return _pcall(
        body, name=name, grid=(S // tm,), comm=comm,
        out_shape=(jax.ShapeDtypeStruct((S, D), F32), jax.ShapeDtypeStruct((D, S), BF16),
                   jax.ShapeDtypeStruct((S, NC * FC), BF16), jax.ShapeDtypeStruct((2, D), F32)),
        in_specs=[_rows(tm, D), _rows(tm, D), _rows(tm, NC * FC), _full(wup4.shape), _full(wdown4.shape),
                  _full((1, D))],
        out_specs=(_rows(tm, D), _cols(D, tm), _rows(tm, NC * FC), _full((2, D))),
        args=(dxo, r, h, wup4, wdown4, g))


def _tail_fwd_store(xv, cat_ref, wo_ref, g_ref, b_ref, xo_ref, xbt_ref, r_ref, catt_ref):
    cat = cat_ref[...]
    y = _dot(cat, wo_ref[...])
    r = ALPHA * xv + y
    xhat, _ = _ln_stats(r)
    xo = xhat * g_ref[...] + b_ref[...]
    xo_ref[...] = xo
    xbt_ref[...] = xo.T.astype(BF16)
    catt_ref[...] = cat.astype(F32).T.astype(BF16)
    r_ref[...] = r


def _tail_bwd_head(dxo_ref, r_ref, g_ref, wo_ref, dgb_ref, drb_ref):
    xhat, rstd = _ln_stats(r_ref[...])
    dy = dxo_ref[...]
    dgb_ref[0:1, :] += jnp.sum(dy * xhat, axis=0, keepdims=True)
    dgb_ref[1:2, :] += jnp.sum(dy, axis=0, keepdims=True)
    dr = _ln_bwd(dy, xhat, rstd, g_ref[...])
    drb = dr.astype(BF16)
    drb_ref[...] = drb
    return dr, _dot_nt(drb, wo_ref[...])


def _causal(shape):
    return lax.broadcasted_iota(jnp.int32, shape, 1) <= lax.broadcasted_iota(jnp.int32, shape, 0)


def _sgu_mixed(ws_ref, bst_ref, gp, vpair):
    tril = _causal((CHUNK, CHUNK))
    w0 = jnp.where(tril, ws_ref[2 * gp], 0.0).astype(BF16)
    w1 = jnp.where(tril, ws_ref[2 * gp + 1], 0.0).astype(BF16)
    m0 = _dot(w0, vpair) + bst_ref[:, 2 * gp:2 * gp + 1]
    m1 = _dot(w1, vpair) + bst_ref[:, 2 * gp + 1:2 * gp + 2]
    return jnp.where(_first_head(), m0, m1), w0, w1


def _mixer_a_fwd(x, wa, sg, sb, ws, bst, mkv, wo, g, b, name, comm=None):
    S, D = x.shape
    DQ = mkv.shape[1] // 2
    DM = D - DQ
    NP = DM // PAIR
    tm = min(512, S)

    def body(x_ref, wa_ref, sg_ref, sb_ref, ws_ref, bst_ref, mkv_ref, wo_ref, g_ref, b_ref,
             xo_ref, xbt_ref, r_ref, catt_ref, xint_ref, z_s, zv_s, cat_ref):
        xv = x_ref[...]
        xint_ref[...] = xv.T.astype(BF16)
        z_s[...] = _dot(xv.astype(BF16), wa_ref[...])
        vhat, _ = _ln_stats(_gelu(z_s[:, DM:2 * DM]))
        zv_s[...] = (vhat * sg_ref[...] + sb_ref[...]).astype(BF16)
        for c in range(tm // CHUNK):
            rs = slice(c * CHUNK, (c + 1) * CHUNK)
            for gp in range(NP):
                ls = slice(gp * PAIR, (gp + 1) * PAIR)
                mixed, _, _ = _sgu_mixed(ws_ref, bst_ref, gp, zv_s[rs, ls])
                cat_ref[rs, ls] = (_gelu(z_s[rs, ls]) * mixed).astype(BF16)
        mo = _mem_attn_fwd(z_s[:, 2 * DM:].astype(BF16), mkv_ref, DQ)
        for mp in range(DQ // PAIR):
            cat_ref[:, DM + mp * PAIR:DM + (mp + 1) * PAIR] = mo[mp].astype(BF16)
        _tail_fwd_store(xv, cat_ref, wo_ref, g_ref, b_ref, xo_ref, xbt_ref, r_ref, catt_ref)

    return _pcall(
        body, name=name, grid=(S // tm,), comm=comm,
        out_shape=(jax.ShapeDtypeStruct((S, D), F32), jax.ShapeDtypeStruct((D, S), BF16),
                   jax.ShapeDtypeStruct((S, D), F32), jax.ShapeDtypeStruct((D, S), BF16),
                   jax.ShapeDtypeStruct((D, S), BF16)),
        in_specs=[_rows(tm, D), _full(wa.shape), _full(sg.shape), _full(sb.shape), _full(ws.shape), _full(bst.shape),
                  _full(mkv.shape), _full(wo.shape), _full((1, D)), _full((1, D))],
        out_specs=(_rows(tm, D), _cols(D, tm), _rows(tm, D), _cols(D, tm), _cols(D, tm)),
        scratch_shapes=[pltpu.VMEM((tm, 2 * DM + DQ), F32), pltpu.VMEM((tm, DM), BF16), pltpu.VMEM((tm, D), BF16)],
        args=(x, wa, sg, sb, ws, bst, mkv, wo, g, b))


def _mixer_a_bwd(x, dxo, r, wa, sg, sb, ws, bst, mkv, wo, g, name, comm=None):
    S, D = x.shape
    DQ = mkv.shape[1] // 2
    DM = D - DQ
    NP = DM // PAIR
    NG = DM // HEAD_DIM
    tm = min(512, S)

    def body(x_ref, dxo_ref, r_ref, wa_ref, sg_ref, sb_ref, ws_ref, bst_ref, mkv_ref, wo_ref, g_ref,
             dx_ref, dz_ref, drb_ref, dgb_ref, dsgb_ref, dws_ref, dbst_ref, dmkv_ref, z_s, zv_s, dzv_s):
        @pl.when(pl.program_id(0) == 0)
        def _():
            dgb_ref[...] = jnp.zeros_like(dgb_ref)
            dsgb_ref[...] = jnp.zeros_like(dsgb_ref)
            dws_ref[...] = jnp.zeros_like(dws_ref)
            dbst_ref[...] = jnp.zeros_like(dbst_ref)
            dmkv_ref[...] = jnp.zeros_like(dmkv_ref)

        dr, dcat = _tail_bwd_head(dxo_ref, r_ref, g_ref, wo_ref, dgb_ref, drb_ref)
        z_s[...] = _dot(x_ref[...].astype(BF16), wa_ref[...])
        vhat, vrstd = _ln_stats(_gelu(z_s[:, DM:2 * DM]))
        zv_s[...] = (vhat * sg_ref[...] + sb_ref[...]).astype(BF16)
        tril = _causal((CHUNK, CHUNK))
        lane = lax.broadcasted_iota(jnp.int32, (1, LANES), 1)
        for c in range(tm // CHUNK):
            rs = slice(c * CHUNK, (c + 1) * CHUNK)
            for gp in range(NP):
                ls = slice(gp * PAIR, (gp + 1) * PAIR)
                vpair = zv_s[rs, ls]
                mixed, w0, w1 = _sgu_mixed(ws_ref, bst_ref, gp, vpair)
                u_pre = z_s[rs, ls]
                dmix = dcat[rs, ls]
                dz_ref[rs, ls] = (dmix * mixed * _gelu_grad(u_pre)).astype(BF16)
                dmixed = dmix * _gelu(u_pre)
                first = _first_head()
                d0 = jnp.where(first, dmixed, 0.0)
                d1 = jnp.where(first, 0.0, dmixed)
                d0b = d0.astype(BF16)
                d1b = d1.astype(BF16)
                dzv_s[rs, ls] = _dot_tn(w0, d0b) + _dot_tn(w1, d1b)
                dws_ref[2 * gp] += jnp.where(tril, _dot_nt(d0b, vpair), 0.0)
                dws_ref[2 * gp + 1] += jnp.where(tril, _dot_nt(d1b, vpair), 0.0)
                dbst_ref[...] += (jnp.where(lane == 2 * gp, jnp.sum(d0, axis=-1, keepdims=True), 0.0)
                                  + jnp.where(lane == 2 * gp + 1, jnp.sum(d1, axis=-1, keepdims=True), 0.0))
        dzv = dzv_s[...]
        dsgb_ref[0:1, :] += jnp.sum(dzv * vhat, axis=0, keepdims=True)
        dsgb_ref[1:2, :] += jnp.sum(dzv, axis=0, keepdims=True)
        dgv = _ln_bwd(dzv, vhat, vrstd, sg_ref[...])
        dz_ref[:, DM:2 * DM] = (dgv * _gelu_grad(z_s[:, DM:2 * DM])).astype(BF16)
        dmo = [dcat[:, DM + mp * PAIR:DM + (mp + 1) * PAIR] for mp in range(DQ // PAIR)]
        dqm = _mem_attn_bwd(z_s[:, 2 * DM:].astype(BF16), mkv_ref, dmo, dmkv_ref, DQ)
        for mp in range(DQ // PAIR):
            dz_ref[:, 2 * DM + mp * PAIR:2 * DM + (mp + 1) * PAIR] = dqm[mp].astype(BF16)
        dx_ref[...] = ALPHA * dr + _dot_nt(dz_ref[...], wa_ref[...])

    ZW = 2 * DM + DQ
    M = mkv.shape[0]
    return _pcall(
        body, name=name, grid=(S // tm,), comm=comm,
        out_shape=(jax.ShapeDtypeStruct((S, D), F32), jax.ShapeDtypeStruct((S, ZW), BF16),
                   jax.ShapeDtypeStruct((S, D), BF16), jax.ShapeDtypeStruct((2, D), F32),
                   jax.ShapeDtypeStruct((2, DM), F32), jax.ShapeDtypeStruct((NG, CHUNK, CHUNK), F32),
                   jax.ShapeDtypeStruct((CHUNK, LANES), F32), jax.ShapeDtypeStruct((M, 2 * DQ), F32)),
        in_specs=[_rows(tm, D), _rows(tm, D), _rows(tm, D), _full(wa.shape), _full(sg.shape), _full(sb.shape),
                  _full(ws.shape), _full(bst.shape), _full(mkv.shape), _full(wo.shape), _full((1, D))],
        out_specs=(_rows(tm, D), _rows(tm, ZW), _rows(tm, D), _full((2, D)), _full((2, DM)),
                   _full((NG, CHUNK, CHUNK)), _full((CHUNK, LANES)), _full((M, 2 * DQ))),
        scratch_shapes=[pltpu.VMEM((tm, ZW), F32), pltpu.VMEM((tm, DM), BF16), pltpu.VMEM((tm, DM), F32)],
        args=(x, dxo, r, wa, sg, sb, ws, bst, mkv, wo, g))


def _proj_b_fwd(xb, wq, wkv, dm, name):
    S, D = xb.shape
    DQ = D - dm
    tm = min(512, S)

    def body(x_ref, wq_ref, wkv_ref, q_ref, qm_ref, k_ref, v_ref, fl_ref):
        xv = x_ref[...]
        z = _dot(xv, wq_ref[...])
        q_ref[...] = z[:, :dm].astype(BF16)
        qm_ref[...] = z[:, dm:].astype(BF16)
        kvf = _dot(xv, wkv_ref[...])
        k_ref[...] = kvf[:, :dm].astype(BF16)
        v_ref[...] = kvf[:, dm:2 * dm].astype(BF16)
        fl_ref[...] = kvf[:, 2 * dm:]

    return pl.pallas_call(
        body, name=name, grid=(S // tm,),
        out_shape=(jax.ShapeDtypeStruct((S, dm), BF16), jax.ShapeDtypeStruct((S, DQ), BF16),
                   jax.ShapeDtypeStruct((S, dm), BF16), jax.ShapeDtypeStruct((S, dm), BF16),
                   jax.ShapeDtypeStruct((S, LANES), F32)),
        in_specs=[_rows(tm, D), _full(wq.shape), _full(wkv.shape)],
        out_specs=(_rows(tm, dm), _rows(tm, DQ), _rows(tm, dm), _rows(tm, dm), _rows(tm, LANES)),
        compiler_params=_params(),
    )(xb, wq, wkv)


def _proj_b_bwd(dr, dqz, dkvf, wq, wkv, name):
    S, D = dr.shape
    tm = min(512, S)

    def body(dr_ref, dqz_ref, dkvf_ref, wq_ref, wkv_ref, dx_ref):
        dx_ref[...] = (ALPHA * dr_ref[...] + _dot_nt(dqz_ref[...], wq_ref[...])
                       + _dot_nt(dkvf_ref[...], wkv_ref[...]))

    return pl.pallas_call(
        body, name=name, grid=(S // tm,), out_shape=jax.ShapeDtypeStruct((S, D), F32),
        in_specs=[_rows(tm, D), _rows(tm, dqz.shape[1]), _rows(tm, dkvf.shape[1]), _full(wq.shape), _full(wkv.shape)],
        out_specs=_rows(tm, D), compiler_params=_params(),
    )(dr, dqz, dkvf, wq, wkv)


def _split3(v):
    hi = v.astype(BF16)
    r1 = v - hi.astype(F32)
    mid = r1.astype(BF16)
    lo = (r1 - mid.astype(F32)).astype(BF16)
    return hi, mid, lo


def _tri_sum(v, tri_b):
    hi, mid, lo = _split3(v)
    return _dot(hi, tri_b) + _dot(mid, tri_b) + _dot(lo, tri_b)


def _log_sigmoid(x):
    return jnp.minimum(x, 0.0) - jnp.log(1.0 + jnp.exp(-jnp.abs(x)))


def _forget_cumsum(flt, bf, name):
    H, S = flt.shape
    nchunk = S // LANES

    def body(fl_ref, bf_ref, p_ref):
        upper = (lax.broadcasted_iota(jnp.int32, (LANES, LANES), 0)
                 <= lax.broadcasted_iota(jnp.int32, (LANES, LANES), 1)).astype(BF16)

        def step(n, carry):
            s0 = pl.multiple_of(n * LANES, LANES)
            lf = _log_sigmoid(fl_ref[:, pl.ds(s0, LANES)] + bf_ref[...])
            cs = _tri_sum(lf, upper) + carry
            for i, piece in enumerate(_split3(-cs)):
                p_ref[i, :, pl.ds(s0, LANES)] = piece.astype(F32)
            return cs[:, LANES - 1:LANES]

        lax.fori_loop(0, nchunk, step, jnp.zeros((H, 1), F32))

    return pl.pallas_call(body, name=name, out_shape=jax.ShapeDtypeStruct((3, H, S), F32),
                          compiler_params=pltpu.CompilerParams(vmem_limit_bytes=VMEM_LIMIT))(flt, bf)


def _forget_cumsum_bwd(dct, flt, bf, name):
    H, S = flt.shape
    nchunk = S // LANES

    def body(dc_ref, fl_ref, bf_ref, dfl_ref, dbf_ref):
        lower = (lax.broadcasted_iota(jnp.int32, (LANES, LANES), 0)
                 >= lax.broadcasted_iota(jnp.int32, (LANES, LANES), 1)).astype(BF16)

        def step(n, carry):
            tail, tot = carry
            s0 = pl.multiple_of((nchunk - 1 - n) * LANES, LANES)
            suffix = _tri_sum(dc_ref[:, pl.ds(s0, LANES)], lower) + tail
            xv = fl_ref[:, pl.ds(s0, LANES)] + bf_ref[...]
            dfl = suffix * (1.0 / (1.0 + jnp.exp(xv)))
            dfl_ref[:, pl.ds(s0, LANES)] = dfl
            return suffix[:, 0:1], tot + jnp.sum(dfl, axis=-1, keepdims=True)

        _, tot = lax.fori_loop(0, nchunk, step, (jnp.zeros((H, 1), F32), jnp.zeros((H, 1), F32)))
        dbf_ref[...] = jnp.broadcast_to(tot, (H, LANES))

    return pl.pallas_call(body, name=name,
                          out_shape=(jax.ShapeDtypeStruct((H, S), F32), jax.ShapeDtypeStruct((H, LANES), F32)),
                          compiler_params=pltpu.CompilerParams(vmem_limit_bytes=VMEM_LIMIT))(dct, flt, bf)


N_BIAS = 3


def _bias_ones():
    lane = lax.broadcasted_iota(jnp.int32, (1, PAIR), 1)
    return ((lane & (HEAD_DIM - 1)) < N_BIAS).astype(BF16)


def _fox_fwd(q, k, v, ca, name, comm=None):
    S, DM = q.shape
    NP = DM // PAIR
    T = min(FOX_BLOCK, S)
    nq = S // T

    def body(q_ref, k_ref, v_ref, ca_ref, o_ref, lse_ref):
        tril = _causal((T, T))
        heads = (_lane_mask(0), _lane_mask(1))
        ones3 = _bias_ones()

        def kv_block(kj):
            s0 = pl.multiple_of(kj * T, T)
            kb, vb, cab = k_ref[pl.ds(s0, T), :], v_ref[pl.ds(s0, T), :], ca_ref[pl.ds(s0, T), :]
            return ([jnp.where(heads[e], kb, cab) for e in (0, 1)],
                    [jnp.where(heads[e], vb, jnp.ones_like(vb)) for e in (0, 1)])

        def q_pair(a, _):
            t0s = [pl.multiple_of((2 * a + r) * T, T) for r in (0, 1)]
            qaug = {}
            for r in (0, 1):
                qb = q_ref[pl.ds(t0s[r], T), :] * QK_SCALE
                for e in (0, 1):
                    qaug[r, e] = jnp.where(heads[e], qb, ones3)
            chains = [(r, e) for r in (0, 1) for e in (0, 1)]

            def update(carries, first_block, seen):
                blocks = [kv_block(first_block + kk) for kk in range(1 + max(kk for s in seen for kk, _ in s))]
                out = []
                scores = [[_dot_nt(qaug[r, e], blocks[kk][0][e]) for kk, _ in seen[r]] for r, e in chains]
                for n, (r, e) in enumerate(chains):
                    m, acc = carries[n]
                    tiles = [jnp.where(tril, s, NEG) if masked else s for s, (_, masked) in zip(scores[n], seen[r])]
                    m_new = jnp.maximum(m, jnp.max(functools.reduce(jnp.maximum, tiles), axis=-1, keepdims=True))
                    acc = jnp.exp(m - m_new) * acc
                    for s, (kk, _) in zip(tiles, seen[r]):
                        acc = acc + _dot(jnp.exp(s - m_new).astype(BF16), blocks[kk][1][e])
                    out.append((m_new, acc))
                return tuple(out)

            def full(first_block, n_blocks, carries):
                return update(carries, first_block, [[(kk, False) for kk in range(n_blocks)]] * 2)

            init = (jnp.full((T, 1), NEG, F32), jnp.zeros((T, PAIR), F32))
            carries = lax.fori_loop(0, a // 2, lambda i, c: full(4 * i, 4, c), (init,) * 4)
            carries = lax.fori_loop(0, a % 2, lambda i, c: full(2 * a - 2, 2, c), carries)
            carries = update(carries, 2 * a, [[(0, True)], [(0, False), (1, True)]])
            first = _first_head()
            for r in (0, 1):
                (m0, acc0), (m1, acc1) = carries[2 * r], carries[2 * r + 1]
                l0, l1 = acc0[:, HEAD_DIM:HEAD_DIM + 1], acc1[:, 0:1]
                o_ref[pl.ds(t0s[r], T), :] = jnp.where(first, acc0 / l0, acc1 / l1).astype(BF16)
                lse_t = jnp.where(first, m0 + jnp.log(l0), m1 + jnp.log(l1)).T
                lse_ref[0:1, pl.ds(t0s[r], T)] = lse_t[0:1, :]
                lse_ref[1:2, pl.ds(t0s[r], T)] = lse_t[HEAD_DIM:HEAD_DIM + 1, :]
            return 0

        lax.fori_loop(0, nq // 2, q_pair, 0)

    col = pl.BlockSpec((S, PAIR), lambda hp: (0, hp))
    return _pcall(
        body, name=name, grid=(NP,), comm=comm,
        out_shape=(jax.ShapeDtypeStruct((S, DM), BF16), jax.ShapeDtypeStruct((NP, 2, S), F32)),
        in_specs=[col, col, col, col], out_specs=(col, pl.BlockSpec((None, 2, S), lambda hp: (hp, 0, 0))),
        args=(q, k, v, ca))


def _fox_bwd(q, k, v, o, do, ca, lse, name, comm=None):
    S, DM = q.shape
    NP = DM // PAIR
    T = min(FOX_BLOCK, S)
    nq = S // T

    def body(q_ref, k_ref, v_ref, o_ref, do_ref, ca_ref, lse_ref, dq_ref, dk_ref, dv_ref, dc_ref,
             dq_acc, dcol_acc, dk_acc, dv_acc, dd_s):
        keep = lax.broadcasted_iota(jnp.int32, (T, T), 0) <= lax.broadcasted_iota(jnp.int32, (T, T), 1)
        lane = lax.broadcasted_iota(jnp.int32, (1, LANES), 1)
        heads = (_lane_mask(0), _lane_mask(1))
        ones3 = _bias_ones()
        dq_acc[...] = jnp.zeros_like(dq_acc)
        dcol_acc[...] = jnp.zeros_like(dcol_acc)
        dc_ref[...] = jnp.zeros_like(dc_ref)

        def row_dots(qi, _):
            t0 = pl.multiple_of(qi * T, T)
            prod = do_ref[pl.ds(t0, T), :].astype(F32) * o_ref[pl.ds(t0, T), :].astype(F32)
            cols = [jnp.sum(jnp.where(heads[e], prod, 0.0), axis=-1, keepdims=True) for e in (0, 1)]
            tile = jnp.where(lane == 0, cols[0], jnp.where(lane == 1, cols[1], 0.0))
            dd_s[:, pl.ds(t0, T)] = tile.T[0:8, :]
            return 0

        lax.fori_loop(0, nq, row_dots, 0)

        def kv_pair(b, _):
            dk_acc[...] = jnp.zeros_like(dk_acc)
            dv_acc[...] = jnp.zeros_like(dv_acc)
            s0s = [pl.multiple_of((2 * b + jj) * T, T) for jj in (0, 1)]
            kaug, ks, vbs = {}, {}, []
            for jj in (0, 1):
                kb = k_ref[pl.ds(s0s[jj], T), :]
                cab = ca_ref[pl.ds(s0s[jj], T), :]
                vbs.append(v_ref[pl.ds(s0s[jj], T), :])
                for e in (0, 1):
                    kaug[jj, e] = jnp.where(heads[e], kb, cab)
                    ks[jj, e] = jnp.where(heads[e], kb, jnp.zeros_like(kb)) * QK_SCALE

            def q_step(qi, blocks):
                t0 = pl.multiple_of(qi * T, T)
                qs = q_ref[pl.ds(t0, T), :] * QK_SCALE
                dob = do_ref[pl.ds(t0, T), :]
                doh = [jnp.where(heads[e], dob, jnp.zeros_like(dob)) for e in (0, 1)]
                qh = [jnp.where(heads[e], qs, jnp.zeros_like(qs)) for e in (0, 1)]
                chains = [(e, jj, masked) for e in (0, 1) for jj, masked in blocks]
                scores = [_dot_nt(kaug[jj, e], jnp.where(heads[e], qs, ones3)) for e, jj, _ in chains]
                dps = [_dot_nt(vbs[jj], doh[e]) for e, jj, _ in chains]
                dq = jnp.zeros((T, PAIR), F32)
                drow = [jnp.zeros((1, T), F32), jnp.zeros((1, T), F32)]
                for n, (e, jj, masked) in enumerate(chains):
                    st = jnp.where(keep, scores[n], NEG) if masked else scores[n]
                    pt = jnp.exp(st - lse_ref[e:e + 1, pl.ds(t0, T)])
                    dv_acc[jj] += _dot(pt.astype(BF16), doh[e])
                    dst = pt * (dps[n] - dd_s[e:e + 1, pl.ds(t0, T)])
                    dsb = dst.astype(BF16)
                    dk_acc[jj] += _dot(dsb, qh[e])
                    dq = dq + _dot_tn(dsb, ks[jj, e])
                    dcol_acc[pl.ds(s0s[jj], T), :] -= jnp.where(lane == e, jnp.sum(dst, axis=-1, keepdims=True), 0.0)
                    drow[e] = drow[e] + jnp.sum(dst, axis=0, keepdims=True)
                for e in (0, 1):
                    dc_ref[e:e + 1, pl.ds(t0, T)] += drow[e]
                dq_acc[pl.ds(t0, T), :] += dq

            q_step(2 * b, [(0, True)])
            q_step(2 * b + 1, [(0, False), (1, True)])

            def rest(i, _):
                q_step(2 * b + 2 + 2 * i, [(0, False), (1, False)])
                q_step(2 * b + 3 + 2 * i, [(0, False), (1, False)])
                return 0

            lax.fori_loop(0, (nq - 2) // 2 - b, rest, 0)
            for jj in (0, 1):
                dk_ref[pl.ds(s0s[jj], T), :] = dk_acc[jj].astype(BF16)
                dv_ref[pl.ds(s0s[jj], T), :] = dv_acc[jj].astype(BF16)
            return 0

        lax.fori_loop(0, nq // 2, kv_pair, 0)
        dq_ref[...] = dq_acc[...].astype(BF16)

        def add_cols(kj, _):
            s0 = pl.multiple_of(kj * T, T)
            dc_ref[:, pl.ds(s0, T)] += dcol_acc[pl.ds(s0, T), :].T[0:2, :]
            return 0

        lax.fori_loop(0, nq, add_cols, 0)

    col = pl.BlockSpec((S, PAIR), lambda hp: (0, hp))
    row2 = pl.BlockSpec((None, 2, S), lambda hp: (hp, 0, 0))
    return _pcall(
        body, name=name, grid=(NP,), comm=comm,
        out_shape=(jax.ShapeDtypeStruct(do.shape, BF16), jax.ShapeDtypeStruct((S, DM), BF16),
                   jax.ShapeDtypeStruct((S, DM), BF16), jax.ShapeDtypeStruct((NP, 2, S), F32)),
        in_specs=[col, col, col, col, col, col, row2], out_specs=(col, col, col, row2),
        scratch_shapes=[pltpu.VMEM((S, PAIR), F32), pltpu.VMEM((S, LANES), F32), pltpu.VMEM((2, T, PAIR), F32),
                        pltpu.VMEM((2, T, PAIR), F32), pltpu.VMEM((8, S), F32)],
        args=(q, k, v, o, do, ca, lse), aliases={4: 0})


def _mixer_b_fwd(x, o, qm, mkv, wo, g, b, name):
    S, D = x.shape
    DQ = qm.shape[1]
    DM = D - DQ
    tm = min(512, S)

    def body(x_ref, o_ref, qm_ref, mkv_ref, wo_ref, g_ref, b_ref, xo_ref, xbt_ref, r_ref, catt_ref, cat_ref):
        cat_ref[:, :DM] = o_ref[...]
        mo = _mem_attn_fwd(qm_ref[...], mkv_ref, DQ)
        for mp in range(DQ // PAIR):
            cat_ref[:, DM + mp * PAIR:DM + (mp + 1) * PAIR] = mo[mp].astype(BF16)
        _tail_fwd_store(x_ref[...], cat_ref, wo_ref, g_ref, b_ref, xo_ref, xbt_ref, r_ref, catt_ref)

    return pl.pallas_call(
        body, name=name, grid=(S // tm,),
        out_shape=(jax.ShapeDtypeStruct((S, D), F32), jax.ShapeDtypeStruct((D, S), BF16),
                   jax.ShapeDtypeStruct((S, D), F32), jax.ShapeDtypeStruct((D, S), BF16)),
        in_specs=[_rows(tm, D), _rows(tm, DM), _rows(tm, DQ), _full(mkv.shape), _full(wo.shape), _full((1, D)),
                  _full((1, D))],
        out_specs=(_rows(tm, D), _cols(D, tm), _rows(tm, D), _cols(D, tm)),
        scratch_shapes=[pltpu.VMEM((tm, D), BF16)],
        compiler_params=_params(),
    )(x, o, qm, mkv, wo, g, b)


def _mixer_b_bwd(dxo, r, qm, mkv, wo, g, name, comm=None):
    S, D = r.shape
    DQ = qm.shape[1]
    DM = D - DQ
    M = mkv.shape[0]
    tm = min(512, S)

    def body(dxo_ref, r_ref, qm_ref, mkv_ref, wo_ref, g_ref, dr_ref, drb_ref, dcat_ref, dgb_ref, dmkv_ref):
        @pl.when(pl.program_id(0) == 0)
        def _():
            dgb_ref[...] = jnp.zeros_like(dgb_ref)
            dmkv_ref[...] = jnp.zeros_like(dmkv_ref)

        dr, dcat = _tail_bwd_head(dxo_ref, r_ref, g_ref, wo_ref, dgb_ref, drb_ref)
        dr_ref[...] = dr
        dcat_ref[:, :DM] = dcat[:, :DM].astype(BF16)
        dmo = [dcat[:, DM + mp * PAIR:DM + (mp + 1) * PAIR] for mp in range(DQ // PAIR)]
        dqm = _mem_attn_bwd(qm_ref[...], mkv_ref, dmo, dmkv_ref, DQ)
        for mp in range(DQ // PAIR):
            dcat_ref[:, DM + mp * PAIR:DM + (mp + 1) * PAIR] = dqm[mp].astype(BF16)

    return _pcall(
        body, name=name, grid=(S // tm,), comm=comm,
        out_shape=(jax.ShapeDtypeStruct((S, D), F32), jax.ShapeDtypeStruct((S, D), BF16),
                   jax.ShapeDtypeStruct((S, D), BF16), jax.ShapeDtypeStruct((2, D), F32),
                   jax.ShapeDtypeStruct((M, 2 * DQ), F32)),
        in_specs=[_rows(tm, D), _rows(tm, D), _rows(tm, DQ), _full(mkv.shape), _full(wo.shape), _full((1, D))],
        out_specs=(_rows(tm, D), _rows(tm, D), _rows(tm, D), _full((2, D)), _full((M, 2 * DQ))),
        args=(dxo, r, qm, mkv, wo, g))


class _NoExchange:
    def plan(self, stage, w, grads):
        return None

    def done(self, stage, results, w):
        pass


def _local_step(x, mem, target, w, hooks):
    S, D = x.shape
    DQ = w["mkv0"].shape[1] // 2
    DM = D - DQ
    NG = DM // HEAD_DIM
    NP = DM // PAIR
    ln_g, ln_b = w["ln_g"], w["ln_b"]
    row = lambda a, i: a[i:i + 1]
    memb = mem.astype(BF16)
    bst =jnp.pad(w["b_s"].T, ((0, 0), (0, LANES - NG)))
    grads = {}

    def staged(stage, fn, *args):
        out, got = fn(*args, stage, comm=hooks.plan(stage, w, grads))
        hooks.done(stage, got, w)
        return out

    mkv0 = _mm_small(memb, w["mkv0"], "mem_kv_0")
    x1, x1t, r1, cat0t, x0t = staged("mixer_a_fwd", _mixer_a_fwd, x, w["wa"], w["sg"], w["sb"], w["ws"], bst, mkv0,
                                w["wo0"], row(ln_g, 0), row(ln_b, 0))
    x2, x2b, x2t, r2, h0 = staged("mlp_fwd_0", _mlp_fwd, x1, w["wup0"], w["wdown0"], row(ln_g, 1), row(ln_b, 1))
    q, qm, k, v, fl = _proj_b_fwd(x2b, w["wq"], w["wkv"], DM, "proj_b_fwd")
    flt = fl[:, :16].T
    bfc = jnp.pad(w["bf"], (0, 16 - NG)).reshape(16, 1)
    pieces = _forget_cumsum(flt, bfc, "forget_cumsum")
    pieces = jnp.pad(pieces[:, :NG].transpose(2, 1, 0), ((0, 0), (0, 0), (0, HEAD_DIM - N_BIAS)))
    ca = pieces.reshape(S, NP, 2, HEAD_DIM)[:, :, ::-1].reshape(S, DM).astype(BF16)
    o, lse = staged("fox_fwd", _fox_fwd, q, k, v, ca)
    mkv1 = _mm_small(memb, w["mkv1"], "mem_kv_1")
    x3, x3t, r3, cat1t = _mixer_b_fwd(x2, o, qm, mkv1, w["wo1"], row(ln_g, 2), row(ln_b, 2), "mixer_b_fwd")
    dx4, r4, h1, loss_part = _mlp_fwd_loss(x3, w["wup1"], w["wdown1"], row(ln_g, 3), row(ln_b, 3), target,
                                           "mlp_fwd_1")

    nc = w["wup0"].shape[0]
    (dx3, dr4t, dpre1, dgb11), _ = _mlp_bwd(dx4, r4, h1, w["wup1"], w["wdown1"], row(ln_g, 3), "mlp_bwd_1")
    grads["wdown1"] = _mm_nn(dr4t, h1, "dw_down_1", out_split=nc)
    grads["wup1"] = _mm_nn(x3t, dpre1, "dw_up_1", out_split=nc)
    dr3, dr3b, dcat1, dgb10, dmkv1 = staged("mixer_b_bwd", _mixer_b_bwd, dx3, r3, qm, mkv1, w["wo1"], row(ln_g, 2))
    dqz, dk, dv, dct3 = staged("fox_bwd", _fox_bwd, q, k, v, o, dcat1, ca, lse)
    dct = jnp.pad(dct3.reshape(NG, S), ((0, 16 - NG), (0, 0)))
    dflt, dbf = _forget_cumsum_bwd(dct, flt, bfc, "forget_cumsum_bwd")
    dfl = jnp.pad(dflt.T, ((0, 0), (0, LANES - 16))).astype(BF16)
    dkvf = jnp.concatenate([dk, dv, dfl], axis=1)
    dx2 = _proj_b_bwd(dr3, dqz, dkvf, w["wq"], w["wkv"], "proj_b_bwd")
    grads["wo1"] = _mm_nn(cat1t, dr3b, "dw_o_1")
    grads["wq"] = _mm_nn(x2t, dqz, "dw_q")
    grads["wkv"] = _mm_nn(x2t, dkvf, "dw_kv")
    grads["mkv1"] = _mm_tn(memb, dmkv1.astype(BF16), "dw_mkv_1")
    dx1, dr2t, dpre0, dgb01 = staged("mlp_bwd_0", _mlp_bwd, dx2, r2, h0, w["wup0"], w["wdown0"], row(ln_g, 1))
    grads["wdown0"] = _mm_nn(dr2t, h0, "dw_down_0", out_split=nc)
    grads["wup0"] = _mm_nn(x1t, dpre0, "dw_up_0", out_split=nc)
    dx0, dz, dr1b, dgb00, dsgb, dws, dbst, dmkv0 = staged(
        "mixer_a_bwd", _mixer_a_bwd, x, dx1, r1, w["wa"], w["sg"], w["sb"], w["ws"], bst, mkv0, w["wo0"], row(ln_g, 0))
    grads["wo0"] = _mm_nn(cat0t, dr1b, "dw_o_0")
    grads["wa"] = _mm_nn(x0t, dz, "dw_a")
    grads["mkv0"] = _mm_tn(memb, dmkv0.astype(BF16), "dw_mkv_0")
    grads.update({
        "ws": dws, "b_s": dbst[:, :NG].T, "sg": dsgb[0:1], "sb": dsgb[1:2], "bf": dbf[:NG, 0], "loss": loss_part[0:1],
        "ln_g": jnp.concatenate([dgb00[0:1], dgb01[0:1], dgb10[0:1], dgb11[0:1]], axis=0),
        "ln_b": jnp.concatenate([dgb00[1:2], dgb01[1:2], dgb10[1:2], dgb11[1:2]], axis=0),
    })
    return loss_part, dx0, grads


def _place():
    xi, yi, ci = lax.axis_index("x"), lax.axis_index("y"), lax.axis_index("c")
    peers = [(xi, 1 - yi), (1 - xi, yi), (1 - xi, 1 - yi)]
    return xi, yi, ci, peers


def _comm_call(body, name, ins, out_shapes, n_remote, n_local):
    return pl.pallas_call(
        body, name=name, out_shape=out_shapes, in_specs=[ANY] * len(ins), out_specs=[ANY] * len(out_shapes),
        scratch_shapes=[pltpu.SemaphoreType.DMA((n_remote,)), pltpu.SemaphoreType.DMA((n_remote,)),
                        pltpu.SemaphoreType.DMA((max(n_local, 1),))],
        compiler_params=pltpu.CompilerParams(has_side_effects=True),
    )(*ins)


def _pair_swap(xs):
    n = len(xs)
    out_shapes = [jax.ShapeDtypeStruct((a.shape[0],) + a.shape[2:], a.dtype) for a in xs]

    def copies(x_refs, o_refs, sems):
        send_sems, recv_sems = sems
        xi, yi, ci, _ = _place()
        return [pltpu.make_async_remote_copy(src_ref=x_refs[i].at[:, 1 - ci], dst_ref=o_refs[i],
                                             send_sem=send_sems.at[i], recv_sem=recv_sems.at[i],
                                             device_id=(xi, yi, 1 - ci), device_id_type=MESH) for i in range(n)]

    def start(x_refs, o_refs, sems):
        for cp in copies(x_refs, o_refs, sems):
            cp.start()

    def finish(x_refs, o_refs, sems):
        cps = copies(x_refs, o_refs, sems)
        for cp in cps:
            cp.wait_recv()
        for cp in cps:
            cp.wait_send()

    return _Comm(xs, out_shapes, {}, [n, n], start, finish)


def _chip_exchange(xs):
    n = len(xs)
    out_shapes = [jax.ShapeDtypeStruct((3,) + a.shape[1:], a.dtype) for a in xs]

    def copies(x_refs, o_refs, sems):
        send_sems, recv_sems = sems
        xi, yi, ci, peers = _place()
        return [pltpu.make_async_remote_copy(src_ref=x_refs[i].at[2 * px + py], dst_ref=o_refs[i].at[d],
                                             send_sem=send_sems.at[3 * i + d], recv_sem=recv_sems.at[3 * i + d],
                                             device_id=(px, py, ci), device_id_type=MESH)
                for i in range(n) for d, (px, py) in enumerate(peers)]

    def start(x_refs, o_refs, sems):
        for cp in copies(x_refs, o_refs, sems):
            cp.start()

    def finish(x_refs, o_refs, sems):
        cps = copies(x_refs, o_refs, sems)
        for cp in cps:
            cp.wait_recv()
        for cp in cps:
            cp.wait_send()

    return _Comm(xs, out_shapes, {}, [3 * n, 3 * n], start, finish)


def _inplace_call(body, name, bufs, sem_counts):
    n = len(bufs)
    return pl.pallas_call(
        body, name=name, out_shape=[jax.ShapeDtypeStruct(a.shape, a.dtype) for a in bufs],
        in_specs=[ANY] * n, out_specs=[ANY] * n, input_output_aliases={i: i for i in range(n)},
        scratch_shapes=[pltpu.SemaphoreType.DMA((k,)) for k in sem_counts],
        compiler_params=pltpu.CompilerParams(has_side_effects=True),
    )(*bufs)


def _gather_weights(bufs, whole=()):
    n, nw = len(bufs), len(whole)

    def ici(w, sems, i, d, px, py, ci, slot):
        ref = w[i].at[slot, ci] if i < n else w[i].at[slot]
        return pltpu.make_async_remote_copy(src_ref=ref, dst_ref=ref, send_sem=sems[0].at[3 * i + d],
                                            recv_sem=sems[1].at[3 * i + d], device_id=(px, py, ci), device_id_type=MESH)

    def d2d(w, sems, i, d, xi, yi, ci, slot, half):
        ref = w[i].at[slot, half]
        return pltpu.make_async_remote_copy(src_ref=ref, dst_ref=ref, send_sem=sems[2].at[3 * i + d],
                                            recv_sem=sems[3].at[3 * i + d], device_id=(xi, yi, 1 - ci),
                                            device_id_type=MESH)

    def start(_, w, sems):
        xi, yi, ci, peers = _place()
        for i in range(n + nw):
            for d, (px, py) in enumerate(peers):
                ici(w, sems, i, d, px, py, ci, 2 * xi + yi).start()

    def finish(_, w, sems):
        xi, yi, ci, peers = _place()
        for d, (px, py) in enumerate(peers):
            for i in range(n + nw):
                ici(w, sems, i, d, px, py, ci, 2 * px + py).wait_recv()
                if i < n:
                    d2d(w, sems, i, d, xi, yi, ci, 2 * px + py, ci).start()
        for d, (px, py) in enumerate(peers):
            for i in range(n):
                d2d(w, sems, i, d, xi, yi, ci, 2 * px + py, 1 - ci).wait_recv()
        for d, (px, py) in enumerate(peers):
            for i in range(n + nw):
                ici(w, sems, i, d, px, py, ci, 2 * xi + yi).wait_send()
                if i < n:
                    d2d(w, sems, i, d, xi, yi, ci, 2 * px + py, ci).wait_send()

    arrays = list(bufs) + list(whole)
    k = 3 * (n + nw)
    return _Comm(arrays, [jax.ShapeDtypeStruct(a.shape, a.dtype) for a in arrays], {i: i for i in range(n + nw)},
                 [k, k, max(3 * n, 1), max(3 * n, 1)], start, finish)


def _pair_gather(bufs, spread, name):
    n = len(bufs)
    where = [(i, l) for i, a in enumerate(bufs) for l in range(a.shape[0])]
    nw = len(where)
    arrays = list(bufs) + ([] if spread is None else [spread])
    na = len(arrays)

    def body(*refs):
        g = refs[na:na + n]
        sp = refs[na + n] if spread is not None else None
        send_sems, recv_sems = refs[2 * na:]
        xi, yi, ci, peers = _place()
        chips = [(xi, yi)] + peers
        others = [(d, f) for d in range(4) for f in (0, 1) if (d, f) != (0, 0)] if spread is not None else []
        sends = []
        for k, (i, l) in enumerate(where):
            mine = g[i].at[l, :, ci]
            cp = pltpu.make_async_remote_copy(src_ref=mine, dst_ref=mine, send_sem=send_sems.at[k],
                                              recv_sem=recv_sems.at[k], device_id=(xi, yi, 1 - ci), device_id_type=MESH)
            cp.start()
            sends.append(cp)
        for k, (d, f) in enumerate(others):
            mine = sp.at[2 * xi + yi, ci]
            cp = pltpu.make_async_remote_copy(src_ref=mine, dst_ref=mine, send_sem=send_sems.at[nw + k],
                                              recv_sem=recv_sems.at[nw + k],
                                              device_id=(chips[d][0], chips[d][1], ci if f == 0 else 1 - ci),
                                              device_id_type=MESH)
            cp.start()
            sends.append(cp)
        for k, (i, l) in enumerate(where):
            other = g[i].at[l, :, 1 - ci]
            pltpu.make_async_remote_copy(src_ref=other, dst_ref=other, send_sem=send_sems.at[k],
                                         recv_sem=recv_sems.at[k], device_id=(xi, yi, 1 - ci),
                                         device_id_type=MESH).wait_recv()
        for k, (d, f) in enumerate(others):
            px, py, pc = chips[d][0], chips[d][1], (ci if f == 0 else 1 - ci)
            theirs = sp.at[2 * px + py, pc]
            pltpu.make_async_remote_copy(src_ref=theirs, dst_ref=theirs, send_sem=send_sems.at[nw + k],
                                         recv_sem=recv_sems.at[nw + k], device_id=(px, py, pc),
                                         device_id_type=MESH).wait_recv()
        for cp in sends:
            cp.wait_send()

    n_sems = nw + (7 if spread is not None else 0)
    out = _inplace_call(body, name, arrays, [n_sems, n_sems])
    return out[:n], (out[n] if spread is not None else None)


def _row_block(rows, cols):
    want = max(8, (2 ** 18 // max(cols, 1)) // 8 * 8)
    if rows <= want:
        return rows
    best = 8
    for t in range(8, want + 1, 8):
        if rows % t == 0:
            best = t
    return best


def _pair_add(g4, recv, sel, out_dtype, name):
    A, _, H, C = g4.shape
    bh = _row_block(H, C)

    def body(sel_ref, g_ref, r_ref, o_ref):
        o_ref[...] = (g_ref[...].astype(F32) + r_ref[...].astype(F32)).astype(out_dtype)

    return pl.pallas_call(
        body, name=name, out_shape=jax.ShapeDtypeStruct((A, H, C), out_dtype),
        grid_spec=pltpu.PrefetchScalarGridSpec(
            num_scalar_prefetch=1, grid=(A, H // bh),
            in_specs=[pl.BlockSpec((None, None, bh, C), lambda a, i, s: (a, s[1], i, 0)),
                      pl.BlockSpec((None, bh, C), lambda a, i, s: (a, i, 0))],
            out_specs=pl.BlockSpec((None, bh, C), lambda a, i, s: (a, i, 0))),
        compiler_params=pltpu.CompilerParams(dimension_semantics=("arbitrary", "arbitrary"),
                                             vmem_limit_bytes=VMEM_LIMIT),
    )(sel, g4, recv)


def _chip_reduce(g4, recv1, recv2, sel, buf, layer, n_layers, name, by_chip=False):
    _, _, H, C = g4.shape
    bh = _row_block(H, C)
    if by_chip:
        out_shape = jax.ShapeDtypeStruct((N_CHIPS, 2, H, C), F32)
        out_spec = pl.BlockSpec((None, None, bh, C), lambda i, s: (s[0], s[1], i, 0))
    else:
        out_shape = jax.ShapeDtypeStruct((n_layers, 1, 2, H, C), F32)
        out_spec = pl.BlockSpec((None, None, None, bh, C), lambda i, s: (layer, 0, s[1], i, 0))

    def body(sel_ref, g_ref, r1_ref, r2_ref, *rest):
        acc = g_ref[...].astype(F32) + r1_ref[...].astype(F32)
        for d in range(3):
            acc = acc + r2_ref[d].astype(F32)
        rest[-1][...] = acc

    in_specs = [pl.BlockSpec((None, None, bh, C), lambda i, s: (s[0], s[1], i, 0)),
                pl.BlockSpec((None, bh, C), lambda i, s: (s[0], i, 0)),
                pl.BlockSpec((3, bh, C), lambda i, s: (0, i, 0))]
    args = [sel, g4, recv1, recv2]
    aliases = {}
    if buf is not None:
        in_specs.append(ANY)
        args.append(buf)
        aliases = {4: 0}
    return pl.pallas_call(
        body, name=name, out_shape=out_shape,
        grid_spec=pltpu.PrefetchScalarGridSpec(num_scalar_prefetch=1, grid=(H // bh,), in_specs=in_specs,
                                               out_specs=out_spec),
        input_output_aliases=aliases,
        compiler_params=pltpu.CompilerParams(dimension_semantics=("arbitrary",), vmem_limit_bytes=VMEM_LIMIT),
    )(*args)


def _cast_place(shards, layer, sel, name):
    _, _, H, C = shards.shape
    bh = _row_block(H, C)

    def body(sel_ref, x_ref, o_ref):
        o_ref[...] = x_ref[...].astype(BF16)

    return pl.pallas_call(
        body, name=name, out_shape=jax.ShapeDtypeStruct((N_CHIPS, 2, H, C), BF16),
        grid_spec=pltpu.PrefetchScalarGridSpec(
            num_scalar_prefetch=1, grid=(2, H // bh),
            in_specs=[pl.BlockSpec((None, None, bh, C), lambda h, i, s: (layer, h, i, 0))],
            out_specs=pl.BlockSpec((None, None, bh, C), lambda h, i, s: (s[0], h, i, 0))),
        compiler_params=pltpu.CompilerParams(dimension_semantics=("arbitrary", "arbitrary"),
                                             vmem_limit_bytes=VMEM_LIMIT),
    )(sel, shards)


def _adamw_many(quads, name, comm=None):
    R, C = quads[0][0].shape
    br = _row_block(R, C)
    n = len(quads)

    def body(*refs):
        for k in range(n):
            g_ref, w_ref, m_ref, v_ref = refs[4 * k:4 * k + 4]
            d_ref, mo_ref, vo_ref = refs[4 * n + 3 * k:4 * n + 3 * k + 3]
            gv = g_ref[...]
            mn = ADAM_B1 * m_ref[...] + (1.0 - ADAM_B1) * gv
            vn = ADAM_B2 * v_ref[...] + (1.0 - ADAM_B2) * (gv * gv)
            m_hat = mn / (1.0 - ADAM_B1 ** ADAM_STEP)
            v_hat = vn / (1.0 - ADAM_B2 ** ADAM_STEP)
            d_ref[...] = -ADAM_LR * (m_hat / (jnp.sqrt(v_hat) + ADAM_EPS) + ADAM_WD * w_ref[...])
            mo_ref[...] = mn
            vo_ref[...] = vn

    spec = pl.BlockSpec((br, C), lambda i: (i, 0))
    res, got = _pcall(body, name=name, grid=(R // br,), comm=comm, out_shape=(jax.ShapeDtypeStruct((R, C), F32),) * (3 * n),
                      in_specs=[spec] * (4 * n), out_specs=(spec,) * (3 * n), args=[a for q in quads for a in q])
    return [res[3 * k:3 * k + 3] for k in range(n)], got


def _adamw(g, w, m, v, name):
    return _adamw_many([(g, w, m, v)], name)[0][0]


def _part_rows(shape):
    return -(-math.prod(shape) // (8 * LANES)) * 8


def _pack_rows(arrs, total_rows):
    parts = []
    for a in arrs:
        flat = a.reshape(-1)
        rows = _part_rows(a.shape)
        parts.append(jnp.pad(flat, (0, rows * LANES - flat.shape[0])).reshape(rows, LANES))
    packed = jnp.concatenate(parts, axis=0)
    return jnp.pad(packed, ((0, total_rows - packed.shape[0]), (0, 0)))


def _unpack_rows(packed, shapes):
    out, r = [], 0
    for shp in shapes:
        size, rows = math.prod(shp), _part_rows(shp)
        out.append(packed[r:r + rows].reshape(-1)[:size].reshape(shp))
        r += rows
    return out


def _rows_of(shapes):
    return sum(_part_rows(s) for s in shapes)


_GATHER_STAGES = {"start": ["wa", "mkv0", "wo0"], "mixer_a_fwd": ["wup0", "wdown0"],
                  "mlp_fwd_0": ["wq", "wkv", "mkv1", "wo1"], "fox_fwd": ["wup1", "wdown1"]}
_SWAP_STAGES = {"mixer_b_bwd": ["wup1", "wdown1"]}
_REDUCE_STAGES = {"fox_bwd": ["wup1", "wdown1"], "mlp_bwd_0": ["wo1", "wq", "wkv", "mkv1"],
                  "mixer_a_bwd": ["wup0", "wdown0"], "end": ["wa", "wo0", "mkv0", "small"]}
_SMALL = ["ws", "b_s", "sg", "sb", "bf", "ln_g", "ln_b", "loss"]


class _Staged:
    def __init__(self, placed, sel, dm, ng, shard_cols, n_small):
        self.placed, self.sel, self.dm, self.ng, self.shard_cols, self.n_small = placed, sel, dm, ng, shard_cols, n_small
        self.g4, self.recv1, self.recv2 = {}, {}, {}

    def _weight(self, name, a):
        a = a.reshape(N_CHIPS, 2 * a.shape[2], a.shape[3])
        if name in ("wa", "wkv"):
            a = a.transpose(1, 0, 2).reshape(a.shape[1], -1)
            if name == "wkv":
                a = jnp.concatenate([a[:, :2 * self.dm],
                                     jnp.pad(a[:, 2 * self.dm:], ((0, 0), (0, LANES - self.ng)))], axis=1)
            return a
        if name.startswith("wup") or name.startswith("wdown"):
            return a
        return a.reshape(-1, a.shape[2])

    def _partial(self, name, grads):
        if name == "small":
            a = _pack_rows([grads[n] for n in _SMALL], self.n_small).reshape(N_CHIPS, -1, LANES)
        elif name in self.shard_cols:
            n = self.shard_cols[name]
            g = grads[name][:, :N_CHIPS * n]
            a = g.reshape(g.shape[0], N_CHIPS, n).transpose(1, 0, 2)
        elif name.startswith("wup") or name.startswith("wdown"):
            a = grads[name]
        else:
            g = grads[name]
            a = g.reshape(N_CHIPS, g.shape[0] // N_CHIPS, g.shape[1])
        return a.reshape(N_CHIPS, 2, a.shape[1] // 2, a.shape[2])

    def plan(self, stage, w, grads):
        if stage in _GATHER_STAGES:
            return _gather_weights([self.placed[n] for n in _GATHER_STAGES[stage]])
        if stage in _SWAP_STAGES:
            for n in _SWAP_STAGES[stage]:
                self.g4[n] = self._partial(n, grads)
            return _pair_swap([self.g4[n] for n in _SWAP_STAGES[stage]])
        if stage in _REDUCE_STAGES:
            names = _REDUCE_STAGES[stage]
            late = [n for n in names if n not in self.recv1]
            for n in late:
                self.g4[n] = self._partial(n, grads)
            if late:
                got = _run_comm(_pair_swap([self.g4[n] for n in late]), "reduce_pair_swap_" + stage)
                self.recv1.update(zip(late, got))
            return _chip_exchange([_pair_add(self.g4[n], self.recv1[n], self.sel, F32 if n == "small" else BF16,
                                             "reduce_pair_add_" + n) for n in names])
        return None

    def done(self, stage, results, w):
        if stage in _GATHER_STAGES:
            for n, a in zip(_GATHER_STAGES[stage], results):
                w[n] = self._weight(n, a)
        elif stage in _SWAP_STAGES:
            self.recv1.update(zip(_SWAP_STAGES[stage], results))
        elif stage in _REDUCE_STAGES:
            self.recv2.update(zip(_REDUCE_STAGES[stage], results))


def kernel(x, mem, a_w_in, a_sgu_ln_g, a_sgu_ln_b, a_w_s, a_b_s, kv_w, kv_b_f, b_w_q, mem_w_kv, w_o, ln_g, ln_b, w_up, w_down, loss_target, m_a_w_in, m_a_sgu_ln_g, m_a_sgu_ln_b, m_a_w_s, m_a_b_s, m_kv_w, m_kv_b_f, m_b_w_q, m_mem_w_kv, m_w_o, m_ln_g, m_ln_b, m_w_up, m_w_down, v_a_w_in, v_a_sgu_ln_g, v_a_sgu_ln_b, v_a_w_s, v_a_b_s, v_kv_w, v_kv_b_f, v_b_w_q, v_mem_w_kv, v_w_o, v_ln_g, v_ln_b, v_w_up, v_w_down):
    xi, yi, ci = lax.axis_index("x"), lax.axis_index("y"), lax.axis_index("c")
    chip = 2 * xi + yi
    sel = jnp.stack([chip, ci]).astype(jnp.int32)
    S, D = x.shape[1], x.shape[2]
    DQ = mem_w_kv.shape[2] // 2
    DM = D - DQ
    NG = DM // HEAD_DIM

    shards = {"wa": (a_w_in, 0), "wkv": (kv_w[None], 0), "wq": (b_w_q, 0), "mkv0": (mem_w_kv, 0), "mkv1": (mem_w_kv, 1),
              "wo0": (w_o, 0), "wo1": (w_o, 1), "wup0": (w_up, 0), "wup1": (w_up, 1), "wdown0": (w_down, 0),
              "wdown1": (w_down, 1)}
    placed = {n: _cast_place(s.reshape(s.shape[0], 2, s.shape[1] // 2, s.shape[2]), layer, sel, "cast_place_" + n)
              for n, (s, layer) in shards.items()}
    ln_pack = jnp.concatenate([ln_g.reshape(4, -1), ln_b.reshape(4, -1)], axis=0)
    ln_buf = lax.dynamic_update_slice(jnp.zeros((N_CHIPS,) + ln_pack.shape, F32), ln_pack[None], (chip, 0, 0))
    ln_shape = (4, D)
    small_shapes = [a_w_s.shape, a_b_s.shape, a_sgu_ln_g.shape, a_sgu_ln_b.shape, kv_b_f.shape, ln_shape, ln_shape,
                    (1, LANES)]
    n_small = -(-_rows_of(small_shapes) // 64) * 64
    hooks = _Staged(placed, sel, DM, NG, {"wa": a_w_in.shape[2], "wkv": kv_w.shape[1]}, n_small)
    weights = {"sg": a_sgu_ln_g, "sb": a_sgu_ln_b, "ws": a_w_s[0], "b_s": a_b_s[0], "bf": kv_b_f}
    got = _run_comm(_gather_weights([placed[n] for n in _GATHER_STAGES["start"]], whole=[ln_buf]),
                    "gather_weights_first")
    hooks.done("start", got[:-1], weights)
    ln_full = got[-1].transpose(1, 0, 2).reshape(8, D)
    weights["ln_g"], weights["ln_b"] = ln_full[:4], ln_full[4:]

    _, grad_x, gr = _local_step(x[0], mem[0], loss_target[0], weights, hooks)

    def chip_sums(groups):
        bufs = []
        for grp in groups:
            buf = None
            for layer, n in enumerate(grp):
                buf = _chip_reduce(hooks.g4[n], hooks.recv1[n], hooks.recv2[n], sel, buf, layer, len(grp),
                                   "reduce_chip_sum_" + n)
            bufs.append(buf)
        return bufs

    whole = lambda a: a.reshape(a.shape[0], 2 * a.shape[3], a.shape[4])
    flat = lambda a: a.reshape(-1, a.shape[-1])

    def update(name, g, w, m, v):
        d, mn, vn = _adamw(flat(g), flat(w), flat(m), flat(v), "adamw_" + name)
        return d.reshape(w.shape), mn.reshape(w.shape), vn.reshape(w.shape)

    hooks.done("end", _run_comm(hooks.plan("end", weights, gr), "reduce_chip_exchange_end"), weights)
    small_mine = _chip_reduce(hooks.g4["small"], hooks.recv1["small"], hooks.recv2["small"], sel, None, 0, 1,
                              "reduce_chip_sum_small", by_chip=True)
    red, small_all = _pair_gather(chip_sums([["wa"], ["wkv"], ["wq"], ["mkv0", "mkv1"], ["wo0", "wo1"],
                                             ["wup0", "wup1"], ["wdown0", "wdown1"]]), small_mine, "reduce_pair_gather")
    g_a_w_in, g_kv_w, g_b_w_q, g_mem_w_kv, g_w_o, g_w_up, g_w_down = [whole(a) for a in red]
    g_kv_w = g_kv_w[0]
    g_w_down = g_w_down.transpose(0, 2, 1)
    small_all = small_all.reshape(n_small, LANES)
    g_ws, g_bs, g_sg, g_sb, g_bf, g_lng, g_lnb, loss_row = _unpack_rows(small_all, small_shapes)
    loss = loss_row[0, 0]
    dsh = D // N_CHIPS
    g_ln_g = lax.dynamic_slice_in_dim(g_lng, chip * dsh, dsh, axis=1).reshape(ln_g.shape)
    g_ln_b = lax.dynamic_slice_in_dim(g_lnb, chip * dsh, dsh, axis=1).reshape(ln_b.shape)

    upd = {
        "a_w_in": update("a_w_in", g_a_w_in, a_w_in, m_a_w_in, v_a_w_in),
        "kv_w": update("kv_w", g_kv_w, kv_w, m_kv_w, v_kv_w),
        "b_w_q": update("b_w_q", g_b_w_q, b_w_q, m_b_w_q, v_b_w_q),
        "mem_w_kv": update("mem_w_kv", g_mem_w_kv, mem_w_kv, m_mem_w_kv, v_mem_w_kv),
        "w_o": update("w_o", g_w_o, w_o, m_w_o, v_w_o),
        "w_up": update("w_up", g_w_up, w_up, m_w_up, v_w_up),
        "w_down": update("w_down", g_w_down, w_down, m_w_down, v_w_down),
    }
    tiny_g = [g_sg, g_sb, g_ws, g_bs, g_bf, g_ln_g, g_ln_b]
    tiny_w = [a_sgu_ln_g, a_sgu_ln_b, a_w_s, a_b_s, kv_b_f, ln_g, ln_b]
    tiny_m = [m_a_sgu_ln_g, m_a_sgu_ln_b, m_a_w_s, m_a_b_s, m_kv_b_f, m_ln_g, m_ln_b]
    tiny_v = [v_a_sgu_ln_g, v_a_sgu_ln_b, v_a_w_s, v_a_b_s, v_kv_b_f, v_ln_g, v_ln_b]
    tiny_shapes = [a.shape for a in tiny_w]
    n_tiny = -(-_rows_of(tiny_shapes) // 8) * 8
    td, tm_, tv = _adamw(_pack_rows(tiny_g, n_tiny), _pack_rows(tiny_w, n_tiny), _pack_rows(tiny_m, n_tiny),
                         _pack_rows(tiny_v, n_tiny), "adamw_small")
    for name, d, mn, vn in zip(["a_sgu_ln_g", "a_sgu_ln_b", "a_w_s", "a_b_s", "kv_b_f", "ln_g", "ln_b"],
                               _unpack_rows(td, tiny_shapes), _unpack_rows(tm_, tiny_shapes),
                               _unpack_rows(tv, tiny_shapes)):
        upd[name] = (d, mn, vn)

    order = ["a_w_in", "a_sgu_ln_g", "a_sgu_ln_b", "a_w_s", "a_b_s", "kv_w", "kv_b_f", "b_w_q", "mem_w_kv", "w_o",
             "ln_g", "ln_b", "w_up", "w_down"]
    grads = {"a_w_in": g_a_w_in.reshape(a_w_in.shape), "a_sgu_ln_g": g_sg.reshape(a_sgu_ln_g.shape),
             "a_sgu_ln_b": g_sb.reshape(a_sgu_ln_b.shape), "a_w_s": g_ws.reshape(a_w_s.shape),
             "a_b_s": g_bs.reshape(a_b_s.shape), "kv_w": g_kv_w, "kv_b_f": g_bf, "b_w_q": g_b_w_q.reshape(b_w_q.shape),
             "mem_w_kv": g_mem_w_kv, "w_o": g_w_o, "ln_g": g_ln_g, "ln_b": g_ln_b, "w_up": g_w_up, "w_down": g_w_down}
    return (loss, grad_x[None], *[grads[n] for n in order], *[upd[n][0] for n in order],
            *[upd[n][1] for n in order], *[upd[n][2] for n in order])
```

```python
import functools
import math

import jax
import jax.numpy as jnp
from jax import lax
from jax.experimental import pallas as pl
from jax.experimental.pallas import tpu as pltpu

F32 = jnp.float32
BF16 = jnp.bfloat16

HEAD_DIM = 64
PAIR = 2 * HEAD_DIM
CHUNK = 128
LN_EPS = 1e-5
ALPHA = 4 ** 0.25
QK_SCALE = 1.0 / math.sqrt(HEAD_DIM)
NEG = -1e30
N_CHIPS = 4
FOX_BLOCK = 256
LANES = 128
VMEM_LIMIT = 48 * 2 ** 20

ADAM_LR, ADAM_B1, ADAM_B2, ADAM_EPS, ADAM_WD, ADAM_STEP = 0.001, 0.9, 0.999, 1e-08, 0.01, 10

MESH = pl.DeviceIdType.MESH
ANY = pl.BlockSpec(memory_space=pl.ANY)


def _dot(a, b):
    return jnp.dot(a, b, preferred_element_type=F32)


def _dot_nt(a, b):
    return lax.dot_general(a, b, (((1,), (1,)), ((), ())), preferred_element_type=F32)


def _dot_tn(a, b):
    return lax.dot_general(a, b, (((0,), (0,)), ((), ())), preferred_element_type=F32)


def _ln_stats(r):
    mu = jnp.mean(r, axis=-1, keepdims=True)
    d = r - mu
    var = jnp.mean(d * d, axis=-1, keepdims=True)
    rstd = lax.rsqrt(var + LN_EPS)
    return d * rstd, rstd


def _ln_bwd(dy, xhat, rstd, g):
    dxh = dy * g
    m1 = jnp.mean(dxh, axis=-1, keepdims=True)
    m2 = jnp.mean(dxh * xhat, axis=-1, keepdims=True)
    return rstd * (dxh - m1 - xhat * m2)


_GELU_K = math.sqrt(2.0 / math.pi)


def _gelu(x):
    return 0.5 * x * (1.0 + jnp.tanh(_GELU_K * (x + 0.044715 * x * x * x)))


def _gelu_grad(x):
    t = jnp.tanh(_GELU_K * (x + 0.044715 * x * x * x))
    return 0.5 * (1.0 + t) + 0.5 * x * (1.0 - t * t) * _GELU_K * (1.0 + 3 * 0.044715 * x * x)


def _lane_mask(e):
    lane = lax.broadcasted_iota(jnp.int32, (1, PAIR), 1)
    return (lane >= HEAD_DIM * e) & (lane < HEAD_DIM * (e + 1))


def _first_head():
    return lax.broadcasted_iota(jnp.int32, (1, PAIR), 1) < HEAD_DIM


def _mem_probs(qh, mkp):
    sc = _dot_nt(qh, mkp) * QK_SCALE
    ex = jnp.exp(sc - jnp.max(sc, axis=-1, keepdims=True))
    return ex / jnp.sum(ex, axis=-1, keepdims=True)


def _mem_attn_fwd(qm_b, mkv_ref, dq_dim):
    outs = []
    for mp in range(dq_dim // PAIR):
        qp = qm_b[:, mp * PAIR:(mp + 1) * PAIR]
        mkp = mkv_ref[:, mp * PAIR:(mp + 1) * PAIR]
        mvp = mkv_ref[:, dq_dim + mp * PAIR:dq_dim + (mp + 1) * PAIR]
        heads = []
        for e in (0, 1):
            qh = jnp.where(_lane_mask(e), qp, jnp.zeros_like(qp))
            p = _mem_probs(qh, mkp)
            heads.append(_dot(p.astype(BF16), mvp))
        outs.append(jnp.where(_first_head(), heads[0], heads[1]))
    return outs


def _mem_attn_bwd(qm_b, mkv_ref, dmo_pairs, dmkv_ref, dq_dim):
    dqs = []
    for mp in range(dq_dim // PAIR):
        ks = slice(mp * PAIR, (mp + 1) * PAIR)
        vs = slice(dq_dim + mp * PAIR, dq_dim + (mp + 1) * PAIR)
        qp = qm_b[:, ks]
        mkp = mkv_ref[:, ks]
        mvp = mkv_ref[:, vs]
        dmo_b = dmo_pairs[mp].astype(BF16)
        dq = None
        dmk = None
        dmv = None
        for e in (0, 1):
            hm = _lane_mask(e)
            qh = jnp.where(hm, qp, jnp.zeros_like(qp))
            p = _mem_probs(qh, mkp)
            doh = jnp.where(hm, dmo_b, jnp.zeros_like(dmo_b))
            dp = _dot_nt(doh, mvp)
            ds = p * (dp - jnp.sum(dp * p, axis=-1, keepdims=True))
            dsb = (ds * QK_SCALE).astype(BF16)
            kh = jnp.where(hm, mkp, jnp.zeros_like(mkp))
            dq_e = _dot(dsb, kh)
            dmk_e = _dot_tn(dsb, qh)
            dmv_e = _dot_tn(p.astype(BF16), doh)
            dq = dq_e if dq is None else dq + dq_e
            dmk = dmk_e if dmk is None else dmk + dmk_e
            dmv = dmv_e if dmv is None else dmv + dmv_e
        dmkv_ref[:, ks] += dmk
        dmkv_ref[:, vs] += dmv
        dqs.append(dq)
    return dqs


def _params(n_axes=1):
    return pltpu.CompilerParams(dimension_semantics=("arbitrary",) * n_axes, vmem_limit_bytes=VMEM_LIMIT)


def _full(shape):
    return pl.BlockSpec(shape, lambda *_: (0,) * len(shape))


def _rows(tm, cols):
    return pl.BlockSpec((tm, cols), lambda i: (i, 0))


def _cols(rows, tm):
    return pl.BlockSpec((rows, tm), lambda i: (0, i))


def _pick(n, pref):
    if n <= pref:
        return n
    best = LANES
    for t in range(LANES, pref + 1, LANES):
        if n % t == 0:
            best = t
    return best if 2 * best >= pref or n > 2 * pref else n


class _Comm:
    def __init__(self, ins, out_shapes, aliases, sem_counts, start, finish, middle=None):
        self.ins, self.out_shapes, self.aliases, self.sem_counts = list(ins), list(out_shapes), dict(aliases), sem_counts
        self.start, self.finish = start, finish
        self.middle = middle


def _pcall(body, *, name, grid, in_specs, out_specs, out_shape, args, scratch_shapes=(), comm=None, aliases=None):
    n_in, n_out, n_scr = len(in_specs), len(out_shape), len(scratch_shapes)
    aliases = dict(aliases or {})
    if comm is None:
        res = pl.pallas_call(body, name=name, grid=grid, out_shape=tuple(out_shape), in_specs=list(in_specs),
                             out_specs=tuple(out_specs), scratch_shapes=list(scratch_shapes),
                             input_output_aliases=aliases, compiler_params=_params())(*args)
        return tuple(res), ()
    nci, nco = len(comm.ins), len(comm.out_shapes)
    last = grid[0] - 1

    def wrapped(*refs):
        ins, refs = refs[:n_in], refs[n_in:]
        cins, refs = refs[:nci], refs[nci:]
        outs, refs = refs[:n_out], refs[n_out:]
        couts, refs = refs[:nco], refs[nco:]
        scr, sems = refs[:n_scr], refs[n_scr:]

        @pl.when(pl.program_id(0) == 0)
        def _():
            comm.start(cins, couts, sems)

        if comm.middle is not None:
            @pl.when(pl.program_id(0) == (last + 1) // 2)
            def _():
                comm.middle(cins, couts, sems)

        body(*ins, *outs, *scr)

        @pl.when(pl.program_id(0) == last)
        def _():
            comm.finish(cins, couts, sems)

    res = pl.pallas_call(
        wrapped, name=name, grid=grid, out_shape=tuple(out_shape) + tuple(comm.out_shapes),
        in_specs=list(in_specs) + [ANY] * nci, out_specs=tuple(out_specs) + (ANY,) * nco,
        input_output_aliases={**aliases, **{n_in + a: n_out + b for a, b in comm.aliases.items()}},
        scratch_shapes=list(scratch_shapes) + [pltpu.SemaphoreType.DMA((k,)) for k in comm.sem_counts],
        compiler_params=pltpu.CompilerParams(dimension_semantics=("arbitrary",), vmem_limit_bytes=VMEM_LIMIT,
                                             has_side_effects=True),
    )(*args, *comm.ins)
    return tuple(res[:n_out]), tuple(res[n_out:])


def _run_comm(comm, name):
    nci, nco = len(comm.ins), len(comm.out_shapes)

    def body(*refs):
        cins, couts, sems = refs[:nci], refs[nci:nci + nco], refs[nci + nco:]
        comm.start(cins, couts, sems)
        if comm.middle is not None:
            comm.middle(cins, couts, sems)
        comm.finish(cins, couts, sems)

    return pl.pallas_call(
        body, name=name, out_shape=tuple(comm.out_shapes), in_specs=[ANY] * nci, out_specs=(ANY,) * nco,
        input_output_aliases=comm.aliases, scratch_shapes=[pltpu.SemaphoreType.DMA((k,)) for k in comm.sem_counts],
        compiler_params=pltpu.CompilerParams(has_side_effects=True),
    )(*comm.ins)


def _mm(a, b, name, a_is_transposed, out_split=1):
    (M, R) = a.shape if a_is_transposed else a.shape[::-1]
    _, N = b.shape
    bm, br = _pick(M, 1024 if a_is_transposed else 512), _pick(R, 1024 if a_is_transposed else 512)
    ncol = N // out_split
    bn = _pick(ncol, 1024)
    per = ncol // bn
    last = R // br - 1

    def body(a_ref, b_ref, o_ref, acc_ref):
        @pl.when(pl.program_id(2) == 0)
        def _():
            acc_ref[...] = jnp.zeros_like(acc_ref)

        acc_ref[...] += (_dot if a_is_transposed else _dot_tn)(a_ref[...], b_ref[...])

        @pl.when(pl.program_id(2) == last)
        def _():
            o_ref[...] = acc_ref[...].astype(BF16)

    if out_split == 1:
        out_shape = jax.ShapeDtypeStruct((M, N), BF16)
        out_spec = pl.BlockSpec((bm, bn), lambda i, j, r: (i, j))
    else:
        out_shape = jax.ShapeDtypeStruct((out_split, M, ncol), BF16)
        out_spec = pl.BlockSpec((None, bm, bn), lambda i, j, r: (j // per, i, j % per))
    a_spec = (pl.BlockSpec((bm, br), lambda i, j, r: (i, r)) if a_is_transposed
              else pl.BlockSpec((br, bm), lambda i, j, r: (r, i)))
    return pl.pallas_call(
        body, name=name, out_shape=out_shape, grid=(M // bm, N // bn, R // br),
        in_specs=[a_spec, pl.BlockSpec((br, bn), lambda i, j, r: (r, j))], out_specs=out_spec,
        scratch_shapes=[pltpu.VMEM((bm, bn), F32)],
        compiler_params=pltpu.CompilerParams(dimension_semantics=("parallel", "parallel", "arbitrary"),
                                             vmem_limit_bytes=VMEM_LIMIT),
    )(a, b)


def _mm_tn(a, b, name, out_split=1):
    return _mm(a, b, name, False, out_split)


def _mm_nn(at, b, name, out_split=1):
    return _mm(at, b, name, True, out_split)


def _mm_small(a, b, name):
    def body(a_ref, b_ref, o_ref):
        o_ref[...] = _dot(a_ref[...], b_ref[...]).astype(BF16)

    return pl.pallas_call(body, name=name, out_shape=jax.ShapeDtypeStruct((a.shape[0], b.shape[1]), BF16),
                          compiler_params=pltpu.CompilerParams(vmem_limit_bytes=VMEM_LIMIT))(a, b)


def _mlp_out(x_ref, wup_ref, wdown_ref, g_ref, b_ref, r_ref, h_ref):
    NC, _, FC = wup_ref.shape
    xv = x_ref[...]
    xb = xv.astype(BF16)
    y = jnp.zeros(xv.shape, F32)
    for c in range(NC):
        a = jnp.maximum(_dot(xb, wup_ref[c]), 0.0)
        hb = (a * a).astype(BF16)
        h_ref[:, c * FC:(c + 1) * FC] = hb
        y = y + _dot(hb, wdown_ref[c])
    r = ALPHA * xv + y
    r_ref[...] = r
    xhat, _ = _ln_stats(r)
    return xhat * g_ref[...] + b_ref[...]


def _mlp_fwd_loss(x, wup4, wdown4, g, b, target, name):
    S, D = x.shape
    NC, _, FC = wup4.shape
    tm = min(256, S)

    def body(x_ref, wup_ref, wdown_ref, g_ref, b_ref, t_ref, d_ref, r_ref, h_ref, l_ref):
        @pl.when(pl.program_id(0) == 0)
        def _():
            l_ref[...] = jnp.zeros_like(l_ref)

        err = _mlp_out(x_ref, wup_ref, wdown_ref, g_ref, b_ref, r_ref, h_ref) - t_ref[...]
        d_ref[...] = err * (1.0 / D)
        l_ref[...] += (0.5 / D) * jnp.sum(err * err)

    return pl.pallas_call(
        body, name=name, grid=(S // tm,),
        out_shape=(jax.ShapeDtypeStruct((S, D), F32), jax.ShapeDtypeStruct((S, D), F32),
                   jax.ShapeDtypeStruct((S, NC * FC), BF16), jax.ShapeDtypeStruct((8, LANES), F32)),
        in_specs=[_rows(tm, D), _full(wup4.shape), _full(wdown4.shape), _full((1, D)), _full((1, D)), _rows(tm, D)],
        out_specs=(_rows(tm, D), _rows(tm, D), _rows(tm, NC * FC), _full((8, LANES))),
        compiler_params=_params(),
    )(x, wup4, wdown4, g, b, target)


def _mlp_fwd(x, wup4, wdown4, g, b, name, comm=None):
    S, D = x.shape
    NC, _, FC = wup4.shape
    tm = min(256, S)

    def body(x_ref, wup_ref, wdown_ref, g_ref, b_ref, xo_ref, xb_ref, xbt_ref, r_ref, h_ref):
        xo = _mlp_out(x_ref, wup_ref, wdown_ref, g_ref, b_ref, r_ref, h_ref)
        xo_ref[...] = xo
        xb_ref[...] = xo.astype(BF16)
        xbt_ref[...] = xo.T.astype(BF16)

    return _pcall(
        body, name=name, grid=(S // tm,), comm=comm,
        out_shape=(jax.ShapeDtypeStruct((S, D), F32), jax.ShapeDtypeStruct((S, D), BF16),
                   jax.ShapeDtypeStruct((D, S), BF16), jax.ShapeDtypeStruct((S, D), F32),
                   jax.ShapeDtypeStruct((S, NC * FC), BF16)),
        in_specs=[_rows(tm, D), _full(wup4.shape), _full(wdown4.shape), _full((1, D)), _full((1, D))],
        out_specs=(_rows(tm, D), _rows(tm, D), _cols(D, tm), _rows(tm, D), _rows(tm, NC * FC)),
        args=(x, wup4, wdown4, g, b))


def _mlp_bwd(dxo, r, h, wup4, wdown4, g, name, comm=None):
    S, D = r.shape
    NC, _, FC = wup4.shape
    tm = min(256, S)

    def body(dxo_ref, r_ref, h_ref, wup_ref, wdown_ref, g_ref, dx_ref, drbt_ref, dpre_ref, dgb_ref):
        @pl.when(pl.program_id(0) == 0)
        def _():
            dgb_ref[...] = jnp.zeros_like(dgb_ref)

        xhat, rstd = _ln_stats(r_ref[...])
        dy = dxo_ref[...]
        dgb_ref[0:1, :] += jnp.sum(dy * xhat, axis=0, keepdims=True)
        dgb_ref[1:2, :] += jnp.sum(dy, axis=0, keepdims=True)
        dr = _ln_bwd(dy, xhat, rstd, g_ref[...])
        drb = dr.astype(BF16)
        drbt_ref[...] = dr.T.astype(BF16)
        dx = ALPHA * dr
        for c in range(NC):
            dh = _dot_nt(drb, wdown_ref[c])
            a = jnp.sqrt(h_ref[:, c * FC:(c + 1) * FC].astype(F32))
            dpre = (2.0 * a * dh).astype(BF16)
            dpre_ref[:, c * FC:(c + 1) * FC] = dpre
            dx = dx + _dot_nt(dpre, wup_ref[c])
        dx_ref[...] = dx

    return _pcall(
        body, name=name, grid=(S // tm,), comm=comm,
        out_shape=(jax.ShapeDtypeStruct((S, D), F32), jax.ShapeDtypeStruct((D, S), BF16),
                   jax.ShapeDtypeStruct((S, NC * FC), BF16), jax.ShapeDtypeStruct((2, D), F32)),
        in_specs=[_rows(tm, D), _rows(tm, D), _rows(tm, NC * FC), _full(wup4.shape), _full(wdown4.shape),
                  _full((1, D))],
        out_specs=(_rows(tm, D), _cols(D, tm), _rows(tm, NC * FC), _full((2, D))),
        args=(dxo, r, h, wup4, wdown4, g))


def _tail_fwd_store(xv, cat_ref, wo_ref, g_ref, b_ref, xo_ref, xbt_ref, r_ref, catt_ref):
    cat = cat_ref[...]
    y = _dot(cat, wo_ref[...])
    r = ALPHA * xv + y
    xhat, _ = _ln_stats(r)
    xo = xhat * g_ref[...] + b_ref[...]
    xo_ref[...] = xo
    xbt_ref[...] = xo.T.astype(BF16)
    catt_ref[...] = cat.astype(F32).T.astype(BF16)
    r_ref[...] = r


def _tail_bwd_head(dxo_ref, r_ref, g_ref, wo_ref, dgb_ref, drb_ref):
    xhat, rstd = _ln_stats(r_ref[...])
    dy = dxo_ref[...]
    dgb_ref[0:1, :] += jnp.sum(dy * xhat, axis=0, keepdims=True)
    dgb_ref[1:2, :] += jnp.sum(dy, axis=0, keepdims=True)
    dr = _ln_bwd(dy, xhat, rstd, g_ref[...])
    drb = dr.astype(BF16)
    drb_ref[...] = drb
    return dr, _dot_nt(drb, wo_ref[...])


def _causal(shape):
    return lax.broadcasted_iota(jnp.int32, shape, 1) <= lax.broadcasted_iota(jnp.int32, shape, 0)


def _sgu_mixed(ws_ref, bst_ref, gp, vpair):
    tril = _causal((CHUNK, CHUNK))
    w0 = jnp.where(tril, ws_ref[2 * gp], 0.0).astype(BF16)
    w1 = jnp.where(tril, ws_ref[2 * gp + 1], 0.0).astype(BF16)
    m0 = _dot(w0, vpair) + bst_ref[:, 2 * gp:2 * gp + 1]
    m1 = _dot(w1, vpair) + bst_ref[:, 2 * gp + 1:2 * gp + 2]
    return jnp.where(_first_head(), m0, m1), w0, w1


def _mixer_a_fwd(x, wa, sg, sb, ws, bst, mkv, wo, g, b, name, comm=None):
    S, D = x.shape
    DQ = mkv.shape[1] // 2
    DM = D - DQ
    NP = DM // PAIR
    tm = min(512, S)

    def body(x_ref, wa_ref, sg_ref, sb_ref, ws_ref, bst_ref, mkv_ref, wo_ref, g_ref, b_ref,
             xo_ref, xbt_ref, r_ref, catt_ref, xint_ref, z_s, zv_s, cat_ref):
        xv = x_ref[...]
        xint_ref[...] = xv.T.astype(BF16)
        z_s[...] = _dot(xv.astype(BF16), wa_ref[...])
        vhat, _ = _ln_stats(_gelu(z_s[:, DM:2 * DM]))
        zv_s[...] = (vhat * sg_ref[...] + sb_ref[...]).astype(BF16)
        for c in range(tm // CHUNK):
            rs = slice(c * CHUNK, (c + 1) * CHUNK)
            for gp in range(NP):
                ls = slice(gp * PAIR, (gp + 1) * PAIR)
                mixed, _, _ = _sgu_mixed(ws_ref, bst_ref, gp, zv_s[rs, ls])
                cat_ref[rs, ls] = (_gelu(z_s[rs, ls]) * mixed).astype(BF16)
        mo = _mem_attn_fwd(z_s[:, 2 * DM:].astype(BF16), mkv_ref, DQ)
        for mp in range(DQ // PAIR):
            cat_ref[:, DM + mp * PAIR:DM + (mp + 1) * PAIR] = mo[mp].astype(BF16)
        _tail_fwd_store(xv, cat_ref, wo_ref, g_ref, b_ref, xo_ref, xbt_ref, r_ref, catt_ref)

    return _pcall(
        body, name=name, grid=(S // tm,), comm=comm,
        out_shape=(jax.ShapeDtypeStruct((S, D), F32), jax.ShapeDtypeStruct((D, S), BF16),
                   jax.ShapeDtypeStruct((S, D), F32), jax.ShapeDtypeStruct((D, S), BF16),
                   jax.ShapeDtypeStruct((D, S), BF16)),
        in_specs=[_rows(tm, D), _full(wa.shape), _full(sg.shape), _full(sb.shape), _full(ws.shape), _full(bst.shape),
                  _full(mkv.shape), _full(wo.shape), _full((1, D)), _full((1, D))],
        out_specs=(_rows(tm, D), _cols(D, tm), _rows(tm, D), _cols(D, tm), _cols(D, tm)),
        scratch_shapes=[pltpu.VMEM((tm, 2 * DM + DQ), F32), pltpu.VMEM((tm, DM), BF16), pltpu.VMEM((tm, D), BF16)],
        args=(x, wa, sg, sb, ws, bst, mkv, wo, g, b))


def _mixer_a_bwd(x, dxo, r, wa, sg, sb, ws, bst, mkv, wo, g, name, comm=None):
    S, D = x.shape
    DQ = mkv.shape[1] // 2
    DM = D - DQ
    NP = DM // PAIR
    NG = DM // HEAD_DIM
    tm = min(512, S)

    def body(x_ref, dxo_ref, r_ref, wa_ref, sg_ref, sb_ref, ws_ref, bst_ref, mkv_ref, wo_ref, g_ref,
             dx_ref, dz_ref, drb_ref, dgb_ref, dsgb_ref, dws_ref, dbst_ref, dmkv_ref, z_s, zv_s, dzv_s):
        @pl.when(pl.program_id(0) == 0)
        def _():
            dgb_ref[...] = jnp.zeros_like(dgb_ref)
            dsgb_ref[...] = jnp.zeros_like(dsgb_ref)
            dws_ref[...] = jnp.zeros_like(dws_ref)
            dbst_ref[...] = jnp.zeros_like(dbst_ref)
            dmkv_ref[...] = jnp.zeros_like(dmkv_ref)

        dr, dcat = _tail_bwd_head(dxo_ref, r_ref, g_ref, wo_ref, dgb_ref, drb_ref)
        z_s[...] = _dot(x_ref[...].astype(BF16), wa_ref[...])
        vhat, vrstd = _ln_stats(_gelu(z_s[:, DM:2 * DM]))
        zv_s[...] = (vhat * sg_ref[...] + sb_ref[...]).astype(BF16)
        tril = _causal((CHUNK, CHUNK))
        lane = lax.broadcasted_iota(jnp.int32, (1, LANES), 1)
        for c in range(tm // CHUNK):
            rs = slice(c * CHUNK, (c + 1) * CHUNK)
            for gp in range(NP):
                ls = slice(gp * PAIR, (gp + 1) * PAIR)
                vpair = zv_s[rs, ls]
                mixed, w0, w1 = _sgu_mixed(ws_ref, bst_ref, gp, vpair)
                u_pre = z_s[rs, ls]
                dmix = dcat[rs, ls]
                dz_ref[rs, ls] = (dmix * mixed * _gelu_grad(u_pre)).astype(BF16)
                dmixed = dmix * _gelu(u_pre)
                first = _first_head()
                d0 = jnp.where(first, dmixed, 0.0)
                d1 = jnp.where(first, 0.0, dmixed)
                d0b = d0.astype(BF16)
                d1b = d1.astype(BF16)
                dzv_s[rs, ls] = _dot_tn(w0, d0b) + _dot_tn(w1, d1b)
                dws_ref[2 * gp] += jnp.where(tril, _dot_nt(d0b, vpair), 0.0)
                dws_ref[2 * gp + 1] += jnp.where(tril, _dot_nt(d1b, vpair), 0.0)
                dbst_ref[...] += (jnp.where(lane == 2 * gp, jnp.sum(d0, axis=-1, keepdims=True), 0.0)
                                  + jnp.where(lane == 2 * gp + 1, jnp.sum(d1, axis=-1, keepdims=True), 0.0))
        dzv = dzv_s[...]
        dsgb_ref[0:1, :] += jnp.sum(dzv * vhat, axis=0, keepdims=True)
        dsgb_ref[1:2, :] += jnp.sum(dzv, axis=0, keepdims=True)
        dgv = _ln_bwd(dzv, vhat, vrstd, sg_ref[...])
        dz_ref[:, DM:2 * DM] = (dgv * _gelu_grad(z_s[:, DM:2 * DM])).astype(BF16)
        dmo = [dcat[:, DM + mp * PAIR:DM + (mp + 1) * PAIR] for mp in range(DQ // PAIR)]
        dqm = _mem_attn_bwd(z_s[:, 2 * DM:].astype(BF16), mkv_ref, dmo, dmkv_ref, DQ)
        for mp in range(DQ // PAIR):
            dz_ref[:, 2 * DM + mp * PAIR:2 * DM + (mp + 1) * PAIR] = dqm[mp].astype(BF16)
        dx_ref[...] = ALPHA * dr + _dot_nt(dz_ref[...], wa_ref[...])

    ZW = 2 * DM + DQ
    M = mkv.shape[0]
    return _pcall(
        body, name=name, grid=(S // tm,), comm=comm,
        out_shape=(jax.ShapeDtypeStruct((S, D), F32), jax.ShapeDtypeStruct((S, ZW), BF16),
                   jax.ShapeDtypeStruct((S, D), BF16), jax.ShapeDtypeStruct((2, D), F32),
                   jax.ShapeDtypeStruct((2, DM), F32), jax.ShapeDtypeStruct((NG, CHUNK, CHUNK), F32),
                   jax.ShapeDtypeStruct((CHUNK, LANES), F32), jax.ShapeDtypeStruct((M, 2 * DQ), F32)),
        in_specs=[_rows(tm, D), _rows(tm, D), _rows(tm, D), _full(wa.shape), _full(sg.shape), _full(sb.shape),
                  _full(ws.shape), _full(bst.shape), _full(mkv.shape), _full(wo.shape), _full((1, D))],
        out_specs=(_rows(tm, D), _rows(tm, ZW), _rows(tm, D), _full((2, D)), _full((2, DM)),
                   _full((NG, CHUNK, CHUNK)), _full((CHUNK, LANES)), _full((M, 2 * DQ))),
        scratch_shapes=[pltpu.VMEM((tm, ZW), F32), pltpu.VMEM((tm, DM), BF16), pltpu.VMEM((tm, DM), F32)],
        args=(x, dxo, r, wa, sg, sb, ws, bst, mkv, wo, g))


def _proj_b_fwd(xb, wq, wkv, dm, name):
    S, D = xb.shape
    DQ = D - dm
    tm = min(512, S)

    def body(x_ref, wq_ref, wkv_ref, q_ref, qm_ref, k_ref, v_ref, fl_ref):
        xv = x_ref[...]
        z = _dot(xv, wq_ref[...])
        q_ref[...] = z[:, :dm].astype(BF16)
        qm_ref[...] = z[:, dm:].astype(BF16)
        kvf = _dot(xv, wkv_ref[...])
        k_ref[...] = kvf[:, :dm].astype(BF16)
        v_ref[...] = kvf[:, dm:2 * dm].astype(BF16)
        fl_ref[...] = kvf[:, 2 * dm:]

    return pl.pallas_call(
        body, name=name, grid=(S // tm,),
        out_shape=(jax.ShapeDtypeStruct((S, dm), BF16), jax.ShapeDtypeStruct((S, DQ), BF16),
                   jax.ShapeDtypeStruct((S, dm), BF16), jax.ShapeDtypeStruct((S, dm), BF16),
                   jax.ShapeDtypeStruct((S, LANES), F32)),
        in_specs=[_rows(tm, D), _full(wq.shape), _full(wkv.shape)],
        out_specs=(_rows(tm, dm), _rows(tm, DQ), _rows(tm, dm), _rows(tm, dm), _rows(tm, LANES)),
        compiler_params=_params(),
    )(xb, wq, wkv)


def _proj_b_bwd(dr, dqz, dkvf, wq, wkv, name):
    S, D = dr.shape
    tm = min(512, S)

    def body(dr_ref, dqz_ref, dkvf_ref, wq_ref, wkv_ref, dx_ref):
        dx_ref[...] = (ALPHA * dr_ref[...] + _dot_nt(dqz_ref[...], wq_ref[...])
                       + _dot_nt(dkvf_ref[...], wkv_ref[...]))

    return pl.pallas_call(
        body, name=name, grid=(S // tm,), out_shape=jax.ShapeDtypeStruct((S, D), F32),
        in_specs=[_rows(tm, D), _rows(tm, dqz.shape[1]), _rows(tm, dkvf.shape[1]), _full(wq.shape), _full(wkv.shape)],
        out_specs=_rows(tm, D), compiler_params=_params(),
    )(dr, dqz, dkvf, wq, wkv)


def _split3(v):
    hi = v.astype(BF16)
    r1 = v - hi.astype(F32)
    mid = r1.astype(BF16)
    lo = (r1 - mid.astype(F32)).astype(BF16)
    return hi, mid, lo


def _tri_sum(v, tri_b):
    hi, mid, lo = _split3(v)
    return _dot(hi, tri_b) + _dot(mid, tri_b) + _dot(lo, tri_b)


def _log_sigmoid(x):
    return jnp.minimum(x, 0.0) - jnp.log(1.0 + jnp.exp(-jnp.abs(x)))


def _forget_cumsum(flt, bf, name):
    H, S = flt.shape
    nchunk = S // LANES

    def body(fl_ref, bf_ref, p_ref):
        upper = (lax.broadcasted_iota(jnp.int32, (LANES, LANES), 0)
                 <= lax.broadcasted_iota(jnp.int32, (LANES, LANES), 1)).astype(BF16)

        def step(n, carry):
            s0 = pl.multiple_of(n * LANES, LANES)
            lf = _log_sigmoid(fl_ref[:, pl.ds(s0, LANES)] + bf_ref[...])
            cs = _tri_sum(lf, upper) + carry
            for i, piece in enumerate(_split3(-cs)):
                p_ref[i, :, pl.ds(s0, LANES)] = piece.astype(F32)
            return cs[:, LANES - 1:LANES]

        lax.fori_loop(0, nchunk, step, jnp.zeros((H, 1), F32))

    return pl.pallas_call(body, name=name, out_shape=jax.ShapeDtypeStruct((3, H, S), F32),
                          compiler_params=pltpu.CompilerParams(vmem_limit_bytes=VMEM_LIMIT))(flt, bf)


def _forget_cumsum_bwd(dct, flt, bf, name):
    H, S = flt.shape
    nchunk = S // LANES

    def body(dc_ref, fl_ref, bf_ref, dfl_ref, dbf_ref):
        lower = (lax.broadcasted_iota(jnp.int32, (LANES, LANES), 0)
                 >= lax.broadcasted_iota(jnp.int32, (LANES, LANES), 1)).astype(BF16)

        def step(n, carry):
            tail, tot = carry
            s0 = pl.multiple_of((nchunk - 1 - n) * LANES, LANES)
            suffix = _tri_sum(dc_ref[:, pl.ds(s0, LANES)], lower) + tail
            xv = fl_ref[:, pl.ds(s0, LANES)] + bf_ref[...]
            dfl = suffix * (1.0 / (1.0 + jnp.exp(xv)))
            dfl_ref[:, pl.ds(s0, LANES)] = dfl
            return suffix[:, 0:1], tot + jnp.sum(dfl, axis=-1, keepdims=True)

        _, tot = lax.fori_loop(0, nchunk, step, (jnp.zeros((H, 1), F32), jnp.zeros((H, 1), F32)))
        dbf_ref[...] = jnp.broadcast_to(tot, (H, LANES))

    return pl.pallas_call(body, name=name,
                          out_shape=(jax.ShapeDtypeStruct((H, S), F32), jax.ShapeDtypeStruct((H, LANES), F32)),
                          compiler_params=pltpu.CompilerParams(vmem_limit_bytes=VMEM_LIMIT))(dct, flt, bf)


N_BIAS = 3


def _bias_ones():
    lane = lax.broadcasted_iota(jnp.int32, (1, PAIR), 1)
    return ((lane & (HEAD_DIM - 1)) < N_BIAS).astype(BF16)


def _fox_fwd(q, k, v, ca, name, comm=None):
    S, DM = q.shape
    NP = DM // PAIR
    T = min(FOX_BLOCK, S)
    nq = S // T

    def body(q_ref, k_ref, v_ref, ca_ref, o_ref, lse_ref):
        tril = _causal((T, T))
        heads = (_lane_mask(0), _lane_mask(1))
        ones3 = _bias_ones()

        def kv_block(kj):
            s0 = pl.multiple_of(kj * T, T)
            kb, vb, cab = k_ref[pl.ds(s0, T), :], v_ref[pl.ds(s0, T), :], ca_ref[pl.ds(s0, T), :]
            return ([jnp.where(heads[e], kb, cab) for e in (0, 1)],
                    [jnp.where(heads[e], vb, jnp.ones_like(vb)) for e in (0, 1)])

        def q_pair(a, _):
            t0s = [pl.multiple_of((2 * a + r) * T, T) for r in (0, 1)]
            qaug = {}
            for r in (0, 1):
                qb = q_ref[pl.ds(t0s[r], T), :] * QK_SCALE
                for e in (0, 1):
                    qaug[r, e] = jnp.where(heads[e], qb, ones3)
            chains = [(r, e) for r in (0, 1) for e in (0, 1)]

            def update(carries, first_block, seen):
                blocks = [kv_block(first_block + kk) for kk in range(1 + max(kk for s in seen for kk, _ in s))]
                out = []
                scores = [[_dot_nt(qaug[r, e], blocks[kk][0][e]) for kk, _ in seen[r]] for r, e in chains]
                for n, (r, e) in enumerate(chains):
                    m, acc = carries[n]
                    tiles = [jnp.where(tril, s, NEG) if masked else s for s, (_, masked) in zip(scores[n], seen[r])]
                    m_new = jnp.maximum(m, jnp.max(functools.reduce(jnp.maximum, tiles), axis=-1, keepdims=True))
                    acc = jnp.exp(m - m_new) * acc
                    for s, (kk, _) in zip(tiles, seen[r]):
                        acc = acc + _dot(jnp.exp(s - m_new).astype(BF16), blocks[kk][1][e])
                    out.append((m_new, acc))
                return tuple(out)

            def full(first_block, n_blocks, carries):
                return update(carries, first_block, [[(kk, False) for kk in range(n_blocks)]] * 2)

            init = (jnp.full((T, 1), NEG, F32), jnp.zeros((T, PAIR), F32))
            carries = lax.fori_loop(0, a // 2, lambda i, c: full(4 * i, 4, c), (init,) * 4)
            carries = lax.fori_loop(0, a % 2, lambda i, c: full(2 * a - 2, 2, c), carries)
            carries = update(carries, 2 * a, [[(0, True)], [(0, False), (1, True)]])
            first = _first_head()
            for r in (0, 1):
                (m0, acc0), (m1, acc1) = carries[2 * r], carries[2 * r + 1]
                l0, l1 = acc0[:, HEAD_DIM:HEAD_DIM + 1], acc1[:, 0:1]
                o_ref[pl.ds(t0s[r], T), :] = jnp.where(first, acc0 / l0, acc1 / l1).astype(BF16)
                lse_t = jnp.where(first, m0 + jnp.log(l0), m1 + jnp.log(l1)).T
                lse_ref[0:1, pl.ds(t0s[r], T)] = lse_t[0:1, :]
                lse_ref[1:2, pl.ds(t0s[r], T)] = lse_t[HEAD_DIM:HEAD_DIM + 1, :]
            return 0

        lax.fori_loop(0, nq // 2, q_pair, 0)

    col = pl.BlockSpec((S, PAIR), lambda hp: (0, hp))
    return _pcall(
        body, name=name, grid=(NP,), comm=comm,
        out_shape=(jax.ShapeDtypeStruct((S, DM), BF16), jax.ShapeDtypeStruct((NP, 2, S), F32)),
        in_specs=[col, col, col, col], out_specs=(col, pl.BlockSpec((None, 2, S), lambda hp: (hp, 0, 0))),
        args=(q, k, v, ca))


def _fox_bwd(q, k, v, o, do, ca, lse, name, comm=None):
    S, DM = q.shape
    NP = DM // PAIR
    T = min(FOX_BLOCK, S)
    nq = S // T

    def body(q_ref, k_ref, v_ref, o_ref, do_ref, ca_ref, lse_ref, dq_ref, dk_ref, dv_ref, dc_ref,
             dq_acc, dcol_acc, dk_acc, dv_acc, dd_s):
        keep = lax.broadcasted_iota(jnp.int32, (T, T), 0) <= lax.broadcasted_iota(jnp.int32, (T, T), 1)
        lane = lax.broadcasted_iota(jnp.int32, (1, LANES), 1)
        heads = (_lane_mask(0), _lane_mask(1))
        ones3 = _bias_ones()
        dq_acc[...] = jnp.zeros_like(dq_acc)
        dcol_acc[...] = jnp.zeros_like(dcol_acc)
        dc_ref[...] = jnp.zeros_like(dc_ref)

        def row_dots(qi, _):
            t0 = pl.multiple_of(qi * T, T)
            prod = do_ref[pl.ds(t0, T), :].astype(F32) * o_ref[pl.ds(t0, T), :].astype(F32)
            cols = [jnp.sum(jnp.where(heads[e], prod, 0.0), axis=-1, keepdims=True) for e in (0, 1)]
            tile = jnp.where(lane == 0, cols[0], jnp.where(lane == 1, cols[1], 0.0))
            dd_s[:, pl.ds(t0, T)] = tile.T[0:8, :]
            return 0

        lax.fori_loop(0, nq, row_dots, 0)

        def kv_pair(b, _):
            dk_acc[...] = jnp.zeros_like(dk_acc)
            dv_acc[...] = jnp.zeros_like(dv_acc)
            s0s = [pl.multiple_of((2 * b + jj) * T, T) for jj in (0, 1)]
            kaug, ks, vbs = {}, {}, []
            for jj in (0, 1):
                kb = k_ref[pl.ds(s0s[jj], T), :]
                cab = ca_ref[pl.ds(s0s[jj], T), :]
                vbs.append(v_ref[pl.ds(s0s[jj], T), :])
                for e in (0, 1):
                    kaug[jj, e] = jnp.where(heads[e], kb, cab)
                    ks[jj, e] = jnp.where(heads[e], kb, jnp.zeros_like(kb)) * QK_SCALE

            def q_step(qi, blocks):
                t0 = pl.multiple_of(qi * T, T)
                qs = q_ref[pl.ds(t0, T), :] * QK_SCALE
                dob = do_ref[pl.ds(t0, T), :]
                doh = [jnp.where(heads[e], dob, jnp.zeros_like(dob)) for e in (0, 1)]
                qh = [jnp.where(heads[e], qs, jnp.zeros_like(qs)) for e in (0, 1)]
                chains = [(e, jj, masked) for e in (0, 1) for jj, masked in blocks]
                scores = [_dot_nt(kaug[jj, e], jnp.where(heads[e], qs, ones3)) for e, jj, _ in chains]
                dps = [_dot_nt(vbs[jj], doh[e]) for e, jj, _ in chains]
                dq = jnp.zeros((T, PAIR), F32)
                drow = [jnp.zeros((1, T), F32), jnp.zeros((1, T), F32)]
                for n, (e, jj, masked) in enumerate(chains):
                    st = jnp.where(keep, scores[n], NEG) if masked else scores[n]
                    pt = jnp.exp(st - lse_ref[e:e + 1, pl.ds(t0, T)])
                    dv_acc[jj] += _dot(pt.astype(BF16), doh[e])
                    dst = pt * (dps[n] - dd_s[e:e + 1, pl.ds(t0, T)])
                    dsb = dst.astype(BF16)
                    dk_acc[jj] += _dot(dsb, qh[e])
                    dq = dq + _dot_tn(dsb, ks[jj, e])
                    dcol_acc[pl.ds(s0s[jj], T), :] -= jnp.where(lane == e, jnp.sum(dst, axis=-1, keepdims=True), 0.0)
                    drow[e] = drow[e] + jnp.sum(dst, axis=0, keepdims=True)
                for e in (0, 1):
                    dc_ref[e:e + 1, pl.ds(t0, T)] += drow[e]
                dq_acc[pl.ds(t0, T), :] += dq

            q_step(2 * b, [(0, True)])
            q_step(2 * b + 1, [(0, False), (1, True)])

            def rest(i, _):
                q_step(2 * b + 2 + 2 * i, [(0, False), (1, False)])
                q_step(2 * b + 3 + 2 * i, [(0, False), (1, False)])
                return 0

            lax.fori_loop(0, (nq - 2) // 2 - b, rest, 0)
            for jj in (0, 1):
                dk_ref[pl.ds(s0s[jj], T), :] = dk_acc[jj].astype(BF16)
                dv_ref[pl.ds(s0s[jj], T), :] = dv_acc[jj].astype(BF16)
            return 0

        lax.fori_loop(0, nq // 2, kv_pair, 0)
        dq_ref[...] = dq_acc[...].astype(BF16)

        def add_cols(kj, _):
            s0 = pl.multiple_of(kj * T, T)
            dc_ref[:, pl.ds(s0, T)] += dcol_acc[pl.ds(s0, T), :].T[0:2, :]
            return 0

        lax.fori_loop(0, nq, add_cols, 0)

    col = pl.BlockSpec((S, PAIR), lambda hp: (0, hp))
    row2 = pl.BlockSpec((None, 2, S), lambda hp: (hp, 0, 0))
    return _pcall(
        body, name=name, grid=(NP,), comm=comm,
        out_shape=(jax.ShapeDtypeStruct(do.shape, BF16), jax.ShapeDtypeStruct((S, DM), BF16),
                   jax.ShapeDtypeStruct((S, DM), BF16), jax.ShapeDtypeStruct((NP, 2, S), F32)),
        in_specs=[col, col, col, col, col, col, row2], out_specs=(col, col, col, row2),
        scratch_shapes=[pltpu.VMEM((S, PAIR), F32), pltpu.VMEM((S, LANES), F32), pltpu.VMEM((2, T, PAIR), F32),
                        pltpu.VMEM((2, T, PAIR), F32), pltpu.VMEM((8, S), F32)],
        args=(q, k, v, o, do, ca, lse), aliases={4: 0})


def _mixer_b_fwd(x, o, qm, mkv, wo, g, b, name):
    S, D = x.shape
    DQ = qm.shape[1]
    DM = D - DQ
    tm = min(512, S)

    def body(x_ref, o_ref, qm_ref, mkv_ref, wo_ref, g_ref, b_ref, xo_ref, xbt_ref, r_ref, catt_ref, cat_ref):
        cat_ref[:, :DM] = o_ref[...]
        mo = _mem_attn_fwd(qm_ref[...], mkv_ref, DQ)
        for mp in range(DQ // PAIR):
            cat_ref[:, DM + mp * PAIR:DM + (mp + 1) * PAIR] = mo[mp].astype(BF16)
        _tail_fwd_store(x_ref[...], cat_ref, wo_ref, g_ref, b_ref, xo_ref, xbt_ref, r_ref, catt_ref)

    return pl.pallas_call(
        body, name=name, grid=(S // tm,),
        out_shape=(jax.ShapeDtypeStruct((S, D), F32), jax.ShapeDtypeStruct((D, S), BF16),
                   jax.ShapeDtypeStruct((S, D), F32), jax.ShapeDtypeStruct((D, S), BF16)),
        in_specs=[_rows(tm, D), _rows(tm, DM), _rows(tm, DQ), _full(mkv.shape), _full(wo.shape), _full((1, D)),
                  _full((1, D))],
        out_specs=(_rows(tm, D), _cols(D, tm), _rows(tm, D), _cols(D, tm)),
        scratch_shapes=[pltpu.VMEM((tm, D), BF16)],
        compiler_params=_params(),
    )(x, o, qm, mkv, wo, g, b)


def _mixer_b_bwd(dxo, r, qm, mkv, wo, g, name, comm=None):
    S, D = r.shape
    DQ = qm.shape[1]
    DM = D - DQ
    M = mkv.shape[0]
    tm = min(512, S)

    def body(dxo_ref, r_ref, qm_ref, mkv_ref, wo_ref, g_ref, dr_ref, drb_ref, dcat_ref, dgb_ref, dmkv_ref):
        @pl.when(pl.program_id(0) == 0)
        def _():
            dgb_ref[...] = jnp.zeros_like(dgb_ref)
            dmkv_ref[...] = jnp.zeros_like(dmkv_ref)

        dr, dcat = _tail_bwd_head(dxo_ref, r_ref, g_ref, wo_ref, dgb_ref, drb_ref)
        dr_ref[...] = dr
        dcat_ref[:, :DM] = dcat[:, :DM].astype(BF16)
        dmo = [dcat[:, DM + mp * PAIR:DM + (mp + 1) * PAIR] for mp in range(DQ // PAIR)]
        dqm = _mem_attn_bwd(qm_ref[...], mkv_ref, dmo, dmkv_ref, DQ)
        for mp in range(DQ // PAIR):
            dcat_ref[:, DM + mp * PAIR:DM + (mp + 1) * PAIR] = dqm[mp].astype(BF16)

    return _pcall(
        body, name=name, grid=(S // tm,), comm=comm,
        out_shape=(jax.ShapeDtypeStruct((S, D), F32), jax.ShapeDtypeStruct((S, D), BF16),
                   jax.ShapeDtypeStruct((S, D), BF16), jax.ShapeDtypeStruct((2, D), F32),
                   jax.ShapeDtypeStruct((M, 2 * DQ), F32)),
        in_specs=[_rows(tm, D), _rows(tm, D), _rows(tm, DQ), _full(mkv.shape), _full(wo.shape), _full((1, D))],
        out_specs=(_rows(tm, D), _rows(tm, D), _rows(tm, D), _full((2, D)), _full((M, 2 * DQ))),
        args=(dxo, r, qm, mkv, wo, g))


class _NoExchange:
    def plan(self, stage, w, grads):
        return None

    def done(self, stage, results, w):
        pass


def _local_step(x, mem, target, w, hooks):
    S, D = x.shape
    DQ = w["mkv0"].shape[1] // 2
    DM = D - DQ
    NG = DM // HEAD_DIM
    NP = DM // PAIR
    ln_g, ln_b = w["ln_g"], w["ln_b"]
    row = lambda a, i: a[i:i + 1]
    memb = mem.astype(BF16)
    bst =jnp.pad(w["b_s"].T, ((0, 0), (0, LANES - NG)))
    grads = {}

    def staged(stage, fn, *args):
        out, got = fn(*args, stage, comm=hooks.plan(stage, w, grads))
        hooks.done(stage, got, w)
        return out

    mkv0 = _mm_small(memb, w["mkv0"], "mem_kv_0")
    x1, x1t, r1, cat0t, x0t = staged("mixer_a_fwd", _mixer_a_fwd, x, w["wa"], w["sg"], w["sb"], w["ws"], bst, mkv0,
                                w["wo0"], row(ln_g, 0), row(ln_b, 0))
    x2, x2b, x2t, r2, h0 = staged("mlp_fwd_0", _mlp_fwd, x1, w["wup0"], w["wdown0"], row(ln_g, 1), row(ln_b, 1))
    q, qm, k, v, fl = _proj_b_fwd(x2b, w["wq"], w["wkv"], DM, "proj_b_fwd")
    flt = fl[:, :16].T
    bfc = jnp.pad(w["bf"], (0, 16 - NG)).reshape(16, 1)
    pieces = _forget_cumsum(flt, bfc, "forget_cumsum")
    pieces = jnp.pad(pieces[:, :NG].transpose(2, 1, 0), ((0, 0), (0, 0), (0, HEAD_DIM - N_BIAS)))
    ca = pieces.reshape(S, NP, 2, HEAD_DIM)[:, :, ::-1].reshape(S, DM).astype(BF16)
    o, lse = staged("fox_fwd", _fox_fwd, q, k, v, ca)
    mkv1 = _mm_small(memb, w["mkv1"], "mem_kv_1")
    x3, x3t, r3, cat1t = _mixer_b_fwd(x2, o, qm, mkv1, w["wo1"], row(ln_g, 2), row(ln_b, 2), "mixer_b_fwd")
    dx4, r4, h1, loss_part = _mlp_fwd_loss(x3, w["wup1"], w["wdown1"], row(ln_g, 3), row(ln_b, 3), target,
                                           "mlp_fwd_1")

    nc = w["wup0"].shape[0]
    (dx3, dr4t, dpre1, dgb11), _ = _mlp_bwd(dx4, r4, h1, w["wup1"], w["wdown1"], row(ln_g, 3), "mlp_bwd_1")
    grads["wdown1"] = _mm_nn(dr4t, h1, "dw_down_1", out_split=nc)
    grads["wup1"] = _mm_nn(x3t, dpre1, "dw_up_1", out_split=nc)
    dr3, dr3b, dcat1, dgb10, dmkv1 = staged("mixer_b_bwd", _mixer_b_bwd, dx3, r3, qm, mkv1, w["wo1"], row(ln_g, 2))
    dqz, dk, dv, dct3 = staged("fox_bwd", _fox_bwd, q, k, v, o, dcat1, ca, lse)
    dct = jnp.pad(dct3.reshape(NG, S), ((0, 16 - NG), (0, 0)))
    dflt, dbf = _forget_cumsum_bwd(dct, flt, bfc, "forget_cumsum_bwd")
    dfl = jnp.pad(dflt.T, ((0, 0), (0, LANES - 16))).astype(BF16)
    dkvf = jnp.concatenate([dk, dv, dfl], axis=1)
    dx2 = _proj_b_bwd(dr3, dqz, dkvf, w["wq"], w["wkv"], "proj_b_bwd")
    grads["wo1"] = _mm_nn(cat1t, dr3b, "dw_o_1")
    grads["wq"] = _mm_nn(x2t, dqz, "dw_q")
    grads["wkv"] = _mm_nn(x2t, dkvf, "dw_kv")
    grads["mkv1"] = _mm_tn(memb, dmkv1.astype(BF16), "dw_mkv_1")
    dx1, dr2t, dpre0, dgb01 = staged("mlp_bwd_0", _mlp_bwd, dx2, r2, h0, w["wup0"], w["wdown0"], row(ln_g, 1))
    grads["wdown0"] = _mm_nn(dr2t, h0, "dw_down_0", out_split=nc)
    grads["wup0"] = _mm_nn(x1t, dpre0, "dw_up_0", out_split=nc)
    dx0, dz, dr1b, dgb00, dsgb, dws, dbst, dmkv0 = staged(
        "mixer_a_bwd", _mixer_a_bwd, x, dx1, r1, w["wa"], w["sg"], w["sb"], w["ws"], bst, mkv0, w["wo0"], row(ln_g, 0))
    grads["wo0"] = _mm_nn(cat0t, dr1b, "dw_o_0")
    grads["wa"] = _mm_nn(x0t, dz, "dw_a")
    grads["mkv0"] = _mm_tn(memb, dmkv0.astype(BF16), "dw_mkv_0")
    grads.update({
        "ws": dws, "b_s": dbst[:, :NG].T, "sg": dsgb[0:1], "sb": dsgb[1:2], "bf": dbf[:NG, 0], "loss": loss_part[0:1],
        "ln_g": jnp.concatenate([dgb00[0:1], dgb01[0:1], dgb10[0:1], dgb11[0:1]], axis=0),
        "ln_b": jnp.concatenate([dgb00[1:2], dgb01[1:2], dgb10[1:2], dgb11[1:2]], axis=0),
    })
    return loss_part, dx0, grads


def _place():
    xi, yi, ci = lax.axis_index("x"), lax.axis_index("y"), lax.axis_index("c")
    peers = [(xi, 1 - yi), (1 - xi, yi), (1 - xi, 1 - yi)]
    return xi, yi, ci, peers


def _comm_call(body, name, ins, out_shapes, n_remote, n_local):
    return pl.pallas_call(
        body, name=name, out_shape=out_shapes, in_specs=[ANY] * len(ins), out_specs=[ANY] * len(out_shapes),
        scratch_shapes=[pltpu.SemaphoreType.DMA((n_remote,)), pltpu.SemaphoreType.DMA((n_remote,)),
                        pltpu.SemaphoreType.DMA((max(n_local, 1),))],
        compiler_params=pltpu.CompilerParams(has_side_effects=True),
    )(*ins)


def _pair_swap(xs):
    n = len(xs)
    out_shapes = [jax.ShapeDtypeStruct((a.shape[0],) + a.shape[2:], a.dtype) for a in xs]

    def copies(x_refs, o_refs, sems):
        send_sems, recv_sems = sems
        xi, yi, ci, _ = _place()
        return [pltpu.make_async_remote_copy(src_ref=x_refs[i].at[:, 1 - ci], dst_ref=o_refs[i],
                                             send_sem=send_sems.at[i], recv_sem=recv_sems.at[i],
                                             device_id=(xi, yi, 1 - ci), device_id_type=MESH) for i in range(n)]

    def start(x_refs, o_refs, sems):
        for cp in copies(x_refs, o_refs, sems):
            cp.start()

    def finish(x_refs, o_refs, sems):
        cps = copies(x_refs, o_refs, sems)
        for cp in cps:
            cp.wait_recv()
        for cp in cps:
            cp.wait_send()

    return _Comm(xs, out_shapes, {}, [n, n], start, finish)


def _chip_exchange(xs):
    n = len(xs)
    out_shapes = [jax.ShapeDtypeStruct((3,) + a.shape[1:], a.dtype) for a in xs]

    def copies(x_refs, o_refs, sems):
        send_sems, recv_sems = sems
        xi, yi, ci, peers = _place()
        return [pltpu.make_async_remote_copy(src_ref=x_refs[i].at[2 * px + py], dst_ref=o_refs[i].at[d],
                                             send_sem=send_sems.at[3 * i + d], recv_sem=recv_sems.at[3 * i + d],
                                             device_id=(px, py, ci), device_id_type=MESH)
                for i in range(n) for d, (px, py) in enumerate(peers)]

    def start(x_refs, o_refs, sems):
        for cp in copies(x_refs, o_refs, sems):
            cp.start()

    def finish(x_refs, o_refs, sems):
        cps = copies(x_refs, o_refs, sems)
        for cp in cps:
            cp.wait_recv()
        for cp in cps:
            cp.wait_send()

    return _Comm(xs, out_shapes, {}, [3 * n, 3 * n], start, finish)


def _inplace_call(body, name, bufs, sem_counts):
    n = len(bufs)
    return pl.pallas_call(
        body, name=name, out_shape=[jax.ShapeDtypeStruct(a.shape, a.dtype) for a in bufs],
        in_specs=[ANY] * n, out_specs=[ANY] * n, input_output_aliases={i: i for i in range(n)},
        scratch_shapes=[pltpu.SemaphoreType.DMA((k,)) for k in sem_counts],
        compiler_params=pltpu.CompilerParams(has_side_effects=True),
    )(*bufs)


def _gather_weights(bufs, whole=()):
    n, nw = len(bufs), len(whole)

    def copy(ref, s, r, k, dev):
        return pltpu.make_async_remote_copy(src_ref=ref, dst_ref=ref, send_sem=s.at[k], recv_sem=r.at[k],
                                            device_id=dev, device_id_type=MESH)

    def plan(w, sems):
        ici_s, ici_r, d2d_s, d2d_r = sems
        xi, yi, ci, peers = _place()
        xn, yn, sib = (1 - xi, yi, ci), (xi, 1 - yi, ci), (xi, yi, 1 - ci)
        j, jx, jy, jd = 2 * xi + yi, 2 * (1 - xi) + yi, 2 * xi + 1 - yi, 2 * (1 - xi) + 1 - yi
        c = {}
        for i in range(n):
            hh = bufs[i].shape[2] // 2
            lo, hi = pl.ds(0, hh), pl.ds(hh, hh)
            at = lambda slot, half, rows=None, i=i: w[i].at[slot, half] if rows is None else w[i].at[slot, half, rows]
            c["to_x", i] = (copy(at(j, ci), ici_s, ici_r, 4 * i, xn), copy(at(jx, ci), ici_s, ici_r, 4 * i, xn))
            c["to_y", i] = (copy(at(j, ci), ici_s, ici_r, 4 * i + 1, yn), copy(at(jy, ci), ici_s, ici_r, 4 * i + 1, yn))
            c["x_on", i] = (copy(at(jx, ci, lo), ici_s, ici_r, 4 * i + 2, yn), copy(at(jd, ci, lo), ici_s, ici_r, 4 * i + 2, yn))
            c["y_on", i] = (copy(at(jy, ci, hi), ici_s, ici_r, 4 * i + 3, xn), copy(at(jd, ci, hi), ici_s, ici_r, 4 * i + 3, xn))
            for k, slot in enumerate((jx, jy, jd)):
                c["sib", i, k] = (copy(at(slot, ci), d2d_s, d2d_r, 3 * i + k, sib),
                                  copy(at(slot, 1 - ci), d2d_s, d2d_r, 3 * i + k, sib))
        for i in range(nw):
            for d, (px, py) in enumerate(peers):
                k = 4 * n + 3 * i + d
                c["whole", i, d] = (copy(w[n + i].at[j], ici_s, ici_r, k, (px, py, ci)),
                                    copy(w[n + i].at[2 * px + py], ici_s, ici_r, k, (px, py, ci)))
        return c

    def start(_, w, sems):
        c = plan(w, sems)
        for i in range(n):
            c["to_x", i][0].start()
            c["to_y", i][0].start()
        for i in range(nw):
            for d in range(3):
                c["whole", i, d][0].start()

    def middle(_, w, sems):
        c = plan(w, sems)
        for i in range(n):
            c["to_x", i][1].wait_recv()
            c["x_on", i][0].start()
            c["sib", i, 0][0].start()
            c["to_y", i][1].wait_recv()
            c["y_on", i][0].start()
            c["sib", i, 1][0].start()

    def finish(_, w, sems):
        c = plan(w, sems)
        for i in range(n):
            c["x_on", i][1].wait_recv()
            c["y_on", i][1].wait_recv()
            c["sib", i, 2][0].start()
        for i in range(n):
            for k in range(3):
                c["sib", i, k][1].wait_recv()
        for i in range(nw):
            for d in range(3):
                c["whole", i, d][1].wait_recv()
        for key, (sent, _) in c.items():
            sent.wait_send()

    arrays = list(bufs) + list(whole)
    k = 4 * n + 3 * nw
    return _Comm(arrays, [jax.ShapeDtypeStruct(a.shape, a.dtype) for a in arrays], {i: i for i in range(n + nw)},
                 [k, k, max(3 * n, 1), max(3 * n, 1)], start, finish, middle)


def _pair_gather(bufs, spread, name):
    n = len(bufs)
    where = [(i, l) for i, a in enumerate(bufs) for l in range(a.shape[0])]
    nw = len(where)
    arrays = list(bufs) + ([] if spread is None else [spread])
    na = len(arrays)

    def body(*refs):
        g = refs[na:na + n]
        sp = refs[na + n] if spread is not None else None
        send_sems, recv_sems = refs[2 * na:]
        xi, yi, ci, peers = _place()
        chips = [(xi, yi)] + peers
        others = [(d, f) for d in range(4) for f in (0, 1) if (d, f) != (0, 0)] if spread is not None else []
        sends = []
        for k, (i, l) in enumerate(where):
            mine = g[i].at[l, :, ci]
            cp = pltpu.make_async_remote_copy(src_ref=mine, dst_ref=mine, send_sem=send_sems.at[k],
                                              recv_sem=recv_sems.at[k], device_id=(xi, yi, 1 - ci), device_id_type=MESH)
            cp.start()
            sends.append(cp)
        for k, (d, f) in enumerate(others):
            mine = sp.at[2 * xi + yi, ci]
            cp = pltpu.make_async_remote_copy(src_ref=mine, dst_ref=mine, send_sem=send_sems.at[nw + k],
                                              recv_sem=recv_sems.at[nw + k],
                                              device_id=(chips[d][0], chips[d][1], ci if f == 0 else 1 - ci),
                                              device_id_type=MESH)
            cp.start()
            sends.append(cp)
        for k, (i, l) in enumerate(where):
            other = g[i].at[l, :, 1 - ci]
            pltpu.make_async_remote_copy(src_ref=other, dst_ref=other, send_sem=send_sems.at[k],
                                         recv_sem=recv_sems.at[k], device_id=(xi, yi, 1 - ci),
                                         device_id_type=MESH).wait_recv()
        for k, (d, f) in enumerate(others):
            px, py, pc = chips[d][0], chips[d][1], (ci if f == 0 else 1 - ci)
            theirs = sp.at[2 * px + py, pc]
            pltpu.make_async_remote_copy(src_ref=theirs, dst_ref=theirs, send_sem=send_sems.at[nw + k],
                                         recv_sem=recv_sems.at[nw + k], device_id=(px, py, pc),
                                         device_id_type=MESH).wait_recv()
        for cp in sends:
            cp.wait_send()

    n_sems = nw + (7 if spread is not None else 0)
    out = _inplace_call(body, name, arrays, [n_sems, n_sems])
    return out[:n], (out[n] if spread is not None else None)


def _row_block(rows, cols):
    want = max(8, (2 ** 18 // max(cols, 1)) // 8 * 8)
    if rows <= want:
        return rows
    best = 8
    for t in range(8, want + 1, 8):
        if rows % t == 0:
            best = t
    return best


def _pair_add(g4, recv, sel, out_dtype, name):
    A, _, H, C = g4.shape
    bh = _row_block(H, C)

    def body(sel_ref, g_ref, r_ref, o_ref):
        o_ref[...] = (g_ref[...].astype(F32) + r_ref[...].astype(F32)).astype(out_dtype)

    return pl.pallas_call(
        body, name=name, out_shape=jax.ShapeDtypeStruct((A, H, C), out_dtype),
        grid_spec=pltpu.PrefetchScalarGridSpec(
            num_scalar_prefetch=1, grid=(A, H // bh),
            in_specs=[pl.BlockSpec((None, None, bh, C), lambda a, i, s: (a, s[1], i, 0)),
                      pl.BlockSpec((None, bh, C), lambda a, i, s: (a, i, 0))],
            out_specs=pl.BlockSpec((None, bh, C), lambda a, i, s: (a, i, 0))),
        compiler_params=pltpu.CompilerParams(dimension_semantics=("arbitrary", "arbitrary"),
                                             vmem_limit_bytes=VMEM_LIMIT),
    )(sel, g4, recv)


def _chip_reduce(g4, recv1, recv2, sel, buf, layer, n_layers, name, by_chip=False):
    _, _, H, C = g4.shape
    bh = _row_block(H, C)
    if by_chip:
        out_shape = jax.ShapeDtypeStruct((N_CHIPS, 2, H, C), F32)
        out_spec = pl.BlockSpec((None, None, bh, C), lambda i, s: (s[0], s[1], i, 0))
    else:
        out_shape = jax.ShapeDtypeStruct((n_layers, 1, 2, H, C), F32)
        out_spec = pl.BlockSpec((None, None, None, bh, C), lambda i, s: (layer, 0, s[1], i, 0))

    def body(sel_ref, g_ref, r1_ref, r2_ref, *rest):
        acc = g_ref[...].astype(F32) + r1_ref[...].astype(F32)
        for d in range(3):
            acc = acc + r2_ref[d].astype(F32)
        rest[-1][...] = acc

    in_specs = [pl.BlockSpec((None, None, bh, C), lambda i, s: (s[0], s[1], i, 0)),
                pl.BlockSpec((None, bh, C), lambda i, s: (s[0], i, 0)),
                pl.BlockSpec((3, bh, C), lambda i, s: (0, i, 0))]
    args = [sel, g4, recv1, recv2]
    aliases = {}
    if buf is not None:
        in_specs.append(ANY)
        args.append(buf)
        aliases = {4: 0}
    return pl.pallas_call(
        body, name=name, out_shape=out_shape,
        grid_spec=pltpu.PrefetchScalarGridSpec(num_scalar_prefetch=1, grid=(H // bh,), in_specs=in_specs,
                                               out_specs=out_spec),
        input_output_aliases=aliases,
        compiler_params=pltpu.CompilerParams(dimension_semantics=("arbitrary",), vmem_limit_bytes=VMEM_LIMIT),
    )(*args)


def _cast_place(shards, layer, sel, name):
    _, _, H, C = shards.shape
    bh = _row_block(H, C)

    def body(sel_ref, x_ref, o_ref):
        o_ref[...] = x_ref[...].astype(BF16)

    return pl.pallas_call(
        body, name=name, out_shape=jax.ShapeDtypeStruct((N_CHIPS, 2, H, C), BF16),
        grid_spec=pltpu.PrefetchScalarGridSpec(
            num_scalar_prefetch=1, grid=(2, H // bh),
            in_specs=[pl.BlockSpec((None, None, bh, C), lambda h, i, s: (layer, h, i, 0))],
            out_specs=pl.BlockSpec((None, None, bh, C), lambda h, i, s: (s[0], h, i, 0))),
        compiler_params=pltpu.CompilerParams(dimension_semantics=("arbitrary", "arbitrary"),
                                             vmem_limit_bytes=VMEM_LIMIT),
    )(sel, shards)


def _adamw_many(quads, name, comm=None):
    R, C = quads[0][0].shape
    br = _row_block(R, C)
    n = len(quads)

    def body(*refs):
        for k in range(n):
            g_ref, w_ref, m_ref, v_ref = refs[4 * k:4 * k + 4]
            d_ref, mo_ref, vo_ref = refs[4 * n + 3 * k:4 * n + 3 * k + 3]
            gv = g_ref[...]
            mn = ADAM_B1 * m_ref[...] + (1.0 - ADAM_B1) * gv
            vn = ADAM_B2 * v_ref[...] + (1.0 - ADAM_B2) * (gv * gv)
            m_hat = mn / (1.0 - ADAM_B1 ** ADAM_STEP)
            v_hat = vn / (1.0 - ADAM_B2 ** ADAM_STEP)
            d_ref[...] = -ADAM_LR * (m_hat / (jnp.sqrt(v_hat) + ADAM_EPS) + ADAM_WD * w_ref[...])
            mo_ref[...] = mn
            vo_ref[...] = vn

    spec = pl.BlockSpec((br, C), lambda i: (i, 0))
    res, got = _pcall(body, name=name, grid=(R // br,), comm=comm, out_shape=(jax.ShapeDtypeStruct((R, C), F32),) * (3 * n),
                      in_specs=[spec] * (4 * n), out_specs=(spec,) * (3 * n), args=[a for q in quads for a in q])
    return [res[3 * k:3 * k + 3] for k in range(n)], got


def _adamw(g, w, m, v, name):
    return _adamw_many([(g, w, m, v)], name)[0][0]


def _part_rows(shape):
    return -(-math.prod(shape) // (8 * LANES)) * 8


def _pack_rows(arrs, total_rows):
    parts = []
    for a in arrs:
        flat = a.reshape(-1)
        rows = _part_rows(a.shape)
        parts.append(jnp.pad(flat, (0, rows * LANES - flat.shape[0])).reshape(rows, LANES))
    packed = jnp.concatenate(parts, axis=0)
    return jnp.pad(packed, ((0, total_rows - packed.shape[0]), (0, 0)))


def _unpack_rows(packed, shapes):
    out, r = [], 0
    for shp in shapes:
        size, rows = math.prod(shp), _part_rows(shp)
        out.append(packed[r:r + rows].reshape(-1)[:size].reshape(shp))
        r += rows
    return out


def _rows_of(shapes):
    return sum(_part_rows(s) for s in shapes)


_GATHER_STAGES = {"start": ["wa", "mkv0", "wo0"], "mixer_a_fwd": ["wup0", "wdown0"],
                  "mlp_fwd_0": ["wq", "wkv", "mkv1", "wo1"], "fox_fwd": ["wup1", "wdown1"]}
_SWAP_STAGES = {"mixer_b_bwd": ["wup1", "wdown1"]}
_REDUCE_STAGES = {"fox_bwd": ["wup1", "wdown1"], "mlp_bwd_0": ["wo1", "wq", "wkv", "mkv1"],
                  "mixer_a_bwd": ["wup0", "wdown0"], "end": ["wa", "wo0", "mkv0", "small"]}
_SMALL = ["ws", "b_s", "sg", "sb", "bf", "ln_g", "ln_b", "loss"]


class _Staged:
    def __init__(self, placed, sel, dm, ng, shard_cols, n_small):
        self.placed, self.sel, self.dm, self.ng, self.shard_cols, self.n_small = placed, sel, dm, ng, shard_cols, n_small
        self.g4, self.recv1, self.recv2 = {}, {}, {}

    def _weight(self, name, a):
        a = a.reshape(N_CHIPS, 2 * a.shape[2], a.shape[3])
        if name in ("wa", "wkv"):
            a = a.transpose(1, 0, 2).reshape(a.shape[1], -1)
            if name == "wkv":
                a = jnp.concatenate([a[:, :2 * self.dm],
                                     jnp.pad(a[:, 2 * self.dm:], ((0, 0), (0, LANES - self.ng)))], axis=1)
            return a
        if name.startswith("wup") or name.startswith("wdown"):
            return a
        return a.reshape(-1, a.shape[2])

    def _partial(self, name, grads):
        if name == "small":
            a = _pack_rows([grads[n] for n in _SMALL], self.n_small).reshape(N_CHIPS, -1, LANES)
        elif name in self.shard_cols:
            n = self.shard_cols[name]
            g = grads[name][:, :N_CHIPS * n]
            a = g.reshape(g.shape[0], N_CHIPS, n).transpose(1, 0, 2)
        elif name.startswith("wup") or name.startswith("wdown"):
            a = grads[name]
        else:
            g = grads[name]
            a = g.reshape(N_CHIPS, g.shape[0] // N_CHIPS, g.shape[1])
        return a.reshape(N_CHIPS, 2, a.shape[1] // 2, a.shape[2])

    def plan(self, stage, w, grads):
        if stage in _GATHER_STAGES:
            return _gather_weights([self.placed[n] for n in _GATHER_STAGES[stage]])
        if stage in _SWAP_STAGES:
            for n in _SWAP_STAGES[stage]:
                self.g4[n] = self._partial(n, grads)
            return _pair_swap([self.g4[n] for n in _SWAP_STAGES[stage]])
        if stage in _REDUCE_STAGES:
            names = _REDUCE_STAGES[stage]
            late = [n for n in names if n not in self.recv1]
            for n in late:
                self.g4[n] = self._partial(n, grads)
            if late:
                got = _run_comm(_pair_swap([self.g4[n] for n in late]), "reduce_pair_swap_" + stage)
                self.recv1.update(zip(late, got))
            return _chip_exchange([_pair_add(self.g4[n], self.recv1[n], self.sel, F32 if n == "small" else BF16,
                                             "reduce_pair_add_" + n) for n in names])
        return None

    def done(self, stage, results, w):
        if stage in _GATHER_STAGES:
            for n, a in zip(_GATHER_STAGES[stage], results):
                w[n] = self._weight(n, a)
        elif stage in _SWAP_STAGES:
            self.recv1.update(zip(_SWAP_STAGES[stage], results))
        elif stage in _REDUCE_STAGES:
            self.recv2.update(zip(_REDUCE_STAGES[stage], results))


def kernel(x, mem, a_w_in, a_sgu_ln_g, a_sgu_ln_b, a_w_s, a_b_s, kv_w, kv_b_f, b_w_q, mem_w_kv, w_o, ln_g, ln_b, w_up, w_down, loss_target, m_a_w_in, m_a_sgu_ln_g, m_a_sgu_ln_b, m_a_w_s, m_a_b_s, m_kv_w, m_kv_b_f, m_b_w_q, m_mem_w_kv, m_w_o, m_ln_g, m_ln_b, m_w_up, m_w_down, v_a_w_in, v_a_sgu_ln_g, v_a_sgu_ln_b, v_a_w_s, v_a_b_s, v_kv_w, v_kv_b_f, v_b_w_q, v_mem_w_kv, v_w_o, v_ln_g, v_ln_b, v_w_up, v_w_down):
    xi, yi, ci = lax.axis_index("x"), lax.axis_index("y"), lax.axis_index("c")
    chip = 2 * xi + yi
    sel = jnp.stack([chip, ci]).astype(jnp.int32)
    S, D = x.shape[1], x.shape[2]
    DQ = mem_w_kv.shape[2] // 2
    DM = D - DQ
    NG = DM // HEAD_DIM

    shards = {"wa": (a_w_in, 0), "wkv": (kv_w[None], 0), "wq": (b_w_q, 0), "mkv0": (mem_w_kv, 0), "mkv1": (mem_w_kv, 1),
              "wo0": (w_o, 0), "wo1": (w_o, 1), "wup0": (w_up, 0), "wup1": (w_up, 1), "wdown0": (w_down, 0),
              "wdown1": (w_down, 1)}
    placed = {n: _cast_place(s.reshape(s.shape[0], 2, s.shape[1] // 2, s.shape[2]), layer, sel, "cast_place_" + n)
              for n, (s, layer) in shards.items()}
    ln_pack = jnp.concatenate([ln_g.reshape(4, -1), ln_b.reshape(4, -1)], axis=0)
    ln_buf = lax.dynamic_update_slice(jnp.zeros((N_CHIPS,) + ln_pack.shape, F32), ln_pack[None], (chip, 0, 0))
    ln_shape = (4, D)
    small_shapes = [a_w_s.shape, a_b_s.shape, a_sgu_ln_g.shape, a_sgu_ln_b.shape, kv_b_f.shape, ln_shape, ln_shape,
                    (1, LANES)]
    n_small = -(-_rows_of(small_shapes) // 64) * 64
    hooks = _Staged(placed, sel, DM, NG, {"wa": a_w_in.shape[2], "wkv": kv_w.shape[1]}, n_small)
    weights = {"sg": a_sgu_ln_g, "sb": a_sgu_ln_b, "ws": a_w_s[0], "b_s": a_b_s[0], "bf": kv_b_f}
    got = _run_comm(_gather_weights([placed[n] for n in _GATHER_STAGES["start"]], whole=[ln_buf]),
                    "gather_weights_first")
    hooks.done("start", got[:-1], weights)
    ln_full = got[-1].transpose(1, 0, 2).reshape(8, D)
    weights["ln_g"], weights["ln_b"] = ln_full[:4], ln_full[4:]

    _, grad_x, gr = _local_step(x[0], mem[0], loss_target[0], weights, hooks)

    def chip_sums(groups):
        bufs = []
        for grp in groups:
            buf = None
            for layer, n in enumerate(grp):
                buf = _chip_reduce(hooks.g4[n], hooks.recv1[n], hooks.recv2[n], sel, buf, layer, len(grp),
                                   "reduce_chip_sum_" + n)
            bufs.append(buf)
        return bufs

    whole = lambda a: a.reshape(a.shape[0], 2 * a.shape[3], a.shape[4])
    flat = lambda a: a.reshape(-1, a.shape[-1])

    def update(name, g, w, m, v):
        d, mn, vn = _adamw(flat(g), flat(w), flat(m), flat(v), "adamw_" + name)
        return d.reshape(w.shape), mn.reshape(w.shape), vn.reshape(w.shape)

    hooks.done("end", _run_comm(hooks.plan("end", weights, gr), "reduce_chip_exchange_end"), weights)
    small_mine = _chip_reduce(hooks.g4["small"], hooks.recv1["small"], hooks.recv2["small"], sel, None, 0, 1,
                              "reduce_chip_sum_small", by_chip=True)
    red, small_all = _pair_gather(chip_sums([["wa"], ["wkv"], ["wq"], ["mkv0", "mkv1"], ["wo0", "wo1"],
                                             ["wup0", "wup1"], ["wdown0", "wdown1"]]), small_mine, "reduce_pair_gather")
    g_a_w_in, g_kv_w, g_b_w_q, g_mem_w_kv, g_w_o, g_w_up, g_w_down = [whole(a) for a in red]
    g_kv_w = g_kv_w[0]
    g_w_down = g_w_down.transpose(0, 2, 1)
    small_all = small_all.reshape(n_small, LANES)
    g_ws, g_bs, g_sg, g_sb, g_bf, g_lng, g_lnb, loss_row = _unpack_rows(small_all, small_shapes)
    loss = loss_row[0, 0]
    dsh = D // N_CHIPS
    g_ln_g = lax.dynamic_slice_in_dim(g_lng, chip * dsh, dsh, axis=1).reshape(ln_g.shape)
    g_ln_b = lax.dynamic_slice_in_dim(g_lnb, chip * dsh, dsh, axis=1).reshape(ln_b.shape)

    upd = {
        "a_w_in": update("a_w_in", g_a_w_in, a_w_in, m_a_w_in, v_a_w_in),
        "kv_w": update("kv_w", g_kv_w, kv_w, m_kv_w, v_kv_w),
        "b_w_q": update("b_w_q", g_b_w_q, b_w_q, m_b_w_q, v_b_w_q),
        "mem_w_kv": update("mem_w_kv", g_mem_w_kv, mem_w_kv, m_mem_w_kv, v_mem_w_kv),
        "w_o": update("w_o", g_w_o, w_o, m_w_o, v_w_o),
        "w_up": update("w_up", g_w_up, w_up, m_w_up, v_w_up),
        "w_down": update("w_down", g_w_down, w_down, m_w_down, v_w_down),
    }
    tiny_g = [g_sg, g_sb, g_ws, g_bs, g_bf, g_ln_g, g_ln_b]
    tiny_w = [a_sgu_ln_g, a_sgu_ln_b, a_w_s, a_b_s, kv_b_f, ln_g, ln_b]
    tiny_m = [m_a_sgu_ln_g, m_a_sgu_ln_b, m_a_w_s, m_a_b_s, m_kv_b_f, m_ln_g, m_ln_b]
    tiny_v = [v_a_sgu_ln_g, v_a_sgu_ln_b, v_a_w_s, v_a_b_s, v_kv_b_f, v_ln_g, v_ln_b]
    tiny_shapes = [a.shape for a in tiny_w]
    n_tiny = -(-_rows_of(tiny_shapes) // 8) * 8
    td, tm_, tv = _adamw(_pack_rows(tiny_g, n_tiny), _pack_rows(tiny_w, n_tiny), _pack_rows(tiny_m, n_tiny),
                         _pack_rows(tiny_v, n_tiny), "adamw_small")
    for name, d, mn, vn in zip(["a_sgu_ln_g", "a_sgu_ln_b", "a_w_s", "a_b_s", "kv_b_f", "ln_g", "ln_b"],
                               _unpack_rows(td, tiny_shapes), _unpack_rows(tm_, tiny_shapes),
                               _unpack_rows(tv, tiny_shapes)):
        upd[name] = (d, mn, vn)

    order = ["a_w_in", "a_sgu_ln_g", "a_sgu_ln_b", "a_w_s", "a_b_s", "kv_w", "kv_b_f", "b_w_q", "mem_w_kv", "w_o",
             "ln_g", "ln_b", "w_up", "w_down"]
    grads = {"a_w_in": g_a_w_in.reshape(a_w_in.shape), "a_sgu_ln_g": g_sg.reshape(a_sgu_ln_g.shape),
             "a_sgu_ln_b": g_sb.reshape(a_sgu_ln_b.shape), "a_w_s": g_ws.reshape(a_w_s.shape),
             "a_b_s": g_bs.reshape(a_b_s.shape), "kv_w": g_kv_w, "kv_b_f": g_bf, "b_w_q": g_b_w_q.reshape(b_w_q.shape),
             "mem_w_kv": g_mem_w_kv, "w_o": g_w_o, "ln_g": g_ln_g, "ln_b": g_ln_b, "w_up": g_w_up, "w_down": g_w_down}
    return (loss, grad_x[None], *[grads[n] for n in order], *[upd[n][0] for n in order],
            *[upd[n][1] for n in order], *[upd[n][2] for n in order])
```

```python
import functools
import math

import jax
import jax.numpy as jnp
from jax import lax
from jax.experimental import pallas as pl
from jax.experimental.pallas import tpu as pltpu

F32 = jnp.float32
BF16 = jnp.bfloat16

HEAD_DIM = 64
PAIR = 2 * HEAD_DIM
CHUNK = 128
LN_EPS = 1e-5
ALPHA = 4 ** 0.25
QK_SCALE = 1.0 / math.sqrt(HEAD_DIM)
NEG = -1e30
N_CHIPS = 4
FOX_BLOCK = 256
LANES = 128
VMEM_LIMIT = 48 * 2 ** 20

ADAM_LR, ADAM_B1, ADAM_B2, ADAM_EPS, ADAM_WD, ADAM_STEP = 0.001, 0.9, 0.999, 1e-08, 0.01, 10

MESH = pl.DeviceIdType.MESH
ANY = pl.BlockSpec(memory_space=pl.ANY)


def _dot(a, b):
    return jnp.dot(a, b, preferred_element_type=F32)


def _dot_nt(a, b):
    return lax.dot_general(a, b, (((1,), (1,)), ((), ())), preferred_element_type=F32)


def _dot_tn(a, b):
    return lax.dot_general(a, b, (((0,), (0,)), ((), ())), preferred_element_type=F32)


def _ln_stats(r):
    mu = jnp.mean(r, axis=-1, keepdims=True)
    d = r - mu
    var = jnp.mean(d * d, axis=-1, keepdims=True)
    rstd = lax.rsqrt(var + LN_EPS)
    return d * rstd, rstd


def _ln_bwd(dy, xhat, rstd, g):
    dxh = dy * g
    m1 = jnp.mean(dxh, axis=-1, keepdims=True)
    m2 = jnp.mean(dxh * xhat, axis=-1, keepdims=True)
    return rstd * (dxh - m1 - xhat * m2)


_GELU_K = math.sqrt(2.0 / math.pi)


def _gelu(x):
    return 0.5 * x * (1.0 + jnp.tanh(_GELU_K * (x + 0.044715 * x * x * x)))


def _gelu_grad(x):
    t = jnp.tanh(_GELU_K * (x + 0.044715 * x * x * x))
    return 0.5 * (1.0 + t) + 0.5 * x * (1.0 - t * t) * _GELU_K * (1.0 + 3 * 0.044715 * x * x)


def _lane_mask(e):
    lane = lax.broadcasted_iota(jnp.int32, (1, PAIR), 1)
    return (lane >= HEAD_DIM * e) & (lane < HEAD_DIM * (e + 1))


def _first_head():
    return lax.broadcasted_iota(jnp.int32, (1, PAIR), 1) < HEAD_DIM


def _mem_probs(qh, mkp):
    sc = _dot_nt(qh, mkp) * QK_SCALE
    ex = jnp.exp(sc - jnp.max(sc, axis=-1, keepdims=True))
    return ex / jnp.sum(ex, axis=-1, keepdims=True)


def _mem_attn_fwd(qm_b, mkv_ref, dq_dim):
    outs = []
    for mp in range(dq_dim // PAIR):
        qp = qm_b[:, mp * PAIR:(mp + 1) * PAIR]
        mkp = mkv_ref[:, mp * PAIR:(mp + 1) * PAIR]
        mvp = mkv_ref[:, dq_dim + mp * PAIR:dq_dim + (mp + 1) * PAIR]
        heads = []
        for e in (0, 1):
            qh = jnp.where(_lane_mask(e), qp, jnp.zeros_like(qp))
            p = _mem_probs(qh, mkp)
            heads.append(_dot(p.astype(BF16), mvp))
        outs.append(jnp.where(_first_head(), heads[0], heads[1]))
    return outs


def _mem_attn_bwd(qm_b, mkv_ref, dmo_pairs, dmkv_ref, dq_dim):
    dqs = []
    for mp in range(dq_dim // PAIR):
        ks = slice(mp * PAIR, (mp + 1) * PAIR)
        vs = slice(dq_dim + mp * PAIR, dq_dim + (mp + 1) * PAIR)
        qp = qm_b[:, ks]
        mkp = mkv_ref[:, ks]
        mvp = mkv_ref[:, vs]
        dmo_b = dmo_pairs[mp].astype(BF16)
        dq = None
        dmk = None
        dmv = None
        for e in (0, 1):
            hm = _lane_mask(e)
            qh = jnp.where(hm, qp, jnp.zeros_like(qp))
            p = _mem_probs(qh, mkp)
            doh = jnp.where(hm, dmo_b, jnp.zeros_like(dmo_b))
            dp = _dot_nt(doh, mvp)
            ds = p * (dp - jnp.sum(dp * p, axis=-1, keepdims=True))
            dsb = (ds * QK_SCALE).astype(BF16)
            kh = jnp.where(hm, mkp, jnp.zeros_like(mkp))
            dq_e = _dot(dsb, kh)
            dmk_e = _dot_tn(dsb, qh)
            dmv_e = _dot_tn(p.astype(BF16), doh)
            dq = dq_e if dq is None else dq + dq_e
            dmk = dmk_e if dmk is None else dmk + dmk_e
            dmv = dmv_e if dmv is None else dmv + dmv_e
        dmkv_ref[:, ks] += dmk
        dmkv_ref[:, vs] += dmv
        dqs.append(dq)
    return dqs


def _params(n_axes=1):
    return pltpu.CompilerParams(dimension_semantics=("arbitrary",) * n_axes, vmem_limit_bytes=VMEM_LIMIT)


def _full(shape):
    return pl.BlockSpec(shape, lambda *_: (0,) * len(shape))


def _rows(tm, cols):
    return pl.BlockSpec((tm, cols), lambda i: (i, 0))


def _cols(rows, tm):
    return pl.BlockSpec((rows, tm), lambda i: (0, i))


def _pick(n, pref):
    if n <= pref:
        return n
    best = LANES
    for t in range(LANES, pref + 1, LANES):
        if n % t == 0:
            best = t
    return best if 2 * best >= pref or n > 2 * pref else n


class _Comm:
    def __init__(self, ins, out_shapes, aliases, sem_counts, start, finish, middle=None):
        self.ins, self.out_shapes, self.aliases, self.sem_counts = list(ins), list(out_shapes), dict(aliases), sem_counts
        self.start, self.finish = start, finish
        self.middle = middle


def _pcall(body, *, name, grid, in_specs, out_specs, out_shape, args, scratch_shapes=(), comm=None, aliases=None):
    n_in, n_out, n_scr = len(in_specs), len(out_shape), len(scratch_shapes)
    aliases = dict(aliases or {})
    if comm is None:
        res = pl.pallas_call(body, name=name, grid=grid, out_shape=tuple(out_shape), in_specs=list(in_specs),
                             out_specs=tuple(out_specs), scratch_shapes=list(scratch_shapes),
                             input_output_aliases=aliases, compiler_params=_params())(*args)
        return tuple(res), ()
    nci, nco = len(comm.ins), len(comm.out_shapes)
    last = grid[0] - 1

    def wrapped(*refs):
        ins, refs = refs[:n_in], refs[n_in:]
        cins, refs = refs[:nci], refs[nci:]
        outs, refs = refs[:n_out], refs[n_out:]
        couts, refs = refs[:nco], refs[nco:]
        scr, sems = refs[:n_scr], refs[n_scr:]

        @pl.when(pl.program_id(0) == 0)
        def _():
            comm.start(cins, couts, sems)

        if comm.middle is not None:
            @pl.when(pl.program_id(0) == (last + 1) // 2)
            def _():
                comm.middle(cins, couts, sems)

        body(*ins, *outs, *scr)

        @pl.when(pl.program_id(0) == last)
        def _():
            comm.finish(cins, couts, sems)

    res = pl.pallas_call(
        wrapped, name=name, grid=grid, out_shape=tuple(out_shape) + tuple(comm.out_shapes),
        in_specs=list(in_specs) + [ANY] * nci, out_specs=tuple(out_specs) + (ANY,) * nco,
        input_output_aliases={**aliases, **{n_in + a: n_out + b for a, b in comm.aliases.items()}},
        scratch_shapes=list(scratch_shapes) + [pltpu.SemaphoreType.DMA((k,)) for k in comm.sem_counts],
        compiler_params=pltpu.CompilerParams(dimension_semantics=("arbitrary",), vmem_limit_bytes=VMEM_LIMIT,
                                             has_side_effects=True),
    )(*args, *comm.ins)
    return tuple(res[:n_out]), tuple(res[n_out:])


def _run_comm(comm, name):
    nci, nco = len(comm.ins), len(comm.out_shapes)

    def body(*refs):
        cins, couts, sems = refs[:nci], refs[nci:nci + nco], refs[nci + nco:]
        comm.start(cins, couts, sems)
        if comm.middle is not None:
            comm.middle(cins, couts, sems)
        comm.finish(cins, couts, sems)

    return pl.pallas_call(
        body, name=name, out_shape=tuple(comm.out_shapes), in_specs=[ANY] * nci, out_specs=(ANY,) * nco,
        input_output_aliases=comm.aliases, scratch_shapes=[pltpu.SemaphoreType.DMA((k,)) for k in comm.sem_counts],
        compiler_params=pltpu.CompilerParams(has_side_effects=True),
    )(*comm.ins)


def _mm(a, b, name, a_is_transposed, out_split=1):
    (M, R) = a.shape if a_is_transposed else a.shape[::-1]
    _, N = b.shape
    bm, br = _pick(M, 1024 if a_is_transposed else 512), _pick(R, 1024 if a_is_transposed else 512)
    ncol = N // out_split
    bn = _pick(ncol, 1024)
    per = ncol // bn
    last = R // br - 1

    def body(a_ref, b_ref, o_ref, acc_ref):
        @pl.when(pl.program_id(2) == 0)
        def _():
            acc_ref[...] = jnp.zeros_like(acc_ref)

        acc_ref[...] += (_dot if a_is_transposed else _dot_tn)(a_ref[...], b_ref[...])

        @pl.when(pl.program_id(2) == last)
        def _():
            o_ref[...] = acc_ref[...].astype(BF16)

    if out_split == 1:
        out_shape = jax.ShapeDtypeStruct((M, N), BF16)
        out_spec = pl.BlockSpec((bm, bn), lambda i, j, r: (i, j))
    else:
        out_shape = jax.ShapeDtypeStruct((out_split, M, ncol), BF16)
        out_spec = pl.BlockSpec((None, bm, bn), lambda i, j, r: (j // per, i, j % per))
    a_spec = (pl.BlockSpec((bm, br), lambda i, j, r: (i, r)) if a_is_transposed
              else pl.BlockSpec((br, bm), lambda i, j, r: (r, i)))
    return pl.pallas_call(
        body, name=name, out_shape=out_shape, grid=(M // bm, N // bn, R // br),
        in_specs=[a_spec, pl.BlockSpec((br, bn), lambda i, j, r: (r, j))], out_specs=out_spec,
        scratch_shapes=[pltpu.VMEM((bm, bn), F32)],
        compiler_params=pltpu.CompilerParams(dimension_semantics=("parallel", "parallel", "arbitrary"),
                                             vmem_limit_bytes=VMEM_LIMIT),
    )(a, b)


def _mm_tn(a, b, name, out_split=1):
    return _mm(a, b, name, False, out_split)


def _mm_nn(at, b, name, out_split=1):
    return _mm(at, b, name, True, out_split)


def _mm_small(a, b, name):
    def body(a_ref, b_ref, o_ref):
        o_ref[...] = _dot(a_ref[...], b_ref[...]).astype(BF16)

    return pl.pallas_call(body, name=name, out_shape=jax.ShapeDtypeStruct((a.shape[0], b.shape[1]), BF16),
                          compiler_params=pltpu.CompilerParams(vmem_limit_bytes=VMEM_LIMIT))(a, b)


def _mlp_out(x_ref, wup_ref, wdown_ref, g_ref, b_ref, r_ref, h_ref):
    NC, _, FC = wup_ref.shape
    xv = x_ref[...]
    xb = xv.astype(BF16)
    y = jnp.zeros(xv.shape, F32)
    for c in range(NC):
        a = jnp.maximum(_dot(xb, wup_ref[c]), 0.0)
        hb = (a * a).astype(BF16)
        h_ref[:, c * FC:(c + 1) * FC] = hb
        y = y + _dot(hb, wdown_ref[c])
    r = ALPHA * xv + y
    r_ref[...] = r
    xhat, _ = _ln_stats(r)
    return xhat * g_ref[...] + b_ref[...]


def _mlp_fwd_loss(x, wup4, wdown4, g, b, target, name):
    S, D = x.shape
    NC, _, FC = wup4.shape
    tm = min(256, S)

    def body(x_ref, wup_ref, wdown_ref, g_ref, b_ref, t_ref, d_ref, r_ref, h_ref, l_ref):
        @pl.when(pl.program_id(0) == 0)
        def _():
            l_ref[...] = jnp.zeros_like(l_ref)

        err = _mlp_out(x_ref, wup_ref, wdown_ref, g_ref, b_ref, r_ref, h_ref) - t_ref[...]
        d_ref[...] = err * (1.0 / D)
        l_ref[...] += (0.5 / D) * jnp.sum(err * err)

    return pl.pallas_call(
        body, name=name, grid=(S // tm,),
        out_shape=(jax.ShapeDtypeStruct((S, D), F32), jax.ShapeDtypeStruct((S, D), F32),
                   jax.ShapeDtypeStruct((S, NC * FC), BF16), jax.ShapeDtypeStruct((8, LANES), F32)),
        in_specs=[_rows(tm, D), _full(wup4.shape), _full(wdown4.shape), _full((1, D)), _full((1, D)), _rows(tm, D)],
        out_specs=(_rows(tm, D), _rows(tm, D), _rows(tm, NC * FC), _full((8, LANES))),
        compiler_params=_params(),
    )(x, wup4, wdown4, g, b, target)


def _mlp_fwd(x, wup4, wdown4, g, b, name, comm=None):
    S, D = x.shape
    NC, _, FC = wup4.shape
    tm = min(256, S)

    def body(x_ref, wup_ref, wdown_ref, g_ref, b_ref, xo_ref, xb_ref, xbt_ref, r_ref, h_ref):
        xo = _mlp_out(x_ref, wup_ref, wdown_ref, g_ref, b_ref, r_ref, h_ref)
        xo_ref[...] = xo
        xb_ref[...] = xo.astype(BF16)
        xbt_ref[...] = xo.T.astype(BF16)

    return _pcall(
        body, name=name, grid=(S // tm,), comm=comm,
        out_shape=(jax.ShapeDtypeStruct((S, D), F32), jax.ShapeDtypeStruct((S, D), BF16),
                   jax.ShapeDtypeStruct((D, S), BF16), jax.ShapeDtypeStruct((S, D), F32),
                   jax.ShapeDtypeStruct((S, NC * FC), BF16)),
        in_specs=[_rows(tm, D), _full(wup4.shape), _full(wdown4.shape), _full((1, D)), _full((1, D))],
        out_specs=(_rows(tm, D), _rows(tm, D), _cols(D, tm), _rows(tm, D), _rows(tm, NC * FC)),
        args=(x, wup4, wdown4, g, b))


def _mlp_bwd(dxo, r, h, wup4, wdown4, g, name, comm=None):
    S, D = r.shape
    NC, _, FC = wup4.shape
    tm = min(256, S)

    def body(dxo_ref, r_ref, h_ref, wup_ref, wdown_ref, g_ref, dx_ref, drbt_ref, dpre_ref, dgb_ref):
        @pl.when(pl.program_id(0) == 0)
        def _():
            dgb_ref[...] = jnp.zeros_like(dgb_ref)

        xhat, rstd = _ln_stats(r_ref[...])
        dy = dxo_ref[...]
        dgb_ref[0:1, :] += jnp.sum(dy * xhat, axis=0, keepdims=True)
        dgb_ref[1:2, :] += jnp.sum(dy, axis=0, keepdims=True)
        dr = _ln_bwd(dy, xhat, rstd, g_ref[...])
        drb = dr.astype(BF16)
        drbt_ref[...] = dr.T.astype(BF16)
        dx = ALPHA * dr
        for c in range(NC):
            dh = _dot_nt(drb, wdown_ref[c])
            a = jnp.sqrt(h_ref[:, c * FC:(c + 1) * FC].astype(F32))
            dpre = (2.0 * a * dh).astype(BF16)
            dpre_ref[:, c * FC:(c + 1) * FC] = dpre
            dx = dx + _dot_nt(dpre, wup_ref[c])
        dx_ref[...] = dx

    return _pcall(
        body, name=name, grid=(S // tm,), comm=comm,
        out_shape=(jax.ShapeDtypeStruct((S, D), F32), jax.ShapeDtypeStruct((D, S), BF16),
                   jax.ShapeDtypeStruct((S, NC * FC), BF16), jax.ShapeDtypeStruct((2, D), F32)),
        in_specs=[_rows(tm, D), _rows(tm, D), _rows(tm, NC * FC), _full(wup4.shape), _full(wdown4.shape),
                  _full((1, D))],
        out_specs=(_rows(tm, D), _cols(D, tm), _rows(tm, NC * FC), _full((2, D))),
        args=(dxo, r, h, wup4, wdown4, g))


def _tail_fwd_store(xv, cat_ref, wo_ref, g_ref, b_ref, xo_ref, xbt_ref, r_ref, catt_ref):
    cat = cat_ref[...]
    y = _dot(cat, wo_ref[...])
    r = ALPHA * xv + y
    xhat, _ = _ln_stats(r)
    xo = xhat * g_ref[...] + b_ref[...]
    xo_ref[...] = xo
    xbt_ref[...] = xo.T.astype(BF16)
    catt_ref[...] = cat.astype(F32).T.astype(BF16)
    r_ref[...] = r


def _tail_bwd_head(dxo_ref, r_ref, g_ref, wo_ref, dgb_ref, drb_ref):
    xhat, rstd = _ln_stats(r_ref[...])
    dy = dxo_ref[...]
    dgb_ref[0:1, :] += jnp.sum(dy * xhat, axis=0, keepdims=True)
    dgb_ref[1:2, :] += jnp.sum(dy, axis=0, keepdims=True)
    dr = _ln_bwd(dy, xhat, rstd, g_ref[...])
    drb = dr.astype(BF16)
    drb_ref[...] = drb
    return dr, _dot_nt(drb, wo_ref[...])


def _causal(shape):
    return lax.broadcasted_iota(jnp.int32, shape, 1) <= lax.broadcasted_iota(jnp.int32, shape, 0)


def _sgu_mixed(ws_ref, bst_ref, gp, vpair):
    tril = _causal((CHUNK, CHUNK))
    w0 = jnp.where(tril, ws_ref[2 * gp], 0.0).astype(BF16)
    w1 = jnp.where(tril, ws_ref[2 * gp + 1], 0.0).astype(BF16)
    m0 = _dot(w0, vpair) + bst_ref[:, 2 * gp:2 * gp + 1]
    m1 = _dot(w1, vpair) + bst_ref[:, 2 * gp + 1:2 * gp + 2]
    return jnp.where(_first_head(), m0, m1), w0, w1


def _mixer_a_fwd(x, wa, sg, sb, ws, bst, mkv, wo, g, b, name, comm=None):
    S, D = x.shape
    DQ = mkv.shape[1] // 2
    DM = D - DQ
    NP = DM // PAIR
    tm = min(512, S)

    def body(x_ref, wa_ref, sg_ref, sb_ref, ws_ref, bst_ref, mkv_ref, wo_ref, g_ref, b_ref,
             xo_ref, xbt_ref, r_ref, catt_ref, xint_ref, zb_ref, z_s, zv_s, cat_ref):
        xv = x_ref[...]
        xint_ref[...] = xv.T.astype(BF16)
        z_s[...] = _dot(xv.astype(BF16), wa_ref[...])
        zb_ref[...] = z_s[...].astype(BF16)
        vhat, _ = _ln_stats(_gelu(z_s[:, DM:2 * DM]))
        zv_s[...] = (vhat * sg_ref[...] + sb_ref[...]).astype(BF16)
        for c in range(tm // CHUNK):
            rs = slice(c * CHUNK, (c + 1) * CHUNK)
            for gp in range(NP):
                ls = slice(gp * PAIR, (gp + 1) * PAIR)
                mixed, _, _ = _sgu_mixed(ws_ref, bst_ref, gp, zv_s[rs, ls])
                cat_ref[rs, ls] = (_gelu(z_s[rs, ls]) * mixed).astype(BF16)
        mo = _mem_attn_fwd(z_s[:, 2 * DM:].astype(BF16), mkv_ref, DQ)
        for mp in range(DQ // PAIR):
            cat_ref[:, DM + mp * PAIR:DM + (mp + 1) * PAIR] = mo[mp].astype(BF16)
        _tail_fwd_store(xv, cat_ref, wo_ref, g_ref, b_ref, xo_ref, xbt_ref, r_ref, catt_ref)

    return _pcall(
        body, name=name, grid=(S // tm,), comm=comm,
        out_shape=(jax.ShapeDtypeStruct((S, D), F32), jax.ShapeDtypeStruct((D, S), BF16),
                   jax.ShapeDtypeStruct((S, D), F32), jax.ShapeDtypeStruct((D, S), BF16),
                   jax.ShapeDtypeStruct((D, S), BF16), jax.ShapeDtypeStruct((S, 2 * DM + DQ), BF16)),
        in_specs=[_rows(tm, D), _full(wa.shape), _full(sg.shape), _full(sb.shape), _full(ws.shape), _full(bst.shape),
                  _full(mkv.shape), _full(wo.shape), _full((1, D)), _full((1, D))],
        out_specs=(_rows(tm, D), _cols(D, tm), _rows(tm, D), _cols(D, tm), _cols(D, tm), _rows(tm, 2 * DM + DQ)),
        scratch_shapes=[pltpu.VMEM((tm, 2 * DM + DQ), F32), pltpu.VMEM((tm, DM), BF16), pltpu.VMEM((tm, D), BF16)],
        args=(x, wa, sg, sb, ws, bst, mkv, wo, g, b))


def _mixer_a_bwd(zb, dxo, r, wa, sg, sb, ws, bst, mkv, wo, g, name, comm=None):
    S, D = r.shape
    DQ = mkv.shape[1] // 2
    DM = D - DQ
    NP = DM // PAIR
    NG = DM // HEAD_DIM
    tm = min(512, S)

    def body(zb_ref, dxo_ref, r_ref, wa_ref, sg_ref, sb_ref, ws_ref, bst_ref, mkv_ref, wo_ref, g_ref,
             dx_ref, dz_ref, drb_ref, dgb_ref, dsgb_ref, dws_ref, dbst_ref, dmkv_ref, z_s, zv_s, dzv_s):
        @pl.when(pl.program_id(0) == 0)
        def _():
            dgb_ref[...] = jnp.zeros_like(dgb_ref)
            dsgb_ref[...] = jnp.zeros_like(dsgb_ref)
            dws_ref[...] = jnp.zeros_like(dws_ref)
            dbst_ref[...] = jnp.zeros_like(dbst_ref)
            dmkv_ref[...] = jnp.zeros_like(dmkv_ref)

        dr, dcat = _tail_bwd_head(dxo_ref, r_ref, g_ref, wo_ref, dgb_ref, drb_ref)
        z_s[...] = zb_ref[...].astype(F32)
        vhat, vrstd = _ln_stats(_gelu(z_s[:, DM:2 * DM]))
        zv_s[...] = (vhat * sg_ref[...] + sb_ref[...]).astype(BF16)
        tril = _causal((CHUNK, CHUNK))
        lane = lax.broadcasted_iota(jnp.int32, (1, LANES), 1)
        for c in range(tm // CHUNK):
            rs = slice(c * CHUNK, (c + 1) * CHUNK)
            for gp in range(NP):
                ls = slice(gp * PAIR, (gp + 1) * PAIR)
                vpair = zv_s[rs, ls]
                mixed, w0, w1 = _sgu_mixed(ws_ref, bst_ref, gp, vpair)
                u_pre = z_s[rs, ls]
                dmix = dcat[rs, ls]
                dz_ref[rs, ls] = (dmix * mixed * _gelu_grad(u_pre)).astype(BF16)
                dmixed = dmix * _gelu(u_pre)
                first = _first_head()
                d0 = jnp.where(first, dmixed, 0.0)
                d1 = jnp.where(first, 0.0, dmixed)
                d0b = d0.astype(BF16)
                d1b = d1.astype(BF16)
                dzv_s[rs, ls] = _dot_tn(w0, d0b) + _dot_tn(w1, d1b)
                dws_ref[2 * gp] += jnp.where(tril, _dot_nt(d0b, vpair), 0.0)
                dws_ref[2 * gp + 1] += jnp.where(tril, _dot_nt(d1b, vpair), 0.0)
                dbst_ref[...] += (jnp.where(lane == 2 * gp, jnp.sum(d0, axis=-1, keepdims=True), 0.0)
                                  + jnp.where(lane == 2 * gp + 1, jnp.sum(d1, axis=-1, keepdims=True), 0.0))
        dzv = dzv_s[...]
        dsgb_ref[0:1, :] += jnp.sum(dzv * vhat, axis=0, keepdims=True)
        dsgb_ref[1:2, :] += jnp.sum(dzv, axis=0, keepdims=True)
        dgv = _ln_bwd(dzv, vhat, vrstd, sg_ref[...])
        dz_ref[:, DM:2 * DM] = (dgv * _gelu_grad(z_s[:, DM:2 * DM])).astype(BF16)
        dmo = [dcat[:, DM + mp * PAIR:DM + (mp + 1) * PAIR] for mp in range(DQ // PAIR)]
        dqm = _mem_attn_bwd(z_s[:, 2 * DM:].astype(BF16), mkv_ref, dmo, dmkv_ref, DQ)
        for mp in range(DQ // PAIR):
            dz_ref[:, 2 * DM + mp * PAIR:2 * DM + (mp + 1) * PAIR] = dqm[mp].astype(BF16)
        dx_ref[...] = ALPHA * dr + _dot_nt(dz_ref[...], wa_ref[...])

    ZW = 2 * DM + DQ
    M = mkv.shape[0]
    return _pcall(
        body, name=name, grid=(S // tm,), comm=comm,
        out_shape=(jax.ShapeDtypeStruct((S, D), F32), jax.ShapeDtypeStruct((S, ZW), BF16),
                   jax.ShapeDtypeStruct((S, D), BF16), jax.ShapeDtypeStruct((2, D), F32),
                   jax.ShapeDtypeStruct((2, DM), F32), jax.ShapeDtypeStruct((NG, CHUNK, CHUNK), F32),
                   jax.ShapeDtypeStruct((CHUNK, LANES), F32), jax.ShapeDtypeStruct((M, 2 * DQ), F32)),
        in_specs=[_rows(tm, ZW), _rows(tm, D), _rows(tm, D), _full(wa.shape), _full(sg.shape), _full(sb.shape),
                  _full(ws.shape), _full(bst.shape), _full(mkv.shape), _full(wo.shape), _full((1, D))],
        out_specs=(_rows(tm, D), _rows(tm, ZW), _rows(tm, D), _full((2, D)), _full((2, DM)),
                   _full((NG, CHUNK, CHUNK)), _full((CHUNK, LANES)), _full((M, 2 * DQ))),
        scratch_shapes=[pltpu.VMEM((tm, ZW), F32), pltpu.VMEM((tm, DM), BF16), pltpu.VMEM((tm, DM), F32)],
        args=(zb, dxo, r, wa, sg, sb, ws, bst, mkv, wo, g))


def _proj_b_fwd(xb, wq, wkv, dm, name):
    S, D = xb.shape
    DQ = D - dm
    tm = min(512, S)

    def body(x_ref, wq_ref, wkv_ref, q_ref, qm_ref, k_ref, v_ref, fl_ref):
        xv = x_ref[...]
        z = _dot(xv, wq_ref[...])
        q_ref[...] = z[:, :dm].astype(BF16)
        qm_ref[...] = z[:, dm:].astype(BF16)
        kvf = _dot(xv, wkv_ref[...])
        k_ref[...] = kvf[:, :dm].astype(BF16)
        v_ref[...] = kvf[:, dm:2 * dm].astype(BF16)
        fl_ref[...] = kvf[:, 2 * dm:]

    return pl.pallas_call(
        body, name=name, grid=(S // tm,),
        out_shape=(jax.ShapeDtypeStruct((S, dm), BF16), jax.ShapeDtypeStruct((S, DQ), BF16),
                   jax.ShapeDtypeStruct((S, dm), BF16), jax.ShapeDtypeStruct((S, dm), BF16),
                   jax.ShapeDtypeStruct((S, LANES), F32)),
        in_specs=[_rows(tm, D), _full(wq.shape), _full(wkv.shape)],
        out_specs=(_rows(tm, dm), _rows(tm, DQ), _rows(tm, dm), _rows(tm, dm), _rows(tm, LANES)),
        compiler_params=_params(),
    )(xb, wq, wkv)


def _proj_b_bwd(dr, dqz, dkvf, wq, wkv, name):
    S, D = dr.shape
    tm = min(512, S)

    def body(dr_ref, dqz_ref, dkvf_ref, wq_ref, wkv_ref, dx_ref):
        dx_ref[...] = (ALPHA * dr_ref[...] + _dot_nt(dqz_ref[...], wq_ref[...])
                       + _dot_nt(dkvf_ref[...], wkv_ref[...]))

    return pl.pallas_call(
        body, name=name, grid=(S // tm,), out_shape=jax.ShapeDtypeStruct((S, D), F32),
        in_specs=[_rows(tm, D), _rows(tm, dqz.shape[1]), _rows(tm, dkvf.shape[1]), _full(wq.shape), _full(wkv.shape)],
        out_specs=_rows(tm, D), compiler_params=_params(),
    )(dr, dqz, dkvf, wq, wkv)


def _split3(v):
    hi = v.astype(BF16)
    r1 = v - hi.astype(F32)
    mid = r1.astype(BF16)
    lo = (r1 - mid.astype(F32)).astype(BF16)
    return hi, mid, lo


def _tri_sum(v, tri_b):
    hi, mid, lo = _split3(v)
    return _dot(hi, tri_b) + _dot(mid, tri_b) + _dot(lo, tri_b)


def _log_sigmoid(x):
    return jnp.minimum(x, 0.0) - jnp.log(1.0 + jnp.exp(-jnp.abs(x)))


def _forget_cumsum(flt, bf, name):
    H, S = flt.shape
    nchunk = S // LANES

    def body(fl_ref, bf_ref, p_ref):
        upper = (lax.broadcasted_iota(jnp.int32, (LANES, LANES), 0)
                 <= lax.broadcasted_iota(jnp.int32, (LANES, LANES), 1)).astype(BF16)

        def step(n, carry):
            s0 = pl.multiple_of(n * LANES, LANES)
            lf = _log_sigmoid(fl_ref[:, pl.ds(s0, LANES)] + bf_ref[...])
            cs = _tri_sum(lf, upper) + carry
            for i, piece in enumerate(_split3(-cs)):
                p_ref[i, :, pl.ds(s0, LANES)] = piece.astype(F32)
            return cs[:, LANES - 1:LANES]

        lax.fori_loop(0, nchunk, step, jnp.zeros((H, 1), F32))

    return pl.pallas_call(body, name=name, out_shape=jax.ShapeDtypeStruct((3, H, S), F32),
                          compiler_params=pltpu.CompilerParams(vmem_limit_bytes=VMEM_LIMIT))(flt, bf)


def _forget_cumsum_bwd(dct, flt, bf, name):
    H, S = flt.shape
    nchunk = S // LANES

    def body(dc_ref, fl_ref, bf_ref, dfl_ref, dbf_ref):
        lower = (lax.broadcasted_iota(jnp.int32, (LANES, LANES), 0)
                 >= lax.broadcasted_iota(jnp.int32, (LANES, LANES), 1)).astype(BF16)

        def step(n, carry):
            tail, tot = carry
            s0 = pl.multiple_of((nchunk - 1 - n) * LANES, LANES)
            suffix = _tri_sum(dc_ref[:, pl.ds(s0, LANES)], lower) + tail
            xv = fl_ref[:, pl.ds(s0, LANES)] + bf_ref[...]
            dfl = suffix * (1.0 / (1.0 + jnp.exp(xv)))
            dfl_ref[:, pl.ds(s0, LANES)] = dfl
            return suffix[:, 0:1], tot + jnp.sum(dfl, axis=-1, keepdims=True)

        _, tot = lax.fori_loop(0, nchunk, step, (jnp.zeros((H, 1), F32), jnp.zeros((H, 1), F32)))
        dbf_ref[...] = jnp.broadcast_to(tot, (H, LANES))

    return pl.pallas_call(body, name=name,
                          out_shape=(jax.ShapeDtypeStruct((H, S), F32), jax.ShapeDtypeStruct((H, LANES), F32)),
                          compiler_params=pltpu.CompilerParams(vmem_limit_bytes=VMEM_LIMIT))(dct, flt, bf)


N_BIAS = 3


def _bias_ones():
    lane = lax.broadcasted_iota(jnp.int32, (1, PAIR), 1)
    return ((lane & (HEAD_DIM - 1)) < N_BIAS).astype(BF16)


def _fox_fwd(q, k, v, ca, name, comm=None):
    S, DM = q.shape
    NP = DM // PAIR
    T = min(FOX_BLOCK, S)
    nq = S // T

    def body(q_ref, k_ref, v_ref, ca_ref, o_ref, lse_ref):
        tril = _causal((T, T))
        heads = (_lane_mask(0), _lane_mask(1))
        ones3 = _bias_ones()

        def kv_block(kj):
            s0 = pl.multiple_of(kj * T, T)
            kb, vb, cab = k_ref[pl.ds(s0, T), :], v_ref[pl.ds(s0, T), :], ca_ref[pl.ds(s0, T), :]
            return ([jnp.where(heads[e], kb, cab) for e in (0, 1)],
                    [jnp.where(heads[e], vb, jnp.ones_like(vb)) for e in (0, 1)])

        def q_pair(a, _):
            t0s = [pl.multiple_of((2 * a + r) * T, T) for r in (0, 1)]
            qaug = {}
            for r in (0, 1):
                qb = q_ref[pl.ds(t0s[r], T), :] * QK_SCALE
                for e in (0, 1):
                    qaug[r, e] = jnp.where(heads[e], qb, ones3)
            chains = [(r, e) for r in (0, 1) for e in (0, 1)]

            def update(carries, first_block, seen):
                blocks = [kv_block(first_block + kk) for kk in range(1 + max(kk for s in seen for kk, _ in s))]
                out = []
                scores = [[_dot_nt(qaug[r, e], blocks[kk][0][e]) for kk, _ in seen[r]] for r, e in chains]
                for n, (r, e) in enumerate(chains):
                    m, acc = carries[n]
                    tiles = [jnp.where(tril, s, NEG) if masked else s for s, (_, masked) in zip(scores[n], seen[r])]
                    m_new = jnp.maximum(m, jnp.max(functools.reduce(jnp.maximum, tiles), axis=-1, keepdims=True))
                    acc = jnp.exp(m - m_new) * acc
                    for s, (kk, _) in zip(tiles, seen[r]):
                        acc = acc + _dot(jnp.exp(s - m_new).astype(BF16), blocks[kk][1][e])
                    out.append((m_new, acc))
                return tuple(out)

            def full(first_block, n_blocks, carries):
                return update(carries, first_block, [[(kk, False) for kk in range(n_blocks)]] * 2)

            init = (jnp.full((T, 1), NEG, F32), jnp.zeros((T, PAIR), F32))
            carries = lax.fori_loop(0, a // 2, lambda i, c: full(4 * i, 4, c), (init,) * 4)
            carries = lax.fori_loop(0, a % 2, lambda i, c: full(2 * a - 2, 2, c), carries)
            carries = update(carries, 2 * a, [[(0, True)], [(0, False), (1, True)]])
            first = _first_head()
            for r in (0, 1):
                (m0, acc0), (m1, acc1) = carries[2 * r], carries[2 * r + 1]
                l0, l1 = acc0[:, HEAD_DIM:HEAD_DIM + 1], acc1[:, 0:1]
                o_ref[pl.ds(t0s[r], T), :] = jnp.where(first, acc0 / l0, acc1 / l1).astype(BF16)
                lse_t = jnp.where(first, m0 + jnp.log(l0), m1 + jnp.log(l1)).T
                lse_ref[0:1, pl.ds(t0s[r], T)] = lse_t[0:1, :]
                lse_ref[1:2, pl.ds(t0s[r], T)] = lse_t[HEAD_DIM:HEAD_DIM + 1, :]
            return 0

        lax.fori_loop(0, nq // 2, q_pair, 0)

    col = pl.BlockSpec((S, PAIR), lambda hp: (0, hp))
    return _pcall(
        body, name=name, grid=(NP,), comm=comm,
        out_shape=(jax.ShapeDtypeStruct((S, DM), BF16), jax.ShapeDtypeStruct((NP, 2, S), F32)),
        in_specs=[col, col, col, col], out_specs=(col, pl.BlockSpec((None, 2, S), lambda hp: (hp, 0, 0))),
        args=(q, k, v, ca))


def _fox_bwd(q, k, v, o, do, ca, lse, name, comm=None):
    S, DM = q.shape
    NP = DM // PAIR
    T = min(FOX_BLOCK, S)
    nq = S // T

    def body(q_ref, k_ref, v_ref, o_ref, do_ref, ca_ref, lse_ref, dq_ref, dk_ref, dv_ref, dc_ref,
             dq_acc, dcol_acc, dk_acc, dv_acc, dd_s):
        keep = lax.broadcasted_iota(jnp.int32, (T, T), 0) <= lax.broadcasted_iota(jnp.int32, (T, T), 1)
        lane = lax.broadcasted_iota(jnp.int32, (1, LANES), 1)
        heads = (_lane_mask(0), _lane_mask(1))
        ones3 = _bias_ones()
        dq_acc[...] = jnp.zeros_like(dq_acc)
        dcol_acc[...] = jnp.zeros_like(dcol_acc)
        dc_ref[...] = jnp.zeros_like(dc_ref)

        def row_dots(qi, _):
            t0 = pl.multiple_of(qi * T, T)
            prod = do_ref[pl.ds(t0, T), :].astype(F32) * o_ref[pl.ds(t0, T), :].astype(F32)
            cols = [jnp.sum(jnp.where(heads[e], prod, 0.0), axis=-1, keepdims=True) for e in (0, 1)]
            tile = jnp.where(lane == 0, cols[0], jnp.where(lane == 1, cols[1], 0.0))
            dd_s[:, pl.ds(t0, T)] = tile.T[0:8, :]
            return 0

        lax.fori_loop(0, nq, row_dots, 0)

        def kv_pair(b, _):
            dk_acc[...] = jnp.zeros_like(dk_acc)
            dv_acc[...] = jnp.zeros_like(dv_acc)
            s0s = [pl.multiple_of((2 * b + jj) * T, T) for jj in (0, 1)]
            kaug, ks, vbs = {}, {}, []
            for jj in (0, 1):
                kb = k_ref[pl.ds(s0s[jj], T), :]
                cab = ca_ref[pl.ds(s0s[jj], T), :]
                vbs.append(v_ref[pl.ds(s0s[jj], T), :])
                for e in (0, 1):
                    kaug[jj, e] = jnp.where(heads[e], kb, cab)
                    ks[jj, e] = jnp.where(heads[e], kb, jnp.zeros_like(kb)) * QK_SCALE

            def q_step(qi, blocks):
                t0 = pl.multiple_of(qi * T, T)
                qs = q_ref[pl.ds(t0, T), :] * QK_SCALE
                dob = do_ref[pl.ds(t0, T), :]
                doh = [jnp.where(heads[e], dob, jnp.zeros_like(dob)) for e in (0, 1)]
                qh = [jnp.where(heads[e], qs, jnp.zeros_like(qs)) for e in (0, 1)]
                chains = [(e, jj, masked) for e in (0, 1) for jj, masked in blocks]
                scores = [_dot_nt(kaug[jj, e], jnp.where(heads[e], qs, ones3)) for e, jj, _ in chains]
                dps = [_dot_nt(vbs[jj], doh[e]) for e, jj, _ in chains]
                dq = jnp.zeros((T, PAIR), F32)
                drow = [jnp.zeros((1, T), F32), jnp.zeros((1, T), F32)]
                for n, (e, jj, masked) in enumerate(chains):
                    st = jnp.where(keep, scores[n], NEG) if masked else scores[n]
                    pt = jnp.exp(st - lse_ref[e:e + 1, pl.ds(t0, T)])
                    dv_acc[jj] += _dot(pt.astype(BF16), doh[e])
                    dst = pt * (dps[n] - dd_s[e:e + 1, pl.ds(t0, T)])
                    dsb = dst.astype(BF16)
                    dk_acc[jj] += _dot(dsb, qh[e])
                    dq = dq + _dot_tn(dsb, ks[jj, e])
                    dcol_acc[pl.ds(s0s[jj], T), :] -= jnp.where(lane == e, jnp.sum(dst, axis=-1, keepdims=True), 0.0)
                    drow[e] = drow[e] + jnp.sum(dst, axis=0, keepdims=True)
                for e in (0, 1):
                    dc_ref[e:e + 1, pl.ds(t0, T)] += drow[e]
                dq_acc[pl.ds(t0, T), :] += dq

            q_step(2 * b, [(0, True)])
            q_step(2 * b + 1, [(0, False), (1, True)])

            def rest(i, _):
                q_step(2 * b + 2 + 2 * i, [(0, False), (1, False)])
                q_step(2 * b + 3 + 2 * i, [(0, False), (1, False)])
                return 0

            lax.fori_loop(0, (nq - 2) // 2 - b, rest, 0)
            for jj in (0, 1):
                dk_ref[pl.ds(s0s[jj], T), :] = dk_acc[jj].astype(BF16)
                dv_ref[pl.ds(s0s[jj], T), :] = dv_acc[jj].astype(BF16)
            return 0

        lax.fori_loop(0, nq // 2, kv_pair, 0)
        dq_ref[...] = dq_acc[...].astype(BF16)

        def add_cols(kj, _):
            s0 = pl.multiple_of(kj * T, T)
            dc_ref[:, pl.ds(s0, T)] += dcol_acc[pl.ds(s0, T), :].T[0:2, :]
            return 0

        lax.fori_loop(0, nq, add_cols, 0)

    col = pl.BlockSpec((S, PAIR), lambda hp: (0, hp))
    row2 = pl.BlockSpec((None, 2, S), lambda hp: (hp, 0, 0))
    return _pcall(
        body, name=name, grid=(NP,), comm=comm,
        out_shape=(jax.ShapeDtypeStruct(do.shape, BF16), jax.ShapeDtypeStruct((S, DM), BF16),
                   jax.ShapeDtypeStruct((S, DM), BF16), jax.ShapeDtypeStruct((NP, 2, S), F32)),
        in_specs=[col, col, col, col, col, col, row2], out_specs=(col, col, col, row2),
        scratch_shapes=[pltpu.VMEM((S, PAIR), F32), pltpu.VMEM((S, LANES), F32), pltpu.VMEM((2, T, PAIR), F32),
                        pltpu.VMEM((2, T, PAIR), F32), pltpu.VMEM((8, S), F32)],
        args=(q, k, v, o, do, ca, lse), aliases={4: 0})


def _mixer_b_fwd(x, o, qm, mkv, wo, g, b, name):
    S, D = x.shape
    DQ = qm.shape[1]
    DM = D - DQ
    tm = min(512, S)

    def body(x_ref, o_ref, qm_ref, mkv_ref, wo_ref, g_ref, b_ref, xo_ref, xbt_ref, r_ref, catt_ref, cat_ref):
        cat_ref[:, :DM] = o_ref[...]
        mo = _mem_attn_fwd(qm_ref[...], mkv_ref, DQ)
        for mp in range(DQ // PAIR):
            cat_ref[:, DM + mp * PAIR:DM + (mp + 1) * PAIR] = mo[mp].astype(BF16)
        _tail_fwd_store(x_ref[...], cat_ref, wo_ref, g_ref, b_ref, xo_ref, xbt_ref, r_ref, catt_ref)

    return pl.pallas_call(
        body, name=name, grid=(S // tm,),
        out_shape=(jax.ShapeDtypeStruct((S, D), F32), jax.ShapeDtypeStruct((D, S), BF16),
                   jax.ShapeDtypeStruct((S, D), F32), jax.ShapeDtypeStruct((D, S), BF16)),
        in_specs=[_rows(tm, D), _rows(tm, DM), _rows(tm, DQ), _full(mkv.shape), _full(wo.shape), _full((1, D)),
                  _full((1, D))],
        out_specs=(_rows(tm, D), _cols(D, tm), _rows(tm, D), _cols(D, tm)),
        scratch_shapes=[pltpu.VMEM((tm, D), BF16)],
        compiler_params=_params(),
    )(x, o, qm, mkv, wo, g, b)


def _mixer_b_bwd(dxo, r, qm, mkv, wo, g, name, comm=None):
    S, D = r.shape
    DQ = qm.shape[1]
    DM = D - DQ
    M = mkv.shape[0]
    tm = min(512, S)

    def body(dxo_ref, r_ref, qm_ref, mkv_ref, wo_ref, g_ref, dr_ref, drb_ref, dcat_ref, dgb_ref, dmkv_ref):
        @pl.when(pl.program_id(0) == 0)
        def _():
            dgb_ref[...] = jnp.zeros_like(dgb_ref)
            dmkv_ref[...] = jnp.zeros_like(dmkv_ref)

        dr, dcat = _tail_bwd_head(dxo_ref, r_ref, g_ref, wo_ref, dgb_ref, drb_ref)
        dr_ref[...] = dr
        dcat_ref[:, :DM] = dcat[:, :DM].astype(BF16)
        dmo = [dcat[:, DM + mp * PAIR:DM + (mp + 1) * PAIR] for mp in range(DQ // PAIR)]
        dqm = _mem_attn_bwd(qm_ref[...], mkv_ref, dmo, dmkv_ref, DQ)
        for mp in range(DQ // PAIR):
            dcat_ref[:, DM + mp * PAIR:DM + (mp + 1) * PAIR] = dqm[mp].astype(BF16)

    return _pcall(
        body, name=name, grid=(S // tm,), comm=comm,
        out_shape=(jax.ShapeDtypeStruct((S, D), F32), jax.ShapeDtypeStruct((S, D), BF16),
                   jax.ShapeDtypeStruct((S, D), BF16), jax.ShapeDtypeStruct((2, D), F32),
                   jax.ShapeDtypeStruct((M, 2 * DQ), F32)),
        in_specs=[_rows(tm, D), _rows(tm, D), _rows(tm, DQ), _full(mkv.shape), _full(wo.shape), _full((1, D))],
        out_specs=(_rows(tm, D), _rows(tm, D), _rows(tm, D), _full((2, D)), _full((M, 2 * DQ))),
        args=(dxo, r, qm, mkv, wo, g))


class _NoExchange:
    def plan(self, stage, w, grads):
        return None

    def done(self, stage, results, w):
        pass


def _local_step(x, mem, target, w, hooks):
    S, D = x.shape
    DQ = w["mkv0"].shape[1] // 2
    DM = D - DQ
    NG = DM // HEAD_DIM
    NP = DM // PAIR
    ln_g, ln_b = w["ln_g"], w["ln_b"]
    row = lambda a, i: a[i:i + 1]
    memb = mem.astype(BF16)
    bst =jnp.pad(w["b_s"].T, ((0, 0), (0, LANES - NG)))
    grads = {}

    def staged(stage, fn, *args):
        out, got = fn(*args, stage, comm=hooks.plan(stage, w, grads))
        hooks.done(stage, got, w)
        return out

    mkv0 = _mm_small(memb, w["mkv0"], "mem_kv_0")
    x1, x1t, r1, cat0t, x0t, z0b = staged("mixer_a_fwd", _mixer_a_fwd, x, w["wa"], w["sg"], w["sb"], w["ws"], bst, mkv0,
                                w["wo0"], row(ln_g, 0), row(ln_b, 0))
    x2, x2b, x2t, r2, h0 = staged("mlp_fwd_0", _mlp_fwd, x1, w["wup0"], w["wdown0"], row(ln_g, 1), row(ln_b, 1))
    q, qm, k, v, fl = _proj_b_fwd(x2b, w["wq"], w["wkv"], DM, "proj_b_fwd")
    flt = fl[:, :16].T
    bfc = jnp.pad(w["bf"], (0, 16 - NG)).reshape(16, 1)
    pieces = _forget_cumsum(flt, bfc, "forget_cumsum")
    pieces = jnp.pad(pieces[:, :NG].transpose(2, 1, 0), ((0, 0), (0, 0), (0, HEAD_DIM - N_BIAS)))
    ca = pieces.reshape(S, NP, 2, HEAD_DIM)[:, :, ::-1].reshape(S, DM).astype(BF16)
    o, lse = staged("fox_fwd", _fox_fwd, q, k, v, ca)
    mkv1 = _mm_small(memb, w["mkv1"], "mem_kv_1")
    x3, x3t, r3, cat1t = _mixer_b_fwd(x2, o, qm, mkv1, w["wo1"], row(ln_g, 2), row(ln_b, 2), "mixer_b_fwd")
    dx4, r4, h1, loss_part = _mlp_fwd_loss(x3, w["wup1"], w["wdown1"], row(ln_g, 3), row(ln_b, 3), target,
                                           "mlp_fwd_1")

    nc = w["wup0"].shape[0]
    (dx3, dr4t, dpre1, dgb11), _ = _mlp_bwd(dx4, r4, h1, w["wup1"], w["wdown1"], row(ln_g, 3), "mlp_bwd_1")
    grads["wdown1"] = _mm_nn(dr4t, h1, "dw_down_1", out_split=nc)
    grads["wup1"] = _mm_nn(x3t, dpre1, "dw_up_1", out_split=nc)
    dr3, dr3b, dcat1, dgb10, dmkv1 = staged("mixer_b_bwd", _mixer_b_bwd, dx3, r3, qm, mkv1, w["wo1"], row(ln_g, 2))
    dqz, dk, dv, dct3 = staged("fox_bwd", _fox_bwd, q, k, v, o, dcat1, ca, lse)
    dct = jnp.pad(dct3.reshape(NG, S), ((0, 16 - NG), (0, 0)))
    dflt, dbf = _forget_cumsum_bwd(dct, flt, bfc, "forget_cumsum_bwd")
    dfl = jnp.pad(dflt.T, ((0, 0), (0, LANES - 16))).astype(BF16)
    dkvf = jnp.concatenate([dk, dv, dfl], axis=1)
    dx2 = _proj_b_bwd(dr3, dqz, dkvf, w["wq"], w["wkv"], "proj_b_bwd")
    grads["wo1"] = _mm_nn(cat1t, dr3b, "dw_o_1")
    grads["wq"] = _mm_nn(x2t, dqz, "dw_q")
    grads["wkv"] = _mm_nn(x2t, dkvf, "dw_kv")
    grads["mkv1"] = _mm_tn(memb, dmkv1.astype(BF16), "dw_mkv_1")
    dx1, dr2t, dpre0, dgb01 = staged("mlp_bwd_0", _mlp_bwd, dx2, r2, h0, w["wup0"], w["wdown0"], row(ln_g, 1))
    grads["wdown0"] = _mm_nn(dr2t, h0, "dw_down_0", out_split=nc)
    grads["wup0"] = _mm_nn(x1t, dpre0, "dw_up_0", out_split=nc)
    dx0, dz, dr1b, dgb00, dsgb, dws, dbst, dmkv0 = staged(
        "mixer_a_bwd", _mixer_a_bwd, z0b, dx1, r1, w["wa"], w["sg"], w["sb"], w["ws"], bst, mkv0, w["wo0"], row(ln_g, 0))
    grads["wo0"] = _mm_nn(cat0t, dr1b, "dw_o_0")
    grads["wa"] = _mm_nn(x0t, dz, "dw_a")
    grads["mkv0"] = _mm_tn(memb, dmkv0.astype(BF16), "dw_mkv_0")
    grads.update({
        "ws": dws, "b_s": dbst[:, :NG].T, "sg": dsgb[0:1], "sb": dsgb[1:2], "bf": dbf[:NG, 0], "loss": loss_part[0:1],
        "ln_g": jnp.concatenate([dgb00[0:1], dgb01[0:1], dgb10[0:1], dgb11[0:1]], axis=0),
        "ln_b": jnp.concatenate([dgb00[1:2], dgb01[1:2], dgb10[1:2], dgb11[1:2]], axis=0),
    })
    return loss_part, dx0, grads


def _place():
    xi, yi, ci = lax.axis_index("x"), lax.axis_index("y"), lax.axis_index("c")
    peers = [(xi, 1 - yi), (1 - xi, yi), (1 - xi, 1 - yi)]
    return xi, yi, ci, peers


def _comm_call(body, name, ins, out_shapes, n_remote, n_local):
    return pl.pallas_call(
        body, name=name, out_shape=out_shapes, in_specs=[ANY] * len(ins), out_specs=[ANY] * len(out_shapes),
        scratch_shapes=[pltpu.SemaphoreType.DMA((n_remote,)), pltpu.SemaphoreType.DMA((n_remote,)),
                        pltpu.SemaphoreType.DMA((max(n_local, 1),))],
        compiler_params=pltpu.CompilerParams(has_side_effects=True),
    )(*ins)


def _pair_swap(xs):
    n = len(xs)
    out_shapes = [jax.ShapeDtypeStruct((a.shape[0],) + a.shape[2:], a.dtype) for a in xs]

    def copies(x_refs, o_refs, sems):
        send_sems, recv_sems = sems
        xi, yi, ci, _ = _place()
        return [pltpu.make_async_remote_copy(src_ref=x_refs[i].at[:, 1 - ci], dst_ref=o_refs[i],
                                             send_sem=send_sems.at[i], recv_sem=recv_sems.at[i],
                                             device_id=(xi, yi, 1 - ci), device_id_type=MESH) for i in range(n)]

    def start(x_refs, o_refs, sems):
        for cp in copies(x_refs, o_refs, sems):
            cp.start()

    def finish(x_refs, o_refs, sems):
        cps = copies(x_refs, o_refs, sems)
        for cp in cps:
            cp.wait_recv()
        for cp in cps:
            cp.wait_send()

    return _Comm(xs, out_shapes, {}, [n, n], start, finish)


def _chip_exchange(xs):
    n = len(xs)
    out_shapes = [jax.ShapeDtypeStruct((3,) + a.shape[1:], a.dtype) for a in xs]

    def copies(x_refs, o_refs, sems):
        send_sems, recv_sems = sems
        xi, yi, ci, peers = _place()
        return [pltpu.make_async_remote_copy(src_ref=x_refs[i].at[2 * px + py], dst_ref=o_refs[i].at[d],
                                             send_sem=send_sems.at[3 * i + d], recv_sem=recv_sems.at[3 * i + d],
                                             device_id=(px, py, ci), device_id_type=MESH)
                for i in range(n) for d, (px, py) in enumerate(peers)]

    def start(x_refs, o_refs, sems):
        for cp in copies(x_refs, o_refs, sems):
            cp.start()

    def finish(x_refs, o_refs, sems):
        cps = copies(x_refs, o_refs, sems)
        for cp in cps:
            cp.wait_recv()
        for cp in cps:
            cp.wait_send()

    return _Comm(xs, out_shapes, {}, [3 * n, 3 * n], start, finish)


def _inplace_call(body, name, bufs, sem_counts):
    n = len(bufs)
    return pl.pallas_call(
        body, name=name, out_shape=[jax.ShapeDtypeStruct(a.shape, a.dtype) for a in bufs],
        in_specs=[ANY] * n, out_specs=[ANY] * n, input_output_aliases={i: i for i in range(n)},
        scratch_shapes=[pltpu.SemaphoreType.DMA((k,)) for k in sem_counts],
        compiler_params=pltpu.CompilerParams(has_side_effects=True),
    )(*bufs)


def _gather_weights(bufs, whole=()):
    n, nw = len(bufs), len(whole)

    def copy(ref, s, r, k, dev):
        return pltpu.make_async_remote_copy(src_ref=ref, dst_ref=ref, send_sem=s.at[k], recv_sem=r.at[k],
                                            device_id=dev, device_id_type=MESH)

    def plan(w, sems):
        ici_s, ici_r, d2d_s, d2d_r = sems
        xi, yi, ci, peers = _place()
        xn, yn, sib = (1 - xi, yi, ci), (xi, 1 - yi, ci), (xi, yi, 1 - ci)
        j, jx, jy, jd = 2 * xi + yi, 2 * (1 - xi) + yi, 2 * xi + 1 - yi, 2 * (1 - xi) + 1 - yi
        c = {}
        for i in range(n):
            hh = bufs[i].shape[2] // 2
            lo, hi = pl.ds(0, hh), pl.ds(hh, hh)
            at = lambda slot, half, rows=None, i=i: w[i].at[slot, half] if rows is None else w[i].at[slot, half, rows]
            c["to_x", i] = (copy(at(j, ci), ici_s, ici_r, 4 * i, xn), copy(at(jx, ci), ici_s, ici_r, 4 * i, xn))
            c["to_y", i] = (copy(at(j, ci), ici_s, ici_r, 4 * i + 1, yn), copy(at(jy, ci), ici_s, ici_r, 4 * i + 1, yn))
            c["x_on", i] = (copy(at(jx, ci, lo), ici_s, ici_r, 4 * i + 2, yn), copy(at(jd, ci, lo), ici_s, ici_r, 4 * i + 2, yn))
            c["y_on", i] = (copy(at(jy, ci, hi), ici_s, ici_r, 4 * i + 3, xn), copy(at(jd, ci, hi), ici_s, ici_r, 4 * i + 3, xn))
            for k, slot in enumerate((jx, jy, jd)):
                c["sib", i, k] = (copy(at(slot, ci), d2d_s, d2d_r, 3 * i + k, sib),
                                  copy(at(slot, 1 - ci), d2d_s, d2d_r, 3 * i + k, sib))
        for i in range(nw):
            for d, (px, py) in enumerate(peers):
                k = 4 * n + 3 * i + d
                c["whole", i, d] = (copy(w[n + i].at[j], ici_s, ici_r, k, (px, py, ci)),
                                    copy(w[n + i].at[2 * px + py], ici_s, ici_r, k, (px, py, ci)))
        return c

    def start(_, w, sems):
        c = plan(w, sems)
        for i in range(n):
            c["to_x", i][0].start()
            c["to_y", i][0].start()
        for i in range(nw):
            for d in range(3):
                c["whole", i, d][0].start()

    def middle(_, w, sems):
        c = plan(w, sems)
        for i in range(n):
            c["to_x", i][1].wait_recv()
            c["x_on", i][0].start()
            c["sib", i, 0][0].start()
            c["to_y", i][1].wait_recv()
            c["y_on", i][0].start()
            c["sib", i, 1][0].start()

    def finish(_, w, sems):
        c = plan(w, sems)
        for i in range(n):
            c["x_on", i][1].wait_recv()
            c["y_on", i][1].wait_recv()
            c["sib", i, 2][0].start()
        for i in range(n):
            for k in range(3):
                c["sib", i, k][1].wait_recv()
        for i in range(nw):
            for d in range(3):
                c["whole", i, d][1].wait_recv()
        for key, (sent, _) in c.items():
            sent.wait_send()

    arrays = list(bufs) + list(whole)
    k = 4 * n + 3 * nw
    return _Comm(arrays, [jax.ShapeDtypeStruct(a.shape, a.dtype) for a in arrays], {i: i for i in range(n + nw)},
                 [k, k, max(3 * n, 1), max(3 * n, 1)], start, finish, middle)


def _pair_gather(bufs, spread, name):
    n = len(bufs)
    where = [(i, l) for i, a in enumerate(bufs) for l in range(a.shape[0])]
    nw = len(where)
    arrays = list(bufs) + ([] if spread is None else [spread])
    na = len(arrays)

    def body(*refs):
        g = refs[na:na + n]
        sp = refs[na + n] if spread is not None else None
        send_sems, recv_sems = refs[2 * na:]
        xi, yi, ci, peers = _place()
        chips = [(xi, yi)] + peers
        others = [(d, f) for d in range(4) for f in (0, 1) if (d, f) != (0, 0)] if spread is not None else []
        sends = []
        for k, (i, l) in enumerate(where):
            mine = g[i].at[l, :, ci]
            cp = pltpu.make_async_remote_copy(src_ref=mine, dst_ref=mine, send_sem=send_sems.at[k],
                                              recv_sem=recv_sems.at[k], device_id=(xi, yi, 1 - ci), device_id_type=MESH)
            cp.start()
            sends.append(cp)
        for k, (d, f) in enumerate(others):
            mine = sp.at[2 * xi + yi, ci]
            cp = pltpu.make_async_remote_copy(src_ref=mine, dst_ref=mine, send_sem=send_sems.at[nw + k],
                                              recv_sem=recv_sems.at[nw + k],
                                              device_id=(chips[d][0], chips[d][1], ci if f == 0 else 1 - ci),
                                              device_id_type=MESH)
            cp.start()
            sends.append(cp)
        for k, (i, l) in enumerate(where):
            other = g[i].at[l, :, 1 - ci]
            pltpu.make_async_remote_copy(src_ref=other, dst_ref=other, send_sem=send_sems.at[k],
                                         recv_sem=recv_sems.at[k], device_id=(xi, yi, 1 - ci),
                                         device_id_type=MESH).wait_recv()
        for k, (d, f) in enumerate(others):
            px, py, pc = chips[d][0], chips[d][1], (ci if f == 0 else 1 - ci)
            theirs = sp.at[2 * px + py, pc]
            pltpu.make_async_remote_copy(src_ref=theirs, dst_ref=theirs, send_sem=send_sems.at[nw + k],
                                         recv_sem=recv_sems.at[nw + k], device_id=(px, py, pc),
                                         device_id_type=MESH).wait_recv()
        for cp in sends:
            cp.wait_send()

    n_sems = nw + (7 if spread is not None else 0)
    out = _inplace_call(body, name, arrays, [n_sems, n_sems])
    return out[:n], (out[n] if spread is not None else None)


def _row_block(rows, cols):
    want = max(8, (2 ** 18 // max(cols, 1)) // 8 * 8)
    if rows <= want:
        return rows
    best = 8
    for t in range(8, want + 1, 8):
        if rows % t == 0:
            best = t
    return best


def _pair_add(g4, recv, sel, out_dtype, name):
    A, _, H, C = g4.shape
    bh = _row_block(H, C)

    def body(sel_ref, g_ref, r_ref, o_ref):
        o_ref[...] = (g_ref[...].astype(F32) + r_ref[...].astype(F32)).astype(out_dtype)

    return pl.pallas_call(
        body, name=name, out_shape=jax.ShapeDtypeStruct((A, H, C), out_dtype),
        grid_spec=pltpu.PrefetchScalarGridSpec(
            num_scalar_prefetch=1, grid=(A, H // bh),
            in_specs=[pl.BlockSpec((None, None, bh, C), lambda a, i, s: (a, s[1], i, 0)),
                      pl.BlockSpec((None, bh, C), lambda a, i, s: (a, i, 0))],
            out_specs=pl.BlockSpec((None, bh, C), lambda a, i, s: (a, i, 0))),
        compiler_params=pltpu.CompilerParams(dimension_semantics=("arbitrary", "arbitrary"),
                                             vmem_limit_bytes=VMEM_LIMIT),
    )(sel, g4, recv)


def _chip_reduce(g4, recv1, recv2, sel, buf, layer, n_layers, name, by_chip=False):
    _, _, H, C = g4.shape
    bh = _row_block(H, C)
    if by_chip:
        out_shape = jax.ShapeDtypeStruct((N_CHIPS, 2, H, C), F32)
        out_spec = pl.BlockSpec((None, None, bh, C), lambda i, s: (s[0], s[1], i, 0))
    else:
        out_shape = jax.ShapeDtypeStruct((n_layers, 1, 2, H, C), F32)
        out_spec = pl.BlockSpec((None, None, None, bh, C), lambda i, s: (layer, 0, s[1], i, 0))

    def body(sel_ref, g_ref, r1_ref, r2_ref, *rest):
        acc = g_ref[...].astype(F32) + r1_ref[...].astype(F32)
        for d in range(3):
            acc = acc + r2_ref[d].astype(F32)
        rest[-1][...] = acc

    in_specs = [pl.BlockSpec((None, None, bh, C), lambda i, s: (s[0], s[1], i, 0)),
                pl.BlockSpec((None, bh, C), lambda i, s: (s[0], i, 0)),
                pl.BlockSpec((3, bh, C), lambda i, s: (0, i, 0))]
    args = [sel, g4, recv1, recv2]
    aliases = {}
    if buf is not None:
        in_specs.append(ANY)
        args.append(buf)
        aliases = {4: 0}
    return pl.pallas_call(
        body, name=name, out_shape=out_shape,
        grid_spec=pltpu.PrefetchScalarGridSpec(num_scalar_prefetch=1, grid=(H // bh,), in_specs=in_specs,
                                               out_specs=out_spec),
        input_output_aliases=aliases,
        compiler_params=pltpu.CompilerParams(dimension_semantics=("arbitrary",), vmem_limit_bytes=VMEM_LIMIT),
    )(*args)


def _cast_place(shards, layer, sel, name):
    _, _, H, C = shards.shape
    bh = _row_block(H, C)

    def body(sel_ref, x_ref, o_ref):
        o_ref[...] = x_ref[...].astype(BF16)

    return pl.pallas_call(
        body, name=name, out_shape=jax.ShapeDtypeStruct((N_CHIPS, 2, H, C), BF16),
        grid_spec=pltpu.PrefetchScalarGridSpec(
            num_scalar_prefetch=1, grid=(2, H // bh),
            in_specs=[pl.BlockSpec((None, None, bh, C), lambda h, i, s: (layer, h, i, 0))],
            out_specs=pl.BlockSpec((None, None, bh, C), lambda h, i, s: (s[0], h, i, 0))),
        compiler_params=pltpu.CompilerParams(dimension_semantics=("arbitrary", "arbitrary"),
                                             vmem_limit_bytes=VMEM_LIMIT),
    )(sel, shards)


def _adamw_many(quads, name, comm=None):
    R, C = quads[0][0].shape
    br = _row_block(R, C)
    n = len(quads)

    def body(*refs):
        for k in range(n):
            g_ref, w_ref, m_ref, v_ref = refs[4 * k:4 * k + 4]
            d_ref, mo_ref, vo_ref = refs[4 * n + 3 * k:4 * n + 3 * k + 3]
            gv = g_ref[...]
            mn = ADAM_B1 * m_ref[...] + (1.0 - ADAM_B1) * gv
            vn = ADAM_B2 * v_ref[...] + (1.0 - ADAM_B2) * (gv * gv)
            m_hat = mn / (1.0 - ADAM_B1 ** ADAM_STEP)
            v_hat = vn / (1.0 - ADAM_B2 ** ADAM_STEP)
            d_ref[...] = -ADAM_LR * (m_hat / (jnp.sqrt(v_hat) + ADAM_EPS) + ADAM_WD * w_ref[...])
            mo_ref[...] = mn
            vo_ref[...] = vn

    spec = pl.BlockSpec((br, C), lambda i: (i, 0))
    res, got = _pcall(body, name=name, grid=(R // br,), comm=comm, out_shape=(jax.ShapeDtypeStruct((R, C), F32),) * (3 * n),
                      in_specs=[spec] * (4 * n), out_specs=(spec,) * (3 * n), args=[a for q in quads for a in q])
    return [res[3 * k:3 * k + 3] for k in range(n)], got


def _adamw(g, w, m, v, name):
    return _adamw_many([(g, w, m, v)], name)[0][0]


def _part_rows(shape):
    return -(-math.prod(shape) // (8 * LANES)) * 8


def _pack_rows(arrs, total_rows):
    parts = []
    for a in arrs:
        flat = a.reshape(-1)
        rows = _part_rows(a.shape)
        parts.append(jnp.pad(flat, (0, rows * LANES - flat.shape[0])).reshape(rows, LANES))
    packed = jnp.concatenate(parts, axis=0)
    return jnp.pad(packed, ((0, total_rows - packed.shape[0]), (0, 0)))


def _unpack_rows(packed, shapes):
    out, r = [], 0
    for shp in shapes:
        size, rows = math.prod(shp), _part_rows(shp)
        out.append(packed[r:r + rows].reshape(-1)[:size].reshape(shp))
        r += rows
    return out


def _rows_of(shapes):
    return sum(_part_rows(s) for s in shapes)


_GATHER_STAGES = {"start": ["wa", "mkv0", "wo0"], "mixer_a_fwd": ["wup0", "wdown0"],
                  "mlp_fwd_0": ["wq", "wkv", "mkv1", "wo1"], "fox_fwd": ["wup1", "wdown1"]}
_SWAP_STAGES = {"mixer_b_bwd": ["wup1", "wdown1"]}
_REDUCE_STAGES = {"fox_bwd": ["wup1", "wdown1"], "mlp_bwd_0": ["wo1", "wq", "wkv", "mkv1"],
                  "mixer_a_bwd": ["wup0", "wdown0"], "end": ["wa", "wo0", "mkv0", "small"]}
_SMALL = ["ws", "b_s", "sg", "sb", "bf", "ln_g", "ln_b", "loss"]


class _Staged:
    def __init__(self, placed, sel, dm, ng, shard_cols, n_small):
        self.placed, self.sel, self.dm, self.ng, self.shard_cols, self.n_small = placed, sel, dm, ng, shard_cols, n_small
        self.g4, self.recv1, self.recv2 = {}, {}, {}

    def _weight(self, name, a):
        a = a.reshape(N_CHIPS, 2 * a.shape[2], a.shape[3])
        if name in ("wa", "wkv"):
            a = a.transpose(1, 0, 2).reshape(a.shape[1], -1)
            if name == "wkv":
                a = jnp.concatenate([a[:, :2 * self.dm],
                                     jnp.pad(a[:, 2 * self.dm:], ((0, 0), (0, LANES - self.ng)))], axis=1)
            return a
        if name.startswith("wup") or name.startswith("wdown"):
            return a
        return a.reshape(-1, a.shape[2])

    def _partial(self, name, grads):
        if name == "small":
            a = _pack_rows([grads[n] for n in _SMALL], self.n_small).reshape(N_CHIPS, -1, LANES)
        elif name in self.shard_cols:
            n = self.shard_cols[name]
            g = grads[name][:, :N_CHIPS * n]
            a = g.reshape(g.shape[0], N_CHIPS, n).transpose(1, 0, 2)
        elif name.startswith("wup") or name.startswith("wdown"):
            a = grads[name]
        else:
            g = grads[name]
            a = g.reshape(N_CHIPS, g.shape[0] // N_CHIPS, g.shape[1])
        return a.reshape(N_CHIPS, 2, a.shape[1] // 2, a.shape[2])

    def plan(self, stage, w, grads):
        if stage in _GATHER_STAGES:
            return _gather_weights([self.placed[n] for n in _GATHER_STAGES[stage]])
        if stage in _SWAP_STAGES:
            for n in _SWAP_STAGES[stage]:
                self.g4[n] = self._partial(n, grads)
            return _pair_swap([self.g4[n] for n in _SWAP_STAGES[stage]])
        if stage in _REDUCE_STAGES:
            names = _REDUCE_STAGES[stage]
            late = [n for n in names if n not in self.recv1]
            for n in late:
                self.g4[n] = self._partial(n, grads)
            if late:
                got = _run_comm(_pair_swap([self.g4[n] for n in late]), "reduce_pair_swap_" + stage)
                self.recv1.update(zip(late, got))
            return _chip_exchange([_pair_add(self.g4[n], self.recv1[n], self.sel, F32 if n == "small" else BF16,
                                             "reduce_pair_add_" + n) for n in names])
        return None

    def done(self, stage, results, w):
        if stage in _GATHER_STAGES:
            for n, a in zip(_GATHER_STAGES[stage], results):
                w[n] = self._weight(n, a)
        elif stage in _SWAP_STAGES:
            self.recv1.update(zip(_SWAP_STAGES[stage], results))
        elif stage in _REDUCE_STAGES:
            self.recv2.update(zip(_REDUCE_STAGES[stage], results))


def kernel(x, mem, a_w_in, a_sgu_ln_g, a_sgu_ln_b, a_w_s, a_b_s, kv_w, kv_b_f, b_w_q, mem_w_kv, w_o, ln_g, ln_b, w_up, w_down, loss_target, m_a_w_in, m_a_sgu_ln_g, m_a_sgu_ln_b, m_a_w_s, m_a_b_s, m_kv_w, m_kv_b_f, m_b_w_q, m_mem_w_kv, m_w_o, m_ln_g, m_ln_b, m_w_up, m_w_down, v_a_w_in, v_a_sgu_ln_g, v_a_sgu_ln_b, v_a_w_s, v_a_b_s, v_kv_w, v_kv_b_f, v_b_w_q, v_mem_w_kv, v_w_o, v_ln_g, v_ln_b, v_w_up, v_w_down):
    xi, yi, ci = lax.axis_index("x"), lax.axis_index("y"), lax.axis_index("c")
    chip = 2 * xi + yi
    sel = jnp.stack([chip, ci]).astype(jnp.int32)
    S, D = x.shape[1], x.shape[2]
    DQ = mem_w_kv.shape[2] // 2
    DM = D - DQ
    NG = DM // HEAD_DIM

    shards = {"wa": (a_w_in, 0), "wkv": (kv_w[None], 0), "wq": (b_w_q, 0), "mkv0": (mem_w_kv, 0), "mkv1": (mem_w_kv, 1),
              "wo0": (w_o, 0), "wo1": (w_o, 1), "wup0": (w_up, 0), "wup1": (w_up, 1), "wdown0": (w_down, 0),
              "wdown1": (w_down, 1)}
    placed = {n: _cast_place(s.reshape(s.shape[0], 2, s.shape[1] // 2, s.shape[2]), layer, sel, "cast_place_" + n)
              for n, (s, layer) in shards.items()}
    ln_pack = jnp.concatenate([ln_g.reshape(4, -1), ln_b.reshape(4, -1)], axis=0)
    ln_buf = lax.dynamic_update_slice(jnp.zeros((N_CHIPS,) + ln_pack.shape, F32), ln_pack[None], (chip, 0, 0))
    ln_shape = (4, D)
    small_shapes = [a_w_s.shape, a_b_s.shape, a_sgu_ln_g.shape, a_sgu_ln_b.shape, kv_b_f.shape, ln_shape, ln_shape,
                    (1, LANES)]
    n_small = -(-_rows_of(small_shapes) // 64) * 64
    hooks = _Staged(placed, sel, DM, NG, {"wa": a_w_in.shape[2], "wkv": kv_w.shape[1]}, n_small)
    weights = {"sg": a_sgu_ln_g, "sb": a_sgu_ln_b, "ws": a_w_s[0], "b_s": a_b_s[0], "bf": kv_b_f}
    got = _run_comm(_gather_weights([placed[n] for n in _GATHER_STAGES["start"]], whole=[ln_buf]),
                    "gather_weights_first")
    hooks.done("start", got[:-1], weights)
    ln_full = got[-1].transpose(1, 0, 2).reshape(8, D)
    weights["ln_g"], weights["ln_b"] = ln_full[:4], ln_full[4:]

    _, grad_x, gr = _local_step(x[0], mem[0], loss_target[0], weights, hooks)

    def chip_sums(groups):
        bufs = []
        for grp in groups:
            buf = None
            for layer, n in enumerate(grp):
                buf = _chip_reduce(hooks.g4[n], hooks.recv1[n], hooks.recv2[n], sel, buf, layer, len(grp),
                                   "reduce_chip_sum_" + n)
            bufs.append(buf)
        return bufs

    whole = lambda a: a.reshape(a.shape[0], 2 * a.shape[3], a.shape[4])
    flat = lambda a: a.reshape(-1, a.shape[-1])

    def update(name, g, w, m, v):
        d, mn, vn = _adamw(flat(g), flat(w), flat(m), flat(v), "adamw_" + name)
        return d.reshape(w.shape), mn.reshape(w.shape), vn.reshape(w.shape)

    hooks.done("end", _run_comm(hooks.plan("end", weights, gr), "reduce_chip_exchange_end"), weights)
    small_mine = _chip_reduce(hooks.g4["small"], hooks.recv1["small"], hooks.recv2["small"], sel, None, 0, 1,
                              "reduce_chip_sum_small", by_chip=True)
    red, small_all = _pair_gather(chip_sums([["wa"], ["wkv"], ["wq"], ["mkv0", "mkv1"], ["wo0", "wo1"],
                                             ["wup0", "wup1"], ["wdown0", "wdown1"]]), small_mine, "reduce_pair_gather")
    g_a_w_in, g_kv_w, g_b_w_q, g_mem_w_kv, g_w_o, g_w_up, g_w_down = [whole(a) for a in red]
    g_kv_w = g_kv_w[0]
    g_w_down = g_w_down.transpose(0, 2, 1)
    small_all = small_all.reshape(n_small, LANES)
    g_ws, g_bs, g_sg, g_sb, g_bf, g_lng, g_lnb, loss_row = _unpack_rows(small_all, small_shapes)
    loss = loss_row[0, 0]
    dsh = D // N_CHIPS
    g_ln_g = lax.dynamic_slice_in_dim(g_lng, chip * dsh, dsh, axis=1).reshape(ln_g.shape)
    g_ln_b = lax.dynamic_slice_in_dim(g_lnb, chip * dsh, dsh, axis=1).reshape(ln_b.shape)

    upd = {
        "a_w_in": update("a_w_in", g_a_w_in, a_w_in, m_a_w_in, v_a_w_in),
        "kv_w": update("kv_w", g_kv_w, kv_w, m_kv_w, v_kv_w),
        "b_w_q": update("b_w_q", g_b_w_q, b_w_q, m_b_w_q, v_b_w_q),
        "mem_w_kv": update("mem_w_kv", g_mem_w_kv, mem_w_kv, m_mem_w_kv, v_mem_w_kv),
        "w_o": update("w_o", g_w_o, w_o, m_w_o, v_w_o),
        "w_up": update("w_up", g_w_up, w_up, m_w_up, v_w_up),
        "w_down": update("w_down", g_w_down, w_down, m_w_down, v_w_down),
    }
    tiny_g = [g_sg, g_sb, g_ws, g_bs, g_bf, g_ln_g, g_ln_b]
    tiny_w = [a_sgu_ln_g, a_sgu_ln_b, a_w_s, a_b_s, kv_b_f, ln_g, ln_b]
    tiny_m = [m_a_sgu_ln_g, m_a_sgu_ln_b, m_a_w_s, m_a_b_s, m_kv_b_f, m_ln_g, m_ln_b]
    tiny_v = [v_a_sgu_ln_g, v_a_sgu_ln_b, v_a_w_s, v_a_b_s, v_kv_b_f, v_ln_g, v_ln_b]
    tiny_shapes = [a.shape for a in tiny_w]
    n_tiny = -(-_rows_of(tiny_shapes) // 8) * 8
    td, tm_, tv = _adamw(_pack_rows(tiny_g, n_tiny), _pack_rows(tiny_w, n_tiny), _pack_rows(tiny_m, n_tiny),
                         _pack_rows(tiny_v, n_tiny), "adamw_small")
    for name, d, mn, vn in zip(["a_sgu_ln_g", "a_sgu_ln_b", "a_w_s", "a_b_s", "kv_b_f", "ln_g", "ln_b"],
                               _unpack_rows(td, tiny_shapes), _unpack_rows(tm_, tiny_shapes),
                               _unpack_rows(tv, tiny_shapes)):
        upd[name] = (d, mn, vn)

    order = ["a_w_in", "a_sgu_ln_g", "a_sgu_ln_b", "a_w_s", "a_b_s", "kv_w", "kv_b_f", "b_w_q", "mem_w_kv", "w_o",
             "ln_g", "ln_b", "w_up", "w_down"]
    grads = {"a_w_in": g_a_w_in.reshape(a_w_in.shape), "a_sgu_ln_g": g_sg.reshape(a_sgu_ln_g.shape),
             "a_sgu_ln_b": g_sb.reshape(a_sgu_ln_b.shape), "a_w_s": g_ws.reshape(a_w_s.shape),
             "a_b_s": g_bs.reshape(a_b_s.shape), "kv_w": g_kv_w, "kv_b_f": g_bf, "b_w_q": g_b_w_q.reshape(b_w_q.shape),
             "mem_w_kv": g_mem_w_kv, "w_o": g_w_o, "ln_g": g_ln_g, "ln_b": g_ln_b, "w_up": g_w_up, "w_down": g_w_down}
    return (loss, grad_x[None], *[grads[n] for n in order], *[upd[n][0] for n in order],
            *[upd[n][1] for n in order], *[upd[n][2] for n in order])
```

```python
import functools
import math

import jax
import jax.numpy as jnp
from jax import lax
from jax.experimental import pallas as pl
from jax.experimental.pallas import tpu as pltpu

F32 = jnp.float32
BF16 = jnp.bfloat16

HEAD_DIM = 64
PAIR = 2 * HEAD_DIM
CHUNK = 128
LN_EPS = 1e-5
ALPHA = 4 ** 0.25
QK_SCALE = 1.0 / math.sqrt(HEAD_DIM)
NEG = -1e30
N_CHIPS = 4
FOX_BLOCK = 256
LANES = 128
VMEM_LIMIT = 48 * 2 ** 20

ADAM_LR, ADAM_B1, ADAM_B2, ADAM_EPS, ADAM_WD, ADAM_STEP = 0.001, 0.9, 0.999, 1e-08, 0.01, 10

MESH = pl.DeviceIdType.MESH
ANY = pl.BlockSpec(memory_space=pl.ANY)


def _dot(a, b):
    return jnp.dot(a, b, preferred_element_type=F32)


def _dot_nt(a, b):
    return lax.dot_general(a, b, (((1,), (1,)), ((), ())), preferred_element_type=F32)


def _dot_tn(a, b):
    return lax.dot_general(a, b, (((0,), (0,)), ((), ())), preferred_element_type=F32)


def _ln_stats(r):
    mu = jnp.mean(r, axis=-1, keepdims=True)
    d = r - mu
    var = jnp.mean(d * d, axis=-1, keepdims=True)
    rstd = lax.rsqrt(var + LN_EPS)
    return d * rstd, rstd


def _ln_bwd(dy, xhat, rstd, g):
    dxh = dy * g
    m1 = jnp.mean(dxh, axis=-1, keepdims=True)
    m2 = jnp.mean(dxh * xhat, axis=-1, keepdims=True)
    return rstd * (dxh - m1 - xhat * m2)


_GELU_K = math.sqrt(2.0 / math.pi)


def _gelu(x):
    return 0.5 * x * (1.0 + jnp.tanh(_GELU_K * (x + 0.044715 * x * x * x)))


def _gelu_grad(x):
    t = jnp.tanh(_GELU_K * (x + 0.044715 * x * x * x))
    return 0.5 * (1.0 + t) + 0.5 * x * (1.0 - t * t) * _GELU_K * (1.0 + 3 * 0.044715 * x * x)


def _lane_mask(e):
    lane = lax.broadcasted_iota(jnp.int32, (1, PAIR), 1)
    return (lane >= HEAD_DIM * e) & (lane < HEAD_DIM * (e + 1))


def _first_head():
    return lax.broadcasted_iota(jnp.int32, (1, PAIR), 1) < HEAD_DIM


def _mem_probs(qh, mkp):
    sc = _dot_nt(qh, mkp) * QK_SCALE
    ex = jnp.exp(sc - jnp.max(sc, axis=-1, keepdims=True))
    return ex / jnp.sum(ex, axis=-1, keepdims=True)


def _mem_attn_fwd(qm_b, mkv_ref, dq_dim):
    outs = []
    for mp in range(dq_dim // PAIR):
        qp = qm_b[:, mp * PAIR:(mp + 1) * PAIR]
        mkp = mkv_ref[:, mp * PAIR:(mp + 1) * PAIR]
        mvp = mkv_ref[:, dq_dim + mp * PAIR:dq_dim + (mp + 1) * PAIR]
        heads = []
        for e in (0, 1):
            qh = jnp.where(_lane_mask(e), qp, jnp.zeros_like(qp))
            p = _mem_probs(qh, mkp)
            heads.append(_dot(p.astype(BF16), mvp))
        outs.append(jnp.where(_first_head(), heads[0], heads[1]))
    return outs


def _mem_attn_bwd(qm_b, mkv_ref, dmo_pairs, dmkv_ref, dq_dim):
    dqs = []
    for mp in range(dq_dim // PAIR):
        ks = slice(mp * PAIR, (mp + 1) * PAIR)
        vs = slice(dq_dim + mp * PAIR, dq_dim + (mp + 1) * PAIR)
        qp = qm_b[:, ks]
        mkp = mkv_ref[:, ks]
        mvp = mkv_ref[:, vs]
        dmo_b = dmo_pairs[mp].astype(BF16)
        dq = None
        dmk = None
        dmv = None
        for e in (0, 1):
            hm = _lane_mask(e)
            qh = jnp.where(hm, qp, jnp.zeros_like(qp))
            p = _mem_probs(qh, mkp)
            doh = jnp.where(hm, dmo_b, jnp.zeros_like(dmo_b))
            dp = _dot_nt(doh, mvp)
            ds = p * (dp - jnp.sum(dp * p, axis=-1, keepdims=True))
            dsb = (ds * QK_SCALE).astype(BF16)
            kh = jnp.where(hm, mkp, jnp.zeros_like(mkp))
            dq_e = _dot(dsb, kh)
            dmk_e = _dot_tn(dsb, qh)
            dmv_e = _dot_tn(p.astype(BF16), doh)
            dq = dq_e if dq is None else dq + dq_e
            dmk = dmk_e if dmk is None else dmk + dmk_e
            dmv = dmv_e if dmv is None else dmv + dmv_e
        dmkv_ref[:, ks] += dmk
        dmkv_ref[:, vs] += dmv
        dqs.append(dq)
    return dqs


def _params(n_axes=1):
    return pltpu.CompilerParams(dimension_semantics=("arbitrary",) * n_axes, vmem_limit_bytes=VMEM_LIMIT)


def _full(shape):
    return pl.BlockSpec(shape, lambda *_: (0,) * len(shape))


def _rows(tm, cols):
    return pl.BlockSpec((tm, cols), lambda i: (i, 0))


def _cols(rows, tm):
    return pl.BlockSpec((rows, tm), lambda i: (0, i))


def _pick(n, pref):
    if n <= pref:
        return n
    best = LANES
    for t in range(LANES, pref + 1, LANES):
        if n % t == 0:
            best = t
    return best if 2 * best >= pref or n > 2 * pref else n


class _Comm:
    def __init__(self, ins, out_shapes, aliases, sem_counts, start, finish, middle=None):
        self.ins, self.out_shapes, self.aliases, self.sem_counts = list(ins), list(out_shapes), dict(aliases), sem_counts
        self.start, self.finish = start, finish
        self.middle = middle


def _pcall(body, *, name, grid, in_specs, out_specs, out_shape, args, scratch_shapes=(), comm=None, aliases=None):
    n_in, n_out, n_scr = len(in_specs), len(out_shape), len(scratch_shapes)
    aliases = dict(aliases or {})
    if comm is None:
        res = pl.pallas_call(body, name=name, grid=grid, out_shape=tuple(out_shape), in_specs=list(in_specs),
                             out_specs=tuple(out_specs), scratch_shapes=list(scratch_shapes),
                             input_output_aliases=aliases, compiler_params=_params())(*args)
        return tuple(res), ()
    nci, nco = len(comm.ins), len(comm.out_shapes)
    last = grid[0] - 1

    def wrapped(*refs):
        ins, refs = refs[:n_in], refs[n_in:]
        cins, refs = refs[:nci], refs[nci:]
        outs, refs = refs[:n_out], refs[n_out:]
        couts, refs = refs[:nco], refs[nco:]
        scr, sems = refs[:n_scr], refs[n_scr:]

        @pl.when(pl.program_id(0) == 0)
        def _():
            comm.start(cins, couts, sems)

        if comm.middle is not None:
            @pl.when(pl.program_id(0) == (last + 1) // 2)
            def _():
                comm.middle(cins, couts, sems)

        body(*ins, *outs, *scr)

        @pl.when(pl.program_id(0) == last)
        def _():
            comm.finish(cins, couts, sems)

    res = pl.pallas_call(
        wrapped, name=name, grid=grid, out_shape=tuple(out_shape) + tuple(comm.out_shapes),
        in_specs=list(in_specs) + [ANY] * nci, out_specs=tuple(out_specs) + (ANY,) * nco,
        input_output_aliases={**aliases, **{n_in + a: n_out + b for a, b in comm.aliases.items()}},
        scratch_shapes=list(scratch_shapes) + [pltpu.SemaphoreType.DMA((k,)) for k in comm.sem_counts],
        compiler_params=pltpu.CompilerParams(dimension_semantics=("arbitrary",), vmem_limit_bytes=VMEM_LIMIT,
                                             has_side_effects=True),
    )(*args, *comm.ins)
    return tuple(res[:n_out]), tuple(res[n_out:])


def _run_comm(comm, name):
    nci, nco = len(comm.ins), len(comm.out_shapes)

    def body(*refs):
        cins, couts, sems = refs[:nci], refs[nci:nci + nco], refs[nci + nco:]
        comm.start(cins, couts, sems)
        if comm.middle is not None:
            comm.middle(cins, couts, sems)
        comm.finish(cins, couts, sems)

    return pl.pallas_call(
        body, name=name, out_shape=tuple(comm.out_shapes), in_specs=[ANY] * nci, out_specs=(ANY,) * nco,
        input_output_aliases=comm.aliases, scratch_shapes=[pltpu.SemaphoreType.DMA((k,)) for k in comm.sem_counts],
        compiler_params=pltpu.CompilerParams(has_side_effects=True),
    )(*comm.ins)


def _mm(a, b, name, a_is_transposed, out_split=1):
    (M, R) = a.shape if a_is_transposed else a.shape[::-1]
    _, N = b.shape
    bm, br = _pick(M, 1024 if a_is_transposed else 512), _pick(R, 1024 if a_is_transposed else 512)
    ncol = N // out_split
    bn = _pick(ncol, 1024)
    per = ncol // bn
    last = R // br - 1

    def body(a_ref, b_ref, o_ref, acc_ref):
        @pl.when(pl.program_id(2) == 0)
        def _():
            acc_ref[...] = jnp.zeros_like(acc_ref)

        acc_ref[...] += (_dot if a_is_transposed else _dot_tn)(a_ref[...], b_ref[...])

        @pl.when(pl.program_id(2) == last)
        def _():
            o_ref[...] = acc_ref[...].astype(BF16)

    if out_split == 1:
        out_shape = jax.ShapeDtypeStruct((M, N), BF16)
        out_spec = pl.BlockSpec((bm, bn), lambda i, j, r: (i, j))
    else:
        out_shape = jax.ShapeDtypeStruct((out_split, M, ncol), BF16)
        out_spec = pl.BlockSpec((None, bm, bn), lambda i, j, r: (j // per, i, j % per))
    a_spec = (pl.BlockSpec((bm, br), lambda i, j, r: (i, r)) if a_is_transposed
              else pl.BlockSpec((br, bm), lambda i, j, r: (r, i)))
    return pl.pallas_call(
        body, name=name, out_shape=out_shape, grid=(M // bm, N // bn, R // br),
        in_specs=[a_spec, pl.BlockSpec((br, bn), lambda i, j, r: (r, j))], out_specs=out_spec,
        scratch_shapes=[pltpu.VMEM((bm, bn), F32)],
        compiler_params=pltpu.CompilerParams(dimension_semantics=("parallel", "parallel", "arbitrary"),
                                             vmem_limit_bytes=VMEM_LIMIT),
    )(a, b)


def _mm_tn(a, b, name, out_split=1):
    return _mm(a, b, name, False, out_split)


def _mm_nn(at, b, name, out_split=1):
    return _mm(at, b, name, True, out_split)


def _mm_small(a, b, name):
    def body(a_ref, b_ref, o_ref):
        o_ref[...] = _dot(a_ref[...], b_ref[...]).astype(BF16)

    return pl.pallas_call(body, name=name, out_shape=jax.ShapeDtypeStruct((a.shape[0], b.shape[1]), BF16),
                          compiler_params=pltpu.CompilerParams(vmem_limit_bytes=VMEM_LIMIT))(a, b)


def _mlp_out(x_ref, wup_ref, wdown_ref, g_ref, b_ref, r_ref, h_ref):
    NC, _, FC = wup_ref.shape
    xv = x_ref[...]
    xb = xv.astype(BF16)
    y = jnp.zeros(xv.shape, F32)
    for c in range(NC):
        a = jnp.maximum(_dot(xb, wup_ref[c]), 0.0)
        hb = (a * a).astype(BF16)
        h_ref[:, c * FC:(c + 1) * FC] = hb
        y = y + _dot(hb, wdown_ref[c])
    r = ALPHA * xv + y
    r_ref[...] = r
    xhat, _ = _ln_stats(r)
    return xhat * g_ref[...] + b_ref[...]


def _mlp_fwd_loss(x, wup4, wdown4, g, b, target, name):
    S, D = x.shape
    NC, _, FC = wup4.shape
    tm = min(256, S)

    def body(x_ref, wup_ref, wdown_ref, g_ref, b_ref, t_ref, d_ref, r_ref, h_ref, l_ref):
        @pl.when(pl.program_id(0) == 0)
        def _():
            l_ref[...] = jnp.zeros_like(l_ref)

        err = _mlp_out(x_ref, wup_ref, wdown_ref, g_ref, b_ref, r_ref, h_ref) - t_ref[...]
        d_ref[...] = err * (1.0 / D)
        l_ref[...] += (0.5 / D) * jnp.sum(err * err)

    return pl.pallas_call(
        body, name=name, grid=(S // tm,),
        out_shape=(jax.ShapeDtypeStruct((S, D), F32), jax.ShapeDtypeStruct((S, D), F32),
                   jax.ShapeDtypeStruct((S, NC * FC), BF16), jax.ShapeDtypeStruct((8, LANES), F32)),
        in_specs=[_rows(tm, D), _full(wup4.shape), _full(wdown4.shape), _full((1, D)), _full((1, D)), _rows(tm, D)],
        out_specs=(_rows(tm, D), _rows(tm, D), _rows(tm, NC * FC), _full((8, LANES))),
        compiler_params=_params(),
    )(x, wup4, wdown4, g, b, target)


def _mlp_fwd(x, wup4, wdown4, g, b, name, comm=None):
    S, D = x.shape
    NC, _, FC = wup4.shape
    tm = min(256, S)

    def body(x_ref, wup_ref, wdown_ref, g_ref, b_ref, xo_ref, xb_ref, xbt_ref, r_ref, h_ref):
        xo = _mlp_out(x_ref, wup_ref, wdown_ref, g_ref, b_ref, r_ref, h_ref)
        xo_ref[...] = xo
        xb_ref[...] = xo.astype(BF16)
        xbt_ref[...] = xo.T.astype(BF16)

    return _pcall(
        body, name=name, grid=(S // tm,), comm=comm,
        out_shape=(jax.ShapeDtypeStruct((S, D), F32), jax.ShapeDtypeStruct((S, D), BF16),
                   jax.ShapeDtypeStruct((D, S), BF16), jax.ShapeDtypeStruct((S, D), F32),
                   jax.ShapeDtypeStruct((S, NC * FC), BF16)),
        in_specs=[_rows(tm, D), _full(wup4.shape), _full(wdown4.shape), _full((1, D)), _full((1, D))],
        out_specs=(_rows(tm, D), _rows(tm, D), _cols(D, tm), _rows(tm, D), _rows(tm, NC * FC)),
        args=(x, wup4, wdown4, g, b))


def _mlp_bwd(dxo, r, h, wup4, wdown4, g, name, comm=None):
    S, D = r.shape
    NC, _, FC = wup4.shape
    tm = min(256, S)

    def body(dxo_ref, r_ref, h_ref, wup_ref, wdown_ref, g_ref, dx_ref, drbt_ref, dpre_ref, dgb_ref):
        @pl.when(pl.program_id(0) == 0)
        def _():
            dgb_ref[...] = jnp.zeros_like(dgb_ref)

        xhat, rstd = _ln_stats(r_ref[...])
        dy = dxo_ref[...]
        dgb_ref[0:1, :] += jnp.sum(dy * xhat, axis=0, keepdims=True)
        dgb_ref[1:2, :] += jnp.sum(dy, axis=0, keepdims=True)
        dr = _ln_bwd(dy, xhat, rstd, g_ref[...])
        drb = dr.astype(BF16)
        drbt_ref[...] = dr.T.astype(BF16)
        dx = ALPHA * dr
        for c in range(NC):
            dh = _dot_nt(drb, wdown_ref[c])
            a = jnp.sqrt(h_ref[:, c * FC:(c + 1) * FC].astype(F32))
            dpre = (2.0 * a * dh).astype(BF16)
            dpre_ref[:, c * FC:(c + 1) * FC] = dpre
            dx = dx + _dot_nt(dpre, wup_ref[c])
        dx_ref[...] = dx

    return _pcall(
        body, name=name, grid=(S // tm,), comm=comm,
        out_shape=(jax.ShapeDtypeStruct((S, D), F32), jax.ShapeDtypeStruct((D, S), BF16),
                   jax.ShapeDtypeStruct((S, NC * FC), BF16), jax.ShapeDtypeStruct((2, D), F32)),
        in_specs=[_rows(tm, D), _rows(tm, D), _rows(tm, NC * FC), _full(wup4.shape), _full(wdown4.shape),
                  _full((1, D))],
        out_specs=(_rows(tm, D), _cols(D, tm), _rows(tm, NC * FC), _full((2, D))),
        args=(dxo, r, h, wup4, wdown4, g))


def _tail_fwd_store(xv, cat_ref, wo_ref, g_ref, b_ref, xo_ref, xbt_ref, r_ref, catt_ref):
    cat = cat_ref[...]
    y = _dot(cat, wo_ref[...])
    r = ALPHA * xv + y
    xhat, _ = _ln_stats(r)
    xo = xhat * g_ref[...] + b_ref[...]
    xo_ref[...] = xo
    xbt_ref[...] = xo.T.astype(BF16)
    catt_ref[...] = cat.astype(F32).T.astype(BF16)
    r_ref[...] = r


def _tail_bwd_head(dxo_ref, r_ref, g_ref, wo_ref, dgb_ref, drb_ref):
    xhat, rstd = _ln_stats(r_ref[...])
    dy = dxo_ref[...]
    dgb_ref[0:1, :] += jnp.sum(dy * xhat, axis=0, keepdims=True)
    dgb_ref[1:2, :] += jnp.sum(dy, axis=0, keepdims=True)
    dr = _ln_bwd(dy, xhat, rstd, g_ref[...])
    drb = dr.astype(BF16)
    drb_ref[...] = drb
    return dr, _dot_nt(drb, wo_ref[...])


def _causal(shape):
    return lax.broadcasted_iota(jnp.int32, shape, 1) <= lax.broadcasted_iota(jnp.int32, shape, 0)


def _sgu_mixed(ws_ref, bst_ref, gp, vpair):
    tril = _causal((CHUNK, CHUNK))
    w0 = jnp.where(tril, ws_ref[2 * gp], 0.0).astype(BF16)
    w1 = jnp.where(tril, ws_ref[2 * gp + 1], 0.0).astype(BF16)
    m0 = _dot(w0, vpair) + bst_ref[:, 2 * gp:2 * gp + 1]
    m1 = _dot(w1, vpair) + bst_ref[:, 2 * gp + 1:2 * gp + 2]
    return jnp.where(_first_head(), m0, m1), w0, w1


def _mixer_a_fwd(x, wa, sg, sb, ws, bst, mkv, wo, g, b, name, comm=None):
    S, D = x.shape
    DQ = mkv.shape[1] // 2
    DM = D - DQ
    NP = DM // PAIR
    tm = min(512, S)

    def body(x_ref, wa_ref, sg_ref, sb_ref, ws_ref, bst_ref, mkv_ref, wo_ref, g_ref, b_ref,
             xo_ref, xbt_ref, r_ref, catt_ref, xint_ref, zb_ref, z_s, zv_s, cat_ref):
        xv = x_ref[...]
        xint_ref[...] = xv.T.astype(BF16)
        z_s[...] = _dot(xv.astype(BF16), wa_ref[...])
        zb_ref[...] = z_s[...].astype(BF16)
        vhat, _ = _ln_stats(_gelu(z_s[:, DM:2 * DM]))
        zv_s[...] = (vhat * sg_ref[...] + sb_ref[...]).astype(BF16)
        for c in range(tm // CHUNK):
            rs = slice(c * CHUNK, (c + 1) * CHUNK)
            for gp in range(NP):
                ls = slice(gp * PAIR, (gp + 1) * PAIR)
                mixed, _, _ = _sgu_mixed(ws_ref, bst_ref, gp, zv_s[rs, ls])
                cat_ref[rs, ls] = (_gelu(z_s[rs, ls]) * mixed).astype(BF16)
        mo = _mem_attn_fwd(z_s[:, 2 * DM:].astype(BF16), mkv_ref, DQ)
        for mp in range(DQ // PAIR):
            cat_ref[:, DM + mp * PAIR:DM + (mp + 1) * PAIR] = mo[mp].astype(BF16)
        _tail_fwd_store(xv, cat_ref, wo_ref, g_ref, b_ref, xo_ref, xbt_ref, r_ref, catt_ref)

    return _pcall(
        body, name=name, grid=(S // tm,), comm=comm,
        out_shape=(jax.ShapeDtypeStruct((S, D), F32), jax.ShapeDtypeStruct((D, S), BF16),
                   jax.ShapeDtypeStruct((S, D), F32), jax.ShapeDtypeStruct((D, S), BF16),
                   jax.ShapeDtypeStruct((D, S), BF16), jax.ShapeDtypeStruct((S, 2 * DM + DQ), BF16)),
        in_specs=[_rows(tm, D), _full(wa.shape), _full(sg.shape), _full(sb.shape), _full(ws.shape), _full(bst.shape),
                  _full(mkv.shape), _full(wo.shape), _full((1, D)), _full((1, D))],
        out_specs=(_rows(tm, D), _cols(D, tm), _rows(tm, D), _cols(D, tm), _cols(D, tm), _rows(tm, 2 * DM + DQ)),
        scratch_shapes=[pltpu.VMEM((tm, 2 * DM + DQ), F32), pltpu.VMEM((tm, DM), BF16), pltpu.VMEM((tm, D), BF16)],
        args=(x, wa, sg, sb, ws, bst, mkv, wo, g, b))


def _mixer_a_bwd(zb, dxo, r, wa, sg, sb, ws, bst, mkv, wo, g, name, comm=None):
    S, D = r.shape
    DQ = mkv.shape[1] // 2
    DM = D - DQ
    NP = DM // PAIR
    NG = DM // HEAD_DIM
    tm = min(512, S)

    def body(zb_ref, dxo_ref, r_ref, wa_ref, sg_ref, sb_ref, ws_ref, bst_ref, mkv_ref, wo_ref, g_ref,
             dx_ref, dz_ref, drb_ref, dgb_ref, dsgb_ref, dws_ref, dbst_ref, dmkv_ref, z_s, zv_s, dzv_s):
        @pl.when(pl.program_id(0) == 0)
        def _():
            dgb_ref[...] = jnp.zeros_like(dgb_ref)
            dsgb_ref[...] = jnp.zeros_like(dsgb_ref)
            dws_ref[...] = jnp.zeros_like(dws_ref)
            dbst_ref[...] = jnp.zeros_like(dbst_ref)
            dmkv_ref[...] = jnp.zeros_like(dmkv_ref)

        dr, dcat = _tail_bwd_head(dxo_ref, r_ref, g_ref, wo_ref, dgb_ref, drb_ref)
        z_s[...] = zb_ref[...].astype(F32)
        vhat, vrstd = _ln_stats(_gelu(z_s[:, DM:2 * DM]))
        zv_s[...] = (vhat * sg_ref[...] + sb_ref[...]).astype(BF16)
        tril = _causal((CHUNK, CHUNK))
        lane = lax.broadcasted_iota(jnp.int32, (1, LANES), 1)
        for c in range(tm // CHUNK):
            rs = slice(c * CHUNK, (c + 1) * CHUNK)
            for gp in range(NP):
                ls = slice(gp * PAIR, (gp + 1) * PAIR)
                vpair = zv_s[rs, ls]
                mixed, w0, w1 = _sgu_mixed(ws_ref, bst_ref, gp, vpair)
                u_pre = z_s[rs, ls]
                dmix = dcat[rs, ls]
                dz_ref[rs, ls] = (dmix * mixed * _gelu_grad(u_pre)).astype(BF16)
                dmixed = dmix * _gelu(u_pre)
                first = _first_head()
                d0 = jnp.where(first, dmixed, 0.0)
                d1 = jnp.where(first, 0.0, dmixed)
                d0b = d0.astype(BF16)
                d1b = d1.astype(BF16)
                dzv_s[rs, ls] = _dot_tn(w0, d0b) + _dot_tn(w1, d1b)
                dws_ref[2 * gp] += jnp.where(tril, _dot_nt(d0b, vpair), 0.0)
                dws_ref[2 * gp + 1] += jnp.where(tril, _dot_nt(d1b, vpair), 0.0)
                dbst_ref[...] += (jnp.where(lane == 2 * gp, jnp.sum(d0, axis=-1, keepdims=True), 0.0)
                                  + jnp.where(lane == 2 * gp + 1, jnp.sum(d1, axis=-1, keepdims=True), 0.0))
        dzv = dzv_s[...]
        dsgb_ref[0:1, :] += jnp.sum(dzv * vhat, axis=0, keepdims=True)
        dsgb_ref[1:2, :] += jnp.sum(dzv, axis=0, keepdims=True)
        dgv = _ln_bwd(dzv, vhat, vrstd, sg_ref[...])
        dz_ref[:, DM:2 * DM] = (dgv * _gelu_grad(z_s[:, DM:2 * DM])).astype(BF16)
        dmo = [dcat[:, DM + mp * PAIR:DM + (mp + 1) * PAIR] for mp in range(DQ // PAIR)]
        dqm = _mem_attn_bwd(z_s[:, 2 * DM:].astype(BF16), mkv_ref, dmo, dmkv_ref, DQ)
        for mp in range(DQ // PAIR):
            dz_ref[:, 2 * DM + mp * PAIR:2 * DM + (mp + 1) * PAIR] = dqm[mp].astype(BF16)
        dx_ref[...] = ALPHA * dr + _dot_nt(dz_ref[...], wa_ref[...])

    ZW = 2 * DM + DQ
    M = mkv.shape[0]
    return _pcall(
        body, name=name, grid=(S // tm,), comm=comm,
        out_shape=(jax.ShapeDtypeStruct((S, D), F32), jax.ShapeDtypeStruct((S, ZW), BF16),
                   jax.ShapeDtypeStruct((S, D), BF16), jax.ShapeDtypeStruct((2, D), F32),
                   jax.ShapeDtypeStruct((2, DM), F32), jax.ShapeDtypeStruct((NG, CHUNK, CHUNK), F32),
                   jax.ShapeDtypeStruct((CHUNK, LANES), F32), jax.ShapeDtypeStruct((M, 2 * DQ), F32)),
        in_specs=[_rows(tm, ZW), _rows(tm, D), _rows(tm, D), _full(wa.shape), _full(sg.shape), _full(sb.shape),
                  _full(ws.shape), _full(bst.shape), _full(mkv.shape), _full(wo.shape), _full((1, D))],
        out_specs=(_rows(tm, D), _rows(tm, ZW), _rows(tm, D), _full((2, D)), _full((2, DM)),
                   _full((NG, CHUNK, CHUNK)), _full((CHUNK, LANES)), _full((M, 2 * DQ))),
        scratch_shapes=[pltpu.VMEM((tm, ZW), F32), pltpu.VMEM((tm, DM), BF16), pltpu.VMEM((tm, DM), F32)],
        args=(zb, dxo, r, wa, sg, sb, ws, bst, mkv, wo, g))


def _proj_b_fwd(xb, wq, wkv, dm, name):
    S, D = xb.shape
    DQ = D - dm
    tm = min(512, S)

    def body(x_ref, wq_ref, wkv_ref, q_ref, qm_ref, k_ref, v_ref, fl_ref):
        xv = x_ref[...]
        z = _dot(xv, wq_ref[...])
        q_ref[...] = z[:, :dm].astype(BF16)
        qm_ref[...] = z[:, dm:].astype(BF16)
        kvf = _dot(xv, wkv_ref[...])
        k_ref[...] = kvf[:, :dm].astype(BF16)
        v_ref[...] = kvf[:, dm:2 * dm].astype(BF16)
        fl_ref[...] = kvf[:, 2 * dm:]

    return pl.pallas_call(
        body, name=name, grid=(S // tm,),
        out_shape=(jax.ShapeDtypeStruct((S, dm), BF16), jax.ShapeDtypeStruct((S, DQ), BF16),
                   jax.ShapeDtypeStruct((S, dm), BF16), jax.ShapeDtypeStruct((S, dm), BF16),
                   jax.ShapeDtypeStruct((S, LANES), F32)),
        in_specs=[_rows(tm, D), _full(wq.shape), _full(wkv.shape)],
        out_specs=(_rows(tm, dm), _rows(tm, DQ), _rows(tm, dm), _rows(tm, dm), _rows(tm, LANES)),
        compiler_params=_params(),
    )(xb, wq, wkv)


def _proj_b_bwd(dr, dqz, dk, dv, dfl, wq, wkv, name):
    S, D = dr.shape
    dm = dk.shape[1]
    tm = min(512, S)

    def body(dr_ref, dqz_ref, dk_ref, dv_ref, dfl_ref, wq_ref, wkv_ref, dx_ref):
        dx_ref[...] = (ALPHA * dr_ref[...] + _dot_nt(dqz_ref[...], wq_ref[...])
                       + _dot_nt(dk_ref[...], wkv_ref[:, :dm]) + _dot_nt(dv_ref[...], wkv_ref[:, dm:2 * dm])
                       + _dot_nt(dfl_ref[...], wkv_ref[:, 2 * dm:]))

    return pl.pallas_call(
        body, name=name, grid=(S // tm,), out_shape=jax.ShapeDtypeStruct((S, D), F32),
        in_specs=[_rows(tm, D), _rows(tm, dqz.shape[1]), _rows(tm, dm), _rows(tm, dm), _rows(tm, dfl.shape[1]),
                  _full(wq.shape), _full(wkv.shape)],
        out_specs=_rows(tm, D), compiler_params=_params(),
    )(dr, dqz, dk, dv, dfl, wq, wkv)


def _mm_nn_cat(at, bs, name):
    M, R = at.shape
    widths = [b.shape[1] for b in bs]
    N = sum(widths)
    br = _pick(R, 1024)
    last = R // br - 1
    nb = len(bs)

    def body(a_ref, *refs):
        b_refs, o_ref, acc_ref = refs[:nb], refs[nb], refs[nb + 1]

        @pl.when(pl.program_id(0) == 0)
        def _():
            acc_ref[...] = jnp.zeros_like(acc_ref)

        a = a_ref[...]
        c0 = 0
        for b_ref, wd in zip(b_refs, widths):
            acc_ref[:, c0:c0 + wd] += _dot(a, b_ref[...])
            c0 += wd

        @pl.when(pl.program_id(0) == last)
        def _():
            o_ref[...] = acc_ref[...].astype(BF16)

    return pl.pallas_call(
        body, name=name, grid=(R // br,), out_shape=jax.ShapeDtypeStruct((M, N), BF16),
        in_specs=[pl.BlockSpec((M, br), lambda r: (0, r))] + [pl.BlockSpec((br, wd), lambda r: (r, 0)) for wd in widths],
        out_specs=_full((M, N)), scratch_shapes=[pltpu.VMEM((M, N), F32)], compiler_params=_params(),
    )(at, *bs)


def _split3(v):
    hi = v.astype(BF16)
    r1 = v - hi.astype(F32)
    mid = r1.astype(BF16)
    lo = (r1 - mid.astype(F32)).astype(BF16)
    return hi, mid, lo


def _tri_sum(v, tri_b):
    hi, mid, lo = _split3(v)
    return _dot(hi, tri_b) + _dot(mid, tri_b) + _dot(lo, tri_b)


def _log_sigmoid(x):
    return jnp.minimum(x, 0.0) - jnp.log(1.0 + jnp.exp(-jnp.abs(x)))


def _forget_cumsum(flt, bf, name):
    H, S = flt.shape
    nchunk = S // LANES

    def body(fl_ref, bf_ref, p_ref):
        upper = (lax.broadcasted_iota(jnp.int32, (LANES, LANES), 0)
                 <= lax.broadcasted_iota(jnp.int32, (LANES, LANES), 1)).astype(BF16)

        def step(n, carry):
            s0 = pl.multiple_of(n * LANES, LANES)
            lf = _log_sigmoid(fl_ref[:, pl.ds(s0, LANES)] + bf_ref[...])
            cs = _tri_sum(lf, upper) + carry
            for i, piece in enumerate(_split3(-cs)):
                p_ref[i, :, pl.ds(s0, LANES)] = piece.astype(F32)
            return cs[:, LANES - 1:LANES]

        lax.fori_loop(0, nchunk, step, jnp.zeros((H, 1), F32))

    return pl.pallas_call(body, name=name, out_shape=jax.ShapeDtypeStruct((3, H, S), F32),
                          compiler_params=pltpu.CompilerParams(vmem_limit_bytes=VMEM_LIMIT))(flt, bf)


def _forget_cumsum_bwd(dct, flt, bf, name):
    H, S = flt.shape
    nchunk = S // LANES

    def body(dc_ref, fl_ref, bf_ref, dfl_ref, dbf_ref):
        lower = (lax.broadcasted_iota(jnp.int32, (LANES, LANES), 0)
                 >= lax.broadcasted_iota(jnp.int32, (LANES, LANES), 1)).astype(BF16)

        def step(n, carry):
            tail, tot = carry
            s0 = pl.multiple_of((nchunk - 1 - n) * LANES, LANES)
            suffix = _tri_sum(dc_ref[:, pl.ds(s0, LANES)], lower) + tail
            xv = fl_ref[:, pl.ds(s0, LANES)] + bf_ref[...]
            dfl = suffix * (1.0 / (1.0 + jnp.exp(xv)))
            dfl_ref[:, pl.ds(s0, LANES)] = dfl
            return suffix[:, 0:1], tot + jnp.sum(dfl, axis=-1, keepdims=True)

        _, tot = lax.fori_loop(0, nchunk, step, (jnp.zeros((H, 1), F32), jnp.zeros((H, 1), F32)))
        dbf_ref[...] = jnp.broadcast_to(tot, (H, LANES))

    return pl.pallas_call(body, name=name,
                          out_shape=(jax.ShapeDtypeStruct((H, S), F32), jax.ShapeDtypeStruct((H, LANES), F32)),
                          compiler_params=pltpu.CompilerParams(vmem_limit_bytes=VMEM_LIMIT))(dct, flt, bf)


N_BIAS = 3


def _bias_ones():
    lane = lax.broadcasted_iota(jnp.int32, (1, PAIR), 1)
    return ((lane & (HEAD_DIM - 1)) < N_BIAS).astype(BF16)


def _fox_fwd(q, k, v, ca, name, comm=None):
    S, DM = q.shape
    NP = DM // PAIR
    T = min(FOX_BLOCK, S)
    nq = S // T

    def body(q_ref, k_ref, v_ref, ca_ref, o_ref, lse_ref):
        tril = _causal((T, T))
        heads = (_lane_mask(0), _lane_mask(1))
        ones3 = _bias_ones()

        def kv_block(kj):
            s0 = pl.multiple_of(kj * T, T)
            kb, vb, cab = k_ref[pl.ds(s0, T), :], v_ref[pl.ds(s0, T), :], ca_ref[pl.ds(s0, T), :]
            return ([jnp.where(heads[e], kb, cab) for e in (0, 1)],
                    [jnp.where(heads[e], vb, jnp.ones_like(vb)) for e in (0, 1)])

        def q_pair(a, _):
            t0s = [pl.multiple_of((2 * a + r) * T, T) for r in (0, 1)]
            qaug = {}
            for r in (0, 1):
                qb = q_ref[pl.ds(t0s[r], T), :] * QK_SCALE
                for e in (0, 1):
                    qaug[r, e] = jnp.where(heads[e], qb, ones3)
            chains = [(r, e) for r in (0, 1) for e in (0, 1)]

            def update(carries, first_block, seen):
                blocks = [kv_block(first_block + kk) for kk in range(1 + max(kk for s in seen for kk, _ in s))]
                out = []
                scores = [[_dot_nt(qaug[r, e], blocks[kk][0][e]) for kk, _ in seen[r]] for r, e in chains]
                for n, (r, e) in enumerate(chains):
                    m, acc = carries[n]
                    tiles = [jnp.where(tril, s, NEG) if masked else s for s, (_, masked) in zip(scores[n], seen[r])]
                    m_new = jnp.maximum(m, jnp.max(functools.reduce(jnp.maximum, tiles), axis=-1, keepdims=True))
                    acc = jnp.exp(m - m_new) * acc
                    for s, (kk, _) in zip(tiles, seen[r]):
                        acc = acc + _dot(jnp.exp(s - m_new).astype(BF16), blocks[kk][1][e])
                    out.append((m_new, acc))
                return tuple(out)

            def full(first_block, n_blocks, carries):
                return update(carries, first_block, [[(kk, False) for kk in range(n_blocks)]] * 2)

            init = (jnp.full((T, 1), NEG, F32), jnp.zeros((T, PAIR), F32))
            carries = lax.fori_loop(0, a // 2, lambda i, c: full(4 * i, 4, c), (init,) * 4)
            carries = lax.fori_loop(0, a % 2, lambda i, c: full(2 * a - 2, 2, c), carries)
            carries = update(carries, 2 * a, [[(0, True)], [(0, False), (1, True)]])
            first = _first_head()
            for r in (0, 1):
                (m0, acc0), (m1, acc1) = carries[2 * r], carries[2 * r + 1]
                l0, l1 = acc0[:, HEAD_DIM:HEAD_DIM + 1], acc1[:, 0:1]
                o_ref[pl.ds(t0s[r], T), :] = jnp.where(first, acc0 / l0, acc1 / l1).astype(BF16)
                lse_t = jnp.where(first, m0 + jnp.log(l0), m1 + jnp.log(l1)).T
                lse_ref[0:1, pl.ds(t0s[r], T)] = lse_t[0:1, :]
                lse_ref[1:2, pl.ds(t0s[r], T)] = lse_t[HEAD_DIM:HEAD_DIM + 1, :]
            return 0

        lax.fori_loop(0, nq // 2, q_pair, 0)

    col = pl.BlockSpec((S, PAIR), lambda hp: (0, hp))
    return _pcall(
        body, name=name, grid=(NP,), comm=comm,
        out_shape=(jax.ShapeDtypeStruct((S, DM), BF16), jax.ShapeDtypeStruct((NP, 2, S), F32)),
        in_specs=[col, col, col, col], out_specs=(col, pl.BlockSpec((None, 2, S), lambda hp: (hp, 0, 0))),
        args=(q, k, v, ca))


def _fox_bwd(q, k, v, o, do, ca, lse, name, comm=None):
    S, DM = q.shape
    NP = DM // PAIR
    T = min(FOX_BLOCK, S)
    nq = S // T

    def body(q_ref, k_ref, v_ref, o_ref, do_ref, ca_ref, lse_ref, dq_ref, dk_ref, dv_ref, dc_ref,
             dq_acc, dcol_acc, dk_acc, dv_acc, dd_s):
        keep = lax.broadcasted_iota(jnp.int32, (T, T), 0) <= lax.broadcasted_iota(jnp.int32, (T, T), 1)
        lane = lax.broadcasted_iota(jnp.int32, (1, LANES), 1)
        heads = (_lane_mask(0), _lane_mask(1))
        ones3 = _bias_ones()
        dq_acc[...] = jnp.zeros_like(dq_acc)
        dcol_acc[...] = jnp.zeros_like(dcol_acc)
        dc_ref[...] = jnp.zeros_like(dc_ref)

        def row_dots(qi, _):
            t0 = pl.multiple_of(qi * T, T)
            prod = do_ref[pl.ds(t0, T), :].astype(F32) * o_ref[pl.ds(t0, T), :].astype(F32)
            cols = [jnp.sum(jnp.where(heads[e], prod, 0.0), axis=-1, keepdims=True) for e in (0, 1)]
            tile = jnp.where(lane == 0, cols[0], jnp.where(lane == 1, cols[1], 0.0))
            dd_s[:, pl.ds(t0, T)] = tile.T[0:8, :]
            return 0

        lax.fori_loop(0, nq, row_dots, 0)

        def kv_pair(b, _):
            dk_acc[...] = jnp.zeros_like(dk_acc)
            dv_acc[...] = jnp.zeros_like(dv_acc)
            s0s = [pl.multiple_of((2 * b + jj) * T, T) for jj in (0, 1)]
            kaug, ks, vbs = {}, {}, []
            for jj in (0, 1):
                kb = k_ref[pl.ds(s0s[jj], T), :]
                cab = ca_ref[pl.ds(s0s[jj], T), :]
                vbs.append(v_ref[pl.ds(s0s[jj], T), :])
                for e in (0, 1):
                    kaug[jj, e] = jnp.where(heads[e], kb, cab)
                    ks[jj, e] = jnp.where(heads[e], kb, jnp.zeros_like(kb)) * QK_SCALE

            def q_step(qi, blocks):
                t0 = pl.multiple_of(qi * T, T)
                qs = q_ref[pl.ds(t0, T), :] * QK_SCALE
                dob = do_ref[pl.ds(t0, T), :]
                doh = [jnp.where(heads[e], dob, jnp.zeros_like(dob)) for e in (0, 1)]
                qh = [jnp.where(heads[e], qs, jnp.zeros_like(qs)) for e in (0, 1)]
                chains = [(e, jj, masked) for e in (0, 1) for jj, masked in blocks]
                scores = [_dot_nt(kaug[jj, e], jnp.where(heads[e], qs, ones3)) for e, jj, _ in chains]
                dps = [_dot_nt(vbs[jj], doh[e]) for e, jj, _ in chains]
                dq = jnp.zeros((T, PAIR), F32)
                drow = [jnp.zeros((1, T), F32), jnp.zeros((1, T), F32)]
                for n, (e, jj, masked) in enumerate(chains):
                    st = jnp.where(keep, scores[n], NEG) if masked else scores[n]
                    pt = jnp.exp(st - lse_ref[e:e + 1, pl.ds(t0, T)])
                    dv_acc[jj] += _dot(pt.astype(BF16), doh[e])
                    dst = pt * (dps[n] - dd_s[e:e + 1, pl.ds(t0, T)])
                    dsb = dst.astype(BF16)
                    dk_acc[jj] += _dot(dsb, qh[e])
                    dq = dq + _dot_tn(dsb, ks[jj, e])
                    dcol_acc[pl.ds(s0s[jj], T), :] -= jnp.where(lane == e, jnp.sum(dst, axis=-1, keepdims=True), 0.0)
                    drow[e] = drow[e] + jnp.sum(dst, axis=0, keepdims=True)
                for e in (0, 1):
                    dc_ref[e:e + 1, pl.ds(t0, T)] += drow[e]
                dq_acc[pl.ds(t0, T), :] += dq

            q_step(2 * b, [(0, True)])
            q_step(2 * b + 1, [(0, False), (1, True)])

            def rest(i, _):
                q_step(2 * b + 2 + 2 * i, [(0, False), (1, False)])
                q_step(2 * b + 3 + 2 * i, [(0, False), (1, False)])
                return 0

            lax.fori_loop(0, (nq - 2) // 2 - b, rest, 0)
            for jj in (0, 1):
                dk_ref[pl.ds(s0s[jj], T), :] = dk_acc[jj].astype(BF16)
                dv_ref[pl.ds(s0s[jj], T), :] = dv_acc[jj].astype(BF16)
            return 0

        lax.fori_loop(0, nq // 2, kv_pair, 0)
        dq_ref[...] = dq_acc[...].astype(BF16)

        def add_cols(kj, _):
            s0 = pl.multiple_of(kj * T, T)
            dc_ref[:, pl.ds(s0, T)] += dcol_acc[pl.ds(s0, T), :].T[0:2, :]
            return 0

        lax.fori_loop(0, nq, add_cols, 0)

    col = pl.BlockSpec((S, PAIR), lambda hp: (0, hp))
    row2 = pl.BlockSpec((None, 2, S), lambda hp: (hp, 0, 0))
    return _pcall(
        body, name=name, grid=(NP,), comm=comm,
        out_shape=(jax.ShapeDtypeStruct(do.shape, BF16), jax.ShapeDtypeStruct((S, DM), BF16),
                   jax.ShapeDtypeStruct((S, DM), BF16), jax.ShapeDtypeStruct((NP, 2, S), F32)),
        in_specs=[col, col, col, col, col, col, row2], out_specs=(col, col, col, row2),
        scratch_shapes=[pltpu.VMEM((S, PAIR), F32), pltpu.VMEM((S, LANES), F32), pltpu.VMEM((2, T, PAIR), F32),
                        pltpu.VMEM((2, T, PAIR), F32), pltpu.VMEM((8, S), F32)],
        args=(q, k, v, o, do, ca, lse), aliases={4: 0})


def _mixer_b_fwd(x, o, qm, mkv, wo, g, b, name):
    S, D = x.shape
    DQ = qm.shape[1]
    DM = D - DQ
    tm = min(512, S)

    def body(x_ref, o_ref, qm_ref, mkv_ref, wo_ref, g_ref, b_ref, xo_ref, xbt_ref, r_ref, catt_ref, cat_ref):
        cat_ref[:, :DM] = o_ref[...]
        mo = _mem_attn_fwd(qm_ref[...], mkv_ref, DQ)
        for mp in range(DQ // PAIR):
            cat_ref[:, DM + mp * PAIR:DM + (mp + 1) * PAIR] = mo[mp].astype(BF16)
        _tail_fwd_store(x_ref[...], cat_ref, wo_ref, g_ref, b_ref, xo_ref, xbt_ref, r_ref, catt_ref)

    return pl.pallas_call(
        body, name=name, grid=(S // tm,),
        out_shape=(jax.ShapeDtypeStruct((S, D), F32), jax.ShapeDtypeStruct((D, S), BF16),
                   jax.ShapeDtypeStruct((S, D), F32), jax.ShapeDtypeStruct((D, S), BF16)),
        in_specs=[_rows(tm, D), _rows(tm, DM), _rows(tm, DQ), _full(mkv.shape), _full(wo.shape), _full((1, D)),
                  _full((1, D))],
        out_specs=(_rows(tm, D), _cols(D, tm), _rows(tm, D), _cols(D, tm)),
        scratch_shapes=[pltpu.VMEM((tm, D), BF16)],
        compiler_params=_params(),
    )(x, o, qm, mkv, wo, g, b)


def _mixer_b_bwd(dxo, r, qm, mkv, wo, g, name, comm=None):
    S, D = r.shape
    DQ = qm.shape[1]
    DM = D - DQ
    M = mkv.shape[0]
    tm = min(512, S)

    def body(dxo_ref, r_ref, qm_ref, mkv_ref, wo_ref, g_ref, dr_ref, drb_ref, dcat_ref, dgb_ref, dmkv_ref):
        @pl.when(pl.program_id(0) == 0)
        def _():
            dgb_ref[...] = jnp.zeros_like(dgb_ref)
            dmkv_ref[...] = jnp.zeros_like(dmkv_ref)

        dr, dcat = _tail_bwd_head(dxo_ref, r_ref, g_ref, wo_ref, dgb_ref, drb_ref)
        dr_ref[...] = dr
        dcat_ref[:, :DM] = dcat[:, :DM].astype(BF16)
        dmo = [dcat[:, DM + mp * PAIR:DM + (mp + 1) * PAIR] for mp in range(DQ // PAIR)]
        dqm = _mem_attn_bwd(qm_ref[...], mkv_ref, dmo, dmkv_ref, DQ)
        for mp in range(DQ // PAIR):
            dcat_ref[:, DM + mp * PAIR:DM + (mp + 1) * PAIR] = dqm[mp].astype(BF16)

    return _pcall(
        body, name=name, grid=(S // tm,), comm=comm,
        out_shape=(jax.ShapeDtypeStruct((S, D), F32), jax.ShapeDtypeStruct((S, D), BF16),
                   jax.ShapeDtypeStruct((S, D), BF16), jax.ShapeDtypeStruct((2, D), F32),
                   jax.ShapeDtypeStruct((M, 2 * DQ), F32)),
        in_specs=[_rows(tm, D), _rows(tm, D), _rows(tm, DQ), _full(mkv.shape), _full(wo.shape), _full((1, D))],
        out_specs=(_rows(tm, D), _rows(tm, D), _rows(tm, D), _full((2, D)), _full((M, 2 * DQ))),
        args=(dxo, r, qm, mkv, wo, g))


class _NoExchange:
    def plan(self, stage, w, grads):
        return None

    def done(self, stage, results, w):
        pass


def _local_step(x, mem, target, w, hooks):
    S, D = x.shape
    DQ = w["mkv0"].shape[1] // 2
    DM = D - DQ
    NG = DM // HEAD_DIM
    NP = DM // PAIR
    ln_g, ln_b = w["ln_g"], w["ln_b"]
    row = lambda a, i: a[i:i + 1]
    memb = mem.astype(BF16)
    bst =jnp.pad(w["b_s"].T, ((0, 0), (0, LANES - NG)))
    grads = {}

    def staged(stage, fn, *args):
        out, got = fn(*args, stage, comm=hooks.plan(stage, w, grads))
        hooks.done(stage, got, w)
        return out

    mkv0 = _mm_small(memb, w["mkv0"], "mem_kv_0")
    x1, x1t, r1, cat0t, x0t, z0b = staged("mixer_a_fwd", _mixer_a_fwd, x, w["wa"], w["sg"], w["sb"], w["ws"], bst, mkv0,
                                w["wo0"], row(ln_g, 0), row(ln_b, 0))
    x2, x2b, x2t, r2, h0 = staged("mlp_fwd_0", _mlp_fwd, x1, w["wup0"], w["wdown0"], row(ln_g, 1), row(ln_b, 1))
    q, qm, k, v, fl = _proj_b_fwd(x2b, w["wq"], w["wkv"], DM, "proj_b_fwd")
    flt = fl[:, :16].T
    bfc = jnp.pad(w["bf"], (0, 16 - NG)).reshape(16, 1)
    pieces = _forget_cumsum(flt, bfc, "forget_cumsum")
    pieces = jnp.pad(pieces[:, :NG].transpose(2, 1, 0), ((0, 0), (0, 0), (0, HEAD_DIM - N_BIAS)))
    ca = pieces.reshape(S, NP, 2, HEAD_DIM)[:, :, ::-1].reshape(S, DM).astype(BF16)
    o, lse = staged("fox_fwd", _fox_fwd, q, k, v, ca)
    mkv1 = _mm_small(memb, w["mkv1"], "mem_kv_1")
    x3, x3t, r3, cat1t = _mixer_b_fwd(x2, o, qm, mkv1, w["wo1"], row(ln_g, 2), row(ln_b, 2), "mixer_b_fwd")
    dx4, r4, h1, loss_part = _mlp_fwd_loss(x3, w["wup1"], w["wdown1"], row(ln_g, 3), row(ln_b, 3), target,
                                           "mlp_fwd_1")

    nc = w["wup0"].shape[0]
    (dx3, dr4t, dpre1, dgb11), _ = _mlp_bwd(dx4, r4, h1, w["wup1"], w["wdown1"], row(ln_g, 3), "mlp_bwd_1")
    grads["wdown1"] = _mm_nn(dr4t, h1, "dw_down_1", out_split=nc)
    grads["wup1"] = _mm_nn(x3t, dpre1, "dw_up_1", out_split=nc)
    dr3, dr3b, dcat1, dgb10, dmkv1 = staged("mixer_b_bwd", _mixer_b_bwd, dx3, r3, qm, mkv1, w["wo1"], row(ln_g, 2))
    dqz, dk, dv, dct3 = staged("fox_bwd", _fox_bwd, q, k, v, o, dcat1, ca, lse)
    dct = jnp.pad(dct3.reshape(NG, S), ((0, 16 - NG), (0, 0)))
    dflt, dbf = _forget_cumsum_bwd(dct, flt, bfc, "forget_cumsum_bwd")
    dfl = jnp.pad(dflt.T, ((0, 0), (0, LANES - 16))).astype(BF16)
    dx2 = _proj_b_bwd(dr3, dqz, dk, dv, dfl, w["wq"], w["wkv"], "proj_b_bwd")
    grads["wo1"] = _mm_nn(cat1t, dr3b, "dw_o_1")
    grads["wq"] = _mm_nn(x2t, dqz, "dw_q")
    grads["wkv"] = _mm_nn_cat(x2t, [dk, dv, dfl], "dw_kv")
    grads["mkv1"] = _mm_tn(memb, dmkv1.astype(BF16), "dw_mkv_1")
    dx1, dr2t, dpre0, dgb01 = staged("mlp_bwd_0", _mlp_bwd, dx2, r2, h0, w["wup0"], w["wdown0"], row(ln_g, 1))
    grads["wdown0"] = _mm_nn(dr2t, h0, "dw_down_0", out_split=nc)
    grads["wup0"] = _mm_nn(x1t, dpre0, "dw_up_0", out_split=nc)
    dx0, dz, dr1b, dgb00, dsgb, dws, dbst, dmkv0 = staged(
        "mixer_a_bwd", _mixer_a_bwd, z0b, dx1, r1, w["wa"], w["sg"], w["sb"], w["ws"], bst, mkv0, w["wo0"], row(ln_g, 0))
    grads["wo0"] = _mm_nn(cat0t, dr1b, "dw_o_0")
    grads["wa"] = _mm_nn(x0t, dz, "dw_a")
    grads["mkv0"] = _mm_tn(memb, dmkv0.astype(BF16), "dw_mkv_0")
    grads.update({
        "ws": dws, "b_s": dbst[:, :NG].T, "sg": dsgb[0:1], "sb": dsgb[1:2], "bf": dbf[:NG, 0], "loss": loss_part[0:1],
        "ln_g": jnp.concatenate([dgb00[0:1], dgb01[0:1], dgb10[0:1], dgb11[0:1]], axis=0),
        "ln_b": jnp.concatenate([dgb00[1:2], dgb01[1:2], dgb10[1:2], dgb11[1:2]], axis=0),
    })
    return loss_part, dx0, grads


def _place():
    xi, yi, ci = lax.axis_index("x"), lax.axis_index("y"), lax.axis_index("c")
    peers = [(xi, 1 - yi), (1 - xi, yi), (1 - xi, 1 - yi)]
    return xi, yi, ci, peers


def _comm_call(body, name, ins, out_shapes, n_remote, n_local):
    return pl.pallas_call(
        body, name=name, out_shape=out_shapes, in_specs=[ANY] * len(ins), out_specs=[ANY] * len(out_shapes),
        scratch_shapes=[pltpu.SemaphoreType.DMA((n_remote,)), pltpu.SemaphoreType.DMA((n_remote,)),
                        pltpu.SemaphoreType.DMA((max(n_local, 1),))],
        compiler_params=pltpu.CompilerParams(has_side_effects=True),
    )(*ins)


def _pair_swap(xs):
    n = len(xs)
    out_shapes = [jax.ShapeDtypeStruct((a.shape[0],) + a.shape[2:], a.dtype) for a in xs]

    def copies(x_refs, o_refs, sems):
        send_sems, recv_sems = sems
        xi, yi, ci, _ = _place()
        return [pltpu.make_async_remote_copy(src_ref=x_refs[i].at[:, 1 - ci], dst_ref=o_refs[i],
                                             send_sem=send_sems.at[i], recv_sem=recv_sems.at[i],
                                             device_id=(xi, yi, 1 - ci), device_id_type=MESH) for i in range(n)]

    def start(x_refs, o_refs, sems):
        for cp in copies(x_refs, o_refs, sems):
            cp.start()

    def finish(x_refs, o_refs, sems):
        cps = copies(x_refs, o_refs, sems)
        for cp in cps:
            cp.wait_recv()
        for cp in cps:
            cp.wait_send()

    return _Comm(xs, out_shapes, {}, [n, n], start, finish)


def _chip_exchange(xs):
    n = len(xs)
    out_shapes = [jax.ShapeDtypeStruct((3,) + a.shape[1:], a.dtype) for a in xs]

    def copies(x_refs, o_refs, sems):
        send_sems, recv_sems = sems
        xi, yi, ci, peers = _place()
        return [pltpu.make_async_remote_copy(src_ref=x_refs[i].at[2 * px + py], dst_ref=o_refs[i].at[d],
                                             send_sem=send_sems.at[3 * i + d], recv_sem=recv_sems.at[3 * i + d],
                                             device_id=(px, py, ci), device_id_type=MESH)
                for i in range(n) for d, (px, py) in enumerate(peers)]

    def start(x_refs, o_refs, sems):
        for cp in copies(x_refs, o_refs, sems):
            cp.start()

    def finish(x_refs, o_refs, sems):
        cps = copies(x_refs, o_refs, sems)
        for cp in cps:
            cp.wait_recv()
        for cp in cps:
            cp.wait_send()

    return _Comm(xs, out_shapes, {}, [3 * n, 3 * n], start, finish)


def _inplace_call(body, name, bufs, sem_counts):
    n = len(bufs)
    return pl.pallas_call(
        body, name=name, out_shape=[jax.ShapeDtypeStruct(a.shape, a.dtype) for a in bufs],
        in_specs=[ANY] * n, out_specs=[ANY] * n, input_output_aliases={i: i for i in range(n)},
        scratch_shapes=[pltpu.SemaphoreType.DMA((k,)) for k in sem_counts],
        compiler_params=pltpu.CompilerParams(has_side_effects=True),
    )(*bufs)


def _gather_weights(bufs, whole=()):
    n, nw = len(bufs), len(whole)

    def copy(ref, s, r, k, dev):
        return pltpu.make_async_remote_copy(src_ref=ref, dst_ref=ref, send_sem=s.at[k], recv_sem=r.at[k],
                                            device_id=dev, device_id_type=MESH)

    def plan(w, sems):
        ici_s, ici_r, d2d_s, d2d_r = sems
        xi, yi, ci, peers = _place()
        xn, yn, sib = (1 - xi, yi, ci), (xi, 1 - yi, ci), (xi, yi, 1 - ci)
        j, jx, jy, jd = 2 * xi + yi, 2 * (1 - xi) + yi, 2 * xi + 1 - yi, 2 * (1 - xi) + 1 - yi
        c = {}
        for i in range(n):
            hh = bufs[i].shape[2] // 2
            lo, hi = pl.ds(0, hh), pl.ds(hh, hh)
            at = lambda slot, half, rows=None, i=i: w[i].at[slot, half] if rows is None else w[i].at[slot, half, rows]
            c["to_x", i] = (copy(at(j, ci), ici_s, ici_r, 4 * i, xn), copy(at(jx, ci), ici_s, ici_r, 4 * i, xn))
            c["to_y", i] = (copy(at(j, ci), ici_s, ici_r, 4 * i + 1, yn), copy(at(jy, ci), ici_s, ici_r, 4 * i + 1, yn))
            c["x_on", i] = (copy(at(jx, ci, lo), ici_s, ici_r, 4 * i + 2, yn), copy(at(jd, ci, lo), ici_s, ici_r, 4 * i + 2, yn))
            c["y_on", i] = (copy(at(jy, ci, hi), ici_s, ici_r, 4 * i + 3, xn), copy(at(jd, ci, hi), ici_s, ici_r, 4 * i + 3, xn))
            for k, slot in enumerate((jx, jy, jd)):
                c["sib", i, k] = (copy(at(slot, ci), d2d_s, d2d_r, 3 * i + k, sib),
                                  copy(at(slot, 1 - ci), d2d_s, d2d_r, 3 * i + k, sib))
        for i in range(nw):
            for d, (px, py) in enumerate(peers):
                k = 4 * n + 3 * i + d
                c["whole", i, d] = (copy(w[n + i].at[j], ici_s, ici_r, k, (px, py, ci)),
                                    copy(w[n + i].at[2 * px + py], ici_s, ici_r, k, (px, py, ci)))
        return c

    def start(_, w, sems):
        c = plan(w, sems)
        for i in range(n):
            c["to_x", i][0].start()
            c["to_y", i][0].start()
        for i in range(nw):
            for d in range(3):
                c["whole", i, d][0].start()

    def middle(_, w, sems):
        c = plan(w, sems)
        for i in range(n):
            c["to_x", i][1].wait_recv()
            c["x_on", i][0].start()
            c["sib", i, 0][0].start()
            c["to_y", i][1].wait_recv()
            c["y_on", i][0].start()
            c["sib", i, 1][0].start()

    def finish(_, w, sems):
        c = plan(w, sems)
        for i in range(n):
            c["x_on", i][1].wait_recv()
            c["y_on", i][1].wait_recv()
            c["sib", i, 2][0].start()
        for i in range(n):
            for k in range(3):
                c["sib", i, k][1].wait_recv()
        for i in range(nw):
            for d in range(3):
                c["whole", i, d][1].wait_recv()
        for key, (sent, _) in c.items():
            sent.wait_send()

    arrays = list(bufs) + list(whole)
    k = 4 * n + 3 * nw
    return _Comm(arrays, [jax.ShapeDtypeStruct(a.shape, a.dtype) for a in arrays], {i: i for i in range(n + nw)},
                 [k, k, max(3 * n, 1), max(3 * n, 1)], start, finish, middle)


def _pair_gather(bufs, spread, name):
    n = len(bufs)
    where = [(i, l) for i, a in enumerate(bufs) for l in range(a.shape[0])]
    nw = len(where)
    arrays = list(bufs) + ([] if spread is None else [spread])
    na = len(arrays)

    def body(*refs):
        g = refs[na:na + n]
        sp = refs[na + n] if spread is not None else None
        send_sems, recv_sems = refs[2 * na:]
        xi, yi, ci, peers = _place()
        chips = [(xi, yi)] + peers
        others = [(d, f) for d in range(4) for f in (0, 1) if (d, f) != (0, 0)] if spread is not None else []
        sends = []
        for k, (i, l) in enumerate(where):
            mine = g[i].at[l, :, ci]
            cp = pltpu.make_async_remote_copy(src_ref=mine, dst_ref=mine, send_sem=send_sems.at[k],
                                              recv_sem=recv_sems.at[k], device_id=(xi, yi, 1 - ci), device_id_type=MESH)
            cp.start()
            sends.append(cp)
        for k, (d, f) in enumerate(others):
            mine = sp.at[2 * xi + yi, ci]
            cp = pltpu.make_async_remote_copy(src_ref=mine, dst_ref=mine, send_sem=send_sems.at[nw + k],
                                              recv_sem=recv_sems.at[nw + k],
                                              device_id=(chips[d][0], chips[d][1], ci if f == 0 else 1 - ci),
                                              device_id_type=MESH)
            cp.start()
            sends.append(cp)
        for k, (i, l) in enumerate(where):
            other = g[i].at[l, :, 1 - ci]
            pltpu.make_async_remote_copy(src_ref=other, dst_ref=other, send_sem=send_sems.at[k],
                                         recv_sem=recv_sems.at[k], device_id=(xi, yi, 1 - ci),
                                         device_id_type=MESH).wait_recv()
        for k, (d, f) in enumerate(others):
            px, py, pc = chips[d][0], chips[d][1], (ci if f == 0 else 1 - ci)
            theirs = sp.at[2 * px + py, pc]
            pltpu.make_async_remote_copy(src_ref=theirs, dst_ref=theirs, send_sem=send_sems.at[nw + k],
                                         recv_sem=recv_sems.at[nw + k], device_id=(px, py, pc),
                                         device_id_type=MESH).wait_recv()
        for cp in sends:
            cp.wait_send()

    n_sems = nw + (7 if spread is not None else 0)
    out = _inplace_call(body, name, arrays, [n_sems, n_sems])
    return out[:n], (out[n] if spread is not None else None)


def _row_block(rows, cols):
    want = max(8, (2 ** 18 // max(cols, 1)) // 8 * 8)
    if rows <= want:
        return rows
    best = 8
    for t in range(8, want + 1, 8):
        if rows % t == 0:
            best = t
    return best


def _pair_add(g4, recv, sel, out_dtype, name):
    A, _, H, C = g4.shape
    bh = _row_block(H, C)

    def body(sel_ref, g_ref, r_ref, o_ref):
        o_ref[...] = (g_ref[...].astype(F32) + r_ref[...].astype(F32)).astype(out_dtype)

    return pl.pallas_call(
        body, name=name, out_shape=jax.ShapeDtypeStruct((A, H, C), out_dtype),
        grid_spec=pltpu.PrefetchScalarGridSpec(
            num_scalar_prefetch=1, grid=(A, H // bh),
            in_specs=[pl.BlockSpec((None, None, bh, C), lambda a, i, s: (a, s[1], i, 0)),
                      pl.BlockSpec((None, bh, C), lambda a, i, s: (a, i, 0))],
            out_specs=pl.BlockSpec((None, bh, C), lambda a, i, s: (a, i, 0))),
        compiler_params=pltpu.CompilerParams(dimension_semantics=("arbitrary", "arbitrary"),
                                             vmem_limit_bytes=VMEM_LIMIT),
    )(sel, g4, recv)


def _chip_reduce(g4, recv1, recv2, sel, buf, layer, n_layers, name, by_chip=False):
    _, _, H, C = g4.shape
    bh = _row_block(H, C)
    if by_chip:
        out_shape = jax.ShapeDtypeStruct((N_CHIPS, 2, H, C), F32)
        out_spec = pl.BlockSpec((None, None, bh, C), lambda i, s: (s[0], s[1], i, 0))
    else:
        out_shape = jax.ShapeDtypeStruct((n_layers, 1, 2, H, C), F32)
        out_spec = pl.BlockSpec((None, None, None, bh, C), lambda i, s: (layer, 0, s[1], i, 0))

    def body(sel_ref, g_ref, r1_ref, r2_ref, *rest):
        acc = g_ref[...].astype(F32) + r1_ref[...].astype(F32)
        for d in range(3):
            acc = acc + r2_ref[d].astype(F32)
        rest[-1][...] = acc

    in_specs = [pl.BlockSpec((None, None, bh, C), lambda i, s: (s[0], s[1], i, 0)),
                pl.BlockSpec((None, bh, C), lambda i, s: (s[0], i, 0)),
                pl.BlockSpec((3, bh, C), lambda i, s: (0, i, 0))]
    args = [sel, g4, recv1, recv2]
    aliases = {}
    if buf is not None:
        in_specs.append(ANY)
        args.append(buf)
        aliases = {4: 0}
    return pl.pallas_call(
        body, name=name, out_shape=out_shape,
        grid_spec=pltpu.PrefetchScalarGridSpec(num_scalar_prefetch=1, grid=(H // bh,), in_specs=in_specs,
                                               out_specs=out_spec),
        input_output_aliases=aliases,
        compiler_params=pltpu.CompilerParams(dimension_semantics=("arbitrary",), vmem_limit_bytes=VMEM_LIMIT),
    )(*args)


def _cast_place(shards, layer, sel, name):
    _, _, H, C = shards.shape
    bh = _row_block(H, C)

    def body(sel_ref, x_ref, o_ref):
        o_ref[...] = x_ref[...].astype(BF16)

    return pl.pallas_call(
        body, name=name, out_shape=jax.ShapeDtypeStruct((N_CHIPS, 2, H, C), BF16),
        grid_spec=pltpu.PrefetchScalarGridSpec(
            num_scalar_prefetch=1, grid=(2, H // bh),
            in_specs=[pl.BlockSpec((None, None, bh, C), lambda h, i, s: (layer, h, i, 0))],
            out_specs=pl.BlockSpec((None, None, bh, C), lambda h, i, s: (s[0], h, i, 0))),
        compiler_params=pltpu.CompilerParams(dimension_semantics=("arbitrary", "arbitrary"),
                                             vmem_limit_bytes=VMEM_LIMIT),
    )(sel, shards)


def _adamw_many(quads, name, comm=None):
    R, C = quads[0][0].shape
    br = _row_block(R, C)
    n = len(quads)

    def body(*refs):
        for k in range(n):
            g_ref, w_ref, m_ref, v_ref = refs[4 * k:4 * k + 4]
            d_ref, mo_ref, vo_ref = refs[4 * n + 3 * k:4 * n + 3 * k + 3]
            gv = g_ref[...]
            mn = ADAM_B1 * m_ref[...] + (1.0 - ADAM_B1) * gv
            vn = ADAM_B2 * v_ref[...] + (1.0 - ADAM_B2) * (gv * gv)
            m_hat = mn / (1.0 - ADAM_B1 ** ADAM_STEP)
            v_hat = vn / (1.0 - ADAM_B2 ** ADAM_STEP)
            d_ref[...] = -ADAM_LR * (m_hat / (jnp.sqrt(v_hat) + ADAM_EPS) + ADAM_WD * w_ref[...])
            mo_ref[...] = mn
            vo_ref[...] = vn

    spec = pl.BlockSpec((br, C), lambda i: (i, 0))
    res, got = _pcall(body, name=name, grid=(R // br,), comm=comm, out_shape=(jax.ShapeDtypeStruct((R, C), F32),) * (3 * n),
                      in_specs=[spec] * (4 * n), out_specs=(spec,) * (3 * n), args=[a for q in quads for a in q])
    return [res[3 * k:3 * k + 3] for k in range(n)], got


def _adamw(g, w, m, v, name):
    return _adamw_many([(g, w, m, v)], name)[0][0]


def _part_rows(shape):
    return -(-math.prod(shape) // (8 * LANES)) * 8


def _pack_rows(arrs, total_rows):
    parts = []
    for a in arrs:
        flat = a.reshape(-1)
        rows = _part_rows(a.shape)
        parts.append(jnp.pad(flat, (0, rows * LANES - flat.shape[0])).reshape(rows, LANES))
    packed = jnp.concatenate(parts, axis=0)
    return jnp.pad(packed, ((0, total_rows - packed.shape[0]), (0, 0)))


def _unpack_rows(packed, shapes):
    out, r = [], 0
    for shp in shapes:
        size, rows = math.prod(shp), _part_rows(shp)
        out.append(packed[r:r + rows].reshape(-1)[:size].reshape(shp))
        r += rows
    return out


def _rows_of(shapes):
    return sum(_part_rows(s) for s in shapes)


_GATHER_STAGES = {"start": ["wa", "mkv0", "wo0"], "mixer_a_fwd": ["wup0", "wdown0"],
                  "mlp_fwd_0": ["wq", "wkv", "mkv1", "wo1"], "fox_fwd": ["wup1", "wdown1"]}
_SWAP_STAGES = {"mixer_b_bwd": ["wup1", "wdown1"]}
_REDUCE_STAGES = {"fox_bwd": ["wup1", "wdown1"], "mlp_bwd_0": ["wo1", "wq", "wkv", "mkv1"],
                  "mixer_a_bwd": ["wup0", "wdown0"], "end": ["wa", "wo0", "mkv0", "small"]}
_SMALL = ["ws", "b_s", "sg", "sb", "bf", "ln_g", "ln_b", "loss"]


class _Staged:
    def __init__(self, placed, sel, dm, ng, shard_cols, n_small):
        self.placed, self.sel, self.dm, self.ng, self.shard_cols, self.n_small = placed, sel, dm, ng, shard_cols, n_small
        self.g4, self.recv1, self.recv2 = {}, {}, {}

    def _weight(self, name, a):
        a = a.reshape(N_CHIPS, 2 * a.shape[2], a.shape[3])
        if name in ("wa", "wkv"):
            a = a.transpose(1, 0, 2).reshape(a.shape[1], -1)
            if name == "wkv":
                a = jnp.concatenate([a[:, :2 * self.dm],
                                     jnp.pad(a[:, 2 * self.dm:], ((0, 0), (0, LANES - self.ng)))], axis=1)
            return a
        if name.startswith("wup") or name.startswith("wdown"):
            return a
        return a.reshape(-1, a.shape[2])

    def _partial(self, name, grads):
        if name == "small":
            a = _pack_rows([grads[n] for n in _SMALL], self.n_small).reshape(N_CHIPS, -1, LANES)
        elif name in self.shard_cols:
            n = self.shard_cols[name]
            g = grads[name][:, :N_CHIPS * n]
            a = g.reshape(g.shape[0], N_CHIPS, n).transpose(1, 0, 2)
        elif name.startswith("wup") or name.startswith("wdown"):
            a = grads[name]
        else:
            g = grads[name]
            a = g.reshape(N_CHIPS, g.shape[0] // N_CHIPS, g.shape[1])
        return a.reshape(N_CHIPS, 2, a.shape[1] // 2, a.shape[2])

    def plan(self, stage, w, grads):
        if stage in _GATHER_STAGES:
            return _gather_weights([self.placed[n] for n in _GATHER_STAGES[stage]])
        if stage in _SWAP_STAGES:
            for n in _SWAP_STAGES[stage]:
                self.g4[n] = self._partial(n, grads)
            return _pair_swap([self.g4[n] for n in _SWAP_STAGES[stage]])
        if stage in _REDUCE_STAGES:
            names = _REDUCE_STAGES[stage]
            late = [n for n in names if n not in self.recv1]
            for n in late:
                self.g4[n] = self._partial(n, grads)
            if late:
                got = _run_comm(_pair_swap([self.g4[n] for n in late]), "reduce_pair_swap_" + stage)
                self.recv1.update(zip(late, got))
            return _chip_exchange([_pair_add(self.g4[n], self.recv1[n], self.sel, F32 if n == "small" else BF16,
                                             "reduce_pair_add_" + n) for n in names])
        return None

    def done(self, stage, results, w):
        if stage in _GATHER_STAGES:
            for n, a in zip(_GATHER_STAGES[stage], results):
                w[n] = self._weight(n, a)
        elif stage in _SWAP_STAGES:
            self.recv1.update(zip(_SWAP_STAGES[stage], results))
        elif stage in _REDUCE_STAGES:
            self.recv2.update(zip(_REDUCE_STAGES[stage], results))


def kernel(x, mem, a_w_in, a_sgu_ln_g, a_sgu_ln_b, a_w_s, a_b_s, kv_w, kv_b_f, b_w_q, mem_w_kv, w_o, ln_g, ln_b, w_up, w_down, loss_target, m_a_w_in, m_a_sgu_ln_g, m_a_sgu_ln_b, m_a_w_s, m_a_b_s, m_kv_w, m_kv_b_f, m_b_w_q, m_mem_w_kv, m_w_o, m_ln_g, m_ln_b, m_w_up, m_w_down, v_a_w_in, v_a_sgu_ln_g, v_a_sgu_ln_b, v_a_w_s, v_a_b_s, v_kv_w, v_kv_b_f, v_b_w_q, v_mem_w_kv, v_w_o, v_ln_g, v_ln_b, v_w_up, v_w_down):
    xi, yi, ci = lax.axis_index("x"), lax.axis_index("y"), lax.axis_index("c")
    chip = 2 * xi + yi
    sel = jnp.stack([chip, ci]).astype(jnp.int32)
    S, D = x.shape[1], x.shape[2]
    DQ = mem_w_kv.shape[2] // 2
    DM = D - DQ
    NG = DM // HEAD_DIM

    shards = {"wa": (a_w_in, 0), "wkv": (kv_w[None], 0), "wq": (b_w_q, 0), "mkv0": (mem_w_kv, 0), "mkv1": (mem_w_kv, 1),
              "wo0": (w_o, 0), "wo1": (w_o, 1), "wup0": (w_up, 0), "wup1": (w_up, 1), "wdown0": (w_down, 0),
              "wdown1": (w_down, 1)}
    placed = {n: _cast_place(s.reshape(s.shape[0], 2, s.shape[1] // 2, s.shape[2]), layer, sel, "cast_place_" + n)
              for n, (s, layer) in shards.items()}
    ln_pack = jnp.concatenate([ln_g.reshape(4, -1), ln_b.reshape(4, -1)], axis=0)
    ln_buf = lax.dynamic_update_slice(jnp.zeros((N_CHIPS,) + ln_pack.shape, F32), ln_pack[None], (chip, 0, 0))
    ln_shape = (4, D)
    small_shapes = [a_w_s.shape, a_b_s.shape, a_sgu_ln_g.shape, a_sgu_ln_b.shape, kv_b_f.shape, ln_shape, ln_shape,
                    (1, LANES)]
    n_small = -(-_rows_of(small_shapes) // 64) * 64
    hooks = _Staged(placed, sel, DM, NG, {"wa": a_w_in.shape[2], "wkv": kv_w.shape[1]}, n_small)
    weights = {"sg": a_sgu_ln_g, "sb": a_sgu_ln_b, "ws": a_w_s[0], "b_s": a_b_s[0], "bf": kv_b_f}
    got = _run_comm(_gather_weights([placed[n] for n in _GATHER_STAGES["start"]], whole=[ln_buf]),
                    "gather_weights_first")
    hooks.done("start", got[:-1], weights)
    ln_full = got[-1].transpose(1, 0, 2).reshape(8, D)
    weights["ln_g"], weights["ln_b"] = ln_full[:4], ln_full[4:]

    _, grad_x, gr = _local_step(x[0], mem[0], loss_target[0], weights, hooks)

    def chip_sums(groups):
        bufs = []
        for grp in groups:
            buf = None
            for layer, n in enumerate(grp):
                buf = _chip_reduce(hooks.g4[n], hooks.recv1[n], hooks.recv2[n], sel, buf, layer, len(grp),
                                   "reduce_chip_sum_" + n)
            bufs.append(buf)
        return bufs

    whole = lambda a: a.reshape(a.shape[0], 2 * a.shape[3], a.shape[4])
    flat = lambda a: a.reshape(-1, a.shape[-1])

    def update(name, g, w, m, v):
        d, mn, vn = _adamw(flat(g), flat(w), flat(m), flat(v), "adamw_" + name)
        return d.reshape(w.shape), mn.reshape(w.shape), vn.reshape(w.shape)

    hooks.done("end", _run_comm(hooks.plan("end", weights, gr), "reduce_chip_exchange_end"), weights)
    small_mine = _chip_reduce(hooks.g4["small"], hooks.recv1["small"], hooks.recv2["small"], sel, None, 0, 1,
                              "reduce_chip_sum_small", by_chip=True)
    red, small_all = _pair_gather(chip_sums([["wa"], ["wkv"], ["wq"], ["mkv0", "mkv1"], ["wo0", "wo1"],
                                             ["wup0", "wup1"], ["wdown0", "wdown1"]]), small_mine, "reduce_pair_gather")
    g_a_w_in, g_kv_w, g_b_w_q, g_mem_w_kv, g_w_o, g_w_up, g_w_down = [whole(a) for a in red]
    g_kv_w = g_kv_w[0]
    g_w_down = g_w_down.transpose(0, 2, 1)
    small_all = small_all.reshape(n_small, LANES)
    g_ws, g_bs, g_sg, g_sb, g_bf, g_lng, g_lnb, loss_row = _unpack_rows(small_all, small_shapes)
    loss = loss_row[0, 0]
    dsh = D // N_CHIPS
    g_ln_g = lax.dynamic_slice_in_dim(g_lng, chip * dsh, dsh, axis=1).reshape(ln_g.shape)
    g_ln_b = lax.dynamic_slice_in_dim(g_lnb, chip * dsh, dsh, axis=1).reshape(ln_b.shape)

    upd = {
        "a_w_in": update("a_w_in", g_a_w_in, a_w_in, m_a_w_in, v_a_w_in),
        "kv_w": update("kv_w", g_kv_w, kv_w, m_kv_w, v_kv_w),
        "b_w_q": update("b_w_q", g_b_w_q, b_w_q, m_b_w_q, v_b_w_q),
        "mem_w_kv": update("mem_w_kv", g_mem_w_kv, mem_w_kv, m_mem_w_kv, v_mem_w_kv),
        "w_o": update("w_o", g_w_o, w_o, m_w_o, v_w_o),
        "w_up": update("w_up", g_w_up, w_up, m_w_up, v_w_up),
        "w_down": update("w_down", g_w_down, w_down, m_w_down, v_w_down),
    }
    tiny_g = [g_sg, g_sb, g_ws, g_bs, g_bf, g_ln_g, g_ln_b]
    tiny_w = [a_sgu_ln_g, a_sgu_ln_b, a_w_s, a_b_s, kv_b_f, ln_g, ln_b]
    tiny_m = [m_a_sgu_ln_g, m_a_sgu_ln_b, m_a_w_s, m_a_b_s, m_kv_b_f, m_ln_g, m_ln_b]
    tiny_v = [v_a_sgu_ln_g, v_a_sgu_ln_b, v_a_w_s, v_a_b_s, v_kv_b_f, v_ln_g, v_ln_b]
    tiny_shapes = [a.shape for a in tiny_w]
    n_tiny = -(-_rows_of(tiny_shapes) // 8) * 8
    td, tm_, tv = _adamw(_pack_rows(tiny_g, n_tiny), _pack_rows(tiny_w, n_tiny), _pack_rows(tiny_m, n_tiny),
                         _pack_rows(tiny_v, n_tiny), "adamw_small")
    for name, d, mn, vn in zip(["a_sgu_ln_g", "a_sgu_ln_b", "a_w_s", "a_b_s", "kv_b_f", "ln_g", "ln_b"],
                               _unpack_rows(td, tiny_shapes), _unpack_rows(tm_, tiny_shapes),
                               _unpack_rows(tv, tiny_shapes)):
        upd[name] = (d, mn, vn)

    order = ["a_w_in", "a_sgu_ln_g", "a_sgu_ln_b", "a_w_s", "a_b_s", "kv_w", "kv_b_f", "b_w_q", "mem_w_kv", "w_o",
             "ln_g", "ln_b", "w_up", "w_down"]
    grads = {"a_w_in": g_a_w_in.reshape(a_w_in.shape), "a_sgu_ln_g": g_sg.reshape(a_sgu_ln_g.shape),
             "a_sgu_ln_b": g_sb.reshape(a_sgu_ln_b.shape), "a_w_s": g_ws.reshape(a_w_s.shape),
             "a_b_s": g_bs.reshape(a_b_s.shape), "kv_w": g_kv_w, "kv_b_f": g_bf, "b_w_q": g_b_w_q.reshape(b_w_q.shape),
             "mem_w_kv": g_mem_w_kv, "w_o": g_w_o, "ln_g": g_ln_g, "ln_b": g_ln_b, "w_up": g_w_up, "w_down": g_w_down}
    return (loss, grad_x[None], *[grads[n] for n in order], *[upd[n][0] for n in order],
            *[upd[n][1] for n in order], *[upd[n][2] for n in order])
```

```python
import functools
import math

import jax
import jax.numpy as jnp
from jax import lax
from jax.experimental import pallas as pl
from jax.experimental.pallas import tpu as pltpu

F32 = jnp.float32
BF16 = jnp.bfloat16

HEAD_DIM = 64
PAIR = 2 * HEAD_DIM
CHUNK = 128
LN_EPS = 1e-5
ALPHA = 4 ** 0.25
QK_SCALE = 1.0 / math.sqrt(HEAD_DIM)
NEG = -1e30
N_CHIPS = 4
FOX_BLOCK = 256
LANES = 128
VMEM_LIMIT = 48 * 2 ** 20

ADAM_LR, ADAM_B1, ADAM_B2, ADAM_EPS, ADAM_WD, ADAM_STEP = 0.001, 0.9, 0.999, 1e-08, 0.01, 10

MESH = pl.DeviceIdType.MESH
ANY = pl.BlockSpec(memory_space=pl.ANY)


def _dot(a, b):
    return jnp.dot(a, b, preferred_element_type=F32)


def _dot_nt(a, b):
    return lax.dot_general(a, b, (((1,), (1,)), ((), ())), preferred_element_type=F32)


def _dot_tn(a, b):
    return lax.dot_general(a, b, (((0,), (0,)), ((), ())), preferred_element_type=F32)


def _ln_stats(r):
    mu = jnp.mean(r, axis=-1, keepdims=True)
    d = r - mu
    var = jnp.mean(d * d, axis=-1, keepdims=True)
    rstd = lax.rsqrt(var + LN_EPS)
    return d * rstd, rstd


def _ln_bwd(dy, xhat, rstd, g):
    dxh = dy * g
    m1 = jnp.mean(dxh, axis=-1, keepdims=True)
    m2 = jnp.mean(dxh * xhat, axis=-1, keepdims=True)
    return rstd * (dxh - m1 - xhat * m2)


_GELU_K = math.sqrt(2.0 / math.pi)


def _gelu(x):
    return 0.5 * x * (1.0 + jnp.tanh(_GELU_K * (x + 0.044715 * x * x * x)))


def _gelu_grad(x):
    t = jnp.tanh(_GELU_K * (x + 0.044715 * x * x * x))
    return 0.5 * (1.0 + t) + 0.5 * x * (1.0 - t * t) * _GELU_K * (1.0 + 3 * 0.044715 * x * x)


def _lane_mask(e):
    lane = lax.broadcasted_iota(jnp.int32, (1, PAIR), 1)
    return (lane >= HEAD_DIM * e) & (lane < HEAD_DIM * (e + 1))


def _first_head():
    return lax.broadcasted_iota(jnp.int32, (1, PAIR), 1) < HEAD_DIM


def _mem_probs(qh, mkp):
    sc = _dot_nt(qh, mkp) * QK_SCALE
    ex = jnp.exp(sc - jnp.max(sc, axis=-1, keepdims=True))
    return ex / jnp.sum(ex, axis=-1, keepdims=True)


def _mem_attn_fwd(qm_b, mkv_ref, dq_dim):
    outs = []
    for mp in range(dq_dim // PAIR):
        qp = qm_b[:, mp * PAIR:(mp + 1) * PAIR]
        mkp = mkv_ref[:, mp * PAIR:(mp + 1) * PAIR]
        mvp = mkv_ref[:, dq_dim + mp * PAIR:dq_dim + (mp + 1) * PAIR]
        heads = []
        for e in (0, 1):
            qh = jnp.where(_lane_mask(e), qp, jnp.zeros_like(qp))
            p = _mem_probs(qh, mkp)
            heads.append(_dot(p.astype(BF16), mvp))
        outs.append(jnp.where(_first_head(), heads[0], heads[1]))
    return outs


def _mem_attn_bwd(qm_b, mkv_ref, dmo_pairs, dmkv_ref, dq_dim):
    dqs = []
    for mp in range(dq_dim // PAIR):
        ks = slice(mp * PAIR, (mp + 1) * PAIR)
        vs = slice(dq_dim + mp * PAIR, dq_dim + (mp + 1) * PAIR)
        qp = qm_b[:, ks]
        mkp = mkv_ref[:, ks]
        mvp = mkv_ref[:, vs]
        dmo_b = dmo_pairs[mp].astype(BF16)
        dq = None
        dmk = None
        dmv = None
        for e in (0, 1):
            hm = _lane_mask(e)
            qh = jnp.where(hm, qp, jnp.zeros_like(qp))
            p = _mem_probs(qh, mkp)
            doh = jnp.where(hm, dmo_b, jnp.zeros_like(dmo_b))
            dp = _dot_nt(doh, mvp)
            ds = p * (dp - jnp.sum(dp * p, axis=-1, keepdims=True))
            dsb = (ds * QK_SCALE).astype(BF16)
            kh = jnp.where(hm, mkp, jnp.zeros_like(mkp))
            dq_e = _dot(dsb, kh)
            dmk_e = _dot_tn(dsb, qh)
            dmv_e = _dot_tn(p.astype(BF16), doh)
            dq = dq_e if dq is None else dq + dq_e
            dmk = dmk_e if dmk is None else dmk + dmk_e
            dmv = dmv_e if dmv is None else dmv + dmv_e
        dmkv_ref[:, ks] += dmk
        dmkv_ref[:, vs] += dmv
        dqs.append(dq)
    return dqs


def _params(n_axes=1):
    return pltpu.CompilerParams(dimension_semantics=("arbitrary",) * n_axes, vmem_limit_bytes=VMEM_LIMIT)


def _full(shape):
    return pl.BlockSpec(shape, lambda *_: (0,) * len(shape))


def _rows(tm, cols):
    return pl.BlockSpec((tm, cols), lambda i: (i, 0))


def _cols(rows, tm):
    return pl.BlockSpec((rows, tm), lambda i: (0, i))


def _pick(n, pref):
    if n <= pref:
        return n
    best = LANES
    for t in range(LANES, pref + 1, LANES):
        if n % t == 0:
            best = t
    return best if 2 * best >= pref or n > 2 * pref else n


class _Comm:
    def __init__(self, ins, out_shapes, aliases, sem_counts, start, finish, middle=None):
        self.ins, self.out_shapes, self.aliases, self.sem_counts = list(ins), list(out_shapes), dict(aliases), sem_counts
        self.start, self.finish = start, finish
        self.middle = middle


def _pcall(body, *, name, grid, in_specs, out_specs, out_shape, args, scratch_shapes=(), comm=None, aliases=None):
    n_in, n_out, n_scr = len(in_specs), len(out_shape), len(scratch_shapes)
    aliases = dict(aliases or {})
    if comm is None:
        res = pl.pallas_call(body, name=name, grid=grid, out_shape=tuple(out_shape), in_specs=list(in_specs),
                             out_specs=tuple(out_specs), scratch_shapes=list(scratch_shapes),
                             input_output_aliases=aliases, compiler_params=_params())(*args)
        return tuple(res), ()
    nci, nco = len(comm.ins), len(comm.out_shapes)
    last = grid[0] - 1

    def wrapped(*refs):
        ins, refs = refs[:n_in], refs[n_in:]
        cins, refs = refs[:nci], refs[nci:]
        outs, refs = refs[:n_out], refs[n_out:]
        couts, refs = refs[:nco], refs[nco:]
        scr, sems = refs[:n_scr], refs[n_scr:]

        @pl.when(pl.program_id(0) == 0)
        def _():
            comm.start(cins, couts, sems)

        if comm.middle is not None:
            @pl.when(pl.program_id(0) == (last + 1) // 2)
            def _():
                comm.middle(cins, couts, sems)

        body(*ins, *outs, *scr)

        @pl.when(pl.program_id(0) == last)
        def _():
            comm.finish(cins, couts, sems)

    res = pl.pallas_call(
        wrapped, name=name, grid=grid, out_shape=tuple(out_shape) + tuple(comm.out_shapes),
        in_specs=list(in_specs) + [ANY] * nci, out_specs=tuple(out_specs) + (ANY,) * nco,
        input_output_aliases={**aliases, **{n_in + a: n_out + b for a, b in comm.aliases.items()}},
        scratch_shapes=list(scratch_shapes) + [pltpu.SemaphoreType.DMA((k,)) for k in comm.sem_counts],
        compiler_params=pltpu.CompilerParams(dimension_semantics=("arbitrary",), vmem_limit_bytes=VMEM_LIMIT,
                                             has_side_effects=True),
    )(*args, *comm.ins)
    return tuple(res[:n_out]), tuple(res[n_out:])


def _run_comm(comm, name):
    nci, nco = len(comm.ins), len(comm.out_shapes)

    def body(*refs):
        cins, couts, sems = refs[:nci], refs[nci:nci + nco], refs[nci + nco:]
        comm.start(cins, couts, sems)
        if comm.middle is not None:
            comm.middle(cins, couts, sems)
        comm.finish(cins, couts, sems)

    return pl.pallas_call(
        body, name=name, out_shape=tuple(comm.out_shapes), in_specs=[ANY] * nci, out_specs=(ANY,) * nco,
        input_output_aliases=comm.aliases, scratch_shapes=[pltpu.SemaphoreType.DMA((k,)) for k in comm.sem_counts],
        compiler_params=pltpu.CompilerParams(has_side_effects=True),
    )(*comm.ins)


def _mm(a, b, name, a_is_transposed, out_split=1):
    (M, R) = a.shape if a_is_transposed else a.shape[::-1]
    _, N = b.shape
    bm, br = _pick(M, 1024 if a_is_transposed else 512), _pick(R, 1024 if a_is_transposed else 512)
    ncol = N // out_split
    bn = _pick(ncol, 1024)
    per = ncol // bn
    last = R // br - 1

    def body(a_ref, b_ref, o_ref, acc_ref):
        @pl.when(pl.program_id(2) == 0)
        def _():
            acc_ref[...] = jnp.zeros_like(acc_ref)

        acc_ref[...] += (_dot if a_is_transposed else _dot_tn)(a_ref[...], b_ref[...])

        @pl.when(pl.program_id(2) == last)
        def _():
            o_ref[...] = acc_ref[...].astype(BF16)

    if out_split == 1:
        out_shape = jax.ShapeDtypeStruct((M, N), BF16)
        out_spec = pl.BlockSpec((bm, bn), lambda i, j, r: (i, j))
    else:
        out_shape = jax.ShapeDtypeStruct((out_split, M, ncol), BF16)
        out_spec = pl.BlockSpec((None, bm, bn), lambda i, j, r: (j // per, i, j % per))
    a_spec = (pl.BlockSpec((bm, br), lambda i, j, r: (i, r)) if a_is_transposed
              else pl.BlockSpec((br, bm), lambda i, j, r: (r, i)))
    return pl.pallas_call(
        body, name=name, out_shape=out_shape, grid=(M // bm, N // bn, R // br),
        in_specs=[a_spec, pl.BlockSpec((br, bn), lambda i, j, r: (r, j))], out_specs=out_spec,
        scratch_shapes=[pltpu.VMEM((bm, bn), F32)],
        compiler_params=pltpu.CompilerParams(dimension_semantics=("parallel", "parallel", "arbitrary"),
                                             vmem_limit_bytes=VMEM_LIMIT),
    )(a, b)


def _mm_tn(a, b, name, out_split=1):
    return _mm(a, b, name, False, out_split)


def _mm_nn(at, b, name, out_split=1):
    return _mm(at, b, name, True, out_split)


def _mm_small(a, b, name):
    def body(a_ref, b_ref, o_ref):
        o_ref[...] = _dot(a_ref[...], b_ref[...]).astype(BF16)

    return pl.pallas_call(body, name=name, out_shape=jax.ShapeDtypeStruct((a.shape[0], b.shape[1]), BF16),
                          compiler_params=pltpu.CompilerParams(vmem_limit_bytes=VMEM_LIMIT))(a, b)


def _mlp_out(x_ref, wup_ref, wdown_ref, g_ref, b_ref, r_ref, h_ref):
    NC, _, FC = wup_ref.shape
    xv = x_ref[...]
    xb = xv.astype(BF16)
    y = jnp.zeros(xv.shape, F32)
    for c in range(NC):
        a = jnp.maximum(_dot(xb, wup_ref[c]), 0.0)
        hb = (a * a).astype(BF16)
        h_ref[:, c * FC:(c + 1) * FC] = hb
        y = y + _dot(hb, wdown_ref[c])
    r = ALPHA * xv + y
    r_ref[...] = r
    xhat, _ = _ln_stats(r)
    return xhat * g_ref[...] + b_ref[...]


def _mlp_fwd_loss(x, wup4, wdown4, g, b, target, name):
    S, D = x.shape
    NC, _, FC = wup4.shape
    tm = min(256, S)

    def body(x_ref, wup_ref, wdown_ref, g_ref, b_ref, t_ref, d_ref, r_ref, h_ref, l_ref):
        @pl.when(pl.program_id(0) == 0)
        def _():
            l_ref[...] = jnp.zeros_like(l_ref)

        err = _mlp_out(x_ref, wup_ref, wdown_ref, g_ref, b_ref, r_ref, h_ref) - t_ref[...]
        d_ref[...] = err * (1.0 / D)
        l_ref[...] += (0.5 / D) * jnp.sum(err * err)

    return pl.pallas_call(
        body, name=name, grid=(S // tm,),
        out_shape=(jax.ShapeDtypeStruct((S, D), F32), jax.ShapeDtypeStruct((S, D), F32),
                   jax.ShapeDtypeStruct((S, NC * FC), BF16), jax.ShapeDtypeStruct((8, LANES), F32)),
        in_specs=[_rows(tm, D), _full(wup4.shape), _full(wdown4.shape), _full((1, D)), _full((1, D)), _rows(tm, D)],
        out_specs=(_rows(tm, D), _rows(tm, D), _rows(tm, NC * FC), _full((8, LANES))),
        compiler_params=_params(),
    )(x, wup4, wdown4, g, b, target)


def _mlp_fwd(x, wup4, wdown4, g, b, name, comm=None):
    S, D = x.shape
    NC, _, FC = wup4.shape
    tm = min(256, S)

    def body(x_ref, wup_ref, wdown_ref, g_ref, b_ref, xo_ref, xb_ref, xbt_ref, r_ref, h_ref):
        xo = _mlp_out(x_ref, wup_ref, wdown_ref, g_ref, b_ref, r_ref, h_ref)
        xo_ref[...] = xo
        xb_ref[...] = xo.astype(BF16)
        xbt_ref[...] = xo.T.astype(BF16)

    return _pcall(
        body, name=name, grid=(S // tm,), comm=comm,
        out_shape=(jax.ShapeDtypeStruct((S, D), F32), jax.ShapeDtypeStruct((S, D), BF16),
                   jax.ShapeDtypeStruct((D, S), BF16), jax.ShapeDtypeStruct((S, D), F32),
                   jax.ShapeDtypeStruct((S, NC * FC), BF16)),
        in_specs=[_rows(tm, D), _full(wup4.shape), _full(wdown4.shape), _full((1, D)), _full((1, D))],
        out_specs=(_rows(tm, D), _rows(tm, D), _cols(D, tm), _rows(tm, D), _rows(tm, NC * FC)),
        args=(x, wup4, wdown4, g, b))


def _mlp_bwd(dxo, r, h, wup4, wdown4, g, name, comm=None):
    S, D = r.shape
    NC, _, FC = wup4.shape
    tm = min(256, S)

    def body(dxo_ref, r_ref, h_ref, wup_ref, wdown_ref, g_ref, dx_ref, drbt_ref, dpre_ref, dgb_ref):
        @pl.when(pl.program_id(0) == 0)
        def _():
            dgb_ref[...] = jnp.zeros_like(dgb_ref)

        xhat, rstd = _ln_stats(r_ref[...])
        dy = dxo_ref[...]
        dgb_ref[0:1, :] += jnp.sum(dy * xhat, axis=0, keepdims=True)
        dgb_ref[1:2, :] += jnp.sum(dy, axis=0, keepdims=True)
        dr = _ln_bwd(dy, xhat, rstd, g_ref[...])
        drb = dr.astype(BF16)
        drbt_ref[...] = dr.T.astype(BF16)
        dx = ALPHA * dr
        for c in range(NC):
            dh = _dot_nt(drb, wdown_ref[c])
            a = jnp.sqrt(h_ref[:, c * FC:(c + 1) * FC].astype(F32))
            dpre = (2.0 * a * dh).astype(BF16)
            dpre_ref[:, c * FC:(c + 1) * FC] = dpre
            dx = dx + _dot_nt(dpre, wup_ref[c])
        dx_ref[...] = dx

    return _pcall(
        body, name=name, grid=(S // tm,), comm=comm,
        out_shape=(jax.ShapeDtypeStruct((S, D), F32), jax.ShapeDtypeStruct((D, S), BF16),
                   jax.ShapeDtypeStruct((S, NC * FC), BF16), jax.ShapeDtypeStruct((2, D), F32)),
        in_specs=[_rows(tm, D), _rows(tm, D), _rows(tm, NC * FC), _full(wup4.shape), _full(wdown4.shape),
                  _full((1, D))],
        out_specs=(_rows(tm, D), _cols(D, tm), _rows(tm, NC * FC), _full((2, D))),
        args=(dxo, r, h, wup4, wdown4, g))


def _tail_fwd_store(xv, cat_ref, wo_ref, g_ref, b_ref, xo_ref, xbt_ref, r_ref, catt_ref):
    cat = cat_ref[...]
    y = _dot(cat, wo_ref[...])
    r = ALPHA * xv + y
    xhat, _ = _ln_stats(r)
    xo = xhat * g_ref[...] + b_ref[...]
    xo_ref[...] = xo
    xbt_ref[...] = xo.T.astype(BF16)
    catt_ref[...] = cat.astype(F32).T.astype(BF16)
    r_ref[...] = r


def _tail_bwd_head(dxo_ref, r_ref, g_ref, wo_ref, dgb_ref, drb_ref):
    xhat, rstd = _ln_stats(r_ref[...])
    dy = dxo_ref[...]
    dgb_ref[0:1, :] += jnp.sum(dy * xhat, axis=0, keepdims=True)
    dgb_ref[1:2, :] += jnp.sum(dy, axis=0, keepdims=True)
    dr = _ln_bwd(dy, xhat, rstd, g_ref[...])
    drb = dr.astype(BF16)
    drb_ref[...] = drb
    return dr, _dot_nt(drb, wo_ref[...])


def _causal(shape):
    return lax.broadcasted_iota(jnp.int32, shape, 1) <= lax.broadcasted_iota(jnp.int32, shape, 0)


def _sgu_mixed(ws_ref, bst_ref, gp, vpair):
    tril = _causal((CHUNK, CHUNK))
    w0 = jnp.where(tril, ws_ref[2 * gp], 0.0).astype(BF16)
    w1 = jnp.where(tril, ws_ref[2 * gp + 1], 0.0).astype(BF16)
    m0 = _dot(w0, vpair) + bst_ref[:, 2 * gp:2 * gp + 1]
    m1 = _dot(w1, vpair) + bst_ref[:, 2 * gp + 1:2 * gp + 2]
    return jnp.where(_first_head(), m0, m1), w0, w1


def _mixer_a_fwd(x, wa, sg, sb, ws, bst, mkv, wo, g, b, name, comm=None):
    S, D = x.shape
    DQ = mkv.shape[1] // 2
    DM = D - DQ
    NP = DM // PAIR
    tm = min(512, S)

    def body(x_ref, wa_ref, sg_ref, sb_ref, ws_ref, bst_ref, mkv_ref, wo_ref, g_ref, b_ref,
             xo_ref, xbt_ref, r_ref, catt_ref, xint_ref, zb_ref, z_s, zv_s, cat_ref):
        xv = x_ref[...]
        xint_ref[...] = xv.T.astype(BF16)
        z_s[...] = _dot(xv.astype(BF16), wa_ref[...])
        zb_ref[...] = z_s[...].astype(BF16)
        vhat, _ = _ln_stats(_gelu(z_s[:, DM:2 * DM]))
        zv_s[...] = (vhat * sg_ref[...] + sb_ref[...]).astype(BF16)
        for c in range(tm // CHUNK):
            rs = slice(c * CHUNK, (c + 1) * CHUNK)
            for gp in range(NP):
                ls = slice(gp * PAIR, (gp + 1) * PAIR)
                mixed, _, _ = _sgu_mixed(ws_ref, bst_ref, gp, zv_s[rs, ls])
                cat_ref[rs, ls] = (_gelu(z_s[rs, ls]) * mixed).astype(BF16)
        mo = _mem_attn_fwd(z_s[:, 2 * DM:].astype(BF16), mkv_ref, DQ)
        for mp in range(DQ // PAIR):
            cat_ref[:, DM + mp * PAIR:DM + (mp + 1) * PAIR] = mo[mp].astype(BF16)
        _tail_fwd_store(xv, cat_ref, wo_ref, g_ref, b_ref, xo_ref, xbt_ref, r_ref, catt_ref)

    return _pcall(
        body, name=name, grid=(S // tm,), comm=comm,
        out_shape=(jax.ShapeDtypeStruct((S, D), F32), jax.ShapeDtypeStruct((D, S), BF16),
                   jax.ShapeDtypeStruct((S, D), F32), jax.ShapeDtypeStruct((D, S), BF16),
                   jax.ShapeDtypeStruct((D, S), BF16), jax.ShapeDtypeStruct((S, 2 * DM + DQ), BF16)),
        in_specs=[_rows(tm, D), _full(wa.shape), _full(sg.shape), _full(sb.shape), _full(ws.shape), _full(bst.shape),
                  _full(mkv.shape), _full(wo.shape), _full((1, D)), _full((1, D))],
        out_specs=(_rows(tm, D), _cols(D, tm), _rows(tm, D), _cols(D, tm), _cols(D, tm), _rows(tm, 2 * DM + DQ)),
        scratch_shapes=[pltpu.VMEM((tm, 2 * DM + DQ), F32), pltpu.VMEM((tm, DM), BF16), pltpu.VMEM((tm, D), BF16)],
        args=(x, wa, sg, sb, ws, bst, mkv, wo, g, b))


def _mixer_a_bwd(zb, dxo, r, wa, sg, sb, ws, bst, mkv, wo, g, name, comm=None):
    S, D = r.shape
    DQ = mkv.shape[1] // 2
    DM = D - DQ
    NP = DM // PAIR
    NG = DM // HEAD_DIM
    tm = min(512, S)

    def body(zb_ref, dxo_ref, r_ref, wa_ref, sg_ref, sb_ref, ws_ref, bst_ref, mkv_ref, wo_ref, g_ref,
             dx_ref, dz_ref, drb_ref, dgb_ref, dsgb_ref, dws_ref, dbst_ref, dmkv_ref, z_s, zv_s, dzv_s):
        @pl.when(pl.program_id(0) == 0)
        def _():
            dgb_ref[...] = jnp.zeros_like(dgb_ref)
            dsgb_ref[...] = jnp.zeros_like(dsgb_ref)
            dws_ref[...] = jnp.zeros_like(dws_ref)
            dbst_ref[...] = jnp.zeros_like(dbst_ref)
            dmkv_ref[...] = jnp.zeros_like(dmkv_ref)

        dr, dcat = _tail_bwd_head(dxo_ref, r_ref, g_ref, wo_ref, dgb_ref, drb_ref)
        z_s[...] = zb_ref[...].astype(F32)
        vhat, vrstd = _ln_stats(_gelu(z_s[:, DM:2 * DM]))
        zv_s[...] = (vhat * sg_ref[...] + sb_ref[...]).astype(BF16)
        tril = _causal((CHUNK, CHUNK))
        lane = lax.broadcasted_iota(jnp.int32, (1, LANES), 1)
        for c in range(tm // CHUNK):
            rs = slice(c * CHUNK, (c + 1) * CHUNK)
            for gp in range(NP):
                ls = slice(gp * PAIR, (gp + 1) * PAIR)
                vpair = zv_s[rs, ls]
                mixed, w0, w1 = _sgu_mixed(ws_ref, bst_ref, gp, vpair)
                u_pre = z_s[rs, ls]
                dmix = dcat[rs, ls]
                dz_ref[rs, ls] = (dmix * mixed * _gelu_grad(u_pre)).astype(BF16)
                dmixed = dmix * _gelu(u_pre)
                first = _first_head()
                d0 = jnp.where(first, dmixed, 0.0)
                d1 = jnp.where(first, 0.0, dmixed)
                d0b = d0.astype(BF16)
                d1b = d1.astype(BF16)
                dzv_s[rs, ls] = _dot_tn(w0, d0b) + _dot_tn(w1, d1b)
                dws_ref[2 * gp] += jnp.where(tril, _dot_nt(d0b, vpair), 0.0)
                dws_ref[2 * gp + 1] += jnp.where(tril, _dot_nt(d1b, vpair), 0.0)
                dbst_ref[...] += (jnp.where(lane == 2 * gp, jnp.sum(d0, axis=-1, keepdims=True), 0.0)
                                  + jnp.where(lane == 2 * gp + 1, jnp.sum(d1, axis=-1, keepdims=True), 0.0))
        dzv = dzv_s[...]
        dsgb_ref[0:1, :] += jnp.sum(dzv * vhat, axis=0, keepdims=True)
        dsgb_ref[1:2, :] += jnp.sum(dzv, axis=0, keepdims=True)
        dgv = _ln_bwd(dzv, vhat, vrstd, sg_ref[...])
        dz_ref[:, DM:2 * DM] = (dgv * _gelu_grad(z_s[:, DM:2 * DM])).astype(BF16)
        dmo = [dcat[:, DM + mp * PAIR:DM + (mp + 1) * PAIR] for mp in range(DQ // PAIR)]
        dqm = _mem_attn_bwd(z_s[:, 2 * DM:].astype(BF16), mkv_ref, dmo, dmkv_ref, DQ)
        for mp in range(DQ // PAIR):
            dz_ref[:, 2 * DM + mp * PAIR:2 * DM + (mp + 1) * PAIR] = dqm[mp].astype(BF16)
        dx_ref[...] = ALPHA * dr + _dot_nt(dz_ref[...], wa_ref[...])

    ZW = 2 * DM + DQ
    M = mkv.shape[0]
    return _pcall(
        body, name=name, grid=(S // tm,), comm=comm,
        out_shape=(jax.ShapeDtypeStruct((S, D), F32), jax.ShapeDtypeStruct((S, ZW), BF16),
                   jax.ShapeDtypeStruct((S, D), BF16), jax.ShapeDtypeStruct((2, D), F32),
                   jax.ShapeDtypeStruct((2, DM), F32), jax.ShapeDtypeStruct((NG, CHUNK, CHUNK), F32),
                   jax.ShapeDtypeStruct((CHUNK, LANES), F32), jax.ShapeDtypeStruct((M, 2 * DQ), F32)),
        in_specs=[_rows(tm, ZW), _rows(tm, D), _rows(tm, D), _full(wa.shape), _full(sg.shape), _full(sb.shape),
                  _full(ws.shape), _full(bst.shape), _full(mkv.shape), _full(wo.shape), _full((1, D))],
        out_specs=(_rows(tm, D), _rows(tm, ZW), _rows(tm, D), _full((2, D)), _full((2, DM)),
                   _full((NG, CHUNK, CHUNK)), _full((CHUNK, LANES)), _full((M, 2 * DQ))),
        scratch_shapes=[pltpu.VMEM((tm, ZW), F32), pltpu.VMEM((tm, DM), BF16), pltpu.VMEM((tm, DM), F32)],
        args=(zb, dxo, r, wa, sg, sb, ws, bst, mkv, wo, g))


def _proj_b_fwd(xb, wq, wkv, dm, name):
    S, D = xb.shape
    DQ = D - dm
    tm = min(512, S)

    def body(x_ref, wq_ref, wkv_ref, q_ref, qm_ref, k_ref, v_ref, fl_ref):
        xv = x_ref[...]
        z = _dot(xv, wq_ref[...])
        q_ref[...] = z[:, :dm].astype(BF16)
        qm_ref[...] = z[:, dm:].astype(BF16)
        kvf = _dot(xv, wkv_ref[...])
        k_ref[...] = kvf[:, :dm].astype(BF16)
        v_ref[...] = kvf[:, dm:2 * dm].astype(BF16)
        fl_ref[...] = kvf[:, 2 * dm:]

    return pl.pallas_call(
        body, name=name, grid=(S // tm,),
        out_shape=(jax.ShapeDtypeStruct((S, dm), BF16), jax.ShapeDtypeStruct((S, DQ), BF16),
                   jax.ShapeDtypeStruct((S, dm), BF16), jax.ShapeDtypeStruct((S, dm), BF16),
                   jax.ShapeDtypeStruct((S, LANES), F32)),
        in_specs=[_rows(tm, D), _full(wq.shape), _full(wkv.shape)],
        out_specs=(_rows(tm, dm), _rows(tm, DQ), _rows(tm, dm), _rows(tm, dm), _rows(tm, LANES)),
        compiler_params=_params(),
    )(xb, wq, wkv)


def _proj_b_bwd(dr, dqz, dk, dv, dfl, wq, wkv, name):
    S, D = dr.shape
    dm = dk.shape[1]
    tm = min(512, S)

    def body(dr_ref, dqz_ref, dk_ref, dv_ref, dfl_ref, wq_ref, wkv_ref, dx_ref):
        dx_ref[...] = (ALPHA * dr_ref[...] + _dot_nt(dqz_ref[...], wq_ref[...])
                       + _dot_nt(dk_ref[...], wkv_ref[:, :dm]) + _dot_nt(dv_ref[...], wkv_ref[:, dm:2 * dm])
                       + _dot_nt(dfl_ref[...], wkv_ref[:, 2 * dm:]))

    return pl.pallas_call(
        body, name=name, grid=(S // tm,), out_shape=jax.ShapeDtypeStruct((S, D), F32),
        in_specs=[_rows(tm, D), _rows(tm, dqz.shape[1]), _rows(tm, dm), _rows(tm, dm), _rows(tm, dfl.shape[1]),
                  _full(wq.shape), _full(wkv.shape)],
        out_specs=_rows(tm, D), compiler_params=_params(),
    )(dr, dqz, dk, dv, dfl, wq, wkv)


def _mm_nn_cat(at, bs, name):
    M, R = at.shape
    widths = [b.shape[1] for b in bs]
    N = sum(widths)
    br = _pick(R, 1024)
    last = R // br - 1
    nb = len(bs)

    def body(a_ref, *refs):
        b_refs, o_ref, acc_ref = refs[:nb], refs[nb], refs[nb + 1]

        @pl.when(pl.program_id(0) == 0)
        def _():
            acc_ref[...] = jnp.zeros_like(acc_ref)

        a = a_ref[...]
        c0 = 0
        for b_ref, wd in zip(b_refs, widths):
            acc_ref[:, c0:c0 + wd] += _dot(a, b_ref[...])
            c0 += wd

        @pl.when(pl.program_id(0) == last)
        def _():
            o_ref[...] = acc_ref[...].astype(BF16)

    return pl.pallas_call(
        body, name=name, grid=(R // br,), out_shape=jax.ShapeDtypeStruct((M, N), BF16),
        in_specs=[pl.BlockSpec((M, br), lambda r: (0, r))] + [pl.BlockSpec((br, wd), lambda r: (r, 0)) for wd in widths],
        out_specs=_full((M, N)), scratch_shapes=[pltpu.VMEM((M, N), F32)], compiler_params=_params(),
    )(at, *bs)


def _split3(v):
    hi = v.astype(BF16)
    r1 = v - hi.astype(F32)
    mid = r1.astype(BF16)
    lo = (r1 - mid.astype(F32)).astype(BF16)
    return hi, mid, lo


def _tri_sum(v, tri_b):
    hi, mid, lo = _split3(v)
    return _dot(hi, tri_b) + _dot(mid, tri_b) + _dot(lo, tri_b)


def _log_sigmoid(x):
    return jnp.minimum(x, 0.0) - jnp.log(1.0 + jnp.exp(-jnp.abs(x)))


def _forget_cumsum(flt, bf, name):
    H, S = flt.shape
    nchunk = S // LANES

    def body(fl_ref, bf_ref, p_ref):
        upper = (lax.broadcasted_iota(jnp.int32, (LANES, LANES), 0)
                 <= lax.broadcasted_iota(jnp.int32, (LANES, LANES), 1)).astype(BF16)

        def step(n, carry):
            s0 = pl.multiple_of(n * LANES, LANES)
            lf = _log_sigmoid(fl_ref[:, pl.ds(s0, LANES)] + bf_ref[...])
            cs = _tri_sum(lf, upper) + carry
            for i, piece in enumerate(_split3(-cs)):
                p_ref[i, :, pl.ds(s0, LANES)] = piece.astype(F32)
            return cs[:, LANES - 1:LANES]

        lax.fori_loop(0, nchunk, step, jnp.zeros((H, 1), F32))

    return pl.pallas_call(body, name=name, out_shape=jax.ShapeDtypeStruct((3, H, S), F32),
                          compiler_params=pltpu.CompilerParams(vmem_limit_bytes=VMEM_LIMIT))(flt, bf)


def _forget_cumsum_bwd(dct, flt, bf, name):
    H, S = flt.shape
    nchunk = S // LANES

    def body(dc_ref, fl_ref, bf_ref, dfl_ref, dbf_ref):
        lower = (lax.broadcasted_iota(jnp.int32, (LANES, LANES), 0)
                 >= lax.broadcasted_iota(jnp.int32, (LANES, LANES), 1)).astype(BF16)

        def step(n, carry):
            tail, tot = carry
            s0 = pl.multiple_of((nchunk - 1 - n) * LANES, LANES)
            suffix = _tri_sum(dc_ref[:, pl.ds(s0, LANES)], lower) + tail
            xv = fl_ref[:, pl.ds(s0, LANES)] + bf_ref[...]
            dfl = suffix * (1.0 / (1.0 + jnp.exp(xv)))
            dfl_ref[:, pl.ds(s0, LANES)] = dfl
            return suffix[:, 0:1], tot + jnp.sum(dfl, axis=-1, keepdims=True)

        _, tot = lax.fori_loop(0, nchunk, step, (jnp.zeros((H, 1), F32), jnp.zeros((H, 1), F32)))
        dbf_ref[...] = jnp.broadcast_to(tot, (H, LANES))

    return pl.pallas_call(body, name=name,
                          out_shape=(jax.ShapeDtypeStruct((H, S), F32), jax.ShapeDtypeStruct((H, LANES), F32)),
                          compiler_params=pltpu.CompilerParams(vmem_limit_bytes=VMEM_LIMIT))(dct, flt, bf)


N_BIAS = 3


def _bias_ones():
    lane = lax.broadcasted_iota(jnp.int32, (1, PAIR), 1)
    return ((lane & (HEAD_DIM - 1)) < N_BIAS).astype(BF16)


def _fox_fwd(q, k, v, ca, name, comm=None):
    S, DM = q.shape
    NP = DM // PAIR
    T = min(FOX_BLOCK, S)
    nq = S // T

    def body(q_ref, k_ref, v_ref, ca_ref, o_ref, lse_ref):
        tril = _causal((T, T))
        heads = (_lane_mask(0), _lane_mask(1))
        ones3 = _bias_ones()

        def kv_block(kj):
            s0 = pl.multiple_of(kj * T, T)
            kb, vb, cab = k_ref[pl.ds(s0, T), :], v_ref[pl.ds(s0, T), :], ca_ref[pl.ds(s0, T), :]
            return ([jnp.where(heads[e], kb, cab) for e in (0, 1)],
                    [jnp.where(heads[e], vb, jnp.ones_like(vb)) for e in (0, 1)])

        def q_pair(a, _):
            t0s = [pl.multiple_of((2 * a + r) * T, T) for r in (0, 1)]
            qaug = {}
            for r in (0, 1):
                qb = q_ref[pl.ds(t0s[r], T), :] * QK_SCALE
                for e in (0, 1):
                    qaug[r, e] = jnp.where(heads[e], qb, ones3)
            chains = [(r, e) for r in (0, 1) for e in (0, 1)]

            def update(carries, first_block, seen):
                blocks = [kv_block(first_block + kk) for kk in range(1 + max(kk for s in seen for kk, _ in s))]
                out = []
                scores = [[_dot_nt(qaug[r, e], blocks[kk][0][e]) for kk, _ in seen[r]] for r, e in chains]
                for n, (r, e) in enumerate(chains):
                    m, acc = carries[n]
                    tiles = [jnp.where(tril, s, NEG) if masked else s for s, (_, masked) in zip(scores[n], seen[r])]
                    m_new = jnp.maximum(m, jnp.max(functools.reduce(jnp.maximum, tiles), axis=-1, keepdims=True))
                    acc = jnp.exp(m - m_new) * acc
                    for s, (kk, _) in zip(tiles, seen[r]):
                        acc = acc + _dot(jnp.exp(s - m_new).astype(BF16), blocks[kk][1][e])
                    out.append((m_new, acc))
                return tuple(out)

            def full(first_block, n_blocks, carries):
                return update(carries, first_block, [[(kk, False) for kk in range(n_blocks)]] * 2)

            init = (jnp.full((T, 1), NEG, F32), jnp.zeros((T, PAIR), F32))
            carries = lax.fori_loop(0, a // 2, lambda i, c: full(4 * i, 4, c), (init,) * 4)
            carries = lax.fori_loop(0, a % 2, lambda i, c: full(2 * a - 2, 2, c), carries)
            carries = update(carries, 2 * a, [[(0, True)], [(0, False), (1, True)]])
            first = _first_head()
            for r in (0, 1):
                (m0, acc0), (m1, acc1) = carries[2 * r], carries[2 * r + 1]
                l0, l1 = acc0[:, HEAD_DIM:HEAD_DIM + 1], acc1[:, 0:1]
                o_ref[pl.ds(t0s[r], T), :] = jnp.where(first, acc0 / l0, acc1 / l1).astype(BF16)
                lse_t = jnp.where(first, m0 + jnp.log(l0), m1 + jnp.log(l1)).T
                lse_ref[0:1, pl.ds(t0s[r], T)] = lse_t[0:1, :]
                lse_ref[1:2, pl.ds(t0s[r], T)] = lse_t[HEAD_DIM:HEAD_DIM + 1, :]
            return 0

        lax.fori_loop(0, nq // 2, q_pair, 0)

    col = pl.BlockSpec((S, PAIR), lambda hp: (0, hp))
    return _pcall(
        body, name=name, grid=(NP,), comm=comm,
        out_shape=(jax.ShapeDtypeStruct((S, DM), BF16), jax.ShapeDtypeStruct((NP, 2, S), F32)),
        in_specs=[col, col, col, col], out_specs=(col, pl.BlockSpec((None, 2, S), lambda hp: (hp, 0, 0))),
        args=(q, k, v, ca))


def _fox_bwd(q, k, v, o, do, ca, lse, name, comm=None):
    S, DM = q.shape
    NP = DM // PAIR
    T = min(FOX_BLOCK, S)
    nq = S // T

    def body(q_ref, k_ref, v_ref, o_ref, do_ref, ca_ref, lse_ref, dq_ref, dk_ref, dv_ref, dc_ref,
             dq_acc, dcol_acc, dk_acc, dv_acc, dd_s):
        keep = lax.broadcasted_iota(jnp.int32, (T, T), 0) <= lax.broadcasted_iota(jnp.int32, (T, T), 1)
        lane = lax.broadcasted_iota(jnp.int32, (1, LANES), 1)
        heads = (_lane_mask(0), _lane_mask(1))
        ones3 = _bias_ones()
        dq_acc[...] = jnp.zeros_like(dq_acc)
        dcol_acc[...] = jnp.zeros_like(dcol_acc)
        dc_ref[...] = jnp.zeros_like(dc_ref)

        def row_dots(qi, _):
            t0 = pl.multiple_of(qi * T, T)
            prod = do_ref[pl.ds(t0, T), :].astype(F32) * o_ref[pl.ds(t0, T), :].astype(F32)
            cols = [jnp.sum(jnp.where(heads[e], prod, 0.0), axis=-1, keepdims=True) for e in (0, 1)]
            tile = jnp.where(lane == 0, cols[0], jnp.where(lane == 1, cols[1], 0.0))
            dd_s[:, pl.ds(t0, T)] = tile.T[0:8, :]
            return 0

        lax.fori_loop(0, nq, row_dots, 0)

        def kv_pair(b, _):
            dk_acc[...] = jnp.zeros_like(dk_acc)
            dv_acc[...] = jnp.zeros_like(dv_acc)
            s0s = [pl.multiple_of((2 * b + jj) * T, T) for jj in (0, 1)]
            kaug, ks, vbs = {}, {}, []
            for jj in (0, 1):
                kb = k_ref[pl.ds(s0s[jj], T), :]
                cab = ca_ref[pl.ds(s0s[jj], T), :]
                vbs.append(v_ref[pl.ds(s0s[jj], T), :])
                for e in (0, 1):
                    kaug[jj, e] = jnp.where(heads[e], kb, cab)
                    ks[jj, e] = jnp.where(heads[e], kb, jnp.zeros_like(kb)) * QK_SCALE

            def q_step(qi, blocks):
                t0 = pl.multiple_of(qi * T, T)
                qs = q_ref[pl.ds(t0, T), :] * QK_SCALE
                dob = do_ref[pl.ds(t0, T), :]
                doh = [jnp.where(heads[e], dob, jnp.zeros_like(dob)) for e in (0, 1)]
                qh = [jnp.where(heads[e], qs, jnp.zeros_like(qs)) for e in (0, 1)]
                chains = [(e, jj, masked) for e in (0, 1) for jj, masked in blocks]
                scores = [_dot_nt(kaug[jj, e], jnp.where(heads[e], qs, ones3)) for e, jj, _ in chains]
                dps = [_dot_nt(vbs[jj], doh[e]) for e, jj, _ in chains]
                dq = jnp.zeros((T, PAIR), F32)
                drow = [jnp.zeros((1, T), F32), jnp.zeros((1, T), F32)]
                for n, (e, jj, masked) in enumerate(chains):
                    st = jnp.where(keep, scores[n], NEG) if masked else scores[n]
                    pt = jnp.exp(st - lse_ref[e:e + 1, pl.ds(t0, T)])
                    dv_acc[jj] += _dot(pt.astype(BF16), doh[e])
                    dst = pt * (dps[n] - dd_s[e:e + 1, pl.ds(t0, T)])
                    dsb = dst.astype(BF16)
                    dk_acc[jj] += _dot(dsb, qh[e])
                    dq = dq + _dot_tn(dsb, ks[jj, e])
                    dcol_acc[pl.ds(s0s[jj], T), :] -= jnp.where(lane == e, jnp.sum(dst, axis=-1, keepdims=True), 0.0)
                    drow[e] = drow[e] + jnp.sum(dst, axis=0, keepdims=True)
                for e in (0, 1):
                    dc_ref[e:e + 1, pl.ds(t0, T)] += drow[e]
                dq_acc[pl.ds(t0, T), :] += dq

            q_step(2 * b, [(0, True)])
            q_step(2 * b + 1, [(0, False), (1, True)])

            def rest(i, _):
                q_step(2 * b + 2 + 2 * i, [(0, False), (1, False)])
                q_step(2 * b + 3 + 2 * i, [(0, False), (1, False)])
                return 0

            lax.fori_loop(0, (nq - 2) // 2 - b, rest, 0)
            for jj in (0, 1):
                dk_ref[pl.ds(s0s[jj], T), :] = dk_acc[jj].astype(BF16)
                dv_ref[pl.ds(s0s[jj], T), :] = dv_acc[jj].astype(BF16)
            return 0

        lax.fori_loop(0, nq // 2, kv_pair, 0)
        dq_ref[...] = dq_acc[...].astype(BF16)

        def add_cols(kj, _):
            s0 = pl.multiple_of(kj * T, T)
            dc_ref[:, pl.ds(s0, T)] += dcol_acc[pl.ds(s0, T), :].T[0:2, :]
            return 0

        lax.fori_loop(0, nq, add_cols, 0)

    col = pl.BlockSpec((S, PAIR), lambda hp: (0, hp))
    row2 = pl.BlockSpec((None, 2, S), lambda hp: (hp, 0, 0))
    return _pcall(
        body, name=name, grid=(NP,), comm=comm,
        out_shape=(jax.ShapeDtypeStruct(do.shape, BF16), jax.ShapeDtypeStruct((S, DM), BF16),
                   jax.ShapeDtypeStruct((S, DM), BF16), jax.ShapeDtypeStruct((NP, 2, S), F32)),
        in_specs=[col, col, col, col, col, col, row2], out_specs=(col, col, col, row2),
        scratch_shapes=[pltpu.VMEM((S, PAIR), F32), pltpu.VMEM((S, LANES), F32), pltpu.VMEM((2, T, PAIR), F32),
                        pltpu.VMEM((2, T, PAIR), F32), pltpu.VMEM((8, S), F32)],
        args=(q, k, v, o, do, ca, lse), aliases={4: 0})


def _mixer_b_fwd(x, o, qm, mkv, wo, g, b, name):
    S, D = x.shape
    DQ = qm.shape[1]
    DM = D - DQ
    tm = min(512, S)

    def body(x_ref, o_ref, qm_ref, mkv_ref, wo_ref, g_ref, b_ref, xo_ref, xbt_ref, r_ref, catt_ref, cat_ref):
        cat_ref[:, :DM] = o_ref[...]
        mo = _mem_attn_fwd(qm_ref[...], mkv_ref, DQ)
        for mp in range(DQ // PAIR):
            cat_ref[:, DM + mp * PAIR:DM + (mp + 1) * PAIR] = mo[mp].astype(BF16)
        _tail_fwd_store(x_ref[...], cat_ref, wo_ref, g_ref, b_ref, xo_ref, xbt_ref, r_ref, catt_ref)

    return pl.pallas_call(
        body, name=name, grid=(S // tm,),
        out_shape=(jax.ShapeDtypeStruct((S, D), F32), jax.ShapeDtypeStruct((D, S), BF16),
                   jax.ShapeDtypeStruct((S, D), F32), jax.ShapeDtypeStruct((D, S), BF16)),
        in_specs=[_rows(tm, D), _rows(tm, DM), _rows(tm, DQ), _full(mkv.shape), _full(wo.shape), _full((1, D)),
                  _full((1, D))],
        out_specs=(_rows(tm, D), _cols(D, tm), _rows(tm, D), _cols(D, tm)),
        scratch_shapes=[pltpu.VMEM((tm, D), BF16)],
        compiler_params=_params(),
    )(x, o, qm, mkv, wo, g, b)


def _mixer_b_bwd(dxo, r, qm, mkv, wo, g, name, comm=None):
    S, D = r.shape
    DQ = qm.shape[1]
    DM = D - DQ
    M = mkv.shape[0]
    tm = min(512, S)

    def body(dxo_ref, r_ref, qm_ref, mkv_ref, wo_ref, g_ref, dr_ref, drb_ref, dcat_ref, dgb_ref, dmkv_ref):
        @pl.when(pl.program_id(0) == 0)
        def _():
            dgb_ref[...] = jnp.zeros_like(dgb_ref)
            dmkv_ref[...] = jnp.zeros_like(dmkv_ref)

        dr, dcat = _tail_bwd_head(dxo_ref, r_ref, g_ref, wo_ref, dgb_ref, drb_ref)
        dr_ref[...] = dr
        dcat_ref[:, :DM] = dcat[:, :DM].astype(BF16)
        dmo = [dcat[:, DM + mp * PAIR:DM + (mp + 1) * PAIR] for mp in range(DQ // PAIR)]
        dqm = _mem_attn_bwd(qm_ref[...], mkv_ref, dmo, dmkv_ref, DQ)
        for mp in range(DQ // PAIR):
            dcat_ref[:, DM + mp * PAIR:DM + (mp + 1) * PAIR] = dqm[mp].astype(BF16)

    return _pcall(
        body, name=name, grid=(S // tm,), comm=comm,
        out_shape=(jax.ShapeDtypeStruct((S, D), F32), jax.ShapeDtypeStruct((S, D), BF16),
                   jax.ShapeDtypeStruct((S, D), BF16), jax.ShapeDtypeStruct((2, D), F32),
                   jax.ShapeDtypeStruct((M, 2 * DQ), F32)),
        in_specs=[_rows(tm, D), _rows(tm, D), _rows(tm, DQ), _full(mkv.shape), _full(wo.shape), _full((1, D))],
        out_specs=(_rows(tm, D), _rows(tm, D), _rows(tm, D), _full((2, D)), _full((M, 2 * DQ))),
        args=(dxo, r, qm, mkv, wo, g))


class _NoExchange:
    def plan(self, stage, w, grads):
        return None

    def done(self, stage, results, w):
        pass


def _local_step(x, mem, target, w, hooks):
    S, D = x.shape
    DQ = w["mkv0"].shape[1] // 2
    DM = D - DQ
    NG = DM // HEAD_DIM
    NP = DM // PAIR
    ln_g, ln_b = w["ln_g"], w["ln_b"]
    row = lambda a, i: a[i:i + 1]
    memb = mem.astype(BF16)
    bst =jnp.pad(w["b_s"].T, ((0, 0), (0, LANES - NG)))
    grads = {}

    def staged(stage, fn, *args):
        out, got = fn(*args, stage, comm=hooks.plan(stage, w, grads))
        hooks.done(stage, got, w)
        return out

    mkv0 = _mm_small(memb, w["mkv0"], "mem_kv_0")
    x1, x1t, r1, cat0t, x0t, z0b = staged("mixer_a_fwd", _mixer_a_fwd, x, w["wa"], w["sg"], w["sb"], w["ws"], bst, mkv0,
                                w["wo0"], row(ln_g, 0), row(ln_b, 0))
    x2, x2b, x2t, r2, h0 = staged("mlp_fwd_0", _mlp_fwd, x1, w["wup0"], w["wdown0"], row(ln_g, 1), row(ln_b, 1))
    q, qm, k, v, fl = _proj_b_fwd(x2b, w["wq"], w["wkv"], DM, "proj_b_fwd")
    flt = fl[:, :16].T
    bfc = jnp.pad(w["bf"], (0, 16 - NG)).reshape(16, 1)
    pieces = _forget_cumsum(flt, bfc, "forget_cumsum")
    pieces = jnp.pad(pieces[:, :NG].transpose(2, 1, 0), ((0, 0), (0, 0), (0, HEAD_DIM - N_BIAS)))
    ca = pieces.reshape(S, NP, 2, HEAD_DIM)[:, :, ::-1].reshape(S, DM).astype(BF16)
    o, lse = staged("fox_fwd", _fox_fwd, q, k, v, ca)
    mkv1 = _mm_small(memb, w["mkv1"], "mem_kv_1")
    x3, x3t, r3, cat1t = _mixer_b_fwd(x2, o, qm, mkv1, w["wo1"], row(ln_g, 2), row(ln_b, 2), "mixer_b_fwd")
    dx4, r4, h1, loss_part = _mlp_fwd_loss(x3, w["wup1"], w["wdown1"], row(ln_g, 3), row(ln_b, 3), target,
                                           "mlp_fwd_1")

    nc = w["wup0"].shape[0]
    (dx3, dr4t, dpre1, dgb11), _ = _mlp_bwd(dx4, r4, h1, w["wup1"], w["wdown1"], row(ln_g, 3), "mlp_bwd_1")
    grads["wdown1"] = _mm_nn(dr4t, h1, "dw_down_1", out_split=nc)
    grads["wup1"] = _mm_nn(x3t, dpre1, "dw_up_1", out_split=nc)
    dr3, dr3b, dcat1, dgb10, dmkv1 = staged("mixer_b_bwd", _mixer_b_bwd, dx3, r3, qm, mkv1, w["wo1"], row(ln_g, 2))
    dqz, dk, dv, dct3 = staged("fox_bwd", _fox_bwd, q, k, v, o, dcat1, ca, lse)
    dct = jnp.pad(dct3.reshape(NG, S), ((0, 16 - NG), (0, 0)))
    dflt, dbf = _forget_cumsum_bwd(dct, flt, bfc, "forget_cumsum_bwd")
    dfl = jnp.pad(dflt.T, ((0, 0), (0, LANES - 16))).astype(BF16)
    dx2 = _proj_b_bwd(dr3, dqz, dk, dv, dfl, w["wq"], w["wkv"], "proj_b_bwd")
    grads["wo1"] = _mm_nn(cat1t, dr3b, "dw_o_1")
    grads["wq"] = _mm_nn(x2t, dqz, "dw_q")
    grads["wkv"] = _mm_nn_cat(x2t, [dk, dv, dfl], "dw_kv")
    grads["mkv1"] = _mm_tn(memb, dmkv1.astype(BF16), "dw_mkv_1")
    dx1, dr2t, dpre0, dgb01 = staged("mlp_bwd_0", _mlp_bwd, dx2, r2, h0, w["wup0"], w["wdown0"], row(ln_g, 1))
    grads["wdown0"] = _mm_nn(dr2t, h0, "dw_down_0", out_split=nc)
    grads["wup0"] = _mm_nn(x1t, dpre0, "dw_up_0", out_split=nc)
    dx0, dz, dr1b, dgb00, dsgb, dws, dbst, dmkv0 = staged(
        "mixer_a_bwd", _mixer_a_bwd, z0b, dx1, r1, w["wa"], w["sg"], w["sb"], w["ws"], bst, mkv0, w["wo0"], row(ln_g, 0))
    grads["wo0"] = _mm_nn(cat0t, dr1b, "dw_o_0")
    grads["wa"] = _mm_nn(x0t, dz, "dw_a")
    grads["mkv0"] = _mm_tn(memb, dmkv0.astype(BF16), "dw_mkv_0")
    grads.update({
        "ws": dws, "b_s": dbst[:, :NG].T, "sg": dsgb[0:1], "sb": dsgb[1:2], "bf": dbf[:NG, 0], "loss": loss_part[0:1],
        "ln_g": jnp.concatenate([dgb00[0:1], dgb01[0:1], dgb10[0:1], dgb11[0:1]], axis=0),
        "ln_b": jnp.concatenate([dgb00[1:2], dgb01[1:2], dgb10[1:2], dgb11[1:2]], axis=0),
    })
    return loss_part, dx0, grads


def _place():
    xi, yi, ci = lax.axis_index("x"), lax.axis_index("y"), lax.axis_index("c")
    peers = [(xi, 1 - yi), (1 - xi, yi), (1 - xi, 1 - yi)]
    return xi, yi, ci, peers


def _comm_call(body, name, ins, out_shapes, n_remote, n_local):
    return pl.pallas_call(
        body, name=name, out_shape=out_shapes, in_specs=[ANY] * len(ins), out_specs=[ANY] * len(out_shapes),
        scratch_shapes=[pltpu.SemaphoreType.DMA((n_remote,)), pltpu.SemaphoreType.DMA((n_remote,)),
                        pltpu.SemaphoreType.DMA((max(n_local, 1),))],
        compiler_params=pltpu.CompilerParams(has_side_effects=True),
    )(*ins)


def _pair_swap(xs):
    n = len(xs)
    out_shapes = [jax.ShapeDtypeStruct((a.shape[0],) + a.shape[2:], a.dtype) for a in xs]

    def copies(x_refs, o_refs, sems):
        send_sems, recv_sems = sems
        xi, yi, ci, _ = _place()
        return [pltpu.make_async_remote_copy(src_ref=x_refs[i].at[:, 1 - ci], dst_ref=o_refs[i],
                                             send_sem=send_sems.at[i], recv_sem=recv_sems.at[i],
                                             device_id=(xi, yi, 1 - ci), device_id_type=MESH) for i in range(n)]

    def start(x_refs, o_refs, sems):
        for cp in copies(x_refs, o_refs, sems):
            cp.start()

    def finish(x_refs, o_refs, sems):
        cps = copies(x_refs, o_refs, sems)
        for cp in cps:
            cp.wait_recv()
        for cp in cps:
            cp.wait_send()

    return _Comm(xs, out_shapes, {}, [n, n], start, finish)


def _chip_exchange(xs):
    n = len(xs)
    out_shapes = [jax.ShapeDtypeStruct((3,) + a.shape[1:], a.dtype) for a in xs]

    def copies(x_refs, o_refs, sems):
        send_sems, recv_sems = sems
        xi, yi, ci, peers = _place()
        return [pltpu.make_async_remote_copy(src_ref=x_refs[i].at[2 * px + py], dst_ref=o_refs[i].at[d],
                                             send_sem=send_sems.at[3 * i + d], recv_sem=recv_sems.at[3 * i + d],
                                             device_id=(px, py, ci), device_id_type=MESH)
                for i in range(n) for d, (px, py) in enumerate(peers)]

    def start(x_refs, o_refs, sems):
        for cp in copies(x_refs, o_refs, sems):
            cp.start()

    def finish(x_refs, o_refs, sems):
        cps = copies(x_refs, o_refs, sems)
        for cp in cps:
            cp.wait_recv()
        for cp in cps:
            cp.wait_send()

    return _Comm(xs, out_shapes, {}, [3 * n, 3 * n], start, finish)


def _inplace_call(body, name, bufs, sem_counts):
    n = len(bufs)
    return pl.pallas_call(
        body, name=name, out_shape=[jax.ShapeDtypeStruct(a.shape, a.dtype) for a in bufs],
        in_specs=[ANY] * n, out_specs=[ANY] * n, input_output_aliases={i: i for i in range(n)},
        scratch_shapes=[pltpu.SemaphoreType.DMA((k,)) for k in sem_counts],
        compiler_params=pltpu.CompilerParams(has_side_effects=True),
    )(*bufs)


def _gather_weights(bufs, whole=(), early=None):
    n, nw = len(bufs), len(whole)

    def copy(ref, s, r, k, dev):
        return pltpu.make_async_remote_copy(src_ref=ref, dst_ref=ref, send_sem=s.at[k], recv_sem=r.at[k],
                                            device_id=dev, device_id_type=MESH)

    def plan(w, sems):
        ici_s, ici_r, d2d_s, d2d_r = sems
        xi, yi, ci, peers = _place()
        xn, yn, sib = (1 - xi, yi, ci), (xi, 1 - yi, ci), (xi, yi, 1 - ci)
        j, jx, jy, jd = 2 * xi + yi, 2 * (1 - xi) + yi, 2 * xi + 1 - yi, 2 * (1 - xi) + 1 - yi
        c = {}
        for i in range(n):
            hh = bufs[i].shape[2] // 2
            lo, hi = pl.ds(0, hh), pl.ds(hh, hh)
            at = lambda slot, half, rows=None, i=i: w[i].at[slot, half] if rows is None else w[i].at[slot, half, rows]
            c["to_x", i] = (copy(at(j, ci), ici_s, ici_r, 4 * i, xn), copy(at(jx, ci), ici_s, ici_r, 4 * i, xn))
            c["to_y", i] = (copy(at(j, ci), ici_s, ici_r, 4 * i + 1, yn), copy(at(jy, ci), ici_s, ici_r, 4 * i + 1, yn))
            c["x_on", i] = (copy(at(jx, ci, lo), ici_s, ici_r, 4 * i + 2, yn), copy(at(jd, ci, lo), ici_s, ici_r, 4 * i + 2, yn))
            c["y_on", i] = (copy(at(jy, ci, hi), ici_s, ici_r, 4 * i + 3, xn), copy(at(jd, ci, hi), ici_s, ici_r, 4 * i + 3, xn))
            for k, slot in enumerate((jx, jy, jd)):
                c["sib", i, k] = (copy(at(slot, ci), d2d_s, d2d_r, 3 * i + k, sib),
                                  copy(at(slot, 1 - ci), d2d_s, d2d_r, 3 * i + k, sib))
        for i in range(nw):
            for d, (px, py) in enumerate(peers):
                k = 4 * n + 3 * i + d
                c["whole", i, d] = (copy(w[n + i].at[j], ici_s, ici_r, k, (px, py, ci)),
                                    copy(w[n + i].at[2 * px + py], ici_s, ici_r, k, (px, py, ci)))
        return c

    def start(_, w, sems):
        c = plan(w, sems)
        for i in range(n):
            c["to_x", i][0].start()
            c["to_y", i][0].start()
        for i in range(nw):
            for d in range(3):
                c["whole", i, d][0].start()

    def pass_on(c, which):
        for i in which:
            c["to_x", i][1].wait_recv()
            c["x_on", i][0].start()
            c["sib", i, 0][0].start()
            c["to_y", i][1].wait_recv()
            c["y_on", i][0].start()
            c["sib", i, 1][0].start()

    n_early = n if early is None else min(early, n)

    def middle(_, w, sems):
        pass_on(plan(w, sems), range(n_early))

    def finish(_, w, sems):
        c = plan(w, sems)
        pass_on(c, range(n_early, n))
        for i in range(n):
            c["x_on", i][1].wait_recv()
            c["y_on", i][1].wait_recv()
            c["sib", i, 2][0].start()
        for i in range(n):
            for k in range(3):
                c["sib", i, k][1].wait_recv()
        for i in range(nw):
            for d in range(3):
                c["whole", i, d][1].wait_recv()
        for key, (sent, _) in c.items():
            sent.wait_send()

    arrays = list(bufs) + list(whole)
    k = 4 * n + 3 * nw
    return _Comm(arrays, [jax.ShapeDtypeStruct(a.shape, a.dtype) for a in arrays], {i: i for i in range(n + nw)},
                 [k, k, max(3 * n, 1), max(3 * n, 1)], start, finish, middle)


def _pair_gather(bufs, spread, name):
    n = len(bufs)
    where = [(i, l) for i, a in enumerate(bufs) for l in range(a.shape[0])]
    nw = len(where)
    arrays = list(bufs) + ([] if spread is None else [spread])
    na = len(arrays)

    def body(*refs):
        g = refs[na:na + n]
        sp = refs[na + n] if spread is not None else None
        send_sems, recv_sems = refs[2 * na:]
        xi, yi, ci, peers = _place()
        chips = [(xi, yi)] + peers
        others = [(d, f) for d in range(4) for f in (0, 1) if (d, f) != (0, 0)] if spread is not None else []
        sends = []
        for k, (i, l) in enumerate(where):
            mine = g[i].at[l, :, ci]
            cp = pltpu.make_async_remote_copy(src_ref=mine, dst_ref=mine, send_sem=send_sems.at[k],
                                              recv_sem=recv_sems.at[k], device_id=(xi, yi, 1 - ci), device_id_type=MESH)
            cp.start()
            sends.append(cp)
        for k, (d, f) in enumerate(others):
            mine = sp.at[2 * xi + yi, ci]
            cp = pltpu.make_async_remote_copy(src_ref=mine, dst_ref=mine, send_sem=send_sems.at[nw + k],
                                              recv_sem=recv_sems.at[nw + k],
                                              device_id=(chips[d][0], chips[d][1], ci if f == 0 else 1 - ci),
                                              device_id_type=MESH)
            cp.start()
            sends.append(cp)
        for k, (i, l) in enumerate(where):
            other = g[i].at[l, :, 1 - ci]
            pltpu.make_async_remote_copy(src_ref=other, dst_ref=other, send_sem=send_sems.at[k],
                                         recv_sem=recv_sems.at[k], device_id=(xi, yi, 1 - ci),
                                         device_id_type=MESH).wait_recv()
        for k, (d, f) in enumerate(others):
            px, py, pc = chips[d][0], chips[d][1], (ci if f == 0 else 1 - ci)
            theirs = sp.at[2 * px + py, pc]
            pltpu.make_async_remote_copy(src_ref=theirs, dst_ref=theirs, send_sem=send_sems.at[nw + k],
                                         recv_sem=recv_sems.at[nw + k], device_id=(px, py, pc),
                                         device_id_type=MESH).wait_recv()
        for cp in sends:
            cp.wait_send()

    n_sems = nw + (7 if spread is not None else 0)
    out = _inplace_call(body, name, arrays, [n_sems, n_sems])
    return out[:n], (out[n] if spread is not None else None)


def _row_block(rows, cols):
    want = max(8, (2 ** 18 // max(cols, 1)) // 8 * 8)
    if rows <= want:
        return rows
    best = 8
    for t in range(8, want + 1, 8):
        if rows % t == 0:
            best = t
    return best


def _pair_add(g4, recv, sel, out_dtype, name):
    A, _, H, C = g4.shape
    bh = _row_block(H, C)

    def body(sel_ref, g_ref, r_ref, o_ref):
        o_ref[...] = (g_ref[...].astype(F32) + r_ref[...].astype(F32)).astype(out_dtype)

    return pl.pallas_call(
        body, name=name, out_shape=jax.ShapeDtypeStruct((A, H, C), out_dtype),
        grid_spec=pltpu.PrefetchScalarGridSpec(
            num_scalar_prefetch=1, grid=(A, H // bh),
            in_specs=[pl.BlockSpec((None, None, bh, C), lambda a, i, s: (a, s[1], i, 0)),
                      pl.BlockSpec((None, bh, C), lambda a, i, s: (a, i, 0))],
            out_specs=pl.BlockSpec((None, bh, C), lambda a, i, s: (a, i, 0))),
        compiler_params=pltpu.CompilerParams(dimension_semantics=("arbitrary", "arbitrary"),
                                             vmem_limit_bytes=VMEM_LIMIT),
    )(sel, g4, recv)


def _chip_reduce(g4, recv1, recv2, sel, buf, layer, n_layers, name, by_chip=False):
    _, _, H, C = g4.shape
    bh = _row_block(H, C)
    if by_chip:
        out_shape = jax.ShapeDtypeStruct((N_CHIPS, 2, H, C), F32)
        out_spec = pl.BlockSpec((None, None, bh, C), lambda i, s: (s[0], s[1], i, 0))
    else:
        out_shape = jax.ShapeDtypeStruct((n_layers, 1, 2, H, C), F32)
        out_spec = pl.BlockSpec((None, None, None, bh, C), lambda i, s: (layer, 0, s[1], i, 0))

    def body(sel_ref, g_ref, r1_ref, r2_ref, *rest):
        acc = g_ref[...].astype(F32) + r1_ref[...].astype(F32)
        for d in range(3):
            acc = acc + r2_ref[d].astype(F32)
        rest[-1][...] = acc

    in_specs = [pl.BlockSpec((None, None, bh, C), lambda i, s: (s[0], s[1], i, 0)),
                pl.BlockSpec((None, bh, C), lambda i, s: (s[0], i, 0)),
                pl.BlockSpec((3, bh, C), lambda i, s: (0, i, 0))]
    args = [sel, g4, recv1, recv2]
    aliases = {}
    if buf is not None:
        in_specs.append(ANY)
        args.append(buf)
        aliases = {4: 0}
    return pl.pallas_call(
        body, name=name, out_shape=out_shape,
        grid_spec=pltpu.PrefetchScalarGridSpec(num_scalar_prefetch=1, grid=(H // bh,), in_specs=in_specs,
                                               out_specs=out_spec),
        input_output_aliases=aliases,
        compiler_params=pltpu.CompilerParams(dimension_semantics=("arbitrary",), vmem_limit_bytes=VMEM_LIMIT),
    )(*args)


def _cast_place(shards, layer, sel, name):
    _, _, H, C = shards.shape
    bh = _row_block(H, C)

    def body(sel_ref, x_ref, o_ref):
        o_ref[...] = x_ref[...].astype(BF16)

    return pl.pallas_call(
        body, name=name, out_shape=jax.ShapeDtypeStruct((N_CHIPS, 2, H, C), BF16),
        grid_spec=pltpu.PrefetchScalarGridSpec(
            num_scalar_prefetch=1, grid=(2, H // bh),
            in_specs=[pl.BlockSpec((None, None, bh, C), lambda h, i, s: (layer, h, i, 0))],
            out_specs=pl.BlockSpec((None, None, bh, C), lambda h, i, s: (s[0], h, i, 0))),
        compiler_params=pltpu.CompilerParams(dimension_semantics=("arbitrary", "arbitrary"),
                                             vmem_limit_bytes=VMEM_LIMIT),
    )(sel, shards)


def _adamw_many(quads, name, comm=None):
    R, C = quads[0][0].shape
    br = _row_block(R, C)
    n = len(quads)

    def body(*refs):
        for k in range(n):
            g_ref, w_ref, m_ref, v_ref = refs[4 * k:4 * k + 4]
            d_ref, mo_ref, vo_ref = refs[4 * n + 3 * k:4 * n + 3 * k + 3]
            gv = g_ref[...]
            mn = ADAM_B1 * m_ref[...] + (1.0 - ADAM_B1) * gv
            vn = ADAM_B2 * v_ref[...] + (1.0 - ADAM_B2) * (gv * gv)
            m_hat = mn / (1.0 - ADAM_B1 ** ADAM_STEP)
            v_hat = vn / (1.0 - ADAM_B2 ** ADAM_STEP)
            d_ref[...] = -ADAM_LR * (m_hat / (jnp.sqrt(v_hat) + ADAM_EPS) + ADAM_WD * w_ref[...])
            mo_ref[...] = mn
            vo_ref[...] = vn

    spec = pl.BlockSpec((br, C), lambda i: (i, 0))
    res, got = _pcall(body, name=name, grid=(R // br,), comm=comm, out_shape=(jax.ShapeDtypeStruct((R, C), F32),) * (3 * n),
                      in_specs=[spec] * (4 * n), out_specs=(spec,) * (3 * n), args=[a for q in quads for a in q])
    return [res[3 * k:3 * k + 3] for k in range(n)], got


def _adamw(g, w, m, v, name):
    return _adamw_many([(g, w, m, v)], name)[0][0]


def _part_rows(shape):
    return -(-math.prod(shape) // (8 * LANES)) * 8


def _pack_rows(arrs, total_rows):
    parts = []
    for a in arrs:
        flat = a.reshape(-1)
        rows = _part_rows(a.shape)
        parts.append(jnp.pad(flat, (0, rows * LANES - flat.shape[0])).reshape(rows, LANES))
    packed = jnp.concatenate(parts, axis=0)
    return jnp.pad(packed, ((0, total_rows - packed.shape[0]), (0, 0)))


def _unpack_rows(packed, shapes):
    out, r = [], 0
    for shp in shapes:
        size, rows = math.prod(shp), _part_rows(shp)
        out.append(packed[r:r + rows].reshape(-1)[:size].reshape(shp))
        r += rows
    return out


def _rows_of(shapes):
    return sum(_part_rows(s) for s in shapes)


_GATHER_STAGES = {"start": ["wa", "mkv0", "wo0"], "mixer_a_fwd": ["wup0", "wdown0"],
                  "mlp_fwd_0": ["wq", "wkv", "mkv1", "wo1"], "fox_fwd": ["wup1", "wdown1"]}
_SWAP_STAGES = {"mixer_b_bwd": ["wup1", "wdown1"]}
_REDUCE_STAGES = {"fox_bwd": ["wup1", "wdown1"], "mlp_bwd_0": ["wo1", "wq", "wkv", "mkv1"],
                  "mixer_a_bwd": ["wup0", "wdown0"], "end": ["wa", "wo0", "mkv0", "small"]}
_SMALL = ["ws", "b_s", "sg", "sb", "bf", "ln_g", "ln_b", "loss"]


class _Staged:
    def __init__(self, placed, sel, dm, ng, shard_cols, n_small):
        self.placed, self.sel, self.dm, self.ng, self.shard_cols, self.n_small = placed, sel, dm, ng, shard_cols, n_small
        self.g4, self.recv1, self.recv2 = {}, {}, {}

    def _weight(self, name, a):
        a = a.reshape(N_CHIPS, 2 * a.shape[2], a.shape[3])
        if name in ("wa", "wkv"):
            a = a.transpose(1, 0, 2).reshape(a.shape[1], -1)
            if name == "wkv":
                a = jnp.concatenate([a[:, :2 * self.dm],
                                     jnp.pad(a[:, 2 * self.dm:], ((0, 0), (0, LANES - self.ng)))], axis=1)
            return a
        if name.startswith("wup") or name.startswith("wdown"):
            return a
        return a.reshape(-1, a.shape[2])

    def _partial(self, name, grads):
        if name == "small":
            a = _pack_rows([grads[n] for n in _SMALL], self.n_small).reshape(N_CHIPS, -1, LANES)
        elif name in self.shard_cols:
            n = self.shard_cols[name]
            g = grads[name][:, :N_CHIPS * n]
            a = g.reshape(g.shape[0], N_CHIPS, n).transpose(1, 0, 2)
        elif name.startswith("wup") or name.startswith("wdown"):
            a = grads[name]
        else:
            g = grads[name]
            a = g.reshape(N_CHIPS, g.shape[0] // N_CHIPS, g.shape[1])
        return a.reshape(N_CHIPS, 2, a.shape[1] // 2, a.shape[2])

    def plan(self, stage, w, grads):
        if stage in _GATHER_STAGES:
            return _gather_weights([self.placed[n] for n in _GATHER_STAGES[stage]],
                                   early=1 if stage == "mixer_a_fwd" else None)
        if stage in _SWAP_STAGES:
            for n in _SWAP_STAGES[stage]:
                self.g4[n] = self._partial(n, grads)
            return _pair_swap([self.g4[n] for n in _SWAP_STAGES[stage]])
        if stage in _REDUCE_STAGES:
            names = _REDUCE_STAGES[stage]
            late = [n for n in names if n not in self.recv1]
            for n in late:
                self.g4[n] = self._partial(n, grads)
            if late:
                got = _run_comm(_pair_swap([self.g4[n] for n in late]), "reduce_pair_swap_" + stage)
                self.recv1.update(zip(late, got))
            return _chip_exchange([_pair_add(self.g4[n], self.recv1[n], self.sel, F32 if n == "small" else BF16,
                                             "reduce_pair_add_" + n) for n in names])
        return None

    def done(self, stage, results, w):
        if stage in _GATHER_STAGES:
            for n, a in zip(_GATHER_STAGES[stage], results):
                w[n] = self._weight(n, a)
        elif stage in _SWAP_STAGES:
            self.recv1.update(zip(_SWAP_STAGES[stage], results))
        elif stage in _REDUCE_STAGES:
            self.recv2.update(zip(_REDUCE_STAGES[stage], results))


def kernel(x, mem, a_w_in, a_sgu_ln_g, a_sgu_ln_b, a_w_s, a_b_s, kv_w, kv_b_f, b_w_q, mem_w_kv, w_o, ln_g, ln_b, w_up, w_down, loss_target, m_a_w_in, m_a_sgu_ln_g, m_a_sgu_ln_b, m_a_w_s, m_a_b_s, m_kv_w, m_kv_b_f, m_b_w_q, m_mem_w_kv, m_w_o, m_ln_g, m_ln_b, m_w_up, m_w_down, v_a_w_in, v_a_sgu_ln_g, v_a_sgu_ln_b, v_a_w_s, v_a_b_s, v_kv_w, v_kv_b_f, v_b_w_q, v_mem_w_kv, v_w_o, v_ln_g, v_ln_b, v_w_up, v_w_down):
    xi, yi, ci = lax.axis_index("x"), lax.axis_index("y"), lax.axis_index("c")
    chip = 2 * xi + yi
    sel = jnp.stack([chip, ci]).astype(jnp.int32)
    S, D = x.shape[1], x.shape[2]
    DQ = mem_w_kv.shape[2] // 2
    DM = D - DQ
    NG = DM // HEAD_DIM

    shards = {"wa": (a_w_in, 0), "wkv": (kv_w[None], 0), "wq": (b_w_q, 0), "mkv0": (mem_w_kv, 0), "mkv1": (mem_w_kv, 1),
              "wo0": (w_o, 0), "wo1": (w_o, 1), "wup0": (w_up, 0), "wup1": (w_up, 1), "wdown0": (w_down, 0),
              "wdown1": (w_down, 1)}
    placed = {n: _cast_place(s.reshape(s.shape[0], 2, s.shape[1] // 2, s.shape[2]), layer, sel, "cast_place_" + n)
              for n, (s, layer) in shards.items()}
    ln_pack = jnp.concatenate([ln_g.reshape(4, -1), ln_b.reshape(4, -1)], axis=0)
    ln_buf = lax.dynamic_update_slice(jnp.zeros((N_CHIPS,) + ln_pack.shape, F32), ln_pack[None], (chip, 0, 0))
    ln_shape = (4, D)
    small_shapes = [a_w_s.shape, a_b_s.shape, a_sgu_ln_g.shape, a_sgu_ln_b.shape, kv_b_f.shape, ln_shape, ln_shape,
                    (1, LANES)]
    n_small = -(-_rows_of(small_shapes) // 64) * 64
    hooks = _Staged(placed, sel, DM, NG, {"wa": a_w_in.shape[2], "wkv": kv_w.shape[1]}, n_small)
    weights = {"sg": a_sgu_ln_g, "sb": a_sgu_ln_b, "ws": a_w_s[0], "b_s": a_b_s[0], "bf": kv_b_f}
    got = _run_comm(_gather_weights([placed[n] for n in _GATHER_STAGES["start"]], whole=[ln_buf]),
                    "gather_weights_first")
    hooks.done("start", got[:-1], weights)
    ln_full = got[-1].transpose(1, 0, 2).reshape(8, D)
    weights["ln_g"], weights["ln_b"] = ln_full[:4], ln_full[4:]

    _, grad_x, gr = _local_step(x[0], mem[0], loss_target[0], weights, hooks)

    def chip_sums(groups):
        bufs = []
        for grp in groups:
            buf = None
            for layer, n in enumerate(grp):
                buf = _chip_reduce(hooks.g4[n], hooks.recv1[n], hooks.recv2[n], sel, buf, layer, len(grp),
                                   "reduce_chip_sum_" + n)
            bufs.append(buf)
        return bufs

    whole = lambda a: a.reshape(a.shape[0], 2 * a.shape[3], a.shape[4])
    flat = lambda a: a.reshape(-1, a.shape[-1])

    def update(name, g, w, m, v):
        d, mn, vn = _adamw(flat(g), flat(w), flat(m), flat(v), "adamw_" + name)
        return d.reshape(w.shape), mn.reshape(w.shape), vn.reshape(w.shape)

    hooks.done("end", _run_comm(hooks.plan("end", weights, gr), "reduce_chip_exchange_end"), weights)
    small_mine = _chip_reduce(hooks.g4["small"], hooks.recv1["small"], hooks.recv2["small"], sel, None, 0, 1,
                              "reduce_chip_sum_small", by_chip=True)
    red, small_all = _pair_gather(chip_sums([["wa"], ["wkv"], ["wq"], ["mkv0", "mkv1"], ["wo0", "wo1"],
                                             ["wup0", "wup1"], ["wdown0", "wdown1"]]), small_mine, "reduce_pair_gather")
    g_a_w_in, g_kv_w, g_b_w_q, g_mem_w_kv, g_w_o, g_w_up, g_w_down = [whole(a) for a in red]
    g_kv_w = g_kv_w[0]
    g_w_down = g_w_down.transpose(0, 2, 1)
    small_all = small_all.reshape(n_small, LANES)
    g_ws, g_bs, g_sg, g_sb, g_bf, g_lng, g_lnb, loss_row = _unpack_rows(small_all, small_shapes)
    loss = loss_row[0, 0]
    dsh = D // N_CHIPS
    g_ln_g = lax.dynamic_slice_in_dim(g_lng, chip * dsh, dsh, axis=1).reshape(ln_g.shape)
    g_ln_b = lax.dynamic_slice_in_dim(g_lnb, chip * dsh, dsh, axis=1).reshape(ln_b.shape)

    upd = {
        "a_w_in": update("a_w_in", g_a_w_in, a_w_in, m_a_w_in, v_a_w_in),
        "kv_w": update("kv_w", g_kv_w, kv_w, m_kv_w, v_kv_w),
        "b_w_q": update("b_w_q", g_b_w_q, b_w_q, m_b_w_q, v_b_w_q),
        "mem_w_kv": update("mem_w_kv", g_mem_w_kv, mem_w_kv, m_mem_w_kv, v_mem_w_kv),
        "w_o": update("w_o", g_w_o, w_o, m_w_o, v_w_o),
        "w_up": update("w_up", g_w_up, w_up, m_w_up, v_w_up),
        "w_down": update("w_down", g_w_down, w_down, m_w_down, v_w_down),
    }
    tiny_g = [g_sg, g_sb, g_ws, g_bs, g_bf, g_ln_g, g_ln_b]
    tiny_w = [a_sgu_ln_g, a_sgu_ln_b, a_w_s, a_b_s, kv_b_f, ln_g, ln_b]
    tiny_m = [m_a_sgu_ln_g, m_a_sgu_ln_b, m_a_w_s, m_a_b_s, m_kv_b_f, m_ln_g, m_ln_b]
    tiny_v = [v_a_sgu_ln_g, v_a_sgu_ln_b, v_a_w_s, v_a_b_s, v_kv_b_f, v_ln_g, v_ln_b]
    tiny_shapes = [a.shape for a in tiny_w]
    n_tiny = -(-_rows_of(tiny_shapes) // 8) * 8
    td, tm_, tv = _adamw(_pack_rows(tiny_g, n_tiny), _pack_rows(tiny_w, n_tiny), _pack_rows(tiny_m, n_tiny),
                         _pack_rows(tiny_v, n_tiny), "adamw_small")
    for name, d, mn, vn in zip(["a_sgu_ln_g", "a_sgu_ln_b", "a_w_s", "a_b_s", "kv_b_f", "ln_g", "ln_b"],
                               _unpack_rows(td, tiny_shapes), _unpack_rows(tm_, tiny_shapes),
                               _unpack_rows(tv, tiny_shapes)):
        upd[name] = (d, mn, vn)

    order = ["a_w_in", "a_sgu_ln_g", "a_sgu_ln_b", "a_w_s", "a_b_s", "kv_w", "kv_b_f", "b_w_q", "mem_w_kv", "w_o",
             "ln_g", "ln_b", "w_up", "w_down"]
    grads = {"a_w_in": g_a_w_in.reshape(a_w_in.shape), "a_sgu_ln_g": g_sg.reshape(a_sgu_ln_g.shape),
             "a_sgu_ln_b": g_sb.reshape(a_sgu_ln_b.shape), "a_w_s": g_ws.reshape(a_w_s.shape),
             "a_b_s": g_bs.reshape(a_b_s.shape), "kv_w": g_kv_w, "kv_b_f": g_bf, "b_w_q": g_b_w_q.reshape(b_w_q.shape),
             "mem_w_kv": g_mem_w_kv, "w_o": g_w_o, "ln_g": g_ln_g, "ln_b": g_ln_b, "w_up": g_w_up, "w_down": g_w_down}
    return (loss, grad_x[None], *[grads[n] for n in order], *[upd[n][0] for n in order],
            *[upd[n][1] for n in order], *[upd[n][2] for n in order])
```

```python
import functools
import math

import jax
import jax.numpy as jnp
from jax import lax
from jax.experimental import pallas as pl
from jax.experimental.pallas import tpu as pltpu

F32 = jnp.float32
BF16 = jnp.bfloat16

HEAD_DIM = 64
PAIR = 2 * HEAD_DIM
CHUNK = 128
LN_EPS = 1e-5
ALPHA = 4 ** 0.25
QK_SCALE = 1.0 / math.sqrt(HEAD_DIM)
NEG = -1e30
N_CHIPS = 4
FOX_BLOCK = 256
LANES = 128
VMEM_LIMIT = 48 * 2 ** 20

ADAM_LR, ADAM_B1, ADAM_B2, ADAM_EPS, ADAM_WD, ADAM_STEP = 0.001, 0.9, 0.999, 1e-08, 0.01, 10

MESH = pl.DeviceIdType.MESH
ANY = pl.BlockSpec(memory_space=pl.ANY)


def _dot(a, b):
    return jnp.dot(a, b, preferred_element_type=F32)


def _dot_nt(a, b):
    return lax.dot_general(a, b, (((1,), (1,)), ((), ())), preferred_element_type=F32)


def _dot_tn(a, b):
    return lax.dot_general(a, b, (((0,), (0,)), ((), ())), preferred_element_type=F32)


def _ln_stats(r):
    mu = jnp.mean(r, axis=-1, keepdims=True)
    d = r - mu
    var = jnp.mean(d * d, axis=-1, keepdims=True)
    rstd = lax.rsqrt(var + LN_EPS)
    return d * rstd, rstd


def _ln_bwd(dy, xhat, rstd, g):
    dxh = dy * g
    m1 = jnp.mean(dxh, axis=-1, keepdims=True)
    m2 = jnp.mean(dxh * xhat, axis=-1, keepdims=True)
    return rstd * (dxh - m1 - xhat * m2)


_GELU_K = math.sqrt(2.0 / math.pi)


def _gelu(x):
    return 0.5 * x * (1.0 + jnp.tanh(_GELU_K * (x + 0.044715 * x * x * x)))


def _gelu_grad(x):
    t = jnp.tanh(_GELU_K * (x + 0.044715 * x * x * x))
    return 0.5 * (1.0 + t) + 0.5 * x * (1.0 - t * t) * _GELU_K * (1.0 + 3 * 0.044715 * x * x)


def _lane_mask(e):
    lane = lax.broadcasted_iota(jnp.int32, (1, PAIR), 1)
    return (lane >= HEAD_DIM * e) & (lane < HEAD_DIM * (e + 1))


def _first_head():
    return lax.broadcasted_iota(jnp.int32, (1, PAIR), 1) < HEAD_DIM


def _mem_probs(qh, mkp):
    sc = _dot_nt(qh, mkp) * QK_SCALE
    ex = jnp.exp(sc - jnp.max(sc, axis=-1, keepdims=True))
    return ex / jnp.sum(ex, axis=-1, keepdims=True)


def _mem_attn_fwd(qm_b, mkv_ref, dq_dim):
    outs = []
    for mp in range(dq_dim // PAIR):
        qp = qm_b[:, mp * PAIR:(mp + 1) * PAIR]
        mkp = mkv_ref[:, mp * PAIR:(mp + 1) * PAIR]
        mvp = mkv_ref[:, dq_dim + mp * PAIR:dq_dim + (mp + 1) * PAIR]
        heads = []
        for e in (0, 1):
            qh = jnp.where(_lane_mask(e), qp, jnp.zeros_like(qp))
            p = _mem_probs(qh, mkp)
            heads.append(_dot(p.astype(BF16), mvp))
        outs.append(jnp.where(_first_head(), heads[0], heads[1]))
    return outs


def _mem_attn_bwd(qm_b, mkv_ref, dmo_pairs, dmkv_ref, dq_dim):
    dqs = []
    for mp in range(dq_dim // PAIR):
        ks = slice(mp * PAIR, (mp + 1) * PAIR)
        vs = slice(dq_dim + mp * PAIR, dq_dim + (mp + 1) * PAIR)
        qp = qm_b[:, ks]
        mkp = mkv_ref[:, ks]
        mvp = mkv_ref[:, vs]
        dmo_b = dmo_pairs[mp].astype(BF16)
        dq = None
        dmk = None
        dmv = None
        for e in (0, 1):
            hm = _lane_mask(e)
            qh = jnp.where(hm, qp, jnp.zeros_like(qp))
            p = _mem_probs(qh, mkp)
            doh = jnp.where(hm, dmo_b, jnp.zeros_like(dmo_b))
            dp = _dot_nt(doh, mvp)
            ds = p * (dp - jnp.sum(dp * p, axis=-1, keepdims=True))
            dsb = (ds * QK_SCALE).astype(BF16)
            kh = jnp.where(hm, mkp, jnp.zeros_like(mkp))
            dq_e = _dot(dsb, kh)
            dmk_e = _dot_tn(dsb, qh)
            dmv_e = _dot_tn(p.astype(BF16), doh)
            dq = dq_e if dq is None else dq + dq_e
            dmk = dmk_e if dmk is None else dmk + dmk_e
            dmv = dmv_e if dmv is None else dmv + dmv_e
        dmkv_ref[:, ks] += dmk
        dmkv_ref[:, vs] += dmv
        dqs.append(dq)
    return dqs


def _params(n_axes=1):
    return pltpu.CompilerParams(dimension_semantics=("arbitrary",) * n_axes, vmem_limit_bytes=VMEM_LIMIT)


def _full(shape):
    return pl.BlockSpec(shape, lambda *_: (0,) * len(shape))


def _rows(tm, cols):
    return pl.BlockSpec((tm, cols), lambda i: (i, 0))


def _cols(rows, tm):
    return pl.BlockSpec((rows, tm), lambda i: (0, i))


def _pick(n, pref):
    if n <= pref:
        return n
    best = LANES
    for t in range(LANES, pref + 1, LANES):
        if n % t == 0:
            best = t
    return best if 2 * best >= pref or n > 2 * pref else n


class _Comm:
    def __init__(self, ins, out_shapes, aliases, sem_counts, start, finish, middle=None):
        self.ins, self.out_shapes, self.aliases, self.sem_counts = list(ins), list(out_shapes), dict(aliases), sem_counts
        self.start, self.finish = start, finish
        self.middle = middle


def _pcall(body, *, name, grid, in_specs, out_specs, out_shape, args, scratch_shapes=(), comm=None, aliases=None):
    n_in, n_out, n_scr = len(in_specs), len(out_shape), len(scratch_shapes)
    aliases = dict(aliases or {})
    if comm is None:
        res = pl.pallas_call(body, name=name, grid=grid, out_shape=tuple(out_shape), in_specs=list(in_specs),
                             out_specs=tuple(out_specs), scratch_shapes=list(scratch_shapes),
                             input_output_aliases=aliases, compiler_params=_params())(*args)
        return tuple(res), ()
    nci, nco = len(comm.ins), len(comm.out_shapes)
    last = grid[0] - 1

    def wrapped(*refs):
        ins, refs = refs[:n_in], refs[n_in:]
        cins, refs = refs[:nci], refs[nci:]
        outs, refs = refs[:n_out], refs[n_out:]
        couts, refs = refs[:nco], refs[nco:]
        scr, sems = refs[:n_scr], refs[n_scr:]

        @pl.when(pl.program_id(0) == 0)
        def _():
            comm.start(cins, couts, sems)

        if comm.middle is not None:
            @pl.when(pl.program_id(0) == (last + 1) // 2)
            def _():
                comm.middle(cins, couts, sems)

        body(*ins, *outs, *scr)

        @pl.when(pl.program_id(0) == last)
        def _():
            comm.finish(cins, couts, sems)

    res = pl.pallas_call(
        wrapped, name=name, grid=grid, out_shape=tuple(out_shape) + tuple(comm.out_shapes),
        in_specs=list(in_specs) + [ANY] * nci, out_specs=tuple(out_specs) + (ANY,) * nco,
        input_output_aliases={**aliases, **{n_in + a: n_out + b for a, b in comm.aliases.items()}},
        scratch_shapes=list(scratch_shapes) + [pltpu.SemaphoreType.DMA((k,)) for k in comm.sem_counts],
        compiler_params=pltpu.CompilerParams(dimension_semantics=("arbitrary",), vmem_limit_bytes=VMEM_LIMIT,
                                             has_side_effects=True),
    )(*args, *comm.ins)
    return tuple(res[:n_out]), tuple(res[n_out:])


def _run_comm(comm, name):
    nci, nco = len(comm.ins), len(comm.out_shapes)

    def body(*refs):
        cins, couts, sems = refs[:nci], refs[nci:nci + nco], refs[nci + nco:]
        comm.start(cins, couts, sems)
        if comm.middle is not None:
            comm.middle(cins, couts, sems)
        comm.finish(cins, couts, sems)

    return pl.pallas_call(
        body, name=name, out_shape=tuple(comm.out_shapes), in_specs=[ANY] * nci, out_specs=(ANY,) * nco,
        input_output_aliases=comm.aliases, scratch_shapes=[pltpu.SemaphoreType.DMA((k,)) for k in comm.sem_counts],
        compiler_params=pltpu.CompilerParams(has_side_effects=True),
    )(*comm.ins)


def _mm(a, b, name, a_is_transposed, out_split=1):
    (M, R) = a.shape if a_is_transposed else a.shape[::-1]
    _, N = b.shape
    bm, br = _pick(M, 1024 if a_is_transposed else 512), _pick(R, 1024 if a_is_transposed else 512)
    ncol = N // out_split
    bn = _pick(ncol, 1024)
    per = ncol // bn
    last = R // br - 1

    def body(a_ref, b_ref, o_ref, acc_ref):
        @pl.when(pl.program_id(2) == 0)
        def _():
            acc_ref[...] = jnp.zeros_like(acc_ref)

        acc_ref[...] += (_dot if a_is_transposed else _dot_tn)(a_ref[...], b_ref[...])

        @pl.when(pl.program_id(2) == last)
        def _():
            o_ref[...] = acc_ref[...].astype(BF16)

    if out_split == 1:
        out_shape = jax.ShapeDtypeStruct((M, N), BF16)
        out_spec = pl.BlockSpec((bm, bn), lambda i, j, r: (i, j))
    else:
        out_shape = jax.ShapeDtypeStruct((out_split, M, ncol), BF16)
        out_spec = pl.BlockSpec((None, bm, bn), lambda i, j, r: (j // per, i, j % per))
    a_spec = (pl.BlockSpec((bm, br), lambda i, j, r: (i, r)) if a_is_transposed
              else pl.BlockSpec((br, bm), lambda i, j, r: (r, i)))
    return pl.pallas_call(
        body, name=name, out_shape=out_shape, grid=(M // bm, N // bn, R // br),
        in_specs=[a_spec, pl.BlockSpec((br, bn), lambda i, j, r: (r, j))], out_specs=out_spec,
        scratch_shapes=[pltpu.VMEM((bm, bn), F32)],
        compiler_params=pltpu.CompilerParams(dimension_semantics=("parallel", "parallel", "arbitrary"),
                                             vmem_limit_bytes=VMEM_LIMIT),
    )(a, b)


def _mm_tn(a, b, name, out_split=1):
    return _mm(a, b, name, False, out_split)


def _mm_nn(at, b, name, out_split=1):
    return _mm(at, b, name, True, out_split)


def _mm_small(a, b, name):
    def body(a_ref, b_ref, o_ref):
        o_ref[...] = _dot(a_ref[...], b_ref[...]).astype(BF16)

    return pl.pallas_call(body, name=name, out_shape=jax.ShapeDtypeStruct((a.shape[0], b.shape[1]), BF16),
                          compiler_params=pltpu.CompilerParams(vmem_limit_bytes=VMEM_LIMIT))(a, b)


def _mlp_out(x_ref, wup_ref, wdown_ref, g_ref, b_ref, r_ref, h_ref):
    NC, _, FC = wup_ref.shape
    xv = x_ref[...]
    xb = xv.astype(BF16)
    y = jnp.zeros(xv.shape, F32)
    for c in range(NC):
        a = jnp.maximum(_dot(xb, wup_ref[c]), 0.0)
        hb = (a * a).astype(BF16)
        h_ref[:, c * FC:(c + 1) * FC] = hb
        y = y + _dot(hb, wdown_ref[c])
    r = ALPHA * xv + y
    r_ref[...] = r
    xhat, _ = _ln_stats(r)
    return xhat * g_ref[...] + b_ref[...]


def _mlp_fwd_loss(x, wup4, wdown4, g, b, target, name):
    S, D = x.shape
    NC, _, FC = wup4.shape
    tm = min(256, S)

    def body(x_ref, wup_ref, wdown_ref, g_ref, b_ref, t_ref, d_ref, r_ref, h_ref, l_ref):
        @pl.when(pl.program_id(0) == 0)
        def _():
            l_ref[...] = jnp.zeros_like(l_ref)

        err = _mlp_out(x_ref, wup_ref, wdown_ref, g_ref, b_ref, r_ref, h_ref) - t_ref[...]
        d_ref[...] = err * (1.0 / D)
        l_ref[...] += (0.5 / D) * jnp.sum(err * err)

    return pl.pallas_call(
        body, name=name, grid=(S // tm,),
        out_shape=(jax.ShapeDtypeStruct((S, D), F32), jax.ShapeDtypeStruct((S, D), F32),
                   jax.ShapeDtypeStruct((S, NC * FC), BF16), jax.ShapeDtypeStruct((8, LANES), F32)),
        in_specs=[_rows(tm, D), _full(wup4.shape), _full(wdown4.shape), _full((1, D)), _full((1, D)), _rows(tm, D)],
        out_specs=(_rows(tm, D), _rows(tm, D), _rows(tm, NC * FC), _full((8, LANES))),
        compiler_params=_params(),
    )(x, wup4, wdown4, g, b, target)


def _mlp_fwd(x, wup4, wdown4, g, b, name, comm=None):
    S, D = x.shape
    NC, _, FC = wup4.shape
    tm = min(256, S)

    def body(x_ref, wup_ref, wdown_ref, g_ref, b_ref, xo_ref, xb_ref, xbt_ref, r_ref, h_ref):
        xo = _mlp_out(x_ref, wup_ref, wdown_ref, g_ref, b_ref, r_ref, h_ref)
        xo_ref[...] = xo
        xb_ref[...] = xo.astype(BF16)
        xbt_ref[...] = xo.T.astype(BF16)

    return _pcall(
        body, name=name, grid=(S // tm,), comm=comm,
        out_shape=(jax.ShapeDtypeStruct((S, D), F32), jax.ShapeDtypeStruct((S, D), BF16),
                   jax.ShapeDtypeStruct((D, S), BF16), jax.ShapeDtypeStruct((S, D), F32),
                   jax.ShapeDtypeStruct((S, NC * FC), BF16)),
        in_specs=[_rows(tm, D), _full(wup4.shape), _full(wdown4.shape), _full((1, D)), _full((1, D))],
        out_specs=(_rows(tm, D), _rows(tm, D), _cols(D, tm), _rows(tm, D), _rows(tm, NC * FC)),
        args=(x, wup4, wdown4, g, b))


def _mlp_bwd(dxo, r, h, wup4, wdown4, g, name, comm=None):
    S, D = r.shape
    NC, _, FC = wup4.shape
    tm = min(256, S)

    def body(dxo_ref, r_ref, h_ref, wup_ref, wdown_ref, g_ref, dx_ref, drbt_ref, dpre_ref, dgb_ref):
        @pl.when(pl.program_id(0) == 0)
        def _():
            dgb_ref[...] = jnp.zeros_like(dgb_ref)

        xhat, rstd = _ln_stats(r_ref[...])
        dy = dxo_ref[...]
        dgb_ref[0:1, :] += jnp.sum(dy * xhat, axis=0, keepdims=True)
        dgb_ref[1:2, :] += jnp.sum(dy, axis=0, keepdims=True)
        dr = _ln_bwd(dy, xhat, rstd, g_ref[...])
        drb = dr.astype(BF16)
        drbt_ref[...] = dr.T.astype(BF16)
        dx = ALPHA * dr
        for c in range(NC):
            dh = _dot_nt(drb, wdown_ref[c])
            a = jnp.sqrt(h_ref[:, c * FC:(c + 1) * FC].astype(F32))
            dpre = (2.0 * a * dh).astype(BF16)
            dpre_ref[:, c * FC:(c + 1) * FC] = dpre
            dx = dx + _dot_nt(dpre, wup_ref[c])
        dx_ref[...] = dx

    return _pcall(
        body, name=name, grid=(S // tm,), comm=comm,
        out_shape=(jax.ShapeDtypeStruct((S, D), F32), jax.ShapeDtypeStruct((D, S), BF16),
                   jax.ShapeDtypeStruct((S, NC * FC), BF16), jax.ShapeDtypeStruct((2, D), F32)),
        in_specs=[_rows(tm, D), _rows(tm, D), _rows(tm, NC * FC), _full(wup4.shape), _full(wdown4.shape),
                  _full((1, D))],
        out_specs=(_rows(tm, D), _cols(D, tm), _rows(tm, NC * FC), _full((2, D))),
        args=(dxo, r, h, wup4, wdown4, g))


def _tail_fwd_store(xv, cat_ref, wo_ref, g_ref, b_ref, xo_ref, xbt_ref, r_ref, catt_ref):
    cat = cat_ref[...]
    y = _dot(cat, wo_ref[...])
    r = ALPHA * xv + y
    xhat, _ = _ln_stats(r)
    xo = xhat * g_ref[...] + b_ref[...]
    xo_ref[...] = xo
    xbt_ref[...] = xo.T.astype(BF16)
    catt_ref[...] = cat.astype(F32).T.astype(BF16)
    r_ref[...] = r


def _tail_bwd_head(dxo_ref, r_ref, g_ref, wo_ref, dgb_ref, drb_ref):
    xhat, rstd = _ln_stats(r_ref[...])
    dy = dxo_ref[...]
    dgb_ref[0:1, :] += jnp.sum(dy * xhat, axis=0, keepdims=True)
    dgb_ref[1:2, :] += jnp.sum(dy, axis=0, keepdims=True)
    dr = _ln_bwd(dy, xhat, rstd, g_ref[...])
    drb = dr.astype(BF16)
    drb_ref[...] = drb
    return dr, _dot_nt(drb, wo_ref[...])


def _causal(shape):
    return lax.broadcasted_iota(jnp.int32, shape, 1) <= lax.broadcasted_iota(jnp.int32, shape, 0)


def _sgu_mixed(ws_ref, bst_ref, gp, vpair):
    tril = _causal((CHUNK, CHUNK))
    w0 = jnp.where(tril, ws_ref[2 * gp], 0.0).astype(BF16)
    w1 = jnp.where(tril, ws_ref[2 * gp + 1], 0.0).astype(BF16)
    m0 = _dot(w0, vpair) + bst_ref[:, 2 * gp:2 * gp + 1]
    m1 = _dot(w1, vpair) + bst_ref[:, 2 * gp + 1:2 * gp + 2]
    return jnp.where(_first_head(), m0, m1), w0, w1


def _mixer_a_fwd(x, wa, sg, sb, ws, bst, mkv, wo, g, b, name, comm=None):
    S, D = x.shape
    DQ = mkv.shape[1] // 2
    DM = D - DQ
    NP = DM // PAIR
    tm = min(512, S)

    def body(x_ref, wa_ref, sg_ref, sb_ref, ws_ref, bst_ref, mkv_ref, wo_ref, g_ref, b_ref,
             xo_ref, xbt_ref, r_ref, catt_ref, xint_ref, zb_ref, z_s, zv_s, cat_ref):
        xv = x_ref[...]
        xint_ref[...] = xv.T.astype(BF16)
        z_s[...] = _dot(xv.astype(BF16), wa_ref[...])
        zb_ref[...] = z_s[...].astype(BF16)
        vhat, _ = _ln_stats(_gelu(z_s[:, DM:2 * DM]))
        zv_s[...] = (vhat * sg_ref[...] + sb_ref[...]).astype(BF16)
        for c in range(tm // CHUNK):
            rs = slice(c * CHUNK, (c + 1) * CHUNK)
            for gp in range(NP):
                ls = slice(gp * PAIR, (gp + 1) * PAIR)
                mixed, _, _ = _sgu_mixed(ws_ref, bst_ref, gp, zv_s[rs, ls])
                cat_ref[rs, ls] = (_gelu(z_s[rs, ls]) * mixed).astype(BF16)
        mo = _mem_attn_fwd(z_s[:, 2 * DM:].astype(BF16), mkv_ref, DQ)
        for mp in range(DQ // PAIR):
            cat_ref[:, DM + mp * PAIR:DM + (mp + 1) * PAIR] = mo[mp].astype(BF16)
        _tail_fwd_store(xv, cat_ref, wo_ref, g_ref, b_ref, xo_ref, xbt_ref, r_ref, catt_ref)

    return _pcall(
        body, name=name, grid=(S // tm,), comm=comm,
        out_shape=(jax.ShapeDtypeStruct((S, D), F32), jax.ShapeDtypeStruct((D, S), BF16),
                   jax.ShapeDtypeStruct((S, D), F32), jax.ShapeDtypeStruct((D, S), BF16),
                   jax.ShapeDtypeStruct((D, S), BF16), jax.ShapeDtypeStruct((S, 2 * DM + DQ), BF16)),
        in_specs=[_rows(tm, D), _full(wa.shape), _full(sg.shape), _full(sb.shape), _full(ws.shape), _full(bst.shape),
                  _full(mkv.shape), _full(wo.shape), _full((1, D)), _full((1, D))],
        out_specs=(_rows(tm, D), _cols(D, tm), _rows(tm, D), _cols(D, tm), _cols(D, tm), _rows(tm, 2 * DM + DQ)),
        scratch_shapes=[pltpu.VMEM((tm, 2 * DM + DQ), F32), pltpu.VMEM((tm, DM), BF16), pltpu.VMEM((tm, D), BF16)],
        args=(x, wa, sg, sb, ws, bst, mkv, wo, g, b))


def _mixer_a_bwd(zb, dxo, r, wa, sg, sb, ws, bst, mkv, wo, g, name, comm=None):
    S, D = r.shape
    DQ = mkv.shape[1] // 2
    DM = D - DQ
    NP = DM // PAIR
    NG = DM // HEAD_DIM
    tm = min(512, S)

    def body(zb_ref, dxo_ref, r_ref, wa_ref, sg_ref, sb_ref, ws_ref, bst_ref, mkv_ref, wo_ref, g_ref,
             dx_ref, dz_ref, drb_ref, dgb_ref, dsgb_ref, dws_ref, dbst_ref, dmkv_ref, z_s, zv_s, dzv_s):
        @pl.when(pl.program_id(0) == 0)
        def _():
            dgb_ref[...] = jnp.zeros_like(dgb_ref)
            dsgb_ref[...] = jnp.zeros_like(dsgb_ref)
            dws_ref[...] = jnp.zeros_like(dws_ref)
            dbst_ref[...] = jnp.zeros_like(dbst_ref)
            dmkv_ref[...] = jnp.zeros_like(dmkv_ref)

        dr, dcat = _tail_bwd_head(dxo_ref, r_ref, g_ref, wo_ref, dgb_ref, drb_ref)
        z_s[...] = zb_ref[...].astype(F32)
        vhat, vrstd = _ln_stats(_gelu(z_s[:, DM:2 * DM]))
        zv_s[...] = (vhat * sg_ref[...] + sb_ref[...]).astype(BF16)
        tril = _causal((CHUNK, CHUNK))
        lane = lax.broadcasted_iota(jnp.int32, (1, LANES), 1)
        for c in range(tm // CHUNK):
            rs = slice(c * CHUNK, (c + 1) * CHUNK)
            for gp in range(NP):
                ls = slice(gp * PAIR, (gp + 1) * PAIR)
                vpair = zv_s[rs, ls]
                mixed, w0, w1 = _sgu_mixed(ws_ref, bst_ref, gp, vpair)
                u_pre = z_s[rs, ls]
                dmix = dcat[rs, ls]
                dz_ref[rs, ls] = (dmix * mixed * _gelu_grad(u_pre)).astype(BF16)
                dmixed = dmix * _gelu(u_pre)
                first = _first_head()
                d0 = jnp.where(first, dmixed, 0.0)
                d1 = jnp.where(first, 0.0, dmixed)
                d0b = d0.astype(BF16)
                d1b = d1.astype(BF16)
                dzv_s[rs, ls] = _dot_tn(w0, d0b) + _dot_tn(w1, d1b)
                dws_ref[2 * gp] += jnp.where(tril, _dot_nt(d0b, vpair), 0.0)
                dws_ref[2 * gp + 1] += jnp.where(tril, _dot_nt(d1b, vpair), 0.0)
                dbst_ref[...] += (jnp.where(lane == 2 * gp, jnp.sum(d0, axis=-1, keepdims=True), 0.0)
                                  + jnp.where(lane == 2 * gp + 1, jnp.sum(d1, axis=-1, keepdims=True), 0.0))
        dzv = dzv_s[...]
        dsgb_ref[0:1, :] += jnp.sum(dzv * vhat, axis=0, keepdims=True)
        dsgb_ref[1:2, :] += jnp.sum(dzv, axis=0, keepdims=True)
        dgv = _ln_bwd(dzv, vhat, vrstd, sg_ref[...])
        dz_ref[:, DM:2 * DM] = (dgv * _gelu_grad(z_s[:, DM:2 * DM])).astype(BF16)
        dmo = [dcat[:, DM + mp * PAIR:DM + (mp + 1) * PAIR] for mp in range(DQ // PAIR)]
        dqm = _mem_attn_bwd(z_s[:, 2 * DM:].astype(BF16), mkv_ref, dmo, dmkv_ref, DQ)
        for mp in range(DQ // PAIR):
            dz_ref[:, 2 * DM + mp * PAIR:2 * DM + (mp + 1) * PAIR] = dqm[mp].astype(BF16)
        dx_ref[...] = ALPHA * dr + _dot_nt(dz_ref[...], wa_ref[...])

    ZW = 2 * DM + DQ
    M = mkv.shape[0]
    return _pcall(
        body, name=name, grid=(S // tm,), comm=comm,
        out_shape=(jax.ShapeDtypeStruct((S, D), F32), jax.ShapeDtypeStruct((S, ZW), BF16),
                   jax.ShapeDtypeStruct((S, D), BF16), jax.ShapeDtypeStruct((2, D), F32),
                   jax.ShapeDtypeStruct((2, DM), F32), jax.ShapeDtypeStruct((NG, CHUNK, CHUNK), F32),
                   jax.ShapeDtypeStruct((CHUNK, LANES), F32), jax.ShapeDtypeStruct((M, 2 * DQ), F32)),
        in_specs=[_rows(tm, ZW), _rows(tm, D), _rows(tm, D), _full(wa.shape), _full(sg.shape), _full(sb.shape),
                  _full(ws.shape), _full(bst.shape), _full(mkv.shape), _full(wo.shape), _full((1, D))],
        out_specs=(_rows(tm, D), _rows(tm, ZW), _rows(tm, D), _full((2, D)), _full((2, DM)),
                   _full((NG, CHUNK, CHUNK)), _full((CHUNK, LANES)), _full((M, 2 * DQ))),
        scratch_shapes=[pltpu.VMEM((tm, ZW), F32), pltpu.VMEM((tm, DM), BF16), pltpu.VMEM((tm, DM), F32)],
        args=(zb, dxo, r, wa, sg, sb, ws, bst, mkv, wo, g))


def _proj_b_fwd(xb, wq, wkv, dm, name):
    S, D = xb.shape
    DQ = D - dm
    tm = min(512, S)

    def body(x_ref, wq_ref, wkv_ref, q_ref, qm_ref, k_ref, v_ref, fl_ref):
        xv = x_ref[...]
        z = _dot(xv, wq_ref[...])
        q_ref[...] = z[:, :dm].astype(BF16)
        qm_ref[...] = z[:, dm:].astype(BF16)
        kvf = _dot(xv, wkv_ref[...])
        k_ref[...] = kvf[:, :dm].astype(BF16)
        v_ref[...] = kvf[:, dm:2 * dm].astype(BF16)
        fl_ref[...] = kvf[:, 2 * dm:]

    return pl.pallas_call(
        body, name=name, grid=(S // tm,),
        out_shape=(jax.ShapeDtypeStruct((S, dm), BF16), jax.ShapeDtypeStruct((S, DQ), BF16),
                   jax.ShapeDtypeStruct((S, dm), BF16), jax.ShapeDtypeStruct((S, dm), BF16),
                   jax.ShapeDtypeStruct((S, LANES), F32)),
        in_specs=[_rows(tm, D), _full(wq.shape), _full(wkv.shape)],
        out_specs=(_rows(tm, dm), _rows(tm, DQ), _rows(tm, dm), _rows(tm, dm), _rows(tm, LANES)),
        compiler_params=_params(),
    )(xb, wq, wkv)


def _proj_b_bwd(dr, dqz, dk, dv, dfl, wq, wkv, name):
    S, D = dr.shape
    dm = dk.shape[1]
    tm = min(512, S)

    def body(dr_ref, dqz_ref, dk_ref, dv_ref, dfl_ref, wq_ref, wkv_ref, dx_ref):
        dx_ref[...] = (ALPHA * dr_ref[...] + _dot_nt(dqz_ref[...], wq_ref[...])
                       + _dot_nt(dk_ref[...], wkv_ref[:, :dm]) + _dot_nt(dv_ref[...], wkv_ref[:, dm:2 * dm])
                       + _dot_nt(dfl_ref[...], wkv_ref[:, 2 * dm:]))

    return pl.pallas_call(
        body, name=name, grid=(S // tm,), out_shape=jax.ShapeDtypeStruct((S, D), F32),
        in_specs=[_rows(tm, D), _rows(tm, dqz.shape[1]), _rows(tm, dm), _rows(tm, dm), _rows(tm, dfl.shape[1]),
                  _full(wq.shape), _full(wkv.shape)],
        out_specs=_rows(tm, D), compiler_params=_params(),
    )(dr, dqz, dk, dv, dfl, wq, wkv)


def _mm_nn_cat(at, bs, name):
    M, R = at.shape
    widths = [b.shape[1] for b in bs]
    N = sum(widths)
    br = _pick(R, 1024)
    last = R // br - 1
    nb = len(bs)

    def body(a_ref, *refs):
        b_refs, o_ref, acc_ref = refs[:nb], refs[nb], refs[nb + 1]

        @pl.when(pl.program_id(0) == 0)
        def _():
            acc_ref[...] = jnp.zeros_like(acc_ref)

        a = a_ref[...]
        c0 = 0
        for b_ref, wd in zip(b_refs, widths):
            acc_ref[:, c0:c0 + wd] += _dot(a, b_ref[...])
            c0 += wd

        @pl.when(pl.program_id(0) == last)
        def _():
            o_ref[...] = acc_ref[...].astype(BF16)

    return pl.pallas_call(
        body, name=name, grid=(R // br,), out_shape=jax.ShapeDtypeStruct((M, N), BF16),
        in_specs=[pl.BlockSpec((M, br), lambda r: (0, r))] + [pl.BlockSpec((br, wd), lambda r: (r, 0)) for wd in widths],
        out_specs=_full((M, N)), scratch_shapes=[pltpu.VMEM((M, N), F32)], compiler_params=_params(),
    )(at, *bs)


def _split3(v):
    hi = v.astype(BF16)
    r1 = v - hi.astype(F32)
    mid = r1.astype(BF16)
    lo = (r1 - mid.astype(F32)).astype(BF16)
    return hi, mid, lo


def _tri_sum(v, tri_b):
    hi, mid, lo = _split3(v)
    return _dot(hi, tri_b) + _dot(mid, tri_b) + _dot(lo, tri_b)


def _log_sigmoid(x):
    return jnp.minimum(x, 0.0) - jnp.log(1.0 + jnp.exp(-jnp.abs(x)))


def _forget_cumsum(flt, bf, name):
    H, S = flt.shape
    nchunk = S // LANES

    def body(fl_ref, bf_ref, p_ref):
        upper = (lax.broadcasted_iota(jnp.int32, (LANES, LANES), 0)
                 <= lax.broadcasted_iota(jnp.int32, (LANES, LANES), 1)).astype(BF16)

        def step(n, carry):
            s0 = pl.multiple_of(n * LANES, LANES)
            lf = _log_sigmoid(fl_ref[:, pl.ds(s0, LANES)] + bf_ref[...])
            cs = _tri_sum(lf, upper) + carry
            for i, piece in enumerate(_split3(-cs)):
                p_ref[i, :, pl.ds(s0, LANES)] = piece.astype(F32)
            return cs[:, LANES - 1:LANES]

        lax.fori_loop(0, nchunk, step, jnp.zeros((H, 1), F32))

    return pl.pallas_call(body, name=name, out_shape=jax.ShapeDtypeStruct((3, H, S), F32),
                          compiler_params=pltpu.CompilerParams(vmem_limit_bytes=VMEM_LIMIT))(flt, bf)


def _forget_cumsum_bwd(dct, flt, bf, name):
    H, S = flt.shape
    nchunk = S // LANES

    def body(dc_ref, fl_ref, bf_ref, dfl_ref, dbf_ref):
        lower = (lax.broadcasted_iota(jnp.int32, (LANES, LANES), 0)
                 >= lax.broadcasted_iota(jnp.int32, (LANES, LANES), 1)).astype(BF16)

        def step(n, carry):
            tail, tot = carry
            s0 = pl.multiple_of((nchunk - 1 - n) * LANES, LANES)
            suffix = _tri_sum(dc_ref[:, pl.ds(s0, LANES)], lower) + tail
            xv = fl_ref[:, pl.ds(s0, LANES)] + bf_ref[...]
            dfl = suffix * (1.0 / (1.0 + jnp.exp(xv)))
            dfl_ref[:, pl.ds(s0, LANES)] = dfl
            return suffix[:, 0:1], tot + jnp.sum(dfl, axis=-1, keepdims=True)

        _, tot = lax.fori_loop(0, nchunk, step, (jnp.zeros((H, 1), F32), jnp.zeros((H, 1), F32)))
        dbf_ref[...] = jnp.broadcast_to(tot, (H, LANES))

    return pl.pallas_call(body, name=name,
                          out_shape=(jax.ShapeDtypeStruct((H, S), F32), jax.ShapeDtypeStruct((H, LANES), F32)),
                          compiler_params=pltpu.CompilerParams(vmem_limit_bytes=VMEM_LIMIT))(dct, flt, bf)


N_BIAS = 3


def _bias_ones():
    lane = lax.broadcasted_iota(jnp.int32, (1, PAIR), 1)
    return ((lane & (HEAD_DIM - 1)) < N_BIAS).astype(BF16)


def _fox_fwd(q, k, v, ca, name, comm=None):
    S, DM = q.shape
    NP = DM // PAIR
    T = min(FOX_BLOCK, S)
    nq = S // T

    def body(q_ref, k_ref, v_ref, ca_ref, o_ref, lse_ref):
        tril = _causal((T, T))
        heads = (_lane_mask(0), _lane_mask(1))
        ones3 = _bias_ones()

        def kv_block(kj):
            s0 = pl.multiple_of(kj * T, T)
            kb, vb, cab = k_ref[pl.ds(s0, T), :], v_ref[pl.ds(s0, T), :], ca_ref[pl.ds(s0, T), :]
            return ([jnp.where(heads[e], kb, cab) for e in (0, 1)],
                    [jnp.where(heads[e], vb, jnp.ones_like(vb)) for e in (0, 1)])

        def q_pair(a, _):
            t0s = [pl.multiple_of((2 * a + r) * T, T) for r in (0, 1)]
            qaug = {}
            for r in (0, 1):
                qb = q_ref[pl.ds(t0s[r], T), :] * QK_SCALE
                for e in (0, 1):
                    qaug[r, e] = jnp.where(heads[e], qb, ones3)
            chains = [(r, e) for r in (0, 1) for e in (0, 1)]

            def update(carries, first_block, seen):
                blocks = [kv_block(first_block + kk) for kk in range(1 + max(kk for s in seen for kk, _ in s))]
                out = []
                scores = [[_dot_nt(qaug[r, e], blocks[kk][0][e]) for kk, _ in seen[r]] for r, e in chains]
                for n, (r, e) in enumerate(chains):
                    m, acc = carries[n]
                    tiles = [jnp.where(tril, s, NEG) if masked else s for s, (_, masked) in zip(scores[n], seen[r])]
                    m_new = jnp.maximum(m, jnp.max(functools.reduce(jnp.maximum, tiles), axis=-1, keepdims=True))
                    acc = jnp.exp(m - m_new) * acc
                    for s, (kk, _) in zip(tiles, seen[r]):
                        acc = acc + _dot(jnp.exp(s - m_new).astype(BF16), blocks[kk][1][e])
                    out.append((m_new, acc))
                return tuple(out)

            def full(first_block, n_blocks, carries):
                return update(carries, first_block, [[(kk, False) for kk in range(n_blocks)]] * 2)

            init = (jnp.full((T, 1), NEG, F32), jnp.zeros((T, PAIR), F32))
            carries = lax.fori_loop(0, a // 2, lambda i, c: full(4 * i, 4, c), (init,) * 4)
            carries = lax.fori_loop(0, a % 2, lambda i, c: full(2 * a - 2, 2, c), carries)
            carries = update(carries, 2 * a, [[(0, True)], [(0, False), (1, True)]])
            first = _first_head()
            for r in (0, 1):
                (m0, acc0), (m1, acc1) = carries[2 * r], carries[2 * r + 1]
                l0, l1 = acc0[:, HEAD_DIM:HEAD_DIM + 1], acc1[:, 0:1]
                o_ref[pl.ds(t0s[r], T), :] = jnp.where(first, acc0 / l0, acc1 / l1).astype(BF16)
                lse_t = jnp.where(first, m0 + jnp.log(l0), m1 + jnp.log(l1)).T
                lse_ref[0:1, pl.ds(t0s[r], T)] = lse_t[0:1, :]
                lse_ref[1:2, pl.ds(t0s[r], T)] = lse_t[HEAD_DIM:HEAD_DIM + 1, :]
            return 0

        lax.fori_loop(0, nq // 2, q_pair, 0)

    col = pl.BlockSpec((S, PAIR), lambda hp: (0, hp))
    return _pcall(
        body, name=name, grid=(NP,), comm=comm,
        out_shape=(jax.ShapeDtypeStruct((S, DM), BF16), jax.ShapeDtypeStruct((NP, 2, S), F32)),
        in_specs=[col, col, col, col], out_specs=(col, pl.BlockSpec((None, 2, S), lambda hp: (hp, 0, 0))),
        args=(q, k, v, ca))


def _fox_bwd(q, k, v, o, do, ca, lse, name, comm=None):
    S, DM = q.shape
    NP = DM // PAIR
    T = min(FOX_BLOCK, S)
    nq = S // T

    def body(q_ref, k_ref, v_ref, o_ref, do_ref, ca_ref, lse_ref, dq_ref, dk_ref, dv_ref, dc_ref,
             dq_acc, dcol_acc, dk_acc, dv_acc, dd_s):
        keep = lax.broadcasted_iota(jnp.int32, (T, T), 0) <= lax.broadcasted_iota(jnp.int32, (T, T), 1)
        lane = lax.broadcasted_iota(jnp.int32, (1, LANES), 1)
        heads = (_lane_mask(0), _lane_mask(1))
        ones3 = _bias_ones()
        dq_acc[...] = jnp.zeros_like(dq_acc)
        dcol_acc[...] = jnp.zeros_like(dcol_acc)
        dc_ref[...] = jnp.zeros_like(dc_ref)

        def row_dots(qi, _):
            t0 = pl.multiple_of(qi * T, T)
            prod = do_ref[pl.ds(t0, T), :].astype(F32) * o_ref[pl.ds(t0, T), :].astype(F32)
            cols = [jnp.sum(jnp.where(heads[e], prod, 0.0), axis=-1, keepdims=True) for e in (0, 1)]
            tile = jnp.where(lane == 0, cols[0], jnp.where(lane == 1, cols[1], 0.0))
            dd_s[:, pl.ds(t0, T)] = tile.T[0:8, :]
            return 0

        lax.fori_loop(0, nq, row_dots, 0)

        def kv_pair(b, _):
            dk_acc[...] = jnp.zeros_like(dk_acc)
            dv_acc[...] = jnp.zeros_like(dv_acc)
            s0s = [pl.multiple_of((2 * b + jj) * T, T) for jj in (0, 1)]
            kaug, ks, vbs = {}, {}, []
            for jj in (0, 1):
                kb = k_ref[pl.ds(s0s[jj], T), :]
                cab = ca_ref[pl.ds(s0s[jj], T), :]
                vbs.append(v_ref[pl.ds(s0s[jj], T), :])
                for e in (0, 1):
                    kaug[jj, e] = jnp.where(heads[e], kb, cab)
                    ks[jj, e] = jnp.where(heads[e], kb, jnp.zeros_like(kb)) * QK_SCALE

            def q_step(qi, blocks):
                t0 = pl.multiple_of(qi * T, T)
                qs = q_ref[pl.ds(t0, T), :] * QK_SCALE
                dob = do_ref[pl.ds(t0, T), :]
                doh = [jnp.where(heads[e], dob, jnp.zeros_like(dob)) for e in (0, 1)]
                qh = [jnp.where(heads[e], qs, jnp.zeros_like(qs)) for e in (0, 1)]
                chains = [(e, jj, masked) for e in (0, 1) for jj, masked in blocks]
                scores = [_dot_nt(kaug[jj, e], jnp.where(heads[e], qs, ones3)) for e, jj, _ in chains]
                dps = [_dot_nt(vbs[jj], doh[e]) for e, jj, _ in chains]
                dq = jnp.zeros((T, PAIR), F32)
                drow = [jnp.zeros((1, T), F32), jnp.zeros((1, T), F32)]
                for n, (e, jj, masked) in enumerate(chains):
                    st = jnp.where(keep, scores[n], NEG) if masked else scores[n]
                    pt = jnp.exp(st - lse_ref[e:e + 1, pl.ds(t0, T)])
                    dv_acc[jj] += _dot(pt.astype(BF16), doh[e])
                    dst = pt * (dps[n] - dd_s[e:e + 1, pl.ds(t0, T)])
                    dsb = dst.astype(BF16)
                    dk_acc[jj] += _dot(dsb, qh[e])
                    dq = dq + _dot_tn(dsb, ks[jj, e])
                    dcol_acc[pl.ds(s0s[jj], T), :] -= jnp.where(lane == e, jnp.sum(dst, axis=-1, keepdims=True), 0.0)
                    drow[e] = drow[e] + jnp.sum(dst, axis=0, keepdims=True)
                for e in (0, 1):
                    dc_ref[e:e + 1, pl.ds(t0, T)] += drow[e]
                dq_acc[pl.ds(t0, T), :] += dq

            q_step(2 * b, [(0, True)])
            q_step(2 * b + 1, [(0, False), (1, True)])

            def rest(i, _):
                q_step(2 * b + 2 + 2 * i, [(0, False), (1, False)])
                q_step(2 * b + 3 + 2 * i, [(0, False), (1, False)])
                return 0

            lax.fori_loop(0, (nq - 2) // 2 - b, rest, 0)
            for jj in (0, 1):
                dk_ref[pl.ds(s0s[jj], T), :] = dk_acc[jj].astype(BF16)
                dv_ref[pl.ds(s0s[jj], T), :] = dv_acc[jj].astype(BF16)
            return 0

        lax.fori_loop(0, nq // 2, kv_pair, 0)
        dq_ref[...] = dq_acc[...].astype(BF16)

        def add_cols(kj, _):
            s0 = pl.multiple_of(kj * T, T)
            dc_ref[:, pl.ds(s0, T)] += dcol_acc[pl.ds(s0, T), :].T[0:2, :]
            return 0

        lax.fori_loop(0, nq, add_cols, 0)

    col = pl.BlockSpec((S, PAIR), lambda hp: (0, hp))
    row2 = pl.BlockSpec((None, 2, S), lambda hp: (hp, 0, 0))
    return _pcall(
        body, name=name, grid=(NP,), comm=comm,
        out_shape=(jax.ShapeDtypeStruct(do.shape, BF16), jax.ShapeDtypeStruct((S, DM), BF16),
                   jax.ShapeDtypeStruct((S, DM), BF16), jax.ShapeDtypeStruct((NP, 2, S), F32)),
        in_specs=[col, col, col, col, col, col, row2], out_specs=(col, col, col, row2),
        scratch_shapes=[pltpu.VMEM((S, PAIR), F32), pltpu.VMEM((S, LANES), F32), pltpu.VMEM((2, T, PAIR), F32),
                        pltpu.VMEM((2, T, PAIR), F32), pltpu.VMEM((8, S), F32)],
        args=(q, k, v, o, do, ca, lse), aliases={4: 0})


def _mixer_b_fwd(x, o, qm, mkv, wo, g, b, name):
    S, D = x.shape
    DQ = qm.shape[1]
    DM = D - DQ
    tm = min(512, S)

    def body(x_ref, o_ref, qm_ref, mkv_ref, wo_ref, g_ref, b_ref, xo_ref, xbt_ref, r_ref, catt_ref, cat_ref):
        cat_ref[:, :DM] = o_ref[...]
        mo = _mem_attn_fwd(qm_ref[...], mkv_ref, DQ)
        for mp in range(DQ // PAIR):
            cat_ref[:, DM + mp * PAIR:DM + (mp + 1) * PAIR] = mo[mp].astype(BF16)
        _tail_fwd_store(x_ref[...], cat_ref, wo_ref, g_ref, b_ref, xo_ref, xbt_ref, r_ref, catt_ref)

    return pl.pallas_call(
        body, name=name, grid=(S // tm,),
        out_shape=(jax.ShapeDtypeStruct((S, D), F32), jax.ShapeDtypeStruct((D, S), BF16),
                   jax.ShapeDtypeStruct((S, D), F32), jax.ShapeDtypeStruct((D, S), BF16)),
        in_specs=[_rows(tm, D), _rows(tm, DM), _rows(tm, DQ), _full(mkv.shape), _full(wo.shape), _full((1, D)),
                  _full((1, D))],
        out_specs=(_rows(tm, D), _cols(D, tm), _rows(tm, D), _cols(D, tm)),
        scratch_shapes=[pltpu.VMEM((tm, D), BF16)],
        compiler_params=_params(),
    )(x, o, qm, mkv, wo, g, b)


def _mixer_b_bwd(dxo, r, qm, mkv, wo, g, name, comm=None):
    S, D = r.shape
    DQ = qm.shape[1]
    DM = D - DQ
    M = mkv.shape[0]
    tm = min(512, S)

    def body(dxo_ref, r_ref, qm_ref, mkv_ref, wo_ref, g_ref, dr_ref, drb_ref, dcat_ref, dgb_ref, dmkv_ref):
        @pl.when(pl.program_id(0) == 0)
        def _():
            dgb_ref[...] = jnp.zeros_like(dgb_ref)
            dmkv_ref[...] = jnp.zeros_like(dmkv_ref)

        dr, dcat = _tail_bwd_head(dxo_ref, r_ref, g_ref, wo_ref, dgb_ref, drb_ref)
        dr_ref[...] = dr
        dcat_ref[:, :DM] = dcat[:, :DM].astype(BF16)
        dmo = [dcat[:, DM + mp * PAIR:DM + (mp + 1) * PAIR] for mp in range(DQ // PAIR)]
        dqm = _mem_attn_bwd(qm_ref[...], mkv_ref, dmo, dmkv_ref, DQ)
        for mp in range(DQ // PAIR):
            dcat_ref[:, DM + mp * PAIR:DM + (mp + 1) * PAIR] = dqm[mp].astype(BF16)

    return _pcall(
        body, name=name, grid=(S // tm,), comm=comm,
        out_shape=(jax.ShapeDtypeStruct((S, D), F32), jax.ShapeDtypeStruct((S, D), BF16),
                   jax.ShapeDtypeStruct((S, D), BF16), jax.ShapeDtypeStruct((2, D), F32),
                   jax.ShapeDtypeStruct((M, 2 * DQ), F32)),
        in_specs=[_rows(tm, D), _rows(tm, D), _rows(tm, DQ), _full(mkv.shape), _full(wo.shape), _full((1, D))],
        out_specs=(_rows(tm, D), _rows(tm, D), _rows(tm, D), _full((2, D)), _full((M, 2 * DQ))),
        args=(dxo, r, qm, mkv, wo, g))


class _NoExchange:
    def plan(self, stage, w, grads):
        return None

    def done(self, stage, results, w):
        pass


def _local_step(x, mem, target, w, hooks):
    S, D = x.shape
    DQ = w["mkv0"].shape[1] // 2
    DM = D - DQ
    NG = DM // HEAD_DIM
    NP = DM // PAIR
    ln_g, ln_b = w["ln_g"], w["ln_b"]
    row = lambda a, i: a[i:i + 1]
    memb = mem.astype(BF16)
    bst =jnp.pad(w["b_s"].T, ((0, 0), (0, LANES - NG)))
    grads = {}

    def staged(stage, fn, *args):
        out, got = fn(*args, stage, comm=hooks.plan(stage, w, grads))
        hooks.done(stage, got, w)
        return out

    mkv0 = _mm_small(memb, w["mkv0"], "mem_kv_0")
    x1, x1t, r1, cat0t, x0t, z0b = staged("mixer_a_fwd", _mixer_a_fwd, x, w["wa"], w["sg"], w["sb"], w["ws"], bst, mkv0,
                                w["wo0"], row(ln_g, 0), row(ln_b, 0))
    x2, x2b, x2t, r2, h0 = staged("mlp_fwd_0", _mlp_fwd, x1, w["wup0"], w["wdown0"], row(ln_g, 1), row(ln_b, 1))
    q, qm, k, v, fl = _proj_b_fwd(x2b, w["wq"], w["wkv"], DM, "proj_b_fwd")
    flt = fl[:, :16].T
    bfc = jnp.pad(w["bf"], (0, 16 - NG)).reshape(16, 1)
    pieces = _forget_cumsum(flt, bfc, "forget_cumsum")
    pieces = jnp.pad(pieces[:, :NG].transpose(2, 1, 0), ((0, 0), (0, 0), (0, HEAD_DIM - N_BIAS)))
    ca = pieces.reshape(S, NP, 2, HEAD_DIM)[:, :, ::-1].reshape(S, DM).astype(BF16)
    o, lse = staged("fox_fwd", _fox_fwd, q, k, v, ca)
    mkv1 = _mm_small(memb, w["mkv1"], "mem_kv_1")
    x3, x3t, r3, cat1t = _mixer_b_fwd(x2, o, qm, mkv1, w["wo1"], row(ln_g, 2), row(ln_b, 2), "mixer_b_fwd")
    dx4, r4, h1, loss_part = _mlp_fwd_loss(x3, w["wup1"], w["wdown1"], row(ln_g, 3), row(ln_b, 3), target,
                                           "mlp_fwd_1")

    nc = w["wup0"].shape[0]
    (dx3, dr4t, dpre1, dgb11), _ = _mlp_bwd(dx4, r4, h1, w["wup1"], w["wdown1"], row(ln_g, 3), "mlp_bwd_1")
    grads["wdown1"] = _mm_nn(dr4t, h1, "dw_down_1", out_split=nc)
    grads["wup1"] = _mm_nn(x3t, dpre1, "dw_up_1", out_split=nc)
    dr3, dr3b, dcat1, dgb10, dmkv1 = staged("mixer_b_bwd", _mixer_b_bwd, dx3, r3, qm, mkv1, w["wo1"], row(ln_g, 2))
    dqz, dk, dv, dct3 = staged("fox_bwd", _fox_bwd, q, k, v, o, dcat1, ca, lse)
    dct = jnp.pad(dct3.reshape(NG, S), ((0, 16 - NG), (0, 0)))
    dflt, dbf = _forget_cumsum_bwd(dct, flt, bfc, "forget_cumsum_bwd")
    dfl = jnp.pad(dflt.T, ((0, 0), (0, LANES - 16))).astype(BF16)
    dx2 = _proj_b_bwd(dr3, dqz, dk, dv, dfl, w["wq"], w["wkv"], "proj_b_bwd")
    grads["wo1"] = _mm_nn(cat1t, dr3b, "dw_o_1")
    grads["wq"] = _mm_nn(x2t, dqz, "dw_q")
    grads["wkv"] = _mm_nn_cat(x2t, [dk, dv, dfl], "dw_kv")
    grads["mkv1"] = _mm_tn(memb, dmkv1.astype(BF16), "dw_mkv_1")
    dx1, dr2t, dpre0, dgb01 = staged("mlp_bwd_0", _mlp_bwd, dx2, r2, h0, w["wup0"], w["wdown0"], row(ln_g, 1))
    grads["wdown0"] = _mm_nn(dr2t, h0, "dw_down_0", out_split=nc)
    grads["wup0"] = _mm_nn(x1t, dpre0, "dw_up_0", out_split=nc)
    dx0, dz, dr1b, dgb00, dsgb, dws, dbst, dmkv0 = staged(
        "mixer_a_bwd", _mixer_a_bwd, z0b, dx1, r1, w["wa"], w["sg"], w["sb"], w["ws"], bst, mkv0, w["wo0"], row(ln_g, 0))
    grads["wo0"] = _mm_nn(cat0t, dr1b, "dw_o_0")
    grads["wa"] = _mm_nn(x0t, dz, "dw_a")
    grads["mkv0"] = _mm_tn(memb, dmkv0.astype(BF16), "dw_mkv_0")
    grads.update({
        "ws": dws, "b_s": dbst[:, :NG].T, "sg": dsgb[0:1], "sb": dsgb[1:2], "bf": dbf[:NG, 0], "loss": loss_part[0:1],
        "ln_g": jnp.concatenate([dgb00[0:1], dgb01[0:1], dgb10[0:1], dgb11[0:1]], axis=0),
        "ln_b": jnp.concatenate([dgb00[1:2], dgb01[1:2], dgb10[1:2], dgb11[1:2]], axis=0),
    })
    return loss_part, dx0, grads


def _place():
    xi, yi, ci = lax.axis_index("x"), lax.axis_index("y"), lax.axis_index("c")
    peers = [(xi, 1 - yi), (1 - xi, yi), (1 - xi, 1 - yi)]
    return xi, yi, ci, peers


def _comm_call(body, name, ins, out_shapes, n_remote, n_local):
    return pl.pallas_call(
        body, name=name, out_shape=out_shapes, in_specs=[ANY] * len(ins), out_specs=[ANY] * len(out_shapes),
        scratch_shapes=[pltpu.SemaphoreType.DMA((n_remote,)), pltpu.SemaphoreType.DMA((n_remote,)),
                        pltpu.SemaphoreType.DMA((max(n_local, 1),))],
        compiler_params=pltpu.CompilerParams(has_side_effects=True),
    )(*ins)


def _pair_swap(xs):
    n = len(xs)
    out_shapes = [jax.ShapeDtypeStruct((a.shape[0],) + a.shape[2:], a.dtype) for a in xs]

    def copies(x_refs, o_refs, sems):
        send_sems, recv_sems = sems
        xi, yi, ci, _ = _place()
        return [pltpu.make_async_remote_copy(src_ref=x_refs[i].at[:, 1 - ci], dst_ref=o_refs[i],
                                             send_sem=send_sems.at[i], recv_sem=recv_sems.at[i],
                                             device_id=(xi, yi, 1 - ci), device_id_type=MESH) for i in range(n)]

    def start(x_refs, o_refs, sems):
        for cp in copies(x_refs, o_refs, sems):
            cp.start()

    def finish(x_refs, o_refs, sems):
        cps = copies(x_refs, o_refs, sems)
        for cp in cps:
            cp.wait_recv()
        for cp in cps:
            cp.wait_send()

    return _Comm(xs, out_shapes, {}, [n, n], start, finish)


def _chip_exchange(xs):
    n = len(xs)
    out_shapes = [jax.ShapeDtypeStruct((3,) + a.shape[1:], a.dtype) for a in xs]

    def copies(x_refs, o_refs, sems):
        send_sems, recv_sems = sems
        xi, yi, ci, peers = _place()
        return [pltpu.make_async_remote_copy(src_ref=x_refs[i].at[2 * px + py], dst_ref=o_refs[i].at[d],
                                             send_sem=send_sems.at[3 * i + d], recv_sem=recv_sems.at[3 * i + d],
                                             device_id=(px, py, ci), device_id_type=MESH)
                for i in range(n) for d, (px, py) in enumerate(peers)]

    def start(x_refs, o_refs, sems):
        for cp in copies(x_refs, o_refs, sems):
            cp.start()

    def finish(x_refs, o_refs, sems):
        cps = copies(x_refs, o_refs, sems)
        for cp in cps:
            cp.wait_recv()
        for cp in cps:
            cp.wait_send()

    return _Comm(xs, out_shapes, {}, [3 * n, 3 * n], start, finish)


def _inplace_call(body, name, bufs, sem_counts):
    n = len(bufs)
    return pl.pallas_call(
        body, name=name, out_shape=[jax.ShapeDtypeStruct(a.shape, a.dtype) for a in bufs],
        in_specs=[ANY] * n, out_specs=[ANY] * n, input_output_aliases={i: i for i in range(n)},
        scratch_shapes=[pltpu.SemaphoreType.DMA((k,)) for k in sem_counts],
        compiler_params=pltpu.CompilerParams(has_side_effects=True),
    )(*bufs)


def _gather_weights(bufs, whole=(), early=None):
    n, nw = len(bufs), len(whole)

    def copy(ref, s, r, k, dev):
        return pltpu.make_async_remote_copy(src_ref=ref, dst_ref=ref, send_sem=s.at[k], recv_sem=r.at[k],
                                            device_id=dev, device_id_type=MESH)

    def plan(w, sems):
        ici_s, ici_r, d2d_s, d2d_r = sems
        xi, yi, ci, peers = _place()
        xn, yn, sib = (1 - xi, yi, ci), (xi, 1 - yi, ci), (xi, yi, 1 - ci)
        j, jx, jy, jd = 2 * xi + yi, 2 * (1 - xi) + yi, 2 * xi + 1 - yi, 2 * (1 - xi) + 1 - yi
        c = {}
        for i in range(n):
            hh = bufs[i].shape[2] // 2
            lo, hi = pl.ds(0, hh), pl.ds(hh, hh)
            at = lambda slot, half, rows=None, i=i: w[i].at[slot, half] if rows is None else w[i].at[slot, half, rows]
            c["to_x", i] = (copy(at(j, ci), ici_s, ici_r, 4 * i, xn), copy(at(jx, ci), ici_s, ici_r, 4 * i, xn))
            c["to_y", i] = (copy(at(j, ci), ici_s, ici_r, 4 * i + 1, yn), copy(at(jy, ci), ici_s, ici_r, 4 * i + 1, yn))
            c["x_on", i] = (copy(at(jx, ci, lo), ici_s, ici_r, 4 * i + 2, yn), copy(at(jd, ci, lo), ici_s, ici_r, 4 * i + 2, yn))
            c["y_on", i] = (copy(at(jy, ci, hi), ici_s, ici_r, 4 * i + 3, xn), copy(at(jd, ci, hi), ici_s, ici_r, 4 * i + 3, xn))
            for k, slot in enumerate((jx, jy, jd)):
                c["sib", i, k] = (copy(at(slot, ci), d2d_s, d2d_r, 3 * i + k, sib),
                                  copy(at(slot, 1 - ci), d2d_s, d2d_r, 3 * i + k, sib))
        for i in range(nw):
            for d, (px, py) in enumerate(peers):
                k = 4 * n + 3 * i + d
                c["whole", i, d] = (copy(w[n + i].at[j], ici_s, ici_r, k, (px, py, ci)),
                                    copy(w[n + i].at[2 * px + py], ici_s, ici_r, k, (px, py, ci)))
        return c

    def start(_, w, sems):
        c = plan(w, sems)
        for i in range(n):
            c["to_x", i][0].start()
            c["to_y", i][0].start()
        for i in range(nw):
            for d in range(3):
                c["whole", i, d][0].start()

    def pass_on(c, which):
        for i in which:
            c["to_x", i][1].wait_recv()
            c["x_on", i][0].start()
            c["sib", i, 0][0].start()
            c["to_y", i][1].wait_recv()
            c["y_on", i][0].start()
            c["sib", i, 1][0].start()

    n_early = n if early is None else min(early, n)

    def middle(_, w, sems):
        pass_on(plan(w, sems), range(n_early))

    def finish(_, w, sems):
        c = plan(w, sems)
        pass_on(c, range(n_early, n))
        for i in range(n):
            c["x_on", i][1].wait_recv()
            c["y_on", i][1].wait_recv()
            c["sib", i, 2][0].start()
        for i in range(n):
            for k in range(3):
                c["sib", i, k][1].wait_recv()
        for i in range(nw):
            for d in range(3):
                c["whole", i, d][1].wait_recv()
        for key, (sent, _) in c.items():
            sent.wait_send()

    arrays = list(bufs) + list(whole)
    k = 4 * n + 3 * nw
    return _Comm(arrays, [jax.ShapeDtypeStruct(a.shape, a.dtype) for a in arrays], {i: i for i in range(n + nw)},
                 [k, k, max(3 * n, 1), max(3 * n, 1)], start, finish, middle)


def _pair_gather(bufs, spread, name):
    n = len(bufs)
    where = [(i, l) for i, a in enumerate(bufs) for l in range(a.shape[0])]
    nw = len(where)
    arrays = list(bufs) + ([] if spread is None else [spread])
    na = len(arrays)

    def body(*refs):
        g = refs[na:na + n]
        sp = refs[na + n] if spread is not None else None
        send_sems, recv_sems = refs[2 * na:]
        xi, yi, ci, peers = _place()
        chips = [(xi, yi)] + peers
        others = [(d, f) for d in range(4) for f in (0, 1) if (d, f) != (0, 0)] if spread is not None else []
        sends = []
        for k, (i, l) in enumerate(where):
            mine = g[i].at[l, :, ci]
            cp = pltpu.make_async_remote_copy(src_ref=mine, dst_ref=mine, send_sem=send_sems.at[k],
                                              recv_sem=recv_sems.at[k], device_id=(xi, yi, 1 - ci), device_id_type=MESH)
            cp.start()
            sends.append(cp)
        for k, (d, f) in enumerate(others):
            mine = sp.at[2 * xi + yi, ci]
            cp = pltpu.make_async_remote_copy(src_ref=mine, dst_ref=mine, send_sem=send_sems.at[nw + k],
                                              recv_sem=recv_sems.at[nw + k],
                                              device_id=(chips[d][0], chips[d][1], ci if f == 0 else 1 - ci),
                                              device_id_type=MESH)
            cp.start()
            sends.append(cp)
        for k, (i, l) in enumerate(where):
            other = g[i].at[l, :, 1 - ci]
            pltpu.make_async_remote_copy(src_ref=other, dst_ref=other, send_sem=send_sems.at[k],
                                         recv_sem=recv_sems.at[k], device_id=(xi, yi, 1 - ci),
                                         device_id_type=MESH).wait_recv()
        for k, (d, f) in enumerate(others):
            px, py, pc = chips[d][0], chips[d][1], (ci if f == 0 else 1 - ci)
            theirs = sp.at[2 * px + py, pc]
            pltpu.make_async_remote_copy(src_ref=theirs, dst_ref=theirs, send_sem=send_sems.at[nw + k],
                                         recv_sem=recv_sems.at[nw + k], device_id=(px, py, pc),
                                         device_id_type=MESH).wait_recv()
        for cp in sends:
            cp.wait_send()

    n_sems = nw + (7 if spread is not None else 0)
    out = _inplace_call(body, name, arrays, [n_sems, n_sems])
    return out[:n], (out[n] if spread is not None else None)


def _row_block(rows, cols):
    want = max(8, (2 ** 18 // max(cols, 1)) // 8 * 8)
    if rows <= want:
        return rows
    best = 8
    for t in range(8, want + 1, 8):
        if rows % t == 0:
            best = t
    return best


def _pair_add(g4, recv, sel, out_dtype, name):
    A, _, H, C = g4.shape
    bh = _row_block(H, C)

    def body(sel_ref, g_ref, r_ref, o_ref):
        o_ref[...] = (g_ref[...].astype(F32) + r_ref[...].astype(F32)).astype(out_dtype)

    return pl.pallas_call(
        body, name=name, out_shape=jax.ShapeDtypeStruct((A, H, C), out_dtype),
        grid_spec=pltpu.PrefetchScalarGridSpec(
            num_scalar_prefetch=1, grid=(A, H // bh),
            in_specs=[pl.BlockSpec((None, None, bh, C), lambda a, i, s: (a, s[1], i, 0)),
                      pl.BlockSpec((None, bh, C), lambda a, i, s: (a, i, 0))],
            out_specs=pl.BlockSpec((None, bh, C), lambda a, i, s: (a, i, 0))),
        compiler_params=pltpu.CompilerParams(dimension_semantics=("arbitrary", "arbitrary"),
                                             vmem_limit_bytes=VMEM_LIMIT),
    )(sel, g4, recv)


def _chip_reduce(g4, recv1, recv2, sel, buf, layer, n_layers, name, by_chip=False):
    _, _, H, C = g4.shape
    bh = _row_block(H, C)
    if by_chip:
        out_shape = jax.ShapeDtypeStruct((N_CHIPS, 2, H, C), F32)
        out_spec = pl.BlockSpec((None, None, bh, C), lambda i, s: (s[0], s[1], i, 0))
    else:
        out_shape = jax.ShapeDtypeStruct((n_layers, 1, 2, H, C), F32)
        out_spec = pl.BlockSpec((None, None, None, bh, C), lambda i, s: (layer, 0, s[1], i, 0))

    def body(sel_ref, g_ref, r1_ref, r2_ref, *rest):
        acc = g_ref[...].astype(F32) + r1_ref[...].astype(F32)
        for d in range(3):
            acc = acc + r2_ref[d].astype(F32)
        rest[-1][...] = acc

    in_specs = [pl.BlockSpec((None, None, bh, C), lambda i, s: (s[0], s[1], i, 0)),
                pl.BlockSpec((None, bh, C), lambda i, s: (s[0], i, 0)),
                pl.BlockSpec((3, bh, C), lambda i, s: (0, i, 0))]
    args = [sel, g4, recv1, recv2]
    aliases = {}
    if buf is not None:
        in_specs.append(ANY)
        args.append(buf)
        aliases = {4: 0}
    return pl.pallas_call(
        body, name=name, out_shape=out_shape,
        grid_spec=pltpu.PrefetchScalarGridSpec(num_scalar_prefetch=1, grid=(H // bh,), in_specs=in_specs,
                                               out_specs=out_spec),
        input_output_aliases=aliases,
        compiler_params=pltpu.CompilerParams(dimension_semantics=("arbitrary",), vmem_limit_bytes=VMEM_LIMIT),
    )(*args)


def _cast_place(shards, layer, sel, name):
    _, _, H, C = shards.shape
    bh = _row_block(H, C)

    def body(sel_ref, x_ref, o_ref):
        o_ref[...] = x_ref[...].astype(BF16)

    return pl.pallas_call(
        body, name=name, out_shape=jax.ShapeDtypeStruct((N_CHIPS, 2, H, C), BF16),
        grid_spec=pltpu.PrefetchScalarGridSpec(
            num_scalar_prefetch=1, grid=(2, H // bh),
            in_specs=[pl.BlockSpec((None, None, bh, C), lambda h, i, s: (layer, h, i, 0))],
            out_specs=pl.BlockSpec((None, None, bh, C), lambda h, i, s: (s[0], h, i, 0))),
        compiler_params=pltpu.CompilerParams(dimension_semantics=("arbitrary", "arbitrary"),
                                             vmem_limit_bytes=VMEM_LIMIT),
    )(sel, shards)


def _cast_place_many(pairs, sel, comm, name):
    n = len(pairs)
    _, _, H, C = pairs[0][0].shape
    bh = _row_block(H, C)
    last = H // bh - 1
    nci, nco = len(comm.ins), len(comm.out_shapes)

    def body(sel_ref, *refs):
        xs, refs = refs[:n], refs[n:]
        cins, refs = refs[:nci], refs[nci:]
        outs, refs = refs[:n], refs[n:]
        couts, sems = refs[:nco], refs[nco:]
        h, i = pl.program_id(0), pl.program_id(1)

        @pl.when((h == 0) & (i == 0))
        def _():
            comm.start(cins, couts, sems)

        if comm.middle is not None:
            @pl.when((h == 1) & (i == 0))
            def _():
                comm.middle(cins, couts, sems)

        for x_ref, o_ref in zip(xs, outs):
            o_ref[...] = x_ref[...].astype(BF16)

        @pl.when((h == 1) & (i == last))
        def _():
            comm.finish(cins, couts, sems)

    res = pl.pallas_call(
        body, name=name,
        out_shape=[jax.ShapeDtypeStruct((N_CHIPS, 2, H, C), BF16)] * n + list(comm.out_shapes),
        grid_spec=pltpu.PrefetchScalarGridSpec(
            num_scalar_prefetch=1, grid=(2, H // bh),
            in_specs=[pl.BlockSpec((None, None, bh, C), lambda h, i, s, layer=layer: (layer, h, i, 0))
                      for _, layer in pairs] + [ANY] * nci,
            out_specs=[pl.BlockSpec((None, None, bh, C), lambda h, i, s: (s[0], h, i, 0))] * n + [ANY] * nco,
            scratch_shapes=[pltpu.SemaphoreType.DMA((k,)) for k in comm.sem_counts]),
        input_output_aliases={1 + n + a: n + b for a, b in comm.aliases.items()},
        compiler_params=pltpu.CompilerParams(dimension_semantics=("arbitrary", "arbitrary"),
                                             vmem_limit_bytes=VMEM_LIMIT, has_side_effects=True),
    )(sel, *[s for s, _ in pairs], *comm.ins)
    return res[:n], res[n:]


def _adamw_many(quads, name, comm=None):
    R, C = quads[0][0].shape
    br = _row_block(R, C)
    n = len(quads)

    def body(*refs):
        for k in range(n):
            g_ref, w_ref, m_ref, v_ref = refs[4 * k:4 * k + 4]
            d_ref, mo_ref, vo_ref = refs[4 * n + 3 * k:4 * n + 3 * k + 3]
            gv = g_ref[...]
            mn = ADAM_B1 * m_ref[...] + (1.0 - ADAM_B1) * gv
            vn = ADAM_B2 * v_ref[...] + (1.0 - ADAM_B2) * (gv * gv)
            m_hat = mn / (1.0 - ADAM_B1 ** ADAM_STEP)
            v_hat = vn / (1.0 - ADAM_B2 ** ADAM_STEP)
            d_ref[...] = -ADAM_LR * (m_hat / (jnp.sqrt(v_hat) + ADAM_EPS) + ADAM_WD * w_ref[...])
            mo_ref[...] = mn
            vo_ref[...] = vn

    spec = pl.BlockSpec((br, C), lambda i: (i, 0))
    res, got = _pcall(body, name=name, grid=(R // br,), comm=comm, out_shape=(jax.ShapeDtypeStruct((R, C), F32),) * (3 * n),
                      in_specs=[spec] * (4 * n), out_specs=(spec,) * (3 * n), args=[a for q in quads for a in q])
    return [res[3 * k:3 * k + 3] for k in range(n)], got


def _adamw(g, w, m, v, name):
    return _adamw_many([(g, w, m, v)], name)[0][0]


def _part_rows(shape):
    return -(-math.prod(shape) // (8 * LANES)) * 8


def _pack_rows(arrs, total_rows):
    parts = []
    for a in arrs:
        flat = a.reshape(-1)
        rows = _part_rows(a.shape)
        parts.append(jnp.pad(flat, (0, rows * LANES - flat.shape[0])).reshape(rows, LANES))
    packed = jnp.concatenate(parts, axis=0)
    return jnp.pad(packed, ((0, total_rows - packed.shape[0]), (0, 0)))


def _unpack_rows(packed, shapes):
    out, r = [], 0
    for shp in shapes:
        size, rows = math.prod(shp), _part_rows(shp)
        out.append(packed[r:r + rows].reshape(-1)[:size].reshape(shp))
        r += rows
    return out


def _rows_of(shapes):
    return sum(_part_rows(s) for s in shapes)


_GATHER_STAGES = {"start": ["wa", "mkv0", "wo0"], "mixer_a_fwd": ["wup0", "wdown0"],
                  "mlp_fwd_0": ["wq", "wkv", "mkv1", "wo1"], "fox_fwd": ["wup1", "wdown1"]}
_SWAP_STAGES = {"mixer_b_bwd": ["wup1", "wdown1"]}
_REDUCE_STAGES = {"fox_bwd": ["wup1", "wdown1"], "mlp_bwd_0": ["wo1", "wq", "wkv", "mkv1"],
                  "mixer_a_bwd": ["wup0", "wdown0"], "end": ["wa", "wo0", "mkv0", "small"]}
_SMALL = ["ws", "b_s", "sg", "sb", "bf", "ln_g", "ln_b", "loss"]


class _Staged:
    def __init__(self, placed, sel, dm, ng, shard_cols, n_small):
        self.placed, self.sel, self.dm, self.ng, self.shard_cols, self.n_small = placed, sel, dm, ng, shard_cols, n_small
        self.g4, self.recv1, self.recv2 = {}, {}, {}

    def _weight(self, name, a):
        a = a.reshape(N_CHIPS, 2 * a.shape[2], a.shape[3])
        if name in ("wa", "wkv"):
            a = a.transpose(1, 0, 2).reshape(a.shape[1], -1)
            if name == "wkv":
                a = jnp.concatenate([a[:, :2 * self.dm],
                                     jnp.pad(a[:, 2 * self.dm:], ((0, 0), (0, LANES - self.ng)))], axis=1)
            return a
        if name.startswith("wup") or name.startswith("wdown"):
            return a
        return a.reshape(-1, a.shape[2])

    def _partial(self, name, grads):
        if name == "small":
            a = _pack_rows([grads[n] for n in _SMALL], self.n_small).reshape(N_CHIPS, -1, LANES)
        elif name in self.shard_cols:
            n = self.shard_cols[name]
            g = grads[name][:, :N_CHIPS * n]
            a = g.reshape(g.shape[0], N_CHIPS, n).transpose(1, 0, 2)
        elif name.startswith("wup") or name.startswith("wdown"):
            a = grads[name]
        else:
            g = grads[name]
            a = g.reshape(N_CHIPS, g.shape[0] // N_CHIPS, g.shape[1])
        return a.reshape(N_CHIPS, 2, a.shape[1] // 2, a.shape[2])

    def plan(self, stage, w, grads):
        if stage in _GATHER_STAGES:
            return _gather_weights([self.placed[n] for n in _GATHER_STAGES[stage]],
                                   early=1 if stage == "mixer_a_fwd" else None)
        if stage in _SWAP_STAGES:
            for n in _SWAP_STAGES[stage]:
                self.g4[n] = self._partial(n, grads)
            return _pair_swap([self.g4[n] for n in _SWAP_STAGES[stage]])
        if stage in _REDUCE_STAGES:
            names = _REDUCE_STAGES[stage]
            late = [n for n in names if n not in self.recv1]
            for n in late:
                self.g4[n] = self._partial(n, grads)
            if late:
                got = _run_comm(_pair_swap([self.g4[n] for n in late]), "reduce_pair_swap_" + stage)
                self.recv1.update(zip(late, got))
            return _chip_exchange([_pair_add(self.g4[n], self.recv1[n], self.sel, F32 if n == "small" else BF16,
                                             "reduce_pair_add_" + n) for n in names])
        return None

    def done(self, stage, results, w):
        if stage in _GATHER_STAGES:
            for n, a in zip(_GATHER_STAGES[stage], results):
                w[n] = self._weight(n, a)
        elif stage in _SWAP_STAGES:
            self.recv1.update(zip(_SWAP_STAGES[stage], results))
        elif stage in _REDUCE_STAGES:
            self.recv2.update(zip(_REDUCE_STAGES[stage], results))


def kernel(x, mem, a_w_in, a_sgu_ln_g, a_sgu_ln_b, a_w_s, a_b_s, kv_w, kv_b_f, b_w_q, mem_w_kv, w_o, ln_g, ln_b, w_up, w_down, loss_target, m_a_w_in, m_a_sgu_ln_g, m_a_sgu_ln_b, m_a_w_s, m_a_b_s, m_kv_w, m_kv_b_f, m_b_w_q, m_mem_w_kv, m_w_o, m_ln_g, m_ln_b, m_w_up, m_w_down, v_a_w_in, v_a_sgu_ln_g, v_a_sgu_ln_b, v_a_w_s, v_a_b_s, v_kv_w, v_kv_b_f, v_b_w_q, v_mem_w_kv, v_w_o, v_ln_g, v_ln_b, v_w_up, v_w_down):
    xi, yi, ci = lax.axis_index("x"), lax.axis_index("y"), lax.axis_index("c")
    chip = 2 * xi + yi
    sel = jnp.stack([chip, ci]).astype(jnp.int32)
    S, D = x.shape[1], x.shape[2]
    DQ = mem_w_kv.shape[2] // 2
    DM = D - DQ
    NG = DM // HEAD_DIM

    shards = {"wa": (a_w_in, 0), "wkv": (kv_w[None], 0), "wq": (b_w_q, 0), "mkv0": (mem_w_kv, 0), "mkv1": (mem_w_kv, 1),
              "wo0": (w_o, 0), "wo1": (w_o, 1), "wup0": (w_up, 0), "wup1": (w_up, 1), "wdown0": (w_down, 0),
              "wdown1": (w_down, 1)}
    halves = lambda s: s.reshape(s.shape[0], 2, s.shape[1] // 2, s.shape[2])
    mlp = ["wup0", "wup1", "wdown0", "wdown1"]
    placed = {n: _cast_place(halves(s), layer, sel, "cast_place_" + n)
              for n, (s, layer) in shards.items() if n not in mlp}
    ln_pack = jnp.concatenate([ln_g.reshape(4, -1), ln_b.reshape(4, -1)], axis=0)
    ln_buf = lax.dynamic_update_slice(jnp.zeros((N_CHIPS,) + ln_pack.shape, F32), ln_pack[None], (chip, 0, 0))
    mlp_placed, got = _cast_place_many(
        [(halves(shards[n][0]), shards[n][1]) for n in mlp], sel,
        _gather_weights([placed[n] for n in _GATHER_STAGES["start"]], whole=[ln_buf]), "cast_place_mlp")
    placed.update(zip(mlp, mlp_placed))
    ln_shape = (4, D)
    small_shapes = [a_w_s.shape, a_b_s.shape, a_sgu_ln_g.shape, a_sgu_ln_b.shape, kv_b_f.shape, ln_shape, ln_shape,
                    (1, LANES)]
    n_small = -(-_rows_of(small_shapes) // 64) * 64
    hooks = _Staged(placed, sel, DM, NG, {"wa": a_w_in.shape[2], "wkv": kv_w.shape[1]}, n_small)
    weights = {"sg": a_sgu_ln_g, "sb": a_sgu_ln_b, "ws": a_w_s[0], "b_s": a_b_s[0], "bf": kv_b_f}
    hooks.done("start", got[:-1], weights)
    ln_full = got[-1].transpose(1, 0, 2).reshape(8, D)
    weights["ln_g"], weights["ln_b"] = ln_full[:4], ln_full[4:]

    _, grad_x, gr = _local_step(x[0], mem[0], loss_target[0], weights, hooks)

    def chip_sums(groups):
        bufs = []
        for grp in groups:
            buf = None
            for layer, n in enumerate(grp):
                buf = _chip_reduce(hooks.g4[n], hooks.recv1[n], hooks.recv2[n], sel, buf, layer, len(grp),
                                   "reduce_chip_sum_" + n)
            bufs.append(buf)
        return bufs

    whole = lambda a: a.reshape(a.shape[0], 2 * a.shape[3], a.shape[4])
    flat = lambda a: a.reshape(-1, a.shape[-1])

    def update(name, g, w, m, v):
        d, mn, vn = _adamw(flat(g), flat(w), flat(m), flat(v), "adamw_" + name)
        return d.reshape(w.shape), mn.reshape(w.shape), vn.reshape(w.shape)

    hooks.done("end", _run_comm(hooks.plan("end", weights, gr), "reduce_chip_exchange_end"), weights)
    small_mine = _chip_reduce(hooks.g4["small"], hooks.recv1["small"], hooks.recv2["small"], sel, None, 0, 1,
                              "reduce_chip_sum_small", by_chip=True)
    red, small_all = _pair_gather(chip_sums([["wa"], ["wkv"], ["wq"], ["mkv0", "mkv1"], ["wo0", "wo1"],
                                             ["wup0", "wup1"], ["wdown0", "wdown1"]]), small_mine, "reduce_pair_gather")
    g_a_w_in, g_kv_w, g_b_w_q, g_mem_w_kv, g_w_o, g_w_up, g_w_down = [whole(a) for a in red]
    g_kv_w = g_kv_w[0]
    g_w_down = g_w_down.transpose(0, 2, 1)
    small_all = small_all.reshape(n_small, LANES)
    g_ws, g_bs, g_sg, g_sb, g_bf, g_lng, g_lnb, loss_row = _unpack_rows(small_all, small_shapes)
    loss = loss_row[0, 0]
    dsh = D // N_CHIPS
    g_ln_g = lax.dynamic_slice_in_dim(g_lng, chip * dsh, dsh, axis=1).reshape(ln_g.shape)
    g_ln_b = lax.dynamic_slice_in_dim(g_lnb, chip * dsh, dsh, axis=1).reshape(ln_b.shape)

    upd = {
        "a_w_in": update("a_w_in", g_a_w_in, a_w_in, m_a_w_in, v_a_w_in),
        "kv_w": update("kv_w", g_kv_w, kv_w, m_kv_w, v_kv_w),
        "b_w_q": update("b_w_q", g_b_w_q, b_w_q, m_b_w_q, v_b_w_q),
        "mem_w_kv": update("mem_w_kv", g_mem_w_kv, mem_w_kv, m_mem_w_kv, v_mem_w_kv),
        "w_o": update("w_o", g_w_o, w_o, m_w_o, v_w_o),
        "w_up": update("w_up", g_w_up, w_up, m_w_up, v_w_up),
        "w_down": update("w_down", g_w_down, w_down, m_w_down, v_w_down),
    }
    tiny_g = [g_sg, g_sb, g_ws, g_bs, g_bf, g_ln_g, g_ln_b]
    tiny_w = [a_sgu_ln_g, a_sgu_ln_b, a_w_s, a_b_s, kv_b_f, ln_g, ln_b]
    tiny_m = [m_a_sgu_ln_g, m_a_sgu_ln_b, m_a_w_s, m_a_b_s, m_kv_b_f, m_ln_g, m_ln_b]
    tiny_v = [v_a_sgu_ln_g, v_a_sgu_ln_b, v_a_w_s, v_a_b_s, v_kv_b_f, v_ln_g, v_ln_b]
    tiny_shapes = [a.shape for a in tiny_w]
    n_tiny = -(-_rows_of(tiny_shapes) // 8) * 8
    td, tm_, tv = _adamw(_pack_rows(tiny_g, n_tiny), _pack_rows(tiny_w, n_tiny), _pack_rows(tiny_m, n_tiny),
                         _pack_rows(tiny_v, n_tiny), "adamw_small")
    for name, d, mn, vn in zip(["a_sgu_ln_g", "a_sgu_ln_b", "a_w_s", "a_b_s", "kv_b_f", "ln_g", "ln_b"],
                               _unpack_rows(td, tiny_shapes), _unpack_rows(tm_, tiny_shapes),
                               _unpack_rows(tv, tiny_shapes)):
        upd[name] = (d, mn, vn)

    order = ["a_w_in", "a_sgu_ln_g", "a_sgu_ln_b", "a_w_s", "a_b_s", "kv_w", "kv_b_f", "b_w_q", "mem_w_kv", "w_o",
             "ln_g", "ln_b", "w_up", "w_down"]
    grads = {"a_w_in": g_a_w_in.reshape(a_w_in.shape), "a_sgu_ln_g": g_sg.reshape(a_sgu_ln_g.shape),
             "a_sgu_ln_b": g_sb.reshape(a_sgu_ln_b.shape), "a_w_s": g_ws.reshape(a_w_s.shape),
             "a_b_s": g_bs.reshape(a_b_s.shape), "kv_w": g_kv_w, "kv_b_f": g_bf, "b_w_q": g_b_w_q.reshape(b_w_q.shape),
             "mem_w_kv": g_mem_w_kv, "w_o": g_w_o, "ln_g": g_ln_g, "ln_b": g_ln_b, "w_up": g_w_up, "w_down": g_w_down}
    return (loss, grad_x[None], *[grads[n] for n in order], *[upd[n][0] for n in order],
            *[upd[n][1] for n in order], *[upd[n][2] for n in order])
```

```python
import functools
import math

import jax
import jax.numpy as jnp
from jax import lax
from jax.experimental import pallas as pl
from jax.experimental.pallas import tpu as pltpu

F32 = jnp.float32
BF16 = jnp.bfloat16

HEAD_DIM = 64
PAIR = 2 * HEAD_DIM
CHUNK = 128
LN_EPS = 1e-5
ALPHA = 4 ** 0.25
QK_SCALE = 1.0 / math.sqrt(HEAD_DIM)
NEG = -1e30
N_CHIPS = 4
FOX_BLOCK = 256
LANES = 128
VMEM_LIMIT = 48 * 2 ** 20

ADAM_LR, ADAM_B1, ADAM_B2, ADAM_EPS, ADAM_WD, ADAM_STEP = 0.001, 0.9, 0.999, 1e-08, 0.01, 10

MESH = pl.DeviceIdType.MESH
ANY = pl.BlockSpec(memory_space=pl.ANY)


def _dot(a, b):
    return jnp.dot(a, b, preferred_element_type=F32)


def _dot_nt(a, b):
    return lax.dot_general(a, b, (((1,), (1,)), ((), ())), preferred_element_type=F32)


def _dot_tn(a, b):
    return lax.dot_general(a, b, (((0,), (0,)), ((), ())), preferred_element_type=F32)


def _ln_stats(r):
    mu = jnp.mean(r, axis=-1, keepdims=True)
    d = r - mu
    var = jnp.mean(d * d, axis=-1, keepdims=True)
    rstd = lax.rsqrt(var + LN_EPS)
    return d * rstd, rstd


def _ln_bwd(dy, xhat, rstd, g):
    dxh = dy * g
    m1 = jnp.mean(dxh, axis=-1, keepdims=True)
    m2 = jnp.mean(dxh * xhat, axis=-1, keepdims=True)
    return rstd * (dxh - m1 - xhat * m2)


_GELU_K = math.sqrt(2.0 / math.pi)


def _gelu(x):
    return 0.5 * x * (1.0 + jnp.tanh(_GELU_K * (x + 0.044715 * x * x * x)))


def _gelu_grad(x):
    t = jnp.tanh(_GELU_K * (x + 0.044715 * x * x * x))
    return 0.5 * (1.0 + t) + 0.5 * x * (1.0 - t * t) * _GELU_K * (1.0 + 3 * 0.044715 * x * x)


def _lane_mask(e):
    lane = lax.broadcasted_iota(jnp.int32, (1, PAIR), 1)
    return (lane >= HEAD_DIM * e) & (lane < HEAD_DIM * (e + 1))


def _first_head():
    return lax.broadcasted_iota(jnp.int32, (1, PAIR), 1) < HEAD_DIM


def _mem_probs(qh, mkp):
    sc = _dot_nt(qh, mkp) * QK_SCALE
    ex = jnp.exp(sc - jnp.max(sc, axis=-1, keepdims=True))
    return ex / jnp.sum(ex, axis=-1, keepdims=True)


def _mem_attn_fwd(qm_b, mkv_ref, dq_dim):
    outs = []
    for mp in range(dq_dim // PAIR):
        qp = qm_b[:, mp * PAIR:(mp + 1) * PAIR]
        mkp = mkv_ref[:, mp * PAIR:(mp + 1) * PAIR]
        mvp = mkv_ref[:, dq_dim + mp * PAIR:dq_dim + (mp + 1) * PAIR]
        heads = []
        for e in (0, 1):
            qh = jnp.where(_lane_mask(e), qp, jnp.zeros_like(qp))
            p = _mem_probs(qh, mkp)
            heads.append(_dot(p.astype(BF16), mvp))
        outs.append(jnp.where(_first_head(), heads[0], heads[1]))
    return outs


def _mem_attn_bwd(qm_b, mkv_ref, dmo_pairs, dmkv_ref, dq_dim):
    dqs = []
    for mp in range(dq_dim // PAIR):
        ks = slice(mp * PAIR, (mp + 1) * PAIR)
        vs = slice(dq_dim + mp * PAIR, dq_dim + (mp + 1) * PAIR)
        qp = qm_b[:, ks]
        mkp = mkv_ref[:, ks]
        mvp = mkv_ref[:, vs]
        dmo_b = dmo_pairs[mp].astype(BF16)
        dq = None
        dmk = None
        dmv = None
        for e in (0, 1):
            hm = _lane_mask(e)
            qh = jnp.where(hm, qp, jnp.zeros_like(qp))
            p = _mem_probs(qh, mkp)
            doh = jnp.where(hm, dmo_b, jnp.zeros_like(dmo_b))
            dp = _dot_nt(doh, mvp)
            ds = p * (dp - jnp.sum(dp * p, axis=-1, keepdims=True))
            dsb = (ds * QK_SCALE).astype(BF16)
            kh = jnp.where(hm, mkp, jnp.zeros_like(mkp))
            dq_e = _dot(dsb, kh)
            dmk_e = _dot_tn(dsb, qh)
            dmv_e = _dot_tn(p.astype(BF16), doh)
            dq = dq_e if dq is None else dq + dq_e
            dmk = dmk_e if dmk is None else dmk + dmk_e
            dmv = dmv_e if dmv is None else dmv + dmv_e
        dmkv_ref[:, ks] += dmk
        dmkv_ref[:, vs] += dmv
        dqs.append(dq)
    return dqs


def _params(n_axes=1):
    return pltpu.CompilerParams(dimension_semantics=("arbitrary",) * n_axes, vmem_limit_bytes=VMEM_LIMIT)


def _full(shape):
    return pl.BlockSpec(shape, lambda *_: (0,) * len(shape))


def _rows(tm, cols):
    return pl.BlockSpec((tm, cols), lambda i: (i, 0))


def _cols(rows, tm):
    return pl.BlockSpec((rows, tm), lambda i: (0, i))


def _pick(n, pref):
    if n <= pref:
        return n
    best = LANES
    for t in range(LANES, pref + 1, LANES):
        if n % t == 0:
            best = t
    return best if 2 * best >= pref or n > 2 * pref else n


class _Comm:
    def __init__(self, ins, out_shapes, aliases, sem_counts, start, finish, middle=None):
        self.ins, self.out_shapes, self.aliases, self.sem_counts = list(ins), list(out_shapes), dict(aliases), sem_counts
        self.start, self.finish = start, finish
        self.middle = middle


def _pcall(body, *, name, grid, in_specs, out_specs, out_shape, args, scratch_shapes=(), comm=None, aliases=None):
    n_in, n_out, n_scr = len(in_specs), len(out_shape), len(scratch_shapes)
    aliases = dict(aliases or {})
    if comm is None:
        res = pl.pallas_call(body, name=name, grid=grid, out_shape=tuple(out_shape), in_specs=list(in_specs),
                             out_specs=tuple(out_specs), scratch_shapes=list(scratch_shapes),
                             input_output_aliases=aliases, compiler_params=_params())(*args)
        return tuple(res), ()
    nci, nco = len(comm.ins), len(comm.out_shapes)
    last = grid[0] - 1

    def wrapped(*refs):
        ins, refs = refs[:n_in], refs[n_in:]
        cins, refs = refs[:nci], refs[nci:]
        outs, refs = refs[:n_out], refs[n_out:]
        couts, refs = refs[:nco], refs[nco:]
        scr, sems = refs[:n_scr], refs[n_scr:]

        @pl.when(pl.program_id(0) == 0)
        def _():
            comm.start(cins, couts, sems)

        if comm.middle is not None:
            @pl.when(pl.program_id(0) == (last + 1) // 2)
            def _():
                comm.middle(cins, couts, sems)

        body(*ins, *outs, *scr)

        @pl.when(pl.program_id(0) == last)
        def _():
            comm.finish(cins, couts, sems)

    res = pl.pallas_call(
        wrapped, name=name, grid=grid, out_shape=tuple(out_shape) + tuple(comm.out_shapes),
        in_specs=list(in_specs) + [ANY] * nci, out_specs=tuple(out_specs) + (ANY,) * nco,
        input_output_aliases={**aliases, **{n_in + a: n_out + b for a, b in comm.aliases.items()}},
        scratch_shapes=list(scratch_shapes) + [pltpu.SemaphoreType.DMA((k,)) for k in comm.sem_counts],
        compiler_params=pltpu.CompilerParams(dimension_semantics=("arbitrary",), vmem_limit_bytes=VMEM_LIMIT,
                                             has_side_effects=True),
    )(*args, *comm.ins)
    return tuple(res[:n_out]), tuple(res[n_out:])


def _run_comm(comm, name):
    nci, nco = len(comm.ins), len(comm.out_shapes)

    def body(*refs):
        cins, couts, sems = refs[:nci], refs[nci:nci + nco], refs[nci + nco:]
        comm.start(cins, couts, sems)
        if comm.middle is not None:
            comm.middle(cins, couts, sems)
        comm.finish(cins, couts, sems)

    return pl.pallas_call(
        body, name=name, out_shape=tuple(comm.out_shapes), in_specs=[ANY] * nci, out_specs=(ANY,) * nco,
        input_output_aliases=comm.aliases, scratch_shapes=[pltpu.SemaphoreType.DMA((k,)) for k in comm.sem_counts],
        compiler_params=pltpu.CompilerParams(has_side_effects=True),
    )(*comm.ins)


def _mm(a, b, name, a_is_transposed, out_split=1):
    (M, R) = a.shape if a_is_transposed else a.shape[::-1]
    _, N = b.shape
    bm, br = _pick(M, 1024 if a_is_transposed else 512), _pick(R, 1024 if a_is_transposed else 512)
    ncol = N // out_split
    bn = _pick(ncol, 1024)
    per = ncol // bn
    last = R // br - 1

    def body(a_ref, b_ref, o_ref, acc_ref):
        @pl.when(pl.program_id(2) == 0)
        def _():
            acc_ref[...] = jnp.zeros_like(acc_ref)

        acc_ref[...] += (_dot if a_is_transposed else _dot_tn)(a_ref[...], b_ref[...])

        @pl.when(pl.program_id(2) == last)
        def _():
            o_ref[...] = acc_ref[...].astype(BF16)

    if out_split == 1:
        out_shape = jax.ShapeDtypeStruct((M, N), BF16)
        out_spec = pl.BlockSpec((bm, bn), lambda i, j, r: (i, j))
    else:
        out_shape = jax.ShapeDtypeStruct((out_split, M, ncol), BF16)
        out_spec = pl.BlockSpec((None, bm, bn), lambda i, j, r: (j // per, i, j % per))
    a_spec = (pl.BlockSpec((bm, br), lambda i, j, r: (i, r)) if a_is_transposed
              else pl.BlockSpec((br, bm), lambda i, j, r: (r, i)))
    return pl.pallas_call(
        body, name=name, out_shape=out_shape, grid=(M // bm, N // bn, R // br),
        in_specs=[a_spec, pl.BlockSpec((br, bn), lambda i, j, r: (r, j))], out_specs=out_spec,
        scratch_shapes=[pltpu.VMEM((bm, bn), F32)],
        compiler_params=pltpu.CompilerParams(dimension_semantics=("parallel", "parallel", "arbitrary"),
                                             vmem_limit_bytes=VMEM_LIMIT),
    )(a, b)


def _mm_tn(a, b, name, out_split=1):
    return _mm(a, b, name, False, out_split)


def _mm_nn(at, b, name, out_split=1):
    return _mm(at, b, name, True, out_split)


def _mm_small(a, b, name):
    def body(a_ref, b_ref, o_ref):
        o_ref[...] = _dot(a_ref[...], b_ref[...]).astype(BF16)

    return pl.pallas_call(body, name=name, out_shape=jax.ShapeDtypeStruct((a.shape[0], b.shape[1]), BF16),
                          compiler_params=pltpu.CompilerParams(vmem_limit_bytes=VMEM_LIMIT))(a, b)


def _mlp_out(x_ref, wup_ref, wdown_ref, g_ref, b_ref, r_ref, h_ref):
    NC, _, FC = wup_ref.shape
    xv = x_ref[...]
    xb = xv.astype(BF16)
    y = jnp.zeros(xv.shape, F32)
    for c in range(NC):
        a = jnp.maximum(_dot(xb, wup_ref[c]), 0.0)
        hb = (a * a).astype(BF16)
        h_ref[:, c * FC:(c + 1) * FC] = hb
        y = y + _dot(hb, wdown_ref[c])
    r = ALPHA * xv + y
    r_ref[...] = r
    xhat, _ = _ln_stats(r)
    return xhat * g_ref[...] + b_ref[...]


def _mlp_fwd_loss(x, wup4, wdown4, g, b, target, name):
    S, D = x.shape
    NC, _, FC = wup4.shape
    tm = min(256, S)

    def body(x_ref, wup_ref, wdown_ref, g_ref, b_ref, t_ref, d_ref, r_ref, h_ref, l_ref):
        @pl.when(pl.program_id(0) == 0)
        def _():
            l_ref[...] = jnp.zeros_like(l_ref)

        err = _mlp_out(x_ref, wup_ref, wdown_ref, g_ref, b_ref, r_ref, h_ref) - t_ref[...]
        d_ref[...] = err * (1.0 / D)
        l_ref[...] += (0.5 / D) * jnp.sum(err * err)

    return pl.pallas_call(
        body, name=name, grid=(S // tm,),
        out_shape=(jax.ShapeDtypeStruct((S, D), F32), jax.ShapeDtypeStruct((S, D), F32),
                   jax.ShapeDtypeStruct((S, NC * FC), BF16), jax.ShapeDtypeStruct((8, LANES), F32)),
        in_specs=[_rows(tm, D), _full(wup4.shape), _full(wdown4.shape), _full((1, D)), _full((1, D)), _rows(tm, D)],
        out_specs=(_rows(tm, D), _rows(tm, D), _rows(tm, NC * FC), _full((8, LANES))),
        compiler_params=_params(),
    )(x, wup4, wdown4, g, b, target)


def _mlp_fwd(x, wup4, wdown4, g, b, name, comm=None):
    S, D = x.shape
    NC, _, FC = wup4.shape
    tm = min(256, S)

    def body(x_ref, wup_ref, wdown_ref, g_ref, b_ref, xo_ref, xb_ref, xbt_ref, r_ref, h_ref):
        xo = _mlp_out(x_ref, wup_ref, wdown_ref, g_ref, b_ref, r_ref, h_ref)
        xo_ref[...] = xo
        xb_ref[...] = xo.astype(BF16)
        xbt_ref[...] = xo.T.astype(BF16)

    return _pcall(
        body, name=name, grid=(S // tm,), comm=comm,
        out_shape=(jax.ShapeDtypeStruct((S, D), F32), jax.ShapeDtypeStruct((S, D), BF16),
                   jax.ShapeDtypeStruct((D, S), BF16), jax.ShapeDtypeStruct((S, D), F32),
                   jax.ShapeDtypeStruct((S, NC * FC), BF16)),
        in_specs=[_rows(tm, D), _full(wup4.shape), _full(wdown4.shape), _full((1, D)), _full((1, D))],
        out_specs=(_rows(tm, D), _rows(tm, D), _cols(D, tm), _rows(tm, D), _rows(tm, NC * FC)),
        args=(x, wup4, wdown4, g, b))


def _mlp_bwd(dxo, r, h, wup4, wdown4, g, name, comm=None):
    S, D = r.shape
    NC, _, FC = wup4.shape
    tm = min(256, S)

    def body(dxo_ref, r_ref, h_ref, wup_ref, wdown_ref, g_ref, dx_ref, drbt_ref, dpre_ref, dgb_ref):
        @pl.when(pl.program_id(0) == 0)
        def _():
            dgb_ref[...] = jnp.zeros_like(dgb_ref)

        xhat, rstd = _ln_stats(r_ref[...])
        dy = dxo_ref[...]
        dgb_ref[0:1, :] += jnp.sum(dy * xhat, axis=0, keepdims=True)
        dgb_ref[1:2, :] += jnp.sum(dy, axis=0, keepdims=True)
        dr = _ln_bwd(dy, xhat, rstd, g_ref[...])
        drb = dr.astype(BF16)
        drbt_ref[...] = dr.T.astype(BF16)
        dx = ALPHA * dr
        for c in range(NC):
            dh = _dot_nt(drb, wdown_ref[c])
            a = jnp.sqrt(h_ref[:, c * FC:(c + 1) * FC].astype(F32))
            dpre = (2.0 * a * dh).astype(BF16)
            dpre_ref[:, c * FC:(c + 1) * FC] = dpre
            dx = dx + _dot_nt(dpre, wup_ref[c])
        dx_ref[...] = dx

    return _pcall(
        body, name=name, grid=(S // tm,), comm=comm,
        out_shape=(jax.ShapeDtypeStruct((S, D), F32), jax.ShapeDtypeStruct((D, S), BF16),
                   jax.ShapeDtypeStruct((S, NC * FC), BF16), jax.ShapeDtypeStruct((2, D), F32)),
        in_specs=[_rows(tm, D), _rows(tm, D), _rows(tm, NC * FC), _full(wup4.shape), _full(wdown4.shape),
                  _full((1, D))],
        out_specs=(_rows(tm, D), _cols(D, tm), _rows(tm, NC * FC), _full((2, D))),
        args=(dxo, r, h, wup4, wdown4, g))


def _tail_fwd_store(xv, cat_ref, wo_ref, g_ref, b_ref, xo_ref, xbt_ref, r_ref, catt_ref):
    cat = cat_ref[...]
    y = _dot(cat, wo_ref[...])
    r = ALPHA * xv + y
    xhat, _ = _ln_stats(r)
    xo = xhat * g_ref[...] + b_ref[...]
    xo_ref[...] = xo
    xbt_ref[...] = xo.T.astype(BF16)
    catt_ref[...] = cat.astype(F32).T.astype(BF16)
    r_ref[...] = r


def _tail_bwd_head(dxo_ref, r_ref, g_ref, wo_ref, dgb_ref, drb_ref):
    xhat, rstd = _ln_stats(r_ref[...])
    dy = dxo_ref[...]
    dgb_ref[0:1, :] += jnp.sum(dy * xhat, axis=0, keepdims=True)
    dgb_ref[1:2, :] += jnp.sum(dy, axis=0, keepdims=True)
    dr = _ln_bwd(dy, xhat, rstd, g_ref[...])
    drb = dr.astype(BF16)
    drb_ref[...] = drb
    return dr, _dot_nt(drb, wo_ref[...])


def _causal(shape):
    return lax.broadcasted_iota(jnp.int32, shape, 1) <= lax.broadcasted_iota(jnp.int32, shape, 0)


def _sgu_mixed(ws_ref, bst_ref, gp, vpair):
    tril = _causal((CHUNK, CHUNK))
    w0 = jnp.where(tril, ws_ref[2 * gp], 0.0).astype(BF16)
    w1 = jnp.where(tril, ws_ref[2 * gp + 1], 0.0).astype(BF16)
    m0 = _dot(w0, vpair) + bst_ref[:, 2 * gp:2 * gp + 1]
    m1 = _dot(w1, vpair) + bst_ref[:, 2 * gp + 1:2 * gp + 2]
    return jnp.where(_first_head(), m0, m1), w0, w1


def _mixer_a_fwd(x, wa, sg, sb, ws, bst, mkv, wo, g, b, name, comm=None):
    S, D = x.shape
    DQ = mkv.shape[1] // 2
    DM = D - DQ
    NP = DM // PAIR
    tm = min(512, S)

    def body(x_ref, wa_ref, sg_ref, sb_ref, ws_ref, bst_ref, mkv_ref, wo_ref, g_ref, b_ref,
             xo_ref, xbt_ref, r_ref, catt_ref, xint_ref, zb_ref, z_s, zv_s, cat_ref):
        xv = x_ref[...]
        xint_ref[...] = xv.T.astype(BF16)
        z_s[...] = _dot(xv.astype(BF16), wa_ref[...])
        zb_ref[...] = z_s[...].astype(BF16)
        vhat, _ = _ln_stats(_gelu(z_s[:, DM:2 * DM]))
        zv_s[...] = (vhat * sg_ref[...] + sb_ref[...]).astype(BF16)
        for c in range(tm // CHUNK):
            rs = slice(c * CHUNK, (c + 1) * CHUNK)
            for gp in range(NP):
                ls = slice(gp * PAIR, (gp + 1) * PAIR)
                mixed, _, _ = _sgu_mixed(ws_ref, bst_ref, gp, zv_s[rs, ls])
                cat_ref[rs, ls] = (_gelu(z_s[rs, ls]) * mixed).astype(BF16)
        mo = _mem_attn_fwd(z_s[:, 2 * DM:].astype(BF16), mkv_ref, DQ)
        for mp in range(DQ // PAIR):
            cat_ref[:, DM + mp * PAIR:DM + (mp + 1) * PAIR] = mo[mp].astype(BF16)
        _tail_fwd_store(xv, cat_ref, wo_ref, g_ref, b_ref, xo_ref, xbt_ref, r_ref, catt_ref)

    return _pcall(
        body, name=name, grid=(S // tm,), comm=comm,
        out_shape=(jax.ShapeDtypeStruct((S, D), F32), jax.ShapeDtypeStruct((D, S), BF16),
                   jax.ShapeDtypeStruct((S, D), F32), jax.ShapeDtypeStruct((D, S), BF16),
                   jax.ShapeDtypeStruct((D, S), BF16), jax.ShapeDtypeStruct((S, 2 * DM + DQ), BF16)),
        in_specs=[_rows(tm, D), _full(wa.shape), _full(sg.shape), _full(sb.shape), _full(ws.shape), _full(bst.shape),
                  _full(mkv.shape), _full(wo.shape), _full((1, D)), _full((1, D))],
        out_specs=(_rows(tm, D), _cols(D, tm), _rows(tm, D), _cols(D, tm), _cols(D, tm), _rows(tm, 2 * DM + DQ)),
        scratch_shapes=[pltpu.VMEM((tm, 2 * DM + DQ), F32), pltpu.VMEM((tm, DM), BF16), pltpu.VMEM((tm, D), BF16)],
        args=(x, wa, sg, sb, ws, bst, mkv, wo, g, b))


def _mixer_a_bwd(zb, dxo, r, wa, sg, sb, ws, bst, mkv, wo, g, name, comm=None):
    S, D = r.shape
    DQ = mkv.shape[1] // 2
    DM = D - DQ
    NP = DM // PAIR
    NG = DM // HEAD_DIM
    tm = min(512, S)

    def body(zb_ref, dxo_ref, r_ref, wa_ref, sg_ref, sb_ref, ws_ref, bst_ref, mkv_ref, wo_ref, g_ref,
             dx_ref, dz_ref, drb_ref, dgb_ref, dsgb_ref, dws_ref, dbst_ref, dmkv_ref, z_s, zv_s, dzv_s):
        @pl.when(pl.program_id(0) == 0)
        def _():
            dgb_ref[...] = jnp.zeros_like(dgb_ref)
            dsgb_ref[...] = jnp.zeros_like(dsgb_ref)
            dws_ref[...] = jnp.zeros_like(dws_ref)
            dbst_ref[...] = jnp.zeros_like(dbst_ref)
            dmkv_ref[...] = jnp.zeros_like(dmkv_ref)

        dr, dcat = _tail_bwd_head(dxo_ref, r_ref, g_ref, wo_ref, dgb_ref, drb_ref)
        z_s[...] = zb_ref[...].astype(F32)
        vhat, vrstd = _ln_stats(_gelu(z_s[:, DM:2 * DM]))
        zv_s[...] = (vhat * sg_ref[...] + sb_ref[...]).astype(BF16)
        tril = _causal((CHUNK, CHUNK))
        lane = lax.broadcasted_iota(jnp.int32, (1, LANES), 1)
        for c in range(tm // CHUNK):
            rs = slice(c * CHUNK, (c + 1) * CHUNK)
            for gp in range(NP):
                ls = slice(gp * PAIR, (gp + 1) * PAIR)
                vpair = zv_s[rs, ls]
                mixed, w0, w1 = _sgu_mixed(ws_ref, bst_ref, gp, vpair)
                u_pre = z_s[rs, ls]
                dmix = dcat[rs, ls]
                dz_ref[rs, ls] = (dmix * mixed * _gelu_grad(u_pre)).astype(BF16)
                dmixed = dmix * _gelu(u_pre)
                first = _first_head()
                d0 = jnp.where(first, dmixed, 0.0)
                d1 = jnp.where(first, 0.0, dmixed)
                d0b = d0.astype(BF16)
                d1b = d1.astype(BF16)
                dzv_s[rs, ls] = _dot_tn(w0, d0b) + _dot_tn(w1, d1b)
                dws_ref[2 * gp] += jnp.where(tril, _dot_nt(d0b, vpair), 0.0)
                dws_ref[2 * gp + 1] += jnp.where(tril, _dot_nt(d1b, vpair), 0.0)
                dbst_ref[...] += (jnp.where(lane == 2 * gp, jnp.sum(d0, axis=-1, keepdims=True), 0.0)
                                  + jnp.where(lane == 2 * gp + 1, jnp.sum(d1, axis=-1, keepdims=True), 0.0))
        dzv = dzv_s[...]
        dsgb_ref[0:1, :] += jnp.sum(dzv * vhat, axis=0, keepdims=True)
        dsgb_ref[1:2, :] += jnp.sum(dzv, axis=0, keepdims=True)
        dgv = _ln_bwd(dzv, vhat, vrstd, sg_ref[...])
        dz_ref[:, DM:2 * DM] = (dgv * _gelu_grad(z_s[:, DM:2 * DM])).astype(BF16)
        dmo = [dcat[:, DM + mp * PAIR:DM + (mp + 1) * PAIR] for mp in range(DQ // PAIR)]
        dqm = _mem_attn_bwd(z_s[:, 2 * DM:].astype(BF16), mkv_ref, dmo, dmkv_ref, DQ)
        for mp in range(DQ // PAIR):
            dz_ref[:, 2 * DM + mp * PAIR:2 * DM + (mp + 1) * PAIR] = dqm[mp].astype(BF16)
        dx_ref[...] = ALPHA * dr + _dot_nt(dz_ref[...], wa_ref[...])

    ZW = 2 * DM + DQ
    M = mkv.shape[0]
    return _pcall(
        body, name=name, grid=(S // tm,), comm=comm,
        out_shape=(jax.ShapeDtypeStruct((S, D), F32), jax.ShapeDtypeStruct((S, ZW), BF16),
                   jax.ShapeDtypeStruct((S, D), BF16), jax.ShapeDtypeStruct((2, D), F32),
                   jax.ShapeDtypeStruct((2, DM), F32), jax.ShapeDtypeStruct((NG, CHUNK, CHUNK), F32),
                   jax.ShapeDtypeStruct((CHUNK, LANES), F32), jax.ShapeDtypeStruct((M, 2 * DQ), F32)),
        in_specs=[_rows(tm, ZW), _rows(tm, D), _rows(tm, D), _full(wa.shape), _full(sg.shape), _full(sb.shape),
                  _full(ws.shape), _full(bst.shape), _full(mkv.shape), _full(wo.shape), _full((1, D))],
        out_specs=(_rows(tm, D), _rows(tm, ZW), _rows(tm, D), _full((2, D)), _full((2, DM)),
                   _full((NG, CHUNK, CHUNK)), _full((CHUNK, LANES)), _full((M, 2 * DQ))),
        scratch_shapes=[pltpu.VMEM((tm, ZW), F32), pltpu.VMEM((tm, DM), BF16), pltpu.VMEM((tm, DM), F32)],
        args=(zb, dxo, r, wa, sg, sb, ws, bst, mkv, wo, g))


def _proj_b_fwd(xb, wq, wkv, dm, name):
    S, D = xb.shape
    DQ = D - dm
    tm = min(512, S)

    def body(x_ref, wq_ref, wkv_ref, q_ref, qm_ref, k_ref, v_ref, fl_ref):
        xv = x_ref[...]
        z = _dot(xv, wq_ref[...])
        q_ref[...] = z[:, :dm].astype(BF16)
        qm_ref[...] = z[:, dm:].astype(BF16)
        kvf = _dot(xv, wkv_ref[...])
        k_ref[...] = kvf[:, :dm].astype(BF16)
        v_ref[...] = kvf[:, dm:2 * dm].astype(BF16)
        fl_ref[...] = kvf[:, 2 * dm:]

    return pl.pallas_call(
        body, name=name, grid=(S // tm,),
        out_shape=(jax.ShapeDtypeStruct((S, dm), BF16), jax.ShapeDtypeStruct((S, DQ), BF16),
                   jax.ShapeDtypeStruct((S, dm), BF16), jax.ShapeDtypeStruct((S, dm), BF16),
                   jax.ShapeDtypeStruct((S, LANES), F32)),
        in_specs=[_rows(tm, D), _full(wq.shape), _full(wkv.shape)],
        out_specs=(_rows(tm, dm), _rows(tm, DQ), _rows(tm, dm), _rows(tm, dm), _rows(tm, LANES)),
        compiler_params=_params(),
    )(xb, wq, wkv)


def _proj_b_bwd(dr, dqz, dk, dv, dfl, wq, wkv, name):
    S, D = dr.shape
    dm = dk.shape[1]
    tm = min(512, S)

    def body(dr_ref, dqz_ref, dk_ref, dv_ref, dfl_ref, wq_ref, wkv_ref, dx_ref):
        dx_ref[...] = (ALPHA * dr_ref[...] + _dot_nt(dqz_ref[...], wq_ref[...])
                       + _dot_nt(dk_ref[...], wkv_ref[:, :dm]) + _dot_nt(dv_ref[...], wkv_ref[:, dm:2 * dm])
                       + _dot_nt(dfl_ref[...], wkv_ref[:, 2 * dm:]))

    return pl.pallas_call(
        body, name=name, grid=(S // tm,), out_shape=jax.ShapeDtypeStruct((S, D), F32),
        in_specs=[_rows(tm, D), _rows(tm, dqz.shape[1]), _rows(tm, dm), _rows(tm, dm), _rows(tm, dfl.shape[1]),
                  _full(wq.shape), _full(wkv.shape)],
        out_specs=_rows(tm, D), compiler_params=_params(),
    )(dr, dqz, dk, dv, dfl, wq, wkv)


def _mm_nn_cat(at, bs, name):
    M, R = at.shape
    widths = [b.shape[1] for b in bs]
    N = sum(widths)
    br = _pick(R, 1024)
    last = R // br - 1
    nb = len(bs)

    def body(a_ref, *refs):
        b_refs, o_ref, acc_ref = refs[:nb], refs[nb], refs[nb + 1]

        @pl.when(pl.program_id(0) == 0)
        def _():
            acc_ref[...] = jnp.zeros_like(acc_ref)

        a = a_ref[...]
        c0 = 0
        for b_ref, wd in zip(b_refs, widths):
            acc_ref[:, c0:c0 + wd] += _dot(a, b_ref[...])
            c0 += wd

        @pl.when(pl.program_id(0) == last)
        def _():
            o_ref[...] = acc_ref[...].astype(BF16)

    return pl.pallas_call(
        body, name=name, grid=(R // br,), out_shape=jax.ShapeDtypeStruct((M, N), BF16),
        in_specs=[pl.BlockSpec((M, br), lambda r: (0, r))] + [pl.BlockSpec((br, wd), lambda r: (r, 0)) for wd in widths],
        out_specs=_full((M, N)), scratch_shapes=[pltpu.VMEM((M, N), F32)], compiler_params=_params(),
    )(at, *bs)


def _split3(v):
    hi = v.astype(BF16)
    r1 = v - hi.astype(F32)
    mid = r1.astype(BF16)
    lo = (r1 - mid.astype(F32)).astype(BF16)
    return hi, mid, lo


def _tri_sum(v, tri_b):
    hi, mid, lo = _split3(v)
    return _dot(hi, tri_b) + _dot(mid, tri_b) + _dot(lo, tri_b)


def _log_sigmoid(x):
    return jnp.minimum(x, 0.0) - jnp.log(1.0 + jnp.exp(-jnp.abs(x)))


def _forget_cumsum(flt, bf, name):
    H, S = flt.shape
    nchunk = S // LANES

    def body(fl_ref, bf_ref, p_ref):
        upper = (lax.broadcasted_iota(jnp.int32, (LANES, LANES), 0)
                 <= lax.broadcasted_iota(jnp.int32, (LANES, LANES), 1)).astype(BF16)

        def step(n, carry):
            s0 = pl.multiple_of(n * LANES, LANES)
            lf = _log_sigmoid(fl_ref[:, pl.ds(s0, LANES)] + bf_ref[...])
            cs = _tri_sum(lf, upper) + carry
            for i, piece in enumerate(_split3(-cs)):
                p_ref[i, :, pl.ds(s0, LANES)] = piece.astype(F32)
            return cs[:, LANES - 1:LANES]

        lax.fori_loop(0, nchunk, step, jnp.zeros((H, 1), F32))

    return pl.pallas_call(body, name=name, out_shape=jax.ShapeDtypeStruct((3, H, S), F32),
                          compiler_params=pltpu.CompilerParams(vmem_limit_bytes=VMEM_LIMIT))(flt, bf)


def _forget_cumsum_bwd(dct, flt, bf, name):
    H, S = flt.shape
    nchunk = S // LANES

    def body(dc_ref, fl_ref, bf_ref, dfl_ref, dbf_ref):
        lower = (lax.broadcasted_iota(jnp.int32, (LANES, LANES), 0)
                 >= lax.broadcasted_iota(jnp.int32, (LANES, LANES), 1)).astype(BF16)

        def step(n, carry):
            tail, tot = carry
            s0 = pl.multiple_of((nchunk - 1 - n) * LANES, LANES)
            suffix = _tri_sum(dc_ref[:, pl.ds(s0, LANES)], lower) + tail
            xv = fl_ref[:, pl.ds(s0, LANES)] + bf_ref[...]
            dfl = suffix * (1.0 / (1.0 + jnp.exp(xv)))
            dfl_ref[:, pl.ds(s0, LANES)] = dfl
            return suffix[:, 0:1], tot + jnp.sum(dfl, axis=-1, keepdims=True)

        _, tot = lax.fori_loop(0, nchunk, step, (jnp.zeros((H, 1), F32), jnp.zeros((H, 1), F32)))
        dbf_ref[...] = jnp.broadcast_to(tot, (H, LANES))

    return pl.pallas_call(body, name=name,
                          out_shape=(jax.ShapeDtypeStruct((H, S), F32), jax.ShapeDtypeStruct((H, LANES), F32)),
                          compiler_params=pltpu.CompilerParams(vmem_limit_bytes=VMEM_LIMIT))(dct, flt, bf)


N_BIAS = 3


def _bias_ones():
    lane = lax.broadcasted_iota(jnp.int32, (1, PAIR), 1)
    return ((lane & (HEAD_DIM - 1)) < N_BIAS).astype(BF16)


def _fox_fwd(q, k, v, ca, name, comm=None):
    S, DM = q.shape
    NP = DM // PAIR
    T = min(FOX_BLOCK, S)
    nq = S // T

    def body(q_ref, k_ref, v_ref, ca_ref, o_ref, lse_ref):
        tril = _causal((T, T))
        heads = (_lane_mask(0), _lane_mask(1))
        ones3 = _bias_ones()

        def kv_block(kj):
            s0 = pl.multiple_of(kj * T, T)
            kb, vb, cab = k_ref[pl.ds(s0, T), :], v_ref[pl.ds(s0, T), :], ca_ref[pl.ds(s0, T), :]
            return ([jnp.where(heads[e], kb, cab) for e in (0, 1)],
                    [jnp.where(heads[e], vb, jnp.ones_like(vb)) for e in (0, 1)])

        def q_pair(a, _):
            t0s = [pl.multiple_of((2 * a + r) * T, T) for r in (0, 1)]
            qaug = {}
            for r in (0, 1):
                qb = q_ref[pl.ds(t0s[r], T), :] * QK_SCALE
                for e in (0, 1):
                    qaug[r, e] = jnp.where(heads[e], qb, ones3)
            chains = [(r, e) for r in (0, 1) for e in (0, 1)]

            def update(carries, first_block, seen):
                blocks = [kv_block(first_block + kk) for kk in range(1 + max(kk for s in seen for kk, _ in s))]
                out = []
                scores = [[_dot_nt(qaug[r, e], blocks[kk][0][e]) for kk, _ in seen[r]] for r, e in chains]
                for n, (r, e) in enumerate(chains):
                    m, acc = carries[n]
                    tiles = [jnp.where(tril, s, NEG) if masked else s for s, (_, masked) in zip(scores[n], seen[r])]
                    m_new = jnp.maximum(m, jnp.max(functools.reduce(jnp.maximum, tiles), axis=-1, keepdims=True))
                    acc = jnp.exp(m - m_new) * acc
                    for s, (kk, _) in zip(tiles, seen[r]):
                        acc = acc + _dot(jnp.exp(s - m_new).astype(BF16), blocks[kk][1][e])
                    out.append((m_new, acc))
                return tuple(out)

            def full(first_block, n_blocks, carries):
                return update(carries, first_block, [[(kk, False) for kk in range(n_blocks)]] * 2)

            init = (jnp.full((T, 1), NEG, F32), jnp.zeros((T, PAIR), F32))
            carries = lax.fori_loop(0, a // 2, lambda i, c: full(4 * i, 4, c), (init,) * 4)
            carries = lax.fori_loop(0, a % 2, lambda i, c: full(2 * a - 2, 2, c), carries)
            carries = update(carries, 2 * a, [[(0, True)], [(0, False), (1, True)]])
            first = _first_head()
            for r in (0, 1):
                (m0, acc0), (m1, acc1) = carries[2 * r], carries[2 * r + 1]
                l0, l1 = acc0[:, HEAD_DIM:HEAD_DIM + 1], acc1[:, 0:1]
                o_ref[pl.ds(t0s[r], T), :] = jnp.where(first, acc0 / l0, acc1 / l1).astype(BF16)
                lse_t = jnp.where(first, m0 + jnp.log(l0), m1 + jnp.log(l1)).T
                lse_ref[0:1, pl.ds(t0s[r], T)] = lse_t[0:1, :]
                lse_ref[1:2, pl.ds(t0s[r], T)] = lse_t[HEAD_DIM:HEAD_DIM + 1, :]
            return 0

        lax.fori_loop(0, nq // 2, q_pair, 0)

    col = pl.BlockSpec((S, PAIR), lambda hp: (0, hp))
    return _pcall(
        body, name=name, grid=(NP,), comm=comm,
        out_shape=(jax.ShapeDtypeStruct((S, DM), BF16), jax.ShapeDtypeStruct((NP, 2, S), F32)),
        in_specs=[col, col, col, col], out_specs=(col, pl.BlockSpec((None, 2, S), lambda hp: (hp, 0, 0))),
        args=(q, k, v, ca))


def _fox_bwd(q, k, v, o, do, ca, lse, name, comm=None):
    S, DM = q.shape
    NP = DM // PAIR
    T = min(FOX_BLOCK, S)
    nq = S // T

    def body(q_ref, k_ref, v_ref, o_ref, do_ref, ca_ref, lse_ref, dq_ref, dk_ref, dv_ref, dc_ref,
             dq_acc, dcol_acc, dk_acc, dv_acc, dd_s):
        keep = lax.broadcasted_iota(jnp.int32, (T, T), 0) <= lax.broadcasted_iota(jnp.int32, (T, T), 1)
        lane = lax.broadcasted_iota(jnp.int32, (1, LANES), 1)
        heads = (_lane_mask(0), _lane_mask(1))
        ones3 = _bias_ones()
        dq_acc[...] = jnp.zeros_like(dq_acc)
        dcol_acc[...] = jnp.zeros_like(dcol_acc)
        dc_ref[...] = jnp.zeros_like(dc_ref)

        def row_dots(qi, _):
            t0 = pl.multiple_of(qi * T, T)
            prod = do_ref[pl.ds(t0, T), :].astype(F32) * o_ref[pl.ds(t0, T), :].astype(F32)
            cols = [jnp.sum(jnp.where(heads[e], prod, 0.0), axis=-1, keepdims=True) for e in (0, 1)]
            tile = jnp.where(lane == 0, cols[0], jnp.where(lane == 1, cols[1], 0.0))
            dd_s[:, pl.ds(t0, T)] = tile.T[0:8, :]
            return 0

        lax.fori_loop(0, nq, row_dots, 0)

        def kv_pair(b, _):
            dk_acc[...] = jnp.zeros_like(dk_acc)
            dv_acc[...] = jnp.zeros_like(dv_acc)
            s0s = [pl.multiple_of((2 * b + jj) * T, T) for jj in (0, 1)]
            kaug, ks, vbs = {}, {}, []
            for jj in (0, 1):
                kb = k_ref[pl.ds(s0s[jj], T), :]
                cab = ca_ref[pl.ds(s0s[jj], T), :]
                vbs.append(v_ref[pl.ds(s0s[jj], T), :])
                for e in (0, 1):
                    kaug[jj, e] = jnp.where(heads[e], kb, cab)
                    ks[jj, e] = jnp.where(heads[e], kb, jnp.zeros_like(kb)) * QK_SCALE

            def q_step(qi, blocks):
                t0 = pl.multiple_of(qi * T, T)
                qs = q_ref[pl.ds(t0, T), :] * QK_SCALE
                dob = do_ref[pl.ds(t0, T), :]
                doh = [jnp.where(heads[e], dob, jnp.zeros_like(dob)) for e in (0, 1)]
                qh = [jnp.where(heads[e], qs, jnp.zeros_like(qs)) for e in (0, 1)]
                chains = [(e, jj, masked) for e in (0, 1) for jj, masked in blocks]
                scores = [_dot_nt(kaug[jj, e], jnp.where(heads[e], qs, ones3)) for e, jj, _ in chains]
                dps = [_dot_nt(vbs[jj], doh[e]) for e, jj, _ in chains]
                dq = jnp.zeros((T, PAIR), F32)
                drow = [jnp.zeros((1, T), F32), jnp.zeros((1, T), F32)]
                for n, (e, jj, masked) in enumerate(chains):
                    st = jnp.where(keep, scores[n], NEG) if masked else scores[n]
                    pt = jnp.exp(st - lse_ref[e:e + 1, pl.ds(t0, T)])
                    dv_acc[jj] += _dot(pt.astype(BF16), doh[e])
                    dst = pt * (dps[n] - dd_s[e:e + 1, pl.ds(t0, T)])
                    dsb = dst.astype(BF16)
                    dk_acc[jj] += _dot(dsb, qh[e])
                    dq = dq + _dot_tn(dsb, ks[jj, e])
                    dcol_acc[pl.ds(s0s[jj], T), :] -= jnp.where(lane == e, jnp.sum(dst, axis=-1, keepdims=True), 0.0)
                    drow[e] = drow[e] + jnp.sum(dst, axis=0, keepdims=True)
                for e in (0, 1):
                    dc_ref[e:e + 1, pl.ds(t0, T)] += drow[e]
                dq_acc[pl.ds(t0, T), :] += dq

            q_step(2 * b, [(0, True)])
            q_step(2 * b + 1, [(0, False), (1, True)])

            def rest(i, _):
                q_step(2 * b + 2 + 2 * i, [(0, False), (1, False)])
                q_step(2 * b + 3 + 2 * i, [(0, False), (1, False)])
                return 0

            lax.fori_loop(0, (nq - 2) // 2 - b, rest, 0)
            for jj in (0, 1):
                dk_ref[pl.ds(s0s[jj], T), :] = dk_acc[jj].astype(BF16)
                dv_ref[pl.ds(s0s[jj], T), :] = dv_acc[jj].astype(BF16)
            return 0

        lax.fori_loop(0, nq // 2, kv_pair, 0)
        dq_ref[...] = dq_acc[...].astype(BF16)

        def add_cols(kj, _):
            s0 = pl.multiple_of(kj * T, T)
            dc_ref[:, pl.ds(s0, T)] += dcol_acc[pl.ds(s0, T), :].T[0:2, :]
            return 0

        lax.fori_loop(0, nq, add_cols, 0)

    col = pl.BlockSpec((S, PAIR), lambda hp: (0, hp))
    row2 = pl.BlockSpec((None, 2, S), lambda hp: (hp, 0, 0))
    return _pcall(
        body, name=name, grid=(NP,), comm=comm,
        out_shape=(jax.ShapeDtypeStruct(do.shape, BF16), jax.ShapeDtypeStruct((S, DM), BF16),
                   jax.ShapeDtypeStruct((S, DM), BF16), jax.ShapeDtypeStruct((NP, 2, S), F32)),
        in_specs=[col, col, col, col, col, col, row2], out_specs=(col, col, col, row2),
        scratch_shapes=[pltpu.VMEM((S, PAIR), F32), pltpu.VMEM((S, LANES), F32), pltpu.VMEM((2, T, PAIR), F32),
                        pltpu.VMEM((2, T, PAIR), F32), pltpu.VMEM((8, S), F32)],
        args=(q, k, v, o, do, ca, lse), aliases={4: 0})


def _mixer_b_fwd(x, o, qm, mkv, wo, g, b, name):
    S, D = x.shape
    DQ = qm.shape[1]
    DM = D - DQ
    tm = min(512, S)

    def body(x_ref, o_ref, qm_ref, mkv_ref, wo_ref, g_ref, b_ref, xo_ref, xbt_ref, r_ref, catt_ref, cat_ref):
        cat_ref[:, :DM] = o_ref[...]
        mo = _mem_attn_fwd(qm_ref[...], mkv_ref, DQ)
        for mp in range(DQ // PAIR):
            cat_ref[:, DM + mp * PAIR:DM + (mp + 1) * PAIR] = mo[mp].astype(BF16)
        _tail_fwd_store(x_ref[...], cat_ref, wo_ref, g_ref, b_ref, xo_ref, xbt_ref, r_ref, catt_ref)

    return pl.pallas_call(
        body, name=name, grid=(S // tm,),
        out_shape=(jax.ShapeDtypeStruct((S, D), F32), jax.ShapeDtypeStruct((D, S), BF16),
                   jax.ShapeDtypeStruct((S, D), F32), jax.ShapeDtypeStruct((D, S), BF16)),
        in_specs=[_rows(tm, D), _rows(tm, DM), _rows(tm, DQ), _full(mkv.shape), _full(wo.shape), _full((1, D)),
                  _full((1, D))],
        out_specs=(_rows(tm, D), _cols(D, tm), _rows(tm, D), _cols(D, tm)),
        scratch_shapes=[pltpu.VMEM((tm, D), BF16)],
        compiler_params=_params(),
    )(x, o, qm, mkv, wo, g, b)


def _mixer_b_bwd(dxo, r, qm, mkv, wo, g, name, comm=None):
    S, D = r.shape
    DQ = qm.shape[1]
    DM = D - DQ
    M = mkv.shape[0]
    tm = min(512, S)

    def body(dxo_ref, r_ref, qm_ref, mkv_ref, wo_ref, g_ref, dr_ref, drb_ref, dcat_ref, dgb_ref, dmkv_ref):
        @pl.when(pl.program_id(0) == 0)
        def _():
            dgb_ref[...] = jnp.zeros_like(dgb_ref)
            dmkv_ref[...] = jnp.zeros_like(dmkv_ref)

        dr, dcat = _tail_bwd_head(dxo_ref, r_ref, g_ref, wo_ref, dgb_ref, drb_ref)
        dr_ref[...] = dr
        dcat_ref[:, :DM] = dcat[:, :DM].astype(BF16)
        dmo = [dcat[:, DM + mp * PAIR:DM + (mp + 1) * PAIR] for mp in range(DQ // PAIR)]
        dqm = _mem_attn_bwd(qm_ref[...], mkv_ref, dmo, dmkv_ref, DQ)
        for mp in range(DQ // PAIR):
            dcat_ref[:, DM + mp * PAIR:DM + (mp + 1) * PAIR] = dqm[mp].astype(BF16)

    return _pcall(
        body, name=name, grid=(S // tm,), comm=comm,
        out_shape=(jax.ShapeDtypeStruct((S, D), F32), jax.ShapeDtypeStruct((S, D), BF16),
                   jax.ShapeDtypeStruct((S, D), BF16), jax.ShapeDtypeStruct((2, D), F32),
                   jax.ShapeDtypeStruct((M, 2 * DQ), F32)),
        in_specs=[_rows(tm, D), _rows(tm, D), _rows(tm, DQ), _full(mkv.shape), _full(wo.shape), _full((1, D))],
        out_specs=(_rows(tm, D), _rows(tm, D), _rows(tm, D), _full((2, D)), _full((M, 2 * DQ))),
        args=(dxo, r, qm, mkv, wo, g))


class _NoExchange:
    def plan(self, stage, w, grads):
        return None

    def done(self, stage, results, w):
        pass


def _local_step(x, mem, target, w, hooks):
    S, D = x.shape
    DQ = w["mkv0"].shape[1] // 2
    DM = D - DQ
    NG = DM // HEAD_DIM
    NP = DM // PAIR
    ln_g, ln_b = w["ln_g"], w["ln_b"]
    row = lambda a, i: a[i:i + 1]
    memb = mem.astype(BF16)
    bst =jnp.pad(w["b_s"].T, ((0, 0), (0, LANES - NG)))
    grads = {}

    def staged(stage, fn, *args):
        out, got = fn(*args, stage, comm=hooks.plan(stage, w, grads))
        hooks.done(stage, got, w)
        return out

    mkv0 = _mm_small(memb, w["mkv0"], "mem_kv_0")
    x1, x1t, r1, cat0t, x0t, z0b = staged("mixer_a_fwd", _mixer_a_fwd, x, w["wa"], w["sg"], w["sb"], w["ws"], bst, mkv0,
                                w["wo0"], row(ln_g, 0), row(ln_b, 0))
    x2, x2b, x2t, r2, h0 = staged("mlp_fwd_0", _mlp_fwd, x1, w["wup0"], w["wdown0"], row(ln_g, 1), row(ln_b, 1))
    q, qm, k, v, fl = _proj_b_fwd(x2b, w["wq"], w["wkv"], DM, "proj_b_fwd")
    flt = fl[:, :16].T
    bfc = jnp.pad(w["bf"], (0, 16 - NG)).reshape(16, 1)
    pieces = _forget_cumsum(flt, bfc, "forget_cumsum")
    pieces = jnp.pad(pieces[:, :NG].transpose(2, 1, 0), ((0, 0), (0, 0), (0, HEAD_DIM - N_BIAS)))
    ca = pieces.reshape(S, NP, 2, HEAD_DIM)[:, :, ::-1].reshape(S, DM).astype(BF16)
    o, lse = staged("fox_fwd", _fox_fwd, q, k, v, ca)
    mkv1 = _mm_small(memb, w["mkv1"], "mem_kv_1")
    x3, x3t, r3, cat1t = _mixer_b_fwd(x2, o, qm, mkv1, w["wo1"], row(ln_g, 2), row(ln_b, 2), "mixer_b_fwd")
    dx4, r4, h1, loss_part = _mlp_fwd_loss(x3, w["wup1"], w["wdown1"], row(ln_g, 3), row(ln_b, 3), target,
                                           "mlp_fwd_1")

    nc = w["wup0"].shape[0]
    (dx3, dr4t, dpre1, dgb11), _ = _mlp_bwd(dx4, r4, h1, w["wup1"], w["wdown1"], row(ln_g, 3), "mlp_bwd_1")
    grads["wdown1"] = _mm_nn(dr4t, h1, "dw_down_1", out_split=nc)
    grads["wup1"] = _mm_nn(x3t, dpre1, "dw_up_1", out_split=nc)
    dr3, dr3b, dcat1, dgb10, dmkv1 = staged("mixer_b_bwd", _mixer_b_bwd, dx3, r3, qm, mkv1, w["wo1"], row(ln_g, 2))
    dqz, dk, dv, dct3 = staged("fox_bwd", _fox_bwd, q, k, v, o, dcat1, ca, lse)
    dct = jnp.pad(dct3.reshape(NG, S), ((0, 16 - NG), (0, 0)))
    dflt, dbf = _forget_cumsum_bwd(dct, flt, bfc, "forget_cumsum_bwd")
    dfl = jnp.pad(dflt.T, ((0, 0), (0, LANES - 16))).astype(BF16)
    dx2 = _proj_b_bwd(dr3, dqz, dk, dv, dfl, w["wq"], w["wkv"], "proj_b_bwd")
    grads["wo1"] = _mm_nn(cat1t, dr3b, "dw_o_1")
    grads["wq"] = _mm_nn(x2t, dqz, "dw_q")
    grads["wkv"] = _mm_nn_cat(x2t, [dk, dv, dfl], "dw_kv")
    grads["mkv1"] = _mm_tn(memb, dmkv1.astype(BF16), "dw_mkv_1")
    dx1, dr2t, dpre0, dgb01 = staged("mlp_bwd_0", _mlp_bwd, dx2, r2, h0, w["wup0"], w["wdown0"], row(ln_g, 1))
    grads["wdown0"] = _mm_nn(dr2t, h0, "dw_down_0", out_split=nc)
    grads["wup0"] = _mm_nn(x1t, dpre0, "dw_up_0", out_split=nc)
    dx0, dz, dr1b, dgb00, dsgb, dws, dbst, dmkv0 = staged(
        "mixer_a_bwd", _mixer_a_bwd, z0b, dx1, r1, w["wa"], w["sg"], w["sb"], w["ws"], bst, mkv0, w["wo0"], row(ln_g, 0))
    grads["wo0"] = _mm_nn(cat0t, dr1b, "dw_o_0")
    grads["wa"] = _mm_nn(x0t, dz, "dw_a")
    grads["mkv0"] = _mm_tn(memb, dmkv0.astype(BF16), "dw_mkv_0")
    grads.update({
        "ws": dws, "b_s": dbst[:, :NG].T, "sg": dsgb[0:1], "sb": dsgb[1:2], "bf": dbf[:NG, 0], "loss": loss_part[0:1],
        "ln_g": jnp.concatenate([dgb00[0:1], dgb01[0:1], dgb10[0:1], dgb11[0:1]], axis=0),
        "ln_b": jnp.concatenate([dgb00[1:2], dgb01[1:2], dgb10[1:2], dgb11[1:2]], axis=0),
    })
    return loss_part, dx0, grads


def _place():
    xi, yi, ci = lax.axis_index("x"), lax.axis_index("y"), lax.axis_index("c")
    peers = [(xi, 1 - yi), (1 - xi, yi), (1 - xi, 1 - yi)]
    return xi, yi, ci, peers


def _comm_call(body, name, ins, out_shapes, n_remote, n_local):
    return pl.pallas_call(
        body, name=name, out_shape=out_shapes, in_specs=[ANY] * len(ins), out_specs=[ANY] * len(out_shapes),
        scratch_shapes=[pltpu.SemaphoreType.DMA((n_remote,)), pltpu.SemaphoreType.DMA((n_remote,)),
                        pltpu.SemaphoreType.DMA((max(n_local, 1),))],
        compiler_params=pltpu.CompilerParams(has_side_effects=True),
    )(*ins)


def _pair_swap(xs):
    n = len(xs)
    out_shapes = [jax.ShapeDtypeStruct((a.shape[0],) + a.shape[2:], a.dtype) for a in xs]

    def copies(x_refs, o_refs, sems):
        send_sems, recv_sems = sems
        xi, yi, ci, _ = _place()
        return [pltpu.make_async_remote_copy(src_ref=x_refs[i].at[:, 1 - ci], dst_ref=o_refs[i],
                                             send_sem=send_sems.at[i], recv_sem=recv_sems.at[i],
                                             device_id=(xi, yi, 1 - ci), device_id_type=MESH) for i in range(n)]

    def start(x_refs, o_refs, sems):
        for cp in copies(x_refs, o_refs, sems):
            cp.start()

    def finish(x_refs, o_refs, sems):
        cps = copies(x_refs, o_refs, sems)
        for cp in cps:
            cp.wait_recv()
        for cp in cps:
            cp.wait_send()

    return _Comm(xs, out_shapes, {}, [n, n], start, finish)


def _chip_exchange(xs):
    n = len(xs)
    out_shapes = [jax.ShapeDtypeStruct((3,) + a.shape[1:], a.dtype) for a in xs]

    def copies(x_refs, o_refs, sems):
        send_sems, recv_sems = sems
        xi, yi, ci, peers = _place()
        return [pltpu.make_async_remote_copy(src_ref=x_refs[i].at[2 * px + py], dst_ref=o_refs[i].at[d],
                                             send_sem=send_sems.at[3 * i + d], recv_sem=recv_sems.at[3 * i + d],
                                             device_id=(px, py, ci), device_id_type=MESH)
                for i in range(n) for d, (px, py) in enumerate(peers)]

    def start(x_refs, o_refs, sems):
        for cp in copies(x_refs, o_refs, sems):
            cp.start()

    def finish(x_refs, o_refs, sems):
        cps = copies(x_refs, o_refs, sems)
        for cp in cps:
            cp.wait_recv()
        for cp in cps:
            cp.wait_send()

    return _Comm(xs, out_shapes, {}, [3 * n, 3 * n], start, finish)


def _inplace_call(body, name, bufs, sem_counts):
    n = len(bufs)
    return pl.pallas_call(
        body, name=name, out_shape=[jax.ShapeDtypeStruct(a.shape, a.dtype) for a in bufs],
        in_specs=[ANY] * n, out_specs=[ANY] * n, input_output_aliases={i: i for i in range(n)},
        scratch_shapes=[pltpu.SemaphoreType.DMA((k,)) for k in sem_counts],
        compiler_params=pltpu.CompilerParams(has_side_effects=True),
    )(*bufs)


def _gather_weights(bufs, whole=(), early=None):
    n, nw = len(bufs), len(whole)

    def copy(ref, s, r, k, dev):
        return pltpu.make_async_remote_copy(src_ref=ref, dst_ref=ref, send_sem=s.at[k], recv_sem=r.at[k],
                                            device_id=dev, device_id_type=MESH)

    def plan(w, sems):
        ici_s, ici_r, d2d_s, d2d_r = sems
        xi, yi, ci, peers = _place()
        xn, yn, sib = (1 - xi, yi, ci), (xi, 1 - yi, ci), (xi, yi, 1 - ci)
        j, jx, jy, jd = 2 * xi + yi, 2 * (1 - xi) + yi, 2 * xi + 1 - yi, 2 * (1 - xi) + 1 - yi
        c = {}
        for i in range(n):
            hh = bufs[i].shape[2] // 2
            lo, hi = pl.ds(0, hh), pl.ds(hh, hh)
            at = lambda slot, half, rows=None, i=i: w[i].at[slot, half] if rows is None else w[i].at[slot, half, rows]
            c["to_x", i] = (copy(at(j, ci), ici_s, ici_r, 4 * i, xn), copy(at(jx, ci), ici_s, ici_r, 4 * i, xn))
            c["to_y", i] = (copy(at(j, ci), ici_s, ici_r, 4 * i + 1, yn), copy(at(jy, ci), ici_s, ici_r, 4 * i + 1, yn))
            c["x_on", i] = (copy(at(jx, ci, lo), ici_s, ici_r, 4 * i + 2, yn), copy(at(jd, ci, lo), ici_s, ici_r, 4 * i + 2, yn))
            c["y_on", i] = (copy(at(jy, ci, hi), ici_s, ici_r, 4 * i + 3, xn), copy(at(jd, ci, hi), ici_s, ici_r, 4 * i + 3, xn))
            for k, slot in enumerate((jx, jy, jd)):
                c["sib", i, k] = (copy(at(slot, ci), d2d_s, d2d_r, 3 * i + k, sib),
                                  copy(at(slot, 1 - ci), d2d_s, d2d_r, 3 * i + k, sib))
        for i in range(nw):
            for d, (px, py) in enumerate(peers):
                k = 4 * n + 3 * i + d
                c["whole", i, d] = (copy(w[n + i].at[j], ici_s, ici_r, k, (px, py, ci)),
                                    copy(w[n + i].at[2 * px + py], ici_s, ici_r, k, (px, py, ci)))
        return c

    def start(_, w, sems):
        c = plan(w, sems)
        for i in range(n):
            c["to_x", i][0].start()
            c["to_y", i][0].start()
        for i in range(nw):
            for d in range(3):
                c["whole", i, d][0].start()

    def pass_on(c, which):
        for i in which:
            c["to_x", i][1].wait_recv()
            c["x_on", i][0].start()
            c["sib", i, 0][0].start()
            c["to_y", i][1].wait_recv()
            c["y_on", i][0].start()
            c["sib", i, 1][0].start()

    n_early = n if early is None else min(early, n)

    def middle(_, w, sems):
        pass_on(plan(w, sems), range(n_early))

    def finish(_, w, sems):
        c = plan(w, sems)
        pass_on(c, range(n_early, n))
        for i in range(n):
            c["x_on", i][1].wait_recv()
            c["y_on", i][1].wait_recv()
            c["sib", i, 2][0].start()
        for i in range(n):
            for k in range(3):
                c["sib", i, k][1].wait_recv()
        for i in range(nw):
            for d in range(3):
                c["whole", i, d][1].wait_recv()
        for key, (sent, _) in c.items():
            sent.wait_send()

    arrays = list(bufs) + list(whole)
    k = 4 * n + 3 * nw
    return _Comm(arrays, [jax.ShapeDtypeStruct(a.shape, a.dtype) for a in arrays], {i: i for i in range(n + nw)},
                 [k, k, max(3 * n, 1), max(3 * n, 1)], start, finish, middle)


def _pair_gather(bufs, spread, name):
    n = len(bufs)
    where = [(i, l) for i, a in enumerate(bufs) for l in range(a.shape[0])]
    nw = len(where)
    arrays = list(bufs) + ([] if spread is None else [spread])
    na = len(arrays)

    def body(*refs):
        g = refs[na:na + n]
        sp = refs[na + n] if spread is not None else None
        send_sems, recv_sems = refs[2 * na:]
        xi, yi, ci, peers = _place()
        chips = [(xi, yi)] + peers
        others = [(d, f) for d in range(4) for f in (0, 1) if (d, f) != (0, 0)] if spread is not None else []
        sends = []
        for k, (i, l) in enumerate(where):
            mine = g[i].at[l, :, ci]
            cp = pltpu.make_async_remote_copy(src_ref=mine, dst_ref=mine, send_sem=send_sems.at[k],
                                              recv_sem=recv_sems.at[k], device_id=(xi, yi, 1 - ci), device_id_type=MESH)
            cp.start()
            sends.append(cp)
        for k, (d, f) in enumerate(others):
            mine = sp.at[2 * xi + yi, ci]
            cp = pltpu.make_async_remote_copy(src_ref=mine, dst_ref=mine, send_sem=send_sems.at[nw + k],
                                              recv_sem=recv_sems.at[nw + k],
                                              device_id=(chips[d][0], chips[d][1], ci if f == 0 else 1 - ci),
                                              device_id_type=MESH)
            cp.start()
            sends.append(cp)
        for k, (i, l) in enumerate(where):
            other = g[i].at[l, :, 1 - ci]
            pltpu.make_async_remote_copy(src_ref=other, dst_ref=other, send_sem=send_sems.at[k],
                                         recv_sem=recv_sems.at[k], device_id=(xi, yi, 1 - ci),
                                         device_id_type=MESH).wait_recv()
        for k, (d, f) in enumerate(others):
            px, py, pc = chips[d][0], chips[d][1], (ci if f == 0 else 1 - ci)
            theirs = sp.at[2 * px + py, pc]
            pltpu.make_async_remote_copy(src_ref=theirs, dst_ref=theirs, send_sem=send_sems.at[nw + k],
                                         recv_sem=recv_sems.at[nw + k], device_id=(px, py, pc),
                                         device_id_type=MESH).wait_recv()
        for cp in sends:
            cp.wait_send()

    n_sems = nw + (7 if spread is not None else 0)
    out = _inplace_call(body, name, arrays, [n_sems, n_sems])
    return out[:n], (out[n] if spread is not None else None)


def _row_block(rows, cols):
    want = max(8, (2 ** 18 // max(cols, 1)) // 8 * 8)
    if rows <= want:
        return rows
    best = 8
    for t in range(8, want + 1, 8):
        if rows % t == 0:
            best = t
    return best


def _pair_add(g4, recv, sel, out_dtype, name):
    A, _, H, C = g4.shape
    bh = _row_block(H, C)

    def body(sel_ref, g_ref, r_ref, o_ref):
        o_ref[...] = (g_ref[...].astype(F32) + r_ref[...].astype(F32)).astype(out_dtype)

    return pl.pallas_call(
        body, name=name, out_shape=jax.ShapeDtypeStruct((A, H, C), out_dtype),
        grid_spec=pltpu.PrefetchScalarGridSpec(
            num_scalar_prefetch=1, grid=(A, H // bh),
            in_specs=[pl.BlockSpec((None, None, bh, C), lambda a, i, s: (a, s[1], i, 0)),
                      pl.BlockSpec((None, bh, C), lambda a, i, s: (a, i, 0))],
            out_specs=pl.BlockSpec((None, bh, C), lambda a, i, s: (a, i, 0))),
        compiler_params=pltpu.CompilerParams(dimension_semantics=("arbitrary", "arbitrary"),
                                             vmem_limit_bytes=VMEM_LIMIT),
    )(sel, g4, recv)


def _chip_reduce(g4, recv1, recv2, sel, buf, layer, n_layers, name, by_chip=False):
    _, _, H, C = g4.shape
    bh = _row_block(H, C)
    if by_chip:
        out_shape = jax.ShapeDtypeStruct((N_CHIPS, 2, H, C), F32)
        out_spec = pl.BlockSpec((None, None, bh, C), lambda i, s: (s[0], s[1], i, 0))
    else:
        out_shape = jax.ShapeDtypeStruct((n_layers, 1, 2, H, C), F32)
        out_spec = pl.BlockSpec((None, None, None, bh, C), lambda i, s: (layer, 0, s[1], i, 0))

    def body(sel_ref, g_ref, r1_ref, r2_ref, *rest):
        acc = g_ref[...].astype(F32) + r1_ref[...].astype(F32)
        for d in range(3):
            acc = acc + r2_ref[d].astype(F32)
        rest[-1][...] = acc

    in_specs = [pl.BlockSpec((None, None, bh, C), lambda i, s: (s[0], s[1], i, 0)),
                pl.BlockSpec((None, bh, C), lambda i, s: (s[0], i, 0)),
                pl.BlockSpec((3, bh, C), lambda i, s: (0, i, 0))]
    args = [sel, g4, recv1, recv2]
    aliases = {}
    if buf is not None:
        in_specs.append(ANY)
        args.append(buf)
        aliases = {4: 0}
    return pl.pallas_call(
        body, name=name, out_shape=out_shape,
        grid_spec=pltpu.PrefetchScalarGridSpec(num_scalar_prefetch=1, grid=(H // bh,), in_specs=in_specs,
                                               out_specs=out_spec),
        input_output_aliases=aliases,
        compiler_params=pltpu.CompilerParams(dimension_semantics=("arbitrary",), vmem_limit_bytes=VMEM_LIMIT),
    )(*args)


def _chip_reduce_many(groups, sel, comm, name):
    flat = [t for grp in groups for t in grp]
    n, ng = len(flat), len(groups)
    _, _, H, C = flat[0][0].shape
    bh = _row_block(H, C)
    last = H // bh - 1
    nci, nco = len(comm.ins), len(comm.out_shapes)

    def body(sel_ref, *refs):
        ins, refs = refs[:3 * n], refs[3 * n:]
        cins, refs = refs[:nci], refs[nci:]
        outs, refs = refs[:ng], refs[ng:]
        couts, sems = refs[:nco], refs[nco:]

        @pl.when(pl.program_id(0) == 0)
        def _():
            comm.start(cins, couts, sems)

        k = 0
        for gi, grp in enumerate(groups):
            for layer in range(len(grp)):
                g_ref, r1_ref, r2_ref = ins[3 * k:3 * k + 3]
                acc = g_ref[...].astype(F32) + r1_ref[...].astype(F32)
                for d in range(3):
                    acc = acc + r2_ref[d].astype(F32)
                outs[gi][layer] = acc
                k += 1

        @pl.when(pl.program_id(0) == last)
        def _():
            if comm.middle is not None:
                comm.middle(cins, couts, sems)
            comm.finish(cins, couts, sems)

    in_specs = [pl.BlockSpec((None, None, bh, C), lambda i, s: (s[0], s[1], i, 0)),
                pl.BlockSpec((None, bh, C), lambda i, s: (s[0], i, 0)),
                pl.BlockSpec((3, bh, C), lambda i, s: (0, i, 0))] * n + [ANY] * nci
    out_specs = [pl.BlockSpec((len(grp), None, None, bh, C), lambda i, s: (0, 0, s[1], i, 0)) for grp in groups]
    res = pl.pallas_call(
        body, name=name,
        out_shape=[jax.ShapeDtypeStruct((len(grp), 1, 2, H, C), F32) for grp in groups] + list(comm.out_shapes),
        grid_spec=pltpu.PrefetchScalarGridSpec(
            num_scalar_prefetch=1, grid=(H // bh,), in_specs=in_specs, out_specs=out_specs + [ANY] * nco,
            scratch_shapes=[pltpu.SemaphoreType.DMA((k,)) for k in comm.sem_counts]),
        input_output_aliases={1 + 3 * n + a: ng + b for a, b in comm.aliases.items()},
        compiler_params=pltpu.CompilerParams(dimension_semantics=("arbitrary",), vmem_limit_bytes=VMEM_LIMIT,
                                             has_side_effects=True),
    )(sel, *[a for t in flat for a in t], *comm.ins)
    return res[:ng], res[ng:]


def _cast_place(shards, layer, sel, name):
    _, _, H, C = shards.shape
    bh = _row_block(H, C)

    def body(sel_ref, x_ref, o_ref):
        o_ref[...] = x_ref[...].astype(BF16)

    return pl.pallas_call(
        body, name=name, out_shape=jax.ShapeDtypeStruct((N_CHIPS, 2, H, C), BF16),
        grid_spec=pltpu.PrefetchScalarGridSpec(
            num_scalar_prefetch=1, grid=(2, H // bh),
            in_specs=[pl.BlockSpec((None, None, bh, C), lambda h, i, s: (layer, h, i, 0))],
            out_specs=pl.BlockSpec((None, None, bh, C), lambda h, i, s: (s[0], h, i, 0))),
        compiler_params=pltpu.CompilerParams(dimension_semantics=("arbitrary", "arbitrary"),
                                             vmem_limit_bytes=VMEM_LIMIT),
    )(sel, shards)


def _cast_place_many(pairs, sel, comm, name):
    n = len(pairs)
    _, _, H, C = pairs[0][0].shape
    bh = _row_block(H, C)
    last = H // bh - 1
    nci, nco = len(comm.ins), len(comm.out_shapes)

    def body(sel_ref, *refs):
        xs, refs = refs[:n], refs[n:]
        cins, refs = refs[:nci], refs[nci:]
        outs, refs = refs[:n], refs[n:]
        couts, sems = refs[:nco], refs[nco:]
        h, i = pl.program_id(0), pl.program_id(1)

        @pl.when((h == 0) & (i == 0))
        def _():
            comm.start(cins, couts, sems)

        if comm.middle is not None:
            @pl.when((h == 1) & (i == 0))
            def _():
                comm.middle(cins, couts, sems)

        for x_ref, o_ref in zip(xs, outs):
            o_ref[...] = x_ref[...].astype(BF16)

        @pl.when((h == 1) & (i == last))
        def _():
            comm.finish(cins, couts, sems)

    res = pl.pallas_call(
        body, name=name,
        out_shape=[jax.ShapeDtypeStruct((N_CHIPS, 2, H, C), BF16)] * n + list(comm.out_shapes),
        grid_spec=pltpu.PrefetchScalarGridSpec(
            num_scalar_prefetch=1, grid=(2, H // bh),
            in_specs=[pl.BlockSpec((None, None, bh, C), lambda h, i, s, layer=layer: (layer, h, i, 0))
                      for _, layer in pairs] + [ANY] * nci,
            out_specs=[pl.BlockSpec((None, None, bh, C), lambda h, i, s: (s[0], h, i, 0))] * n + [ANY] * nco,
            scratch_shapes=[pltpu.SemaphoreType.DMA((k,)) for k in comm.sem_counts]),
        input_output_aliases={1 + n + a: n + b for a, b in comm.aliases.items()},
        compiler_params=pltpu.CompilerParams(dimension_semantics=("arbitrary", "arbitrary"),
                                             vmem_limit_bytes=VMEM_LIMIT, has_side_effects=True),
    )(sel, *[s for s, _ in pairs], *comm.ins)
    return res[:n], res[n:]


def _adamw_many(quads, name, comm=None):
    R, C = quads[0][0].shape
    br = _row_block(R, C)
    n = len(quads)

    def body(*refs):
        for k in range(n):
            g_ref, w_ref, m_ref, v_ref = refs[4 * k:4 * k + 4]
            d_ref, mo_ref, vo_ref = refs[4 * n + 3 * k:4 * n + 3 * k + 3]
            gv = g_ref[...]
            mn = ADAM_B1 * m_ref[...] + (1.0 - ADAM_B1) * gv
            vn = ADAM_B2 * v_ref[...] + (1.0 - ADAM_B2) * (gv * gv)
            m_hat = mn / (1.0 - ADAM_B1 ** ADAM_STEP)
            v_hat = vn / (1.0 - ADAM_B2 ** ADAM_STEP)
            d_ref[...] = -ADAM_LR * (m_hat / (jnp.sqrt(v_hat) + ADAM_EPS) + ADAM_WD * w_ref[...])
            mo_ref[...] = mn
            vo_ref[...] = vn

    spec = pl.BlockSpec((br, C), lambda i: (i, 0))
    res, got = _pcall(body, name=name, grid=(R // br,), comm=comm, out_shape=(jax.ShapeDtypeStruct((R, C), F32),) * (3 * n),
                      in_specs=[spec] * (4 * n), out_specs=(spec,) * (3 * n), args=[a for q in quads for a in q])
    return [res[3 * k:3 * k + 3] for k in range(n)], got


def _adamw(g, w, m, v, name):
    return _adamw_many([(g, w, m, v)], name)[0][0]


def _part_rows(shape):
    return -(-math.prod(shape) // (8 * LANES)) * 8


def _pack_rows(arrs, total_rows):
    parts = []
    for a in arrs:
        flat = a.reshape(-1)
        rows = _part_rows(a.shape)
        parts.append(jnp.pad(flat, (0, rows * LANES - flat.shape[0])).reshape(rows, LANES))
    packed = jnp.concatenate(parts, axis=0)
    return jnp.pad(packed, ((0, total_rows - packed.shape[0]), (0, 0)))


def _unpack_rows(packed, shapes):
    out, r = [], 0
    for shp in shapes:
        size, rows = math.prod(shp), _part_rows(shp)
        out.append(packed[r:r + rows].reshape(-1)[:size].reshape(shp))
        r += rows
    return out


def _rows_of(shapes):
    return sum(_part_rows(s) for s in shapes)


_GATHER_STAGES = {"start": ["wa", "mkv0", "wo0"], "mixer_a_fwd": ["wup0", "wdown0"],
                  "mlp_fwd_0": ["wq", "wkv", "mkv1", "wo1"], "fox_fwd": ["wup1", "wdown1"]}
_SWAP_STAGES = {"mixer_b_bwd": ["wup1", "wdown1"]}
_REDUCE_STAGES = {"fox_bwd": ["wup1", "wdown1"], "mlp_bwd_0": ["wo1", "wq", "wkv", "mkv1"],
                  "mixer_a_bwd": ["wup0", "wdown0"], "end": ["wa", "wo0", "mkv0", "small"]}
_SMALL = ["ws", "b_s", "sg", "sb", "bf", "ln_g", "ln_b", "loss"]


class _Staged:
    def __init__(self, placed, sel, dm, ng, shard_cols, n_small):
        self.placed, self.sel, self.dm, self.ng, self.shard_cols, self.n_small = placed, sel, dm, ng, shard_cols, n_small
        self.g4, self.recv1, self.recv2 = {}, {}, {}

    def _weight(self, name, a):
        a = a.reshape(N_CHIPS, 2 * a.shape[2], a.shape[3])
        if name in ("wa", "wkv"):
            a = a.transpose(1, 0, 2).reshape(a.shape[1], -1)
            if name == "wkv":
                a = jnp.concatenate([a[:, :2 * self.dm],
                                     jnp.pad(a[:, 2 * self.dm:], ((0, 0), (0, LANES - self.ng)))], axis=1)
            return a
        if name.startswith("wup") or name.startswith("wdown"):
            return a
        return a.reshape(-1, a.shape[2])

    def _partial(self, name, grads):
        if name == "small":
            a = _pack_rows([grads[n] for n in _SMALL], self.n_small).reshape(N_CHIPS, -1, LANES)
        elif name in self.shard_cols:
            n = self.shard_cols[name]
            g = grads[name][:, :N_CHIPS * n]
            a = g.reshape(g.shape[0], N_CHIPS, n).transpose(1, 0, 2)
        elif name.startswith("wup") or name.startswith("wdown"):
            a = grads[name]
        else:
            g = grads[name]
            a = g.reshape(N_CHIPS, g.shape[0] // N_CHIPS, g.shape[1])
        return a.reshape(N_CHIPS, 2, a.shape[1] // 2, a.shape[2])

    def plan(self, stage, w, grads):
        if stage in _GATHER_STAGES:
            return _gather_weights([self.placed[n] for n in _GATHER_STAGES[stage]],
                                   early=1 if stage == "mixer_a_fwd" else None)
        if stage in _SWAP_STAGES:
            for n in _SWAP_STAGES[stage]:
                self.g4[n] = self._partial(n, grads)
            return _pair_swap([self.g4[n] for n in _SWAP_STAGES[stage]])
        if stage in _REDUCE_STAGES:
            names = _REDUCE_STAGES[stage]
            late = [n for n in names if n not in self.recv1]
            for n in late:
                self.g4[n] = self._partial(n, grads)
            if late:
                got = _run_comm(_pair_swap([self.g4[n] for n in late]), "reduce_pair_swap_" + stage)
                self.recv1.update(zip(late, got))
            return _chip_exchange([_pair_add(self.g4[n], self.recv1[n], self.sel, F32 if n == "small" else BF16,
                                             "reduce_pair_add_" + n) for n in names])
        return None

    def done(self, stage, results, w):
        if stage in _GATHER_STAGES:
            for n, a in zip(_GATHER_STAGES[stage], results):
                w[n] = self._weight(n, a)
        elif stage in _SWAP_STAGES:
            self.recv1.update(zip(_SWAP_STAGES[stage], results))
        elif stage in _REDUCE_STAGES:
            self.recv2.update(zip(_REDUCE_STAGES[stage], results))


def kernel(x, mem, a_w_in, a_sgu_ln_g, a_sgu_ln_b, a_w_s, a_b_s, kv_w, kv_b_f, b_w_q, mem_w_kv, w_o, ln_g, ln_b, w_up, w_down, loss_target, m_a_w_in, m_a_sgu_ln_g, m_a_sgu_ln_b, m_a_w_s, m_a_b_s, m_kv_w, m_kv_b_f, m_b_w_q, m_mem_w_kv, m_w_o, m_ln_g, m_ln_b, m_w_up, m_w_down, v_a_w_in, v_a_sgu_ln_g, v_a_sgu_ln_b, v_a_w_s, v_a_b_s, v_kv_w, v_kv_b_f, v_b_w_q, v_mem_w_kv, v_w_o, v_ln_g, v_ln_b, v_w_up, v_w_down):
    xi, yi, ci = lax.axis_index("x"), lax.axis_index("y"), lax.axis_index("c")
    chip = 2 * xi + yi
    sel = jnp.stack([chip, ci]).astype(jnp.int32)
    S, D = x.shape[1], x.shape[2]
    DQ = mem_w_kv.shape[2] // 2
    DM = D - DQ
    NG = DM // HEAD_DIM

    shards = {"wa": (a_w_in, 0), "wkv": (kv_w[None], 0), "wq": (b_w_q, 0), "mkv0": (mem_w_kv, 0), "mkv1": (mem_w_kv, 1),
              "wo0": (w_o, 0), "wo1": (w_o, 1), "wup0": (w_up, 0), "wup1": (w_up, 1), "wdown0": (w_down, 0),
              "wdown1": (w_down, 1)}
    halves = lambda s: s.reshape(s.shape[0], 2, s.shape[1] // 2, s.shape[2])
    mlp = ["wup0", "wup1", "wdown0", "wdown1"]
    placed = {n: _cast_place(halves(s), layer, sel, "cast_place_" + n)
              for n, (s, layer) in shards.items() if n not in mlp}
    ln_pack = jnp.concatenate([ln_g.reshape(4, -1), ln_b.reshape(4, -1)], axis=0)
    ln_buf = lax.dynamic_update_slice(jnp.zeros((N_CHIPS,) + ln_pack.shape, F32), ln_pack[None], (chip, 0, 0))
    mlp_placed, got = _cast_place_many(
        [(halves(shards[n][0]), shards[n][1]) for n in mlp], sel,
        _gather_weights([placed[n] for n in _GATHER_STAGES["start"]], whole=[ln_buf]), "cast_place_mlp")
    placed.update(zip(mlp, mlp_placed))
    ln_shape = (4, D)
    small_shapes = [a_w_s.shape, a_b_s.shape, a_sgu_ln_g.shape, a_sgu_ln_b.shape, kv_b_f.shape, ln_shape, ln_shape,
                    (1, LANES)]
    n_small = -(-_rows_of(small_shapes) // 64) * 64
    hooks = _Staged(placed, sel, DM, NG, {"wa": a_w_in.shape[2], "wkv": kv_w.shape[1]}, n_small)
    weights = {"sg": a_sgu_ln_g, "sb": a_sgu_ln_b, "ws": a_w_s[0], "b_s": a_b_s[0], "bf": kv_b_f}
    hooks.done("start", got[:-1], weights)
    ln_full = got[-1].transpose(1, 0, 2).reshape(8, D)
    weights["ln_g"], weights["ln_b"] = ln_full[:4], ln_full[4:]

    _, grad_x, gr = _local_step(x[0], mem[0], loss_target[0], weights, hooks)

    def chip_sums(groups):
        bufs = []
        for grp in groups:
            buf = None
            for layer, n in enumerate(grp):
                buf = _chip_reduce(hooks.g4[n], hooks.recv1[n], hooks.recv2[n], sel, buf, layer, len(grp),
                                   "reduce_chip_sum_" + n)
            bufs.append(buf)
        return bufs

    whole = lambda a: a.reshape(a.shape[0], 2 * a.shape[3], a.shape[4])
    flat = lambda a: a.reshape(-1, a.shape[-1])

    def update(name, g, w, m, v):
        d, mn, vn = _adamw(flat(g), flat(w), flat(m), flat(v), "adamw_" + name)
        return d.reshape(w.shape), mn.reshape(w.shape), vn.reshape(w.shape)

    parts = lambda n: (hooks.g4[n], hooks.recv1[n], hooks.recv2[n])
    mlp_sums, got = _chip_reduce_many([[parts("wup0"), parts("wup1")], [parts("wdown0"), parts("wdown1")]], sel,
                                      hooks.plan("end", weights, gr), "reduce_chip_sum_mlp")
    hooks.done("end", got, weights)
    small_mine = _chip_reduce(hooks.g4["small"], hooks.recv1["small"], hooks.recv2["small"], sel, None, 0, 1,
                              "reduce_chip_sum_small", by_chip=True)
    red, small_all = _pair_gather(chip_sums([["wa"], ["wkv"], ["wq"], ["mkv0", "mkv1"], ["wo0", "wo1"]])
                                  + list(mlp_sums), small_mine, "reduce_pair_gather")
    g_a_w_in, g_kv_w, g_b_w_q, g_mem_w_kv, g_w_o, g_w_up, g_w_down = [whole(a) for a in red]
    g_kv_w = g_kv_w[0]
    g_w_down = g_w_down.transpose(0, 2, 1)
    small_all = small_all.reshape(n_small, LANES)
    g_ws, g_bs, g_sg, g_sb, g_bf, g_lng, g_lnb, loss_row = _unpack_rows(small_all, small_shapes)
    loss = loss_row[0, 0]
    dsh = D // N_CHIPS
    g_ln_g = lax.dynamic_slice_in_dim(g_lng, chip * dsh, dsh, axis=1).reshape(ln_g.shape)
    g_ln_b = lax.dynamic_slice_in_dim(g_lnb, chip * dsh, dsh, axis=1).reshape(ln_b.shape)

    upd = {
        "a_w_in": update("a_w_in", g_a_w_in, a_w_in, m_a_w_in, v_a_w_in),
        "kv_w": update("kv_w", g_kv_w, kv_w, m_kv_w, v_kv_w),
        "b_w_q": update("b_w_q", g_b_w_q, b_w_q, m_b_w_q, v_b_w_q),
        "mem_w_kv": update("mem_w_kv", g_mem_w_kv, mem_w_kv, m_mem_w_kv, v_mem_w_kv),
        "w_o": update("w_o", g_w_o, w_o, m_w_o, v_w_o),
        "w_up": update("w_up", g_w_up, w_up, m_w_up, v_w_up),
        "w_down": update("w_down", g_w_down, w_down, m_w_down, v_w_down),
    }
    tiny_g = [g_sg, g_sb, g_ws, g_bs, g_bf, g_ln_g, g_ln_b]
    tiny_w = [a_sgu_ln_g, a_sgu_ln_b, a_w_s, a_b_s, kv_b_f, ln_g, ln_b]
    tiny_m = [m_a_sgu_ln_g, m_a_sgu_ln_b, m_a_w_s, m_a_b_s, m_kv_b_f, m_ln_g, m_ln_b]
    tiny_v = [v_a_sgu_ln_g, v_a_sgu_ln_b, v_a_w_s, v_a_b_s, v_kv_b_f, v_ln_g, v_ln_b]
    tiny_shapes = [a.shape for a in tiny_w]
    n_tiny = -(-_rows_of(tiny_shapes) // 8) * 8
    td, tm_, tv = _adamw(_pack_rows(tiny_g, n_tiny), _pack_rows(tiny_w, n_tiny), _pack_rows(tiny_m, n_tiny),
                         _pack_rows(tiny_v, n_tiny), "adamw_small")
    for name, d, mn, vn in zip(["a_sgu_ln_g", "a_sgu_ln_b", "a_w_s", "a_b_s", "kv_b_f", "ln_g", "ln_b"],
                               _unpack_rows(td, tiny_shapes), _unpack_rows(tm_, tiny_shapes),
                               _unpack_rows(tv, tiny_shapes)):
        upd[name] = (d, mn, vn)

    order = ["a_w_in", "a_sgu_ln_g", "a_sgu_ln_b", "a_w_s", "a_b_s", "kv_w", "kv_b_f", "b_w_q", "mem_w_kv", "w_o",
             "ln_g", "ln_b", "w_up", "w_down"]
    grads = {"a_w_in": g_a_w_in.reshape(a_w_in.shape), "a_sgu_ln_g": g_sg.reshape(a_sgu_ln_g.shape),
             "a_sgu_ln_b": g_sb.reshape(a_sgu_ln_b.shape), "a_w_s": g_ws.reshape(a_w_s.shape),
             "a_b_s": g_bs.reshape(a_b_s.shape), "kv_w": g_kv_w, "kv_b_f": g_bf, "b_w_q": g_b_w_q.reshape(b_w_q.shape),
             "mem_w_kv": g_mem_w_kv, "w_o": g_w_o, "ln_g": g_ln_g, "ln_b": g_ln_b, "w_up": g_w_up, "w_down": g_w_down}
    return (loss, grad_x[None], *[grads[n] for n in order], *[upd[n][0] for n in order],
            *[upd[n][1] for n in order], *[upd[n][2] for n in order])
```

```python
import functools
import math

import jax
import jax.numpy as jnp
from jax import lax
from jax.experimental import pallas as pl
from jax.experimental.pallas import tpu as pltpu

F32 = jnp.float32
BF16 = jnp.bfloat16

HEAD_DIM = 64
PAIR = 2 * HEAD_DIM
CHUNK = 128
LN_EPS = 1e-5
ALPHA = 4 ** 0.25
QK_SCALE = 1.0 / math.sqrt(HEAD_DIM)
NEG = -1e30
N_CHIPS = 4
FOX_BLOCK = 256
LANES = 128
VMEM_LIMIT = 48 * 2 ** 20

ADAM_LR, ADAM_B1, ADAM_B2, ADAM_EPS, ADAM_WD, ADAM_STEP = 0.001, 0.9, 0.999, 1e-08, 0.01, 10

MESH = pl.DeviceIdType.MESH
ANY = pl.BlockSpec(memory_space=pl.ANY)


def _dot(a, b):
    return jnp.dot(a, b, preferred_element_type=F32)


def _dot_nt(a, b):
    return lax.dot_general(a, b, (((1,), (1,)), ((), ())), preferred_element_type=F32)


def _dot_tn(a, b):
    return lax.dot_general(a, b, (((0,), (0,)), ((), ())), preferred_element_type=F32)


def _ln_stats(r):
    mu = jnp.mean(r, axis=-1, keepdims=True)
    d = r - mu
    var = jnp.mean(d * d, axis=-1, keepdims=True)
    rstd = lax.rsqrt(var + LN_EPS)
    return d * rstd, rstd


def _ln_bwd(dy, xhat, rstd, g):
    dxh = dy * g
    m1 = jnp.mean(dxh, axis=-1, keepdims=True)
    m2 = jnp.mean(dxh * xhat, axis=-1, keepdims=True)
    return rstd * (dxh - m1 - xhat * m2)


_GELU_K = math.sqrt(2.0 / math.pi)


def _gelu(x):
    return 0.5 * x * (1.0 + jnp.tanh(_GELU_K * (x + 0.044715 * x * x * x)))


def _gelu_grad(x):
    t = jnp.tanh(_GELU_K * (x + 0.044715 * x * x * x))
    return 0.5 * (1.0 + t) + 0.5 * x * (1.0 - t * t) * _GELU_K * (1.0 + 3 * 0.044715 * x * x)


def _lane_mask(e):
    lane = lax.broadcasted_iota(jnp.int32, (1, PAIR), 1)
    return (lane >= HEAD_DIM * e) & (lane < HEAD_DIM * (e + 1))


def _first_head():
    return lax.broadcasted_iota(jnp.int32, (1, PAIR), 1) < HEAD_DIM


def _mem_probs(qh, mkp):
    sc = _dot_nt(qh, mkp) * QK_SCALE
    ex = jnp.exp(sc - jnp.max(sc, axis=-1, keepdims=True))
    return ex / jnp.sum(ex, axis=-1, keepdims=True)


def _mem_attn_fwd(qm_b, mkv_ref, dq_dim):
    outs = []
    for mp in range(dq_dim // PAIR):
        qp = qm_b[:, mp * PAIR:(mp + 1) * PAIR]
        mkp = mkv_ref[:, mp * PAIR:(mp + 1) * PAIR]
        mvp = mkv_ref[:, dq_dim + mp * PAIR:dq_dim + (mp + 1) * PAIR]
        heads = []
        for e in (0, 1):
            qh = jnp.where(_lane_mask(e), qp, jnp.zeros_like(qp))
            p = _mem_probs(qh, mkp)
            heads.append(_dot(p.astype(BF16), mvp))
        outs.append(jnp.where(_first_head(), heads[0], heads[1]))
    return outs


def _mem_attn_bwd(qm_b, mkv_ref, dmo_pairs, dmkv_ref, dq_dim):
    dqs = []
    for mp in range(dq_dim // PAIR):
        ks = slice(mp * PAIR, (mp + 1) * PAIR)
        vs = slice(dq_dim + mp * PAIR, dq_dim + (mp + 1) * PAIR)
        qp = qm_b[:, ks]
        mkp = mkv_ref[:, ks]
        mvp = mkv_ref[:, vs]
        dmo_b = dmo_pairs[mp].astype(BF16)
        dq = None
        dmk = None
        dmv = None
        for e in (0, 1):
            hm = _lane_mask(e)
            qh = jnp.where(hm, qp, jnp.zeros_like(qp))
            p = _mem_probs(qh, mkp)
            doh = jnp.where(hm, dmo_b, jnp.zeros_like(dmo_b))
            dp = _dot_nt(doh, mvp)
            ds = p * (dp - jnp.sum(dp * p, axis=-1, keepdims=True))
            dsb = (ds * QK_SCALE).astype(BF16)
            kh = jnp.where(hm, mkp, jnp.zeros_like(mkp))
            dq_e = _dot(dsb, kh)
            dmk_e = _dot_tn(dsb, qh)
            dmv_e = _dot_tn(p.astype(BF16), doh)
            dq = dq_e if dq is None else dq + dq_e
            dmk = dmk_e if dmk is None else dmk + dmk_e
            dmv = dmv_e if dmv is None else dmv + dmv_e
        dmkv_ref[:, ks] += dmk
        dmkv_ref[:, vs] += dmv
        dqs.append(dq)
    return dqs


def _params(n_axes=1):
    return pltpu.CompilerParams(dimension_semantics=("arbitrary",) * n_axes, vmem_limit_bytes=VMEM_LIMIT)


def _full(shape):
    return pl.BlockSpec(shape, lambda *_: (0,) * len(shape))


def _rows(tm, cols):
    return pl.BlockSpec((tm, cols), lambda i: (i, 0))


def _cols(rows, tm):
    return pl.BlockSpec((rows, tm), lambda i: (0, i))


def _pick(n, pref):
    if n <= pref:
        return n
    best = LANES
    for t in range(LANES, pref + 1, LANES):
        if n % t == 0:
            best = t
    return best if 2 * best >= pref or n > 2 * pref else n


class _Comm:
    def __init__(self, ins, out_shapes, aliases, sem_counts, start, finish, middle=None):
        self.ins, self.out_shapes, self.aliases, self.sem_counts = list(ins), list(out_shapes), dict(aliases), sem_counts
        self.start, self.finish = start, finish
        self.middle = middle


def _pcall(body, *, name, grid, in_specs, out_specs, out_shape, args, scratch_shapes=(), comm=None, aliases=None):
    n_in, n_out, n_scr = len(in_specs), len(out_shape), len(scratch_shapes)
    aliases = dict(aliases or {})
    if comm is None:
        res = pl.pallas_call(body, name=name, grid=grid, out_shape=tuple(out_shape), in_specs=list(in_specs),
                             out_specs=tuple(out_specs), scratch_shapes=list(scratch_shapes),
                             input_output_aliases=aliases, compiler_params=_params())(*args)
        return tuple(res), ()
    nci, nco = len(comm.ins), len(comm.out_shapes)
    last = grid[0] - 1

    def wrapped(*refs):
        ins, refs = refs[:n_in], refs[n_in:]
        cins, refs = refs[:nci], refs[nci:]
        outs, refs = refs[:n_out], refs[n_out:]
        couts, refs = refs[:nco], refs[nco:]
        scr, sems = refs[:n_scr], refs[n_scr:]

        @pl.when(pl.program_id(0) == 0)
        def _():
            comm.start(cins, couts, sems)

        if comm.middle is not None:
            @pl.when(pl.program_id(0) == (last + 1) // 2)
            def _():
                comm.middle(cins, couts, sems)

        body(*ins, *outs, *scr)

        @pl.when(pl.program_id(0) == last)
        def _():
            comm.finish(cins, couts, sems)

    res = pl.pallas_call(
        wrapped, name=name, grid=grid, out_shape=tuple(out_shape) + tuple(comm.out_shapes),
        in_specs=list(in_specs) + [ANY] * nci, out_specs=tuple(out_specs) + (ANY,) * nco,
        input_output_aliases={**aliases, **{n_in + a: n_out + b for a, b in comm.aliases.items()}},
        scratch_shapes=list(scratch_shapes) + [pltpu.SemaphoreType.DMA((k,)) for k in comm.sem_counts],
        compiler_params=pltpu.CompilerParams(dimension_semantics=("arbitrary",), vmem_limit_bytes=VMEM_LIMIT,
                                             has_side_effects=True),
    )(*args, *comm.ins)
    return tuple(res[:n_out]), tuple(res[n_out:])


def _run_comm(comm, name):
    nci, nco = len(comm.ins), len(comm.out_shapes)

    def body(*refs):
        cins, couts, sems = refs[:nci], refs[nci:nci + nco], refs[nci + nco:]
        comm.start(cins, couts, sems)
        if comm.middle is not None:
            comm.middle(cins, couts, sems)
        comm.finish(cins, couts, sems)

    return pl.pallas_call(
        body, name=name, out_shape=tuple(comm.out_shapes), in_specs=[ANY] * nci, out_specs=(ANY,) * nco,
        input_output_aliases=comm.aliases, scratch_shapes=[pltpu.SemaphoreType.DMA((k,)) for k in comm.sem_counts],
        compiler_params=pltpu.CompilerParams(has_side_effects=True),
    )(*comm.ins)


def _mm(a, b, name, a_is_transposed, out_split=1):
    (M, R) = a.shape if a_is_transposed else a.shape[::-1]
    _, N = b.shape
    bm, br = _pick(M, 1024 if a_is_transposed else 512), _pick(R, 1024 if a_is_transposed else 512)
    ncol = N // out_split
    bn = _pick(ncol, 1024)
    per = ncol // bn
    last = R // br - 1

    def body(a_ref, b_ref, o_ref, acc_ref):
        @pl.when(pl.program_id(2) == 0)
        def _():
            acc_ref[...] = jnp.zeros_like(acc_ref)

        acc_ref[...] += (_dot if a_is_transposed else _dot_tn)(a_ref[...], b_ref[...])

        @pl.when(pl.program_id(2) == last)
        def _():
            o_ref[...] = acc_ref[...].astype(BF16)

    if out_split == 1:
        out_shape = jax.ShapeDtypeStruct((M, N), BF16)
        out_spec = pl.BlockSpec((bm, bn), lambda i, j, r: (i, j))
    else:
        out_shape = jax.ShapeDtypeStruct((out_split, M, ncol), BF16)
        out_spec = pl.BlockSpec((None, bm, bn), lambda i, j, r: (j // per, i, j % per))
    a_spec = (pl.BlockSpec((bm, br), lambda i, j, r: (i, r)) if a_is_transposed
              else pl.BlockSpec((br, bm), lambda i, j, r: (r, i)))
    return pl.pallas_call(
        body, name=name, out_shape=out_shape, grid=(M // bm, N // bn, R // br),
        in_specs=[a_spec, pl.BlockSpec((br, bn), lambda i, j, r: (r, j))], out_specs=out_spec,
        scratch_shapes=[pltpu.VMEM((bm, bn), F32)],
        compiler_params=pltpu.CompilerParams(dimension_semantics=("parallel", "parallel", "arbitrary"),
                                             vmem_limit_bytes=VMEM_LIMIT),
    )(a, b)


def _mm_tn(a, b, name, out_split=1):
    return _mm(a, b, name, False, out_split)


def _mm_nn(at, b, name, out_split=1):
    return _mm(at, b, name, True, out_split)


def _mm_small(a, b, name):
    def body(a_ref, b_ref, o_ref):
        o_ref[...] = _dot(a_ref[...], b_ref[...]).astype(BF16)

    return pl.pallas_call(body, name=name, out_shape=jax.ShapeDtypeStruct((a.shape[0], b.shape[1]), BF16),
                          compiler_params=pltpu.CompilerParams(vmem_limit_bytes=VMEM_LIMIT))(a, b)


def _mlp_out(x_ref, wup_ref, wdown_ref, g_ref, b_ref, r_ref, h_ref):
    NC, _, FC = wup_ref.shape
    xv = x_ref[...]
    xb = xv.astype(BF16)
    y = jnp.zeros(xv.shape, F32)
    for c in range(NC):
        a = jnp.maximum(_dot(xb, wup_ref[c]), 0.0)
        hb = (a * a).astype(BF16)
        h_ref[:, c * FC:(c + 1) * FC] = hb
        y = y + _dot(hb, wdown_ref[c])
    r = ALPHA * xv + y
    r_ref[...] = r
    xhat, _ = _ln_stats(r)
    return xhat * g_ref[...] + b_ref[...]


def _mlp_fwd_loss(x, wup4, wdown4, g, b, target, name):
    S, D = x.shape
    NC, _, FC = wup4.shape
    tm = min(512, S)

    def body(x_ref, wup_ref, wdown_ref, g_ref, b_ref, t_ref, d_ref, r_ref, h_ref, l_ref):
        @pl.when(pl.program_id(0) == 0)
        def _():
            l_ref[...] = jnp.zeros_like(l_ref)

        err = _mlp_out(x_ref, wup_ref, wdown_ref, g_ref, b_ref, r_ref, h_ref) - t_ref[...]
        d_ref[...] = err * (1.0 / D)
        l_ref[...] += (0.5 / D) * jnp.sum(err * err)

    return pl.pallas_call(
        body, name=name, grid=(S // tm,),
        out_shape=(jax.ShapeDtypeStruct((S, D), F32), jax.ShapeDtypeStruct((S, D), F32),
                   jax.ShapeDtypeStruct((S, NC * FC), BF16), jax.ShapeDtypeStruct((8, LANES), F32)),
        in_specs=[_rows(tm, D), _full(wup4.shape), _full(wdown4.shape), _full((1, D)), _full((1, D)), _rows(tm, D)],
        out_specs=(_rows(tm, D), _rows(tm, D), _rows(tm, NC * FC), _full((8, LANES))),
        compiler_params=_params(),
    )(x, wup4, wdown4, g, b, target)


def _mlp_fwd(x, wup4, wdown4, g, b, name, comm=None):
    S, D = x.shape
    NC, _, FC = wup4.shape
    tm = min(512, S)

    def body(x_ref, wup_ref, wdown_ref, g_ref, b_ref, xo_ref, xb_ref, xbt_ref, r_ref, h_ref):
        xo = _mlp_out(x_ref, wup_ref, wdown_ref, g_ref, b_ref, r_ref, h_ref)
        xo_ref[...] = xo
        xb_ref[...] = xo.astype(BF16)
        xbt_ref[...] = xo.T.astype(BF16)

    return _pcall(
        body, name=name, grid=(S // tm,), comm=comm,
        out_shape=(jax.ShapeDtypeStruct((S, D), F32), jax.ShapeDtypeStruct((S, D), BF16),
                   jax.ShapeDtypeStruct((D, S), BF16), jax.ShapeDtypeStruct((S, D), F32),
                   jax.ShapeDtypeStruct((S, NC * FC), BF16)),
        in_specs=[_rows(tm, D), _full(wup4.shape), _full(wdown4.shape), _full((1, D)), _full((1, D))],
        out_specs=(_rows(tm, D), _rows(tm, D), _cols(D, tm), _rows(tm, D), _rows(tm, NC * FC)),
        args=(x, wup4, wdown4, g, b))


def _mlp_bwd(dxo, r, h, wup4, wdown4, g, name, comm=None):
    S, D = r.shape
    NC, _, FC = wup4.shape
    tm = min(256, S)

    def body(dxo_ref, r_ref, h_ref, wup_ref, wdown_ref, g_ref, dx_ref, drbt_ref, dpre_ref, dgb_ref):
        @pl.when(pl.program_id(0) == 0)
        def _():
            dgb_ref[...] = jnp.zeros_like(dgb_ref)

        xhat, rstd = _ln_stats(r_ref[...])
        dy = dxo_ref[...]
        dgb_ref[0:1, :] += jnp.sum(dy * xhat, axis=0, keepdims=True)
        dgb_ref[1:2, :] += jnp.sum(dy, axis=0, keepdims=True)
        dr = _ln_bwd(dy, xhat, rstd, g_ref[...])
        drb = dr.astype(BF16)
        drbt_ref[...] = dr.T.astype(BF16)
        dx = ALPHA * dr
        for c in range(NC):
            dh = _dot_nt(drb, wdown_ref[c])
            a = jnp.sqrt(h_ref[:, c * FC:(c + 1) * FC].astype(F32))
            dpre = (2.0 * a * dh).astype(BF16)
            dpre_ref[:, c * FC:(c + 1) * FC] = dpre
            dx = dx + _dot_nt(dpre, wup_ref[c])
        dx_ref[...] = dx

    return _pcall(
        body, name=name, grid=(S // tm,), comm=comm,
        out_shape=(jax.ShapeDtypeStruct((S, D), F32), jax.ShapeDtypeStruct((D, S), BF16),
                   jax.ShapeDtypeStruct((S, NC * FC), BF16), jax.ShapeDtypeStruct((2, D), F32)),
        in_specs=[_rows(tm, D), _rows(tm, D), _rows(tm, NC * FC), _full(wup4.shape), _full(wdown4.shape),
                  _full((1, D))],
        out_specs=(_rows(tm, D), _cols(D, tm), _rows(tm, NC * FC), _full((2, D))),
        args=(dxo, r, h, wup4, wdown4, g))


def _tail_fwd_store(xv, cat_ref, wo_ref, g_ref, b_ref, xo_ref, xbt_ref, r_ref, catt_ref):
    cat = cat_ref[...]
    y = _dot(cat, wo_ref[...])
    r = ALPHA * xv + y
    xhat, _ = _ln_stats(r)
    xo = xhat * g_ref[...] + b_ref[...]
    xo_ref[...] = xo
    xbt_ref[...] = xo.T.astype(BF16)
    catt_ref[...] = cat.astype(F32).T.astype(BF16)
    r_ref[...] = r


def _tail_bwd_head(dxo_ref, r_ref, g_ref, wo_ref, dgb_ref, drb_ref):
    xhat, rstd = _ln_stats(r_ref[...])
    dy = dxo_ref[...]
    dgb_ref[0:1, :] += jnp.sum(dy * xhat, axis=0, keepdims=True)
    dgb_ref[1:2, :] += jnp.sum(dy, axis=0, keepdims=True)
    dr = _ln_bwd(dy, xhat, rstd, g_ref[...])
    drb = dr.astype(BF16)
    drb_ref[...] = drb
    return dr, _dot_nt(drb, wo_ref[...])


def _causal(shape):
    return lax.broadcasted_iota(jnp.int32, shape, 1) <= lax.broadcasted_iota(jnp.int32, shape, 0)


def _sgu_mixed(ws_ref, bst_ref, gp, vpair):
    tril = _causal((CHUNK, CHUNK))
    w0 = jnp.where(tril, ws_ref[2 * gp], 0.0).astype(BF16)
    w1 = jnp.where(tril, ws_ref[2 * gp + 1], 0.0).astype(BF16)
    m0 = _dot(w0, vpair) + bst_ref[:, 2 * gp:2 * gp + 1]
    m1 = _dot(w1, vpair) + bst_ref[:, 2 * gp + 1:2 * gp + 2]
    return jnp.where(_first_head(), m0, m1), w0, w1


def _mixer_a_fwd(x, wa, sg, sb, ws, bst, mkv, wo, g, b, name, comm=None):
    S, D = x.shape
    DQ = mkv.shape[1] // 2
    DM = D - DQ
    NP = DM // PAIR
    tm = min(512, S)

    def body(x_ref, wa_ref, sg_ref, sb_ref, ws_ref, bst_ref, mkv_ref, wo_ref, g_ref, b_ref,
             xo_ref, xbt_ref, r_ref, catt_ref, xint_ref, zb_ref, z_s, zv_s, cat_ref):
        xv = x_ref[...]
        xint_ref[...] = xv.T.astype(BF16)
        z_s[...] = _dot(xv.astype(BF16), wa_ref[...])
        zb_ref[...] = z_s[...].astype(BF16)
        vhat, _ = _ln_stats(_gelu(z_s[:, DM:2 * DM]))
        zv_s[...] = (vhat * sg_ref[...] + sb_ref[...]).astype(BF16)
        for c in range(tm // CHUNK):
            rs = slice(c * CHUNK, (c + 1) * CHUNK)
            for gp in range(NP):
                ls = slice(gp * PAIR, (gp + 1) * PAIR)
                mixed, _, _ = _sgu_mixed(ws_ref, bst_ref, gp, zv_s[rs, ls])
                cat_ref[rs, ls] = (_gelu(z_s[rs, ls]) * mixed).astype(BF16)
        mo = _mem_attn_fwd(z_s[:, 2 * DM:].astype(BF16), mkv_ref, DQ)
        for mp in range(DQ // PAIR):
            cat_ref[:, DM + mp * PAIR:DM + (mp + 1) * PAIR] = mo[mp].astype(BF16)
        _tail_fwd_store(xv, cat_ref, wo_ref, g_ref, b_ref, xo_ref, xbt_ref, r_ref, catt_ref)

    return _pcall(
        body, name=name, grid=(S // tm,), comm=comm,
        out_shape=(jax.ShapeDtypeStruct((S, D), F32), jax.ShapeDtypeStruct((D, S), BF16),
                   jax.ShapeDtypeStruct((S, D), F32), jax.ShapeDtypeStruct((D, S), BF16),
                   jax.ShapeDtypeStruct((D, S), BF16), jax.ShapeDtypeStruct((S, 2 * DM + DQ), BF16)),
        in_specs=[_rows(tm, D), _full(wa.shape), _full(sg.shape), _full(sb.shape), _full(ws.shape), _full(bst.shape),
                  _full(mkv.shape), _full(wo.shape), _full((1, D)), _full((1, D))],
        out_specs=(_rows(tm, D), _cols(D, tm), _rows(tm, D), _cols(D, tm), _cols(D, tm), _rows(tm, 2 * DM + DQ)),
        scratch_shapes=[pltpu.VMEM((tm, 2 * DM + DQ), F32), pltpu.VMEM((tm, DM), BF16), pltpu.VMEM((tm, D), BF16)],
        args=(x, wa, sg, sb, ws, bst, mkv, wo, g, b))


def _mixer_a_bwd(zb, dxo, r, wa, sg, sb, ws, bst, mkv, wo, g, name, comm=None):
    S, D = r.shape
    DQ = mkv.shape[1] // 2
    DM = D - DQ
    NP = DM // PAIR
    NG = DM // HEAD_DIM
    tm = min(512, S)

    def body(zb_ref, dxo_ref, r_ref, wa_ref, sg_ref, sb_ref, ws_ref, bst_ref, mkv_ref, wo_ref, g_ref,
             dx_ref, dz_ref, drb_ref, dgb_ref, dsgb_ref, dws_ref, dbst_ref, dmkv_ref, z_s, zv_s, dzv_s):
        @pl.when(pl.program_id(0) == 0)
        def _():
            dgb_ref[...] = jnp.zeros_like(dgb_ref)
            dsgb_ref[...] = jnp.zeros_like(dsgb_ref)
            dws_ref[...] = jnp.zeros_like(dws_ref)
            dbst_ref[...] = jnp.zeros_like(dbst_ref)
            dmkv_ref[...] = jnp.zeros_like(dmkv_ref)

        dr, dcat = _tail_bwd_head(dxo_ref, r_ref, g_ref, wo_ref, dgb_ref, drb_ref)
        z_s[...] = zb_ref[...].astype(F32)
        vhat, vrstd = _ln_stats(_gelu(z_s[:, DM:2 * DM]))
        zv_s[...] = (vhat * sg_ref[...] + sb_ref[...]).astype(BF16)
        tril = _causal((CHUNK, CHUNK))
        lane = lax.broadcasted_iota(jnp.int32, (1, LANES), 1)
        for c in range(tm // CHUNK):
            rs = slice(c * CHUNK, (c + 1) * CHUNK)
            for gp in range(NP):
                ls = slice(gp * PAIR, (gp + 1) * PAIR)
                vpair = zv_s[rs, ls]
                mixed, w0, w1 = _sgu_mixed(ws_ref, bst_ref, gp, vpair)
                u_pre = z_s[rs, ls]
                dmix = dcat[rs, ls]
                dz_ref[rs, ls] = (dmix * mixed * _gelu_grad(u_pre)).astype(BF16)
                dmixed = dmix * _gelu(u_pre)
                first = _first_head()
                d0 = jnp.where(first, dmixed, 0.0)
                d1 = jnp.where(first, 0.0, dmixed)
                d0b = d0.astype(BF16)
                d1b = d1.astype(BF16)
                dzv_s[rs, ls] = _dot_tn(w0, d0b) + _dot_tn(w1, d1b)
                dws_ref[2 * gp] += jnp.where(tril, _dot_nt(d0b, vpair), 0.0)
                dws_ref[2 * gp + 1] += jnp.where(tril, _dot_nt(d1b, vpair), 0.0)
                dbst_ref[...] += (jnp.where(lane == 2 * gp, jnp.sum(d0, axis=-1, keepdims=True), 0.0)
                                  + jnp.where(lane == 2 * gp + 1, jnp.sum(d1, axis=-1, keepdims=True), 0.0))
        dzv = dzv_s[...]
        dsgb_ref[0:1, :] += jnp.sum(dzv * vhat, axis=0, keepdims=True)
        dsgb_ref[1:2, :] += jnp.sum(dzv, axis=0, keepdims=True)
        dgv = _ln_bwd(dzv, vhat, vrstd, sg_ref[...])
        dz_ref[:, DM:2 * DM] = (dgv * _gelu_grad(z_s[:, DM:2 * DM])).astype(BF16)
        dmo = [dcat[:, DM + mp * PAIR:DM + (mp + 1) * PAIR] for mp in range(DQ // PAIR)]
        dqm = _mem_attn_bwd(z_s[:, 2 * DM:].astype(BF16), mkv_ref, dmo, dmkv_ref, DQ)
        for mp in range(DQ // PAIR):
            dz_ref[:, 2 * DM + mp * PAIR:2 * DM + (mp + 1) * PAIR] = dqm[mp].astype(BF16)
        dx_ref[...] = ALPHA * dr + _dot_nt(dz_ref[...], wa_ref[...])

    ZW = 2 * DM + DQ
    M = mkv.shape[0]
    return _pcall(
        body, name=name, grid=(S // tm,), comm=comm,
        out_shape=(jax.ShapeDtypeStruct((S, D), F32), jax.ShapeDtypeStruct((S, ZW), BF16),
                   jax.ShapeDtypeStruct((S, D), BF16), jax.ShapeDtypeStruct((2, D), F32),
                   jax.ShapeDtypeStruct((2, DM), F32), jax.ShapeDtypeStruct((NG, CHUNK, CHUNK), F32),
                   jax.ShapeDtypeStruct((CHUNK, LANES), F32), jax.ShapeDtypeStruct((M, 2 * DQ), F32)),
        in_specs=[_rows(tm, ZW), _rows(tm, D), _rows(tm, D), _full(wa.shape), _full(sg.shape), _full(sb.shape),
                  _full(ws.shape), _full(bst.shape), _full(mkv.shape), _full(wo.shape), _full((1, D))],
        out_specs=(_rows(tm, D), _rows(tm, ZW), _rows(tm, D), _full((2, D)), _full((2, DM)),
                   _full((NG, CHUNK, CHUNK)), _full((CHUNK, LANES)), _full((M, 2 * DQ))),
        scratch_shapes=[pltpu.VMEM((tm, ZW), F32), pltpu.VMEM((tm, DM), BF16), pltpu.VMEM((tm, DM), F32)],
        args=(zb, dxo, r, wa, sg, sb, ws, bst, mkv, wo, g))


def _proj_b_fwd(xb, wq, wkv, dm, name):
    S, D = xb.shape
    DQ = D - dm
    tm = min(512, S)

    def body(x_ref, wq_ref, wkv_ref, q_ref, qm_ref, k_ref, v_ref, fl_ref):
        xv = x_ref[...]
        z = _dot(xv, wq_ref[...])
        q_ref[...] = z[:, :dm].astype(BF16)
        qm_ref[...] = z[:, dm:].astype(BF16)
        kvf = _dot(xv, wkv_ref[...])
        k_ref[...] = kvf[:, :dm].astype(BF16)
        v_ref[...] = kvf[:, dm:2 * dm].astype(BF16)
        fl_ref[...] = kvf[:, 2 * dm:]

    return pl.pallas_call(
        body, name=name, grid=(S // tm,),
        out_shape=(jax.ShapeDtypeStruct((S, dm), BF16), jax.ShapeDtypeStruct((S, DQ), BF16),
                   jax.ShapeDtypeStruct((S, dm), BF16), jax.ShapeDtypeStruct((S, dm), BF16),
                   jax.ShapeDtypeStruct((S, LANES), F32)),
        in_specs=[_rows(tm, D), _full(wq.shape), _full(wkv.shape)],
        out_specs=(_rows(tm, dm), _rows(tm, DQ), _rows(tm, dm), _rows(tm, dm), _rows(tm, LANES)),
        compiler_params=_params(),
    )(xb, wq, wkv)


def _proj_b_bwd(dr, dqz, dk, dv, dfl, wq, wkv, name):
    S, D = dr.shape
    dm = dk.shape[1]
    tm = min(512, S)

    def body(dr_ref, dqz_ref, dk_ref, dv_ref, dfl_ref, wq_ref, wkv_ref, dx_ref):
        dx_ref[...] = (ALPHA * dr_ref[...] + _dot_nt(dqz_ref[...], wq_ref[...])
                       + _dot_nt(dk_ref[...], wkv_ref[:, :dm]) + _dot_nt(dv_ref[...], wkv_ref[:, dm:2 * dm])
                       + _dot_nt(dfl_ref[...], wkv_ref[:, 2 * dm:]))

    return pl.pallas_call(
        body, name=name, grid=(S // tm,), out_shape=jax.ShapeDtypeStruct((S, D), F32),
        in_specs=[_rows(tm, D), _rows(tm, dqz.shape[1]), _rows(tm, dm), _rows(tm, dm), _rows(tm, dfl.shape[1]),
                  _full(wq.shape), _full(wkv.shape)],
        out_specs=_rows(tm, D), compiler_params=_params(),
    )(dr, dqz, dk, dv, dfl, wq, wkv)


def _mm_nn_cat(at, bs, name):
    M, R = at.shape
    widths = [b.shape[1] for b in bs]
    N = sum(widths)
    br = _pick(R, 1024)
    last = R // br - 1
    nb = len(bs)

    def body(a_ref, *refs):
        b_refs, o_ref, acc_ref = refs[:nb], refs[nb], refs[nb + 1]

        @pl.when(pl.program_id(0) == 0)
        def _():
            acc_ref[...] = jnp.zeros_like(acc_ref)

        a = a_ref[...]
        c0 = 0
        for b_ref, wd in zip(b_refs, widths):
            acc_ref[:, c0:c0 + wd] += _dot(a, b_ref[...])
            c0 += wd

        @pl.when(pl.program_id(0) == last)
        def _():
            o_ref[...] = acc_ref[...].astype(BF16)

    return pl.pallas_call(
        body, name=name, grid=(R // br,), out_shape=jax.ShapeDtypeStruct((M, N), BF16),
        in_specs=[pl.BlockSpec((M, br), lambda r: (0, r))] + [pl.BlockSpec((br, wd), lambda r: (r, 0)) for wd in widths],
        out_specs=_full((M, N)), scratch_shapes=[pltpu.VMEM((M, N), F32)], compiler_params=_params(),
    )(at, *bs)


def _split3(v):
    hi = v.astype(BF16)
    r1 = v - hi.astype(F32)
    mid = r1.astype(BF16)
    lo = (r1 - mid.astype(F32)).astype(BF16)
    return hi, mid, lo


def _tri_sum(v, tri_b):
    hi, mid, lo = _split3(v)
    return _dot(hi, tri_b) + _dot(mid, tri_b) + _dot(lo, tri_b)


def _log_sigmoid(x):
    return jnp.minimum(x, 0.0) - jnp.log(1.0 + jnp.exp(-jnp.abs(x)))


def _forget_cumsum(flt, bf, name):
    H, S = flt.shape
    nchunk = S // LANES

    def body(fl_ref, bf_ref, p_ref):
        upper = (lax.broadcasted_iota(jnp.int32, (LANES, LANES), 0)
                 <= lax.broadcasted_iota(jnp.int32, (LANES, LANES), 1)).astype(BF16)

        def step(n, carry):
            s0 = pl.multiple_of(n * LANES, LANES)
            lf = _log_sigmoid(fl_ref[:, pl.ds(s0, LANES)] + bf_ref[...])
            cs = _tri_sum(lf, upper) + carry
            for i, piece in enumerate(_split3(-cs)):
                p_ref[i, :, pl.ds(s0, LANES)] = piece.astype(F32)
            return cs[:, LANES - 1:LANES]

        lax.fori_loop(0, nchunk, step, jnp.zeros((H, 1), F32))

    return pl.pallas_call(body, name=name, out_shape=jax.ShapeDtypeStruct((3, H, S), F32),
                          compiler_params=pltpu.CompilerParams(vmem_limit_bytes=VMEM_LIMIT))(flt, bf)


def _forget_cumsum_bwd(dct, flt, bf, name):
    H, S = flt.shape
    nchunk = S // LANES

    def body(dc_ref, fl_ref, bf_ref, dfl_ref, dbf_ref):
        lower = (lax.broadcasted_iota(jnp.int32, (LANES, LANES), 0)
                 >= lax.broadcasted_iota(jnp.int32, (LANES, LANES), 1)).astype(BF16)

        def step(n, carry):
            tail, tot = carry
            s0 = pl.multiple_of((nchunk - 1 - n) * LANES, LANES)
            suffix = _tri_sum(dc_ref[:, pl.ds(s0, LANES)], lower) + tail
            xv = fl_ref[:, pl.ds(s0, LANES)] + bf_ref[...]
            dfl = suffix * (1.0 / (1.0 + jnp.exp(xv)))
            dfl_ref[:, pl.ds(s0, LANES)] = dfl
            return suffix[:, 0:1], tot + jnp.sum(dfl, axis=-1, keepdims=True)

        _, tot = lax.fori_loop(0, nchunk, step, (jnp.zeros((H, 1), F32), jnp.zeros((H, 1), F32)))
        dbf_ref[...] = jnp.broadcast_to(tot, (H, LANES))

    return pl.pallas_call(body, name=name,
                          out_shape=(jax.ShapeDtypeStruct((H, S), F32), jax.ShapeDtypeStruct((H, LANES), F32)),
                          compiler_params=pltpu.CompilerParams(vmem_limit_bytes=VMEM_LIMIT))(dct, flt, bf)


N_BIAS = 3


def _bias_ones():
    lane = lax.broadcasted_iota(jnp.int32, (1, PAIR), 1)
    return ((lane & (HEAD_DIM - 1)) < N_BIAS).astype(BF16)


def _fox_fwd(q, k, v, ca, name, comm=None):
    S, DM = q.shape
    NP = DM // PAIR
    T = min(FOX_BLOCK, S)
    nq = S // T

    def body(q_ref, k_ref, v_ref, ca_ref, o_ref, lse_ref):
        tril = _causal((T, T))
        heads = (_lane_mask(0), _lane_mask(1))
        ones3 = _bias_ones()

        def kv_block(kj):
            s0 = pl.multiple_of(kj * T, T)
            kb, vb, cab = k_ref[pl.ds(s0, T), :], v_ref[pl.ds(s0, T), :], ca_ref[pl.ds(s0, T), :]
            return ([jnp.where(heads[e], kb, cab) for e in (0, 1)],
                    [jnp.where(heads[e], vb, jnp.ones_like(vb)) for e in (0, 1)])

        def q_pair(a, _):
            t0s = [pl.multiple_of((2 * a + r) * T, T) for r in (0, 1)]
            qaug = {}
            for r in (0, 1):
                qb = q_ref[pl.ds(t0s[r], T), :] * QK_SCALE
                for e in (0, 1):
                    qaug[r, e] = jnp.where(heads[e], qb, ones3)
            chains = [(r, e) for r in (0, 1) for e in (0, 1)]

            def update(carries, first_block, seen):
                blocks = [kv_block(first_block + kk) for kk in range(1 + max(kk for s in seen for kk, _ in s))]
                out = []
                scores = [[_dot_nt(qaug[r, e], blocks[kk][0][e]) for kk, _ in seen[r]] for r, e in chains]
                for n, (r, e) in enumerate(chains):
                    m, acc = carries[n]
                    tiles = [jnp.where(tril, s, NEG) if masked else s for s, (_, masked) in zip(scores[n], seen[r])]
                    m_new = jnp.maximum(m, jnp.max(functools.reduce(jnp.maximum, tiles), axis=-1, keepdims=True))
                    acc = jnp.exp(m - m_new) * acc
                    for s, (kk, _) in zip(tiles, seen[r]):
                        acc = acc + _dot(jnp.exp(s - m_new).astype(BF16), blocks[kk][1][e])
                    out.append((m_new, acc))
                return tuple(out)

            def full(first_block, n_blocks, carries):
                return update(carries, first_block, [[(kk, False) for kk in range(n_blocks)]] * 2)

            init = (jnp.full((T, 1), NEG, F32), jnp.zeros((T, PAIR), F32))
            carries = lax.fori_loop(0, a // 2, lambda i, c: full(4 * i, 4, c), (init,) * 4)
            carries = lax.fori_loop(0, a % 2, lambda i, c: full(2 * a - 2, 2, c), carries)
            carries = update(carries, 2 * a, [[(0, True)], [(0, False), (1, True)]])
            first = _first_head()
            for r in (0, 1):
                (m0, acc0), (m1, acc1) = carries[2 * r], carries[2 * r + 1]
                l0, l1 = acc0[:, HEAD_DIM:HEAD_DIM + 1], acc1[:, 0:1]
                o_ref[pl.ds(t0s[r], T), :] = jnp.where(first, acc0 / l0, acc1 / l1).astype(BF16)
                lse_t = jnp.where(first, m0 + jnp.log(l0), m1 + jnp.log(l1)).T
                lse_ref[0:1, pl.ds(t0s[r], T)] = lse_t[0:1, :]
                lse_ref[1:2, pl.ds(t0s[r], T)] = lse_t[HEAD_DIM:HEAD_DIM + 1, :]
            return 0

        lax.fori_loop(0, nq // 2, q_pair, 0)

    col = pl.BlockSpec((S, PAIR), lambda hp: (0, hp))
    return _pcall(
        body, name=name, grid=(NP,), comm=comm,
        out_shape=(jax.ShapeDtypeStruct((S, DM), BF16), jax.ShapeDtypeStruct((NP, 2, S), F32)),
        in_specs=[col, col, col, col], out_specs=(col, pl.BlockSpec((None, 2, S), lambda hp: (hp, 0, 0))),
        args=(q, k, v, ca))


def _fox_bwd(q, k, v, o, do, ca, lse, name, comm=None):
    S, DM = q.shape
    NP = DM // PAIR
    T = min(FOX_BLOCK, S)
    nq = S // T

    def body(q_ref, k_ref, v_ref, o_ref, do_ref, ca_ref, lse_ref, dq_ref, dk_ref, dv_ref, dc_ref,
             dq_acc, dcol_acc, dk_acc, dv_acc, dd_s):
        keep = lax.broadcasted_iota(jnp.int32, (T, T), 0) <= lax.broadcasted_iota(jnp.int32, (T, T), 1)
        lane = lax.broadcasted_iota(jnp.int32, (1, LANES), 1)
        heads = (_lane_mask(0), _lane_mask(1))
        ones3 = _bias_ones()
        dq_acc[...] = jnp.zeros_like(dq_acc)
        dcol_acc[...] = jnp.zeros_like(dcol_acc)
        dc_ref[...] = jnp.zeros_like(dc_ref)

        def row_dots(qi, _):
            t0 = pl.multiple_of(qi * T, T)
            prod = do_ref[pl.ds(t0, T), :].astype(F32) * o_ref[pl.ds(t0, T), :].astype(F32)
            cols = [jnp.sum(jnp.where(heads[e], prod, 0.0), axis=-1, keepdims=True) for e in (0, 1)]
            tile = jnp.where(lane == 0, cols[0], jnp.where(lane == 1, cols[1], 0.0))
            dd_s[:, pl.ds(t0, T)] = tile.T[0:8, :]
            return 0

        lax.fori_loop(0, nq, row_dots, 0)

        def kv_pair(b, _):
            dk_acc[...] = jnp.zeros_like(dk_acc)
            dv_acc[...] = jnp.zeros_like(dv_acc)
            s0s = [pl.multiple_of((2 * b + jj) * T, T) for jj in (0, 1)]
            kaug, ks, vbs = {}, {}, []
            for jj in (0, 1):
                kb = k_ref[pl.ds(s0s[jj], T), :]
                cab = ca_ref[pl.ds(s0s[jj], T), :]
                vbs.append(v_ref[pl.ds(s0s[jj], T), :])
                for e in (0, 1):
                    kaug[jj, e] = jnp.where(heads[e], kb, cab)
                    ks[jj, e] = jnp.where(heads[e], kb, jnp.zeros_like(kb)) * QK_SCALE

            def q_step(qi, blocks):
                t0 = pl.multiple_of(qi * T, T)
                qs = q_ref[pl.ds(t0, T), :] * QK_SCALE
                dob = do_ref[pl.ds(t0, T), :]
                doh = [jnp.where(heads[e], dob, jnp.zeros_like(dob)) for e in (0, 1)]
                qh = [jnp.where(heads[e], qs, jnp.zeros_like(qs)) for e in (0, 1)]
                chains = [(e, jj, masked) for e in (0, 1) for jj, masked in blocks]
                scores = [_dot_nt(kaug[jj, e], jnp.where(heads[e], qs, ones3)) for e, jj, _ in chains]
                dps = [_dot_nt(vbs[jj], doh[e]) for e, jj, _ in chains]
                dq = jnp.zeros((T, PAIR), F32)
                drow = [jnp.zeros((1, T), F32), jnp.zeros((1, T), F32)]
                for n, (e, jj, masked) in enumerate(chains):
                    st = jnp.where(keep, scores[n], NEG) if masked else scores[n]
                    pt = jnp.exp(st - lse_ref[e:e + 1, pl.ds(t0, T)])
                    dv_acc[jj] += _dot(pt.astype(BF16), doh[e])
                    dst = pt * (dps[n] - dd_s[e:e + 1, pl.ds(t0, T)])
                    dsb = dst.astype(BF16)
                    dk_acc[jj] += _dot(dsb, qh[e])
                    dq = dq + _dot_tn(dsb, ks[jj, e])
                    dcol_acc[pl.ds(s0s[jj], T), :] -= jnp.where(lane == e, jnp.sum(dst, axis=-1, keepdims=True), 0.0)
                    drow[e] = drow[e] + jnp.sum(dst, axis=0, keepdims=True)
                for e in (0, 1):
                    dc_ref[e:e + 1, pl.ds(t0, T)] += drow[e]
                dq_acc[pl.ds(t0, T), :] += dq

            q_step(2 * b, [(0, True)])
            q_step(2 * b + 1, [(0, False), (1, True)])

            def rest(i, _):
                q_step(2 * b + 2 + 2 * i, [(0, False), (1, False)])
                q_step(2 * b + 3 + 2 * i, [(0, False), (1, False)])
                return 0

            lax.fori_loop(0, (nq - 2) // 2 - b, rest, 0)
            for jj in (0, 1):
                dk_ref[pl.ds(s0s[jj], T), :] = dk_acc[jj].astype(BF16)
                dv_ref[pl.ds(s0s[jj], T), :] = dv_acc[jj].astype(BF16)
            return 0

        lax.fori_loop(0, nq // 2, kv_pair, 0)
        dq_ref[...] = dq_acc[...].astype(BF16)

        def add_cols(kj, _):
            s0 = pl.multiple_of(kj * T, T)
            dc_ref[:, pl.ds(s0, T)] += dcol_acc[pl.ds(s0, T), :].T[0:2, :]
            return 0

        lax.fori_loop(0, nq, add_cols, 0)

    col = pl.BlockSpec((S, PAIR), lambda hp: (0, hp))
    row2 = pl.BlockSpec((None, 2, S), lambda hp: (hp, 0, 0))
    return _pcall(
        body, name=name, grid=(NP,), comm=comm,
        out_shape=(jax.ShapeDtypeStruct(do.shape, BF16), jax.ShapeDtypeStruct((S, DM), BF16),
                   jax.ShapeDtypeStruct((S, DM), BF16), jax.ShapeDtypeStruct((NP, 2, S), F32)),
        in_specs=[col, col, col, col, col, col, row2], out_specs=(col, col, col, row2),
        scratch_shapes=[pltpu.VMEM((S, PAIR), F32), pltpu.VMEM((S, LANES), F32), pltpu.VMEM((2, T, PAIR), F32),
                        pltpu.VMEM((2, T, PAIR), F32), pltpu.VMEM((8, S), F32)],
        args=(q, k, v, o, do, ca, lse), aliases={4: 0})


def _mixer_b_fwd(x, o, qm, mkv, wo, g, b, name):
    S, D = x.shape
    DQ = qm.shape[1]
    DM = D - DQ
    tm = min(512, S)

    def body(x_ref, o_ref, qm_ref, mkv_ref, wo_ref, g_ref, b_ref, xo_ref, xbt_ref, r_ref, catt_ref, cat_ref):
        cat_ref[:, :DM] = o_ref[...]
        mo = _mem_attn_fwd(qm_ref[...], mkv_ref, DQ)
        for mp in range(DQ // PAIR):
            cat_ref[:, DM + mp * PAIR:DM + (mp + 1) * PAIR] = mo[mp].astype(BF16)
        _tail_fwd_store(x_ref[...], cat_ref, wo_ref, g_ref, b_ref, xo_ref, xbt_ref, r_ref, catt_ref)

    return pl.pallas_call(
        body, name=name, grid=(S // tm,),
        out_shape=(jax.ShapeDtypeStruct((S, D), F32), jax.ShapeDtypeStruct((D, S), BF16),
                   jax.ShapeDtypeStruct((S, D), F32), jax.ShapeDtypeStruct((D, S), BF16)),
        in_specs=[_rows(tm, D), _rows(tm, DM), _rows(tm, DQ), _full(mkv.shape), _full(wo.shape), _full((1, D)),
                  _full((1, D))],
        out_specs=(_rows(tm, D), _cols(D, tm), _rows(tm, D), _cols(D, tm)),
        scratch_shapes=[pltpu.VMEM((tm, D), BF16)],
        compiler_params=_params(),
    )(x, o, qm, mkv, wo, g, b)


def _mixer_b_bwd(dxo, r, qm, mkv, wo, g, name, comm=None):
    S, D = r.shape
    DQ = qm.shape[1]
    DM = D - DQ
    M = mkv.shape[0]
    tm = min(512, S)

    def body(dxo_ref, r_ref, qm_ref, mkv_ref, wo_ref, g_ref, dr_ref, drb_ref, dcat_ref, dgb_ref, dmkv_ref):
        @pl.when(pl.program_id(0) == 0)
        def _():
            dgb_ref[...] = jnp.zeros_like(dgb_ref)
            dmkv_ref[...] = jnp.zeros_like(dmkv_ref)

        dr, dcat = _tail_bwd_head(dxo_ref, r_ref, g_ref, wo_ref, dgb_ref, drb_ref)
        dr_ref[...] = dr
        dcat_ref[:, :DM] = dcat[:, :DM].astype(BF16)
        dmo = [dcat[:, DM + mp * PAIR:DM + (mp + 1) * PAIR] for mp in range(DQ // PAIR)]
        dqm = _mem_attn_bwd(qm_ref[...], mkv_ref, dmo, dmkv_ref, DQ)
        for mp in range(DQ // PAIR):
            dcat_ref[:, DM + mp * PAIR:DM + (mp + 1) * PAIR] = dqm[mp].astype(BF16)

    return _pcall(
        body, name=name, grid=(S // tm,), comm=comm,
        out_shape=(jax.ShapeDtypeStruct((S, D), F32), jax.ShapeDtypeStruct((S, D), BF16),
                   jax.ShapeDtypeStruct((S, D), BF16), jax.ShapeDtypeStruct((2, D), F32),
                   jax.ShapeDtypeStruct((M, 2 * DQ), F32)),
        in_specs=[_rows(tm, D), _rows(tm, D), _rows(tm, DQ), _full(mkv.shape), _full(wo.shape), _full((1, D))],
        out_specs=(_rows(tm, D), _rows(tm, D), _rows(tm, D), _full((2, D)), _full((M, 2 * DQ))),
        args=(dxo, r, qm, mkv, wo, g))


class _NoExchange:
    def plan(self, stage, w, grads):
        return None

    def done(self, stage, results, w):
        pass


def _local_step(x, mem, target, w, hooks):
    S, D = x.shape
    DQ = w["mkv0"].shape[1] // 2
    DM = D - DQ
    NG = DM // HEAD_DIM
    NP = DM // PAIR
    ln_g, ln_b = w["ln_g"], w["ln_b"]
    row = lambda a, i: a[i:i + 1]
    memb = mem.astype(BF16)
    bst =jnp.pad(w["b_s"].T, ((0, 0), (0, LANES - NG)))
    grads = {}

    def staged(stage, fn, *args):
        out, got = fn(*args, stage, comm=hooks.plan(stage, w, grads))
        hooks.done(stage, got, w)
        return out

    mkv0 = _mm_small(memb, w["mkv0"], "mem_kv_0")
    x1, x1t, r1, cat0t, x0t, z0b = staged("mixer_a_fwd", _mixer_a_fwd, x, w["wa"], w["sg"], w["sb"], w["ws"], bst, mkv0,
                                w["wo0"], row(ln_g, 0), row(ln_b, 0))
    x2, x2b, x2t, r2, h0 = staged("mlp_fwd_0", _mlp_fwd, x1, w["wup0"], w["wdown0"], row(ln_g, 1), row(ln_b, 1))
    q, qm, k, v, fl = _proj_b_fwd(x2b, w["wq"], w["wkv"], DM, "proj_b_fwd")
    flt = fl[:, :16].T
    bfc = jnp.pad(w["bf"], (0, 16 - NG)).reshape(16, 1)
    pieces = _forget_cumsum(flt, bfc, "forget_cumsum")
    pieces = jnp.pad(pieces[:, :NG].transpose(2, 1, 0), ((0, 0), (0, 0), (0, HEAD_DIM - N_BIAS)))
    ca = pieces.reshape(S, NP, 2, HEAD_DIM)[:, :, ::-1].reshape(S, DM).astype(BF16)
    o, lse = staged("fox_fwd", _fox_fwd, q, k, v, ca)
    mkv1 = _mm_small(memb, w["mkv1"], "mem_kv_1")
    x3, x3t, r3, cat1t = _mixer_b_fwd(x2, o, qm, mkv1, w["wo1"], row(ln_g, 2), row(ln_b, 2), "mixer_b_fwd")
    dx4, r4, h1, loss_part = _mlp_fwd_loss(x3, w["wup1"], w["wdown1"], row(ln_g, 3), row(ln_b, 3), target,
                                           "mlp_fwd_1")

    nc = w["wup0"].shape[0]
    (dx3, dr4t, dpre1, dgb11), _ = _mlp_bwd(dx4, r4, h1, w["wup1"], w["wdown1"], row(ln_g, 3), "mlp_bwd_1")
    grads["wdown1"] = _mm_nn(dr4t, h1, "dw_down_1", out_split=nc)
    grads["wup1"] = _mm_nn(x3t, dpre1, "dw_up_1", out_split=nc)
    dr3, dr3b, dcat1, dgb10, dmkv1 = staged("mixer_b_bwd", _mixer_b_bwd, dx3, r3, qm, mkv1, w["wo1"], row(ln_g, 2))
    dqz, dk, dv, dct3 = staged("fox_bwd", _fox_bwd, q, k, v, o, dcat1, ca, lse)
    dct = jnp.pad(dct3.reshape(NG, S), ((0, 16 - NG), (0, 0)))
    dflt, dbf = _forget_cumsum_bwd(dct, flt, bfc, "forget_cumsum_bwd")
    dfl = jnp.pad(dflt.T, ((0, 0), (0, LANES - 16))).astype(BF16)
    dx2 = _proj_b_bwd(dr3, dqz, dk, dv, dfl, w["wq"], w["wkv"], "proj_b_bwd")
    grads["wo1"] = _mm_nn(cat1t, dr3b, "dw_o_1")
    grads["wq"] = _mm_nn(x2t, dqz, "dw_q")
    grads["wkv"] = _mm_nn_cat(x2t, [dk, dv, dfl], "dw_kv")
    grads["mkv1"] = _mm_tn(memb, dmkv1.astype(BF16), "dw_mkv_1")
    dx1, dr2t, dpre0, dgb01 = staged("mlp_bwd_0", _mlp_bwd, dx2, r2, h0, w["wup0"], w["wdown0"], row(ln_g, 1))
    grads["wdown0"] = _mm_nn(dr2t, h0, "dw_down_0", out_split=nc)
    grads["wup0"] = _mm_nn(x1t, dpre0, "dw_up_0", out_split=nc)
    dx0, dz, dr1b, dgb00, dsgb, dws, dbst, dmkv0 = staged(
        "mixer_a_bwd", _mixer_a_bwd, z0b, dx1, r1, w["wa"], w["sg"], w["sb"], w["ws"], bst, mkv0, w["wo0"], row(ln_g, 0))
    grads["wo0"] = _mm_nn(cat0t, dr1b, "dw_o_0")
    grads["wa"] = _mm_nn(x0t, dz, "dw_a")
    grads["mkv0"] = _mm_tn(memb, dmkv0.astype(BF16), "dw_mkv_0")
    grads.update({
        "ws": dws, "b_s": dbst[:, :NG].T, "sg": dsgb[0:1], "sb": dsgb[1:2], "bf": dbf[:NG, 0], "loss": loss_part[0:1],
        "ln_g": jnp.concatenate([dgb00[0:1], dgb01[0:1], dgb10[0:1], dgb11[0:1]], axis=0),
        "ln_b": jnp.concatenate([dgb00[1:2], dgb01[1:2], dgb10[1:2], dgb11[1:2]], axis=0),
    })
    return loss_part, dx0, grads


def _place():
    xi, yi, ci = lax.axis_index("x"), lax.axis_index("y"), lax.axis_index("c")
    peers = [(xi, 1 - yi), (1 - xi, yi), (1 - xi, 1 - yi)]
    return xi, yi, ci, peers


def _comm_call(body, name, ins, out_shapes, n_remote, n_local):
    return pl.pallas_call(
        body, name=name, out_shape=out_shapes, in_specs=[ANY] * len(ins), out_specs=[ANY] * len(out_shapes),
        scratch_shapes=[pltpu.SemaphoreType.DMA((n_remote,)), pltpu.SemaphoreType.DMA((n_remote,)),
                        pltpu.SemaphoreType.DMA((max(n_local, 1),))],
        compiler_params=pltpu.CompilerParams(has_side_effects=True),
    )(*ins)


def _pair_swap(xs):
    n = len(xs)
    out_shapes = [jax.ShapeDtypeStruct((a.shape[0],) + a.shape[2:], a.dtype) for a in xs]

    def copies(x_refs, o_refs, sems):
        send_sems, recv_sems = sems
        xi, yi, ci, _ = _place()
        return [pltpu.make_async_remote_copy(src_ref=x_refs[i].at[:, 1 - ci], dst_ref=o_refs[i],
                                             send_sem=send_sems.at[i], recv_sem=recv_sems.at[i],
                                             device_id=(xi, yi, 1 - ci), device_id_type=MESH) for i in range(n)]

    def start(x_refs, o_refs, sems):
        for cp in copies(x_refs, o_refs, sems):
            cp.start()

    def finish(x_refs, o_refs, sems):
        cps = copies(x_refs, o_refs, sems)
        for cp in cps:
            cp.wait_recv()
        for cp in cps:
            cp.wait_send()

    return _Comm(xs, out_shapes, {}, [n, n], start, finish)


def _chip_exchange(xs):
    n = len(xs)
    out_shapes = [jax.ShapeDtypeStruct((3,) + a.shape[1:], a.dtype) for a in xs]

    def copies(x_refs, o_refs, sems):
        send_sems, recv_sems = sems
        xi, yi, ci, peers = _place()
        return [pltpu.make_async_remote_copy(src_ref=x_refs[i].at[2 * px + py], dst_ref=o_refs[i].at[d],
                                             send_sem=send_sems.at[3 * i + d], recv_sem=recv_sems.at[3 * i + d],
                                             device_id=(px, py, ci), device_id_type=MESH)
                for i in range(n) for d, (px, py) in enumerate(peers)]

    def start(x_refs, o_refs, sems):
        for cp in copies(x_refs, o_refs, sems):
            cp.start()

    def finish(x_refs, o_refs, sems):
        cps = copies(x_refs, o_refs, sems)
        for cp in cps:
            cp.wait_recv()
        for cp in cps:
            cp.wait_send()

    return _Comm(xs, out_shapes, {}, [3 * n, 3 * n], start, finish)


def _inplace_call(body, name, bufs, sem_counts):
    n = len(bufs)
    return pl.pallas_call(
        body, name=name, out_shape=[jax.ShapeDtypeStruct(a.shape, a.dtype) for a in bufs],
        in_specs=[ANY] * n, out_specs=[ANY] * n, input_output_aliases={i: i for i in range(n)},
        scratch_shapes=[pltpu.SemaphoreType.DMA((k,)) for k in sem_counts],
        compiler_params=pltpu.CompilerParams(has_side_effects=True),
    )(*bufs)


def _gather_weights(bufs, whole=(), early=None):
    n, nw = len(bufs), len(whole)

    def copy(ref, s, r, k, dev):
        return pltpu.make_async_remote_copy(src_ref=ref, dst_ref=ref, send_sem=s.at[k], recv_sem=r.at[k],
                                            device_id=dev, device_id_type=MESH)

    def plan(w, sems):
        ici_s, ici_r, d2d_s, d2d_r = sems
        xi, yi, ci, peers = _place()
        xn, yn, sib = (1 - xi, yi, ci), (xi, 1 - yi, ci), (xi, yi, 1 - ci)
        j, jx, jy, jd = 2 * xi + yi, 2 * (1 - xi) + yi, 2 * xi + 1 - yi, 2 * (1 - xi) + 1 - yi
        c = {}
        for i in range(n):
            hh = bufs[i].shape[2] // 2
            lo, hi = pl.ds(0, hh), pl.ds(hh, hh)
            at = lambda slot, half, rows=None, i=i: w[i].at[slot, half] if rows is None else w[i].at[slot, half, rows]
            c["to_x", i] = (copy(at(j, ci), ici_s, ici_r, 4 * i, xn), copy(at(jx, ci), ici_s, ici_r, 4 * i, xn))
            c["to_y", i] = (copy(at(j, ci), ici_s, ici_r, 4 * i + 1, yn), copy(at(jy, ci), ici_s, ici_r, 4 * i + 1, yn))
            c["x_on", i] = (copy(at(jx, ci, lo), ici_s, ici_r, 4 * i + 2, yn), copy(at(jd, ci, lo), ici_s, ici_r, 4 * i + 2, yn))
            c["y_on", i] = (copy(at(jy, ci, hi), ici_s, ici_r, 4 * i + 3, xn), copy(at(jd, ci, hi), ici_s, ici_r, 4 * i + 3, xn))
            for k, slot in enumerate((jx, jy, jd)):
                c["sib", i, k] = (copy(at(slot, ci), d2d_s, d2d_r, 3 * i + k, sib),
                                  copy(at(slot, 1 - ci), d2d_s, d2d_r, 3 * i + k, sib))
        for i in range(nw):
            for d, (px, py) in enumerate(peers):
                k = 4 * n + 3 * i + d
                c["whole", i, d] = (copy(w[n + i].at[j], ici_s, ici_r, k, (px, py, ci)),
                                    copy(w[n + i].at[2 * px + py], ici_s, ici_r, k, (px, py, ci)))
        return c

    def start(_, w, sems):
        c = plan(w, sems)
        for i in range(n):
            c["to_x", i][0].start()
            c["to_y", i][0].start()
        for i in range(nw):
            for d in range(3):
                c["whole", i, d][0].start()

    def pass_on(c, which):
        for i in which:
            c["to_x", i][1].wait_recv()
            c["x_on", i][0].start()
            c["sib", i, 0][0].start()
            c["to_y", i][1].wait_recv()
            c["y_on", i][0].start()
            c["sib", i, 1][0].start()

    n_early = n if early is None else min(early, n)

    def middle(_, w, sems):
        pass_on(plan(w, sems), range(n_early))

    def finish(_, w, sems):
        c = plan(w, sems)
        pass_on(c, range(n_early, n))
        for i in range(n):
            c["x_on", i][1].wait_recv()
            c["y_on", i][1].wait_recv()
            c["sib", i, 2][0].start()
        for i in range(n):
            for k in range(3):
                c["sib", i, k][1].wait_recv()
        for i in range(nw):
            for d in range(3):
                c["whole", i, d][1].wait_recv()
        for key, (sent, _) in c.items():
            sent.wait_send()

    arrays = list(bufs) + list(whole)
    k = 4 * n + 3 * nw
    return _Comm(arrays, [jax.ShapeDtypeStruct(a.shape, a.dtype) for a in arrays], {i: i for i in range(n + nw)},
                 [k, k, max(3 * n, 1), max(3 * n, 1)], start, finish, middle)


def _pair_gather(bufs, spread, name):
    n = len(bufs)
    where = [(i, l) for i, a in enumerate(bufs) for l in range(a.shape[0])]
    nw = len(where)
    arrays = list(bufs) + ([] if spread is None else [spread])
    na = len(arrays)

    def body(*refs):
        g = refs[na:na + n]
        sp = refs[na + n] if spread is not None else None
        send_sems, recv_sems = refs[2 * na:]
        xi, yi, ci, peers = _place()
        chips = [(xi, yi)] + peers
        others = [(d, f) for d in range(4) for f in (0, 1) if (d, f) != (0, 0)] if spread is not None else []
        sends = []
        for k, (i, l) in enumerate(where):
            mine = g[i].at[l, :, ci]
            cp = pltpu.make_async_remote_copy(src_ref=mine, dst_ref=mine, send_sem=send_sems.at[k],
                                              recv_sem=recv_sems.at[k], device_id=(xi, yi, 1 - ci), device_id_type=MESH)
            cp.start()
            sends.append(cp)
        for k, (d, f) in enumerate(others):
            mine = sp.at[2 * xi + yi, ci]
            cp = pltpu.make_async_remote_copy(src_ref=mine, dst_ref=mine, send_sem=send_sems.at[nw + k],
                                              recv_sem=recv_sems.at[nw + k],
                                              device_id=(chips[d][0], chips[d][1], ci if f == 0 else 1 - ci),
                                              device_id_type=MESH)
            cp.start()
            sends.append(cp)
        for k, (i, l) in enumerate(where):
            other = g[i].at[l, :, 1 - ci]
            pltpu.make_async_remote_copy(src_ref=other, dst_ref=other, send_sem=send_sems.at[k],
                                         recv_sem=recv_sems.at[k], device_id=(xi, yi, 1 - ci),
                                         device_id_type=MESH).wait_recv()
        for k, (d, f) in enumerate(others):
            px, py, pc = chips[d][0], chips[d][1], (ci if f == 0 else 1 - ci)
            theirs = sp.at[2 * px + py, pc]
            pltpu.make_async_remote_copy(src_ref=theirs, dst_ref=theirs, send_sem=send_sems.at[nw + k],
                                         recv_sem=recv_sems.at[nw + k], device_id=(px, py, pc),
                                         device_id_type=MESH).wait_recv()
        for cp in sends:
            cp.wait_send()

    n_sems = nw + (7 if spread is not None else 0)
    out = _inplace_call(body, name, arrays, [n_sems, n_sems])
    return out[:n], (out[n] if spread is not None else None)


def _row_block(rows, cols):
    want = max(8, (2 ** 18 // max(cols, 1)) // 8 * 8)
    if rows <= want:
        return rows
    best = 8
    for t in range(8, want + 1, 8):
        if rows % t == 0:
            best = t
    return best


def _pair_add(g4, recv, sel, out_dtype, name):
    A, _, H, C = g4.shape
    bh = _row_block(H, C)

    def body(sel_ref, g_ref, r_ref, o_ref):
        o_ref[...] = (g_ref[...].astype(F32) + r_ref[...].astype(F32)).astype(out_dtype)

    return pl.pallas_call(
        body, name=name, out_shape=jax.ShapeDtypeStruct((A, H, C), out_dtype),
        grid_spec=pltpu.PrefetchScalarGridSpec(
            num_scalar_prefetch=1, grid=(A, H // bh),
            in_specs=[pl.BlockSpec((None, None, bh, C), lambda a, i, s: (a, s[1], i, 0)),
                      pl.BlockSpec((None, bh, C), lambda a, i, s: (a, i, 0))],
            out_specs=pl.BlockSpec((None, bh, C), lambda a, i, s: (a, i, 0))),
        compiler_params=pltpu.CompilerParams(dimension_semantics=("arbitrary", "arbitrary"),
                                             vmem_limit_bytes=VMEM_LIMIT),
    )(sel, g4, recv)


def _chip_reduce(g4, recv1, recv2, sel, buf, layer, n_layers, name, by_chip=False):
    _, _, H, C = g4.shape
    bh = _row_block(H, C)
    if by_chip:
        out_shape = jax.ShapeDtypeStruct((N_CHIPS, 2, H, C), F32)
        out_spec = pl.BlockSpec((None, None, bh, C), lambda i, s: (s[0], s[1], i, 0))
    else:
        out_shape = jax.ShapeDtypeStruct((n_layers, 1, 2, H, C), F32)
        out_spec = pl.BlockSpec((None, None, None, bh, C), lambda i, s: (layer, 0, s[1], i, 0))

    def body(sel_ref, g_ref, r1_ref, r2_ref, *rest):
        acc = g_ref[...].astype(F32) + r1_ref[...].astype(F32)
        for d in range(3):
            acc = acc + r2_ref[d].astype(F32)
        rest[-1][...] = acc

    in_specs = [pl.BlockSpec((None, None, bh, C), lambda i, s: (s[0], s[1], i, 0)),
                pl.BlockSpec((None, bh, C), lambda i, s: (s[0], i, 0)),
                pl.BlockSpec((3, bh, C), lambda i, s: (0, i, 0))]
    args = [sel, g4, recv1, recv2]
    aliases = {}
    if buf is not None:
        in_specs.append(ANY)
        args.append(buf)
        aliases = {4: 0}
    return pl.pallas_call(
        body, name=name, out_shape=out_shape,
        grid_spec=pltpu.PrefetchScalarGridSpec(num_scalar_prefetch=1, grid=(H // bh,), in_specs=in_specs,
                                               out_specs=out_spec),
        input_output_aliases=aliases,
        compiler_params=pltpu.CompilerParams(dimension_semantics=("arbitrary",), vmem_limit_bytes=VMEM_LIMIT),
    )(*args)


def _chip_reduce_many(groups, sel, comm, name):
    flat = [t for grp in groups for t in grp]
    n, ng = len(flat), len(groups)
    _, _, H, C = flat[0][0].shape
    bh = _row_block(H, C)
    last = H // bh - 1
    nci, nco = len(comm.ins), len(comm.out_shapes)

    def body(sel_ref, *refs):
        ins, refs = refs[:3 * n], refs[3 * n:]
        cins, refs = refs[:nci], refs[nci:]
        outs, refs = refs[:ng], refs[ng:]
        couts, sems = refs[:nco], refs[nco:]

        @pl.when(pl.program_id(0) == 0)
        def _():
            comm.start(cins, couts, sems)

        k = 0
        for gi, grp in enumerate(groups):
            for layer in range(len(grp)):
                g_ref, r1_ref, r2_ref = ins[3 * k:3 * k + 3]
                acc = g_ref[...].astype(F32) + r1_ref[...].astype(F32)
                for d in range(3):
                    acc = acc + r2_ref[d].astype(F32)
                outs[gi][layer] = acc
                k += 1

        @pl.when(pl.program_id(0) == last)
        def _():
            if comm.middle is not None:
                comm.middle(cins, couts, sems)
            comm.finish(cins, couts, sems)

    in_specs = [pl.BlockSpec((None, None, bh, C), lambda i, s: (s[0], s[1], i, 0)),
                pl.BlockSpec((None, bh, C), lambda i, s: (s[0], i, 0)),
                pl.BlockSpec((3, bh, C), lambda i, s: (0, i, 0))] * n + [ANY] * nci
    out_specs = [pl.BlockSpec((len(grp), None, None, bh, C), lambda i, s: (0, 0, s[1], i, 0)) for grp in groups]
    res = pl.pallas_call(
        body, name=name,
        out_shape=[jax.ShapeDtypeStruct((len(grp), 1, 2, H, C), F32) for grp in groups] + list(comm.out_shapes),
        grid_spec=pltpu.PrefetchScalarGridSpec(
            num_scalar_prefetch=1, grid=(H // bh,), in_specs=in_specs, out_specs=out_specs + [ANY] * nco,
            scratch_shapes=[pltpu.SemaphoreType.DMA((k,)) for k in comm.sem_counts]),
        input_output_aliases={1 + 3 * n + a: ng + b for a, b in comm.aliases.items()},
        compiler_params=pltpu.CompilerParams(dimension_semantics=("arbitrary",), vmem_limit_bytes=VMEM_LIMIT,
                                             has_side_effects=True),
    )(sel, *[a for t in flat for a in t], *comm.ins)
    return res[:ng], res[ng:]


def _cast_place(shards, layer, sel, name):
    _, _, H, C = shards.shape
    bh = _row_block(H, C)

    def body(sel_ref, x_ref, o_ref):
        o_ref[...] = x_ref[...].astype(BF16)

    return pl.pallas_call(
        body, name=name, out_shape=jax.ShapeDtypeStruct((N_CHIPS, 2, H, C), BF16),
        grid_spec=pltpu.PrefetchScalarGridSpec(
            num_scalar_prefetch=1, grid=(2, H // bh),
            in_specs=[pl.BlockSpec((None, None, bh, C), lambda h, i, s: (layer, h, i, 0))],
            out_specs=pl.BlockSpec((None, None, bh, C), lambda h, i, s: (s[0], h, i, 0))),
        compiler_params=pltpu.CompilerParams(dimension_semantics=("arbitrary", "arbitrary"),
                                             vmem_limit_bytes=VMEM_LIMIT),
    )(sel, shards)


def _cast_place_many(pairs, sel, comm, name):
    n = len(pairs)
    _, _, H, C = pairs[0][0].shape
    bh = _row_block(H, C)
    last = H // bh - 1
    nci, nco = len(comm.ins), len(comm.out_shapes)

    def body(sel_ref, *refs):
        xs, refs = refs[:n], refs[n:]
        cins, refs = refs[:nci], refs[nci:]
        outs, refs = refs[:n], refs[n:]
        couts, sems = refs[:nco], refs[nco:]
        h, i = pl.program_id(0), pl.program_id(1)

        @pl.when((h == 0) & (i == 0))
        def _():
            comm.start(cins, couts, sems)

        if comm.middle is not None:
            @pl.when((h == 1) & (i == 0))
            def _():
                comm.middle(cins, couts, sems)

        for x_ref, o_ref in zip(xs, outs):
            o_ref[...] = x_ref[...].astype(BF16)

        @pl.when((h == 1) & (i == last))
        def _():
            comm.finish(cins, couts, sems)

    res = pl.pallas_call(
        body, name=name,
        out_shape=[jax.ShapeDtypeStruct((N_CHIPS, 2, H, C), BF16)] * n + list(comm.out_shapes),
        grid_spec=pltpu.PrefetchScalarGridSpec(
            num_scalar_prefetch=1, grid=(2, H // bh),
            in_specs=[pl.BlockSpec((None, None, bh, C), lambda h, i, s, layer=layer: (layer, h, i, 0))
                      for _, layer in pairs] + [ANY] * nci,
            out_specs=[pl.BlockSpec((None, None, bh, C), lambda h, i, s: (s[0], h, i, 0))] * n + [ANY] * nco,
            scratch_shapes=[pltpu.SemaphoreType.DMA((k,)) for k in comm.sem_counts]),
        input_output_aliases={1 + n + a: n + b for a, b in comm.aliases.items()},
        compiler_params=pltpu.CompilerParams(dimension_semantics=("arbitrary", "arbitrary"),
                                             vmem_limit_bytes=VMEM_LIMIT, has_side_effects=True),
    )(sel, *[s for s, _ in pairs], *comm.ins)
    return res[:n], res[n:]


def _adamw_many(quads, name, comm=None):
    R, C = quads[0][0].shape
    br = _row_block(R, C)
    n = len(quads)

    def body(*refs):
        for k in range(n):
            g_ref, w_ref, m_ref, v_ref = refs[4 * k:4 * k + 4]
            d_ref, mo_ref, vo_ref = refs[4 * n + 3 * k:4 * n + 3 * k + 3]
            gv = g_ref[...]
            mn = ADAM_B1 * m_ref[...] + (1.0 - ADAM_B1) * gv
            vn = ADAM_B2 * v_ref[...] + (1.0 - ADAM_B2) * (gv * gv)
            m_hat = mn / (1.0 - ADAM_B1 ** ADAM_STEP)
            v_hat = vn / (1.0 - ADAM_B2 ** ADAM_STEP)
            d_ref[...] = -ADAM_LR * (m_hat / (jnp.sqrt(v_hat) + ADAM_EPS) + ADAM_WD * w_ref[...])
            mo_ref[...] = mn
            vo_ref[...] = vn

    spec = pl.BlockSpec((br, C), lambda i: (i, 0))
    res, got = _pcall(body, name=name, grid=(R // br,), comm=comm, out_shape=(jax.ShapeDtypeStruct((R, C), F32),) * (3 * n),
                      in_specs=[spec] * (4 * n), out_specs=(spec,) * (3 * n), args=[a for q in quads for a in q])
    return [res[3 * k:3 * k + 3] for k in range(n)], got


def _adamw(g, w, m, v, name):
    return _adamw_many([(g, w, m, v)], name)[0][0]


def _part_rows(shape):
    return -(-math.prod(shape) // (8 * LANES)) * 8


def _pack_rows(arrs, total_rows):
    parts = []
    for a in arrs:
        flat = a.reshape(-1)
        rows = _part_rows(a.shape)
        parts.append(jnp.pad(flat, (0, rows * LANES - flat.shape[0])).reshape(rows, LANES))
    packed = jnp.concatenate(parts, axis=0)
    return jnp.pad(packed, ((0, total_rows - packed.shape[0]), (0, 0)))


def _unpack_rows(packed, shapes):
    out, r = [], 0
    for shp in shapes:
        size, rows = math.prod(shp), _part_rows(shp)
        out.append(packed[r:r + rows].reshape(-1)[:size].reshape(shp))
        r += rows
    return out


def _rows_of(shapes):
    return sum(_part_rows(s) for s in shapes)


_GATHER_STAGES = {"start": ["wa", "mkv0", "wo0"], "mixer_a_fwd": ["wup0", "wdown0"],
                  "mlp_fwd_0": ["wq", "wkv", "mkv1", "wo1"], "fox_fwd": ["wup1", "wdown1"]}
_SWAP_STAGES = {"mixer_b_bwd": ["wup1", "wdown1"]}
_REDUCE_STAGES = {"fox_bwd": ["wup1", "wdown1"], "mlp_bwd_0": ["wo1", "wq", "wkv", "mkv1"],
                  "mixer_a_bwd": ["wup0", "wdown0"], "end": ["wa", "wo0", "mkv0", "small"]}
_SMALL = ["ws", "b_s", "sg", "sb", "bf", "ln_g", "ln_b", "loss"]


class _Staged:
    def __init__(self, placed, sel, dm, ng, shard_cols, n_small):
        self.placed, self.sel, self.dm, self.ng, self.shard_cols, self.n_small = placed, sel, dm, ng, shard_cols, n_small
        self.g4, self.recv1, self.recv2 = {}, {}, {}

    def _weight(self, name, a):
        a = a.reshape(N_CHIPS, 2 * a.shape[2], a.shape[3])
        if name in ("wa", "wkv"):
            a = a.transpose(1, 0, 2).reshape(a.shape[1], -1)
            if name == "wkv":
                a = jnp.concatenate([a[:, :2 * self.dm],
                                     jnp.pad(a[:, 2 * self.dm:], ((0, 0), (0, LANES - self.ng)))], axis=1)
            return a
        if name.startswith("wup") or name.startswith("wdown"):
            return a
        return a.reshape(-1, a.shape[2])

    def _partial(self, name, grads):
        if name == "small":
            a = _pack_rows([grads[n] for n in _SMALL], self.n_small).reshape(N_CHIPS, -1, LANES)
        elif name in self.shard_cols:
            n = self.shard_cols[name]
            g = grads[name][:, :N_CHIPS * n]
            a = g.reshape(g.shape[0], N_CHIPS, n).transpose(1, 0, 2)
        elif name.startswith("wup") or name.startswith("wdown"):
            a = grads[name]
        else:
            g = grads[name]
            a = g.reshape(N_CHIPS, g.shape[0] // N_CHIPS, g.shape[1])
        return a.reshape(N_CHIPS, 2, a.shape[1] // 2, a.shape[2])

    def plan(self, stage, w, grads):
        if stage in _GATHER_STAGES:
            return _gather_weights([self.placed[n] for n in _GATHER_STAGES[stage]],
                                   early=1 if stage == "mixer_a_fwd" else None)
        if stage in _SWAP_STAGES:
            for n in _SWAP_STAGES[stage]:
                self.g4[n] = self._partial(n, grads)
            return _pair_swap([self.g4[n] for n in _SWAP_STAGES[stage]])
        if stage in _REDUCE_STAGES:
            names = _REDUCE_STAGES[stage]
            late = [n for n in names if n not in self.recv1]
            for n in late:
                self.g4[n] = self._partial(n, grads)
            if late:
                got = _run_comm(_pair_swap([self.g4[n] for n in late]), "reduce_pair_swap_" + stage)
                self.recv1.update(zip(late, got))
            return _chip_exchange([_pair_add(self.g4[n], self.recv1[n], self.sel, F32 if n == "small" else BF16,
                                             "reduce_pair_add_" + n) for n in names])
        return None

    def done(self, stage, results, w):
        if stage in _GATHER_STAGES:
            for n, a in zip(_GATHER_STAGES[stage], results):
                w[n] = self._weight(n, a)
        elif stage in _SWAP_STAGES:
            self.recv1.update(zip(_SWAP_STAGES[stage], results))
        elif stage in _REDUCE_STAGES:
            self.recv2.update(zip(_REDUCE_STAGES[stage], results))


def kernel(x, mem, a_w_in, a_sgu_ln_g, a_sgu_ln_b, a_w_s, a_b_s, kv_w, kv_b_f, b_w_q, mem_w_kv, w_o, ln_g, ln_b, w_up, w_down, loss_target, m_a_w_in, m_a_sgu_ln_g, m_a_sgu_ln_b, m_a_w_s, m_a_b_s, m_kv_w, m_kv_b_f, m_b_w_q, m_mem_w_kv, m_w_o, m_ln_g, m_ln_b, m_w_up, m_w_down, v_a_w_in, v_a_sgu_ln_g, v_a_sgu_ln_b, v_a_w_s, v_a_b_s, v_kv_w, v_kv_b_f, v_b_w_q, v_mem_w_kv, v_w_o, v_ln_g, v_ln_b, v_w_up, v_w_down):
    xi, yi, ci = lax.axis_index("x"), lax.axis_index("y"), lax.axis_index("c")
    chip = 2 * xi + yi
    sel = jnp.stack([chip, ci]).astype(jnp.int32)
    S, D = x.shape[1], x.shape[2]
    DQ = mem_w_kv.shape[2] // 2
    DM = D - DQ
    NG = DM // HEAD_DIM

    shards = {"wa": (a_w_in, 0), "wkv": (kv_w[None], 0), "wq": (b_w_q, 0), "mkv0": (mem_w_kv, 0), "mkv1": (mem_w_kv, 1),
              "wo0": (w_o, 0), "wo1": (w_o, 1), "wup0": (w_up, 0), "wup1": (w_up, 1), "wdown0": (w_down, 0),
              "wdown1": (w_down, 1)}
    halves = lambda s: s.reshape(s.shape[0], 2, s.shape[1] // 2, s.shape[2])
    mlp = ["wup0", "wup1", "wdown0", "wdown1"]
    placed = {n: _cast_place(halves(s), layer, sel, "cast_place_" + n)
              for n, (s, layer) in shards.items() if n not in mlp}
    ln_pack = jnp.concatenate([ln_g.reshape(4, -1), ln_b.reshape(4, -1)], axis=0)
    ln_buf = lax.dynamic_update_slice(jnp.zeros((N_CHIPS,) + ln_pack.shape, F32), ln_pack[None], (chip, 0, 0))
    mlp_placed, got = _cast_place_many(
        [(halves(shards[n][0]), shards[n][1]) for n in mlp], sel,
        _gather_weights([placed[n] for n in _GATHER_STAGES["start"]], whole=[ln_buf]), "cast_place_mlp")
    placed.update(zip(mlp, mlp_placed))
    ln_shape = (4, D)
    small_shapes = [a_w_s.shape, a_b_s.shape, a_sgu_ln_g.shape, a_sgu_ln_b.shape, kv_b_f.shape, ln_shape, ln_shape,
                    (1, LANES)]
    n_small = -(-_rows_of(small_shapes) // 64) * 64
    hooks = _Staged(placed, sel, DM, NG, {"wa": a_w_in.shape[2], "wkv": kv_w.shape[1]}, n_small)
    weights = {"sg": a_sgu_ln_g, "sb": a_sgu_ln_b, "ws": a_w_s[0], "b_s": a_b_s[0], "bf": kv_b_f}
    hooks.done("start", got[:-1], weights)
    ln_full = got[-1].transpose(1, 0, 2).reshape(8, D)
    weights["ln_g"], weights["ln_b"] = ln_full[:4], ln_full[4:]

    _, grad_x, gr = _local_step(x[0], mem[0], loss_target[0], weights, hooks)

    def chip_sums(groups):
        bufs = []
        for grp in groups:
            buf = None
            for layer, n in enumerate(grp):
                buf = _chip_reduce(hooks.g4[n], hooks.recv1[n], hooks.recv2[n], sel, buf, layer, len(grp),
                                   "reduce_chip_sum_" + n)
            bufs.append(buf)
        return bufs

    whole = lambda a: a.reshape(a.shape[0], 2 * a.shape[3], a.shape[4])
    flat = lambda a: a.reshape(-1, a.shape[-1])

    def update(name, g, w, m, v):
        d, mn, vn = _adamw(flat(g), flat(w), flat(m), flat(v), "adamw_" + name)
        return d.reshape(w.shape), mn.reshape(w.shape), vn.reshape(w.shape)

    parts = lambda n: (hooks.g4[n], hooks.recv1[n], hooks.recv2[n])
    mlp_sums, got = _chip_reduce_many([[parts("wup0"), parts("wup1")], [parts("wdown0"), parts("wdown1")]], sel,
                                      hooks.plan("end", weights, gr), "reduce_chip_sum_mlp")
    hooks.done("end", got, weights)
    small_mine = _chip_reduce(hooks.g4["small"], hooks.recv1["small"], hooks.recv2["small"], sel, None, 0, 1,
                              "reduce_chip_sum_small", by_chip=True)
    red, small_all = _pair_gather(chip_sums([["wa"], ["wkv"], ["wq"], ["mkv0", "mkv1"], ["wo0", "wo1"]])
                                  + list(mlp_sums), small_mine, "reduce_pair_gather")
    g_a_w_in, g_kv_w, g_b_w_q, g_mem_w_kv, g_w_o, g_w_up, g_w_down = [whole(a) for a in red]
    g_kv_w = g_kv_w[0]
    g_w_down = g_w_down.transpose(0, 2, 1)
    small_all = small_all.reshape(n_small, LANES)
    g_ws, g_bs, g_sg, g_sb, g_bf, g_lng, g_lnb, loss_row = _unpack_rows(small_all, small_shapes)
    loss = loss_row[0, 0]
    dsh = D // N_CHIPS
    g_ln_g = lax.dynamic_slice_in_dim(g_lng, chip * dsh, dsh, axis=1).reshape(ln_g.shape)
    g_ln_b = lax.dynamic_slice_in_dim(g_lnb, chip * dsh, dsh, axis=1).reshape(ln_b.shape)

    upd = {
        "a_w_in": update("a_w_in", g_a_w_in, a_w_in, m_a_w_in, v_a_w_in),
        "kv_w": update("kv_w", g_kv_w, kv_w, m_kv_w, v_kv_w),
        "b_w_q": update("b_w_q", g_b_w_q, b_w_q, m_b_w_q, v_b_w_q),
        "mem_w_kv": update("mem_w_kv", g_mem_w_kv, mem_w_kv, m_mem_w_kv, v_mem_w_kv),
        "w_o": update("w_o", g_w_o, w_o, m_w_o, v_w_o),
        "w_up": update("w_up", g_w_up, w_up, m_w_up, v_w_up),
        "w_down": update("w_down", g_w_down, w_down, m_w_down, v_w_down),
    }
    tiny_g = [g_sg, g_sb, g_ws, g_bs, g_bf, g_ln_g, g_ln_b]
    tiny_w = [a_sgu_ln_g, a_sgu_ln_b, a_w_s, a_b_s, kv_b_f, ln_g, ln_b]
    tiny_m = [m_a_sgu_ln_g, m_a_sgu_ln_b, m_a_w_s, m_a_b_s, m_kv_b_f, m_ln_g, m_ln_b]
    tiny_v = [v_a_sgu_ln_g, v_a_sgu_ln_b, v_a_w_s, v_a_b_s, v_kv_b_f, v_ln_g, v_ln_b]
    tiny_shapes = [a.shape for a in tiny_w]
    n_tiny = -(-_rows_of(tiny_shapes) // 8) * 8
    td, tm_, tv = _adamw(_pack_rows(tiny_g, n_tiny), _pack_rows(tiny_w, n_tiny), _pack_rows(tiny_m, n_tiny),
                         _pack_rows(tiny_v, n_tiny), "adamw_small")
    for name, d, mn, vn in zip(["a_sgu_ln_g", "a_sgu_ln_b", "a_w_s", "a_b_s", "kv_b_f", "ln_g", "ln_b"],
                               _unpack_rows(td, tiny_shapes), _unpack_rows(tm_, tiny_shapes),
                               _unpack_rows(tv, tiny_shapes)):
        upd[name] = (d, mn, vn)

    order = ["a_w_in", "a_sgu_ln_g", "a_sgu_ln_b", "a_w_s", "a_b_s", "kv_w", "kv_b_f", "b_w_q", "mem_w_kv", "w_o",
             "ln_g", "ln_b", "w_up", "w_down"]
    grads = {"a_w_in": g_a_w_in.reshape(a_w_in.shape), "a_sgu_ln_g": g_sg.reshape(a_sgu_ln_g.shape),
             "a_sgu_ln_b": g_sb.reshape(a_sgu_ln_b.shape), "a_w_s": g_ws.reshape(a_w_s.shape),
             "a_b_s": g_bs.reshape(a_b_s.shape), "kv_w": g_kv_w, "kv_b_f": g_bf, "b_w_q": g_b_w_q.reshape(b_w_q.shape),
             "mem_w_kv": g_mem_w_kv, "w_o": g_w_o, "ln_g": g_ln_g, "ln_b": g_ln_b, "w_up": g_w_up, "w_down": g_w_down}
    return (loss, grad_x[None], *[grads[n] for n in order], *[upd[n][0] for n in order],
            *[upd[n][1] for n in order], *[upd[n][2] for n in order])
```

```python
import functools
import math

import jax
import jax.numpy as jnp
from jax import lax
from jax.experimental import pallas as pl
from jax.experimental.pallas import tpu as pltpu

F32 = jnp.float32
BF16 = jnp.bfloat16

HEAD_DIM = 64
PAIR = 2 * HEAD_DIM
CHUNK = 128
LN_EPS = 1e-5
ALPHA = 4 ** 0.25
QK_SCALE = 1.0 / math.sqrt(HEAD_DIM)
NEG = -1e30
N_CHIPS = 4
FOX_BLOCK = 256
LANES = 128
VMEM_LIMIT = 56 * 2 ** 20

ADAM_LR, ADAM_B1, ADAM_B2, ADAM_EPS, ADAM_WD, ADAM_STEP = 0.001, 0.9, 0.999, 1e-08, 0.01, 10

MESH = pl.DeviceIdType.MESH
ANY = pl.BlockSpec(memory_space=pl.ANY)


def _dot(a, b):
    return jnp.dot(a, b, preferred_element_type=F32)


def _dot_nt(a, b):
    return lax.dot_general(a, b, (((1,), (1,)), ((), ())), preferred_element_type=F32)


def _dot_tn(a, b):
    return lax.dot_general(a, b, (((0,), (0,)), ((), ())), preferred_element_type=F32)


def _ln_stats(r):
    mu = jnp.mean(r, axis=-1, keepdims=True)
    d = r - mu
    var = jnp.mean(d * d, axis=-1, keepdims=True)
    rstd = lax.rsqrt(var + LN_EPS)
    return d * rstd, rstd


def _ln_bwd(dy, xhat, rstd, g):
    dxh = dy * g
    m1 = jnp.mean(dxh, axis=-1, keepdims=True)
    m2 = jnp.mean(dxh * xhat, axis=-1, keepdims=True)
    return rstd * (dxh - m1 - xhat * m2)


_GELU_K = math.sqrt(2.0 / math.pi)


def _gelu(x):
    return 0.5 * x * (1.0 + jnp.tanh(_GELU_K * (x + 0.044715 * x * x * x)))


def _gelu_grad(x):
    t = jnp.tanh(_GELU_K * (x + 0.044715 * x * x * x))
    return 0.5 * (1.0 + t) + 0.5 * x * (1.0 - t * t) * _GELU_K * (1.0 + 3 * 0.044715 * x * x)


def _lane_mask(e):
    lane = lax.broadcasted_iota(jnp.int32, (1, PAIR), 1)
    return (lane >= HEAD_DIM * e) & (lane < HEAD_DIM * (e + 1))


def _first_head():
    return lax.broadcasted_iota(jnp.int32, (1, PAIR), 1) < HEAD_DIM


def _mem_probs(qh, mkp):
    sc = _dot_nt(qh, mkp) * QK_SCALE
    ex = jnp.exp(sc - jnp.max(sc, axis=-1, keepdims=True))
    return ex / jnp.sum(ex, axis=-1, keepdims=True)


def _mem_attn_fwd(qm_b, mkv_ref, dq_dim):
    outs = []
    for mp in range(dq_dim // PAIR):
        qp = qm_b[:, mp * PAIR:(mp + 1) * PAIR]
        mkp = mkv_ref[:, mp * PAIR:(mp + 1) * PAIR]
        mvp = mkv_ref[:, dq_dim + mp * PAIR:dq_dim + (mp + 1) * PAIR]
        heads = []
        for e in (0, 1):
            qh = jnp.where(_lane_mask(e), qp, jnp.zeros_like(qp))
            p = _mem_probs(qh, mkp)
            heads.append(_dot(p.astype(BF16), mvp))
        outs.append(jnp.where(_first_head(), heads[0], heads[1]))
    return outs


def _mem_attn_bwd(qm_b, mkv_ref, dmo_pairs, dmkv_ref, dq_dim):
    dqs = []
    for mp in range(dq_dim // PAIR):
        ks = slice(mp * PAIR, (mp + 1) * PAIR)
        vs = slice(dq_dim + mp * PAIR, dq_dim + (mp + 1) * PAIR)
        qp = qm_b[:, ks]
        mkp = mkv_ref[:, ks]
        mvp = mkv_ref[:, vs]
        dmo_b = dmo_pairs[mp].astype(BF16)
        dq = None
        dmk = None
        dmv = None
        for e in (0, 1):
            hm = _lane_mask(e)
            qh = jnp.where(hm, qp, jnp.zeros_like(qp))
            p = _mem_probs(qh, mkp)
            doh = jnp.where(hm, dmo_b, jnp.zeros_like(dmo_b))
            dp = _dot_nt(doh, mvp)
            ds = p * (dp - jnp.sum(dp * p, axis=-1, keepdims=True))
            dsb = (ds * QK_SCALE).astype(BF16)
            kh = jnp.where(hm, mkp, jnp.zeros_like(mkp))
            dq_e = _dot(dsb, kh)
            dmk_e = _dot_tn(dsb, qh)
            dmv_e = _dot_tn(p.astype(BF16), doh)
            dq = dq_e if dq is None else dq + dq_e
            dmk = dmk_e if dmk is None else dmk + dmk_e
            dmv = dmv_e if dmv is None else dmv + dmv_e
        dmkv_ref[:, ks] += dmk
        dmkv_ref[:, vs] += dmv
        dqs.append(dq)
    return dqs


def _params(n_axes=1):
    return pltpu.CompilerParams(dimension_semantics=("arbitrary",) * n_axes, vmem_limit_bytes=VMEM_LIMIT)


def _full(shape):
    return pl.BlockSpec(shape, lambda *_: (0,) * len(shape))


def _rows(tm, cols):
    return pl.BlockSpec((tm, cols), lambda i: (i, 0))


def _cols(rows, tm):
    return pl.BlockSpec((rows, tm), lambda i: (0, i))


def _pick(n, pref):
    if n <= pref:
        return n
    best = LANES
    for t in range(LANES, pref + 1, LANES):
        if n % t == 0:
            best = t
    return best if 2 * best >= pref or n > 2 * pref else n


class _Comm:
    def __init__(self, ins, out_shapes, aliases, sem_counts, start, finish, middle=None):
        self.ins, self.out_shapes, self.aliases, self.sem_counts = list(ins), list(out_shapes), dict(aliases), sem_counts
        self.start, self.finish = start, finish
        self.middle = middle


def _pcall(body, *, name, grid, in_specs, out_specs, out_shape, args, scratch_shapes=(), comm=None, aliases=None):
    n_in, n_out, n_scr = len(in_specs), len(out_shape), len(scratch_shapes)
    aliases = dict(aliases or {})
    if comm is None:
        res = pl.pallas_call(body, name=name, grid=grid, out_shape=tuple(out_shape), in_specs=list(in_specs),
                             out_specs=tuple(out_specs), scratch_shapes=list(scratch_shapes),
                             input_output_aliases=aliases, compiler_params=_params())(*args)
        return tuple(res), ()
    nci, nco = len(comm.ins), len(comm.out_shapes)
    last = grid[0] - 1

    def wrapped(*refs):
        ins, refs = refs[:n_in], refs[n_in:]
        cins, refs = refs[:nci], refs[nci:]
        outs, refs = refs[:n_out], refs[n_out:]
        couts, refs = refs[:nco], refs[nco:]
        scr, sems = refs[:n_scr], refs[n_scr:]

        @pl.when(pl.program_id(0) == 0)
        def _():
            comm.start(cins, couts, sems)

        if comm.middle is not None:
            @pl.when(pl.program_id(0) == (last + 1) // 2)
            def _():
                comm.middle(cins, couts, sems)

        body(*ins, *outs, *scr)

        @pl.when(pl.program_id(0) == last)
        def _():
            comm.finish(cins, couts, sems)

    res = pl.pallas_call(
        wrapped, name=name, grid=grid, out_shape=tuple(out_shape) + tuple(comm.out_shapes),
        in_specs=list(in_specs) + [ANY] * nci, out_specs=tuple(out_specs) + (ANY,) * nco,
        input_output_aliases={**aliases, **{n_in + a: n_out + b for a, b in comm.aliases.items()}},
        scratch_shapes=list(scratch_shapes) + [pltpu.SemaphoreType.DMA((k,)) for k in comm.sem_counts],
        compiler_params=pltpu.CompilerParams(dimension_semantics=("arbitrary",), vmem_limit_bytes=VMEM_LIMIT,
                                             has_side_effects=True),
    )(*args, *comm.ins)
    return tuple(res[:n_out]), tuple(res[n_out:])


def _run_comm(comm, name):
    nci, nco = len(comm.ins), len(comm.out_shapes)

    def body(*refs):
        cins, couts, sems = refs[:nci], refs[nci:nci + nco], refs[nci + nco:]
        comm.start(cins, couts, sems)
        if comm.middle is not None:
            comm.middle(cins, couts, sems)
        comm.finish(cins, couts, sems)

    return pl.pallas_call(
        body, name=name, out_shape=tuple(comm.out_shapes), in_specs=[ANY] * nci, out_specs=(ANY,) * nco,
        input_output_aliases=comm.aliases, scratch_shapes=[pltpu.SemaphoreType.DMA((k,)) for k in comm.sem_counts],
        compiler_params=pltpu.CompilerParams(has_side_effects=True),
    )(*comm.ins)


def _mm(a, b, name, a_is_transposed, out_split=1):
    (M, R) = a.shape if a_is_transposed else a.shape[::-1]
    _, N = b.shape
    bm, br = _pick(M, 1024 if a_is_transposed else 512), _pick(R, 1024 if a_is_transposed else 512)
    ncol = N // out_split
    bn = _pick(ncol, 1024)
    per = ncol // bn
    last = R // br - 1

    def body(a_ref, b_ref, o_ref, acc_ref):
        @pl.when(pl.program_id(2) == 0)
        def _():
            acc_ref[...] = jnp.zeros_like(acc_ref)

        acc_ref[...] += (_dot if a_is_transposed else _dot_tn)(a_ref[...], b_ref[...])

        @pl.when(pl.program_id(2) == last)
        def _():
            o_ref[...] = acc_ref[...].astype(BF16)

    if out_split == 1:
        out_shape = jax.ShapeDtypeStruct((M, N), BF16)
        out_spec = pl.BlockSpec((bm, bn), lambda i, j, r: (i, j))
    else:
        out_shape = jax.ShapeDtypeStruct((out_split, M, ncol), BF16)
        out_spec = pl.BlockSpec((None, bm, bn), lambda i, j, r: (j // per, i, j % per))
    a_spec = (pl.BlockSpec((bm, br), lambda i, j, r: (i, r)) if a_is_transposed
              else pl.BlockSpec((br, bm), lambda i, j, r: (r, i)))
    return pl.pallas_call(
        body, name=name, out_shape=out_shape, grid=(M // bm, N // bn, R // br),
        in_specs=[a_spec, pl.BlockSpec((br, bn), lambda i, j, r: (r, j))], out_specs=out_spec,
        scratch_shapes=[pltpu.VMEM((bm, bn), F32)],
        compiler_params=pltpu.CompilerParams(dimension_semantics=("parallel", "parallel", "arbitrary"),
                                             vmem_limit_bytes=VMEM_LIMIT),
    )(a, b)


def _mm_tn(a, b, name, out_split=1):
    return _mm(a, b, name, False, out_split)


def _mm_nn(at, b, name, out_split=1):
    return _mm(at, b, name, True, out_split)


def _mm_small(a, b, name):
    def body(a_ref, b_ref, o_ref):
        o_ref[...] = _dot(a_ref[...], b_ref[...]).astype(BF16)

    return pl.pallas_call(body, name=name, out_shape=jax.ShapeDtypeStruct((a.shape[0], b.shape[1]), BF16),
                          compiler_params=pltpu.CompilerParams(vmem_limit_bytes=VMEM_LIMIT))(a, b)


def _mlp_out(x_ref, wup_ref, wdown_ref, g_ref, b_ref, r_ref, h_ref):
    NC, _, FC = wup_ref.shape
    xv = x_ref[...]
    xb = xv.astype(BF16)
    y = jnp.zeros(xv.shape, F32)
    for c in range(NC):
        a = jnp.maximum(_dot(xb, wup_ref[c]), 0.0)
        hb = (a * a).astype(BF16)
        h_ref[:, c * FC:(c + 1) * FC] = hb
        y = y + _dot(hb, wdown_ref[c])
    r = ALPHA * xv + y
    r_ref[...] = r
    xhat, _ = _ln_stats(r)
    return xhat * g_ref[...] + b_ref[...]


def _mlp_fwd_loss(x, wup4, wdown4, g, b, target, name):
    S, D = x.shape
    NC, _, FC = wup4.shape
    tm = min(512, S)

    def body(x_ref, wup_ref, wdown_ref, g_ref, b_ref, t_ref, d_ref, r_ref, h_ref, l_ref):
        @pl.when(pl.program_id(0) == 0)
        def _():
            l_ref[...] = jnp.zeros_like(l_ref)

        err = _mlp_out(x_ref, wup_ref, wdown_ref, g_ref, b_ref, r_ref, h_ref) - t_ref[...]
        d_ref[...] = err * (1.0 / D)
        l_ref[...] += (0.5 / D) * jnp.sum(err * err)

    return pl.pallas_call(
        body, name=name, grid=(S // tm,),
        out_shape=(jax.ShapeDtypeStruct((S, D), F32), jax.ShapeDtypeStruct((S, D), F32),
                   jax.ShapeDtypeStruct((S, NC * FC), BF16), jax.ShapeDtypeStruct((8, LANES), F32)),
        in_specs=[_rows(tm, D), _full(wup4.shape), _full(wdown4.shape), _full((1, D)), _full((1, D)), _rows(tm, D)],
        out_specs=(_rows(tm, D), _rows(tm, D), _rows(tm, NC * FC), _full((8, LANES))),
        compiler_params=_params(),
    )(x, wup4, wdown4, g, b, target)


def _mlp_fwd(x, wup4, wdown4, g, b, name, comm=None):
    S, D = x.shape
    NC, _, FC = wup4.shape
    tm = min(512, S)

    def body(x_ref, wup_ref, wdown_ref, g_ref, b_ref, xo_ref, xb_ref, xbt_ref, r_ref, h_ref):
        xo = _mlp_out(x_ref, wup_ref, wdown_ref, g_ref, b_ref, r_ref, h_ref)
        xo_ref[...] = xo
        xb_ref[...] = xo.astype(BF16)
        xbt_ref[...] = xo.T.astype(BF16)

    return _pcall(
        body, name=name, grid=(S // tm,), comm=comm,
        out_shape=(jax.ShapeDtypeStruct((S, D), F32), jax.ShapeDtypeStruct((S, D), BF16),
                   jax.ShapeDtypeStruct((D, S), BF16), jax.ShapeDtypeStruct((S, D), F32),
                   jax.ShapeDtypeStruct((S, NC * FC), BF16)),
        in_specs=[_rows(tm, D), _full(wup4.shape), _full(wdown4.shape), _full((1, D)), _full((1, D))],
        out_specs=(_rows(tm, D), _rows(tm, D), _cols(D, tm), _rows(tm, D), _rows(tm, NC * FC)),
        args=(x, wup4, wdown4, g, b))


def _mlp_bwd(dxo, r, h, wup4, wdown4, g, name, comm=None):
    S, D = r.shape
    NC, _, FC = wup4.shape
    tm = min(512, S)

    def body(dxo_ref, r_ref, h_ref, wup_ref, wdown_ref, g_ref, dx_ref, drbt_ref, dpre_ref, dgb_ref):
        @pl.when(pl.program_id(0) == 0)
        def _():
            dgb_ref[...] = jnp.zeros_like(dgb_ref)

        xhat, rstd = _ln_stats(r_ref[...])
        dy = dxo_ref[...]
        dgb_ref[0:1, :] += jnp.sum(dy * xhat, axis=0, keepdims=True)
        dgb_ref[1:2, :] += jnp.sum(dy, axis=0, keepdims=True)
        dr = _ln_bwd(dy, xhat, rstd, g_ref[...])
        drb = dr.astype(BF16)
        drbt_ref[...] = dr.T.astype(BF16)
        dx = ALPHA * dr
        for c in range(NC):
            dh = _dot_nt(drb, wdown_ref[c])
            a = jnp.sqrt(h_ref[:, c * FC:(c + 1) * FC].astype(F32))
            dpre = (2.0 * a * dh).astype(BF16)
            dpre_ref[:, c * FC:(c + 1) * FC] = dpre
            dx = dx + _dot_nt(dpre, wup_ref[c])
        dx_ref[...] = dx

    return _pcall(
        body, name=name, grid=(S // tm,), comm=comm,
        out_shape=(jax.ShapeDtypeStruct((S, D), F32), jax.ShapeDtypeStruct((D, S), BF16),
                   jax.ShapeDtypeStruct((S, NC * FC), BF16), jax.ShapeDtypeStruct((2, D), F32)),
        in_specs=[_rows(tm, D), _rows(tm, D), _rows(tm, NC * FC), _full(wup4.shape), _full(wdown4.shape),
                  _full((1, D))],
        out_specs=(_rows(tm, D), _cols(D, tm), _rows(tm, NC * FC), _full((2, D))),
        args=(dxo, r, h, wup4, wdown4, g))


def _tail_fwd_store(xv, cat_ref, wo_ref, g_ref, b_ref, xo_ref, xbt_ref, r_ref, catt_ref):
    cat = cat_ref[...]
    y = _dot(cat, wo_ref[...])
    r = ALPHA * xv + y
    xhat, _ = _ln_stats(r)
    xo = xhat * g_ref[...] + b_ref[...]
    xo_ref[...] = xo
    xbt_ref[...] = xo.T.astype(BF16)
    catt_ref[...] = cat.astype(F32).T.astype(BF16)
    r_ref[...] = r


def _tail_bwd_head(dxo_ref, r_ref, g_ref, wo_ref, dgb_ref, drb_ref):
    xhat, rstd = _ln_stats(r_ref[...])
    dy = dxo_ref[...]
    dgb_ref[0:1, :] += jnp.sum(dy * xhat, axis=0, keepdims=True)
    dgb_ref[1:2, :] += jnp.sum(dy, axis=0, keepdims=True)
    dr = _ln_bwd(dy, xhat, rstd, g_ref[...])
    drb = dr.astype(BF16)
    drb_ref[...] = drb
    return dr, _dot_nt(drb, wo_ref[...])


def _causal(shape):
    return lax.broadcasted_iota(jnp.int32, shape, 1) <= lax.broadcasted_iota(jnp.int32, shape, 0)


def _sgu_mixed(ws_ref, bst_ref, gp, vpair):
    tril = _causal((CHUNK, CHUNK))
    w0 = jnp.where(tril, ws_ref[2 * gp], 0.0).astype(BF16)
    w1 = jnp.where(tril, ws_ref[2 * gp + 1], 0.0).astype(BF16)
    m0 = _dot(w0, vpair) + bst_ref[:, 2 * gp:2 * gp + 1]
    m1 = _dot(w1, vpair) + bst_ref[:, 2 * gp + 1:2 * gp + 2]
    return jnp.where(_first_head(), m0, m1), w0, w1


def _mixer_a_fwd(x, wa, sg, sb, ws, bst, mkv, wo, g, b, name, comm=None):
    S, D = x.shape
    DQ = mkv.shape[1] // 2
    DM = D - DQ
    NP = DM // PAIR
    tm = min(512, S)

    def body(x_ref, wa_ref, sg_ref, sb_ref, ws_ref, bst_ref, mkv_ref, wo_ref, g_ref, b_ref,
             xo_ref, xbt_ref, r_ref, catt_ref, xint_ref, zb_ref, z_s, zv_s, cat_ref):
        xv = x_ref[...]
        xint_ref[...] = xv.T.astype(BF16)
        z_s[...] = _dot(xv.astype(BF16), wa_ref[...])
        zb_ref[...] = z_s[...].astype(BF16)
        vhat, _ = _ln_stats(_gelu(z_s[:, DM:2 * DM]))
        zv_s[...] = (vhat * sg_ref[...] + sb_ref[...]).astype(BF16)
        for c in range(tm // CHUNK):
            rs = slice(c * CHUNK, (c + 1) * CHUNK)
            for gp in range(NP):
                ls = slice(gp * PAIR, (gp + 1) * PAIR)
                mixed, _, _ = _sgu_mixed(ws_ref, bst_ref, gp, zv_s[rs, ls])
                cat_ref[rs, ls] = (_gelu(z_s[rs, ls]) * mixed).astype(BF16)
        mo = _mem_attn_fwd(z_s[:, 2 * DM:].astype(BF16), mkv_ref, DQ)
        for mp in range(DQ // PAIR):
            cat_ref[:, DM + mp * PAIR:DM + (mp + 1) * PAIR] = mo[mp].astype(BF16)
        _tail_fwd_store(xv, cat_ref, wo_ref, g_ref, b_ref, xo_ref, xbt_ref, r_ref, catt_ref)

    return _pcall(
        body, name=name, grid=(S // tm,), comm=comm,
        out_shape=(jax.ShapeDtypeStruct((S, D), F32), jax.ShapeDtypeStruct((D, S), BF16),
                   jax.ShapeDtypeStruct((S, D), F32), jax.ShapeDtypeStruct((D, S), BF16),
                   jax.ShapeDtypeStruct((D, S), BF16), jax.ShapeDtypeStruct((S, 2 * DM + DQ), BF16)),
        in_specs=[_rows(tm, D), _full(wa.shape), _full(sg.shape), _full(sb.shape), _full(ws.shape), _full(bst.shape),
                  _full(mkv.shape), _full(wo.shape), _full((1, D)), _full((1, D))],
        out_specs=(_rows(tm, D), _cols(D, tm), _rows(tm, D), _cols(D, tm), _cols(D, tm), _rows(tm, 2 * DM + DQ)),
        scratch_shapes=[pltpu.VMEM((tm, 2 * DM + DQ), F32), pltpu.VMEM((tm, DM), BF16), pltpu.VMEM((tm, D), BF16)],
        args=(x, wa, sg, sb, ws, bst, mkv, wo, g, b))


def _mixer_a_bwd(zb, dxo, r, wa, sg, sb, ws, bst, mkv, wo, g, name, comm=None):
    S, D = r.shape
    DQ = mkv.shape[1] // 2
    DM = D - DQ
    NP = DM // PAIR
    NG = DM // HEAD_DIM
    tm = min(512, S)

    def body(zb_ref, dxo_ref, r_ref, wa_ref, sg_ref, sb_ref, ws_ref, bst_ref, mkv_ref, wo_ref, g_ref,
             dx_ref, dz_ref, drb_ref, dgb_ref, dsgb_ref, dws_ref, dbst_ref, dmkv_ref, z_s, zv_s, dzv_s):
        @pl.when(pl.program_id(0) == 0)
        def _():
            dgb_ref[...] = jnp.zeros_like(dgb_ref)
            dsgb_ref[...] = jnp.zeros_like(dsgb_ref)
            dws_ref[...] = jnp.zeros_like(dws_ref)
            dbst_ref[...] = jnp.zeros_like(dbst_ref)
            dmkv_ref[...] = jnp.zeros_like(dmkv_ref)

        dr, dcat = _tail_bwd_head(dxo_ref, r_ref, g_ref, wo_ref, dgb_ref, drb_ref)
        z_s[...] = zb_ref[...].astype(F32)
        vhat, vrstd = _ln_stats(_gelu(z_s[:, DM:2 * DM]))
        zv_s[...] = (vhat * sg_ref[...] + sb_ref[...]).astype(BF16)
        tril = _causal((CHUNK, CHUNK))
        lane = lax.broadcasted_iota(jnp.int32, (1, LANES), 1)
        for c in range(tm // CHUNK):
            rs = slice(c * CHUNK, (c + 1) * CHUNK)
            for gp in range(NP):
                ls = slice(gp * PAIR, (gp + 1) * PAIR)
                vpair = zv_s[rs, ls]
                mixed, w0, w1 = _sgu_mixed(ws_ref, bst_ref, gp, vpair)
                u_pre = z_s[rs, ls]
                dmix = dcat[rs, ls]
                dz_ref[rs, ls] = (dmix * mixed * _gelu_grad(u_pre)).astype(BF16)
                dmixed = dmix * _gelu(u_pre)
                first = _first_head()
                d0 = jnp.where(first, dmixed, 0.0)
                d1 = jnp.where(first, 0.0, dmixed)
                d0b = d0.astype(BF16)
                d1b = d1.astype(BF16)
                dzv_s[rs, ls] = _dot_tn(w0, d0b) + _dot_tn(w1, d1b)
                dws_ref[2 * gp] += jnp.where(tril, _dot_nt(d0b, vpair), 0.0)
                dws_ref[2 * gp + 1] += jnp.where(tril, _dot_nt(d1b, vpair), 0.0)
                dbst_ref[...] += (jnp.where(lane == 2 * gp, jnp.sum(d0, axis=-1, keepdims=True), 0.0)
                                  + jnp.where(lane == 2 * gp + 1, jnp.sum(d1, axis=-1, keepdims=True), 0.0))
        dzv = dzv_s[...]
        dsgb_ref[0:1, :] += jnp.sum(dzv * vhat, axis=0, keepdims=True)
        dsgb_ref[1:2, :] += jnp.sum(dzv, axis=0, keepdims=True)
        dgv = _ln_bwd(dzv, vhat, vrstd, sg_ref[...])
        dz_ref[:, DM:2 * DM] = (dgv * _gelu_grad(z_s[:, DM:2 * DM])).astype(BF16)
        dmo = [dcat[:, DM + mp * PAIR:DM + (mp + 1) * PAIR] for mp in range(DQ // PAIR)]
        dqm = _mem_attn_bwd(z_s[:, 2 * DM:].astype(BF16), mkv_ref, dmo, dmkv_ref, DQ)
        for mp in range(DQ // PAIR):
            dz_ref[:, 2 * DM + mp * PAIR:2 * DM + (mp + 1) * PAIR] = dqm[mp].astype(BF16)
        dx_ref[...] = ALPHA * dr + _dot_nt(dz_ref[...], wa_ref[...])

    ZW = 2 * DM + DQ
    M = mkv.shape[0]
    return _pcall(
        body, name=name, grid=(S // tm,), comm=comm,
        out_shape=(jax.ShapeDtypeStruct((S, D), F32), jax.ShapeDtypeStruct((S, ZW), BF16),
                   jax.ShapeDtypeStruct((S, D), BF16), jax.ShapeDtypeStruct((2, D), F32),
                   jax.ShapeDtypeStruct((2, DM), F32), jax.ShapeDtypeStruct((NG, CHUNK, CHUNK), F32),
                   jax.ShapeDtypeStruct((CHUNK, LANES), F32), jax.ShapeDtypeStruct((M, 2 * DQ), F32)),
        in_specs=[_rows(tm, ZW), _rows(tm, D), _rows(tm, D), _full(wa.shape), _full(sg.shape), _full(sb.shape),
                  _full(ws.shape), _full(bst.shape), _full(mkv.shape), _full(wo.shape), _full((1, D))],
        out_specs=(_rows(tm, D), _rows(tm, ZW), _rows(tm, D), _full((2, D)), _full((2, DM)),
                   _full((NG, CHUNK, CHUNK)), _full((CHUNK, LANES)), _full((M, 2 * DQ))),
        scratch_shapes=[pltpu.VMEM((tm, ZW), F32), pltpu.VMEM((tm, DM), BF16), pltpu.VMEM((tm, DM), F32)],
        args=(zb, dxo, r, wa, sg, sb, ws, bst, mkv, wo, g))


def _proj_b_fwd(xb, wq, wkv, dm, name):
    S, D = xb.shape
    DQ = D - dm
    tm = min(512, S)

    def body(x_ref, wq_ref, wkv_ref, q_ref, qm_ref, k_ref, v_ref, fl_ref):
        xv = x_ref[...]
        z = _dot(xv, wq_ref[...])
        q_ref[...] = z[:, :dm].astype(BF16)
        qm_ref[...] = z[:, dm:].astype(BF16)
        kvf = _dot(xv, wkv_ref[...])
        k_ref[...] = kvf[:, :dm].astype(BF16)
        v_ref[...] = kvf[:, dm:2 * dm].astype(BF16)
        fl_ref[...] = kvf[:, 2 * dm:]

    return pl.pallas_call(
        body, name=name, grid=(S // tm,),
        out_shape=(jax.ShapeDtypeStruct((S, dm), BF16), jax.ShapeDtypeStruct((S, DQ), BF16),
                   jax.ShapeDtypeStruct((S, dm), BF16), jax.ShapeDtypeStruct((S, dm), BF16),
                   jax.ShapeDtypeStruct((S, LANES), F32)),
        in_specs=[_rows(tm, D), _full(wq.shape), _full(wkv.shape)],
        out_specs=(_rows(tm, dm), _rows(tm, DQ), _rows(tm, dm), _rows(tm, dm), _rows(tm, LANES)),
        compiler_params=_params(),
    )(xb, wq, wkv)


def _proj_b_bwd(dr, dqz, dk, dv, dfl, wq, wkv, name):
    S, D = dr.shape
    dm = dk.shape[1]
    tm = min(512, S)

    def body(dr_ref, dqz_ref, dk_ref, dv_ref, dfl_ref, wq_ref, wkv_ref, dx_ref):
        dx_ref[...] = (ALPHA * dr_ref[...] + _dot_nt(dqz_ref[...], wq_ref[...])
                       + _dot_nt(dk_ref[...], wkv_ref[:, :dm]) + _dot_nt(dv_ref[...], wkv_ref[:, dm:2 * dm])
                       + _dot_nt(dfl_ref[...], wkv_ref[:, 2 * dm:]))

    return pl.pallas_call(
        body, name=name, grid=(S // tm,), out_shape=jax.ShapeDtypeStruct((S, D), F32),
        in_specs=[_rows(tm, D), _rows(tm, dqz.shape[1]), _rows(tm, dm), _rows(tm, dm), _rows(tm, dfl.shape[1]),
                  _full(wq.shape), _full(wkv.shape)],
        out_specs=_rows(tm, D), compiler_params=_params(),
    )(dr, dqz, dk, dv, dfl, wq, wkv)


def _mm_nn_cat(at, bs, name):
    M, R = at.shape
    widths = [b.shape[1] for b in bs]
    N = sum(widths)
    br = _pick(R, 1024)
    last = R // br - 1
    nb = len(bs)

    def body(a_ref, *refs):
        b_refs, o_ref, acc_ref = refs[:nb], refs[nb], refs[nb + 1]

        @pl.when(pl.program_id(0) == 0)
        def _():
            acc_ref[...] = jnp.zeros_like(acc_ref)

        a = a_ref[...]
        c0 = 0
        for b_ref, wd in zip(b_refs, widths):
            acc_ref[:, c0:c0 + wd] += _dot(a, b_ref[...])
            c0 += wd

        @pl.when(pl.program_id(0) == last)
        def _():
            o_ref[...] = acc_ref[...].astype(BF16)

    return pl.pallas_call(
        body, name=name, grid=(R // br,), out_shape=jax.ShapeDtypeStruct((M, N), BF16),
        in_specs=[pl.BlockSpec((M, br), lambda r: (0, r))] + [pl.BlockSpec((br, wd), lambda r: (r, 0)) for wd in widths],
        out_specs=_full((M, N)), scratch_shapes=[pltpu.VMEM((M, N), F32)], compiler_params=_params(),
    )(at, *bs)


def _split3(v):
    hi = v.astype(BF16)
    r1 = v - hi.astype(F32)
    mid = r1.astype(BF16)
    lo = (r1 - mid.astype(F32)).astype(BF16)
    return hi, mid, lo


def _tri_sum(v, tri_b):
    hi, mid, lo = _split3(v)
    return _dot(hi, tri_b) + _dot(mid, tri_b) + _dot(lo, tri_b)


def _log_sigmoid(x):
    return jnp.minimum(x, 0.0) - jnp.log(1.0 + jnp.exp(-jnp.abs(x)))


def _forget_cumsum(flt, bf, name):
    H, S = flt.shape
    nchunk = S // LANES

    def body(fl_ref, bf_ref, p_ref):
        upper = (lax.broadcasted_iota(jnp.int32, (LANES, LANES), 0)
                 <= lax.broadcasted_iota(jnp.int32, (LANES, LANES), 1)).astype(BF16)

        def step(n, carry):
            s0 = pl.multiple_of(n * LANES, LANES)
            lf = _log_sigmoid(fl_ref[:, pl.ds(s0, LANES)] + bf_ref[...])
            cs = _tri_sum(lf, upper) + carry
            for i, piece in enumerate(_split3(-cs)):
                p_ref[i, :, pl.ds(s0, LANES)] = piece.astype(F32)
            return cs[:, LANES - 1:LANES]

        lax.fori_loop(0, nchunk, step, jnp.zeros((H, 1), F32))

    return pl.pallas_call(body, name=name, out_shape=jax.ShapeDtypeStruct((3, H, S), F32),
                          compiler_params=pltpu.CompilerParams(vmem_limit_bytes=VMEM_LIMIT))(flt, bf)


def _forget_cumsum_bwd(dct, flt, bf, name):
    H, S = flt.shape
    nchunk = S // LANES

    def body(dc_ref, fl_ref, bf_ref, dfl_ref, dbf_ref):
        lower = (lax.broadcasted_iota(jnp.int32, (LANES, LANES), 0)
                 >= lax.broadcasted_iota(jnp.int32, (LANES, LANES), 1)).astype(BF16)

        def step(n, carry):
            tail, tot = carry
            s0 = pl.multiple_of((nchunk - 1 - n) * LANES, LANES)
            suffix = _tri_sum(dc_ref[:, pl.ds(s0, LANES)], lower) + tail
            xv = fl_ref[:, pl.ds(s0, LANES)] + bf_ref[...]
            dfl = suffix * (1.0 / (1.0 + jnp.exp(xv)))
            dfl_ref[:, pl.ds(s0, LANES)] = dfl
            return suffix[:, 0:1], tot + jnp.sum(dfl, axis=-1, keepdims=True)

        _, tot = lax.fori_loop(0, nchunk, step, (jnp.zeros((H, 1), F32), jnp.zeros((H, 1), F32)))
        dbf_ref[...] = jnp.broadcast_to(tot, (H, LANES))

    return pl.pallas_call(body, name=name,
                          out_shape=(jax.ShapeDtypeStruct((H, S), F32), jax.ShapeDtypeStruct((H, LANES), F32)),
                          compiler_params=pltpu.CompilerParams(vmem_limit_bytes=VMEM_LIMIT))(dct, flt, bf)


N_BIAS = 3


def _bias_ones():
    lane = lax.broadcasted_iota(jnp.int32, (1, PAIR), 1)
    return ((lane & (HEAD_DIM - 1)) < N_BIAS).astype(BF16)


def _fox_fwd(q, k, v, ca, name, comm=None):
    S, DM = q.shape
    NP = DM // PAIR
    T = min(FOX_BLOCK, S)
    nq = S // T

    def body(q_ref, k_ref, v_ref, ca_ref, o_ref, lse_ref):
        tril = _causal((T, T))
        heads = (_lane_mask(0), _lane_mask(1))
        ones3 = _bias_ones()

        def kv_block(kj):
            s0 = pl.multiple_of(kj * T, T)
            kb, vb, cab = k_ref[pl.ds(s0, T), :], v_ref[pl.ds(s0, T), :], ca_ref[pl.ds(s0, T), :]
            return ([jnp.where(heads[e], kb, cab) for e in (0, 1)],
                    [jnp.where(heads[e], vb, jnp.ones_like(vb)) for e in (0, 1)])

        def q_pair(a, _):
            t0s = [pl.multiple_of((2 * a + r) * T, T) for r in (0, 1)]
            qaug = {}
            for r in (0, 1):
                qb = q_ref[pl.ds(t0s[r], T), :] * QK_SCALE
                for e in (0, 1):
                    qaug[r, e] = jnp.where(heads[e], qb, ones3)
            chains = [(r, e) for r in (0, 1) for e in (0, 1)]

            def update(carries, first_block, seen):
                blocks = [kv_block(first_block + kk) for kk in range(1 + max(kk for s in seen for kk, _ in s))]
                out = []
                scores = [[_dot_nt(qaug[r, e], blocks[kk][0][e]) for kk, _ in seen[r]] for r, e in chains]
                for n, (r, e) in enumerate(chains):
                    m, acc = carries[n]
                    tiles = [jnp.where(tril, s, NEG) if masked else s for s, (_, masked) in zip(scores[n], seen[r])]
                    m_new = jnp.maximum(m, jnp.max(functools.reduce(jnp.maximum, tiles), axis=-1, keepdims=True))
                    acc = jnp.exp(m - m_new) * acc
                    for s, (kk, _) in zip(tiles, seen[r]):
                        acc = acc + _dot(jnp.exp(s - m_new).astype(BF16), blocks[kk][1][e])
                    out.append((m_new, acc))
                return tuple(out)

            def full(first_block, n_blocks, carries):
                return update(carries, first_block, [[(kk, False) for kk in range(n_blocks)]] * 2)

            init = (jnp.full((T, 1), NEG, F32), jnp.zeros((T, PAIR), F32))
            carries = lax.fori_loop(0, a // 2, lambda i, c: full(4 * i, 4, c), (init,) * 4)
            carries = lax.fori_loop(0, a % 2, lambda i, c: full(2 * a - 2, 2, c), carries)
            carries = update(carries, 2 * a, [[(0, True)], [(0, False), (1, True)]])
            first = _first_head()
            for r in (0, 1):
                (m0, acc0), (m1, acc1) = carries[2 * r], carries[2 * r + 1]
                l0, l1 = acc0[:, HEAD_DIM:HEAD_DIM + 1], acc1[:, 0:1]
                o_ref[pl.ds(t0s[r], T), :] = jnp.where(first, acc0 / l0, acc1 / l1).astype(BF16)
                lse_t = jnp.where(first, m0 + jnp.log(l0), m1 + jnp.log(l1)).T
                lse_ref[0:1, pl.ds(t0s[r], T)] = lse_t[0:1, :]
                lse_ref[1:2, pl.ds(t0s[r], T)] = lse_t[HEAD_DIM:HEAD_DIM + 1, :]
            return 0

        lax.fori_loop(0, nq // 2, q_pair, 0)

    col = pl.BlockSpec((S, PAIR), lambda hp: (0, hp))
    return _pcall(
        body, name=name, grid=(NP,), comm=comm,
        out_shape=(jax.ShapeDtypeStruct((S, DM), BF16), jax.ShapeDtypeStruct((NP, 2, S), F32)),
        in_specs=[col, col, col, col], out_specs=(col, pl.BlockSpec((None, 2, S), lambda hp: (hp, 0, 0))),
        args=(q, k, v, ca))


def _fox_bwd(q, k, v, o, do, ca, lse, name, comm=None):
    S, DM = q.shape
    NP = DM // PAIR
    T = min(FOX_BLOCK, S)
    nq = S // T

    def body(q_ref, k_ref, v_ref, o_ref, do_ref, ca_ref, lse_ref, dq_ref, dk_ref, dv_ref, dc_ref,
             dq_acc, dcol_acc, dk_acc, dv_acc, dd_s):
        keep = lax.broadcasted_iota(jnp.int32, (T, T), 0) <= lax.broadcasted_iota(jnp.int32, (T, T), 1)
        lane = lax.broadcasted_iota(jnp.int32, (1, LANES), 1)
        heads = (_lane_mask(0), _lane_mask(1))
        ones3 = _bias_ones()
        dq_acc[...] = jnp.zeros_like(dq_acc)
        dcol_acc[...] = jnp.zeros_like(dcol_acc)
        dc_ref[...] = jnp.zeros_like(dc_ref)

        def row_dots(qi, _):
            t0 = pl.multiple_of(qi * T, T)
            prod = do_ref[pl.ds(t0, T), :].astype(F32) * o_ref[pl.ds(t0, T), :].astype(F32)
            cols = [jnp.sum(jnp.where(heads[e], prod, 0.0), axis=-1, keepdims=True) for e in (0, 1)]
            tile = jnp.where(lane == 0, cols[0], jnp.where(lane == 1, cols[1], 0.0))
            dd_s[:, pl.ds(t0, T)] = tile.T[0:8, :]
            return 0

        lax.fori_loop(0, nq, row_dots, 0)

        def kv_pair(b, _):
            dk_acc[...] = jnp.zeros_like(dk_acc)
            dv_acc[...] = jnp.zeros_like(dv_acc)
            s0s = [pl.multiple_of((2 * b + jj) * T, T) for jj in (0, 1)]
            kaug, ks, vbs = {}, {}, []
            for jj in (0, 1):
                kb = k_ref[pl.ds(s0s[jj], T), :]
                cab = ca_ref[pl.ds(s0s[jj], T), :]
                vbs.append(v_ref[pl.ds(s0s[jj], T), :])
                for e in (0, 1):
                    kaug[jj, e] = jnp.where(heads[e], kb, cab)
                    ks[jj, e] = jnp.where(heads[e], kb, jnp.zeros_like(kb)) * QK_SCALE

            def q_step(qi, blocks):
                t0 = pl.multiple_of(qi * T, T)
                qs = q_ref[pl.ds(t0, T), :] * QK_SCALE
                dob = do_ref[pl.ds(t0, T), :]
                doh = [jnp.where(heads[e], dob, jnp.zeros_like(dob)) for e in (0, 1)]
                qh = [jnp.where(heads[e], qs, jnp.zeros_like(qs)) for e in (0, 1)]
                chains = [(e, jj, masked) for e in (0, 1) for jj, masked in blocks]
                scores = [_dot_nt(kaug[jj, e], jnp.where(heads[e], qs, ones3)) for e, jj, _ in chains]
                dps = [_dot_nt(vbs[jj], doh[e]) for e, jj, _ in chains]
                dq = jnp.zeros((T, PAIR), F32)
                drow = [jnp.zeros((1, T), F32), jnp.zeros((1, T), F32)]
                for n, (e, jj, masked) in enumerate(chains):
                    st = jnp.where(keep, scores[n], NEG) if masked else scores[n]
                    pt = jnp.exp(st - lse_ref[e:e + 1, pl.ds(t0, T)])
                    dv_acc[jj] += _dot(pt.astype(BF16), doh[e])
                    dst = pt * (dps[n] - dd_s[e:e + 1, pl.ds(t0, T)])
                    dsb = dst.astype(BF16)
                    dk_acc[jj] += _dot(dsb, qh[e])
                    dq = dq + _dot_tn(dsb, ks[jj, e])
                    dcol_acc[pl.ds(s0s[jj], T), :] -= jnp.where(lane == e, jnp.sum(dst, axis=-1, keepdims=True), 0.0)
                    drow[e] = drow[e] + jnp.sum(dst, axis=0, keepdims=True)
                for e in (0, 1):
                    dc_ref[e:e + 1, pl.ds(t0, T)] += drow[e]
                dq_acc[pl.ds(t0, T), :] += dq

            q_step(2 * b, [(0, True)])
            q_step(2 * b + 1, [(0, False), (1, True)])

            def rest(i, _):
                q_step(2 * b + 2 + 2 * i, [(0, False), (1, False)])
                q_step(2 * b + 3 + 2 * i, [(0, False), (1, False)])
                return 0

            lax.fori_loop(0, (nq - 2) // 2 - b, rest, 0)
            for jj in (0, 1):
                dk_ref[pl.ds(s0s[jj], T), :] = dk_acc[jj].astype(BF16)
                dv_ref[pl.ds(s0s[jj], T), :] = dv_acc[jj].astype(BF16)
            return 0

        lax.fori_loop(0, nq // 2, kv_pair, 0)
        dq_ref[...] = dq_acc[...].astype(BF16)

        def add_cols(kj, _):
            s0 = pl.multiple_of(kj * T, T)
            dc_ref[:, pl.ds(s0, T)] += dcol_acc[pl.ds(s0, T), :].T[0:2, :]
            return 0

        lax.fori_loop(0, nq, add_cols, 0)

    col = pl.BlockSpec((S, PAIR), lambda hp: (0, hp))
    row2 = pl.BlockSpec((None, 2, S), lambda hp: (hp, 0, 0))
    return _pcall(
        body, name=name, grid=(NP,), comm=comm,
        out_shape=(jax.ShapeDtypeStruct(do.shape, BF16), jax.ShapeDtypeStruct((S, DM), BF16),
                   jax.ShapeDtypeStruct((S, DM), BF16), jax.ShapeDtypeStruct((NP, 2, S), F32)),
        in_specs=[col, col, col, col, col, col, row2], out_specs=(col, col, col, row2),
        scratch_shapes=[pltpu.VMEM((S, PAIR), F32), pltpu.VMEM((S, LANES), F32), pltpu.VMEM((2, T, PAIR), F32),
                        pltpu.VMEM((2, T, PAIR), F32), pltpu.VMEM((8, S), F32)],
        args=(q, k, v, o, do, ca, lse), aliases={4: 0})


def _mixer_b_fwd(x, o, qm, mkv, wo, g, b, name):
    S, D = x.shape
    DQ = qm.shape[1]
    DM = D - DQ
    tm = min(512, S)

    def body(x_ref, o_ref, qm_ref, mkv_ref, wo_ref, g_ref, b_ref, xo_ref, xbt_ref, r_ref, catt_ref, cat_ref):
        cat_ref[:, :DM] = o_ref[...]
        mo = _mem_attn_fwd(qm_ref[...], mkv_ref, DQ)
        for mp in range(DQ // PAIR):
            cat_ref[:, DM + mp * PAIR:DM + (mp + 1) * PAIR] = mo[mp].astype(BF16)
        _tail_fwd_store(x_ref[...], cat_ref, wo_ref, g_ref, b_ref, xo_ref, xbt_ref, r_ref, catt_ref)

    return pl.pallas_call(
        body, name=name, grid=(S // tm,),
        out_shape=(jax.ShapeDtypeStruct((S, D), F32), jax.ShapeDtypeStruct((D, S), BF16),
                   jax.ShapeDtypeStruct((S, D), F32), jax.ShapeDtypeStruct((D, S), BF16)),
        in_specs=[_rows(tm, D), _rows(tm, DM), _rows(tm, DQ), _full(mkv.shape), _full(wo.shape), _full((1, D)),
                  _full((1, D))],
        out_specs=(_rows(tm, D), _cols(D, tm), _rows(tm, D), _cols(D, tm)),
        scratch_shapes=[pltpu.VMEM((tm, D), BF16)],
        compiler_params=_params(),
    )(x, o, qm, mkv, wo, g, b)


def _mixer_b_bwd(dxo, r, qm, mkv, wo, g, name, comm=None):
    S, D = r.shape
    DQ = qm.shape[1]
    DM = D - DQ
    M = mkv.shape[0]
    tm = min(512, S)

    def body(dxo_ref, r_ref, qm_ref, mkv_ref, wo_ref, g_ref, dr_ref, drb_ref, dcat_ref, dgb_ref, dmkv_ref):
        @pl.when(pl.program_id(0) == 0)
        def _():
            dgb_ref[...] = jnp.zeros_like(dgb_ref)
            dmkv_ref[...] = jnp.zeros_like(dmkv_ref)

        dr, dcat = _tail_bwd_head(dxo_ref, r_ref, g_ref, wo_ref, dgb_ref, drb_ref)
        dr_ref[...] = dr
        dcat_ref[:, :DM] = dcat[:, :DM].astype(BF16)
        dmo = [dcat[:, DM + mp * PAIR:DM + (mp + 1) * PAIR] for mp in range(DQ // PAIR)]
        dqm = _mem_attn_bwd(qm_ref[...], mkv_ref, dmo, dmkv_ref, DQ)
        for mp in range(DQ // PAIR):
            dcat_ref[:, DM + mp * PAIR:DM + (mp + 1) * PAIR] = dqm[mp].astype(BF16)

    return _pcall(
        body, name=name, grid=(S // tm,), comm=comm,
        out_shape=(jax.ShapeDtypeStruct((S, D), F32), jax.ShapeDtypeStruct((S, D), BF16),
                   jax.ShapeDtypeStruct((S, D), BF16), jax.ShapeDtypeStruct((2, D), F32),
                   jax.ShapeDtypeStruct((M, 2 * DQ), F32)),
        in_specs=[_rows(tm, D), _rows(tm, D), _rows(tm, DQ), _full(mkv.shape), _full(wo.shape), _full((1, D))],
        out_specs=(_rows(tm, D), _rows(tm, D), _rows(tm, D), _full((2, D)), _full((M, 2 * DQ))),
        args=(dxo, r, qm, mkv, wo, g))


class _NoExchange:
    def plan(self, stage, w, grads):
        return None

    def done(self, stage, results, w):
        pass


def _local_step(x, mem, target, w, hooks):
    S, D = x.shape
    DQ = w["mkv0"].shape[1] // 2
    DM = D - DQ
    NG = DM // HEAD_DIM
    NP = DM // PAIR
    ln_g, ln_b = w["ln_g"], w["ln_b"]
    row = lambda a, i: a[i:i + 1]
    memb = mem.astype(BF16)
    bst =jnp.pad(w["b_s"].T, ((0, 0), (0, LANES - NG)))
    grads = {}

    def staged(stage, fn, *args):
        out, got = fn(*args, stage, comm=hooks.plan(stage, w, grads))
        hooks.done(stage, got, w)
        return out

    mkv0 = _mm_small(memb, w["mkv0"], "mem_kv_0")
    x1, x1t, r1, cat0t, x0t, z0b = staged("mixer_a_fwd", _mixer_a_fwd, x, w["wa"], w["sg"], w["sb"], w["ws"], bst, mkv0,
                                w["wo0"], row(ln_g, 0), row(ln_b, 0))
    x2, x2b, x2t, r2, h0 = staged("mlp_fwd_0", _mlp_fwd, x1, w["wup0"], w["wdown0"], row(ln_g, 1), row(ln_b, 1))
    q, qm, k, v, fl = _proj_b_fwd(x2b, w["wq"], w["wkv"], DM, "proj_b_fwd")
    flt = fl[:, :16].T
    bfc = jnp.pad(w["bf"], (0, 16 - NG)).reshape(16, 1)
    pieces = _forget_cumsum(flt, bfc, "forget_cumsum")
    pieces = jnp.pad(pieces[:, :NG].transpose(2, 1, 0), ((0, 0), (0, 0), (0, HEAD_DIM - N_BIAS)))
    ca = pieces.reshape(S, NP, 2, HEAD_DIM)[:, :, ::-1].reshape(S, DM).astype(BF16)
    o, lse = staged("fox_fwd", _fox_fwd, q, k, v, ca)
    mkv1 = _mm_small(memb, w["mkv1"], "mem_kv_1")
    x3, x3t, r3, cat1t = _mixer_b_fwd(x2, o, qm, mkv1, w["wo1"], row(ln_g, 2), row(ln_b, 2), "mixer_b_fwd")
    dx4, r4, h1, loss_part = _mlp_fwd_loss(x3, w["wup1"], w["wdown1"], row(ln_g, 3), row(ln_b, 3), target,
                                           "mlp_fwd_1")

    nc = w["wup0"].shape[0]
    (dx3, dr4t, dpre1, dgb11), _ = _mlp_bwd(dx4, r4, h1, w["wup1"], w["wdown1"], row(ln_g, 3), "mlp_bwd_1")
    grads["wdown1"] = _mm_nn(dr4t, h1, "dw_down_1", out_split=nc)
    grads["wup1"] = _mm_nn(x3t, dpre1, "dw_up_1", out_split=nc)
    dr3, dr3b, dcat1, dgb10, dmkv1 = staged("mixer_b_bwd", _mixer_b_bwd, dx3, r3, qm, mkv1, w["wo1"], row(ln_g, 2))
    dqz, dk, dv, dct3 = staged("fox_bwd", _fox_bwd, q, k, v, o, dcat1, ca, lse)
    dct = jnp.pad(dct3.reshape(NG, S), ((0, 16 - NG), (0, 0)))
    dflt, dbf = _forget_cumsum_bwd(dct, flt, bfc, "forget_cumsum_bwd")
    dfl = jnp.pad(dflt.T, ((0, 0), (0, LANES - 16))).astype(BF16)
    dx2 = _proj_b_bwd(dr3, dqz, dk, dv, dfl, w["wq"], w["wkv"], "proj_b_bwd")
    grads["wo1"] = _mm_nn(cat1t, dr3b, "dw_o_1")
    grads["wq"] = _mm_nn(x2t, dqz, "dw_q")
    grads["wkv"] = _mm_nn_cat(x2t, [dk, dv, dfl], "dw_kv")
    grads["mkv1"] = _mm_tn(memb, dmkv1.astype(BF16), "dw_mkv_1")
    dx1, dr2t, dpre0, dgb01 = staged("mlp_bwd_0", _mlp_bwd, dx2, r2, h0, w["wup0"], w["wdown0"], row(ln_g, 1))
    grads["wdown0"] = _mm_nn(dr2t, h0, "dw_down_0", out_split=nc)
    grads["wup0"] = _mm_nn(x1t, dpre0, "dw_up_0", out_split=nc)
    dx0, dz, dr1b, dgb00, dsgb, dws, dbst, dmkv0 = staged(
        "mixer_a_bwd", _mixer_a_bwd, z0b, dx1, r1, w["wa"], w["sg"], w["sb"], w["ws"], bst, mkv0, w["wo0"], row(ln_g, 0))
    grads["wo0"] = _mm_nn(cat0t, dr1b, "dw_o_0")
    grads["wa"] = _mm_nn(x0t, dz, "dw_a")
    grads["mkv0"] = _mm_tn(memb, dmkv0.astype(BF16), "dw_mkv_0")
    grads.update({
        "ws": dws, "b_s": dbst[:, :NG].T, "sg": dsgb[0:1], "sb": dsgb[1:2], "bf": dbf[:NG, 0], "loss": loss_part[0:1],
        "ln_g": jnp.concatenate([dgb00[0:1], dgb01[0:1], dgb10[0:1], dgb11[0:1]], axis=0),
        "ln_b": jnp.concatenate([dgb00[1:2], dgb01[1:2], dgb10[1:2], dgb11[1:2]], axis=0),
    })
    return loss_part, dx0, grads


def _place():
    xi, yi, ci = lax.axis_index("x"), lax.axis_index("y"), lax.axis_index("c")
    peers = [(xi, 1 - yi), (1 - xi, yi), (1 - xi, 1 - yi)]
    return xi, yi, ci, peers


def _comm_call(body, name, ins, out_shapes, n_remote, n_local):
    return pl.pallas_call(
        body, name=name, out_shape=out_shapes, in_specs=[ANY] * len(ins), out_specs=[ANY] * len(out_shapes),
        scratch_shapes=[pltpu.SemaphoreType.DMA((n_remote,)), pltpu.SemaphoreType.DMA((n_remote,)),
                        pltpu.SemaphoreType.DMA((max(n_local, 1),))],
        compiler_params=pltpu.CompilerParams(has_side_effects=True),
    )(*ins)


def _pair_swap(xs):
    n = len(xs)
    out_shapes = [jax.ShapeDtypeStruct((a.shape[0],) + a.shape[2:], a.dtype) for a in xs]

    def copies(x_refs, o_refs, sems):
        send_sems, recv_sems = sems
        xi, yi, ci, _ = _place()
        return [pltpu.make_async_remote_copy(src_ref=x_refs[i].at[:, 1 - ci], dst_ref=o_refs[i],
                                             send_sem=send_sems.at[i], recv_sem=recv_sems.at[i],
                                             device_id=(xi, yi, 1 - ci), device_id_type=MESH) for i in range(n)]

    def start(x_refs, o_refs, sems):
        for cp in copies(x_refs, o_refs, sems):
            cp.start()

    def finish(x_refs, o_refs, sems):
        cps = copies(x_refs, o_refs, sems)
        for cp in cps:
            cp.wait_recv()
        for cp in cps:
            cp.wait_send()

    return _Comm(xs, out_shapes, {}, [n, n], start, finish)


def _chip_exchange(xs):
    n = len(xs)
    out_shapes = [jax.ShapeDtypeStruct((3,) + a.shape[1:], a.dtype) for a in xs]

    def copies(x_refs, o_refs, sems):
        send_sems, recv_sems = sems
        xi, yi, ci, peers = _place()
        return [pltpu.make_async_remote_copy(src_ref=x_refs[i].at[2 * px + py], dst_ref=o_refs[i].at[d],
                                             send_sem=send_sems.at[3 * i + d], recv_sem=recv_sems.at[3 * i + d],
                                             device_id=(px, py, ci), device_id_type=MESH)
                for i in range(n) for d, (px, py) in enumerate(peers)]

    def start(x_refs, o_refs, sems):
        for cp in copies(x_refs, o_refs, sems):
            cp.start()

    def finish(x_refs, o_refs, sems):
        cps = copies(x_refs, o_refs, sems)
        for cp in cps:
            cp.wait_recv()
        for cp in cps:
            cp.wait_send()

    return _Comm(xs, out_shapes, {}, [3 * n, 3 * n], start, finish)


def _inplace_call(body, name, bufs, sem_counts):
    n = len(bufs)
    return pl.pallas_call(
        body, name=name, out_shape=[jax.ShapeDtypeStruct(a.shape, a.dtype) for a in bufs],
        in_specs=[ANY] * n, out_specs=[ANY] * n, input_output_aliases={i: i for i in range(n)},
        scratch_shapes=[pltpu.SemaphoreType.DMA((k,)) for k in sem_counts],
        compiler_params=pltpu.CompilerParams(has_side_effects=True),
    )(*bufs)


def _gather_weights(bufs, whole=(), early=None):
    n, nw = len(bufs), len(whole)

    def copy(ref, s, r, k, dev):
        return pltpu.make_async_remote_copy(src_ref=ref, dst_ref=ref, send_sem=s.at[k], recv_sem=r.at[k],
                                            device_id=dev, device_id_type=MESH)

    def plan(w, sems):
        ici_s, ici_r, d2d_s, d2d_r = sems
        xi, yi, ci, peers = _place()
        xn, yn, sib = (1 - xi, yi, ci), (xi, 1 - yi, ci), (xi, yi, 1 - ci)
        j, jx, jy, jd = 2 * xi + yi, 2 * (1 - xi) + yi, 2 * xi + 1 - yi, 2 * (1 - xi) + 1 - yi
        c = {}
        for i in range(n):
            hh = bufs[i].shape[2] // 2
            lo, hi = pl.ds(0, hh), pl.ds(hh, hh)
            at = lambda slot, half, rows=None, i=i: w[i].at[slot, half] if rows is None else w[i].at[slot, half, rows]
            c["to_x", i] = (copy(at(j, ci), ici_s, ici_r, 4 * i, xn), copy(at(jx, ci), ici_s, ici_r, 4 * i, xn))
            c["to_y", i] = (copy(at(j, ci), ici_s, ici_r, 4 * i + 1, yn), copy(at(jy, ci), ici_s, ici_r, 4 * i + 1, yn))
            c["x_on", i] = (copy(at(jx, ci, lo), ici_s, ici_r, 4 * i + 2, yn), copy(at(jd, ci, lo), ici_s, ici_r, 4 * i + 2, yn))
            c["y_on", i] = (copy(at(jy, ci, hi), ici_s, ici_r, 4 * i + 3, xn), copy(at(jd, ci, hi), ici_s, ici_r, 4 * i + 3, xn))
            for k, slot in enumerate((jx, jy, jd)):
                c["sib", i, k] = (copy(at(slot, ci), d2d_s, d2d_r, 3 * i + k, sib),
                                  copy(at(slot, 1 - ci), d2d_s, d2d_r, 3 * i + k, sib))
        for i in range(nw):
            for d, (px, py) in enumerate(peers):
                k = 4 * n + 3 * i + d
                c["whole", i, d] = (copy(w[n + i].at[j], ici_s, ici_r, k, (px, py, ci)),
                                    copy(w[n + i].at[2 * px + py], ici_s, ici_r, k, (px, py, ci)))
        return c

    def start(_, w, sems):
        c = plan(w, sems)
        for i in range(n):
            c["to_x", i][0].start()
            c["to_y", i][0].start()
        for i in range(nw):
            for d in range(3):
                c["whole", i, d][0].start()

    def pass_on(c, which):
        for i in which:
            c["to_x", i][1].wait_recv()
            c["x_on", i][0].start()
            c["sib", i, 0][0].start()
            c["to_y", i][1].wait_recv()
            c["y_on", i][0].start()
            c["sib", i, 1][0].start()

    n_early = n if early is None else min(early, n)

    def middle(_, w, sems):
        pass_on(plan(w, sems), range(n_early))

    def finish(_, w, sems):
        c = plan(w, sems)
        pass_on(c, range(n_early, n))
        for i in range(n):
            c["x_on", i][1].wait_recv()
            c["y_on", i][1].wait_recv()
            c["sib", i, 2][0].start()
        for i in range(n):
            for k in range(3):
                c["sib", i, k][1].wait_recv()
        for i in range(nw):
            for d in range(3):
                c["whole", i, d][1].wait_recv()
        for key, (sent, _) in c.items():
            sent.wait_send()

    arrays = list(bufs) + list(whole)
    k = 4 * n + 3 * nw
    return _Comm(arrays, [jax.ShapeDtypeStruct(a.shape, a.dtype) for a in arrays], {i: i for i in range(n + nw)},
                 [k, k, max(3 * n, 1), max(3 * n, 1)], start, finish, middle)


def _pair_gather(bufs, spread, name):
    n = len(bufs)
    where = [(i, l) for i, a in enumerate(bufs) for l in range(a.shape[0])]
    nw = len(where)
    arrays = list(bufs) + ([] if spread is None else [spread])
    na = len(arrays)

    def body(*refs):
        g = refs[na:na + n]
        sp = refs[na + n] if spread is not None else None
        send_sems, recv_sems = refs[2 * na:]
        xi, yi, ci, peers = _place()
        chips = [(xi, yi)] + peers
        others = [(d, f) for d in range(4) for f in (0, 1) if (d, f) != (0, 0)] if spread is not None else []
        sends = []
        for k, (i, l) in enumerate(where):
            mine = g[i].at[l, :, ci]
            cp = pltpu.make_async_remote_copy(src_ref=mine, dst_ref=mine, send_sem=send_sems.at[k],
                                              recv_sem=recv_sems.at[k], device_id=(xi, yi, 1 - ci), device_id_type=MESH)
            cp.start()
            sends.append(cp)
        for k, (d, f) in enumerate(others):
            mine = sp.at[2 * xi + yi, ci]
            cp = pltpu.make_async_remote_copy(src_ref=mine, dst_ref=mine, send_sem=send_sems.at[nw + k],
                                              recv_sem=recv_sems.at[nw + k],
                                              device_id=(chips[d][0], chips[d][1], ci if f == 0 else 1 - ci),
                                              device_id_type=MESH)
            cp.start()
            sends.append(cp)
        for k, (i, l) in enumerate(where):
            other = g[i].at[l, :, 1 - ci]
            pltpu.make_async_remote_copy(src_ref=other, dst_ref=other, send_sem=send_sems.at[k],
                                         recv_sem=recv_sems.at[k], device_id=(xi, yi, 1 - ci),
                                         device_id_type=MESH).wait_recv()
        for k, (d, f) in enumerate(others):
            px, py, pc = chips[d][0], chips[d][1], (ci if f == 0 else 1 - ci)
            theirs = sp.at[2 * px + py, pc]
            pltpu.make_async_remote_copy(src_ref=theirs, dst_ref=theirs, send_sem=send_sems.at[nw + k],
                                         recv_sem=recv_sems.at[nw + k], device_id=(px, py, pc),
                                         device_id_type=MESH).wait_recv()
        for cp in sends:
            cp.wait_send()

    n_sems = nw + (7 if spread is not None else 0)
    out = _inplace_call(body, name, arrays, [n_sems, n_sems])
    return out[:n], (out[n] if spread is not None else None)


def _row_block(rows, cols):
    want = max(8, (2 ** 18 // max(cols, 1)) // 8 * 8)
    if rows <= want:
        return rows
    best = 8
    for t in range(8, want + 1, 8):
        if rows % t == 0:
            best = t
    return best


def _pair_add(g4, recv, sel, out_dtype, name):
    A, _, H, C = g4.shape
    bh = _row_block(H, C)

    def body(sel_ref, g_ref, r_ref, o_ref):
        o_ref[...] = (g_ref[...].astype(F32) + r_ref[...].astype(F32)).astype(out_dtype)

    return pl.pallas_call(
        body, name=name, out_shape=jax.ShapeDtypeStruct((A, H, C), out_dtype),
        grid_spec=pltpu.PrefetchScalarGridSpec(
            num_scalar_prefetch=1, grid=(A, H // bh),
            in_specs=[pl.BlockSpec((None, None, bh, C), lambda a, i, s: (a, s[1], i, 0)),
                      pl.BlockSpec((None, bh, C), lambda a, i, s: (a, i, 0))],
            out_specs=pl.BlockSpec((None, bh, C), lambda a, i, s: (a, i, 0))),
        compiler_params=pltpu.CompilerParams(dimension_semantics=("arbitrary", "arbitrary"),
                                             vmem_limit_bytes=VMEM_LIMIT),
    )(sel, g4, recv)


def _chip_reduce(g4, recv1, recv2, sel, buf, layer, n_layers, name, by_chip=False):
    _, _, H, C = g4.shape
    bh = _row_block(H, C)
    if by_chip:
        out_shape = jax.ShapeDtypeStruct((N_CHIPS, 2, H, C), F32)
        out_spec = pl.BlockSpec((None, None, bh, C), lambda i, s: (s[0], s[1], i, 0))
    else:
        out_shape = jax.ShapeDtypeStruct((n_layers, 1, 2, H, C), F32)
        out_spec = pl.BlockSpec((None, None, None, bh, C), lambda i, s: (layer, 0, s[1], i, 0))

    def body(sel_ref, g_ref, r1_ref, r2_ref, *rest):
        acc = g_ref[...].astype(F32) + r1_ref[...].astype(F32)
        for d in range(3):
            acc = acc + r2_ref[d].astype(F32)
        rest[-1][...] = acc

    in_specs = [pl.BlockSpec((None, None, bh, C), lambda i, s: (s[0], s[1], i, 0)),
                pl.BlockSpec((None, bh, C), lambda i, s: (s[0], i, 0)),
                pl.BlockSpec((3, bh, C), lambda i, s: (0, i, 0))]
    args = [sel, g4, recv1, recv2]
    aliases = {}
    if buf is not None:
        in_specs.append(ANY)
        args.append(buf)
        aliases = {4: 0}
    return pl.pallas_call(
        body, name=name, out_shape=out_shape,
        grid_spec=pltpu.PrefetchScalarGridSpec(num_scalar_prefetch=1, grid=(H // bh,), in_specs=in_specs,
                                               out_specs=out_spec),
        input_output_aliases=aliases,
        compiler_params=pltpu.CompilerParams(dimension_semantics=("arbitrary",), vmem_limit_bytes=VMEM_LIMIT),
    )(*args)


def _chip_reduce_many(groups, sel, comm, name):
    flat = [t for grp in groups for t in grp]
    n, ng = len(flat), len(groups)
    _, _, H, C = flat[0][0].shape
    bh = _row_block(H, C)
    last = H // bh - 1
    nci, nco = len(comm.ins), len(comm.out_shapes)

    def body(sel_ref, *refs):
        ins, refs = refs[:3 * n], refs[3 * n:]
        cins, refs = refs[:nci], refs[nci:]
        outs, refs = refs[:ng], refs[ng:]
        couts, sems = refs[:nco], refs[nco:]

        @pl.when(pl.program_id(0) == 0)
        def _():
            comm.start(cins, couts, sems)

        k = 0
        for gi, grp in enumerate(groups):
            for layer in range(len(grp)):
                g_ref, r1_ref, r2_ref = ins[3 * k:3 * k + 3]
                acc = g_ref[...].astype(F32) + r1_ref[...].astype(F32)
                for d in range(3):
                    acc = acc + r2_ref[d].astype(F32)
                outs[gi][layer] = acc
                k += 1

        @pl.when(pl.program_id(0) == last)
        def _():
            if comm.middle is not None:
                comm.middle(cins, couts, sems)
            comm.finish(cins, couts, sems)

    in_specs = [pl.BlockSpec((None, None, bh, C), lambda i, s: (s[0], s[1], i, 0)),
                pl.BlockSpec((None, bh, C), lambda i, s: (s[0], i, 0)),
                pl.BlockSpec((3, bh, C), lambda i, s: (0, i, 0))] * n + [ANY] * nci
    out_specs = [pl.BlockSpec((len(grp), None, None, bh, C), lambda i, s: (0, 0, s[1], i, 0)) for grp in groups]
    res = pl.pallas_call(
        body, name=name,
        out_shape=[jax.ShapeDtypeStruct((len(grp), 1, 2, H, C), F32) for grp in groups] + list(comm.out_shapes),
        grid_spec=pltpu.PrefetchScalarGridSpec(
            num_scalar_prefetch=1, grid=(H // bh,), in_specs=in_specs, out_specs=out_specs + [ANY] * nco,
            scratch_shapes=[pltpu.SemaphoreType.DMA((k,)) for k in comm.sem_counts]),
        input_output_aliases={1 + 3 * n + a: ng + b for a, b in comm.aliases.items()},
        compiler_params=pltpu.CompilerParams(dimension_semantics=("arbitrary",), vmem_limit_bytes=VMEM_LIMIT,
                                             has_side_effects=True),
    )(sel, *[a for t in flat for a in t], *comm.ins)
    return res[:ng], res[ng:]


def _cast_place(shards, layer, sel, name):
    _, _, H, C = shards.shape
    bh = _row_block(H, C)

    def body(sel_ref, x_ref, o_ref):
        o_ref[...] = x_ref[...].astype(BF16)

    return pl.pallas_call(
        body, name=name, out_shape=jax.ShapeDtypeStruct((N_CHIPS, 2, H, C), BF16),
        grid_spec=pltpu.PrefetchScalarGridSpec(
            num_scalar_prefetch=1, grid=(2, H // bh),
            in_specs=[pl.BlockSpec((None, None, bh, C), lambda h, i, s: (layer, h, i, 0))],
            out_specs=pl.BlockSpec((None, None, bh, C), lambda h, i, s: (s[0], h, i, 0))),
        compiler_params=pltpu.CompilerParams(dimension_semantics=("arbitrary", "arbitrary"),
                                             vmem_limit_bytes=VMEM_LIMIT),
    )(sel, shards)


def _cast_place_many(pairs, sel, comm, name):
    n = len(pairs)
    _, _, H, C = pairs[0][0].shape
    bh = _row_block(H, C)
    last = H // bh - 1
    nci, nco = len(comm.ins), len(comm.out_shapes)

    def body(sel_ref, *refs):
        xs, refs = refs[:n], refs[n:]
        cins, refs = refs[:nci], refs[nci:]
        outs, refs = refs[:n], refs[n:]
        couts, sems = refs[:nco], refs[nco:]
        h, i = pl.program_id(0), pl.program_id(1)

        @pl.when((h == 0) & (i == 0))
        def _():
            comm.start(cins, couts, sems)

        if comm.middle is not None:
            @pl.when((h == 1) & (i == 0))
            def _():
                comm.middle(cins, couts, sems)

        for x_ref, o_ref in zip(xs, outs):
            o_ref[...] = x_ref[...].astype(BF16)

        @pl.when((h == 1) & (i == last))
        def _():
            comm.finish(cins, couts, sems)

    res = pl.pallas_call(
        body, name=name,
        out_shape=[jax.ShapeDtypeStruct((N_CHIPS, 2, H, C), BF16)] * n + list(comm.out_shapes),
        grid_spec=pltpu.PrefetchScalarGridSpec(
            num_scalar_prefetch=1, grid=(2, H // bh),
            in_specs=[pl.BlockSpec((None, None, bh, C), lambda h, i, s, layer=layer: (layer, h, i, 0))
                      for _, layer in pairs] + [ANY] * nci,
            out_specs=[pl.BlockSpec((None, None, bh, C), lambda h, i, s: (s[0], h, i, 0))] * n + [ANY] * nco,
            scratch_shapes=[pltpu.SemaphoreType.DMA((k,)) for k in comm.sem_counts]),
        input_output_aliases={1 + n + a: n + b for a, b in comm.aliases.items()},
        compiler_params=pltpu.CompilerParams(dimension_semantics=("arbitrary", "arbitrary"),
                                             vmem_limit_bytes=VMEM_LIMIT, has_side_effects=True),
    )(sel, *[s for s, _ in pairs], *comm.ins)
    return res[:n], res[n:]


def _adamw_many(quads, name, comm=None):
    R, C = quads[0][0].shape
    br = _row_block(R, C)
    n = len(quads)

    def body(*refs):
        for k in range(n):
            g_ref, w_ref, m_ref, v_ref = refs[4 * k:4 * k + 4]
            d_ref, mo_ref, vo_ref = refs[4 * n + 3 * k:4 * n + 3 * k + 3]
            gv = g_ref[...]
            mn = ADAM_B1 * m_ref[...] + (1.0 - ADAM_B1) * gv
            vn = ADAM_B2 * v_ref[...] + (1.0 - ADAM_B2) * (gv * gv)
            m_hat = mn / (1.0 - ADAM_B1 ** ADAM_STEP)
            v_hat = vn / (1.0 - ADAM_B2 ** ADAM_STEP)
            d_ref[...] = -ADAM_LR * (m_hat / (jnp.sqrt(v_hat) + ADAM_EPS) + ADAM_WD * w_ref[...])
            mo_ref[...] = mn
            vo_ref[...] = vn

    spec = pl.BlockSpec((br, C), lambda i: (i, 0))
    res, got = _pcall(body, name=name, grid=(R // br,), comm=comm, out_shape=(jax.ShapeDtypeStruct((R, C), F32),) * (3 * n),
                      in_specs=[spec] * (4 * n), out_specs=(spec,) * (3 * n), args=[a for q in quads for a in q])
    return [res[3 * k:3 * k + 3] for k in range(n)], got


def _adamw(g, w, m, v, name):
    return _adamw_many([(g, w, m, v)], name)[0][0]


def _part_rows(shape):
    return -(-math.prod(shape) // (8 * LANES)) * 8


def _pack_rows(arrs, total_rows):
    parts = []
    for a in arrs:
        flat = a.reshape(-1)
        rows = _part_rows(a.shape)
        parts.append(jnp.pad(flat, (0, rows * LANES - flat.shape[0])).reshape(rows, LANES))
    packed = jnp.concatenate(parts, axis=0)
    return jnp.pad(packed, ((0, total_rows - packed.shape[0]), (0, 0)))


def _unpack_rows(packed, shapes):
    out, r = [], 0
    for shp in shapes:
        size, rows = math.prod(shp), _part_rows(shp)
        out.append(packed[r:r + rows].reshape(-1)[:size].reshape(shp))
        r += rows
    return out


def _rows_of(shapes):
    return sum(_part_rows(s) for s in shapes)


_GATHER_STAGES = {"start": ["wa", "mkv0", "wo0"], "mixer_a_fwd": ["wup0", "wdown0"],
                  "mlp_fwd_0": ["wq", "wkv", "mkv1", "wo1"], "fox_fwd": ["wup1", "wdown1"]}
_SWAP_STAGES = {"mixer_b_bwd": ["wup1", "wdown1"]}
_REDUCE_STAGES = {"fox_bwd": ["wup1", "wdown1"], "mlp_bwd_0": ["wo1", "wq", "wkv", "mkv1"],
                  "mixer_a_bwd": ["wup0", "wdown0"], "end": ["wa", "wo0", "mkv0", "small"]}
_SMALL = ["ws", "b_s", "sg", "sb", "bf", "ln_g", "ln_b", "loss"]


class _Staged:
    def __init__(self, placed, sel, dm, ng, shard_cols, n_small):
        self.placed, self.sel, self.dm, self.ng, self.shard_cols, self.n_small = placed, sel, dm, ng, shard_cols, n_small
        self.g4, self.recv1, self.recv2 = {}, {}, {}

    def _weight(self, name, a):
        a = a.reshape(N_CHIPS, 2 * a.shape[2], a.shape[3])
        if name in ("wa", "wkv"):
            a = a.transpose(1, 0, 2).reshape(a.shape[1], -1)
            if name == "wkv":
                a = jnp.concatenate([a[:, :2 * self.dm],
                                     jnp.pad(a[:, 2 * self.dm:], ((0, 0), (0, LANES - self.ng)))], axis=1)
            return a
        if name.startswith("wup") or name.startswith("wdown"):
            return a
        return a.reshape(-1, a.shape[2])

    def _partial(self, name, grads):
        if name == "small":
            a = _pack_rows([grads[n] for n in _SMALL], self.n_small).reshape(N_CHIPS, -1, LANES)
        elif name in self.shard_cols:
            n = self.shard_cols[name]
            g = grads[name][:, :N_CHIPS * n]
            a = g.reshape(g.shape[0], N_CHIPS, n).transpose(1, 0, 2)
        elif name.startswith("wup") or name.startswith("wdown"):
            a = grads[name]
        else:
            g = grads[name]
            a = g.reshape(N_CHIPS, g.shape[0] // N_CHIPS, g.shape[1])
        return a.reshape(N_CHIPS, 2, a.shape[1] // 2, a.shape[2])

    def plan(self, stage, w, grads):
        if stage in _GATHER_STAGES:
            return _gather_weights([self.placed[n] for n in _GATHER_STAGES[stage]],
                                   early=1 if stage == "mixer_a_fwd" else None)
        if stage in _SWAP_STAGES:
            for n in _SWAP_STAGES[stage]:
                self.g4[n] = self._partial(n, grads)
            return _pair_swap([self.g4[n] for n in _SWAP_STAGES[stage]])
        if stage in _REDUCE_STAGES:
            names = _REDUCE_STAGES[stage]
            late = [n for n in names if n not in self.recv1]
            for n in late:
                self.g4[n] = self._partial(n, grads)
            if late:
                got = _run_comm(_pair_swap([self.g4[n] for n in late]), "reduce_pair_swap_" + stage)
                self.recv1.update(zip(late, got))
            return _chip_exchange([_pair_add(self.g4[n], self.recv1[n], self.sel, F32 if n == "small" else BF16,
                                             "reduce_pair_add_" + n) for n in names])
        return None

    def done(self, stage, results, w):
        if stage in _GATHER_STAGES:
            for n, a in zip(_GATHER_STAGES[stage], results):
                w[n] = self._weight(n, a)
        elif stage in _SWAP_STAGES:
            self.recv1.update(zip(_SWAP_STAGES[stage], results))
        elif stage in _REDUCE_STAGES:
            self.recv2.update(zip(_REDUCE_STAGES[stage], results))


def kernel(x, mem, a_w_in, a_sgu_ln_g, a_sgu_ln_b, a_w_s, a_b_s, kv_w, kv_b_f, b_w_q, mem_w_kv, w_o, ln_g, ln_b, w_up, w_down, loss_target, m_a_w_in, m_a_sgu_ln_g, m_a_sgu_ln_b, m_a_w_s, m_a_b_s, m_kv_w, m_kv_b_f, m_b_w_q, m_mem_w_kv, m_w_o, m_ln_g, m_ln_b, m_w_up, m_w_down, v_a_w_in, v_a_sgu_ln_g, v_a_sgu_ln_b, v_a_w_s, v_a_b_s, v_kv_w, v_kv_b_f, v_b_w_q, v_mem_w_kv, v_w_o, v_ln_g, v_ln_b, v_w_up, v_w_down):
    xi, yi, ci = lax.axis_index("x"), lax.axis_index("y"), lax.axis_index("c")
    chip = 2 * xi + yi
    sel = jnp.stack([chip, ci]).astype(jnp.int32)
    S, D = x.shape[1], x.shape[2]
    DQ = mem_w_kv.shape[2] // 2
    DM = D - DQ
    NG = DM // HEAD_DIM

    shards = {"wa": (a_w_in, 0), "wkv": (kv_w[None], 0), "wq": (b_w_q, 0), "mkv0": (mem_w_kv, 0), "mkv1": (mem_w_kv, 1),
              "wo0": (w_o, 0), "wo1": (w_o, 1), "wup0": (w_up, 0), "wup1": (w_up, 1), "wdown0": (w_down, 0),
              "wdown1": (w_down, 1)}
    halves = lambda s: s.reshape(s.shape[0], 2, s.shape[1] // 2, s.shape[2])
    mlp = ["wup0", "wup1", "wdown0", "wdown1"]
    placed = {n: _cast_place(halves(s), layer, sel, "cast_place_" + n)
              for n, (s, layer) in shards.items() if n not in mlp}
    ln_pack = jnp.concatenate([ln_g.reshape(4, -1), ln_b.reshape(4, -1)], axis=0)
    ln_buf = lax.dynamic_update_slice(jnp.zeros((N_CHIPS,) + ln_pack.shape, F32), ln_pack[None], (chip, 0, 0))
    mlp_placed, got = _cast_place_many(
        [(halves(shards[n][0]), shards[n][1]) for n in mlp], sel,
        _gather_weights([placed[n] for n in _GATHER_STAGES["start"]], whole=[ln_buf]), "cast_place_mlp")
    placed.update(zip(mlp, mlp_placed))
    ln_shape = (4, D)
    small_shapes = [a_w_s.shape, a_b_s.shape, a_sgu_ln_g.shape, a_sgu_ln_b.shape, kv_b_f.shape, ln_shape, ln_shape,
                    (1, LANES)]
    n_small = -(-_rows_of(small_shapes) // 64) * 64
    hooks = _Staged(placed, sel, DM, NG, {"wa": a_w_in.shape[2], "wkv": kv_w.shape[1]}, n_small)
    weights = {"sg": a_sgu_ln_g, "sb": a_sgu_ln_b, "ws": a_w_s[0], "b_s": a_b_s[0], "bf": kv_b_f}
    hooks.done("start", got[:-1], weights)
    ln_full = got[-1].transpose(1, 0, 2).reshape(8, D)
    weights["ln_g"], weights["ln_b"] = ln_full[:4], ln_full[4:]

    _, grad_x, gr = _local_step(x[0], mem[0], loss_target[0], weights, hooks)

    def chip_sums(groups):
        bufs = []
        for grp in groups:
            buf = None
            for layer, n in enumerate(grp):
                buf = _chip_reduce(hooks.g4[n], hooks.recv1[n], hooks.recv2[n], sel, buf, layer, len(grp),
                                   "reduce_chip_sum_" + n)
            bufs.append(buf)
        return bufs

    whole = lambda a: a.reshape(a.shape[0], 2 * a.shape[3], a.shape[4])
    flat = lambda a: a.reshape(-1, a.shape[-1])

    def update(name, g, w, m, v):
        d, mn, vn = _adamw(flat(g), flat(w), flat(m), flat(v), "adamw_" + name)
        return d.reshape(w.shape), mn.reshape(w.shape), vn.reshape(w.shape)

    parts = lambda n: (hooks.g4[n], hooks.recv1[n], hooks.recv2[n])
    mlp_sums, got = _chip_reduce_many([[parts("wup0"), parts("wup1")], [parts("wdown0"), parts("wdown1")]], sel,
                                      hooks.plan("end", weights, gr), "reduce_chip_sum_mlp")
    hooks.done("end", got, weights)
    small_mine = _chip_reduce(hooks.g4["small"], hooks.recv1["small"], hooks.recv2["small"], sel, None, 0, 1,
                              "reduce_chip_sum_small", by_chip=True)
    red, small_all = _pair_gather(chip_sums([["wa"], ["wkv"], ["wq"], ["mkv0", "mkv1"], ["wo0", "wo1"]])
                                  + list(mlp_sums), small_mine, "reduce_pair_gather")
    g_a_w_in, g_kv_w, g_b_w_q, g_mem_w_kv, g_w_o, g_w_up, g_w_down = [whole(a) for a in red]
    g_kv_w = g_kv_w[0]
    g_w_down = g_w_down.transpose(0, 2, 1)
    small_all = small_all.reshape(n_small, LANES)
    g_ws, g_bs, g_sg, g_sb, g_bf, g_lng, g_lnb, loss_row = _unpack_rows(small_all, small_shapes)
    loss = loss_row[0, 0]
    dsh = D // N_CHIPS
    g_ln_g = lax.dynamic_slice_in_dim(g_lng, chip * dsh, dsh, axis=1).reshape(ln_g.shape)
    g_ln_b = lax.dynamic_slice_in_dim(g_lnb, chip * dsh, dsh, axis=1).reshape(ln_b.shape)

    upd = {
        "a_w_in": update("a_w_in", g_a_w_in, a_w_in, m_a_w_in, v_a_w_in),
        "kv_w": update("kv_w", g_kv_w, kv_w, m_kv_w, v_kv_w),
        "b_w_q": update("b_w_q", g_b_w_q, b_w_q, m_b_w_q, v_b_w_q),
        "mem_w_kv": update("mem_w_kv", g_mem_w_kv, mem_w_kv, m_mem_w_kv, v_mem_w_kv),
        "w_o": update("w_o", g_w_o, w_o, m_w_o, v_w_o),
        "w_up": update("w_up", g_w_up, w_up, m_w_up, v_w_up),
        "w_down": update("w_down", g_w_down, w_down, m_w_down, v_w_down),
    }
    tiny_g = [g_sg, g_sb, g_ws, g_bs, g_bf, g_ln_g, g_ln_b]
    tiny_w = [a_sgu_ln_g, a_sgu_ln_b, a_w_s, a_b_s, kv_b_f, ln_g, ln_b]
    tiny_m = [m_a_sgu_ln_g, m_a_sgu_ln_b, m_a_w_s, m_a_b_s, m_kv_b_f, m_ln_g, m_ln_b]
    tiny_v = [v_a_sgu_ln_g, v_a_sgu_ln_b, v_a_w_s, v_a_b_s, v_kv_b_f, v_ln_g, v_ln_b]
    tiny_shapes = [a.shape for a in tiny_w]
    n_tiny = -(-_rows_of(tiny_shapes) // 8) * 8
    td, tm_, tv = _adamw(_pack_rows(tiny_g, n_tiny), _pack_rows(tiny_w, n_tiny), _pack_rows(tiny_m, n_tiny),
                         _pack_rows(tiny_v, n_tiny), "adamw_small")
    for name, d, mn, vn in zip(["a_sgu_ln_g", "a_sgu_ln_b", "a_w_s", "a_b_s", "kv_b_f", "ln_g", "ln_b"],
                               _unpack_rows(td, tiny_shapes), _unpack_rows(tm_, tiny_shapes),
                               _unpack_rows(tv, tiny_shapes)):
        upd[name] = (d, mn, vn)

    order = ["a_w_in", "a_sgu_ln_g", "a_sgu_ln_b", "a_w_s", "a_b_s", "kv_w", "kv_b_f", "b_w_q", "mem_w_kv", "w_o",
             "ln_g", "ln_b", "w_up", "w_down"]
    grads = {"a_w_in": g_a_w_in.reshape(a_w_in.shape), "a_sgu_ln_g": g_sg.reshape(a_sgu_ln_g.shape),
             "a_sgu_ln_b": g_sb.reshape(a_sgu_ln_b.shape), "a_w_s": g_ws.reshape(a_w_s.shape),
             "a_b_s": g_bs.reshape(a_b_s.shape), "kv_w": g_kv_w, "kv_b_f": g_bf, "b_w_q": g_b_w_q.reshape(b_w_q.shape),
             "mem_w_kv": g_mem_w_kv, "w_o": g_w_o, "ln_g": g_ln_g, "ln_b": g_ln_b, "w_up": g_w_up, "w_down": g_w_down}
    return (loss, grad_x[None], *[grads[n] for n in order], *[upd[n][0] for n in order],
            *[upd[n][1] for n in order], *[upd[n][2] for n in order])
```

```python
import functools
import math

import jax
import jax.numpy as jnp
from jax import lax
from jax.experimental import pallas as pl
from jax.experimental.pallas import tpu as pltpu

F32 = jnp.float32
BF16 = jnp.bfloat16

HEAD_DIM = 64
PAIR = 2 * HEAD_DIM
CHUNK = 128
LN_EPS = 1e-5
ALPHA = 4 ** 0.25
QK_SCALE = 1.0 / math.sqrt(HEAD_DIM)
NEG = -1e30
N_CHIPS = 4
FOX_BLOCK = 256
LANES = 128
VMEM_LIMIT = 48 * 2 ** 20
VMEM_LIMIT_MLP_BWD = 56 * 2 ** 20

ADAM_LR, ADAM_B1, ADAM_B2, ADAM_EPS, ADAM_WD, ADAM_STEP = 0.001, 0.9, 0.999, 1e-08, 0.01, 10

MESH = pl.DeviceIdType.MESH
ANY = pl.BlockSpec(memory_space=pl.ANY)


def _dot(a, b):
    return jnp.dot(a, b, preferred_element_type=F32)


def _dot_nt(a, b):
    return lax.dot_general(a, b, (((1,), (1,)), ((), ())), preferred_element_type=F32)


def _dot_tn(a, b):
    return lax.dot_general(a, b, (((0,), (0,)), ((), ())), preferred_element_type=F32)


def _ln_stats(r):
    mu = jnp.mean(r, axis=-1, keepdims=True)
    d = r - mu
    var = jnp.mean(d * d, axis=-1, keepdims=True)
    rstd = lax.rsqrt(var + LN_EPS)
    return d * rstd, rstd


def _ln_bwd(dy, xhat, rstd, g):
    dxh = dy * g
    m1 = jnp.mean(dxh, axis=-1, keepdims=True)
    m2 = jnp.mean(dxh * xhat, axis=-1, keepdims=True)
    return rstd * (dxh - m1 - xhat * m2)


_GELU_K = math.sqrt(2.0 / math.pi)


def _gelu(x):
    return 0.5 * x * (1.0 + jnp.tanh(_GELU_K * (x + 0.044715 * x * x * x)))


def _gelu_grad(x):
    t = jnp.tanh(_GELU_K * (x + 0.044715 * x * x * x))
    return 0.5 * (1.0 + t) + 0.5 * x * (1.0 - t * t) * _GELU_K * (1.0 + 3 * 0.044715 * x * x)


def _lane_mask(e):
    lane = lax.broadcasted_iota(jnp.int32, (1, PAIR), 1)
    return (lane >= HEAD_DIM * e) & (lane < HEAD_DIM * (e + 1))


def _first_head():
    return lax.broadcasted_iota(jnp.int32, (1, PAIR), 1) < HEAD_DIM


def _mem_probs(qh, mkp):
    sc = _dot_nt(qh, mkp) * QK_SCALE
    ex = jnp.exp(sc - jnp.max(sc, axis=-1, keepdims=True))
    return ex / jnp.sum(ex, axis=-1, keepdims=True)


def _mem_attn_fwd(qm_b, mkv_ref, dq_dim):
    outs = []
    for mp in range(dq_dim // PAIR):
        qp = qm_b[:, mp * PAIR:(mp + 1) * PAIR]
        mkp = mkv_ref[:, mp * PAIR:(mp + 1) * PAIR]
        mvp = mkv_ref[:, dq_dim + mp * PAIR:dq_dim + (mp + 1) * PAIR]
        heads = []
        for e in (0, 1):
            qh = jnp.where(_lane_mask(e), qp, jnp.zeros_like(qp))
            p = _mem_probs(qh, mkp)
            heads.append(_dot(p.astype(BF16), mvp))
        outs.append(jnp.where(_first_head(), heads[0], heads[1]))
    return outs


def _mem_attn_bwd(qm_b, mkv_ref, dmo_pairs, dmkv_ref, dq_dim):
    dqs = []
    for mp in range(dq_dim // PAIR):
        ks = slice(mp * PAIR, (mp + 1) * PAIR)
        vs = slice(dq_dim + mp * PAIR, dq_dim + (mp + 1) * PAIR)
        qp = qm_b[:, ks]
        mkp = mkv_ref[:, ks]
        mvp = mkv_ref[:, vs]
        dmo_b = dmo_pairs[mp].astype(BF16)
        dq = None
        dmk = None
        dmv = None
        for e in (0, 1):
            hm = _lane_mask(e)
            qh = jnp.where(hm, qp, jnp.zeros_like(qp))
            p = _mem_probs(qh, mkp)
            doh = jnp.where(hm, dmo_b, jnp.zeros_like(dmo_b))
            dp = _dot_nt(doh, mvp)
            ds = p * (dp - jnp.sum(dp * p, axis=-1, keepdims=True))
            dsb = (ds * QK_SCALE).astype(BF16)
            kh = jnp.where(hm, mkp, jnp.zeros_like(mkp))
            dq_e = _dot(dsb, kh)
            dmk_e = _dot_tn(dsb, qh)
            dmv_e = _dot_tn(p.astype(BF16), doh)
            dq = dq_e if dq is None else dq + dq_e
            dmk = dmk_e if dmk is None else dmk + dmk_e
            dmv = dmv_e if dmv is None else dmv + dmv_e
        dmkv_ref[:, ks] += dmk
        dmkv_ref[:, vs] += dmv
        dqs.append(dq)
    return dqs


def _params(n_axes=1):
    return pltpu.CompilerParams(dimension_semantics=("arbitrary",) * n_axes, vmem_limit_bytes=VMEM_LIMIT)


def _full(shape):
    return pl.BlockSpec(shape, lambda *_: (0,) * len(shape))


def _rows(tm, cols):
    return pl.BlockSpec((tm, cols), lambda i: (i, 0))


def _cols(rows, tm):
    return pl.BlockSpec((rows, tm), lambda i: (0, i))


def _pick(n, pref):
    if n <= pref:
        return n
    best = LANES
    for t in range(LANES, pref + 1, LANES):
        if n % t == 0:
            best = t
    return best if 2 * best >= pref or n > 2 * pref else n


class _Comm:
    def __init__(self, ins, out_shapes, aliases, sem_counts, start, finish, middle=None):
        self.ins, self.out_shapes, self.aliases, self.sem_counts = list(ins), list(out_shapes), dict(aliases), sem_counts
        self.start, self.finish = start, finish
        self.middle = middle


def _pcall(body, *, name, grid, in_specs, out_specs, out_shape, args, scratch_shapes=(), comm=None, aliases=None,
           vmem_limit=VMEM_LIMIT):
    n_in, n_out, n_scr = len(in_specs), len(out_shape), len(scratch_shapes)
    aliases = dict(aliases or {})
    if comm is None:
        res = pl.pallas_call(body, name=name, grid=grid, out_shape=tuple(out_shape), in_specs=list(in_specs),
                             out_specs=tuple(out_specs), scratch_shapes=list(scratch_shapes),
                             input_output_aliases=aliases,
                             compiler_params=pltpu.CompilerParams(dimension_semantics=("arbitrary",),
                                                                  vmem_limit_bytes=vmem_limit))(*args)
        return tuple(res), ()
    nci, nco = len(comm.ins), len(comm.out_shapes)
    last = grid[0] - 1

    def wrapped(*refs):
        ins, refs = refs[:n_in], refs[n_in:]
        cins, refs = refs[:nci], refs[nci:]
        outs, refs = refs[:n_out], refs[n_out:]
        couts, refs = refs[:nco], refs[nco:]
        scr, sems = refs[:n_scr], refs[n_scr:]

        @pl.when(pl.program_id(0) == 0)
        def _():
            comm.start(cins, couts, sems)

        if comm.middle is not None:
            @pl.when(pl.program_id(0) == (last + 1) // 2)
            def _():
                comm.middle(cins, couts, sems)

        body(*ins, *outs, *scr)

        @pl.when(pl.program_id(0) == last)
        def _():
            comm.finish(cins, couts, sems)

    res = pl.pallas_call(
        wrapped, name=name, grid=grid, out_shape=tuple(out_shape) + tuple(comm.out_shapes),
        in_specs=list(in_specs) + [ANY] * nci, out_specs=tuple(out_specs) + (ANY,) * nco,
        input_output_aliases={**aliases, **{n_in + a: n_out + b for a, b in comm.aliases.items()}},
        scratch_shapes=list(scratch_shapes) + [pltpu.SemaphoreType.DMA((k,)) for k in comm.sem_counts],
        compiler_params=pltpu.CompilerParams(dimension_semantics=("arbitrary",), vmem_limit_bytes=vmem_limit,
                                             has_side_effects=True),
    )(*args, *comm.ins)
    return tuple(res[:n_out]), tuple(res[n_out:])


def _run_comm(comm, name):
    nci, nco = len(comm.ins), len(comm.out_shapes)

    def body(*refs):
        cins, couts, sems = refs[:nci], refs[nci:nci + nco], refs[nci + nco:]
        comm.start(cins, couts, sems)
        if comm.middle is not None:
            comm.middle(cins, couts, sems)
        comm.finish(cins, couts, sems)

    return pl.pallas_call(
        body, name=name, out_shape=tuple(comm.out_shapes), in_specs=[ANY] * nci, out_specs=(ANY,) * nco,
        input_output_aliases=comm.aliases, scratch_shapes=[pltpu.SemaphoreType.DMA((k,)) for k in comm.sem_counts],
        compiler_params=pltpu.CompilerParams(has_side_effects=True),
    )(*comm.ins)


def _mm(a, b, name, a_is_transposed, out_split=1):
    (M, R) = a.shape if a_is_transposed else a.shape[::-1]
    _, N = b.shape
    bm, br = _pick(M, 1024 if a_is_transposed else 512), _pick(R, 1024 if a_is_transposed else 512)
    ncol = N // out_split
    bn = _pick(ncol, 1024)
    per = ncol // bn
    last = R // br - 1

    def body(a_ref, b_ref, o_ref, acc_ref):
        @pl.when(pl.program_id(2) == 0)
        def _():
            acc_ref[...] = jnp.zeros_like(acc_ref)

        acc_ref[...] += (_dot if a_is_transposed else _dot_tn)(a_ref[...], b_ref[...])

        @pl.when(pl.program_id(2) == last)
        def _():
            o_ref[...] = acc_ref[...].astype(BF16)

    if out_split == 1:
        out_shape = jax.ShapeDtypeStruct((M, N), BF16)
        out_spec = pl.BlockSpec((bm, bn), lambda i, j, r: (i, j))
    else:
        out_shape = jax.ShapeDtypeStruct((out_split, M, ncol), BF16)
        out_spec = pl.BlockSpec((None, bm, bn), lambda i, j, r: (j // per, i, j % per))
    a_spec = (pl.BlockSpec((bm, br), lambda i, j, r: (i, r)) if a_is_transposed
              else pl.BlockSpec((br, bm), lambda i, j, r: (r, i)))
    return pl.pallas_call(
        body, name=name, out_shape=out_shape, grid=(M // bm, N // bn, R // br),
        in_specs=[a_spec, pl.BlockSpec((br, bn), lambda i, j, r: (r, j))], out_specs=out_spec,
        scratch_shapes=[pltpu.VMEM((bm, bn), F32)],
        compiler_params=pltpu.CompilerParams(dimension_semantics=("parallel", "parallel", "arbitrary"),
                                             vmem_limit_bytes=VMEM_LIMIT),
    )(a, b)


def _mm_tn(a, b, name, out_split=1):
    return _mm(a, b, name, False, out_split)


def _mm_nn(at, b, name, out_split=1):
    return _mm(at, b, name, True, out_split)


def _mm_small(a, b, name):
    def body(a_ref, b_ref, o_ref):
        o_ref[...] = _dot(a_ref[...], b_ref[...]).astype(BF16)

    return pl.pallas_call(body, name=name, out_shape=jax.ShapeDtypeStruct((a.shape[0], b.shape[1]), BF16),
                          compiler_params=pltpu.CompilerParams(vmem_limit_bytes=VMEM_LIMIT))(a, b)


def _mlp_out(x_ref, wup_ref, wdown_ref, g_ref, b_ref, r_ref, h_ref):
    NC, _, FC = wup_ref.shape
    xv = x_ref[...]
    xb = xv.astype(BF16)
    y = jnp.zeros(xv.shape, F32)
    for c in range(NC):
        a = jnp.maximum(_dot(xb, wup_ref[c]), 0.0)
        hb = (a * a).astype(BF16)
        h_ref[:, c * FC:(c + 1) * FC] = hb
        y = y + _dot(hb, wdown_ref[c])
    r = ALPHA * xv + y
    r_ref[...] = r
    xhat, _ = _ln_stats(r)
    return xhat * g_ref[...] + b_ref[...]


def _mlp_fwd_loss(x, wup4, wdown4, g, b, target, name):
    S, D = x.shape
    NC, _, FC = wup4.shape
    tm = min(512, S)

    def body(x_ref, wup_ref, wdown_ref, g_ref, b_ref, t_ref, d_ref, r_ref, h_ref, l_ref):
        @pl.when(pl.program_id(0) == 0)
        def _():
            l_ref[...] = jnp.zeros_like(l_ref)

        err = _mlp_out(x_ref, wup_ref, wdown_ref, g_ref, b_ref, r_ref, h_ref) - t_ref[...]
        d_ref[...] = err * (1.0 / D)
        l_ref[...] += (0.5 / D) * jnp.sum(err * err)

    return pl.pallas_call(
        body, name=name, grid=(S // tm,),
        out_shape=(jax.ShapeDtypeStruct((S, D), F32), jax.ShapeDtypeStruct((S, D), F32),
                   jax.ShapeDtypeStruct((S, NC * FC), BF16), jax.ShapeDtypeStruct((8, LANES), F32)),
        in_specs=[_rows(tm, D), _full(wup4.shape), _full(wdown4.shape), _full((1, D)), _full((1, D)), _rows(tm, D)],
        out_specs=(_rows(tm, D), _rows(tm, D), _rows(tm, NC * FC), _full((8, LANES))),
        compiler_params=_params(),
    )(x, wup4, wdown4, g, b, target)


def _mlp_fwd(x, wup4, wdown4, g, b, name, comm=None):
    S, D = x.shape
    NC, _, FC = wup4.shape
    tm = min(512, S)

    def body(x_ref, wup_ref, wdown_ref, g_ref, b_ref, xo_ref, xb_ref, xbt_ref, r_ref, h_ref):
        xo = _mlp_out(x_ref, wup_ref, wdown_ref, g_ref, b_ref, r_ref, h_ref)
        xo_ref[...] = xo
        xb_ref[...] = xo.astype(BF16)
        xbt_ref[...] = xo.T.astype(BF16)

    return _pcall(
        body, name=name, grid=(S // tm,), comm=comm,
        out_shape=(jax.ShapeDtypeStruct((S, D), F32), jax.ShapeDtypeStruct((S, D), BF16),
                   jax.ShapeDtypeStruct((D, S), BF16), jax.ShapeDtypeStruct((S, D), F32),
                   jax.ShapeDtypeStruct((S, NC * FC), BF16)),
        in_specs=[_rows(tm, D), _full(wup4.shape), _full(wdown4.shape), _full((1, D)), _full((1, D))],
        out_specs=(_rows(tm, D), _rows(tm, D), _cols(D, tm), _rows(tm, D), _rows(tm, NC * FC)),
        args=(x, wup4, wdown4, g, b))


def _mlp_bwd(dxo, r, h, wup4, wdown4, g, name, comm=None):
    S, D = r.shape
    NC, _, FC = wup4.shape
    tm = min(512, S)

    def body(dxo_ref, r_ref, h_ref, wup_ref, wdown_ref, g_ref, dx_ref, drbt_ref, dpre_ref, dgb_ref):
        @pl.when(pl.program_id(0) == 0)
        def _():
            dgb_ref[...] = jnp.zeros_like(dgb_ref)

        xhat, rstd = _ln_stats(r_ref[...])
        dy = dxo_ref[...]
        dgb_ref[0:1, :] += jnp.sum(dy * xhat, axis=0, keepdims=True)
        dgb_ref[1:2, :] += jnp.sum(dy, axis=0, keepdims=True)
        dr = _ln_bwd(dy, xhat, rstd, g_ref[...])
        drb = dr.astype(BF16)
        drbt_ref[...] = dr.T.astype(BF16)
        dx = ALPHA * dr
        for c in range(NC):
            dh = _dot_nt(drb, wdown_ref[c])
            a = jnp.sqrt(h_ref[:, c * FC:(c + 1) * FC].astype(F32))
            dpre = (2.0 * a * dh).astype(BF16)
            dpre_ref[:, c * FC:(c + 1) * FC] = dpre
            dx = dx + _dot_nt(dpre, wup_ref[c])
        dx_ref[...] = dx

    return _pcall(
        body, name=name, grid=(S // tm,), comm=comm,
        out_shape=(jax.ShapeDtypeStruct((S, D), F32), jax.ShapeDtypeStruct((D, S), BF16),
                   jax.ShapeDtypeStruct((S, NC * FC), BF16), jax.ShapeDtypeStruct((2, D), F32)),
        in_specs=[_rows(tm, D), _rows(tm, D), _rows(tm, NC * FC), _full(wup4.shape), _full(wdown4.shape),
                  _full((1, D))],
        out_specs=(_rows(tm, D), _cols(D, tm), _rows(tm, NC * FC), _full((2, D))),
        args=(dxo, r, h, wup4, wdown4, g), vmem_limit=VMEM_LIMIT_MLP_BWD)


def _tail_fwd_store(xv, cat_ref, wo_ref, g_ref, b_ref, xo_ref, xbt_ref, r_ref, catt_ref):
    cat = cat_ref[...]
    y = _dot(cat, wo_ref[...])
    r = ALPHA * xv + y
    xhat, _ = _ln_stats(r)
    xo = xhat * g_ref[...] + b_ref[...]
    xo_ref[...] = xo
    xbt_ref[...] = xo.T.astype(BF16)
    catt_ref[...] = cat.astype(F32).T.astype(BF16)
    r_ref[...] = r


def _tail_bwd_head(dxo_ref, r_ref, g_ref, wo_ref, dgb_ref, drb_ref):
    xhat, rstd = _ln_stats(r_ref[...])
    dy = dxo_ref[...]
    dgb_ref[0:1, :] += jnp.sum(dy * xhat, axis=0, keepdims=True)
    dgb_ref[1:2, :] += jnp.sum(dy, axis=0, keepdims=True)
    dr = _ln_bwd(dy, xhat, rstd, g_ref[...])
    drb = dr.astype(BF16)
    drb_ref[...] = drb
    return dr, _dot_nt(drb, wo_ref[...])


def _causal(shape):
    return lax.broadcasted_iota(jnp.int32, shape, 1) <= lax.broadcasted_iota(jnp.int32, shape, 0)


def _sgu_mixed(ws_ref, bst_ref, gp, vpair):
    tril = _causal((CHUNK, CHUNK))
    w0 = jnp.where(tril, ws_ref[2 * gp], 0.0).astype(BF16)
    w1 = jnp.where(tril, ws_ref[2 * gp + 1], 0.0).astype(BF16)
    m0 = _dot(w0, vpair) + bst_ref[:, 2 * gp:2 * gp + 1]
    m1 = _dot(w1, vpair) + bst_ref[:, 2 * gp + 1:2 * gp + 2]
    return jnp.where(_first_head(), m0, m1), w0, w1


def _mixer_a_fwd(x, wa, sg, sb, ws, bst, mkv, wo, g, b, name, comm=None):
    S, D = x.shape
    DQ = mkv.shape[1] // 2
    DM = D - DQ
    NP = DM // PAIR
    tm = min(512, S)

    def body(x_ref, wa_ref, sg_ref, sb_ref, ws_ref, bst_ref, mkv_ref, wo_ref, g_ref, b_ref,
             xo_ref, xbt_ref, r_ref, catt_ref, xint_ref, zb_ref, z_s, zv_s, cat_ref):
        xv = x_ref[...]
        xint_ref[...] = xv.T.astype(BF16)
        z_s[...] = _dot(xv.astype(BF16), wa_ref[...])
        zb_ref[...] = z_s[...].astype(BF16)
        vhat, _ = _ln_stats(_gelu(z_s[:, DM:2 * DM]))
        zv_s[...] = (vhat * sg_ref[...] + sb_ref[...]).astype(BF16)
        for c in range(tm // CHUNK):
            rs = slice(c * CHUNK, (c + 1) * CHUNK)
            for gp in range(NP):
                ls = slice(gp * PAIR, (gp + 1) * PAIR)
                mixed, _, _ = _sgu_mixed(ws_ref, bst_ref, gp, zv_s[rs, ls])
                cat_ref[rs, ls] = (_gelu(z_s[rs, ls]) * mixed).astype(BF16)
        mo = _mem_attn_fwd(z_s[:, 2 * DM:].astype(BF16), mkv_ref, DQ)
        for mp in range(DQ // PAIR):
            cat_ref[:, DM + mp * PAIR:DM + (mp + 1) * PAIR] = mo[mp].astype(BF16)
        _tail_fwd_store(xv, cat_ref, wo_ref, g_ref, b_ref, xo_ref, xbt_ref, r_ref, catt_ref)

    return _pcall(
        body, name=name, grid=(S // tm,), comm=comm,
        out_shape=(jax.ShapeDtypeStruct((S, D), F32), jax.ShapeDtypeStruct((D, S), BF16),
                   jax.ShapeDtypeStruct((S, D), F32), jax.ShapeDtypeStruct((D, S), BF16),
                   jax.ShapeDtypeStruct((D, S), BF16), jax.ShapeDtypeStruct((S, 2 * DM + DQ), BF16)),
        in_specs=[_rows(tm, D), _full(wa.shape), _full(sg.shape), _full(sb.shape), _full(ws.shape), _full(bst.shape),
                  _full(mkv.shape), _full(wo.shape), _full((1, D)), _full((1, D))],
        out_specs=(_rows(tm, D), _cols(D, tm), _rows(tm, D), _cols(D, tm), _cols(D, tm), _rows(tm, 2 * DM + DQ)),
        scratch_shapes=[pltpu.VMEM((tm, 2 * DM + DQ), F32), pltpu.VMEM((tm, DM), BF16), pltpu.VMEM((tm, D), BF16)],
        args=(x, wa, sg, sb, ws, bst, mkv, wo, g, b))


def _mixer_a_bwd(zb, dxo, r, wa, sg, sb, ws, bst, mkv, wo, g, name, comm=None):
    S, D = r.shape
    DQ = mkv.shape[1] // 2
    DM = D - DQ
    NP = DM // PAIR
    NG = DM // HEAD_DIM
    tm = min(512, S)

    def body(zb_ref, dxo_ref, r_ref, wa_ref, sg_ref, sb_ref, ws_ref, bst_ref, mkv_ref, wo_ref, g_ref,
             dx_ref, dz_ref, drb_ref, dgb_ref, dsgb_ref, dws_ref, dbst_ref, dmkv_ref, z_s, zv_s, dzv_s):
        @pl.when(pl.program_id(0) == 0)
        def _():
            dgb_ref[...] = jnp.zeros_like(dgb_ref)
            dsgb_ref[...] = jnp.zeros_like(dsgb_ref)
            dws_ref[...] = jnp.zeros_like(dws_ref)
            dbst_ref[...] = jnp.zeros_like(dbst_ref)
            dmkv_ref[...] = jnp.zeros_like(dmkv_ref)

        dr, dcat = _tail_bwd_head(dxo_ref, r_ref, g_ref, wo_ref, dgb_ref, drb_ref)
        z_s[...] = zb_ref[...].astype(F32)
        vhat, vrstd = _ln_stats(_gelu(z_s[:, DM:2 * DM]))
        zv_s[...] = (vhat * sg_ref[...] + sb_ref[...]).astype(BF16)
        tril = _causal((CHUNK, CHUNK))
        lane = lax.broadcasted_iota(jnp.int32, (1, LANES), 1)
        for c in range(tm // CHUNK):
            rs = slice(c * CHUNK, (c + 1) * CHUNK)
            for gp in range(NP):
                ls = slice(gp * PAIR, (gp + 1) * PAIR)
                vpair = zv_s[rs, ls]
                mixed, w0, w1 = _sgu_mixed(ws_ref, bst_ref, gp, vpair)
                u_pre = z_s[rs, ls]
                dmix = dcat[rs, ls]
                dz_ref[rs, ls] = (dmix * mixed * _gelu_grad(u_pre)).astype(BF16)
                dmixed = dmix * _gelu(u_pre)
                first = _first_head()
                d0 = jnp.where(first, dmixed, 0.0)
                d1 = jnp.where(first, 0.0, dmixed)
                d0b = d0.astype(BF16)
                d1b = d1.astype(BF16)
                dzv_s[rs, ls] = _dot_tn(w0, d0b) + _dot_tn(w1, d1b)
                dws_ref[2 * gp] += jnp.where(tril, _dot_nt(d0b, vpair), 0.0)
                dws_ref[2 * gp + 1] += jnp.where(tril, _dot_nt(d1b, vpair), 0.0)
                dbst_ref[...] += (jnp.where(lane == 2 * gp, jnp.sum(d0, axis=-1, keepdims=True), 0.0)
                                  + jnp.where(lane == 2 * gp + 1, jnp.sum(d1, axis=-1, keepdims=True), 0.0))
        dzv = dzv_s[...]
        dsgb_ref[0:1, :] += jnp.sum(dzv * vhat, axis=0, keepdims=True)
        dsgb_ref[1:2, :] += jnp.sum(dzv, axis=0, keepdims=True)
        dgv = _ln_bwd(dzv, vhat, vrstd, sg_ref[...])
        dz_ref[:, DM:2 * DM] = (dgv * _gelu_grad(z_s[:, DM:2 * DM])).astype(BF16)
        dmo = [dcat[:, DM + mp * PAIR:DM + (mp + 1) * PAIR] for mp in range(DQ // PAIR)]
        dqm = _mem_attn_bwd(z_s[:, 2 * DM:].astype(BF16), mkv_ref, dmo, dmkv_ref, DQ)
        for mp in range(DQ // PAIR):
            dz_ref[:, 2 * DM + mp * PAIR:2 * DM + (mp + 1) * PAIR] = dqm[mp].astype(BF16)
        dx_ref[...] = ALPHA * dr + _dot_nt(dz_ref[...], wa_ref[...])

    ZW = 2 * DM + DQ
    M = mkv.shape[0]
    return _pcall(
        body, name=name, grid=(S // tm,), comm=comm,
        out_shape=(jax.ShapeDtypeStruct((S, D), F32), jax.ShapeDtypeStruct((S, ZW), BF16),
                   jax.ShapeDtypeStruct((S, D), BF16), jax.ShapeDtypeStruct((2, D), F32),
                   jax.ShapeDtypeStruct((2, DM), F32), jax.ShapeDtypeStruct((NG, CHUNK, CHUNK), F32),
                   jax.ShapeDtypeStruct((CHUNK, LANES), F32), jax.ShapeDtypeStruct((M, 2 * DQ), F32)),
        in_specs=[_rows(tm, ZW), _rows(tm, D), _rows(tm, D), _full(wa.shape), _full(sg.shape), _full(sb.shape),
                  _full(ws.shape), _full(bst.shape), _full(mkv.shape), _full(wo.shape), _full((1, D))],
        out_specs=(_rows(tm, D), _rows(tm, ZW), _rows(tm, D), _full((2, D)), _full((2, DM)),
                   _full((NG, CHUNK, CHUNK)), _full((CHUNK, LANES)), _full((M, 2 * DQ))),
        scratch_shapes=[pltpu.VMEM((tm, ZW), F32), pltpu.VMEM((tm, DM), BF16), pltpu.VMEM((tm, DM), F32)],
        args=(zb, dxo, r, wa, sg, sb, ws, bst, mkv, wo, g))


def _proj_b_fwd(xb, wq, wkv, dm, name):
    S, D = xb.shape
    DQ = D - dm
    tm = min(512, S)

    def body(x_ref, wq_ref, wkv_ref, q_ref, qm_ref, k_ref, v_ref, fl_ref):
        xv = x_ref[...]
        z = _dot(xv, wq_ref[...])
        q_ref[...] = z[:, :dm].astype(BF16)
        qm_ref[...] = z[:, dm:].astype(BF16)
        kvf = _dot(xv, wkv_ref[...])
        k_ref[...] = kvf[:, :dm].astype(BF16)
        v_ref[...] = kvf[:, dm:2 * dm].astype(BF16)
        fl_ref[...] = kvf[:, 2 * dm:]

    return pl.pallas_call(
        body, name=name, grid=(S // tm,),
        out_shape=(jax.ShapeDtypeStruct((S, dm), BF16), jax.ShapeDtypeStruct((S, DQ), BF16),
                   jax.ShapeDtypeStruct((S, dm), BF16), jax.ShapeDtypeStruct((S, dm), BF16),
                   jax.ShapeDtypeStruct((S, LANES), F32)),
        in_specs=[_rows(tm, D), _full(wq.shape), _full(wkv.shape)],
        out_specs=(_rows(tm, dm), _rows(tm, DQ), _rows(tm, dm), _rows(tm, dm), _rows(tm, LANES)),
        compiler_params=_params(),
    )(xb, wq, wkv)


def _proj_b_bwd(dr, dqz, dk, dv, dfl, wq, wkv, name):
    S, D = dr.shape
    dm = dk.shape[1]
    tm = min(512, S)

    def body(dr_ref, dqz_ref, dk_ref, dv_ref, dfl_ref, wq_ref, wkv_ref, dx_ref):
        dx_ref[...] = (ALPHA * dr_ref[...] + _dot_nt(dqz_ref[...], wq_ref[...])
                       + _dot_nt(dk_ref[...], wkv_ref[:, :dm]) + _dot_nt(dv_ref[...], wkv_ref[:, dm:2 * dm])
                       + _dot_nt(dfl_ref[...], wkv_ref[:, 2 * dm:]))

    return pl.pallas_call(
        body, name=name, grid=(S // tm,), out_shape=jax.ShapeDtypeStruct((S, D), F32),
        in_specs=[_rows(tm, D), _rows(tm, dqz.shape[1]), _rows(tm, dm), _rows(tm, dm), _rows(tm, dfl.shape[1]),
                  _full(wq.shape), _full(wkv.shape)],
        out_specs=_rows(tm, D), compiler_params=_params(),
    )(dr, dqz, dk, dv, dfl, wq, wkv)


def _mm_nn_cat(at, bs, name):
    M, R = at.shape
    widths = [b.shape[1] for b in bs]
    N = sum(widths)
    br = _pick(R, 1024)
    last = R // br - 1
    nb = len(bs)

    def body(a_ref, *refs):
        b_refs, o_ref, acc_ref = refs[:nb], refs[nb], refs[nb + 1]

        @pl.when(pl.program_id(0) == 0)
        def _():
            acc_ref[...] = jnp.zeros_like(acc_ref)

        a = a_ref[...]
        c0 = 0
        for b_ref, wd in zip(b_refs, widths):
            acc_ref[:, c0:c0 + wd] += _dot(a, b_ref[...])
            c0 += wd

        @pl.when(pl.program_id(0) == last)
        def _():
            o_ref[...] = acc_ref[...].astype(BF16)

    return pl.pallas_call(
        body, name=name, grid=(R // br,), out_shape=jax.ShapeDtypeStruct((M, N), BF16),
        in_specs=[pl.BlockSpec((M, br), lambda r: (0, r))] + [pl.BlockSpec((br, wd), lambda r: (r, 0)) for wd in widths],
        out_specs=_full((M, N)), scratch_shapes=[pltpu.VMEM((M, N), F32)], compiler_params=_params(),
    )(at, *bs)


def _split3(v):
    hi = v.astype(BF16)
    r1 = v - hi.astype(F32)
    mid = r1.astype(BF16)
    lo = (r1 - mid.astype(F32)).astype(BF16)
    return hi, mid, lo


def _tri_sum(v, tri_b):
    hi, mid, lo = _split3(v)
    return _dot(hi, tri_b) + _dot(mid, tri_b) + _dot(lo, tri_b)


def _log_sigmoid(x):
    return jnp.minimum(x, 0.0) - jnp.log(1.0 + jnp.exp(-jnp.abs(x)))


def _forget_cumsum(flt, bf, name):
    H, S = flt.shape
    nchunk = S // LANES

    def body(fl_ref, bf_ref, p_ref):
        upper = (lax.broadcasted_iota(jnp.int32, (LANES, LANES), 0)
                 <= lax.broadcasted_iota(jnp.int32, (LANES, LANES), 1)).astype(BF16)

        def step(n, carry):
            s0 = pl.multiple_of(n * LANES, LANES)
            lf = _log_sigmoid(fl_ref[:, pl.ds(s0, LANES)] + bf_ref[...])
            cs = _tri_sum(lf, upper) + carry
            for i, piece in enumerate(_split3(-cs)):
                p_ref[i, :, pl.ds(s0, LANES)] = piece.astype(F32)
            return cs[:, LANES - 1:LANES]

        lax.fori_loop(0, nchunk, step, jnp.zeros((H, 1), F32))

    return pl.pallas_call(body, name=name, out_shape=jax.ShapeDtypeStruct((3, H, S), F32),
                          compiler_params=pltpu.CompilerParams(vmem_limit_bytes=VMEM_LIMIT))(flt, bf)


def _forget_cumsum_bwd(dct, flt, bf, name):
    H, S = flt.shape
    nchunk = S // LANES

    def body(dc_ref, fl_ref, bf_ref, dfl_ref, dbf_ref):
        lower = (lax.broadcasted_iota(jnp.int32, (LANES, LANES), 0)
                 >= lax.broadcasted_iota(jnp.int32, (LANES, LANES), 1)).astype(BF16)

        def step(n, carry):
            tail, tot = carry
            s0 = pl.multiple_of((nchunk - 1 - n) * LANES, LANES)
            suffix = _tri_sum(dc_ref[:, pl.ds(s0, LANES)], lower) + tail
            xv = fl_ref[:, pl.ds(s0, LANES)] + bf_ref[...]
            dfl = suffix * (1.0 / (1.0 + jnp.exp(xv)))
            dfl_ref[:, pl.ds(s0, LANES)] = dfl
            return suffix[:, 0:1], tot + jnp.sum(dfl, axis=-1, keepdims=True)

        _, tot = lax.fori_loop(0, nchunk, step, (jnp.zeros((H, 1), F32), jnp.zeros((H, 1), F32)))
        dbf_ref[...] = jnp.broadcast_to(tot, (H, LANES))

    return pl.pallas_call(body, name=name,
                          out_shape=(jax.ShapeDtypeStruct((H, S), F32), jax.ShapeDtypeStruct((H, LANES), F32)),
                          compiler_params=pltpu.CompilerParams(vmem_limit_bytes=VMEM_LIMIT))(dct, flt, bf)


N_BIAS = 3


def _bias_ones():
    lane = lax.broadcasted_iota(jnp.int32, (1, PAIR), 1)
    return ((lane & (HEAD_DIM - 1)) < N_BIAS).astype(BF16)


def _fox_fwd(q, k, v, ca, name, comm=None):
    S, DM = q.shape
    NP = DM // PAIR
    T = min(FOX_BLOCK, S)
    nq = S // T

    def body(q_ref, k_ref, v_ref, ca_ref, o_ref, lse_ref):
        tril = _causal((T, T))
        heads = (_lane_mask(0), _lane_mask(1))
        ones3 = _bias_ones()

        def kv_block(kj):
            s0 = pl.multiple_of(kj * T, T)
            kb, vb, cab = k_ref[pl.ds(s0, T), :], v_ref[pl.ds(s0, T), :], ca_ref[pl.ds(s0, T), :]
            return ([jnp.where(heads[e], kb, cab) for e in (0, 1)],
                    [jnp.where(heads[e], vb, jnp.ones_like(vb)) for e in (0, 1)])

        def q_pair(a, _):
            t0s = [pl.multiple_of((2 * a + r) * T, T) for r in (0, 1)]
            qaug = {}
            for r in (0, 1):
                qb = q_ref[pl.ds(t0s[r], T), :] * QK_SCALE
                for e in (0, 1):
                    qaug[r, e] = jnp.where(heads[e], qb, ones3)
            chains = [(r, e) for r in (0, 1) for e in (0, 1)]

            def update(carries, first_block, seen):
                blocks = [kv_block(first_block + kk) for kk in range(1 + max(kk for s in seen for kk, _ in s))]
                out = []
                scores = [[_dot_nt(qaug[r, e], blocks[kk][0][e]) for kk, _ in seen[r]] for r, e in chains]
                for n, (r, e) in enumerate(chains):
                    m, acc = carries[n]
                    tiles = [jnp.where(tril, s, NEG) if masked else s for s, (_, masked) in zip(scores[n], seen[r])]
                    m_new = jnp.maximum(m, jnp.max(functools.reduce(jnp.maximum, tiles), axis=-1, keepdims=True))
                    acc = jnp.exp(m - m_new) * acc
                    for s, (kk, _) in zip(tiles, seen[r]):
                        acc = acc + _dot(jnp.exp(s - m_new).astype(BF16), blocks[kk][1][e])
                    out.append((m_new, acc))
                return tuple(out)

            def full(first_block, n_blocks, carries):
                return update(carries, first_block, [[(kk, False) for kk in range(n_blocks)]] * 2)

            init = (jnp.full((T, 1), NEG, F32), jnp.zeros((T, PAIR), F32))
            carries = lax.fori_loop(0, a // 2, lambda i, c: full(4 * i, 4, c), (init,) * 4)
            carries = lax.fori_loop(0, a % 2, lambda i, c: full(2 * a - 2, 2, c), carries)
            carries = update(carries, 2 * a, [[(0, True)], [(0, False), (1, True)]])
            first = _first_head()
            for r in (0, 1):
                (m0, acc0), (m1, acc1) = carries[2 * r], carries[2 * r + 1]
                l0, l1 = acc0[:, HEAD_DIM:HEAD_DIM + 1], acc1[:, 0:1]
                o_ref[pl.ds(t0s[r], T), :] = jnp.where(first, acc0 / l0, acc1 / l1).astype(BF16)
                lse_t = jnp.where(first, m0 + jnp.log(l0), m1 + jnp.log(l1)).T
                lse_ref[0:1, pl.ds(t0s[r], T)] = lse_t[0:1, :]
                lse_ref[1:2, pl.ds(t0s[r], T)] = lse_t[HEAD_DIM:HEAD_DIM + 1, :]
            return 0

        lax.fori_loop(0, nq // 2, q_pair, 0)

    col = pl.BlockSpec((S, PAIR), lambda hp: (0, hp))
    return _pcall(
        body, name=name, grid=(NP,), comm=comm,
        out_shape=(jax.ShapeDtypeStruct((S, DM), BF16), jax.ShapeDtypeStruct((NP, 2, S), F32)),
        in_specs=[col, col, col, col], out_specs=(col, pl.BlockSpec((None, 2, S), lambda hp: (hp, 0, 0))),
        args=(q, k, v, ca))


def _fox_bwd(q, k, v, o, do, ca, lse, name, comm=None):
    S, DM = q.shape
    NP = DM // PAIR
    T = min(FOX_BLOCK, S)
    nq = S // T

    def body(q_ref, k_ref, v_ref, o_ref, do_ref, ca_ref, lse_ref, dq_ref, dk_ref, dv_ref, dc_ref,
             dq_acc, dcol_acc, dk_acc, dv_acc, dd_s):
        keep = lax.broadcasted_iota(jnp.int32, (T, T), 0) <= lax.broadcasted_iota(jnp.int32, (T, T), 1)
        lane = lax.broadcasted_iota(jnp.int32, (1, LANES), 1)
        heads = (_lane_mask(0), _lane_mask(1))
        ones3 = _bias_ones()
        dq_acc[...] = jnp.zeros_like(dq_acc)
        dcol_acc[...] = jnp.zeros_like(dcol_acc)
        dc_ref[...] = jnp.zeros_like(dc_ref)

        def row_dots(qi, _):
            t0 = pl.multiple_of(qi * T, T)
            prod = do_ref[pl.ds(t0, T), :].astype(F32) * o_ref[pl.ds(t0, T), :].astype(F32)
            cols = [jnp.sum(jnp.where(heads[e], prod, 0.0), axis=-1, keepdims=True) for e in (0, 1)]
            tile = jnp.where(lane == 0, cols[0], jnp.where(lane == 1, cols[1], 0.0))
            dd_s[:, pl.ds(t0, T)] = tile.T[0:8, :]
            return 0

        lax.fori_loop(0, nq, row_dots, 0)

        def kv_pair(b, _):
            dk_acc[...] = jnp.zeros_like(dk_acc)
            dv_acc[...] = jnp.zeros_like(dv_acc)
            s0s = [pl.multiple_of((2 * b + jj) * T, T) for jj in (0, 1)]
            kaug, ks, vbs = {}, {}, []
            for jj in (0, 1):
                kb = k_ref[pl.ds(s0s[jj], T), :]
                cab = ca_ref[pl.ds(s0s[jj], T), :]
                vbs.append(v_ref[pl.ds(s0s[jj], T), :])
                for e in (0, 1):
                    kaug[jj, e] = jnp.where(heads[e], kb, cab)
                    ks[jj, e] = jnp.where(heads[e], kb, jnp.zeros_like(kb)) * QK_SCALE

            def q_step(qi, blocks):
                t0 = pl.multiple_of(qi * T, T)
                qs = q_ref[pl.ds(t0, T), :] * QK_SCALE
                dob = do_ref[pl.ds(t0, T), :]
                doh = [jnp.where(heads[e], dob, jnp.zeros_like(dob)) for e in (0, 1)]
                qh = [jnp.where(heads[e], qs, jnp.zeros_like(qs)) for e in (0, 1)]
                chains = [(e, jj, masked) for e in (0, 1) for jj, masked in blocks]
                scores = [_dot_nt(kaug[jj, e], jnp.where(heads[e], qs, ones3)) for e, jj, _ in chains]
                dps = [_dot_nt(vbs[jj], doh[e]) for e, jj, _ in chains]
                dq = jnp.zeros((T, PAIR), F32)
                drow = [jnp.zeros((1, T), F32), jnp.zeros((1, T), F32)]
                for n, (e, jj, masked) in enumerate(chains):
                    st = jnp.where(keep, scores[n], NEG) if masked else scores[n]
                    pt = jnp.exp(st - lse_ref[e:e + 1, pl.ds(t0, T)])
                    dv_acc[jj] += _dot(pt.astype(BF16), doh[e])
                    dst = pt * (dps[n] - dd_s[e:e + 1, pl.ds(t0, T)])
                    dsb = dst.astype(BF16)
                    dk_acc[jj] += _dot(dsb, qh[e])
                    dq = dq + _dot_tn(dsb, ks[jj, e])
                    dcol_acc[pl.ds(s0s[jj], T), :] -= jnp.where(lane == e, jnp.sum(dst, axis=-1, keepdims=True), 0.0)
                    drow[e] = drow[e] + jnp.sum(dst, axis=0, keepdims=True)
                for e in (0, 1):
                    dc_ref[e:e + 1, pl.ds(t0, T)] += drow[e]
                dq_acc[pl.ds(t0, T), :] += dq

            q_step(2 * b, [(0, True)])
            q_step(2 * b + 1, [(0, False), (1, True)])

            def rest(i, _):
                q_step(2 * b + 2 + 2 * i, [(0, False), (1, False)])
                q_step(2 * b + 3 + 2 * i, [(0, False), (1, False)])
                return 0

            lax.fori_loop(0, (nq - 2) // 2 - b, rest, 0)
            for jj in (0, 1):
                dk_ref[pl.ds(s0s[jj], T), :] = dk_acc[jj].astype(BF16)
                dv_ref[pl.ds(s0s[jj], T), :] = dv_acc[jj].astype(BF16)
            return 0

        lax.fori_loop(0, nq // 2, kv_pair, 0)
        dq_ref[...] = dq_acc[...].astype(BF16)

        def add_cols(kj, _):
            s0 = pl.multiple_of(kj * T, T)
            dc_ref[:, pl.ds(s0, T)] += dcol_acc[pl.ds(s0, T), :].T[0:2, :]
            return 0

        lax.fori_loop(0, nq, add_cols, 0)

    col = pl.BlockSpec((S, PAIR), lambda hp: (0, hp))
    row2 = pl.BlockSpec((None, 2, S), lambda hp: (hp, 0, 0))
    return _pcall(
        body, name=name, grid=(NP,), comm=comm,
        out_shape=(jax.ShapeDtypeStruct(do.shape, BF16), jax.ShapeDtypeStruct((S, DM), BF16),
                   jax.ShapeDtypeStruct((S, DM), BF16), jax.ShapeDtypeStruct((NP, 2, S), F32)),
        in_specs=[col, col, col, col, col, col, row2], out_specs=(col, col, col, row2),
        scratch_shapes=[pltpu.VMEM((S, PAIR), F32), pltpu.VMEM((S, LANES), F32), pltpu.VMEM((2, T, PAIR), F32),
                        pltpu.VMEM((2, T, PAIR), F32), pltpu.VMEM((8, S), F32)],
        args=(q, k, v, o, do, ca, lse), aliases={4: 0})


def _mixer_b_fwd(x, o, qm, mkv, wo, g, b, name):
    S, D = x.shape
    DQ = qm.shape[1]
    DM = D - DQ
    tm = min(512, S)

    def body(x_ref, o_ref, qm_ref, mkv_ref, wo_ref, g_ref, b_ref, xo_ref, xbt_ref, r_ref, catt_ref, cat_ref):
        cat_ref[:, :DM] = o_ref[...]
        mo = _mem_attn_fwd(qm_ref[...], mkv_ref, DQ)
        for mp in range(DQ // PAIR):
            cat_ref[:, DM + mp * PAIR:DM + (mp + 1) * PAIR] = mo[mp].astype(BF16)
        _tail_fwd_store(x_ref[...], cat_ref, wo_ref, g_ref, b_ref, xo_ref, xbt_ref, r_ref, catt_ref)

    return pl.pallas_call(
        body, name=name, grid=(S // tm,),
        out_shape=(jax.ShapeDtypeStruct((S, D), F32), jax.ShapeDtypeStruct((D, S), BF16),
                   jax.ShapeDtypeStruct((S, D), F32), jax.ShapeDtypeStruct((D, S), BF16)),
        in_specs=[_rows(tm, D), _rows(tm, DM), _rows(tm, DQ), _full(mkv.shape), _full(wo.shape), _full((1, D)),
                  _full((1, D))],
        out_specs=(_rows(tm, D), _cols(D, tm), _rows(tm, D), _cols(D, tm)),
        scratch_shapes=[pltpu.VMEM((tm, D), BF16)],
        compiler_params=_params(),
    )(x, o, qm, mkv, wo, g, b)


def _mixer_b_bwd(dxo, r, qm, mkv, wo, g, name, comm=None):
    S, D = r.shape
    DQ = qm.shape[1]
    DM = D - DQ
    M = mkv.shape[0]
    tm = min(512, S)

    def body(dxo_ref, r_ref, qm_ref, mkv_ref, wo_ref, g_ref, dr_ref, drb_ref, dcat_ref, dgb_ref, dmkv_ref):
        @pl.when(pl.program_id(0) == 0)
        def _():
            dgb_ref[...] = jnp.zeros_like(dgb_ref)
            dmkv_ref[...] = jnp.zeros_like(dmkv_ref)

        dr, dcat = _tail_bwd_head(dxo_ref, r_ref, g_ref, wo_ref, dgb_ref, drb_ref)
        dr_ref[...] = dr
        dcat_ref[:, :DM] = dcat[:, :DM].astype(BF16)
        dmo = [dcat[:, DM + mp * PAIR:DM + (mp + 1) * PAIR] for mp in range(DQ // PAIR)]
        dqm = _mem_attn_bwd(qm_ref[...], mkv_ref, dmo, dmkv_ref, DQ)
        for mp in range(DQ // PAIR):
            dcat_ref[:, DM + mp * PAIR:DM + (mp + 1) * PAIR] = dqm[mp].astype(BF16)

    return _pcall(
        body, name=name, grid=(S // tm,), comm=comm,
        out_shape=(jax.ShapeDtypeStruct((S, D), F32), jax.ShapeDtypeStruct((S, D), BF16),
                   jax.ShapeDtypeStruct((S, D), BF16), jax.ShapeDtypeStruct((2, D), F32),
                   jax.ShapeDtypeStruct((M, 2 * DQ), F32)),
        in_specs=[_rows(tm, D), _rows(tm, D), _rows(tm, DQ), _full(mkv.shape), _full(wo.shape), _full((1, D))],
        out_specs=(_rows(tm, D), _rows(tm, D), _rows(tm, D), _full((2, D)), _full((M, 2 * DQ))),
        args=(dxo, r, qm, mkv, wo, g))


class _NoExchange:
    def plan(self, stage, w, grads):
        return None

    def done(self, stage, results, w):
        pass


def _local_step(x, mem, target, w, hooks):
    S, D = x.shape
    DQ = w["mkv0"].shape[1] // 2
    DM = D - DQ
    NG = DM // HEAD_DIM
    NP = DM // PAIR
    ln_g, ln_b = w["ln_g"], w["ln_b"]
    row = lambda a, i: a[i:i + 1]
    memb = mem.astype(BF16)
    bst =jnp.pad(w["b_s"].T, ((0, 0), (0, LANES - NG)))
    grads = {}

    def staged(stage, fn, *args):
        out, got = fn(*args, stage, comm=hooks.plan(stage, w, grads))
        hooks.done(stage, got, w)
        return out

    mkv0 = _mm_small(memb, w["mkv0"], "mem_kv_0")
    x1, x1t, r1, cat0t, x0t, z0b = staged("mixer_a_fwd", _mixer_a_fwd, x, w["wa"], w["sg"], w["sb"], w["ws"], bst, mkv0,
                                w["wo0"], row(ln_g, 0), row(ln_b, 0))
    x2, x2b, x2t, r2, h0 = staged("mlp_fwd_0", _mlp_fwd, x1, w["wup0"], w["wdown0"], row(ln_g, 1), row(ln_b, 1))
    q, qm, k, v, fl = _proj_b_fwd(x2b, w["wq"], w["wkv"], DM, "proj_b_fwd")
    flt = fl[:, :16].T
    bfc = jnp.pad(w["bf"], (0, 16 - NG)).reshape(16, 1)
    pieces = _forget_cumsum(flt, bfc, "forget_cumsum")
    pieces = jnp.pad(pieces[:, :NG].transpose(2, 1, 0), ((0, 0), (0, 0), (0, HEAD_DIM - N_BIAS)))
    ca = pieces.reshape(S, NP, 2, HEAD_DIM)[:, :, ::-1].reshape(S, DM).astype(BF16)
    o, lse = staged("fox_fwd", _fox_fwd, q, k, v, ca)
    mkv1 = _mm_small(memb, w["mkv1"], "mem_kv_1")
    x3, x3t, r3, cat1t = _mixer_b_fwd(x2, o, qm, mkv1, w["wo1"], row(ln_g, 2), row(ln_b, 2), "mixer_b_fwd")
    dx4, r4, h1, loss_part = _mlp_fwd_loss(x3, w["wup1"], w["wdown1"], row(ln_g, 3), row(ln_b, 3), target,
                                           "mlp_fwd_1")

    nc = w["wup0"].shape[0]
    (dx3, dr4t, dpre1, dgb11), _ = _mlp_bwd(dx4, r4, h1, w["wup1"], w["wdown1"], row(ln_g, 3), "mlp_bwd_1")
    grads["wdown1"] = _mm_nn(dr4t, h1, "dw_down_1", out_split=nc)
    grads["wup1"] = _mm_nn(x3t, dpre1, "dw_up_1", out_split=nc)
    dr3, dr3b, dcat1, dgb10, dmkv1 = staged("mixer_b_bwd", _mixer_b_bwd, dx3, r3, qm, mkv1, w["wo1"], row(ln_g, 2))
    dqz, dk, dv, dct3 = staged("fox_bwd", _fox_bwd, q, k, v, o, dcat1, ca, lse)
    dct = jnp.pad(dct3.reshape(NG, S), ((0, 16 - NG), (0, 0)))
    dflt, dbf = _forget_cumsum_bwd(dct, flt, bfc, "forget_cumsum_bwd")
    dfl = jnp.pad(dflt.T, ((0, 0), (0, LANES - 16))).astype(BF16)
    dx2 = _proj_b_bwd(dr3, dqz, dk, dv, dfl, w["wq"], w["wkv"], "proj_b_bwd")
    grads["wo1"] = _mm_nn(cat1t, dr3b, "dw_o_1")
    grads["wq"] = _mm_nn(x2t, dqz, "dw_q")
    grads["wkv"] = _mm_nn_cat(x2t, [dk, dv, dfl], "dw_kv")
    grads["mkv1"] = _mm_tn(memb, dmkv1.astype(BF16), "dw_mkv_1")
    dx1, dr2t, dpre0, dgb01 = staged("mlp_bwd_0", _mlp_bwd, dx2, r2, h0, w["wup0"], w["wdown0"], row(ln_g, 1))
    grads["wdown0"] = _mm_nn(dr2t, h0, "dw_down_0", out_split=nc)
    grads["wup0"] = _mm_nn(x1t, dpre0, "dw_up_0", out_split=nc)
    dx0, dz, dr1b, dgb00, dsgb, dws, dbst, dmkv0 = staged(
        "mixer_a_bwd", _mixer_a_bwd, z0b, dx1, r1, w["wa"], w["sg"], w["sb"], w["ws"], bst, mkv0, w["wo0"], row(ln_g, 0))
    grads["wo0"] = _mm_nn(cat0t, dr1b, "dw_o_0")
    grads["wa"] = _mm_nn(x0t, dz, "dw_a")
    grads["mkv0"] = _mm_tn(memb, dmkv0.astype(BF16), "dw_mkv_0")
    grads.update({
        "ws": dws, "b_s": dbst[:, :NG].T, "sg": dsgb[0:1], "sb": dsgb[1:2], "bf": dbf[:NG, 0], "loss": loss_part[0:1],
        "ln_g": jnp.concatenate([dgb00[0:1], dgb01[0:1], dgb10[0:1], dgb11[0:1]], axis=0),
        "ln_b": jnp.concatenate([dgb00[1:2], dgb01[1:2], dgb10[1:2], dgb11[1:2]], axis=0),
    })
    return loss_part, dx0, grads


def _place():
    xi, yi, ci = lax.axis_index("x"), lax.axis_index("y"), lax.axis_index("c")
    peers = [(xi, 1 - yi), (1 - xi, yi), (1 - xi, 1 - yi)]
    return xi, yi, ci, peers


def _comm_call(body, name, ins, out_shapes, n_remote, n_local):
    return pl.pallas_call(
        body, name=name, out_shape=out_shapes, in_specs=[ANY] * len(ins), out_specs=[ANY] * len(out_shapes),
        scratch_shapes=[pltpu.SemaphoreType.DMA((n_remote,)), pltpu.SemaphoreType.DMA((n_remote,)),
                        pltpu.SemaphoreType.DMA((max(n_local, 1),))],
        compiler_params=pltpu.CompilerParams(has_side_effects=True),
    )(*ins)


def _pair_swap(xs):
    n = len(xs)
    out_shapes = [jax.ShapeDtypeStruct((a.shape[0],) + a.shape[2:], a.dtype) for a in xs]

    def copies(x_refs, o_refs, sems):
        send_sems, recv_sems = sems
        xi, yi, ci, _ = _place()
        return [pltpu.make_async_remote_copy(src_ref=x_refs[i].at[:, 1 - ci], dst_ref=o_refs[i],
                                             send_sem=send_sems.at[i], recv_sem=recv_sems.at[i],
                                             device_id=(xi, yi, 1 - ci), device_id_type=MESH) for i in range(n)]

    def start(x_refs, o_refs, sems):
        for cp in copies(x_refs, o_refs, sems):
            cp.start()

    def finish(x_refs, o_refs, sems):
        cps = copies(x_refs, o_refs, sems)
        for cp in cps:
            cp.wait_recv()
        for cp in cps:
            cp.wait_send()

    return _Comm(xs, out_shapes, {}, [n, n], start, finish)


def _chip_exchange(xs):
    n = len(xs)
    out_shapes = [jax.ShapeDtypeStruct((3,) + a.shape[1:], a.dtype) for a in xs]

    def copies(x_refs, o_refs, sems):
        send_sems, recv_sems = sems
        xi, yi, ci, peers = _place()
        return [pltpu.make_async_remote_copy(src_ref=x_refs[i].at[2 * px + py], dst_ref=o_refs[i].at[d],
                                             send_sem=send_sems.at[3 * i + d], recv_sem=recv_sems.at[3 * i + d],
                                             device_id=(px, py, ci), device_id_type=MESH)
                for i in range(n) for d, (px, py) in enumerate(peers)]

    def start(x_refs, o_refs, sems):
        for cp in copies(x_refs, o_refs, sems):
            cp.start()

    def finish(x_refs, o_refs, sems):
        cps = copies(x_refs, o_refs, sems)
        for cp in cps:
            cp.wait_recv()
        for cp in cps:
            cp.wait_send()

    return _Comm(xs, out_shapes, {}, [3 * n, 3 * n], start, finish)


def _inplace_call(body, name, bufs, sem_counts):
    n = len(bufs)
    return pl.pallas_call(
        body, name=name, out_shape=[jax.ShapeDtypeStruct(a.shape, a.dtype) for a in bufs],
        in_specs=[ANY] * n, out_specs=[ANY] * n, input_output_aliases={i: i for i in range(n)},
        scratch_shapes=[pltpu.SemaphoreType.DMA((k,)) for k in sem_counts],
        compiler_params=pltpu.CompilerParams(has_side_effects=True),
    )(*bufs)


def _gather_weights(bufs, whole=(), early=None):
    n, nw = len(bufs), len(whole)

    def copy(ref, s, r, k, dev):
        return pltpu.make_async_remote_copy(src_ref=ref, dst_ref=ref, send_sem=s.at[k], recv_sem=r.at[k],
                                            device_id=dev, device_id_type=MESH)

    def plan(w, sems):
        ici_s, ici_r, d2d_s, d2d_r = sems
        xi, yi, ci, peers = _place()
        xn, yn, sib = (1 - xi, yi, ci), (xi, 1 - yi, ci), (xi, yi, 1 - ci)
        j, jx, jy, jd = 2 * xi + yi, 2 * (1 - xi) + yi, 2 * xi + 1 - yi, 2 * (1 - xi) + 1 - yi
        c = {}
        for i in range(n):
            hh = bufs[i].shape[2] // 2
            lo, hi = pl.ds(0, hh), pl.ds(hh, hh)
            at = lambda slot, half, rows=None, i=i: w[i].at[slot, half] if rows is None else w[i].at[slot, half, rows]
            c["to_x", i] = (copy(at(j, ci), ici_s, ici_r, 4 * i, xn), copy(at(jx, ci), ici_s, ici_r, 4 * i, xn))
            c["to_y", i] = (copy(at(j, ci), ici_s, ici_r, 4 * i + 1, yn), copy(at(jy, ci), ici_s, ici_r, 4 * i + 1, yn))
            c["x_on", i] = (copy(at(jx, ci, lo), ici_s, ici_r, 4 * i + 2, yn), copy(at(jd, ci, lo), ici_s, ici_r, 4 * i + 2, yn))
            c["y_on", i] = (copy(at(jy, ci, hi), ici_s, ici_r, 4 * i + 3, xn), copy(at(jd, ci, hi), ici_s, ici_r, 4 * i + 3, xn))
            for k, slot in enumerate((jx, jy, jd)):
                c["sib", i, k] = (copy(at(slot, ci), d2d_s, d2d_r, 3 * i + k, sib),
                                  copy(at(slot, 1 - ci), d2d_s, d2d_r, 3 * i + k, sib))
        for i in range(nw):
            for d, (px, py) in enumerate(peers):
                k = 4 * n + 3 * i + d
                c["whole", i, d] = (copy(w[n + i].at[j], ici_s, ici_r, k, (px, py, ci)),
                                    copy(w[n + i].at[2 * px + py], ici_s, ici_r, k, (px, py, ci)))
        return c

    def start(_, w, sems):
        c = plan(w, sems)
        for i in range(n):
            c["to_x", i][0].start()
            c["to_y", i][0].start()
        for i in range(nw):
            for d in range(3):
                c["whole", i, d][0].start()

    def pass_on(c, which):
        for i in which:
            c["to_x", i][1].wait_recv()
            c["x_on", i][0].start()
            c["sib", i, 0][0].start()
            c["to_y", i][1].wait_recv()
            c["y_on", i][0].start()
            c["sib", i, 1][0].start()

    n_early = n if early is None else min(early, n)

    def middle(_, w, sems):
        pass_on(plan(w, sems), range(n_early))

    def finish(_, w, sems):
        c = plan(w, sems)
        pass_on(c, range(n_early, n))
        for i in range(n):
            c["x_on", i][1].wait_recv()
            c["y_on", i][1].wait_recv()
            c["sib", i, 2][0].start()
        for i in range(n):
            for k in range(3):
                c["sib", i, k][1].wait_recv()
        for i in range(nw):
            for d in range(3):
                c["whole", i, d][1].wait_recv()
        for key, (sent, _) in c.items():
            sent.wait_send()

    arrays = list(bufs) + list(whole)
    k = 4 * n + 3 * nw
    return _Comm(arrays, [jax.ShapeDtypeStruct(a.shape, a.dtype) for a in arrays], {i: i for i in range(n + nw)},
                 [k, k, max(3 * n, 1), max(3 * n, 1)], start, finish, middle)


def _pair_gather(bufs, spread, name):
    n = len(bufs)
    where = [(i, l) for i, a in enumerate(bufs) for l in range(a.shape[0])]
    nw = len(where)
    arrays = list(bufs) + ([] if spread is None else [spread])
    na = len(arrays)

    def body(*refs):
        g = refs[na:na + n]
        sp = refs[na + n] if spread is not None else None
        send_sems, recv_sems = refs[2 * na:]
        xi, yi, ci, peers = _place()
        chips = [(xi, yi)] + peers
        others = [(d, f) for d in range(4) for f in (0, 1) if (d, f) != (0, 0)] if spread is not None else []
        sends = []
        for k, (i, l) in enumerate(where):
            mine = g[i].at[l, :, ci]
            cp = pltpu.make_async_remote_copy(src_ref=mine, dst_ref=mine, send_sem=send_sems.at[k],
                                              recv_sem=recv_sems.at[k], device_id=(xi, yi, 1 - ci), device_id_type=MESH)
            cp.start()
            sends.append(cp)
        for k, (d, f) in enumerate(others):
            mine = sp.at[2 * xi + yi, ci]
            cp = pltpu.make_async_remote_copy(src_ref=mine, dst_ref=mine, send_sem=send_sems.at[nw + k],
                                              recv_sem=recv_sems.at[nw + k],
                                              device_id=(chips[d][0], chips[d][1], ci if f == 0 else 1 - ci),
                                              device_id_type=MESH)
            cp.start()
            sends.append(cp)
        for k, (i, l) in enumerate(where):
            other = g[i].at[l, :, 1 - ci]
            pltpu.make_async_remote_copy(src_ref=other, dst_ref=other, send_sem=send_sems.at[k],
                                         recv_sem=recv_sems.at[k], device_id=(xi, yi, 1 - ci),
                                         device_id_type=MESH).wait_recv()
        for k, (d, f) in enumerate(others):
            px, py, pc = chips[d][0], chips[d][1], (ci if f == 0 else 1 - ci)
            theirs = sp.at[2 * px + py, pc]
            pltpu.make_async_remote_copy(src_ref=theirs, dst_ref=theirs, send_sem=send_sems.at[nw + k],
                                         recv_sem=recv_sems.at[nw + k], device_id=(px, py, pc),
                                         device_id_type=MESH).wait_recv()
        for cp in sends:
            cp.wait_send()

    n_sems = nw + (7 if spread is not None else 0)
    out = _inplace_call(body, name, arrays, [n_sems, n_sems])
    return out[:n], (out[n] if spread is not None else None)


def _row_block(rows, cols):
    want = max(8, (2 ** 18 // max(cols, 1)) // 8 * 8)
    if rows <= want:
        return rows
    best = 8
    for t in range(8, want + 1, 8):
        if rows % t == 0:
            best = t
    return best


def _pair_add(g4, recv, sel, out_dtype, name):
    A, _, H, C = g4.shape
    bh = _row_block(H, C)

    def body(sel_ref, g_ref, r_ref, o_ref):
        o_ref[...] = (g_ref[...].astype(F32) + r_ref[...].astype(F32)).astype(out_dtype)

    return pl.pallas_call(
        body, name=name, out_shape=jax.ShapeDtypeStruct((A, H, C), out_dtype),
        grid_spec=pltpu.PrefetchScalarGridSpec(
            num_scalar_prefetch=1, grid=(A, H // bh),
            in_specs=[pl.BlockSpec((None, None, bh, C), lambda a, i, s: (a, s[1], i, 0)),
                      pl.BlockSpec((None, bh, C), lambda a, i, s: (a, i, 0))],
            out_specs=pl.BlockSpec((None, bh, C), lambda a, i, s: (a, i, 0))),
        compiler_params=pltpu.CompilerParams(dimension_semantics=("arbitrary", "arbitrary"),
                                             vmem_limit_bytes=VMEM_LIMIT),
    )(sel, g4, recv)


def _chip_reduce(g4, recv1, recv2, sel, buf, layer, n_layers, name, by_chip=False):
    _, _, H, C = g4.shape
    bh = _row_block(H, C)
    if by_chip:
        out_shape = jax.ShapeDtypeStruct((N_CHIPS, 2, H, C), F32)
        out_spec = pl.BlockSpec((None, None, bh, C), lambda i, s: (s[0], s[1], i, 0))
    else:
        out_shape = jax.ShapeDtypeStruct((n_layers, 1, 2, H, C), F32)
        out_spec = pl.BlockSpec((None, None, None, bh, C), lambda i, s: (layer, 0, s[1], i, 0))

    def body(sel_ref, g_ref, r1_ref, r2_ref, *rest):
        acc = g_ref[...].astype(F32) + r1_ref[...].astype(F32)
        for d in range(3):
            acc = acc + r2_ref[d].astype(F32)
        rest[-1][...] = acc

    in_specs = [pl.BlockSpec((None, None, bh, C), lambda i, s: (s[0], s[1], i, 0)),
                pl.BlockSpec((None, bh, C), lambda i, s: (s[0], i, 0)),
                pl.BlockSpec((3, bh, C), lambda i, s: (0, i, 0))]
    args = [sel, g4, recv1, recv2]
    aliases = {}
    if buf is not None:
        in_specs.append(ANY)
        args.append(buf)
        aliases = {4: 0}
    return pl.pallas_call(
        body, name=name, out_shape=out_shape,
        grid_spec=pltpu.PrefetchScalarGridSpec(num_scalar_prefetch=1, grid=(H // bh,), in_specs=in_specs,
                                               out_specs=out_spec),
        input_output_aliases=aliases,
        compiler_params=pltpu.CompilerParams(dimension_semantics=("arbitrary",), vmem_limit_bytes=VMEM_LIMIT),
    )(*args)


def _chip_reduce_many(groups, sel, comm, name):
    flat = [t for grp in groups for t in grp]
    n, ng = len(flat), len(groups)
    _, _, H, C = flat[0][0].shape
    bh = _row_block(H, C)
    last = H // bh - 1
    nci, nco = len(comm.ins), len(comm.out_shapes)

    def body(sel_ref, *refs):
        ins, refs = refs[:3 * n], refs[3 * n:]
        cins, refs = refs[:nci], refs[nci:]
        outs, refs = refs[:ng], refs[ng:]
        couts, sems = refs[:nco], refs[nco:]

        @pl.when(pl.program_id(0) == 0)
        def _():
            comm.start(cins, couts, sems)

        k = 0
        for gi, grp in enumerate(groups):
            for layer in range(len(grp)):
                g_ref, r1_ref, r2_ref = ins[3 * k:3 * k + 3]
                acc = g_ref[...].astype(F32) + r1_ref[...].astype(F32)
                for d in range(3):
                    acc = acc + r2_ref[d].astype(F32)
                outs[gi][layer] = acc
                k += 1

        @pl.when(pl.program_id(0) == last)
        def _():
            if comm.middle is not None:
                comm.middle(cins, couts, sems)
            comm.finish(cins, couts, sems)

    in_specs = [pl.BlockSpec((None, None, bh, C), lambda i, s: (s[0], s[1], i, 0)),
                pl.BlockSpec((None, bh, C), lambda i, s: (s[0], i, 0)),
                pl.BlockSpec((3, bh, C), lambda i, s: (0, i, 0))] * n + [ANY] * nci
    out_specs = [pl.BlockSpec((len(grp), None, None, bh, C), lambda i, s: (0, 0, s[1], i, 0)) for grp in groups]
    res = pl.pallas_call(
        body, name=name,
        out_shape=[jax.ShapeDtypeStruct((len(grp), 1, 2, H, C), F32) for grp in groups] + list(comm.out_shapes),
        grid_spec=pltpu.PrefetchScalarGridSpec(
            num_scalar_prefetch=1, grid=(H // bh,), in_specs=in_specs, out_specs=out_specs + [ANY] * nco,
            scratch_shapes=[pltpu.SemaphoreType.DMA((k,)) for k in comm.sem_counts]),
        input_output_aliases={1 + 3 * n + a: ng + b for a, b in comm.aliases.items()},
        compiler_params=pltpu.CompilerParams(dimension_semantics=("arbitrary",), vmem_limit_bytes=VMEM_LIMIT,
                                             has_side_effects=True),
    )(sel, *[a for t in flat for a in t], *comm.ins)
    return res[:ng], res[ng:]


def _cast_place(shards, layer, sel, name):
    _, _, H, C = shards.shape
    bh = _row_block(H, C)

    def body(sel_ref, x_ref, o_ref):
        o_ref[...] = x_ref[...].astype(BF16)

    return pl.pallas_call(
        body, name=name, out_shape=jax.ShapeDtypeStruct((N_CHIPS, 2, H, C), BF16),
        grid_spec=pltpu.PrefetchScalarGridSpec(
            num_scalar_prefetch=1, grid=(2, H // bh),
            in_specs=[pl.BlockSpec((None, None, bh, C), lambda h, i, s: (layer, h, i, 0))],
            out_specs=pl.BlockSpec((None, None, bh, C), lambda h, i, s: (s[0], h, i, 0))),
        compiler_params=pltpu.CompilerParams(dimension_semantics=("arbitrary", "arbitrary"),
                                             vmem_limit_bytes=VMEM_LIMIT),
    )(sel, shards)


def _cast_place_many(pairs, sel, comm, name):
    n = len(pairs)
    _, _, H, C = pairs[0][0].shape
    bh = _row_block(H, C)
    last = H // bh - 1
    nci, nco = len(comm.ins), len(comm.out_shapes)

    def body(sel_ref, *refs):
        xs, refs = refs[:n], refs[n:]
        cins, refs = refs[:nci], refs[nci:]
        outs, refs = refs[:n], refs[n:]
        couts, sems = refs[:nco], refs[nco:]
        h, i = pl.program_id(0), pl.program_id(1)

        @pl.when((h == 0) & (i == 0))
        def _():
            comm.start(cins, couts, sems)

        if comm.middle is not None:
            @pl.when((h == 1) & (i == 0))
            def _():
                comm.middle(cins, couts, sems)

        for x_ref, o_ref in zip(xs, outs):
            o_ref[...] = x_ref[...].astype(BF16)

        @pl.when((h == 1) & (i == last))
        def _():
            comm.finish(cins, couts, sems)

    res = pl.pallas_call(
        body, name=name,
        out_shape=[jax.ShapeDtypeStruct((N_CHIPS, 2, H, C), BF16)] * n + list(comm.out_shapes),
        grid_spec=pltpu.PrefetchScalarGridSpec(
            num_scalar_prefetch=1, grid=(2, H // bh),
            in_specs=[pl.BlockSpec((None, None, bh, C), lambda h, i, s, layer=layer: (layer, h, i, 0))
                      for _, layer in pairs] + [ANY] * nci,
            out_specs=[pl.BlockSpec((None, None, bh, C), lambda h, i, s: (s[0], h, i, 0))] * n + [ANY] * nco,
            scratch_shapes=[pltpu.SemaphoreType.DMA((k,)) for k in comm.sem_counts]),
        input_output_aliases={1 + n + a: n + b for a, b in comm.aliases.items()},
        compiler_params=pltpu.CompilerParams(dimension_semantics=("arbitrary", "arbitrary"),
                                             vmem_limit_bytes=VMEM_LIMIT, has_side_effects=True),
    )(sel, *[s for s, _ in pairs], *comm.ins)
    return res[:n], res[n:]


def _adamw_many(quads, name, comm=None):
    R, C = quads[0][0].shape
    br = _row_block(R, C)
    n = len(quads)

    def body(*refs):
        for k in range(n):
            g_ref, w_ref, m_ref, v_ref = refs[4 * k:4 * k + 4]
            d_ref, mo_ref, vo_ref = refs[4 * n + 3 * k:4 * n + 3 * k + 3]
            gv = g_ref[...]
            mn = ADAM_B1 * m_ref[...] + (1.0 - ADAM_B1) * gv
            vn = ADAM_B2 * v_ref[...] + (1.0 - ADAM_B2) * (gv * gv)
            m_hat = mn / (1.0 - ADAM_B1 ** ADAM_STEP)
            v_hat = vn / (1.0 - ADAM_B2 ** ADAM_STEP)
            d_ref[...] = -ADAM_LR * (m_hat / (jnp.sqrt(v_hat) + ADAM_EPS) + ADAM_WD * w_ref[...])
            mo_ref[...] = mn
            vo_ref[...] = vn

    spec = pl.BlockSpec((br, C), lambda i: (i, 0))
    res, got = _pcall(body, name=name, grid=(R // br,), comm=comm, out_shape=(jax.ShapeDtypeStruct((R, C), F32),) * (3 * n),
                      in_specs=[spec] * (4 * n), out_specs=(spec,) * (3 * n), args=[a for q in quads for a in q])
    return [res[3 * k:3 * k + 3] for k in range(n)], got


def _adamw(g, w, m, v, name):
    return _adamw_many([(g, w, m, v)], name)[0][0]


def _part_rows(shape):
    return -(-math.prod(shape) // (8 * LANES)) * 8


def _pack_rows(arrs, total_rows):
    parts = []
    for a in arrs:
        flat = a.reshape(-1)
        rows = _part_rows(a.shape)
        parts.append(jnp.pad(flat, (0, rows * LANES - flat.shape[0])).reshape(rows, LANES))
    packed = jnp.concatenate(parts, axis=0)
    return jnp.pad(packed, ((0, total_rows - packed.shape[0]), (0, 0)))


def _unpack_rows(packed, shapes):
    out, r = [], 0
    for shp in shapes:
        size, rows = math.prod(shp), _part_rows(shp)
        out.append(packed[r:r + rows].reshape(-1)[:size].reshape(shp))
        r += rows
    return out


def _rows_of(shapes):
    return sum(_part_rows(s) for s in shapes)


_GATHER_STAGES = {"start": ["wa", "mkv0", "wo0"], "mixer_a_fwd": ["wup0", "wdown0"],
                  "mlp_fwd_0": ["wq", "wkv", "mkv1", "wo1"], "fox_fwd": ["wup1", "wdown1"]}
_SWAP_STAGES = {"mixer_b_bwd": ["wup1", "wdown1"]}
_REDUCE_STAGES = {"fox_bwd": ["wup1", "wdown1"], "mlp_bwd_0": ["wo1", "wq", "wkv", "mkv1"],
                  "mixer_a_bwd": ["wup0", "wdown0"], "end": ["wa", "wo0", "mkv0", "small"]}
_SMALL = ["ws", "b_s", "sg", "sb", "bf", "ln_g", "ln_b", "loss"]


class _Staged:
    def __init__(self, placed, sel, dm, ng, shard_cols, n_small):
        self.placed, self.sel, self.dm, self.ng, self.shard_cols, self.n_small = placed, sel, dm, ng, shard_cols, n_small
        self.g4, self.recv1, self.recv2 = {}, {}, {}

    def _weight(self, name, a):
        a = a.reshape(N_CHIPS, 2 * a.shape[2], a.shape[3])
        if name in ("wa", "wkv"):
            a = a.transpose(1, 0, 2).reshape(a.shape[1], -1)
            if name == "wkv":
                a = jnp.concatenate([a[:, :2 * self.dm],
                                     jnp.pad(a[:, 2 * self.dm:], ((0, 0), (0, LANES - self.ng)))], axis=1)
            return a
        if name.startswith("wup") or name.startswith("wdown"):
            return a
        return a.reshape(-1, a.shape[2])

    def _partial(self, name, grads):
        if name == "small":
            a = _pack_rows([grads[n] for n in _SMALL], self.n_small).reshape(N_CHIPS, -1, LANES)
        elif name in self.shard_cols:
            n = self.shard_cols[name]
            g = grads[name][:, :N_CHIPS * n]
            a = g.reshape(g.shape[0], N_CHIPS, n).transpose(1, 0, 2)
        elif name.startswith("wup") or name.startswith("wdown"):
            a = grads[name]
        else:
            g = grads[name]
            a = g.reshape(N_CHIPS, g.shape[0] // N_CHIPS, g.shape[1])
        return a.reshape(N_CHIPS, 2, a.shape[1] // 2, a.shape[2])

    def plan(self, stage, w, grads):
        if stage in _GATHER_STAGES:
            return _gather_weights([self.placed[n] for n in _GATHER_STAGES[stage]],
                                   early=1 if stage == "mixer_a_fwd" else None)
        if stage in _SWAP_STAGES:
            for n in _SWAP_STAGES[stage]:
                self.g4[n] = self._partial(n, grads)
            return _pair_swap([self.g4[n] for n in _SWAP_STAGES[stage]])
        if stage in _REDUCE_STAGES:
            names = _REDUCE_STAGES[stage]
            late = [n for n in names if n not in self.recv1]
            for n in late:
                self.g4[n] = self._partial(n, grads)
            if late:
                got = _run_comm(_pair_swap([self.g4[n] for n in late]), "reduce_pair_swap_" + stage)
                self.recv1.update(zip(late, got))
            return _chip_exchange([_pair_add(self.g4[n], self.recv1[n], self.sel, F32 if n == "small" else BF16,
                                             "reduce_pair_add_" + n) for n in names])
        return None

    def done(self, stage, results, w):
        if stage in _GATHER_STAGES:
            for n, a in zip(_GATHER_STAGES[stage], results):
                w[n] = self._weight(n, a)
        elif stage in _SWAP_STAGES:
            self.recv1.update(zip(_SWAP_STAGES[stage], results))
        elif stage in _REDUCE_STAGES:
            self.recv2.update(zip(_REDUCE_STAGES[stage], results))


def kernel(x, mem, a_w_in, a_sgu_ln_g, a_sgu_ln_b, a_w_s, a_b_s, kv_w, kv_b_f, b_w_q, mem_w_kv, w_o, ln_g, ln_b, w_up, w_down, loss_target, m_a_w_in, m_a_sgu_ln_g, m_a_sgu_ln_b, m_a_w_s, m_a_b_s, m_kv_w, m_kv_b_f, m_b_w_q, m_mem_w_kv, m_w_o, m_ln_g, m_ln_b, m_w_up, m_w_down, v_a_w_in, v_a_sgu_ln_g, v_a_sgu_ln_b, v_a_w_s, v_a_b_s, v_kv_w, v_kv_b_f, v_b_w_q, v_mem_w_kv, v_w_o, v_ln_g, v_ln_b, v_w_up, v_w_down):
    xi, yi, ci = lax.axis_index("x"), lax.axis_index("y"), lax.axis_index("c")
    chip = 2 * xi + yi
    sel = jnp.stack([chip, ci]).astype(jnp.int32)
    S, D = x.shape[1], x.shape[2]
    DQ = mem_w_kv.shape[2] // 2
    DM = D - DQ
    NG = DM // HEAD_DIM

    shards = {"wa": (a_w_in, 0), "wkv": (kv_w[None], 0), "wq": (b_w_q, 0), "mkv0": (mem_w_kv, 0), "mkv1": (mem_w_kv, 1),
              "wo0": (w_o, 0), "wo1": (w_o, 1), "wup0": (w_up, 0), "wup1": (w_up, 1), "wdown0": (w_down, 0),
              "wdown1": (w_down, 1)}
    halves = lambda s: s.reshape(s.shape[0], 2, s.shape[1] // 2, s.shape[2])
    mlp = ["wup0", "wup1", "wdown0", "wdown1"]
    placed = {n: _cast_place(halves(s), layer, sel, "cast_place_" + n)
              for n, (s, layer) in shards.items() if n not in mlp}
    ln_pack = jnp.concatenate([ln_g.reshape(4, -1), ln_b.reshape(4, -1)], axis=0)
    ln_buf = lax.dynamic_update_slice(jnp.zeros((N_CHIPS,) + ln_pack.shape, F32), ln_pack[None], (chip, 0, 0))
    mlp_placed, got = _cast_place_many(
        [(halves(shards[n][0]), shards[n][1]) for n in mlp], sel,
        _gather_weights([placed[n] for n in _GATHER_STAGES["start"]], whole=[ln_buf]), "cast_place_mlp")
    placed.update(zip(mlp, mlp_placed))
    ln_shape = (4, D)
    small_shapes = [a_w_s.shape, a_b_s.shape, a_sgu_ln_g.shape, a_sgu_ln_b.shape, kv_b_f.shape, ln_shape, ln_shape,
                    (1, LANES)]
    n_small = -(-_rows_of(small_shapes) // 64) * 64
    hooks = _Staged(placed, sel, DM, NG, {"wa": a_w_in.shape[2], "wkv": kv_w.shape[1]}, n_small)
    weights = {"sg": a_sgu_ln_g, "sb": a_sgu_ln_b, "ws": a_w_s[0], "b_s": a_b_s[0], "bf": kv_b_f}
    hooks.done("start", got[:-1], weights)
    ln_full = got[-1].transpose(1, 0, 2).reshape(8, D)
    weights["ln_g"], weights["ln_b"] = ln_full[:4], ln_full[4:]

    _, grad_x, gr = _local_step(x[0], mem[0], loss_target[0], weights, hooks)

    def chip_sums(groups):
        bufs = []
        for grp in groups:
            buf = None
            for layer, n in enumerate(grp):
                buf = _chip_reduce(hooks.g4[n], hooks.recv1[n], hooks.recv2[n], sel, buf, layer, len(grp),
                                   "reduce_chip_sum_" + n)
            bufs.append(buf)
        return bufs

    whole = lambda a: a.reshape(a.shape[0], 2 * a.shape[3], a.shape[4])
    flat = lambda a: a.reshape(-1, a.shape[-1])

    def update(name, g, w, m, v):
        d, mn, vn = _adamw(flat(g), flat(w), flat(m), flat(v), "adamw_" + name)
        return d.reshape(w.shape), mn.reshape(w.shape), vn.reshape(w.shape)

    parts = lambda n: (hooks.g4[n], hooks.recv1[n], hooks.recv2[n])
    mlp_sums, got = _chip_reduce_many([[parts("wup0"), parts("wup1")], [parts("wdown0"), parts("wdown1")]], sel,
                                      hooks.plan("end", weights, gr), "reduce_chip_sum_mlp")
    hooks.done("end", got, weights)
    small_mine = _chip_reduce(hooks.g4["small"], hooks.recv1["small"], hooks.recv2["small"], sel, None, 0, 1,
                              "reduce_chip_sum_small", by_chip=True)
    red, small_all = _pair_gather(chip_sums([["wa"], ["wkv"], ["wq"], ["mkv0", "mkv1"], ["wo0", "wo1"]])
                                  + list(mlp_sums), small_mine, "reduce_pair_gather")
    g_a_w_in, g_kv_w, g_b_w_q, g_mem_w_kv, g_w_o, g_w_up, g_w_down = [whole(a) for a in red]
    g_kv_w = g_kv_w[0]
    g_w_down = g_w_down.transpose(0, 2, 1)
    small_all = small_all.reshape(n_small, LANES)
    g_ws, g_bs, g_sg, g_sb, g_bf, g_lng, g_lnb, loss_row = _unpack_rows(small_all, small_shapes)
    loss = loss_row[0, 0]
    dsh = D // N_CHIPS
    g_ln_g = lax.dynamic_slice_in_dim(g_lng, chip * dsh, dsh, axis=1).reshape(ln_g.shape)
    g_ln_b = lax.dynamic_slice_in_dim(g_lnb, chip * dsh, dsh, axis=1).reshape(ln_b.shape)

    upd = {
        "a_w_in": update("a_w_in", g_a_w_in, a_w_in, m_a_w_in, v_a_w_in),
        "kv_w": update("kv_w", g_kv_w, kv_w, m_kv_w, v_kv_w),
        "b_w_q": update("b_w_q", g_b_w_q, b_w_q, m_b_w_q, v_b_w_q),
        "mem_w_kv": update("mem_w_kv", g_mem_w_kv, mem_w_kv, m_mem_w_kv, v_mem_w_kv),
        "w_o": update("w_o", g_w_o, w_o, m_w_o, v_w_o),
        "w_up": update("w_up", g_w_up, w_up, m_w_up, v_w_up),
        "w_down": update("w_down", g_w_down, w_down, m_w_down, v_w_down),
    }
    tiny_g = [g_sg, g_sb, g_ws, g_bs, g_bf, g_ln_g, g_ln_b]
    tiny_w = [a_sgu_ln_g, a_sgu_ln_b, a_w_s, a_b_s, kv_b_f, ln_g, ln_b]
    tiny_m = [m_a_sgu_ln_g, m_a_sgu_ln_b, m_a_w_s, m_a_b_s, m_kv_b_f, m_ln_g, m_ln_b]
    tiny_v = [v_a_sgu_ln_g, v_a_sgu_ln_b, v_a_w_s, v_a_b_s, v_kv_b_f, v_ln_g, v_ln_b]
    tiny_shapes = [a.shape for a in tiny_w]
    n_tiny = -(-_rows_of(tiny_shapes) // 8) * 8
    td, tm_, tv = _adamw(_pack_rows(tiny_g, n_tiny), _pack_rows(tiny_w, n_tiny), _pack_rows(tiny_m, n_tiny),
                         _pack_rows(tiny_v, n_tiny), "adamw_small")
    for name, d, mn, vn in zip(["a_sgu_ln_g", "a_sgu_ln_b", "a_w_s", "a_b_s", "kv_b_f", "ln_g", "ln_b"],
                               _unpack_rows(td, tiny_shapes), _unpack_rows(tm_, tiny_shapes),
                               _unpack_rows(tv, tiny_shapes)):
        upd[name] = (d, mn, vn)

    order = ["a_w_in", "a_sgu_ln_g", "a_sgu_ln_b", "a_w_s", "a_b_s", "kv_w", "kv_b_f", "b_w_q", "mem_w_kv", "w_o",
             "ln_g", "ln_b", "w_up", "w_down"]
    grads = {"a_w_in": g_a_w_in.reshape(a_w_in.shape), "a_sgu_ln_g": g_sg.reshape(a_sgu_ln_g.shape),
             "a_sgu_ln_b": g_sb.reshape(a_sgu_ln_b.shape), "a_w_s": g_ws.reshape(a_w_s.shape),
             "a_b_s": g_bs.reshape(a_b_s.shape), "kv_w": g_kv_w, "kv_b_f": g_bf, "b_w_q": g_b_w_q.reshape(b_w_q.shape),
             "mem_w_kv": g_mem_w_kv, "w_o": g_w_o, "ln_g": g_ln_g, "ln_b": g_ln_b, "w_up": g_w_up, "w_down": g_w_down}
    return (loss, grad_x[None], *[grads[n] for n in order], *[upd[n][0] for n in order],
            *[upd[n][1] for n in order], *[upd[n][2] for n in order])
```

```python
import functools
import math

import jax
import jax.numpy as jnp
from jax import lax
from jax.experimental import pallas as pl
from jax.experimental.pallas import tpu as pltpu

F32 = jnp.float32
BF16 = jnp.bfloat16

HEAD_DIM = 64
PAIR = 2 * HEAD_DIM
CHUNK = 128
LN_EPS = 1e-5
ALPHA = 4 ** 0.25
QK_SCALE = 1.0 / math.sqrt(HEAD_DIM)
NEG = -1e30
N_CHIPS = 4
FOX_BLOCK = 256
LANES = 128
VMEM_LIMIT = 48 * 2 ** 20
VMEM_LIMIT_MLP_BWD = 56 * 2 ** 20

ADAM_LR, ADAM_B1, ADAM_B2, ADAM_EPS, ADAM_WD, ADAM_STEP = 0.001, 0.9, 0.999, 1e-08, 0.01, 10

MESH = pl.DeviceIdType.MESH
ANY = pl.BlockSpec(memory_space=pl.ANY)


def _dot(a, b):
    return jnp.dot(a, b, preferred_element_type=F32)


def _dot_nt(a, b):
    return lax.dot_general(a, b, (((1,), (1,)), ((), ())), preferred_element_type=F32)


def _dot_tn(a, b):
    return lax.dot_general(a, b, (((0,), (0,)), ((), ())), preferred_element_type=F32)


def _ln_stats(r):
    mu = jnp.mean(r, axis=-1, keepdims=True)
    d = r - mu
    var = jnp.mean(d * d, axis=-1, keepdims=True)
    rstd = lax.rsqrt(var + LN_EPS)
    return d * rstd, rstd


def _ln_bwd(dy, xhat, rstd, g):
    dxh = dy * g
    m1 = jnp.mean(dxh, axis=-1, keepdims=True)
    m2 = jnp.mean(dxh * xhat, axis=-1, keepdims=True)
    return rstd * (dxh - m1 - xhat * m2)


_GELU_K = math.sqrt(2.0 / math.pi)


def _gelu(x):
    return 0.5 * x * (1.0 + jnp.tanh(_GELU_K * (x + 0.044715 * x * x * x)))


def _gelu_grad(x):
    t = jnp.tanh(_GELU_K * (x + 0.044715 * x * x * x))
    return 0.5 * (1.0 + t) + 0.5 * x * (1.0 - t * t) * _GELU_K * (1.0 + 3 * 0.044715 * x * x)


def _lane_mask(e):
    lane = lax.broadcasted_iota(jnp.int32, (1, PAIR), 1)
    return (lane >= HEAD_DIM * e) & (lane < HEAD_DIM * (e + 1))


def _first_head():
    return lax.broadcasted_iota(jnp.int32, (1, PAIR), 1) < HEAD_DIM


def _mem_probs(qh, mkp):
    sc = _dot_nt(qh, mkp) * QK_SCALE
    ex = jnp.exp(sc - jnp.max(sc, axis=-1, keepdims=True))
    return ex / jnp.sum(ex, axis=-1, keepdims=True)


def _mem_attn_fwd(qm_b, mkv_ref, dq_dim):
    outs = []
    for mp in range(dq_dim // PAIR):
        qp = qm_b[:, mp * PAIR:(mp + 1) * PAIR]
        mkp = mkv_ref[:, mp * PAIR:(mp + 1) * PAIR]
        mvp = mkv_ref[:, dq_dim + mp * PAIR:dq_dim + (mp + 1) * PAIR]
        heads = []
        for e in (0, 1):
            qh = jnp.where(_lane_mask(e), qp, jnp.zeros_like(qp))
            p = _mem_probs(qh, mkp)
            heads.append(_dot(p.astype(BF16), mvp))
        outs.append(jnp.where(_first_head(), heads[0], heads[1]))
    return outs


def _mem_attn_bwd(qm_b, mkv_ref, dmo_pairs, dmkv_ref, dq_dim):
    dqs = []
    for mp in range(dq_dim // PAIR):
        ks = slice(mp * PAIR, (mp + 1) * PAIR)
        vs = slice(dq_dim + mp * PAIR, dq_dim + (mp + 1) * PAIR)
        qp = qm_b[:, ks]
        mkp = mkv_ref[:, ks]
        mvp = mkv_ref[:, vs]
        dmo_b = dmo_pairs[mp].astype(BF16)
        dq = None
        dmk = None
        dmv = None
        for e in (0, 1):
            hm = _lane_mask(e)
            qh = jnp.where(hm, qp, jnp.zeros_like(qp))
            p = _mem_probs(qh, mkp)
            doh = jnp.where(hm, dmo_b, jnp.zeros_like(dmo_b))
            dp = _dot_nt(doh, mvp)
            ds = p * (dp - jnp.sum(dp * p, axis=-1, keepdims=True))
            dsb = (ds * QK_SCALE).astype(BF16)
            kh = jnp.where(hm, mkp, jnp.zeros_like(mkp))
            dq_e = _dot(dsb, kh)
            dmk_e = _dot_tn(dsb, qh)
            dmv_e = _dot_tn(p.astype(BF16), doh)
            dq = dq_e if dq is None else dq + dq_e
            dmk = dmk_e if dmk is None else dmk + dmk_e
            dmv = dmv_e if dmv is None else dmv + dmv_e
        dmkv_ref[:, ks] += dmk
        dmkv_ref[:, vs] += dmv
        dqs.append(dq)
    return dqs


def _params(n_axes=1):
    return pltpu.CompilerParams(dimension_semantics=("arbitrary",) * n_axes, vmem_limit_bytes=VMEM_LIMIT)


def _full(shape):
    return pl.BlockSpec(shape, lambda *_: (0,) * len(shape))


def _rows(tm, cols):
    return pl.BlockSpec((tm, cols), lambda i: (i, 0))


def _cols(rows, tm):
    return pl.BlockSpec((rows, tm), lambda i: (0, i))


def _pick(n, pref):
    if n <= pref:
        return n
    best = LANES
    for t in range(LANES, pref + 1, LANES):
        if n % t == 0:
            best = t
    return best if 2 * best >= pref or n > 2 * pref else n


class _Comm:
    def __init__(self, ins, out_shapes, aliases, sem_counts, start, finish, middle=None):
        self.ins, self.out_shapes, self.aliases, self.sem_counts = list(ins), list(out_shapes), dict(aliases), sem_counts
        self.start, self.finish = start, finish
        self.middle = middle


def _pcall(body, *, name, grid, in_specs, out_specs, out_shape, args, scratch_shapes=(), comm=None, aliases=None,
           vmem_limit=VMEM_LIMIT):
    n_in, n_out, n_scr = len(in_specs), len(out_shape), len(scratch_shapes)
    aliases = dict(aliases or {})
    if comm is None:
        res = pl.pallas_call(body, name=name, grid=grid, out_shape=tuple(out_shape), in_specs=list(in_specs),
                             out_specs=tuple(out_specs), scratch_shapes=list(scratch_shapes),
                             input_output_aliases=aliases,
                             compiler_params=pltpu.CompilerParams(dimension_semantics=("arbitrary",),
                                                                  vmem_limit_bytes=vmem_limit))(*args)
        return tuple(res), ()
    nci, nco = len(comm.ins), len(comm.out_shapes)
    last = grid[0] - 1

    def wrapped(*refs):
        ins, refs = refs[:n_in], refs[n_in:]
        cins, refs = refs[:nci], refs[nci:]
        outs, refs = refs[:n_out], refs[n_out:]
        couts, refs = refs[:nco], refs[nco:]
        scr, sems = refs[:n_scr], refs[n_scr:]

        @pl.when(pl.program_id(0) == 0)
        def _():
            comm.start(cins, couts, sems)

        if comm.middle is not None:
            @pl.when(pl.program_id(0) == (last + 1) // 2)
            def _():
                comm.middle(cins, couts, sems)

        body(*ins, *outs, *scr)

        @pl.when(pl.program_id(0) == last)
        def _():
            comm.finish(cins, couts, sems)

    res = pl.pallas_call(
        wrapped, name=name, grid=grid, out_shape=tuple(out_shape) + tuple(comm.out_shapes),
        in_specs=list(in_specs) + [ANY] * nci, out_specs=tuple(out_specs) + (ANY,) * nco,
        input_output_aliases={**aliases, **{n_in + a: n_out + b for a, b in comm.aliases.items()}},
        scratch_shapes=list(scratch_shapes) + [pltpu.SemaphoreType.DMA((k,)) for k in comm.sem_counts],
        compiler_params=pltpu.CompilerParams(dimension_semantics=("arbitrary",), vmem_limit_bytes=vmem_limit,
                                             has_side_effects=True),
    )(*args, *comm.ins)
    return tuple(res[:n_out]), tuple(res[n_out:])


def _run_comm(comm, name):
    nci, nco = len(comm.ins), len(comm.out_shapes)

    def body(*refs):
        cins, couts, sems = refs[:nci], refs[nci:nci + nco], refs[nci + nco:]
        comm.start(cins, couts, sems)
        if comm.middle is not None:
            comm.middle(cins, couts, sems)
        comm.finish(cins, couts, sems)

    return pl.pallas_call(
        body, name=name, out_shape=tuple(comm.out_shapes), in_specs=[ANY] * nci, out_specs=(ANY,) * nco,
        input_output_aliases=comm.aliases, scratch_shapes=[pltpu.SemaphoreType.DMA((k,)) for k in comm.sem_counts],
        compiler_params=pltpu.CompilerParams(has_side_effects=True),
    )(*comm.ins)


def _mm(a, b, name, a_is_transposed, out_split=1):
    (M, R) = a.shape if a_is_transposed else a.shape[::-1]
    _, N = b.shape
    bm, br = _pick(M, 1024 if a_is_transposed else 512), _pick(R, 1024 if a_is_transposed else 512)
    ncol = N // out_split
    bn = _pick(ncol, 1024)
    per = ncol // bn
    last = R // br - 1

    def body(a_ref, b_ref, o_ref, acc_ref):
        @pl.when(pl.program_id(2) == 0)
        def _():
            acc_ref[...] = jnp.zeros_like(acc_ref)

        acc_ref[...] += (_dot if a_is_transposed else _dot_tn)(a_ref[...], b_ref[...])

        @pl.when(pl.program_id(2) == last)
        def _():
            o_ref[...] = acc_ref[...].astype(BF16)

    if out_split == 1:
        out_shape = jax.ShapeDtypeStruct((M, N), BF16)
        out_spec = pl.BlockSpec((bm, bn), lambda i, j, r: (i, j))
    else:
        out_shape = jax.ShapeDtypeStruct((out_split, M, ncol), BF16)
        out_spec = pl.BlockSpec((None, bm, bn), lambda i, j, r: (j // per, i, j % per))
    a_spec = (pl.BlockSpec((bm, br), lambda i, j, r: (i, r)) if a_is_transposed
              else pl.BlockSpec((br, bm), lambda i, j, r: (r, i)))
    return pl.pallas_call(
        body, name=name, out_shape=out_shape, grid=(M // bm, N // bn, R // br),
        in_specs=[a_spec, pl.BlockSpec((br, bn), lambda i, j, r: (r, j))], out_specs=out_spec,
        scratch_shapes=[pltpu.VMEM((bm, bn), F32)],
        compiler_params=pltpu.CompilerParams(dimension_semantics=("parallel", "parallel", "arbitrary"),
                                             vmem_limit_bytes=VMEM_LIMIT),
    )(a, b)


def _mm_tn(a, b, name, out_split=1):
    return _mm(a, b, name, False, out_split)


def _mm_nn(at, b, name, out_split=1):
    return _mm(at, b, name, True, out_split)


def _mm_small(a, b, name):
    def body(a_ref, b_ref, o_ref):
        o_ref[...] = _dot(a_ref[...], b_ref[...]).astype(BF16)

    return pl.pallas_call(body, name=name, out_shape=jax.ShapeDtypeStruct((a.shape[0], b.shape[1]), BF16),
                          compiler_params=pltpu.CompilerParams(vmem_limit_bytes=VMEM_LIMIT))(a, b)


def _mlp_out(x_ref, wup_ref, wdown_ref, g_ref, b_ref, r_ref, h_ref):
    NC, _, FC = wup_ref.shape
    xv = x_ref[...]
    xb = xv.astype(BF16)
    y = jnp.zeros(xv.shape, F32)
    for c in range(NC):
        a = jnp.maximum(_dot(xb, wup_ref[c]), 0.0)
        hb = (a * a).astype(BF16)
        h_ref[:, c * FC:(c + 1) * FC] = hb
        y = y + _dot(hb, wdown_ref[c])
    r = ALPHA * xv + y
    r_ref[...] = r
    xhat, _ = _ln_stats(r)
    return xhat * g_ref[...] + b_ref[...]


def _mlp_fwd_loss(x, wup4, wdown4, g, b, target, name):
    S, D = x.shape
    NC, _, FC = wup4.shape
    tm = min(512, S)

    def body(x_ref, wup_ref, wdown_ref, g_ref, b_ref, t_ref, d_ref, r_ref, h_ref, l_ref):
        @pl.when(pl.program_id(0) == 0)
        def _():
            l_ref[...] = jnp.zeros_like(l_ref)

        err = _mlp_out(x_ref, wup_ref, wdown_ref, g_ref, b_ref, r_ref, h_ref) - t_ref[...]
        d_ref[...] = err * (1.0 / D)
        l_ref[...] += (0.5 / D) * jnp.sum(err * err)

    return pl.pallas_call(
        body, name=name, grid=(S // tm,),
        out_shape=(jax.ShapeDtypeStruct((S, D), F32), jax.ShapeDtypeStruct((S, D), F32),
                   jax.ShapeDtypeStruct((S, NC * FC), BF16), jax.ShapeDtypeStruct((8, LANES), F32)),
        in_specs=[_rows(tm, D), _full(wup4.shape), _full(wdown4.shape), _full((1, D)), _full((1, D)), _rows(tm, D)],
        out_specs=(_rows(tm, D), _rows(tm, D), _rows(tm, NC * FC), _full((8, LANES))),
        compiler_params=_params(),
    )(x, wup4, wdown4, g, b, target)


def _mlp_fwd(x, wup4, wdown4, g, b, name, comm=None):
    S, D = x.shape
    NC, _, FC = wup4.shape
    tm = min(512, S)

    def body(x_ref, wup_ref, wdown_ref, g_ref, b_ref, xo_ref, xb_ref, xbt_ref, r_ref, h_ref):
        xo = _mlp_out(x_ref, wup_ref, wdown_ref, g_ref, b_ref, r_ref, h_ref)
        xo_ref[...] = xo
        xb_ref[...] = xo.astype(BF16)
        xbt_ref[...] = xo.T.astype(BF16)

    return _pcall(
        body, name=name, grid=(S // tm,), comm=comm,
        out_shape=(jax.ShapeDtypeStruct((S, D), F32), jax.ShapeDtypeStruct((S, D), BF16),
                   jax.ShapeDtypeStruct((D, S), BF16), jax.ShapeDtypeStruct((S, D), F32),
                   jax.ShapeDtypeStruct((S, NC * FC), BF16)),
        in_specs=[_rows(tm, D), _full(wup4.shape), _full(wdown4.shape), _full((1, D)), _full((1, D))],
        out_specs=(_rows(tm, D), _rows(tm, D), _cols(D, tm), _rows(tm, D), _rows(tm, NC * FC)),
        args=(x, wup4, wdown4, g, b))


def _mlp_bwd(dxo, r, h, wup4, wdown4, g, name, comm=None):
    S, D = r.shape
    NC, _, FC = wup4.shape
    tm = min(512, S)

    def body(dxo_ref, r_ref, h_ref, wup_ref, wdown_ref, g_ref, dx_ref, drbt_ref, dpre_ref, dgb_ref):
        @pl.when(pl.program_id(0) == 0)
        def _():
            dgb_ref[...] = jnp.zeros_like(dgb_ref)

        xhat, rstd = _ln_stats(r_ref[...])
        dy = dxo_ref[...]
        dgb_ref[0:1, :] += jnp.sum(dy * xhat, axis=0, keepdims=True)
        dgb_ref[1:2, :] += jnp.sum(dy, axis=0, keepdims=True)
        dr = _ln_bwd(dy, xhat, rstd, g_ref[...])
        drb = dr.astype(BF16)
        drbt_ref[...] = dr.T.astype(BF16)
        dx = ALPHA * dr
        for c in range(NC):
            dh = _dot_nt(drb, wdown_ref[c])
            a = jnp.sqrt(h_ref[:, c * FC:(c + 1) * FC].astype(F32))
            dpre = (2.0 * a * dh).astype(BF16)
            dpre_ref[:, c * FC:(c + 1) * FC] = dpre
            dx = dx + _dot_nt(dpre, wup_ref[c])
        dx_ref[...] = dx

    return _pcall(
        body, name=name, grid=(S // tm,), comm=comm,
        out_shape=(jax.ShapeDtypeStruct((S, D), F32), jax.ShapeDtypeStruct((D, S), BF16),
                   jax.ShapeDtypeStruct((S, NC * FC), BF16), jax.ShapeDtypeStruct((2, D), F32)),
        in_specs=[_rows(tm, D), _rows(tm, D), _rows(tm, NC * FC), _full(wup4.shape), _full(wdown4.shape),
                  _full((1, D))],
        out_specs=(_rows(tm, D), _cols(D, tm), _rows(tm, NC * FC), _full((2, D))),
        args=(dxo, r, h, wup4, wdown4, g), vmem_limit=VMEM_LIMIT_MLP_BWD)


def _tail_fwd_store(xv, cat_ref, wo_ref, g_ref, b_ref, xo_ref, xbt_ref, r_ref, catt_ref):
    cat = cat_ref[...]
    y = _dot(cat, wo_ref[...])
    r = ALPHA * xv + y
    xhat, _ = _ln_stats(r)
    xo = xhat * g_ref[...] + b_ref[...]
    xo_ref[...] = xo
    xbt_ref[...] = xo.T.astype(BF16)
    catt_ref[...] = cat.astype(F32).T.astype(BF16)
    r_ref[...] = r


def _tail_bwd_head(dxo_ref, r_ref, g_ref, wo_ref, dgb_ref, drb_ref):
    xhat, rstd = _ln_stats(r_ref[...])
    dy = dxo_ref[...]
    dgb_ref[0:1, :] += jnp.sum(dy * xhat, axis=0, keepdims=True)
    dgb_ref[1:2, :] += jnp.sum(dy, axis=0, keepdims=True)
    dr = _ln_bwd(dy, xhat, rstd, g_ref[...])
    drb = dr.astype(BF16)
    drb_ref[...] = drb
    return dr, _dot_nt(drb, wo_ref[...])


def _causal(shape):
    return lax.broadcasted_iota(jnp.int32, shape, 1) <= lax.broadcasted_iota(jnp.int32, shape, 0)


def _sgu_mixed(ws_ref, bst_ref, gp, vpair):
    tril = _causal((CHUNK, CHUNK))
    w0 = jnp.where(tril, ws_ref[2 * gp], 0.0).astype(BF16)
    w1 = jnp.where(tril, ws_ref[2 * gp + 1], 0.0).astype(BF16)
    m0 = _dot(w0, vpair) + bst_ref[:, 2 * gp:2 * gp + 1]
    m1 = _dot(w1, vpair) + bst_ref[:, 2 * gp + 1:2 * gp + 2]
    return jnp.where(_first_head(), m0, m1), w0, w1


def _mixer_a_fwd(x, wa, sg, sb, ws, bst, mkv, wo, g, b, name, comm=None):
    S, D = x.shape
    DQ = mkv.shape[1] // 2
    DM = D - DQ
    NP = DM // PAIR
    tm = min(512, S)

    def body(x_ref, wa_ref, sg_ref, sb_ref, ws_ref, bst_ref, mkv_ref, wo_ref, g_ref, b_ref,
             xo_ref, xbt_ref, r_ref, catt_ref, xint_ref, zb_ref, z_s, zv_s, cat_ref):
        xv = x_ref[...]
        xint_ref[...] = xv.T.astype(BF16)
        z_s[...] = _dot(xv.astype(BF16), wa_ref[...])
        zb_ref[...] = z_s[...].astype(BF16)
        vhat, _ = _ln_stats(_gelu(z_s[:, DM:2 * DM]))
        zv_s[...] = (vhat * sg_ref[...] + sb_ref[...]).astype(BF16)
        for c in range(tm // CHUNK):
            rs = slice(c * CHUNK, (c + 1) * CHUNK)
            for gp in range(NP):
                ls = slice(gp * PAIR, (gp + 1) * PAIR)
                mixed, _, _ = _sgu_mixed(ws_ref, bst_ref, gp, zv_s[rs, ls])
                cat_ref[rs, ls] = (_gelu(z_s[rs, ls]) * mixed).astype(BF16)
        mo = _mem_attn_fwd(z_s[:, 2 * DM:].astype(BF16), mkv_ref, DQ)
        for mp in range(DQ // PAIR):
            cat_ref[:, DM + mp * PAIR:DM + (mp + 1) * PAIR] = mo[mp].astype(BF16)
        _tail_fwd_store(xv, cat_ref, wo_ref, g_ref, b_ref, xo_ref, xbt_ref, r_ref, catt_ref)

    return _pcall(
        body, name=name, grid=(S // tm,), comm=comm,
        out_shape=(jax.ShapeDtypeStruct((S, D), F32), jax.ShapeDtypeStruct((D, S), BF16),
                   jax.ShapeDtypeStruct((S, D), F32), jax.ShapeDtypeStruct((D, S), BF16),
                   jax.ShapeDtypeStruct((D, S), BF16), jax.ShapeDtypeStruct((S, 2 * DM + DQ), BF16)),
        in_specs=[_rows(tm, D), _full(wa.shape), _full(sg.shape), _full(sb.shape), _full(ws.shape), _full(bst.shape),
                  _full(mkv.shape), _full(wo.shape), _full((1, D)), _full((1, D))],
        out_specs=(_rows(tm, D), _cols(D, tm), _rows(tm, D), _cols(D, tm), _cols(D, tm), _rows(tm, 2 * DM + DQ)),
        scratch_shapes=[pltpu.VMEM((tm, 2 * DM + DQ), F32), pltpu.VMEM((tm, DM), BF16), pltpu.VMEM((tm, D), BF16)],
        args=(x, wa, sg, sb, ws, bst, mkv, wo, g, b))


def _mixer_a_bwd(zb, dxo, r, wa, sg, sb, ws, bst, mkv, wo, g, name, comm=None):
    S, D = r.shape
    DQ = mkv.shape[1] // 2
    DM = D - DQ
    NP = DM // PAIR
    NG = DM // HEAD_DIM
    tm = min(512, S)

    def body(zb_ref, dxo_ref, r_ref, wa_ref, sg_ref, sb_ref, ws_ref, bst_ref, mkv_ref, wo_ref, g_ref,
             dx_ref, dz_ref, drb_ref, dgb_ref, dsgb_ref, dws_ref, dbst_ref, dmkv_ref, z_s, zv_s, dzv_s):
        @pl.when(pl.program_id(0) == 0)
        def _():
            dgb_ref[...] = jnp.zeros_like(dgb_ref)
            dsgb_ref[...] = jnp.zeros_like(dsgb_ref)
            dws_ref[...] = jnp.zeros_like(dws_ref)
            dbst_ref[...] = jnp.zeros_like(dbst_ref)
            dmkv_ref[...] = jnp.zeros_like(dmkv_ref)

        dr, dcat = _tail_bwd_head(dxo_ref, r_ref, g_ref, wo_ref, dgb_ref, drb_ref)
        z_s[...] = zb_ref[...].astype(F32)
        vhat, vrstd = _ln_stats(_gelu(z_s[:, DM:2 * DM]))
        zv_s[...] = (vhat * sg_ref[...] + sb_ref[...]).astype(BF16)
        tril = _causal((CHUNK, CHUNK))
        lane = lax.broadcasted_iota(jnp.int32, (1, LANES), 1)
        for c in range(tm // CHUNK):
            rs = slice(c * CHUNK, (c + 1) * CHUNK)
            for gp in range(NP):
                ls = slice(gp * PAIR, (gp + 1) * PAIR)
                vpair = zv_s[rs, ls]
                mixed, w0, w1 = _sgu_mixed(ws_ref, bst_ref, gp, vpair)
                u_pre = z_s[rs, ls]
                dmix = dcat[rs, ls]
                dz_ref[rs, ls] = (dmix * mixed * _gelu_grad(u_pre)).astype(BF16)
                dmixed = dmix * _gelu(u_pre)
                first = _first_head()
                d0 = jnp.where(first, dmixed, 0.0)
                d1 = jnp.where(first, 0.0, dmixed)
                d0b = d0.astype(BF16)
                d1b = d1.astype(BF16)
                dzv_s[rs, ls] = _dot_tn(w0, d0b) + _dot_tn(w1, d1b)
                dws_ref[2 * gp] += jnp.where(tril, _dot_nt(d0b, vpair), 0.0)
                dws_ref[2 * gp + 1] += jnp.where(tril, _dot_nt(d1b, vpair), 0.0)
                dbst_ref[...] += (jnp.where(lane == 2 * gp, jnp.sum(d0, axis=-1, keepdims=True), 0.0)
                                  + jnp.where(lane == 2 * gp + 1, jnp.sum(d1, axis=-1, keepdims=True), 0.0))
        dzv = dzv_s[...]
        dsgb_ref[0:1, :] += jnp.sum(dzv * vhat, axis=0, keepdims=True)
        dsgb_ref[1:2, :] += jnp.sum(dzv, axis=0, keepdims=True)
        dgv = _ln_bwd(dzv, vhat, vrstd, sg_ref[...])
        dz_ref[:, DM:2 * DM] = (dgv * _gelu_grad(z_s[:, DM:2 * DM])).astype(BF16)
        dmo = [dcat[:, DM + mp * PAIR:DM + (mp + 1) * PAIR] for mp in range(DQ // PAIR)]
        dqm = _mem_attn_bwd(z_s[:, 2 * DM:].astype(BF16), mkv_ref, dmo, dmkv_ref, DQ)
        for mp in range(DQ // PAIR):
            dz_ref[:, 2 * DM + mp * PAIR:2 * DM + (mp + 1) * PAIR] = dqm[mp].astype(BF16)
        dx_ref[...] = ALPHA * dr + _dot_nt(dz_ref[...], wa_ref[...])

    ZW = 2 * DM + DQ
    M = mkv.shape[0]
    return _pcall(
        body, name=name, grid=(S // tm,), comm=comm,
        out_shape=(jax.ShapeDtypeStruct((S, D), F32), jax.ShapeDtypeStruct((S, ZW), BF16),
                   jax.ShapeDtypeStruct((S, D), BF16), jax.ShapeDtypeStruct((2, D), F32),
                   jax.ShapeDtypeStruct((2, DM), F32), jax.ShapeDtypeStruct((NG, CHUNK, CHUNK), F32),
                   jax.ShapeDtypeStruct((CHUNK, LANES), F32), jax.ShapeDtypeStruct((M, 2 * DQ), F32)),
        in_specs=[_rows(tm, ZW), _rows(tm, D), _rows(tm, D), _full(wa.shape), _full(sg.shape), _full(sb.shape),
                  _full(ws.shape), _full(bst.shape), _full(mkv.shape), _full(wo.shape), _full((1, D))],
        out_specs=(_rows(tm, D), _rows(tm, ZW), _rows(tm, D), _full((2, D)), _full((2, DM)),
                   _full((NG, CHUNK, CHUNK)), _full((CHUNK, LANES)), _full((M, 2 * DQ))),
        scratch_shapes=[pltpu.VMEM((tm, ZW), F32), pltpu.VMEM((tm, DM), BF16), pltpu.VMEM((tm, DM), F32)],
        args=(zb, dxo, r, wa, sg, sb, ws, bst, mkv, wo, g))


def _proj_b_fwd(xb, wq, wkv, dm, name):
    S, D = xb.shape
    DQ = D - dm
    tm = min(512, S)

    def body(x_ref, wq_ref, wkv_ref, q_ref, qm_ref, k_ref, v_ref, fl_ref):
        xv = x_ref[...]
        z = _dot(xv, wq_ref[...])
        q_ref[...] = z[:, :dm].astype(BF16)
        qm_ref[...] = z[:, dm:].astype(BF16)
        kvf = _dot(xv, wkv_ref[...])
        k_ref[...] = kvf[:, :dm].astype(BF16)
        v_ref[...] = kvf[:, dm:2 * dm].astype(BF16)
        fl_ref[...] = kvf[:, 2 * dm:]

    return pl.pallas_call(
        body, name=name, grid=(S // tm,),
        out_shape=(jax.ShapeDtypeStruct((S, dm), BF16), jax.ShapeDtypeStruct((S, DQ), BF16),
                   jax.ShapeDtypeStruct((S, dm), BF16), jax.ShapeDtypeStruct((S, dm), BF16),
                   jax.ShapeDtypeStruct((S, LANES), F32)),
        in_specs=[_rows(tm, D), _full(wq.shape), _full(wkv.shape)],
        out_specs=(_rows(tm, dm), _rows(tm, DQ), _rows(tm, dm), _rows(tm, dm), _rows(tm, LANES)),
        compiler_params=_params(),
    )(xb, wq, wkv)


def _proj_b_bwd(dr, dqz, dk, dv, dfl, wq, wkv, name):
    S, D = dr.shape
    dm = dk.shape[1]
    tm = min(512, S)

    def body(dr_ref, dqz_ref, dk_ref, dv_ref, dfl_ref, wq_ref, wkv_ref, dx_ref):
        dx_ref[...] = (ALPHA * dr_ref[...] + _dot_nt(dqz_ref[...], wq_ref[...])
                       + _dot_nt(dk_ref[...], wkv_ref[:, :dm]) + _dot_nt(dv_ref[...], wkv_ref[:, dm:2 * dm])
                       + _dot_nt(dfl_ref[...], wkv_ref[:, 2 * dm:]))

    return pl.pallas_call(
        body, name=name, grid=(S // tm,), out_shape=jax.ShapeDtypeStruct((S, D), F32),
        in_specs=[_rows(tm, D), _rows(tm, dqz.shape[1]), _rows(tm, dm), _rows(tm, dm), _rows(tm, dfl.shape[1]),
                  _full(wq.shape), _full(wkv.shape)],
        out_specs=_rows(tm, D), compiler_params=_params(),
    )(dr, dqz, dk, dv, dfl, wq, wkv)


def _mm_nn_cat(at, bs, name):
    M, R = at.shape
    widths = [b.shape[1] for b in bs]
    N = sum(widths)
    br = _pick(R, 1024)
    last = R // br - 1
    nb = len(bs)

    def body(a_ref, *refs):
        b_refs, o_ref, acc_ref = refs[:nb], refs[nb], refs[nb + 1]

        @pl.when(pl.program_id(0) == 0)
        def _():
            acc_ref[...] = jnp.zeros_like(acc_ref)

        a = a_ref[...]
        c0 = 0
        for b_ref, wd in zip(b_refs, widths):
            acc_ref[:, c0:c0 + wd] += _dot(a, b_ref[...])
            c0 += wd

        @pl.when(pl.program_id(0) == last)
        def _():
            o_ref[...] = acc_ref[...].astype(BF16)

    return pl.pallas_call(
        body, name=name, grid=(R // br,), out_shape=jax.ShapeDtypeStruct((M, N), BF16),
        in_specs=[pl.BlockSpec((M, br), lambda r: (0, r))] + [pl.BlockSpec((br, wd), lambda r: (r, 0)) for wd in widths],
        out_specs=_full((M, N)), scratch_shapes=[pltpu.VMEM((M, N), F32)], compiler_params=_params(),
    )(at, *bs)


def _split3(v):
    hi = v.astype(BF16)
    r1 = v - hi.astype(F32)
    mid = r1.astype(BF16)
    lo = (r1 - mid.astype(F32)).astype(BF16)
    return hi, mid, lo


def _tri_sum(v, tri_b):
    hi, mid, lo = _split3(v)
    return _dot(hi, tri_b) + _dot(mid, tri_b) + _dot(lo, tri_b)


def _log_sigmoid(x):
    return jnp.minimum(x, 0.0) - jnp.log(1.0 + jnp.exp(-jnp.abs(x)))


def _forget_cumsum(flt, bf, name):
    H, S = flt.shape
    nchunk = S // LANES

    def body(fl_ref, bf_ref, p_ref):
        upper = (lax.broadcasted_iota(jnp.int32, (LANES, LANES), 0)
                 <= lax.broadcasted_iota(jnp.int32, (LANES, LANES), 1)).astype(BF16)

        def step(n, carry):
            s0 = pl.multiple_of(n * LANES, LANES)
            lf = _log_sigmoid(fl_ref[:, pl.ds(s0, LANES)] + bf_ref[...])
            cs = _tri_sum(lf, upper) + carry
            for i, piece in enumerate(_split3(-cs)):
                p_ref[i, :, pl.ds(s0, LANES)] = piece.astype(F32)
            return cs[:, LANES - 1:LANES]

        lax.fori_loop(0, nchunk, step, jnp.zeros((H, 1), F32))

    return pl.pallas_call(body, name=name, out_shape=jax.ShapeDtypeStruct((3, H, S), F32),
                          compiler_params=pltpu.CompilerParams(vmem_limit_bytes=VMEM_LIMIT))(flt, bf)


def _forget_cumsum_bwd(dct, flt, bf, name):
    H, S = flt.shape
    nchunk = S // LANES

    def body(dc_ref, fl_ref, bf_ref, dfl_ref, dbf_ref):
        lower = (lax.broadcasted_iota(jnp.int32, (LANES, LANES), 0)
                 >= lax.broadcasted_iota(jnp.int32, (LANES, LANES), 1)).astype(BF16)

        def step(n, carry):
            tail, tot = carry
            s0 = pl.multiple_of((nchunk - 1 - n) * LANES, LANES)
            suffix = _tri_sum(dc_ref[:, pl.ds(s0, LANES)], lower) + tail
            xv = fl_ref[:, pl.ds(s0, LANES)] + bf_ref[...]
            dfl = suffix * (1.0 / (1.0 + jnp.exp(xv)))
            dfl_ref[:, pl.ds(s0, LANES)] = dfl
            return suffix[:, 0:1], tot + jnp.sum(dfl, axis=-1, keepdims=True)

        _, tot = lax.fori_loop(0, nchunk, step, (jnp.zeros((H, 1), F32), jnp.zeros((H, 1), F32)))
        dbf_ref[...] = jnp.broadcast_to(tot, (H, LANES))

    return pl.pallas_call(body, name=name,
                          out_shape=(jax.ShapeDtypeStruct((H, S), F32), jax.ShapeDtypeStruct((H, LANES), F32)),
                          compiler_params=pltpu.CompilerParams(vmem_limit_bytes=VMEM_LIMIT))(dct, flt, bf)


N_BIAS = 3


def _bias_ones():
    lane = lax.broadcasted_iota(jnp.int32, (1, PAIR), 1)
    return ((lane & (HEAD_DIM - 1)) < N_BIAS).astype(BF16)


def _fox_fwd(q, k, v, ca, name, comm=None):
    S, DM = q.shape
    NP = DM // PAIR
    T = min(FOX_BLOCK, S)
    nq = S // T

    def body(q_ref, k_ref, v_ref, ca_ref, o_ref, lse_ref):
        tril = _causal((T, T))
        heads = (_lane_mask(0), _lane_mask(1))
        ones3 = _bias_ones()

        def kv_block(kj):
            s0 = pl.multiple_of(kj * T, T)
            kb, vb, cab = k_ref[pl.ds(s0, T), :], v_ref[pl.ds(s0, T), :], ca_ref[pl.ds(s0, T), :]
            return ([jnp.where(heads[e], kb, cab) for e in (0, 1)],
                    [jnp.where(heads[e], vb, jnp.ones_like(vb)) for e in (0, 1)])

        def q_pair(a, _):
            t0s = [pl.multiple_of((2 * a + r) * T, T) for r in (0, 1)]
            qaug = {}
            for r in (0, 1):
                qb = q_ref[pl.ds(t0s[r], T), :] * QK_SCALE
                for e in (0, 1):
                    qaug[r, e] = jnp.where(heads[e], qb, ones3)
            chains = [(r, e) for r in (0, 1) for e in (0, 1)]

            def update(carries, first_block, seen):
                blocks = [kv_block(first_block + kk) for kk in range(1 + max(kk for s in seen for kk, _ in s))]
                out = []
                scores = [[_dot_nt(qaug[r, e], blocks[kk][0][e]) for kk, _ in seen[r]] for r, e in chains]
                for n, (r, e) in enumerate(chains):
                    m, acc = carries[n]
                    tiles = [jnp.where(tril, s, NEG) if masked else s for s, (_, masked) in zip(scores[n], seen[r])]
                    m_new = jnp.maximum(m, jnp.max(functools.reduce(jnp.maximum, tiles), axis=-1, keepdims=True))
                    acc = jnp.exp(m - m_new) * acc
                    for s, (kk, _) in zip(tiles, seen[r]):
                        acc = acc + _dot(jnp.exp(s - m_new).astype(BF16), blocks[kk][1][e])
                    out.append((m_new, acc))
                return tuple(out)

            def full(first_block, n_blocks, carries):
                return update(carries, first_block, [[(kk, False) for kk in range(n_blocks)]] * 2)

            init = (jnp.full((T, 1), NEG, F32), jnp.zeros((T, PAIR), F32))
            carries = lax.fori_loop(0, a // 2, lambda i, c: full(4 * i, 4, c), (init,) * 4)
            carries = lax.fori_loop(0, a % 2, lambda i, c: full(2 * a - 2, 2, c), carries)
            carries = update(carries, 2 * a, [[(0, True)], [(0, False), (1, True)]])
            first = _first_head()
            for r in (0, 1):
                (m0, acc0), (m1, acc1) = carries[2 * r], carries[2 * r + 1]
                l0, l1 = acc0[:, HEAD_DIM:HEAD_DIM + 1], acc1[:, 0:1]
                o_ref[pl.ds(t0s[r], T), :] = jnp.where(first, acc0 / l0, acc1 / l1).astype(BF16)
                lse_t = jnp.where(first, m0 + jnp.log(l0), m1 + jnp.log(l1)).T
                lse_ref[0:1, pl.ds(t0s[r], T)] = lse_t[0:1, :]
                lse_ref[1:2, pl.ds(t0s[r], T)] = lse_t[HEAD_DIM:HEAD_DIM + 1, :]
            return 0

        lax.fori_loop(0, nq // 2, q_pair, 0)

    col = pl.BlockSpec((S, PAIR), lambda hp: (0, hp))
    return _pcall(
        body, name=name, grid=(NP,), comm=comm,
        out_shape=(jax.ShapeDtypeStruct((S, DM), BF16), jax.ShapeDtypeStruct((NP, 2, S), F32)),
        in_specs=[col, col, col, col], out_specs=(col, pl.BlockSpec((None, 2, S), lambda hp: (hp, 0, 0))),
        args=(q, k, v, ca))


def _fox_bwd(q, k, v, o, do, ca, lse, name, comm=None):
    S, DM = q.shape
    NP = DM // PAIR
    T = min(FOX_BLOCK, S)
    nq = S // T

    def body(q_ref, k_ref, v_ref, o_ref, do_ref, ca_ref, lse_ref, dq_ref, dk_ref, dv_ref, dc_ref,
             dq_acc, dcol_acc, dk_acc, dv_acc, dd_s):
        keep = lax.broadcasted_iota(jnp.int32, (T, T), 0) <= lax.broadcasted_iota(jnp.int32, (T, T), 1)
        lane = lax.broadcasted_iota(jnp.int32, (1, LANES), 1)
        heads = (_lane_mask(0), _lane_mask(1))
        ones3 = _bias_ones()
        dq_acc[...] = jnp.zeros_like(dq_acc)
        dcol_acc[...] = jnp.zeros_like(dcol_acc)
        dc_ref[...] = jnp.zeros_like(dc_ref)

        def row_dots(qi, _):
            t0 = pl.multiple_of(qi * T, T)
            prod = do_ref[pl.ds(t0, T), :].astype(F32) * o_ref[pl.ds(t0, T), :].astype(F32)
            prod_t = prod.T
            dd_s[0:1, pl.ds(t0, T)] = jnp.sum(prod_t[0:HEAD_DIM, :], axis=0, keepdims=True)
            dd_s[1:2, pl.ds(t0, T)] = jnp.sum(prod_t[HEAD_DIM:PAIR, :], axis=0, keepdims=True)
            return 0

        lax.fori_loop(0, nq, row_dots, 0)

        def kv_pair(b, _):
            dk_acc[...] = jnp.zeros_like(dk_acc)
            dv_acc[...] = jnp.zeros_like(dv_acc)
            s0s = [pl.multiple_of((2 * b + jj) * T, T) for jj in (0, 1)]
            kaug, ks, vbs = {}, {}, []
            for jj in (0, 1):
                kb = k_ref[pl.ds(s0s[jj], T), :]
                cab = ca_ref[pl.ds(s0s[jj], T), :]
                vbs.append(v_ref[pl.ds(s0s[jj], T), :])
                for e in (0, 1):
                    kaug[jj, e] = jnp.where(heads[e], kb, cab)
                    ks[jj, e] = jnp.where(heads[e], kb, jnp.zeros_like(kb)) * QK_SCALE

            def q_step(qi, blocks):
                t0 = pl.multiple_of(qi * T, T)
                qs = q_ref[pl.ds(t0, T), :] * QK_SCALE
                dob = do_ref[pl.ds(t0, T), :]
                doh = [jnp.where(heads[e], dob, jnp.zeros_like(dob)) for e in (0, 1)]
                qh = [jnp.where(heads[e], qs, jnp.zeros_like(qs)) for e in (0, 1)]
                chains = [(e, jj, masked) for e in (0, 1) for jj, masked in blocks]
                scores = [_dot_nt(kaug[jj, e], jnp.where(heads[e], qs, ones3)) for e, jj, _ in chains]
                dps = [_dot_nt(vbs[jj], doh[e]) for e, jj, _ in chains]
                dq = jnp.zeros((T, PAIR), F32)
                drow = [jnp.zeros((1, T), F32), jnp.zeros((1, T), F32)]
                for n, (e, jj, masked) in enumerate(chains):
                    st = jnp.where(keep, scores[n], NEG) if masked else scores[n]
                    pt = jnp.exp(st - lse_ref[e:e + 1, pl.ds(t0, T)])
                    dv_acc[jj] += _dot(pt.astype(BF16), doh[e])
                    dst = pt * (dps[n] - dd_s[e:e + 1, pl.ds(t0, T)])
                    dsb = dst.astype(BF16)
                    dk_acc[jj] += _dot(dsb, qh[e])
                    dq = dq + _dot_tn(dsb, ks[jj, e])
                    dcol_acc[pl.ds(s0s[jj], T), :] -= jnp.where(lane == e, jnp.sum(dst, axis=-1, keepdims=True), 0.0)
                    drow[e] = drow[e] + jnp.sum(dst, axis=0, keepdims=True)
                for e in (0, 1):
                    dc_ref[e:e + 1, pl.ds(t0, T)] += drow[e]
                dq_acc[pl.ds(t0, T), :] += dq

            q_step(2 * b, [(0, True)])
            q_step(2 * b + 1, [(0, False), (1, True)])

            def rest(i, _):
                q_step(2 * b + 2 + 2 * i, [(0, False), (1, False)])
                q_step(2 * b + 3 + 2 * i, [(0, False), (1, False)])
                return 0

            lax.fori_loop(0, (nq - 2) // 2 - b, rest, 0)
            for jj in (0, 1):
                dk_ref[pl.ds(s0s[jj], T), :] = dk_acc[jj].astype(BF16)
                dv_ref[pl.ds(s0s[jj], T), :] = dv_acc[jj].astype(BF16)
            return 0

        lax.fori_loop(0, nq // 2, kv_pair, 0)
        dq_ref[...] = dq_acc[...].astype(BF16)

        def add_cols(kj, _):
            s0 = pl.multiple_of(kj * T, T)
            dc_ref[:, pl.ds(s0, T)] += dcol_acc[pl.ds(s0, T), :].T[0:2, :]
            return 0

        lax.fori_loop(0, nq, add_cols, 0)

    col = pl.BlockSpec((S, PAIR), lambda hp: (0, hp))
    row2 = pl.BlockSpec((None, 2, S), lambda hp: (hp, 0, 0))
    return _pcall(
        body, name=name, grid=(NP,), comm=comm,
        out_shape=(jax.ShapeDtypeStruct(do.shape, BF16), jax.ShapeDtypeStruct((S, DM), BF16),
                   jax.ShapeDtypeStruct((S, DM), BF16), jax.ShapeDtypeStruct((NP, 2, S), F32)),
        in_specs=[col, col, col, col, col, col, row2], out_specs=(col, col, col, row2),
        scratch_shapes=[pltpu.VMEM((S, PAIR), F32), pltpu.VMEM((S, LANES), F32), pltpu.VMEM((2, T, PAIR), F32),
                        pltpu.VMEM((2, T, PAIR), F32), pltpu.VMEM((8, S), F32)],
        args=(q, k, v, o, do, ca, lse), aliases={4: 0})


def _mixer_b_fwd(x, o, qm, mkv, wo, g, b, name):
    S, D = x.shape
    DQ = qm.shape[1]
    DM = D - DQ
    tm = min(512, S)

    def body(x_ref, o_ref, qm_ref, mkv_ref, wo_ref, g_ref, b_ref, xo_ref, xbt_ref, r_ref, catt_ref, cat_ref):
        cat_ref[:, :DM] = o_ref[...]
        mo = _mem_attn_fwd(qm_ref[...], mkv_ref, DQ)
        for mp in range(DQ // PAIR):
            cat_ref[:, DM + mp * PAIR:DM + (mp + 1) * PAIR] = mo[mp].astype(BF16)
        _tail_fwd_store(x_ref[...], cat_ref, wo_ref, g_ref, b_ref, xo_ref, xbt_ref, r_ref, catt_ref)

    return pl.pallas_call(
        body, name=name, grid=(S // tm,),
        out_shape=(jax.ShapeDtypeStruct((S, D), F32), jax.ShapeDtypeStruct((D, S), BF16),
                   jax.ShapeDtypeStruct((S, D), F32), jax.ShapeDtypeStruct((D, S), BF16)),
        in_specs=[_rows(tm, D), _rows(tm, DM), _rows(tm, DQ), _full(mkv.shape), _full(wo.shape), _full((1, D)),
                  _full((1, D))],
        out_specs=(_rows(tm, D), _cols(D, tm), _rows(tm, D), _cols(D, tm)),
        scratch_shapes=[pltpu.VMEM((tm, D), BF16)],
        compiler_params=_params(),
    )(x, o, qm, mkv, wo, g, b)


def _mixer_b_bwd(dxo, r, qm, mkv, wo, g, name, comm=None):
    S, D = r.shape
    DQ = qm.shape[1]
    DM = D - DQ
    M = mkv.shape[0]
    tm = min(512, S)

    def body(dxo_ref, r_ref, qm_ref, mkv_ref, wo_ref, g_ref, dr_ref, drb_ref, dcat_ref, dgb_ref, dmkv_ref):
        @pl.when(pl.program_id(0) == 0)
        def _():
            dgb_ref[...] = jnp.zeros_like(dgb_ref)
            dmkv_ref[...] = jnp.zeros_like(dmkv_ref)

        dr, dcat = _tail_bwd_head(dxo_ref, r_ref, g_ref, wo_ref, dgb_ref, drb_ref)
        dr_ref[...] = dr
        dcat_ref[:, :DM] = dcat[:, :DM].astype(BF16)
        dmo = [dcat[:, DM + mp * PAIR:DM + (mp + 1) * PAIR] for mp in range(DQ // PAIR)]
        dqm = _mem_attn_bwd(qm_ref[...], mkv_ref, dmo, dmkv_ref, DQ)
        for mp in range(DQ // PAIR):
            dcat_ref[:, DM + mp * PAIR:DM + (mp + 1) * PAIR] = dqm[mp].astype(BF16)

    return _pcall(
        body, name=name, grid=(S // tm,), comm=comm,
        out_shape=(jax.ShapeDtypeStruct((S, D), F32), jax.ShapeDtypeStruct((S, D), BF16),
                   jax.ShapeDtypeStruct((S, D), BF16), jax.ShapeDtypeStruct((2, D), F32),
                   jax.ShapeDtypeStruct((M, 2 * DQ), F32)),
        in_specs=[_rows(tm, D), _rows(tm, D), _rows(tm, DQ), _full(mkv.shape), _full(wo.shape), _full((1, D))],
        out_specs=(_rows(tm, D), _rows(tm, D), _rows(tm, D), _full((2, D)), _full((M, 2 * DQ))),
        args=(dxo, r, qm, mkv, wo, g))


class _NoExchange:
    def plan(self, stage, w, grads):
        return None

    def done(self, stage, results, w):
        pass


def _local_step(x, mem, target, w, hooks):
    S, D = x.shape
    DQ = w["mkv0"].shape[1] // 2
    DM = D - DQ
    NG = DM // HEAD_DIM
    NP = DM // PAIR
    ln_g, ln_b = w["ln_g"], w["ln_b"]
    row = lambda a, i: a[i:i + 1]
    memb = mem.astype(BF16)
    bst =jnp.pad(w["b_s"].T, ((0, 0), (0, LANES - NG)))
    grads = {}

    def staged(stage, fn, *args):
        out, got = fn(*args, stage, comm=hooks.plan(stage, w, grads))
        hooks.done(stage, got, w)
        return out

    mkv0 = _mm_small(memb, w["mkv0"], "mem_kv_0")
    x1, x1t, r1, cat0t, x0t, z0b = staged("mixer_a_fwd", _mixer_a_fwd, x, w["wa"], w["sg"], w["sb"], w["ws"], bst, mkv0,
                                w["wo0"], row(ln_g, 0), row(ln_b, 0))
    x2, x2b, x2t, r2, h0 = staged("mlp_fwd_0", _mlp_fwd, x1, w["wup0"], w["wdown0"], row(ln_g, 1), row(ln_b, 1))
    q, qm, k, v, fl = _proj_b_fwd(x2b, w["wq"], w["wkv"], DM, "proj_b_fwd")
    flt = fl[:, :16].T
    bfc = jnp.pad(w["bf"], (0, 16 - NG)).reshape(16, 1)
    pieces = _forget_cumsum(flt, bfc, "forget_cumsum")
    pieces = jnp.pad(pieces[:, :NG].transpose(2, 1, 0), ((0, 0), (0, 0), (0, HEAD_DIM - N_BIAS)))
    ca = pieces.reshape(S, NP, 2, HEAD_DIM)[:, :, ::-1].reshape(S, DM).astype(BF16)
    o, lse = staged("fox_fwd", _fox_fwd, q, k, v, ca)
    mkv1 = _mm_small(memb, w["mkv1"], "mem_kv_1")
    x3, x3t, r3, cat1t = _mixer_b_fwd(x2, o, qm, mkv1, w["wo1"], row(ln_g, 2), row(ln_b, 2), "mixer_b_fwd")
    dx4, r4, h1, loss_part = _mlp_fwd_loss(x3, w["wup1"], w["wdown1"], row(ln_g, 3), row(ln_b, 3), target,
                                           "mlp_fwd_1")

    nc = w["wup0"].shape[0]
    (dx3, dr4t, dpre1, dgb11), _ = _mlp_bwd(dx4, r4, h1, w["wup1"], w["wdown1"], row(ln_g, 3), "mlp_bwd_1")
    grads["wdown1"] = _mm_nn(dr4t, h1, "dw_down_1", out_split=nc)
    grads["wup1"] = _mm_nn(x3t, dpre1, "dw_up_1", out_split=nc)
    dr3, dr3b, dcat1, dgb10, dmkv1 = staged("mixer_b_bwd", _mixer_b_bwd, dx3, r3, qm, mkv1, w["wo1"], row(ln_g, 2))
    dqz, dk, dv, dct3 = staged("fox_bwd", _fox_bwd, q, k, v, o, dcat1, ca, lse)
    dct = jnp.pad(dct3.reshape(NG, S), ((0, 16 - NG), (0, 0)))
    dflt, dbf = _forget_cumsum_bwd(dct, flt, bfc, "forget_cumsum_bwd")
    dfl = jnp.pad(dflt.T, ((0, 0), (0, LANES - 16))).astype(BF16)
    dx2 = _proj_b_bwd(dr3, dqz, dk, dv, dfl, w["wq"], w["wkv"], "proj_b_bwd")
    grads["wo1"] = _mm_nn(cat1t, dr3b, "dw_o_1")
    grads["wq"] = _mm_nn(x2t, dqz, "dw_q")
    grads["wkv"] = _mm_nn_cat(x2t, [dk, dv, dfl], "dw_kv")
    grads["mkv1"] = _mm_tn(memb, dmkv1.astype(BF16), "dw_mkv_1")
    dx1, dr2t, dpre0, dgb01 = staged("mlp_bwd_0", _mlp_bwd, dx2, r2, h0, w["wup0"], w["wdown0"], row(ln_g, 1))
    grads["wdown0"] = _mm_nn(dr2t, h0, "dw_down_0", out_split=nc)
    grads["wup0"] = _mm_nn(x1t, dpre0, "dw_up_0", out_split=nc)
    dx0, dz, dr1b, dgb00, dsgb, dws, dbst, dmkv0 = staged(
        "mixer_a_bwd", _mixer_a_bwd, z0b, dx1, r1, w["wa"], w["sg"], w["sb"], w["ws"], bst, mkv0, w["wo0"], row(ln_g, 0))
    grads["wo0"] = _mm_nn(cat0t, dr1b, "dw_o_0")
    grads["wa"] = _mm_nn(x0t, dz, "dw_a")
    grads["mkv0"] = _mm_tn(memb, dmkv0.astype(BF16), "dw_mkv_0")
    grads.update({
        "ws": dws, "b_s": dbst[:, :NG].T, "sg": dsgb[0:1], "sb": dsgb[1:2], "bf": dbf[:NG, 0], "loss": loss_part[0:1],
        "ln_g": jnp.concatenate([dgb00[0:1], dgb01[0:1], dgb10[0:1], dgb11[0:1]], axis=0),
        "ln_b": jnp.concatenate([dgb00[1:2], dgb01[1:2], dgb10[1:2], dgb11[1:2]], axis=0),
    })
    return loss_part, dx0, grads


def _place():
    xi, yi, ci = lax.axis_index("x"), lax.axis_index("y"), lax.axis_index("c")
    peers = [(xi, 1 - yi), (1 - xi, yi), (1 - xi, 1 - yi)]
    return xi, yi, ci, peers


def _comm_call(body, name, ins, out_shapes, n_remote, n_local):
    return pl.pallas_call(
        body, name=name, out_shape=out_shapes, in_specs=[ANY] * len(ins), out_specs=[ANY] * len(out_shapes),
        scratch_shapes=[pltpu.SemaphoreType.DMA((n_remote,)), pltpu.SemaphoreType.DMA((n_remote,)),
                        pltpu.SemaphoreType.DMA((max(n_local, 1),))],
        compiler_params=pltpu.CompilerParams(has_side_effects=True),
    )(*ins)


def _pair_swap(xs):
    n = len(xs)
    out_shapes = [jax.ShapeDtypeStruct((a.shape[0],) + a.shape[2:], a.dtype) for a in xs]

    def copies(x_refs, o_refs, sems):
        send_sems, recv_sems = sems
        xi, yi, ci, _ = _place()
        return [pltpu.make_async_remote_copy(src_ref=x_refs[i].at[:, 1 - ci], dst_ref=o_refs[i],
                                             send_sem=send_sems.at[i], recv_sem=recv_sems.at[i],
                                             device_id=(xi, yi, 1 - ci), device_id_type=MESH) for i in range(n)]

    def start(x_refs, o_refs, sems):
        for cp in copies(x_refs, o_refs, sems):
            cp.start()

    def finish(x_refs, o_refs, sems):
        cps = copies(x_refs, o_refs, sems)
        for cp in cps:
            cp.wait_recv()
        for cp in cps:
            cp.wait_send()

    return _Comm(xs, out_shapes, {}, [n, n], start, finish)


def _chip_exchange(xs):
    n = len(xs)
    out_shapes = [jax.ShapeDtypeStruct((3,) + a.shape[1:], a.dtype) for a in xs]

    def copies(x_refs, o_refs, sems):
        send_sems, recv_sems = sems
        xi, yi, ci, peers = _place()
        return [pltpu.make_async_remote_copy(src_ref=x_refs[i].at[2 * px + py], dst_ref=o_refs[i].at[d],
                                             send_sem=send_sems.at[3 * i + d], recv_sem=recv_sems.at[3 * i + d],
                                             device_id=(px, py, ci), device_id_type=MESH)
                for i in range(n) for d, (px, py) in enumerate(peers)]

    def start(x_refs, o_refs, sems):
        for cp in copies(x_refs, o_refs, sems):
            cp.start()

    def finish(x_refs, o_refs, sems):
        cps = copies(x_refs, o_refs, sems)
        for cp in cps:
            cp.wait_recv()
        for cp in cps:
            cp.wait_send()

    return _Comm(xs, out_shapes, {}, [3 * n, 3 * n], start, finish)


def _inplace_call(body, name, bufs, sem_counts):
    n = len(bufs)
    return pl.pallas_call(
        body, name=name, out_shape=[jax.ShapeDtypeStruct(a.shape, a.dtype) for a in bufs],
        in_specs=[ANY] * n, out_specs=[ANY] * n, input_output_aliases={i: i for i in range(n)},
        scratch_shapes=[pltpu.SemaphoreType.DMA((k,)) for k in sem_counts],
        compiler_params=pltpu.CompilerParams(has_side_effects=True),
    )(*bufs)


def _gather_weights(bufs, whole=(), early=None):
    n, nw = len(bufs), len(whole)

    def copy(ref, s, r, k, dev):
        return pltpu.make_async_remote_copy(src_ref=ref, dst_ref=ref, send_sem=s.at[k], recv_sem=r.at[k],
                                            device_id=dev, device_id_type=MESH)

    def plan(w, sems):
        ici_s, ici_r, d2d_s, d2d_r = sems
        xi, yi, ci, peers = _place()
        xn, yn, sib = (1 - xi, yi, ci), (xi, 1 - yi, ci), (xi, yi, 1 - ci)
        j, jx, jy, jd = 2 * xi + yi, 2 * (1 - xi) + yi, 2 * xi + 1 - yi, 2 * (1 - xi) + 1 - yi
        c = {}
        for i in range(n):
            hh = bufs[i].shape[2] // 2
            lo, hi = pl.ds(0, hh), pl.ds(hh, hh)
            at = lambda slot, half, rows=None, i=i: w[i].at[slot, half] if rows is None else w[i].at[slot, half, rows]
            c["to_x", i] = (copy(at(j, ci), ici_s, ici_r, 4 * i, xn), copy(at(jx, ci), ici_s, ici_r, 4 * i, xn))
            c["to_y", i] = (copy(at(j, ci), ici_s, ici_r, 4 * i + 1, yn), copy(at(jy, ci), ici_s, ici_r, 4 * i + 1, yn))
            c["x_on", i] = (copy(at(jx, ci, lo), ici_s, ici_r, 4 * i + 2, yn), copy(at(jd, ci, lo), ici_s, ici_r, 4 * i + 2, yn))
            c["y_on", i] = (copy(at(jy, ci, hi), ici_s, ici_r, 4 * i + 3, xn), copy(at(jd, ci, hi), ici_s, ici_r, 4 * i + 3, xn))
            for k, slot in enumerate((jx, jy, jd)):
                c["sib", i, k] = (copy(at(slot, ci), d2d_s, d2d_r, 3 * i + k, sib),
                                  copy(at(slot, 1 - ci), d2d_s, d2d_r, 3 * i + k, sib))
        for i in range(nw):
            for d, (px, py) in enumerate(peers):
                k = 4 * n + 3 * i + d
                c["whole", i, d] = (copy(w[n + i].at[j], ici_s, ici_r, k, (px, py, ci)),
                                    copy(w[n + i].at[2 * px + py], ici_s, ici_r, k, (px, py, ci)))
        return c

    def start(_, w, sems):
        c = plan(w, sems)
        for i in range(n):
            c["to_x", i][0].start()
            c["to_y", i][0].start()
        for i in range(nw):
            for d in range(3):
                c["whole", i, d][0].start()

    def pass_on(c, which):
        for i in which:
            c["to_x", i][1].wait_recv()
            c["x_on", i][0].start()
            c["sib", i, 0][0].start()
            c["to_y", i][1].wait_recv()
            c["y_on", i][0].start()
            c["sib", i, 1][0].start()

    n_early = n if early is None else min(early, n)

    def middle(_, w, sems):
        pass_on(plan(w, sems), range(n_early))

    def finish(_, w, sems):
        c = plan(w, sems)
        pass_on(c, range(n_early, n))
        for i in range(n):
            c["x_on", i][1].wait_recv()
            c["y_on", i][1].wait_recv()
            c["sib", i, 2][0].start()
        for i in range(n):
            for k in range(3):
                c["sib", i, k][1].wait_recv()
        for i in range(nw):
            for d in range(3):
                c["whole", i, d][1].wait_recv()
        for key, (sent, _) in c.items():
            sent.wait_send()

    arrays = list(bufs) + list(whole)
    k = 4 * n + 3 * nw
    return _Comm(arrays, [jax.ShapeDtypeStruct(a.shape, a.dtype) for a in arrays], {i: i for i in range(n + nw)},
                 [k, k, max(3 * n, 1), max(3 * n, 1)], start, finish, middle)


def _pair_gather(bufs, spread, name):
    n = len(bufs)
    where = [(i, l) for i, a in enumerate(bufs) for l in range(a.shape[0])]
    nw = len(where)
    arrays = list(bufs) + ([] if spread is None else [spread])
    na = len(arrays)

    def body(*refs):
        g = refs[na:na + n]
        sp = refs[na + n] if spread is not None else None
        send_sems, recv_sems = refs[2 * na:]
        xi, yi, ci, peers = _place()
        chips = [(xi, yi)] + peers
        others = [(d, f) for d in range(4) for f in (0, 1) if (d, f) != (0, 0)] if spread is not None else []
        sends = []
        for k, (i, l) in enumerate(where):
            mine = g[i].at[l, :, ci]
            cp = pltpu.make_async_remote_copy(src_ref=mine, dst_ref=mine, send_sem=send_sems.at[k],
                                              recv_sem=recv_sems.at[k], device_id=(xi, yi, 1 - ci), device_id_type=MESH)
            cp.start()
            sends.append(cp)
        for k, (d, f) in enumerate(others):
            mine = sp.at[2 * xi + yi, ci]
            cp = pltpu.make_async_remote_copy(src_ref=mine, dst_ref=mine, send_sem=send_sems.at[nw + k],
                                              recv_sem=recv_sems.at[nw + k],
                                              device_id=(chips[d][0], chips[d][1], ci if f == 0 else 1 - ci),
                                              device_id_type=MESH)
            cp.start()
            sends.append(cp)
        for k, (i, l) in enumerate(where):
            other = g[i].at[l, :, 1 - ci]
            pltpu.make_async_remote_copy(src_ref=other, dst_ref=other, send_sem=send_sems.at[k],
                                         recv_sem=recv_sems.at[k], device_id=(xi, yi, 1 - ci),
                                         device_id_type=MESH).wait_recv()
        for k, (d, f) in enumerate(others):
            px, py, pc = chips[d][0], chips[d][1], (ci if f == 0 else 1 - ci)
            theirs = sp.at[2 * px + py, pc]
            pltpu.make_async_remote_copy(src_ref=theirs, dst_ref=theirs, send_sem=send_sems.at[nw + k],
                                         recv_sem=recv_sems.at[nw + k], device_id=(px, py, pc),
                                         device_id_type=MESH).wait_recv()
        for cp in sends:
            cp.wait_send()

    n_sems = nw + (7 if spread is not None else 0)
    out = _inplace_call(body, name, arrays, [n_sems, n_sems])
    return out[:n], (out[n] if spread is not None else None)


def _row_block(rows, cols):
    want = max(8, (2 ** 18 // max(cols, 1)) // 8 * 8)
    if rows <= want:
        return rows
    best = 8
    for t in range(8, want + 1, 8):
        if rows % t == 0:
            best = t
    return best


def _pair_add(g4, recv, sel, out_dtype, name):
    A, _, H, C = g4.shape
    bh = _row_block(H, C)

    def body(sel_ref, g_ref, r_ref, o_ref):
        o_ref[...] = (g_ref[...].astype(F32) + r_ref[...].astype(F32)).astype(out_dtype)

    return pl.pallas_call(
        body, name=name, out_shape=jax.ShapeDtypeStruct((A, H, C), out_dtype),
        grid_spec=pltpu.PrefetchScalarGridSpec(
            num_scalar_prefetch=1, grid=(A, H // bh),
            in_specs=[pl.BlockSpec((None, None, bh, C), lambda a, i, s: (a, s[1], i, 0)),
                      pl.BlockSpec((None, bh, C), lambda a, i, s: (a, i, 0))],
            out_specs=pl.BlockSpec((None, bh, C), lambda a, i, s: (a, i, 0))),
        compiler_params=pltpu.CompilerParams(dimension_semantics=("arbitrary", "arbitrary"),
                                             vmem_limit_bytes=VMEM_LIMIT),
    )(sel, g4, recv)


def _chip_reduce(g4, recv1, recv2, sel, buf, layer, n_layers, name, by_chip=False):
    _, _, H, C = g4.shape
    bh = _row_block(H, C)
    if by_chip:
        out_shape = jax.ShapeDtypeStruct((N_CHIPS, 2, H, C), F32)
        out_spec = pl.BlockSpec((None, None, bh, C), lambda i, s: (s[0], s[1], i, 0))
    else:
        out_shape = jax.ShapeDtypeStruct((n_layers, 1, 2, H, C), F32)
        out_spec = pl.BlockSpec((None, None, None, bh, C), lambda i, s: (layer, 0, s[1], i, 0))

    def body(sel_ref, g_ref, r1_ref, r2_ref, *rest):
        acc = g_ref[...].astype(F32) + r1_ref[...].astype(F32)
        for d in range(3):
            acc = acc + r2_ref[d].astype(F32)
        rest[-1][...] = acc

    in_specs = [pl.BlockSpec((None, None, bh, C), lambda i, s: (s[0], s[1], i, 0)),
                pl.BlockSpec((None, bh, C), lambda i, s: (s[0], i, 0)),
                pl.BlockSpec((3, bh, C), lambda i, s: (0, i, 0))]
    args = [sel, g4, recv1, recv2]
    aliases = {}
    if buf is not None:
        in_specs.append(ANY)
        args.append(buf)
        aliases = {4: 0}
    return pl.pallas_call(
        body, name=name, out_shape=out_shape,
        grid_spec=pltpu.PrefetchScalarGridSpec(num_scalar_prefetch=1, grid=(H // bh,), in_specs=in_specs,
                                               out_specs=out_spec),
        input_output_aliases=aliases,
        compiler_params=pltpu.CompilerParams(dimension_semantics=("arbitrary",), vmem_limit_bytes=VMEM_LIMIT),
    )(*args)


def _chip_reduce_many(groups, sel, comm, name):
    flat = [t for grp in groups for t in grp]
    n, ng = len(flat), len(groups)
    _, _, H, C = flat[0][0].shape
    bh = _row_block(H, C)
    last = H // bh - 1
    nci, nco = len(comm.ins), len(comm.out_shapes)

    def body(sel_ref, *refs):
        ins, refs = refs[:3 * n], refs[3 * n:]
        cins, refs = refs[:nci], refs[nci:]
        outs, refs = refs[:ng], refs[ng:]
        couts, sems = refs[:nco], refs[nco:]

        @pl.when(pl.program_id(0) == 0)
        def _():
            comm.start(cins, couts, sems)

        k = 0
        for gi, grp in enumerate(groups):
            for layer in range(len(grp)):
                g_ref, r1_ref, r2_ref = ins[3 * k:3 * k + 3]
                acc = g_ref[...].astype(F32) + r1_ref[...].astype(F32)
                for d in range(3):
                    acc = acc + r2_ref[d].astype(F32)
                outs[gi][layer] = acc
                k += 1

        @pl.when(pl.program_id(0) == last)
        def _():
            if comm.middle is not None:
                comm.middle(cins, couts, sems)
            comm.finish(cins, couts, sems)

    in_specs = [pl.BlockSpec((None, None, bh, C), lambda i, s: (s[0], s[1], i, 0)),
                pl.BlockSpec((None, bh, C), lambda i, s: (s[0], i, 0)),
                pl.BlockSpec((3, bh, C), lambda i, s: (0, i, 0))] * n + [ANY] * nci
    out_specs = [pl.BlockSpec((len(grp), None, None, bh, C), lambda i, s: (0, 0, s[1], i, 0)) for grp in groups]
    res = pl.pallas_call(
        body, name=name,
        out_shape=[jax.ShapeDtypeStruct((len(grp), 1, 2, H, C), F32) for grp in groups] + list(comm.out_shapes),
        grid_spec=pltpu.PrefetchScalarGridSpec(
            num_scalar_prefetch=1, grid=(H // bh,), in_specs=in_specs, out_specs=out_specs + [ANY] * nco,
            scratch_shapes=[pltpu.SemaphoreType.DMA((k,)) for k in comm.sem_counts]),
        input_output_aliases={1 + 3 * n + a: ng + b for a, b in comm.aliases.items()},
        compiler_params=pltpu.CompilerParams(dimension_semantics=("arbitrary",), vmem_limit_bytes=VMEM_LIMIT,
                                             has_side_effects=True),
    )(sel, *[a for t in flat for a in t], *comm.ins)
    return res[:ng], res[ng:]


def _cast_place(shards, layer, sel, name):
    _, _, H, C = shards.shape
    bh = _row_block(H, C)

    def body(sel_ref, x_ref, o_ref):
        o_ref[...] = x_ref[...].astype(BF16)

    return pl.pallas_call(
        body, name=name, out_shape=jax.ShapeDtypeStruct((N_CHIPS, 2, H, C), BF16),
        grid_spec=pltpu.PrefetchScalarGridSpec(
            num_scalar_prefetch=1, grid=(2, H // bh),
            in_specs=[pl.BlockSpec((None, None, bh, C), lambda h, i, s: (layer, h, i, 0))],
            out_specs=pl.BlockSpec((None, None, bh, C), lambda h, i, s: (s[0], h, i, 0))),
        compiler_params=pltpu.CompilerParams(dimension_semantics=("arbitrary", "arbitrary"),
                                             vmem_limit_bytes=VMEM_LIMIT),
    )(sel, shards)


def _cast_place_many(pairs, sel, comm, name):
    n = len(pairs)
    _, _, H, C = pairs[0][0].shape
    bh = _row_block(H, C)
    last = H // bh - 1
    nci, nco = len(comm.ins), len(comm.out_shapes)

    def body(sel_ref, *refs):
        xs, refs = refs[:n], refs[n:]
        cins, refs = refs[:nci], refs[nci:]
        outs, refs = refs[:n], refs[n:]
        couts, sems = refs[:nco], refs[nco:]
        h, i = pl.program_id(0), pl.program_id(1)

        @pl.when((h == 0) & (i == 0))
        def _():
            comm.start(cins, couts, sems)

        if comm.middle is not None:
            @pl.when((h == 1) & (i == 0))
            def _():
                comm.middle(cins, couts, sems)

        for x_ref, o_ref in zip(xs, outs):
            o_ref[...] = x_ref[...].astype(BF16)

        @pl.when((h == 1) & (i == last))
        def _():
            comm.finish(cins, couts, sems)

    res = pl.pallas_call(
        body, name=name,
        out_shape=[jax.ShapeDtypeStruct((N_CHIPS, 2, H, C), BF16)] * n + list(comm.out_shapes),
        grid_spec=pltpu.PrefetchScalarGridSpec(
            num_scalar_prefetch=1, grid=(2, H // bh),
            in_specs=[pl.BlockSpec((None, None, bh, C), lambda h, i, s, layer=layer: (layer, h, i, 0))
                      for _, layer in pairs] + [ANY] * nci,
            out_specs=[pl.BlockSpec((None, None, bh, C), lambda h, i, s: (s[0], h, i, 0))] * n + [ANY] * nco,
            scratch_shapes=[pltpu.SemaphoreType.DMA((k,)) for k in comm.sem_counts]),
        input_output_aliases={1 + n + a: n + b for a, b in comm.aliases.items()},
        compiler_params=pltpu.CompilerParams(dimension_semantics=("arbitrary", "arbitrary"),
                                             vmem_limit_bytes=VMEM_LIMIT, has_side_effects=True),
    )(sel, *[s for s, _ in pairs], *comm.ins)
    return res[:n], res[n:]


def _adamw_many(quads, name, comm=None):
    R, C = quads[0][0].shape
    br = _row_block(R, C)
    n = len(quads)

    def body(*refs):
        for k in range(n):
            g_ref, w_ref, m_ref, v_ref = refs[4 * k:4 * k + 4]
            d_ref, mo_ref, vo_ref = refs[4 * n + 3 * k:4 * n + 3 * k + 3]
            gv = g_ref[...]
            mn = ADAM_B1 * m_ref[...] + (1.0 - ADAM_B1) * gv
            vn = ADAM_B2 * v_ref[...] + (1.0 - ADAM_B2) * (gv * gv)
            m_hat = mn / (1.0 - ADAM_B1 ** ADAM_STEP)
            v_hat = vn / (1.0 - ADAM_B2 ** ADAM_STEP)
            d_ref[...] = -ADAM_LR * (m_hat / (jnp.sqrt(v_hat) + ADAM_EPS) + ADAM_WD * w_ref[...])
            mo_ref[...] = mn
            vo_ref[...] = vn

    spec = pl.BlockSpec((br, C), lambda i: (i, 0))
    res, got = _pcall(body, name=name, grid=(R // br,), comm=comm, out_shape=(jax.ShapeDtypeStruct((R, C), F32),) * (3 * n),
                      in_specs=[spec] * (4 * n), out_specs=(spec,) * (3 * n), args=[a for q in quads for a in q])
    return [res[3 * k:3 * k + 3] for k in range(n)], got


def _adamw(g, w, m, v, name):
    return _adamw_many([(g, w, m, v)], name)[0][0]


def _part_rows(shape):
    return -(-math.prod(shape) // (8 * LANES)) * 8


def _pack_rows(arrs, total_rows):
    parts = []
    for a in arrs:
        flat = a.reshape(-1)
        rows = _part_rows(a.shape)
        parts.append(jnp.pad(flat, (0, rows * LANES - flat.shape[0])).reshape(rows, LANES))
    packed = jnp.concatenate(parts, axis=0)
    return jnp.pad(packed, ((0, total_rows - packed.shape[0]), (0, 0)))


def _unpack_rows(packed, shapes):
    out, r = [], 0
    for shp in shapes:
        size, rows = math.prod(shp), _part_rows(shp)
        out.append(packed[r:r + rows].reshape(-1)[:size].reshape(shp))
        r += rows
    return out


def _rows_of(shapes):
    return sum(_part_rows(s) for s in shapes)


_GATHER_STAGES = {"start": ["wa", "mkv0", "wo0"], "mixer_a_fwd": ["wup0", "wdown0"],
                  "mlp_fwd_0": ["wq", "wkv", "mkv1", "wo1"], "fox_fwd": ["wup1", "wdown1"]}
_SWAP_STAGES = {"mixer_b_bwd": ["wup1", "wdown1"]}
_REDUCE_STAGES = {"fox_bwd": ["wup1", "wdown1"], "mlp_bwd_0": ["wo1", "wq", "wkv", "mkv1"],
                  "mixer_a_bwd": ["wup0", "wdown0"], "end": ["wa", "wo0", "mkv0", "small"]}
_SMALL = ["ws", "b_s", "sg", "sb", "bf", "ln_g", "ln_b", "loss"]


class _Staged:
    def __init__(self, placed, sel, dm, ng, shard_cols, n_small):
        self.placed, self.sel, self.dm, self.ng, self.shard_cols, self.n_small = placed, sel, dm, ng, shard_cols, n_small
        self.g4, self.recv1, self.recv2 = {}, {}, {}

    def _weight(self, name, a):
        a = a.reshape(N_CHIPS, 2 * a.shape[2], a.shape[3])
        if name in ("wa", "wkv"):
            a = a.transpose(1, 0, 2).reshape(a.shape[1], -1)
            if name == "wkv":
                a = jnp.concatenate([a[:, :2 * self.dm],
                                     jnp.pad(a[:, 2 * self.dm:], ((0, 0), (0, LANES - self.ng)))], axis=1)
            return a
        if name.startswith("wup") or name.startswith("wdown"):
            return a
        return a.reshape(-1, a.shape[2])

    def _partial(self, name, grads):
        if name == "small":
            a = _pack_rows([grads[n] for n in _SMALL], self.n_small).reshape(N_CHIPS, -1, LANES)
        elif name in self.shard_cols:
            n = self.shard_cols[name]
            g = grads[name][:, :N_CHIPS * n]
            a = g.reshape(g.shape[0], N_CHIPS, n).transpose(1, 0, 2)
        elif name.startswith("wup") or name.startswith("wdown"):
            a = grads[name]
        else:
            g = grads[name]
            a = g.reshape(N_CHIPS, g.shape[0] // N_CHIPS, g.shape[1])
        return a.reshape(N_CHIPS, 2, a.shape[1] // 2, a.shape[2])

    def plan(self, stage, w, grads):
        if stage in _GATHER_STAGES:
            return _gather_weights([self.placed[n] for n in _GATHER_STAGES[stage]],
                                   early=1 if stage == "mixer_a_fwd" else None)
        if stage in _SWAP_STAGES:
            for n in _SWAP_STAGES[stage]:
                self.g4[n] = self._partial(n, grads)
            return _pair_swap([self.g4[n] for n in _SWAP_STAGES[stage]])
        if stage in _REDUCE_STAGES:
            names = _REDUCE_STAGES[stage]
            late = [n for n in names if n not in self.recv1]
            for n in late:
                self.g4[n] = self._partial(n, grads)
            if late:
                got = _run_comm(_pair_swap([self.g4[n] for n in late]), "reduce_pair_swap_" + stage)
                self.recv1.update(zip(late, got))
            return _chip_exchange([_pair_add(self.g4[n], self.recv1[n], self.sel, F32 if n == "small" else BF16,
                                             "reduce_pair_add_" + n) for n in names])
        return None

    def done(self, stage, results, w):
        if stage in _GATHER_STAGES:
            for n, a in zip(_GATHER_STAGES[stage], results):
                w[n] = self._weight(n, a)
        elif stage in _SWAP_STAGES:
            self.recv1.update(zip(_SWAP_STAGES[stage], results))
        elif stage in _REDUCE_STAGES:
            self.recv2.update(zip(_REDUCE_STAGES[stage], results))


def kernel(x, mem, a_w_in, a_sgu_ln_g, a_sgu_ln_b, a_w_s, a_b_s, kv_w, kv_b_f, b_w_q, mem_w_kv, w_o, ln_g, ln_b, w_up, w_down, loss_target, m_a_w_in, m_a_sgu_ln_g, m_a_sgu_ln_b, m_a_w_s, m_a_b_s, m_kv_w, m_kv_b_f, m_b_w_q, m_mem_w_kv, m_w_o, m_ln_g, m_ln_b, m_w_up, m_w_down, v_a_w_in, v_a_sgu_ln_g, v_a_sgu_ln_b, v_a_w_s, v_a_b_s, v_kv_w, v_kv_b_f, v_b_w_q, v_mem_w_kv, v_w_o, v_ln_g, v_ln_b, v_w_up, v_w_down):
    xi, yi, ci = lax.axis_index("x"), lax.axis_index("y"), lax.axis_index("c")
    chip = 2 * xi + yi
    sel = jnp.stack([chip, ci]).astype(jnp.int32)
    S, D = x.shape[1], x.shape[2]
    DQ = mem_w_kv.shape[2] // 2
    DM = D - DQ
    NG = DM // HEAD_DIM

    shards = {"wa": (a_w_in, 0), "wkv": (kv_w[None], 0), "wq": (b_w_q, 0), "mkv0": (mem_w_kv, 0), "mkv1": (mem_w_kv, 1),
              "wo0": (w_o, 0), "wo1": (w_o, 1), "wup0": (w_up, 0), "wup1": (w_up, 1), "wdown0": (w_down, 0),
              "wdown1": (w_down, 1)}
    halves = lambda s: s.reshape(s.shape[0], 2, s.shape[1] // 2, s.shape[2])
    mlp = ["wup0", "wup1", "wdown0", "wdown1"]
    placed = {n: _cast_place(halves(s), layer, sel, "cast_place_" + n)
              for n, (s, layer) in shards.items() if n not in mlp}
    ln_pack = jnp.concatenate([ln_g.reshape(4, -1), ln_b.reshape(4, -1)], axis=0)
    ln_buf = lax.dynamic_update_slice(jnp.zeros((N_CHIPS,) + ln_pack.shape, F32), ln_pack[None], (chip, 0, 0))
    mlp_placed, got = _cast_place_many(
        [(halves(shards[n][0]), shards[n][1]) for n in mlp], sel,
        _gather_weights([placed[n] for n in _GATHER_STAGES["start"]], whole=[ln_buf]), "cast_place_mlp")
    placed.update(zip(mlp, mlp_placed))
    ln_shape = (4, D)
    small_shapes = [a_w_s.shape, a_b_s.shape, a_sgu_ln_g.shape, a_sgu_ln_b.shape, kv_b_f.shape, ln_shape, ln_shape,
                    (1, LANES)]
    n_small = -(-_rows_of(small_shapes) // 64) * 64
    hooks = _Staged(placed, sel, DM, NG, {"wa": a_w_in.shape[2], "wkv": kv_w.shape[1]}, n_small)
    weights = {"sg": a_sgu_ln_g, "sb": a_sgu_ln_b, "ws": a_w_s[0], "b_s": a_b_s[0], "bf": kv_b_f}
    hooks.done("start", got[:-1], weights)
    ln_full = got[-1].transpose(1, 0, 2).reshape(8, D)
    weights["ln_g"], weights["ln_b"] = ln_full[:4], ln_full[4:]

    _, grad_x, gr = _local_step(x[0], mem[0], loss_target[0], weights, hooks)

    def chip_sums(groups):
        bufs = []
        for grp in groups:
            buf = None
            for layer, n in enumerate(grp):
                buf = _chip_reduce(hooks.g4[n], hooks.recv1[n], hooks.recv2[n], sel, buf, layer, len(grp),
                                   "reduce_chip_sum_" + n)
            bufs.append(buf)
        return bufs

    whole = lambda a: a.reshape(a.shape[0], 2 * a.shape[3], a.shape[4])
    flat = lambda a: a.reshape(-1, a.shape[-1])

    def update(name, g, w, m, v):
        d, mn, vn = _adamw(flat(g), flat(w), flat(m), flat(v), "adamw_" + name)
        return d.reshape(w.shape), mn.reshape(w.shape), vn.reshape(w.shape)

    parts = lambda n: (hooks.g4[n], hooks.recv1[n], hooks.recv2[n])
    mlp_sums, got = _chip_reduce_many([[parts("wup0"), parts("wup1")], [parts("wdown0"), parts("wdown1")]], sel,
                                      hooks.plan("end", weights, gr), "reduce_chip_sum_mlp")
    hooks.done("end", got, weights)
    small_mine = _chip_reduce(hooks.g4["small"], hooks.recv1["small"], hooks.recv2["small"], sel, None, 0, 1,
                              "reduce_chip_sum_small", by_chip=True)
    red, small_all = _pair_gather(chip_sums([["wa"], ["wkv"], ["wq"], ["mkv0", "mkv1"], ["wo0", "wo1"]])
                                  + list(mlp_sums), small_mine, "reduce_pair_gather")
    g_a_w_in, g_kv_w, g_b_w_q, g_mem_w_kv, g_w_o, g_w_up, g_w_down = [whole(a) for a in red]
    g_kv_w = g_kv_w[0]
    g_w_down = g_w_down.transpose(0, 2, 1)
    small_all = small_all.reshape(n_small, LANES)
    g_ws, g_bs, g_sg, g_sb, g_bf, g_lng, g_lnb, loss_row = _unpack_rows(small_all, small_shapes)
    loss = loss_row[0, 0]
    dsh = D // N_CHIPS
    g_ln_g = lax.dynamic_slice_in_dim(g_lng, chip * dsh, dsh, axis=1).reshape(ln_g.shape)
    g_ln_b = lax.dynamic_slice_in_dim(g_lnb, chip * dsh, dsh, axis=1).reshape(ln_b.shape)

    upd = {
        "a_w_in": update("a_w_in", g_a_w_in, a_w_in, m_a_w_in, v_a_w_in),
        "kv_w": update("kv_w", g_kv_w, kv_w, m_kv_w, v_kv_w),
        "b_w_q": update("b_w_q", g_b_w_q, b_w_q, m_b_w_q, v_b_w_q),
        "mem_w_kv": update("mem_w_kv", g_mem_w_kv, mem_w_kv, m_mem_w_kv, v_mem_w_kv),
        "w_o": update("w_o", g_w_o, w_o, m_w_o, v_w_o),
        "w_up": update("w_up", g_w_up, w_up, m_w_up, v_w_up),
        "w_down": update("w_down", g_w_down, w_down, m_w_down, v_w_down),
    }
    tiny_g = [g_sg, g_sb, g_ws, g_bs, g_bf, g_ln_g, g_ln_b]
    tiny_w = [a_sgu_ln_g, a_sgu_ln_b, a_w_s, a_b_s, kv_b_f, ln_g, ln_b]
    tiny_m = [m_a_sgu_ln_g, m_a_sgu_ln_b, m_a_w_s, m_a_b_s, m_kv_b_f, m_ln_g, m_ln_b]
    tiny_v = [v_a_sgu_ln_g, v_a_sgu_ln_b, v_a_w_s, v_a_b_s, v_kv_b_f, v_ln_g, v_ln_b]
    tiny_shapes = [a.shape for a in tiny_w]
    n_tiny = -(-_rows_of(tiny_shapes) // 8) * 8
    td, tm_, tv = _adamw(_pack_rows(tiny_g, n_tiny), _pack_rows(tiny_w, n_tiny), _pack_rows(tiny_m, n_tiny),
                         _pack_rows(tiny_v, n_tiny), "adamw_small")
    for name, d, mn, vn in zip(["a_sgu_ln_g", "a_sgu_ln_b", "a_w_s", "a_b_s", "kv_b_f", "ln_g", "ln_b"],
                               _unpack_rows(td, tiny_shapes), _unpack_rows(tm_, tiny_shapes),
                               _unpack_rows(tv, tiny_shapes)):
        upd[name] = (d, mn, vn)

    order = ["a_w_in", "a_sgu_ln_g", "a_sgu_ln_b", "a_w_s", "a_b_s", "kv_w", "kv_b_f", "b_w_q", "mem_w_kv", "w_o",
             "ln_g", "ln_b", "w_up", "w_down"]
    grads = {"a_w_in": g_a_w_in.reshape(a_w_in.shape), "a_sgu_ln_g": g_sg.reshape(a_sgu_ln_g.shape),
             "a_sgu_ln_b": g_sb.reshape(a_sgu_ln_b.shape), "a_w_s": g_ws.reshape(a_w_s.shape),
             "a_b_s": g_bs.reshape(a_b_s.shape), "kv_w": g_kv_w, "kv_b_f": g_bf, "b_w_q": g_b_w_q.reshape(b_w_q.shape),
             "mem_w_kv": g_mem_w_kv, "w_o": g_w_o, "ln_g": g_ln_g, "ln_b": g_ln_b, "w_up": g_w_up, "w_down": g_w_down}
    return (loss, grad_x[None], *[grads[n] for n in order], *[upd[n][0] for n in order],
            *[upd[n][1] for n in order], *[upd[n][2] for n in order])
```

```python
import functools
import math

import jax
import jax.numpy as jnp
from jax import lax
from jax.experimental import pallas as pl
from jax.experimental.pallas import tpu as pltpu

F32 = jnp.float32
BF16 = jnp.bfloat16

HEAD_DIM = 64
PAIR = 2 * HEAD_DIM
CHUNK = 128
LN_EPS = 1e-5
ALPHA = 4 ** 0.25
QK_SCALE = 1.0 / math.sqrt(HEAD_DIM)
NEG = -1e30
N_CHIPS = 4
FOX_BLOCK = 256
LANES = 128
VMEM_LIMIT = 48 * 2 ** 20
VMEM_LIMIT_MLP_BWD = 56 * 2 ** 20

ADAM_LR, ADAM_B1, ADAM_B2, ADAM_EPS, ADAM_WD, ADAM_STEP = 0.001, 0.9, 0.999, 1e-08, 0.01, 10

MESH = pl.DeviceIdType.MESH
ANY = pl.BlockSpec(memory_space=pl.ANY)


def _dot(a, b):
    return jnp.dot(a, b, preferred_element_type=F32)


def _dot_nt(a, b):
    return lax.dot_general(a, b, (((1,), (1,)), ((), ())), preferred_element_type=F32)


def _dot_tn(a, b):
    return lax.dot_general(a, b, (((0,), (0,)), ((), ())), preferred_element_type=F32)


def _ln_stats(r):
    mu = jnp.mean(r, axis=-1, keepdims=True)
    d = r - mu
    var = jnp.mean(d * d, axis=-1, keepdims=True)
    rstd = lax.rsqrt(var + LN_EPS)
    return d * rstd, rstd


def _ln_bwd(dy, xhat, rstd, g):
    dxh = dy * g
    m1 = jnp.mean(dxh, axis=-1, keepdims=True)
    m2 = jnp.mean(dxh * xhat, axis=-1, keepdims=True)
    return rstd * (dxh - m1 - xhat * m2)


_GELU_K = math.sqrt(2.0 / math.pi)


def _gelu(x):
    return 0.5 * x * (1.0 + jnp.tanh(_GELU_K * (x + 0.044715 * x * x * x)))


def _gelu_grad(x):
    t = jnp.tanh(_GELU_K * (x + 0.044715 * x * x * x))
    return 0.5 * (1.0 + t) + 0.5 * x * (1.0 - t * t) * _GELU_K * (1.0 + 3 * 0.044715 * x * x)


def _lane_mask(e):
    lane = lax.broadcasted_iota(jnp.int32, (1, PAIR), 1)
    return (lane >= HEAD_DIM * e) & (lane < HEAD_DIM * (e + 1))


def _first_head():
    return lax.broadcasted_iota(jnp.int32, (1, PAIR), 1) < HEAD_DIM


def _mem_probs(qh, mkp):
    sc = _dot_nt(qh, mkp) * QK_SCALE
    ex = jnp.exp(sc - jnp.max(sc, axis=-1, keepdims=True))
    return ex / jnp.sum(ex, axis=-1, keepdims=True)


def _mem_attn_fwd(qm_b, mkv_ref, dq_dim):
    outs = []
    for mp in range(dq_dim // PAIR):
        qp = qm_b[:, mp * PAIR:(mp + 1) * PAIR]
        mkp = mkv_ref[:, mp * PAIR:(mp + 1) * PAIR]
        mvp = mkv_ref[:, dq_dim + mp * PAIR:dq_dim + (mp + 1) * PAIR]
        heads = []
        for e in (0, 1):
            qh = jnp.where(_lane_mask(e), qp, jnp.zeros_like(qp))
            p = _mem_probs(qh, mkp)
            heads.append(_dot(p.astype(BF16), mvp))
        outs.append(jnp.where(_first_head(), heads[0], heads[1]))
    return outs


def _mem_attn_bwd(qm_b, mkv_ref, dmo_pairs, dmkv_ref, dq_dim):
    dqs = []
    for mp in range(dq_dim // PAIR):
        ks = slice(mp * PAIR, (mp + 1) * PAIR)
        vs = slice(dq_dim + mp * PAIR, dq_dim + (mp + 1) * PAIR)
        qp = qm_b[:, ks]
        mkp = mkv_ref[:, ks]
        mvp = mkv_ref[:, vs]
        dmo_b = dmo_pairs[mp].astype(BF16)
        dq = None
        dmk = None
        dmv = None
        for e in (0, 1):
            hm = _lane_mask(e)
            qh = jnp.where(hm, qp, jnp.zeros_like(qp))
            p = _mem_probs(qh, mkp)
            doh = jnp.where(hm, dmo_b, jnp.zeros_like(dmo_b))
            dp = _dot_nt(doh, mvp)
            ds = p * (dp - jnp.sum(dp * p, axis=-1, keepdims=True))
            dsb = (ds * QK_SCALE).astype(BF16)
            kh = jnp.where(hm, mkp, jnp.zeros_like(mkp))
            dq_e = _dot(dsb, kh)
            dmk_e = _dot_tn(dsb, qh)
            dmv_e = _dot_tn(p.astype(BF16), doh)
            dq = dq_e if dq is None else dq + dq_e
            dmk = dmk_e if dmk is None else dmk + dmk_e
            dmv = dmv_e if dmv is None else dmv + dmv_e
        dmkv_ref[:, ks] += dmk
        dmkv_ref[:, vs] += dmv
        dqs.append(dq)
    return dqs


def _params(n_axes=1):
    return pltpu.CompilerParams(dimension_semantics=("arbitrary",) * n_axes, vmem_limit_bytes=VMEM_LIMIT)


def _full(shape):
    return pl.BlockSpec(shape, lambda *_: (0,) * len(shape))


def _rows(tm, cols):
    return pl.BlockSpec((tm, cols), lambda i: (i, 0))


def _cols(rows, tm):
    return pl.BlockSpec((rows, tm), lambda i: (0, i))


def _pick(n, pref):
    if n <= pref:
        return n
    best = LANES
    for t in range(LANES, pref + 1, LANES):
        if n % t == 0:
            best = t
    return best if 2 * best >= pref or n > 2 * pref else n


class _Comm:
    def __init__(self, ins, out_shapes, aliases, sem_counts, start, finish, middle=None):
        self.ins, self.out_shapes, self.aliases, self.sem_counts = list(ins), list(out_shapes), dict(aliases), sem_counts
        self.start, self.finish = start, finish
        self.middle = middle


def _pcall(body, *, name, grid, in_specs, out_specs, out_shape, args, scratch_shapes=(), comm=None, aliases=None,
           vmem_limit=VMEM_LIMIT):
    n_in, n_out, n_scr = len(in_specs), len(out_shape), len(scratch_shapes)
    aliases = dict(aliases or {})
    if comm is None:
        res = pl.pallas_call(body, name=name, grid=grid, out_shape=tuple(out_shape), in_specs=list(in_specs),
                             out_specs=tuple(out_specs), scratch_shapes=list(scratch_shapes),
                             input_output_aliases=aliases,
                             compiler_params=pltpu.CompilerParams(dimension_semantics=("arbitrary",),
                                                                  vmem_limit_bytes=vmem_limit))(*args)
        return tuple(res), ()
    nci, nco = len(comm.ins), len(comm.out_shapes)
    last = grid[0] - 1

    def wrapped(*refs):
        ins, refs = refs[:n_in], refs[n_in:]
        cins, refs = refs[:nci], refs[nci:]
        outs, refs = refs[:n_out], refs[n_out:]
        couts, refs = refs[:nco], refs[nco:]
        scr, sems = refs[:n_scr], refs[n_scr:]

        @pl.when(pl.program_id(0) == 0)
        def _():
            comm.start(cins, couts, sems)

        if comm.middle is not None:
            @pl.when(pl.program_id(0) == (last + 1) // 2)
            def _():
                comm.middle(cins, couts, sems)

        body(*ins, *outs, *scr)

        @pl.when(pl.program_id(0) == last)
        def _():
            comm.finish(cins, couts, sems)

    res = pl.pallas_call(
        wrapped, name=name, grid=grid, out_shape=tuple(out_shape) + tuple(comm.out_shapes),
        in_specs=list(in_specs) + [ANY] * nci, out_specs=tuple(out_specs) + (ANY,) * nco,
        input_output_aliases={**aliases, **{n_in + a: n_out + b for a, b in comm.aliases.items()}},
        scratch_shapes=list(scratch_shapes) + [pltpu.SemaphoreType.DMA((k,)) for k in comm.sem_counts],
        compiler_params=pltpu.CompilerParams(dimension_semantics=("arbitrary",), vmem_limit_bytes=vmem_limit,
                                             has_side_effects=True),
    )(*args, *comm.ins)
    return tuple(res[:n_out]), tuple(res[n_out:])


def _run_comm(comm, name):
    nci, nco = len(comm.ins), len(comm.out_shapes)

    def body(*refs):
        cins, couts, sems = refs[:nci], refs[nci:nci + nco], refs[nci + nco:]
        comm.start(cins, couts, sems)
        if comm.middle is not None:
            comm.middle(cins, couts, sems)
        comm.finish(cins, couts, sems)

    return pl.pallas_call(
        body, name=name, out_shape=tuple(comm.out_shapes), in_specs=[ANY] * nci, out_specs=(ANY,) * nco,
        input_output_aliases=comm.aliases, scratch_shapes=[pltpu.SemaphoreType.DMA((k,)) for k in comm.sem_counts],
        compiler_params=pltpu.CompilerParams(has_side_effects=True),
    )(*comm.ins)


def _mm(a, b, name, a_is_transposed, out_split=1):
    (M, R) = a.shape if a_is_transposed else a.shape[::-1]
    _, N = b.shape
    bm, br = _pick(M, 1024 if a_is_transposed else 512), _pick(R, 1024 if a_is_transposed else 512)
    ncol = N // out_split
    bn = _pick(ncol, 1024)
    per = ncol // bn
    last = R // br - 1

    def body(a_ref, b_ref, o_ref, acc_ref):
        @pl.when(pl.program_id(2) == 0)
        def _():
            acc_ref[...] = jnp.zeros_like(acc_ref)

        acc_ref[...] += (_dot if a_is_transposed else _dot_tn)(a_ref[...], b_ref[...])

        @pl.when(pl.program_id(2) == last)
        def _():
            o_ref[...] = acc_ref[...].astype(BF16)

    if out_split == 1:
        out_shape = jax.ShapeDtypeStruct((M, N), BF16)
        out_spec = pl.BlockSpec((bm, bn), lambda i, j, r: (i, j))
    else:
        out_shape = jax.ShapeDtypeStruct((out_split, M, ncol), BF16)
        out_spec = pl.BlockSpec((None, bm, bn), lambda i, j, r: (j // per, i, j % per))
    a_spec = (pl.BlockSpec((bm, br), lambda i, j, r: (i, r)) if a_is_transposed
              else pl.BlockSpec((br, bm), lambda i, j, r: (r, i)))
    return pl.pallas_call(
        body, name=name, out_shape=out_shape, grid=(M // bm, N // bn, R // br),
        in_specs=[a_spec, pl.BlockSpec((br, bn), lambda i, j, r: (r, j))], out_specs=out_spec,
        scratch_shapes=[pltpu.VMEM((bm, bn), F32)],
        compiler_params=pltpu.CompilerParams(dimension_semantics=("parallel", "parallel", "arbitrary"),
                                             vmem_limit_bytes=VMEM_LIMIT),
    )(a, b)


def _mm_tn(a, b, name, out_split=1):
    return _mm(a, b, name, False, out_split)


def _mm_nn(at, b, name, out_split=1):
    return _mm(at, b, name, True, out_split)


def _mm_small(a, b, name):
    def body(a_ref, b_ref, o_ref):
        o_ref[...] = _dot(a_ref[...], b_ref[...]).astype(BF16)

    return pl.pallas_call(body, name=name, out_shape=jax.ShapeDtypeStruct((a.shape[0], b.shape[1]), BF16),
                          compiler_params=pltpu.CompilerParams(vmem_limit_bytes=VMEM_LIMIT))(a, b)


def _mlp_out(x_ref, wup_ref, wdown_ref, g_ref, b_ref, r_ref, h_ref):
    NC, _, FC = wup_ref.shape
    xv = x_ref[...]
    xb = xv.astype(BF16)
    y = jnp.zeros(xv.shape, F32)
    for c in range(NC):
        a = jnp.maximum(_dot(xb, wup_ref[c]), 0.0)
        hb = (a * a).astype(BF16)
        h_ref[:, c * FC:(c + 1) * FC] = hb
        y = y + _dot(hb, wdown_ref[c])
    r = ALPHA * xv + y
    r_ref[...] = r
    xhat, _ = _ln_stats(r)
    return xhat * g_ref[...] + b_ref[...]


def _mlp_fwd_loss(x, wup4, wdown4, g, b, target, name):
    S, D = x.shape
    NC, _, FC = wup4.shape
    tm = min(512, S)

    def body(x_ref, wup_ref, wdown_ref, g_ref, b_ref, t_ref, d_ref, r_ref, h_ref, l_ref):
        @pl.when(pl.program_id(0) == 0)
        def _():
            l_ref[...] = jnp.zeros_like(l_ref)

        err = _mlp_out(x_ref, wup_ref, wdown_ref, g_ref, b_ref, r_ref, h_ref) - t_ref[...]
        d_ref[...] = err * (1.0 / D)
        l_ref[...] += (0.5 / D) * jnp.sum(err * err)

    return pl.pallas_call(
        body, name=name, grid=(S // tm,),
        out_shape=(jax.ShapeDtypeStruct((S, D), F32), jax.ShapeDtypeStruct((S, D), F32),
                   jax.ShapeDtypeStruct((S, NC * FC), BF16), jax.ShapeDtypeStruct((8, LANES), F32)),
        in_specs=[_rows(tm, D), _full(wup4.shape), _full(wdown4.shape), _full((1, D)), _full((1, D)), _rows(tm, D)],
        out_specs=(_rows(tm, D), _rows(tm, D), _rows(tm, NC * FC), _full((8, LANES))),
        compiler_params=_params(),
    )(x, wup4, wdown4, g, b, target)


def _mlp_fwd(x, wup4, wdown4, g, b, name, comm=None):
    S, D = x.shape
    NC, _, FC = wup4.shape
    tm = min(512, S)

    def body(x_ref, wup_ref, wdown_ref, g_ref, b_ref, xo_ref, xb_ref, xbt_ref, r_ref, h_ref):
        xo = _mlp_out(x_ref, wup_ref, wdown_ref, g_ref, b_ref, r_ref, h_ref)
        xo_ref[...] = xo
        xb_ref[...] = xo.astype(BF16)
        xbt_ref[...] = xo.T.astype(BF16)

    return _pcall(
        body, name=name, grid=(S // tm,), comm=comm,
        out_shape=(jax.ShapeDtypeStruct((S, D), F32), jax.ShapeDtypeStruct((S, D), BF16),
                   jax.ShapeDtypeStruct((D, S), BF16), jax.ShapeDtypeStruct((S, D), F32),
                   jax.ShapeDtypeStruct((S, NC * FC), BF16)),
        in_specs=[_rows(tm, D), _full(wup4.shape), _full(wdown4.shape), _full((1, D)), _full((1, D))],
        out_specs=(_rows(tm, D), _rows(tm, D), _cols(D, tm), _rows(tm, D), _rows(tm, NC * FC)),
        args=(x, wup4, wdown4, g, b))


def _mlp_bwd(dxo, r, h, wup4, wdown4, g, name, comm=None):
    S, D = r.shape
    NC, _, FC = wup4.shape
    tm = min(512, S)

    def body(dxo_ref, r_ref, h_ref, wup_ref, wdown_ref, g_ref, dx_ref, drbt_ref, dpre_ref, dgb_ref):
        @pl.when(pl.program_id(0) == 0)
        def _():
            dgb_ref[...] = jnp.zeros_like(dgb_ref)

        xhat, rstd = _ln_stats(r_ref[...])
        dy = dxo_ref[...]
        dgb_ref[0:1, :] += jnp.sum(dy * xhat, axis=0, keepdims=True)
        dgb_ref[1:2, :] += jnp.sum(dy, axis=0, keepdims=True)
        dr = _ln_bwd(dy, xhat, rstd, g_ref[...])
        drb = dr.astype(BF16)
        drbt_ref[...] = dr.T.astype(BF16)
        dx = ALPHA * dr
        for c in range(NC):
            dh = _dot_nt(drb, wdown_ref[c])
            a = jnp.sqrt(h_ref[:, c * FC:(c + 1) * FC].astype(F32))
            dpre = (2.0 * a * dh).astype(BF16)
            dpre_ref[:, c * FC:(c + 1) * FC] = dpre
            dx = dx + _dot_nt(dpre, wup_ref[c])
        dx_ref[...] = dx

    return _pcall(
        body, name=name, grid=(S // tm,), comm=comm,
        out_shape=(jax.ShapeDtypeStruct((S, D), F32), jax.ShapeDtypeStruct((D, S), BF16),
                   jax.ShapeDtypeStruct((S, NC * FC), BF16), jax.ShapeDtypeStruct((2, D), F32)),
        in_specs=[_rows(tm, D), _rows(tm, D), _rows(tm, NC * FC), _full(wup4.shape), _full(wdown4.shape),
                  _full((1, D))],
        out_specs=(_rows(tm, D), _cols(D, tm), _rows(tm, NC * FC), _full((2, D))),
        args=(dxo, r, h, wup4, wdown4, g), vmem_limit=VMEM_LIMIT_MLP_BWD)


def _tail_fwd_store(xv, cat_ref, wo_ref, g_ref, b_ref, xo_ref, xbt_ref, r_ref, catt_ref):
    cat = cat_ref[...]
    y = _dot(cat, wo_ref[...])
    r = ALPHA * xv + y
    xhat, _ = _ln_stats(r)
    xo = xhat * g_ref[...] + b_ref[...]
    xo_ref[...] = xo
    xbt_ref[...] = xo.T.astype(BF16)
    catt_ref[...] = cat.astype(F32).T.astype(BF16)
    r_ref[...] = r


def _tail_bwd_head(dxo_ref, r_ref, g_ref, wo_ref, dgb_ref, drb_ref):
    xhat, rstd = _ln_stats(r_ref[...])
    dy = dxo_ref[...]
    dgb_ref[0:1, :] += jnp.sum(dy * xhat, axis=0, keepdims=True)
    dgb_ref[1:2, :] += jnp.sum(dy, axis=0, keepdims=True)
    dr = _ln_bwd(dy, xhat, rstd, g_ref[...])
    drb = dr.astype(BF16)
    drb_ref[...] = drb
    return dr, _dot_nt(drb, wo_ref[...])


def _causal(shape):
    return lax.broadcasted_iota(jnp.int32, shape, 1) <= lax.broadcasted_iota(jnp.int32, shape, 0)


def _sgu_mixed(ws_ref, bst_ref, gp, vpair):
    tril = _causal((CHUNK, CHUNK))
    w0 = jnp.where(tril, ws_ref[2 * gp], 0.0).astype(BF16)
    w1 = jnp.where(tril, ws_ref[2 * gp + 1], 0.0).astype(BF16)
    m0 = _dot(w0, vpair) + bst_ref[:, 2 * gp:2 * gp + 1]
    m1 = _dot(w1, vpair) + bst_ref[:, 2 * gp + 1:2 * gp + 2]
    return jnp.where(_first_head(), m0, m1), w0, w1


def _mixer_a_fwd(x, wa, sg, sb, ws, bst, mkv, wo, g, b, name, comm=None):
    S, D = x.shape
    DQ = mkv.shape[1] // 2
    DM = D - DQ
    NP = DM // PAIR
    tm = min(512, S)

    def body(x_ref, wa_ref, sg_ref, sb_ref, ws_ref, bst_ref, mkv_ref, wo_ref, g_ref, b_ref,
             xo_ref, xbt_ref, r_ref, catt_ref, xint_ref, zb_ref, z_s, zv_s, cat_ref):
        xv = x_ref[...]
        xint_ref[...] = xv.T.astype(BF16)
        z_s[...] = _dot(xv.astype(BF16), wa_ref[...])
        zb_ref[...] = z_s[...].astype(BF16)
        vhat, _ = _ln_stats(_gelu(z_s[:, DM:2 * DM]))
        zv_s[...] = (vhat * sg_ref[...] + sb_ref[...]).astype(BF16)
        for c in range(tm // CHUNK):
            rs = slice(c * CHUNK, (c + 1) * CHUNK)
            for gp in range(NP):
                ls = slice(gp * PAIR, (gp + 1) * PAIR)
                mixed, _, _ = _sgu_mixed(ws_ref, bst_ref, gp, zv_s[rs, ls])
                cat_ref[rs, ls] = (_gelu(z_s[rs, ls]) * mixed).astype(BF16)
        mo = _mem_attn_fwd(z_s[:, 2 * DM:].astype(BF16), mkv_ref, DQ)
        for mp in range(DQ // PAIR):
            cat_ref[:, DM + mp * PAIR:DM + (mp + 1) * PAIR] = mo[mp].astype(BF16)
        _tail_fwd_store(xv, cat_ref, wo_ref, g_ref, b_ref, xo_ref, xbt_ref, r_ref, catt_ref)

    return _pcall(
        body, name=name, grid=(S // tm,), comm=comm,
        out_shape=(jax.ShapeDtypeStruct((S, D), F32), jax.ShapeDtypeStruct((D, S), BF16),
                   jax.ShapeDtypeStruct((S, D), F32), jax.ShapeDtypeStruct((D, S), BF16),
                   jax.ShapeDtypeStruct((D, S), BF16), jax.ShapeDtypeStruct((S, 2 * DM + DQ), BF16)),
        in_specs=[_rows(tm, D), _full(wa.shape), _full(sg.shape), _full(sb.shape), _full(ws.shape), _full(bst.shape),
                  _full(mkv.shape), _full(wo.shape), _full((1, D)), _full((1, D))],
        out_specs=(_rows(tm, D), _cols(D, tm), _rows(tm, D), _cols(D, tm), _cols(D, tm), _rows(tm, 2 * DM + DQ)),
        scratch_shapes=[pltpu.VMEM((tm, 2 * DM + DQ), F32), pltpu.VMEM((tm, DM), BF16), pltpu.VMEM((tm, D), BF16)],
        args=(x, wa, sg, sb, ws, bst, mkv, wo, g, b))


def _mixer_a_bwd(zb, dxo, r, wa, sg, sb, ws, bst, mkv, wo, g, name, comm=None):
    S, D = r.shape
    DQ = mkv.shape[1] // 2
    DM = D - DQ
    NP = DM // PAIR
    NG = DM // HEAD_DIM
    tm = min(512, S)

    def body(zb_ref, dxo_ref, r_ref, wa_ref, sg_ref, sb_ref, ws_ref, bst_ref, mkv_ref, wo_ref, g_ref,
             dx_ref, dz_ref, drb_ref, dgb_ref, dsgb_ref, dws_ref, dbst_ref, dmkv_ref, z_s, zv_s, dzv_s):
        @pl.when(pl.program_id(0) == 0)
        def _():
            dgb_ref[...] = jnp.zeros_like(dgb_ref)
            dsgb_ref[...] = jnp.zeros_like(dsgb_ref)
            dws_ref[...] = jnp.zeros_like(dws_ref)
            dbst_ref[...] = jnp.zeros_like(dbst_ref)
            dmkv_ref[...] = jnp.zeros_like(dmkv_ref)

        dr, dcat = _tail_bwd_head(dxo_ref, r_ref, g_ref, wo_ref, dgb_ref, drb_ref)
        z_s[...] = zb_ref[...].astype(F32)
        vhat, vrstd = _ln_stats(_gelu(z_s[:, DM:2 * DM]))
        zv_s[...] = (vhat * sg_ref[...] + sb_ref[...]).astype(BF16)
        tril = _causal((CHUNK, CHUNK))
        lane = lax.broadcasted_iota(jnp.int32, (1, LANES), 1)
        for c in range(tm // CHUNK):
            rs = slice(c * CHUNK, (c + 1) * CHUNK)
            for gp in range(NP):
                ls = slice(gp * PAIR, (gp + 1) * PAIR)
                vpair = zv_s[rs, ls]
                mixed, w0, w1 = _sgu_mixed(ws_ref, bst_ref, gp, vpair)
                u_pre = z_s[rs, ls]
                dmix = dcat[rs, ls]
                dz_ref[rs, ls] = (dmix * mixed * _gelu_grad(u_pre)).astype(BF16)
                dmixed = dmix * _gelu(u_pre)
                first = _first_head()
                d0 = jnp.where(first, dmixed, 0.0)
                d1 = jnp.where(first, 0.0, dmixed)
                d0b = d0.astype(BF16)
                d1b = d1.astype(BF16)
                dzv_s[rs, ls] = _dot_tn(w0, d0b) + _dot_tn(w1, d1b)
                dws_ref[2 * gp] += jnp.where(tril, _dot_nt(d0b, vpair), 0.0)
                dws_ref[2 * gp + 1] += jnp.where(tril, _dot_nt(d1b, vpair), 0.0)
                dbst_ref[...] += (jnp.where(lane == 2 * gp, jnp.sum(d0, axis=-1, keepdims=True), 0.0)
                                  + jnp.where(lane == 2 * gp + 1, jnp.sum(d1, axis=-1, keepdims=True), 0.0))
        dzv = dzv_s[...]
        dsgb_ref[0:1, :] += jnp.sum(dzv * vhat, axis=0, keepdims=True)
        dsgb_ref[1:2, :] += jnp.sum(dzv, axis=0, keepdims=True)
        dgv = _ln_bwd(dzv, vhat, vrstd, sg_ref[...])
        dz_ref[:, DM:2 * DM] = (dgv * _gelu_grad(z_s[:, DM:2 * DM])).astype(BF16)
        dmo = [dcat[:, DM + mp * PAIR:DM + (mp + 1) * PAIR] for mp in range(DQ // PAIR)]
        dqm = _mem_attn_bwd(z_s[:, 2 * DM:].astype(BF16), mkv_ref, dmo, dmkv_ref, DQ)
        for mp in range(DQ // PAIR):
            dz_ref[:, 2 * DM + mp * PAIR:2 * DM + (mp + 1) * PAIR] = dqm[mp].astype(BF16)
        dx_ref[...] = ALPHA * dr + _dot_nt(dz_ref[...], wa_ref[...])

    ZW = 2 * DM + DQ
    M = mkv.shape[0]
    return _pcall(
        body, name=name, grid=(S // tm,), comm=comm,
        out_shape=(jax.ShapeDtypeStruct((S, D), F32), jax.ShapeDtypeStruct((S, ZW), BF16),
                   jax.ShapeDtypeStruct((S, D), BF16), jax.ShapeDtypeStruct((2, D), F32),
                   jax.ShapeDtypeStruct((2, DM), F32), jax.ShapeDtypeStruct((NG, CHUNK, CHUNK), F32),
                   jax.ShapeDtypeStruct((CHUNK, LANES), F32), jax.ShapeDtypeStruct((M, 2 * DQ), F32)),
        in_specs=[_rows(tm, ZW), _rows(tm, D), _rows(tm, D), _full(wa.shape), _full(sg.shape), _full(sb.shape),
                  _full(ws.shape), _full(bst.shape), _full(mkv.shape), _full(wo.shape), _full((1, D))],
        out_specs=(_rows(tm, D), _rows(tm, ZW), _rows(tm, D), _full((2, D)), _full((2, DM)),
                   _full((NG, CHUNK, CHUNK)), _full((CHUNK, LANES)), _full((M, 2 * DQ))),
        scratch_shapes=[pltpu.VMEM((tm, ZW), F32), pltpu.VMEM((tm, DM), BF16), pltpu.VMEM((tm, DM), F32)],
        args=(zb, dxo, r, wa, sg, sb, ws, bst, mkv, wo, g))


def _proj_b_fwd(xb, wq, wkv, dm, name):
    S, D = xb.shape
    DQ = D - dm
    tm = min(512, S)

    def body(x_ref, wq_ref, wkv_ref, q_ref, qm_ref, k_ref, v_ref, fl_ref):
        xv = x_ref[...]
        z = _dot(xv, wq_ref[...])
        q_ref[...] = z[:, :dm].astype(BF16)
        qm_ref[...] = z[:, dm:].astype(BF16)
        kvf = _dot(xv, wkv_ref[...])
        k_ref[...] = kvf[:, :dm].astype(BF16)
        v_ref[...] = kvf[:, dm:2 * dm].astype(BF16)
        fl_ref[...] = kvf[:, 2 * dm:]

    return pl.pallas_call(
        body, name=name, grid=(S // tm,),
        out_shape=(jax.ShapeDtypeStruct((S, dm), BF16), jax.ShapeDtypeStruct((S, DQ), BF16),
                   jax.ShapeDtypeStruct((S, dm), BF16), jax.ShapeDtypeStruct((S, dm), BF16),
                   jax.ShapeDtypeStruct((S, LANES), F32)),
        in_specs=[_rows(tm, D), _full(wq.shape), _full(wkv.shape)],
        out_specs=(_rows(tm, dm), _rows(tm, DQ), _rows(tm, dm), _rows(tm, dm), _rows(tm, LANES)),
        compiler_params=_params(),
    )(xb, wq, wkv)


def _proj_b_bwd(dr, dqz, dk, dv, dfl, wq, wkv, name):
    S, D = dr.shape
    dm = dk.shape[1]
    tm = min(512, S)

    def body(dr_ref, dqz_ref, dk_ref, dv_ref, dfl_ref, wq_ref, wkv_ref, dx_ref):
        dx_ref[...] = (ALPHA * dr_ref[...] + _dot_nt(dqz_ref[...], wq_ref[...])
                       + _dot_nt(dk_ref[...], wkv_ref[:, :dm]) + _dot_nt(dv_ref[...], wkv_ref[:, dm:2 * dm])
                       + _dot_nt(dfl_ref[...], wkv_ref[:, 2 * dm:]))

    return pl.pallas_call(
        body, name=name, grid=(S // tm,), out_shape=jax.ShapeDtypeStruct((S, D), F32),
        in_specs=[_rows(tm, D), _rows(tm, dqz.shape[1]), _rows(tm, dm), _rows(tm, dm), _rows(tm, dfl.shape[1]),
                  _full(wq.shape), _full(wkv.shape)],
        out_specs=_rows(tm, D), compiler_params=_params(),
    )(dr, dqz, dk, dv, dfl, wq, wkv)


def _mm_nn_cat(at, bs, name):
    M, R = at.shape
    widths = [b.shape[1] for b in bs]
    N = sum(widths)
    br = _pick(R, 1024)
    last = R // br - 1
    nb = len(bs)

    def body(a_ref, *refs):
        b_refs, o_ref, acc_ref = refs[:nb], refs[nb], refs[nb + 1]

        @pl.when(pl.program_id(0) == 0)
        def _():
            acc_ref[...] = jnp.zeros_like(acc_ref)

        a = a_ref[...]
        c0 = 0
        for b_ref, wd in zip(b_refs, widths):
            acc_ref[:, c0:c0 + wd] += _dot(a, b_ref[...])
            c0 += wd

        @pl.when(pl.program_id(0) == last)
        def _():
            o_ref[...] = acc_ref[...].astype(BF16)

    return pl.pallas_call(
        body, name=name, grid=(R // br,), out_shape=jax.ShapeDtypeStruct((M, N), BF16),
        in_specs=[pl.BlockSpec((M, br), lambda r: (0, r))] + [pl.BlockSpec((br, wd), lambda r: (r, 0)) for wd in widths],
        out_specs=_full((M, N)), scratch_shapes=[pltpu.VMEM((M, N), F32)], compiler_params=_params(),
    )(at, *bs)


def _split3(v):
    hi = v.astype(BF16)
    r1 = v - hi.astype(F32)
    mid = r1.astype(BF16)
    lo = (r1 - mid.astype(F32)).astype(BF16)
    return hi, mid, lo


def _tri_sum(v, tri_b):
    hi, mid, lo = _split3(v)
    return _dot(hi, tri_b) + _dot(mid, tri_b) + _dot(lo, tri_b)


def _log_sigmoid(x):
    return jnp.minimum(x, 0.0) - jnp.log(1.0 + jnp.exp(-jnp.abs(x)))


def _forget_cumsum(flt, bf, name):
    H, S = flt.shape
    nchunk = S // LANES

    def body(fl_ref, bf_ref, p_ref):
        upper = (lax.broadcasted_iota(jnp.int32, (LANES, LANES), 0)
                 <= lax.broadcasted_iota(jnp.int32, (LANES, LANES), 1)).astype(BF16)

        def step(n, carry):
            s0 = pl.multiple_of(n * LANES, LANES)
            lf = _log_sigmoid(fl_ref[:, pl.ds(s0, LANES)] + bf_ref[...])
            cs = _tri_sum(lf, upper) + carry
            for i, piece in enumerate(_split3(-cs)):
                p_ref[i, :, pl.ds(s0, LANES)] = piece.astype(F32)
            return cs[:, LANES - 1:LANES]

        lax.fori_loop(0, nchunk, step, jnp.zeros((H, 1), F32))

    return pl.pallas_call(body, name=name, out_shape=jax.ShapeDtypeStruct((3, H, S), F32),
                          compiler_params=pltpu.CompilerParams(vmem_limit_bytes=VMEM_LIMIT))(flt, bf)


def _forget_cumsum_bwd(dct, flt, bf, name):
    H, S = flt.shape
    nchunk = S // LANES

    def body(dc_ref, fl_ref, bf_ref, dfl_ref, dbf_ref):
        lower = (lax.broadcasted_iota(jnp.int32, (LANES, LANES), 0)
                 >= lax.broadcasted_iota(jnp.int32, (LANES, LANES), 1)).astype(BF16)

        def step(n, carry):
            tail, tot = carry
            s0 = pl.multiple_of((nchunk - 1 - n) * LANES, LANES)
            suffix = _tri_sum(dc_ref[:, pl.ds(s0, LANES)], lower) + tail
            xv = fl_ref[:, pl.ds(s0, LANES)] + bf_ref[...]
            dfl = suffix * (1.0 / (1.0 + jnp.exp(xv)))
            dfl_ref[:, pl.ds(s0, LANES)] = dfl
            return suffix[:, 0:1], tot + jnp.sum(dfl, axis=-1, keepdims=True)

        _, tot = lax.fori_loop(0, nchunk, step, (jnp.zeros((H, 1), F32), jnp.zeros((H, 1), F32)))
        dbf_ref[...] = jnp.broadcast_to(tot, (H, LANES))

    return pl.pallas_call(body, name=name,
                          out_shape=(jax.ShapeDtypeStruct((H, S), F32), jax.ShapeDtypeStruct((H, LANES), F32)),
                          compiler_params=pltpu.CompilerParams(vmem_limit_bytes=VMEM_LIMIT))(dct, flt, bf)


N_BIAS = 3


def _bias_ones():
    lane = lax.broadcasted_iota(jnp.int32, (1, PAIR), 1)
    return ((lane & (HEAD_DIM - 1)) < N_BIAS).astype(BF16)


def _fox_fwd(q, k, v, ca, name, comm=None):
    S, DM = q.shape
    NP = DM // PAIR
    T = min(FOX_BLOCK, S)
    nq = S // T

    def body(q_ref, k_ref, v_ref, ca_ref, o_ref, lse_ref):
        tril = _causal((T, T))
        heads = (_lane_mask(0), _lane_mask(1))
        ones3 = _bias_ones()

        def kv_block(kj):
            s0 = pl.multiple_of(kj * T, T)
            kb, vb, cab = k_ref[pl.ds(s0, T), :], v_ref[pl.ds(s0, T), :], ca_ref[pl.ds(s0, T), :]
            return ([jnp.where(heads[e], kb, cab) for e in (0, 1)],
                    [jnp.where(heads[e], vb, jnp.ones_like(vb)) for e in (0, 1)])

        def q_pair(a, _):
            t0s = [pl.multiple_of((2 * a + r) * T, T) for r in (0, 1)]
            qaug = {}
            for r in (0, 1):
                qb = q_ref[pl.ds(t0s[r], T), :] * QK_SCALE
                for e in (0, 1):
                    qaug[r, e] = jnp.where(heads[e], qb, ones3)
            chains = [(r, e) for r in (0, 1) for e in (0, 1)]

            def update(carries, first_block, seen):
                blocks = [kv_block(first_block + kk) for kk in range(1 + max(kk for s in seen for kk, _ in s))]
                out = []
                scores = [[_dot_nt(qaug[r, e], blocks[kk][0][e]) for kk, _ in seen[r]] for r, e in chains]
                for n, (r, e) in enumerate(chains):
                    m, acc = carries[n]
                    tiles = [jnp.where(tril, s, NEG) if masked else s for s, (_, masked) in zip(scores[n], seen[r])]
                    m_new = jnp.maximum(m, jnp.max(functools.reduce(jnp.maximum, tiles), axis=-1, keepdims=True))
                    acc = jnp.exp(m - m_new) * acc
                    for s, (kk, _) in zip(tiles, seen[r]):
                        acc = acc + _dot(jnp.exp(s - m_new).astype(BF16), blocks[kk][1][e])
                    out.append((m_new, acc))
                return tuple(out)

            def full(first_block, n_blocks, carries):
                return update(carries, first_block, [[(kk, False) for kk in range(n_blocks)]] * 2)

            init = (jnp.full((T, 1), NEG, F32), jnp.zeros((T, PAIR), F32))
            carries = lax.fori_loop(0, a // 4, lambda i, c: full(8 * i, 8, c), (init,) * 4)
            carries = lax.fori_loop(0, (a % 4) // 2, lambda i, c: full(8 * (a // 4), 4, c), carries)
            carries = lax.fori_loop(0, a % 2, lambda i, c: full(2 * a - 2, 2, c), carries)
            carries = update(carries, 2 * a, [[(0, True)], [(0, False), (1, True)]])
            first = _first_head()
            for r in (0, 1):
                (m0, acc0), (m1, acc1) = carries[2 * r], carries[2 * r + 1]
                l0, l1 = acc0[:, HEAD_DIM:HEAD_DIM + 1], acc1[:, 0:1]
                o_ref[pl.ds(t0s[r], T), :] = jnp.where(first, acc0 / l0, acc1 / l1).astype(BF16)
                lse_t = jnp.where(first, m0 + jnp.log(l0), m1 + jnp.log(l1)).T
                lse_ref[0:1, pl.ds(t0s[r], T)] = lse_t[0:1, :]
                lse_ref[1:2, pl.ds(t0s[r], T)] = lse_t[HEAD_DIM:HEAD_DIM + 1, :]
            return 0

        lax.fori_loop(0, nq // 2, q_pair, 0)

    col = pl.BlockSpec((S, PAIR), lambda hp: (0, hp))
    return _pcall(
        body, name=name, grid=(NP,), comm=comm,
        out_shape=(jax.ShapeDtypeStruct((S, DM), BF16), jax.ShapeDtypeStruct((NP, 2, S), F32)),
        in_specs=[col, col, col, col], out_specs=(col, pl.BlockSpec((None, 2, S), lambda hp: (hp, 0, 0))),
        args=(q, k, v, ca))


def _fox_bwd(q, k, v, o, do, ca, lse, name, comm=None):
    S, DM = q.shape
    NP = DM // PAIR
    T = min(FOX_BLOCK, S)
    nq = S // T

    def body(q_ref, k_ref, v_ref, o_ref, do_ref, ca_ref, lse_ref, dq_ref, dk_ref, dv_ref, dc_ref,
             dq_acc, dcol_acc, dk_acc, dv_acc, dd_s):
        keep = lax.broadcasted_iota(jnp.int32, (T, T), 0) <= lax.broadcasted_iota(jnp.int32, (T, T), 1)
        lane = lax.broadcasted_iota(jnp.int32, (1, LANES), 1)
        heads = (_lane_mask(0), _lane_mask(1))
        ones3 = _bias_ones()
        dq_acc[...] = jnp.zeros_like(dq_acc)
        dcol_acc[...] = jnp.zeros_like(dcol_acc)
        dc_ref[...] = jnp.zeros_like(dc_ref)

        def row_dots(qi, _):
            t0 = pl.multiple_of(qi * T, T)
            prod = do_ref[pl.ds(t0, T), :].astype(F32) * o_ref[pl.ds(t0, T), :].astype(F32)
            prod_t = prod.T
            dd_s[0:1, pl.ds(t0, T)] = jnp.sum(prod_t[0:HEAD_DIM, :], axis=0, keepdims=True)
            dd_s[1:2, pl.ds(t0, T)] = jnp.sum(prod_t[HEAD_DIM:PAIR, :], axis=0, keepdims=True)
            return 0

        lax.fori_loop(0, nq, row_dots, 0)

        def kv_pair(b, _):
            dk_acc[...] = jnp.zeros_like(dk_acc)
            dv_acc[...] = jnp.zeros_like(dv_acc)
            s0s = [pl.multiple_of((2 * b + jj) * T, T) for jj in (0, 1)]
            kaug, ks, vbs = {}, {}, []
            for jj in (0, 1):
                kb = k_ref[pl.ds(s0s[jj], T), :]
                cab = ca_ref[pl.ds(s0s[jj], T), :]
                vbs.append(v_ref[pl.ds(s0s[jj], T), :])
                for e in (0, 1):
                    kaug[jj, e] = jnp.where(heads[e], kb, cab)
                    ks[jj, e] = jnp.where(heads[e], kb, jnp.zeros_like(kb)) * QK_SCALE

            def q_step(qi, blocks):
                t0 = pl.multiple_of(qi * T, T)
                qs = q_ref[pl.ds(t0, T), :] * QK_SCALE
                dob = do_ref[pl.ds(t0, T), :]
                doh = [jnp.where(heads[e], dob, jnp.zeros_like(dob)) for e in (0, 1)]
                qh = [jnp.where(heads[e], qs, jnp.zeros_like(qs)) for e in (0, 1)]
                chains = [(e, jj, masked) for e in (0, 1) for jj, masked in blocks]
                scores = [_dot_nt(kaug[jj, e], jnp.where(heads[e], qs, ones3)) for e, jj, _ in chains]
                dps = [_dot_nt(vbs[jj], doh[e]) for e, jj, _ in chains]
                dq = jnp.zeros((T, PAIR), F32)
                drow = [jnp.zeros((1, T), F32), jnp.zeros((1, T), F32)]
                for n, (e, jj, masked) in enumerate(chains):
                    st = jnp.where(keep, scores[n], NEG) if masked else scores[n]
                    pt = jnp.exp(st - lse_ref[e:e + 1, pl.ds(t0, T)])
                    dv_acc[jj] += _dot(pt.astype(BF16), doh[e])
                    dst = pt * (dps[n] - dd_s[e:e + 1, pl.ds(t0, T)])
                    dsb = dst.astype(BF16)
                    dk_acc[jj] += _dot(dsb, qh[e])
                    dq = dq + _dot_tn(dsb, ks[jj, e])
                    dcol_acc[pl.ds(s0s[jj], T), :] -= jnp.where(lane == e, jnp.sum(dst, axis=-1, keepdims=True), 0.0)
                    drow[e] = drow[e] + jnp.sum(dst, axis=0, keepdims=True)
                for e in (0, 1):
                    dc_ref[e:e + 1, pl.ds(t0, T)] += drow[e]
                dq_acc[pl.ds(t0, T), :] += dq

            q_step(2 * b, [(0, True)])
            q_step(2 * b + 1, [(0, False), (1, True)])

            def rest(i, _):
                q_step(2 * b + 2 + 2 * i, [(0, False), (1, False)])
                q_step(2 * b + 3 + 2 * i, [(0, False), (1, False)])
                return 0

            lax.fori_loop(0, (nq - 2) // 2 - b, rest, 0)
            for jj in (0, 1):
                dk_ref[pl.ds(s0s[jj], T), :] = dk_acc[jj].astype(BF16)
                dv_ref[pl.ds(s0s[jj], T), :] = dv_acc[jj].astype(BF16)
            return 0

        lax.fori_loop(0, nq // 2, kv_pair, 0)
        dq_ref[...] = dq_acc[...].astype(BF16)

        def add_cols(kj, _):
            s0 = pl.multiple_of(kj * T, T)
            dc_ref[:, pl.ds(s0, T)] += dcol_acc[pl.ds(s0, T), :].T[0:2, :]
            return 0

        lax.fori_loop(0, nq, add_cols, 0)

    col = pl.BlockSpec((S, PAIR), lambda hp: (0, hp))
    row2 = pl.BlockSpec((None, 2, S), lambda hp: (hp, 0, 0))
    return _pcall(
        body, name=name, grid=(NP,), comm=comm,
        out_shape=(jax.ShapeDtypeStruct(do.shape, BF16), jax.ShapeDtypeStruct((S, DM), BF16),
                   jax.ShapeDtypeStruct((S, DM), BF16), jax.ShapeDtypeStruct((NP, 2, S), F32)),
        in_specs=[col, col, col, col, col, col, row2], out_specs=(col, col, col, row2),
        scratch_shapes=[pltpu.VMEM((S, PAIR), F32), pltpu.VMEM((S, LANES), F32), pltpu.VMEM((2, T, PAIR), F32),
                        pltpu.VMEM((2, T, PAIR), F32), pltpu.VMEM((8, S), F32)],
        args=(q, k, v, o, do, ca, lse), aliases={4: 0})


def _mixer_b_fwd(x, o, qm, mkv, wo, g, b, name):
    S, D = x.shape
    DQ = qm.shape[1]
    DM = D - DQ
    tm = min(512, S)

    def body(x_ref, o_ref, qm_ref, mkv_ref, wo_ref, g_ref, b_ref, xo_ref, xbt_ref, r_ref, catt_ref, cat_ref):
        cat_ref[:, :DM] = o_ref[...]
        mo = _mem_attn_fwd(qm_ref[...], mkv_ref, DQ)
        for mp in range(DQ // PAIR):
            cat_ref[:, DM + mp * PAIR:DM + (mp + 1) * PAIR] = mo[mp].astype(BF16)
        _tail_fwd_store(x_ref[...], cat_ref, wo_ref, g_ref, b_ref, xo_ref, xbt_ref, r_ref, catt_ref)

    return pl.pallas_call(
        body, name=name, grid=(S // tm,),
        out_shape=(jax.ShapeDtypeStruct((S, D), F32), jax.ShapeDtypeStruct((D, S), BF16),
                   jax.ShapeDtypeStruct((S, D), F32), jax.ShapeDtypeStruct((D, S), BF16)),
        in_specs=[_rows(tm, D), _rows(tm, DM), _rows(tm, DQ), _full(mkv.shape), _full(wo.shape), _full((1, D)),
                  _full((1, D))],
        out_specs=(_rows(tm, D), _cols(D, tm), _rows(tm, D), _cols(D, tm)),
        scratch_shapes=[pltpu.VMEM((tm, D), BF16)],
        compiler_params=_params(),
    )(x, o, qm, mkv, wo, g, b)


def _mixer_b_bwd(dxo, r, qm, mkv, wo, g, name, comm=None):
    S, D = r.shape
    DQ = qm.shape[1]
    DM = D - DQ
    M = mkv.shape[0]
    tm = min(512, S)

    def body(dxo_ref, r_ref, qm_ref, mkv_ref, wo_ref, g_ref, dr_ref, drb_ref, dcat_ref, dgb_ref, dmkv_ref):
        @pl.when(pl.program_id(0) == 0)
        def _():
            dgb_ref[...] = jnp.zeros_like(dgb_ref)
            dmkv_ref[...] = jnp.zeros_like(dmkv_ref)

        dr, dcat = _tail_bwd_head(dxo_ref, r_ref, g_ref, wo_ref, dgb_ref, drb_ref)
        dr_ref[...] = dr
        dcat_ref[:, :DM] = dcat[:, :DM].astype(BF16)
        dmo = [dcat[:, DM + mp * PAIR:DM + (mp + 1) * PAIR] for mp in range(DQ // PAIR)]
        dqm = _mem_attn_bwd(qm_ref[...], mkv_ref, dmo, dmkv_ref, DQ)
        for mp in range(DQ // PAIR):
            dcat_ref[:, DM + mp * PAIR:DM + (mp + 1) * PAIR] = dqm[mp].astype(BF16)

    return _pcall(
        body, name=name, grid=(S // tm,), comm=comm,
        out_shape=(jax.ShapeDtypeStruct((S, D), F32), jax.ShapeDtypeStruct((S, D), BF16),
                   jax.ShapeDtypeStruct((S, D), BF16), jax.ShapeDtypeStruct((2, D), F32),
                   jax.ShapeDtypeStruct((M, 2 * DQ), F32)),
        in_specs=[_rows(tm, D), _rows(tm, D), _rows(tm, DQ), _full(mkv.shape), _full(wo.shape), _full((1, D))],
        out_specs=(_rows(tm, D), _rows(tm, D), _rows(tm, D), _full((2, D)), _full((M, 2 * DQ))),
        args=(dxo, r, qm, mkv, wo, g))


class _NoExchange:
    def plan(self, stage, w, grads):
        return None

    def done(self, stage, results, w):
        pass


def _local_step(x, mem, target, w, hooks):
    S, D = x.shape
    DQ = w["mkv0"].shape[1] // 2
    DM = D - DQ
    NG = DM // HEAD_DIM
    NP = DM // PAIR
    ln_g, ln_b = w["ln_g"], w["ln_b"]
    row = lambda a, i: a[i:i + 1]
    memb = mem.astype(BF16)
    bst =jnp.pad(w["b_s"].T, ((0, 0), (0, LANES - NG)))
    grads = {}

    def staged(stage, fn, *args):
        out, got = fn(*args, stage, comm=hooks.plan(stage, w, grads))
        hooks.done(stage, got, w)
        return out

    mkv0 = _mm_small(memb, w["mkv0"], "mem_kv_0")
    x1, x1t, r1, cat0t, x0t, z0b = staged("mixer_a_fwd", _mixer_a_fwd, x, w["wa"], w["sg"], w["sb"], w["ws"], bst, mkv0,
                                w["wo0"], row(ln_g, 0), row(ln_b, 0))
    x2, x2b, x2t, r2, h0 = staged("mlp_fwd_0", _mlp_fwd, x1, w["wup0"], w["wdown0"], row(ln_g, 1), row(ln_b, 1))
    q, qm, k, v, fl = _proj_b_fwd(x2b, w["wq"], w["wkv"], DM, "proj_b_fwd")
    flt = fl[:, :16].T
    bfc = jnp.pad(w["bf"], (0, 16 - NG)).reshape(16, 1)
    pieces = _forget_cumsum(flt, bfc, "forget_cumsum")
    pieces = jnp.pad(pieces[:, :NG].transpose(2, 1, 0), ((0, 0), (0, 0), (0, HEAD_DIM - N_BIAS)))
    ca = pieces.reshape(S, NP, 2, HEAD_DIM)[:, :, ::-1].reshape(S, DM).astype(BF16)
    o, lse = staged("fox_fwd", _fox_fwd, q, k, v, ca)
    mkv1 = _mm_small(memb, w["mkv1"], "mem_kv_1")
    x3, x3t, r3, cat1t = _mixer_b_fwd(x2, o, qm, mkv1, w["wo1"], row(ln_g, 2), row(ln_b, 2), "mixer_b_fwd")
    dx4, r4, h1, loss_part = _mlp_fwd_loss(x3, w["wup1"], w["wdown1"], row(ln_g, 3), row(ln_b, 3), target,
                                           "mlp_fwd_1")

    nc = w["wup0"].shape[0]
    (dx3, dr4t, dpre1, dgb11), _ = _mlp_bwd(dx4, r4, h1, w["wup1"], w["wdown1"], row(ln_g, 3), "mlp_bwd_1")
    grads["wdown1"] = _mm_nn(dr4t, h1, "dw_down_1", out_split=nc)
    grads["wup1"] = _mm_nn(x3t, dpre1, "dw_up_1", out_split=nc)
    dr3, dr3b, dcat1, dgb10, dmkv1 = staged("mixer_b_bwd", _mixer_b_bwd, dx3, r3, qm, mkv1, w["wo1"], row(ln_g, 2))
    dqz, dk, dv, dct3 = staged("fox_bwd", _fox_bwd, q, k, v, o, dcat1, ca, lse)
    dct = jnp.pad(dct3.reshape(NG, S), ((0, 16 - NG), (0, 0)))
    dflt, dbf = _forget_cumsum_bwd(dct, flt, bfc, "forget_cumsum_bwd")
    dfl = jnp.pad(dflt.T, ((0, 0), (0, LANES - 16))).astype(BF16)
    dx2 = _proj_b_bwd(dr3, dqz, dk, dv, dfl, w["wq"], w["wkv"], "proj_b_bwd")
    grads["wo1"] = _mm_nn(cat1t, dr3b, "dw_o_1")
    grads["wq"] = _mm_nn(x2t, dqz, "dw_q")
    grads["wkv"] = _mm_nn_cat(x2t, [dk, dv, dfl], "dw_kv")
    grads["mkv1"] = _mm_tn(memb, dmkv1.astype(BF16), "dw_mkv_1")
    dx1, dr2t, dpre0, dgb01 = staged("mlp_bwd_0", _mlp_bwd, dx2, r2, h0, w["wup0"], w["wdown0"], row(ln_g, 1))
    grads["wdown0"] = _mm_nn(dr2t, h0, "dw_down_0", out_split=nc)
    grads["wup0"] = _mm_nn(x1t, dpre0, "dw_up_0", out_split=nc)
    dx0, dz, dr1b, dgb00, dsgb, dws, dbst, dmkv0 = staged(
        "mixer_a_bwd", _mixer_a_bwd, z0b, dx1, r1, w["wa"], w["sg"], w["sb"], w["ws"], bst, mkv0, w["wo0"], row(ln_g, 0))
    grads["wo0"] = _mm_nn(cat0t, dr1b, "dw_o_0")
    grads["wa"] = _mm_nn(x0t, dz, "dw_a")
    grads["mkv0"] = _mm_tn(memb, dmkv0.astype(BF16), "dw_mkv_0")
    grads.update({
        "ws": dws, "b_s": dbst[:, :NG].T, "sg": dsgb[0:1], "sb": dsgb[1:2], "bf": dbf[:NG, 0], "loss": loss_part[0:1],
        "ln_g": jnp.concatenate([dgb00[0:1], dgb01[0:1], dgb10[0:1], dgb11[0:1]], axis=0),
        "ln_b": jnp.concatenate([dgb00[1:2], dgb01[1:2], dgb10[1:2], dgb11[1:2]], axis=0),
    })
    return loss_part, dx0, grads


def _place():
    xi, yi, ci = lax.axis_index("x"), lax.axis_index("y"), lax.axis_index("c")
    peers = [(xi, 1 - yi), (1 - xi, yi), (1 - xi, 1 - yi)]
    return xi, yi, ci, peers


def _comm_call(body, name, ins, out_shapes, n_remote, n_local):
    return pl.pallas_call(
        body, name=name, out_shape=out_shapes, in_specs=[ANY] * len(ins), out_specs=[ANY] * len(out_shapes),
        scratch_shapes=[pltpu.SemaphoreType.DMA((n_remote,)), pltpu.SemaphoreType.DMA((n_remote,)),
                        pltpu.SemaphoreType.DMA((max(n_local, 1),))],
        compiler_params=pltpu.CompilerParams(has_side_effects=True),
    )(*ins)


def _pair_swap(xs):
    n = len(xs)
    out_shapes = [jax.ShapeDtypeStruct((a.shape[0],) + a.shape[2:], a.dtype) for a in xs]

    def copies(x_refs, o_refs, sems):
        send_sems, recv_sems = sems
        xi, yi, ci, _ = _place()
        return [pltpu.make_async_remote_copy(src_ref=x_refs[i].at[:, 1 - ci], dst_ref=o_refs[i],
                                             send_sem=send_sems.at[i], recv_sem=recv_sems.at[i],
                                             device_id=(xi, yi, 1 - ci), device_id_type=MESH) for i in range(n)]

    def start(x_refs, o_refs, sems):
        for cp in copies(x_refs, o_refs, sems):
            cp.start()

    def finish(x_refs, o_refs, sems):
        cps = copies(x_refs, o_refs, sems)
        for cp in cps:
            cp.wait_recv()
        for cp in cps:
            cp.wait_send()

    return _Comm(xs, out_shapes, {}, [n, n], start, finish)


def _chip_exchange(xs):
    n = len(xs)
    out_shapes = [jax.ShapeDtypeStruct((3,) + a.shape[1:], a.dtype) for a in xs]

    def copies(x_refs, o_refs, sems):
        send_sems, recv_sems = sems
        xi, yi, ci, peers = _place()
        return [pltpu.make_async_remote_copy(src_ref=x_refs[i].at[2 * px + py], dst_ref=o_refs[i].at[d],
                                             send_sem=send_sems.at[3 * i + d], recv_sem=recv_sems.at[3 * i + d],
                                             device_id=(px, py, ci), device_id_type=MESH)
                for i in range(n) for d, (px, py) in enumerate(peers)]

    def start(x_refs, o_refs, sems):
        for cp in copies(x_refs, o_refs, sems):
            cp.start()

    def finish(x_refs, o_refs, sems):
        cps = copies(x_refs, o_refs, sems)
        for cp in cps:
            cp.wait_recv()
        for cp in cps:
            cp.wait_send()

    return _Comm(xs, out_shapes, {}, [3 * n, 3 * n], start, finish)


def _inplace_call(body, name, bufs, sem_counts):
    n = len(bufs)
    return pl.pallas_call(
        body, name=name, out_shape=[jax.ShapeDtypeStruct(a.shape, a.dtype) for a in bufs],
        in_specs=[ANY] * n, out_specs=[ANY] * n, input_output_aliases={i: i for i in range(n)},
        scratch_shapes=[pltpu.SemaphoreType.DMA((k,)) for k in sem_counts],
        compiler_params=pltpu.CompilerParams(has_side_effects=True),
    )(*bufs)


def _gather_weights(bufs, whole=(), early=None):
    n, nw = len(bufs), len(whole)

    def copy(ref, s, r, k, dev):
        return pltpu.make_async_remote_copy(src_ref=ref, dst_ref=ref, send_sem=s.at[k], recv_sem=r.at[k],
                                            device_id=dev, device_id_type=MESH)

    def plan(w, sems):
        ici_s, ici_r, d2d_s, d2d_r = sems
        xi, yi, ci, peers = _place()
        xn, yn, sib = (1 - xi, yi, ci), (xi, 1 - yi, ci), (xi, yi, 1 - ci)
        j, jx, jy, jd = 2 * xi + yi, 2 * (1 - xi) + yi, 2 * xi + 1 - yi, 2 * (1 - xi) + 1 - yi
        c = {}
        for i in range(n):
            hh = bufs[i].shape[2] // 2
            lo, hi = pl.ds(0, hh), pl.ds(hh, hh)
            at = lambda slot, half, rows=None, i=i: w[i].at[slot, half] if rows is None else w[i].at[slot, half, rows]
            c["to_x", i] = (copy(at(j, ci), ici_s, ici_r, 4 * i, xn), copy(at(jx, ci), ici_s, ici_r, 4 * i, xn))
            c["to_y", i] = (copy(at(j, ci), ici_s, ici_r, 4 * i + 1, yn), copy(at(jy, ci), ici_s, ici_r, 4 * i + 1, yn))
            c["x_on", i] = (copy(at(jx, ci, lo), ici_s, ici_r, 4 * i + 2, yn), copy(at(jd, ci, lo), ici_s, ici_r, 4 * i + 2, yn))
            c["y_on", i] = (copy(at(jy, ci, hi), ici_s, ici_r, 4 * i + 3, xn), copy(at(jd, ci, hi), ici_s, ici_r, 4 * i + 3, xn))
            for k, slot in enumerate((jx, jy, jd)):
                c["sib", i, k] = (copy(at(slot, ci), d2d_s, d2d_r, 3 * i + k, sib),
                                  copy(at(slot, 1 - ci), d2d_s, d2d_r, 3 * i + k, sib))
        for i in range(nw):
            for d, (px, py) in enumerate(peers):
                k = 4 * n + 3 * i + d
                c["whole", i, d] = (copy(w[n + i].at[j], ici_s, ici_r, k, (px, py, ci)),
                                    copy(w[n + i].at[2 * px + py], ici_s, ici_r, k, (px, py, ci)))
        return c

    def start(_, w, sems):
        c = plan(w, sems)
        for i in range(n):
            c["to_x", i][0].start()
            c["to_y", i][0].start()
        for i in range(nw):
            for d in range(3):
                c["whole", i, d][0].start()

    def pass_on(c, which):
        for i in which:
            c["to_x", i][1].wait_recv()
            c["x_on", i][0].start()
            c["sib", i, 0][0].start()
            c["to_y", i][1].wait_recv()
            c["y_on", i][0].start()
            c["sib", i, 1][0].start()

    n_early = n if early is None else min(early, n)

    def middle(_, w, sems):
        pass_on(plan(w, sems), range(n_early))

    def finish(_, w, sems):
        c = plan(w, sems)
        pass_on(c, range(n_early, n))
        for i in range(n):
            c["x_on", i][1].wait_recv()
            c["y_on", i][1].wait_recv()
            c["sib", i, 2][0].start()
        for i in range(n):
            for k in range(3):
                c["sib", i, k][1].wait_recv()
        for i in range(nw):
            for d in range(3):
                c["whole", i, d][1].wait_recv()
        for key, (sent, _) in c.items():
            sent.wait_send()

    arrays = list(bufs) + list(whole)
    k = 4 * n + 3 * nw
    return _Comm(arrays, [jax.ShapeDtypeStruct(a.shape, a.dtype) for a in arrays], {i: i for i in range(n + nw)},
                 [k, k, max(3 * n, 1), max(3 * n, 1)], start, finish, middle)


def _pair_gather(bufs, spread, name):
    n = len(bufs)
    where = [(i, l) for i, a in enumerate(bufs) for l in range(a.shape[0])]
    nw = len(where)
    arrays = list(bufs) + ([] if spread is None else [spread])
    na = len(arrays)

    def body(*refs):
        g = refs[na:na + n]
        sp = refs[na + n] if spread is not None else None
        send_sems, recv_sems = refs[2 * na:]
        xi, yi, ci, peers = _place()
        chips = [(xi, yi)] + peers
        others = [(d, f) for d in range(4) for f in (0, 1) if (d, f) != (0, 0)] if spread is not None else []
        sends = []
        for k, (i, l) in enumerate(where):
            mine = g[i].at[l, :, ci]
            cp = pltpu.make_async_remote_copy(src_ref=mine, dst_ref=mine, send_sem=send_sems.at[k],
                                              recv_sem=recv_sems.at[k], device_id=(xi, yi, 1 - ci), device_id_type=MESH)
            cp.start()
            sends.append(cp)
        for k, (d, f) in enumerate(others):
            mine = sp.at[2 * xi + yi, ci]
            cp = pltpu.make_async_remote_copy(src_ref=mine, dst_ref=mine, send_sem=send_sems.at[nw + k],
                                              recv_sem=recv_sems.at[nw + k],
                                              device_id=(chips[d][0], chips[d][1], ci if f == 0 else 1 - ci),
                                              device_id_type=MESH)
            cp.start()
            sends.append(cp)
        for k, (i, l) in enumerate(where):
            other = g[i].at[l, :, 1 - ci]
            pltpu.make_async_remote_copy(src_ref=other, dst_ref=other, send_sem=send_sems.at[k],
                                         recv_sem=recv_sems.at[k], device_id=(xi, yi, 1 - ci),
                                         device_id_type=MESH).wait_recv()
        for k, (d, f) in enumerate(others):
            px, py, pc = chips[d][0], chips[d][1], (ci if f == 0 else 1 - ci)
            theirs = sp.at[2 * px + py, pc]
            pltpu.make_async_remote_copy(src_ref=theirs, dst_ref=theirs, send_sem=send_sems.at[nw + k],
                                         recv_sem=recv_sems.at[nw + k], device_id=(px, py, pc),
                                         device_id_type=MESH).wait_recv()
        for cp in sends:
            cp.wait_send()

    n_sems = nw + (7 if spread is not None else 0)
    out = _inplace_call(body, name, arrays, [n_sems, n_sems])
    return out[:n], (out[n] if spread is not None else None)


def _row_block(rows, cols):
    want = max(8, (2 ** 18 // max(cols, 1)) // 8 * 8)
    if rows <= want:
        return rows
    best = 8
    for t in range(8, want + 1, 8):
        if rows % t == 0:
            best = t
    return best


def _pair_add(g4, recv, sel, out_dtype, name):
    A, _, H, C = g4.shape
    bh = _row_block(H, C)

    def body(sel_ref, g_ref, r_ref, o_ref):
        o_ref[...] = (g_ref[...].astype(F32) + r_ref[...].astype(F32)).astype(out_dtype)

    return pl.pallas_call(
        body, name=name, out_shape=jax.ShapeDtypeStruct((A, H, C), out_dtype),
        grid_spec=pltpu.PrefetchScalarGridSpec(
            num_scalar_prefetch=1, grid=(A, H // bh),
            in_specs=[pl.BlockSpec((None, None, bh, C), lambda a, i, s: (a, s[1], i, 0)),
                      pl.BlockSpec((None, bh, C), lambda a, i, s: (a, i, 0))],
            out_specs=pl.BlockSpec((None, bh, C), lambda a, i, s: (a, i, 0))),
        compiler_params=pltpu.CompilerParams(dimension_semantics=("arbitrary", "arbitrary"),
                                             vmem_limit_bytes=VMEM_LIMIT),
    )(sel, g4, recv)


def _chip_reduce(g4, recv1, recv2, sel, buf, layer, n_layers, name, by_chip=False):
    _, _, H, C = g4.shape
    bh = _row_block(H, C)
    if by_chip:
        out_shape = jax.ShapeDtypeStruct((N_CHIPS, 2, H, C), F32)
        out_spec = pl.BlockSpec((None, None, bh, C), lambda i, s: (s[0], s[1], i, 0))
    else:
        out_shape = jax.ShapeDtypeStruct((n_layers, 1, 2, H, C), F32)
        out_spec = pl.BlockSpec((None, None, None, bh, C), lambda i, s: (layer, 0, s[1], i, 0))

    def body(sel_ref, g_ref, r1_ref, r2_ref, *rest):
        acc = g_ref[...].astype(F32) + r1_ref[...].astype(F32)
        for d in range(3):
            acc = acc + r2_ref[d].astype(F32)
        rest[-1][...] = acc

    in_specs = [pl.BlockSpec((None, None, bh, C), lambda i, s: (s[0], s[1], i, 0)),
                pl.BlockSpec((None, bh, C), lambda i, s: (s[0], i, 0)),
                pl.BlockSpec((3, bh, C), lambda i, s: (0, i, 0))]
    args = [sel, g4, recv1, recv2]
    aliases = {}
    if buf is not None:
        in_specs.append(ANY)
        args.append(buf)
        aliases = {4: 0}
    return pl.pallas_call(
        body, name=name, out_shape=out_shape,
        grid_spec=pltpu.PrefetchScalarGridSpec(num_scalar_prefetch=1, grid=(H // bh,), in_specs=in_specs,
                                               out_specs=out_spec),
        input_output_aliases=aliases,
        compiler_params=pltpu.CompilerParams(dimension_semantics=("arbitrary",), vmem_limit_bytes=VMEM_LIMIT),
    )(*args)


def _chip_reduce_many(groups, sel, comm, name):
    flat = [t for grp in groups for t in grp]
    n, ng = len(flat), len(groups)
    _, _, H, C = flat[0][0].shape
    bh = _row_block(H, C)
    last = H // bh - 1
    nci, nco = len(comm.ins), len(comm.out_shapes)

    def body(sel_ref, *refs):
        ins, refs = refs[:3 * n], refs[3 * n:]
        cins, refs = refs[:nci], refs[nci:]
        outs, refs = refs[:ng], refs[ng:]
        couts, sems = refs[:nco], refs[nco:]

        @pl.when(pl.program_id(0) == 0)
        def _():
            comm.start(cins, couts, sems)

        k = 0
        for gi, grp in enumerate(groups):
            for layer in range(len(grp)):
                g_ref, r1_ref, r2_ref = ins[3 * k:3 * k + 3]
                acc = g_ref[...].astype(F32) + r1_ref[...].astype(F32)
                for d in range(3):
                    acc = acc + r2_ref[d].astype(F32)
                outs[gi][layer] = acc
                k += 1

        @pl.when(pl.program_id(0) == last)
        def _():
            if comm.middle is not None:
                comm.middle(cins, couts, sems)
            comm.finish(cins, couts, sems)

    in_specs = [pl.BlockSpec((None, None, bh, C), lambda i, s: (s[0], s[1], i, 0)),
                pl.BlockSpec((None, bh, C), lambda i, s: (s[0], i, 0)),
                pl.BlockSpec((3, bh, C), lambda i, s: (0, i, 0))] * n + [ANY] * nci
    out_specs = [pl.BlockSpec((len(grp), None, None, bh, C), lambda i, s: (0, 0, s[1], i, 0)) for grp in groups]
    res = pl.pallas_call(
        body, name=name,
        out_shape=[jax.ShapeDtypeStruct((len(grp), 1, 2, H, C), F32) for grp in groups] + list(comm.out_shapes),
        grid_spec=pltpu.PrefetchScalarGridSpec(
            num_scalar_prefetch=1, grid=(H // bh,), in_specs=in_specs, out_specs=out_specs + [ANY] * nco,
            scratch_shapes=[pltpu.SemaphoreType.DMA((k,)) for k in comm.sem_counts]),
        input_output_aliases={1 + 3 * n + a: ng + b for a, b in comm.aliases.items()},
        compiler_params=pltpu.CompilerParams(dimension_semantics=("arbitrary",), vmem_limit_bytes=VMEM_LIMIT,
                                             has_side_effects=True),
    )(sel, *[a for t in flat for a in t], *comm.ins)
    return res[:ng], res[ng:]


def _cast_place(shards, layer, sel, name):
    _, _, H, C = shards.shape
    bh = _row_block(H, C)

    def body(sel_ref, x_ref, o_ref):
        o_ref[...] = x_ref[...].astype(BF16)

    return pl.pallas_call(
        body, name=name, out_shape=jax.ShapeDtypeStruct((N_CHIPS, 2, H, C), BF16),
        grid_spec=pltpu.PrefetchScalarGridSpec(
            num_scalar_prefetch=1, grid=(2, H // bh),
            in_specs=[pl.BlockSpec((None, None, bh, C), lambda h, i, s: (layer, h, i, 0))],
            out_specs=pl.BlockSpec((None, None, bh, C), lambda h, i, s: (s[0], h, i, 0))),
        compiler_params=pltpu.CompilerParams(dimension_semantics=("arbitrary", "arbitrary"),
                                             vmem_limit_bytes=VMEM_LIMIT),
    )(sel, shards)


def _cast_place_many(pairs, sel, comm, name):
    n = len(pairs)
    _, _, H, C = pairs[0][0].shape
    bh = _row_block(H, C)
    last = H // bh - 1
    nci, nco = len(comm.ins), len(comm.out_shapes)

    def body(sel_ref, *refs):
        xs, refs = refs[:n], refs[n:]
        cins, refs = refs[:nci], refs[nci:]
        outs, refs = refs[:n], refs[n:]
        couts, sems = refs[:nco], refs[nco:]
        h, i = pl.program_id(0), pl.program_id(1)

        @pl.when((h == 0) & (i == 0))
        def _():
            comm.start(cins, couts, sems)

        if comm.middle is not None:
            @pl.when((h == 1) & (i == 0))
            def _():
                comm.middle(cins, couts, sems)

        for x_ref, o_ref in zip(xs, outs):
            o_ref[...] = x_ref[...].astype(BF16)

        @pl.when((h == 1) & (i == last))
        def _():
            comm.finish(cins, couts, sems)

    res = pl.pallas_call(
        body, name=name,
        out_shape=[jax.ShapeDtypeStruct((N_CHIPS, 2, H, C), BF16)] * n + list(comm.out_shapes),
        grid_spec=pltpu.PrefetchScalarGridSpec(
            num_scalar_prefetch=1, grid=(2, H // bh),
            in_specs=[pl.BlockSpec((None, None, bh, C), lambda h, i, s, layer=layer: (layer, h, i, 0))
                      for _, layer in pairs] + [ANY] * nci,
            out_specs=[pl.BlockSpec((None, None, bh, C), lambda h, i, s: (s[0], h, i, 0))] * n + [ANY] * nco,
            scratch_shapes=[pltpu.SemaphoreType.DMA((k,)) for k in comm.sem_counts]),
        input_output_aliases={1 + n + a: n + b for a, b in comm.aliases.items()},
        compiler_params=pltpu.CompilerParams(dimension_semantics=("arbitrary", "arbitrary"),
                                             vmem_limit_bytes=VMEM_LIMIT, has_side_effects=True),
    )(sel, *[s for s, _ in pairs], *comm.ins)
    return res[:n], res[n:]


def _adamw_many(quads, name, comm=None):
    R, C = quads[0][0].shape
    br = _row_block(R, C)
    n = len(quads)

    def body(*refs):
        for k in range(n):
            g_ref, w_ref, m_ref, v_ref = refs[4 * k:4 * k + 4]
            d_ref, mo_ref, vo_ref = refs[4 * n + 3 * k:4 * n + 3 * k + 3]
            gv = g_ref[...]
            mn = ADAM_B1 * m_ref[...] + (1.0 - ADAM_B1) * gv
            vn = ADAM_B2 * v_ref[...] + (1.0 - ADAM_B2) * (gv * gv)
            m_hat = mn / (1.0 - ADAM_B1 ** ADAM_STEP)
            v_hat = vn / (1.0 - ADAM_B2 ** ADAM_STEP)
            d_ref[...] = -ADAM_LR * (m_hat / (jnp.sqrt(v_hat) + ADAM_EPS) + ADAM_WD * w_ref[...])
            mo_ref[...] = mn
            vo_ref[...] = vn

    spec = pl.BlockSpec((br, C), lambda i: (i, 0))
    res, got = _pcall(body, name=name, grid=(R // br,), comm=comm, out_shape=(jax.ShapeDtypeStruct((R, C), F32),) * (3 * n),
                      in_specs=[spec] * (4 * n), out_specs=(spec,) * (3 * n), args=[a for q in quads for a in q])
    return [res[3 * k:3 * k + 3] for k in range(n)], got


def _adamw(g, w, m, v, name):
    return _adamw_many([(g, w, m, v)], name)[0][0]


def _part_rows(shape):
    return -(-math.prod(shape) // (8 * LANES)) * 8


def _pack_rows(arrs, total_rows):
    parts = []
    for a in arrs:
        flat = a.reshape(-1)
        rows = _part_rows(a.shape)
        parts.append(jnp.pad(flat, (0, rows * LANES - flat.shape[0])).reshape(rows, LANES))
    packed = jnp.concatenate(parts, axis=0)
    return jnp.pad(packed, ((0, total_rows - packed.shape[0]), (0, 0)))


def _unpack_rows(packed, shapes):
    out, r = [], 0
    for shp in shapes:
        size, rows = math.prod(shp), _part_rows(shp)
        out.append(packed[r:r + rows].reshape(-1)[:size].reshape(shp))
        r += rows
    return out


def _rows_of(shapes):
    return sum(_part_rows(s) for s in shapes)


_GATHER_STAGES = {"start": ["wa", "mkv0", "wo0"], "mixer_a_fwd": ["wup0", "wdown0"],
                  "mlp_fwd_0": ["wq", "wkv", "mkv1", "wo1"], "fox_fwd": ["wup1", "wdown1"]}
_SWAP_STAGES = {"mixer_b_bwd": ["wup1", "wdown1"]}
_REDUCE_STAGES = {"fox_bwd": ["wup1", "wdown1"], "mlp_bwd_0": ["wo1", "wq", "wkv", "mkv1"],
                  "mixer_a_bwd": ["wup0", "wdown0"], "end": ["wa", "wo0", "mkv0", "small"]}
_SMALL = ["ws", "b_s", "sg", "sb", "bf", "ln_g", "ln_b", "loss"]


class _Staged:
    def __init__(self, placed, sel, dm, ng, shard_cols, n_small):
        self.placed, self.sel, self.dm, self.ng, self.shard_cols, self.n_small = placed, sel, dm, ng, shard_cols, n_small
        self.g4, self.recv1, self.recv2 = {}, {}, {}

    def _weight(self, name, a):
        a = a.reshape(N_CHIPS, 2 * a.shape[2], a.shape[3])
        if name in ("wa", "wkv"):
            a = a.transpose(1, 0, 2).reshape(a.shape[1], -1)
            if name == "wkv":
                a = jnp.concatenate([a[:, :2 * self.dm],
                                     jnp.pad(a[:, 2 * self.dm:], ((0, 0), (0, LANES - self.ng)))], axis=1)
            return a
        if name.startswith("wup") or name.startswith("wdown"):
            return a
        return a.reshape(-1, a.shape[2])

    def _partial(self, name, grads):
        if name == "small":
            a = _pack_rows([grads[n] for n in _SMALL], self.n_small).reshape(N_CHIPS, -1, LANES)
        elif name in self.shard_cols:
            n = self.shard_cols[name]
            g = grads[name][:, :N_CHIPS * n]
            a = g.reshape(g.shape[0], N_CHIPS, n).transpose(1, 0, 2)
        elif name.startswith("wup") or name.startswith("wdown"):
            a = grads[name]
        else:
            g = grads[name]
            a = g.reshape(N_CHIPS, g.shape[0] // N_CHIPS, g.shape[1])
        return a.reshape(N_CHIPS, 2, a.shape[1] // 2, a.shape[2])

    def plan(self, stage, w, grads):
        if stage in _GATHER_STAGES:
            return _gather_weights([self.placed[n] for n in _GATHER_STAGES[stage]],
                                   early=1 if stage == "mixer_a_fwd" else None)
        if stage in _SWAP_STAGES:
            for n in _SWAP_STAGES[stage]:
                self.g4[n] = self._partial(n, grads)
            return _pair_swap([self.g4[n] for n in _SWAP_STAGES[stage]])
        if stage in _REDUCE_STAGES:
            names = _REDUCE_STAGES[stage]
            late = [n for n in names if n not in self.recv1]
            for n in late:
                self.g4[n] = self._partial(n, grads)
            if late:
                got = _run_comm(_pair_swap([self.g4[n] for n in late]), "reduce_pair_swap_" + stage)
                self.recv1.update(zip(late, got))
            return _chip_exchange([_pair_add(self.g4[n], self.recv1[n], self.sel, F32 if n == "small" else BF16,
                                             "reduce_pair_add_" + n) for n in names])
        return None

    def done(self, stage, results, w):
        if stage in _GATHER_STAGES:
            for n, a in zip(_GATHER_STAGES[stage], results):
                w[n] = self._weight(n, a)
        elif stage in _SWAP_STAGES:
            self.recv1.update(zip(_SWAP_STAGES[stage], results))
        elif stage in _REDUCE_STAGES:
            self.recv2.update(zip(_REDUCE_STAGES[stage], results))


def kernel(x, mem, a_w_in, a_sgu_ln_g, a_sgu_ln_b, a_w_s, a_b_s, kv_w, kv_b_f, b_w_q, mem_w_kv, w_o, ln_g, ln_b, w_up, w_down, loss_target, m_a_w_in, m_a_sgu_ln_g, m_a_sgu_ln_b, m_a_w_s, m_a_b_s, m_kv_w, m_kv_b_f, m_b_w_q, m_mem_w_kv, m_w_o, m_ln_g, m_ln_b, m_w_up, m_w_down, v_a_w_in, v_a_sgu_ln_g, v_a_sgu_ln_b, v_a_w_s, v_a_b_s, v_kv_w, v_kv_b_f, v_b_w_q, v_mem_w_kv, v_w_o, v_ln_g, v_ln_b, v_w_up, v_w_down):
    xi, yi, ci = lax.axis_index("x"), lax.axis_index("y"), lax.axis_index("c")
    chip = 2 * xi + yi
    sel = jnp.stack([chip, ci]).astype(jnp.int32)
    S, D = x.shape[1], x.shape[2]
    DQ = mem_w_kv.shape[2] // 2
    DM = D - DQ
    NG = DM // HEAD_DIM

    shards = {"wa": (a_w_in, 0), "wkv": (kv_w[None], 0), "wq": (b_w_q, 0), "mkv0": (mem_w_kv, 0), "mkv1": (mem_w_kv, 1),
              "wo0": (w_o, 0), "wo1": (w_o, 1), "wup0": (w_up, 0), "wup1": (w_up, 1), "wdown0": (w_down, 0),
              "wdown1": (w_down, 1)}
    halves = lambda s: s.reshape(s.shape[0], 2, s.shape[1] // 2, s.shape[2])
    mlp = ["wup0", "wup1", "wdown0", "wdown1"]
    placed = {n: _cast_place(halves(s), layer, sel, "cast_place_" + n)
              for n, (s, layer) in shards.items() if n not in mlp}
    ln_pack = jnp.concatenate([ln_g.reshape(4, -1), ln_b.reshape(4, -1)], axis=0)
    ln_buf = lax.dynamic_update_slice(jnp.zeros((N_CHIPS,) + ln_pack.shape, F32), ln_pack[None], (chip, 0, 0))
    mlp_placed, got = _cast_place_many(
        [(halves(shards[n][0]), shards[n][1]) for n in mlp], sel,
        _gather_weights([placed[n] for n in _GATHER_STAGES["start"]], whole=[ln_buf]), "cast_place_mlp")
    placed.update(zip(mlp, mlp_placed))
    ln_shape = (4, D)
    small_shapes = [a_w_s.shape, a_b_s.shape, a_sgu_ln_g.shape, a_sgu_ln_b.shape, kv_b_f.shape, ln_shape, ln_shape,
                    (1, LANES)]
    n_small = -(-_rows_of(small_shapes) // 64) * 64
    hooks = _Staged(placed, sel, DM, NG, {"wa": a_w_in.shape[2], "wkv": kv_w.shape[1]}, n_small)
    weights = {"sg": a_sgu_ln_g, "sb": a_sgu_ln_b, "ws": a_w_s[0], "b_s": a_b_s[0], "bf": kv_b_f}
    hooks.done("start", got[:-1], weights)
    ln_full = got[-1].transpose(1, 0, 2).reshape(8, D)
    weights["ln_g"], weights["ln_b"] = ln_full[:4], ln_full[4:]

    _, grad_x, gr = _local_step(x[0], mem[0], loss_target[0], weights, hooks)

    def chip_sums(groups):
        bufs = []
        for grp in groups:
            buf = None
            for layer, n in enumerate(grp):
                buf = _chip_reduce(hooks.g4[n], hooks.recv1[n], hooks.recv2[n], sel, buf, layer, len(grp),
                                   "reduce_chip_sum_" + n)
            bufs.append(buf)
        return bufs

    whole = lambda a: a.reshape(a.shape[0], 2 * a.shape[3], a.shape[4])
    flat = lambda a: a.reshape(-1, a.shape[-1])

    def update(name, g, w, m, v):
        d, mn, vn = _adamw(flat(g), flat(w), flat(m), flat(v), "adamw_" + name)
        return d.reshape(w.shape), mn.reshape(w.shape), vn.reshape(w.shape)

    parts = lambda n: (hooks.g4[n], hooks.recv1[n], hooks.recv2[n])
    mlp_sums, got = _chip_reduce_many([[parts("wup0"), parts("wup1")], [parts("wdown0"), parts("wdown1")]], sel,
                                      hooks.plan("end", weights, gr), "reduce_chip_sum_mlp")
    hooks.done("end", got, weights)
    small_mine = _chip_reduce(hooks.g4["small"], hooks.recv1["small"], hooks.recv2["small"], sel, None, 0, 1,
                              "reduce_chip_sum_small", by_chip=True)
    red, small_all = _pair_gather(chip_sums([["wa"], ["wkv"], ["wq"], ["mkv0", "mkv1"], ["wo0", "wo1"]])
                                  + list(mlp_sums), small_mine, "reduce_pair_gather")
    g_a_w_in, g_kv_w, g_b_w_q, g_mem_w_kv, g_w_o, g_w_up, g_w_down = [whole(a) for a in red]
    g_kv_w = g_kv_w[0]
    g_w_down = g_w_down.transpose(0, 2, 1)
    small_all = small_all.reshape(n_small, LANES)
    g_ws, g_bs, g_sg, g_sb, g_bf, g_lng, g_lnb, loss_row = _unpack_rows(small_all, small_shapes)
    loss = loss_row[0, 0]
    dsh = D // N_CHIPS
    g_ln_g = lax.dynamic_slice_in_dim(g_lng, chip * dsh, dsh, axis=1).reshape(ln_g.shape)
    g_ln_b = lax.dynamic_slice_in_dim(g_lnb, chip * dsh, dsh, axis=1).reshape(ln_b.shape)

    upd = {
        "a_w_in": update("a_w_in", g_a_w_in, a_w_in, m_a_w_in, v_a_w_in),
        "kv_w": update("kv_w", g_kv_w, kv_w, m_kv_w, v_kv_w),
        "b_w_q": update("b_w_q", g_b_w_q, b_w_q, m_b_w_q, v_b_w_q),
        "mem_w_kv": update("mem_w_kv", g_mem_w_kv, mem_w_kv, m_mem_w_kv, v_mem_w_kv),
        "w_o": update("w_o", g_w_o, w_o, m_w_o, v_w_o),
        "w_up": update("w_up", g_w_up, w_up, m_w_up, v_w_up),
        "w_down": update("w_down", g_w_down, w_down, m_w_down, v_w_down),
    }
    tiny_g = [g_sg, g_sb, g_ws, g_bs, g_bf, g_ln_g, g_ln_b]
    tiny_w = [a_sgu_ln_g, a_sgu_ln_b, a_w_s, a_b_s, kv_b_f, ln_g, ln_b]
    tiny_m = [m_a_sgu_ln_g, m_a_sgu_ln_b, m_a_w_s, m_a_b_s, m_kv_b_f, m_ln_g, m_ln_b]
    tiny_v = [v_a_sgu_ln_g, v_a_sgu_ln_b, v_a_w_s, v_a_b_s, v_kv_b_f, v_ln_g, v_ln_b]
    tiny_shapes = [a.shape for a in tiny_w]
    n_tiny = -(-_rows_of(tiny_shapes) // 8) * 8
    td, tm_, tv = _adamw(_pack_rows(tiny_g, n_tiny), _pack_rows(tiny_w, n_tiny), _pack_rows(tiny_m, n_tiny),
                         _pack_rows(tiny_v, n_tiny), "adamw_small")
    for name, d, mn, vn in zip(["a_sgu_ln_g", "a_sgu_ln_b", "a_w_s", "a_b_s", "kv_b_f", "ln_g", "ln_b"],
                               _unpack_rows(td, tiny_shapes), _unpack_rows(tm_, tiny_shapes),
                               _unpack_rows(tv, tiny_shapes)):
        upd[name] = (d, mn, vn)

    order = ["a_w_in", "a_sgu_ln_g", "a_sgu_ln_b", "a_w_s", "a_b_s", "kv_w", "kv_b_f", "b_w_q", "mem_w_kv", "w_o",
             "ln_g", "ln_b", "w_up", "w_down"]
    grads = {"a_w_in": g_a_w_in.reshape(a_w_in.shape), "a_sgu_ln_g": g_sg.reshape(a_sgu_ln_g.shape),
             "a_sgu_ln_b": g_sb.reshape(a_sgu_ln_b.shape), "a_w_s": g_ws.reshape(a_w_s.shape),
             "a_b_s": g_bs.reshape(a_b_s.shape), "kv_w": g_kv_w, "kv_b_f": g_bf, "b_w_q": g_b_w_q.reshape(b_w_q.shape),
             "mem_w_kv": g_mem_w_kv, "w_o": g_w_o, "ln_g": g_ln_g, "ln_b": g_ln_b, "w_up": g_w_up, "w_down": g_w_down}
    return (loss, grad_x[None], *[grads[n] for n in order], *[upd[n][0] for n in order],
            *[upd[n][1] for n in order], *[upd[n][2] for n in order])
```

```python
import functools
import math

import jax
import jax.numpy as jnp
from jax import lax
from jax.experimental import pallas as pl
from jax.experimental.pallas import tpu as pltpu

F32 = jnp.float32
BF16 = jnp.bfloat16

HEAD_DIM = 64
PAIR = 2 * HEAD_DIM
CHUNK = 128
LN_EPS = 1e-5
ALPHA = 4 ** 0.25
QK_SCALE = 1.0 / math.sqrt(HEAD_DIM)
NEG = -1e30
N_CHIPS = 4
FOX_BLOCK = 256
LANES = 128
VMEM_LIMIT = 48 * 2 ** 20
VMEM_LIMIT_MLP_BWD = 56 * 2 ** 20

ADAM_LR, ADAM_B1, ADAM_B2, ADAM_EPS, ADAM_WD, ADAM_STEP = 0.001, 0.9, 0.999, 1e-08, 0.01, 10

MESH = pl.DeviceIdType.MESH
ANY = pl.BlockSpec(memory_space=pl.ANY)


def _dot(a, b):
    return jnp.dot(a, b, preferred_element_type=F32)


def _dot_nt(a, b):
    return lax.dot_general(a, b, (((1,), (1,)), ((), ())), preferred_element_type=F32)


def _dot_tn(a, b):
    return lax.dot_general(a, b, (((0,), (0,)), ((), ())), preferred_element_type=F32)


def _ln_stats(r):
    mu = jnp.mean(r, axis=-1, keepdims=True)
    d = r - mu
    var = jnp.mean(d * d, axis=-1, keepdims=True)
    rstd = lax.rsqrt(var + LN_EPS)
    return d * rstd, rstd


def _ln_bwd(dy, xhat, rstd, g):
    dxh = dy * g
    m1 = jnp.mean(dxh, axis=-1, keepdims=True)
    m2 = jnp.mean(dxh * xhat, axis=-1, keepdims=True)
    return rstd * (dxh - m1 - xhat * m2)


_GELU_K = math.sqrt(2.0 / math.pi)


def _gelu(x):
    return 0.5 * x * (1.0 + jnp.tanh(_GELU_K * (x + 0.044715 * x * x * x)))


def _gelu_grad(x):
    t = jnp.tanh(_GELU_K * (x + 0.044715 * x * x * x))
    return 0.5 * (1.0 + t) + 0.5 * x * (1.0 - t * t) * _GELU_K * (1.0 + 3 * 0.044715 * x * x)


def _lane_mask(e):
    lane = lax.broadcasted_iota(jnp.int32, (1, PAIR), 1)
    return (lane >= HEAD_DIM * e) & (lane < HEAD_DIM * (e + 1))


def _first_head():
    return lax.broadcasted_iota(jnp.int32, (1, PAIR), 1) < HEAD_DIM


def _mem_probs(qh, mkp):
    sc = _dot_nt(qh, mkp) * QK_SCALE
    ex = jnp.exp(sc - jnp.max(sc, axis=-1, keepdims=True))
    return ex / jnp.sum(ex, axis=-1, keepdims=True)


def _mem_attn_fwd(qm_b, mkv_ref, dq_dim):
    outs = []
    for mp in range(dq_dim // PAIR):
        qp = qm_b[:, mp * PAIR:(mp + 1) * PAIR]
        mkp = mkv_ref[:, mp * PAIR:(mp + 1) * PAIR]
        mvp = mkv_ref[:, dq_dim + mp * PAIR:dq_dim + (mp + 1) * PAIR]
        heads = []
        for e in (0, 1):
            qh = jnp.where(_lane_mask(e), qp, jnp.zeros_like(qp))
            p = _mem_probs(qh, mkp)
            heads.append(_dot(p.astype(BF16), mvp))
        outs.append(jnp.where(_first_head(), heads[0], heads[1]))
    return outs


def _mem_attn_bwd(qm_b, mkv_ref, dmo_pairs, dmkv_ref, dq_dim):
    dqs = []
    for mp in range(dq_dim // PAIR):
        ks = slice(mp * PAIR, (mp + 1) * PAIR)
        vs = slice(dq_dim + mp * PAIR, dq_dim + (mp + 1) * PAIR)
        qp = qm_b[:, ks]
        mkp = mkv_ref[:, ks]
        mvp = mkv_ref[:, vs]
        dmo_b = dmo_pairs[mp].astype(BF16)
        dq = None
        dmk = None
        dmv = None
        for e in (0, 1):
            hm = _lane_mask(e)
            qh = jnp.where(hm, qp, jnp.zeros_like(qp))
            p = _mem_probs(qh, mkp)
            doh = jnp.where(hm, dmo_b, jnp.zeros_like(dmo_b))
            dp = _dot_nt(doh, mvp)
            ds = p * (dp - jnp.sum(dp * p, axis=-1, keepdims=True))
            dsb = (ds * QK_SCALE).astype(BF16)
            kh = jnp.where(hm, mkp, jnp.zeros_like(mkp))
            dq_e = _dot(dsb, kh)
            dmk_e = _dot_tn(dsb, qh)
            dmv_e = _dot_tn(p.astype(BF16), doh)
            dq = dq_e if dq is None else dq + dq_e
            dmk = dmk_e if dmk is None else dmk + dmk_e
            dmv = dmv_e if dmv is None else dmv + dmv_e
        dmkv_ref[:, ks] += dmk
        dmkv_ref[:, vs] += dmv
        dqs.append(dq)
    return dqs


def _params(n_axes=1):
    return pltpu.CompilerParams(dimension_semantics=("arbitrary",) * n_axes, vmem_limit_bytes=VMEM_LIMIT)


def _full(shape):
    return pl.BlockSpec(shape, lambda *_: (0,) * len(shape))


def _rows(tm, cols):
    return pl.BlockSpec((tm, cols), lambda i: (i, 0))


def _cols(rows, tm):
    return pl.BlockSpec((rows, tm), lambda i: (0, i))


def _pick(n, pref):
    if n <= pref:
        return n
    best = LANES
    for t in range(LANES, pref + 1, LANES):
        if n % t == 0:
            best = t
    return best if 2 * best >= pref or n > 2 * pref else n


class _Comm:
    def __init__(self, ins, out_shapes, aliases, sem_counts, start, finish, middle=None):
        self.ins, self.out_shapes, self.aliases, self.sem_counts = list(ins), list(out_shapes), dict(aliases), sem_counts
        self.start, self.finish = start, finish
        self.middle = middle


def _pcall(body, *, name, grid, in_specs, out_specs, out_shape, args, scratch_shapes=(), comm=None, aliases=None,
           vmem_limit=VMEM_LIMIT):
    n_in, n_out, n_scr = len(in_specs), len(out_shape), len(scratch_shapes)
    aliases = dict(aliases or {})
    if comm is None:
        res = pl.pallas_call(body, name=name, grid=grid, out_shape=tuple(out_shape), in_specs=list(in_specs),
                             out_specs=tuple(out_specs), scratch_shapes=list(scratch_shapes),
                             input_output_aliases=aliases,
                             compiler_params=pltpu.CompilerParams(dimension_semantics=("arbitrary",),
                                                                  vmem_limit_bytes=vmem_limit))(*args)
        return tuple(res), ()
    nci, nco = len(comm.ins), len(comm.out_shapes)
    last = grid[0] - 1

    def wrapped(*refs):
        ins, refs = refs[:n_in], refs[n_in:]
        cins, refs = refs[:nci], refs[nci:]
        outs, refs = refs[:n_out], refs[n_out:]
        couts, refs = refs[:nco], refs[nco:]
        scr, sems = refs[:n_scr], refs[n_scr:]

        @pl.when(pl.program_id(0) == 0)
        def _():
            comm.start(cins, couts, sems)

        if comm.middle is not None:
            @pl.when(pl.program_id(0) == (last + 1) // 2)
            def _():
                comm.middle(cins, couts, sems)

        body(*ins, *outs, *scr)

        @pl.when(pl.program_id(0) == last)
        def _():
            comm.finish(cins, couts, sems)

    res = pl.pallas_call(
        wrapped, name=name, grid=grid, out_shape=tuple(out_shape) + tuple(comm.out_shapes),
        in_specs=list(in_specs) + [ANY] * nci, out_specs=tuple(out_specs) + (ANY,) * nco,
        input_output_aliases={**aliases, **{n_in + a: n_out + b for a, b in comm.aliases.items()}},
        scratch_shapes=list(scratch_shapes) + [pltpu.SemaphoreType.DMA((k,)) for k in comm.sem_counts],
        compiler_params=pltpu.CompilerParams(dimension_semantics=("arbitrary",), vmem_limit_bytes=vmem_limit,
                                             has_side_effects=True),
    )(*args, *comm.ins)
    return tuple(res[:n_out]), tuple(res[n_out:])


def _run_comm(comm, name):
    nci, nco = len(comm.ins), len(comm.out_shapes)

    def body(*refs):
        cins, couts, sems = refs[:nci], refs[nci:nci + nco], refs[nci + nco:]
        comm.start(cins, couts, sems)
        if comm.middle is not None:
            comm.middle(cins, couts, sems)
        comm.finish(cins, couts, sems)

    return pl.pallas_call(
        body, name=name, out_shape=tuple(comm.out_shapes), in_specs=[ANY] * nci, out_specs=(ANY,) * nco,
        input_output_aliases=comm.aliases, scratch_shapes=[pltpu.SemaphoreType.DMA((k,)) for k in comm.sem_counts],
        compiler_params=pltpu.CompilerParams(has_side_effects=True),
    )(*comm.ins)


def _mm(a, b, name, a_is_transposed, out_split=1):
    (M, R) = a.shape if a_is_transposed else a.shape[::-1]
    _, N = b.shape
    bm, br = _pick(M, 1024 if a_is_transposed else 512), _pick(R, 1024 if a_is_transposed else 512)
    ncol = N // out_split
    bn = _pick(ncol, 1024)
    per = ncol // bn
    last = R // br - 1

    def body(a_ref, b_ref, o_ref, acc_ref):
        @pl.when(pl.program_id(2) == 0)
        def _():
            acc_ref[...] = jnp.zeros_like(acc_ref)

        acc_ref[...] += (_dot if a_is_transposed else _dot_tn)(a_ref[...], b_ref[...])

        @pl.when(pl.program_id(2) == last)
        def _():
            o_ref[...] = acc_ref[...].astype(BF16)

    if out_split == 1:
        out_shape = jax.ShapeDtypeStruct((M, N), BF16)
        out_spec = pl.BlockSpec((bm, bn), lambda i, j, r: (i, j))
    else:
        out_shape = jax.ShapeDtypeStruct((out_split, M, ncol), BF16)
        out_spec = pl.BlockSpec((None, bm, bn), lambda i, j, r: (j // per, i, j % per))
    a_spec = (pl.BlockSpec((bm, br), lambda i, j, r: (i, r)) if a_is_transposed
              else pl.BlockSpec((br, bm), lambda i, j, r: (r, i)))
    return pl.pallas_call(
        body, name=name, out_shape=out_shape, grid=(M // bm, N // bn, R // br),
        in_specs=[a_spec, pl.BlockSpec((br, bn), lambda i, j, r: (r, j))], out_specs=out_spec,
        scratch_shapes=[pltpu.VMEM((bm, bn), F32)],
        compiler_params=pltpu.CompilerParams(dimension_semantics=("parallel", "parallel", "arbitrary"),
                                             vmem_limit_bytes=VMEM_LIMIT),
    )(a, b)


def _mm_tn(a, b, name, out_split=1):
    return _mm(a, b, name, False, out_split)


def _mm_nn(at, b, name, out_split=1):
    return _mm(at, b, name, True, out_split)


def _mm_small(a, b, name):
    def body(a_ref, b_ref, o_ref):
        o_ref[...] = _dot(a_ref[...], b_ref[...]).astype(BF16)

    return pl.pallas_call(body, name=name, out_shape=jax.ShapeDtypeStruct((a.shape[0], b.shape[1]), BF16),
                          compiler_params=pltpu.CompilerParams(vmem_limit_bytes=VMEM_LIMIT))(a, b)


def _mlp_out(x_ref, wup_ref, wdown_ref, g_ref, b_ref, r_ref, h_ref):
    NC, _, FC = wup_ref.shape
    xv = x_ref[...]
    xb = xv.astype(BF16)
    y = jnp.zeros(xv.shape, F32)
    for c in range(NC):
        a = jnp.maximum(_dot(xb, wup_ref[c]), 0.0)
        hb = (a * a).astype(BF16)
        h_ref[:, c * FC:(c + 1) * FC] = hb
        y = y + _dot(hb, wdown_ref[c])
    r = ALPHA * xv + y
    r_ref[...] = r
    xhat, _ = _ln_stats(r)
    return xhat * g_ref[...] + b_ref[...]


def _mlp_fwd_loss(x, wup4, wdown4, g, b, target, name):
    S, D = x.shape
    NC, _, FC = wup4.shape
    tm = min(512, S)

    def body(x_ref, wup_ref, wdown_ref, g_ref, b_ref, t_ref, d_ref, r_ref, h_ref, l_ref):
        @pl.when(pl.program_id(0) == 0)
        def _():
            l_ref[...] = jnp.zeros_like(l_ref)

        err = _mlp_out(x_ref, wup_ref, wdown_ref, g_ref, b_ref, r_ref, h_ref) - t_ref[...]
        d_ref[...] = err * (1.0 / D)
        l_ref[...] += (0.5 / D) * jnp.sum(err * err)

    return pl.pallas_call(
        body, name=name, grid=(S // tm,),
        out_shape=(jax.ShapeDtypeStruct((S, D), F32), jax.ShapeDtypeStruct((S, D), F32),
                   jax.ShapeDtypeStruct((S, NC * FC), BF16), jax.ShapeDtypeStruct((8, LANES), F32)),
        in_specs=[_rows(tm, D), _full(wup4.shape), _full(wdown4.shape), _full((1, D)), _full((1, D)), _rows(tm, D)],
        out_specs=(_rows(tm, D), _rows(tm, D), _rows(tm, NC * FC), _full((8, LANES))),
        compiler_params=_params(),
    )(x, wup4, wdown4, g, b, target)


def _mlp_fwd(x, wup4, wdown4, g, b, name, comm=None):
    S, D = x.shape
    NC, _, FC = wup4.shape
    tm = min(512, S)

    def body(x_ref, wup_ref, wdown_ref, g_ref, b_ref, xo_ref, xb_ref, xbt_ref, r_ref, h_ref):
        xo = _mlp_out(x_ref, wup_ref, wdown_ref, g_ref, b_ref, r_ref, h_ref)
        xo_ref[...] = xo
        xb_ref[...] = xo.astype(BF16)
        xbt_ref[...] = xo.T.astype(BF16)

    return _pcall(
        body, name=name, grid=(S // tm,), comm=comm,
        out_shape=(jax.ShapeDtypeStruct((S, D), F32), jax.ShapeDtypeStruct((S, D), BF16),
                   jax.ShapeDtypeStruct((D, S), BF16), jax.ShapeDtypeStruct((S, D), F32),
                   jax.ShapeDtypeStruct((S, NC * FC), BF16)),
        in_specs=[_rows(tm, D), _full(wup4.shape), _full(wdown4.shape), _full((1, D)), _full((1, D))],
        out_specs=(_rows(tm, D), _rows(tm, D), _cols(D, tm), _rows(tm, D), _rows(tm, NC * FC)),
        args=(x, wup4, wdown4, g, b))


def _mlp_bwd(dxo, r, h, wup4, wdown4, g, name, comm=None):
    S, D = r.shape
    NC, _, FC = wup4.shape
    tm = min(512, S)

    def body(dxo_ref, r_ref, h_ref, wup_ref, wdown_ref, g_ref, dx_ref, drbt_ref, dpre_ref, dgb_ref):
        @pl.when(pl.program_id(0) == 0)
        def _():
            dgb_ref[...] = jnp.zeros_like(dgb_ref)

        xhat, rstd = _ln_stats(r_ref[...])
        dy = dxo_ref[...]
        dgb_ref[0:1, :] += jnp.sum(dy * xhat, axis=0, keepdims=True)
        dgb_ref[1:2, :] += jnp.sum(dy, axis=0, keepdims=True)
        dr = _ln_bwd(dy, xhat, rstd, g_ref[...])
        drb = dr.astype(BF16)
        drbt_ref[...] = dr.T.astype(BF16)
        dx = ALPHA * dr
        for c in range(NC):
            dh = _dot_nt(drb, wdown_ref[c])
            a = jnp.sqrt(h_ref[:, c * FC:(c + 1) * FC].astype(F32))
            dpre = (2.0 * a * dh).astype(BF16)
            dpre_ref[:, c * FC:(c + 1) * FC] = dpre
            dx = dx + _dot_nt(dpre, wup_ref[c])
        dx_ref[...] = dx

    return _pcall(
        body, name=name, grid=(S // tm,), comm=comm,
        out_shape=(jax.ShapeDtypeStruct((S, D), F32), jax.ShapeDtypeStruct((D, S), BF16),
                   jax.ShapeDtypeStruct((S, NC * FC), BF16), jax.ShapeDtypeStruct((2, D), F32)),
        in_specs=[_rows(tm, D), _rows(tm, D), _rows(tm, NC * FC), _full(wup4.shape), _full(wdown4.shape),
                  _full((1, D))],
        out_specs=(_rows(tm, D), _cols(D, tm), _rows(tm, NC * FC), _full((2, D))),
        args=(dxo, r, h, wup4, wdown4, g), vmem_limit=VMEM_LIMIT_MLP_BWD)


def _tail_fwd_store(xv, cat_ref, wo_ref, g_ref, b_ref, xo_ref, xbt_ref, r_ref, catt_ref):
    cat = cat_ref[...]
    y = _dot(cat, wo_ref[...])
    r = ALPHA * xv + y
    xhat, _ = _ln_stats(r)
    xo = xhat * g_ref[...] + b_ref[...]
    xo_ref[...] = xo
    xbt_ref[...] = xo.T.astype(BF16)
    catt_ref[...] = cat.astype(F32).T.astype(BF16)
    r_ref[...] = r


def _tail_bwd_head(dxo_ref, r_ref, g_ref, wo_ref, dgb_ref, drb_ref):
    xhat, rstd = _ln_stats(r_ref[...])
    dy = dxo_ref[...]
    dgb_ref[0:1, :] += jnp.sum(dy * xhat, axis=0, keepdims=True)
    dgb_ref[1:2, :] += jnp.sum(dy, axis=0, keepdims=True)
    dr = _ln_bwd(dy, xhat, rstd, g_ref[...])
    drb = dr.astype(BF16)
    drb_ref[...] = drb
    return dr, _dot_nt(drb, wo_ref[...])


def _causal(shape):
    return lax.broadcasted_iota(jnp.int32, shape, 1) <= lax.broadcasted_iota(jnp.int32, shape, 0)


def _sgu_mixed(ws_ref, bst_ref, gp, vpair):
    tril = _causal((CHUNK, CHUNK))
    w0 = jnp.where(tril, ws_ref[2 * gp], 0.0).astype(BF16)
    w1 = jnp.where(tril, ws_ref[2 * gp + 1], 0.0).astype(BF16)
    m0 = _dot(w0, vpair) + bst_ref[:, 2 * gp:2 * gp + 1]
    m1 = _dot(w1, vpair) + bst_ref[:, 2 * gp + 1:2 * gp + 2]
    return jnp.where(_first_head(), m0, m1), w0, w1


def _mixer_a_fwd(x, wa, sg, sb, ws, bst, mkv, wo, g, b, name, comm=None):
    S, D = x.shape
    DQ = mkv.shape[1] // 2
    DM = D - DQ
    NP = DM // PAIR
    tm = min(512, S)

    def body(x_ref, wa_ref, sg_ref, sb_ref, ws_ref, bst_ref, mkv_ref, wo_ref, g_ref, b_ref,
             xo_ref, xbt_ref, r_ref, catt_ref, xint_ref, zb_ref, z_s, zv_s, cat_ref):
        xv = x_ref[...]
        xint_ref[...] = xv.T.astype(BF16)
        z_s[...] = _dot(xv.astype(BF16), wa_ref[...])
        zb_ref[...] = z_s[...].astype(BF16)
        vhat, _ = _ln_stats(_gelu(z_s[:, DM:2 * DM]))
        zv_s[...] = (vhat * sg_ref[...] + sb_ref[...]).astype(BF16)
        for c in range(tm // CHUNK):
            rs = slice(c * CHUNK, (c + 1) * CHUNK)
            for gp in range(NP):
                ls = slice(gp * PAIR, (gp + 1) * PAIR)
                mixed, _, _ = _sgu_mixed(ws_ref, bst_ref, gp, zv_s[rs, ls])
                cat_ref[rs, ls] = (_gelu(z_s[rs, ls]) * mixed).astype(BF16)
        mo = _mem_attn_fwd(z_s[:, 2 * DM:].astype(BF16), mkv_ref, DQ)
        for mp in range(DQ // PAIR):
            cat_ref[:, DM + mp * PAIR:DM + (mp + 1) * PAIR] = mo[mp].astype(BF16)
        _tail_fwd_store(xv, cat_ref, wo_ref, g_ref, b_ref, xo_ref, xbt_ref, r_ref, catt_ref)

    return _pcall(
        body, name=name, grid=(S // tm,), comm=comm,
        out_shape=(jax.ShapeDtypeStruct((S, D), F32), jax.ShapeDtypeStruct((D, S), BF16),
                   jax.ShapeDtypeStruct((S, D), F32), jax.ShapeDtypeStruct((D, S), BF16),
                   jax.ShapeDtypeStruct((D, S), BF16), jax.ShapeDtypeStruct((S, 2 * DM + DQ), BF16)),
        in_specs=[_rows(tm, D), _full(wa.shape), _full(sg.shape), _full(sb.shape), _full(ws.shape), _full(bst.shape),
                  _full(mkv.shape), _full(wo.shape), _full((1, D)), _full((1, D))],
        out_specs=(_rows(tm, D), _cols(D, tm), _rows(tm, D), _cols(D, tm), _cols(D, tm), _rows(tm, 2 * DM + DQ)),
        scratch_shapes=[pltpu.VMEM((tm, 2 * DM + DQ), F32), pltpu.VMEM((tm, DM), BF16), pltpu.VMEM((tm, D), BF16)],
        args=(x, wa, sg, sb, ws, bst, mkv, wo, g, b))


def _mixer_a_bwd(zb, dxo, r, wa, sg, sb, ws, bst, mkv, wo, g, name, comm=None):
    S, D = r.shape
    DQ = mkv.shape[1] // 2
    DM = D - DQ
    NP = DM // PAIR
    NG = DM // HEAD_DIM
    tm = min(512, S)

    def body(zb_ref, dxo_ref, r_ref, wa_ref, sg_ref, sb_ref, ws_ref, bst_ref, mkv_ref, wo_ref, g_ref,
             dx_ref, dz_ref, drb_ref, dgb_ref, dsgb_ref, dws_ref, dbst_ref, dmkv_ref, z_s, zv_s, dzv_s):
        @pl.when(pl.program_id(0) == 0)
        def _():
            dgb_ref[...] = jnp.zeros_like(dgb_ref)
            dsgb_ref[...] = jnp.zeros_like(dsgb_ref)
            dws_ref[...] = jnp.zeros_like(dws_ref)
            dbst_ref[...] = jnp.zeros_like(dbst_ref)
            dmkv_ref[...] = jnp.zeros_like(dmkv_ref)

        dr, dcat = _tail_bwd_head(dxo_ref, r_ref, g_ref, wo_ref, dgb_ref, drb_ref)
        z_s[...] = zb_ref[...].astype(F32)
        vhat, vrstd = _ln_stats(_gelu(z_s[:, DM:2 * DM]))
        zv_s[...] = (vhat * sg_ref[...] + sb_ref[...]).astype(BF16)
        tril = _causal((CHUNK, CHUNK))
        lane = lax.broadcasted_iota(jnp.int32, (1, LANES), 1)
        for c in range(tm // CHUNK):
            rs = slice(c * CHUNK, (c + 1) * CHUNK)
            for gp in range(NP):
                ls = slice(gp * PAIR, (gp + 1) * PAIR)
                vpair = zv_s[rs, ls]
                mixed, w0, w1 = _sgu_mixed(ws_ref, bst_ref, gp, vpair)
                u_pre = z_s[rs, ls]
                dmix = dcat[rs, ls]
                dz_ref[rs, ls] = (dmix * mixed * _gelu_grad(u_pre)).astype(BF16)
                dmixed = dmix * _gelu(u_pre)
                first = _first_head()
                d0 = jnp.where(first, dmixed, 0.0)
                d1 = jnp.where(first, 0.0, dmixed)
                d0b = d0.astype(BF16)
                d1b = d1.astype(BF16)
                dzv_s[rs, ls] = _dot_tn(w0, d0b) + _dot_tn(w1, d1b)
                dws_ref[2 * gp] += jnp.where(tril, _dot_nt(d0b, vpair), 0.0)
                dws_ref[2 * gp + 1] += jnp.where(tril, _dot_nt(d1b, vpair), 0.0)
                dbst_ref[...] += (jnp.where(lane == 2 * gp, jnp.sum(d0, axis=-1, keepdims=True), 0.0)
                                  + jnp.where(lane == 2 * gp + 1, jnp.sum(d1, axis=-1, keepdims=True), 0.0))
        dzv = dzv_s[...]
        dsgb_ref[0:1, :] += jnp.sum(dzv * vhat, axis=0, keepdims=True)
        dsgb_ref[1:2, :] += jnp.sum(dzv, axis=0, keepdims=True)
        dgv = _ln_bwd(dzv, vhat, vrstd, sg_ref[...])
        dz_ref[:, DM:2 * DM] = (dgv * _gelu_grad(z_s[:, DM:2 * DM])).astype(BF16)
        dmo = [dcat[:, DM + mp * PAIR:DM + (mp + 1) * PAIR] for mp in range(DQ // PAIR)]
        dqm = _mem_attn_bwd(z_s[:, 2 * DM:].astype(BF16), mkv_ref, dmo, dmkv_ref, DQ)
        for mp in range(DQ // PAIR):
            dz_ref[:, 2 * DM + mp * PAIR:2 * DM + (mp + 1) * PAIR] = dqm[mp].astype(BF16)
        dx_ref[...] = ALPHA * dr + _dot_nt(dz_ref[...], wa_ref[...])

    ZW = 2 * DM + DQ
    M = mkv.shape[0]
    return _pcall(
        body, name=name, grid=(S // tm,), comm=comm,
        out_shape=(jax.ShapeDtypeStruct((S, D), F32), jax.ShapeDtypeStruct((S, ZW), BF16),
                   jax.ShapeDtypeStruct((S, D), BF16), jax.ShapeDtypeStruct((2, D), F32),
                   jax.ShapeDtypeStruct((2, DM), F32), jax.ShapeDtypeStruct((NG, CHUNK, CHUNK), F32),
                   jax.ShapeDtypeStruct((CHUNK, LANES), F32), jax.ShapeDtypeStruct((M, 2 * DQ), F32)),
        in_specs=[_rows(tm, ZW), _rows(tm, D), _rows(tm, D), _full(wa.shape), _full(sg.shape), _full(sb.shape),
                  _full(ws.shape), _full(bst.shape), _full(mkv.shape), _full(wo.shape), _full((1, D))],
        out_specs=(_rows(tm, D), _rows(tm, ZW), _rows(tm, D), _full((2, D)), _full((2, DM)),
                   _full((NG, CHUNK, CHUNK)), _full((CHUNK, LANES)), _full((M, 2 * DQ))),
        scratch_shapes=[pltpu.VMEM((tm, ZW), F32), pltpu.VMEM((tm, DM), BF16), pltpu.VMEM((tm, DM), F32)],
        args=(zb, dxo, r, wa, sg, sb, ws, bst, mkv, wo, g))


def _proj_b_fwd(xb, wq, wkv, dm, name):
    S, D = xb.shape
    DQ = D - dm
    tm = min(512, S)

    def body(x_ref, wq_ref, wkv_ref, q_ref, qm_ref, k_ref, v_ref, fl_ref):
        xv = x_ref[...]
        z = _dot(xv, wq_ref[...])
        q_ref[...] = z[:, :dm].astype(BF16)
        qm_ref[...] = z[:, dm:].astype(BF16)
        kvf = _dot(xv, wkv_ref[...])
        k_ref[...] = kvf[:, :dm].astype(BF16)
        v_ref[...] = kvf[:, dm:2 * dm].astype(BF16)
        fl_ref[...] = kvf[:, 2 * dm:]

    return pl.pallas_call(
        body, name=name, grid=(S // tm,),
        out_shape=(jax.ShapeDtypeStruct((S, dm), BF16), jax.ShapeDtypeStruct((S, DQ), BF16),
                   jax.ShapeDtypeStruct((S, dm), BF16), jax.ShapeDtypeStruct((S, dm), BF16),
                   jax.ShapeDtypeStruct((S, LANES), F32)),
        in_specs=[_rows(tm, D), _full(wq.shape), _full(wkv.shape)],
        out_specs=(_rows(tm, dm), _rows(tm, DQ), _rows(tm, dm), _rows(tm, dm), _rows(tm, LANES)),
        compiler_params=_params(),
    )(xb, wq, wkv)


def _proj_b_bwd(dr, dqz, dk, dv, dfl, wq, wkv, name, comm=None):
    S, D = dr.shape
    dm = dk.shape[1]
    tm = min(512, S)

    def body(dr_ref, dqz_ref, dk_ref, dv_ref, dfl_ref, wq_ref, wkv_ref, dx_ref):
        dx_ref[...] = (ALPHA * dr_ref[...] + _dot_nt(dqz_ref[...], wq_ref[...])
                       + _dot_nt(dk_ref[...], wkv_ref[:, :dm]) + _dot_nt(dv_ref[...], wkv_ref[:, dm:2 * dm])
                       + _dot_nt(dfl_ref[...], wkv_ref[:, 2 * dm:]))

    return _pcall(
        body, name=name, grid=(S // tm,), comm=comm, out_shape=(jax.ShapeDtypeStruct((S, D), F32),),
        in_specs=[_rows(tm, D), _rows(tm, dqz.shape[1]), _rows(tm, dm), _rows(tm, dm), _rows(tm, dfl.shape[1]),
                  _full(wq.shape), _full(wkv.shape)],
        out_specs=(_rows(tm, D),), args=(dr, dqz, dk, dv, dfl, wq, wkv))


def _mm_nn_cat(at, bs, name):
    M, R = at.shape
    widths = [b.shape[1] for b in bs]
    N = sum(widths)
    br = _pick(R, 1024)
    last = R // br - 1
    nb = len(bs)

    def body(a_ref, *refs):
        b_refs, o_ref, acc_ref = refs[:nb], refs[nb], refs[nb + 1]

        @pl.when(pl.program_id(0) == 0)
        def _():
            acc_ref[...] = jnp.zeros_like(acc_ref)

        a = a_ref[...]
        c0 = 0
        for b_ref, wd in zip(b_refs, widths):
            acc_ref[:, c0:c0 + wd] += _dot(a, b_ref[...])
            c0 += wd

        @pl.when(pl.program_id(0) == last)
        def _():
            o_ref[...] = acc_ref[...].astype(BF16)

    return pl.pallas_call(
        body, name=name, grid=(R // br,), out_shape=jax.ShapeDtypeStruct((M, N), BF16),
        in_specs=[pl.BlockSpec((M, br), lambda r: (0, r))] + [pl.BlockSpec((br, wd), lambda r: (r, 0)) for wd in widths],
        out_specs=_full((M, N)), scratch_shapes=[pltpu.VMEM((M, N), F32)], compiler_params=_params(),
    )(at, *bs)


def _split3(v):
    hi = v.astype(BF16)
    r1 = v - hi.astype(F32)
    mid = r1.astype(BF16)
    lo = (r1 - mid.astype(F32)).astype(BF16)
    return hi, mid, lo


def _tri_sum(v, tri_b):
    hi, mid, lo = _split3(v)
    return _dot(hi, tri_b) + _dot(mid, tri_b) + _dot(lo, tri_b)


def _log_sigmoid(x):
    return jnp.minimum(x, 0.0) - jnp.log(1.0 + jnp.exp(-jnp.abs(x)))


def _forget_cumsum(flt, bf, name):
    H, S = flt.shape
    nchunk = S // LANES

    def body(fl_ref, bf_ref, p_ref):
        upper = (lax.broadcasted_iota(jnp.int32, (LANES, LANES), 0)
                 <= lax.broadcasted_iota(jnp.int32, (LANES, LANES), 1)).astype(BF16)

        def step(n, carry):
            s0 = pl.multiple_of(n * LANES, LANES)
            lf = _log_sigmoid(fl_ref[:, pl.ds(s0, LANES)] + bf_ref[...])
            cs = _tri_sum(lf, upper) + carry
            for i, piece in enumerate(_split3(-cs)):
                p_ref[i, :, pl.ds(s0, LANES)] = piece.astype(F32)
            return cs[:, LANES - 1:LANES]

        lax.fori_loop(0, nchunk, step, jnp.zeros((H, 1), F32))

    return pl.pallas_call(body, name=name, out_shape=jax.ShapeDtypeStruct((3, H, S), F32),
                          compiler_params=pltpu.CompilerParams(vmem_limit_bytes=VMEM_LIMIT))(flt, bf)


def _forget_cumsum_bwd(dct, flt, bf, name):
    H, S = flt.shape
    nchunk = S // LANES

    def body(dc_ref, fl_ref, bf_ref, dfl_ref, dbf_ref):
        lower = (lax.broadcasted_iota(jnp.int32, (LANES, LANES), 0)
                 >= lax.broadcasted_iota(jnp.int32, (LANES, LANES), 1)).astype(BF16)

        def step(n, carry):
            tail, tot = carry
            s0 = pl.multiple_of((nchunk - 1 - n) * LANES, LANES)
            suffix = _tri_sum(dc_ref[:, pl.ds(s0, LANES)], lower) + tail
            xv = fl_ref[:, pl.ds(s0, LANES)] + bf_ref[...]
            dfl = suffix * (1.0 / (1.0 + jnp.exp(xv)))
            dfl_ref[:, pl.ds(s0, LANES)] = dfl
            return suffix[:, 0:1], tot + jnp.sum(dfl, axis=-1, keepdims=True)

        _, tot = lax.fori_loop(0, nchunk, step, (jnp.zeros((H, 1), F32), jnp.zeros((H, 1), F32)))
        dbf_ref[...] = jnp.broadcast_to(tot, (H, LANES))

    return pl.pallas_call(body, name=name,
                          out_shape=(jax.ShapeDtypeStruct((H, S), F32), jax.ShapeDtypeStruct((H, LANES), F32)),
                          compiler_params=pltpu.CompilerParams(vmem_limit_bytes=VMEM_LIMIT))(dct, flt, bf)


N_BIAS = 3


def _bias_ones():
    lane = lax.broadcasted_iota(jnp.int32, (1, PAIR), 1)
    return ((lane & (HEAD_DIM - 1)) < N_BIAS).astype(BF16)


def _fox_fwd(q, k, v, ca, name, comm=None):
    S, DM = q.shape
    NP = DM // PAIR
    T = min(FOX_BLOCK, S)
    nq = S // T

    def body(q_ref, k_ref, v_ref, ca_ref, o_ref, lse_ref):
        tril = _causal((T, T))
        heads = (_lane_mask(0), _lane_mask(1))
        ones3 = _bias_ones()

        def kv_block(kj):
            s0 = pl.multiple_of(kj * T, T)
            kb, vb, cab = k_ref[pl.ds(s0, T), :], v_ref[pl.ds(s0, T), :], ca_ref[pl.ds(s0, T), :]
            return ([jnp.where(heads[e], kb, cab) for e in (0, 1)],
                    [jnp.where(heads[e], vb, jnp.ones_like(vb)) for e in (0, 1)])

        def q_pair(a, _):
            t0s = [pl.multiple_of((2 * a + r) * T, T) for r in (0, 1)]
            qaug = {}
            for r in (0, 1):
                qb = q_ref[pl.ds(t0s[r], T), :] * QK_SCALE
                for e in (0, 1):
                    qaug[r, e] = jnp.where(heads[e], qb, ones3)
            chains = [(r, e) for r in (0, 1) for e in (0, 1)]

            def update(carries, first_block, seen):
                blocks = [kv_block(first_block + kk) for kk in range(1 + max(kk for s in seen for kk, _ in s))]
                out = []
                scores = [[_dot_nt(qaug[r, e], blocks[kk][0][e]) for kk, _ in seen[r]] for r, e in chains]
                for n, (r, e) in enumerate(chains):
                    m, acc = carries[n]
                    tiles = [jnp.where(tril, s, NEG) if masked else s for s, (_, masked) in zip(scores[n], seen[r])]
                    m_new = jnp.maximum(m, jnp.max(functools.reduce(jnp.maximum, tiles), axis=-1, keepdims=True))
                    acc = jnp.exp(m - m_new) * acc
                    for s, (kk, _) in zip(tiles, seen[r]):
                        acc = acc + _dot(jnp.exp(s - m_new).astype(BF16), blocks[kk][1][e])
                    out.append((m_new, acc))
                return tuple(out)

            def full(first_block, n_blocks, carries):
                return update(carries, first_block, [[(kk, False) for kk in range(n_blocks)]] * 2)

            init = (jnp.full((T, 1), NEG, F32), jnp.zeros((T, PAIR), F32))
            carries = lax.fori_loop(0, a // 2, lambda i, c: full(4 * i, 4, c), (init,) * 4)
            carries = lax.fori_loop(0, a % 2, lambda i, c: full(2 * a - 2, 2, c), carries)
            carries = update(carries, 2 * a, [[(0, True)], [(0, False), (1, True)]])
            first = _first_head()
            for r in (0, 1):
                (m0, acc0), (m1, acc1) = carries[2 * r], carries[2 * r + 1]
                l0, l1 = acc0[:, HEAD_DIM:HEAD_DIM + 1], acc1[:, 0:1]
                o_ref[pl.ds(t0s[r], T), :] = jnp.where(first, acc0 / l0, acc1 / l1).astype(BF16)
                lse_t = jnp.where(first, m0 + jnp.log(l0), m1 + jnp.log(l1)).T
                lse_ref[0:1, pl.ds(t0s[r], T)] = lse_t[0:1, :]
                lse_ref[1:2, pl.ds(t0s[r], T)] = lse_t[HEAD_DIM:HEAD_DIM + 1, :]
            return 0

        lax.fori_loop(0, nq // 2, q_pair, 0)

    col = pl.BlockSpec((S, PAIR), lambda hp: (0, hp))
    return _pcall(
        body, name=name, grid=(NP,), comm=comm,
        out_shape=(jax.ShapeDtypeStruct((S, DM), BF16), jax.ShapeDtypeStruct((NP, 2, S), F32)),
        in_specs=[col, col, col, col], out_specs=(col, pl.BlockSpec((None, 2, S), lambda hp: (hp, 0, 0))),
        args=(q, k, v, ca))


def _fox_bwd(q, k, v, o, do, ca, lse, name, comm=None):
    S, DM = q.shape
    NP = DM // PAIR
    T = min(FOX_BLOCK, S)
    nq = S // T

    def body(q_ref, k_ref, v_ref, o_ref, do_ref, ca_ref, lse_ref, dq_ref, dk_ref, dv_ref, dc_ref,
             dq_acc, dcol_acc, dk_acc, dv_acc, dd_s):
        keep = lax.broadcasted_iota(jnp.int32, (T, T), 0) <= lax.broadcasted_iota(jnp.int32, (T, T), 1)
        lane = lax.broadcasted_iota(jnp.int32, (1, LANES), 1)
        heads = (_lane_mask(0), _lane_mask(1))
        ones3 = _bias_ones()
        dq_acc[...] = jnp.zeros_like(dq_acc)
        dcol_acc[...] = jnp.zeros_like(dcol_acc)
        dc_ref[...] = jnp.zeros_like(dc_ref)

        def row_dots(qi, _):
            t0 = pl.multiple_of(qi * T, T)
            prod = do_ref[pl.ds(t0, T), :].astype(F32) * o_ref[pl.ds(t0, T), :].astype(F32)
            prod_t = prod.T
            dd_s[0:1, pl.ds(t0, T)] = jnp.sum(prod_t[0:HEAD_DIM, :], axis=0, keepdims=True)
            dd_s[1:2, pl.ds(t0, T)] = jnp.sum(prod_t[HEAD_DIM:PAIR, :], axis=0, keepdims=True)
            return 0

        lax.fori_loop(0, nq, row_dots, 0)

        def kv_pair(b, _):
            dk_acc[...] = jnp.zeros_like(dk_acc)
            dv_acc[...] = jnp.zeros_like(dv_acc)
            s0s = [pl.multiple_of((2 * b + jj) * T, T) for jj in (0, 1)]
            kaug, ks, vbs = {}, {}, []
            for jj in (0, 1):
                kb = k_ref[pl.ds(s0s[jj], T), :]
                cab = ca_ref[pl.ds(s0s[jj], T), :]
                vbs.append(v_ref[pl.ds(s0s[jj], T), :])
                for e in (0, 1):
                    kaug[jj, e] = jnp.where(heads[e], kb, cab)
                    ks[jj, e] = jnp.where(heads[e], kb, jnp.zeros_like(kb)) * QK_SCALE

            def q_step(qi, blocks):
                t0 = pl.multiple_of(qi * T, T)
                qs = q_ref[pl.ds(t0, T), :] * QK_SCALE
                dob = do_ref[pl.ds(t0, T), :]
                doh = [jnp.where(heads[e], dob, jnp.zeros_like(dob)) for e in (0, 1)]
                qh = [jnp.where(heads[e], qs, jnp.zeros_like(qs)) for e in (0, 1)]
                chains = [(e, jj, masked) for e in (0, 1) for jj, masked in blocks]
                scores = [_dot_nt(kaug[jj, e], jnp.where(heads[e], qs, ones3)) for e, jj, _ in chains]
                dps = [_dot_nt(vbs[jj], doh[e]) for e, jj, _ in chains]
                dq = jnp.zeros((T, PAIR), F32)
                drow = [jnp.zeros((1, T), F32), jnp.zeros((1, T), F32)]
                for n, (e, jj, masked) in enumerate(chains):
                    st = jnp.where(keep, scores[n], NEG) if masked else scores[n]
                    pt = jnp.exp(st - lse_ref[e:e + 1, pl.ds(t0, T)])
                    dv_acc[jj] += _dot(pt.astype(BF16), doh[e])
                    dst = pt * (dps[n] - dd_s[e:e + 1, pl.ds(t0, T)])
                    dsb = dst.astype(BF16)
                    dk_acc[jj] += _dot(dsb, qh[e])
                    dq = dq + _dot_tn(dsb, ks[jj, e])
                    dcol_acc[pl.ds(s0s[jj], T), :] -= jnp.where(lane == e, jnp.sum(dst, axis=-1, keepdims=True), 0.0)
                    drow[e] = drow[e] + jnp.sum(dst, axis=0, keepdims=True)
                for e in (0, 1):
                    dc_ref[e:e + 1, pl.ds(t0, T)] += drow[e]
                dq_acc[pl.ds(t0, T), :] += dq

            q_step(2 * b, [(0, True)])
            q_step(2 * b + 1, [(0, False), (1, True)])

            def rest(i, _):
                q_step(2 * b + 2 + 2 * i, [(0, False), (1, False)])
                q_step(2 * b + 3 + 2 * i, [(0, False), (1, False)])
                return 0

            lax.fori_loop(0, (nq - 2) // 2 - b, rest, 0)
            for jj in (0, 1):
                dk_ref[pl.ds(s0s[jj], T), :] = dk_acc[jj].astype(BF16)
                dv_ref[pl.ds(s0s[jj], T), :] = dv_acc[jj].astype(BF16)
            return 0

        lax.fori_loop(0, nq // 2, kv_pair, 0)
        dq_ref[...] = dq_acc[...].astype(BF16)

        def add_cols(kj, _):
            s0 = pl.multiple_of(kj * T, T)
            dc_ref[:, pl.ds(s0, T)] += dcol_acc[pl.ds(s0, T), :].T[0:2, :]
            return 0

        lax.fori_loop(0, nq, add_cols, 0)

    col = pl.BlockSpec((S, PAIR), lambda hp: (0, hp))
    row2 = pl.BlockSpec((None, 2, S), lambda hp: (hp, 0, 0))
    return _pcall(
        body, name=name, grid=(NP,), comm=comm,
        out_shape=(jax.ShapeDtypeStruct(do.shape, BF16), jax.ShapeDtypeStruct((S, DM), BF16),
                   jax.ShapeDtypeStruct((S, DM), BF16), jax.ShapeDtypeStruct((NP, 2, S), F32)),
        in_specs=[col, col, col, col, col, col, row2], out_specs=(col, col, col, row2),
        scratch_shapes=[pltpu.VMEM((S, PAIR), F32), pltpu.VMEM((S, LANES), F32), pltpu.VMEM((2, T, PAIR), F32),
                        pltpu.VMEM((2, T, PAIR), F32), pltpu.VMEM((8, S), F32)],
        args=(q, k, v, o, do, ca, lse), aliases={4: 0})


def _mixer_b_fwd(x, o, qm, mkv, wo, g, b, name):
    S, D = x.shape
    DQ = qm.shape[1]
    DM = D - DQ
    tm = min(512, S)

    def body(x_ref, o_ref, qm_ref, mkv_ref, wo_ref, g_ref, b_ref, xo_ref, xbt_ref, r_ref, catt_ref, cat_ref):
        cat_ref[:, :DM] = o_ref[...]
        mo = _mem_attn_fwd(qm_ref[...], mkv_ref, DQ)
        for mp in range(DQ // PAIR):
            cat_ref[:, DM + mp * PAIR:DM + (mp + 1) * PAIR] = mo[mp].astype(BF16)
        _tail_fwd_store(x_ref[...], cat_ref, wo_ref, g_ref, b_ref, xo_ref, xbt_ref, r_ref, catt_ref)

    return pl.pallas_call(
        body, name=name, grid=(S // tm,),
        out_shape=(jax.ShapeDtypeStruct((S, D), F32), jax.ShapeDtypeStruct((D, S), BF16),
                   jax.ShapeDtypeStruct((S, D), F32), jax.ShapeDtypeStruct((D, S), BF16)),
        in_specs=[_rows(tm, D), _rows(tm, DM), _rows(tm, DQ), _full(mkv.shape), _full(wo.shape), _full((1, D)),
                  _full((1, D))],
        out_specs=(_rows(tm, D), _cols(D, tm), _rows(tm, D), _cols(D, tm)),
        scratch_shapes=[pltpu.VMEM((tm, D), BF16)],
        compiler_params=_params(),
    )(x, o, qm, mkv, wo, g, b)


def _mixer_b_bwd(dxo, r, qm, mkv, wo, g, name, comm=None):
    S, D = r.shape
    DQ = qm.shape[1]
    DM = D - DQ
    M = mkv.shape[0]
    tm = min(512, S)

    def body(dxo_ref, r_ref, qm_ref, mkv_ref, wo_ref, g_ref, dr_ref, drb_ref, dcat_ref, dgb_ref, dmkv_ref):
        @pl.when(pl.program_id(0) == 0)
        def _():
            dgb_ref[...] = jnp.zeros_like(dgb_ref)
            dmkv_ref[...] = jnp.zeros_like(dmkv_ref)

        dr, dcat = _tail_bwd_head(dxo_ref, r_ref, g_ref, wo_ref, dgb_ref, drb_ref)
        dr_ref[...] = dr
        dcat_ref[:, :DM] = dcat[:, :DM].astype(BF16)
        dmo = [dcat[:, DM + mp * PAIR:DM + (mp + 1) * PAIR] for mp in range(DQ // PAIR)]
        dqm = _mem_attn_bwd(qm_ref[...], mkv_ref, dmo, dmkv_ref, DQ)
        for mp in range(DQ // PAIR):
            dcat_ref[:, DM + mp * PAIR:DM + (mp + 1) * PAIR] = dqm[mp].astype(BF16)

    return _pcall(
        body, name=name, grid=(S // tm,), comm=comm,
        out_shape=(jax.ShapeDtypeStruct((S, D), F32), jax.ShapeDtypeStruct((S, D), BF16),
                   jax.ShapeDtypeStruct((S, D), BF16), jax.ShapeDtypeStruct((2, D), F32),
                   jax.ShapeDtypeStruct((M, 2 * DQ), F32)),
        in_specs=[_rows(tm, D), _rows(tm, D), _rows(tm, DQ), _full(mkv.shape), _full(wo.shape), _full((1, D))],
        out_specs=(_rows(tm, D), _rows(tm, D), _rows(tm, D), _full((2, D)), _full((M, 2 * DQ))),
        args=(dxo, r, qm, mkv, wo, g))


class _NoExchange:
    def plan(self, stage, w, grads):
        return None

    def done(self, stage, results, w):
        pass


def _local_step(x, mem, target, w, hooks):
    S, D = x.shape
    DQ = w["mkv0"].shape[1] // 2
    DM = D - DQ
    NG = DM // HEAD_DIM
    NP = DM // PAIR
    ln_g, ln_b = w["ln_g"], w["ln_b"]
    row = lambda a, i: a[i:i + 1]
    memb = mem.astype(BF16)
    bst =jnp.pad(w["b_s"].T, ((0, 0), (0, LANES - NG)))
    grads = {}

    def staged(stage, fn, *args):
        out, got = fn(*args, stage, comm=hooks.plan(stage, w, grads))
        hooks.done(stage, got, w)
        return out

    mkv0 = _mm_small(memb, w["mkv0"], "mem_kv_0")
    x1, x1t, r1, cat0t, x0t, z0b = staged("mixer_a_fwd", _mixer_a_fwd, x, w["wa"], w["sg"], w["sb"], w["ws"], bst, mkv0,
                                w["wo0"], row(ln_g, 0), row(ln_b, 0))
    x2, x2b, x2t, r2, h0 = staged("mlp_fwd_0", _mlp_fwd, x1, w["wup0"], w["wdown0"], row(ln_g, 1), row(ln_b, 1))
    q, qm, k, v, fl = _proj_b_fwd(x2b, w["wq"], w["wkv"], DM, "proj_b_fwd")
    flt = fl[:, :16].T
    bfc = jnp.pad(w["bf"], (0, 16 - NG)).reshape(16, 1)
    pieces = _forget_cumsum(flt, bfc, "forget_cumsum")
    pieces = jnp.pad(pieces[:, :NG].transpose(2, 1, 0), ((0, 0), (0, 0), (0, HEAD_DIM - N_BIAS)))
    ca = pieces.reshape(S, NP, 2, HEAD_DIM)[:, :, ::-1].reshape(S, DM).astype(BF16)
    o, lse = staged("fox_fwd", _fox_fwd, q, k, v, ca)
    mkv1 = _mm_small(memb, w["mkv1"], "mem_kv_1")
    x3, x3t, r3, cat1t = _mixer_b_fwd(x2, o, qm, mkv1, w["wo1"], row(ln_g, 2), row(ln_b, 2), "mixer_b_fwd")
    dx4, r4, h1, loss_part = _mlp_fwd_loss(x3, w["wup1"], w["wdown1"], row(ln_g, 3), row(ln_b, 3), target,
                                           "mlp_fwd_1")

    nc = w["wup0"].shape[0]
    (dx3, dr4t, dpre1, dgb11), _ = _mlp_bwd(dx4, r4, h1, w["wup1"], w["wdown1"], row(ln_g, 3), "mlp_bwd_1")
    grads["wdown1"] = _mm_nn(dr4t, h1, "dw_down_1", out_split=nc)
    grads["wup1"] = _mm_nn(x3t, dpre1, "dw_up_1", out_split=nc)
    dr3, dr3b, dcat1, dgb10, dmkv1 = staged("mixer_b_bwd", _mixer_b_bwd, dx3, r3, qm, mkv1, w["wo1"], row(ln_g, 2))
    dqz, dk, dv, dct3 = staged("fox_bwd", _fox_bwd, q, k, v, o, dcat1, ca, lse)
    dct = jnp.pad(dct3.reshape(NG, S), ((0, 16 - NG), (0, 0)))
    dflt, dbf = _forget_cumsum_bwd(dct, flt, bfc, "forget_cumsum_bwd")
    dfl = jnp.pad(dflt.T, ((0, 0), (0, LANES - 16))).astype(BF16)
    grads["wo1"] = _mm_nn(cat1t, dr3b, "dw_o_1")
    grads["wq"] = _mm_nn(x2t, dqz, "dw_q")
    grads["wkv"] = _mm_nn_cat(x2t, [dk, dv, dfl], "dw_kv")
    grads["mkv1"] = _mm_tn(memb, dmkv1.astype(BF16), "dw_mkv_1")
    (dx2,) = staged("proj_b_bwd", _proj_b_bwd, dr3, dqz, dk, dv, dfl, w["wq"], w["wkv"])
    dx1, dr2t, dpre0, dgb01 = staged("mlp_bwd_0", _mlp_bwd, dx2, r2, h0, w["wup0"], w["wdown0"], row(ln_g, 1))
    grads["wdown0"] = _mm_nn(dr2t, h0, "dw_down_0", out_split=nc)
    grads["wup0"] = _mm_nn(x1t, dpre0, "dw_up_0", out_split=nc)
    dx0, dz, dr1b, dgb00, dsgb, dws, dbst, dmkv0 = staged(
        "mixer_a_bwd", _mixer_a_bwd, z0b, dx1, r1, w["wa"], w["sg"], w["sb"], w["ws"], bst, mkv0, w["wo0"], row(ln_g, 0))
    grads["wo0"] = _mm_nn(cat0t, dr1b, "dw_o_0")
    grads["wa"] = _mm_nn(x0t, dz, "dw_a")
    grads["mkv0"] = _mm_tn(memb, dmkv0.astype(BF16), "dw_mkv_0")
    grads.update({
        "ws": dws, "b_s": dbst[:, :NG].T, "sg": dsgb[0:1], "sb": dsgb[1:2], "bf": dbf[:NG, 0], "loss": loss_part[0:1],
        "ln_g": jnp.concatenate([dgb00[0:1], dgb01[0:1], dgb10[0:1], dgb11[0:1]], axis=0),
        "ln_b": jnp.concatenate([dgb00[1:2], dgb01[1:2], dgb10[1:2], dgb11[1:2]], axis=0),
    })
    return loss_part, dx0, grads


def _place():
    xi, yi, ci = lax.axis_index("x"), lax.axis_index("y"), lax.axis_index("c")
    peers = [(xi, 1 - yi), (1 - xi, yi), (1 - xi, 1 - yi)]
    return xi, yi, ci, peers


def _comm_call(body, name, ins, out_shapes, n_remote, n_local):
    return pl.pallas_call(
        body, name=name, out_shape=out_shapes, in_specs=[ANY] * len(ins), out_specs=[ANY] * len(out_shapes),
        scratch_shapes=[pltpu.SemaphoreType.DMA((n_remote,)), pltpu.SemaphoreType.DMA((n_remote,)),
                        pltpu.SemaphoreType.DMA((max(n_local, 1),))],
        compiler_params=pltpu.CompilerParams(has_side_effects=True),
    )(*ins)


def _pair_swap(xs):
    n = len(xs)
    out_shapes = [jax.ShapeDtypeStruct((a.shape[0],) + a.shape[2:], a.dtype) for a in xs]

    def copies(x_refs, o_refs, sems):
        send_sems, recv_sems = sems
        xi, yi, ci, _ = _place()
        return [pltpu.make_async_remote_copy(src_ref=x_refs[i].at[:, 1 - ci], dst_ref=o_refs[i],
                                             send_sem=send_sems.at[i], recv_sem=recv_sems.at[i],
                                             device_id=(xi, yi, 1 - ci), device_id_type=MESH) for i in range(n)]

    def start(x_refs, o_refs, sems):
        for cp in copies(x_refs, o_refs, sems):
            cp.start()

    def finish(x_refs, o_refs, sems):
        cps = copies(x_refs, o_refs, sems)
        for cp in cps:
            cp.wait_recv()
        for cp in cps:
            cp.wait_send()

    return _Comm(xs, out_shapes, {}, [n, n], start, finish)


def _chip_exchange(xs):
    n = len(xs)
    out_shapes = [jax.ShapeDtypeStruct((3,) + a.shape[1:], a.dtype) for a in xs]

    def copies(x_refs, o_refs, sems):
        send_sems, recv_sems = sems
        xi, yi, ci, peers = _place()
        return [pltpu.make_async_remote_copy(src_ref=x_refs[i].at[2 * px + py], dst_ref=o_refs[i].at[d],
                                             send_sem=send_sems.at[3 * i + d], recv_sem=recv_sems.at[3 * i + d],
                                             device_id=(px, py, ci), device_id_type=MESH)
                for i in range(n) for d, (px, py) in enumerate(peers)]

    def start(x_refs, o_refs, sems):
        for cp in copies(x_refs, o_refs, sems):
            cp.start()

    def finish(x_refs, o_refs, sems):
        cps = copies(x_refs, o_refs, sems)
        for cp in cps:
            cp.wait_recv()
        for cp in cps:
            cp.wait_send()

    return _Comm(xs, out_shapes, {}, [3 * n, 3 * n], start, finish)


def _inplace_call(body, name, bufs, sem_counts):
    n = len(bufs)
    return pl.pallas_call(
        body, name=name, out_shape=[jax.ShapeDtypeStruct(a.shape, a.dtype) for a in bufs],
        in_specs=[ANY] * n, out_specs=[ANY] * n, input_output_aliases={i: i for i in range(n)},
        scratch_shapes=[pltpu.SemaphoreType.DMA((k,)) for k in sem_counts],
        compiler_params=pltpu.CompilerParams(has_side_effects=True),
    )(*bufs)


def _gather_weights(bufs, whole=(), early=None):
    n, nw = len(bufs), len(whole)

    def copy(ref, s, r, k, dev):
        return pltpu.make_async_remote_copy(src_ref=ref, dst_ref=ref, send_sem=s.at[k], recv_sem=r.at[k],
                                            device_id=dev, device_id_type=MESH)

    def plan(w, sems):
        ici_s, ici_r, d2d_s, d2d_r = sems
        xi, yi, ci, peers = _place()
        xn, yn, sib = (1 - xi, yi, ci), (xi, 1 - yi, ci), (xi, yi, 1 - ci)
        j, jx, jy, jd = 2 * xi + yi, 2 * (1 - xi) + yi, 2 * xi + 1 - yi, 2 * (1 - xi) + 1 - yi
        c = {}
        for i in range(n):
            hh = bufs[i].shape[2] // 2
            lo, hi = pl.ds(0, hh), pl.ds(hh, hh)
            at = lambda slot, half, rows=None, i=i: w[i].at[slot, half] if rows is None else w[i].at[slot, half, rows]
            c["to_x", i] = (copy(at(j, ci), ici_s, ici_r, 4 * i, xn), copy(at(jx, ci), ici_s, ici_r, 4 * i, xn))
            c["to_y", i] = (copy(at(j, ci), ici_s, ici_r, 4 * i + 1, yn), copy(at(jy, ci), ici_s, ici_r, 4 * i + 1, yn))
            c["x_on", i] = (copy(at(jx, ci, lo), ici_s, ici_r, 4 * i + 2, yn), copy(at(jd, ci, lo), ici_s, ici_r, 4 * i + 2, yn))
            c["y_on", i] = (copy(at(jy, ci, hi), ici_s, ici_r, 4 * i + 3, xn), copy(at(jd, ci, hi), ici_s, ici_r, 4 * i + 3, xn))
            for k, slot in enumerate((jx, jy, jd)):
                c["sib", i, k] = (copy(at(slot, ci), d2d_s, d2d_r, 3 * i + k, sib),
                                  copy(at(slot, 1 - ci), d2d_s, d2d_r, 3 * i + k, sib))
        for i in range(nw):
            for d, (px, py) in enumerate(peers):
                k = 4 * n + 3 * i + d
                c["whole", i, d] = (copy(w[n + i].at[j], ici_s, ici_r, k, (px, py, ci)),
                                    copy(w[n + i].at[2 * px + py], ici_s, ici_r, k, (px, py, ci)))
        return c

    def start(_, w, sems):
        c = plan(w, sems)
        for i in range(n):
            c["to_x", i][0].start()
            c["to_y", i][0].start()
        for i in range(nw):
            for d in range(3):
                c["whole", i, d][0].start()

    def pass_on(c, which):
        for i in which:
            c["to_x", i][1].wait_recv()
            c["x_on", i][0].start()
            c["sib", i, 0][0].start()
            c["to_y", i][1].wait_recv()
            c["y_on", i][0].start()
            c["sib", i, 1][0].start()

    n_early = n if early is None else min(early, n)

    def middle(_, w, sems):
        pass_on(plan(w, sems), range(n_early))

    def finish(_, w, sems):
        c = plan(w, sems)
        pass_on(c, range(n_early, n))
        for i in range(n):
            c["x_on", i][1].wait_recv()
            c["y_on", i][1].wait_recv()
            c["sib", i, 2][0].start()
        for i in range(n):
            for k in range(3):
                c["sib", i, k][1].wait_recv()
        for i in range(nw):
            for d in range(3):
                c["whole", i, d][1].wait_recv()
        for key, (sent, _) in c.items():
            sent.wait_send()

    arrays = list(bufs) + list(whole)
    k = 4 * n + 3 * nw
    return _Comm(arrays, [jax.ShapeDtypeStruct(a.shape, a.dtype) for a in arrays], {i: i for i in range(n + nw)},
                 [k, k, max(3 * n, 1), max(3 * n, 1)], start, finish, middle)


def _pair_gather(bufs, spread, name):
    n = len(bufs)
    where = [(i, l) for i, a in enumerate(bufs) for l in range(a.shape[0])]
    nw = len(where)
    arrays = list(bufs) + ([] if spread is None else [spread])
    na = len(arrays)

    def body(*refs):
        g = refs[na:na + n]
        sp = refs[na + n] if spread is not None else None
        send_sems, recv_sems = refs[2 * na:]
        xi, yi, ci, peers = _place()
        chips = [(xi, yi)] + peers
        others = [(d, f) for d in range(4) for f in (0, 1) if (d, f) != (0, 0)] if spread is not None else []
        sends = []
        for k, (i, l) in enumerate(where):
            mine = g[i].at[l, :, ci]
            cp = pltpu.make_async_remote_copy(src_ref=mine, dst_ref=mine, send_sem=send_sems.at[k],
                                              recv_sem=recv_sems.at[k], device_id=(xi, yi, 1 - ci), device_id_type=MESH)
            cp.start()
            sends.append(cp)
        for k, (d, f) in enumerate(others):
            mine = sp.at[2 * xi + yi, ci]
            cp = pltpu.make_async_remote_copy(src_ref=mine, dst_ref=mine, send_sem=send_sems.at[nw + k],
                                              recv_sem=recv_sems.at[nw + k],
                                              device_id=(chips[d][0], chips[d][1], ci if f == 0 else 1 - ci),
                                              device_id_type=MESH)
            cp.start()
            sends.append(cp)
        for k, (i, l) in enumerate(where):
            other = g[i].at[l, :, 1 - ci]
            pltpu.make_async_remote_copy(src_ref=other, dst_ref=other, send_sem=send_sems.at[k],
                                         recv_sem=recv_sems.at[k], device_id=(xi, yi, 1 - ci),
                                         device_id_type=MESH).wait_recv()
        for k, (d, f) in enumerate(others):
            px, py, pc = chips[d][0], chips[d][1], (ci if f == 0 else 1 - ci)
            theirs = sp.at[2 * px + py, pc]
            pltpu.make_async_remote_copy(src_ref=theirs, dst_ref=theirs, send_sem=send_sems.at[nw + k],
                                         recv_sem=recv_sems.at[nw + k], device_id=(px, py, pc),
                                         device_id_type=MESH).wait_recv()
        for cp in sends:
            cp.wait_send()

    n_sems = nw + (7 if spread is not None else 0)
    out = _inplace_call(body, name, arrays, [n_sems, n_sems])
    return out[:n], (out[n] if spread is not None else None)


def _row_block(rows, cols):
    want = max(8, (2 ** 18 // max(cols, 1)) // 8 * 8)
    if rows <= want:
        return rows
    best = 8
    for t in range(8, want + 1, 8):
        if rows % t == 0:
            best = t
    return best


def _pair_add(g4, recv, sel, out_dtype, name):
    A, _, H, C = g4.shape
    bh = _row_block(H, C)

    def body(sel_ref, g_ref, r_ref, o_ref):
        o_ref[...] = (g_ref[...].astype(F32) + r_ref[...].astype(F32)).astype(out_dtype)

    return pl.pallas_call(
        body, name=name, out_shape=jax.ShapeDtypeStruct((A, H, C), out_dtype),
        grid_spec=pltpu.PrefetchScalarGridSpec(
            num_scalar_prefetch=1, grid=(A, H // bh),
            in_specs=[pl.BlockSpec((None, None, bh, C), lambda a, i, s: (a, s[1], i, 0)),
                      pl.BlockSpec((None, bh, C), lambda a, i, s: (a, i, 0))],
            out_specs=pl.BlockSpec((None, bh, C), lambda a, i, s: (a, i, 0))),
        compiler_params=pltpu.CompilerParams(dimension_semantics=("arbitrary", "arbitrary"),
                                             vmem_limit_bytes=VMEM_LIMIT),
    )(sel, g4, recv)


def _chip_reduce(g4, recv1, recv2, sel, buf, layer, n_layers, name, by_chip=False):
    _, _, H, C = g4.shape
    bh = _row_block(H, C)
    if by_chip:
        out_shape = jax.ShapeDtypeStruct((N_CHIPS, 2, H, C), F32)
        out_spec = pl.BlockSpec((None, None, bh, C), lambda i, s: (s[0], s[1], i, 0))
    else:
        out_shape = jax.ShapeDtypeStruct((n_layers, 1, 2, H, C), F32)
        out_spec = pl.BlockSpec((None, None, None, bh, C), lambda i, s: (layer, 0, s[1], i, 0))

    def body(sel_ref, g_ref, r1_ref, r2_ref, *rest):
        acc = g_ref[...].astype(F32) + r1_ref[...].astype(F32)
        for d in range(3):
            acc = acc + r2_ref[d].astype(F32)
        rest[-1][...] = acc

    in_specs = [pl.BlockSpec((None, None, bh, C), lambda i, s: (s[0], s[1], i, 0)),
                pl.BlockSpec((None, bh, C), lambda i, s: (s[0], i, 0)),
                pl.BlockSpec((3, bh, C), lambda i, s: (0, i, 0))]
    args = [sel, g4, recv1, recv2]
    aliases = {}
    if buf is not None:
        in_specs.append(ANY)
        args.append(buf)
        aliases = {4: 0}
    return pl.pallas_call(
        body, name=name, out_shape=out_shape,
        grid_spec=pltpu.PrefetchScalarGridSpec(num_scalar_prefetch=1, grid=(H // bh,), in_specs=in_specs,
                                               out_specs=out_spec),
        input_output_aliases=aliases,
        compiler_params=pltpu.CompilerParams(dimension_semantics=("arbitrary",), vmem_limit_bytes=VMEM_LIMIT),
    )(*args)


def _chip_reduce_many(groups, sel, comm, name):
    flat = [t for grp in groups for t in grp]
    n, ng = len(flat), len(groups)
    _, _, H, C = flat[0][0].shape
    bh = _row_block(H, C)
    last = H // bh - 1
    nci, nco = len(comm.ins), len(comm.out_shapes)

    def body(sel_ref, *refs):
        ins, refs = refs[:3 * n], refs[3 * n:]
        cins, refs = refs[:nci], refs[nci:]
        outs, refs = refs[:ng], refs[ng:]
        couts, sems = refs[:nco], refs[nco:]

        @pl.when(pl.program_id(0) == 0)
        def _():
            comm.start(cins, couts, sems)

        k = 0
        for gi, grp in enumerate(groups):
            for layer in range(len(grp)):
                g_ref, r1_ref, r2_ref = ins[3 * k:3 * k + 3]
                acc = g_ref[...].astype(F32) + r1_ref[...].astype(F32)
                for d in range(3):
                    acc = acc + r2_ref[d].astype(F32)
                outs[gi][layer] = acc
                k += 1

        @pl.when(pl.program_id(0) == last)
        def _():
            if comm.middle is not None:
                comm.middle(cins, couts, sems)
            comm.finish(cins, couts, sems)

    in_specs = [pl.BlockSpec((None, None, bh, C), lambda i, s: (s[0], s[1], i, 0)),
                pl.BlockSpec((None, bh, C), lambda i, s: (s[0], i, 0)),
                pl.BlockSpec((3, bh, C), lambda i, s: (0, i, 0))] * n + [ANY] * nci
    out_specs = [pl.BlockSpec((len(grp), None, None, bh, C), lambda i, s: (0, 0, s[1], i, 0)) for grp in groups]
    res = pl.pallas_call(
        body, name=name,
        out_shape=[jax.ShapeDtypeStruct((len(grp), 1, 2, H, C), F32) for grp in groups] + list(comm.out_shapes),
        grid_spec=pltpu.PrefetchScalarGridSpec(
            num_scalar_prefetch=1, grid=(H // bh,), in_specs=in_specs, out_specs=out_specs + [ANY] * nco,
            scratch_shapes=[pltpu.SemaphoreType.DMA((k,)) for k in comm.sem_counts]),
        input_output_aliases={1 + 3 * n + a: ng + b for a, b in comm.aliases.items()},
        compiler_params=pltpu.CompilerParams(dimension_semantics=("arbitrary",), vmem_limit_bytes=VMEM_LIMIT,
                                             has_side_effects=True),
    )(sel, *[a for t in flat for a in t], *comm.ins)
    return res[:ng], res[ng:]


def _cast_place(shards, layer, sel, name):
    _, _, H, C = shards.shape
    bh = _row_block(H, C)

    def body(sel_ref, x_ref, o_ref):
        o_ref[...] = x_ref[...].astype(BF16)

    return pl.pallas_call(
        body, name=name, out_shape=jax.ShapeDtypeStruct((N_CHIPS, 2, H, C), BF16),
        grid_spec=pltpu.PrefetchScalarGridSpec(
            num_scalar_prefetch=1, grid=(2, H // bh),
            in_specs=[pl.BlockSpec((None, None, bh, C), lambda h, i, s: (layer, h, i, 0))],
            out_specs=pl.BlockSpec((None, None, bh, C), lambda h, i, s: (s[0], h, i, 0))),
        compiler_params=pltpu.CompilerParams(dimension_semantics=("arbitrary", "arbitrary"),
                                             vmem_limit_bytes=VMEM_LIMIT),
    )(sel, shards)


def _cast_place_many(pairs, sel, comm, name):
    n = len(pairs)
    _, _, H, C = pairs[0][0].shape
    bh = _row_block(H, C)
    last = H // bh - 1
    nci, nco = len(comm.ins), len(comm.out_shapes)

    def body(sel_ref, *refs):
        xs, refs = refs[:n], refs[n:]
        cins, refs = refs[:nci], refs[nci:]
        outs, refs = refs[:n], refs[n:]
        couts, sems = refs[:nco], refs[nco:]
        h, i = pl.program_id(0), pl.program_id(1)

        @pl.when((h == 0) & (i == 0))
        def _():
            comm.start(cins, couts, sems)

        if comm.middle is not None:
            @pl.when((h == 1) & (i == 0))
            def _():
                comm.middle(cins, couts, sems)

        for x_ref, o_ref in zip(xs, outs):
            o_ref[...] = x_ref[...].astype(BF16)

        @pl.when((h == 1) & (i == last))
        def _():
            comm.finish(cins, couts, sems)

    res = pl.pallas_call(
        body, name=name,
        out_shape=[jax.ShapeDtypeStruct((N_CHIPS, 2, H, C), BF16)] * n + list(comm.out_shapes),
        grid_spec=pltpu.PrefetchScalarGridSpec(
            num_scalar_prefetch=1, grid=(2, H // bh),
            in_specs=[pl.BlockSpec((None, None, bh, C), lambda h, i, s, layer=layer: (layer, h, i, 0))
                      for _, layer in pairs] + [ANY] * nci,
            out_specs=[pl.BlockSpec((None, None, bh, C), lambda h, i, s: (s[0], h, i, 0))] * n + [ANY] * nco,
            scratch_shapes=[pltpu.SemaphoreType.DMA((k,)) for k in comm.sem_counts]),
        input_output_aliases={1 + n + a: n + b for a, b in comm.aliases.items()},
        compiler_params=pltpu.CompilerParams(dimension_semantics=("arbitrary", "arbitrary"),
                                             vmem_limit_bytes=VMEM_LIMIT, has_side_effects=True),
    )(sel, *[s for s, _ in pairs], *comm.ins)
    return res[:n], res[n:]


def _adamw_many(quads, name, comm=None):
    R, C = quads[0][0].shape
    br = _row_block(R, C)
    n = len(quads)

    def body(*refs):
        for k in range(n):
            g_ref, w_ref, m_ref, v_ref = refs[4 * k:4 * k + 4]
            d_ref, mo_ref, vo_ref = refs[4 * n + 3 * k:4 * n + 3 * k + 3]
            gv = g_ref[...]
            mn = ADAM_B1 * m_ref[...] + (1.0 - ADAM_B1) * gv
            vn = ADAM_B2 * v_ref[...] + (1.0 - ADAM_B2) * (gv * gv)
            m_hat = mn / (1.0 - ADAM_B1 ** ADAM_STEP)
            v_hat = vn / (1.0 - ADAM_B2 ** ADAM_STEP)
            d_ref[...] = -ADAM_LR * (m_hat / (jnp.sqrt(v_hat) + ADAM_EPS) + ADAM_WD * w_ref[...])
            mo_ref[...] = mn
            vo_ref[...] = vn

    spec = pl.BlockSpec((br, C), lambda i: (i, 0))
    res, got = _pcall(body, name=name, grid=(R // br,), comm=comm, out_shape=(jax.ShapeDtypeStruct((R, C), F32),) * (3 * n),
                      in_specs=[spec] * (4 * n), out_specs=(spec,) * (3 * n), args=[a for q in quads for a in q])
    return [res[3 * k:3 * k + 3] for k in range(n)], got


def _adamw(g, w, m, v, name):
    return _adamw_many([(g, w, m, v)], name)[0][0]


def _part_rows(shape):
    return -(-math.prod(shape) // (8 * LANES)) * 8


def _pack_rows(arrs, total_rows):
    parts = []
    for a in arrs:
        flat = a.reshape(-1)
        rows = _part_rows(a.shape)
        parts.append(jnp.pad(flat, (0, rows * LANES - flat.shape[0])).reshape(rows, LANES))
    packed = jnp.concatenate(parts, axis=0)
    return jnp.pad(packed, ((0, total_rows - packed.shape[0]), (0, 0)))


def _unpack_rows(packed, shapes):
    out, r = [], 0
    for shp in shapes:
        size, rows = math.prod(shp), _part_rows(shp)
        out.append(packed[r:r + rows].reshape(-1)[:size].reshape(shp))
        r += rows
    return out


def _rows_of(shapes):
    return sum(_part_rows(s) for s in shapes)


_GATHER_STAGES = {"start": ["wa", "mkv0", "wo0"], "mixer_a_fwd": ["wup0", "wdown0"],
                  "mlp_fwd_0": ["wq", "wkv", "mkv1", "wo1"], "fox_fwd": ["wup1", "wdown1"]}
_SWAP_STAGES = {"mixer_b_bwd": ["wup1", "wdown1"], "proj_b_bwd": ["wo1", "wq", "wkv", "mkv1"]}
_REDUCE_STAGES = {"fox_bwd": ["wup1", "wdown1"], "mlp_bwd_0": ["wo1", "wq", "wkv", "mkv1"],
                  "mixer_a_bwd": ["wup0", "wdown0"], "end": ["wa", "wo0", "mkv0", "small"]}
_SMALL = ["ws", "b_s", "sg", "sb", "bf", "ln_g", "ln_b", "loss"]


class _Staged:
    def __init__(self, placed, sel, dm, ng, shard_cols, n_small):
        self.placed, self.sel, self.dm, self.ng, self.shard_cols, self.n_small = placed, sel, dm, ng, shard_cols, n_small
        self.g4, self.recv1, self.recv2 = {}, {}, {}

    def _weight(self, name, a):
        a = a.reshape(N_CHIPS, 2 * a.shape[2], a.shape[3])
        if name in ("wa", "wkv"):
            a = a.transpose(1, 0, 2).reshape(a.shape[1], -1)
            if name == "wkv":
                a = jnp.concatenate([a[:, :2 * self.dm],
                                     jnp.pad(a[:, 2 * self.dm:], ((0, 0), (0, LANES - self.ng)))], axis=1)
            return a
        if name.startswith("wup") or name.startswith("wdown"):
            return a
        return a.reshape(-1, a.shape[2])

    def _partial(self, name, grads):
        if name == "small":
            a = _pack_rows([grads[n] for n in _SMALL], self.n_small).reshape(N_CHIPS, -1, LANES)
        elif name in self.shard_cols:
            n = self.shard_cols[name]
            g = grads[name][:, :N_CHIPS * n]
            a = g.reshape(g.shape[0], N_CHIPS, n).transpose(1, 0, 2)
        elif name.startswith("wup") or name.startswith("wdown"):
            a = grads[name]
        else:
            g = grads[name]
            a = g.reshape(N_CHIPS, g.shape[0] // N_CHIPS, g.shape[1])
        return a.reshape(N_CHIPS, 2, a.shape[1] // 2, a.shape[2])

    def plan(self, stage, w, grads):
        if stage in _GATHER_STAGES:
            return _gather_weights([self.placed[n] for n in _GATHER_STAGES[stage]],
                                   early=1 if stage == "mixer_a_fwd" else None)
        if stage in _SWAP_STAGES:
            for n in _SWAP_STAGES[stage]:
                self.g4[n] = self._partial(n, grads)
            return _pair_swap([self.g4[n] for n in _SWAP_STAGES[stage]])
        if stage in _REDUCE_STAGES:
            names = _REDUCE_STAGES[stage]
            late = [n for n in names if n not in self.recv1]
            for n in late:
                self.g4[n] = self._partial(n, grads)
            if late:
                got = _run_comm(_pair_swap([self.g4[n] for n in late]), "reduce_pair_swap_" + stage)
                self.recv1.update(zip(late, got))
            return _chip_exchange([_pair_add(self.g4[n], self.recv1[n], self.sel, F32 if n == "small" else BF16,
                                             "reduce_pair_add_" + n) for n in names])
        return None

    def done(self, stage, results, w):
        if stage in _GATHER_STAGES:
            for n, a in zip(_GATHER_STAGES[stage], results):
                w[n] = self._weight(n, a)
        elif stage in _SWAP_STAGES:
            self.recv1.update(zip(_SWAP_STAGES[stage], results))
        elif stage in _REDUCE_STAGES:
            self.recv2.update(zip(_REDUCE_STAGES[stage], results))


def kernel(x, mem, a_w_in, a_sgu_ln_g, a_sgu_ln_b, a_w_s, a_b_s, kv_w, kv_b_f, b_w_q, mem_w_kv, w_o, ln_g, ln_b, w_up, w_down, loss_target, m_a_w_in, m_a_sgu_ln_g, m_a_sgu_ln_b, m_a_w_s, m_a_b_s, m_kv_w, m_kv_b_f, m_b_w_q, m_mem_w_kv, m_w_o, m_ln_g, m_ln_b, m_w_up, m_w_down, v_a_w_in, v_a_sgu_ln_g, v_a_sgu_ln_b, v_a_w_s, v_a_b_s, v_kv_w, v_kv_b_f, v_b_w_q, v_mem_w_kv, v_w_o, v_ln_g, v_ln_b, v_w_up, v_w_down):
    xi, yi, ci = lax.axis_index("x"), lax.axis_index("y"), lax.axis_index("c")
    chip = 2 * xi + yi
    sel = jnp.stack([chip, ci]).astype(jnp.int32)
    S, D = x.shape[1], x.shape[2]
    DQ = mem_w_kv.shape[2] // 2
    DM = D - DQ
    NG = DM // HEAD_DIM

    shards = {"wa": (a_w_in, 0), "wkv": (kv_w[None], 0), "wq": (b_w_q, 0), "mkv0": (mem_w_kv, 0), "mkv1": (mem_w_kv, 1),
              "wo0": (w_o, 0), "wo1": (w_o, 1), "wup0": (w_up, 0), "wup1": (w_up, 1), "wdown0": (w_down, 0),
              "wdown1": (w_down, 1)}
    halves = lambda s: s.reshape(s.shape[0], 2, s.shape[1] // 2, s.shape[2])
    mlp = ["wup0", "wup1", "wdown0", "wdown1"]
    placed = {n: _cast_place(halves(s), layer, sel, "cast_place_" + n)
              for n, (s, layer) in shards.items() if n not in mlp}
    ln_pack = jnp.concatenate([ln_g.reshape(4, -1), ln_b.reshape(4, -1)], axis=0)
    ln_buf = lax.dynamic_update_slice(jnp.zeros((N_CHIPS,) + ln_pack.shape, F32), ln_pack[None], (chip, 0, 0))
    mlp_placed, got = _cast_place_many(
        [(halves(shards[n][0]), shards[n][1]) for n in mlp], sel,
        _gather_weights([placed[n] for n in _GATHER_STAGES["start"]], whole=[ln_buf]), "cast_place_mlp")
    placed.update(zip(mlp, mlp_placed))
    ln_shape = (4, D)
    small_shapes = [a_w_s.shape, a_b_s.shape, a_sgu_ln_g.shape, a_sgu_ln_b.shape, kv_b_f.shape, ln_shape, ln_shape,
                    (1, LANES)]
    n_small = -(-_rows_of(small_shapes) // 64) * 64
    hooks = _Staged(placed, sel, DM, NG, {"wa": a_w_in.shape[2], "wkv": kv_w.shape[1]}, n_small)
    weights = {"sg": a_sgu_ln_g, "sb": a_sgu_ln_b, "ws": a_w_s[0], "b_s": a_b_s[0], "bf": kv_b_f}
    hooks.done("start", got[:-1], weights)
    ln_full = got[-1].transpose(1, 0, 2).reshape(8, D)
    weights["ln_g"], weights["ln_b"] = ln_full[:4], ln_full[4:]

    _, grad_x, gr = _local_step(x[0], mem[0], loss_target[0], weights, hooks)

    def chip_sums(groups):
        bufs = []
        for grp in groups:
            buf = None
            for layer, n in enumerate(grp):
                buf = _chip_reduce(hooks.g4[n], hooks.recv1[n], hooks.recv2[n], sel, buf, layer, len(grp),
                                   "reduce_chip_sum_" + n)
            bufs.append(buf)
        return bufs

    whole = lambda a: a.reshape(a.shape[0], 2 * a.shape[3], a.shape[4])
    flat = lambda a: a.reshape(-1, a.shape[-1])

    def update(name, g, w, m, v):
        d, mn, vn = _adamw(flat(g), flat(w), flat(m), flat(v), "adamw_" + name)
        return d.reshape(w.shape), mn.reshape(w.shape), vn.reshape(w.shape)

    parts = lambda n: (hooks.g4[n], hooks.recv1[n], hooks.recv2[n])
    mlp_sums, got = _chip_reduce_many([[parts("wup0"), parts("wup1")], [parts("wdown0"), parts("wdown1")]], sel,
                                      hooks.plan("end", weights, gr), "reduce_chip_sum_mlp")
    hooks.done("end", got, weights)
    small_mine = _chip_reduce(hooks.g4["small"], hooks.recv1["small"], hooks.recv2["small"], sel, None, 0, 1,
                              "reduce_chip_sum_small", by_chip=True)
    red, small_all = _pair_gather(chip_sums([["wa"], ["wkv"], ["wq"], ["mkv0", "mkv1"], ["wo0", "wo1"]])
                                  + list(mlp_sums), small_mine, "reduce_pair_gather")
    g_a_w_in, g_kv_w, g_b_w_q, g_mem_w_kv, g_w_o, g_w_up, g_w_down = [whole(a) for a in red]
    g_kv_w = g_kv_w[0]
    g_w_down = g_w_down.transpose(0, 2, 1)
    small_all = small_all.reshape(n_small, LANES)
    g_ws, g_bs, g_sg, g_sb, g_bf, g_lng, g_lnb, loss_row = _unpack_rows(small_all, small_shapes)
    loss = loss_row[0, 0]
    dsh = D // N_CHIPS
    g_ln_g = lax.dynamic_slice_in_dim(g_lng, chip * dsh, dsh, axis=1).reshape(ln_g.shape)
    g_ln_b = lax.dynamic_slice_in_dim(g_lnb, chip * dsh, dsh, axis=1).reshape(ln_b.shape)

    upd = {
        "a_w_in": update("a_w_in", g_a_w_in, a_w_in, m_a_w_in, v_a_w_in),
        "kv_w": update("kv_w", g_kv_w, kv_w, m_kv_w, v_kv_w),
        "b_w_q": update("b_w_q", g_b_w_q, b_w_q, m_b_w_q, v_b_w_q),
        "mem_w_kv": update("mem_w_kv", g_mem_w_kv, mem_w_kv, m_mem_w_kv, v_mem_w_kv),
        "w_o": update("w_o", g_w_o, w_o, m_w_o, v_w_o),
        "w_up": update("w_up", g_w_up, w_up, m_w_up, v_w_up),
        "w_down": update("w_down", g_w_down, w_down, m_w_down, v_w_down),
    }
    tiny_g = [g_sg, g_sb, g_ws, g_bs, g_bf, g_ln_g, g_ln_b]
    tiny_w = [a_sgu_ln_g, a_sgu_ln_b, a_w_s, a_b_s, kv_b_f, ln_g, ln_b]
    tiny_m = [m_a_sgu_ln_g, m_a_sgu_ln_b, m_a_w_s, m_a_b_s, m_kv_b_f, m_ln_g, m_ln_b]
    tiny_v = [v_a_sgu_ln_g, v_a_sgu_ln_b, v_a_w_s, v_a_b_s, v_kv_b_f, v_ln_g, v_ln_b]
    tiny_shapes = [a.shape for a in tiny_w]
    n_tiny = -(-_rows_of(tiny_shapes) // 8) * 8
    td, tm_, tv = _adamw(_pack_rows(tiny_g, n_tiny), _pack_rows(tiny_w, n_tiny), _pack_rows(tiny_m, n_tiny),
                         _pack_rows(tiny_v, n_tiny), "adamw_small")
    for name, d, mn, vn in zip(["a_sgu_ln_g", "a_sgu_ln_b", "a_w_s", "a_b_s", "kv_b_f", "ln_g", "ln_b"],
                               _unpack_rows(td, tiny_shapes), _unpack_rows(tm_, tiny_shapes),
                               _unpack_rows(tv, tiny_shapes)):
        upd[name] = (d, mn, vn)

    order = ["a_w_in", "a_sgu_ln_g", "a_sgu_ln_b", "a_w_s", "a_b_s", "kv_w", "kv_b_f", "b_w_q", "mem_w_kv", "w_o",
             "ln_g", "ln_b", "w_up", "w_down"]
    grads = {"a_w_in": g_a_w_in.reshape(a_w_in.shape), "a_sgu_ln_g": g_sg.reshape(a_sgu_ln_g.shape),
             "a_sgu_ln_b": g_sb.reshape(a_sgu_ln_b.shape), "a_w_s": g_ws.reshape(a_w_s.shape),
             "a_b_s": g_bs.reshape(a_b_s.shape), "kv_w": g_kv_w, "kv_b_f": g_bf, "b_w_q": g_b_w_q.reshape(b_w_q.shape),
             "mem_w_kv": g_mem_w_kv, "w_o": g_w_o, "ln_g": g_ln_g, "ln_b": g_ln_b, "w_up": g_w_up, "w_down": g_w_down}
    return (loss, grad_x[None], *[grads[n] for n in order], *[upd[n][0] for n in order],
            *[upd[n][1] for n in order], *[upd[n][2] for n in order])
```
